```python
import math
import jax, jax.numpy as jnp
from jax import lax
import numpy as np

D_MODEL = 1024
BATCH = 32
SEQ = 2048
DEPTH = 1

N_META = 16
EPS = 1e-6

GLA_HEADS = 4
GLA_DK = 128
GLA_DV = 256
GLA_GATE_RANK = 16
GLA_GATE_NORMALIZER = 16.0
GLA_CHUNK = 64
GLA_KW = GLA_HEADS * GLA_DK
GLA_VW = GLA_HEADS * GLA_DV

MLA_HEADS = 8
MLA_NOPE = 128
MLA_ROPE = 64
MLA_DV = 128
MLA_Q_RANK = 256
MLA_KV_RANK = 128
MLA_QK = MLA_NOPE + MLA_ROPE
MLA_VW = MLA_HEADS * MLA_DV
ROPE_BASE = 10000.0
ATTN_BLOCK = 128

SPLITS = (GLA_KW, GLA_KW, GLA_VW, GLA_GATE_RANK, GLA_VW,
          MLA_Q_RANK, MLA_KV_RANK, MLA_ROPE, MLA_VW,
          D_MODEL, D_MODEL)
IN_WIDTH = (2 * GLA_KW + 2 * GLA_VW + GLA_GATE_RANK + MLA_Q_RANK + MLA_KV_RANK
            + MLA_ROPE + MLA_VW + 2 * D_MODEL)

kernel_name = "hybrid_gla_mla_gated_meta"


def rms_norm(x, g):
    xf = x.astype(jnp.float32)
    y = xf * lax.rsqrt(jnp.mean(xf * xf, axis=-1, keepdims=True) + EPS)
    return (y * g.astype(jnp.float32)).astype(x.dtype)


def rope_tables(n, dim):
    inv = 1.0 / (ROPE_BASE ** (jnp.arange(0, dim, 2, dtype=jnp.float32) / dim))
    ang = jnp.arange(n, dtype=jnp.float32)[:, None] * inv[None, :]
    return jnp.cos(ang), jnp.sin(ang)


def apply_rope(x, cos, sin):
    xf = x.astype(jnp.float32)
    x1, x2 = jnp.split(xf, 2, axis=-1)
    return jnp.concatenate([x1 * cos - x2 * sin, x2 * cos + x1 * sin], axis=-1).astype(x.dtype)


def gla_chunked(q, k, v, g):
    B, L, H, dk = q.shape
    dv = v.shape[-1]
    C = GLA_CHUNK
    front = (-N_META) % C
    back = (-(L - N_META)) % C
    padw = ((0, 0), (front, back), (0, 0), (0, 0))
    q, k, v, g = [jnp.pad(t.astype(jnp.float32), padw) for t in (q, k, v, g)]
    N = q.shape[1] // C

    def chunks(t):
        return t.reshape(B, N, C, H, t.shape[-1]).transpose(1, 0, 3, 2, 4)

    q, k, v, g = chunks(q), chunks(k), chunks(v), chunks(g)
    b = jnp.cumsum(g, axis=3)
    b_last = b[:, :, :, -1:, :]
    qe = q * jnp.exp(b)
    ke = k * jnp.exp(-b)
    kl = k * jnp.exp(b_last - b)
    mask = jnp.tril(jnp.ones((C, C), dtype=bool))
    A = jnp.where(mask, jnp.einsum('nbhid,nbhjd->nbhij', qe, ke), 0.0)
    o_intra = jnp.einsum('nbhij,nbhjv->nbhiv', A, v)
    decay = jnp.exp(b_last[:, :, :, 0, :])

    def step(S, inp):
        qe_n, kl_n, v_n, d_n = inp
        o = jnp.einsum('bhid,bhdv->bhiv', qe_n, S)
        S = S * d_n[..., None] + jnp.einsum('bhjd,bhjv->bhdv', kl_n, v_n)
        return S, o

    S0 = jnp.zeros((B, H, dk, dv), jnp.float32)
    _, o_inter = lax.scan(step, S0, (qe, kl, v, decay))
    o = (o_intra + o_inter).transpose(1, 0, 3, 2, 4).reshape(B, N * C, H, dv)
    return o[:, front:front + L]


def mla_attention(c_q, c_kv, k_rope, q_norm_g, w_uq, kv_norm_g, w_ukv):
    B, L, _ = c_q.shape
    H = MLA_HEADS
    cos, sin = rope_tables(L, MLA_ROPE)
    q = (rms_norm(c_q, q_norm_g) @ w_uq).reshape(B, L, H, MLA_QK)
    q_nope, q_rope = q[..., :MLA_NOPE], q[..., MLA_NOPE:]
    q_rope = apply_rope(q_rope, cos[:, None, :], sin[:, None, :])
    kv = (rms_norm(c_kv, kv_norm_g) @ w_ukv).reshape(B, L, H, MLA_NOPE + MLA_DV)
    k_nope, v = kv[..., :MLA_NOPE], kv[..., MLA_NOPE:]
    k_rope = apply_rope(k_rope, cos, sin)
    k = jnp.concatenate([k_nope, jnp.broadcast_to(k_rope[:, :, None, :], (B, L, H, MLA_ROPE))], axis=-1)
    q = jnp.concatenate([q_nope, q_rope], axis=-1)

    Lp = ((L + ATTN_BLOCK - 1) // ATTN_BLOCK) * ATTN_BLOCK
    nb = Lp // ATTN_BLOCK
    padw = ((0, 0), (0, Lp - L), (0, 0), (0, 0))
    q = jnp.pad(q, padw).transpose(0, 2, 1, 3)
    k = jnp.pad(k, padw).transpose(0, 2, 1, 3)
    v = jnp.pad(v, padw).transpose(0, 2, 1, 3)
    qb = q.reshape(B, H, nb, ATTN_BLOCK, MLA_QK).transpose(2, 0, 1, 3, 4)
    scale = 1.0 / math.sqrt(MLA_QK)
    kpos = jnp.arange(Lp)

    def block(args):
        q_blk, i = args
        s = jnp.einsum('bhqd,bhkd->bhqk', q_blk, k).astype(jnp.float32) * scale
        qpos = i * ATTN_BLOCK + jnp.arange(ATTN_BLOCK)
        s = jnp.where(kpos[None, :] <= qpos[:, None], s, -jnp.inf)
        p = jax.nn.softmax(s, axis=-1)
        return jnp.einsum('bhqk,bhkv->bhqv', p.astype(v.dtype), v)

    o = lax.map(block, (qb, jnp.arange(nb)))
    o = o.transpose(1, 0, 3, 2, 4).reshape(B, Lp, H * MLA_DV)
    return o[:, :L]


def _fwd_setup_inputs(seed: int = 0) -> dict:
    key = jax.random.key(seed)
    ks = jax.random.split(key, 16)
    f = jnp.float32
    n = lambda k, s, sc: jax.random.normal(k, s, f) * sc
    return {
        "x": n(ks[0], (BATCH, SEQ, D_MODEL), 1.0),
        "meta_tokens": n(ks[1], (N_META, D_MODEL), 1.0),
        "norm_g": 1.0 + n(ks[2], (DEPTH, D_MODEL), 0.02),
        "w_in": n(ks[3], (DEPTH, D_MODEL, IN_WIDTH), D_MODEL ** -0.5),
        "gla_gate_w": n(ks[4], (DEPTH, GLA_GATE_RANK, GLA_KW), GLA_GATE_RANK ** -0.5),
        "gla_gate_b": n(ks[5], (DEPTH, GLA_KW), 0.1),
        "gla_norm_g": 1.0 + n(ks[6], (DEPTH, GLA_DV), 0.02),
        "gla_proj": n(ks[7], (DEPTH, GLA_VW, D_MODEL), GLA_VW ** -0.5),
        "mla_q_norm_g": 1.0 + n(ks[8], (DEPTH, MLA_Q_RANK), 0.02),
        "mla_w_uq": n(ks[9], (DEPTH, MLA_Q_RANK, MLA_HEADS * MLA_QK), MLA_Q_RANK ** -0.5),
        "mla_kv_norm_g": 1.0 + n(ks[10], (DEPTH, MLA_KV_RANK), 0.02),
        "mla_w_ukv": n(ks[11], (DEPTH, MLA_KV_RANK, MLA_HEADS * (MLA_NOPE + MLA_DV)), MLA_KV_RANK ** -0.5),
        "mla_proj": n(ks[12], (DEPTH, MLA_VW, D_MODEL), MLA_VW ** -0.5),
        "w_out": n(ks[13], (DEPTH, D_MODEL, D_MODEL), D_MODEL ** -0.5),
        "final_norm_g": 1.0 + n(ks[14], (D_MODEL,), 0.02),
    }


def _fwd_reference(x, meta_tokens, norm_g, w_in, gla_gate_w, gla_gate_b, gla_norm_g, gla_proj,
              mla_q_norm_g, mla_w_uq, mla_kv_norm_g, mla_w_ukv, mla_proj, w_out, final_norm_g):
    B = x.shape[0]
    meta = jnp.broadcast_to(meta_tokens[None].astype(x.dtype), (B, N_META, D_MODEL))
    h = jnp.concatenate([meta, x], axis=1)
    L = h.shape[1]
    cuts = [int(c) for c in np.cumsum(SPLITS)[:-1]]
    for l in range(DEPTH):
        u = rms_norm(h, norm_g[l])
        proj = u @ w_in[l]
        (g_q, g_k, g_v, g_lr, g_z, m_cq, m_ckv, m_kr, m_z,
         gate_gla, gate_mla) = jnp.split(proj, cuts, axis=-1)

        q = g_q.reshape(B, L, GLA_HEADS, GLA_DK) * (GLA_DK ** -0.5)
        k = g_k.reshape(B, L, GLA_HEADS, GLA_DK)
        v = g_v.reshape(B, L, GLA_HEADS, GLA_DV)
        gk = jax.nn.log_sigmoid((g_lr @ gla_gate_w[l] + gla_gate_b[l]).astype(jnp.float32)) / GLA_GATE_NORMALIZER
        gk = gk.reshape(B, L, GLA_HEADS, GLA_DK)
        o_a = gla_chunked(q, k, v, gk)
        o_a = rms_norm(o_a, gla_norm_g[l]).reshape(B, L, GLA_VW).astype(h.dtype)
        y_a = (o_a * jax.nn.silu(g_z)) @ gla_proj[l]

        o_b = mla_attention(m_cq, m_ckv, m_kr, mla_q_norm_g[l], mla_w_uq[l],
                            mla_kv_norm_g[l], mla_w_ukv[l])
        y_b = (o_b * jax.nn.silu(m_z)) @ mla_proj[l]

        merged = jax.nn.sigmoid(gate_gla) * y_a + jax.nn.sigmoid(gate_mla) * y_b
        h = h + merged @ w_out[l]
    out = rms_norm(h, final_norm_g)
    return out[:, N_META:]


import jax as _jax
import jax.numpy as _jnp

TWIN_FORMAT = 'train_step'
FWD_PARAMS = ['x', 'meta_tokens', 'norm_g', 'w_in', 'gla_gate_w', 'gla_gate_b', 'gla_norm_g', 'gla_proj', 'mla_q_norm_g', 'mla_w_uq', 'mla_kv_norm_g', 'mla_w_ukv', 'mla_proj', 'w_out', 'final_norm_g']
TWIN_WEIGHTS = ['meta_tokens', 'norm_g', 'w_in', 'gla_gate_w', 'gla_gate_b', 'gla_norm_g', 'gla_proj', 'mla_q_norm_g', 'mla_w_uq', 'mla_kv_norm_g', 'mla_w_ukv', 'mla_proj', 'w_out', 'final_norm_g']
TWIN_DIFF_INPUT = 'x'
TWIN_INPUTS = ['x', 'meta_tokens', 'norm_g', 'w_in', 'gla_gate_w', 'gla_gate_b', 'gla_norm_g', 'gla_proj', 'mla_q_norm_g', 'mla_w_uq', 'mla_kv_norm_g', 'mla_w_ukv', 'mla_proj', 'w_out', 'final_norm_g', 'loss_target', 'm_meta_tokens', 'm_norm_g', 'm_w_in', 'm_gla_gate_w', 'm_gla_gate_b', 'm_gla_norm_g', 'm_gla_proj', 'm_mla_q_norm_g', 'm_mla_w_uq', 'm_mla_kv_norm_g', 'm_mla_w_ukv', 'm_mla_proj', 'm_w_out', 'm_final_norm_g', 'v_meta_tokens', 'v_norm_g', 'v_w_in', 'v_gla_gate_w', 'v_gla_gate_b', 'v_gla_norm_g', 'v_gla_proj', 'v_mla_q_norm_g', 'v_mla_w_uq', 'v_mla_kv_norm_g', 'v_mla_w_ukv', 'v_mla_proj', 'v_w_out', 'v_final_norm_g']
TWIN_OUTPUTS = ['loss', 'grad_x', 'grad_meta_tokens', 'grad_norm_g', 'grad_w_in', 'grad_gla_gate_w', 'grad_gla_gate_b', 'grad_gla_norm_g', 'grad_gla_proj', 'grad_mla_q_norm_g', 'grad_mla_w_uq', 'grad_mla_kv_norm_g', 'grad_mla_w_ukv', 'grad_mla_proj', 'grad_w_out', 'grad_final_norm_g', 'delta_meta_tokens', 'delta_norm_g', 'delta_w_in', 'delta_gla_gate_w', 'delta_gla_gate_b', 'delta_gla_norm_g', 'delta_gla_proj', 'delta_mla_q_norm_g', 'delta_mla_w_uq', 'delta_mla_kv_norm_g', 'delta_mla_w_ukv', 'delta_mla_proj', 'delta_w_out', 'delta_final_norm_g', 'new_m_meta_tokens', 'new_m_norm_g', 'new_m_w_in', 'new_m_gla_gate_w', 'new_m_gla_gate_b', 'new_m_gla_norm_g', 'new_m_gla_proj', 'new_m_mla_q_norm_g', 'new_m_mla_w_uq', 'new_m_mla_kv_norm_g', 'new_m_mla_w_ukv', 'new_m_mla_proj', 'new_m_w_out', 'new_m_final_norm_g', 'new_v_meta_tokens', 'new_v_norm_g', 'new_v_w_in', 'new_v_gla_gate_w', 'new_v_gla_gate_b', 'new_v_gla_norm_g', 'new_v_gla_proj', 'new_v_mla_q_norm_g', 'new_v_mla_w_uq', 'new_v_mla_kv_norm_g', 'new_v_mla_w_ukv', 'new_v_mla_proj', 'new_v_w_out', 'new_v_final_norm_g']
TWIN_LEAF_KINDS = {'loss': 'loss', 'grad_x': 'grad_x', 'grad_meta_tokens': 'grad_w', 'grad_norm_g': 'grad_w', 'grad_w_in': 'grad_w', 'grad_gla_gate_w': 'grad_w', 'grad_gla_gate_b': 'grad_w', 'grad_gla_norm_g': 'grad_w', 'grad_gla_proj': 'grad_w', 'grad_mla_q_norm_g': 'grad_w', 'grad_mla_w_uq': 'grad_w', 'grad_mla_kv_norm_g': 'grad_w', 'grad_mla_w_ukv': 'grad_w', 'grad_mla_proj': 'grad_w', 'grad_w_out': 'grad_w', 'grad_final_norm_g': 'grad_w', 'delta_meta_tokens': 'delta_w', 'delta_norm_g': 'delta_w', 'delta_w_in': 'delta_w', 'delta_gla_gate_w': 'delta_w', 'delta_gla_gate_b': 'delta_w', 'delta_gla_norm_g': 'delta_w', 'delta_gla_proj': 'delta_w', 'delta_mla_q_norm_g': 'delta_w', 'delta_mla_w_uq': 'delta_w', 'delta_mla_kv_norm_g': 'delta_w', 'delta_mla_w_ukv': 'delta_w', 'delta_mla_proj': 'delta_w', 'delta_w_out': 'delta_w', 'delta_final_norm_g': 'delta_w', 'new_m_meta_tokens': 'new_m', 'new_m_norm_g': 'new_m', 'new_m_w_in': 'new_m', 'new_m_gla_gate_w': 'new_m', 'new_m_gla_gate_b': 'new_m', 'new_m_gla_norm_g': 'new_m', 'new_m_gla_proj': 'new_m', 'new_m_mla_q_norm_g': 'new_m', 'new_m_mla_w_uq': 'new_m', 'new_m_mla_kv_norm_g': 'new_m', 'new_m_mla_w_ukv': 'new_m', 'new_m_mla_proj': 'new_m', 'new_m_w_out': 'new_m', 'new_m_final_norm_g': 'new_m', 'new_v_meta_tokens': 'new_v', 'new_v_norm_g': 'new_v', 'new_v_w_in': 'new_v', 'new_v_gla_gate_w': 'new_v', 'new_v_gla_gate_b': 'new_v', 'new_v_gla_norm_g': 'new_v', 'new_v_gla_proj': 'new_v', 'new_v_mla_q_norm_g': 'new_v', 'new_v_mla_w_uq': 'new_v', 'new_v_mla_kv_norm_g': 'new_v', 'new_v_mla_w_ukv': 'new_v', 'new_v_mla_proj': 'new_v', 'new_v_w_out': 'new_v', 'new_v_final_norm_g': 'new_v'}


def _forward(args):
    return _fwd_reference(*[args[k] for k in FWD_PARAMS])


def _output_shape():
    out = _jax.eval_shape(lambda: _forward(_fwd_setup_inputs(0)))
    return out.shape, out.dtype

N_MICROBATCH = 1
ADAM_LR = 0.001
ADAM_B1 = 0.9
ADAM_B2 = 0.999
ADAM_EPS = 1e-08
ADAM_WD = 0.01
ADAM_STEP = 10
PER_EXAMPLE_BATCH_AXIS = {'x': 0, 'loss_target': 0}
SHARED_INPUTS = []
_WEIGHT_DTYPES = {'meta_tokens': _jnp.float32, 'norm_g': _jnp.float32, 'w_in': _jnp.float32, 'gla_gate_w': _jnp.float32, 'gla_gate_b': _jnp.float32, 'gla_norm_g': _jnp.float32, 'gla_proj': _jnp.float32, 'mla_q_norm_g': _jnp.float32, 'mla_w_uq': _jnp.float32, 'mla_kv_norm_g': _jnp.float32, 'mla_w_ukv': _jnp.float32, 'mla_proj': _jnp.float32, 'w_out': _jnp.float32, 'final_norm_g': _jnp.float32}
MOMENT_SCALE = {'meta_tokens': 5.588295e-03, 'norm_g': 1.707194e-01, 'w_in': 6.432089e-02, 'gla_gate_w': 1.226427e-02, 'gla_gate_b': 4.826277e-02, 'gla_norm_g': 1.858181e-01, 'gla_proj': 7.720932e-02, 'mla_q_norm_g': 2.572007e-02, 'mla_w_uq': 1.052569e-02, 'mla_kv_norm_g': 5.451866e-02, 'mla_w_ukv': 1.241253e-02, 'mla_proj': 1.388488e-02, 'w_out': 7.842144e-02, 'final_norm_g': 6.393442e+01}


def _to_microbatches(a, axis):
    t = _jnp.moveaxis(a, axis, 0)
    t = t.reshape((N_MICROBATCH, t.shape[0] // N_MICROBATCH) + t.shape[1:])
    return _jnp.moveaxis(t, 1, axis + 1)


def setup_inputs(seed: int = 0) -> dict:
    inp = _fwd_setup_inputs(seed)
    key = _jax.random.fold_in(_jax.random.key(seed), 7919)
    shape, _ = _output_shape()
    out = dict(inp)
    out["loss_target"] = _jax.random.normal(_jax.random.fold_in(key, 0), shape, _jnp.float32)
    for i, name in enumerate(TWIN_WEIGHTS):
        w = inp[name].astype(_jnp.float32)
        if MOMENT_SCALE is None:
            s = _jnp.sqrt(_jnp.mean(_jnp.square(w)) + 1e-30)
        else:
            s = MOMENT_SCALE[name]
        km, kv = _jax.random.split(_jax.random.fold_in(key, i + 1))
        out[name] = w
        out["m_" + name] = s * _jax.random.normal(km, w.shape, _jnp.float32)
        out["v_" + name] = (s * s) * _jax.random.uniform(kv, w.shape, _jnp.float32, 0.5, 1.5)
    if N_MICROBATCH > 1:
        for name, axis in PER_EXAMPLE_BATCH_AXIS.items():
            out[name] = _to_microbatches(out[name], axis)
    return {'x': out['x'], 'meta_tokens': out['meta_tokens'], 'norm_g': out['norm_g'], 'w_in': out['w_in'], 'gla_gate_w': out['gla_gate_w'], 'gla_gate_b': out['gla_gate_b'], 'gla_norm_g': out['gla_norm_g'], 'gla_proj': out['gla_proj'], 'mla_q_norm_g': out['mla_q_norm_g'], 'mla_w_uq': out['mla_w_uq'], 'mla_kv_norm_g': out['mla_kv_norm_g'], 'mla_w_ukv': out['mla_w_ukv'], 'mla_proj': out['mla_proj'], 'w_out': out['w_out'], 'final_norm_g': out['final_norm_g'], 'loss_target': out['loss_target'], 'm_meta_tokens': out['m_meta_tokens'], 'm_norm_g': out['m_norm_g'], 'm_w_in': out['m_w_in'], 'm_gla_gate_w': out['m_gla_gate_w'], 'm_gla_gate_b': out['m_gla_gate_b'], 'm_gla_norm_g': out['m_gla_norm_g'], 'm_gla_proj': out['m_gla_proj'], 'm_mla_q_norm_g': out['m_mla_q_norm_g'], 'm_mla_w_uq': out['m_mla_w_uq'], 'm_mla_kv_norm_g': out['m_mla_kv_norm_g'], 'm_mla_w_ukv': out['m_mla_w_ukv'], 'm_mla_proj': out['m_mla_proj'], 'm_w_out': out['m_w_out'], 'm_final_norm_g': out['m_final_norm_g'], 'v_meta_tokens': out['v_meta_tokens'], 'v_norm_g': out['v_norm_g'], 'v_w_in': out['v_w_in'], 'v_gla_gate_w': out['v_gla_gate_w'], 'v_gla_gate_b': out['v_gla_gate_b'], 'v_gla_norm_g': out['v_gla_norm_g'], 'v_gla_proj': out['v_gla_proj'], 'v_mla_q_norm_g': out['v_mla_q_norm_g'], 'v_mla_w_uq': out['v_mla_w_uq'], 'v_mla_kv_norm_g': out['v_mla_kv_norm_g'], 'v_mla_w_ukv': out['v_mla_w_ukv'], 'v_mla_proj': out['v_mla_proj'], 'v_w_out': out['v_w_out'], 'v_final_norm_g': out['v_final_norm_g']}


def _loss(weights, diff, rest, loss_target):
    with _jax.named_scope("forward"):
        args = {**rest, TWIN_DIFF_INPUT: diff, **{k: w.astype(_WEIGHT_DTYPES[k]) for k, w in weights.items()}}
        y = _forward(args)
    with _jax.named_scope("loss_head"):
        err = _jnp.square(y.astype(_jnp.float32) - loss_target)
        return 0.5 * _jnp.sum(_jnp.mean(err, axis=-1)) if err.ndim else 0.5 * err


def _adamw(w, g, m, v):
    m = ADAM_B1 * m + (1.0 - ADAM_B1) * g
    v = ADAM_B2 * v + (1.0 - ADAM_B2) * _jnp.square(g)
    m_hat = m / (1.0 - ADAM_B1 ** ADAM_STEP)
    v_hat = v / (1.0 - ADAM_B2 ** ADAM_STEP)
    delta = -ADAM_LR * (m_hat / (_jnp.sqrt(v_hat) + ADAM_EPS) + ADAM_WD * w)
    return delta, m, v


def reference(x, meta_tokens, norm_g, w_in, gla_gate_w, gla_gate_b, gla_norm_g, gla_proj, mla_q_norm_g, mla_w_uq, mla_kv_norm_g, mla_w_ukv, mla_proj, w_out, final_norm_g, loss_target, m_meta_tokens, m_norm_g, m_w_in, m_gla_gate_w, m_gla_gate_b, m_gla_norm_g, m_gla_proj, m_mla_q_norm_g, m_mla_w_uq, m_mla_kv_norm_g, m_mla_w_ukv, m_mla_proj, m_w_out, m_final_norm_g, v_meta_tokens, v_norm_g, v_w_in, v_gla_gate_w, v_gla_gate_b, v_gla_norm_g, v_gla_proj, v_mla_q_norm_g, v_mla_w_uq, v_mla_kv_norm_g, v_mla_w_ukv, v_mla_proj, v_w_out, v_final_norm_g):
    given = dict(x=x, meta_tokens=meta_tokens, norm_g=norm_g, w_in=w_in, gla_gate_w=gla_gate_w, gla_gate_b=gla_gate_b, gla_norm_g=gla_norm_g, gla_proj=gla_proj, mla_q_norm_g=mla_q_norm_g, mla_w_uq=mla_w_uq, mla_kv_norm_g=mla_kv_norm_g, mla_w_ukv=mla_w_ukv, mla_proj=mla_proj, w_out=w_out, final_norm_g=final_norm_g, loss_target=loss_target, m_meta_tokens=m_meta_tokens, m_norm_g=m_norm_g, m_w_in=m_w_in, m_gla_gate_w=m_gla_gate_w, m_gla_gate_b=m_gla_gate_b, m_gla_norm_g=m_gla_norm_g, m_gla_proj=m_gla_proj, m_mla_q_norm_g=m_mla_q_norm_g, m_mla_w_uq=m_mla_w_uq, m_mla_kv_norm_g=m_mla_kv_norm_g, m_mla_w_ukv=m_mla_w_ukv, m_mla_proj=m_mla_proj, m_w_out=m_w_out, m_final_norm_g=m_final_norm_g, v_meta_tokens=v_meta_tokens, v_norm_g=v_norm_g, v_w_in=v_w_in, v_gla_gate_w=v_gla_gate_w, v_gla_gate_b=v_gla_gate_b, v_gla_norm_g=v_gla_norm_g, v_gla_proj=v_gla_proj, v_mla_q_norm_g=v_mla_q_norm_g, v_mla_w_uq=v_mla_w_uq, v_mla_kv_norm_g=v_mla_kv_norm_g, v_mla_w_ukv=v_mla_w_ukv, v_mla_proj=v_mla_proj, v_w_out=v_w_out, v_final_norm_g=v_final_norm_g)
    weights = {n: given[n] for n in TWIN_WEIGHTS}
    shared = {n: given[n] for n in SHARED_INPUTS}
    per_example = {n: given[n] for n in ['x']}
    grad_fn = _jax.value_and_grad(_loss, argnums=(0, 1))

    def one_microbatch(ex, loss_target):
        ex = dict(ex)
        diff = ex.pop(TWIN_DIFF_INPUT)
        return grad_fn(weights, diff, {**shared, **ex}, loss_target)

    if N_MICROBATCH == 1:
        loss, (grad_w, grad_x) = one_microbatch(per_example, given["loss_target"])
    else:
        def body(carry, xs):
            loss_sum, grad_sum = carry
            l_k, (gw_k, gx_k) = one_microbatch(xs[0], xs[1])
            with _jax.named_scope("update"):
                return (loss_sum + l_k, _jax.tree.map(_jnp.add, grad_sum, gw_k)), gx_k

        init = (_jnp.zeros((), _jnp.float32), _jax.tree.map(_jnp.zeros_like, weights))
        (loss, grad_w), grad_x = _jax.lax.scan(body, init, (per_example, given["loss_target"]))
    with _jax.named_scope("update"):
        delta_w, new_m, new_v = {}, {}, {}
        for n in TWIN_WEIGHTS:
            delta_w[n], new_m[n], new_v[n] = _adamw(weights[n], grad_w[n], given["m_" + n], given["v_" + n])
    return (loss, grad_x, *[grad_w[n] for n in TWIN_WEIGHTS], *[delta_w[n] for n in TWIN_WEIGHTS],
            *[new_m[n] for n in TWIN_WEIGHTS], *[new_v[n] for n in TWIN_WEIGHTS])
```

```python
import functools
import math

import jax
import jax.numpy as jnp
import numpy as np
from jax import lax
from jax.experimental import pallas as pl
from jax.experimental.pallas import tpu as pltpu

F32 = jnp.float32
BF16 = jnp.bfloat16

D = 1024
N_META = 16
FRONT = 112
HEAD_ROWS = FRONT + N_META
EPS = 1e-6

GLA_H, GLA_DK, GLA_DV, GLA_RANK, GLA_C = 4, 128, 256, 16, 64
GLA_NORMALIZER = 16.0
GLA_KW, GLA_VW = GLA_H * GLA_DK, GLA_H * GLA_DV
MLA_H, NOPE, ROPE, MLA_DV, Q_RANK, KV_RANK = 8, 128, 64, 128, 256, 128
MLA_QK = NOPE + ROPE
ROPE_BASE = 10000.0
QB = 128
SPLITS = (GLA_KW, GLA_KW, GLA_VW, GLA_RANK, GLA_VW, Q_RANK, KV_RANK, ROPE, MLA_H * MLA_DV, D, D)
IN_WIDTH = sum(SPLITS)

ADAM_LR, ADAM_B1, ADAM_B2, ADAM_EPS, ADAM_WD, ADAM_STEP = 0.001, 0.9, 0.999, 1e-08, 0.01, 10

LANES = 128
VMEM_CAP_V7X = 56 * 1024 * 1024
MESH = pl.DeviceIdType.MESH
NEG = -1e30

SEG_ROWS = (
    ("w_in", D * (IN_WIDTH // 4) // LANES),
    ("gla_gate_w", GLA_RANK * (GLA_KW // 4) // LANES),
    ("gla_proj", (GLA_VW // 4) * D // LANES),
    ("mla_w_uq", Q_RANK * (MLA_H * MLA_QK // 4) // LANES),
    ("mla_w_ukv", KV_RANK * (MLA_H * (NOPE + MLA_DV) // 4) // LANES),
    ("mla_proj", (MLA_H * MLA_DV // 4) * D // LANES),
    ("w_out", (D // 4) * D // LANES),
)
BIG_ROWS = sum(r for _, r in SEG_ROWS)
META_ROWS_F32 = N_META * (D // 4) // LANES
SMALL = (("norm_g", D), ("gla_gate_b", GLA_KW), ("gla_norm_g", GLA_DV), ("mla_q_norm_g", Q_RANK),
         ("mla_kv_norm_g", KV_RANK), ("final_norm_g", D))
SMALL_ROWS = sum(n for _, n in SMALL) // LANES
FLAT_ROWS = 20736
HALF_ROWS = FLAT_ROWS // 2


def _div_tile(n, target, mult):
    best = None
    for d in range(mult, min(n, target) + 1, mult):
        if n % d == 0:
            best = d
    assert best is not None, (n, target, mult)
    return best


def _params(sem, block_bytes, scratch_bytes=0):
    est = 2 * block_bytes + scratch_bytes + 12 * 1024 * 1024
    return pltpu.CompilerParams(dimension_semantics=sem, vmem_limit_bytes=int(min(max(est, 24 * 1024 * 1024), VMEM_CAP_V7X)))


def _nbytes(shape, dtype):
    return int(np.prod(shape)) * jnp.dtype(dtype).itemsize


def _sigmoid(x):
    return 1.0 / (1.0 + jnp.exp(-x))


def _nt(a, b):
    return lax.dot_general(a, b, (((1,), (1,)), ((), ())), preferred_element_type=F32)


def _tn(a, b):
    return lax.dot_general(a, b, (((0,), (0,)), ((), ())), preferred_element_type=F32)


def _nn(a, b):
    return jnp.dot(a, b, preferred_element_type=F32)


def _split3(x):
    a = x.astype(BF16)
    r = x - a.astype(F32)
    b = r.astype(BF16)
    c = (r - b.astype(F32)).astype(BF16)
    return a, b, c


def _mm(a, b, *, name, trans_a=False, out_dtype=F32, tm, tn, tk):
    if trans_a:
        K, M = a.shape
    else:
        M, K = a.shape
    assert b.shape[0] == K
    N = b.shape[1]
    assert M % tm == 0 and N % tn == 0 and K % tk == 0, (name, M, N, K, tm, tn, tk)
    nk = K // tk

    def body(a_ref, b_ref, o_ref, *scratch):
        av = a_ref[...].astype(BF16)
        bv = b_ref[...].astype(BF16)
        prod = _tn(av, bv) if trans_a else _nn(av, bv)
        if nk == 1:
            o_ref[...] = prod.astype(out_dtype)
        else:
            acc = scratch[0]
            k = pl.program_id(2)

            @pl.when(k == 0)
            def _():
                acc[...] = prod

            @pl.when(k > 0)
            def _():
                acc[...] += prod

            @pl.when(k == nk - 1)
            def _():
                o_ref[...] = acc[...].astype(out_dtype)

    if trans_a:
        a_spec = pl.BlockSpec((tk, tm), lambda i, j, k: (k, i))
    else:
        a_spec = pl.BlockSpec((tm, tk), lambda i, j, k: (i, k))
    blocks = (_nbytes((tm, tk), a.dtype) + _nbytes((tk, tn), b.dtype) + _nbytes((tm, tn), out_dtype))
    scratch = [pltpu.VMEM((tm, tn), F32)] if nk > 1 else []
    return pl.pallas_call(
        body,
        out_shape=jax.ShapeDtypeStruct((M, N), out_dtype),
        grid=(M // tm, N // tn, nk),
        in_specs=[a_spec, pl.BlockSpec((tk, tn), lambda i, j, k: (k, j))],
        out_specs=pl.BlockSpec((tm, tn), lambda i, j, k: (i, j)),
        scratch_shapes=scratch,
        compiler_params=_params(("parallel", "parallel", "arbitrary"), blocks + _nbytes((tm, tn), F32),
                                _nbytes((tm, tn), F32) if nk > 1 else 0),
        name=name,
    )(a, b)


def _rms_in(hp, g, tr):
    T = hp.shape[0]

    def body(h_ref, g_ref, u_ref):
        h = h_ref[...]
        r = lax.rsqrt(jnp.mean(h * h, axis=-1, keepdims=True) + EPS)
        u_ref[...] = (h * r * g_ref[...]).astype(BF16)

    return pl.pallas_call(
        body,
        out_shape=jax.ShapeDtypeStruct((T, D), BF16),
        grid=(T // tr,),
        in_specs=[pl.BlockSpec((tr, D), lambda i: (i, 0)), pl.BlockSpec((1, D), lambda i: (0, 0))],
        out_specs=pl.BlockSpec((tr, D), lambda i: (i, 0)),
        compiler_params=_params(("parallel",), _nbytes((tr, D), F32) * 2),
        name="rms_in",
    )(hp, g)


def _gla_gate(lr, wg, bg, valid):
    pre = _nn(lr.astype(BF16), wg) + bg
    logsig = jnp.minimum(pre, 0.0) - jnp.log(1.0 + jnp.exp(-jnp.abs(pre)))
    return pre, jnp.where(valid, logsig / GLA_NORMALIZER, 0.0)


def _tri_masks():
    ri = lax.broadcasted_iota(jnp.int32, (GLA_C, GLA_C), 0)
    ci = lax.broadcasted_iota(jnp.int32, (GLA_C, GLA_C), 1)
    return ci <= ri, ci >= ri


def _cumsum_rows(x, ones_mask):
    w = jnp.where(ones_mask, 1.0, 0.0).astype(BF16)
    a, b, c = _split3(x)
    return _nn(w, a) + _nn(w, b) + _nn(w, c)


def _gla_fwd(projA, projB, wg, bg, gn4, B, Lp):
    T = B * Lp
    NC = Lp // GLA_C
    C = GLA_C
    scale = GLA_DK ** -0.5

    def body(q_ref, k_ref, v_ref, lr_ref, z_ref, wg_ref, bg_ref, gn_ref, oa_ref, ya_ref, ssave_ref, st_ref):
        n = pl.program_id(1)

        @pl.when(n == 0)
        def _():
            st_ref[...] = jnp.zeros_like(st_ref)

        ssave_ref[0, 0] = st_ref[...]
        pos = n * C + lax.broadcasted_iota(jnp.int32, (C, 1), 0)
        _, glog = _gla_gate(lr_ref[...], wg_ref[...], bg_ref[...], pos >= FRONT)
        lower, _ = _tri_masks()
        bcum = _cumsum_rows(glog, lower)
        is_last = lax.broadcasted_iota(jnp.int32, (C, 1), 0) == C - 1
        for h in range(GLA_H):
            ks = slice(h * GLA_DK, (h + 1) * GLA_DK)
            vs = slice(h * GLA_DV, (h + 1) * GLA_DV)
            bh = bcum[:, ks]
            blast = jnp.sum(jnp.where(is_last, bh, 0.0), axis=0, keepdims=True)
            qh = q_ref[:, ks] * scale
            kh = k_ref[:, ks]
            qe = (qh * jnp.exp(bh)).astype(BF16)
            ke = (kh * jnp.exp(-bh)).astype(BF16)
            kl = (kh * jnp.exp(blast - bh)).astype(BF16)
            vh = v_ref[:, vs].astype(BF16)
            a = jnp.where(lower, _nt(qe, ke), 0.0).astype(BF16)
            st = st_ref[h]
            o = _nn(a, vh) + _nt(qe, st.astype(BF16))
            st_ref[h] = st * jnp.exp(blast) + _tn(vh, kl)
            oa_ref[:, vs] = o
            on = o * lax.rsqrt(jnp.mean(o * o, axis=-1, keepdims=True) + EPS) * gn_ref[:, vs]
            z = z_ref[:, vs]
            ya_ref[:, vs] = (on * (z * _sigmoid(z))).astype(BF16)

    row = lambda b, n: b * NC + n
    blocks = (_nbytes((C, 512), F32) * 2 + _nbytes((C, 1024), F32) * 3 + _nbytes((C, 1024), BF16)
              + _nbytes((GLA_H, GLA_DV, GLA_DK), F32) + _nbytes((128, 512), BF16))
    return pl.pallas_call(
        body,
        out_shape=(jax.ShapeDtypeStruct((T, GLA_VW), F32), jax.ShapeDtypeStruct((T, GLA_VW), BF16),
                   jax.ShapeDtypeStruct((B, NC, GLA_H, GLA_DV, GLA_DK), F32)),
        grid=(B, NC),
        in_specs=[
            pl.BlockSpec((C, 512), lambda b, n: (row(b, n), 10)),
            pl.BlockSpec((C, 512), lambda b, n: (row(b, n), 11)),
            pl.BlockSpec((C, 1024), lambda b, n: (row(b, n), 0)),
            pl.BlockSpec((C, 128), lambda b, n: (row(b, n), 3)),
            pl.BlockSpec((C, 1024), lambda b, n: (row(b, n), 1)),
            pl.BlockSpec((128, 512), lambda b, n: (0, 0)),
            pl.BlockSpec((1, 512), lambda b, n: (0, 0)),
            pl.BlockSpec((1, 1024), lambda b, n: (0, 0)),
        ],
        out_specs=(pl.BlockSpec((C, 1024), lambda b, n: (row(b, n), 0)),
                   pl.BlockSpec((C, 1024), lambda b, n: (row(b, n), 0)),
                   pl.BlockSpec((1, 1, GLA_H, GLA_DV, GLA_DK), lambda b, n: (b, n, 0, 0, 0))),
        scratch_shapes=[pltpu.VMEM((GLA_H, GLA_DV, GLA_DK), F32)],
        compiler_params=_params(("parallel", "arbitrary"), blocks, _nbytes((GLA_H, GLA_DV, GLA_DK), F32)),
        name="gla_fwd",
    )(projA, projA, projA, projB, projA, wg, bg, gn4)


def _swap_halves(x):
    lane = lax.broadcasted_iota(jnp.int32, x.shape, 1)
    return jnp.where((lane % 64) < 32, pltpu.roll(x, 96, 1), pltpu.roll(x, 32, 1))


def _mla_prep(projB, cos_t, sin_t, gq, gkv, wuq2, wukv, B, Lp, tr):
    T = B * Lp
    nt = Lp // tr
    HW = 2 * LANES

    def body(pb_ref, cos_ref, sin_ref, gq_ref, gkv_ref, wuq_ref, wukv_ref, q_ref, k_ref, v_ref, cqn_ref, ckvn_ref):
        cq = pb_ref[:, 0:Q_RANK]
        ckv = pb_ref[:, Q_RANK:Q_RANK + KV_RANK]
        kr = pb_ref[:, 512:640]
        cqn = (cq * lax.rsqrt(jnp.mean(cq * cq, axis=-1, keepdims=True) + EPS) * gq_ref[...]).astype(BF16)
        ckvn = (ckv * lax.rsqrt(jnp.mean(ckv * ckv, axis=-1, keepdims=True) + EPS) * gkv_ref[...]).astype(BF16)
        cqn_ref[...] = cqn
        ckvn_ref[...] = ckvn
        qf = _nn(cqn, wuq_ref[...])
        kvf = _nn(ckvn, wukv_ref[...])
        cs = cos_ref[...]
        sn = sin_ref[...]
        rope = lambda t: t * cs + _swap_halves(t) * sn
        kr_r = rope(kr).astype(BF16)
        for h in range(MLA_H):
            q_ref[:, h * HW:h * HW + LANES] = qf[:, h * HW:h * HW + LANES].astype(BF16)
            q_ref[:, h * HW + LANES:(h + 1) * HW] = rope(qf[:, h * HW + LANES:(h + 1) * HW]).astype(BF16)
            k_ref[:, h * HW:h * HW + LANES] = kvf[:, h * HW:h * HW + LANES].astype(BF16)
            k_ref[:, h * HW + LANES:(h + 1) * HW] = kr_r
            v_ref[:, h * MLA_DV:(h + 1) * MLA_DV] = kvf[:, h * HW + LANES:(h + 1) * HW].astype(BF16)

    blocks = (_nbytes((tr, 640), F32) + 2 * _nbytes((tr, 128), F32) + _nbytes((Q_RANK, 2048), BF16)
              + _nbytes((KV_RANK, 2048), BF16) + _nbytes((tr, 2048 * 2 + 1024 + 384), BF16)
              + 2 * _nbytes((tr, 2048), F32))
    return pl.pallas_call(
        body,
        out_shape=(jax.ShapeDtypeStruct((T, MLA_H * HW), BF16), jax.ShapeDtypeStruct((T, MLA_H * HW), BF16),
                   jax.ShapeDtypeStruct((T, MLA_H * MLA_DV), BF16), jax.ShapeDtypeStruct((T, Q_RANK), BF16),
                   jax.ShapeDtypeStruct((T, KV_RANK), BF16)),
        grid=(B, nt),
        in_specs=[
            pl.BlockSpec((tr, 640), lambda b, j: (b * nt + j, 0)),
            pl.BlockSpec((tr, 128), lambda b, j: (j, 0)),
            pl.BlockSpec((tr, 128), lambda b, j: (j, 0)),
            pl.BlockSpec((1, Q_RANK), lambda b, j: (0, 0)),
            pl.BlockSpec((1, KV_RANK), lambda b, j: (0, 0)),
            pl.BlockSpec((Q_RANK, 2048), lambda b, j: (0, 0)),
            pl.BlockSpec((KV_RANK, 2048), lambda b, j: (0, 0)),
        ],
        out_specs=(pl.BlockSpec((tr, 2048), lambda b, j: (b * nt + j, 0)),
                   pl.BlockSpec((tr, 2048), lambda b, j: (b * nt + j, 0)),
                   pl.BlockSpec((tr, 1024), lambda b, j: (b * nt + j, 0)),
                   pl.BlockSpec((tr, Q_RANK), lambda b, j: (b * nt + j, 0)),
                   pl.BlockSpec((tr, KV_RANK), lambda b, j: (b * nt + j, 0))),
        compiler_params=_params(("parallel", "parallel"), blocks),
        name="mla_prep",
    )(projB, cos_t, sin_t, gq, gkv, wuq2, wukv)


def _attn_mask(row, col):
    return (col <= row) & ((col >= FRONT) | (row < FRONT))


def _attn_fwd(q_att, k_att, v_att, projA, B, Lp):
    T = B * Lp
    NQ = Lp // QB
    scale = 1.0 / math.sqrt(MLA_QK)

    def body(q_ref, k_ref, v_ref, mz_ref, o_ref, yb_ref, lser_ref, lsec_ref):
        qi = pl.program_id(2)
        q = q_ref[...]
        row = qi * QB + lax.broadcasted_iota(jnp.int32, (QB, QB), 0)
        coli = lax.broadcasted_iota(jnp.int32, (QB, QB), 1)

        def step(kj, carry):
            m, l, acc = carry
            off = pl.multiple_of(kj * QB, QB)
            kb = k_ref[pl.ds(off, QB), :]
            vb = v_ref[pl.ds(off, QB), :]
            s = _nt(q, kb) * scale
            s = jnp.where(_attn_mask(row, kj * QB + coli), s, NEG)
            m_new = jnp.maximum(m, jnp.max(s, axis=-1, keepdims=True))
            alpha = jnp.exp(m - m_new)
            p = jnp.exp(s - m_new)
            l = alpha * l + jnp.sum(p, axis=-1, keepdims=True)
            acc = alpha * acc + _nn(p.astype(BF16), vb)
            return m_new, l, acc

        init = (jnp.full((QB, 1), NEG, F32), jnp.zeros((QB, 1), F32), jnp.zeros((QB, MLA_DV), F32))
        m, l, acc = lax.fori_loop(0, qi + 1, step, init)
        o = acc / l
        o_ref[...] = o
        z = mz_ref[...]
        yb_ref[...] = (o * (z * _sigmoid(z))).astype(BF16)
        lse = jnp.broadcast_to(m + jnp.log(l), (QB, QB))
        lser_ref[0, 0] = lse
        lsec_ref[0, 0, pl.ds(qi, 1), :] = jnp.transpose(lse)[0:1, :]

    blocks = (_nbytes((QB, 256), BF16) + _nbytes((Lp, 256), BF16) + _nbytes((Lp, 128), BF16)
              + 3 * _nbytes((QB, 128), F32) + _nbytes((NQ, 128), F32))
    return pl.pallas_call(
        body,
        out_shape=(jax.ShapeDtypeStruct((T, MLA_H * MLA_DV), F32), jax.ShapeDtypeStruct((T, MLA_H * MLA_DV), BF16),
                   jax.ShapeDtypeStruct((B, MLA_H, Lp, LANES), F32), jax.ShapeDtypeStruct((B, MLA_H, NQ, LANES), F32)),
        grid=(B, MLA_H, NQ),
        in_specs=[
            pl.BlockSpec((QB, 256), lambda b, h, i: (b * NQ + i, h)),
            pl.BlockSpec((Lp, 256), lambda b, h, i: (b, h)),
            pl.BlockSpec((Lp, 128), lambda b, h, i: (b, h)),
            pl.BlockSpec((QB, 128), lambda b, h, i: (b * NQ + i, 16 + h)),
        ],
        out_specs=(pl.BlockSpec((QB, 128), lambda b, h, i: (b * NQ + i, h)),
                   pl.BlockSpec((QB, 128), lambda b, h, i: (b * NQ + i, h)),
                   pl.BlockSpec((1, 1, QB, LANES), lambda b, h, i: (b, h, i, 0)),
                   pl.BlockSpec((1, 1, NQ, LANES), lambda b, h, i: (b, h, 0, 0))),
        compiler_params=_params(("parallel", "parallel", "arbitrary"), blocks),
        name="attn_fwd",
    )(q_att, k_att, v_att, projA)


def _merge_fwd(projA, ya, yb, tr):
    T = ya.shape[0]

    def body(gg_ref, gm_ref, ya_ref, yb_ref, o_ref):
        o_ref[...] = (_sigmoid(gg_ref[...]) * ya_ref[...] + _sigmoid(gm_ref[...]) * yb_ref[...]).astype(BF16)

    spec = lambda c: pl.BlockSpec((tr, D), lambda i: (i, c))
    return pl.pallas_call(
        body,
        out_shape=jax.ShapeDtypeStruct((T, D), BF16),
        grid=(T // tr,),
        in_specs=[spec(3), spec(4), spec(0), spec(0)],
        out_specs=spec(0),
        compiler_params=_params(("parallel",), 5 * _nbytes((tr, D), F32)),
        name="merge_fwd",
    )(projA, projA, ya, yb)


def _final_loss(hp, mo, gf, tgt, B, Lp):
    T = B * Lp
    NQ = Lp // QB

    def body(h_ref, mo_ref, gf_ref, t_ref, dh_ref, dhb_ref, loss_ref, dgf_ref):
        b = pl.program_id(0)
        j = pl.program_id(1)

        @pl.when((b == 0) & (j == 0))
        def _():
            loss_ref[...] = jnp.zeros_like(loss_ref)
            dgf_ref[...] = jnp.zeros_like(dgf_ref)

        h1 = h_ref[...] + mo_ref[...]
        r = lax.rsqrt(jnp.mean(h1 * h1, axis=-1, keepdims=True) + EPS)
        hn = h1 * r
        gfv = gf_ref[...]
        diff = jnp.where(j > 0, hn * gfv - t_ref[0], 0.0)
        loss_ref[...] += (0.5 / D) * jnp.sum(jnp.sum(diff * diff, axis=-1, keepdims=True), axis=0, keepdims=True)
        dout = diff * (1.0 / D)
        dgf_ref[...] += jnp.sum(dout * hn, axis=0, keepdims=True)
        dhn = dout * gfv
        dh = r * (dhn - hn * jnp.mean(dhn * hn, axis=-1, keepdims=True))
        dh_ref[...] = dh
        dhb_ref[...] = dh.astype(BF16)

    rows = pl.BlockSpec((QB, D), lambda b, j: (b * NQ + j, 0))
    return pl.pallas_call(
        body,
        out_shape=(jax.ShapeDtypeStruct((T, D), F32), jax.ShapeDtypeStruct((T, D), BF16),
                   jax.ShapeDtypeStruct((1, 1), F32), jax.ShapeDtypeStruct((1, D), F32)),
        grid=(B, NQ),
        in_specs=[rows, rows, pl.BlockSpec((1, D), lambda b, j: (0, 0)),
                  pl.BlockSpec((1, QB, D), lambda b, j: (b, jnp.maximum(j - 1, 0), 0))],
        out_specs=(rows, rows, pl.BlockSpec((1, 1), lambda b, j: (0, 0)), pl.BlockSpec((1, D), lambda b, j: (0, 0))),
        compiler_params=_params(("arbitrary", "arbitrary"), 5 * _nbytes((QB, D), F32)),
        name="final_loss",
    )(hp, mo, gf, tgt)


def _merge_bwd(dm, projA, ya, yb, tr):
    T = dm.shape[0]

    def body(dm_ref, gg_ref, gm_ref, ya_ref, yb_ref, dya_ref, dyb_ref, da_ref):
        d = dm_ref[...]
        sg = _sigmoid(gg_ref[...])
        sm = _sigmoid(gm_ref[...])
        dya_ref[...] = (d * sg).astype(BF16)
        dyb_ref[...] = (d * sm).astype(BF16)
        da_ref[:, 0:D] = (d * ya_ref[...] * (sg * (1.0 - sg))).astype(BF16)
        da_ref[:, D:2 * D] = (d * yb_ref[...] * (sm * (1.0 - sm))).astype(BF16)

    spec = lambda c: pl.BlockSpec((tr, D), lambda i: (i, c))
    return pl.pallas_call(
        body,
        out_shape=(jax.ShapeDtypeStruct((T, D), BF16), jax.ShapeDtypeStruct((T, D), BF16),
                   jax.ShapeDtypeStruct((T, 2 * D), BF16)),
        grid=(T // tr,),
        in_specs=[spec(0), spec(3), spec(4), spec(0), spec(0)],
        out_specs=(spec(0), spec(0), pl.BlockSpec((tr, 2 * D), lambda i: (i, 0))),
        compiler_params=_params(("parallel",), 8 * _nbytes((tr, D), F32)),
        name="merge_bwd",
    )(dm, projA, projA, ya, yb)


def _gla_out_bwd(dyin, oa, projA, gn4, tr):
    T = dyin.shape[0]
    nsteps = T // tr

    def body(dy_ref, oa_ref, z_ref, gn_ref, do_ref, dz_ref, dgn_ref, acc_ref):
        i = pl.program_id(0)

        @pl.when(i == 0)
        def _():
            acc_ref[...] = jnp.zeros_like(acc_ref)

        for h in range(GLA_H):
            vs = slice(h * GLA_DV, (h + 1) * GLA_DV)
            dy = dy_ref[:, vs]
            o = oa_ref[:, vs]
            z = z_ref[:, vs]
            gn = gn_ref[:, vs]
            s = _sigmoid(z)
            ra = lax.rsqrt(jnp.mean(o * o, axis=-1, keepdims=True) + EPS)
            on = o * ra
            don = dy * (z * s)
            t = don * gn
            do_ref[:, vs] = (ra * (t - on * jnp.mean(t * on, axis=-1, keepdims=True))).astype(BF16)
            dz_ref[:, vs] = (dy * (on * gn) * (s * (1.0 + z * (1.0 - s)))).astype(BF16)
            acc_ref[:, vs] += jnp.sum(don * on, axis=0, keepdims=True)

        @pl.when(i == nsteps - 1)
        def _():
            a = acc_ref[...]
            dgn_ref[...] = a[:, 0:256] + a[:, 256:512] + a[:, 512:768] + a[:, 768:1024]

    spec = lambda c: pl.BlockSpec((tr, D), lambda i: (i, c))
    return pl.pallas_call(
        body,
        out_shape=(jax.ShapeDtypeStruct((T, D), BF16), jax.ShapeDtypeStruct((T, D), BF16),
                   jax.ShapeDtypeStruct((1, GLA_DV), F32)),
        grid=(nsteps,),
        in_specs=[spec(0), spec(0), spec(1), pl.BlockSpec((1, D), lambda i: (0, 0))],
        out_specs=(spec(0), spec(0), pl.BlockSpec((1, GLA_DV), lambda i: (0, 0))),
        scratch_shapes=[pltpu.VMEM((1, D), F32)],
        compiler_params=_params(("arbitrary",), 6 * _nbytes((tr, D), F32)),
        name="gla_out_bwd",
    )(dyin, oa, projA, gn4)


def _gla_bwd(projA, projB, ssave, doa, wg, bg, B, Lp):
    T = B * Lp
    NC = Lp // GLA_C
    C = GLA_C
    scale = GLA_DK ** -0.5
    WC = 2304

    def body(q_ref, k_ref, v_ref, lr_ref, ss_ref, do_ref, wg_ref, bg_ref, dc_ref, dwg_ref, dbg_ref, dst_ref):
        b = pl.program_id(0)
        i = pl.program_id(1)
        n = NC - 1 - i

        @pl.when(i == 0)
        def _():
            dst_ref[...] = jnp.zeros_like(dst_ref)

        @pl.when((b == 0) & (i == 0))
        def _():
            dwg_ref[...] = jnp.zeros_like(dwg_ref)
            dbg_ref[...] = jnp.zeros_like(dbg_ref)

        pos = n * C + lax.broadcasted_iota(jnp.int32, (C, 1), 0)
        valid = pos >= FRONT
        lr = lr_ref[...]
        pre, glog = _gla_gate(lr, wg_ref[...], bg_ref[...], valid)
        lower, upper = _tri_masks()
        bcum = _cumsum_rows(glog, lower)
        is_last = lax.broadcasted_iota(jnp.int32, (C, 1), 0) == C - 1
        db_parts = []
        for h in range(GLA_H):
            ks = slice(h * GLA_DK, (h + 1) * GLA_DK)
            vs = slice(h * GLA_DV, (h + 1) * GLA_DV)
            bh = bcum[:, ks]
            blast = jnp.sum(jnp.where(is_last, bh, 0.0), axis=0, keepdims=True)
            eb, enb, ekl, ebl = jnp.exp(bh), jnp.exp(-bh), jnp.exp(blast - bh), jnp.exp(blast)
            qh = q_ref[:, ks] * scale
            kh = k_ref[:, ks]
            qe_f, ke_f, kl_f = qh * eb, kh * enb, kh * ekl
            qe, ke, kl = qe_f.astype(BF16), ke_f.astype(BF16), kl_f.astype(BF16)
            vh = v_ref[:, vs].astype(BF16)
            doh = do_ref[:, vs]
            st = ss_ref[0, 0, h]
            dst = dst_ref[h]
            st_b, dst_b = st.astype(BF16), dst.astype(BF16)
            da = jnp.where(lower, _nt(doh, vh), 0.0).astype(BF16)
            da_t = jnp.where(upper, _nt(vh, doh), 0.0).astype(BF16)
            a_t = jnp.where(upper, _nt(ke, qe), 0.0).astype(BF16)
            dqe = _nn(da, ke) + _nn(doh, st_b)
            dke = _nn(da_t, qe)
            dvh = _nn(a_t, doh) + _nt(kl, dst_b)
            dkl = _nn(vh, dst_b)
            dst_ref[h] = dst * ebl + _tn(doh, qe)
            deb = jnp.sum(st * dst, axis=0, keepdims=True)
            db = dqe * qe_f - dke * ke_f - dkl * kl_f
            db_last = jnp.sum(dkl * kl_f, axis=0, keepdims=True) + deb * ebl
            db_parts.append(db + jnp.where(is_last, db_last, 0.0))
            dc_ref[:, vs] = dvh.astype(BF16)
            dc_ref[:, 1024 + h * GLA_DK:1024 + (h + 1) * GLA_DK] = (dqe * eb * scale).astype(BF16)
            dc_ref[:, 1536 + h * GLA_DK:1536 + (h + 1) * GLA_DK] = (dke * enb + dkl * ekl).astype(BF16)
        dglog = _cumsum_rows(jnp.concatenate(db_parts, axis=1), upper)
        dpre = jnp.where(valid, dglog * (1.0 / GLA_NORMALIZER) / (1.0 + jnp.exp(pre)), 0.0)
        dpre_b = dpre.astype(BF16)
        dc_ref[:, 2048:2176] = _nt(dpre_b, wg_ref[...]).astype(BF16)
        dc_ref[:, 2176:2304] = jnp.zeros((C, 128), BF16)
        dwg_ref[...] += _tn(lr.astype(BF16), dpre_b)
        dbg_ref[...] += jnp.sum(dpre, axis=0, keepdims=True)

    row = lambda b, i: b * NC + (NC - 1 - i)
    blocks = (_nbytes((C, 512), F32) * 2 + _nbytes((C, 1024), F32) + _nbytes((C, 1024), BF16)
              + _nbytes((GLA_H, GLA_DV, GLA_DK), F32) + _nbytes((C, WC), BF16) + 3 * _nbytes((128, 512), F32))
    return pl.pallas_call(
        body,
        out_shape=(jax.ShapeDtypeStruct((T, WC), BF16), jax.ShapeDtypeStruct((128, GLA_KW), F32),
                   jax.ShapeDtypeStruct((1, GLA_KW), F32)),
        grid=(B, NC),
        in_specs=[
            pl.BlockSpec((C, 512), lambda b, i: (row(b, i), 10)),
            pl.BlockSpec((C, 512), lambda b, i: (row(b, i), 11)),
            pl.BlockSpec((C, 1024), lambda b, i: (row(b, i), 0)),
            pl.BlockSpec((C, 128), lambda b, i: (row(b, i), 3)),
            pl.BlockSpec((1, 1, GLA_H, GLA_DV, GLA_DK), lambda b, i: (b, NC - 1 - i, 0, 0, 0)),
            pl.BlockSpec((C, 1024), lambda b, i: (row(b, i), 0)),
            pl.BlockSpec((128, 512), lambda b, i: (0, 0)),
            pl.BlockSpec((1, 512), lambda b, i: (0, 0)),
        ],
        out_specs=(pl.BlockSpec((C, WC), lambda b, i: (row(b, i), 0)),
                   pl.BlockSpec((128, GLA_KW), lambda b, i: (0, 0)),
                   pl.BlockSpec((1, GLA_KW), lambda b, i: (0, 0))),
        scratch_shapes=[pltpu.VMEM((GLA_H, GLA_DV, GLA_DK), F32)],
        compiler_params=_params(("arbitrary", "arbitrary"), blocks, _nbytes((GLA_H, GLA_DV, GLA_DK), F32)),
        name="gla_bwd",
    )(projA, projA, projA, projB, ssave, doa, wg, bg)


def _attn_bwd_pre(dyin, projA, ob, B, Lp):
    T = B * Lp
    NQ = Lp // QB

    def body(dy_ref, z_ref, o_ref, do_ref, dz_ref, dr_ref, dcol_ref):
        j = pl.program_id(1)
        for h in range(MLA_H):
            hs = slice(h * MLA_DV, (h + 1) * MLA_DV)
            dy = dy_ref[:, hs]
            z = z_ref[:, hs]
            o = o_ref[:, hs]
            s = _sigmoid(z)
            do = dy * (z * s)
            do_ref[:, hs] = do.astype(BF16)
            dz_ref[:, hs] = (dy * o * (s * (1.0 + z * (1.0 - s)))).astype(BF16)
            dl = jnp.broadcast_to(jnp.sum(do * o, axis=-1, keepdims=True), (QB, QB))
            dr_ref[0, h] = dl
            dcol_ref[0, h, pl.ds(j, 1), :] = jnp.transpose(dl)[0:1, :]

    rows = lambda c: pl.BlockSpec((QB, D), lambda b, j: (b * NQ + j, c))
    return pl.pallas_call(
        body,
        out_shape=(jax.ShapeDtypeStruct((T, D), BF16), jax.ShapeDtypeStruct((T, D), BF16),
                   jax.ShapeDtypeStruct((B, MLA_H, Lp, LANES), F32), jax.ShapeDtypeStruct((B, MLA_H, NQ, LANES), F32)),
        grid=(B, NQ),
        in_specs=[rows(0), rows(2), rows(0)],
        out_specs=(rows(0), rows(0), pl.BlockSpec((1, MLA_H, QB, LANES), lambda b, j: (b, 0, j, 0)),
                   pl.BlockSpec((1, MLA_H, NQ, LANES), lambda b, j: (b, 0, 0, 0))),
        compiler_params=_params(("parallel", "arbitrary"), 6 * _nbytes((QB, D), F32)),
        name="attn_bwd_pre",
    )(dyin, projA, ob)


def _attn_bwd_dq(q_att, k_att, v_att, do, lse_r, delta_r, B, Lp):
    T = B * Lp
    NQ = Lp // QB
    scale = 1.0 / math.sqrt(MLA_QK)

    def body(q_ref, k_ref, v_ref, do_ref, lse_ref, dl_ref, dq_ref):
        qi = pl.program_id(2)
        q = q_ref[...]
        dob = do_ref[...]
        lse = lse_ref[0, 0]
        delta = dl_ref[0, 0]
        row = qi * QB + lax.broadcasted_iota(jnp.int32, (QB, QB), 0)
        coli = lax.broadcasted_iota(jnp.int32, (QB, QB), 1)

        def step(kj, dq):
            off = pl.multiple_of(kj * QB, QB)
            kb = k_ref[pl.ds(off, QB), :]
            vb = v_ref[pl.ds(off, QB), :]
            s = _nt(q, kb) * scale
            p = jnp.where(_attn_mask(row, kj * QB + coli), jnp.exp(s - lse), 0.0)
            ds = p * (_nt(dob, vb) - delta) * scale
            return dq + _nn(ds.astype(BF16), kb)

        dq_ref[...] = lax.fori_loop(0, qi + 1, step, jnp.zeros((QB, 256), F32))

    blocks = (_nbytes((QB, 256), BF16) + _nbytes((Lp, 256), BF16) + _nbytes((Lp, 128), BF16)
              + 4 * _nbytes((QB, 128), F32) + _nbytes((QB, 256), F32))
    return pl.pallas_call(
        body,
        out_shape=jax.ShapeDtypeStruct((T, MLA_H * 256), F32),
        grid=(B, MLA_H, NQ),
        in_specs=[
            pl.BlockSpec((QB, 256), lambda b, h, i: (b * NQ + i, h)),
            pl.BlockSpec((Lp, 256), lambda b, h, i: (b, h)),
            pl.BlockSpec((Lp, 128), lambda b, h, i: (b, h)),
            pl.BlockSpec((QB, 128), lambda b, h, i: (b * NQ + i, h)),
            pl.BlockSpec((1, 1, QB, LANES), lambda b, h, i: (b, h, i, 0)),
            pl.BlockSpec((1, 1, QB, LANES), lambda b, h, i: (b, h, i, 0)),
        ],
        out_specs=pl.BlockSpec((QB, 256), lambda b, h, i: (b * NQ + i, h)),
        compiler_params=_params(("parallel", "parallel", "parallel"), blocks),
        name="attn_bwd_dq",
    )(q_att, k_att, v_att, do, lse_r, delta_r)


def _attn_bwd_dkv(q_att, k_att, v_att, do, lse_c, delta_c, B, Lp):
    T = B * Lp
    NQ = Lp // QB
    scale = 1.0 / math.sqrt(MLA_QK)

    def body(q_ref, k_ref, v_ref, do_ref, lse_ref, dl_ref, dk_ref, dv_ref):
        kj = pl.program_id(2)
        kb = k_ref[...]
        vb = v_ref[...]
        col = kj * QB + lax.broadcasted_iota(jnp.int32, (QB, QB), 0)
        rowi = lax.broadcasted_iota(jnp.int32, (QB, QB), 1)

        def step(qi, carry):
            dk, dv = carry
            off = pl.multiple_of(qi * QB, QB)
            qb = q_ref[pl.ds(off, QB), :]
            dob = do_ref[pl.ds(off, QB), :]
            lse = lse_ref[0, 0, pl.ds(qi, 1), :]
            delta = dl_ref[0, 0, pl.ds(qi, 1), :]
            s_t = _nt(kb, qb) * scale
            p_t = jnp.where(_attn_mask(qi * QB + rowi, col), jnp.exp(s_t - lse), 0.0)
            dv = dv + _nn(p_t.astype(BF16), dob)
            ds_t = p_t * (_nt(vb, dob) - delta) * scale
            dk = dk + _nn(ds_t.astype(BF16), qb)
            return dk, dv

        dk, dv = lax.fori_loop(kj, NQ, step, (jnp.zeros((QB, 256), F32), jnp.zeros((QB, MLA_DV), F32)))
        dk_ref[...] = dk
        dv_ref[...] = dv

    blocks = (_nbytes((Lp, 256), BF16) + _nbytes((Lp, 128), BF16) + _nbytes((QB, 384), BF16)
              + 2 * _nbytes((NQ, 128), F32) + _nbytes((QB, 384), F32))
    return pl.pallas_call(
        body,
        out_shape=(jax.ShapeDtypeStruct((T, MLA_H * 256), F32), jax.ShapeDtypeStruct((T, MLA_H * MLA_DV), F32)),
        grid=(B, MLA_H, NQ),
        in_specs=[
            pl.BlockSpec((Lp, 256), lambda b, h, j: (b, h)),
            pl.BlockSpec((QB, 256), lambda b, h, j: (b * NQ + j, h)),
            pl.BlockSpec((QB, 128), lambda b, h, j: (b * NQ + j, h)),
            pl.BlockSpec((Lp, 128), lambda b, h, j: (b, h)),
            pl.BlockSpec((1, 1, NQ, LANES), lambda b, h, j: (b, h, 0, 0)),
            pl.BlockSpec((1, 1, NQ, LANES), lambda b, h, j: (b, h, 0, 0)),
        ],
        out_specs=(pl.BlockSpec((QB, 256), lambda b, h, j: (b * NQ + j, h)),
                   pl.BlockSpec((QB, 128), lambda b, h, j: (b * NQ + j, h))),
        compiler_params=_params(("parallel", "parallel", "parallel"), blocks),
        name="attn_bwd_dkv",
    )(q_att, k_att, v_att, do, lse_c, delta_c)


def _mla_bwd_post(dq, dk, dv, projB, cos_t, sin_t, gq, gkv, wuq2_t, wukv_t, B, Lp, tr):
    T = B * Lp
    nt = Lp // tr
    HW = 2 * LANES

    def body(dq_ref, dk_ref, dv_ref, pb_ref, cos_ref, sin_ref, gq_ref, gkv_ref, wuq_ref, wukv_ref,
             dqf_ref, dkvf_ref, de_ref, dgq_ref, dgkv_ref):
        first = (pl.program_id(0) == 0) & (pl.program_id(1) == 0)

        @pl.when(first)
        def _():
            dgq_ref[...] = jnp.zeros_like(dgq_ref)
            dgkv_ref[...] = jnp.zeros_like(dgkv_ref)

        cs = cos_ref[...]
        sn = sin_ref[...]
        rope_t = lambda t: t * cs + _swap_halves(t * sn)
        dkr = jnp.zeros((tr, LANES), F32)
        for h in range(MLA_H):
            dqf_ref[:, h * HW:h * HW + LANES] = dq_ref[:, h * HW:h * HW + LANES].astype(BF16)
            dqf_ref[:, h * HW + LANES:(h + 1) * HW] = rope_t(dq_ref[:, h * HW + LANES:(h + 1) * HW]).astype(BF16)
            dkvf_ref[:, h * HW:h * HW + LANES] = dk_ref[:, h * HW:h * HW + LANES].astype(BF16)
            dkvf_ref[:, h * HW + LANES:(h + 1) * HW] = dv_ref[:, h * MLA_DV:(h + 1) * MLA_DV].astype(BF16)
            dkr = dkr + dk_ref[:, h * HW + LANES:(h + 1) * HW]

        def norm_bwd(x, dn, g):
            r = lax.rsqrt(jnp.mean(x * x, axis=-1, keepdims=True) + EPS)
            xn = x * r
            t = dn * g
            return r * (t - xn * jnp.mean(t * xn, axis=-1, keepdims=True)), jnp.sum(dn * xn, axis=0, keepdims=True)

        dcq, dgq = norm_bwd(pb_ref[:, 0:Q_RANK], _nn(dqf_ref[...], wuq_ref[...]), gq_ref[...])
        dckv, dgkv = norm_bwd(pb_ref[:, Q_RANK:Q_RANK + KV_RANK], _nn(dkvf_ref[...], wukv_ref[...]), gkv_ref[...])
        dgq_ref[...] += dgq
        dgkv_ref[...] += dgkv
        de_ref[:, 0:Q_RANK] = dcq.astype(BF16)
        de_ref[:, Q_RANK:Q_RANK + KV_RANK] = dckv.astype(BF16)
        de_ref[:, 384:512] = rope_t(dkr).astype(BF16)

    rows = lambda w: pl.BlockSpec((tr, w), lambda b, j: (b * nt + j, 0))
    const = lambda s: pl.BlockSpec(s, lambda b, j: (0, 0))
    blocks = (2 * _nbytes((tr, 2048), F32) + _nbytes((tr, 1024), F32) + _nbytes((tr, 640), F32)
              + 2 * _nbytes((tr, 2048), BF16) + _nbytes((2048, 384), BF16) + 2 * _nbytes((tr, 2048), F32))
    return pl.pallas_call(
        body,
        out_shape=(jax.ShapeDtypeStruct((T, 2048), BF16), jax.ShapeDtypeStruct((T, 2048), BF16),
                   jax.ShapeDtypeStruct((T, 512), BF16), jax.ShapeDtypeStruct((1, Q_RANK), F32),
                   jax.ShapeDtypeStruct((1, KV_RANK), F32)),
        grid=(B, nt),
        in_specs=[rows(2048), rows(2048), rows(1024), rows(640),
                  pl.BlockSpec((tr, 128), lambda b, j: (j, 0)), pl.BlockSpec((tr, 128), lambda b, j: (j, 0)),
                  const((1, Q_RANK)), const((1, KV_RANK)), const((2048, Q_RANK)), const((2048, KV_RANK))],
        out_specs=(rows(2048), rows(2048), rows(512), const((1, Q_RANK)), const((1, KV_RANK))),
        compiler_params=_params(("arbitrary", "arbitrary"), blocks),
        name="mla_bwd_post",
    )(dq, dk, dv, projB, cos_t, sin_t, gq, gkv, wuq2_t, wukv_t)


def _du_matmul(dparts, wparts, tm):
    T = dparts[0].shape[0]
    n = len(dparts)

    def body(*refs):
        d_refs, w_refs, o_ref = refs[:n], refs[n:2 * n], refs[2 * n]
        acc = _nn(d_refs[0][...], w_refs[0][...])
        for d_ref, w_ref in zip(d_refs[1:], w_refs[1:]):
            acc = acc + _nn(d_ref[...], w_ref[...])
        o_ref[...] = acc

    widths = [d.shape[1] for d in dparts]
    blocks = sum(_nbytes((tm, w), BF16) + _nbytes((w, D), BF16) for w in widths) + _nbytes((tm, D), F32)
    return pl.pallas_call(
        body,
        out_shape=jax.ShapeDtypeStruct((T, D), F32),
        grid=(T // tm,),
        in_specs=[pl.BlockSpec((tm, w), lambda i: (i, 0)) for w in widths]
        + [pl.BlockSpec((w, D), lambda i: (0, 0)) for w in widths],
        out_specs=pl.BlockSpec((tm, D), lambda i: (i, 0)),
        compiler_params=_params(("parallel",), blocks),
        name="du_matmul",
    )(*dparts, *wparts)


def _in_norm_bwd(hp, dh1, du, g, B, Lp, seq):
    NQ = Lp // QB

    def body(h_ref, dh_ref, du_ref, g_ref, gx_ref, dmeta_ref, dg_ref):
        b = pl.program_id(0)
        j = pl.program_id(1)

        @pl.when((b == 0) & (j == 0))
        def _():
            dg_ref[...] = jnp.zeros_like(dg_ref)

        x = h_ref[...]
        r = lax.rsqrt(jnp.mean(x * x, axis=-1, keepdims=True) + EPS)
        xn = x * r
        du = du_ref[...]
        t = du * g_ref[...]
        dh0 = dh_ref[...] + r * (t - xn * jnp.mean(t * xn, axis=-1, keepdims=True))
        dg_ref[...] += jnp.sum(du * xn, axis=0, keepdims=True)
        gx_ref[0] = dh0

        @pl.when((j == 0) & (b == 0))
        def _():
            dmeta_ref[...] = dh0[FRONT:HEAD_ROWS, :]

        @pl.when((j == 0) & (b > 0))
        def _():
            dmeta_ref[...] += dh0[FRONT:HEAD_ROWS, :]

    rows = pl.BlockSpec((QB, D), lambda b, j: (b * NQ + j, 0))
    return pl.pallas_call(
        body,
        out_shape=(jax.ShapeDtypeStruct((B, seq, D), F32), jax.ShapeDtypeStruct((N_META, D), F32),
                   jax.ShapeDtypeStruct((1, D), F32)),
        grid=(B, NQ),
        in_specs=[rows, rows, rows, pl.BlockSpec((1, D), lambda b, j: (0, 0))],
        out_specs=(pl.BlockSpec((1, QB, D), lambda b, j: (b, jnp.maximum(j - 1, 0), 0)),
                   pl.BlockSpec((N_META, D), lambda b, j: (0, 0)), pl.BlockSpec((1, D), lambda b, j: (0, 0))),
        compiler_params=_params(("arbitrary", "arbitrary"), 5 * _nbytes((QB, D), F32)),
        name="in_norm_bwd",
    )(hp, dh1, du, g)


def _flat_tile(rows):
    return _div_tile(rows, 1296, 8)


def _add_half(gp, recv, c):
    H = gp.shape[2]
    th = _flat_tile(H)

    def body(c_ref, a_ref, b_ref, o_ref):
        o_ref[...] = a_ref[:, 0] + b_ref[...]

    return pl.pallas_call(
        body,
        out_shape=jax.ShapeDtypeStruct((4, H, LANES), F32),
        grid_spec=pltpu.PrefetchScalarGridSpec(
            num_scalar_prefetch=1,
            grid=(H // th,),
            in_specs=[pl.BlockSpec((4, 1, th, LANES), lambda i, c_ref: (0, c_ref[0], i, 0)),
                      pl.BlockSpec((4, th, LANES), lambda i, c_ref: (0, i, 0))],
            out_specs=pl.BlockSpec((4, th, LANES), lambda i, c_ref: (0, i, 0)),
        ),
        compiler_params=_params(("parallel",), 3 * _nbytes((4, th, LANES), F32)),
        name="grad_add_pair",
    )(c, gp, recv)


def _sum_chips(parts):
    H = parts.shape[1]
    th = _flat_tile(H)

    def body(p_ref, o_ref):
        o_ref[...] = ((p_ref[0] + p_ref[1]) + p_ref[2]) + p_ref[3]

    return pl.pallas_call(
        body,
        out_shape=jax.ShapeDtypeStruct((H, LANES), F32),
        grid=(H // th,),
        in_specs=[pl.BlockSpec((4, th, LANES), lambda i: (0, i, 0))],
        out_specs=pl.BlockSpec((th, LANES), lambda i: (i, 0)),
        compiler_params=_params(("parallel",), 5 * _nbytes((th, LANES), F32)),
        name="grad_sum_chips",
    )(parts)


def _adamw(w, g, m, v):
    R = w.shape[0]
    tr = _flat_tile(R)
    c1 = 1.0 - ADAM_B1 ** ADAM_STEP
    c2 = 1.0 - ADAM_B2 ** ADAM_STEP

    def body(w_ref, g_ref, m_ref, v_ref, d_ref, mo_ref, vo_ref):
        gv = g_ref[...]
        mn = ADAM_B1 * m_ref[...] + (1.0 - ADAM_B1) * gv
        vn = ADAM_B2 * v_ref[...] + (1.0 - ADAM_B2) * (gv * gv)
        mo_ref[...] = mn
        vo_ref[...] = vn
        d_ref[...] = -ADAM_LR * ((mn / c1) / (jnp.sqrt(vn / c2) + ADAM_EPS) + ADAM_WD * w_ref[...])

    spec = pl.BlockSpec((tr, LANES), lambda i: (i, 0))
    shp = jax.ShapeDtypeStruct((R, LANES), F32)
    return pl.pallas_call(
        body,
        out_shape=(shp, shp, shp),
        grid=(R // tr,),
        in_specs=[spec] * 4,
        out_specs=(spec, spec, spec),
        compiler_params=_params(("parallel",), 7 * _nbytes((tr, LANES), F32)),
        name="adamw",
    )(w, g, m, v)


def _mesh_pos():
    return lax.axis_index("x"), lax.axis_index("y"), lax.axis_index("c")


def _other_chips(x, y):
    return [(1 - x, y), (x, 1 - y), (1 - x, 1 - y)]


_ANY = pl.BlockSpec(memory_space=pl.ANY)


def _weight_gather(wsh):
    R = wsh.shape[0]
    H = R // 2

    def body(w_ref, out_ref, send_sems, recv_sems, local_sem):
        x, y, c = _mesh_pos()
        chips = _other_chips(x, y)
        mine = pl.ds(pl.multiple_of(c * H, 16), H)
        theirs = pl.ds(pl.multiple_of((1 - c) * H, 16), H)

        def copy(k, slot, half, to):
            ref = out_ref.at[slot, half]
            return pltpu.make_async_remote_copy(src_ref=ref, dst_ref=ref, send_sem=send_sems.at[k],
                                                recv_sem=recv_sems.at[k], device_id=to, device_id_type=MESH)

        own = pltpu.make_async_copy(w_ref, out_ref.at[2 * x + y], local_sem)
        own.start()
        first = []
        for k, (px, py) in enumerate(chips):
            cp = pltpu.make_async_remote_copy(src_ref=w_ref.at[mine], dst_ref=out_ref.at[2 * x + y, mine],
                                              send_sem=send_sems.at[k], recv_sem=recv_sems.at[k],
                                              device_id=(px, py, c), device_id_type=MESH)
            cp.start()
            first.append(cp)
        passed = []
        for k, (px, py) in enumerate(chips):
            copy(k, 2 * px + py, mine, (x, y, c)).wait_recv()
            cp = copy(3 + k, 2 * px + py, mine, (x, y, 1 - c))
            cp.start()
            passed.append(cp)
        for k, (px, py) in enumerate(chips):
            copy(3 + k, 2 * px + py, theirs, (x, y, c)).wait_recv()
        for cp in first + passed:
            cp.wait_send()
        own.wait()

    return pl.pallas_call(
        body,
        out_shape=jax.ShapeDtypeStruct((4, R, LANES), wsh.dtype),
        in_specs=[_ANY],
        out_specs=_ANY,
        scratch_shapes=[pltpu.SemaphoreType.DMA((6,)), pltpu.SemaphoreType.DMA((6,)), pltpu.SemaphoreType.DMA],
        name="weight_gather",
    )(wsh)


def _pair_swap(gp):
    H = gp.shape[2]

    def body(g_ref, out_ref, send_sem, recv_sem):
        x, y, c = _mesh_pos()
        cp = pltpu.make_async_remote_copy(src_ref=g_ref.at[:, 1 - c], dst_ref=out_ref, send_sem=send_sem,
                                          recv_sem=recv_sem, device_id=(x, y, 1 - c), device_id_type=MESH)
        cp.start()
        cp.wait_send()
        cp.wait_recv()

    return pl.pallas_call(
        body,
        out_shape=jax.ShapeDtypeStruct((4, H, LANES), gp.dtype),
        in_specs=[_ANY],
        out_specs=_ANY,
        scratch_shapes=[pltpu.SemaphoreType.DMA, pltpu.SemaphoreType.DMA],
        name="grad_pair_swap",
    )(gp)


def _chip_scatter(s1):
    H = s1.shape[1]

    def body(s_ref, out_ref, send_sems, recv_sems, local_sem):
        x, y, c = _mesh_pos()
        me = 2 * x + y
        own = pltpu.make_async_copy(s_ref.at[me], out_ref.at[me], local_sem)
        own.start()
        sends = []
        for k, (px, py) in enumerate(_other_chips(x, y)):
            cp = pltpu.make_async_remote_copy(src_ref=s_ref.at[2 * px + py], dst_ref=out_ref.at[me],
                                              send_sem=send_sems.at[k], recv_sem=recv_sems.at[k],
                                              device_id=(px, py, c), device_id_type=MESH)
            cp.start()
            sends.append(cp)
        for k, (px, py) in enumerate(_other_chips(x, y)):
            pltpu.make_async_remote_copy(src_ref=s_ref.at[me], dst_ref=out_ref.at[2 * px + py],
                                         send_sem=send_sems.at[k], recv_sem=recv_sems.at[k],
                                         device_id=(x, y, c), device_id_type=MESH).wait_recv()
        for cp in sends:
            cp.wait_send()
        own.wait()

    return pl.pallas_call(
        body,
        out_shape=jax.ShapeDtypeStruct((4, H, LANES), s1.dtype),
        in_specs=[_ANY],
        out_specs=_ANY,
        scratch_shapes=[pltpu.SemaphoreType.DMA((3,)), pltpu.SemaphoreType.DMA((3,)), pltpu.SemaphoreType.DMA],
        name="grad_chip_scatter",
    )(s1)


def _pair_join(f):
    H = f.shape[0]

    def body(f_ref, out_ref, send_sem, recv_sem, local_sem):
        x, y, c = _mesh_pos()
        own = pltpu.make_async_copy(f_ref, out_ref.at[c], local_sem)
        own.start()
        cp = pltpu.make_async_remote_copy(src_ref=f_ref, dst_ref=out_ref.at[c], send_sem=send_sem,
                                          recv_sem=recv_sem, device_id=(x, y, 1 - c), device_id_type=MESH)
        cp.start()
        pltpu.make_async_remote_copy(src_ref=f_ref, dst_ref=out_ref.at[1 - c], send_sem=send_sem,
                                     recv_sem=recv_sem, device_id=(x, y, c), device_id_type=MESH).wait_recv()
        cp.wait_send()
        own.wait()

    return pl.pallas_call(
        body,
        out_shape=jax.ShapeDtypeStruct((2, H, LANES), f.dtype),
        in_specs=[_ANY],
        out_specs=_ANY,
        scratch_shapes=[pltpu.SemaphoreType.DMA, pltpu.SemaphoreType.DMA, pltpu.SemaphoreType.DMA],
        name="grad_pair_join",
    )(f)


def _pack_rows(parts, dtype):
    flat = jnp.concatenate([p.reshape(-1).astype(dtype) for p in parts])
    flat = jnp.pad(flat, (0, FLAT_ROWS * LANES - flat.shape[0]))
    return flat.reshape(FLAT_ROWS, LANES)


def _rope_tables(Lp):
    inv = 1.0 / (ROPE_BASE ** (jnp.arange(0, ROPE, 2, dtype=F32) / ROPE))
    ang = (jnp.arange(Lp, dtype=F32) - FRONT)[:, None] * inv[None, :]
    cs, sn = jnp.cos(ang), jnp.sin(ang)
    return jnp.tile(cs, (1, 4)), jnp.concatenate([-sn, sn, -sn, sn], axis=1)


def _local_step(x, loss_target, meta, norm_g, w_in, gate_w, gate_b, gla_norm_g, gla_proj, q_norm_g, w_uq,
                kv_norm_g, w_ukv, mla_proj, w_out, final_norm_g):
    B, seq, _ = x.shape
    Lp = HEAD_ROWS + seq
    T = B * Lp
    tr = _div_tile(Lp, 544, 16)
    tq = _div_tile(Lp, 1088, 16)

    cuts = np.cumsum((0,) + SPLITS)
    col = lambda i: w_in[:, cuts[i]:cuts[i + 1]]
    w_q, w_k, w_v, w_lr, w_z, w_cq, w_ckv, w_kr, w_mz, w_gg, w_gm = [col(i) for i in range(11)]
    pad_cols = lambda w, n: jnp.pad(w, ((0, 0), (0, n - w.shape[1])))
    wA = jnp.concatenate([w_v, w_z, w_mz, w_gg, w_gm, w_q, w_k], axis=1)
    wB = jnp.concatenate([w_cq, w_ckv, pad_cols(w_lr, 128), pad_cols(w_kr, 128)], axis=1)
    wt_a = jnp.concatenate([w_gg, w_gm], axis=1).T
    wt_b = w_z.T
    wt_c = jnp.concatenate([w_v, w_q, w_k, pad_cols(w_lr, 256)], axis=1).T
    wt_d = w_mz.T
    wt_e = jnp.concatenate([w_cq, w_ckv, pad_cols(w_kr, 128)], axis=1).T
    wg = jnp.pad(gate_w, ((0, 128 - GLA_RANK), (0, 0)))
    wuq2 = jnp.pad(w_uq.reshape(Q_RANK, MLA_H, MLA_QK), ((0, 0), (0, 0), (0, 256 - MLA_QK))).reshape(Q_RANK, 2048)
    gn4 = jnp.tile(gla_norm_g, (1, GLA_H))
    cos_t, sin_t = _rope_tables(Lp)

    hp = jnp.concatenate([jnp.zeros((B, FRONT, D), F32), jnp.broadcast_to(meta[None], (B, N_META, D)), x], axis=1)
    hp = hp.reshape(T, D)

    u = _rms_in(hp, norm_g, tr)
    projA = _mm(u, wA, name="in_proj_a", tm=tq, tn=512, tk=D)
    projB = _mm(u, wB, name="in_proj_b", tm=tq, tn=640, tk=D)
    oa, ya_in, ssave = _gla_fwd(projA, projB, wg, gate_b, gn4, B, Lp)
    ya = _mm(ya_in, gla_proj, name="gla_proj", tm=tq, tn=512, tk=D)
    q_att, k_att, v_att, cqn, ckvn = _mla_prep(projB, cos_t, sin_t, q_norm_g, kv_norm_g, wuq2, w_ukv, B, Lp, tr)
    ob, yb_in, lse_r, lse_c = _attn_fwd(q_att, k_att, v_att, projA, B, Lp)
    yb = _mm(yb_in, mla_proj, name="mla_proj", tm=tq, tn=512, tk=D)
    merged = _merge_fwd(projA, ya, yb, tr)
    mo = _mm(merged, w_out, name="w_out", tm=tq, tn=512, tk=D)
    dh1, dh1_b, loss, d_gf = _final_loss(hp, mo, final_norm_g.reshape(1, D), loss_target, B, Lp)

    dmerged = _mm(dh1_b, w_out.T, name="d_merged", tm=tq, tn=512, tk=D)
    g_w_out = _mm(merged, dh1_b, name="dw_out", trans_a=True, tm=D, tn=512, tk=tq)
    dya, dyb, dA = _merge_bwd(dmerged, projA, ya, yb, tr)
    dya_in = _mm(dya, gla_proj.T, name="d_ya_in", tm=tq, tn=512, tk=D)
    g_gla_proj = _mm(ya_in, dya, name="dw_gla_proj", trans_a=True, tm=D, tn=512, tk=tq)
    dyb_in = _mm(dyb, mla_proj.T, name="d_yb_in", tm=tq, tn=512, tk=D)
    g_mla_proj = _mm(yb_in, dyb, name="dw_mla_proj", trans_a=True, tm=D, tn=512, tk=tq)
    doa, dBz, d_gn = _gla_out_bwd(dya_in, oa, projA, gn4, tr)
    dC, g_wg, d_bg = _gla_bwd(projA, projB, ssave, doa, wg, gate_b, B, Lp)
    do, dDz, delta_r, delta_c = _attn_bwd_pre(dyb_in, projA, ob, B, Lp)
    dq = _attn_bwd_dq(q_att, k_att, v_att, do, lse_r, delta_r, B, Lp)
    dk, dv = _attn_bwd_dkv(q_att, k_att, v_att, do, lse_c, delta_c, B, Lp)
    dqf, dkvf, dE, d_gq, d_gkv = _mla_bwd_post(dq, dk, dv, projB, cos_t, sin_t, q_norm_g, kv_norm_g,
                                                wuq2.T, w_ukv.T, B, Lp, tr)
    g_wuq2 = _mm(cqn, dqf, name="dw_uq", trans_a=True, tm=Q_RANK, tn=512, tk=tq)
    g_wukv = _mm(ckvn, dkvf, name="dw_ukv", trans_a=True, tm=KV_RANK, tn=512, tk=tq)
    dparts = [dA, dBz, dC, dDz, dE]
    g_in = [_mm(u, dp, name="dw_in_%d" % i, trans_a=True, tm=D, tn=_div_tile(dp.shape[1], 1024, 256), tk=tq)
            for i, dp in enumerate(dparts)]
    du = _du_matmul(dparts, [wt_a, wt_b, wt_c, wt_d, wt_e], _div_tile(Lp, 272, 16))
    grad_x, d_meta, d_ng = _in_norm_bwd(hp, dh1, du, norm_g, B, Lp, seq)

    gA, gBz, gC, gDz, gE = g_in
    g_w_in = jnp.concatenate([
        gC[:, 1024:1536], gC[:, 1536:2048], gC[:, 0:1024], gC[:, 2048:2048 + GLA_RANK], gBz,
        gE[:, 0:Q_RANK], gE[:, Q_RANK:Q_RANK + KV_RANK], gE[:, 384:384 + ROPE], gDz, gA[:, 0:D], gA[:, D:2 * D]], axis=1)
    g_wuq = g_wuq2.reshape(Q_RANK, MLA_H, 256)[:, :, :MLA_QK].reshape(Q_RANK, MLA_H * MLA_QK)
    grads = dict(w_in=g_w_in, gla_gate_w=g_wg[:GLA_RANK], gla_proj=g_gla_proj, mla_w_uq=g_wuq, mla_w_ukv=g_wukv,
                 mla_proj=g_mla_proj, w_out=g_w_out, meta_tokens=d_meta, norm_g=d_ng, gla_gate_b=d_bg,
                 gla_norm_g=d_gn, mla_q_norm_g=d_gq, mla_kv_norm_g=d_gkv, final_norm_g=d_gf)
    return loss[0, 0], grad_x, grads


_COL_SHARDED = ("w_in", "gla_gate_w", "mla_w_uq", "mla_w_ukv")
_ROW_SHARDED = ("gla_proj", "mla_proj", "w_out")
_BIG = tuple(n for n, _ in SEG_ROWS)
_SMALL = tuple(n for n, _ in SMALL)
_ORDER = ("meta_tokens", "norm_g", "w_in", "gla_gate_w", "gla_gate_b", "gla_norm_g", "gla_proj", "mla_q_norm_g",
          "mla_w_uq", "mla_kv_norm_g", "mla_w_ukv", "mla_proj", "w_out", "final_norm_g")


def _shard_of(name, full, j):
    if name in _ROW_SHARDED:
        n = full.shape[0] // 4
        return full[j * n:(j + 1) * n]
    n = full.shape[1] // 4
    return full[:, j * n:(j + 1) * n]


def kernel(x, meta_tokens, norm_g, w_in, gla_gate_w, gla_gate_b, gla_norm_g, gla_proj, mla_q_norm_g, mla_w_uq, mla_kv_norm_g, mla_w_ukv, mla_proj, w_out, final_norm_g, loss_target, m_meta_tokens, m_norm_g, m_w_in, m_gla_gate_w, m_gla_gate_b, m_gla_norm_g, m_gla_proj, m_mla_q_norm_g, m_mla_w_uq, m_mla_kv_norm_g, m_mla_w_ukv, m_mla_proj, m_w_out, m_final_norm_g, v_meta_tokens, v_norm_g, v_w_in, v_gla_gate_w, v_gla_gate_b, v_gla_norm_g, v_gla_proj, v_mla_q_norm_g, v_mla_w_uq, v_mla_kv_norm_g, v_mla_w_ukv, v_mla_proj, v_w_out, v_final_norm_g):
    w = dict(meta_tokens=meta_tokens, norm_g=norm_g, w_in=w_in[0], gla_gate_w=gla_gate_w[0], gla_gate_b=gla_gate_b,
             gla_norm_g=gla_norm_g, gla_proj=gla_proj[0], mla_q_norm_g=mla_q_norm_g, mla_w_uq=mla_w_uq[0],
             mla_kv_norm_g=mla_kv_norm_g, mla_w_ukv=mla_w_ukv[0], mla_proj=mla_proj[0], w_out=w_out[0],
             final_norm_g=final_norm_g)
    mom = dict(meta_tokens=m_meta_tokens, norm_g=m_norm_g, w_in=m_w_in[0], gla_gate_w=m_gla_gate_w[0],
               gla_gate_b=m_gla_gate_b, gla_norm_g=m_gla_norm_g, gla_proj=m_gla_proj[0], mla_q_norm_g=m_mla_q_norm_g,
               mla_w_uq=m_mla_w_uq[0], mla_kv_norm_g=m_mla_kv_norm_g, mla_w_ukv=m_mla_w_ukv[0], mla_proj=m_mla_proj[0],
               w_out=m_w_out[0], final_norm_g=m_final_norm_g)
    var = dict(meta_tokens=v_meta_tokens, norm_g=v_norm_g, w_in=v_w_in[0], gla_gate_w=v_gla_gate_w[0],
               gla_gate_b=v_gla_gate_b, gla_norm_g=v_gla_norm_g, gla_proj=v_gla_proj[0], mla_q_norm_g=v_mla_q_norm_g,
               mla_w_uq=v_mla_w_uq[0], mla_kv_norm_g=v_mla_kv_norm_g, mla_w_ukv=v_mla_w_ukv[0], mla_proj=v_mla_proj[0],
               w_out=v_w_out[0], final_norm_g=v_final_norm_g)
    out_shapes = {n: a.shape for n, a in zip(_ORDER, (meta_tokens, norm_g, w_in, gla_gate_w, gla_gate_b, gla_norm_g,
                                                     gla_proj, mla_q_norm_g, mla_w_uq, mla_kv_norm_g, mla_w_ukv,
                                                     mla_proj, w_out, final_norm_g))}

    meta_bits = lax.bitcast_convert_type(meta_tokens, BF16)
    wsh = _pack_rows([w[n] for n in _BIG] + [meta_bits], BF16)
    wg_all = _weight_gather(wsh)
    full = {}
    off = 0
    for name, rows in SEG_ROWS:
        shard_shape = w[name].shape
        parts = [wg_all[j, off:off + rows].reshape(shard_shape) for j in range(4)]
        full[name] = jnp.concatenate(parts, axis=0 if name in _ROW_SHARDED else 1)
        off += rows
    meta_parts = [lax.bitcast_convert_type(wg_all[j, off:off + 2 * META_ROWS_F32].reshape(N_META, D // 4, 2), F32)
                  for j in range(4)]
    meta_full = jnp.concatenate(meta_parts, axis=1)

    loss_local, grad_x, g = _local_step(
        x, loss_target, meta_full, norm_g, full["w_in"], full["gla_gate_w"], gla_gate_b, gla_norm_g, full["gla_proj"],
        mla_q_norm_g, full["mla_w_uq"], mla_kv_norm_g, full["mla_w_ukv"], full["mla_proj"], full["w_out"], final_norm_g)
    loss = lax.psum(loss_local, ("x", "y", "c"))

    small = [g[n] for n in _SMALL]
    gp = jnp.stack([_pack_rows([_shard_of(n, g[n], j) for n in _BIG] + [_shard_of("meta_tokens", g["meta_tokens"], j)]
                               + small, F32) for j in range(4)])
    gp = gp.reshape(4, 2, HALF_ROWS, LANES)
    c_idx = lax.axis_index("c").astype(jnp.int32).reshape(1)
    s1 = _add_half(gp, _pair_swap(gp), c_idx)
    f_half = _sum_chips(_chip_scatter(s1))
    g_flat = _pair_join(f_half).reshape(FLAT_ROWS, LANES)

    names = _BIG + ("meta_tokens",) + _SMALL
    w_flat = _pack_rows([w[n] for n in names], F32)
    m_flat = _pack_rows([mom[n] for n in names], F32)
    v_flat = _pack_rows([var[n] for n in names], F32)
    d_flat, mn_flat, vn_flat = _adamw(w_flat, g_flat, m_flat, v_flat)

    def unpack(flat):
        res = {}
        o = 0
        flat1 = flat.reshape(-1)
        for n in names:
            size = int(np.prod(w[n].shape))
            res[n] = flat1[o:o + size].reshape(out_shapes[n])
            o += size
        return res

    outs = [unpack(f) for f in (g_flat, d_flat, mn_flat, vn_flat)]
    return (loss, grad_x, *[o[n] for o in outs for n in _ORDER])
```

```python
import functools
import math

import jax
import jax.numpy as jnp
import numpy as np
from jax import lax
from jax.experimental import pallas as pl
from jax.experimental.pallas import tpu as pltpu

F32 = jnp.float32
BF16 = jnp.bfloat16

D = 1024
N_META = 16
QB = 256
FRONT = QB - N_META
HEAD_ROWS = FRONT + N_META
assert FRONT % 64 == 48
EPS = 1e-6

GLA_H, GLA_DK, GLA_DV, GLA_RANK, GLA_C = 4, 128, 256, 16, 64
GLA_NORMALIZER = 16.0
GLA_KW, GLA_VW = GLA_H * GLA_DK, GLA_H * GLA_DV
MLA_H, NOPE, ROPE, MLA_DV, Q_RANK, KV_RANK = 8, 128, 64, 128, 256, 128
MLA_QK = NOPE + ROPE
ROPE_BASE = 10000.0
SPLITS = (GLA_KW, GLA_KW, GLA_VW, GLA_RANK, GLA_VW, Q_RANK, KV_RANK, ROPE, MLA_H * MLA_DV, D, D)
IN_WIDTH = sum(SPLITS)

ADAM_LR, ADAM_B1, ADAM_B2, ADAM_EPS, ADAM_WD, ADAM_STEP = 0.001, 0.9, 0.999, 1e-08, 0.01, 10

LANES = 128
VMEM_CAP_V7X = 56 * 1024 * 1024
MESH = pl.DeviceIdType.MESH
NEG = -1e30

SEG_ROWS = (
    ("w_in", D * (IN_WIDTH // 4) // LANES),
    ("gla_gate_w", GLA_RANK * (GLA_KW // 4) // LANES),
    ("gla_proj", (GLA_VW // 4) * D // LANES),
    ("mla_w_uq", Q_RANK * (MLA_H * MLA_QK // 4) // LANES),
    ("mla_w_ukv", KV_RANK * (MLA_H * (NOPE + MLA_DV) // 4) // LANES),
    ("mla_proj", (MLA_H * MLA_DV // 4) * D // LANES),
    ("w_out", (D // 4) * D // LANES),
)
BIG_ROWS = sum(r for _, r in SEG_ROWS)
META_ROWS_F32 = N_META * (D // 4) // LANES
SMALL = (("norm_g", D), ("gla_gate_b", GLA_KW), ("gla_norm_g", GLA_DV), ("mla_q_norm_g", Q_RANK),
         ("mla_kv_norm_g", KV_RANK), ("final_norm_g", D))
SMALL_ROWS = sum(n for _, n in SMALL) // LANES
FLAT_ROWS = 20736
HALF_ROWS = FLAT_ROWS // 2


def _div_tile(n, target, mult):
    best = None
    for d in range(mult, min(n, target) + 1, mult):
        if n % d == 0:
            best = d
    assert best is not None, (n, target, mult)
    return best


def _params(sem, block_bytes, scratch_bytes=0):
    est = 2 * block_bytes + scratch_bytes + 12 * 1024 * 1024
    return pltpu.CompilerParams(dimension_semantics=sem, vmem_limit_bytes=int(min(max(est, 24 * 1024 * 1024), VMEM_CAP_V7X)))


def _nbytes(shape, dtype):
    return int(np.prod(shape)) * jnp.dtype(dtype).itemsize


def _sigmoid(x):
    return 1.0 / (1.0 + jnp.exp(-x))


def _nt(a, b):
    return lax.dot_general(a, b, (((1,), (1,)), ((), ())), preferred_element_type=F32)


def _tn(a, b):
    return lax.dot_general(a, b, (((0,), (0,)), ((), ())), preferred_element_type=F32)


def _nn(a, b):
    return jnp.dot(a, b, preferred_element_type=F32)


def _split3(x):
    a = x.astype(BF16)
    r = x - a.astype(F32)
    b = r.astype(BF16)
    c = (r - b.astype(F32)).astype(BF16)
    return a, b, c


def _mm(a, b, *, name, trans_a=False, out_dtype=F32, tm, tn, tk):
    if trans_a:
        K, M = a.shape
    else:
        M, K = a.shape
    assert b.shape[0] == K
    N = b.shape[1]
    assert M % tm == 0 and N % tn == 0 and K % tk == 0, (name, M, N, K, tm, tn, tk)
    nk = K // tk

    def body(a_ref, b_ref, o_ref, *scratch):
        av = a_ref[...].astype(BF16)
        bv = b_ref[...].astype(BF16)
        prod = _tn(av, bv) if trans_a else _nn(av, bv)
        if nk == 1:
            o_ref[...] = prod.astype(out_dtype)
        else:
            acc = scratch[0]
            k = pl.program_id(2)

            @pl.when(k == 0)
            def _():
                acc[...] = prod

            @pl.when(k > 0)
            def _():
                acc[...] += prod

            @pl.when(k == nk - 1)
            def _():
                o_ref[...] = acc[...].astype(out_dtype)

    if trans_a:
        a_spec = pl.BlockSpec((tk, tm), lambda i, j, k: (k, i))
    else:
        a_spec = pl.BlockSpec((tm, tk), lambda i, j, k: (i, k))
    blocks = (_nbytes((tm, tk), a.dtype) + _nbytes((tk, tn), b.dtype) + _nbytes((tm, tn), out_dtype))
    scratch = [pltpu.VMEM((tm, tn), F32)] if nk > 1 else []
    return pl.pallas_call(
        body,
        out_shape=jax.ShapeDtypeStruct((M, N), out_dtype),
        grid=(M // tm, N // tn, nk),
        in_specs=[a_spec, pl.BlockSpec((tk, tn), lambda i, j, k: (k, j))],
        out_specs=pl.BlockSpec((tm, tn), lambda i, j, k: (i, j)),
        scratch_shapes=scratch,
        compiler_params=_params(("parallel", "parallel", "arbitrary"), blocks + _nbytes((tm, tn), F32),
                                _nbytes((tm, tn), F32) if nk > 1 else 0),
        name=name,
    )(a, b)


def _rms_in(hp, g, tr):
    T = hp.shape[0]

    def body(h_ref, g_ref, u_ref):
        h = h_ref[...]
        r = lax.rsqrt(jnp.mean(h * h, axis=-1, keepdims=True) + EPS)
        u_ref[...] = (h * r * g_ref[...]).astype(BF16)

    return pl.pallas_call(
        body,
        out_shape=jax.ShapeDtypeStruct((T, D), BF16),
        grid=(T // tr,),
        in_specs=[pl.BlockSpec((tr, D), lambda i: (i, 0)), pl.BlockSpec((1, D), lambda i: (0, 0))],
        out_specs=pl.BlockSpec((tr, D), lambda i: (i, 0)),
        compiler_params=_params(("parallel",), _nbytes((tr, D), F32) * 2),
        name="rms_in",
    )(hp, g)


def _gla_gate(lr, wg, bg, valid):
    pre = _nn(lr.astype(BF16), wg) + bg
    logsig = jnp.minimum(pre, 0.0) - jnp.log(1.0 + jnp.exp(-jnp.abs(pre)))
    return pre, jnp.where(valid, logsig / GLA_NORMALIZER, 0.0)


def _tri_masks():
    ri = lax.broadcasted_iota(jnp.int32, (GLA_C, GLA_C), 0)
    ci = lax.broadcasted_iota(jnp.int32, (GLA_C, GLA_C), 1)
    return ci <= ri, ci >= ri


def _cumsum_rows(x, ones_mask):
    w = jnp.where(ones_mask, 1.0, 0.0).astype(BF16)
    a, b, c = _split3(x)
    return _nn(w, a) + _nn(w, b) + _nn(w, c)


def _gla_fwd(projA, projB, wg, bg, gn4, B, Lp):
    T = B * Lp
    NC = Lp // GLA_C
    C = GLA_C
    scale = GLA_DK ** -0.5

    def body(q_ref, k_ref, v_ref, lr_ref, z_ref, wg_ref, bg_ref, gn_ref, oa_ref, ya_ref, ssave_ref, st_ref):
        n = pl.program_id(1)

        @pl.when(n == 0)
        def _():
            st_ref[...] = jnp.zeros_like(st_ref)

        ssave_ref[0, 0] = st_ref[...]
        pos = n * C + lax.broadcasted_iota(jnp.int32, (C, 1), 0)
        _, glog = _gla_gate(lr_ref[...], wg_ref[...], bg_ref[...], pos >= FRONT)
        lower, _ = _tri_masks()
        bcum = _cumsum_rows(glog, lower)
        is_last = lax.broadcasted_iota(jnp.int32, (C, 1), 0) == C - 1
        for h in range(GLA_H):
            ks = slice(h * GLA_DK, (h + 1) * GLA_DK)
            vs = slice(h * GLA_DV, (h + 1) * GLA_DV)
            bh = bcum[:, ks]
            blast = jnp.sum(jnp.where(is_last, bh, 0.0), axis=0, keepdims=True)
            qh = q_ref[:, ks] * scale
            kh = k_ref[:, ks]
            qe = (qh * jnp.exp(bh)).astype(BF16)
            ke = (kh * jnp.exp(-bh)).astype(BF16)
            kl = (kh * jnp.exp(blast - bh)).astype(BF16)
            vh = v_ref[:, vs].astype(BF16)
            a = jnp.where(lower, _nt(qe, ke), 0.0).astype(BF16)
            st = st_ref[h]
            o = _nn(a, vh) + _nt(qe, st.astype(BF16))
            st_ref[h] = st * jnp.exp(blast) + _tn(vh, kl)
            oa_ref[:, vs] = o
            on = o * lax.rsqrt(jnp.mean(o * o, axis=-1, keepdims=True) + EPS) * gn_ref[:, vs]
            z = z_ref[:, vs]
            ya_ref[:, vs] = (on * (z * _sigmoid(z))).astype(BF16)

    row = lambda b, n: b * NC + n
    blocks = (_nbytes((C, 512), F32) * 2 + _nbytes((C, 1024), F32) * 3 + _nbytes((C, 1024), BF16)
              + _nbytes((GLA_H, GLA_DV, GLA_DK), F32) + _nbytes((128, 512), BF16))
    return pl.pallas_call(
        body,
        out_shape=(jax.ShapeDtypeStruct((T, GLA_VW), F32), jax.ShapeDtypeStruct((T, GLA_VW), BF16),
                   jax.ShapeDtypeStruct((B, NC, GLA_H, GLA_DV, GLA_DK), F32)),
        grid=(B, NC),
        in_specs=[
            pl.BlockSpec((C, 512), lambda b, n: (row(b, n), 10)),
            pl.BlockSpec((C, 512), lambda b, n: (row(b, n), 11)),
            pl.BlockSpec((C, 1024), lambda b, n: (row(b, n), 0)),
            pl.BlockSpec((C, 128), lambda b, n: (row(b, n), 3)),
            pl.BlockSpec((C, 1024), lambda b, n: (row(b, n), 1)),
            pl.BlockSpec((128, 512), lambda b, n: (0, 0)),
            pl.BlockSpec((1, 512), lambda b, n: (0, 0)),
            pl.BlockSpec((1, 1024), lambda b, n: (0, 0)),
        ],
        out_specs=(pl.BlockSpec((C, 1024), lambda b, n: (row(b, n), 0)),
                   pl.BlockSpec((C, 1024), lambda b, n: (row(b, n), 0)),
                   pl.BlockSpec((1, 1, GLA_H, GLA_DV, GLA_DK), lambda b, n: (b, n, 0, 0, 0))),
        scratch_shapes=[pltpu.VMEM((GLA_H, GLA_DV, GLA_DK), F32)],
        compiler_params=_params(("parallel", "arbitrary"), blocks, _nbytes((GLA_H, GLA_DV, GLA_DK), F32)),
        name="gla_fwd",
    )(projA, projA, projA, projB, projA, wg, bg, gn4)


def _swap_halves(x):
    lane = lax.broadcasted_iota(jnp.int32, x.shape, 1)
    return jnp.where((lane % 64) < 32, pltpu.roll(x, 96, 1), pltpu.roll(x, 32, 1))


def _mla_prep(projB, cos_t, sin_t, gq, gkv, wuq2, wukv, B, Lp, tr):
    T = B * Lp
    nt = Lp // tr
    HW = 2 * LANES

    def body(pb_ref, cos_ref, sin_ref, gq_ref, gkv_ref, wuq_ref, wukv_ref, q_ref, k_ref, v_ref, cqn_ref, ckvn_ref):
        cq = pb_ref[:, 0:Q_RANK]
        ckv = pb_ref[:, Q_RANK:Q_RANK + KV_RANK]
        kr = pb_ref[:, 512:640]
        cqn = (cq * lax.rsqrt(jnp.mean(cq * cq, axis=-1, keepdims=True) + EPS) * gq_ref[...]).astype(BF16)
        ckvn = (ckv * lax.rsqrt(jnp.mean(ckv * ckv, axis=-1, keepdims=True) + EPS) * gkv_ref[...]).astype(BF16)
        cqn_ref[...] = cqn
        ckvn_ref[...] = ckvn
        qf = _nn(cqn, wuq_ref[...])
        kvf = _nn(ckvn, wukv_ref[...])
        cs = cos_ref[...]
        sn = sin_ref[...]
        rope = lambda t: t * cs + _swap_halves(t) * sn
        kr_r = rope(kr).astype(BF16)
        for h in range(MLA_H):
            q_ref[:, h * HW:h * HW + LANES] = qf[:, h * HW:h * HW + LANES].astype(BF16)
            q_ref[:, h * HW + LANES:(h + 1) * HW] = rope(qf[:, h * HW + LANES:(h + 1) * HW]).astype(BF16)
            k_ref[:, h * HW:h * HW + LANES] = kvf[:, h * HW:h * HW + LANES].astype(BF16)
            k_ref[:, h * HW + LANES:(h + 1) * HW] = kr_r
            v_ref[:, h * MLA_DV:(h + 1) * MLA_DV] = kvf[:, h * HW + LANES:(h + 1) * HW].astype(BF16)

    blocks = (_nbytes((tr, 640), F32) + 2 * _nbytes((tr, 128), F32) + _nbytes((Q_RANK, 2048), BF16)
              + _nbytes((KV_RANK, 2048), BF16) + _nbytes((tr, 2048 * 2 + 1024 + 384), BF16)
              + 2 * _nbytes((tr, 2048), F32))
    return pl.pallas_call(
        body,
        out_shape=(jax.ShapeDtypeStruct((T, MLA_H * HW), BF16), jax.ShapeDtypeStruct((T, MLA_H * HW), BF16),
                   jax.ShapeDtypeStruct((T, MLA_H * MLA_DV), BF16), jax.ShapeDtypeStruct((T, Q_RANK), BF16),
                   jax.ShapeDtypeStruct((T, KV_RANK), BF16)),
        grid=(B, nt),
        in_specs=[
            pl.BlockSpec((tr, 640), lambda b, j: (b * nt + j, 0)),
            pl.BlockSpec((tr, 128), lambda b, j: (j, 0)),
            pl.BlockSpec((tr, 128), lambda b, j: (j, 0)),
            pl.BlockSpec((1, Q_RANK), lambda b, j: (0, 0)),
            pl.BlockSpec((1, KV_RANK), lambda b, j: (0, 0)),
            pl.BlockSpec((Q_RANK, 2048), lambda b, j: (0, 0)),
            pl.BlockSpec((KV_RANK, 2048), lambda b, j: (0, 0)),
        ],
        out_specs=(pl.BlockSpec((tr, 2048), lambda b, j: (b * nt + j, 0)),
                   pl.BlockSpec((tr, 2048), lambda b, j: (b * nt + j, 0)),
                   pl.BlockSpec((tr, 1024), lambda b, j: (b * nt + j, 0)),
                   pl.BlockSpec((tr, Q_RANK), lambda b, j: (b * nt + j, 0)),
                   pl.BlockSpec((tr, KV_RANK), lambda b, j: (b * nt + j, 0))),
        compiler_params=_params(("parallel", "parallel"), blocks),
        name="mla_prep",
    )(projB, cos_t, sin_t, gq, gkv, wuq2, wukv)


def _attn_mask(row, col):
    return (col <= row) & ((col >= FRONT) | (row < FRONT))


def _attn_fwd(q_att, k_att, v_att, projA, B, Lp):
    T = B * Lp
    NQ = Lp // QB
    HW = 2 * LANES
    scale = 1.0 / math.sqrt(MLA_QK)

    def body(q_ref, k_ref, v_ref, mz_ref, o_ref, yb_ref, lser_ref, lsec_ref, m_ref, l_ref, acc_ref):
        qi = pl.program_id(1)
        m_ref[...] = jnp.full(m_ref.shape, NEG, F32)
        l_ref[...] = jnp.zeros_like(l_ref)
        acc_ref[...] = jnp.zeros_like(acc_ref)
        row = qi * QB + lax.broadcasted_iota(jnp.int32, (QB, QB), 0)
        coli = lax.broadcasted_iota(jnp.int32, (QB, QB), 1)

        def step(kj, carry):
            off = pl.multiple_of(kj * QB, QB)
            ok = _attn_mask(row, kj * QB + coli)
            for h in range(MLA_H):
                q = q_ref[:, h * HW:(h + 1) * HW]
                kb = k_ref[pl.ds(off, QB), h * HW:(h + 1) * HW]
                vb = v_ref[pl.ds(off, QB), h * MLA_DV:(h + 1) * MLA_DV]
                s = jnp.where(ok, _nt(q, kb) * scale, NEG)
                m_old = m_ref[h]
                m_new = jnp.maximum(m_old, jnp.max(s, axis=-1, keepdims=True))
                alpha = jnp.exp(m_old - m_new)
                p = jnp.exp(s - jnp.tile(m_new, (1, QB // LANES)))
                m_ref[h] = m_new
                l_ref[h] = alpha * l_ref[h] + jnp.sum(p, axis=-1, keepdims=True)
                acc_ref[h] = alpha * acc_ref[h] + _nn(p.astype(BF16), vb)
            return carry

        lax.fori_loop(0, qi + 1, step, 0)
        for h in range(MLA_H):
            hs = slice(h * MLA_DV, (h + 1) * MLA_DV)
            l = l_ref[h]
            o = acc_ref[h] / l
            o_ref[:, hs] = o
            z = mz_ref[:, hs]
            yb_ref[:, hs] = (o * (z * _sigmoid(z))).astype(BF16)
            lse = m_ref[h] + jnp.log(l)
            lser_ref[0, h] = lse
            lsec_ref[0, h, pl.ds(qi, 1), :] = jnp.transpose(lse)[0:1, :]

    blocks = (_nbytes((QB, 2048), BF16) + _nbytes((Lp, 2048), BF16) + _nbytes((Lp, 1024), BF16)
              + 2 * _nbytes((QB, 1024), F32) + _nbytes((QB, 1024), BF16) + _nbytes((MLA_H, QB, LANES), F32)
              + _nbytes((MLA_H, NQ, QB), F32))
    return pl.pallas_call(
        body,
        out_shape=(jax.ShapeDtypeStruct((T, MLA_H * MLA_DV), F32), jax.ShapeDtypeStruct((T, MLA_H * MLA_DV), BF16),
                   jax.ShapeDtypeStruct((B, MLA_H, Lp, LANES), F32), jax.ShapeDtypeStruct((B, MLA_H, NQ, QB), F32)),
        grid=(B, NQ),
        in_specs=[
            pl.BlockSpec((QB, MLA_H * HW), lambda b, i: (b * NQ + i, 0)),
            pl.BlockSpec((Lp, MLA_H * HW), lambda b, i: (b, 0)),
            pl.BlockSpec((Lp, MLA_H * MLA_DV), lambda b, i: (b, 0)),
            pl.BlockSpec((QB, 1024), lambda b, i: (b * NQ + i, 2)),
        ],
        out_specs=(pl.BlockSpec((QB, 1024), lambda b, i: (b * NQ + i, 0)),
                   pl.BlockSpec((QB, 1024), lambda b, i: (b * NQ + i, 0)),
                   pl.BlockSpec((1, MLA_H, QB, LANES), lambda b, i: (b, 0, i, 0)),
                   pl.BlockSpec((1, MLA_H, NQ, QB), lambda b, i: (b, 0, 0, 0))),
        scratch_shapes=[pltpu.VMEM((MLA_H, QB, LANES), F32), pltpu.VMEM((MLA_H, QB, LANES), F32),
                        pltpu.VMEM((MLA_H, QB, MLA_DV), F32)],
        compiler_params=_params(("parallel", "arbitrary"), blocks, 3 * _nbytes((MLA_H, QB, LANES), F32)),
        name="attn_fwd",
    )(q_att, k_att, v_att, projA)


def _merge_fwd(projA, ya, yb, tr):
    T = ya.shape[0]

    def body(gg_ref, gm_ref, ya_ref, yb_ref, o_ref):
        o_ref[...] = (_sigmoid(gg_ref[...]) * ya_ref[...] + _sigmoid(gm_ref[...]) * yb_ref[...]).astype(BF16)

    spec = lambda c: pl.BlockSpec((tr, D), lambda i: (i, c))
    return pl.pallas_call(
        body,
        out_shape=jax.ShapeDtypeStruct((T, D), BF16),
        grid=(T // tr,),
        in_specs=[spec(3), spec(4), spec(0), spec(0)],
        out_specs=spec(0),
        compiler_params=_params(("parallel",), 5 * _nbytes((tr, D), F32)),
        name="merge_fwd",
    )(projA, projA, ya, yb)


def _final_loss(hp, mo, gf, tgt, B, Lp):
    T = B * Lp
    NQ = Lp // QB

    def body(h_ref, mo_ref, gf_ref, t_ref, dh_ref, dhb_ref, loss_ref, dgf_ref):
        b = pl.program_id(0)
        j = pl.program_id(1)

        @pl.when((b == 0) & (j == 0))
        def _():
            loss_ref[...] = jnp.zeros_like(loss_ref)
            dgf_ref[...] = jnp.zeros_like(dgf_ref)

        h1 = h_ref[...] + mo_ref[...]
        r = lax.rsqrt(jnp.mean(h1 * h1, axis=-1, keepdims=True) + EPS)
        hn = h1 * r
        gfv = gf_ref[...]
        diff = jnp.where(j > 0, hn * gfv - t_ref[0], 0.0)
        loss_ref[...] += (0.5 / D) * jnp.sum(jnp.sum(diff * diff, axis=-1, keepdims=True), axis=0, keepdims=True)
        dout = diff * (1.0 / D)
        dgf_ref[...] += jnp.sum(dout * hn, axis=0, keepdims=True)
        dhn = dout * gfv
        dh = r * (dhn - hn * jnp.mean(dhn * hn, axis=-1, keepdims=True))
        dh_ref[...] = dh
        dhb_ref[...] = dh.astype(BF16)

    rows = pl.BlockSpec((QB, D), lambda b, j: (b * NQ + j, 0))
    return pl.pallas_call(
        body,
        out_shape=(jax.ShapeDtypeStruct((T, D), F32), jax.ShapeDtypeStruct((T, D), BF16),
                   jax.ShapeDtypeStruct((1, 1), F32), jax.ShapeDtypeStruct((1, D), F32)),
        grid=(B, NQ),
        in_specs=[rows, rows, pl.BlockSpec((1, D), lambda b, j: (0, 0)),
                  pl.BlockSpec((1, QB, D), lambda b, j: (b, jnp.maximum(j - 1, 0), 0))],
        out_specs=(rows, rows, pl.BlockSpec((1, 1), lambda b, j: (0, 0)), pl.BlockSpec((1, D), lambda b, j: (0, 0))),
        compiler_params=_params(("arbitrary", "arbitrary"), 5 * _nbytes((QB, D), F32)),
        name="final_loss",
    )(hp, mo, gf, tgt)


def _merge_bwd(dm, projA, ya, yb, tr):
    T = dm.shape[0]

    def body(dm_ref, gg_ref, gm_ref, ya_ref, yb_ref, dya_ref, dyb_ref, da_ref):
        d = dm_ref[...]
        sg = _sigmoid(gg_ref[...])
        sm = _sigmoid(gm_ref[...])
        dya_ref[...] = (d * sg).astype(BF16)
        dyb_ref[...] = (d * sm).astype(BF16)
        da_ref[:, 0:D] = (d * ya_ref[...] * (sg * (1.0 - sg))).astype(BF16)
        da_ref[:, D:2 * D] = (d * yb_ref[...] * (sm * (1.0 - sm))).astype(BF16)

    spec = lambda c: pl.BlockSpec((tr, D), lambda i: (i, c))
    return pl.pallas_call(
        body,
        out_shape=(jax.ShapeDtypeStruct((T, D), BF16), jax.ShapeDtypeStruct((T, D), BF16),
                   jax.ShapeDtypeStruct((T, 2 * D), BF16)),
        grid=(T // tr,),
        in_specs=[spec(0), spec(3), spec(4), spec(0), spec(0)],
        out_specs=(spec(0), spec(0), pl.BlockSpec((tr, 2 * D), lambda i: (i, 0))),
        compiler_params=_params(("parallel",), 8 * _nbytes((tr, D), F32)),
        name="merge_bwd",
    )(dm, projA, projA, ya, yb)


def _gla_out_bwd(dyin, oa, projA, gn4, tr):
    T = dyin.shape[0]
    nsteps = T // tr

    def body(dy_ref, oa_ref, z_ref, gn_ref, do_ref, dz_ref, dgn_ref, acc_ref):
        i = pl.program_id(0)

        @pl.when(i == 0)
        def _():
            acc_ref[...] = jnp.zeros_like(acc_ref)

        for h in range(GLA_H):
            vs = slice(h * GLA_DV, (h + 1) * GLA_DV)
            dy = dy_ref[:, vs]
            o = oa_ref[:, vs]
            z = z_ref[:, vs]
            gn = gn_ref[:, vs]
            s = _sigmoid(z)
            ra = lax.rsqrt(jnp.mean(o * o, axis=-1, keepdims=True) + EPS)
            on = o * ra
            don = dy * (z * s)
            t = don * gn
            do_ref[:, vs] = (ra * (t - on * jnp.mean(t * on, axis=-1, keepdims=True))).astype(BF16)
            dz_ref[:, vs] = (dy * (on * gn) * (s * (1.0 + z * (1.0 - s)))).astype(BF16)
            acc_ref[:, vs] += jnp.sum(don * on, axis=0, keepdims=True)

        @pl.when(i == nsteps - 1)
        def _():
            a = acc_ref[...]
            dgn_ref[...] = a[:, 0:256] + a[:, 256:512] + a[:, 512:768] + a[:, 768:1024]

    spec = lambda c: pl.BlockSpec((tr, D), lambda i: (i, c))
    return pl.pallas_call(
        body,
        out_shape=(jax.ShapeDtypeStruct((T, D), BF16), jax.ShapeDtypeStruct((T, D), BF16),
                   jax.ShapeDtypeStruct((1, GLA_DV), F32)),
        grid=(nsteps,),
        in_specs=[spec(0), spec(0), spec(1), pl.BlockSpec((1, D), lambda i: (0, 0))],
        out_specs=(spec(0), spec(0), pl.BlockSpec((1, GLA_DV), lambda i: (0, 0))),
        scratch_shapes=[pltpu.VMEM((1, D), F32)],
        compiler_params=_params(("arbitrary",), 6 * _nbytes((tr, D), F32)),
        name="gla_out_bwd",
    )(dyin, oa, projA, gn4)


def _gla_bwd(projA, projB, ssave, doa, wg, bg, B, Lp):
    T = B * Lp
    NC = Lp // GLA_C
    C = GLA_C
    scale = GLA_DK ** -0.5
    WC = 2304

    def body(q_ref, k_ref, v_ref, lr_ref, ss_ref, do_ref, wg_ref, bg_ref, dc_ref, dwg_ref, dbg_ref, dst_ref):
        b = pl.program_id(0)
        i = pl.program_id(1)
        n = NC - 1 - i

        @pl.when(i == 0)
        def _():
            dst_ref[...] = jnp.zeros_like(dst_ref)

        @pl.when((b == 0) & (i == 0))
        def _():
            dwg_ref[...] = jnp.zeros_like(dwg_ref)
            dbg_ref[...] = jnp.zeros_like(dbg_ref)

        pos = n * C + lax.broadcasted_iota(jnp.int32, (C, 1), 0)
        valid = pos >= FRONT
        lr = lr_ref[...]
        pre, glog = _gla_gate(lr, wg_ref[...], bg_ref[...], valid)
        lower, upper = _tri_masks()
        bcum = _cumsum_rows(glog, lower)
        is_last = lax.broadcasted_iota(jnp.int32, (C, 1), 0) == C - 1
        db_parts = []
        for h in range(GLA_H):
            ks = slice(h * GLA_DK, (h + 1) * GLA_DK)
            vs = slice(h * GLA_DV, (h + 1) * GLA_DV)
            bh = bcum[:, ks]
            blast = jnp.sum(jnp.where(is_last, bh, 0.0), axis=0, keepdims=True)
            eb, enb, ekl, ebl = jnp.exp(bh), jnp.exp(-bh), jnp.exp(blast - bh), jnp.exp(blast)
            qh = q_ref[:, ks] * scale
            kh = k_ref[:, ks]
            qe_f, ke_f, kl_f = qh * eb, kh * enb, kh * ekl
            qe, ke, kl = qe_f.astype(BF16), ke_f.astype(BF16), kl_f.astype(BF16)
            vh = v_ref[:, vs].astype(BF16)
            doh = do_ref[:, vs]
            st = ss_ref[0, 0, h]
            dst = dst_ref[h]
            st_b, dst_b = st.astype(BF16), dst.astype(BF16)
            da = jnp.where(lower, _nt(doh, vh), 0.0).astype(BF16)
            da_t = jnp.where(upper, _nt(vh, doh), 0.0).astype(BF16)
            a_t = jnp.where(upper, _nt(ke, qe), 0.0).astype(BF16)
            dqe = _nn(da, ke) + _nn(doh, st_b)
            dke = _nn(da_t, qe)
            dvh = _nn(a_t, doh) + _nt(kl, dst_b)
            dkl = _nn(vh, dst_b)
            dst_ref[h] = dst * ebl + _tn(doh, qe)
            deb = jnp.sum(st * dst, axis=0, keepdims=True)
            db = dqe * qe_f - dke * ke_f - dkl * kl_f
            db_last = jnp.sum(dkl * kl_f, axis=0, keepdims=True) + deb * ebl
            db_parts.append(db + jnp.where(is_last, db_last, 0.0))
            dc_ref[:, vs] = dvh.astype(BF16)
            dc_ref[:, 1024 + h * GLA_DK:1024 + (h + 1) * GLA_DK] = (dqe * eb * scale).astype(BF16)
            dc_ref[:, 1536 + h * GLA_DK:1536 + (h + 1) * GLA_DK] = (dke * enb + dkl * ekl).astype(BF16)
        dglog = _cumsum_rows(jnp.concatenate(db_parts, axis=1), upper)
        dpre = jnp.where(valid, dglog * (1.0 / GLA_NORMALIZER) / (1.0 + jnp.exp(pre)), 0.0)
        dpre_b = dpre.astype(BF16)
        dc_ref[:, 2048:2176] = _nt(dpre_b, wg_ref[...]).astype(BF16)
        dc_ref[:, 2176:2304] = jnp.zeros((C, 128), BF16)
        dwg_ref[...] += _tn(lr.astype(BF16), dpre_b)
        dbg_ref[...] += jnp.sum(dpre, axis=0, keepdims=True)

    row = lambda b, i: b * NC + (NC - 1 - i)
    blocks = (_nbytes((C, 512), F32) * 2 + _nbytes((C, 1024), F32) + _nbytes((C, 1024), BF16)
              + _nbytes((GLA_H, GLA_DV, GLA_DK), F32) + _nbytes((C, WC), BF16) + 3 * _nbytes((128, 512), F32))
    return pl.pallas_call(
        body,
        out_shape=(jax.ShapeDtypeStruct((T, WC), BF16), jax.ShapeDtypeStruct((128, GLA_KW), F32),
                   jax.ShapeDtypeStruct((1, GLA_KW), F32)),
        grid=(B, NC),
        in_specs=[
            pl.BlockSpec((C, 512), lambda b, i: (row(b, i), 10)),
            pl.BlockSpec((C, 512), lambda b, i: (row(b, i), 11)),
            pl.BlockSpec((C, 1024), lambda b, i: (row(b, i), 0)),
            pl.BlockSpec((C, 128), lambda b, i: (row(b, i), 3)),
            pl.BlockSpec((1, 1, GLA_H, GLA_DV, GLA_DK), lambda b, i: (b, NC - 1 - i, 0, 0, 0)),
            pl.BlockSpec((C, 1024), lambda b, i: (row(b, i), 0)),
            pl.BlockSpec((128, 512), lambda b, i: (0, 0)),
            pl.BlockSpec((1, 512), lambda b, i: (0, 0)),
        ],
        out_specs=(pl.BlockSpec((C, WC), lambda b, i: (row(b, i), 0)),
                   pl.BlockSpec((128, GLA_KW), lambda b, i: (0, 0)),
                   pl.BlockSpec((1, GLA_KW), lambda b, i: (0, 0))),
        scratch_shapes=[pltpu.VMEM((GLA_H, GLA_DV, GLA_DK), F32)],
        compiler_params=_params(("arbitrary", "arbitrary"), blocks, _nbytes((GLA_H, GLA_DV, GLA_DK), F32)),
        name="gla_bwd",
    )(projA, projA, projA, projB, ssave, doa, wg, bg)


def _attn_bwd_pre(dyin, projA, ob, B, Lp):
    T = B * Lp
    NQ = Lp // QB

    def body(dy_ref, z_ref, o_ref, do_ref, dz_ref, dr_ref, dcol_ref):
        j = pl.program_id(1)
        for h in range(MLA_H):
            hs = slice(h * MLA_DV, (h + 1) * MLA_DV)
            dy = dy_ref[:, hs]
            z = z_ref[:, hs]
            o = o_ref[:, hs]
            s = _sigmoid(z)
            do = dy * (z * s)
            do_ref[:, hs] = do.astype(BF16)
            dz_ref[:, hs] = (dy * o * (s * (1.0 + z * (1.0 - s)))).astype(BF16)
            dl = jnp.broadcast_to(jnp.sum(do * o, axis=-1, keepdims=True), (QB, LANES))
            dr_ref[0, h] = dl
            dcol_ref[0, h, pl.ds(j, 1), :] = jnp.transpose(dl)[0:1, :]

    rows = lambda c: pl.BlockSpec((QB, D), lambda b, j: (b * NQ + j, c))
    return pl.pallas_call(
        body,
        out_shape=(jax.ShapeDtypeStruct((T, D), BF16), jax.ShapeDtypeStruct((T, D), BF16),
                   jax.ShapeDtypeStruct((B, MLA_H, Lp, LANES), F32), jax.ShapeDtypeStruct((B, MLA_H, NQ, QB), F32)),
        grid=(B, NQ),
        in_specs=[rows(0), rows(2), rows(0)],
        out_specs=(rows(0), rows(0), pl.BlockSpec((1, MLA_H, QB, LANES), lambda b, j: (b, 0, j, 0)),
                   pl.BlockSpec((1, MLA_H, NQ, QB), lambda b, j: (b, 0, 0, 0))),
        compiler_params=_params(("parallel", "arbitrary"), 6 * _nbytes((QB, D), F32)),
        name="attn_bwd_pre",
    )(dyin, projA, ob)


def _attn_bwd_dq(q_att, k_att, v_att, do, lse_r, delta_r, B, Lp):
    T = B * Lp
    NQ = Lp // QB
    scale = 1.0 / math.sqrt(MLA_QK)

    HW = 2 * LANES

    def body(q_ref, k_ref, v_ref, do_ref, lse_ref, dl_ref, dq_ref):
        qi = pl.program_id(1)
        dq_ref[...] = jnp.zeros_like(dq_ref)
        row = qi * QB + lax.broadcasted_iota(jnp.int32, (QB, QB), 0)
        coli = lax.broadcasted_iota(jnp.int32, (QB, QB), 1)

        def step(kj, carry):
            off = pl.multiple_of(kj * QB, QB)
            ok = _attn_mask(row, kj * QB + coli)
            for h in range(MLA_H):
                ws = slice(h * HW, (h + 1) * HW)
                hs = slice(h * MLA_DV, (h + 1) * MLA_DV)
                kb = k_ref[pl.ds(off, QB), ws]
                vb = v_ref[pl.ds(off, QB), hs]
                lse = jnp.tile(lse_ref[0, h], (1, QB // LANES))
                delta = jnp.tile(dl_ref[0, h], (1, QB // LANES))
                s = _nt(q_ref[:, ws], kb) * scale
                p = jnp.where(ok, jnp.exp(s - lse), 0.0)
                ds = p * (_nt(do_ref[:, hs], vb) - delta) * scale
                dq_ref[:, ws] += _nn(ds.astype(BF16), kb)
            return carry

        lax.fori_loop(0, qi + 1, step, 0)

    blocks = (_nbytes((QB, 2048), BF16) + _nbytes((Lp, 2048), BF16) + _nbytes((Lp, 1024), BF16)
              + _nbytes((QB, 1024), BF16) + 2 * _nbytes((MLA_H, QB, LANES), F32) + _nbytes((QB, 2048), F32))
    return pl.pallas_call(
        body,
        out_shape=jax.ShapeDtypeStruct((T, MLA_H * HW), F32),
        grid=(B, NQ),
        in_specs=[
            pl.BlockSpec((QB, MLA_H * HW), lambda b, i: (b * NQ + i, 0)),
            pl.BlockSpec((Lp, MLA_H * HW), lambda b, i: (b, 0)),
            pl.BlockSpec((Lp, MLA_H * MLA_DV), lambda b, i: (b, 0)),
            pl.BlockSpec((QB, MLA_H * MLA_DV), lambda b, i: (b * NQ + i, 0)),
            pl.BlockSpec((1, MLA_H, QB, LANES), lambda b, i: (b, 0, i, 0)),
            pl.BlockSpec((1, MLA_H, QB, LANES), lambda b, i: (b, 0, i, 0)),
        ],
        out_specs=pl.BlockSpec((QB, MLA_H * HW), lambda b, i: (b * NQ + i, 0)),
        compiler_params=_params(("parallel", "parallel"), blocks),
        name="attn_bwd_dq",
    )(q_att, k_att, v_att, do, lse_r, delta_r)


def _attn_bwd_dkv(q_att, k_att, v_att, do, lse_c, delta_c, B, Lp):
    T = B * Lp
    NQ = Lp // QB
    scale = 1.0 / math.sqrt(MLA_QK)

    HW = 2 * LANES

    def body(q_ref, k_ref, v_ref, do_ref, lse_ref, dl_ref, dk_ref, dv_ref):
        kj = pl.program_id(1)
        dk_ref[...] = jnp.zeros_like(dk_ref)
        dv_ref[...] = jnp.zeros_like(dv_ref)
        col = kj * QB + lax.broadcasted_iota(jnp.int32, (QB, QB), 0)
        rowi = lax.broadcasted_iota(jnp.int32, (QB, QB), 1)

        def step(qi, carry):
            off = pl.multiple_of(qi * QB, QB)
            ok = _attn_mask(qi * QB + rowi, col)
            for h in range(MLA_H):
                ws = slice(h * HW, (h + 1) * HW)
                hs = slice(h * MLA_DV, (h + 1) * MLA_DV)
                qb = q_ref[pl.ds(off, QB), ws]
                dob = do_ref[pl.ds(off, QB), hs]
                lse = lse_ref[0, h, pl.ds(qi, 1), :]
                delta = dl_ref[0, h, pl.ds(qi, 1), :]
                s_t = _nt(k_ref[:, ws], qb) * scale
                p_t = jnp.where(ok, jnp.exp(s_t - lse), 0.0)
                dv_ref[:, hs] += _nn(p_t.astype(BF16), dob)
                ds_t = p_t * (_nt(v_ref[:, hs], dob) - delta) * scale
                dk_ref[:, ws] += _nn(ds_t.astype(BF16), qb)
            return carry

        lax.fori_loop(kj, NQ, step, 0)

    blocks = (_nbytes((Lp, 2048), BF16) + _nbytes((Lp, 1024), BF16) + _nbytes((QB, 3072), BF16)
              + 2 * _nbytes((MLA_H, NQ, QB), F32) + _nbytes((QB, 3072), F32))
    return pl.pallas_call(
        body,
        out_shape=(jax.ShapeDtypeStruct((T, MLA_H * HW), F32), jax.ShapeDtypeStruct((T, MLA_H * MLA_DV), F32)),
        grid=(B, NQ),
        in_specs=[
            pl.BlockSpec((Lp, MLA_H * HW), lambda b, j: (b, 0)),
            pl.BlockSpec((QB, MLA_H * HW), lambda b, j: (b * NQ + j, 0)),
            pl.BlockSpec((QB, MLA_H * MLA_DV), lambda b, j: (b * NQ + j, 0)),
            pl.BlockSpec((Lp, MLA_H * MLA_DV), lambda b, j: (b, 0)),
            pl.BlockSpec((1, MLA_H, NQ, QB), lambda b, j: (b, 0, 0, 0)),
            pl.BlockSpec((1, MLA_H, NQ, QB), lambda b, j: (b, 0, 0, 0)),
        ],
        out_specs=(pl.BlockSpec((QB, MLA_H * HW), lambda b, j: (b * NQ + j, 0)),
                   pl.BlockSpec((QB, MLA_H * MLA_DV), lambda b, j: (b * NQ + j, 0))),
        compiler_params=_params(("parallel", "parallel"), blocks),
        name="attn_bwd_dkv",
    )(q_att, k_att, v_att, do, lse_c, delta_c)


def _mla_bwd_post(dq, dk, dv, projB, cos_t, sin_t, gq, gkv, wuq2_t, wukv_t, B, Lp, tr):
    T = B * Lp
    nt = Lp // tr
    HW = 2 * LANES

    def body(dq_ref, dk_ref, dv_ref, pb_ref, cos_ref, sin_ref, gq_ref, gkv_ref, wuq_ref, wukv_ref,
             dqf_ref, dkvf_ref, de_ref, dgq_ref, dgkv_ref):
        first = (pl.program_id(0) == 0) & (pl.program_id(1) == 0)

        @pl.when(first)
        def _():
            dgq_ref[...] = jnp.zeros_like(dgq_ref)
            dgkv_ref[...] = jnp.zeros_like(dgkv_ref)

        cs = cos_ref[...]
        sn = sin_ref[...]
        rope_t = lambda t: t * cs + _swap_halves(t * sn)
        dkr = jnp.zeros((tr, LANES), F32)
        for h in range(MLA_H):
            dqf_ref[:, h * HW:h * HW + LANES] = dq_ref[:, h * HW:h * HW + LANES].astype(BF16)
            dqf_ref[:, h * HW + LANES:(h + 1) * HW] = rope_t(dq_ref[:, h * HW + LANES:(h + 1) * HW]).astype(BF16)
            dkvf_ref[:, h * HW:h * HW + LANES] = dk_ref[:, h * HW:h * HW + LANES].astype(BF16)
            dkvf_ref[:, h * HW + LANES:(h + 1) * HW] = dv_ref[:, h * MLA_DV:(h + 1) * MLA_DV].astype(BF16)
            dkr = dkr + dk_ref[:, h * HW + LANES:(h + 1) * HW]

        def norm_bwd(x, dn, g):
            r = lax.rsqrt(jnp.mean(x * x, axis=-1, keepdims=True) + EPS)
            xn = x * r
            t = dn * g
            return r * (t - xn * jnp.mean(t * xn, axis=-1, keepdims=True)), jnp.sum(dn * xn, axis=0, keepdims=True)

        dcq, dgq = norm_bwd(pb_ref[:, 0:Q_RANK], _nn(dqf_ref[...], wuq_ref[...]), gq_ref[...])
        dckv, dgkv = norm_bwd(pb_ref[:, Q_RANK:Q_RANK + KV_RANK], _nn(dkvf_ref[...], wukv_ref[...]), gkv_ref[...])
        dgq_ref[...] += dgq
        dgkv_ref[...] += dgkv
        de_ref[:, 0:Q_RANK] = dcq.astype(BF16)
        de_ref[:, Q_RANK:Q_RANK + KV_RANK] = dckv.astype(BF16)
        de_ref[:, 384:512] = rope_t(dkr).astype(BF16)

    rows = lambda w: pl.BlockSpec((tr, w), lambda b, j: (b * nt + j, 0))
    const = lambda s: pl.BlockSpec(s, lambda b, j: (0, 0))
    blocks = (2 * _nbytes((tr, 2048), F32) + _nbytes((tr, 1024), F32) + _nbytes((tr, 640), F32)
              + 2 * _nbytes((tr, 2048), BF16) + _nbytes((2048, 384), BF16) + 2 * _nbytes((tr, 2048), F32))
    return pl.pallas_call(
        body,
        out_shape=(jax.ShapeDtypeStruct((T, 2048), BF16), jax.ShapeDtypeStruct((T, 2048), BF16),
                   jax.ShapeDtypeStruct((T, 512), BF16), jax.ShapeDtypeStruct((1, Q_RANK), F32),
                   jax.ShapeDtypeStruct((1, KV_RANK), F32)),
        grid=(B, nt),
        in_specs=[rows(2048), rows(2048), rows(1024), rows(640),
                  pl.BlockSpec((tr, 128), lambda b, j: (j, 0)), pl.BlockSpec((tr, 128), lambda b, j: (j, 0)),
                  const((1, Q_RANK)), const((1, KV_RANK)), const((2048, Q_RANK)), const((2048, KV_RANK))],
        out_specs=(rows(2048), rows(2048), rows(512), const((1, Q_RANK)), const((1, KV_RANK))),
        compiler_params=_params(("arbitrary", "arbitrary"), blocks),
        name="mla_bwd_post",
    )(dq, dk, dv, projB, cos_t, sin_t, gq, gkv, wuq2_t, wukv_t)


def _du_matmul(dparts, wparts, tm):
    T = dparts[0].shape[0]
    n = len(dparts)

    def body(*refs):
        d_refs, w_refs, o_ref = refs[:n], refs[n:2 * n], refs[2 * n]
        acc = _nn(d_refs[0][...], w_refs[0][...])
        for d_ref, w_ref in zip(d_refs[1:], w_refs[1:]):
            acc = acc + _nn(d_ref[...], w_ref[...])
        o_ref[...] = acc

    widths = [d.shape[1] for d in dparts]
    blocks = sum(_nbytes((tm, w), BF16) + _nbytes((w, D), BF16) for w in widths) + _nbytes((tm, D), F32)
    return pl.pallas_call(
        body,
        out_shape=jax.ShapeDtypeStruct((T, D), F32),
        grid=(T // tm,),
        in_specs=[pl.BlockSpec((tm, w), lambda i: (i, 0)) for w in widths]
        + [pl.BlockSpec((w, D), lambda i: (0, 0)) for w in widths],
        out_specs=pl.BlockSpec((tm, D), lambda i: (i, 0)),
        compiler_params=_params(("parallel",), blocks),
        name="du_matmul",
    )(*dparts, *wparts)


def _in_norm_bwd(hp, dh1, du, g, B, Lp, seq):
    NQ = Lp // QB

    def body(h_ref, dh_ref, du_ref, g_ref, gx_ref, dmeta_ref, dg_ref):
        b = pl.program_id(0)
        j = pl.program_id(1)

        @pl.when((b == 0) & (j == 0))
        def _():
            dg_ref[...] = jnp.zeros_like(dg_ref)

        x = h_ref[...]
        r = lax.rsqrt(jnp.mean(x * x, axis=-1, keepdims=True) + EPS)
        xn = x * r
        du = du_ref[...]
        t = du * g_ref[...]
        dh0 = dh_ref[...] + r * (t - xn * jnp.mean(t * xn, axis=-1, keepdims=True))
        dg_ref[...] += jnp.sum(du * xn, axis=0, keepdims=True)
        gx_ref[0] = dh0

        @pl.when((j == 0) & (b == 0))
        def _():
            dmeta_ref[...] = dh0[FRONT:HEAD_ROWS, :]

        @pl.when((j == 0) & (b > 0))
        def _():
            dmeta_ref[...] += dh0[FRONT:HEAD_ROWS, :]

    rows = pl.BlockSpec((QB, D), lambda b, j: (b * NQ + j, 0))
    return pl.pallas_call(
        body,
        out_shape=(jax.ShapeDtypeStruct((B, seq, D), F32), jax.ShapeDtypeStruct((N_META, D), F32),
                   jax.ShapeDtypeStruct((1, D), F32)),
        grid=(B, NQ),
        in_specs=[rows, rows, rows, pl.BlockSpec((1, D), lambda b, j: (0, 0))],
        out_specs=(pl.BlockSpec((1, QB, D), lambda b, j: (b, jnp.maximum(j - 1, 0), 0)),
                   pl.BlockSpec((N_META, D), lambda b, j: (0, 0)), pl.BlockSpec((1, D), lambda b, j: (0, 0))),
        compiler_params=_params(("arbitrary", "arbitrary"), 5 * _nbytes((QB, D), F32)),
        name="in_norm_bwd",
    )(hp, dh1, du, g)


def _flat_tile(rows):
    return _div_tile(rows, 1296, 8)


def _add_half(gp, recv, c):
    H = gp.shape[2]
    th = _flat_tile(H)

    def body(c_ref, a_ref, b_ref, o_ref):
        o_ref[...] = a_ref[:, 0] + b_ref[...]

    return pl.pallas_call(
        body,
        out_shape=jax.ShapeDtypeStruct((4, H, LANES), F32),
        grid_spec=pltpu.PrefetchScalarGridSpec(
            num_scalar_prefetch=1,
            grid=(H // th,),
            in_specs=[pl.BlockSpec((4, 1, th, LANES), lambda i, c_ref: (0, c_ref[0], i, 0)),
                      pl.BlockSpec((4, th, LANES), lambda i, c_ref: (0, i, 0))],
            out_specs=pl.BlockSpec((4, th, LANES), lambda i, c_ref: (0, i, 0)),
        ),
        compiler_params=_params(("parallel",), 3 * _nbytes((4, th, LANES), F32)),
        name="grad_add_pair",
    )(c, gp, recv)


def _sum_chips(parts):
    H = parts.shape[1]
    th = _flat_tile(H)

    def body(p_ref, o_ref):
        o_ref[...] = ((p_ref[0] + p_ref[1]) + p_ref[2]) + p_ref[3]

    return pl.pallas_call(
        body,
        out_shape=jax.ShapeDtypeStruct((H, LANES), F32),
        grid=(H // th,),
        in_specs=[pl.BlockSpec((4, th, LANES), lambda i: (0, i, 0))],
        out_specs=pl.BlockSpec((th, LANES), lambda i: (i, 0)),
        compiler_params=_params(("parallel",), 5 * _nbytes((th, LANES), F32)),
        name="grad_sum_chips",
    )(parts)


def _adamw(w, g, m, v):
    R = w.shape[0]
    tr = _flat_tile(R)
    c1 = 1.0 - ADAM_B1 ** ADAM_STEP
    c2 = 1.0 - ADAM_B2 ** ADAM_STEP

    def body(w_ref, g_ref, m_ref, v_ref, d_ref, mo_ref, vo_ref):
        gv = g_ref[...]
        mn = ADAM_B1 * m_ref[...] + (1.0 - ADAM_B1) * gv
        vn = ADAM_B2 * v_ref[...] + (1.0 - ADAM_B2) * (gv * gv)
        mo_ref[...] = mn
        vo_ref[...] = vn
        d_ref[...] = -ADAM_LR * ((mn / c1) / (jnp.sqrt(vn / c2) + ADAM_EPS) + ADAM_WD * w_ref[...])

    spec = pl.BlockSpec((tr, LANES), lambda i: (i, 0))
    shp = jax.ShapeDtypeStruct((R, LANES), F32)
    return pl.pallas_call(
        body,
        out_shape=(shp, shp, shp),
        grid=(R // tr,),
        in_specs=[spec] * 4,
        out_specs=(spec, spec, spec),
        compiler_params=_params(("parallel",), 7 * _nbytes((tr, LANES), F32)),
        name="adamw",
    )(w, g, m, v)


def _mesh_pos():
    return lax.axis_index("x"), lax.axis_index("y"), lax.axis_index("c")


def _other_chips(x, y):
    return [(1 - x, y), (x, 1 - y), (1 - x, 1 - y)]


_ANY = pl.BlockSpec(memory_space=pl.ANY)


def _weight_gather(wsh):
    R = wsh.shape[0]
    H = R // 2

    def body(w_ref, out_ref, send_sems, recv_sems, local_sem):
        x, y, c = _mesh_pos()
        chips = _other_chips(x, y)
        mine = pl.ds(pl.multiple_of(c * H, 16), H)
        theirs = pl.ds(pl.multiple_of((1 - c) * H, 16), H)

        def copy(k, slot, half, to):
            ref = out_ref.at[slot, half]
            return pltpu.make_async_remote_copy(src_ref=ref, dst_ref=ref, send_sem=send_sems.at[k],
                                                recv_sem=recv_sems.at[k], device_id=to, device_id_type=MESH)

        own = pltpu.make_async_copy(w_ref, out_ref.at[2 * x + y], local_sem)
        own.start()
        first = []
        for k, (px, py) in enumerate(chips):
            cp = pltpu.make_async_remote_copy(src_ref=w_ref.at[mine], dst_ref=out_ref.at[2 * x + y, mine],
                                              send_sem=send_sems.at[k], recv_sem=recv_sems.at[k],
                                              device_id=(px, py, c), device_id_type=MESH)
            cp.start()
            first.append(cp)
        passed = []
        for k, (px, py) in enumerate(chips):
            copy(k, 2 * px + py, mine, (x, y, c)).wait_recv()
            cp = copy(3 + k, 2 * px + py, mine, (x, y, 1 - c))
            cp.start()
            passed.append(cp)
        for k, (px, py) in enumerate(chips):
            copy(3 + k, 2 * px + py, theirs, (x, y, c)).wait_recv()
        for cp in first + passed:
            cp.wait_send()
        own.wait()

    return pl.pallas_call(
        body,
        out_shape=jax.ShapeDtypeStruct((4, R, LANES), wsh.dtype),
        in_specs=[_ANY],
        out_specs=_ANY,
        scratch_shapes=[pltpu.SemaphoreType.DMA((6,)), pltpu.SemaphoreType.DMA((6,)), pltpu.SemaphoreType.DMA],
        name="weight_gather",
    )(wsh)


def _pair_swap(gp):
    H = gp.shape[2]

    def body(g_ref, out_ref, send_sem, recv_sem):
        x, y, c = _mesh_pos()
        cp = pltpu.make_async_remote_copy(src_ref=g_ref.at[:, 1 - c], dst_ref=out_ref, send_sem=send_sem,
                                          recv_sem=recv_sem, device_id=(x, y, 1 - c), device_id_type=MESH)
        cp.start()
        cp.wait_send()
        cp.wait_recv()

    return pl.pallas_call(
        body,
        out_shape=jax.ShapeDtypeStruct((4, H, LANES), gp.dtype),
        in_specs=[_ANY],
        out_specs=_ANY,
        scratch_shapes=[pltpu.SemaphoreType.DMA, pltpu.SemaphoreType.DMA],
        name="grad_pair_swap",
    )(gp)


def _chip_scatter(s1):
    H = s1.shape[1]

    def body(s_ref, out_ref, send_sems, recv_sems, local_sem):
        x, y, c = _mesh_pos()
        me = 2 * x + y
        own = pltpu.make_async_copy(s_ref.at[me], out_ref.at[me], local_sem)
        own.start()
        sends = []
        for k, (px, py) in enumerate(_other_chips(x, y)):
            cp = pltpu.make_async_remote_copy(src_ref=s_ref.at[2 * px + py], dst_ref=out_ref.at[me],
                                              send_sem=send_sems.at[k], recv_sem=recv_sems.at[k],
                                              device_id=(px, py, c), device_id_type=MESH)
            cp.start()
            sends.append(cp)
        for k, (px, py) in enumerate(_other_chips(x, y)):
            pltpu.make_async_remote_copy(src_ref=s_ref.at[me], dst_ref=out_ref.at[2 * px + py],
                                         send_sem=send_sems.at[k], recv_sem=recv_sems.at[k],
                                         device_id=(x, y, c), device_id_type=MESH).wait_recv()
        for cp in sends:
            cp.wait_send()
        own.wait()

    return pl.pallas_call(
        body,
        out_shape=jax.ShapeDtypeStruct((4, H, LANES), s1.dtype),
        in_specs=[_ANY],
        out_specs=_ANY,
        scratch_shapes=[pltpu.SemaphoreType.DMA((3,)), pltpu.SemaphoreType.DMA((3,)), pltpu.SemaphoreType.DMA],
        name="grad_chip_scatter",
    )(s1)


def _pair_join(f):
    H = f.shape[0]

    def body(f_ref, out_ref, send_sem, recv_sem, local_sem):
        x, y, c = _mesh_pos()
        own = pltpu.make_async_copy(f_ref, out_ref.at[c], local_sem)
        own.start()
        cp = pltpu.make_async_remote_copy(src_ref=f_ref, dst_ref=out_ref.at[c], send_sem=send_sem,
                                          recv_sem=recv_sem, device_id=(x, y, 1 - c), device_id_type=MESH)
        cp.start()
        pltpu.make_async_remote_copy(src_ref=f_ref, dst_ref=out_ref.at[1 - c], send_sem=send_sem,
                                     recv_sem=recv_sem, device_id=(x, y, c), device_id_type=MESH).wait_recv()
        cp.wait_send()
        own.wait()

    return pl.pallas_call(
        body,
        out_shape=jax.ShapeDtypeStruct((2, H, LANES), f.dtype),
        in_specs=[_ANY],
        out_specs=_ANY,
        scratch_shapes=[pltpu.SemaphoreType.DMA, pltpu.SemaphoreType.DMA, pltpu.SemaphoreType.DMA],
        name="grad_pair_join",
    )(f)


def _pack_rows(parts, dtype):
    flat = jnp.concatenate([p.reshape(-1).astype(dtype) for p in parts])
    flat = jnp.pad(flat, (0, FLAT_ROWS * LANES - flat.shape[0]))
    return flat.reshape(FLAT_ROWS, LANES)


def _rope_tables(Lp):
    inv = 1.0 / (ROPE_BASE ** (jnp.arange(0, ROPE, 2, dtype=F32) / ROPE))
    ang = (jnp.arange(Lp, dtype=F32) - FRONT)[:, None] * inv[None, :]
    cs, sn = jnp.cos(ang), jnp.sin(ang)
    return jnp.tile(cs, (1, 4)), jnp.concatenate([-sn, sn, -sn, sn], axis=1)


def _local_step(x, loss_target, meta, norm_g, w_in, gate_w, gate_b, gla_norm_g, gla_proj, q_norm_g, w_uq,
                kv_norm_g, w_ukv, mla_proj, w_out, final_norm_g):
    B, seq, _ = x.shape
    Lp = HEAD_ROWS + seq
    T = B * Lp
    tr = _div_tile(Lp, 544, 16)
    tq = _div_tile(T, 1024, QB)

    cuts = np.cumsum((0,) + SPLITS)
    col = lambda i: w_in[:, cuts[i]:cuts[i + 1]]
    w_q, w_k, w_v, w_lr, w_z, w_cq, w_ckv, w_kr, w_mz, w_gg, w_gm = [col(i) for i in range(11)]
    pad_cols = lambda w, n: jnp.pad(w, ((0, 0), (0, n - w.shape[1])))
    wA = jnp.concatenate([w_v, w_z, w_mz, w_gg, w_gm, w_q, w_k], axis=1)
    wB = jnp.concatenate([w_cq, w_ckv, pad_cols(w_lr, 128), pad_cols(w_kr, 128)], axis=1)
    wt_a = jnp.concatenate([w_gg, w_gm], axis=1).T
    wt_b = w_z.T
    wt_c = jnp.concatenate([w_v, w_q, w_k, pad_cols(w_lr, 256)], axis=1).T
    wt_d = w_mz.T
    wt_e = jnp.concatenate([w_cq, w_ckv, pad_cols(w_kr, 128)], axis=1).T
    wg = jnp.pad(gate_w, ((0, 128 - GLA_RANK), (0, 0)))
    wuq2 = jnp.pad(w_uq.reshape(Q_RANK, MLA_H, MLA_QK), ((0, 0), (0, 0), (0, 256 - MLA_QK))).reshape(Q_RANK, 2048)
    gn4 = jnp.tile(gla_norm_g, (1, GLA_H))
    cos_t, sin_t = _rope_tables(Lp)

    hp = jnp.concatenate([jnp.zeros((B, FRONT, D), F32), jnp.broadcast_to(meta[None], (B, N_META, D)), x], axis=1)
    hp = hp.reshape(T, D)

    u = _rms_in(hp, norm_g, tr)
    projA = _mm(u, wA, name="in_proj_a", tm=tq, tn=1024, tk=D)
    projB = _mm(u, wB, name="in_proj_b", tm=tq, tn=640, tk=D)
    oa, ya_in, ssave = _gla_fwd(projA, projB, wg, gate_b, gn4, B, Lp)
    ya = _mm(ya_in, gla_proj, name="gla_proj", tm=tq, tn=512, tk=D)
    q_att, k_att, v_att, cqn, ckvn = _mla_prep(projB, cos_t, sin_t, q_norm_g, kv_norm_g, wuq2, w_ukv, B, Lp, tr)
    ob, yb_in, lse_r, lse_c = _attn_fwd(q_att, k_att, v_att, projA, B, Lp)
    yb = _mm(yb_in, mla_proj, name="mla_proj", tm=tq, tn=512, tk=D)
    merged = _merge_fwd(projA, ya, yb, tr)
    mo = _mm(merged, w_out, name="w_out", tm=tq, tn=512, tk=D)
    dh1, dh1_b, loss, d_gf = _final_loss(hp, mo, final_norm_g.reshape(1, D), loss_target, B, Lp)

    dmerged = _mm(dh1_b, w_out.T, name="d_merged", tm=tq, tn=512, tk=D)
    g_w_out = _mm(merged, dh1_b, name="dw_out", trans_a=True, tm=D, tn=512, tk=tq)
    dya, dyb, dA = _merge_bwd(dmerged, projA, ya, yb, tr)
    dya_in = _mm(dya, gla_proj.T, name="d_ya_in", tm=tq, tn=512, tk=D)
    g_gla_proj = _mm(ya_in, dya, name="dw_gla_proj", trans_a=True, tm=D, tn=512, tk=tq)
    dyb_in = _mm(dyb, mla_proj.T, name="d_yb_in", tm=tq, tn=512, tk=D)
    g_mla_proj = _mm(yb_in, dyb, name="dw_mla_proj", trans_a=True, tm=D, tn=512, tk=tq)
    doa, dBz, d_gn = _gla_out_bwd(dya_in, oa, projA, gn4, tr)
    dC, g_wg, d_bg = _gla_bwd(projA, projB, ssave, doa, wg, gate_b, B, Lp)
    do, dDz, delta_r, delta_c = _attn_bwd_pre(dyb_in, projA, ob, B, Lp)
    dq = _attn_bwd_dq(q_att, k_att, v_att, do, lse_r, delta_r, B, Lp)
    dk, dv = _attn_bwd_dkv(q_att, k_att, v_att, do, lse_c, delta_c, B, Lp)
    dqf, dkvf, dE, d_gq, d_gkv = _mla_bwd_post(dq, dk, dv, projB, cos_t, sin_t, q_norm_g, kv_norm_g,
                                                wuq2.T, w_ukv.T, B, Lp, tr)
    g_wuq2 = _mm(cqn, dqf, name="dw_uq", trans_a=True, tm=Q_RANK, tn=512, tk=tq)
    g_wukv = _mm(ckvn, dkvf, name="dw_ukv", trans_a=True, tm=KV_RANK, tn=512, tk=tq)
    dparts = [dA, dBz, dC, dDz, dE]
    g_in = [_mm(u, dp, name="dw_in_%d" % i, trans_a=True, tm=D, tn=_div_tile(dp.shape[1], 1024, 256), tk=tq)
            for i, dp in enumerate(dparts)]
    du = _du_matmul(dparts, [wt_a, wt_b, wt_c, wt_d, wt_e], _div_tile(Lp, 272, 16))
    grad_x, d_meta, d_ng = _in_norm_bwd(hp, dh1, du, norm_g, B, Lp, seq)

    gA, gBz, gC, gDz, gE = g_in
    g_w_in = jnp.concatenate([
        gC[:, 1024:1536], gC[:, 1536:2048], gC[:, 0:1024], gC[:, 2048:2048 + GLA_RANK], gBz,
        gE[:, 0:Q_RANK], gE[:, Q_RANK:Q_RANK + KV_RANK], gE[:, 384:384 + ROPE], gDz, gA[:, 0:D], gA[:, D:2 * D]], axis=1)
    g_wuq = g_wuq2.reshape(Q_RANK, MLA_H, 256)[:, :, :MLA_QK].reshape(Q_RANK, MLA_H * MLA_QK)
    grads = dict(w_in=g_w_in, gla_gate_w=g_wg[:GLA_RANK], gla_proj=g_gla_proj, mla_w_uq=g_wuq, mla_w_ukv=g_wukv,
                 mla_proj=g_mla_proj, w_out=g_w_out, meta_tokens=d_meta, norm_g=d_ng, gla_gate_b=d_bg,
                 gla_norm_g=d_gn, mla_q_norm_g=d_gq, mla_kv_norm_g=d_gkv, final_norm_g=d_gf)
    return loss[0, 0], grad_x, grads


_COL_SHARDED = ("w_in", "gla_gate_w", "mla_w_uq", "mla_w_ukv")
_ROW_SHARDED = ("gla_proj", "mla_proj", "w_out")
_BIG = tuple(n for n, _ in SEG_ROWS)
_SMALL = tuple(n for n, _ in SMALL)
_ORDER = ("meta_tokens", "norm_g", "w_in", "gla_gate_w", "gla_gate_b", "gla_norm_g", "gla_proj", "mla_q_norm_g",
          "mla_w_uq", "mla_kv_norm_g", "mla_w_ukv", "mla_proj", "w_out", "final_norm_g")


def _shard_of(name, full, j):
    if name in _ROW_SHARDED:
        n = full.shape[0] // 4
        return full[j * n:(j + 1) * n]
    n = full.shape[1] // 4
    return full[:, j * n:(j + 1) * n]


def kernel(x, meta_tokens, norm_g, w_in, gla_gate_w, gla_gate_b, gla_norm_g, gla_proj, mla_q_norm_g, mla_w_uq, mla_kv_norm_g, mla_w_ukv, mla_proj, w_out, final_norm_g, loss_target, m_meta_tokens, m_norm_g, m_w_in, m_gla_gate_w, m_gla_gate_b, m_gla_norm_g, m_gla_proj, m_mla_q_norm_g, m_mla_w_uq, m_mla_kv_norm_g, m_mla_w_ukv, m_mla_proj, m_w_out, m_final_norm_g, v_meta_tokens, v_norm_g, v_w_in, v_gla_gate_w, v_gla_gate_b, v_gla_norm_g, v_gla_proj, v_mla_q_norm_g, v_mla_w_uq, v_mla_kv_norm_g, v_mla_w_ukv, v_mla_proj, v_w_out, v_final_norm_g):
    w = dict(meta_tokens=meta_tokens, norm_g=norm_g, w_in=w_in[0], gla_gate_w=gla_gate_w[0], gla_gate_b=gla_gate_b,
             gla_norm_g=gla_norm_g, gla_proj=gla_proj[0], mla_q_norm_g=mla_q_norm_g, mla_w_uq=mla_w_uq[0],
             mla_kv_norm_g=mla_kv_norm_g, mla_w_ukv=mla_w_ukv[0], mla_proj=mla_proj[0], w_out=w_out[0],
             final_norm_g=final_norm_g)
    mom = dict(meta_tokens=m_meta_tokens, norm_g=m_norm_g, w_in=m_w_in[0], gla_gate_w=m_gla_gate_w[0],
               gla_gate_b=m_gla_gate_b, gla_norm_g=m_gla_norm_g, gla_proj=m_gla_proj[0], mla_q_norm_g=m_mla_q_norm_g,
               mla_w_uq=m_mla_w_uq[0], mla_kv_norm_g=m_mla_kv_norm_g, mla_w_ukv=m_mla_w_ukv[0], mla_proj=m_mla_proj[0],
               w_out=m_w_out[0], final_norm_g=m_final_norm_g)
    var = dict(meta_tokens=v_meta_tokens, norm_g=v_norm_g, w_in=v_w_in[0], gla_gate_w=v_gla_gate_w[0],
               gla_gate_b=v_gla_gate_b, gla_norm_g=v_gla_norm_g, gla_proj=v_gla_proj[0], mla_q_norm_g=v_mla_q_norm_g,
               mla_w_uq=v_mla_w_uq[0], mla_kv_norm_g=v_mla_kv_norm_g, mla_w_ukv=v_mla_w_ukv[0], mla_proj=v_mla_proj[0],
               w_out=v_w_out[0], final_norm_g=v_final_norm_g)
    out_shapes = {n: a.shape for n, a in zip(_ORDER, (meta_tokens, norm_g, w_in, gla_gate_w, gla_gate_b, gla_norm_g,
                                                     gla_proj, mla_q_norm_g, mla_w_uq, mla_kv_norm_g, mla_w_ukv,
                                                     mla_proj, w_out, final_norm_g))}

    meta_bits = lax.bitcast_convert_type(meta_tokens, BF16)
    wsh = _pack_rows([w[n] for n in _BIG] + [meta_bits], BF16)
    wg_all = _weight_gather(wsh)
    full = {}
    off = 0
    for name, rows in SEG_ROWS:
        shard_shape = w[name].shape
        parts = [wg_all[j, off:off + rows].reshape(shard_shape) for j in range(4)]
        full[name] = jnp.concatenate(parts, axis=0 if name in _ROW_SHARDED else 1)
        off += rows
    meta_parts = [lax.bitcast_convert_type(wg_all[j, off:off + 2 * META_ROWS_F32].reshape(N_META, D // 4, 2), F32)
                  for j in range(4)]
    meta_full = jnp.concatenate(meta_parts, axis=1)

    loss_local, grad_x, g = _local_step(
        x, loss_target, meta_full, norm_g, full["w_in"], full["gla_gate_w"], gla_gate_b, gla_norm_g, full["gla_proj"],
        mla_q_norm_g, full["mla_w_uq"], mla_kv_norm_g, full["mla_w_ukv"], full["mla_proj"], full["w_out"], final_norm_g)
    loss = lax.psum(loss_local, ("x", "y", "c"))

    small = [g[n] for n in _SMALL]
    gp = jnp.stack([_pack_rows([_shard_of(n, g[n], j) for n in _BIG] + [_shard_of("meta_tokens", g["meta_tokens"], j)]
                               + small, F32) for j in range(4)])
    gp = gp.reshape(4, 2, HALF_ROWS, LANES)
    c_idx = lax.axis_index("c").astype(jnp.int32).reshape(1)
    s1 = _add_half(gp, _pair_swap(gp), c_idx)
    f_half = _sum_chips(_chip_scatter(s1))
    g_flat = _pair_join(f_half).reshape(FLAT_ROWS, LANES)

    names = _BIG + ("meta_tokens",) + _SMALL
    w_flat = _pack_rows([w[n] for n in names], F32)
    m_flat = _pack_rows([mom[n] for n in names], F32)
    v_flat = _pack_rows([var[n] for n in names], F32)
    d_flat, mn_flat, vn_flat = _adamw(w_flat, g_flat, m_flat, v_flat)

    def unpack(flat):
        res = {}
        o = 0
        flat1 = flat.reshape(-1)
        for n in names:
            size = int(np.prod(w[n].shape))
            res[n] = flat1[o:o + size].reshape(out_shapes[n])
            o += size
        return res

    outs = [unpack(f) for f in (g_flat, d_flat, mn_flat, vn_flat)]
    return (loss, grad_x, *[o[n] for o in outs for n in _ORDER])
```

```python
import functools
import math

import jax
import jax.numpy as jnp
import numpy as np
from jax import lax
from jax.experimental import pallas as pl
from jax.experimental.pallas import tpu as pltpu

F32 = jnp.float32
BF16 = jnp.bfloat16

D = 1024
N_META = 16
QB = 256
FRONT = QB - N_META
HEAD_ROWS = FRONT + N_META
assert FRONT % 64 == 48
EPS = 1e-6

GLA_H, GLA_DK, GLA_DV, GLA_RANK, GLA_C = 4, 128, 256, 16, 64
GLA_NORMALIZER = 16.0
GLA_KW, GLA_VW = GLA_H * GLA_DK, GLA_H * GLA_DV
MLA_H, NOPE, ROPE, MLA_DV, Q_RANK, KV_RANK = 8, 128, 64, 128, 256, 128
MLA_QK = NOPE + ROPE
ROPE_BASE = 10000.0
SPLITS = (GLA_KW, GLA_KW, GLA_VW, GLA_RANK, GLA_VW, Q_RANK, KV_RANK, ROPE, MLA_H * MLA_DV, D, D)
IN_WIDTH = sum(SPLITS)

ADAM_LR, ADAM_B1, ADAM_B2, ADAM_EPS, ADAM_WD, ADAM_STEP = 0.001, 0.9, 0.999, 1e-08, 0.01, 10

LANES = 128
VMEM_CAP_V7X = 56 * 1024 * 1024
MESH = pl.DeviceIdType.MESH
NEG = -1e30

SMALL = (("norm_g", D), ("gla_gate_b", GLA_KW), ("gla_norm_g", GLA_DV), ("mla_q_norm_g", Q_RANK),
         ("mla_kv_norm_g", KV_RANK), ("final_norm_g", D))


def _div_tile(n, target, mult):
    best = None
    for d in range(mult, min(n, target) + 1, mult):
        if n % d == 0:
            best = d
    assert best is not None, (n, target, mult)
    return best


def _params(sem, block_bytes, scratch_bytes=0):
    est = 2 * block_bytes + scratch_bytes + 12 * 1024 * 1024
    return pltpu.CompilerParams(dimension_semantics=sem, vmem_limit_bytes=int(min(max(est, 24 * 1024 * 1024), VMEM_CAP_V7X)))


def _nbytes(shape, dtype):
    return int(np.prod(shape)) * jnp.dtype(dtype).itemsize


def _sigmoid(x):
    return 1.0 / (1.0 + jnp.exp(-x))


def _nt(a, b):
    return lax.dot_general(a, b, (((1,), (1,)), ((), ())), preferred_element_type=F32)


def _tn(a, b):
    return lax.dot_general(a, b, (((0,), (0,)), ((), ())), preferred_element_type=F32)


def _nn(a, b):
    return jnp.dot(a, b, preferred_element_type=F32)


def _split3(x):
    a = x.astype(BF16)
    r = x - a.astype(F32)
    b = r.astype(BF16)
    c = (r - b.astype(F32)).astype(BF16)
    return a, b, c


def _mm(a, b, *, name, trans_a=False, trans_b=False, out_dtype=F32, tm, tn, tk):
    assert not (trans_a and trans_b)
    if trans_a:
        K, M = a.shape
    else:
        M, K = a.shape
    N = b.shape[0] if trans_b else b.shape[1]
    assert (b.shape[1] if trans_b else b.shape[0]) == K
    assert M % tm == 0 and N % tn == 0 and K % tk == 0, (name, M, N, K, tm, tn, tk)
    nk = K // tk

    def body(a_ref, b_ref, o_ref, *scratch):
        av = a_ref[...].astype(BF16)
        bv = b_ref[...].astype(BF16)
        prod = _tn(av, bv) if trans_a else (_nt(av, bv) if trans_b else _nn(av, bv))
        if nk == 1:
            o_ref[...] = prod.astype(out_dtype)
        else:
            acc = scratch[0]
            k = pl.program_id(2)

            @pl.when(k == 0)
            def _():
                acc[...] = prod

            @pl.when(k > 0)
            def _():
                acc[...] += prod

            @pl.when(k == nk - 1)
            def _():
                o_ref[...] = acc[...].astype(out_dtype)

    if trans_a:
        a_spec = pl.BlockSpec((tk, tm), lambda i, j, k: (k, i))
    else:
        a_spec = pl.BlockSpec((tm, tk), lambda i, j, k: (i, k))
    if trans_b:
        b_spec = pl.BlockSpec((tn, tk), lambda i, j, k: (j, k))
    else:
        b_spec = pl.BlockSpec((tk, tn), lambda i, j, k: (k, j))
    blocks = (_nbytes((tm, tk), a.dtype) + _nbytes((tk, tn), b.dtype) + _nbytes((tm, tn), out_dtype))
    scratch = [pltpu.VMEM((tm, tn), F32)] if nk > 1 else []
    return pl.pallas_call(
        body,
        out_shape=jax.ShapeDtypeStruct((M, N), out_dtype),
        grid=(M // tm, N // tn, nk),
        in_specs=[a_spec, b_spec],
        out_specs=pl.BlockSpec((tm, tn), lambda i, j, k: (i, j)),
        scratch_shapes=scratch,
        compiler_params=_params(("parallel", "parallel", "arbitrary"), blocks + _nbytes((tm, tn), F32),
                                _nbytes((tm, tn), F32) if nk > 1 else 0),
        name=name,
    )(a, b)


def _h_tile(j, x_ref, meta_ref):
    head = jnp.concatenate([jnp.zeros((FRONT, D), F32), meta_ref[...]], axis=0)
    return jnp.where(j > 0, x_ref[0], head)


def _x_spec():
    return pl.BlockSpec((1, QB, D), lambda b, j: (b, jnp.maximum(j - 1, 0), 0))


def _rms_in(x, meta, g, B, Lp):
    T = B * Lp
    NQ = Lp // QB

    def body(x_ref, meta_ref, g_ref, u_ref):
        h = _h_tile(pl.program_id(1), x_ref, meta_ref)
        r = lax.rsqrt(jnp.mean(h * h, axis=-1, keepdims=True) + EPS)
        u_ref[...] = (h * r * g_ref[...]).astype(BF16)

    return pl.pallas_call(
        body,
        out_shape=jax.ShapeDtypeStruct((T, D), BF16),
        grid=(B, NQ),
        in_specs=[_x_spec(), pl.BlockSpec((N_META, D), lambda b, j: (0, 0)), pl.BlockSpec((1, D), lambda b, j: (0, 0))],
        out_specs=pl.BlockSpec((QB, D), lambda b, j: (b * NQ + j, 0)),
        compiler_params=_params(("parallel", "parallel"), _nbytes((QB, D), F32) * 2),
        name="rms_in",
    )(x, meta, g)


def _gla_gate(lr, wg, bg, valid):
    pre = _nn(lr.astype(BF16), wg) + bg
    logsig = jnp.minimum(pre, 0.0) - jnp.log(1.0 + jnp.exp(-jnp.abs(pre)))
    return pre, jnp.where(valid, logsig / GLA_NORMALIZER, 0.0)


def _tri_masks():
    ri = lax.broadcasted_iota(jnp.int32, (GLA_C, GLA_C), 0)
    ci = lax.broadcasted_iota(jnp.int32, (GLA_C, GLA_C), 1)
    return ci <= ri, ci >= ri


def _cumsum_rows(x, ones_mask):
    w = jnp.where(ones_mask, 1.0, 0.0).astype(BF16)
    a, b, c = _split3(x)
    return _nn(w, a) + _nn(w, b) + _nn(w, c)


def _gla_fwd(projA, projB, wg, bg, gn4, B, Lp):
    T = B * Lp
    NC = Lp // GLA_C
    C = GLA_C
    scale = GLA_DK ** -0.5

    def body(q_ref, k_ref, v_ref, lr_ref, z_ref, wg_ref, bg_ref, gn_ref, oa_ref, ya_ref, ssave_ref, st_ref):
        n = pl.program_id(1)

        @pl.when(n == 0)
        def _():
            st_ref[...] = jnp.zeros_like(st_ref)

        ssave_ref[0, 0] = st_ref[...]
        pos = n * C + lax.broadcasted_iota(jnp.int32, (C, 1), 0)
        _, glog = _gla_gate(lr_ref[...], wg_ref[...], bg_ref[...], pos >= FRONT)
        lower, _ = _tri_masks()
        bcum = _cumsum_rows(glog, lower)
        is_last = lax.broadcasted_iota(jnp.int32, (C, 1), 0) == C - 1
        for h in range(GLA_H):
            ks = slice(h * GLA_DK, (h + 1) * GLA_DK)
            vs = slice(h * GLA_DV, (h + 1) * GLA_DV)
            bh = bcum[:, ks]
            blast = jnp.sum(jnp.where(is_last, bh, 0.0), axis=0, keepdims=True)
            qh = q_ref[:, ks] * scale
            kh = k_ref[:, ks]
            qe = (qh * jnp.exp(bh)).astype(BF16)
            ke = (kh * jnp.exp(-bh)).astype(BF16)
            kl = (kh * jnp.exp(blast - bh)).astype(BF16)
            vh = v_ref[:, vs].astype(BF16)
            a = jnp.where(lower, _nt(qe, ke), 0.0).astype(BF16)
            st = st_ref[h]
            o = _nn(a, vh) + _nt(qe, st.astype(BF16))
            st_ref[h] = st * jnp.exp(blast) + _tn(vh, kl)
            oa_ref[:, vs] = o
            on = o * lax.rsqrt(jnp.mean(o * o, axis=-1, keepdims=True) + EPS) * gn_ref[:, vs]
            z = z_ref[:, vs]
            ya_ref[:, vs] = (on * (z * _sigmoid(z))).astype(BF16)

    row = lambda b, n: b * NC + n
    blocks = (_nbytes((C, 512), F32) * 2 + _nbytes((C, 1024), F32) * 3 + _nbytes((C, 1024), BF16)
              + _nbytes((GLA_H, GLA_DV, GLA_DK), F32) + _nbytes((128, 512), BF16))
    return pl.pallas_call(
        body,
        out_shape=(jax.ShapeDtypeStruct((T, GLA_VW), F32), jax.ShapeDtypeStruct((T, GLA_VW), BF16),
                   jax.ShapeDtypeStruct((B, NC, GLA_H, GLA_DV, GLA_DK), F32)),
        grid=(B, NC),
        in_specs=[
            pl.BlockSpec((C, 512), lambda b, n: (row(b, n), 10)),
            pl.BlockSpec((C, 512), lambda b, n: (row(b, n), 11)),
            pl.BlockSpec((C, 1024), lambda b, n: (row(b, n), 0)),
            pl.BlockSpec((C, 128), lambda b, n: (row(b, n), 3)),
            pl.BlockSpec((C, 1024), lambda b, n: (row(b, n), 1)),
            pl.BlockSpec((128, 512), lambda b, n: (0, 0)),
            pl.BlockSpec((1, 512), lambda b, n: (0, 0)),
            pl.BlockSpec((1, 1024), lambda b, n: (0, 0)),
        ],
        out_specs=(pl.BlockSpec((C, 1024), lambda b, n: (row(b, n), 0)),
                   pl.BlockSpec((C, 1024), lambda b, n: (row(b, n), 0)),
                   pl.BlockSpec((1, 1, GLA_H, GLA_DV, GLA_DK), lambda b, n: (b, n, 0, 0, 0))),
        scratch_shapes=[pltpu.VMEM((GLA_H, GLA_DV, GLA_DK), F32)],
        compiler_params=_params(("parallel", "arbitrary"), blocks, _nbytes((GLA_H, GLA_DV, GLA_DK), F32)),
        name="gla_fwd",
    )(projA, projA, projA, projB, projA, wg, bg, gn4)


def _swap_halves(x):
    lane = lax.broadcasted_iota(jnp.int32, x.shape, 1)
    return jnp.where((lane % 64) < 32, pltpu.roll(x, 96, 1), pltpu.roll(x, 32, 1))


def _mla_prep(projB, cos_t, sin_t, gq, gkv, wuq2, wukv, B, Lp, tr):
    T = B * Lp
    nt = Lp // tr
    HW = 2 * LANES

    def body(pb_ref, cos_ref, sin_ref, gq_ref, gkv_ref, wuq_ref, wukv_ref, q_ref, k_ref, v_ref, cqn_ref, ckvn_ref):
        cq = pb_ref[:, 0:Q_RANK]
        ckv = pb_ref[:, Q_RANK:Q_RANK + KV_RANK]
        kr = pb_ref[:, 512:640]
        cqn = (cq * lax.rsqrt(jnp.mean(cq * cq, axis=-1, keepdims=True) + EPS) * gq_ref[...]).astype(BF16)
        ckvn = (ckv * lax.rsqrt(jnp.mean(ckv * ckv, axis=-1, keepdims=True) + EPS) * gkv_ref[...]).astype(BF16)
        cqn_ref[...] = cqn
        ckvn_ref[...] = ckvn
        qf = _nn(cqn, wuq_ref[...])
        kvf = _nn(ckvn, wukv_ref[...])
        cs = cos_ref[...]
        sn = sin_ref[...]
        rope = lambda t: t * cs + _swap_halves(t) * sn
        kr_r = rope(kr).astype(BF16)
        for h in range(MLA_H):
            q_ref[:, h * HW:h * HW + LANES] = qf[:, h * HW:h * HW + LANES].astype(BF16)
            q_ref[:, h * HW + LANES:(h + 1) * HW] = rope(qf[:, h * HW + LANES:(h + 1) * HW]).astype(BF16)
            k_ref[:, h * HW:h * HW + LANES] = kvf[:, h * HW:h * HW + LANES].astype(BF16)
            k_ref[:, h * HW + LANES:(h + 1) * HW] = kr_r
            v_ref[:, h * MLA_DV:(h + 1) * MLA_DV] = kvf[:, h * HW + LANES:(h + 1) * HW].astype(BF16)

    blocks = (_nbytes((tr, 640), F32) + 2 * _nbytes((tr, 128), F32) + _nbytes((Q_RANK, 2048), BF16)
              + _nbytes((KV_RANK, 2048), BF16) + _nbytes((tr, 2048 * 2 + 1024 + 384), BF16)
              + 2 * _nbytes((tr, 2048), F32))
    return pl.pallas_call(
        body,
        out_shape=(jax.ShapeDtypeStruct((T, MLA_H * HW), BF16), jax.ShapeDtypeStruct((T, MLA_H * HW), BF16),
                   jax.ShapeDtypeStruct((T, MLA_H * MLA_DV), BF16), jax.ShapeDtypeStruct((T, Q_RANK), BF16),
                   jax.ShapeDtypeStruct((T, KV_RANK), BF16)),
        grid=(B, nt),
        in_specs=[
            pl.BlockSpec((tr, 640), lambda b, j: (b * nt + j, 0)),
            pl.BlockSpec((tr, 128), lambda b, j: (j, 0)),
            pl.BlockSpec((tr, 128), lambda b, j: (j, 0)),
            pl.BlockSpec((1, Q_RANK), lambda b, j: (0, 0)),
            pl.BlockSpec((1, KV_RANK), lambda b, j: (0, 0)),
            pl.BlockSpec((Q_RANK, 2048), lambda b, j: (0, 0)),
            pl.BlockSpec((KV_RANK, 2048), lambda b, j: (0, 0)),
        ],
        out_specs=(pl.BlockSpec((tr, 2048), lambda b, j: (b * nt + j, 0)),
                   pl.BlockSpec((tr, 2048), lambda b, j: (b * nt + j, 0)),
                   pl.BlockSpec((tr, 1024), lambda b, j: (b * nt + j, 0)),
                   pl.BlockSpec((tr, Q_RANK), lambda b, j: (b * nt + j, 0)),
                   pl.BlockSpec((tr, KV_RANK), lambda b, j: (b * nt + j, 0))),
        compiler_params=_params(("parallel", "parallel"), blocks),
        name="mla_prep",
    )(projB, cos_t, sin_t, gq, gkv, wuq2, wukv)


def _attn_mask(row, col):
    return (col <= row) & ((col >= FRONT) | (row < FRONT))


def _attn_fwd(q_att, k_att, v_att, projA, B, Lp):
    T = B * Lp
    NQ = Lp // QB
    HW = 2 * LANES
    scale = 1.0 / math.sqrt(MLA_QK)

    def body(q_ref, k_ref, v_ref, mz_ref, o_ref, yb_ref, lser_ref, lsec_ref, m_ref, l_ref, acc_ref):
        qi = pl.program_id(1)
        m_ref[...] = jnp.full(m_ref.shape, NEG, F32)
        l_ref[...] = jnp.zeros_like(l_ref)
        acc_ref[...] = jnp.zeros_like(acc_ref)
        row = qi * QB + lax.broadcasted_iota(jnp.int32, (QB, QB), 0)
        coli = lax.broadcasted_iota(jnp.int32, (QB, QB), 1)

        def step(kj, carry):
            off = pl.multiple_of(kj * QB, QB)
            ok = _attn_mask(row, kj * QB + coli)
            for h in range(MLA_H):
                q = q_ref[:, h * HW:(h + 1) * HW]
                kb = k_ref[pl.ds(off, QB), h * HW:(h + 1) * HW]
                vb = v_ref[pl.ds(off, QB), h * MLA_DV:(h + 1) * MLA_DV]
                s = jnp.where(ok, _nt(q, kb) * scale, NEG)
                m_old = m_ref[h]
                m_new = jnp.maximum(m_old, jnp.max(s, axis=-1, keepdims=True))
                alpha = jnp.exp(m_old - m_new)
                p = jnp.exp(s - jnp.tile(m_new, (1, QB // LANES)))
                m_ref[h] = m_new
                l_ref[h] = alpha * l_ref[h] + jnp.sum(p, axis=-1, keepdims=True)
                acc_ref[h] = alpha * acc_ref[h] + _nn(p.astype(BF16), vb)
            return carry

        lax.fori_loop(0, qi + 1, step, 0)
        for h in range(MLA_H):
            hs = slice(h * MLA_DV, (h + 1) * MLA_DV)
            l = l_ref[h]
            o = acc_ref[h] / l
            o_ref[:, hs] = o
            z = mz_ref[:, hs]
            yb_ref[:, hs] = (o * (z * _sigmoid(z))).astype(BF16)
            lse = m_ref[h] + jnp.log(l)
            lser_ref[0, h] = lse
            lsec_ref[0, h, pl.ds(qi, 1), :] = jnp.transpose(lse)[0:1, :]

    blocks = (_nbytes((QB, 2048), BF16) + _nbytes((Lp, 2048), BF16) + _nbytes((Lp, 1024), BF16)
              + 2 * _nbytes((QB, 1024), F32) + _nbytes((QB, 1024), BF16) + _nbytes((MLA_H, QB, LANES), F32)
              + _nbytes((MLA_H, NQ, QB), F32))
    return pl.pallas_call(
        body,
        out_shape=(jax.ShapeDtypeStruct((T, MLA_H * MLA_DV), F32), jax.ShapeDtypeStruct((T, MLA_H * MLA_DV), BF16),
                   jax.ShapeDtypeStruct((B, MLA_H, Lp, LANES), F32), jax.ShapeDtypeStruct((B, MLA_H, NQ, QB), F32)),
        grid=(B, NQ),
        in_specs=[
            pl.BlockSpec((QB, MLA_H * HW), lambda b, i: (b * NQ + i, 0)),
            pl.BlockSpec((Lp, MLA_H * HW), lambda b, i: (b, 0)),
            pl.BlockSpec((Lp, MLA_H * MLA_DV), lambda b, i: (b, 0)),
            pl.BlockSpec((QB, 1024), lambda b, i: (b * NQ + i, 2)),
        ],
        out_specs=(pl.BlockSpec((QB, 1024), lambda b, i: (b * NQ + i, 0)),
                   pl.BlockSpec((QB, 1024), lambda b, i: (b * NQ + i, 0)),
                   pl.BlockSpec((1, MLA_H, QB, LANES), lambda b, i: (b, 0, i, 0)),
                   pl.BlockSpec((1, MLA_H, NQ, QB), lambda b, i: (b, 0, 0, 0))),
        scratch_shapes=[pltpu.VMEM((MLA_H, QB, LANES), F32), pltpu.VMEM((MLA_H, QB, LANES), F32),
                        pltpu.VMEM((MLA_H, QB, MLA_DV), F32)],
        compiler_params=_params(("parallel", "arbitrary"), blocks, 3 * _nbytes((MLA_H, QB, LANES), F32)),
        name="attn_fwd",
    )(q_att, k_att, v_att, projA)


def _merge_fwd(projA, ya, yb, tr):
    T = ya.shape[0]

    def body(gg_ref, gm_ref, ya_ref, yb_ref, o_ref):
        o_ref[...] = (_sigmoid(gg_ref[...]) * ya_ref[...] + _sigmoid(gm_ref[...]) * yb_ref[...]).astype(BF16)

    spec = lambda c: pl.BlockSpec((tr, D), lambda i: (i, c))
    return pl.pallas_call(
        body,
        out_shape=jax.ShapeDtypeStruct((T, D), BF16),
        grid=(T // tr,),
        in_specs=[spec(3), spec(4), spec(0), spec(0)],
        out_specs=spec(0),
        compiler_params=_params(("parallel",), 5 * _nbytes((tr, D), F32)),
        name="merge_fwd",
    )(projA, projA, ya, yb)


def _final_loss(x, meta, mo, gf, tgt, B, Lp):
    T = B * Lp
    NQ = Lp // QB

    def body(x_ref, meta_ref, mo_ref, gf_ref, t_ref, dh_ref, dhb_ref, loss_ref, dgf_ref):
        b = pl.program_id(0)
        j = pl.program_id(1)

        @pl.when((b == 0) & (j == 0))
        def _():
            loss_ref[...] = jnp.zeros_like(loss_ref)
            dgf_ref[...] = jnp.zeros_like(dgf_ref)

        h1 = _h_tile(j, x_ref, meta_ref) + mo_ref[...]
        r = lax.rsqrt(jnp.mean(h1 * h1, axis=-1, keepdims=True) + EPS)
        hn = h1 * r
        gfv = gf_ref[...]
        diff = jnp.where(j > 0, hn * gfv - t_ref[0], 0.0)
        loss_ref[...] += (0.5 / D) * jnp.sum(jnp.sum(diff * diff, axis=-1, keepdims=True), axis=0, keepdims=True)
        dout = diff * (1.0 / D)
        dgf_ref[...] += jnp.sum(dout * hn, axis=0, keepdims=True)
        dhn = dout * gfv
        dh = r * (dhn - hn * jnp.mean(dhn * hn, axis=-1, keepdims=True))
        dh_ref[...] = dh
        dhb_ref[...] = dh.astype(BF16)

    rows = pl.BlockSpec((QB, D), lambda b, j: (b * NQ + j, 0))
    return pl.pallas_call(
        body,
        out_shape=(jax.ShapeDtypeStruct((T, D), F32), jax.ShapeDtypeStruct((T, D), BF16),
                   jax.ShapeDtypeStruct((1, 1), F32), jax.ShapeDtypeStruct((1, D), F32)),
        grid=(B, NQ),
        in_specs=[_x_spec(), pl.BlockSpec((N_META, D), lambda b, j: (0, 0)), rows,
                  pl.BlockSpec((1, D), lambda b, j: (0, 0)), _x_spec()],
        out_specs=(rows, rows, pl.BlockSpec((1, 1), lambda b, j: (0, 0)), pl.BlockSpec((1, D), lambda b, j: (0, 0))),
        compiler_params=_params(("arbitrary", "arbitrary"), 5 * _nbytes((QB, D), F32)),
        name="final_loss",
    )(x, meta, mo, gf, tgt)


def _merge_bwd(dm, projA, ya, yb, tr):
    T = dm.shape[0]

    def body(dm_ref, gg_ref, gm_ref, ya_ref, yb_ref, dya_ref, dyb_ref, da_ref):
        d = dm_ref[...]
        sg = _sigmoid(gg_ref[...])
        sm = _sigmoid(gm_ref[...])
        dya_ref[...] = (d * sg).astype(BF16)
        dyb_ref[...] = (d * sm).astype(BF16)
        da_ref[:, 0:D] = (d * ya_ref[...] * (sg * (1.0 - sg))).astype(BF16)
        da_ref[:, D:2 * D] = (d * yb_ref[...] * (sm * (1.0 - sm))).astype(BF16)

    spec = lambda c: pl.BlockSpec((tr, D), lambda i: (i, c))
    return pl.pallas_call(
        body,
        out_shape=(jax.ShapeDtypeStruct((T, D), BF16), jax.ShapeDtypeStruct((T, D), BF16),
                   jax.ShapeDtypeStruct((T, 2 * D), BF16)),
        grid=(T // tr,),
        in_specs=[spec(0), spec(3), spec(4), spec(0), spec(0)],
        out_specs=(spec(0), spec(0), pl.BlockSpec((tr, 2 * D), lambda i: (i, 0))),
        compiler_params=_params(("parallel",), 8 * _nbytes((tr, D), F32)),
        name="merge_bwd",
    )(dm, projA, projA, ya, yb)


def _gla_out_bwd(dyin, oa, projA, gn4, tr):
    T = dyin.shape[0]
    nsteps = T // tr

    def body(dy_ref, oa_ref, z_ref, gn_ref, do_ref, dz_ref, dgn_ref, acc_ref):
        i = pl.program_id(0)

        @pl.when(i == 0)
        def _():
            acc_ref[...] = jnp.zeros_like(acc_ref)

        for h in range(GLA_H):
            vs = slice(h * GLA_DV, (h + 1) * GLA_DV)
            dy = dy_ref[:, vs]
            o = oa_ref[:, vs]
            z = z_ref[:, vs]
            gn = gn_ref[:, vs]
            s = _sigmoid(z)
            ra = lax.rsqrt(jnp.mean(o * o, axis=-1, keepdims=True) + EPS)
            on = o * ra
            don = dy * (z * s)
            t = don * gn
            do_ref[:, vs] = (ra * (t - on * jnp.mean(t * on, axis=-1, keepdims=True))).astype(BF16)
            dz_ref[:, vs] = (dy * (on * gn) * (s * (1.0 + z * (1.0 - s)))).astype(BF16)
            acc_ref[:, vs] += jnp.sum(don * on, axis=0, keepdims=True)

        @pl.when(i == nsteps - 1)
        def _():
            a = acc_ref[...]
            dgn_ref[...] = a[:, 0:256] + a[:, 256:512] + a[:, 512:768] + a[:, 768:1024]

    spec = lambda c: pl.BlockSpec((tr, D), lambda i: (i, c))
    return pl.pallas_call(
        body,
        out_shape=(jax.ShapeDtypeStruct((T, D), BF16), jax.ShapeDtypeStruct((T, D), BF16),
                   jax.ShapeDtypeStruct((1, GLA_DV), F32)),
        grid=(nsteps,),
        in_specs=[spec(0), spec(0), spec(1), pl.BlockSpec((1, D), lambda i: (0, 0))],
        out_specs=(spec(0), spec(0), pl.BlockSpec((1, GLA_DV), lambda i: (0, 0))),
        scratch_shapes=[pltpu.VMEM((1, D), F32)],
        compiler_params=_params(("arbitrary",), 6 * _nbytes((tr, D), F32)),
        name="gla_out_bwd",
    )(dyin, oa, projA, gn4)


def _gla_bwd(projA, projB, ssave, doa, wg, bg, B, Lp):
    T = B * Lp
    NC = Lp // GLA_C
    C = GLA_C
    scale = GLA_DK ** -0.5
    WC = 2304

    def body(q_ref, k_ref, v_ref, lr_ref, ss_ref, do_ref, wg_ref, bg_ref, dc_ref, dwg_ref, dbg_ref, dst_ref):
        b = pl.program_id(0)
        i = pl.program_id(1)
        n = NC - 1 - i

        @pl.when(i == 0)
        def _():
            dst_ref[...] = jnp.zeros_like(dst_ref)

        @pl.when((b == 0) & (i == 0))
        def _():
            dwg_ref[...] = jnp.zeros_like(dwg_ref)
            dbg_ref[...] = jnp.zeros_like(dbg_ref)

        pos = n * C + lax.broadcasted_iota(jnp.int32, (C, 1), 0)
        valid = pos >= FRONT
        lr = lr_ref[...]
        pre, glog = _gla_gate(lr, wg_ref[...], bg_ref[...], valid)
        lower, upper = _tri_masks()
        bcum = _cumsum_rows(glog, lower)
        is_last = lax.broadcasted_iota(jnp.int32, (C, 1), 0) == C - 1
        db_parts = []
        for h in range(GLA_H):
            ks = slice(h * GLA_DK, (h + 1) * GLA_DK)
            vs = slice(h * GLA_DV, (h + 1) * GLA_DV)
            bh = bcum[:, ks]
            blast = jnp.sum(jnp.where(is_last, bh, 0.0), axis=0, keepdims=True)
            eb, enb, ekl, ebl = jnp.exp(bh), jnp.exp(-bh), jnp.exp(blast - bh), jnp.exp(blast)
            qh = q_ref[:, ks] * scale
            kh = k_ref[:, ks]
            qe_f, ke_f, kl_f = qh * eb, kh * enb, kh * ekl
            qe, ke, kl = qe_f.astype(BF16), ke_f.astype(BF16), kl_f.astype(BF16)
            vh = v_ref[:, vs].astype(BF16)
            doh = do_ref[:, vs]
            st = ss_ref[0, 0, h]
            dst = dst_ref[h]
            st_b, dst_b = st.astype(BF16), dst.astype(BF16)
            da = jnp.where(lower, _nt(doh, vh), 0.0).astype(BF16)
            da_t = jnp.where(upper, _nt(vh, doh), 0.0).astype(BF16)
            a_t = jnp.where(upper, _nt(ke, qe), 0.0).astype(BF16)
            dqe = _nn(da, ke) + _nn(doh, st_b)
            dke = _nn(da_t, qe)
            dvh = _nn(a_t, doh) + _nt(kl, dst_b)
            dkl = _nn(vh, dst_b)
            dst_ref[h] = dst * ebl + _tn(doh, qe)
            deb = jnp.sum(st * dst, axis=0, keepdims=True)
            db = dqe * qe_f - dke * ke_f - dkl * kl_f
            db_last = jnp.sum(dkl * kl_f, axis=0, keepdims=True) + deb * ebl
            db_parts.append(db + jnp.where(is_last, db_last, 0.0))
            dc_ref[:, vs] = dvh.astype(BF16)
            dc_ref[:, 1024 + h * GLA_DK:1024 + (h + 1) * GLA_DK] = (dqe * eb * scale).astype(BF16)
            dc_ref[:, 1536 + h * GLA_DK:1536 + (h + 1) * GLA_DK] = (dke * enb + dkl * ekl).astype(BF16)
        dglog = _cumsum_rows(jnp.concatenate(db_parts, axis=1), upper)
        dpre = jnp.where(valid, dglog * (1.0 / GLA_NORMALIZER) / (1.0 + jnp.exp(pre)), 0.0)
        dpre_b = dpre.astype(BF16)
        dc_ref[:, 2048:2176] = _nt(dpre_b, wg_ref[...]).astype(BF16)
        dc_ref[:, 2176:2304] = jnp.zeros((C, 128), BF16)
        dwg_ref[...] += _tn(lr.astype(BF16), dpre_b)
        dbg_ref[...] += jnp.sum(dpre, axis=0, keepdims=True)

    row = lambda b, i: b * NC + (NC - 1 - i)
    blocks = (_nbytes((C, 512), F32) * 2 + _nbytes((C, 1024), F32) + _nbytes((C, 1024), BF16)
              + _nbytes((GLA_H, GLA_DV, GLA_DK), F32) + _nbytes((C, WC), BF16) + 3 * _nbytes((128, 512), F32))
    return pl.pallas_call(
        body,
        out_shape=(jax.ShapeDtypeStruct((T, WC), BF16), jax.ShapeDtypeStruct((128, GLA_KW), F32),
                   jax.ShapeDtypeStruct((1, GLA_KW), F32)),
        grid=(B, NC),
        in_specs=[
            pl.BlockSpec((C, 512), lambda b, i: (row(b, i), 10)),
            pl.BlockSpec((C, 512), lambda b, i: (row(b, i), 11)),
            pl.BlockSpec((C, 1024), lambda b, i: (row(b, i), 0)),
            pl.BlockSpec((C, 128), lambda b, i: (row(b, i), 3)),
            pl.BlockSpec((1, 1, GLA_H, GLA_DV, GLA_DK), lambda b, i: (b, NC - 1 - i, 0, 0, 0)),
            pl.BlockSpec((C, 1024), lambda b, i: (row(b, i), 0)),
            pl.BlockSpec((128, 512), lambda b, i: (0, 0)),
            pl.BlockSpec((1, 512), lambda b, i: (0, 0)),
        ],
        out_specs=(pl.BlockSpec((C, WC), lambda b, i: (row(b, i), 0)),
                   pl.BlockSpec((128, GLA_KW), lambda b, i: (0, 0)),
                   pl.BlockSpec((1, GLA_KW), lambda b, i: (0, 0))),
        scratch_shapes=[pltpu.VMEM((GLA_H, GLA_DV, GLA_DK), F32)],
        compiler_params=_params(("arbitrary", "arbitrary"), blocks, _nbytes((GLA_H, GLA_DV, GLA_DK), F32)),
        name="gla_bwd",
    )(projA, projA, projA, projB, ssave, doa, wg, bg)


def _attn_bwd_pre(dyin, projA, ob, B, Lp):
    T = B * Lp
    NQ = Lp // QB

    def body(dy_ref, z_ref, o_ref, do_ref, dz_ref, dr_ref, dcol_ref):
        j = pl.program_id(1)
        for h in range(MLA_H):
            hs = slice(h * MLA_DV, (h + 1) * MLA_DV)
            dy = dy_ref[:, hs]
            z = z_ref[:, hs]
            o = o_ref[:, hs]
            s = _sigmoid(z)
            do = dy * (z * s)
            do_ref[:, hs] = do.astype(BF16)
            dz_ref[:, hs] = (dy * o * (s * (1.0 + z * (1.0 - s)))).astype(BF16)
            dl = jnp.broadcast_to(jnp.sum(do * o, axis=-1, keepdims=True), (QB, LANES))
            dr_ref[0, h] = dl
            dcol_ref[0, h, pl.ds(j, 1), :] = jnp.transpose(dl)[0:1, :]

    rows = lambda c: pl.BlockSpec((QB, D), lambda b, j: (b * NQ + j, c))
    return pl.pallas_call(
        body,
        out_shape=(jax.ShapeDtypeStruct((T, D), BF16), jax.ShapeDtypeStruct((T, D), BF16),
                   jax.ShapeDtypeStruct((B, MLA_H, Lp, LANES), F32), jax.ShapeDtypeStruct((B, MLA_H, NQ, QB), F32)),
        grid=(B, NQ),
        in_specs=[rows(0), rows(2), rows(0)],
        out_specs=(rows(0), rows(0), pl.BlockSpec((1, MLA_H, QB, LANES), lambda b, j: (b, 0, j, 0)),
                   pl.BlockSpec((1, MLA_H, NQ, QB), lambda b, j: (b, 0, 0, 0))),
        compiler_params=_params(("parallel", "arbitrary"), 6 * _nbytes((QB, D), F32)),
        name="attn_bwd_pre",
    )(dyin, projA, ob)


def _attn_bwd_dq(q_att, k_att, v_att, do, lse_r, delta_r, B, Lp):
    T = B * Lp
    NQ = Lp // QB
    scale = 1.0 / math.sqrt(MLA_QK)

    HW = 2 * LANES

    def body(q_ref, k_ref, v_ref, do_ref, lse_ref, dl_ref, dq_ref):
        qi = pl.program_id(1)
        dq_ref[...] = jnp.zeros_like(dq_ref)
        row = qi * QB + lax.broadcasted_iota(jnp.int32, (QB, QB), 0)
        coli = lax.broadcasted_iota(jnp.int32, (QB, QB), 1)

        def step(kj, carry):
            off = pl.multiple_of(kj * QB, QB)
            ok = _attn_mask(row, kj * QB + coli)
            for h in range(MLA_H):
                ws = slice(h * HW, (h + 1) * HW)
                hs = slice(h * MLA_DV, (h + 1) * MLA_DV)
                kb = k_ref[pl.ds(off, QB), ws]
                vb = v_ref[pl.ds(off, QB), hs]
                lse = jnp.tile(lse_ref[0, h], (1, QB // LANES))
                delta = jnp.tile(dl_ref[0, h], (1, QB // LANES))
                s = _nt(q_ref[:, ws], kb) * scale
                p = jnp.where(ok, jnp.exp(s - lse), 0.0)
                ds = p * (_nt(do_ref[:, hs], vb) - delta) * scale
                dq_ref[:, ws] += _nn(ds.astype(BF16), kb)
            return carry

        lax.fori_loop(0, qi + 1, step, 0)

    blocks = (_nbytes((QB, 2048), BF16) + _nbytes((Lp, 2048), BF16) + _nbytes((Lp, 1024), BF16)
              + _nbytes((QB, 1024), BF16) + 2 * _nbytes((MLA_H, QB, LANES), F32) + _nbytes((QB, 2048), F32))
    return pl.pallas_call(
        body,
        out_shape=jax.ShapeDtypeStruct((T, MLA_H * HW), F32),
        grid=(B, NQ),
        in_specs=[
            pl.BlockSpec((QB, MLA_H * HW), lambda b, i: (b * NQ + i, 0)),
            pl.BlockSpec((Lp, MLA_H * HW), lambda b, i: (b, 0)),
            pl.BlockSpec((Lp, MLA_H * MLA_DV), lambda b, i: (b, 0)),
            pl.BlockSpec((QB, MLA_H * MLA_DV), lambda b, i: (b * NQ + i, 0)),
            pl.BlockSpec((1, MLA_H, QB, LANES), lambda b, i: (b, 0, i, 0)),
            pl.BlockSpec((1, MLA_H, QB, LANES), lambda b, i: (b, 0, i, 0)),
        ],
        out_specs=pl.BlockSpec((QB, MLA_H * HW), lambda b, i: (b * NQ + i, 0)),
        compiler_params=_params(("parallel", "parallel"), blocks),
        name="attn_bwd_dq",
    )(q_att, k_att, v_att, do, lse_r, delta_r)


def _attn_bwd_dkv(q_att, k_att, v_att, do, lse_c, delta_c, B, Lp):
    T = B * Lp
    NQ = Lp // QB
    scale = 1.0 / math.sqrt(MLA_QK)

    HW = 2 * LANES

    def body(q_ref, k_ref, v_ref, do_ref, lse_ref, dl_ref, dk_ref, dv_ref):
        kj = pl.program_id(1)
        dk_ref[...] = jnp.zeros_like(dk_ref)
        dv_ref[...] = jnp.zeros_like(dv_ref)
        col = kj * QB + lax.broadcasted_iota(jnp.int32, (QB, QB), 0)
        rowi = lax.broadcasted_iota(jnp.int32, (QB, QB), 1)

        def step(qi, carry):
            off = pl.multiple_of(qi * QB, QB)
            ok = _attn_mask(qi * QB + rowi, col)
            for h in range(MLA_H):
                ws = slice(h * HW, (h + 1) * HW)
                hs = slice(h * MLA_DV, (h + 1) * MLA_DV)
                qb = q_ref[pl.ds(off, QB), ws]
                dob = do_ref[pl.ds(off, QB), hs]
                lse = lse_ref[0, h, pl.ds(qi, 1), :]
                delta = dl_ref[0, h, pl.ds(qi, 1), :]
                s_t = _nt(k_ref[:, ws], qb) * scale
                p_t = jnp.where(ok, jnp.exp(s_t - lse), 0.0)
                dv_ref[:, hs] += _nn(p_t.astype(BF16), dob)
                ds_t = p_t * (_nt(v_ref[:, hs], dob) - delta) * scale
                dk_ref[:, ws] += _nn(ds_t.astype(BF16), qb)
            return carry

        lax.fori_loop(kj, NQ, step, 0)

    blocks = (_nbytes((Lp, 2048), BF16) + _nbytes((Lp, 1024), BF16) + _nbytes((QB, 3072), BF16)
              + 2 * _nbytes((MLA_H, NQ, QB), F32) + _nbytes((QB, 3072), F32))
    return pl.pallas_call(
        body,
        out_shape=(jax.ShapeDtypeStruct((T, MLA_H * HW), F32), jax.ShapeDtypeStruct((T, MLA_H * MLA_DV), F32)),
        grid=(B, NQ),
        in_specs=[
            pl.BlockSpec((Lp, MLA_H * HW), lambda b, j: (b, 0)),
            pl.BlockSpec((QB, MLA_H * HW), lambda b, j: (b * NQ + j, 0)),
            pl.BlockSpec((QB, MLA_H * MLA_DV), lambda b, j: (b * NQ + j, 0)),
            pl.BlockSpec((Lp, MLA_H * MLA_DV), lambda b, j: (b, 0)),
            pl.BlockSpec((1, MLA_H, NQ, QB), lambda b, j: (b, 0, 0, 0)),
            pl.BlockSpec((1, MLA_H, NQ, QB), lambda b, j: (b, 0, 0, 0)),
        ],
        out_specs=(pl.BlockSpec((QB, MLA_H * HW), lambda b, j: (b * NQ + j, 0)),
                   pl.BlockSpec((QB, MLA_H * MLA_DV), lambda b, j: (b * NQ + j, 0))),
        compiler_params=_params(("parallel", "parallel"), blocks),
        name="attn_bwd_dkv",
    )(q_att, k_att, v_att, do, lse_c, delta_c)


def _mla_bwd_post(dq, dk, dv, projB, cos_t, sin_t, gq, gkv, wuq2, wukv, B, Lp, tr):
    T = B * Lp
    nt = Lp // tr
    HW = 2 * LANES

    def body(dq_ref, dk_ref, dv_ref, pb_ref, cos_ref, sin_ref, gq_ref, gkv_ref, wuq_ref, wukv_ref,
             dqf_ref, dkvf_ref, de_ref, dgq_ref, dgkv_ref):
        first = (pl.program_id(0) == 0) & (pl.program_id(1) == 0)

        @pl.when(first)
        def _():
            dgq_ref[...] = jnp.zeros_like(dgq_ref)
            dgkv_ref[...] = jnp.zeros_like(dgkv_ref)

        cs = cos_ref[...]
        sn = sin_ref[...]
        rope_t = lambda t: t * cs + _swap_halves(t * sn)
        dkr = jnp.zeros((tr, LANES), F32)
        for h in range(MLA_H):
            dqf_ref[:, h * HW:h * HW + LANES] = dq_ref[:, h * HW:h * HW + LANES].astype(BF16)
            dqf_ref[:, h * HW + LANES:(h + 1) * HW] = rope_t(dq_ref[:, h * HW + LANES:(h + 1) * HW]).astype(BF16)
            dkvf_ref[:, h * HW:h * HW + LANES] = dk_ref[:, h * HW:h * HW + LANES].astype(BF16)
            dkvf_ref[:, h * HW + LANES:(h + 1) * HW] = dv_ref[:, h * MLA_DV:(h + 1) * MLA_DV].astype(BF16)
            dkr = dkr + dk_ref[:, h * HW + LANES:(h + 1) * HW]

        def norm_bwd(x, dn, g):
            r = lax.rsqrt(jnp.mean(x * x, axis=-1, keepdims=True) + EPS)
            xn = x * r
            t = dn * g
            return r * (t - xn * jnp.mean(t * xn, axis=-1, keepdims=True)), jnp.sum(dn * xn, axis=0, keepdims=True)

        dcq, dgq = norm_bwd(pb_ref[:, 0:Q_RANK], _nt(dqf_ref[...], wuq_ref[...]), gq_ref[...])
        dckv, dgkv = norm_bwd(pb_ref[:, Q_RANK:Q_RANK + KV_RANK], _nt(dkvf_ref[...], wukv_ref[...]), gkv_ref[...])
        dgq_ref[...] += dgq
        dgkv_ref[...] += dgkv
        de_ref[:, 0:Q_RANK] = dcq.astype(BF16)
        de_ref[:, Q_RANK:Q_RANK + KV_RANK] = dckv.astype(BF16)
        de_ref[:, 384:512] = rope_t(dkr).astype(BF16)

    rows = lambda w: pl.BlockSpec((tr, w), lambda b, j: (b * nt + j, 0))
    const = lambda s: pl.BlockSpec(s, lambda b, j: (0, 0))
    blocks = (2 * _nbytes((tr, 2048), F32) + _nbytes((tr, 1024), F32) + _nbytes((tr, 640), F32)
              + 2 * _nbytes((tr, 2048), BF16) + _nbytes((2048, 384), BF16) + 2 * _nbytes((tr, 2048), F32))
    return pl.pallas_call(
        body,
        out_shape=(jax.ShapeDtypeStruct((T, 2048), BF16), jax.ShapeDtypeStruct((T, 2048), BF16),
                   jax.ShapeDtypeStruct((T, 512), BF16), jax.ShapeDtypeStruct((1, Q_RANK), F32),
                   jax.ShapeDtypeStruct((1, KV_RANK), F32)),
        grid=(B, nt),
        in_specs=[rows(2048), rows(2048), rows(1024), rows(640),
                  pl.BlockSpec((tr, 128), lambda b, j: (j, 0)), pl.BlockSpec((tr, 128), lambda b, j: (j, 0)),
                  const((1, Q_RANK)), const((1, KV_RANK)), const((Q_RANK, 2048)), const((KV_RANK, 2048))],
        out_specs=(rows(2048), rows(2048), rows(512), const((1, Q_RANK)), const((1, KV_RANK))),
        compiler_params=_params(("arbitrary", "arbitrary"), blocks),
        name="mla_bwd_post",
    )(dq, dk, dv, projB, cos_t, sin_t, gq, gkv, wuq2, wukv)


def _du_matmul(dA, dBz, dC, dDz, dE, wA, wB, tm):
    T = dA.shape[0]

    def body(da_ref, db_ref, dc_ref, dd_ref, de_ref, wa_ref, wb_ref, o_ref):
        acc = _nt(da_ref[...], wa_ref[:, 3072:5120])
        acc = acc + _nt(db_ref[...], wa_ref[:, 1024:2048])
        acc = acc + _nt(dd_ref[...], wa_ref[:, 2048:3072])
        acc = acc + _nt(dc_ref[:, 0:1024], wa_ref[:, 0:1024])
        acc = acc + _nt(dc_ref[:, 1024:2048], wa_ref[:, 5120:6144])
        acc = acc + _nt(dc_ref[:, 2048:2176], wb_ref[:, 384:512])
        acc = acc + _nt(de_ref[:, 0:384], wb_ref[:, 0:384])
        acc = acc + _nt(de_ref[:, 384:512], wb_ref[:, 512:640])
        o_ref[...] = acc

    widths = [a.shape[1] for a in (dA, dBz, dC, dDz, dE)]
    blocks = (sum(_nbytes((tm, w), BF16) for w in widths) + _nbytes(wA.shape, BF16) + _nbytes(wB.shape, BF16)
              + _nbytes((tm, D), F32))
    return pl.pallas_call(
        body,
        out_shape=jax.ShapeDtypeStruct((T, D), F32),
        grid=(T // tm,),
        in_specs=[pl.BlockSpec((tm, w), lambda i: (i, 0)) for w in widths]
        + [pl.BlockSpec(wA.shape, lambda i: (0, 0)), pl.BlockSpec(wB.shape, lambda i: (0, 0))],
        out_specs=pl.BlockSpec((tm, D), lambda i: (i, 0)),
        compiler_params=_params(("parallel",), blocks),
        name="du_matmul",
    )(dA, dBz, dC, dDz, dE, wA, wB)


def _in_norm_bwd(x, meta, dh1, du, g, B, Lp):
    NQ = Lp // QB
    seq = x.shape[1]

    def body(x_ref, meta_ref, dh_ref, du_ref, g_ref, gx_ref, dmeta_ref, dg_ref):
        b = pl.program_id(0)
        j = pl.program_id(1)

        @pl.when((b == 0) & (j == 0))
        def _():
            dg_ref[...] = jnp.zeros_like(dg_ref)

        x = _h_tile(j, x_ref, meta_ref)
        r = lax.rsqrt(jnp.mean(x * x, axis=-1, keepdims=True) + EPS)
        xn = x * r
        du = du_ref[...]
        t = du * g_ref[...]
        dh0 = dh_ref[...] + r * (t - xn * jnp.mean(t * xn, axis=-1, keepdims=True))
        dg_ref[...] += jnp.sum(du * xn, axis=0, keepdims=True)
        gx_ref[0] = dh0

        @pl.when((j == 0) & (b == 0))
        def _():
            dmeta_ref[...] = dh0[FRONT:HEAD_ROWS, :]

        @pl.when((j == 0) & (b > 0))
        def _():
            dmeta_ref[...] += dh0[FRONT:HEAD_ROWS, :]

    rows = pl.BlockSpec((QB, D), lambda b, j: (b * NQ + j, 0))
    return pl.pallas_call(
        body,
        out_shape=(jax.ShapeDtypeStruct((B, seq, D), F32), jax.ShapeDtypeStruct((N_META, D), F32),
                   jax.ShapeDtypeStruct((1, D), F32)),
        grid=(B, NQ),
        in_specs=[_x_spec(), pl.BlockSpec((N_META, D), lambda b, j: (0, 0)), rows, rows,
                  pl.BlockSpec((1, D), lambda b, j: (0, 0))],
        out_specs=(_x_spec(), pl.BlockSpec((N_META, D), lambda b, j: (0, 0)), pl.BlockSpec((1, D), lambda b, j: (0, 0))),
        compiler_params=_params(("arbitrary", "arbitrary"), 5 * _nbytes((QB, D), F32)),
        name="in_norm_bwd",
    )(x, meta, dh1, du, g)


_VMEM_WHOLE = pl.BlockSpec(memory_space=pltpu.VMEM)


def _params_whole(arrays):
    total = sum(_nbytes(a.shape, a.dtype) for a in arrays)
    return pltpu.CompilerParams(vmem_limit_bytes=int(min(total + 12 * 1024 * 1024, VMEM_CAP_V7X)))


def _pair_add_big(gp, recv, c):
    _, half, cols = recv.shape
    th = _div_tile(half, 64, 8)

    def body(c_ref, a_ref, b_ref, o_ref):
        o_ref[...] = a_ref[:, 0] + b_ref[...]

    return pl.pallas_call(
        body,
        out_shape=jax.ShapeDtypeStruct(recv.shape, F32),
        grid_spec=pltpu.PrefetchScalarGridSpec(
            num_scalar_prefetch=1,
            grid=(half // th,),
            in_specs=[pl.BlockSpec((4, 1, th, cols), lambda i, c_ref: (0, c_ref[0], i, 0)),
                      pl.BlockSpec((4, th, cols), lambda i, c_ref: (0, i, 0))],
            out_specs=pl.BlockSpec((4, th, cols), lambda i, c_ref: (0, i, 0)),
        ),
        compiler_params=_params(("parallel",), 3 * _nbytes((4, th, cols), F32)),
        name="grad_pair_add_big",
    )(c, gp.reshape(4, 2, half, cols), recv)


def _pair_add_small(gps, recvs):
    n = len(gps)

    def body(*refs):
        c = lax.axis_index("c")
        for t in range(n):
            g_ref, r_ref, o_ref = refs[t], refs[n + t], refs[2 * n + t]
            half = r_ref.shape[1]
            o_ref[...] = g_ref[:, pl.ds(pl.multiple_of(c * half, 8), half), :] + r_ref[...]

    return pl.pallas_call(
        body,
        out_shape=[jax.ShapeDtypeStruct(r.shape, F32) for r in recvs],
        in_specs=[_VMEM_WHOLE] * (2 * n),
        out_specs=[_VMEM_WHOLE] * n,
        compiler_params=_params_whole(list(gps) + 2 * list(recvs)),
        name="grad_pair_add_small",
    )(*gps, *recvs)


def _chip_order_sum(p_ref):
    return ((p_ref[0] + p_ref[1]) + p_ref[2]) + p_ref[3]


def _sum_chips_big(parts):
    _, half, cols = parts.shape
    th = _div_tile(half, 64, 8)

    def body(p_ref, o_ref):
        o_ref[...] = _chip_order_sum(p_ref)

    return pl.pallas_call(
        body,
        out_shape=jax.ShapeDtypeStruct((half, cols), F32),
        grid=(half // th,),
        in_specs=[pl.BlockSpec((4, th, cols), lambda i: (0, i, 0))],
        out_specs=pl.BlockSpec((th, cols), lambda i: (i, 0)),
        compiler_params=_params(("parallel",), 5 * _nbytes((th, cols), F32)),
        name="grad_sum_chips_big",
    )(parts)


def _sum_chips_small(parts):
    n = len(parts)

    def body(*refs):
        for t in range(n):
            refs[n + t][...] = _chip_order_sum(refs[t])

    return pl.pallas_call(
        body,
        out_shape=[jax.ShapeDtypeStruct(p.shape[1:], F32) for p in parts],
        in_specs=[_VMEM_WHOLE] * n,
        out_specs=[_VMEM_WHOLE] * n,
        compiler_params=_params_whole(list(parts) + list(parts)),
        name="grad_sum_chips_small",
    )(*parts)


def _adamw_update(w_ref, g_ref, m_ref, v_ref, d_ref, mo_ref, vo_ref):
    c1 = 1.0 - ADAM_B1 ** ADAM_STEP
    c2 = 1.0 - ADAM_B2 ** ADAM_STEP
    gv = g_ref[...]
    mn = ADAM_B1 * m_ref[...] + (1.0 - ADAM_B1) * gv
    vn = ADAM_B2 * v_ref[...] + (1.0 - ADAM_B2) * (gv * gv)
    mo_ref[...] = mn
    vo_ref[...] = vn
    d_ref[...] = -ADAM_LR * ((mn / c1) / (jnp.sqrt(vn / c2) + ADAM_EPS) + ADAM_WD * w_ref[...])


def _adamw_big(w, g, m, v):
    rows, cols = w.shape
    tr = _div_tile(rows, 128, 8)
    spec = pl.BlockSpec((tr, cols), lambda i: (i, 0))
    shp = jax.ShapeDtypeStruct((rows, cols), F32)
    return pl.pallas_call(
        functools.partial(_adamw_update),
        out_shape=(shp, shp, shp),
        grid=(rows // tr,),
        in_specs=[spec] * 4,
        out_specs=(spec, spec, spec),
        compiler_params=_params(("parallel",), 7 * _nbytes((tr, cols), F32)),
        name="adamw_big",
    )(w, g, m, v)


def _adamw_small(ws, gs, ms, vs):
    n = len(ws)

    def body(*refs):
        for t in range(n):
            _adamw_update(refs[t], refs[n + t], refs[2 * n + t], refs[3 * n + t],
                          refs[4 * n + t], refs[5 * n + t], refs[6 * n + t])

    shapes = [jax.ShapeDtypeStruct(w.shape, F32) for w in ws]
    return pl.pallas_call(
        body,
        out_shape=shapes * 3,
        in_specs=[_VMEM_WHOLE] * (4 * n),
        out_specs=[_VMEM_WHOLE] * (3 * n),
        compiler_params=_params_whole(list(ws) * 7),
        name="adamw_small",
    )(*ws, *gs, *ms, *vs)


def _mesh_pos():
    return lax.axis_index("x"), lax.axis_index("y"), lax.axis_index("c")


def _other_chips(x, y):
    return [(1 - x, y), (x, 1 - y), (1 - x, 1 - y)]


_ANY = pl.BlockSpec(memory_space=pl.ANY)


PAIR_SPLIT_MIN_ROWS = 64


def _weight_gather(shards):
    n = len(shards)
    split = [s.shape[0] >= PAIR_SPLIT_MIN_ROWS for s in shards]

    def body(*refs):
        w_refs, o_refs = refs[:n], refs[n:2 * n]
        send_sems, recv_sems, local_sems = refs[2 * n:]
        x, y, c = _mesh_pos()
        me = 2 * x + y
        chips = _other_chips(x, y)

        def rows_of(t, core):
            rows = shards[t].shape[0]
            if not split[t]:
                return pl.ds(0, rows)
            return pl.ds(pl.multiple_of(core * (rows // 2), 16), rows // 2)

        def landed(t, k, slot, rows, to):
            ref = o_refs[t].at[slot, rows]
            return pltpu.make_async_remote_copy(src_ref=ref, dst_ref=ref, send_sem=send_sems.at[6 * t + k],
                                                recv_sem=recv_sems.at[6 * t + k], device_id=to, device_id_type=MESH)

        local, sends = [], []
        for t in range(n):
            own = pltpu.make_async_copy(w_refs[t], o_refs[t].at[me], local_sems.at[t])
            own.start()
            local.append(own)
            mine = rows_of(t, c)
            for k, (px, py) in enumerate(chips):
                cp = pltpu.make_async_remote_copy(src_ref=w_refs[t].at[mine], dst_ref=o_refs[t].at[me, mine],
                                                  send_sem=send_sems.at[6 * t + k], recv_sem=recv_sems.at[6 * t + k],
                                                  device_id=(px, py, c), device_id_type=MESH)
                cp.start()
                sends.append(cp)
        for t in range(n):
            mine = rows_of(t, c)
            for k, (px, py) in enumerate(chips):
                landed(t, k, 2 * px + py, mine, (x, y, c)).wait_recv()
                if split[t]:
                    cp = landed(t, 3 + k, 2 * px + py, mine, (x, y, 1 - c))
                    cp.start()
                    sends.append(cp)
        for t in range(n):
            if split[t]:
                for k, (px, py) in enumerate(chips):
                    landed(t, 3 + k, 2 * px + py, rows_of(t, 1 - c), (x, y, c)).wait_recv()
        for cp in sends:
            cp.wait_send()
        for own in local:
            own.wait()

    return pl.pallas_call(
        body,
        out_shape=[jax.ShapeDtypeStruct((4,) + s.shape, s.dtype) for s in shards],
        in_specs=[_ANY] * n,
        out_specs=[_ANY] * n,
        scratch_shapes=[pltpu.SemaphoreType.DMA((6 * n,)), pltpu.SemaphoreType.DMA((6 * n,)),
                        pltpu.SemaphoreType.DMA((n,))],
        name="weight_gather",
    )(*shards)


def _pair_swap(gps):
    n = len(gps)

    def body(*refs):
        g_refs, o_refs = refs[:n], refs[n:2 * n]
        send_sems, recv_sems = refs[2 * n:]
        x, y, c = _mesh_pos()
        copies = []
        for t in range(n):
            half = gps[t].shape[1] // 2
            theirs = pl.ds(pl.multiple_of((1 - c) * half, 8), half)
            cp = pltpu.make_async_remote_copy(src_ref=g_refs[t].at[:, theirs], dst_ref=o_refs[t],
                                              send_sem=send_sems.at[t], recv_sem=recv_sems.at[t],
                                              device_id=(x, y, 1 - c), device_id_type=MESH)
            cp.start()
            copies.append(cp)
        for cp in copies:
            cp.wait_send()
            cp.wait_recv()

    return pl.pallas_call(
        body,
        out_shape=[jax.ShapeDtypeStruct((4, g.shape[1] // 2, g.shape[2]), g.dtype) for g in gps],
        in_specs=[_ANY] * n,
        out_specs=[_ANY] * n,
        scratch_shapes=[pltpu.SemaphoreType.DMA((n,)), pltpu.SemaphoreType.DMA((n,))],
        name="grad_pair_swap",
    )(*gps)


def _chip_scatter(parts):
    n = len(parts)

    def body(*refs):
        s_refs, o_refs = refs[:n], refs[n:2 * n]
        send_sems, recv_sems, local_sems = refs[2 * n:]
        x, y, c = _mesh_pos()
        me = 2 * x + y
        chips = _other_chips(x, y)
        local, sends = [], []
        for t in range(n):
            own = pltpu.make_async_copy(s_refs[t].at[me], o_refs[t].at[me], local_sems.at[t])
            own.start()
            local.append(own)
            for k, (px, py) in enumerate(chips):
                cp = pltpu.make_async_remote_copy(src_ref=s_refs[t].at[2 * px + py], dst_ref=o_refs[t].at[me],
                                                  send_sem=send_sems.at[3 * t + k], recv_sem=recv_sems.at[3 * t + k],
                                                  device_id=(px, py, c), device_id_type=MESH)
                cp.start()
                sends.append(cp)
        for t in range(n):
            for k, (px, py) in enumerate(chips):
                pltpu.make_async_remote_copy(src_ref=s_refs[t].at[me], dst_ref=o_refs[t].at[2 * px + py],
                                             send_sem=send_sems.at[3 * t + k], recv_sem=recv_sems.at[3 * t + k],
                                             device_id=(x, y, c), device_id_type=MESH).wait_recv()
        for cp in sends:
            cp.wait_send()
        for own in local:
            own.wait()

    return pl.pallas_call(
        body,
        out_shape=[jax.ShapeDtypeStruct(p.shape, p.dtype) for p in parts],
        in_specs=[_ANY] * n,
        out_specs=[_ANY] * n,
        scratch_shapes=[pltpu.SemaphoreType.DMA((3 * n,)), pltpu.SemaphoreType.DMA((3 * n,)),
                        pltpu.SemaphoreType.DMA((n,))],
        name="grad_chip_scatter",
    )(*parts)


def _pair_join(fs):
    n = len(fs)

    def body(*refs):
        f_refs, o_refs = refs[:n], refs[n:2 * n]
        send_sems, recv_sems, local_sems = refs[2 * n:]
        x, y, c = _mesh_pos()
        local, sends = [], []
        for t in range(n):
            own = pltpu.make_async_copy(f_refs[t], o_refs[t].at[c], local_sems.at[t])
            own.start()
            local.append(own)
            cp = pltpu.make_async_remote_copy(src_ref=f_refs[t], dst_ref=o_refs[t].at[c], send_sem=send_sems.at[t],
                                              recv_sem=recv_sems.at[t], device_id=(x, y, 1 - c), device_id_type=MESH)
            cp.start()
            sends.append(cp)
        for t in range(n):
            pltpu.make_async_remote_copy(src_ref=f_refs[t], dst_ref=o_refs[t].at[1 - c], send_sem=send_sems.at[t],
                                         recv_sem=recv_sems.at[t], device_id=(x, y, c), device_id_type=MESH).wait_recv()
        for cp in sends:
            cp.wait_send()
        for own in local:
            own.wait()

    return pl.pallas_call(
        body,
        out_shape=[jax.ShapeDtypeStruct((2,) + f.shape, f.dtype) for f in fs],
        in_specs=[_ANY] * n,
        out_specs=[_ANY] * n,
        scratch_shapes=[pltpu.SemaphoreType.DMA((n,)), pltpu.SemaphoreType.DMA((n,)), pltpu.SemaphoreType.DMA((n,))],
        name="grad_pair_join",
    )(*fs)


def _rope_tables(Lp):
    inv = 1.0 / (ROPE_BASE ** (jnp.arange(0, ROPE, 2, dtype=F32) / ROPE))
    ang = (jnp.arange(Lp, dtype=F32) - FRONT)[:, None] * inv[None, :]
    cs, sn = jnp.cos(ang), jnp.sin(ang)
    return jnp.tile(cs, (1, 4)), jnp.concatenate([-sn, sn, -sn, sn], axis=1)


def _local_step(x, loss_target, meta, norm_g, w_in, gate_w, gate_b, gla_norm_g, gla_proj, q_norm_g, w_uq,
                kv_norm_g, w_ukv, mla_proj, w_out, final_norm_g):
    B, seq, _ = x.shape
    Lp = HEAD_ROWS + seq
    T = B * Lp
    tr = _div_tile(Lp, 544, 16)
    tq = _div_tile(T, 1024, QB)

    cuts = np.cumsum((0,) + SPLITS)
    col = lambda i: w_in[:, cuts[i]:cuts[i + 1]]
    w_q, w_k, w_v, w_lr, w_z, w_cq, w_ckv, w_kr, w_mz, w_gg, w_gm = [col(i) for i in range(11)]
    pad_cols = lambda w, n: jnp.pad(w, ((0, 0), (0, n - w.shape[1])))
    wA = jnp.concatenate([w_v, w_z, w_mz, w_gg, w_gm, w_q, w_k], axis=1)
    wB = jnp.concatenate([w_cq, w_ckv, pad_cols(w_lr, 128), pad_cols(w_kr, 128)], axis=1)
    wg = jnp.pad(gate_w, ((0, 128 - GLA_RANK), (0, 0)))
    wuq2 = jnp.pad(w_uq.reshape(Q_RANK, MLA_H, MLA_QK), ((0, 0), (0, 0), (0, 256 - MLA_QK))).reshape(Q_RANK, 2048)
    gn4 = jnp.tile(gla_norm_g, (1, GLA_H))
    cos_t, sin_t = _rope_tables(Lp)

    u = _rms_in(x, meta, norm_g, B, Lp)
    projA = _mm(u, wA, name="in_proj_a", tm=tq, tn=1024, tk=D)
    projB = _mm(u, wB, name="in_proj_b", tm=tq, tn=640, tk=D)
    oa, ya_in, ssave = _gla_fwd(projA, projB, wg, gate_b, gn4, B, Lp)
    ya = _mm(ya_in, gla_proj, name="gla_proj", tm=tq, tn=512, tk=D)
    q_att, k_att, v_att, cqn, ckvn = _mla_prep(projB, cos_t, sin_t, q_norm_g, kv_norm_g, wuq2, w_ukv, B, Lp, tr)
    ob, yb_in, lse_r, lse_c = _attn_fwd(q_att, k_att, v_att, projA, B, Lp)
    yb = _mm(yb_in, mla_proj, name="mla_proj", tm=tq, tn=512, tk=D)
    merged = _merge_fwd(projA, ya, yb, tr)
    mo = _mm(merged, w_out, name="w_out", tm=tq, tn=512, tk=D)
    dh1, dh1_b, loss, d_gf = _final_loss(x, meta, mo, final_norm_g.reshape(1, D), loss_target, B, Lp)

    dmerged = _mm(dh1_b, w_out, name="d_merged", trans_b=True, tm=tq, tn=512, tk=D)
    g_w_out = _mm(merged, dh1_b, name="dw_out", trans_a=True, tm=D, tn=512, tk=tq)
    dya, dyb, dA = _merge_bwd(dmerged, projA, ya, yb, tr)
    dya_in = _mm(dya, gla_proj, name="d_ya_in", trans_b=True, tm=tq, tn=512, tk=D)
    g_gla_proj = _mm(ya_in, dya, name="dw_gla_proj", trans_a=True, tm=D, tn=512, tk=tq)
    dyb_in = _mm(dyb, mla_proj, name="d_yb_in", trans_b=True, tm=tq, tn=512, tk=D)
    g_mla_proj = _mm(yb_in, dyb, name="dw_mla_proj", trans_a=True, tm=D, tn=512, tk=tq)
    doa, dBz, d_gn = _gla_out_bwd(dya_in, oa, projA, gn4, tr)
    dC, g_wg, d_bg = _gla_bwd(projA, projB, ssave, doa, wg, gate_b, B, Lp)
    do, dDz, delta_r, delta_c = _attn_bwd_pre(dyb_in, projA, ob, B, Lp)
    dq = _attn_bwd_dq(q_att, k_att, v_att, do, lse_r, delta_r, B, Lp)
    dk, dv = _attn_bwd_dkv(q_att, k_att, v_att, do, lse_c, delta_c, B, Lp)
    dqf, dkvf, dE, d_gq, d_gkv = _mla_bwd_post(dq, dk, dv, projB, cos_t, sin_t, q_norm_g, kv_norm_g,
                                                wuq2, w_ukv, B, Lp, tr)
    g_wuq2 = _mm(cqn, dqf, name="dw_uq", trans_a=True, tm=Q_RANK, tn=512, tk=tq)
    g_wukv = _mm(ckvn, dkvf, name="dw_ukv", trans_a=True, tm=KV_RANK, tn=512, tk=tq)
    dparts = [dA, dBz, dC, dDz, dE]
    g_in = [_mm(u, dp, name="dw_in_%d" % i, trans_a=True, tm=D, tn=_div_tile(dp.shape[1], 1024, 256), tk=tq)
            for i, dp in enumerate(dparts)]
    du = _du_matmul(dA, dBz, dC, dDz, dE, wA, wB, QB)
    grad_x, d_meta, d_ng = _in_norm_bwd(x, meta, dh1, du, norm_g, B, Lp)

    gA, gBz, gC, gDz, gE = g_in
    g_w_in = jnp.concatenate([
        gC[:, 1024:1536], gC[:, 1536:2048], gC[:, 0:1024], gC[:, 2048:2048 + GLA_RANK], gBz,
        gE[:, 0:Q_RANK], gE[:, Q_RANK:Q_RANK + KV_RANK], gE[:, 384:384 + ROPE], gDz, gA[:, 0:D], gA[:, D:2 * D]], axis=1)
    g_wuq = g_wuq2.reshape(Q_RANK, MLA_H, 256)[:, :, :MLA_QK].reshape(Q_RANK, MLA_H * MLA_QK)
    grads = dict(w_in=g_w_in, gla_gate_w=g_wg[:GLA_RANK], gla_proj=g_gla_proj, mla_w_uq=g_wuq, mla_w_ukv=g_wukv,
                 mla_proj=g_mla_proj, w_out=g_w_out, meta_tokens=d_meta, norm_g=d_ng, gla_gate_b=d_bg,
                 gla_norm_g=d_gn, mla_q_norm_g=d_gq, mla_kv_norm_g=d_gkv, final_norm_g=d_gf)
    return loss[0, 0], grad_x, grads


_MATS = ("w_in", "gla_gate_w", "gla_proj", "mla_w_uq", "mla_w_ukv", "mla_proj", "w_out")
_ROW_SHARDED = ("gla_proj", "mla_proj", "w_out")
_ORDER = ("meta_tokens", "norm_g", "w_in", "gla_gate_w", "gla_gate_b", "gla_norm_g", "gla_proj", "mla_q_norm_g",
          "mla_w_uq", "mla_kv_norm_g", "mla_w_ukv", "mla_proj", "w_out", "final_norm_g")
SMALL_PACK_ROWS = 16


def _pack_small(d):
    rows = [jnp.pad(d[n].reshape(1, size), ((0, 0), (0, D - size))) for n, size in SMALL]
    return jnp.pad(jnp.concatenate(rows, axis=0), ((0, SMALL_PACK_ROWS - len(rows)), (0, 0)))


def _unpack_small(packed):
    return {n: packed[i, :size] for i, (n, size) in enumerate(SMALL)}


def kernel(x, meta_tokens, norm_g, w_in, gla_gate_w, gla_gate_b, gla_norm_g, gla_proj, mla_q_norm_g, mla_w_uq, mla_kv_norm_g, mla_w_ukv, mla_proj, w_out, final_norm_g, loss_target, m_meta_tokens, m_norm_g, m_w_in, m_gla_gate_w, m_gla_gate_b, m_gla_norm_g, m_gla_proj, m_mla_q_norm_g, m_mla_w_uq, m_mla_kv_norm_g, m_mla_w_ukv, m_mla_proj, m_w_out, m_final_norm_g, v_meta_tokens, v_norm_g, v_w_in, v_gla_gate_w, v_gla_gate_b, v_gla_norm_g, v_gla_proj, v_mla_q_norm_g, v_mla_w_uq, v_mla_kv_norm_g, v_mla_w_ukv, v_mla_proj, v_w_out, v_final_norm_g):
    w = dict(meta_tokens=meta_tokens, norm_g=norm_g, w_in=w_in[0], gla_gate_w=gla_gate_w[0], gla_gate_b=gla_gate_b,
             gla_norm_g=gla_norm_g, gla_proj=gla_proj[0], mla_q_norm_g=mla_q_norm_g, mla_w_uq=mla_w_uq[0],
             mla_kv_norm_g=mla_kv_norm_g, mla_w_ukv=mla_w_ukv[0], mla_proj=mla_proj[0], w_out=w_out[0],
             final_norm_g=final_norm_g)
    mom = dict(meta_tokens=m_meta_tokens, norm_g=m_norm_g, w_in=m_w_in[0], gla_gate_w=m_gla_gate_w[0],
               gla_gate_b=m_gla_gate_b, gla_norm_g=m_gla_norm_g, gla_proj=m_gla_proj[0], mla_q_norm_g=m_mla_q_norm_g,
               mla_w_uq=m_mla_w_uq[0], mla_kv_norm_g=m_mla_kv_norm_g, mla_w_ukv=m_mla_w_ukv[0], mla_proj=m_mla_proj[0],
               w_out=m_w_out[0], final_norm_g=m_final_norm_g)
    var = dict(meta_tokens=v_meta_tokens, norm_g=v_norm_g, w_in=v_w_in[0], gla_gate_w=v_gla_gate_w[0],
               gla_gate_b=v_gla_gate_b, gla_norm_g=v_gla_norm_g, gla_proj=v_gla_proj[0], mla_q_norm_g=v_mla_q_norm_g,
               mla_w_uq=v_mla_w_uq[0], mla_kv_norm_g=v_mla_kv_norm_g, mla_w_ukv=v_mla_w_ukv[0], mla_proj=v_mla_proj[0],
               w_out=v_w_out[0], final_norm_g=v_final_norm_g)
    out_shapes = {n: a.shape for n, a in zip(_ORDER, (meta_tokens, norm_g, w_in, gla_gate_w, gla_gate_b, gla_norm_g,
                                                     gla_proj, mla_q_norm_g, mla_w_uq, mla_kv_norm_g, mla_w_ukv,
                                                     mla_proj, w_out, final_norm_g))}

    gathered = _weight_gather([w[n].astype(BF16) for n in _MATS] + [meta_tokens])
    full = {}
    for name, gth in zip(_MATS, gathered):
        if name in _ROW_SHARDED:
            full[name] = gth.reshape(4 * gth.shape[1], gth.shape[2])
        else:
            full[name] = gth.transpose(1, 0, 2).reshape(gth.shape[1], 4 * gth.shape[2])
    meta_full = gathered[-1].transpose(1, 0, 2).reshape(N_META, D)

    loss_local, grad_x, g = _local_step(
        x, loss_target, meta_full, norm_g, full["w_in"], full["gla_gate_w"], gla_gate_b, gla_norm_g, full["gla_proj"],
        mla_q_norm_g, full["mla_w_uq"], mla_kv_norm_g, full["mla_w_ukv"], full["mla_proj"], full["w_out"], final_norm_g)
    loss = lax.psum(loss_local, ("x", "y", "c"))

    def by_owner(name, arr):
        if name in _ROW_SHARDED:
            return arr.reshape(4, arr.shape[0] // 4, arr.shape[1])
        return arr.reshape(arr.shape[0], 4, arr.shape[1] // 4).transpose(1, 0, 2)

    names = _MATS + ("meta_tokens",)
    gps = [by_owner(n, g[n]) for n in names] + [jnp.broadcast_to(_pack_small(g)[None], (4, SMALL_PACK_ROWS, D))]
    recvs = _pair_swap(gps)
    c_idx = lax.axis_index("c").astype(jnp.int32).reshape(1)
    s1 = [_pair_add_big(gps[0], recvs[0], c_idx)] + list(_pair_add_small(gps[1:], recvs[1:]))
    landed = _chip_scatter(s1)
    halves = [_sum_chips_big(landed[0])] + list(_sum_chips_small(landed[1:]))
    g_red = [j.reshape(2 * j.shape[1], j.shape[2]) for j in _pair_join(halves)]

    tens = lambda d: [d[n].reshape(g_red[i].shape) for i, n in enumerate(names)] + [_pack_small(d)]
    w_t, m_t, v_t = tens(w), tens(mom), tens(var)
    big = _adamw_big(w_t[0], g_red[0], m_t[0], v_t[0])
    rest = _adamw_small(w_t[1:], g_red[1:], m_t[1:], v_t[1:])
    k = len(names)
    results = {"grad": g_red}
    for i, kind in enumerate(("delta", "new_m", "new_v")):
        results[kind] = [big[i]] + list(rest[i * k:(i + 1) * k])

    outs = []
    for kind in ("grad", "delta", "new_m", "new_v"):
        vals = dict(zip(names, results[kind][:-1]))
        vals.update(_unpack_small(results[kind][-1]))
        outs += [vals[n].reshape(out_shapes[n]) for n in _ORDER]
    return (loss, grad_x, *outs)
```

```python
import functools
import math

import jax
import jax.numpy as jnp
import numpy as np
from jax import lax
from jax.experimental import pallas as pl
from jax.experimental.pallas import tpu as pltpu

F32 = jnp.float32
BF16 = jnp.bfloat16

D = 1024
N_META = 16
QB = 256
FRONT = QB - N_META
HEAD_ROWS = FRONT + N_META
assert FRONT % 64 == 48
EPS = 1e-6

GLA_H, GLA_DK, GLA_DV, GLA_RANK, GLA_C = 4, 128, 256, 16, 64
GLA_NORMALIZER = 16.0
GLA_KW, GLA_VW = GLA_H * GLA_DK, GLA_H * GLA_DV
MLA_H, NOPE, ROPE, MLA_DV, Q_RANK, KV_RANK = 8, 128, 64, 128, 256, 128
MLA_QK = NOPE + ROPE
ROPE_BASE = 10000.0
SPLITS = (GLA_KW, GLA_KW, GLA_VW, GLA_RANK, GLA_VW, Q_RANK, KV_RANK, ROPE, MLA_H * MLA_DV, D, D)
IN_WIDTH = sum(SPLITS)

ADAM_LR, ADAM_B1, ADAM_B2, ADAM_EPS, ADAM_WD, ADAM_STEP = 0.001, 0.9, 0.999, 1e-08, 0.01, 10

LANES = 128
VMEM_CAP_V7X = 56 * 1024 * 1024
MESH = pl.DeviceIdType.MESH
NEG = -1e30

SMALL = (("norm_g", D), ("gla_gate_b", GLA_KW), ("gla_norm_g", GLA_DV), ("mla_q_norm_g", Q_RANK),
         ("mla_kv_norm_g", KV_RANK), ("final_norm_g", D))


def _div_tile(n, target, mult):
    best = None
    for d in range(mult, min(n, target) + 1, mult):
        if n % d == 0:
            best = d
    assert best is not None, (n, target, mult)
    return best


def _params(sem, block_bytes, scratch_bytes=0):
    est = 2 * block_bytes + scratch_bytes + 12 * 1024 * 1024
    return pltpu.CompilerParams(dimension_semantics=sem, vmem_limit_bytes=int(min(max(est, 24 * 1024 * 1024), VMEM_CAP_V7X)))


def _nbytes(shape, dtype):
    return int(np.prod(shape)) * jnp.dtype(dtype).itemsize


def _sigmoid(x):
    return 1.0 / (1.0 + jnp.exp(-x))


def _nt(a, b):
    return lax.dot_general(a, b, (((1,), (1,)), ((), ())), preferred_element_type=F32)


def _tn(a, b):
    return lax.dot_general(a, b, (((0,), (0,)), ((), ())), preferred_element_type=F32)


def _nn(a, b):
    return jnp.dot(a, b, preferred_element_type=F32)


def _split3(x):
    a = x.astype(BF16)
    r = x - a.astype(F32)
    b = r.astype(BF16)
    c = (r - b.astype(F32)).astype(BF16)
    return a, b, c


def _mm(a, b, *, name, trans_a=False, trans_b=False, out_dtype=F32, tm, tn, tk):
    assert not (trans_a and trans_b)
    if trans_a:
        K, M = a.shape
    else:
        M, K = a.shape
    N = b.shape[0] if trans_b else b.shape[1]
    assert (b.shape[1] if trans_b else b.shape[0]) == K
    assert M % tm == 0 and N % tn == 0 and K % tk == 0, (name, M, N, K, tm, tn, tk)
    nk = K // tk

    def body(a_ref, b_ref, o_ref, *scratch):
        av = a_ref[...].astype(BF16)
        bv = b_ref[...].astype(BF16)
        prod = _tn(av, bv) if trans_a else (_nt(av, bv) if trans_b else _nn(av, bv))
        if nk == 1:
            o_ref[...] = prod.astype(out_dtype)
        else:
            acc = scratch[0]
            k = pl.program_id(2)

            @pl.when(k == 0)
            def _():
                acc[...] = prod

            @pl.when(k > 0)
            def _():
                acc[...] += prod

            @pl.when(k == nk - 1)
            def _():
                o_ref[...] = acc[...].astype(out_dtype)

    if trans_a:
        a_spec = pl.BlockSpec((tk, tm), lambda i, j, k: (k, i))
    else:
        a_spec = pl.BlockSpec((tm, tk), lambda i, j, k: (i, k))
    if trans_b:
        b_spec = pl.BlockSpec((tn, tk), lambda i, j, k: (j, k))
    else:
        b_spec = pl.BlockSpec((tk, tn), lambda i, j, k: (k, j))
    blocks = (_nbytes((tm, tk), a.dtype) + _nbytes((tk, tn), b.dtype) + _nbytes((tm, tn), out_dtype))
    scratch = [pltpu.VMEM((tm, tn), F32)] if nk > 1 else []
    return pl.pallas_call(
        body,
        out_shape=jax.ShapeDtypeStruct((M, N), out_dtype),
        grid=(M // tm, N // tn, nk),
        in_specs=[a_spec, b_spec],
        out_specs=pl.BlockSpec((tm, tn), lambda i, j, k: (i, j)),
        scratch_shapes=scratch,
        compiler_params=_params(("parallel", "parallel", "arbitrary"), blocks + _nbytes((tm, tn), F32),
                                _nbytes((tm, tn), F32) if nk > 1 else 0),
        name=name,
    )(a, b)


def _h_tile(j, x_ref, meta_ref):
    head = jnp.concatenate([jnp.zeros((FRONT, D), F32), meta_ref[...]], axis=0)
    return jnp.where(j > 0, x_ref[0], head)


def _x_spec():
    return pl.BlockSpec((1, QB, D), lambda b, j: (b, jnp.maximum(j - 1, 0), 0))


def _rms_in(x, meta, g, B, Lp):
    T = B * Lp
    NQ = Lp // QB

    def body(x_ref, meta_ref, g_ref, u_ref):
        h = _h_tile(pl.program_id(1), x_ref, meta_ref)
        r = lax.rsqrt(jnp.mean(h * h, axis=-1, keepdims=True) + EPS)
        u_ref[...] = (h * r * g_ref[...]).astype(BF16)

    return pl.pallas_call(
        body,
        out_shape=jax.ShapeDtypeStruct((T, D), BF16),
        grid=(B, NQ),
        in_specs=[_x_spec(), pl.BlockSpec((N_META, D), lambda b, j: (0, 0)), pl.BlockSpec((1, D), lambda b, j: (0, 0))],
        out_specs=pl.BlockSpec((QB, D), lambda b, j: (b * NQ + j, 0)),
        compiler_params=_params(("parallel", "parallel"), _nbytes((QB, D), F32) * 2),
        name="rms_in",
    )(x, meta, g)


def _gla_gate(lr, wg, bg, valid):
    pre = _nn(lr.astype(BF16), wg) + bg
    logsig = jnp.minimum(pre, 0.0) - jnp.log(1.0 + jnp.exp(-jnp.abs(pre)))
    return pre, jnp.where(valid, logsig / GLA_NORMALIZER, 0.0)


def _tri_masks():
    ri = lax.broadcasted_iota(jnp.int32, (GLA_C, GLA_C), 0)
    ci = lax.broadcasted_iota(jnp.int32, (GLA_C, GLA_C), 1)
    return ci <= ri, ci >= ri


def _cumsum_rows(x, ones_mask):
    w = jnp.where(ones_mask, 1.0, 0.0).astype(BF16)
    a, b, c = _split3(x)
    return _nn(w, a) + _nn(w, b) + _nn(w, c)


def _gla_fwd(projA, projB, wg, bg, gn4, B, Lp):
    T = B * Lp
    NC = Lp // GLA_C
    C = GLA_C
    scale = GLA_DK ** -0.5

    def body(q_ref, k_ref, v_ref, lr_ref, z_ref, wg_ref, bg_ref, gn_ref, oa_ref, ya_ref, ssave_ref, st_ref):
        n = pl.program_id(1)

        @pl.when(n == 0)
        def _():
            st_ref[...] = jnp.zeros_like(st_ref)

        ssave_ref[0, 0] = st_ref[...]
        pos = n * C + lax.broadcasted_iota(jnp.int32, (C, 1), 0)
        _, glog = _gla_gate(lr_ref[...], wg_ref[...], bg_ref[...], pos >= FRONT)
        lower, _ = _tri_masks()
        bcum = _cumsum_rows(glog, lower)
        is_last = lax.broadcasted_iota(jnp.int32, (C, 1), 0) == C - 1
        for h in range(GLA_H):
            ks = slice(h * GLA_DK, (h + 1) * GLA_DK)
            vs = slice(h * GLA_DV, (h + 1) * GLA_DV)
            bh = bcum[:, ks]
            blast = jnp.sum(jnp.where(is_last, bh, 0.0), axis=0, keepdims=True)
            qh = q_ref[:, ks] * scale
            kh = k_ref[:, ks]
            qe = (qh * jnp.exp(bh)).astype(BF16)
            ke = (kh * jnp.exp(-bh)).astype(BF16)
            kl = (kh * jnp.exp(blast - bh)).astype(BF16)
            vh = v_ref[:, vs].astype(BF16)
            a = jnp.where(lower, _nt(qe, ke), 0.0).astype(BF16)
            st = st_ref[h]
            o = _nn(a, vh) + _nt(qe, st.astype(BF16))
            st_ref[h] = st * jnp.exp(blast) + _tn(vh, kl)
            oa_ref[:, vs] = o
            on = o * lax.rsqrt(jnp.mean(o * o, axis=-1, keepdims=True) + EPS) * gn_ref[:, vs]
            z = z_ref[:, vs]
            ya_ref[:, vs] = (on * (z * _sigmoid(z))).astype(BF16)

    row = lambda b, n: b * NC + n
    blocks = (_nbytes((C, 512), F32) * 2 + _nbytes((C, 1024), F32) * 3 + _nbytes((C, 1024), BF16)
              + _nbytes((GLA_H, GLA_DV, GLA_DK), F32) + _nbytes((128, 512), BF16))
    return pl.pallas_call(
        body,
        out_shape=(jax.ShapeDtypeStruct((T, GLA_VW), F32), jax.ShapeDtypeStruct((T, GLA_VW), BF16),
                   jax.ShapeDtypeStruct((B, NC, GLA_H, GLA_DV, GLA_DK), F32)),
        grid=(B, NC),
        in_specs=[
            pl.BlockSpec((C, 512), lambda b, n: (row(b, n), 10)),
            pl.BlockSpec((C, 512), lambda b, n: (row(b, n), 11)),
            pl.BlockSpec((C, 1024), lambda b, n: (row(b, n), 0)),
            pl.BlockSpec((C, 128), lambda b, n: (row(b, n), 3)),
            pl.BlockSpec((C, 1024), lambda b, n: (row(b, n), 1)),
            pl.BlockSpec((128, 512), lambda b, n: (0, 0)),
            pl.BlockSpec((1, 512), lambda b, n: (0, 0)),
            pl.BlockSpec((1, 1024), lambda b, n: (0, 0)),
        ],
        out_specs=(pl.BlockSpec((C, 1024), lambda b, n: (row(b, n), 0)),
                   pl.BlockSpec((C, 1024), lambda b, n: (row(b, n), 0)),
                   pl.BlockSpec((1, 1, GLA_H, GLA_DV, GLA_DK), lambda b, n: (b, n, 0, 0, 0))),
        scratch_shapes=[pltpu.VMEM((GLA_H, GLA_DV, GLA_DK), F32)],
        compiler_params=_params(("parallel", "arbitrary"), blocks, _nbytes((GLA_H, GLA_DV, GLA_DK), F32)),
        name="gla_fwd",
    )(projA, projA, projA, projB, projA, wg, bg, gn4)


def _swap_halves(x):
    lane = lax.broadcasted_iota(jnp.int32, x.shape, 1)
    return jnp.where((lane % 64) < 32, pltpu.roll(x, 96, 1), pltpu.roll(x, 32, 1))


def _mla_prep(projB, cos_t, sin_t, gq, gkv, wuq2, wukv, B, Lp, tr):
    T = B * Lp
    nt = Lp // tr
    HW = 2 * LANES

    def body(pb_ref, cos_ref, sin_ref, gq_ref, gkv_ref, wuq_ref, wukv_ref, q_ref, k_ref, v_ref, cqn_ref, ckvn_ref):
        cq = pb_ref[:, 0:Q_RANK]
        ckv = pb_ref[:, Q_RANK:Q_RANK + KV_RANK]
        kr = pb_ref[:, 512:640]
        cqn = (cq * lax.rsqrt(jnp.mean(cq * cq, axis=-1, keepdims=True) + EPS) * gq_ref[...]).astype(BF16)
        ckvn = (ckv * lax.rsqrt(jnp.mean(ckv * ckv, axis=-1, keepdims=True) + EPS) * gkv_ref[...]).astype(BF16)
        cqn_ref[...] = cqn
        ckvn_ref[...] = ckvn
        qf = _nn(cqn, wuq_ref[...])
        kvf = _nn(ckvn, wukv_ref[...])
        cs = cos_ref[...]
        sn = sin_ref[...]
        rope = lambda t: t * cs + _swap_halves(t) * sn
        kr_r = rope(kr).astype(BF16)
        for h in range(MLA_H):
            q_ref[:, h * HW:h * HW + LANES] = qf[:, h * HW:h * HW + LANES].astype(BF16)
            q_ref[:, h * HW + LANES:(h + 1) * HW] = rope(qf[:, h * HW + LANES:(h + 1) * HW]).astype(BF16)
            k_ref[:, h * HW:h * HW + LANES] = kvf[:, h * HW:h * HW + LANES].astype(BF16)
            k_ref[:, h * HW + LANES:(h + 1) * HW] = kr_r
            v_ref[:, h * MLA_DV:(h + 1) * MLA_DV] = kvf[:, h * HW + LANES:(h + 1) * HW].astype(BF16)

    blocks = (_nbytes((tr, 640), F32) + 2 * _nbytes((tr, 128), F32) + _nbytes((Q_RANK, 2048), BF16)
              + _nbytes((KV_RANK, 2048), BF16) + _nbytes((tr, 2048 * 2 + 1024 + 384), BF16)
              + 2 * _nbytes((tr, 2048), F32))
    return pl.pallas_call(
        body,
        out_shape=(jax.ShapeDtypeStruct((T, MLA_H * HW), BF16), jax.ShapeDtypeStruct((T, MLA_H * HW), BF16),
                   jax.ShapeDtypeStruct((T, MLA_H * MLA_DV), BF16), jax.ShapeDtypeStruct((T, Q_RANK), BF16),
                   jax.ShapeDtypeStruct((T, KV_RANK), BF16)),
        grid=(B, nt),
        in_specs=[
            pl.BlockSpec((tr, 640), lambda b, j: (b * nt + j, 0)),
            pl.BlockSpec((tr, 128), lambda b, j: (j, 0)),
            pl.BlockSpec((tr, 128), lambda b, j: (j, 0)),
            pl.BlockSpec((1, Q_RANK), lambda b, j: (0, 0)),
            pl.BlockSpec((1, KV_RANK), lambda b, j: (0, 0)),
            pl.BlockSpec((Q_RANK, 2048), lambda b, j: (0, 0)),
            pl.BlockSpec((KV_RANK, 2048), lambda b, j: (0, 0)),
        ],
        out_specs=(pl.BlockSpec((tr, 2048), lambda b, j: (b * nt + j, 0)),
                   pl.BlockSpec((tr, 2048), lambda b, j: (b * nt + j, 0)),
                   pl.BlockSpec((tr, 1024), lambda b, j: (b * nt + j, 0)),
                   pl.BlockSpec((tr, Q_RANK), lambda b, j: (b * nt + j, 0)),
                   pl.BlockSpec((tr, KV_RANK), lambda b, j: (b * nt + j, 0))),
        compiler_params=_params(("parallel", "parallel"), blocks),
        name="mla_prep",
    )(projB, cos_t, sin_t, gq, gkv, wuq2, wukv)


def _attn_mask(row, col):
    return (col <= row) & ((col >= FRONT) | (row < FRONT))


def _attn_fwd(q_att, k_att, v_att, projA, B, Lp):
    T = B * Lp
    NQ = Lp // QB
    HW = 2 * LANES
    scale = 1.0 / math.sqrt(MLA_QK)

    def body(q_ref, k_ref, v_ref, mz_ref, o_ref, yb_ref, lser_ref, lsec_ref, m_ref, l_ref, acc_ref):
        qi = pl.program_id(1)
        m_ref[...] = jnp.full(m_ref.shape, NEG, F32)
        l_ref[...] = jnp.zeros_like(l_ref)
        acc_ref[...] = jnp.zeros_like(acc_ref)
        row = qi * QB + lax.broadcasted_iota(jnp.int32, (QB, QB), 0)
        coli = lax.broadcasted_iota(jnp.int32, (QB, QB), 1)

        def step(kj, carry):
            off = pl.multiple_of(kj * QB, QB)
            ok = _attn_mask(row, kj * QB + coli)
            for h in range(MLA_H):
                q = q_ref[:, h * HW:(h + 1) * HW]
                kb = k_ref[pl.ds(off, QB), h * HW:(h + 1) * HW]
                vb = v_ref[pl.ds(off, QB), h * MLA_DV:(h + 1) * MLA_DV]
                s = jnp.where(ok, _nt(q, kb) * scale, NEG)
                m_old = m_ref[h]
                m_new = jnp.maximum(m_old, jnp.max(s, axis=-1, keepdims=True))
                alpha = jnp.exp(m_old - m_new)
                p = jnp.exp(s - jnp.tile(m_new, (1, QB // LANES)))
                m_ref[h] = m_new
                l_ref[h] = alpha * l_ref[h] + jnp.sum(p, axis=-1, keepdims=True)
                acc_ref[h] = alpha * acc_ref[h] + _nn(p.astype(BF16), vb)
            return carry

        lax.fori_loop(0, qi + 1, step, 0)
        for h in range(MLA_H):
            hs = slice(h * MLA_DV, (h + 1) * MLA_DV)
            l = l_ref[h]
            o = acc_ref[h] / l
            o_ref[:, hs] = o
            z = mz_ref[:, hs]
            yb_ref[:, hs] = (o * (z * _sigmoid(z))).astype(BF16)
            lse = m_ref[h] + jnp.log(l)
            lser_ref[0, h] = lse
            lsec_ref[0, h, pl.ds(qi, 1), :] = jnp.transpose(lse)[0:1, :]

    blocks = (_nbytes((QB, 2048), BF16) + _nbytes((Lp, 2048), BF16) + _nbytes((Lp, 1024), BF16)
              + 2 * _nbytes((QB, 1024), F32) + _nbytes((QB, 1024), BF16) + _nbytes((MLA_H, QB, LANES), F32)
              + _nbytes((MLA_H, NQ, QB), F32))
    return pl.pallas_call(
        body,
        out_shape=(jax.ShapeDtypeStruct((T, MLA_H * MLA_DV), F32), jax.ShapeDtypeStruct((T, MLA_H * MLA_DV), BF16),
                   jax.ShapeDtypeStruct((B, MLA_H, Lp, LANES), F32), jax.ShapeDtypeStruct((B, MLA_H, NQ, QB), F32)),
        grid=(B, NQ),
        in_specs=[
            pl.BlockSpec((QB, MLA_H * HW), lambda b, i: (b * NQ + i, 0)),
            pl.BlockSpec((Lp, MLA_H * HW), lambda b, i: (b, 0)),
            pl.BlockSpec((Lp, MLA_H * MLA_DV), lambda b, i: (b, 0)),
            pl.BlockSpec((QB, 1024), lambda b, i: (b * NQ + i, 2)),
        ],
        out_specs=(pl.BlockSpec((QB, 1024), lambda b, i: (b * NQ + i, 0)),
                   pl.BlockSpec((QB, 1024), lambda b, i: (b * NQ + i, 0)),
                   pl.BlockSpec((1, MLA_H, QB, LANES), lambda b, i: (b, 0, i, 0)),
                   pl.BlockSpec((1, MLA_H, NQ, QB), lambda b, i: (b, 0, 0, 0))),
        scratch_shapes=[pltpu.VMEM((MLA_H, QB, LANES), F32), pltpu.VMEM((MLA_H, QB, LANES), F32),
                        pltpu.VMEM((MLA_H, QB, MLA_DV), F32)],
        compiler_params=_params(("parallel", "arbitrary"), blocks, 3 * _nbytes((MLA_H, QB, LANES), F32)),
        name="attn_fwd",
    )(q_att, k_att, v_att, projA)


def _merge_fwd(projA, ya, yb, tr):
    T = ya.shape[0]

    def body(gg_ref, gm_ref, ya_ref, yb_ref, o_ref):
        o_ref[...] = (_sigmoid(gg_ref[...]) * ya_ref[...] + _sigmoid(gm_ref[...]) * yb_ref[...]).astype(BF16)

    spec = lambda c: pl.BlockSpec((tr, D), lambda i: (i, c))
    return pl.pallas_call(
        body,
        out_shape=jax.ShapeDtypeStruct((T, D), BF16),
        grid=(T // tr,),
        in_specs=[spec(3), spec(4), spec(0), spec(0)],
        out_specs=spec(0),
        compiler_params=_params(("parallel",), 5 * _nbytes((tr, D), F32)),
        name="merge_fwd",
    )(projA, projA, ya, yb)


def _final_loss(x, meta, mo, gf, tgt, B, Lp):
    T = B * Lp
    NQ = Lp // QB

    def body(x_ref, meta_ref, mo_ref, gf_ref, t_ref, dh_ref, dhb_ref, loss_ref, dgf_ref):
        b = pl.program_id(0)
        j = pl.program_id(1)

        @pl.when((b == 0) & (j == 0))
        def _():
            loss_ref[...] = jnp.zeros_like(loss_ref)
            dgf_ref[...] = jnp.zeros_like(dgf_ref)

        h1 = _h_tile(j, x_ref, meta_ref) + mo_ref[...]
        r = lax.rsqrt(jnp.mean(h1 * h1, axis=-1, keepdims=True) + EPS)
        hn = h1 * r
        gfv = gf_ref[...]
        diff = jnp.where(j > 0, hn * gfv - t_ref[0], 0.0)
        loss_ref[...] += (0.5 / D) * jnp.sum(jnp.sum(diff * diff, axis=-1, keepdims=True), axis=0, keepdims=True)
        dout = diff * (1.0 / D)
        dgf_ref[...] += jnp.sum(dout * hn, axis=0, keepdims=True)
        dhn = dout * gfv
        dh = r * (dhn - hn * jnp.mean(dhn * hn, axis=-1, keepdims=True))
        dh_ref[...] = dh
        dhb_ref[...] = dh.astype(BF16)

    rows = pl.BlockSpec((QB, D), lambda b, j: (b * NQ + j, 0))
    return pl.pallas_call(
        body,
        out_shape=(jax.ShapeDtypeStruct((T, D), F32), jax.ShapeDtypeStruct((T, D), BF16),
                   jax.ShapeDtypeStruct((1, 1), F32), jax.ShapeDtypeStruct((1, D), F32)),
        grid=(B, NQ),
        in_specs=[_x_spec(), pl.BlockSpec((N_META, D), lambda b, j: (0, 0)), rows,
                  pl.BlockSpec((1, D), lambda b, j: (0, 0)), _x_spec()],
        out_specs=(rows, rows, pl.BlockSpec((1, 1), lambda b, j: (0, 0)), pl.BlockSpec((1, D), lambda b, j: (0, 0))),
        compiler_params=_params(("arbitrary", "arbitrary"), 5 * _nbytes((QB, D), F32)),
        name="final_loss",
    )(x, meta, mo, gf, tgt)


def _merge_bwd(dm, projA, ya, yb, tr):
    T = dm.shape[0]

    def body(dm_ref, gg_ref, gm_ref, ya_ref, yb_ref, dya_ref, dyb_ref, da_ref):
        d = dm_ref[...]
        sg = _sigmoid(gg_ref[...])
        sm = _sigmoid(gm_ref[...])
        dya_ref[...] = (d * sg).astype(BF16)
        dyb_ref[...] = (d * sm).astype(BF16)
        da_ref[:, 0:D] = (d * ya_ref[...] * (sg * (1.0 - sg))).astype(BF16)
        da_ref[:, D:2 * D] = (d * yb_ref[...] * (sm * (1.0 - sm))).astype(BF16)

    spec = lambda c: pl.BlockSpec((tr, D), lambda i: (i, c))
    return pl.pallas_call(
        body,
        out_shape=(jax.ShapeDtypeStruct((T, D), BF16), jax.ShapeDtypeStruct((T, D), BF16),
                   jax.ShapeDtypeStruct((T, 2 * D), BF16)),
        grid=(T // tr,),
        in_specs=[spec(0), spec(3), spec(4), spec(0), spec(0)],
        out_specs=(spec(0), spec(0), pl.BlockSpec((tr, 2 * D), lambda i: (i, 0))),
        compiler_params=_params(("parallel",), 8 * _nbytes((tr, D), F32)),
        name="merge_bwd",
    )(dm, projA, projA, ya, yb)


def _gla_out_bwd(dyin, oa, projA, gn4, tr):
    T = dyin.shape[0]
    nsteps = T // tr

    def body(dy_ref, oa_ref, z_ref, gn_ref, do_ref, dz_ref, dgn_ref, acc_ref):
        i = pl.program_id(0)

        @pl.when(i == 0)
        def _():
            acc_ref[...] = jnp.zeros_like(acc_ref)

        for h in range(GLA_H):
            vs = slice(h * GLA_DV, (h + 1) * GLA_DV)
            dy = dy_ref[:, vs]
            o = oa_ref[:, vs]
            z = z_ref[:, vs]
            gn = gn_ref[:, vs]
            s = _sigmoid(z)
            ra = lax.rsqrt(jnp.mean(o * o, axis=-1, keepdims=True) + EPS)
            on = o * ra
            don = dy * (z * s)
            t = don * gn
            do_ref[:, vs] = (ra * (t - on * jnp.mean(t * on, axis=-1, keepdims=True))).astype(BF16)
            dz_ref[:, vs] = (dy * (on * gn) * (s * (1.0 + z * (1.0 - s)))).astype(BF16)
            acc_ref[:, vs] += jnp.sum(don * on, axis=0, keepdims=True)

        @pl.when(i == nsteps - 1)
        def _():
            a = acc_ref[...]
            dgn_ref[...] = a[:, 0:256] + a[:, 256:512] + a[:, 512:768] + a[:, 768:1024]

    spec = lambda c: pl.BlockSpec((tr, D), lambda i: (i, c))
    return pl.pallas_call(
        body,
        out_shape=(jax.ShapeDtypeStruct((T, D), BF16), jax.ShapeDtypeStruct((T, D), BF16),
                   jax.ShapeDtypeStruct((1, GLA_DV), F32)),
        grid=(nsteps,),
        in_specs=[spec(0), spec(0), spec(1), pl.BlockSpec((1, D), lambda i: (0, 0))],
        out_specs=(spec(0), spec(0), pl.BlockSpec((1, GLA_DV), lambda i: (0, 0))),
        scratch_shapes=[pltpu.VMEM((1, D), F32)],
        compiler_params=_params(("arbitrary",), 6 * _nbytes((tr, D), F32)),
        name="gla_out_bwd",
    )(dyin, oa, projA, gn4)


def _gla_bwd(projA, projB, ssave, doa, wg, bg, B, Lp):
    T = B * Lp
    NC = Lp // GLA_C
    C = GLA_C
    scale = GLA_DK ** -0.5
    WC = 2304

    def body(q_ref, k_ref, v_ref, lr_ref, ss_ref, do_ref, wg_ref, bg_ref, dc_ref, dwg_ref, dbg_ref, dst_ref):
        b = pl.program_id(0)
        i = pl.program_id(1)
        n = NC - 1 - i

        @pl.when(i == 0)
        def _():
            dst_ref[...] = jnp.zeros_like(dst_ref)

        @pl.when((b == 0) & (i == 0))
        def _():
            dwg_ref[...] = jnp.zeros_like(dwg_ref)
            dbg_ref[...] = jnp.zeros_like(dbg_ref)

        pos = n * C + lax.broadcasted_iota(jnp.int32, (C, 1), 0)
        valid = pos >= FRONT
        lr = lr_ref[...]
        pre, glog = _gla_gate(lr, wg_ref[...], bg_ref[...], valid)
        lower, upper = _tri_masks()
        bcum = _cumsum_rows(glog, lower)
        is_last = lax.broadcasted_iota(jnp.int32, (C, 1), 0) == C - 1
        db_parts = []
        for h in range(GLA_H):
            ks = slice(h * GLA_DK, (h + 1) * GLA_DK)
            vs = slice(h * GLA_DV, (h + 1) * GLA_DV)
            bh = bcum[:, ks]
            blast = jnp.sum(jnp.where(is_last, bh, 0.0), axis=0, keepdims=True)
            eb, enb, ekl, ebl = jnp.exp(bh), jnp.exp(-bh), jnp.exp(blast - bh), jnp.exp(blast)
            qh = q_ref[:, ks] * scale
            kh = k_ref[:, ks]
            qe_f, ke_f, kl_f = qh * eb, kh * enb, kh * ekl
            qe, ke, kl = qe_f.astype(BF16), ke_f.astype(BF16), kl_f.astype(BF16)
            vh = v_ref[:, vs].astype(BF16)
            doh = do_ref[:, vs]
            st = ss_ref[0, 0, h]
            dst = dst_ref[h]
            st_b, dst_b = st.astype(BF16), dst.astype(BF16)
            da = jnp.where(lower, _nt(doh, vh), 0.0).astype(BF16)
            da_t = jnp.where(upper, _nt(vh, doh), 0.0).astype(BF16)
            a_t = jnp.where(upper, _nt(ke, qe), 0.0).astype(BF16)
            dqe = _nn(da, ke) + _nn(doh, st_b)
            dke = _nn(da_t, qe)
            dvh = _nn(a_t, doh) + _nt(kl, dst_b)
            dkl = _nn(vh, dst_b)
            dst_ref[h] = dst * ebl + _tn(doh, qe)
            deb = jnp.sum(st * dst, axis=0, keepdims=True)
            db = dqe * qe_f - dke * ke_f - dkl * kl_f
            db_last = jnp.sum(dkl * kl_f, axis=0, keepdims=True) + deb * ebl
            db_parts.append(db + jnp.where(is_last, db_last, 0.0))
            dc_ref[:, vs] = dvh.astype(BF16)
            dc_ref[:, 1024 + h * GLA_DK:1024 + (h + 1) * GLA_DK] = (dqe * eb * scale).astype(BF16)
            dc_ref[:, 1536 + h * GLA_DK:1536 + (h + 1) * GLA_DK] = (dke * enb + dkl * ekl).astype(BF16)
        dglog = _cumsum_rows(jnp.concatenate(db_parts, axis=1), upper)
        dpre = jnp.where(valid, dglog * (1.0 / GLA_NORMALIZER) / (1.0 + jnp.exp(pre)), 0.0)
        dpre_b = dpre.astype(BF16)
        dc_ref[:, 2048:2176] = _nt(dpre_b, wg_ref[...]).astype(BF16)
        dc_ref[:, 2176:2304] = jnp.zeros((C, 128), BF16)
        dwg_ref[...] += _tn(lr.astype(BF16), dpre_b)
        dbg_ref[...] += jnp.sum(dpre, axis=0, keepdims=True)

    row = lambda b, i: b * NC + (NC - 1 - i)
    blocks = (_nbytes((C, 512), F32) * 2 + _nbytes((C, 1024), F32) + _nbytes((C, 1024), BF16)
              + _nbytes((GLA_H, GLA_DV, GLA_DK), F32) + _nbytes((C, WC), BF16) + 3 * _nbytes((128, 512), F32))
    return pl.pallas_call(
        body,
        out_shape=(jax.ShapeDtypeStruct((T, WC), BF16), jax.ShapeDtypeStruct((128, GLA_KW), F32),
                   jax.ShapeDtypeStruct((1, GLA_KW), F32)),
        grid=(B, NC),
        in_specs=[
            pl.BlockSpec((C, 512), lambda b, i: (row(b, i), 10)),
            pl.BlockSpec((C, 512), lambda b, i: (row(b, i), 11)),
            pl.BlockSpec((C, 1024), lambda b, i: (row(b, i), 0)),
            pl.BlockSpec((C, 128), lambda b, i: (row(b, i), 3)),
            pl.BlockSpec((1, 1, GLA_H, GLA_DV, GLA_DK), lambda b, i: (b, NC - 1 - i, 0, 0, 0)),
            pl.BlockSpec((C, 1024), lambda b, i: (row(b, i), 0)),
            pl.BlockSpec((128, 512), lambda b, i: (0, 0)),
            pl.BlockSpec((1, 512), lambda b, i: (0, 0)),
        ],
        out_specs=(pl.BlockSpec((C, WC), lambda b, i: (row(b, i), 0)),
                   pl.BlockSpec((128, GLA_KW), lambda b, i: (0, 0)),
                   pl.BlockSpec((1, GLA_KW), lambda b, i: (0, 0))),
        scratch_shapes=[pltpu.VMEM((GLA_H, GLA_DV, GLA_DK), F32)],
        compiler_params=_params(("arbitrary", "arbitrary"), blocks, _nbytes((GLA_H, GLA_DV, GLA_DK), F32)),
        name="gla_bwd",
    )(projA, projA, projA, projB, ssave, doa, wg, bg)


def _attn_bwd_pre(dyin, projA, ob, B, Lp):
    T = B * Lp
    NQ = Lp // QB

    def body(dy_ref, z_ref, o_ref, do_ref, dz_ref, dr_ref, dcol_ref):
        j = pl.program_id(1)
        for h in range(MLA_H):
            hs = slice(h * MLA_DV, (h + 1) * MLA_DV)
            dy = dy_ref[:, hs]
            z = z_ref[:, hs]
            o = o_ref[:, hs]
            s = _sigmoid(z)
            do = dy * (z * s)
            do_ref[:, hs] = do.astype(BF16)
            dz_ref[:, hs] = (dy * o * (s * (1.0 + z * (1.0 - s)))).astype(BF16)
            dl = jnp.broadcast_to(jnp.sum(do * o, axis=-1, keepdims=True), (QB, LANES))
            dr_ref[0, h] = dl
            dcol_ref[0, h, pl.ds(j, 1), :] = jnp.transpose(dl)[0:1, :]

    rows = lambda c: pl.BlockSpec((QB, D), lambda b, j: (b * NQ + j, c))
    return pl.pallas_call(
        body,
        out_shape=(jax.ShapeDtypeStruct((T, D), BF16), jax.ShapeDtypeStruct((T, D), BF16),
                   jax.ShapeDtypeStruct((B, MLA_H, Lp, LANES), F32), jax.ShapeDtypeStruct((B, MLA_H, NQ, QB), F32)),
        grid=(B, NQ),
        in_specs=[rows(0), rows(2), rows(0)],
        out_specs=(rows(0), rows(0), pl.BlockSpec((1, MLA_H, QB, LANES), lambda b, j: (b, 0, j, 0)),
                   pl.BlockSpec((1, MLA_H, NQ, QB), lambda b, j: (b, 0, 0, 0))),
        compiler_params=_params(("parallel", "arbitrary"), 6 * _nbytes((QB, D), F32)),
        name="attn_bwd_pre",
    )(dyin, projA, ob)


def _attn_bwd_dq(q_att, k_att, v_att, do, lse_r, delta_r, B, Lp):
    T = B * Lp
    NQ = Lp // QB
    scale = 1.0 / math.sqrt(MLA_QK)

    HW = 2 * LANES

    def body(q_ref, k_ref, v_ref, do_ref, lse_ref, dl_ref, dq_ref):
        qi = pl.program_id(1)
        dq_ref[...] = jnp.zeros_like(dq_ref)
        row = qi * QB + lax.broadcasted_iota(jnp.int32, (QB, QB), 0)
        coli = lax.broadcasted_iota(jnp.int32, (QB, QB), 1)

        def step(kj, carry):
            off = pl.multiple_of(kj * QB, QB)
            ok = _attn_mask(row, kj * QB + coli)
            for h in range(MLA_H):
                ws = slice(h * HW, (h + 1) * HW)
                hs = slice(h * MLA_DV, (h + 1) * MLA_DV)
                kb = k_ref[pl.ds(off, QB), ws]
                vb = v_ref[pl.ds(off, QB), hs]
                lse = jnp.tile(lse_ref[0, h], (1, QB // LANES))
                delta = jnp.tile(dl_ref[0, h], (1, QB // LANES))
                s = _nt(q_ref[:, ws], kb) * scale
                p = jnp.where(ok, jnp.exp(s - lse), 0.0)
                ds = p * (_nt(do_ref[:, hs], vb) - delta) * scale
                dq_ref[:, ws] += _nn(ds.astype(BF16), kb)
            return carry

        lax.fori_loop(0, qi + 1, step, 0)

    blocks = (_nbytes((QB, 2048), BF16) + _nbytes((Lp, 2048), BF16) + _nbytes((Lp, 1024), BF16)
              + _nbytes((QB, 1024), BF16) + 2 * _nbytes((MLA_H, QB, LANES), F32) + _nbytes((QB, 2048), F32))
    return pl.pallas_call(
        body,
        out_shape=jax.ShapeDtypeStruct((T, MLA_H * HW), F32),
        grid=(B, NQ),
        in_specs=[
            pl.BlockSpec((QB, MLA_H * HW), lambda b, i: (b * NQ + i, 0)),
            pl.BlockSpec((Lp, MLA_H * HW), lambda b, i: (b, 0)),
            pl.BlockSpec((Lp, MLA_H * MLA_DV), lambda b, i: (b, 0)),
            pl.BlockSpec((QB, MLA_H * MLA_DV), lambda b, i: (b * NQ + i, 0)),
            pl.BlockSpec((1, MLA_H, QB, LANES), lambda b, i: (b, 0, i, 0)),
            pl.BlockSpec((1, MLA_H, QB, LANES), lambda b, i: (b, 0, i, 0)),
        ],
        out_specs=pl.BlockSpec((QB, MLA_H * HW), lambda b, i: (b * NQ + i, 0)),
        compiler_params=_params(("parallel", "parallel"), blocks),
        name="attn_bwd_dq",
    )(q_att, k_att, v_att, do, lse_r, delta_r)


def _attn_bwd_dkv(q_att, k_att, v_att, do, lse_c, delta_c, B, Lp):
    T = B * Lp
    NQ = Lp // QB
    scale = 1.0 / math.sqrt(MLA_QK)

    HW = 2 * LANES

    def body(q_ref, k_ref, v_ref, do_ref, lse_ref, dl_ref, dk_ref, dv_ref):
        kj = pl.program_id(1)
        dk_ref[...] = jnp.zeros_like(dk_ref)
        dv_ref[...] = jnp.zeros_like(dv_ref)
        col = kj * QB + lax.broadcasted_iota(jnp.int32, (QB, QB), 0)
        rowi = lax.broadcasted_iota(jnp.int32, (QB, QB), 1)

        def step(qi, carry):
            off = pl.multiple_of(qi * QB, QB)
            ok = _attn_mask(qi * QB + rowi, col)
            for h in range(MLA_H):
                ws = slice(h * HW, (h + 1) * HW)
                hs = slice(h * MLA_DV, (h + 1) * MLA_DV)
                qb = q_ref[pl.ds(off, QB), ws]
                dob = do_ref[pl.ds(off, QB), hs]
                lse = lse_ref[0, h, pl.ds(qi, 1), :]
                delta = dl_ref[0, h, pl.ds(qi, 1), :]
                s_t = _nt(k_ref[:, ws], qb) * scale
                p_t = jnp.where(ok, jnp.exp(s_t - lse), 0.0)
                dv_ref[:, hs] += _nn(p_t.astype(BF16), dob)
                ds_t = p_t * (_nt(v_ref[:, hs], dob) - delta) * scale
                dk_ref[:, ws] += _nn(ds_t.astype(BF16), qb)
            return carry

        lax.fori_loop(kj, NQ, step, 0)

    blocks = (_nbytes((Lp, 2048), BF16) + _nbytes((Lp, 1024), BF16) + _nbytes((QB, 3072), BF16)
              + 2 * _nbytes((MLA_H, NQ, QB), F32) + _nbytes((QB, 3072), F32))
    return pl.pallas_call(
        body,
        out_shape=(jax.ShapeDtypeStruct((T, MLA_H * HW), F32), jax.ShapeDtypeStruct((T, MLA_H * MLA_DV), F32)),
        grid=(B, NQ),
        in_specs=[
            pl.BlockSpec((Lp, MLA_H * HW), lambda b, j: (b, 0)),
            pl.BlockSpec((QB, MLA_H * HW), lambda b, j: (b * NQ + j, 0)),
            pl.BlockSpec((QB, MLA_H * MLA_DV), lambda b, j: (b * NQ + j, 0)),
            pl.BlockSpec((Lp, MLA_H * MLA_DV), lambda b, j: (b, 0)),
            pl.BlockSpec((1, MLA_H, NQ, QB), lambda b, j: (b, 0, 0, 0)),
            pl.BlockSpec((1, MLA_H, NQ, QB), lambda b, j: (b, 0, 0, 0)),
        ],
        out_specs=(pl.BlockSpec((QB, MLA_H * HW), lambda b, j: (b * NQ + j, 0)),
                   pl.BlockSpec((QB, MLA_H * MLA_DV), lambda b, j: (b * NQ + j, 0))),
        compiler_params=_params(("parallel", "parallel"), blocks),
        name="attn_bwd_dkv",
    )(q_att, k_att, v_att, do, lse_c, delta_c)


def _mla_bwd_post(dq, dk, dv, projB, cos_t, sin_t, gq, gkv, wuq2, wukv, B, Lp, tr):
    T = B * Lp
    nt = Lp // tr
    HW = 2 * LANES

    def body(dq_ref, dk_ref, dv_ref, pb_ref, cos_ref, sin_ref, gq_ref, gkv_ref, wuq_ref, wukv_ref,
             dqf_ref, dkvf_ref, de_ref, dgq_ref, dgkv_ref):
        first = (pl.program_id(0) == 0) & (pl.program_id(1) == 0)

        @pl.when(first)
        def _():
            dgq_ref[...] = jnp.zeros_like(dgq_ref)
            dgkv_ref[...] = jnp.zeros_like(dgkv_ref)

        cs = cos_ref[...]
        sn = sin_ref[...]
        rope_t = lambda t: t * cs + _swap_halves(t * sn)
        dkr = jnp.zeros((tr, LANES), F32)
        for h in range(MLA_H):
            dqf_ref[:, h * HW:h * HW + LANES] = dq_ref[:, h * HW:h * HW + LANES].astype(BF16)
            dqf_ref[:, h * HW + LANES:(h + 1) * HW] = rope_t(dq_ref[:, h * HW + LANES:(h + 1) * HW]).astype(BF16)
            dkvf_ref[:, h * HW:h * HW + LANES] = dk_ref[:, h * HW:h * HW + LANES].astype(BF16)
            dkvf_ref[:, h * HW + LANES:(h + 1) * HW] = dv_ref[:, h * MLA_DV:(h + 1) * MLA_DV].astype(BF16)
            dkr = dkr + dk_ref[:, h * HW + LANES:(h + 1) * HW]

        def norm_bwd(x, dn, g):
            r = lax.rsqrt(jnp.mean(x * x, axis=-1, keepdims=True) + EPS)
            xn = x * r
            t = dn * g
            return r * (t - xn * jnp.mean(t * xn, axis=-1, keepdims=True)), jnp.sum(dn * xn, axis=0, keepdims=True)

        dcq, dgq = norm_bwd(pb_ref[:, 0:Q_RANK], _nt(dqf_ref[...], wuq_ref[...]), gq_ref[...])
        dckv, dgkv = norm_bwd(pb_ref[:, Q_RANK:Q_RANK + KV_RANK], _nt(dkvf_ref[...], wukv_ref[...]), gkv_ref[...])
        dgq_ref[...] += dgq
        dgkv_ref[...] += dgkv
        de_ref[:, 0:Q_RANK] = dcq.astype(BF16)
        de_ref[:, Q_RANK:Q_RANK + KV_RANK] = dckv.astype(BF16)
        de_ref[:, 384:512] = rope_t(dkr).astype(BF16)

    rows = lambda w: pl.BlockSpec((tr, w), lambda b, j: (b * nt + j, 0))
    const = lambda s: pl.BlockSpec(s, lambda b, j: (0, 0))
    blocks = (2 * _nbytes((tr, 2048), F32) + _nbytes((tr, 1024), F32) + _nbytes((tr, 640), F32)
              + 2 * _nbytes((tr, 2048), BF16) + _nbytes((2048, 384), BF16) + 2 * _nbytes((tr, 2048), F32))
    return pl.pallas_call(
        body,
        out_shape=(jax.ShapeDtypeStruct((T, 2048), BF16), jax.ShapeDtypeStruct((T, 2048), BF16),
                   jax.ShapeDtypeStruct((T, 512), BF16), jax.ShapeDtypeStruct((1, Q_RANK), F32),
                   jax.ShapeDtypeStruct((1, KV_RANK), F32)),
        grid=(B, nt),
        in_specs=[rows(2048), rows(2048), rows(1024), rows(640),
                  pl.BlockSpec((tr, 128), lambda b, j: (j, 0)), pl.BlockSpec((tr, 128), lambda b, j: (j, 0)),
                  const((1, Q_RANK)), const((1, KV_RANK)), const((Q_RANK, 2048)), const((KV_RANK, 2048))],
        out_specs=(rows(2048), rows(2048), rows(512), const((1, Q_RANK)), const((1, KV_RANK))),
        compiler_params=_params(("arbitrary", "arbitrary"), blocks),
        name="mla_bwd_post",
    )(dq, dk, dv, projB, cos_t, sin_t, gq, gkv, wuq2, wukv)


def _du_matmul(dA, dBz, dC, dDz, dE, wA, wB, tm):
    T = dA.shape[0]

    def body(da_ref, db_ref, dc_ref, dd_ref, de_ref, wa_ref, wb_ref, o_ref):
        acc = _nt(da_ref[...], wa_ref[:, 3072:5120])
        acc = acc + _nt(db_ref[...], wa_ref[:, 1024:2048])
        acc = acc + _nt(dd_ref[...], wa_ref[:, 2048:3072])
        acc = acc + _nt(dc_ref[:, 0:1024], wa_ref[:, 0:1024])
        acc = acc + _nt(dc_ref[:, 1024:2048], wa_ref[:, 5120:6144])
        acc = acc + _nt(dc_ref[:, 2048:2176], wb_ref[:, 384:512])
        acc = acc + _nt(de_ref[:, 0:384], wb_ref[:, 0:384])
        acc = acc + _nt(de_ref[:, 384:512], wb_ref[:, 512:640])
        o_ref[...] = acc

    widths = [a.shape[1] for a in (dA, dBz, dC, dDz, dE)]
    blocks = (sum(_nbytes((tm, w), BF16) for w in widths) + _nbytes(wA.shape, BF16) + _nbytes(wB.shape, BF16)
              + _nbytes((tm, D), F32))
    return pl.pallas_call(
        body,
        out_shape=jax.ShapeDtypeStruct((T, D), F32),
        grid=(T // tm,),
        in_specs=[pl.BlockSpec((tm, w), lambda i: (i, 0)) for w in widths]
        + [pl.BlockSpec(wA.shape, lambda i: (0, 0)), pl.BlockSpec(wB.shape, lambda i: (0, 0))],
        out_specs=pl.BlockSpec((tm, D), lambda i: (i, 0)),
        compiler_params=_params(("parallel",), blocks),
        name="du_matmul",
    )(dA, dBz, dC, dDz, dE, wA, wB)


def _in_norm_bwd(x, meta, dh1, du, g, B, Lp):
    NQ = Lp // QB
    seq = x.shape[1]

    def body(x_ref, meta_ref, dh_ref, du_ref, g_ref, gx_ref, dmeta_ref, dg_ref):
        b = pl.program_id(0)
        j = pl.program_id(1)

        @pl.when((b == 0) & (j == 0))
        def _():
            dg_ref[...] = jnp.zeros_like(dg_ref)

        x = _h_tile(j, x_ref, meta_ref)
        r = lax.rsqrt(jnp.mean(x * x, axis=-1, keepdims=True) + EPS)
        xn = x * r
        du = du_ref[...]
        t = du * g_ref[...]
        dh0 = dh_ref[...] + r * (t - xn * jnp.mean(t * xn, axis=-1, keepdims=True))
        dg_ref[...] += jnp.sum(du * xn, axis=0, keepdims=True)
        gx_ref[0] = dh0

        @pl.when((j == 0) & (b == 0))
        def _():
            dmeta_ref[...] = dh0[FRONT:HEAD_ROWS, :]

        @pl.when((j == 0) & (b > 0))
        def _():
            dmeta_ref[...] += dh0[FRONT:HEAD_ROWS, :]

    rows = pl.BlockSpec((QB, D), lambda b, j: (b * NQ + j, 0))
    return pl.pallas_call(
        body,
        out_shape=(jax.ShapeDtypeStruct((B, seq, D), F32), jax.ShapeDtypeStruct((N_META, D), F32),
                   jax.ShapeDtypeStruct((1, D), F32)),
        grid=(B, NQ),
        in_specs=[_x_spec(), pl.BlockSpec((N_META, D), lambda b, j: (0, 0)), rows, rows,
                  pl.BlockSpec((1, D), lambda b, j: (0, 0))],
        out_specs=(_x_spec(), pl.BlockSpec((N_META, D), lambda b, j: (0, 0)), pl.BlockSpec((1, D), lambda b, j: (0, 0))),
        compiler_params=_params(("arbitrary", "arbitrary"), 5 * _nbytes((QB, D), F32)),
        name="in_norm_bwd",
    )(x, meta, dh1, du, g)


_VMEM_WHOLE = pl.BlockSpec(memory_space=pltpu.VMEM)


def _params_whole(arrays):
    total = sum(_nbytes(a.shape, a.dtype) for a in arrays)
    return pltpu.CompilerParams(vmem_limit_bytes=int(min(total + 12 * 1024 * 1024, VMEM_CAP_V7X)))


def _wire_dtype(shape):
    return BF16 if shape[-2] * shape[-1] >= WIRE_BF16_MIN_ELEMS else F32


def _pair_add_big(gp, recv, c):
    _, half, cols = recv.shape
    th = _div_tile(half, 64, 16)
    out_dtype = _wire_dtype(recv.shape)

    def body(c_ref, a_ref, b_ref, o_ref):
        o_ref[...] = (a_ref[:, 0] + b_ref[...]).astype(out_dtype)

    return pl.pallas_call(
        body,
        out_shape=jax.ShapeDtypeStruct(recv.shape, out_dtype),
        grid_spec=pltpu.PrefetchScalarGridSpec(
            num_scalar_prefetch=1,
            grid=(half // th,),
            in_specs=[pl.BlockSpec((4, 1, th, cols), lambda i, c_ref: (0, c_ref[0], i, 0)),
                      pl.BlockSpec((4, th, cols), lambda i, c_ref: (0, i, 0))],
            out_specs=pl.BlockSpec((4, th, cols), lambda i, c_ref: (0, i, 0)),
        ),
        compiler_params=_params(("parallel",), 3 * _nbytes((4, th, cols), F32)),
        name="grad_pair_add_big",
    )(c, gp.reshape(4, 2, half, cols), recv)


def _pair_add_small(gps, recvs):
    n = len(gps)

    def body(*refs):
        c = lax.axis_index("c")
        for t in range(n):
            g_ref, r_ref, o_ref = refs[t], refs[n + t], refs[2 * n + t]
            half = r_ref.shape[1]
            s = g_ref[:, pl.ds(pl.multiple_of(c * half, 8), half), :] + r_ref[...]
            o_ref[...] = s.astype(o_ref.dtype)

    return pl.pallas_call(
        body,
        out_shape=[jax.ShapeDtypeStruct(r.shape, _wire_dtype(r.shape)) for r in recvs],
        in_specs=[_VMEM_WHOLE] * (2 * n),
        out_specs=[_VMEM_WHOLE] * n,
        compiler_params=_params_whole(list(gps) + 2 * list(recvs)),
        name="grad_pair_add_small",
    )(*gps, *recvs)


def _chip_order_sum(p_ref):
    p = [p_ref[k].astype(F32) for k in range(4)]
    return ((p[0] + p[1]) + p[2]) + p[3]


def _sum_chips_big(parts):
    _, half, cols = parts.shape
    th = _div_tile(half, 64, 8)

    def body(p_ref, o_ref):
        o_ref[...] = _chip_order_sum(p_ref)

    return pl.pallas_call(
        body,
        out_shape=jax.ShapeDtypeStruct((half, cols), F32),
        grid=(half // th,),
        in_specs=[pl.BlockSpec((4, th, cols), lambda i: (0, i, 0))],
        out_specs=pl.BlockSpec((th, cols), lambda i: (i, 0)),
        compiler_params=_params(("parallel",), 5 * _nbytes((th, cols), F32)),
        name="grad_sum_chips_big",
    )(parts)


def _sum_chips_small(parts):
    n = len(parts)

    def body(*refs):
        for t in range(n):
            refs[n + t][...] = _chip_order_sum(refs[t])

    return pl.pallas_call(
        body,
        out_shape=[jax.ShapeDtypeStruct(p.shape[1:], F32) for p in parts],
        in_specs=[_VMEM_WHOLE] * n,
        out_specs=[_VMEM_WHOLE] * n,
        compiler_params=_params_whole(list(parts) + list(parts)),
        name="grad_sum_chips_small",
    )(*parts)


def _adamw_update(w_ref, g_ref, m_ref, v_ref, d_ref, mo_ref, vo_ref):
    c1 = 1.0 - ADAM_B1 ** ADAM_STEP
    c2 = 1.0 - ADAM_B2 ** ADAM_STEP
    gv = g_ref[...]
    mn = ADAM_B1 * m_ref[...] + (1.0 - ADAM_B1) * gv
    vn = ADAM_B2 * v_ref[...] + (1.0 - ADAM_B2) * (gv * gv)
    mo_ref[...] = mn
    vo_ref[...] = vn
    d_ref[...] = -ADAM_LR * ((mn / c1) / (jnp.sqrt(vn / c2) + ADAM_EPS) + ADAM_WD * w_ref[...])


def _adamw_big(w, g, m, v):
    rows, cols = w.shape
    tr = _div_tile(rows, 128, 8)
    spec = pl.BlockSpec((tr, cols), lambda i: (i, 0))
    shp = jax.ShapeDtypeStruct((rows, cols), F32)
    return pl.pallas_call(
        functools.partial(_adamw_update),
        out_shape=(shp, shp, shp),
        grid=(rows // tr,),
        in_specs=[spec] * 4,
        out_specs=(spec, spec, spec),
        compiler_params=_params(("parallel",), 7 * _nbytes((tr, cols), F32)),
        name="adamw_big",
    )(w, g, m, v)


def _adamw_small(ws, gs, ms, vs):
    n = len(ws)

    def body(*refs):
        for t in range(n):
            _adamw_update(refs[t], refs[n + t], refs[2 * n + t], refs[3 * n + t],
                          refs[4 * n + t], refs[5 * n + t], refs[6 * n + t])

    shapes = [jax.ShapeDtypeStruct(w.shape, F32) for w in ws]
    return pl.pallas_call(
        body,
        out_shape=shapes * 3,
        in_specs=[_VMEM_WHOLE] * (4 * n),
        out_specs=[_VMEM_WHOLE] * (3 * n),
        compiler_params=_params_whole(list(ws) * 7),
        name="adamw_small",
    )(*ws, *gs, *ms, *vs)


def _mesh_pos():
    return lax.axis_index("x"), lax.axis_index("y"), lax.axis_index("c")


def _other_chips(x, y):
    return [(1 - x, y), (x, 1 - y), (1 - x, 1 - y)]


_ANY = pl.BlockSpec(memory_space=pl.ANY)


PAIR_SPLIT_MIN_ROWS = 64


def _weight_gather(shards):
    n = len(shards)
    split = [s.shape[0] >= PAIR_SPLIT_MIN_ROWS for s in shards]

    def body(*refs):
        w_refs, o_refs = refs[:n], refs[n:2 * n]
        send_sems, recv_sems, local_sems = refs[2 * n:]
        x, y, c = _mesh_pos()
        me = 2 * x + y
        chips = _other_chips(x, y)

        def rows_of(t, core):
            rows = shards[t].shape[0]
            if not split[t]:
                return pl.ds(0, rows)
            return pl.ds(pl.multiple_of(core * (rows // 2), 16), rows // 2)

        def landed(t, k, slot, rows, to):
            ref = o_refs[t].at[slot, rows]
            return pltpu.make_async_remote_copy(src_ref=ref, dst_ref=ref, send_sem=send_sems.at[6 * t + k],
                                                recv_sem=recv_sems.at[6 * t + k], device_id=to, device_id_type=MESH)

        local, sends = [], []
        for t in range(n):
            own = pltpu.make_async_copy(w_refs[t], o_refs[t].at[me], local_sems.at[t])
            own.start()
            local.append(own)
            mine = rows_of(t, c)
            for k, (px, py) in enumerate(chips):
                cp = pltpu.make_async_remote_copy(src_ref=w_refs[t].at[mine], dst_ref=o_refs[t].at[me, mine],
                                                  send_sem=send_sems.at[6 * t + k], recv_sem=recv_sems.at[6 * t + k],
                                                  device_id=(px, py, c), device_id_type=MESH)
                cp.start()
                sends.append(cp)
        for t in range(n):
            mine = rows_of(t, c)
            for k, (px, py) in enumerate(chips):
                landed(t, k, 2 * px + py, mine, (x, y, c)).wait_recv()
                if split[t]:
                    cp = landed(t, 3 + k, 2 * px + py, mine, (x, y, 1 - c))
                    cp.start()
                    sends.append(cp)
        for t in range(n):
            if split[t]:
                for k, (px, py) in enumerate(chips):
                    landed(t, 3 + k, 2 * px + py, rows_of(t, 1 - c), (x, y, c)).wait_recv()
        for cp in sends:
            cp.wait_send()
        for own in local:
            own.wait()

    return pl.pallas_call(
        body,
        out_shape=[jax.ShapeDtypeStruct((4,) + s.shape, s.dtype) for s in shards],
        in_specs=[_ANY] * n,
        out_specs=[_ANY] * n,
        scratch_shapes=[pltpu.SemaphoreType.DMA((6 * n,)), pltpu.SemaphoreType.DMA((6 * n,)),
                        pltpu.SemaphoreType.DMA((n,))],
        name="weight_gather",
    )(*shards)


def _pair_swap(gps):
    n = len(gps)

    def body(*refs):
        g_refs, o_refs = refs[:n], refs[n:2 * n]
        send_sems, recv_sems = refs[2 * n:]
        x, y, c = _mesh_pos()
        copies = []
        for t in range(n):
            half = gps[t].shape[1] // 2
            theirs = pl.ds(pl.multiple_of((1 - c) * half, 8), half)
            cp = pltpu.make_async_remote_copy(src_ref=g_refs[t].at[:, theirs], dst_ref=o_refs[t],
                                              send_sem=send_sems.at[t], recv_sem=recv_sems.at[t],
                                              device_id=(x, y, 1 - c), device_id_type=MESH)
            cp.start()
            copies.append(cp)
        for cp in copies:
            cp.wait_send()
            cp.wait_recv()

    return pl.pallas_call(
        body,
        out_shape=[jax.ShapeDtypeStruct((4, g.shape[1] // 2, g.shape[2]), g.dtype) for g in gps],
        in_specs=[_ANY] * n,
        out_specs=[_ANY] * n,
        scratch_shapes=[pltpu.SemaphoreType.DMA((n,)), pltpu.SemaphoreType.DMA((n,))],
        name="grad_pair_swap",
    )(*gps)


def _chip_scatter(parts):
    n = len(parts)

    def body(*refs):
        s_refs, o_refs = refs[:n], refs[n:2 * n]
        send_sems, recv_sems, local_sems = refs[2 * n:]
        x, y, c = _mesh_pos()
        me = 2 * x + y
        chips = _other_chips(x, y)
        local, sends = [], []
        for t in range(n):
            own = pltpu.make_async_copy(s_refs[t].at[me], o_refs[t].at[me], local_sems.at[t])
            own.start()
            local.append(own)
            for k, (px, py) in enumerate(chips):
                cp = pltpu.make_async_remote_copy(src_ref=s_refs[t].at[2 * px + py], dst_ref=o_refs[t].at[me],
                                                  send_sem=send_sems.at[3 * t + k], recv_sem=recv_sems.at[3 * t + k],
                                                  device_id=(px, py, c), device_id_type=MESH)
                cp.start()
                sends.append(cp)
        for t in range(n):
            for k, (px, py) in enumerate(chips):
                pltpu.make_async_remote_copy(src_ref=s_refs[t].at[me], dst_ref=o_refs[t].at[2 * px + py],
                                             send_sem=send_sems.at[3 * t + k], recv_sem=recv_sems.at[3 * t + k],
                                             device_id=(x, y, c), device_id_type=MESH).wait_recv()
        for cp in sends:
            cp.wait_send()
        for own in local:
            own.wait()

    return pl.pallas_call(
        body,
        out_shape=[jax.ShapeDtypeStruct(p.shape, p.dtype) for p in parts],
        in_specs=[_ANY] * n,
        out_specs=[_ANY] * n,
        scratch_shapes=[pltpu.SemaphoreType.DMA((3 * n,)), pltpu.SemaphoreType.DMA((3 * n,)),
                        pltpu.SemaphoreType.DMA((n,))],
        name="grad_chip_scatter",
    )(*parts)


def _pair_join(fs):
    n = len(fs)

    def body(*refs):
        f_refs, o_refs = refs[:n], refs[n:2 * n]
        send_sems, recv_sems, local_sems = refs[2 * n:]
        x, y, c = _mesh_pos()
        local, sends = [], []
        for t in range(n):
            own = pltpu.make_async_copy(f_refs[t], o_refs[t].at[c], local_sems.at[t])
            own.start()
            local.append(own)
            cp = pltpu.make_async_remote_copy(src_ref=f_refs[t], dst_ref=o_refs[t].at[c], send_sem=send_sems.at[t],
                                              recv_sem=recv_sems.at[t], device_id=(x, y, 1 - c), device_id_type=MESH)
            cp.start()
            sends.append(cp)
        for t in range(n):
            pltpu.make_async_remote_copy(src_ref=f_refs[t], dst_ref=o_refs[t].at[1 - c], send_sem=send_sems.at[t],
                                         recv_sem=recv_sems.at[t], device_id=(x, y, c), device_id_type=MESH).wait_recv()
        for cp in sends:
            cp.wait_send()
        for own in local:
            own.wait()

    return pl.pallas_call(
        body,
        out_shape=[jax.ShapeDtypeStruct((2,) + f.shape, f.dtype) for f in fs],
        in_specs=[_ANY] * n,
        out_specs=[_ANY] * n,
        scratch_shapes=[pltpu.SemaphoreType.DMA((n,)), pltpu.SemaphoreType.DMA((n,)), pltpu.SemaphoreType.DMA((n,))],
        name="grad_pair_join",
    )(*fs)


def _rope_tables(Lp):
    inv = 1.0 / (ROPE_BASE ** (jnp.arange(0, ROPE, 2, dtype=F32) / ROPE))
    ang = (jnp.arange(Lp, dtype=F32) - FRONT)[:, None] * inv[None, :]
    cs, sn = jnp.cos(ang), jnp.sin(ang)
    return jnp.tile(cs, (1, 4)), jnp.concatenate([-sn, sn, -sn, sn], axis=1)


def _local_step(x, loss_target, meta, norm_g, w_in, gate_w, gate_b, gla_norm_g, gla_proj, q_norm_g, w_uq,
                kv_norm_g, w_ukv, mla_proj, w_out, final_norm_g):
    B, seq, _ = x.shape
    Lp = HEAD_ROWS + seq
    T = B * Lp
    tr = _div_tile(Lp, 544, 16)
    tq = _div_tile(T, 1024, QB)

    cuts = np.cumsum((0,) + SPLITS)
    col = lambda i: w_in[:, cuts[i]:cuts[i + 1]]
    w_q, w_k, w_v, w_lr, w_z, w_cq, w_ckv, w_kr, w_mz, w_gg, w_gm = [col(i) for i in range(11)]
    pad_cols = lambda w, n: jnp.pad(w, ((0, 0), (0, n - w.shape[1])))
    wA = jnp.concatenate([w_v, w_z, w_mz, w_gg, w_gm, w_q, w_k], axis=1)
    wB = jnp.concatenate([w_cq, w_ckv, pad_cols(w_lr, 128), pad_cols(w_kr, 128)], axis=1)
    wg = jnp.pad(gate_w, ((0, 128 - GLA_RANK), (0, 0)))
    wuq2 = jnp.pad(w_uq.reshape(Q_RANK, MLA_H, MLA_QK), ((0, 0), (0, 0), (0, 256 - MLA_QK))).reshape(Q_RANK, 2048)
    gn4 = jnp.tile(gla_norm_g, (1, GLA_H))
    cos_t, sin_t = _rope_tables(Lp)

    u = _rms_in(x, meta, norm_g, B, Lp)
    projA = _mm(u, wA, name="in_proj_a", tm=tq, tn=1024, tk=D)
    projB = _mm(u, wB, name="in_proj_b", tm=tq, tn=640, tk=D)
    oa, ya_in, ssave = _gla_fwd(projA, projB, wg, gate_b, gn4, B, Lp)
    ya = _mm(ya_in, gla_proj, name="gla_proj", tm=tq, tn=512, tk=D)
    q_att, k_att, v_att, cqn, ckvn = _mla_prep(projB, cos_t, sin_t, q_norm_g, kv_norm_g, wuq2, w_ukv, B, Lp, tr)
    ob, yb_in, lse_r, lse_c = _attn_fwd(q_att, k_att, v_att, projA, B, Lp)
    yb = _mm(yb_in, mla_proj, name="mla_proj", tm=tq, tn=512, tk=D)
    merged = _merge_fwd(projA, ya, yb, tr)
    mo = _mm(merged, w_out, name="w_out", tm=tq, tn=512, tk=D)
    dh1, dh1_b, loss, d_gf = _final_loss(x, meta, mo, final_norm_g.reshape(1, D), loss_target, B, Lp)

    dmerged = _mm(dh1_b, w_out, name="d_merged", trans_b=True, tm=tq, tn=512, tk=D)
    g_w_out = _mm(merged, dh1_b, name="dw_out", trans_a=True, tm=D, tn=512, tk=tq)
    dya, dyb, dA = _merge_bwd(dmerged, projA, ya, yb, tr)
    dya_in = _mm(dya, gla_proj, name="d_ya_in", trans_b=True, tm=tq, tn=512, tk=D)
    g_gla_proj = _mm(ya_in, dya, name="dw_gla_proj", trans_a=True, tm=D, tn=512, tk=tq)
    dyb_in = _mm(dyb, mla_proj, name="d_yb_in", trans_b=True, tm=tq, tn=512, tk=D)
    g_mla_proj = _mm(yb_in, dyb, name="dw_mla_proj", trans_a=True, tm=D, tn=512, tk=tq)
    doa, dBz, d_gn = _gla_out_bwd(dya_in, oa, projA, gn4, tr)
    dC, g_wg, d_bg = _gla_bwd(projA, projB, ssave, doa, wg, gate_b, B, Lp)
    do, dDz, delta_r, delta_c = _attn_bwd_pre(dyb_in, projA, ob, B, Lp)
    dq = _attn_bwd_dq(q_att, k_att, v_att, do, lse_r, delta_r, B, Lp)
    dk, dv = _attn_bwd_dkv(q_att, k_att, v_att, do, lse_c, delta_c, B, Lp)
    dqf, dkvf, dE, d_gq, d_gkv = _mla_bwd_post(dq, dk, dv, projB, cos_t, sin_t, q_norm_g, kv_norm_g,
                                                wuq2, w_ukv, B, Lp, tr)
    g_wuq2 = _mm(cqn, dqf, name="dw_uq", trans_a=True, tm=Q_RANK, tn=512, tk=tq)
    g_wukv = _mm(ckvn, dkvf, name="dw_ukv", trans_a=True, tm=KV_RANK, tn=512, tk=tq)
    dparts = [dA, dBz, dC, dDz, dE]
    g_in = [_mm(u, dp, name="dw_in_%d" % i, trans_a=True, tm=D, tn=_div_tile(dp.shape[1], 1024, 256), tk=tq)
            for i, dp in enumerate(dparts)]
    du = _du_matmul(dA, dBz, dC, dDz, dE, wA, wB, QB)
    grad_x, d_meta, d_ng = _in_norm_bwd(x, meta, dh1, du, norm_g, B, Lp)

    gA, gBz, gC, gDz, gE = g_in
    g_w_in = jnp.concatenate([
        gC[:, 1024:1536], gC[:, 1536:2048], gC[:, 0:1024], gC[:, 2048:2048 + GLA_RANK], gBz,
        gE[:, 0:Q_RANK], gE[:, Q_RANK:Q_RANK + KV_RANK], gE[:, 384:384 + ROPE], gDz, gA[:, 0:D], gA[:, D:2 * D]], axis=1)
    g_wuq = g_wuq2.reshape(Q_RANK, MLA_H, 256)[:, :, :MLA_QK].reshape(Q_RANK, MLA_H * MLA_QK)
    grads = dict(w_in=g_w_in, gla_gate_w=g_wg[:GLA_RANK], gla_proj=g_gla_proj, mla_w_uq=g_wuq, mla_w_ukv=g_wukv,
                 mla_proj=g_mla_proj, w_out=g_w_out, meta_tokens=d_meta, norm_g=d_ng, gla_gate_b=d_bg,
                 gla_norm_g=d_gn, mla_q_norm_g=d_gq, mla_kv_norm_g=d_gkv, final_norm_g=d_gf)
    return loss[0, 0], grad_x, grads


_MATS = ("w_in", "gla_gate_w", "gla_proj", "mla_w_uq", "mla_w_ukv", "mla_proj", "w_out")
_ROW_SHARDED = ("gla_proj", "mla_proj", "w_out")
_ORDER = ("meta_tokens", "norm_g", "w_in", "gla_gate_w", "gla_gate_b", "gla_norm_g", "gla_proj", "mla_q_norm_g",
          "mla_w_uq", "mla_kv_norm_g", "mla_w_ukv", "mla_proj", "w_out", "final_norm_g")
WIRE_BF16_MIN_ELEMS = 128 * 128
SMALL_PACK_ROWS = 16


def _pack_small(d):
    rows = [jnp.pad(d[n].reshape(1, size), ((0, 0), (0, D - size))) for n, size in SMALL]
    return jnp.pad(jnp.concatenate(rows, axis=0), ((0, SMALL_PACK_ROWS - len(rows)), (0, 0)))


def _unpack_small(packed):
    return {n: packed[i, :size] for i, (n, size) in enumerate(SMALL)}


def kernel(x, meta_tokens, norm_g, w_in, gla_gate_w, gla_gate_b, gla_norm_g, gla_proj, mla_q_norm_g, mla_w_uq, mla_kv_norm_g, mla_w_ukv, mla_proj, w_out, final_norm_g, loss_target, m_meta_tokens, m_norm_g, m_w_in, m_gla_gate_w, m_gla_gate_b, m_gla_norm_g, m_gla_proj, m_mla_q_norm_g, m_mla_w_uq, m_mla_kv_norm_g, m_mla_w_ukv, m_mla_proj, m_w_out, m_final_norm_g, v_meta_tokens, v_norm_g, v_w_in, v_gla_gate_w, v_gla_gate_b, v_gla_norm_g, v_gla_proj, v_mla_q_norm_g, v_mla_w_uq, v_mla_kv_norm_g, v_mla_w_ukv, v_mla_proj, v_w_out, v_final_norm_g):
    w = dict(meta_tokens=meta_tokens, norm_g=norm_g, w_in=w_in[0], gla_gate_w=gla_gate_w[0], gla_gate_b=gla_gate_b,
             gla_norm_g=gla_norm_g, gla_proj=gla_proj[0], mla_q_norm_g=mla_q_norm_g, mla_w_uq=mla_w_uq[0],
             mla_kv_norm_g=mla_kv_norm_g, mla_w_ukv=mla_w_ukv[0], mla_proj=mla_proj[0], w_out=w_out[0],
             final_norm_g=final_norm_g)
    mom = dict(meta_tokens=m_meta_tokens, norm_g=m_norm_g, w_in=m_w_in[0], gla_gate_w=m_gla_gate_w[0],
               gla_gate_b=m_gla_gate_b, gla_norm_g=m_gla_norm_g, gla_proj=m_gla_proj[0], mla_q_norm_g=m_mla_q_norm_g,
               mla_w_uq=m_mla_w_uq[0], mla_kv_norm_g=m_mla_kv_norm_g, mla_w_ukv=m_mla_w_ukv[0], mla_proj=m_mla_proj[0],
               w_out=m_w_out[0], final_norm_g=m_final_norm_g)
    var = dict(meta_tokens=v_meta_tokens, norm_g=v_norm_g, w_in=v_w_in[0], gla_gate_w=v_gla_gate_w[0],
               gla_gate_b=v_gla_gate_b, gla_norm_g=v_gla_norm_g, gla_proj=v_gla_proj[0], mla_q_norm_g=v_mla_q_norm_g,
               mla_w_uq=v_mla_w_uq[0], mla_kv_norm_g=v_mla_kv_norm_g, mla_w_ukv=v_mla_w_ukv[0], mla_proj=v_mla_proj[0],
               w_out=v_w_out[0], final_norm_g=v_final_norm_g)
    out_shapes = {n: a.shape for n, a in zip(_ORDER, (meta_tokens, norm_g, w_in, gla_gate_w, gla_gate_b, gla_norm_g,
                                                     gla_proj, mla_q_norm_g, mla_w_uq, mla_kv_norm_g, mla_w_ukv,
                                                     mla_proj, w_out, final_norm_g))}

    gathered = _weight_gather([w[n].astype(BF16) for n in _MATS] + [meta_tokens])
    full = {}
    for name, gth in zip(_MATS, gathered):
        if name in _ROW_SHARDED:
            full[name] = gth.reshape(4 * gth.shape[1], gth.shape[2])
        else:
            full[name] = gth.transpose(1, 0, 2).reshape(gth.shape[1], 4 * gth.shape[2])
    meta_full = gathered[-1].transpose(1, 0, 2).reshape(N_META, D)

    loss_local, grad_x, g = _local_step(
        x, loss_target, meta_full, norm_g, full["w_in"], full["gla_gate_w"], gla_gate_b, gla_norm_g, full["gla_proj"],
        mla_q_norm_g, full["mla_w_uq"], mla_kv_norm_g, full["mla_w_ukv"], full["mla_proj"], full["w_out"], final_norm_g)
    loss = lax.psum(loss_local, ("x", "y", "c"))

    def by_owner(name, arr):
        if name in _ROW_SHARDED:
            return arr.reshape(4, arr.shape[0] // 4, arr.shape[1])
        return arr.reshape(arr.shape[0], 4, arr.shape[1] // 4).transpose(1, 0, 2)

    names = _MATS + ("meta_tokens",)
    gps = [by_owner(n, g[n]) for n in names] + [jnp.broadcast_to(_pack_small(g)[None], (4, SMALL_PACK_ROWS, D))]
    recvs = _pair_swap(gps)
    c_idx = lax.axis_index("c").astype(jnp.int32).reshape(1)
    s1 = [_pair_add_big(gps[0], recvs[0], c_idx)] + list(_pair_add_small(gps[1:], recvs[1:]))
    landed = _chip_scatter(s1)
    halves = [_sum_chips_big(landed[0])] + list(_sum_chips_small(landed[1:]))
    g_red = [j.reshape(2 * j.shape[1], j.shape[2]) for j in _pair_join(halves)]

    tens = lambda d: [d[n].reshape(g_red[i].shape) for i, n in enumerate(names)] + [_pack_small(d)]
    w_t, m_t, v_t = tens(w), tens(mom), tens(var)
    big = _adamw_big(w_t[0], g_red[0], m_t[0], v_t[0])
    rest = _adamw_small(w_t[1:], g_red[1:], m_t[1:], v_t[1:])
    k = len(names)
    results = {"grad": g_red}
    for i, kind in enumerate(("delta", "new_m", "new_v")):
        results[kind] = [big[i]] + list(rest[i * k:(i + 1) * k])

    outs = []
    for kind in ("grad", "delta", "new_m", "new_v"):
        vals = dict(zip(names, results[kind][:-1]))
        vals.update(_unpack_small(results[kind][-1]))
        outs += [vals[n].reshape(out_shapes[n]) for n in _ORDER]
    return (loss, grad_x, *outs)
```

```python
import functools
import math

import jax
import jax.numpy as jnp
import numpy as np
from jax import lax
from jax.experimental import pallas as pl
from jax.experimental.pallas import tpu as pltpu

F32 = jnp.float32
BF16 = jnp.bfloat16

D = 1024
N_META = 16
QB = 256
FRONT = QB - N_META
HEAD_ROWS = FRONT + N_META
assert FRONT % 64 == 48
EPS = 1e-6

GLA_H, GLA_DK, GLA_DV, GLA_RANK, GLA_C = 4, 128, 256, 16, 64
GLA_NORMALIZER = 16.0
GLA_KW, GLA_VW = GLA_H * GLA_DK, GLA_H * GLA_DV
MLA_H, NOPE, ROPE, MLA_DV, Q_RANK, KV_RANK = 8, 128, 64, 128, 256, 128
MLA_QK = NOPE + ROPE
ROPE_BASE = 10000.0
SPLITS = (GLA_KW, GLA_KW, GLA_VW, GLA_RANK, GLA_VW, Q_RANK, KV_RANK, ROPE, MLA_H * MLA_DV, D, D)
IN_WIDTH = sum(SPLITS)

ADAM_LR, ADAM_B1, ADAM_B2, ADAM_EPS, ADAM_WD, ADAM_STEP = 0.001, 0.9, 0.999, 1e-08, 0.01, 10

LANES = 128
VMEM_CAP_V7X = 56 * 1024 * 1024
MESH = pl.DeviceIdType.MESH
NEG = -1e30

SMALL = (("norm_g", D), ("gla_gate_b", GLA_KW), ("gla_norm_g", GLA_DV), ("mla_q_norm_g", Q_RANK),
         ("mla_kv_norm_g", KV_RANK), ("final_norm_g", D))


def _div_tile(n, target, mult):
    best = None
    for d in range(mult, min(n, target) + 1, mult):
        if n % d == 0:
            best = d
    assert best is not None, (n, target, mult)
    return best


def _params(sem, block_bytes, scratch_bytes=0):
    est = 2 * block_bytes + scratch_bytes + 12 * 1024 * 1024
    return pltpu.CompilerParams(dimension_semantics=sem, vmem_limit_bytes=int(min(max(est, 24 * 1024 * 1024), VMEM_CAP_V7X)))


def _nbytes(shape, dtype):
    return int(np.prod(shape)) * jnp.dtype(dtype).itemsize


def _sigmoid(x):
    return 1.0 / (1.0 + jnp.exp(-x))


def _nt(a, b):
    return lax.dot_general(a, b, (((1,), (1,)), ((), ())), preferred_element_type=F32)


def _tn(a, b):
    return lax.dot_general(a, b, (((0,), (0,)), ((), ())), preferred_element_type=F32)


def _nn(a, b):
    return jnp.dot(a, b, preferred_element_type=F32)


def _split3(x):
    a = x.astype(BF16)
    r = x - a.astype(F32)
    b = r.astype(BF16)
    c = (r - b.astype(F32)).astype(BF16)
    return a, b, c


def _mm(a, b, *, name, trans_a=False, trans_b=False, out_dtype=F32, tm, tn, tk):
    assert not (trans_a and trans_b)
    if trans_a:
        K, M = a.shape
    else:
        M, K = a.shape
    N = b.shape[0] if trans_b else b.shape[1]
    assert (b.shape[1] if trans_b else b.shape[0]) == K
    assert M % tm == 0 and N % tn == 0 and K % tk == 0, (name, M, N, K, tm, tn, tk)
    nk = K // tk

    def body(a_ref, b_ref, o_ref, *scratch):
        av = a_ref[...].astype(BF16)
        bv = b_ref[...].astype(BF16)
        prod = _tn(av, bv) if trans_a else (_nt(av, bv) if trans_b else _nn(av, bv))
        if nk == 1:
            o_ref[...] = prod.astype(out_dtype)
        else:
            acc = scratch[0]
            k = pl.program_id(2)

            @pl.when(k == 0)
            def _():
                acc[...] = prod

            @pl.when(k > 0)
            def _():
                acc[...] += prod

            @pl.when(k == nk - 1)
            def _():
                o_ref[...] = acc[...].astype(out_dtype)

    if trans_a:
        a_spec = pl.BlockSpec((tk, tm), lambda i, j, k: (k, i))
    else:
        a_spec = pl.BlockSpec((tm, tk), lambda i, j, k: (i, k))
    if trans_b:
        b_spec = pl.BlockSpec((tn, tk), lambda i, j, k: (j, k))
    else:
        b_spec = pl.BlockSpec((tk, tn), lambda i, j, k: (k, j))
    blocks = (_nbytes((tm, tk), a.dtype) + _nbytes((tk, tn), b.dtype) + _nbytes((tm, tn), out_dtype))
    scratch = [pltpu.VMEM((tm, tn), F32)] if nk > 1 else []
    return pl.pallas_call(
        body,
        out_shape=jax.ShapeDtypeStruct((M, N), out_dtype),
        grid=(M // tm, N // tn, nk),
        in_specs=[a_spec, b_spec],
        out_specs=pl.BlockSpec((tm, tn), lambda i, j, k: (i, j)),
        scratch_shapes=scratch,
        compiler_params=_params(("parallel", "parallel", "arbitrary"), blocks + _nbytes((tm, tn), F32),
                                _nbytes((tm, tn), F32) if nk > 1 else 0),
        name=name,
    )(a, b)


def _h_tile(j, x_ref, meta_ref):
    head = jnp.concatenate([jnp.zeros((FRONT, D), F32), meta_ref[...]], axis=0)
    return jnp.where(j > 0, x_ref[0], head)


def _x_spec():
    return pl.BlockSpec((1, QB, D), lambda b, j: (b, jnp.maximum(j - 1, 0), 0))


def _rms_in(x, meta, g, B, Lp):
    T = B * Lp
    NQ = Lp // QB

    def body(x_ref, meta_ref, g_ref, u_ref):
        h = _h_tile(pl.program_id(1), x_ref, meta_ref)
        r = lax.rsqrt(jnp.mean(h * h, axis=-1, keepdims=True) + EPS)
        u_ref[...] = (h * r * g_ref[...]).astype(BF16)

    return pl.pallas_call(
        body,
        out_shape=jax.ShapeDtypeStruct((T, D), BF16),
        grid=(B, NQ),
        in_specs=[_x_spec(), pl.BlockSpec((N_META, D), lambda b, j: (0, 0)), pl.BlockSpec((1, D), lambda b, j: (0, 0))],
        out_specs=pl.BlockSpec((QB, D), lambda b, j: (b * NQ + j, 0)),
        compiler_params=_params(("parallel", "parallel"), _nbytes((QB, D), F32) * 2),
        name="rms_in",
    )(x, meta, g)


def _gla_gate(lr, wg, bg, valid):
    pre = _nn(lr.astype(BF16), wg) + bg
    logsig = jnp.minimum(pre, 0.0) - jnp.log(1.0 + jnp.exp(-jnp.abs(pre)))
    return pre, jnp.where(valid, logsig / GLA_NORMALIZER, 0.0)


def _tri_masks():
    ri = lax.broadcasted_iota(jnp.int32, (GLA_C, GLA_C), 0)
    ci = lax.broadcasted_iota(jnp.int32, (GLA_C, GLA_C), 1)
    return ci <= ri, ci >= ri


def _cumsum_rows(x, ones_mask):
    w = jnp.where(ones_mask, 1.0, 0.0).astype(BF16)
    a, b, c = _split3(x)
    return _nn(w, a) + _nn(w, b) + _nn(w, c)


def _gla_fwd(projA, projB, wg, bg, gn4, B, Lp):
    T = B * Lp
    NC = Lp // GLA_C
    C = GLA_C
    scale = GLA_DK ** -0.5

    def body(q_ref, k_ref, v_ref, lr_ref, z_ref, wg_ref, bg_ref, gn_ref, oa_ref, ya_ref, ssave_ref, st_ref):
        n = pl.program_id(0)

        @pl.when(n == 0)
        def _():
            st_ref[...] = jnp.zeros_like(st_ref)

        pos = n * C + lax.broadcasted_iota(jnp.int32, (C, 1), 0)
        lower, _ = _tri_masks()
        is_last = lax.broadcasted_iota(jnp.int32, (C, 1), 0) == C - 1
        for b in range(B):
            ssave_ref[b, 0] = st_ref[b]
            _, glog = _gla_gate(lr_ref[b], wg_ref[...], bg_ref[...], pos >= FRONT)
            bcum = _cumsum_rows(glog, lower)
            for h in range(GLA_H):
                ks = slice(h * GLA_DK, (h + 1) * GLA_DK)
                vs = slice(h * GLA_DV, (h + 1) * GLA_DV)
                bh = bcum[:, ks]
                blast = jnp.sum(jnp.where(is_last, bh, 0.0), axis=0, keepdims=True)
                qh = q_ref[b, :, ks] * scale
                kh = k_ref[b, :, ks]
                qe = (qh * jnp.exp(bh)).astype(BF16)
                ke = (kh * jnp.exp(-bh)).astype(BF16)
                kl = (kh * jnp.exp(blast - bh)).astype(BF16)
                vh = v_ref[b, :, vs].astype(BF16)
                a = jnp.where(lower, _nt(qe, ke), 0.0).astype(BF16)
                st = st_ref[b, h]
                o = _nn(a, vh) + _nt(qe, st.astype(BF16))
                st_ref[b, h] = st * jnp.exp(blast) + _tn(vh, kl)
                oa_ref[b, :, vs] = o
                on = o * lax.rsqrt(jnp.mean(o * o, axis=-1, keepdims=True) + EPS) * gn_ref[:, vs]
                z = z_ref[b, :, vs]
                ya_ref[b, :, vs] = (on * (z * _sigmoid(z))).astype(BF16)

    blocks = B * (_nbytes((C, 512), F32) * 2 + _nbytes((C, 1024), F32) * 3 + _nbytes((C, 1024), BF16)
                  + _nbytes((GLA_H, GLA_DV, GLA_DK), F32)) + _nbytes((128, 512), BF16)
    state = _nbytes((B, GLA_H, GLA_DV, GLA_DK), F32)
    pa = projA.reshape(B, Lp, projA.shape[1])
    oa, ya, ssave = pl.pallas_call(
        body,
        out_shape=(jax.ShapeDtypeStruct((B, Lp, GLA_VW), F32), jax.ShapeDtypeStruct((B, Lp, GLA_VW), BF16),
                   jax.ShapeDtypeStruct((B, NC, GLA_H, GLA_DV, GLA_DK), F32)),
        grid=(NC,),
        in_specs=[
            pl.BlockSpec((B, C, 512), lambda n: (0, n, 10)),
            pl.BlockSpec((B, C, 512), lambda n: (0, n, 11)),
            pl.BlockSpec((B, C, 1024), lambda n: (0, n, 0)),
            pl.BlockSpec((B, C, 128), lambda n: (0, n, 3)),
            pl.BlockSpec((B, C, 1024), lambda n: (0, n, 1)),
            pl.BlockSpec((128, 512), lambda n: (0, 0)),
            pl.BlockSpec((1, 512), lambda n: (0, 0)),
            pl.BlockSpec((1, 1024), lambda n: (0, 0)),
        ],
        out_specs=(pl.BlockSpec((B, C, 1024), lambda n: (0, n, 0)),
                   pl.BlockSpec((B, C, 1024), lambda n: (0, n, 0)),
                   pl.BlockSpec((B, 1, GLA_H, GLA_DV, GLA_DK), lambda n: (0, n, 0, 0, 0))),
        scratch_shapes=[pltpu.VMEM((B, GLA_H, GLA_DV, GLA_DK), F32)],
        compiler_params=_params(("arbitrary",), blocks, state),
        name="gla_fwd",
    )(pa, pa, pa, projB.reshape(B, Lp, projB.shape[1]), pa, wg, bg, gn4)
    return oa.reshape(T, GLA_VW), ya.reshape(T, GLA_VW), ssave


def _swap_halves(x):
    lane = lax.broadcasted_iota(jnp.int32, x.shape, 1)
    return jnp.where((lane % 64) < 32, pltpu.roll(x, 96, 1), pltpu.roll(x, 32, 1))


def _mla_prep(projB, cos_t, sin_t, gq, gkv, wuq2, wukv, B, Lp, tr):
    T = B * Lp
    nt = Lp // tr
    HW = 2 * LANES

    def body(pb_ref, cos_ref, sin_ref, gq_ref, gkv_ref, wuq_ref, wukv_ref, q_ref, k_ref, v_ref, cqn_ref, ckvn_ref):
        cq = pb_ref[:, 0:Q_RANK]
        ckv = pb_ref[:, Q_RANK:Q_RANK + KV_RANK]
        kr = pb_ref[:, 512:640]
        cqn = (cq * lax.rsqrt(jnp.mean(cq * cq, axis=-1, keepdims=True) + EPS) * gq_ref[...]).astype(BF16)
        ckvn = (ckv * lax.rsqrt(jnp.mean(ckv * ckv, axis=-1, keepdims=True) + EPS) * gkv_ref[...]).astype(BF16)
        cqn_ref[...] = cqn
        ckvn_ref[...] = ckvn
        qf = _nn(cqn, wuq_ref[...])
        kvf = _nn(ckvn, wukv_ref[...])
        cs = cos_ref[...]
        sn = sin_ref[...]
        rope = lambda t: t * cs + _swap_halves(t) * sn
        kr_r = rope(kr).astype(BF16)
        for h in range(MLA_H):
            q_ref[:, h * HW:h * HW + LANES] = qf[:, h * HW:h * HW + LANES].astype(BF16)
            q_ref[:, h * HW + LANES:(h + 1) * HW] = rope(qf[:, h * HW + LANES:(h + 1) * HW]).astype(BF16)
            k_ref[:, h * HW:h * HW + LANES] = kvf[:, h * HW:h * HW + LANES].astype(BF16)
            k_ref[:, h * HW + LANES:(h + 1) * HW] = kr_r
            v_ref[:, h * MLA_DV:(h + 1) * MLA_DV] = kvf[:, h * HW + LANES:(h + 1) * HW].astype(BF16)

    blocks = (_nbytes((tr, 640), F32) + 2 * _nbytes((tr, 128), F32) + _nbytes((Q_RANK, 2048), BF16)
              + _nbytes((KV_RANK, 2048), BF16) + _nbytes((tr, 2048 * 2 + 1024 + 384), BF16)
              + 2 * _nbytes((tr, 2048), F32))
    return pl.pallas_call(
        body,
        out_shape=(jax.ShapeDtypeStruct((T, MLA_H * HW), BF16), jax.ShapeDtypeStruct((T, MLA_H * HW), BF16),
                   jax.ShapeDtypeStruct((T, MLA_H * MLA_DV), BF16), jax.ShapeDtypeStruct((T, Q_RANK), BF16),
                   jax.ShapeDtypeStruct((T, KV_RANK), BF16)),
        grid=(B, nt),
        in_specs=[
            pl.BlockSpec((tr, 640), lambda b, j: (b * nt + j, 0)),
            pl.BlockSpec((tr, 128), lambda b, j: (j, 0)),
            pl.BlockSpec((tr, 128), lambda b, j: (j, 0)),
            pl.BlockSpec((1, Q_RANK), lambda b, j: (0, 0)),
            pl.BlockSpec((1, KV_RANK), lambda b, j: (0, 0)),
            pl.BlockSpec((Q_RANK, 2048), lambda b, j: (0, 0)),
            pl.BlockSpec((KV_RANK, 2048), lambda b, j: (0, 0)),
        ],
        out_specs=(pl.BlockSpec((tr, 2048), lambda b, j: (b * nt + j, 0)),
                   pl.BlockSpec((tr, 2048), lambda b, j: (b * nt + j, 0)),
                   pl.BlockSpec((tr, 1024), lambda b, j: (b * nt + j, 0)),
                   pl.BlockSpec((tr, Q_RANK), lambda b, j: (b * nt + j, 0)),
                   pl.BlockSpec((tr, KV_RANK), lambda b, j: (b * nt + j, 0))),
        compiler_params=_params(("parallel", "parallel"), blocks),
        name="mla_prep",
    )(projB, cos_t, sin_t, gq, gkv, wuq2, wukv)


def _attn_mask(row, col):
    return (col <= row) & ((col >= FRONT) | (row < FRONT))


def _attn_fwd(q_att, k_att, v_att, projA, B, Lp):
    T = B * Lp
    NQ = Lp // QB
    HW = 2 * LANES
    scale = 1.0 / math.sqrt(MLA_QK)

    def body(q_ref, k_ref, v_ref, mz_ref, o_ref, yb_ref, lser_ref, lsec_ref, m_ref, l_ref, acc_ref):
        qi = pl.program_id(1)
        m_ref[...] = jnp.full(m_ref.shape, NEG, F32)
        l_ref[...] = jnp.zeros_like(l_ref)
        acc_ref[...] = jnp.zeros_like(acc_ref)
        row = qi * QB + lax.broadcasted_iota(jnp.int32, (QB, QB), 0)
        coli = lax.broadcasted_iota(jnp.int32, (QB, QB), 1)

        def step(kj, carry):
            off = pl.multiple_of(kj * QB, QB)
            ok = _attn_mask(row, kj * QB + coli)
            for h in range(MLA_H):
                q = q_ref[:, h * HW:(h + 1) * HW]
                kb = k_ref[pl.ds(off, QB), h * HW:(h + 1) * HW]
                vb = v_ref[pl.ds(off, QB), h * MLA_DV:(h + 1) * MLA_DV]
                s = jnp.where(ok, _nt(q, kb) * scale, NEG)
                m_old = m_ref[h]
                m_new = jnp.maximum(m_old, jnp.max(s, axis=-1, keepdims=True))
                alpha = jnp.exp(m_old - m_new)
                p = jnp.exp(s - jnp.tile(m_new, (1, QB // LANES)))
                m_ref[h] = m_new
                l_ref[h] = alpha * l_ref[h] + jnp.sum(p, axis=-1, keepdims=True)
                acc_ref[h] = alpha * acc_ref[h] + _nn(p.astype(BF16), vb)
            return carry

        lax.fori_loop(0, qi + 1, step, 0)
        for h in range(MLA_H):
            hs = slice(h * MLA_DV, (h + 1) * MLA_DV)
            l = l_ref[h]
            o = acc_ref[h] / l
            o_ref[:, hs] = o
            z = mz_ref[:, hs]
            yb_ref[:, hs] = (o * (z * _sigmoid(z))).astype(BF16)
            lse = m_ref[h] + jnp.log(l)
            lser_ref[0, h] = lse
            lsec_ref[0, h, pl.ds(qi, 1), :] = jnp.transpose(lse)[0:1, :]

    blocks = (_nbytes((QB, 2048), BF16) + _nbytes((Lp, 2048), BF16) + _nbytes((Lp, 1024), BF16)
              + 2 * _nbytes((QB, 1024), F32) + _nbytes((QB, 1024), BF16) + _nbytes((MLA_H, QB, LANES), F32)
              + _nbytes((MLA_H, NQ, QB), F32))
    return pl.pallas_call(
        body,
        out_shape=(jax.ShapeDtypeStruct((T, MLA_H * MLA_DV), F32), jax.ShapeDtypeStruct((T, MLA_H * MLA_DV), BF16),
                   jax.ShapeDtypeStruct((B, MLA_H, Lp, LANES), F32), jax.ShapeDtypeStruct((B, MLA_H, NQ, QB), F32)),
        grid=(B, NQ),
        in_specs=[
            pl.BlockSpec((QB, MLA_H * HW), lambda b, i: (b * NQ + i, 0)),
            pl.BlockSpec((Lp, MLA_H * HW), lambda b, i: (b, 0)),
            pl.BlockSpec((Lp, MLA_H * MLA_DV), lambda b, i: (b, 0)),
            pl.BlockSpec((QB, 1024), lambda b, i: (b * NQ + i, 2)),
        ],
        out_specs=(pl.BlockSpec((QB, 1024), lambda b, i: (b * NQ + i, 0)),
                   pl.BlockSpec((QB, 1024), lambda b, i: (b * NQ + i, 0)),
                   pl.BlockSpec((1, MLA_H, QB, LANES), lambda b, i: (b, 0, i, 0)),
                   pl.BlockSpec((1, MLA_H, NQ, QB), lambda b, i: (b, 0, 0, 0))),
        scratch_shapes=[pltpu.VMEM((MLA_H, QB, LANES), F32), pltpu.VMEM((MLA_H, QB, LANES), F32),
                        pltpu.VMEM((MLA_H, QB, MLA_DV), F32)],
        compiler_params=_params(("parallel", "arbitrary"), blocks, 3 * _nbytes((MLA_H, QB, LANES), F32)),
        name="attn_fwd",
    )(q_att, k_att, v_att, projA)


def _merge_fwd(projA, ya, yb, tr):
    T = ya.shape[0]

    def body(gg_ref, gm_ref, ya_ref, yb_ref, o_ref):
        o_ref[...] = (_sigmoid(gg_ref[...]) * ya_ref[...] + _sigmoid(gm_ref[...]) * yb_ref[...]).astype(BF16)

    spec = lambda c: pl.BlockSpec((tr, D), lambda i: (i, c))
    return pl.pallas_call(
        body,
        out_shape=jax.ShapeDtypeStruct((T, D), BF16),
        grid=(T // tr,),
        in_specs=[spec(3), spec(4), spec(0), spec(0)],
        out_specs=spec(0),
        compiler_params=_params(("parallel",), 5 * _nbytes((tr, D), F32)),
        name="merge_fwd",
    )(projA, projA, ya, yb)


def _final_loss(x, meta, mo, gf, tgt, B, Lp):
    T = B * Lp
    NQ = Lp // QB

    def body(x_ref, meta_ref, mo_ref, gf_ref, t_ref, dh_ref, dhb_ref, loss_ref, dgf_ref):
        b = pl.program_id(0)
        j = pl.program_id(1)

        @pl.when((b == 0) & (j == 0))
        def _():
            loss_ref[...] = jnp.zeros_like(loss_ref)
            dgf_ref[...] = jnp.zeros_like(dgf_ref)

        h1 = _h_tile(j, x_ref, meta_ref) + mo_ref[...]
        r = lax.rsqrt(jnp.mean(h1 * h1, axis=-1, keepdims=True) + EPS)
        hn = h1 * r
        gfv = gf_ref[...]
        diff = jnp.where(j > 0, hn * gfv - t_ref[0], 0.0)
        loss_ref[...] += (0.5 / D) * jnp.sum(jnp.sum(diff * diff, axis=-1, keepdims=True), axis=0, keepdims=True)
        dout = diff * (1.0 / D)
        dgf_ref[...] += jnp.sum(dout * hn, axis=0, keepdims=True)
        dhn = dout * gfv
        dh = r * (dhn - hn * jnp.mean(dhn * hn, axis=-1, keepdims=True))
        dh_ref[...] = dh
        dhb_ref[...] = dh.astype(BF16)

    rows = pl.BlockSpec((QB, D), lambda b, j: (b * NQ + j, 0))
    return pl.pallas_call(
        body,
        out_shape=(jax.ShapeDtypeStruct((T, D), F32), jax.ShapeDtypeStruct((T, D), BF16),
                   jax.ShapeDtypeStruct((1, 1), F32), jax.ShapeDtypeStruct((1, D), F32)),
        grid=(B, NQ),
        in_specs=[_x_spec(), pl.BlockSpec((N_META, D), lambda b, j: (0, 0)), rows,
                  pl.BlockSpec((1, D), lambda b, j: (0, 0)), _x_spec()],
        out_specs=(rows, rows, pl.BlockSpec((1, 1), lambda b, j: (0, 0)), pl.BlockSpec((1, D), lambda b, j: (0, 0))),
        compiler_params=_params(("arbitrary", "arbitrary"), 5 * _nbytes((QB, D), F32)),
        name="final_loss",
    )(x, meta, mo, gf, tgt)


def _merge_bwd(dm, projA, ya, yb, tr):
    T = dm.shape[0]

    def body(dm_ref, gg_ref, gm_ref, ya_ref, yb_ref, dya_ref, dyb_ref, da_ref):
        d = dm_ref[...]
        sg = _sigmoid(gg_ref[...])
        sm = _sigmoid(gm_ref[...])
        dya_ref[...] = (d * sg).astype(BF16)
        dyb_ref[...] = (d * sm).astype(BF16)
        da_ref[:, 0:D] = (d * ya_ref[...] * (sg * (1.0 - sg))).astype(BF16)
        da_ref[:, D:2 * D] = (d * yb_ref[...] * (sm * (1.0 - sm))).astype(BF16)

    spec = lambda c: pl.BlockSpec((tr, D), lambda i: (i, c))
    return pl.pallas_call(
        body,
        out_shape=(jax.ShapeDtypeStruct((T, D), BF16), jax.ShapeDtypeStruct((T, D), BF16),
                   jax.ShapeDtypeStruct((T, 2 * D), BF16)),
        grid=(T // tr,),
        in_specs=[spec(0), spec(3), spec(4), spec(0), spec(0)],
        out_specs=(spec(0), spec(0), pl.BlockSpec((tr, 2 * D), lambda i: (i, 0))),
        compiler_params=_params(("parallel",), 8 * _nbytes((tr, D), F32)),
        name="merge_bwd",
    )(dm, projA, projA, ya, yb)


def _gla_out_bwd(dyin, oa, projA, gn4, tr):
    T = dyin.shape[0]
    nsteps = T // tr

    def body(dy_ref, oa_ref, z_ref, gn_ref, do_ref, dz_ref, dgn_ref, acc_ref):
        i = pl.program_id(0)

        @pl.when(i == 0)
        def _():
            acc_ref[...] = jnp.zeros_like(acc_ref)

        for h in range(GLA_H):
            vs = slice(h * GLA_DV, (h + 1) * GLA_DV)
            dy = dy_ref[:, vs]
            o = oa_ref[:, vs]
            z = z_ref[:, vs]
            gn = gn_ref[:, vs]
            s = _sigmoid(z)
            ra = lax.rsqrt(jnp.mean(o * o, axis=-1, keepdims=True) + EPS)
            on = o * ra
            don = dy * (z * s)
            t = don * gn
            do_ref[:, vs] = (ra * (t - on * jnp.mean(t * on, axis=-1, keepdims=True))).astype(BF16)
            dz_ref[:, vs] = (dy * (on * gn) * (s * (1.0 + z * (1.0 - s)))).astype(BF16)
            acc_ref[:, vs] += jnp.sum(don * on, axis=0, keepdims=True)

        @pl.when(i == nsteps - 1)
        def _():
            a = acc_ref[...]
            dgn_ref[...] = a[:, 0:256] + a[:, 256:512] + a[:, 512:768] + a[:, 768:1024]

    spec = lambda c: pl.BlockSpec((tr, D), lambda i: (i, c))
    return pl.pallas_call(
        body,
        out_shape=(jax.ShapeDtypeStruct((T, D), BF16), jax.ShapeDtypeStruct((T, D), BF16),
                   jax.ShapeDtypeStruct((1, GLA_DV), F32)),
        grid=(nsteps,),
        in_specs=[spec(0), spec(0), spec(1), pl.BlockSpec((1, D), lambda i: (0, 0))],
        out_specs=(spec(0), spec(0), pl.BlockSpec((1, GLA_DV), lambda i: (0, 0))),
        scratch_shapes=[pltpu.VMEM((1, D), F32)],
        compiler_params=_params(("arbitrary",), 6 * _nbytes((tr, D), F32)),
        name="gla_out_bwd",
    )(dyin, oa, projA, gn4)


def _gla_bwd(projA, projB, ssave, doa, wg, bg, B, Lp):
    T = B * Lp
    NC = Lp // GLA_C
    C = GLA_C
    scale = GLA_DK ** -0.5
    WC = 2304

    def body(q_ref, k_ref, v_ref, lr_ref, ss_ref, do_ref, wg_ref, bg_ref, dc_ref, dwg_ref, dbg_ref, dst_ref):
        i = pl.program_id(0)
        n = NC - 1 - i

        @pl.when(i == 0)
        def _():
            dst_ref[...] = jnp.zeros_like(dst_ref)
            dwg_ref[...] = jnp.zeros_like(dwg_ref)
            dbg_ref[...] = jnp.zeros_like(dbg_ref)

        pos = n * C + lax.broadcasted_iota(jnp.int32, (C, 1), 0)
        valid = pos >= FRONT
        lower, upper = _tri_masks()
        is_last = lax.broadcasted_iota(jnp.int32, (C, 1), 0) == C - 1
        for b in range(B):
            lr = lr_ref[b]
            pre, glog = _gla_gate(lr, wg_ref[...], bg_ref[...], valid)
            bcum = _cumsum_rows(glog, lower)
            db_parts = []
            for h in range(GLA_H):
                ks = slice(h * GLA_DK, (h + 1) * GLA_DK)
                vs = slice(h * GLA_DV, (h + 1) * GLA_DV)
                bh = bcum[:, ks]
                blast = jnp.sum(jnp.where(is_last, bh, 0.0), axis=0, keepdims=True)
                eb, enb, ekl, ebl = jnp.exp(bh), jnp.exp(-bh), jnp.exp(blast - bh), jnp.exp(blast)
                qh = q_ref[b, :, ks] * scale
                kh = k_ref[b, :, ks]
                qe_f, ke_f, kl_f = qh * eb, kh * enb, kh * ekl
                qe, ke, kl = qe_f.astype(BF16), ke_f.astype(BF16), kl_f.astype(BF16)
                vh = v_ref[b, :, vs].astype(BF16)
                doh = do_ref[b, :, vs]
                st = ss_ref[b, 0, h]
                dst = dst_ref[b, h]
                st_b, dst_b = st.astype(BF16), dst.astype(BF16)
                da = jnp.where(lower, _nt(doh, vh), 0.0).astype(BF16)
                da_t = jnp.where(upper, _nt(vh, doh), 0.0).astype(BF16)
                a_t = jnp.where(upper, _nt(ke, qe), 0.0).astype(BF16)
                dqe = _nn(da, ke) + _nn(doh, st_b)
                dke = _nn(da_t, qe)
                dvh = _nn(a_t, doh) + _nt(kl, dst_b)
                dkl = _nn(vh, dst_b)
                dst_ref[b, h] = dst * ebl + _tn(doh, qe)
                deb = jnp.sum(st * dst, axis=0, keepdims=True)
                db = dqe * qe_f - dke * ke_f - dkl * kl_f
                db_last = jnp.sum(dkl * kl_f, axis=0, keepdims=True) + deb * ebl
                db_parts.append(db + jnp.where(is_last, db_last, 0.0))
                dc_ref[b, :, vs] = dvh.astype(BF16)
                dc_ref[b, :, 1024 + h * GLA_DK:1024 + (h + 1) * GLA_DK] = (dqe * eb * scale).astype(BF16)
                dc_ref[b, :, 1536 + h * GLA_DK:1536 + (h + 1) * GLA_DK] = (dke * enb + dkl * ekl).astype(BF16)
            dglog = _cumsum_rows(jnp.concatenate(db_parts, axis=1), upper)
            dpre = jnp.where(valid, dglog * (1.0 / GLA_NORMALIZER) / (1.0 + jnp.exp(pre)), 0.0)
            dpre_b = dpre.astype(BF16)
            dc_ref[b, :, 2048:2176] = _nt(dpre_b, wg_ref[...]).astype(BF16)
            dc_ref[b, :, 2176:2304] = jnp.zeros((C, 128), BF16)
            dwg_ref[...] += _tn(lr.astype(BF16), dpre_b)
            dbg_ref[...] += jnp.sum(dpre, axis=0, keepdims=True)

    blocks = B * (_nbytes((C, 512), F32) * 2 + _nbytes((C, 1024), F32) + _nbytes((C, 1024), BF16)
                  + _nbytes((GLA_H, GLA_DV, GLA_DK), F32) + _nbytes((C, WC), BF16)) + 3 * _nbytes((128, 512), F32)
    state = _nbytes((B, GLA_H, GLA_DV, GLA_DK), F32)
    pa = projA.reshape(B, Lp, projA.shape[1])
    rev = lambda i: NC - 1 - i
    dc, dwg, dbg = pl.pallas_call(
        body,
        out_shape=(jax.ShapeDtypeStruct((B, Lp, WC), BF16), jax.ShapeDtypeStruct((128, GLA_KW), F32),
                   jax.ShapeDtypeStruct((1, GLA_KW), F32)),
        grid=(NC,),
        in_specs=[
            pl.BlockSpec((B, C, 512), lambda i: (0, rev(i), 10)),
            pl.BlockSpec((B, C, 512), lambda i: (0, rev(i), 11)),
            pl.BlockSpec((B, C, 1024), lambda i: (0, rev(i), 0)),
            pl.BlockSpec((B, C, 128), lambda i: (0, rev(i), 3)),
            pl.BlockSpec((B, 1, GLA_H, GLA_DV, GLA_DK), lambda i: (0, rev(i), 0, 0, 0)),
            pl.BlockSpec((B, C, 1024), lambda i: (0, rev(i), 0)),
            pl.BlockSpec((128, 512), lambda i: (0, 0)),
            pl.BlockSpec((1, 512), lambda i: (0, 0)),
        ],
        out_specs=(pl.BlockSpec((B, C, WC), lambda i: (0, rev(i), 0)),
                   pl.BlockSpec((128, GLA_KW), lambda i: (0, 0)),
                   pl.BlockSpec((1, GLA_KW), lambda i: (0, 0))),
        scratch_shapes=[pltpu.VMEM((B, GLA_H, GLA_DV, GLA_DK), F32)],
        compiler_params=_params(("arbitrary",), blocks, state),
        name="gla_bwd",
    )(pa, pa, pa, projB.reshape(B, Lp, projB.shape[1]), ssave, doa.reshape(B, Lp, GLA_VW), wg, bg)
    return dc.reshape(T, WC), dwg, dbg


def _attn_bwd_pre(dyin, projA, ob, B, Lp):
    T = B * Lp
    NQ = Lp // QB

    def body(dy_ref, z_ref, o_ref, do_ref, dz_ref, dr_ref, dcol_ref):
        j = pl.program_id(1)
        for h in range(MLA_H):
            hs = slice(h * MLA_DV, (h + 1) * MLA_DV)
            dy = dy_ref[:, hs]
            z = z_ref[:, hs]
            o = o_ref[:, hs]
            s = _sigmoid(z)
            do = dy * (z * s)
            do_ref[:, hs] = do.astype(BF16)
            dz_ref[:, hs] = (dy * o * (s * (1.0 + z * (1.0 - s)))).astype(BF16)
            dl = jnp.broadcast_to(jnp.sum(do * o, axis=-1, keepdims=True), (QB, LANES))
            dr_ref[0, h] = dl
            dcol_ref[0, h, pl.ds(j, 1), :] = jnp.transpose(dl)[0:1, :]

    rows = lambda c: pl.BlockSpec((QB, D), lambda b, j: (b * NQ + j, c))
    return pl.pallas_call(
        body,
        out_shape=(jax.ShapeDtypeStruct((T, D), BF16), jax.ShapeDtypeStruct((T, D), BF16),
                   jax.ShapeDtypeStruct((B, MLA_H, Lp, LANES), F32), jax.ShapeDtypeStruct((B, MLA_H, NQ, QB), F32)),
        grid=(B, NQ),
        in_specs=[rows(0), rows(2), rows(0)],
        out_specs=(rows(0), rows(0), pl.BlockSpec((1, MLA_H, QB, LANES), lambda b, j: (b, 0, j, 0)),
                   pl.BlockSpec((1, MLA_H, NQ, QB), lambda b, j: (b, 0, 0, 0))),
        compiler_params=_params(("parallel", "arbitrary"), 6 * _nbytes((QB, D), F32)),
        name="attn_bwd_pre",
    )(dyin, projA, ob)


def _attn_bwd_dq(q_att, k_att, v_att, do, lse_r, delta_r, B, Lp):
    T = B * Lp
    NQ = Lp // QB
    scale = 1.0 / math.sqrt(MLA_QK)

    HW = 2 * LANES

    def body(q_ref, k_ref, v_ref, do_ref, lse_ref, dl_ref, dq_ref):
        qi = pl.program_id(1)
        dq_ref[...] = jnp.zeros_like(dq_ref)
        row = qi * QB + lax.broadcasted_iota(jnp.int32, (QB, QB), 0)
        coli = lax.broadcasted_iota(jnp.int32, (QB, QB), 1)

        def step(kj, carry):
            off = pl.multiple_of(kj * QB, QB)
            ok = _attn_mask(row, kj * QB + coli)
            for h in range(MLA_H):
                ws = slice(h * HW, (h + 1) * HW)
                hs = slice(h * MLA_DV, (h + 1) * MLA_DV)
                kb = k_ref[pl.ds(off, QB), ws]
                vb = v_ref[pl.ds(off, QB), hs]
                lse = jnp.tile(lse_ref[0, h], (1, QB // LANES))
                delta = jnp.tile(dl_ref[0, h], (1, QB // LANES))
                s = _nt(q_ref[:, ws], kb) * scale
                p = jnp.where(ok, jnp.exp(s - lse), 0.0)
                ds = p * (_nt(do_ref[:, hs], vb) - delta) * scale
                dq_ref[:, ws] += _nn(ds.astype(BF16), kb)
            return carry

        lax.fori_loop(0, qi + 1, step, 0)

    blocks = (_nbytes((QB, 2048), BF16) + _nbytes((Lp, 2048), BF16) + _nbytes((Lp, 1024), BF16)
              + _nbytes((QB, 1024), BF16) + 2 * _nbytes((MLA_H, QB, LANES), F32) + _nbytes((QB, 2048), F32))
    return pl.pallas_call(
        body,
        out_shape=jax.ShapeDtypeStruct((T, MLA_H * HW), F32),
        grid=(B, NQ),
        in_specs=[
            pl.BlockSpec((QB, MLA_H * HW), lambda b, i: (b * NQ + i, 0)),
            pl.BlockSpec((Lp, MLA_H * HW), lambda b, i: (b, 0)),
            pl.BlockSpec((Lp, MLA_H * MLA_DV), lambda b, i: (b, 0)),
            pl.BlockSpec((QB, MLA_H * MLA_DV), lambda b, i: (b * NQ + i, 0)),
            pl.BlockSpec((1, MLA_H, QB, LANES), lambda b, i: (b, 0, i, 0)),
            pl.BlockSpec((1, MLA_H, QB, LANES), lambda b, i: (b, 0, i, 0)),
        ],
        out_specs=pl.BlockSpec((QB, MLA_H * HW), lambda b, i: (b * NQ + i, 0)),
        compiler_params=_params(("parallel", "parallel"), blocks),
        name="attn_bwd_dq",
    )(q_att, k_att, v_att, do, lse_r, delta_r)


def _attn_bwd_dkv(q_att, k_att, v_att, do, lse_c, delta_c, B, Lp):
    T = B * Lp
    NQ = Lp // QB
    scale = 1.0 / math.sqrt(MLA_QK)

    HW = 2 * LANES

    def body(q_ref, k_ref, v_ref, do_ref, lse_ref, dl_ref, dk_ref, dv_ref):
        kj = pl.program_id(1)
        dk_ref[...] = jnp.zeros_like(dk_ref)
        dv_ref[...] = jnp.zeros_like(dv_ref)
        col = kj * QB + lax.broadcasted_iota(jnp.int32, (QB, QB), 0)
        rowi = lax.broadcasted_iota(jnp.int32, (QB, QB), 1)

        def step(qi, carry):
            off = pl.multiple_of(qi * QB, QB)
            ok = _attn_mask(qi * QB + rowi, col)
            for h in range(MLA_H):
                ws = slice(h * HW, (h + 1) * HW)
                hs = slice(h * MLA_DV, (h + 1) * MLA_DV)
                qb = q_ref[pl.ds(off, QB), ws]
                dob = do_ref[pl.ds(off, QB), hs]
                lse = lse_ref[0, h, pl.ds(qi, 1), :]
                delta = dl_ref[0, h, pl.ds(qi, 1), :]
                s_t = _nt(k_ref[:, ws], qb) * scale
                p_t = jnp.where(ok, jnp.exp(s_t - lse), 0.0)
                dv_ref[:, hs] += _nn(p_t.astype(BF16), dob)
                ds_t = p_t * (_nt(v_ref[:, hs], dob) - delta) * scale
                dk_ref[:, ws] += _nn(ds_t.astype(BF16), qb)
            return carry

        lax.fori_loop(kj, NQ, step, 0)

    blocks = (_nbytes((Lp, 2048), BF16) + _nbytes((Lp, 1024), BF16) + _nbytes((QB, 3072), BF16)
              + 2 * _nbytes((MLA_H, NQ, QB), F32) + _nbytes((QB, 3072), F32))
    return pl.pallas_call(
        body,
        out_shape=(jax.ShapeDtypeStruct((T, MLA_H * HW), F32), jax.ShapeDtypeStruct((T, MLA_H * MLA_DV), F32)),
        grid=(B, NQ),
        in_specs=[
            pl.BlockSpec((Lp, MLA_H * HW), lambda b, j: (b, 0)),
            pl.BlockSpec((QB, MLA_H * HW), lambda b, j: (b * NQ + j, 0)),
            pl.BlockSpec((QB, MLA_H * MLA_DV), lambda b, j: (b * NQ + j, 0)),
            pl.BlockSpec((Lp, MLA_H * MLA_DV), lambda b, j: (b, 0)),
            pl.BlockSpec((1, MLA_H, NQ, QB), lambda b, j: (b, 0, 0, 0)),
            pl.BlockSpec((1, MLA_H, NQ, QB), lambda b, j: (b, 0, 0, 0)),
        ],
        out_specs=(pl.BlockSpec((QB, MLA_H * HW), lambda b, j: (b * NQ + j, 0)),
                   pl.BlockSpec((QB, MLA_H * MLA_DV), lambda b, j: (b * NQ + j, 0))),
        compiler_params=_params(("parallel", "parallel"), blocks),
        name="attn_bwd_dkv",
    )(q_att, k_att, v_att, do, lse_c, delta_c)


def _mla_bwd_post(dq, dk, dv, projB, cos_t, sin_t, gq, gkv, wuq2, wukv, B, Lp, tr):
    T = B * Lp
    nt = Lp // tr
    HW = 2 * LANES

    def body(dq_ref, dk_ref, dv_ref, pb_ref, cos_ref, sin_ref, gq_ref, gkv_ref, wuq_ref, wukv_ref,
             dqf_ref, dkvf_ref, de_ref, dgq_ref, dgkv_ref):
        first = (pl.program_id(0) == 0) & (pl.program_id(1) == 0)

        @pl.when(first)
        def _():
            dgq_ref[...] = jnp.zeros_like(dgq_ref)
            dgkv_ref[...] = jnp.zeros_like(dgkv_ref)

        cs = cos_ref[...]
        sn = sin_ref[...]
        rope_t = lambda t: t * cs + _swap_halves(t * sn)
        dkr = jnp.zeros((tr, LANES), F32)
        for h in range(MLA_H):
            dqf_ref[:, h * HW:h * HW + LANES] = dq_ref[:, h * HW:h * HW + LANES].astype(BF16)
            dqf_ref[:, h * HW + LANES:(h + 1) * HW] = rope_t(dq_ref[:, h * HW + LANES:(h + 1) * HW]).astype(BF16)
            dkvf_ref[:, h * HW:h * HW + LANES] = dk_ref[:, h * HW:h * HW + LANES].astype(BF16)
            dkvf_ref[:, h * HW + LANES:(h + 1) * HW] = dv_ref[:, h * MLA_DV:(h + 1) * MLA_DV].astype(BF16)
            dkr = dkr + dk_ref[:, h * HW + LANES:(h + 1) * HW]

        def norm_bwd(x, dn, g):
            r = lax.rsqrt(jnp.mean(x * x, axis=-1, keepdims=True) + EPS)
            xn = x * r
            t = dn * g
            return r * (t - xn * jnp.mean(t * xn, axis=-1, keepdims=True)), jnp.sum(dn * xn, axis=0, keepdims=True)

        dcq, dgq = norm_bwd(pb_ref[:, 0:Q_RANK], _nt(dqf_ref[...], wuq_ref[...]), gq_ref[...])
        dckv, dgkv = norm_bwd(pb_ref[:, Q_RANK:Q_RANK + KV_RANK], _nt(dkvf_ref[...], wukv_ref[...]), gkv_ref[...])
        dgq_ref[...] += dgq
        dgkv_ref[...] += dgkv
        de_ref[:, 0:Q_RANK] = dcq.astype(BF16)
        de_ref[:, Q_RANK:Q_RANK + KV_RANK] = dckv.astype(BF16)
        de_ref[:, 384:512] = rope_t(dkr).astype(BF16)

    rows = lambda w: pl.BlockSpec((tr, w), lambda b, j: (b * nt + j, 0))
    const = lambda s: pl.BlockSpec(s, lambda b, j: (0, 0))
    blocks = (2 * _nbytes((tr, 2048), F32) + _nbytes((tr, 1024), F32) + _nbytes((tr, 640), F32)
              + 2 * _nbytes((tr, 2048), BF16) + _nbytes((2048, 384), BF16) + 2 * _nbytes((tr, 2048), F32))
    return pl.pallas_call(
        body,
        out_shape=(jax.ShapeDtypeStruct((T, 2048), BF16), jax.ShapeDtypeStruct((T, 2048), BF16),
                   jax.ShapeDtypeStruct((T, 512), BF16), jax.ShapeDtypeStruct((1, Q_RANK), F32),
                   jax.ShapeDtypeStruct((1, KV_RANK), F32)),
        grid=(B, nt),
        in_specs=[rows(2048), rows(2048), rows(1024), rows(640),
                  pl.BlockSpec((tr, 128), lambda b, j: (j, 0)), pl.BlockSpec((tr, 128), lambda b, j: (j, 0)),
                  const((1, Q_RANK)), const((1, KV_RANK)), const((Q_RANK, 2048)), const((KV_RANK, 2048))],
        out_specs=(rows(2048), rows(2048), rows(512), const((1, Q_RANK)), const((1, KV_RANK))),
        compiler_params=_params(("arbitrary", "arbitrary"), blocks),
        name="mla_bwd_post",
    )(dq, dk, dv, projB, cos_t, sin_t, gq, gkv, wuq2, wukv)


def _du_matmul(dA, dBz, dC, dDz, dE, wA, wB, tm):
    T = dA.shape[0]

    def body(da_ref, db_ref, dc_ref, dd_ref, de_ref, wa_ref, wb_ref, o_ref):
        acc = _nt(da_ref[...], wa_ref[:, 3072:5120])
        acc = acc + _nt(db_ref[...], wa_ref[:, 1024:2048])
        acc = acc + _nt(dd_ref[...], wa_ref[:, 2048:3072])
        acc = acc + _nt(dc_ref[:, 0:1024], wa_ref[:, 0:1024])
        acc = acc + _nt(dc_ref[:, 1024:2048], wa_ref[:, 5120:6144])
        acc = acc + _nt(dc_ref[:, 2048:2176], wb_ref[:, 384:512])
        acc = acc + _nt(de_ref[:, 0:384], wb_ref[:, 0:384])
        acc = acc + _nt(de_ref[:, 384:512], wb_ref[:, 512:640])
        o_ref[...] = acc

    widths = [a.shape[1] for a in (dA, dBz, dC, dDz, dE)]
    blocks = (sum(_nbytes((tm, w), BF16) for w in widths) + _nbytes(wA.shape, BF16) + _nbytes(wB.shape, BF16)
              + _nbytes((tm, D), F32))
    return pl.pallas_call(
        body,
        out_shape=jax.ShapeDtypeStruct((T, D), F32),
        grid=(T // tm,),
        in_specs=[pl.BlockSpec((tm, w), lambda i: (i, 0)) for w in widths]
        + [pl.BlockSpec(wA.shape, lambda i: (0, 0)), pl.BlockSpec(wB.shape, lambda i: (0, 0))],
        out_specs=pl.BlockSpec((tm, D), lambda i: (i, 0)),
        compiler_params=_params(("parallel",), blocks),
        name="du_matmul",
    )(dA, dBz, dC, dDz, dE, wA, wB)


def _in_norm_bwd(x, meta, dh1, du, g, B, Lp):
    NQ = Lp // QB
    seq = x.shape[1]

    def body(x_ref, meta_ref, dh_ref, du_ref, g_ref, gx_ref, dmeta_ref, dg_ref):
        b = pl.program_id(0)
        j = pl.program_id(1)

        @pl.when((b == 0) & (j == 0))
        def _():
            dg_ref[...] = jnp.zeros_like(dg_ref)

        x = _h_tile(j, x_ref, meta_ref)
        r = lax.rsqrt(jnp.mean(x * x, axis=-1, keepdims=True) + EPS)
        xn = x * r
        du = du_ref[...]
        t = du * g_ref[...]
        dh0 = dh_ref[...] + r * (t - xn * jnp.mean(t * xn, axis=-1, keepdims=True))
        dg_ref[...] += jnp.sum(du * xn, axis=0, keepdims=True)
        gx_ref[0] = dh0

        @pl.when((j == 0) & (b == 0))
        def _():
            dmeta_ref[...] = dh0[FRONT:HEAD_ROWS, :]

        @pl.when((j == 0) & (b > 0))
        def _():
            dmeta_ref[...] += dh0[FRONT:HEAD_ROWS, :]

    rows = pl.BlockSpec((QB, D), lambda b, j: (b * NQ + j, 0))
    return pl.pallas_call(
        body,
        out_shape=(jax.ShapeDtypeStruct((B, seq, D), F32), jax.ShapeDtypeStruct((N_META, D), F32),
                   jax.ShapeDtypeStruct((1, D), F32)),
        grid=(B, NQ),
        in_specs=[_x_spec(), pl.BlockSpec((N_META, D), lambda b, j: (0, 0)), rows, rows,
                  pl.BlockSpec((1, D), lambda b, j: (0, 0))],
        out_specs=(_x_spec(), pl.BlockSpec((N_META, D), lambda b, j: (0, 0)), pl.BlockSpec((1, D), lambda b, j: (0, 0))),
        compiler_params=_params(("arbitrary", "arbitrary"), 5 * _nbytes((QB, D), F32)),
        name="in_norm_bwd",
    )(x, meta, dh1, du, g)


_VMEM_WHOLE = pl.BlockSpec(memory_space=pltpu.VMEM)


def _params_whole(arrays):
    total = sum(_nbytes(a.shape, a.dtype) for a in arrays)
    return pltpu.CompilerParams(vmem_limit_bytes=int(min(total + 12 * 1024 * 1024, VMEM_CAP_V7X)))


def _wire_dtype(shape):
    return BF16 if shape[-2] * shape[-1] >= WIRE_BF16_MIN_ELEMS else F32


def _pair_add_big(gp, recv, c):
    _, half, cols = recv.shape
    th = _div_tile(half, 64, 16)
    out_dtype = _wire_dtype(recv.shape)

    def body(c_ref, a_ref, b_ref, o_ref):
        o_ref[...] = (a_ref[:, 0] + b_ref[...]).astype(out_dtype)

    return pl.pallas_call(
        body,
        out_shape=jax.ShapeDtypeStruct(recv.shape, out_dtype),
        grid_spec=pltpu.PrefetchScalarGridSpec(
            num_scalar_prefetch=1,
            grid=(half // th,),
            in_specs=[pl.BlockSpec((4, 1, th, cols), lambda i, c_ref: (0, c_ref[0], i, 0)),
                      pl.BlockSpec((4, th, cols), lambda i, c_ref: (0, i, 0))],
            out_specs=pl.BlockSpec((4, th, cols), lambda i, c_ref: (0, i, 0)),
        ),
        compiler_params=_params(("parallel",), 3 * _nbytes((4, th, cols), F32)),
        name="grad_pair_add_big",
    )(c, gp.reshape(4, 2, half, cols), recv)


def _pair_add_small(gps, recvs):
    n = len(gps)

    def body(*refs):
        c = lax.axis_index("c")
        for t in range(n):
            g_ref, r_ref, o_ref = refs[t], refs[n + t], refs[2 * n + t]
            half = r_ref.shape[1]
            s = g_ref[:, pl.ds(pl.multiple_of(c * half, 8), half), :] + r_ref[...]
            o_ref[...] = s.astype(o_ref.dtype)

    return pl.pallas_call(
        body,
        out_shape=[jax.ShapeDtypeStruct(r.shape, _wire_dtype(r.shape)) for r in recvs],
        in_specs=[_VMEM_WHOLE] * (2 * n),
        out_specs=[_VMEM_WHOLE] * n,
        compiler_params=_params_whole(list(gps) + 2 * list(recvs)),
        name="grad_pair_add_small",
    )(*gps, *recvs)


def _chip_order_sum(landed_ref, own_ref, me):
    p = [jnp.where(me == k, own_ref[k], landed_ref[k]).astype(F32) for k in range(4)]
    return ((p[0] + p[1]) + p[2]) + p[3]


def _sum_chips_big(landed, own, pos):
    _, half, cols = landed.shape
    th = _div_tile(half, 64, 16)

    def body(pos_ref, l_ref, s_ref, o_ref):
        o_ref[0] = _chip_order_sum(l_ref, s_ref, pos_ref[1])

    spec = pl.BlockSpec((4, th, cols), lambda i, pos_ref: (0, i, 0))
    return pl.pallas_call(
        body,
        out_shape=jax.ShapeDtypeStruct((2, half, cols), F32),
        grid_spec=pltpu.PrefetchScalarGridSpec(
            num_scalar_prefetch=1,
            grid=(half // th,),
            in_specs=[spec, spec],
            out_specs=pl.BlockSpec((1, th, cols), lambda i, pos_ref: (pos_ref[0], i, 0)),
        ),
        compiler_params=_params(("parallel",), 3 * _nbytes((4, th, cols), F32)),
        name="grad_sum_chips_big",
    )(pos, landed, own)


def _sum_chips_small(landed, own):
    n = len(landed)

    def body(*refs):
        x, y, c = _mesh_pos()
        for t in range(n):
            refs[2 * n + t][c] = _chip_order_sum(refs[t], refs[n + t], 2 * x + y)

    return pl.pallas_call(
        body,
        out_shape=[jax.ShapeDtypeStruct((2,) + p.shape[1:], F32) for p in landed],
        in_specs=[_VMEM_WHOLE] * (2 * n),
        out_specs=[_VMEM_WHOLE] * n,
        compiler_params=_params_whole(list(landed) * 3),
        name="grad_sum_chips_small",
    )(*landed, *own)


def _adamw_update(w_ref, g_ref, m_ref, v_ref, d_ref, mo_ref, vo_ref):
    c1 = 1.0 - ADAM_B1 ** ADAM_STEP
    c2 = 1.0 - ADAM_B2 ** ADAM_STEP
    gv = g_ref[...]
    mn = ADAM_B1 * m_ref[...] + (1.0 - ADAM_B1) * gv
    vn = ADAM_B2 * v_ref[...] + (1.0 - ADAM_B2) * (gv * gv)
    mo_ref[...] = mn
    vo_ref[...] = vn
    d_ref[...] = -ADAM_LR * ((mn / c1) / (jnp.sqrt(vn / c2) + ADAM_EPS) + ADAM_WD * w_ref[...])


def _adamw_big(w, g, m, v):
    rows, cols = w.shape
    tr = _div_tile(rows, 128, 8)
    spec = pl.BlockSpec((tr, cols), lambda i: (i, 0))
    shp = jax.ShapeDtypeStruct((rows, cols), F32)
    return pl.pallas_call(
        functools.partial(_adamw_update),
        out_shape=(shp, shp, shp),
        grid=(rows // tr,),
        in_specs=[spec] * 4,
        out_specs=(spec, spec, spec),
        compiler_params=_params(("parallel",), 7 * _nbytes((tr, cols), F32)),
        name="adamw_big",
    )(w, g, m, v)


def _adamw_small(ws, gs, ms, vs):
    n = len(ws)

    def body(*refs):
        for t in range(n):
            _adamw_update(refs[t], refs[n + t], refs[2 * n + t], refs[3 * n + t],
                          refs[4 * n + t], refs[5 * n + t], refs[6 * n + t])

    shapes = [jax.ShapeDtypeStruct(w.shape, F32) for w in ws]
    return pl.pallas_call(
        body,
        out_shape=shapes * 3,
        in_specs=[_VMEM_WHOLE] * (4 * n),
        out_specs=[_VMEM_WHOLE] * (3 * n),
        compiler_params=_params_whole(list(ws) * 7),
        name="adamw_small",
    )(*ws, *gs, *ms, *vs)


def _mesh_pos():
    return lax.axis_index("x"), lax.axis_index("y"), lax.axis_index("c")


def _other_chips(x, y):
    return [(1 - x, y), (x, 1 - y), (1 - x, 1 - y)]


_ANY = pl.BlockSpec(memory_space=pl.ANY)


PAIR_SPLIT_MIN_ROWS = 64


def _weight_gather(shards):
    n = len(shards)
    split = [s.shape[0] >= PAIR_SPLIT_MIN_ROWS for s in shards]

    def body(*refs):
        w_refs, o_refs = refs[:n], refs[n:2 * n]
        send_sems, recv_sems = refs[2 * n:]
        x, y, c = _mesh_pos()
        me = 2 * x + y
        chips = _other_chips(x, y)

        def rows_of(t, core):
            rows = shards[t].shape[0]
            if not split[t]:
                return pl.ds(0, rows)
            return pl.ds(pl.multiple_of(core * (rows // 2), 16), rows // 2)

        def landed(t, k, slot, rows, to):
            ref = o_refs[t].at[slot, rows]
            return pltpu.make_async_remote_copy(src_ref=ref, dst_ref=ref, send_sem=send_sems.at[6 * t + k],
                                                recv_sem=recv_sems.at[6 * t + k], device_id=to, device_id_type=MESH)

        sends = []
        for t in range(n):
            mine = rows_of(t, c)
            for k, (px, py) in enumerate(chips):
                cp = pltpu.make_async_remote_copy(src_ref=w_refs[t].at[mine], dst_ref=o_refs[t].at[me, mine],
                                                  send_sem=send_sems.at[6 * t + k], recv_sem=recv_sems.at[6 * t + k],
                                                  device_id=(px, py, c), device_id_type=MESH)
                cp.start()
                sends.append(cp)
        for t in range(n):
            mine = rows_of(t, c)
            for k, (px, py) in enumerate(chips):
                landed(t, k, 2 * px + py, mine, (x, y, c)).wait_recv()
                if split[t]:
                    cp = landed(t, 3 + k, 2 * px + py, mine, (x, y, 1 - c))
                    cp.start()
                    sends.append(cp)
        for t in range(n):
            if split[t]:
                for k, (px, py) in enumerate(chips):
                    landed(t, 3 + k, 2 * px + py, rows_of(t, 1 - c), (x, y, c)).wait_recv()
        for cp in sends:
            cp.wait_send()

    return pl.pallas_call(
        body,
        out_shape=[jax.ShapeDtypeStruct((4,) + s.shape, s.dtype) for s in shards],
        in_specs=[_ANY] * n,
        out_specs=[_ANY] * n,
        scratch_shapes=[pltpu.SemaphoreType.DMA((6 * n,)), pltpu.SemaphoreType.DMA((6 * n,))],
        name="weight_gather",
    )(*shards)


def _pair_swap(gps):
    n = len(gps)

    def body(*refs):
        g_refs, o_refs = refs[:n], refs[n:2 * n]
        send_sems, recv_sems = refs[2 * n:]
        x, y, c = _mesh_pos()
        copies = []
        for t in range(n):
            half = gps[t].shape[1] // 2
            theirs = pl.ds(pl.multiple_of((1 - c) * half, 8), half)
            cp = pltpu.make_async_remote_copy(src_ref=g_refs[t].at[:, theirs], dst_ref=o_refs[t],
                                              send_sem=send_sems.at[t], recv_sem=recv_sems.at[t],
                                              device_id=(x, y, 1 - c), device_id_type=MESH)
            cp.start()
            copies.append(cp)
        for cp in copies:
            cp.wait_send()
            cp.wait_recv()

    return pl.pallas_call(
        body,
        out_shape=[jax.ShapeDtypeStruct((4, g.shape[1] // 2, g.shape[2]), g.dtype) for g in gps],
        in_specs=[_ANY] * n,
        out_specs=[_ANY] * n,
        scratch_shapes=[pltpu.SemaphoreType.DMA((n,)), pltpu.SemaphoreType.DMA((n,))],
        name="grad_pair_swap",
    )(*gps)


def _chip_scatter(parts):
    n = len(parts)

    def body(*refs):
        s_refs, o_refs = refs[:n], refs[n:2 * n]
        send_sems, recv_sems = refs[2 * n:]
        x, y, c = _mesh_pos()
        me = 2 * x + y
        chips = _other_chips(x, y)
        sends = []
        for t in range(n):
            for k, (px, py) in enumerate(chips):
                cp = pltpu.make_async_remote_copy(src_ref=s_refs[t].at[2 * px + py], dst_ref=o_refs[t].at[me],
                                                  send_sem=send_sems.at[3 * t + k], recv_sem=recv_sems.at[3 * t + k],
                                                  device_id=(px, py, c), device_id_type=MESH)
                cp.start()
                sends.append(cp)
        for t in range(n):
            for k, (px, py) in enumerate(chips):
                pltpu.make_async_remote_copy(src_ref=s_refs[t].at[me], dst_ref=o_refs[t].at[2 * px + py],
                                             send_sem=send_sems.at[3 * t + k], recv_sem=recv_sems.at[3 * t + k],
                                             device_id=(x, y, c), device_id_type=MESH).wait_recv()
        for cp in sends:
            cp.wait_send()

    return pl.pallas_call(
        body,
        out_shape=[jax.ShapeDtypeStruct(p.shape, p.dtype) for p in parts],
        in_specs=[_ANY] * n,
        out_specs=[_ANY] * n,
        scratch_shapes=[pltpu.SemaphoreType.DMA((3 * n,)), pltpu.SemaphoreType.DMA((3 * n,))],
        name="grad_chip_scatter",
    )(*parts)


def _pair_join(fs):
    n = len(fs)

    def body(*refs):
        f_refs, o_refs = refs[:n], refs[n:2 * n]
        send_sems, recv_sems = refs[2 * n:]
        x, y, c = _mesh_pos()
        sends = []
        for t in range(n):
            cp = pltpu.make_async_remote_copy(src_ref=f_refs[t].at[c], dst_ref=o_refs[t].at[c], send_sem=send_sems.at[t],
                                              recv_sem=recv_sems.at[t], device_id=(x, y, 1 - c), device_id_type=MESH)
            cp.start()
            sends.append(cp)
        for t in range(n):
            pltpu.make_async_remote_copy(src_ref=f_refs[t].at[c], dst_ref=o_refs[t].at[1 - c], send_sem=send_sems.at[t],
                                         recv_sem=recv_sems.at[t], device_id=(x, y, c), device_id_type=MESH).wait_recv()
        for cp in sends:
            cp.wait_send()

    return pl.pallas_call(
        body,
        out_shape=[jax.ShapeDtypeStruct(f.shape, f.dtype) for f in fs],
        in_specs=[_ANY] * n,
        out_specs=[_ANY] * n,
        input_output_aliases={t: t for t in range(n)},
        scratch_shapes=[pltpu.SemaphoreType.DMA((n,)), pltpu.SemaphoreType.DMA((n,))],
        name="grad_pair_join",
    )(*fs)


def _rope_tables(Lp):
    inv = 1.0 / (ROPE_BASE ** (jnp.arange(0, ROPE, 2, dtype=F32) / ROPE))
    ang = (jnp.arange(Lp, dtype=F32) - FRONT)[:, None] * inv[None, :]
    cs, sn = jnp.cos(ang), jnp.sin(ang)
    return jnp.tile(cs, (1, 4)), jnp.concatenate([-sn, sn, -sn, sn], axis=1)


def _local_step(x, loss_target, meta, norm_g, w_in, gate_w, gate_b, gla_norm_g, gla_proj, q_norm_g, w_uq,
                kv_norm_g, w_ukv, mla_proj, w_out, final_norm_g):
    B, seq, _ = x.shape
    Lp = HEAD_ROWS + seq
    T = B * Lp
    tr = _div_tile(Lp, 544, 16)
    tq = _div_tile(T, 1024, QB)

    cuts = np.cumsum((0,) + SPLITS)
    col = lambda i: w_in[:, cuts[i]:cuts[i + 1]]
    w_q, w_k, w_v, w_lr, w_z, w_cq, w_ckv, w_kr, w_mz, w_gg, w_gm = [col(i) for i in range(11)]
    pad_cols = lambda w, n: jnp.pad(w, ((0, 0), (0, n - w.shape[1])))
    wA = jnp.concatenate([w_v, w_z, w_mz, w_gg, w_gm, w_q, w_k], axis=1)
    wB = jnp.concatenate([w_cq, w_ckv, pad_cols(w_lr, 128), pad_cols(w_kr, 128)], axis=1)
    wg = jnp.pad(gate_w, ((0, 128 - GLA_RANK), (0, 0)))
    wuq2 = jnp.pad(w_uq.reshape(Q_RANK, MLA_H, MLA_QK), ((0, 0), (0, 0), (0, 256 - MLA_QK))).reshape(Q_RANK, 2048)
    gn4 = jnp.tile(gla_norm_g, (1, GLA_H))
    cos_t, sin_t = _rope_tables(Lp)

    u = _rms_in(x, meta, norm_g, B, Lp)
    projA = _mm(u, wA, name="in_proj_a", tm=tq, tn=1024, tk=D)
    projB = _mm(u, wB, name="in_proj_b", tm=tq, tn=640, tk=D)
    oa, ya_in, ssave = _gla_fwd(projA, projB, wg, gate_b, gn4, B, Lp)
    ya = _mm(ya_in, gla_proj, name="gla_proj", tm=tq, tn=512, tk=D)
    q_att, k_att, v_att, cqn, ckvn = _mla_prep(projB, cos_t, sin_t, q_norm_g, kv_norm_g, wuq2, w_ukv, B, Lp, tr)
    ob, yb_in, lse_r, lse_c = _attn_fwd(q_att, k_att, v_att, projA, B, Lp)
    yb = _mm(yb_in, mla_proj, name="mla_proj", tm=tq, tn=512, tk=D)
    merged = _merge_fwd(projA, ya, yb, tr)
    mo = _mm(merged, w_out, name="w_out", tm=tq, tn=512, tk=D)
    dh1, dh1_b, loss, d_gf = _final_loss(x, meta, mo, final_norm_g.reshape(1, D), loss_target, B, Lp)

    dmerged = _mm(dh1_b, w_out, name="d_merged", trans_b=True, tm=tq, tn=512, tk=D)
    g_w_out = _mm(merged, dh1_b, name="dw_out", trans_a=True, tm=D, tn=512, tk=tq)
    dya, dyb, dA = _merge_bwd(dmerged, projA, ya, yb, tr)
    dya_in = _mm(dya, gla_proj, name="d_ya_in", trans_b=True, tm=tq, tn=512, tk=D)
    g_gla_proj = _mm(ya_in, dya, name="dw_gla_proj", trans_a=True, tm=D, tn=512, tk=tq)
    dyb_in = _mm(dyb, mla_proj, name="d_yb_in", trans_b=True, tm=tq, tn=512, tk=D)
    g_mla_proj = _mm(yb_in, dyb, name="dw_mla_proj", trans_a=True, tm=D, tn=512, tk=tq)
    doa, dBz, d_gn = _gla_out_bwd(dya_in, oa, projA, gn4, tr)
    dC, g_wg, d_bg = _gla_bwd(projA, projB, ssave, doa, wg, gate_b, B, Lp)
    do, dDz, delta_r, delta_c = _attn_bwd_pre(dyb_in, projA, ob, B, Lp)
    dq = _attn_bwd_dq(q_att, k_att, v_att, do, lse_r, delta_r, B, Lp)
    dk, dv = _attn_bwd_dkv(q_att, k_att, v_att, do, lse_c, delta_c, B, Lp)
    dqf, dkvf, dE, d_gq, d_gkv = _mla_bwd_post(dq, dk, dv, projB, cos_t, sin_t, q_norm_g, kv_norm_g,
                                                wuq2, w_ukv, B, Lp, tr)
    g_wuq2 = _mm(cqn, dqf, name="dw_uq", trans_a=True, tm=Q_RANK, tn=512, tk=tq)
    g_wukv = _mm(ckvn, dkvf, name="dw_ukv", trans_a=True, tm=KV_RANK, tn=512, tk=tq)
    dparts = [dA, dBz, dC, dDz, dE]
    g_in = [_mm(u, dp, name="dw_in_%d" % i, trans_a=True, tm=D, tn=_div_tile(dp.shape[1], 1024, 256), tk=tq)
            for i, dp in enumerate(dparts)]
    du = _du_matmul(dA, dBz, dC, dDz, dE, wA, wB, QB)
    grad_x, d_meta, d_ng = _in_norm_bwd(x, meta, dh1, du, norm_g, B, Lp)

    gA, gBz, gC, gDz, gE = g_in
    g_w_in = jnp.concatenate([
        gC[:, 1024:1536], gC[:, 1536:2048], gC[:, 0:1024], gC[:, 2048:2048 + GLA_RANK], gBz,
        gE[:, 0:Q_RANK], gE[:, Q_RANK:Q_RANK + KV_RANK], gE[:, 384:384 + ROPE], gDz, gA[:, 0:D], gA[:, D:2 * D]], axis=1)
    g_wuq = g_wuq2.reshape(Q_RANK, MLA_H, 256)[:, :, :MLA_QK].reshape(Q_RANK, MLA_H * MLA_QK)
    grads = dict(w_in=g_w_in, gla_gate_w=g_wg[:GLA_RANK], gla_proj=g_gla_proj, mla_w_uq=g_wuq, mla_w_ukv=g_wukv,
                 mla_proj=g_mla_proj, w_out=g_w_out, meta_tokens=d_meta, norm_g=d_ng, gla_gate_b=d_bg,
                 gla_norm_g=d_gn, mla_q_norm_g=d_gq, mla_kv_norm_g=d_gkv, final_norm_g=d_gf)
    return loss[0, 0], grad_x, grads


_MATS = ("w_in", "gla_gate_w", "gla_proj", "mla_w_uq", "mla_w_ukv", "mla_proj", "w_out")
_ROW_SHARDED = ("gla_proj", "mla_proj", "w_out")
_ORDER = ("meta_tokens", "norm_g", "w_in", "gla_gate_w", "gla_gate_b", "gla_norm_g", "gla_proj", "mla_q_norm_g",
          "mla_w_uq", "mla_kv_norm_g", "mla_w_ukv", "mla_proj", "w_out", "final_norm_g")
WIRE_BF16_MIN_ELEMS = 128 * 128
SMALL_PACK_ROWS = 16


def _pack_small(d):
    rows = [jnp.pad(d[n].reshape(1, size), ((0, 0), (0, D - size))) for n, size in SMALL]
    return jnp.pad(jnp.concatenate(rows, axis=0), ((0, SMALL_PACK_ROWS - len(rows)), (0, 0)))


def _unpack_small(packed):
    return {n: packed[i, :size] for i, (n, size) in enumerate(SMALL)}


def kernel(x, meta_tokens, norm_g, w_in, gla_gate_w, gla_gate_b, gla_norm_g, gla_proj, mla_q_norm_g, mla_w_uq, mla_kv_norm_g, mla_w_ukv, mla_proj, w_out, final_norm_g, loss_target, m_meta_tokens, m_norm_g, m_w_in, m_gla_gate_w, m_gla_gate_b, m_gla_norm_g, m_gla_proj, m_mla_q_norm_g, m_mla_w_uq, m_mla_kv_norm_g, m_mla_w_ukv, m_mla_proj, m_w_out, m_final_norm_g, v_meta_tokens, v_norm_g, v_w_in, v_gla_gate_w, v_gla_gate_b, v_gla_norm_g, v_gla_proj, v_mla_q_norm_g, v_mla_w_uq, v_mla_kv_norm_g, v_mla_w_ukv, v_mla_proj, v_w_out, v_final_norm_g):
    w = dict(meta_tokens=meta_tokens, norm_g=norm_g, w_in=w_in[0], gla_gate_w=gla_gate_w[0], gla_gate_b=gla_gate_b,
             gla_norm_g=gla_norm_g, gla_proj=gla_proj[0], mla_q_norm_g=mla_q_norm_g, mla_w_uq=mla_w_uq[0],
             mla_kv_norm_g=mla_kv_norm_g, mla_w_ukv=mla_w_ukv[0], mla_proj=mla_proj[0], w_out=w_out[0],
             final_norm_g=final_norm_g)
    mom = dict(meta_tokens=m_meta_tokens, norm_g=m_norm_g, w_in=m_w_in[0], gla_gate_w=m_gla_gate_w[0],
               gla_gate_b=m_gla_gate_b, gla_norm_g=m_gla_norm_g, gla_proj=m_gla_proj[0], mla_q_norm_g=m_mla_q_norm_g,
               mla_w_uq=m_mla_w_uq[0], mla_kv_norm_g=m_mla_kv_norm_g, mla_w_ukv=m_mla_w_ukv[0], mla_proj=m_mla_proj[0],
               w_out=m_w_out[0], final_norm_g=m_final_norm_g)
    var = dict(meta_tokens=v_meta_tokens, norm_g=v_norm_g, w_in=v_w_in[0], gla_gate_w=v_gla_gate_w[0],
               gla_gate_b=v_gla_gate_b, gla_norm_g=v_gla_norm_g, gla_proj=v_gla_proj[0], mla_q_norm_g=v_mla_q_norm_g,
               mla_w_uq=v_mla_w_uq[0], mla_kv_norm_g=v_mla_kv_norm_g, mla_w_ukv=v_mla_w_ukv[0], mla_proj=v_mla_proj[0],
               w_out=v_w_out[0], final_norm_g=v_final_norm_g)
    out_shapes = {n: a.shape for n, a in zip(_ORDER, (meta_tokens, norm_g, w_in, gla_gate_w, gla_gate_b, gla_norm_g,
                                                     gla_proj, mla_q_norm_g, mla_w_uq, mla_kv_norm_g, mla_w_ukv,
                                                     mla_proj, w_out, final_norm_g))}

    me = (2 * lax.axis_index("x") + lax.axis_index("y")).astype(jnp.int32)
    shards = [w[n].astype(BF16) for n in _MATS] + [meta_tokens]
    gathered = [lax.dynamic_update_slice(gth, own[None], (me, 0, 0))
                for gth, own in zip(_weight_gather(shards), shards)]
    full = {}
    for name, gth in zip(_MATS, gathered):
        if name in _ROW_SHARDED:
            full[name] = gth.reshape(4 * gth.shape[1], gth.shape[2])
        else:
            full[name] = gth.transpose(1, 0, 2).reshape(gth.shape[1], 4 * gth.shape[2])
    meta_full = gathered[-1].transpose(1, 0, 2).reshape(N_META, D)

    loss_local, grad_x, g = _local_step(
        x, loss_target, meta_full, norm_g, full["w_in"], full["gla_gate_w"], gla_gate_b, gla_norm_g, full["gla_proj"],
        mla_q_norm_g, full["mla_w_uq"], mla_kv_norm_g, full["mla_w_ukv"], full["mla_proj"], full["w_out"], final_norm_g)
    loss = lax.psum(loss_local, ("x", "y", "c"))

    def by_owner(name, arr):
        if name in _ROW_SHARDED:
            return arr.reshape(4, arr.shape[0] // 4, arr.shape[1])
        return arr.reshape(arr.shape[0], 4, arr.shape[1] // 4).transpose(1, 0, 2)

    names = _MATS + ("meta_tokens",)
    gps = [by_owner(n, g[n]) for n in names] + [jnp.broadcast_to(_pack_small(g)[None], (4, SMALL_PACK_ROWS, D))]
    recvs = _pair_swap(gps)
    c_idx = lax.axis_index("c").astype(jnp.int32).reshape(1)
    s1 = [_pair_add_big(gps[0], recvs[0], c_idx)] + list(_pair_add_small(gps[1:], recvs[1:]))
    landed = _chip_scatter(s1)
    pos = jnp.stack([c_idx[0], me])
    halves = [_sum_chips_big(landed[0], s1[0], pos)] + list(_sum_chips_small(landed[1:], s1[1:]))
    g_red = [j.reshape(2 * j.shape[1], j.shape[2]) for j in _pair_join(halves)]

    tens = lambda d: [d[n].reshape(g_red[i].shape) for i, n in enumerate(names)] + [_pack_small(d)]
    w_t, m_t, v_t = tens(w), tens(mom), tens(var)
    big = _adamw_big(w_t[0], g_red[0], m_t[0], v_t[0])
    rest = _adamw_small(w_t[1:], g_red[1:], m_t[1:], v_t[1:])
    k = len(names)
    results = {"grad": g_red}
    for i, kind in enumerate(("delta", "new_m", "new_v")):
        results[kind] = [big[i]] + list(rest[i * k:(i + 1) * k])

    outs = []
    for kind in ("grad", "delta", "new_m", "new_v"):
        vals = dict(zip(names, results[kind][:-1]))
        vals.update(_unpack_small(results[kind][-1]))
        outs += [vals[n].reshape(out_shapes[n]) for n in _ORDER]
    return (loss, grad_x, *outs)
```

```python
import functools
import math

import jax
import jax.numpy as jnp
import numpy as np
from jax import lax
from jax.experimental import pallas as pl
from jax.experimental.pallas import tpu as pltpu

F32 = jnp.float32
BF16 = jnp.bfloat16

D = 1024
N_META = 16
QB = 256
FRONT = QB - N_META
HEAD_ROWS = FRONT + N_META
assert FRONT % 64 == 48
EPS = 1e-6

GLA_H, GLA_DK, GLA_DV, GLA_RANK, GLA_C = 4, 128, 256, 16, 64
GLA_NORMALIZER = 16.0
GLA_KW, GLA_VW = GLA_H * GLA_DK, GLA_H * GLA_DV
MLA_H, NOPE, ROPE, MLA_DV, Q_RANK, KV_RANK = 8, 128, 64, 128, 256, 128
MLA_QK = NOPE + ROPE
ROPE_BASE = 10000.0
SPLITS = (GLA_KW, GLA_KW, GLA_VW, GLA_RANK, GLA_VW, Q_RANK, KV_RANK, ROPE, MLA_H * MLA_DV, D, D)
IN_WIDTH = sum(SPLITS)

ADAM_LR, ADAM_B1, ADAM_B2, ADAM_EPS, ADAM_WD, ADAM_STEP = 0.001, 0.9, 0.999, 1e-08, 0.01, 10

LANES = 128
VMEM_CAP_V7X = 56 * 1024 * 1024
MESH = pl.DeviceIdType.MESH
NEG = -1e30

SMALL = (("norm_g", D), ("gla_gate_b", GLA_KW), ("gla_norm_g", GLA_DV), ("mla_q_norm_g", Q_RANK),
         ("mla_kv_norm_g", KV_RANK), ("final_norm_g", D))


def _div_tile(n, target, mult):
    best = None
    for d in range(mult, min(n, target) + 1, mult):
        if n % d == 0:
            best = d
    assert best is not None, (n, target, mult)
    return best


def _params(sem, block_bytes, scratch_bytes=0):
    est = 2 * block_bytes + scratch_bytes + 12 * 1024 * 1024
    return pltpu.CompilerParams(dimension_semantics=sem, vmem_limit_bytes=int(min(max(est, 24 * 1024 * 1024), VMEM_CAP_V7X)))


def _nbytes(shape, dtype):
    return int(np.prod(shape)) * jnp.dtype(dtype).itemsize


def _sigmoid(x):
    return 1.0 / (1.0 + jnp.exp(-x))


def _nt(a, b):
    return lax.dot_general(a, b, (((1,), (1,)), ((), ())), preferred_element_type=F32)


def _tn(a, b):
    return lax.dot_general(a, b, (((0,), (0,)), ((), ())), preferred_element_type=F32)


def _nn(a, b):
    return jnp.dot(a, b, preferred_element_type=F32)


def _split3(x):
    a = x.astype(BF16)
    r = x - a.astype(F32)
    b = r.astype(BF16)
    c = (r - b.astype(F32)).astype(BF16)
    return a, b, c


def _mm(a, b, *, name, trans_a=False, trans_b=False, out_dtype=F32, tm, tn, tk):
    assert not (trans_a and trans_b)
    if trans_a:
        K, M = a.shape
    else:
        M, K = a.shape
    N = b.shape[0] if trans_b else b.shape[1]
    assert (b.shape[1] if trans_b else b.shape[0]) == K
    assert M % tm == 0 and N % tn == 0 and K % tk == 0, (name, M, N, K, tm, tn, tk)
    nk = K // tk

    def body(a_ref, b_ref, o_ref, *scratch):
        av = a_ref[...].astype(BF16)
        bv = b_ref[...].astype(BF16)
        prod = _tn(av, bv) if trans_a else (_nt(av, bv) if trans_b else _nn(av, bv))
        if nk == 1:
            o_ref[...] = prod.astype(out_dtype)
        else:
            acc = scratch[0]
            k = pl.program_id(2)

            @pl.when(k == 0)
            def _():
                acc[...] = prod

            @pl.when(k > 0)
            def _():
                acc[...] += prod

            @pl.when(k == nk - 1)
            def _():
                o_ref[...] = acc[...].astype(out_dtype)

    if trans_a:
        a_spec = pl.BlockSpec((tk, tm), lambda i, j, k: (k, i))
    else:
        a_spec = pl.BlockSpec((tm, tk), lambda i, j, k: (i, k))
    if trans_b:
        b_spec = pl.BlockSpec((tn, tk), lambda i, j, k: (j, k))
    else:
        b_spec = pl.BlockSpec((tk, tn), lambda i, j, k: (k, j))
    blocks = (_nbytes((tm, tk), a.dtype) + _nbytes((tk, tn), b.dtype) + _nbytes((tm, tn), out_dtype))
    scratch = [pltpu.VMEM((tm, tn), F32)] if nk > 1 else []
    return pl.pallas_call(
        body,
        out_shape=jax.ShapeDtypeStruct((M, N), out_dtype),
        grid=(M // tm, N // tn, nk),
        in_specs=[a_spec, b_spec],
        out_specs=pl.BlockSpec((tm, tn), lambda i, j, k: (i, j)),
        scratch_shapes=scratch,
        compiler_params=_params(("parallel", "parallel", "arbitrary"), blocks + _nbytes((tm, tn), F32),
                                _nbytes((tm, tn), F32) if nk > 1 else 0),
        name=name,
    )(a, b)


def _h_tile(j, x_ref, meta_ref):
    head = jnp.concatenate([jnp.zeros((FRONT, D), F32), meta_ref[...]], axis=0)
    return jnp.where(j > 0, x_ref[0], head)


def _x_spec():
    return pl.BlockSpec((1, QB, D), lambda b, j: (b, jnp.maximum(j - 1, 0), 0))


def _rms_in(x, meta, g, B, Lp):
    T = B * Lp
    NQ = Lp // QB

    def body(x_ref, meta_ref, g_ref, u_ref):
        h = _h_tile(pl.program_id(1), x_ref, meta_ref)
        r = lax.rsqrt(jnp.mean(h * h, axis=-1, keepdims=True) + EPS)
        u_ref[...] = (h * r * g_ref[...]).astype(BF16)

    return pl.pallas_call(
        body,
        out_shape=jax.ShapeDtypeStruct((T, D), BF16),
        grid=(B, NQ),
        in_specs=[_x_spec(), pl.BlockSpec((N_META, D), lambda b, j: (0, 0)), pl.BlockSpec((1, D), lambda b, j: (0, 0))],
        out_specs=pl.BlockSpec((QB, D), lambda b, j: (b * NQ + j, 0)),
        compiler_params=_params(("parallel", "parallel"), _nbytes((QB, D), F32) * 2),
        name="rms_in",
    )(x, meta, g)


def _gla_gate(lr, wg, bg, valid):
    pre = _nn(lr.astype(BF16), wg) + bg
    logsig = jnp.minimum(pre, 0.0) - jnp.log(1.0 + jnp.exp(-jnp.abs(pre)))
    return pre, jnp.where(valid, logsig / GLA_NORMALIZER, 0.0)


def _tri_masks():
    ri = lax.broadcasted_iota(jnp.int32, (GLA_C, GLA_C), 0)
    ci = lax.broadcasted_iota(jnp.int32, (GLA_C, GLA_C), 1)
    return ci <= ri, ci >= ri


def _cumsum_rows(x, ones_mask):
    w = jnp.where(ones_mask, 1.0, 0.0).astype(BF16)
    a, b, c = _split3(x)
    return _nn(w, a) + _nn(w, b) + _nn(w, c)


def _gla_fwd(projA, projB, wg, bg, gn4, B, Lp):
    T = B * Lp
    NC = Lp // GLA_C
    C = GLA_C
    scale = GLA_DK ** -0.5

    def body(q_ref, k_ref, v_ref, lr_ref, z_ref, wg_ref, bg_ref, gn_ref, oa_ref, ya_ref, ssave_ref, st_ref):
        n = pl.program_id(0)

        @pl.when(n == 0)
        def _():
            st_ref[...] = jnp.zeros_like(st_ref)

        pos = n * C + lax.broadcasted_iota(jnp.int32, (C, 1), 0)
        lower, _ = _tri_masks()
        is_last = lax.broadcasted_iota(jnp.int32, (C, 1), 0) == C - 1
        for b in range(B):
            ssave_ref[b, 0] = st_ref[b]
            _, glog = _gla_gate(lr_ref[b], wg_ref[...], bg_ref[...], pos >= FRONT)
            bcum = _cumsum_rows(glog, lower)
            for h in range(GLA_H):
                ks = slice(h * GLA_DK, (h + 1) * GLA_DK)
                vs = slice(h * GLA_DV, (h + 1) * GLA_DV)
                bh = bcum[:, ks]
                blast = jnp.sum(jnp.where(is_last, bh, 0.0), axis=0, keepdims=True)
                qh = q_ref[b, :, ks] * scale
                kh = k_ref[b, :, ks]
                qe = (qh * jnp.exp(bh)).astype(BF16)
                ke = (kh * jnp.exp(-bh)).astype(BF16)
                kl = (kh * jnp.exp(blast - bh)).astype(BF16)
                vh = v_ref[b, :, vs].astype(BF16)
                a = jnp.where(lower, _nt(qe, ke), 0.0).astype(BF16)
                st = st_ref[b, h]
                o = _nn(a, vh) + _nt(qe, st.astype(BF16))
                st_ref[b, h] = st * jnp.exp(blast) + _tn(vh, kl)
                oa_ref[b, :, vs] = o
                on = o * lax.rsqrt(jnp.mean(o * o, axis=-1, keepdims=True) + EPS) * gn_ref[:, vs]
                z = z_ref[b, :, vs]
                ya_ref[b, :, vs] = (on * (z * _sigmoid(z))).astype(BF16)

    blocks = B * (_nbytes((C, 512), F32) * 2 + _nbytes((C, 1024), F32) * 3 + _nbytes((C, 1024), BF16)
                  + _nbytes((GLA_H, GLA_DV, GLA_DK), F32)) + _nbytes((128, 512), BF16)
    state = _nbytes((B, GLA_H, GLA_DV, GLA_DK), F32)
    pa = projA.reshape(B, Lp, projA.shape[1])
    oa, ya, ssave = pl.pallas_call(
        body,
        out_shape=(jax.ShapeDtypeStruct((B, Lp, GLA_VW), F32), jax.ShapeDtypeStruct((B, Lp, GLA_VW), BF16),
                   jax.ShapeDtypeStruct((B, NC, GLA_H, GLA_DV, GLA_DK), F32)),
        grid=(NC,),
        in_specs=[
            pl.BlockSpec((B, C, 512), lambda n: (0, n, 10)),
            pl.BlockSpec((B, C, 512), lambda n: (0, n, 11)),
            pl.BlockSpec((B, C, 1024), lambda n: (0, n, 0)),
            pl.BlockSpec((B, C, 128), lambda n: (0, n, 3)),
            pl.BlockSpec((B, C, 1024), lambda n: (0, n, 1)),
            pl.BlockSpec((128, 512), lambda n: (0, 0)),
            pl.BlockSpec((1, 512), lambda n: (0, 0)),
            pl.BlockSpec((1, 1024), lambda n: (0, 0)),
        ],
        out_specs=(pl.BlockSpec((B, C, 1024), lambda n: (0, n, 0)),
                   pl.BlockSpec((B, C, 1024), lambda n: (0, n, 0)),
                   pl.BlockSpec((B, 1, GLA_H, GLA_DV, GLA_DK), lambda n: (0, n, 0, 0, 0))),
        scratch_shapes=[pltpu.VMEM((B, GLA_H, GLA_DV, GLA_DK), F32)],
        compiler_params=_params(("arbitrary",), blocks, state),
        name="gla_fwd",
    )(pa, pa, pa, projB.reshape(B, Lp, projB.shape[1]), pa, wg, bg, gn4)
    return oa.reshape(T, GLA_VW), ya.reshape(T, GLA_VW), ssave


def _swap_halves(x):
    lane = lax.broadcasted_iota(jnp.int32, x.shape, 1)
    return jnp.where((lane % 64) < 32, pltpu.roll(x, 96, 1), pltpu.roll(x, 32, 1))


def _mla_prep(projB, cos_t, sin_t, gq, gkv, wuq2, wukv, B, Lp, tr):
    T = B * Lp
    nt = Lp // tr
    HW = 2 * LANES

    def body(pb_ref, cos_ref, sin_ref, gq_ref, gkv_ref, wuq_ref, wukv_ref, q_ref, k_ref, v_ref, cqn_ref, ckvn_ref):
        cq = pb_ref[:, 0:Q_RANK]
        ckv = pb_ref[:, Q_RANK:Q_RANK + KV_RANK]
        kr = pb_ref[:, 512:640]
        cqn = (cq * lax.rsqrt(jnp.mean(cq * cq, axis=-1, keepdims=True) + EPS) * gq_ref[...]).astype(BF16)
        ckvn = (ckv * lax.rsqrt(jnp.mean(ckv * ckv, axis=-1, keepdims=True) + EPS) * gkv_ref[...]).astype(BF16)
        cqn_ref[...] = cqn
        ckvn_ref[...] = ckvn
        qf = _nn(cqn, wuq_ref[...])
        kvf = _nn(ckvn, wukv_ref[...])
        cs = cos_ref[...]
        sn = sin_ref[...]
        rope = lambda t: t * cs + _swap_halves(t) * sn
        kr_r = rope(kr).astype(BF16)
        for h in range(MLA_H):
            q_ref[:, h * HW:h * HW + LANES] = qf[:, h * HW:h * HW + LANES].astype(BF16)
            q_ref[:, h * HW + LANES:(h + 1) * HW] = rope(qf[:, h * HW + LANES:(h + 1) * HW]).astype(BF16)
            k_ref[:, h * HW:h * HW + LANES] = kvf[:, h * HW:h * HW + LANES].astype(BF16)
            k_ref[:, h * HW + LANES:(h + 1) * HW] = kr_r
            v_ref[:, h * MLA_DV:(h + 1) * MLA_DV] = kvf[:, h * HW + LANES:(h + 1) * HW].astype(BF16)

    blocks = (_nbytes((tr, 640), F32) + 2 * _nbytes((tr, 128), F32) + _nbytes((Q_RANK, 2048), BF16)
              + _nbytes((KV_RANK, 2048), BF16) + _nbytes((tr, 2048 * 2 + 1024 + 384), BF16)
              + 2 * _nbytes((tr, 2048), F32))
    return pl.pallas_call(
        body,
        out_shape=(jax.ShapeDtypeStruct((T, MLA_H * HW), BF16), jax.ShapeDtypeStruct((T, MLA_H * HW), BF16),
                   jax.ShapeDtypeStruct((T, MLA_H * MLA_DV), BF16), jax.ShapeDtypeStruct((T, Q_RANK), BF16),
                   jax.ShapeDtypeStruct((T, KV_RANK), BF16)),
        grid=(B, nt),
        in_specs=[
            pl.BlockSpec((tr, 640), lambda b, j: (b * nt + j, 0)),
            pl.BlockSpec((tr, 128), lambda b, j: (j, 0)),
            pl.BlockSpec((tr, 128), lambda b, j: (j, 0)),
            pl.BlockSpec((1, Q_RANK), lambda b, j: (0, 0)),
            pl.BlockSpec((1, KV_RANK), lambda b, j: (0, 0)),
            pl.BlockSpec((Q_RANK, 2048), lambda b, j: (0, 0)),
            pl.BlockSpec((KV_RANK, 2048), lambda b, j: (0, 0)),
        ],
        out_specs=(pl.BlockSpec((tr, 2048), lambda b, j: (b * nt + j, 0)),
                   pl.BlockSpec((tr, 2048), lambda b, j: (b * nt + j, 0)),
                   pl.BlockSpec((tr, 1024), lambda b, j: (b * nt + j, 0)),
                   pl.BlockSpec((tr, Q_RANK), lambda b, j: (b * nt + j, 0)),
                   pl.BlockSpec((tr, KV_RANK), lambda b, j: (b * nt + j, 0))),
        compiler_params=_params(("parallel", "parallel"), blocks),
        name="mla_prep",
    )(projB, cos_t, sin_t, gq, gkv, wuq2, wukv)


def _attn_mask(row, col):
    return (col <= row) & ((col >= FRONT) | (row < FRONT))


def _attn_fwd(q_att, k_att, v_att, projA, B, Lp):
    T = B * Lp
    NQ = Lp // QB
    HW = 2 * LANES
    scale = 1.0 / math.sqrt(MLA_QK)

    def body(q_ref, k_ref, v_ref, mz_ref, o_ref, yb_ref, lsec_ref, m_ref, l_ref, acc_ref):
        qi = pl.program_id(1)
        m_ref[...] = jnp.full(m_ref.shape, NEG, F32)
        l_ref[...] = jnp.zeros_like(l_ref)
        acc_ref[...] = jnp.zeros_like(acc_ref)
        row = qi * QB + lax.broadcasted_iota(jnp.int32, (QB, QB), 0)
        coli = lax.broadcasted_iota(jnp.int32, (QB, QB), 1)

        def step(kj, carry):
            off = pl.multiple_of(kj * QB, QB)
            ok = _attn_mask(row, kj * QB + coli)
            for h in range(MLA_H):
                q = q_ref[:, h * HW:(h + 1) * HW]
                kb = k_ref[pl.ds(off, QB), h * HW:(h + 1) * HW]
                vb = v_ref[pl.ds(off, QB), h * MLA_DV:(h + 1) * MLA_DV]
                s = jnp.where(ok, _nt(q, kb) * scale, NEG)
                m_old = m_ref[h]
                m_new = jnp.maximum(m_old, jnp.max(s, axis=-1, keepdims=True))
                alpha = jnp.exp(m_old - m_new)
                p = jnp.exp(s - jnp.tile(m_new, (1, QB // LANES)))
                m_ref[h] = m_new
                l_ref[h] = alpha * l_ref[h] + jnp.sum(p, axis=-1, keepdims=True)
                acc_ref[h] = alpha * acc_ref[h] + _nn(p.astype(BF16), vb)
            return carry

        lax.fori_loop(0, qi + 1, step, 0)
        for h in range(MLA_H):
            hs = slice(h * MLA_DV, (h + 1) * MLA_DV)
            l = l_ref[h]
            o = acc_ref[h] / l
            o_ref[:, hs] = o
            z = mz_ref[:, hs]
            yb_ref[:, hs] = (o * (z * _sigmoid(z))).astype(BF16)
            lse = m_ref[h] + jnp.log(l)
            lsec_ref[0, h, pl.ds(qi, 1), :] = jnp.transpose(lse)[0:1, :]

    blocks = (_nbytes((QB, 2048), BF16) + _nbytes((Lp, 2048), BF16) + _nbytes((Lp, 1024), BF16)
              + 2 * _nbytes((QB, 1024), F32) + _nbytes((QB, 1024), BF16) + _nbytes((MLA_H, QB, LANES), F32)
              + _nbytes((MLA_H, NQ, QB), F32))
    return pl.pallas_call(
        body,
        out_shape=(jax.ShapeDtypeStruct((T, MLA_H * MLA_DV), F32), jax.ShapeDtypeStruct((T, MLA_H * MLA_DV), BF16),
                   jax.ShapeDtypeStruct((B, MLA_H, NQ, QB), F32)),
        grid=(B, NQ),
        in_specs=[
            pl.BlockSpec((QB, MLA_H * HW), lambda b, i: (b * NQ + i, 0)),
            pl.BlockSpec((Lp, MLA_H * HW), lambda b, i: (b, 0)),
            pl.BlockSpec((Lp, MLA_H * MLA_DV), lambda b, i: (b, 0)),
            pl.BlockSpec((QB, 1024), lambda b, i: (b * NQ + i, 2)),
        ],
        out_specs=(pl.BlockSpec((QB, 1024), lambda b, i: (b * NQ + i, 0)),
                   pl.BlockSpec((QB, 1024), lambda b, i: (b * NQ + i, 0)),
                   pl.BlockSpec((1, MLA_H, NQ, QB), lambda b, i: (b, 0, 0, 0))),
        scratch_shapes=[pltpu.VMEM((MLA_H, QB, LANES), F32), pltpu.VMEM((MLA_H, QB, LANES), F32),
                        pltpu.VMEM((MLA_H, QB, MLA_DV), F32)],
        compiler_params=_params(("parallel", "arbitrary"), blocks, 3 * _nbytes((MLA_H, QB, LANES), F32)),
        name="attn_fwd",
    )(q_att, k_att, v_att, projA)


def _merge_fwd(projA, ya, yb, tr):
    T = ya.shape[0]

    def body(gg_ref, gm_ref, ya_ref, yb_ref, o_ref):
        o_ref[...] = (_sigmoid(gg_ref[...]) * ya_ref[...] + _sigmoid(gm_ref[...]) * yb_ref[...]).astype(BF16)

    spec = lambda c: pl.BlockSpec((tr, D), lambda i: (i, c))
    return pl.pallas_call(
        body,
        out_shape=jax.ShapeDtypeStruct((T, D), BF16),
        grid=(T // tr,),
        in_specs=[spec(3), spec(4), spec(0), spec(0)],
        out_specs=spec(0),
        compiler_params=_params(("parallel",), 5 * _nbytes((tr, D), F32)),
        name="merge_fwd",
    )(projA, projA, ya, yb)


def _final_loss(x, meta, mo, gf, tgt, B, Lp):
    T = B * Lp
    NQ = Lp // QB

    def body(x_ref, meta_ref, mo_ref, gf_ref, t_ref, dh_ref, dhb_ref, loss_ref, dgf_ref):
        b = pl.program_id(0)
        j = pl.program_id(1)

        @pl.when((b == 0) & (j == 0))
        def _():
            loss_ref[...] = jnp.zeros_like(loss_ref)
            dgf_ref[...] = jnp.zeros_like(dgf_ref)

        h1 = _h_tile(j, x_ref, meta_ref) + mo_ref[...]
        r = lax.rsqrt(jnp.mean(h1 * h1, axis=-1, keepdims=True) + EPS)
        hn = h1 * r
        gfv = gf_ref[...]
        diff = jnp.where(j > 0, hn * gfv - t_ref[0], 0.0)
        loss_ref[...] += (0.5 / D) * jnp.sum(jnp.sum(diff * diff, axis=-1, keepdims=True), axis=0, keepdims=True)
        dout = diff * (1.0 / D)
        dgf_ref[...] += jnp.sum(dout * hn, axis=0, keepdims=True)
        dhn = dout * gfv
        dh = r * (dhn - hn * jnp.mean(dhn * hn, axis=-1, keepdims=True))
        dh_ref[...] = dh
        dhb_ref[...] = dh.astype(BF16)

    rows = pl.BlockSpec((QB, D), lambda b, j: (b * NQ + j, 0))
    return pl.pallas_call(
        body,
        out_shape=(jax.ShapeDtypeStruct((T, D), F32), jax.ShapeDtypeStruct((T, D), BF16),
                   jax.ShapeDtypeStruct((1, 1), F32), jax.ShapeDtypeStruct((1, D), F32)),
        grid=(B, NQ),
        in_specs=[_x_spec(), pl.BlockSpec((N_META, D), lambda b, j: (0, 0)), rows,
                  pl.BlockSpec((1, D), lambda b, j: (0, 0)), _x_spec()],
        out_specs=(rows, rows, pl.BlockSpec((1, 1), lambda b, j: (0, 0)), pl.BlockSpec((1, D), lambda b, j: (0, 0))),
        compiler_params=_params(("arbitrary", "arbitrary"), 5 * _nbytes((QB, D), F32)),
        name="final_loss",
    )(x, meta, mo, gf, tgt)


def _merge_bwd(dm, projA, ya, yb, tr):
    T = dm.shape[0]

    def body(dm_ref, gg_ref, gm_ref, ya_ref, yb_ref, dya_ref, dyb_ref, da_ref):
        d = dm_ref[...]
        sg = _sigmoid(gg_ref[...])
        sm = _sigmoid(gm_ref[...])
        dya_ref[...] = (d * sg).astype(BF16)
        dyb_ref[...] = (d * sm).astype(BF16)
        da_ref[:, 0:D] = (d * ya_ref[...] * (sg * (1.0 - sg))).astype(BF16)
        da_ref[:, D:2 * D] = (d * yb_ref[...] * (sm * (1.0 - sm))).astype(BF16)

    spec = lambda c: pl.BlockSpec((tr, D), lambda i: (i, c))
    return pl.pallas_call(
        body,
        out_shape=(jax.ShapeDtypeStruct((T, D), BF16), jax.ShapeDtypeStruct((T, D), BF16),
                   jax.ShapeDtypeStruct((T, 2 * D), BF16)),
        grid=(T // tr,),
        in_specs=[spec(0), spec(3), spec(4), spec(0), spec(0)],
        out_specs=(spec(0), spec(0), pl.BlockSpec((tr, 2 * D), lambda i: (i, 0))),
        compiler_params=_params(("parallel",), 8 * _nbytes((tr, D), F32)),
        name="merge_bwd",
    )(dm, projA, projA, ya, yb)


def _gla_out_bwd(dyin, oa, projA, gn4, tr):
    T = dyin.shape[0]
    nsteps = T // tr

    def body(dy_ref, oa_ref, z_ref, gn_ref, do_ref, dz_ref, dgn_ref, acc_ref):
        i = pl.program_id(0)

        @pl.when(i == 0)
        def _():
            acc_ref[...] = jnp.zeros_like(acc_ref)

        for h in range(GLA_H):
            vs = slice(h * GLA_DV, (h + 1) * GLA_DV)
            dy = dy_ref[:, vs]
            o = oa_ref[:, vs]
            z = z_ref[:, vs]
            gn = gn_ref[:, vs]
            s = _sigmoid(z)
            ra = lax.rsqrt(jnp.mean(o * o, axis=-1, keepdims=True) + EPS)
            on = o * ra
            don = dy * (z * s)
            t = don * gn
            do_ref[:, vs] = (ra * (t - on * jnp.mean(t * on, axis=-1, keepdims=True))).astype(BF16)
            dz_ref[:, vs] = (dy * (on * gn) * (s * (1.0 + z * (1.0 - s)))).astype(BF16)
            acc_ref[:, vs] += jnp.sum(don * on, axis=0, keepdims=True)

        @pl.when(i == nsteps - 1)
        def _():
            a = acc_ref[...]
            dgn_ref[...] = a[:, 0:256] + a[:, 256:512] + a[:, 512:768] + a[:, 768:1024]

    spec = lambda c: pl.BlockSpec((tr, D), lambda i: (i, c))
    return pl.pallas_call(
        body,
        out_shape=(jax.ShapeDtypeStruct((T, D), BF16), jax.ShapeDtypeStruct((T, D), BF16),
                   jax.ShapeDtypeStruct((1, GLA_DV), F32)),
        grid=(nsteps,),
        in_specs=[spec(0), spec(0), spec(1), pl.BlockSpec((1, D), lambda i: (0, 0))],
        out_specs=(spec(0), spec(0), pl.BlockSpec((1, GLA_DV), lambda i: (0, 0))),
        scratch_shapes=[pltpu.VMEM((1, D), F32)],
        compiler_params=_params(("arbitrary",), 6 * _nbytes((tr, D), F32)),
        name="gla_out_bwd",
    )(dyin, oa, projA, gn4)


def _gla_bwd(projA, projB, ssave, doa, wg, bg, B, Lp):
    T = B * Lp
    NC = Lp // GLA_C
    C = GLA_C
    scale = GLA_DK ** -0.5
    WC = 2304

    def body(q_ref, k_ref, v_ref, lr_ref, ss_ref, do_ref, wg_ref, bg_ref, dc_ref, dwg_ref, dbg_ref, dst_ref):
        i = pl.program_id(0)
        n = NC - 1 - i

        @pl.when(i == 0)
        def _():
            dst_ref[...] = jnp.zeros_like(dst_ref)
            dwg_ref[...] = jnp.zeros_like(dwg_ref)
            dbg_ref[...] = jnp.zeros_like(dbg_ref)

        pos = n * C + lax.broadcasted_iota(jnp.int32, (C, 1), 0)
        valid = pos >= FRONT
        lower, upper = _tri_masks()
        is_last = lax.broadcasted_iota(jnp.int32, (C, 1), 0) == C - 1
        for b in range(B):
            lr = lr_ref[b]
            pre, glog = _gla_gate(lr, wg_ref[...], bg_ref[...], valid)
            bcum = _cumsum_rows(glog, lower)
            db_parts = []
            for h in range(GLA_H):
                ks = slice(h * GLA_DK, (h + 1) * GLA_DK)
                vs = slice(h * GLA_DV, (h + 1) * GLA_DV)
                bh = bcum[:, ks]
                blast = jnp.sum(jnp.where(is_last, bh, 0.0), axis=0, keepdims=True)
                eb, enb, ekl, ebl = jnp.exp(bh), jnp.exp(-bh), jnp.exp(blast - bh), jnp.exp(blast)
                qh = q_ref[b, :, ks] * scale
                kh = k_ref[b, :, ks]
                qe_f, ke_f, kl_f = qh * eb, kh * enb, kh * ekl
                qe, ke, kl = qe_f.astype(BF16), ke_f.astype(BF16), kl_f.astype(BF16)
                vh = v_ref[b, :, vs].astype(BF16)
                doh = do_ref[b, :, vs]
                st = ss_ref[b, 0, h]
                dst = dst_ref[b, h]
                st_b, dst_b = st.astype(BF16), dst.astype(BF16)
                da = jnp.where(lower, _nt(doh, vh), 0.0).astype(BF16)
                da_t = jnp.where(upper, _nt(vh, doh), 0.0).astype(BF16)
                a_t = jnp.where(upper, _nt(ke, qe), 0.0).astype(BF16)
                dqe = _nn(da, ke) + _nn(doh, st_b)
                dke = _nn(da_t, qe)
                dvh = _nn(a_t, doh) + _nt(kl, dst_b)
                dkl = _nn(vh, dst_b)
                dst_ref[b, h] = dst * ebl + _tn(doh, qe)
                deb = jnp.sum(st * dst, axis=0, keepdims=True)
                db = dqe * qe_f - dke * ke_f - dkl * kl_f
                db_last = jnp.sum(dkl * kl_f, axis=0, keepdims=True) + deb * ebl
                db_parts.append(db + jnp.where(is_last, db_last, 0.0))
                dc_ref[b, :, vs] = dvh.astype(BF16)
                dc_ref[b, :, 1024 + h * GLA_DK:1024 + (h + 1) * GLA_DK] = (dqe * eb * scale).astype(BF16)
                dc_ref[b, :, 1536 + h * GLA_DK:1536 + (h + 1) * GLA_DK] = (dke * enb + dkl * ekl).astype(BF16)
            dglog = _cumsum_rows(jnp.concatenate(db_parts, axis=1), upper)
            dpre = jnp.where(valid, dglog * (1.0 / GLA_NORMALIZER) / (1.0 + jnp.exp(pre)), 0.0)
            dpre_b = dpre.astype(BF16)
            dc_ref[b, :, 2048:2176] = _nt(dpre_b, wg_ref[...]).astype(BF16)
            dc_ref[b, :, 2176:2304] = jnp.zeros((C, 128), BF16)
            dwg_ref[...] += _tn(lr.astype(BF16), dpre_b)
            dbg_ref[...] += jnp.sum(dpre, axis=0, keepdims=True)

    blocks = B * (_nbytes((C, 512), F32) * 2 + _nbytes((C, 1024), F32) + _nbytes((C, 1024), BF16)
                  + _nbytes((GLA_H, GLA_DV, GLA_DK), F32) + _nbytes((C, WC), BF16)) + 3 * _nbytes((128, 512), F32)
    state = _nbytes((B, GLA_H, GLA_DV, GLA_DK), F32)
    pa = projA.reshape(B, Lp, projA.shape[1])
    rev = lambda i: NC - 1 - i
    dc, dwg, dbg = pl.pallas_call(
        body,
        out_shape=(jax.ShapeDtypeStruct((B, Lp, WC), BF16), jax.ShapeDtypeStruct((128, GLA_KW), F32),
                   jax.ShapeDtypeStruct((1, GLA_KW), F32)),
        grid=(NC,),
        in_specs=[
            pl.BlockSpec((B, C, 512), lambda i: (0, rev(i), 10)),
            pl.BlockSpec((B, C, 512), lambda i: (0, rev(i), 11)),
            pl.BlockSpec((B, C, 1024), lambda i: (0, rev(i), 0)),
            pl.BlockSpec((B, C, 128), lambda i: (0, rev(i), 3)),
            pl.BlockSpec((B, 1, GLA_H, GLA_DV, GLA_DK), lambda i: (0, rev(i), 0, 0, 0)),
            pl.BlockSpec((B, C, 1024), lambda i: (0, rev(i), 0)),
            pl.BlockSpec((128, 512), lambda i: (0, 0)),
            pl.BlockSpec((1, 512), lambda i: (0, 0)),
        ],
        out_specs=(pl.BlockSpec((B, C, WC), lambda i: (0, rev(i), 0)),
                   pl.BlockSpec((128, GLA_KW), lambda i: (0, 0)),
                   pl.BlockSpec((1, GLA_KW), lambda i: (0, 0))),
        scratch_shapes=[pltpu.VMEM((B, GLA_H, GLA_DV, GLA_DK), F32)],
        compiler_params=_params(("arbitrary",), blocks, state),
        name="gla_bwd",
    )(pa, pa, pa, projB.reshape(B, Lp, projB.shape[1]), ssave, doa.reshape(B, Lp, GLA_VW), wg, bg)
    return dc.reshape(T, WC), dwg, dbg


def _attn_bwd_pre(dyin, projA, ob, B, Lp):
    T = B * Lp
    NQ = Lp // QB

    def body(dy_ref, z_ref, o_ref, do_ref, dz_ref, dcol_ref):
        j = pl.program_id(1)
        for h in range(MLA_H):
            hs = slice(h * MLA_DV, (h + 1) * MLA_DV)
            dy = dy_ref[:, hs]
            z = z_ref[:, hs]
            o = o_ref[:, hs]
            s = _sigmoid(z)
            do = dy * (z * s)
            do_ref[:, hs] = do.astype(BF16)
            dz_ref[:, hs] = (dy * o * (s * (1.0 + z * (1.0 - s)))).astype(BF16)
            dl = jnp.broadcast_to(jnp.sum(do * o, axis=-1, keepdims=True), (QB, LANES))
            dcol_ref[0, h, pl.ds(j, 1), :] = jnp.transpose(dl)[0:1, :]

    rows = lambda c: pl.BlockSpec((QB, D), lambda b, j: (b * NQ + j, c))
    return pl.pallas_call(
        body,
        out_shape=(jax.ShapeDtypeStruct((T, D), BF16), jax.ShapeDtypeStruct((T, D), BF16),
                   jax.ShapeDtypeStruct((B, MLA_H, NQ, QB), F32)),
        grid=(B, NQ),
        in_specs=[rows(0), rows(2), rows(0)],
        out_specs=(rows(0), rows(0), pl.BlockSpec((1, MLA_H, NQ, QB), lambda b, j: (b, 0, 0, 0))),
        compiler_params=_params(("parallel", "arbitrary"), 6 * _nbytes((QB, D), F32)),
        name="attn_bwd_pre",
    )(dyin, projA, ob)


ATTN_BWD_HEADS = 4


def _attn_bwd(q_att, k_att, v_att, do, lse_c, delta_c, B, Lp):
    T = B * Lp
    NQ = Lp // QB
    G = ATTN_BWD_HEADS
    NG = MLA_H // G
    HW = 2 * LANES
    scale = 1.0 / math.sqrt(MLA_QK)

    def body(q_ref, k_ref, v_ref, do_ref, lse_ref, dl_ref, dq_ref, dk_ref, dv_ref):
        kj = pl.program_id(2)

        @pl.when(kj == 0)
        def _():
            dq_ref[...] = jnp.zeros_like(dq_ref)

        dk_ref[...] = jnp.zeros_like(dk_ref)
        dv_ref[...] = jnp.zeros_like(dv_ref)
        col = kj * QB + lax.broadcasted_iota(jnp.int32, (QB, QB), 0)
        rowi = lax.broadcasted_iota(jnp.int32, (QB, QB), 1)

        def step(qi, carry):
            off = pl.multiple_of(qi * QB, QB)
            ok = _attn_mask(qi * QB + rowi, col)
            for h in range(G):
                ws = slice(h * HW, (h + 1) * HW)
                hs = slice(h * MLA_DV, (h + 1) * MLA_DV)
                qb = q_ref[pl.ds(off, QB), ws]
                dob = do_ref[pl.ds(off, QB), hs]
                kb = k_ref[:, ws]
                lse = lse_ref[0, h, pl.ds(qi, 1), :]
                delta = dl_ref[0, h, pl.ds(qi, 1), :]
                s_t = _nt(kb, qb) * scale
                p_t = jnp.where(ok, jnp.exp(s_t - lse), 0.0)
                dv_ref[:, hs] += _nn(p_t.astype(BF16), dob)
                ds_t = (p_t * (_nt(v_ref[:, hs], dob) - delta) * scale).astype(BF16)
                dk_ref[:, ws] += _nn(ds_t, qb)
                dq_ref[pl.ds(off, QB), ws] += _tn(ds_t, kb)
            return carry

        lax.fori_loop(kj, NQ, step, 0)

    blocks = (_nbytes((Lp, G * HW), BF16) + _nbytes((Lp, G * MLA_DV), BF16) + _nbytes((QB, G * 384), BF16)
              + 2 * _nbytes((G, NQ, QB), F32) + _nbytes((QB, G * 384), F32) + _nbytes((Lp, G * HW), F32))
    return pl.pallas_call(
        body,
        out_shape=(jax.ShapeDtypeStruct((T, MLA_H * HW), F32), jax.ShapeDtypeStruct((T, MLA_H * HW), F32),
                   jax.ShapeDtypeStruct((T, MLA_H * MLA_DV), F32)),
        grid=(B, NG, NQ),
        in_specs=[
            pl.BlockSpec((Lp, G * HW), lambda b, g, j: (b, g)),
            pl.BlockSpec((QB, G * HW), lambda b, g, j: (b * NQ + j, g)),
            pl.BlockSpec((QB, G * MLA_DV), lambda b, g, j: (b * NQ + j, g)),
            pl.BlockSpec((Lp, G * MLA_DV), lambda b, g, j: (b, g)),
            pl.BlockSpec((1, G, NQ, QB), lambda b, g, j: (b, g, 0, 0)),
            pl.BlockSpec((1, G, NQ, QB), lambda b, g, j: (b, g, 0, 0)),
        ],
        out_specs=(pl.BlockSpec((Lp, G * HW), lambda b, g, j: (b, g)),
                   pl.BlockSpec((QB, G * HW), lambda b, g, j: (b * NQ + j, g)),
                   pl.BlockSpec((QB, G * MLA_DV), lambda b, g, j: (b * NQ + j, g))),
        compiler_params=_params(("parallel", "parallel", "arbitrary"), blocks),
        name="attn_bwd",
    )(q_att, k_att, v_att, do, lse_c, delta_c)


def _mla_bwd_post(dq, dk, dv, projB, cos_t, sin_t, gq, gkv, wuq2, wukv, B, Lp, tr):
    T = B * Lp
    nt = Lp // tr
    HW = 2 * LANES

    def body(dq_ref, dk_ref, dv_ref, pb_ref, cos_ref, sin_ref, gq_ref, gkv_ref, wuq_ref, wukv_ref,
             dqf_ref, dkvf_ref, de_ref, dgq_ref, dgkv_ref):
        first = (pl.program_id(0) == 0) & (pl.program_id(1) == 0)

        @pl.when(first)
        def _():
            dgq_ref[...] = jnp.zeros_like(dgq_ref)
            dgkv_ref[...] = jnp.zeros_like(dgkv_ref)

        cs = cos_ref[...]
        sn = sin_ref[...]
        rope_t = lambda t: t * cs + _swap_halves(t * sn)
        dkr = jnp.zeros((tr, LANES), F32)
        for h in range(MLA_H):
            dqf_ref[:, h * HW:h * HW + LANES] = dq_ref[:, h * HW:h * HW + LANES].astype(BF16)
            dqf_ref[:, h * HW + LANES:(h + 1) * HW] = rope_t(dq_ref[:, h * HW + LANES:(h + 1) * HW]).astype(BF16)
            dkvf_ref[:, h * HW:h * HW + LANES] = dk_ref[:, h * HW:h * HW + LANES].astype(BF16)
            dkvf_ref[:, h * HW + LANES:(h + 1) * HW] = dv_ref[:, h * MLA_DV:(h + 1) * MLA_DV].astype(BF16)
            dkr = dkr + dk_ref[:, h * HW + LANES:(h + 1) * HW]

        def norm_bwd(x, dn, g):
            r = lax.rsqrt(jnp.mean(x * x, axis=-1, keepdims=True) + EPS)
            xn = x * r
            t = dn * g
            return r * (t - xn * jnp.mean(t * xn, axis=-1, keepdims=True)), jnp.sum(dn * xn, axis=0, keepdims=True)

        dcq, dgq = norm_bwd(pb_ref[:, 0:Q_RANK], _nt(dqf_ref[...], wuq_ref[...]), gq_ref[...])
        dckv, dgkv = norm_bwd(pb_ref[:, Q_RANK:Q_RANK + KV_RANK], _nt(dkvf_ref[...], wukv_ref[...]), gkv_ref[...])
        dgq_ref[...] += dgq
        dgkv_ref[...] += dgkv
        de_ref[:, 0:Q_RANK] = dcq.astype(BF16)
        de_ref[:, Q_RANK:Q_RANK + KV_RANK] = dckv.astype(BF16)
        de_ref[:, 384:512] = rope_t(dkr).astype(BF16)

    rows = lambda w: pl.BlockSpec((tr, w), lambda b, j: (b * nt + j, 0))
    const = lambda s: pl.BlockSpec(s, lambda b, j: (0, 0))
    blocks = (2 * _nbytes((tr, 2048), F32) + _nbytes((tr, 1024), F32) + _nbytes((tr, 640), F32)
              + 2 * _nbytes((tr, 2048), BF16) + _nbytes((2048, 384), BF16) + 2 * _nbytes((tr, 2048), F32))
    return pl.pallas_call(
        body,
        out_shape=(jax.ShapeDtypeStruct((T, 2048), BF16), jax.ShapeDtypeStruct((T, 2048), BF16),
                   jax.ShapeDtypeStruct((T, 512), BF16), jax.ShapeDtypeStruct((1, Q_RANK), F32),
                   jax.ShapeDtypeStruct((1, KV_RANK), F32)),
        grid=(B, nt),
        in_specs=[rows(2048), rows(2048), rows(1024), rows(640),
                  pl.BlockSpec((tr, 128), lambda b, j: (j, 0)), pl.BlockSpec((tr, 128), lambda b, j: (j, 0)),
                  const((1, Q_RANK)), const((1, KV_RANK)), const((Q_RANK, 2048)), const((KV_RANK, 2048))],
        out_specs=(rows(2048), rows(2048), rows(512), const((1, Q_RANK)), const((1, KV_RANK))),
        compiler_params=_params(("arbitrary", "arbitrary"), blocks),
        name="mla_bwd_post",
    )(dq, dk, dv, projB, cos_t, sin_t, gq, gkv, wuq2, wukv)


def _du_matmul(dA, dBz, dC, dDz, dE, wA, wB, tm):
    T = dA.shape[0]

    def body(da_ref, db_ref, dc_ref, dd_ref, de_ref, wa_ref, wb_ref, o_ref):
        acc = _nt(da_ref[...], wa_ref[:, 3072:5120])
        acc = acc + _nt(db_ref[...], wa_ref[:, 1024:2048])
        acc = acc + _nt(dd_ref[...], wa_ref[:, 2048:3072])
        acc = acc + _nt(dc_ref[:, 0:1024], wa_ref[:, 0:1024])
        acc = acc + _nt(dc_ref[:, 1024:2048], wa_ref[:, 5120:6144])
        acc = acc + _nt(dc_ref[:, 2048:2176], wb_ref[:, 384:512])
        acc = acc + _nt(de_ref[:, 0:384], wb_ref[:, 0:384])
        acc = acc + _nt(de_ref[:, 384:512], wb_ref[:, 512:640])
        o_ref[...] = acc

    widths = [a.shape[1] for a in (dA, dBz, dC, dDz, dE)]
    blocks = (sum(_nbytes((tm, w), BF16) for w in widths) + _nbytes(wA.shape, BF16) + _nbytes(wB.shape, BF16)
              + _nbytes((tm, D), F32))
    return pl.pallas_call(
        body,
        out_shape=jax.ShapeDtypeStruct((T, D), F32),
        grid=(T // tm,),
        in_specs=[pl.BlockSpec((tm, w), lambda i: (i, 0)) for w in widths]
        + [pl.BlockSpec(wA.shape, lambda i: (0, 0)), pl.BlockSpec(wB.shape, lambda i: (0, 0))],
        out_specs=pl.BlockSpec((tm, D), lambda i: (i, 0)),
        compiler_params=_params(("parallel",), blocks),
        name="du_matmul",
    )(dA, dBz, dC, dDz, dE, wA, wB)


def _in_norm_bwd(x, meta, dh1, du, g, B, Lp):
    NQ = Lp // QB
    seq = x.shape[1]

    def body(x_ref, meta_ref, dh_ref, du_ref, g_ref, gx_ref, dmeta_ref, dg_ref):
        b = pl.program_id(0)
        j = pl.program_id(1)

        @pl.when((b == 0) & (j == 0))
        def _():
            dg_ref[...] = jnp.zeros_like(dg_ref)

        x = _h_tile(j, x_ref, meta_ref)
        r = lax.rsqrt(jnp.mean(x * x, axis=-1, keepdims=True) + EPS)
        xn = x * r
        du = du_ref[...]
        t = du * g_ref[...]
        dh0 = dh_ref[...] + r * (t - xn * jnp.mean(t * xn, axis=-1, keepdims=True))
        dg_ref[...] += jnp.sum(du * xn, axis=0, keepdims=True)
        gx_ref[0] = dh0

        @pl.when((j == 0) & (b == 0))
        def _():
            dmeta_ref[...] = dh0[FRONT:HEAD_ROWS, :]

        @pl.when((j == 0) & (b > 0))
        def _():
            dmeta_ref[...] += dh0[FRONT:HEAD_ROWS, :]

    rows = pl.BlockSpec((QB, D), lambda b, j: (b * NQ + j, 0))
    return pl.pallas_call(
        body,
        out_shape=(jax.ShapeDtypeStruct((B, seq, D), F32), jax.ShapeDtypeStruct((N_META, D), F32),
                   jax.ShapeDtypeStruct((1, D), F32)),
        grid=(B, NQ),
        in_specs=[_x_spec(), pl.BlockSpec((N_META, D), lambda b, j: (0, 0)), rows, rows,
                  pl.BlockSpec((1, D), lambda b, j: (0, 0))],
        out_specs=(_x_spec(), pl.BlockSpec((N_META, D), lambda b, j: (0, 0)), pl.BlockSpec((1, D), lambda b, j: (0, 0))),
        compiler_params=_params(("arbitrary", "arbitrary"), 5 * _nbytes((QB, D), F32)),
        name="in_norm_bwd",
    )(x, meta, dh1, du, g)


_VMEM_WHOLE = pl.BlockSpec(memory_space=pltpu.VMEM)


def _params_whole(arrays):
    total = sum(_nbytes(a.shape, a.dtype) for a in arrays)
    return pltpu.CompilerParams(vmem_limit_bytes=int(min(total + 12 * 1024 * 1024, VMEM_CAP_V7X)))


def _wire_dtype(shape):
    return BF16 if shape[-2] * shape[-1] >= WIRE_BF16_MIN_ELEMS else F32


def _pair_add_big(gp, recv, c):
    _, half, cols = recv.shape
    th = _div_tile(half, 64, 16)
    out_dtype = _wire_dtype(recv.shape)

    def body(c_ref, a_ref, b_ref, o_ref):
        o_ref[...] = (a_ref[:, 0] + b_ref[...]).astype(out_dtype)

    return pl.pallas_call(
        body,
        out_shape=jax.ShapeDtypeStruct(recv.shape, out_dtype),
        grid_spec=pltpu.PrefetchScalarGridSpec(
            num_scalar_prefetch=1,
            grid=(half // th,),
            in_specs=[pl.BlockSpec((4, 1, th, cols), lambda i, c_ref: (0, c_ref[0], i, 0)),
                      pl.BlockSpec((4, th, cols), lambda i, c_ref: (0, i, 0))],
            out_specs=pl.BlockSpec((4, th, cols), lambda i, c_ref: (0, i, 0)),
        ),
        compiler_params=_params(("parallel",), 3 * _nbytes((4, th, cols), F32)),
        name="grad_pair_add_big",
    )(c, gp.reshape(4, 2, half, cols), recv)


def _pair_add_small(gps, recvs):
    n = len(gps)

    def body(*refs):
        c = lax.axis_index("c")
        for t in range(n):
            g_ref, r_ref, o_ref = refs[t], refs[n + t], refs[2 * n + t]
            half = r_ref.shape[1]
            s = g_ref[:, pl.ds(pl.multiple_of(c * half, 8), half), :] + r_ref[...]
            o_ref[...] = s.astype(o_ref.dtype)

    return pl.pallas_call(
        body,
        out_shape=[jax.ShapeDtypeStruct(r.shape, _wire_dtype(r.shape)) for r in recvs],
        in_specs=[_VMEM_WHOLE] * (2 * n),
        out_specs=[_VMEM_WHOLE] * n,
        compiler_params=_params_whole(list(gps) + 2 * list(recvs)),
        name="grad_pair_add_small",
    )(*gps, *recvs)


def _chip_order_sum(landed_ref, own_ref, me):
    p = [jnp.where(me == k, own_ref[k], landed_ref[k]).astype(F32) for k in range(4)]
    return ((p[0] + p[1]) + p[2]) + p[3]


def _sum_chips_big(landed, own, pos):
    _, half, cols = landed.shape
    th = _div_tile(half, 64, 16)

    def body(pos_ref, l_ref, s_ref, o_ref):
        o_ref[0] = _chip_order_sum(l_ref, s_ref, pos_ref[1])

    spec = pl.BlockSpec((4, th, cols), lambda i, pos_ref: (0, i, 0))
    return pl.pallas_call(
        body,
        out_shape=jax.ShapeDtypeStruct((2, half, cols), F32),
        grid_spec=pltpu.PrefetchScalarGridSpec(
            num_scalar_prefetch=1,
            grid=(half // th,),
            in_specs=[spec, spec],
            out_specs=pl.BlockSpec((1, th, cols), lambda i, pos_ref: (pos_ref[0], i, 0)),
        ),
        compiler_params=_params(("parallel",), 3 * _nbytes((4, th, cols), F32)),
        name="grad_sum_chips_big",
    )(pos, landed, own)


def _sum_chips_small(landed, own):
    n = len(landed)

    def body(*refs):
        x, y, c = _mesh_pos()
        for t in range(n):
            refs[2 * n + t][c] = _chip_order_sum(refs[t], refs[n + t], 2 * x + y)

    return pl.pallas_call(
        body,
        out_shape=[jax.ShapeDtypeStruct((2,) + p.shape[1:], F32) for p in landed],
        in_specs=[_VMEM_WHOLE] * (2 * n),
        out_specs=[_VMEM_WHOLE] * n,
        compiler_params=_params_whole(list(landed) * 3),
        name="grad_sum_chips_small",
    )(*landed, *own)


def _adamw_update(w_ref, g_ref, m_ref, v_ref, d_ref, mo_ref, vo_ref):
    c1 = 1.0 - ADAM_B1 ** ADAM_STEP
    c2 = 1.0 - ADAM_B2 ** ADAM_STEP
    gv = g_ref[...]
    mn = ADAM_B1 * m_ref[...] + (1.0 - ADAM_B1) * gv
    vn = ADAM_B2 * v_ref[...] + (1.0 - ADAM_B2) * (gv * gv)
    mo_ref[...] = mn
    vo_ref[...] = vn
    d_ref[...] = -ADAM_LR * ((mn / c1) / (jnp.sqrt(vn / c2) + ADAM_EPS) + ADAM_WD * w_ref[...])


def _adamw_big(w, g, m, v):
    rows, cols = w.shape
    tr = _div_tile(rows, 128, 8)
    spec = pl.BlockSpec((tr, cols), lambda i: (i, 0))
    shp = jax.ShapeDtypeStruct((rows, cols), F32)
    return pl.pallas_call(
        functools.partial(_adamw_update),
        out_shape=(shp, shp, shp),
        grid=(rows // tr,),
        in_specs=[spec] * 4,
        out_specs=(spec, spec, spec),
        compiler_params=_params(("parallel",), 7 * _nbytes((tr, cols), F32)),
        name="adamw_big",
    )(w, g, m, v)


def _adamw_small(ws, gs, ms, vs):
    n = len(ws)

    def body(*refs):
        for t in range(n):
            _adamw_update(refs[t], refs[n + t], refs[2 * n + t], refs[3 * n + t],
                          refs[4 * n + t], refs[5 * n + t], refs[6 * n + t])

    shapes = [jax.ShapeDtypeStruct(w.shape, F32) for w in ws]
    return pl.pallas_call(
        body,
        out_shape=shapes * 3,
        in_specs=[_VMEM_WHOLE] * (4 * n),
        out_specs=[_VMEM_WHOLE] * (3 * n),
        compiler_params=_params_whole(list(ws) * 7),
        name="adamw_small",
    )(*ws, *gs, *ms, *vs)


def _mesh_pos():
    return lax.axis_index("x"), lax.axis_index("y"), lax.axis_index("c")


def _other_chips(x, y):
    return [(1 - x, y), (x, 1 - y), (1 - x, 1 - y)]


_ANY = pl.BlockSpec(memory_space=pl.ANY)


PAIR_SPLIT_MIN_ROWS = 64


def _weight_gather(shards):
    n = len(shards)
    split = [s.shape[0] >= PAIR_SPLIT_MIN_ROWS for s in shards]

    def body(*refs):
        w_refs, o_refs = refs[:n], refs[n:2 * n]
        send_sems, recv_sems = refs[2 * n:]
        x, y, c = _mesh_pos()
        me = 2 * x + y
        chips = _other_chips(x, y)

        def rows_of(t, core):
            rows = shards[t].shape[0]
            if not split[t]:
                return pl.ds(0, rows)
            return pl.ds(pl.multiple_of(core * (rows // 2), 16), rows // 2)

        def landed(t, k, slot, rows, to):
            ref = o_refs[t].at[slot, rows]
            return pltpu.make_async_remote_copy(src_ref=ref, dst_ref=ref, send_sem=send_sems.at[6 * t + k],
                                                recv_sem=recv_sems.at[6 * t + k], device_id=to, device_id_type=MESH)

        sends = []
        for t in range(n):
            mine = rows_of(t, c)
            for k, (px, py) in enumerate(chips):
                cp = pltpu.make_async_remote_copy(src_ref=w_refs[t].at[mine], dst_ref=o_refs[t].at[me, mine],
                                                  send_sem=send_sems.at[6 * t + k], recv_sem=recv_sems.at[6 * t + k],
                                                  device_id=(px, py, c), device_id_type=MESH)
                cp.start()
                sends.append(cp)
        for t in range(n):
            mine = rows_of(t, c)
            for k, (px, py) in enumerate(chips):
                landed(t, k, 2 * px + py, mine, (x, y, c)).wait_recv()
                if split[t]:
                    cp = landed(t, 3 + k, 2 * px + py, mine, (x, y, 1 - c))
                    cp.start()
                    sends.append(cp)
        for t in range(n):
            if split[t]:
                for k, (px, py) in enumerate(chips):
                    landed(t, 3 + k, 2 * px + py, rows_of(t, 1 - c), (x, y, c)).wait_recv()
        for cp in sends:
            cp.wait_send()

    return pl.pallas_call(
        body,
        out_shape=[jax.ShapeDtypeStruct((4,) + s.shape, s.dtype) for s in shards],
        in_specs=[_ANY] * n,
        out_specs=[_ANY] * n,
        scratch_shapes=[pltpu.SemaphoreType.DMA((6 * n,)), pltpu.SemaphoreType.DMA((6 * n,))],
        name="weight_gather",
    )(*shards)


def _pair_swap(gps):
    n = len(gps)

    def body(*refs):
        g_refs, o_refs = refs[:n], refs[n:2 * n]
        send_sems, recv_sems = refs[2 * n:]
        x, y, c = _mesh_pos()
        copies = []
        for t in range(n):
            half = gps[t].shape[1] // 2
            theirs = pl.ds(pl.multiple_of((1 - c) * half, 8), half)
            cp = pltpu.make_async_remote_copy(src_ref=g_refs[t].at[:, theirs], dst_ref=o_refs[t],
                                              send_sem=send_sems.at[t], recv_sem=recv_sems.at[t],
                                              device_id=(x, y, 1 - c), device_id_type=MESH)
            cp.start()
            copies.append(cp)
        for cp in copies:
            cp.wait_send()
            cp.wait_recv()

    return pl.pallas_call(
        body,
        out_shape=[jax.ShapeDtypeStruct((4, g.shape[1] // 2, g.shape[2]), g.dtype) for g in gps],
        in_specs=[_ANY] * n,
        out_specs=[_ANY] * n,
        scratch_shapes=[pltpu.SemaphoreType.DMA((n,)), pltpu.SemaphoreType.DMA((n,))],
        name="grad_pair_swap",
    )(*gps)


def _chip_scatter(parts):
    n = len(parts)

    def body(*refs):
        s_refs, o_refs = refs[:n], refs[n:2 * n]
        send_sems, recv_sems = refs[2 * n:]
        x, y, c = _mesh_pos()
        me = 2 * x + y
        chips = _other_chips(x, y)
        sends = []
        for t in range(n):
            for k, (px, py) in enumerate(chips):
                cp = pltpu.make_async_remote_copy(src_ref=s_refs[t].at[2 * px + py], dst_ref=o_refs[t].at[me],
                                                  send_sem=send_sems.at[3 * t + k], recv_sem=recv_sems.at[3 * t + k],
                                                  device_id=(px, py, c), device_id_type=MESH)
                cp.start()
                sends.append(cp)
        for t in range(n):
            for k, (px, py) in enumerate(chips):
                pltpu.make_async_remote_copy(src_ref=s_refs[t].at[me], dst_ref=o_refs[t].at[2 * px + py],
                                             send_sem=send_sems.at[3 * t + k], recv_sem=recv_sems.at[3 * t + k],
                                             device_id=(x, y, c), device_id_type=MESH).wait_recv()
        for cp in sends:
            cp.wait_send()

    return pl.pallas_call(
        body,
        out_shape=[jax.ShapeDtypeStruct(p.shape, p.dtype) for p in parts],
        in_specs=[_ANY] * n,
        out_specs=[_ANY] * n,
        scratch_shapes=[pltpu.SemaphoreType.DMA((3 * n,)), pltpu.SemaphoreType.DMA((3 * n,))],
        name="grad_chip_scatter",
    )(*parts)


def _pair_join(fs):
    n = len(fs)

    def body(*refs):
        f_refs, o_refs = refs[:n], refs[n:2 * n]
        send_sems, recv_sems = refs[2 * n:]
        x, y, c = _mesh_pos()
        sends = []
        for t in range(n):
            cp = pltpu.make_async_remote_copy(src_ref=f_refs[t].at[c], dst_ref=o_refs[t].at[c], send_sem=send_sems.at[t],
                                              recv_sem=recv_sems.at[t], device_id=(x, y, 1 - c), device_id_type=MESH)
            cp.start()
            sends.append(cp)
        for t in range(n):
            pltpu.make_async_remote_copy(src_ref=f_refs[t].at[c], dst_ref=o_refs[t].at[1 - c], send_sem=send_sems.at[t],
                                         recv_sem=recv_sems.at[t], device_id=(x, y, c), device_id_type=MESH).wait_recv()
        for cp in sends:
            cp.wait_send()

    return pl.pallas_call(
        body,
        out_shape=[jax.ShapeDtypeStruct(f.shape, f.dtype) for f in fs],
        in_specs=[_ANY] * n,
        out_specs=[_ANY] * n,
        input_output_aliases={t: t for t in range(n)},
        scratch_shapes=[pltpu.SemaphoreType.DMA((n,)), pltpu.SemaphoreType.DMA((n,))],
        name="grad_pair_join",
    )(*fs)


def _rope_tables(Lp):
    inv = 1.0 / (ROPE_BASE ** (jnp.arange(0, ROPE, 2, dtype=F32) / ROPE))
    ang = (jnp.arange(Lp, dtype=F32) - FRONT)[:, None] * inv[None, :]
    cs, sn = jnp.cos(ang), jnp.sin(ang)
    return jnp.tile(cs, (1, 4)), jnp.concatenate([-sn, sn, -sn, sn], axis=1)


def _local_step(x, loss_target, meta, norm_g, w_in, gate_w, gate_b, gla_norm_g, gla_proj, q_norm_g, w_uq,
                kv_norm_g, w_ukv, mla_proj, w_out, final_norm_g):
    B, seq, _ = x.shape
    Lp = HEAD_ROWS + seq
    T = B * Lp
    tr = _div_tile(Lp, 544, 16)
    tq = _div_tile(T, 1024, QB)

    cuts = np.cumsum((0,) + SPLITS)
    col = lambda i: w_in[:, cuts[i]:cuts[i + 1]]
    w_q, w_k, w_v, w_lr, w_z, w_cq, w_ckv, w_kr, w_mz, w_gg, w_gm = [col(i) for i in range(11)]
    pad_cols = lambda w, n: jnp.pad(w, ((0, 0), (0, n - w.shape[1])))
    wA = jnp.concatenate([w_v, w_z, w_mz, w_gg, w_gm, w_q, w_k], axis=1)
    wB = jnp.concatenate([w_cq, w_ckv, pad_cols(w_lr, 128), pad_cols(w_kr, 128)], axis=1)
    wg = jnp.pad(gate_w, ((0, 128 - GLA_RANK), (0, 0)))
    wuq2 = jnp.pad(w_uq.reshape(Q_RANK, MLA_H, MLA_QK), ((0, 0), (0, 0), (0, 256 - MLA_QK))).reshape(Q_RANK, 2048)
    gn4 = jnp.tile(gla_norm_g, (1, GLA_H))
    cos_t, sin_t = _rope_tables(Lp)

    u = _rms_in(x, meta, norm_g, B, Lp)
    projA = _mm(u, wA, name="in_proj_a", tm=tq, tn=1024, tk=D)
    projB = _mm(u, wB, name="in_proj_b", tm=tq, tn=640, tk=D)
    oa, ya_in, ssave = _gla_fwd(projA, projB, wg, gate_b, gn4, B, Lp)
    ya = _mm(ya_in, gla_proj, name="gla_proj", tm=tq, tn=512, tk=D)
    q_att, k_att, v_att, cqn, ckvn = _mla_prep(projB, cos_t, sin_t, q_norm_g, kv_norm_g, wuq2, w_ukv, B, Lp, tr)
    ob, yb_in, lse_c = _attn_fwd(q_att, k_att, v_att, projA, B, Lp)
    yb = _mm(yb_in, mla_proj, name="mla_proj", tm=tq, tn=512, tk=D)
    merged = _merge_fwd(projA, ya, yb, tr)
    mo = _mm(merged, w_out, name="w_out", tm=tq, tn=512, tk=D)
    dh1, dh1_b, loss, d_gf = _final_loss(x, meta, mo, final_norm_g.reshape(1, D), loss_target, B, Lp)

    dmerged = _mm(dh1_b, w_out, name="d_merged", trans_b=True, tm=tq, tn=512, tk=D)
    g_w_out = _mm(merged, dh1_b, name="dw_out", trans_a=True, tm=D, tn=512, tk=tq)
    dya, dyb, dA = _merge_bwd(dmerged, projA, ya, yb, tr)
    dya_in = _mm(dya, gla_proj, name="d_ya_in", trans_b=True, tm=tq, tn=512, tk=D)
    g_gla_proj = _mm(ya_in, dya, name="dw_gla_proj", trans_a=True, tm=D, tn=512, tk=tq)
    dyb_in = _mm(dyb, mla_proj, name="d_yb_in", trans_b=True, tm=tq, tn=512, tk=D)
    g_mla_proj = _mm(yb_in, dyb, name="dw_mla_proj", trans_a=True, tm=D, tn=512, tk=tq)
    doa, dBz, d_gn = _gla_out_bwd(dya_in, oa, projA, gn4, tr)
    dC, g_wg, d_bg = _gla_bwd(projA, projB, ssave, doa, wg, gate_b, B, Lp)
    do, dDz, delta_c = _attn_bwd_pre(dyb_in, projA, ob, B, Lp)
    dq, dk, dv = _attn_bwd(q_att, k_att, v_att, do, lse_c, delta_c, B, Lp)
    dqf, dkvf, dE, d_gq, d_gkv = _mla_bwd_post(dq, dk, dv, projB, cos_t, sin_t, q_norm_g, kv_norm_g,
                                                wuq2, w_ukv, B, Lp, tr)
    g_wuq2 = _mm(cqn, dqf, name="dw_uq", trans_a=True, tm=Q_RANK, tn=512, tk=tq)
    g_wukv = _mm(ckvn, dkvf, name="dw_ukv", trans_a=True, tm=KV_RANK, tn=512, tk=tq)
    dparts = [dA, dBz, dC, dDz, dE]
    g_in = [_mm(u, dp, name="dw_in_%d" % i, trans_a=True, tm=D, tn=_div_tile(dp.shape[1], 1024, 256), tk=tq)
            for i, dp in enumerate(dparts)]
    du = _du_matmul(dA, dBz, dC, dDz, dE, wA, wB, QB)
    grad_x, d_meta, d_ng = _in_norm_bwd(x, meta, dh1, du, norm_g, B, Lp)

    gA, gBz, gC, gDz, gE = g_in
    g_w_in = jnp.concatenate([
        gC[:, 1024:1536], gC[:, 1536:2048], gC[:, 0:1024], gC[:, 2048:2048 + GLA_RANK], gBz,
        gE[:, 0:Q_RANK], gE[:, Q_RANK:Q_RANK + KV_RANK], gE[:, 384:384 + ROPE], gDz, gA[:, 0:D], gA[:, D:2 * D]], axis=1)
    g_wuq = g_wuq2.reshape(Q_RANK, MLA_H, 256)[:, :, :MLA_QK].reshape(Q_RANK, MLA_H * MLA_QK)
    grads = dict(w_in=g_w_in, gla_gate_w=g_wg[:GLA_RANK], gla_proj=g_gla_proj, mla_w_uq=g_wuq, mla_w_ukv=g_wukv,
                 mla_proj=g_mla_proj, w_out=g_w_out, meta_tokens=d_meta, norm_g=d_ng, gla_gate_b=d_bg,
                 gla_norm_g=d_gn, mla_q_norm_g=d_gq, mla_kv_norm_g=d_gkv, final_norm_g=d_gf)
    return loss[0, 0], grad_x, grads


_MATS = ("w_in", "gla_gate_w", "gla_proj", "mla_w_uq", "mla_w_ukv", "mla_proj", "w_out")
_ROW_SHARDED = ("gla_proj", "mla_proj", "w_out")
_ORDER = ("meta_tokens", "norm_g", "w_in", "gla_gate_w", "gla_gate_b", "gla_norm_g", "gla_proj", "mla_q_norm_g",
          "mla_w_uq", "mla_kv_norm_g", "mla_w_ukv", "mla_proj", "w_out", "final_norm_g")
WIRE_BF16_MIN_ELEMS = 128 * 128
SMALL_PACK_ROWS = 16


def _pack_small(d):
    rows = [jnp.pad(d[n].reshape(1, size), ((0, 0), (0, D - size))) for n, size in SMALL]
    return jnp.pad(jnp.concatenate(rows, axis=0), ((0, SMALL_PACK_ROWS - len(rows)), (0, 0)))


def _unpack_small(packed):
    return {n: packed[i, :size] for i, (n, size) in enumerate(SMALL)}


def kernel(x, meta_tokens, norm_g, w_in, gla_gate_w, gla_gate_b, gla_norm_g, gla_proj, mla_q_norm_g, mla_w_uq, mla_kv_norm_g, mla_w_ukv, mla_proj, w_out, final_norm_g, loss_target, m_meta_tokens, m_norm_g, m_w_in, m_gla_gate_w, m_gla_gate_b, m_gla_norm_g, m_gla_proj, m_mla_q_norm_g, m_mla_w_uq, m_mla_kv_norm_g, m_mla_w_ukv, m_mla_proj, m_w_out, m_final_norm_g, v_meta_tokens, v_norm_g, v_w_in, v_gla_gate_w, v_gla_gate_b, v_gla_norm_g, v_gla_proj, v_mla_q_norm_g, v_mla_w_uq, v_mla_kv_norm_g, v_mla_w_ukv, v_mla_proj, v_w_out, v_final_norm_g):
    w = dict(meta_tokens=meta_tokens, norm_g=norm_g, w_in=w_in[0], gla_gate_w=gla_gate_w[0], gla_gate_b=gla_gate_b,
             gla_norm_g=gla_norm_g, gla_proj=gla_proj[0], mla_q_norm_g=mla_q_norm_g, mla_w_uq=mla_w_uq[0],
             mla_kv_norm_g=mla_kv_norm_g, mla_w_ukv=mla_w_ukv[0], mla_proj=mla_proj[0], w_out=w_out[0],
             final_norm_g=final_norm_g)
    mom = dict(meta_tokens=m_meta_tokens, norm_g=m_norm_g, w_in=m_w_in[0], gla_gate_w=m_gla_gate_w[0],
               gla_gate_b=m_gla_gate_b, gla_norm_g=m_gla_norm_g, gla_proj=m_gla_proj[0], mla_q_norm_g=m_mla_q_norm_g,
               mla_w_uq=m_mla_w_uq[0], mla_kv_norm_g=m_mla_kv_norm_g, mla_w_ukv=m_mla_w_ukv[0], mla_proj=m_mla_proj[0],
               w_out=m_w_out[0], final_norm_g=m_final_norm_g)
    var = dict(meta_tokens=v_meta_tokens, norm_g=v_norm_g, w_in=v_w_in[0], gla_gate_w=v_gla_gate_w[0],
               gla_gate_b=v_gla_gate_b, gla_norm_g=v_gla_norm_g, gla_proj=v_gla_proj[0], mla_q_norm_g=v_mla_q_norm_g,
               mla_w_uq=v_mla_w_uq[0], mla_kv_norm_g=v_mla_kv_norm_g, mla_w_ukv=v_mla_w_ukv[0], mla_proj=v_mla_proj[0],
               w_out=v_w_out[0], final_norm_g=v_final_norm_g)
    out_shapes = {n: a.shape for n, a in zip(_ORDER, (meta_tokens, norm_g, w_in, gla_gate_w, gla_gate_b, gla_norm_g,
                                                     gla_proj, mla_q_norm_g, mla_w_uq, mla_kv_norm_g, mla_w_ukv,
                                                     mla_proj, w_out, final_norm_g))}

    me = (2 * lax.axis_index("x") + lax.axis_index("y")).astype(jnp.int32)
    shards = [w[n].astype(BF16) for n in _MATS] + [meta_tokens]
    gathered = [lax.dynamic_update_slice(gth, own[None], (me, 0, 0))
                for gth, own in zip(_weight_gather(shards), shards)]
    full = {}
    for name, gth in zip(_MATS, gathered):
        if name in _ROW_SHARDED:
            full[name] = gth.reshape(4 * gth.shape[1], gth.shape[2])
        else:
            full[name] = gth.transpose(1, 0, 2).reshape(gth.shape[1], 4 * gth.shape[2])
    meta_full = gathered[-1].transpose(1, 0, 2).reshape(N_META, D)

    loss_local, grad_x, g = _local_step(
        x, loss_target, meta_full, norm_g, full["w_in"], full["gla_gate_w"], gla_gate_b, gla_norm_g, full["gla_proj"],
        mla_q_norm_g, full["mla_w_uq"], mla_kv_norm_g, full["mla_w_ukv"], full["mla_proj"], full["w_out"], final_norm_g)
    loss = lax.psum(loss_local, ("x", "y", "c"))

    def by_owner(name, arr):
        if name in _ROW_SHARDED:
            return arr.reshape(4, arr.shape[0] // 4, arr.shape[1])
        return arr.reshape(arr.shape[0], 4, arr.shape[1] // 4).transpose(1, 0, 2)

    names = _MATS + ("meta_tokens",)
    gps = [by_owner(n, g[n]) for n in names] + [jnp.broadcast_to(_pack_small(g)[None], (4, SMALL_PACK_ROWS, D))]
    recvs = _pair_swap(gps)
    c_idx = lax.axis_index("c").astype(jnp.int32).reshape(1)
    s1 = [_pair_add_big(gps[0], recvs[0], c_idx)] + list(_pair_add_small(gps[1:], recvs[1:]))
    landed = _chip_scatter(s1)
    pos = jnp.stack([c_idx[0], me])
    halves = [_sum_chips_big(landed[0], s1[0], pos)] + list(_sum_chips_small(landed[1:], s1[1:]))
    g_red = [j.reshape(2 * j.shape[1], j.shape[2]) for j in _pair_join(halves)]

    tens = lambda d: [d[n].reshape(g_red[i].shape) for i, n in enumerate(names)] + [_pack_small(d)]
    w_t, m_t, v_t = tens(w), tens(mom), tens(var)
    big = _adamw_big(w_t[0], g_red[0], m_t[0], v_t[0])
    rest = _adamw_small(w_t[1:], g_red[1:], m_t[1:], v_t[1:])
    k = len(names)
    results = {"grad": g_red}
    for i, kind in enumerate(("delta", "new_m", "new_v")):
        results[kind] = [big[i]] + list(rest[i * k:(i + 1) * k])

    outs = []
    for kind in ("grad", "delta", "new_m", "new_v"):
        vals = dict(zip(names, results[kind][:-1]))
        vals.update(_unpack_small(results[kind][-1]))
        outs += [vals[n].reshape(out_shapes[n]) for n in _ORDER]
    return (loss, grad_x, *outs)
```

```python
import functools
import math

import jax
import jax.numpy as jnp
import numpy as np
from jax import lax
from jax.experimental import pallas as pl
from jax.experimental.pallas import tpu as pltpu

F32 = jnp.float32
BF16 = jnp.bfloat16

D = 1024
N_META = 16
QB = 256
FRONT = QB - N_META
HEAD_ROWS = FRONT + N_META
assert FRONT % 64 == 48
EPS = 1e-6

GLA_H, GLA_DK, GLA_DV, GLA_RANK, GLA_C = 4, 128, 256, 16, 64
GLA_NORMALIZER = 16.0
GLA_KW, GLA_VW = GLA_H * GLA_DK, GLA_H * GLA_DV
MLA_H, NOPE, ROPE, MLA_DV, Q_RANK, KV_RANK = 8, 128, 64, 128, 256, 128
MLA_QK = NOPE + ROPE
ROPE_BASE = 10000.0
SPLITS = (GLA_KW, GLA_KW, GLA_VW, GLA_RANK, GLA_VW, Q_RANK, KV_RANK, ROPE, MLA_H * MLA_DV, D, D)
IN_WIDTH = sum(SPLITS)

ADAM_LR, ADAM_B1, ADAM_B2, ADAM_EPS, ADAM_WD, ADAM_STEP = 0.001, 0.9, 0.999, 1e-08, 0.01, 10

LANES = 128
VMEM_CAP_V7X = 56 * 1024 * 1024
MESH = pl.DeviceIdType.MESH
NEG = -1e30

SMALL = (("norm_g", D), ("gla_gate_b", GLA_KW), ("gla_norm_g", GLA_DV), ("mla_q_norm_g", Q_RANK),
         ("mla_kv_norm_g", KV_RANK), ("final_norm_g", D))


def _div_tile(n, target, mult):
    best = None
    for d in range(mult, min(n, target) + 1, mult):
        if n % d == 0:
            best = d
    assert best is not None, (n, target, mult)
    return best


def _params(sem, block_bytes, scratch_bytes=0):
    est = 2 * block_bytes + scratch_bytes + 12 * 1024 * 1024
    return pltpu.CompilerParams(dimension_semantics=sem, vmem_limit_bytes=int(min(max(est, 24 * 1024 * 1024), VMEM_CAP_V7X)))


def _nbytes(shape, dtype):
    return int(np.prod(shape)) * jnp.dtype(dtype).itemsize


def _sigmoid(x):
    return 1.0 / (1.0 + jnp.exp(-x))


def _nt(a, b):
    return lax.dot_general(a, b, (((1,), (1,)), ((), ())), preferred_element_type=F32)


def _tn(a, b):
    return lax.dot_general(a, b, (((0,), (0,)), ((), ())), preferred_element_type=F32)


def _nn(a, b):
    return jnp.dot(a, b, preferred_element_type=F32)


def _split3(x):
    a = x.astype(BF16)
    r = x - a.astype(F32)
    b = r.astype(BF16)
    c = (r - b.astype(F32)).astype(BF16)
    return a, b, c


def _mm(a, b, *, name, trans_a=False, trans_b=False, out_dtype=F32, tm, tn, tk):
    assert not (trans_a and trans_b)
    if trans_a:
        K, M = a.shape
    else:
        M, K = a.shape
    N = b.shape[0] if trans_b else b.shape[1]
    assert (b.shape[1] if trans_b else b.shape[0]) == K
    assert M % tm == 0 and N % tn == 0 and K % tk == 0, (name, M, N, K, tm, tn, tk)
    nk = K // tk

    def body(a_ref, b_ref, o_ref, *scratch):
        av = a_ref[...].astype(BF16)
        bv = b_ref[...].astype(BF16)
        prod = _tn(av, bv) if trans_a else (_nt(av, bv) if trans_b else _nn(av, bv))
        if nk == 1:
            o_ref[...] = prod.astype(out_dtype)
        else:
            acc = scratch[0]
            k = pl.program_id(2)

            @pl.when(k == 0)
            def _():
                acc[...] = prod

            @pl.when(k > 0)
            def _():
                acc[...] += prod

            @pl.when(k == nk - 1)
            def _():
                o_ref[...] = acc[...].astype(out_dtype)

    if trans_a:
        a_spec = pl.BlockSpec((tk, tm), lambda i, j, k: (k, i))
    else:
        a_spec = pl.BlockSpec((tm, tk), lambda i, j, k: (i, k))
    if trans_b:
        b_spec = pl.BlockSpec((tn, tk), lambda i, j, k: (j, k))
    else:
        b_spec = pl.BlockSpec((tk, tn), lambda i, j, k: (k, j))
    blocks = (_nbytes((tm, tk), a.dtype) + _nbytes((tk, tn), b.dtype) + _nbytes((tm, tn), out_dtype))
    scratch = [pltpu.VMEM((tm, tn), F32)] if nk > 1 else []
    return pl.pallas_call(
        body,
        out_shape=jax.ShapeDtypeStruct((M, N), out_dtype),
        grid=(M // tm, N // tn, nk),
        in_specs=[a_spec, b_spec],
        out_specs=pl.BlockSpec((tm, tn), lambda i, j, k: (i, j)),
        scratch_shapes=scratch,
        compiler_params=_params(("parallel", "parallel", "arbitrary"), blocks + _nbytes((tm, tn), F32),
                                _nbytes((tm, tn), F32) if nk > 1 else 0),
        name=name,
    )(a, b)


def _h_tile(j, x_ref, meta_ref):
    head = jnp.concatenate([jnp.zeros((FRONT, D), F32), meta_ref[...]], axis=0)
    return jnp.where(j > 0, x_ref[0], head)


def _x_spec():
    return pl.BlockSpec((1, QB, D), lambda b, j: (b, jnp.maximum(j - 1, 0), 0))


def _rms_in(x, meta, g, B, Lp):
    T = B * Lp
    NQ = Lp // QB

    def body(x_ref, meta_ref, g_ref, u_ref):
        h = _h_tile(pl.program_id(1), x_ref, meta_ref)
        r = lax.rsqrt(jnp.mean(h * h, axis=-1, keepdims=True) + EPS)
        u_ref[...] = (h * r * g_ref[...]).astype(BF16)

    return pl.pallas_call(
        body,
        out_shape=jax.ShapeDtypeStruct((T, D), BF16),
        grid=(B, NQ),
        in_specs=[_x_spec(), pl.BlockSpec((N_META, D), lambda b, j: (0, 0)), pl.BlockSpec((1, D), lambda b, j: (0, 0))],
        out_specs=pl.BlockSpec((QB, D), lambda b, j: (b * NQ + j, 0)),
        compiler_params=_params(("parallel", "parallel"), _nbytes((QB, D), F32) * 2),
        name="rms_in",
    )(x, meta, g)


def _gla_gate(lr, wg, bg, valid):
    pre = _nn(lr.astype(BF16), wg) + bg
    logsig = jnp.minimum(pre, 0.0) - jnp.log(1.0 + jnp.exp(-jnp.abs(pre)))
    return pre, jnp.where(valid, logsig / GLA_NORMALIZER, 0.0)


def _tri_masks():
    ri = lax.broadcasted_iota(jnp.int32, (GLA_C, GLA_C), 0)
    ci = lax.broadcasted_iota(jnp.int32, (GLA_C, GLA_C), 1)
    return ci <= ri, ci >= ri


def _cumsum_rows(x, ones_mask):
    w = jnp.where(ones_mask, 1.0, 0.0).astype(BF16)
    a, b, c = _split3(x)
    return _nn(w, a) + _nn(w, b) + _nn(w, c)


def _gla_fwd(projA, projB, wg, bg, gn4, B, Lp):
    T = B * Lp
    NC = Lp // GLA_C
    C = GLA_C
    scale = GLA_DK ** -0.5

    def body(q_ref, k_ref, v_ref, lr_ref, z_ref, wg_ref, bg_ref, gn_ref, oa_ref, ya_ref, ssave_ref, st_ref):
        n = pl.program_id(0)

        @pl.when(n == 0)
        def _():
            st_ref[...] = jnp.zeros_like(st_ref)

        pos = n * C + lax.broadcasted_iota(jnp.int32, (C, 1), 0)
        lower, _ = _tri_masks()
        is_last = lax.broadcasted_iota(jnp.int32, (C, 1), 0) == C - 1
        for b in range(B):
            ssave_ref[b, 0] = st_ref[b]
            _, glog = _gla_gate(lr_ref[b], wg_ref[...], bg_ref[...], pos >= FRONT)
            bcum = _cumsum_rows(glog, lower)
            for h in range(GLA_H):
                ks = slice(h * GLA_DK, (h + 1) * GLA_DK)
                vs = slice(h * GLA_DV, (h + 1) * GLA_DV)
                bh = bcum[:, ks]
                blast = jnp.sum(jnp.where(is_last, bh, 0.0), axis=0, keepdims=True)
                qh = q_ref[b, :, ks] * scale
                kh = k_ref[b, :, ks]
                qe = (qh * jnp.exp(bh)).astype(BF16)
                ke = (kh * jnp.exp(-bh)).astype(BF16)
                kl = (kh * jnp.exp(blast - bh)).astype(BF16)
                vh = v_ref[b, :, vs].astype(BF16)
                a = jnp.where(lower, _nt(qe, ke), 0.0).astype(BF16)
                st = st_ref[b, h]
                o = _nn(a, vh) + _nt(qe, st.astype(BF16))
                st_ref[b, h] = st * jnp.exp(blast) + _tn(vh, kl)
                oa_ref[b, :, vs] = o
                on = o * lax.rsqrt(jnp.mean(o * o, axis=-1, keepdims=True) + EPS) * gn_ref[:, vs]
                z = z_ref[b, :, vs]
                ya_ref[b, :, vs] = (on * (z * _sigmoid(z))).astype(BF16)

    blocks = B * (_nbytes((C, 512), F32) * 2 + _nbytes((C, 1024), F32) * 3 + _nbytes((C, 1024), BF16)
                  + _nbytes((GLA_H, GLA_DV, GLA_DK), F32)) + _nbytes((128, 512), BF16)
    state = _nbytes((B, GLA_H, GLA_DV, GLA_DK), F32)
    pa = projA.reshape(B, Lp, projA.shape[1])
    oa, ya, ssave = pl.pallas_call(
        body,
        out_shape=(jax.ShapeDtypeStruct((B, Lp, GLA_VW), F32), jax.ShapeDtypeStruct((B, Lp, GLA_VW), BF16),
                   jax.ShapeDtypeStruct((B, NC, GLA_H, GLA_DV, GLA_DK), F32)),
        grid=(NC,),
        in_specs=[
            pl.BlockSpec((B, C, 512), lambda n: (0, n, 10)),
            pl.BlockSpec((B, C, 512), lambda n: (0, n, 11)),
            pl.BlockSpec((B, C, 1024), lambda n: (0, n, 0)),
            pl.BlockSpec((B, C, 128), lambda n: (0, n, 3)),
            pl.BlockSpec((B, C, 1024), lambda n: (0, n, 1)),
            pl.BlockSpec((128, 512), lambda n: (0, 0)),
            pl.BlockSpec((1, 512), lambda n: (0, 0)),
            pl.BlockSpec((1, 1024), lambda n: (0, 0)),
        ],
        out_specs=(pl.BlockSpec((B, C, 1024), lambda n: (0, n, 0)),
                   pl.BlockSpec((B, C, 1024), lambda n: (0, n, 0)),
                   pl.BlockSpec((B, 1, GLA_H, GLA_DV, GLA_DK), lambda n: (0, n, 0, 0, 0))),
        scratch_shapes=[pltpu.VMEM((B, GLA_H, GLA_DV, GLA_DK), F32)],
        compiler_params=_params(("arbitrary",), blocks, state),
        name="gla_fwd",
    )(pa, pa, pa, projB.reshape(B, Lp, projB.shape[1]), pa, wg, bg, gn4)
    return oa.reshape(T, GLA_VW), ya.reshape(T, GLA_VW), ssave


def _swap_halves(x):
    lane = lax.broadcasted_iota(jnp.int32, x.shape, 1)
    return jnp.where((lane % 64) < 32, pltpu.roll(x, 96, 1), pltpu.roll(x, 32, 1))


def _mla_prep(projB, cos_t, sin_t, gq, gkv, wuq2, wukv, B, Lp, tr):
    T = B * Lp
    nt = Lp // tr
    HW = 2 * LANES

    def body(pb_ref, cos_ref, sin_ref, gq_ref, gkv_ref, wuq_ref, wukv_ref, q_ref, k_ref, v_ref, cqn_ref, ckvn_ref):
        cq = pb_ref[:, 0:Q_RANK]
        ckv = pb_ref[:, Q_RANK:Q_RANK + KV_RANK]
        kr = pb_ref[:, 512:640]
        cqn = (cq * lax.rsqrt(jnp.mean(cq * cq, axis=-1, keepdims=True) + EPS) * gq_ref[...]).astype(BF16)
        ckvn = (ckv * lax.rsqrt(jnp.mean(ckv * ckv, axis=-1, keepdims=True) + EPS) * gkv_ref[...]).astype(BF16)
        cqn_ref[...] = cqn
        ckvn_ref[...] = ckvn
        qf = _nn(cqn, wuq_ref[...])
        kvf = _nn(ckvn, wukv_ref[...])
        cs = cos_ref[...]
        sn = sin_ref[...]
        rope = lambda t: t * cs + _swap_halves(t) * sn
        kr_r = rope(kr).astype(BF16)
        for h in range(MLA_H):
            q_ref[:, h * HW:h * HW + LANES] = qf[:, h * HW:h * HW + LANES].astype(BF16)
            q_ref[:, h * HW + LANES:(h + 1) * HW] = rope(qf[:, h * HW + LANES:(h + 1) * HW]).astype(BF16)
            k_ref[:, h * HW:h * HW + LANES] = kvf[:, h * HW:h * HW + LANES].astype(BF16)
            k_ref[:, h * HW + LANES:(h + 1) * HW] = kr_r
            v_ref[:, h * MLA_DV:(h + 1) * MLA_DV] = kvf[:, h * HW + LANES:(h + 1) * HW].astype(BF16)

    blocks = (_nbytes((tr, 640), F32) + 2 * _nbytes((tr, 128), F32) + _nbytes((Q_RANK, 2048), BF16)
              + _nbytes((KV_RANK, 2048), BF16) + _nbytes((tr, 2048 * 2 + 1024 + 384), BF16)
              + 2 * _nbytes((tr, 2048), F32))
    return pl.pallas_call(
        body,
        out_shape=(jax.ShapeDtypeStruct((T, MLA_H * HW), BF16), jax.ShapeDtypeStruct((T, MLA_H * HW), BF16),
                   jax.ShapeDtypeStruct((T, MLA_H * MLA_DV), BF16), jax.ShapeDtypeStruct((T, Q_RANK), BF16),
                   jax.ShapeDtypeStruct((T, KV_RANK), BF16)),
        grid=(B, nt),
        in_specs=[
            pl.BlockSpec((tr, 640), lambda b, j: (b * nt + j, 0)),
            pl.BlockSpec((tr, 128), lambda b, j: (j, 0)),
            pl.BlockSpec((tr, 128), lambda b, j: (j, 0)),
            pl.BlockSpec((1, Q_RANK), lambda b, j: (0, 0)),
            pl.BlockSpec((1, KV_RANK), lambda b, j: (0, 0)),
            pl.BlockSpec((Q_RANK, 2048), lambda b, j: (0, 0)),
            pl.BlockSpec((KV_RANK, 2048), lambda b, j: (0, 0)),
        ],
        out_specs=(pl.BlockSpec((tr, 2048), lambda b, j: (b * nt + j, 0)),
                   pl.BlockSpec((tr, 2048), lambda b, j: (b * nt + j, 0)),
                   pl.BlockSpec((tr, 1024), lambda b, j: (b * nt + j, 0)),
                   pl.BlockSpec((tr, Q_RANK), lambda b, j: (b * nt + j, 0)),
                   pl.BlockSpec((tr, KV_RANK), lambda b, j: (b * nt + j, 0))),
        compiler_params=_params(("parallel", "parallel"), blocks),
        name="mla_prep",
    )(projB, cos_t, sin_t, gq, gkv, wuq2, wukv)


def _attn_mask(row, col):
    return (col <= row) & ((col >= FRONT) | (row < FRONT))


def _attn_fwd(q_att, k_att, v_att, projA, B, Lp):
    T = B * Lp
    NQ = Lp // QB
    HW = 2 * LANES
    scale = 1.0 / math.sqrt(MLA_QK)

    def body(q_ref, k_ref, v_ref, mz_ref, o_ref, yb_ref, lsec_ref, m_ref, l_ref, acc_ref):
        qi = pl.program_id(1)
        m_ref[...] = jnp.full(m_ref.shape, NEG, F32)
        l_ref[...] = jnp.zeros_like(l_ref)
        acc_ref[...] = jnp.zeros_like(acc_ref)
        row = qi * QB + lax.broadcasted_iota(jnp.int32, (QB, QB), 0)
        coli = lax.broadcasted_iota(jnp.int32, (QB, QB), 1)

        def step(kj, carry):
            off = pl.multiple_of(kj * QB, QB)
            ok = _attn_mask(row, kj * QB + coli)
            for h in range(MLA_H):
                q = q_ref[:, h * HW:(h + 1) * HW]
                kb = k_ref[pl.ds(off, QB), h * HW:(h + 1) * HW]
                vb = v_ref[pl.ds(off, QB), h * MLA_DV:(h + 1) * MLA_DV]
                s = jnp.where(ok, _nt(q, kb) * scale, NEG)
                m_old = m_ref[h]
                m_new = jnp.maximum(m_old, jnp.max(s, axis=-1, keepdims=True))
                alpha = jnp.exp(m_old - m_new)
                p = jnp.exp(s - jnp.tile(m_new, (1, QB // LANES)))
                m_ref[h] = m_new
                l_ref[h] = alpha * l_ref[h] + jnp.sum(p, axis=-1, keepdims=True)
                acc_ref[h] = alpha * acc_ref[h] + _nn(p.astype(BF16), vb)
            return carry

        lax.fori_loop(0, qi + 1, step, 0)
        for h in range(MLA_H):
            hs = slice(h * MLA_DV, (h + 1) * MLA_DV)
            l = l_ref[h]
            o = acc_ref[h] / l
            o_ref[:, hs] = o
            z = mz_ref[:, hs]
            yb_ref[:, hs] = (o * (z * _sigmoid(z))).astype(BF16)
            lse = m_ref[h] + jnp.log(l)
            lsec_ref[0, h, pl.ds(qi, 1), :] = jnp.transpose(lse)[0:1, :]

    blocks = (_nbytes((QB, 2048), BF16) + _nbytes((Lp, 2048), BF16) + _nbytes((Lp, 1024), BF16)
              + 2 * _nbytes((QB, 1024), F32) + _nbytes((QB, 1024), BF16) + _nbytes((MLA_H, QB, LANES), F32)
              + _nbytes((MLA_H, NQ, QB), F32))
    return pl.pallas_call(
        body,
        out_shape=(jax.ShapeDtypeStruct((T, MLA_H * MLA_DV), F32), jax.ShapeDtypeStruct((T, MLA_H * MLA_DV), BF16),
                   jax.ShapeDtypeStruct((B, MLA_H, NQ, QB), F32)),
        grid=(B, NQ),
        in_specs=[
            pl.BlockSpec((QB, MLA_H * HW), lambda b, i: (b * NQ + i, 0)),
            pl.BlockSpec((Lp, MLA_H * HW), lambda b, i: (b, 0)),
            pl.BlockSpec((Lp, MLA_H * MLA_DV), lambda b, i: (b, 0)),
            pl.BlockSpec((QB, 1024), lambda b, i: (b * NQ + i, 2)),
        ],
        out_specs=(pl.BlockSpec((QB, 1024), lambda b, i: (b * NQ + i, 0)),
                   pl.BlockSpec((QB, 1024), lambda b, i: (b * NQ + i, 0)),
                   pl.BlockSpec((1, MLA_H, NQ, QB), lambda b, i: (b, 0, 0, 0))),
        scratch_shapes=[pltpu.VMEM((MLA_H, QB, LANES), F32), pltpu.VMEM((MLA_H, QB, LANES), F32),
                        pltpu.VMEM((MLA_H, QB, MLA_DV), F32)],
        compiler_params=_params(("parallel", "arbitrary"), blocks, 3 * _nbytes((MLA_H, QB, LANES), F32)),
        name="attn_fwd",
    )(q_att, k_att, v_att, projA)


def _out_proj_loss(x, meta, projA, ya, yb, w_out, gf, tgt, B, Lp):
    T = B * Lp
    NQ = Lp // QB

    def body(x_ref, meta_ref, gg_ref, gm_ref, ya_ref, yb_ref, w_ref, gf_ref, t_ref,
             dh_ref, dhb_ref, mg_ref, loss_ref, dgf_ref):
        b = pl.program_id(0)
        j = pl.program_id(1)

        @pl.when((b == 0) & (j == 0))
        def _():
            loss_ref[...] = jnp.zeros_like(loss_ref)
            dgf_ref[...] = jnp.zeros_like(dgf_ref)

        merged = (_sigmoid(gg_ref[...]) * ya_ref[...] + _sigmoid(gm_ref[...]) * yb_ref[...]).astype(BF16)
        mg_ref[...] = merged
        h1 = _h_tile(j, x_ref, meta_ref) + _nn(merged, w_ref[...])
        r = lax.rsqrt(jnp.mean(h1 * h1, axis=-1, keepdims=True) + EPS)
        hn = h1 * r
        gfv = gf_ref[...]
        diff = jnp.where(j > 0, hn * gfv - t_ref[0], 0.0)
        loss_ref[...] += (0.5 / D) * jnp.sum(jnp.sum(diff * diff, axis=-1, keepdims=True), axis=0, keepdims=True)
        dout = diff * (1.0 / D)
        dgf_ref[...] += jnp.sum(dout * hn, axis=0, keepdims=True)
        dhn = dout * gfv
        dh = r * (dhn - hn * jnp.mean(dhn * hn, axis=-1, keepdims=True))
        dh_ref[...] = dh
        dhb_ref[...] = dh.astype(BF16)

    rows = lambda c: pl.BlockSpec((QB, D), lambda b, j: (b * NQ + j, c))
    const = lambda s: pl.BlockSpec(s, lambda b, j: (0, 0))
    return pl.pallas_call(
        body,
        out_shape=(jax.ShapeDtypeStruct((T, D), F32), jax.ShapeDtypeStruct((T, D), BF16),
                   jax.ShapeDtypeStruct((T, D), BF16), jax.ShapeDtypeStruct((1, 1), F32),
                   jax.ShapeDtypeStruct((1, D), F32)),
        grid=(B, NQ),
        in_specs=[_x_spec(), const((N_META, D)), rows(3), rows(4), rows(0), rows(0), const((D, D)),
                  const((1, D)), _x_spec()],
        out_specs=(rows(0), rows(0), rows(0), const((1, 1)), const((1, D))),
        compiler_params=_params(("arbitrary", "arbitrary"), 10 * _nbytes((QB, D), F32)),
        name="out_proj_loss",
    )(x, meta, projA, projA, ya, yb, w_out, gf, tgt)


def _merge_bwd(dh1_b, w_out, projA, ya, yb, tr):
    T = dh1_b.shape[0]

    def body(dh_ref, w_ref, gg_ref, gm_ref, ya_ref, yb_ref, dya_ref, dyb_ref, da_ref):
        d = _nt(dh_ref[...], w_ref[...])
        sg = _sigmoid(gg_ref[...])
        sm = _sigmoid(gm_ref[...])
        dya_ref[...] = (d * sg).astype(BF16)
        dyb_ref[...] = (d * sm).astype(BF16)
        da_ref[:, 0:D] = (d * ya_ref[...] * (sg * (1.0 - sg))).astype(BF16)
        da_ref[:, D:2 * D] = (d * yb_ref[...] * (sm * (1.0 - sm))).astype(BF16)

    spec = lambda c: pl.BlockSpec((tr, D), lambda i: (i, c))
    return pl.pallas_call(
        body,
        out_shape=(jax.ShapeDtypeStruct((T, D), BF16), jax.ShapeDtypeStruct((T, D), BF16),
                   jax.ShapeDtypeStruct((T, 2 * D), BF16)),
        grid=(T // tr,),
        in_specs=[spec(0), pl.BlockSpec((D, D), lambda i: (0, 0)), spec(3), spec(4), spec(0), spec(0)],
        out_specs=(spec(0), spec(0), pl.BlockSpec((tr, 2 * D), lambda i: (i, 0))),
        compiler_params=_params(("parallel",), 8 * _nbytes((tr, D), F32)),
        name="merge_bwd",
    )(dh1_b, w_out, projA, projA, ya, yb)


def _gla_out_bwd(dya, gla_proj, oa, projA, gn4, tr):
    T = dya.shape[0]
    nsteps = T // tr

    def body(dya_ref, w_ref, oa_ref, z_ref, gn_ref, do_ref, dz_ref, dgn_ref, acc_ref):
        i = pl.program_id(0)

        @pl.when(i == 0)
        def _():
            acc_ref[...] = jnp.zeros_like(acc_ref)

        dy_all = _nt(dya_ref[...], w_ref[...])
        for h in range(GLA_H):
            vs = slice(h * GLA_DV, (h + 1) * GLA_DV)
            dy = dy_all[:, vs]
            o = oa_ref[:, vs]
            z = z_ref[:, vs]
            gn = gn_ref[:, vs]
            s = _sigmoid(z)
            ra = lax.rsqrt(jnp.mean(o * o, axis=-1, keepdims=True) + EPS)
            on = o * ra
            don = dy * (z * s)
            t = don * gn
            do_ref[:, vs] = (ra * (t - on * jnp.mean(t * on, axis=-1, keepdims=True))).astype(BF16)
            dz_ref[:, vs] = (dy * (on * gn) * (s * (1.0 + z * (1.0 - s)))).astype(BF16)
            acc_ref[:, vs] += jnp.sum(don * on, axis=0, keepdims=True)

        @pl.when(i == nsteps - 1)
        def _():
            a = acc_ref[...]
            dgn_ref[...] = a[:, 0:256] + a[:, 256:512] + a[:, 512:768] + a[:, 768:1024]

    spec = lambda c: pl.BlockSpec((tr, D), lambda i: (i, c))
    return pl.pallas_call(
        body,
        out_shape=(jax.ShapeDtypeStruct((T, D), BF16), jax.ShapeDtypeStruct((T, D), BF16),
                   jax.ShapeDtypeStruct((1, GLA_DV), F32)),
        grid=(nsteps,),
        in_specs=[spec(0), pl.BlockSpec((D, D), lambda i: (0, 0)), spec(0), spec(1),
                  pl.BlockSpec((1, D), lambda i: (0, 0))],
        out_specs=(spec(0), spec(0), pl.BlockSpec((1, GLA_DV), lambda i: (0, 0))),
        scratch_shapes=[pltpu.VMEM((1, D), F32)],
        compiler_params=_params(("arbitrary",), 6 * _nbytes((tr, D), F32)),
        name="gla_out_bwd",
    )(dya, gla_proj, oa, projA, gn4)


def _gla_bwd(projA, projB, ssave, doa, wg, bg, B, Lp):
    T = B * Lp
    NC = Lp // GLA_C
    C = GLA_C
    scale = GLA_DK ** -0.5
    WC = 2304

    def body(q_ref, k_ref, v_ref, lr_ref, ss_ref, do_ref, wg_ref, bg_ref, dc_ref, dwg_ref, dbg_ref, dst_ref):
        i = pl.program_id(0)
        n = NC - 1 - i

        @pl.when(i == 0)
        def _():
            dst_ref[...] = jnp.zeros_like(dst_ref)
            dwg_ref[...] = jnp.zeros_like(dwg_ref)
            dbg_ref[...] = jnp.zeros_like(dbg_ref)

        pos = n * C + lax.broadcasted_iota(jnp.int32, (C, 1), 0)
        valid = pos >= FRONT
        lower, upper = _tri_masks()
        is_last = lax.broadcasted_iota(jnp.int32, (C, 1), 0) == C - 1
        for b in range(B):
            lr = lr_ref[b]
            pre, glog = _gla_gate(lr, wg_ref[...], bg_ref[...], valid)
            bcum = _cumsum_rows(glog, lower)
            db_parts = []
            for h in range(GLA_H):
                ks = slice(h * GLA_DK, (h + 1) * GLA_DK)
                vs = slice(h * GLA_DV, (h + 1) * GLA_DV)
                bh = bcum[:, ks]
                blast = jnp.sum(jnp.where(is_last, bh, 0.0), axis=0, keepdims=True)
                eb, enb, ekl, ebl = jnp.exp(bh), jnp.exp(-bh), jnp.exp(blast - bh), jnp.exp(blast)
                qh = q_ref[b, :, ks] * scale
                kh = k_ref[b, :, ks]
                qe_f, ke_f, kl_f = qh * eb, kh * enb, kh * ekl
                qe, ke, kl = qe_f.astype(BF16), ke_f.astype(BF16), kl_f.astype(BF16)
                vh = v_ref[b, :, vs].astype(BF16)
                doh = do_ref[b, :, vs]
                st = ss_ref[b, 0, h]
                dst = dst_ref[b, h]
                st_b, dst_b = st.astype(BF16), dst.astype(BF16)
                da = jnp.where(lower, _nt(doh, vh), 0.0).astype(BF16)
                da_t = jnp.where(upper, _nt(vh, doh), 0.0).astype(BF16)
                a_t = jnp.where(upper, _nt(ke, qe), 0.0).astype(BF16)
                dqe = _nn(da, ke) + _nn(doh, st_b)
                dke = _nn(da_t, qe)
                dvh = _nn(a_t, doh) + _nt(kl, dst_b)
                dkl = _nn(vh, dst_b)
                dst_ref[b, h] = dst * ebl + _tn(doh, qe)
                deb = jnp.sum(st * dst, axis=0, keepdims=True)
                db = dqe * qe_f - dke * ke_f - dkl * kl_f
                db_last = jnp.sum(dkl * kl_f, axis=0, keepdims=True) + deb * ebl
                db_parts.append(db + jnp.where(is_last, db_last, 0.0))
                dc_ref[b, :, vs] = dvh.astype(BF16)
                dc_ref[b, :, 1024 + h * GLA_DK:1024 + (h + 1) * GLA_DK] = (dqe * eb * scale).astype(BF16)
                dc_ref[b, :, 1536 + h * GLA_DK:1536 + (h + 1) * GLA_DK] = (dke * enb + dkl * ekl).astype(BF16)
            dglog = _cumsum_rows(jnp.concatenate(db_parts, axis=1), upper)
            dpre = jnp.where(valid, dglog * (1.0 / GLA_NORMALIZER) / (1.0 + jnp.exp(pre)), 0.0)
            dpre_b = dpre.astype(BF16)
            dc_ref[b, :, 2048:2176] = _nt(dpre_b, wg_ref[...]).astype(BF16)
            dc_ref[b, :, 2176:2304] = jnp.zeros((C, 128), BF16)
            dwg_ref[...] += _tn(lr.astype(BF16), dpre_b)
            dbg_ref[...] += jnp.sum(dpre, axis=0, keepdims=True)

    blocks = B * (_nbytes((C, 512), F32) * 2 + _nbytes((C, 1024), F32) + _nbytes((C, 1024), BF16)
                  + _nbytes((GLA_H, GLA_DV, GLA_DK), F32) + _nbytes((C, WC), BF16)) + 3 * _nbytes((128, 512), F32)
    state = _nbytes((B, GLA_H, GLA_DV, GLA_DK), F32)
    pa = projA.reshape(B, Lp, projA.shape[1])
    rev = lambda i: NC - 1 - i
    dc, dwg, dbg = pl.pallas_call(
        body,
        out_shape=(jax.ShapeDtypeStruct((B, Lp, WC), BF16), jax.ShapeDtypeStruct((128, GLA_KW), F32),
                   jax.ShapeDtypeStruct((1, GLA_KW), F32)),
        grid=(NC,),
        in_specs=[
            pl.BlockSpec((B, C, 512), lambda i: (0, rev(i), 10)),
            pl.BlockSpec((B, C, 512), lambda i: (0, rev(i), 11)),
            pl.BlockSpec((B, C, 1024), lambda i: (0, rev(i), 0)),
            pl.BlockSpec((B, C, 128), lambda i: (0, rev(i), 3)),
            pl.BlockSpec((B, 1, GLA_H, GLA_DV, GLA_DK), lambda i: (0, rev(i), 0, 0, 0)),
            pl.BlockSpec((B, C, 1024), lambda i: (0, rev(i), 0)),
            pl.BlockSpec((128, 512), lambda i: (0, 0)),
            pl.BlockSpec((1, 512), lambda i: (0, 0)),
        ],
        out_specs=(pl.BlockSpec((B, C, WC), lambda i: (0, rev(i), 0)),
                   pl.BlockSpec((128, GLA_KW), lambda i: (0, 0)),
                   pl.BlockSpec((1, GLA_KW), lambda i: (0, 0))),
        scratch_shapes=[pltpu.VMEM((B, GLA_H, GLA_DV, GLA_DK), F32)],
        compiler_params=_params(("arbitrary",), blocks, state),
        name="gla_bwd",
    )(pa, pa, pa, projB.reshape(B, Lp, projB.shape[1]), ssave, doa.reshape(B, Lp, GLA_VW), wg, bg)
    return dc.reshape(T, WC), dwg, dbg


def _attn_bwd_pre(dyb, mla_proj, projA, ob, B, Lp):
    T = B * Lp
    NQ = Lp // QB

    def body(dyb_ref, w_ref, z_ref, o_ref, do_ref, dz_ref, dcol_ref):
        j = pl.program_id(1)
        dy_all = _nt(dyb_ref[...], w_ref[...])
        for h in range(MLA_H):
            hs = slice(h * MLA_DV, (h + 1) * MLA_DV)
            dy = dy_all[:, hs]
            z = z_ref[:, hs]
            o = o_ref[:, hs]
            s = _sigmoid(z)
            do = dy * (z * s)
            do_ref[:, hs] = do.astype(BF16)
            dz_ref[:, hs] = (dy * o * (s * (1.0 + z * (1.0 - s)))).astype(BF16)
            dl = jnp.broadcast_to(jnp.sum(do * o, axis=-1, keepdims=True), (QB, LANES))
            dcol_ref[0, h, pl.ds(j, 1), :] = jnp.transpose(dl)[0:1, :]

    rows = lambda c: pl.BlockSpec((QB, D), lambda b, j: (b * NQ + j, c))
    return pl.pallas_call(
        body,
        out_shape=(jax.ShapeDtypeStruct((T, D), BF16), jax.ShapeDtypeStruct((T, D), BF16),
                   jax.ShapeDtypeStruct((B, MLA_H, NQ, QB), F32)),
        grid=(B, NQ),
        in_specs=[rows(0), pl.BlockSpec((D, D), lambda b, j: (0, 0)), rows(2), rows(0)],
        out_specs=(rows(0), rows(0), pl.BlockSpec((1, MLA_H, NQ, QB), lambda b, j: (b, 0, 0, 0))),
        compiler_params=_params(("parallel", "arbitrary"), 6 * _nbytes((QB, D), F32)),
        name="attn_bwd_pre",
    )(dyb, mla_proj, projA, ob)


ATTN_BWD_HEADS = 4


def _attn_bwd(q_att, k_att, v_att, do, lse_c, delta_c, B, Lp):
    T = B * Lp
    NQ = Lp // QB
    G = ATTN_BWD_HEADS
    NG = MLA_H // G
    HW = 2 * LANES
    scale = 1.0 / math.sqrt(MLA_QK)

    def body(q_ref, k_ref, v_ref, do_ref, lse_ref, dl_ref, dq_ref, dk_ref, dv_ref):
        kj = pl.program_id(2)

        @pl.when(kj == 0)
        def _():
            dq_ref[...] = jnp.zeros_like(dq_ref)

        dk_ref[...] = jnp.zeros_like(dk_ref)
        dv_ref[...] = jnp.zeros_like(dv_ref)
        col = kj * QB + lax.broadcasted_iota(jnp.int32, (QB, QB), 0)
        rowi = lax.broadcasted_iota(jnp.int32, (QB, QB), 1)

        def step(qi, carry):
            off = pl.multiple_of(qi * QB, QB)
            ok = _attn_mask(qi * QB + rowi, col)
            for h in range(G):
                ws = slice(h * HW, (h + 1) * HW)
                hs = slice(h * MLA_DV, (h + 1) * MLA_DV)
                qb = q_ref[pl.ds(off, QB), ws]
                dob = do_ref[pl.ds(off, QB), hs]
                kb = k_ref[:, ws]
                lse = lse_ref[0, h, pl.ds(qi, 1), :]
                delta = dl_ref[0, h, pl.ds(qi, 1), :]
                s_t = _nt(kb, qb) * scale
                p_t = jnp.where(ok, jnp.exp(s_t - lse), 0.0)
                dv_ref[:, hs] += _nn(p_t.astype(BF16), dob)
                ds_t = (p_t * (_nt(v_ref[:, hs], dob) - delta) * scale).astype(BF16)
                dk_ref[:, ws] += _nn(ds_t, qb)
                dq_ref[pl.ds(off, QB), ws] += _tn(ds_t, kb)
            return carry

        lax.fori_loop(kj, NQ, step, 0)

    blocks = (_nbytes((Lp, G * HW), BF16) + _nbytes((Lp, G * MLA_DV), BF16) + _nbytes((QB, G * 384), BF16)
              + 2 * _nbytes((G, NQ, QB), F32) + _nbytes((QB, G * 384), F32) + _nbytes((Lp, G * HW), F32))
    return pl.pallas_call(
        body,
        out_shape=(jax.ShapeDtypeStruct((T, MLA_H * HW), F32), jax.ShapeDtypeStruct((T, MLA_H * HW), F32),
                   jax.ShapeDtypeStruct((T, MLA_H * MLA_DV), F32)),
        grid=(B, NG, NQ),
        in_specs=[
            pl.BlockSpec((Lp, G * HW), lambda b, g, j: (b, g)),
            pl.BlockSpec((QB, G * HW), lambda b, g, j: (b * NQ + j, g)),
            pl.BlockSpec((QB, G * MLA_DV), lambda b, g, j: (b * NQ + j, g)),
            pl.BlockSpec((Lp, G * MLA_DV), lambda b, g, j: (b, g)),
            pl.BlockSpec((1, G, NQ, QB), lambda b, g, j: (b, g, 0, 0)),
            pl.BlockSpec((1, G, NQ, QB), lambda b, g, j: (b, g, 0, 0)),
        ],
        out_specs=(pl.BlockSpec((Lp, G * HW), lambda b, g, j: (b, g)),
                   pl.BlockSpec((QB, G * HW), lambda b, g, j: (b * NQ + j, g)),
                   pl.BlockSpec((QB, G * MLA_DV), lambda b, g, j: (b * NQ + j, g))),
        compiler_params=_params(("parallel", "parallel", "arbitrary"), blocks),
        name="attn_bwd",
    )(q_att, k_att, v_att, do, lse_c, delta_c)


def _mla_bwd_post(dq, dk, dv, projB, cos_t, sin_t, gq, gkv, wuq2, wukv, B, Lp, tr):
    T = B * Lp
    nt = Lp // tr
    HW = 2 * LANES

    def body(dq_ref, dk_ref, dv_ref, pb_ref, cos_ref, sin_ref, gq_ref, gkv_ref, wuq_ref, wukv_ref,
             dqf_ref, dkvf_ref, de_ref, dgq_ref, dgkv_ref):
        first = (pl.program_id(0) == 0) & (pl.program_id(1) == 0)

        @pl.when(first)
        def _():
            dgq_ref[...] = jnp.zeros_like(dgq_ref)
            dgkv_ref[...] = jnp.zeros_like(dgkv_ref)

        cs = cos_ref[...]
        sn = sin_ref[...]
        rope_t = lambda t: t * cs + _swap_halves(t * sn)
        dkr = jnp.zeros((tr, LANES), F32)
        for h in range(MLA_H):
            dqf_ref[:, h * HW:h * HW + LANES] = dq_ref[:, h * HW:h * HW + LANES].astype(BF16)
            dqf_ref[:, h * HW + LANES:(h + 1) * HW] = rope_t(dq_ref[:, h * HW + LANES:(h + 1) * HW]).astype(BF16)
            dkvf_ref[:, h * HW:h * HW + LANES] = dk_ref[:, h * HW:h * HW + LANES].astype(BF16)
            dkvf_ref[:, h * HW + LANES:(h + 1) * HW] = dv_ref[:, h * MLA_DV:(h + 1) * MLA_DV].astype(BF16)
            dkr = dkr + dk_ref[:, h * HW + LANES:(h + 1) * HW]

        def norm_bwd(x, dn, g):
            r = lax.rsqrt(jnp.mean(x * x, axis=-1, keepdims=True) + EPS)
            xn = x * r
            t = dn * g
            return r * (t - xn * jnp.mean(t * xn, axis=-1, keepdims=True)), jnp.sum(dn * xn, axis=0, keepdims=True)

        dcq, dgq = norm_bwd(pb_ref[:, 0:Q_RANK], _nt(dqf_ref[...], wuq_ref[...]), gq_ref[...])
        dckv, dgkv = norm_bwd(pb_ref[:, Q_RANK:Q_RANK + KV_RANK], _nt(dkvf_ref[...], wukv_ref[...]), gkv_ref[...])
        dgq_ref[...] += dgq
        dgkv_ref[...] += dgkv
        de_ref[:, 0:Q_RANK] = dcq.astype(BF16)
        de_ref[:, Q_RANK:Q_RANK + KV_RANK] = dckv.astype(BF16)
        de_ref[:, 384:512] = rope_t(dkr).astype(BF16)

    rows = lambda w: pl.BlockSpec((tr, w), lambda b, j: (b * nt + j, 0))
    const = lambda s: pl.BlockSpec(s, lambda b, j: (0, 0))
    blocks = (2 * _nbytes((tr, 2048), F32) + _nbytes((tr, 1024), F32) + _nbytes((tr, 640), F32)
              + 2 * _nbytes((tr, 2048), BF16) + _nbytes((2048, 384), BF16) + 2 * _nbytes((tr, 2048), F32))
    return pl.pallas_call(
        body,
        out_shape=(jax.ShapeDtypeStruct((T, 2048), BF16), jax.ShapeDtypeStruct((T, 2048), BF16),
                   jax.ShapeDtypeStruct((T, 512), BF16), jax.ShapeDtypeStruct((1, Q_RANK), F32),
                   jax.ShapeDtypeStruct((1, KV_RANK), F32)),
        grid=(B, nt),
        in_specs=[rows(2048), rows(2048), rows(1024), rows(640),
                  pl.BlockSpec((tr, 128), lambda b, j: (j, 0)), pl.BlockSpec((tr, 128), lambda b, j: (j, 0)),
                  const((1, Q_RANK)), const((1, KV_RANK)), const((Q_RANK, 2048)), const((KV_RANK, 2048))],
        out_specs=(rows(2048), rows(2048), rows(512), const((1, Q_RANK)), const((1, KV_RANK))),
        compiler_params=_params(("arbitrary", "arbitrary"), blocks),
        name="mla_bwd_post",
    )(dq, dk, dv, projB, cos_t, sin_t, gq, gkv, wuq2, wukv)


def _in_proj_bwd(x, meta, dh1, dA, dBz, dC, dDz, dE, wA, wB, g, B, Lp):
    NQ = Lp // QB
    seq = x.shape[1]

    def body(x_ref, meta_ref, dh_ref, da_ref, db_ref, dc_ref, dd_ref, de_ref, wa_ref, wb_ref, g_ref,
             gx_ref, dmeta_ref, dg_ref):
        b = pl.program_id(0)
        j = pl.program_id(1)

        @pl.when((b == 0) & (j == 0))
        def _():
            dg_ref[...] = jnp.zeros_like(dg_ref)

        du = _nt(da_ref[...], wa_ref[:, 3072:5120])
        du = du + _nt(db_ref[...], wa_ref[:, 1024:2048])
        du = du + _nt(dd_ref[...], wa_ref[:, 2048:3072])
        du = du + _nt(dc_ref[:, 0:1024], wa_ref[:, 0:1024])
        du = du + _nt(dc_ref[:, 1024:2048], wa_ref[:, 5120:6144])
        du = du + _nt(dc_ref[:, 2048:2176], wb_ref[:, 384:512])
        du = du + _nt(de_ref[:, 0:384], wb_ref[:, 0:384])
        du = du + _nt(de_ref[:, 384:512], wb_ref[:, 512:640])

        x = _h_tile(j, x_ref, meta_ref)
        r = lax.rsqrt(jnp.mean(x * x, axis=-1, keepdims=True) + EPS)
        xn = x * r
        t = du * g_ref[...]
        dh0 = dh_ref[...] + r * (t - xn * jnp.mean(t * xn, axis=-1, keepdims=True))
        dg_ref[...] += jnp.sum(du * xn, axis=0, keepdims=True)
        gx_ref[0] = dh0

        @pl.when((j == 0) & (b == 0))
        def _():
            dmeta_ref[...] = dh0[FRONT:HEAD_ROWS, :]

        @pl.when((j == 0) & (b > 0))
        def _():
            dmeta_ref[...] += dh0[FRONT:HEAD_ROWS, :]

    rows = lambda w: pl.BlockSpec((QB, w), lambda b, j: (b * NQ + j, 0))
    const = lambda s: pl.BlockSpec(s, lambda b, j: (0, 0))
    widths = [a.shape[1] for a in (dA, dBz, dC, dDz, dE)]
    blocks = (sum(_nbytes((QB, w), BF16) for w in widths) + _nbytes(wA.shape, BF16) + _nbytes(wB.shape, BF16)
              + 4 * _nbytes((QB, D), F32))
    return pl.pallas_call(
        body,
        out_shape=(jax.ShapeDtypeStruct((B, seq, D), F32), jax.ShapeDtypeStruct((N_META, D), F32),
                   jax.ShapeDtypeStruct((1, D), F32)),
        grid=(B, NQ),
        in_specs=[_x_spec(), const((N_META, D)), rows(D)] + [rows(w) for w in widths]
        + [const(wA.shape), const(wB.shape), const((1, D))],
        out_specs=(_x_spec(), const((N_META, D)), const((1, D))),
        compiler_params=_params(("arbitrary", "arbitrary"), blocks),
        name="in_proj_bwd",
    )(x, meta, dh1, dA, dBz, dC, dDz, dE, wA, wB, g)


_VMEM_WHOLE = pl.BlockSpec(memory_space=pltpu.VMEM)


def _params_whole(arrays):
    total = sum(_nbytes(a.shape, a.dtype) for a in arrays)
    return pltpu.CompilerParams(vmem_limit_bytes=int(min(total + 12 * 1024 * 1024, VMEM_CAP_V7X)))


def _wire_dtype(shape):
    return BF16 if shape[-2] * shape[-1] >= WIRE_BF16_MIN_ELEMS else F32


def _pair_add_big(gp, recv, c):
    _, half, cols = recv.shape
    th = _div_tile(half, 64, 16)
    out_dtype = _wire_dtype(recv.shape)

    def body(c_ref, a_ref, b_ref, o_ref):
        o_ref[...] = (a_ref[:, 0] + b_ref[...]).astype(out_dtype)

    return pl.pallas_call(
        body,
        out_shape=jax.ShapeDtypeStruct(recv.shape, out_dtype),
        grid_spec=pltpu.PrefetchScalarGridSpec(
            num_scalar_prefetch=1,
            grid=(half // th,),
            in_specs=[pl.BlockSpec((4, 1, th, cols), lambda i, c_ref: (0, c_ref[0], i, 0)),
                      pl.BlockSpec((4, th, cols), lambda i, c_ref: (0, i, 0))],
            out_specs=pl.BlockSpec((4, th, cols), lambda i, c_ref: (0, i, 0)),
        ),
        compiler_params=_params(("parallel",), 3 * _nbytes((4, th, cols), F32)),
        name="grad_pair_add_big",
    )(c, gp.reshape(4, 2, half, cols), recv)


def _pair_add_small(gps, recvs):
    n = len(gps)

    def body(*refs):
        c = lax.axis_index("c")
        for t in range(n):
            g_ref, r_ref, o_ref = refs[t], refs[n + t], refs[2 * n + t]
            half = r_ref.shape[1]
            s = g_ref[:, pl.ds(pl.multiple_of(c * half, 8), half), :] + r_ref[...]
            o_ref[...] = s.astype(o_ref.dtype)

    return pl.pallas_call(
        body,
        out_shape=[jax.ShapeDtypeStruct(r.shape, _wire_dtype(r.shape)) for r in recvs],
        in_specs=[_VMEM_WHOLE] * (2 * n),
        out_specs=[_VMEM_WHOLE] * n,
        compiler_params=_params_whole(list(gps) + 2 * list(recvs)),
        name="grad_pair_add_small",
    )(*gps, *recvs)


def _chip_order_sum(landed_ref, own_ref, me):
    p = [jnp.where(me == k, own_ref[k], landed_ref[k]).astype(F32) for k in range(4)]
    return ((p[0] + p[1]) + p[2]) + p[3]


def _sum_chips_big(landed, own, pos):
    _, half, cols = landed.shape
    th = _div_tile(half, 64, 16)

    def body(pos_ref, l_ref, s_ref, o_ref):
        o_ref[0] = _chip_order_sum(l_ref, s_ref, pos_ref[1])

    spec = pl.BlockSpec((4, th, cols), lambda i, pos_ref: (0, i, 0))
    return pl.pallas_call(
        body,
        out_shape=jax.ShapeDtypeStruct((2, half, cols), F32),
        grid_spec=pltpu.PrefetchScalarGridSpec(
            num_scalar_prefetch=1,
            grid=(half // th,),
            in_specs=[spec, spec],
            out_specs=pl.BlockSpec((1, th, cols), lambda i, pos_ref: (pos_ref[0], i, 0)),
        ),
        compiler_params=_params(("parallel",), 3 * _nbytes((4, th, cols), F32)),
        name="grad_sum_chips_big",
    )(pos, landed, own)


def _sum_chips_small(landed, own):
    n = len(landed)

    def body(*refs):
        x, y, c = _mesh_pos()
        for t in range(n):
            refs[2 * n + t][c] = _chip_order_sum(refs[t], refs[n + t], 2 * x + y)

    return pl.pallas_call(
        body,
        out_shape=[jax.ShapeDtypeStruct((2,) + p.shape[1:], F32) for p in landed],
        in_specs=[_VMEM_WHOLE] * (2 * n),
        out_specs=[_VMEM_WHOLE] * n,
        compiler_params=_params_whole(list(landed) * 3),
        name="grad_sum_chips_small",
    )(*landed, *own)


def _adamw_update(w_ref, g_ref, m_ref, v_ref, d_ref, mo_ref, vo_ref):
    c1 = 1.0 - ADAM_B1 ** ADAM_STEP
    c2 = 1.0 - ADAM_B2 ** ADAM_STEP
    gv = g_ref[...]
    mn = ADAM_B1 * m_ref[...] + (1.0 - ADAM_B1) * gv
    vn = ADAM_B2 * v_ref[...] + (1.0 - ADAM_B2) * (gv * gv)
    mo_ref[...] = mn
    vo_ref[...] = vn
    d_ref[...] = -ADAM_LR * ((mn / c1) / (jnp.sqrt(vn / c2) + ADAM_EPS) + ADAM_WD * w_ref[...])


def _adamw_big(w, g, m, v):
    rows, cols = w.shape
    tr = _div_tile(rows, 128, 8)
    spec = pl.BlockSpec((tr, cols), lambda i: (i, 0))
    shp = jax.ShapeDtypeStruct((rows, cols), F32)
    return pl.pallas_call(
        functools.partial(_adamw_update),
        out_shape=(shp, shp, shp),
        grid=(rows // tr,),
        in_specs=[spec] * 4,
        out_specs=(spec, spec, spec),
        compiler_params=_params(("parallel",), 7 * _nbytes((tr, cols), F32)),
        name="adamw_big",
    )(w, g, m, v)


def _adamw_small(ws, gs, ms, vs):
    n = len(ws)

    def body(*refs):
        for t in range(n):
            _adamw_update(refs[t], refs[n + t], refs[2 * n + t], refs[3 * n + t],
                          refs[4 * n + t], refs[5 * n + t], refs[6 * n + t])

    shapes = [jax.ShapeDtypeStruct(w.shape, F32) for w in ws]
    return pl.pallas_call(
        body,
        out_shape=shapes * 3,
        in_specs=[_VMEM_WHOLE] * (4 * n),
        out_specs=[_VMEM_WHOLE] * (3 * n),
        compiler_params=_params_whole(list(ws) * 7),
        name="adamw_small",
    )(*ws, *gs, *ms, *vs)


def _mesh_pos():
    return lax.axis_index("x"), lax.axis_index("y"), lax.axis_index("c")


def _other_chips(x, y):
    return [(1 - x, y), (x, 1 - y), (1 - x, 1 - y)]


_ANY = pl.BlockSpec(memory_space=pl.ANY)


PAIR_SPLIT_MIN_ROWS = 64


def _weight_gather(shards):
    n = len(shards)
    split = [s.shape[0] >= PAIR_SPLIT_MIN_ROWS for s in shards]

    def body(*refs):
        w_refs, o_refs = refs[:n], refs[n:2 * n]
        send_sems, recv_sems = refs[2 * n:]
        x, y, c = _mesh_pos()
        me = 2 * x + y
        chips = _other_chips(x, y)

        def rows_of(t, core):
            rows = shards[t].shape[0]
            if not split[t]:
                return pl.ds(0, rows)
            return pl.ds(pl.multiple_of(core * (rows // 2), 16), rows // 2)

        def landed(t, k, slot, rows, to):
            ref = o_refs[t].at[slot, rows]
            return pltpu.make_async_remote_copy(src_ref=ref, dst_ref=ref, send_sem=send_sems.at[6 * t + k],
                                                recv_sem=recv_sems.at[6 * t + k], device_id=to, device_id_type=MESH)

        sends = []
        for t in range(n):
            mine = rows_of(t, c)
            for k, (px, py) in enumerate(chips):
                cp = pltpu.make_async_remote_copy(src_ref=w_refs[t].at[mine], dst_ref=o_refs[t].at[me, mine],
                                                  send_sem=send_sems.at[6 * t + k], recv_sem=recv_sems.at[6 * t + k],
                                                  device_id=(px, py, c), device_id_type=MESH)
                cp.start()
                sends.append(cp)
        for t in range(n):
            mine = rows_of(t, c)
            for k, (px, py) in enumerate(chips):
                landed(t, k, 2 * px + py, mine, (x, y, c)).wait_recv()
                if split[t]:
                    cp = landed(t, 3 + k, 2 * px + py, mine, (x, y, 1 - c))
                    cp.start()
                    sends.append(cp)
        for t in range(n):
            if split[t]:
                for k, (px, py) in enumerate(chips):
                    landed(t, 3 + k, 2 * px + py, rows_of(t, 1 - c), (x, y, c)).wait_recv()
        for cp in sends:
            cp.wait_send()

    return pl.pallas_call(
        body,
        out_shape=[jax.ShapeDtypeStruct((4,) + s.shape, s.dtype) for s in shards],
        in_specs=[_ANY] * n,
        out_specs=[_ANY] * n,
        scratch_shapes=[pltpu.SemaphoreType.DMA((6 * n,)), pltpu.SemaphoreType.DMA((6 * n,))],
        name="weight_gather",
    )(*shards)


def _pair_swap(gps):
    n = len(gps)

    def body(*refs):
        g_refs, o_refs = refs[:n], refs[n:2 * n]
        send_sems, recv_sems = refs[2 * n:]
        x, y, c = _mesh_pos()
        copies = []
        for t in range(n):
            half = gps[t].shape[1] // 2
            theirs = pl.ds(pl.multiple_of((1 - c) * half, 8), half)
            cp = pltpu.make_async_remote_copy(src_ref=g_refs[t].at[:, theirs], dst_ref=o_refs[t],
                                              send_sem=send_sems.at[t], recv_sem=recv_sems.at[t],
                                              device_id=(x, y, 1 - c), device_id_type=MESH)
            cp.start()
            copies.append(cp)
        for cp in copies:
            cp.wait_send()
            cp.wait_recv()

    return pl.pallas_call(
        body,
        out_shape=[jax.ShapeDtypeStruct((4, g.shape[1] // 2, g.shape[2]), g.dtype) for g in gps],
        in_specs=[_ANY] * n,
        out_specs=[_ANY] * n,
        scratch_shapes=[pltpu.SemaphoreType.DMA((n,)), pltpu.SemaphoreType.DMA((n,))],
        name="grad_pair_swap",
    )(*gps)


def _chip_scatter(parts):
    n = len(parts)

    def body(*refs):
        s_refs, o_refs = refs[:n], refs[n:2 * n]
        send_sems, recv_sems = refs[2 * n:]
        x, y, c = _mesh_pos()
        me = 2 * x + y
        chips = _other_chips(x, y)
        sends = []
        for t in range(n):
            for k, (px, py) in enumerate(chips):
                cp = pltpu.make_async_remote_copy(src_ref=s_refs[t].at[2 * px + py], dst_ref=o_refs[t].at[me],
                                                  send_sem=send_sems.at[3 * t + k], recv_sem=recv_sems.at[3 * t + k],
                                                  device_id=(px, py, c), device_id_type=MESH)
                cp.start()
                sends.append(cp)
        for t in range(n):
            for k, (px, py) in enumerate(chips):
                pltpu.make_async_remote_copy(src_ref=s_refs[t].at[me], dst_ref=o_refs[t].at[2 * px + py],
                                             send_sem=send_sems.at[3 * t + k], recv_sem=recv_sems.at[3 * t + k],
                                             device_id=(x, y, c), device_id_type=MESH).wait_recv()
        for cp in sends:
            cp.wait_send()

    return pl.pallas_call(
        body,
        out_shape=[jax.ShapeDtypeStruct(p.shape, p.dtype) for p in parts],
        in_specs=[_ANY] * n,
        out_specs=[_ANY] * n,
        scratch_shapes=[pltpu.SemaphoreType.DMA((3 * n,)), pltpu.SemaphoreType.DMA((3 * n,))],
        name="grad_chip_scatter",
    )(*parts)


def _pair_join(fs):
    n = len(fs)

    def body(*refs):
        f_refs, o_refs = refs[:n], refs[n:2 * n]
        send_sems, recv_sems = refs[2 * n:]
        x, y, c = _mesh_pos()
        sends = []
        for t in range(n):
            cp = pltpu.make_async_remote_copy(src_ref=f_refs[t].at[c], dst_ref=o_refs[t].at[c], send_sem=send_sems.at[t],
                                              recv_sem=recv_sems.at[t], device_id=(x, y, 1 - c), device_id_type=MESH)
            cp.start()
            sends.append(cp)
        for t in range(n):
            pltpu.make_async_remote_copy(src_ref=f_refs[t].at[c], dst_ref=o_refs[t].at[1 - c], send_sem=send_sems.at[t],
                                         recv_sem=recv_sems.at[t], device_id=(x, y, c), device_id_type=MESH).wait_recv()
        for cp in sends:
            cp.wait_send()

    return pl.pallas_call(
        body,
        out_shape=[jax.ShapeDtypeStruct(f.shape, f.dtype) for f in fs],
        in_specs=[_ANY] * n,
        out_specs=[_ANY] * n,
        input_output_aliases={t: t for t in range(n)},
        scratch_shapes=[pltpu.SemaphoreType.DMA((n,)), pltpu.SemaphoreType.DMA((n,))],
        name="grad_pair_join",
    )(*fs)


def _rope_tables(Lp):
    inv = 1.0 / (ROPE_BASE ** (jnp.arange(0, ROPE, 2, dtype=F32) / ROPE))
    ang = (jnp.arange(Lp, dtype=F32) - FRONT)[:, None] * inv[None, :]
    cs, sn = jnp.cos(ang), jnp.sin(ang)
    return jnp.tile(cs, (1, 4)), jnp.concatenate([-sn, sn, -sn, sn], axis=1)


def _local_step(x, loss_target, meta, norm_g, w_in, gate_w, gate_b, gla_norm_g, gla_proj, q_norm_g, w_uq,
                kv_norm_g, w_ukv, mla_proj, w_out, final_norm_g):
    B, seq, _ = x.shape
    Lp = HEAD_ROWS + seq
    T = B * Lp
    tr = _div_tile(Lp, 544, 16)
    tq = _div_tile(T, 1024, QB)

    cuts = np.cumsum((0,) + SPLITS)
    col = lambda i: w_in[:, cuts[i]:cuts[i + 1]]
    w_q, w_k, w_v, w_lr, w_z, w_cq, w_ckv, w_kr, w_mz, w_gg, w_gm = [col(i) for i in range(11)]
    pad_cols = lambda w, n: jnp.pad(w, ((0, 0), (0, n - w.shape[1])))
    wA = jnp.concatenate([w_v, w_z, w_mz, w_gg, w_gm, w_q, w_k], axis=1)
    wB = jnp.concatenate([w_cq, w_ckv, pad_cols(w_lr, 128), pad_cols(w_kr, 128)], axis=1)
    wg = jnp.pad(gate_w, ((0, 128 - GLA_RANK), (0, 0)))
    wuq2 = jnp.pad(w_uq.reshape(Q_RANK, MLA_H, MLA_QK), ((0, 0), (0, 0), (0, 256 - MLA_QK))).reshape(Q_RANK, 2048)
    gn4 = jnp.tile(gla_norm_g, (1, GLA_H))
    cos_t, sin_t = _rope_tables(Lp)

    u = _rms_in(x, meta, norm_g, B, Lp)
    projA = _mm(u, wA, name="in_proj_a", tm=tq, tn=1024, tk=D)
    projB = _mm(u, wB, name="in_proj_b", tm=tq, tn=640, tk=D)
    oa, ya_in, ssave = _gla_fwd(projA, projB, wg, gate_b, gn4, B, Lp)
    ya = _mm(ya_in, gla_proj, name="gla_proj", tm=tq, tn=D, tk=D)
    q_att, k_att, v_att, cqn, ckvn = _mla_prep(projB, cos_t, sin_t, q_norm_g, kv_norm_g, wuq2, w_ukv, B, Lp, tr)
    ob, yb_in, lse_c = _attn_fwd(q_att, k_att, v_att, projA, B, Lp)
    yb = _mm(yb_in, mla_proj, name="mla_proj", tm=tq, tn=D, tk=D)
    dh1, dh1_b, merged, loss, d_gf = _out_proj_loss(x, meta, projA, ya, yb, w_out, final_norm_g.reshape(1, D),
                                                     loss_target, B, Lp)

    g_w_out = _mm(merged, dh1_b, name="dw_out", trans_a=True, tm=D, tn=D, tk=tq)
    dya, dyb, dA = _merge_bwd(dh1_b, w_out, projA, ya, yb, tr)
    g_gla_proj = _mm(ya_in, dya, name="dw_gla_proj", trans_a=True, tm=D, tn=D, tk=tq)
    g_mla_proj = _mm(yb_in, dyb, name="dw_mla_proj", trans_a=True, tm=D, tn=D, tk=tq)
    doa, dBz, d_gn = _gla_out_bwd(dya, gla_proj, oa, projA, gn4, tr)
    dC, g_wg, d_bg = _gla_bwd(projA, projB, ssave, doa, wg, gate_b, B, Lp)
    do, dDz, delta_c = _attn_bwd_pre(dyb, mla_proj, projA, ob, B, Lp)
    dq, dk, dv = _attn_bwd(q_att, k_att, v_att, do, lse_c, delta_c, B, Lp)
    dqf, dkvf, dE, d_gq, d_gkv = _mla_bwd_post(dq, dk, dv, projB, cos_t, sin_t, q_norm_g, kv_norm_g,
                                                wuq2, w_ukv, B, Lp, tr)
    g_wuq2 = _mm(cqn, dqf, name="dw_uq", trans_a=True, tm=Q_RANK, tn=2048, tk=tq)
    g_wukv = _mm(ckvn, dkvf, name="dw_ukv", trans_a=True, tm=KV_RANK, tn=2048, tk=tq)
    dparts = [dA, dBz, dC, dDz, dE]
    g_in = [_mm(u, dp, name="dw_in_%d" % i, trans_a=True, tm=D, tn=_div_tile(dp.shape[1], 1024, 256), tk=tq)
            for i, dp in enumerate(dparts)]
    grad_x, d_meta, d_ng = _in_proj_bwd(x, meta, dh1, dA, dBz, dC, dDz, dE, wA, wB, norm_g, B, Lp)

    gA, gBz, gC, gDz, gE = g_in
    g_w_in = jnp.concatenate([
        gC[:, 1024:1536], gC[:, 1536:2048], gC[:, 0:1024], gC[:, 2048:2048 + GLA_RANK], gBz,
        gE[:, 0:Q_RANK], gE[:, Q_RANK:Q_RANK + KV_RANK], gE[:, 384:384 + ROPE], gDz, gA[:, 0:D], gA[:, D:2 * D]], axis=1)
    g_wuq = g_wuq2.reshape(Q_RANK, MLA_H, 256)[:, :, :MLA_QK].reshape(Q_RANK, MLA_H * MLA_QK)
    grads = dict(w_in=g_w_in, gla_gate_w=g_wg[:GLA_RANK], gla_proj=g_gla_proj, mla_w_uq=g_wuq, mla_w_ukv=g_wukv,
                 mla_proj=g_mla_proj, w_out=g_w_out, meta_tokens=d_meta, norm_g=d_ng, gla_gate_b=d_bg,
                 gla_norm_g=d_gn, mla_q_norm_g=d_gq, mla_kv_norm_g=d_gkv, final_norm_g=d_gf)
    return loss[0, 0], grad_x, grads


_MATS = ("w_in", "gla_gate_w", "gla_proj", "mla_w_uq", "mla_w_ukv", "mla_proj", "w_out")
_ROW_SHARDED = ("gla_proj", "mla_proj", "w_out")
_ORDER = ("meta_tokens", "norm_g", "w_in", "gla_gate_w", "gla_gate_b", "gla_norm_g", "gla_proj", "mla_q_norm_g",
          "mla_w_uq", "mla_kv_norm_g", "mla_w_ukv", "mla_proj", "w_out", "final_norm_g")
WIRE_BF16_MIN_ELEMS = 128 * 128
SMALL_PACK_ROWS = 16


def _pack_small(d):
    rows = [jnp.pad(d[n].reshape(1, size), ((0, 0), (0, D - size))) for n, size in SMALL]
    return jnp.pad(jnp.concatenate(rows, axis=0), ((0, SMALL_PACK_ROWS - len(rows)), (0, 0)))


def _unpack_small(packed):
    return {n: packed[i, :size] for i, (n, size) in enumerate(SMALL)}


def kernel(x, meta_tokens, norm_g, w_in, gla_gate_w, gla_gate_b, gla_norm_g, gla_proj, mla_q_norm_g, mla_w_uq, mla_kv_norm_g, mla_w_ukv, mla_proj, w_out, final_norm_g, loss_target, m_meta_tokens, m_norm_g, m_w_in, m_gla_gate_w, m_gla_gate_b, m_gla_norm_g, m_gla_proj, m_mla_q_norm_g, m_mla_w_uq, m_mla_kv_norm_g, m_mla_w_ukv, m_mla_proj, m_w_out, m_final_norm_g, v_meta_tokens, v_norm_g, v_w_in, v_gla_gate_w, v_gla_gate_b, v_gla_norm_g, v_gla_proj, v_mla_q_norm_g, v_mla_w_uq, v_mla_kv_norm_g, v_mla_w_ukv, v_mla_proj, v_w_out, v_final_norm_g):
    w = dict(meta_tokens=meta_tokens, norm_g=norm_g, w_in=w_in[0], gla_gate_w=gla_gate_w[0], gla_gate_b=gla_gate_b,
             gla_norm_g=gla_norm_g, gla_proj=gla_proj[0], mla_q_norm_g=mla_q_norm_g, mla_w_uq=mla_w_uq[0],
             mla_kv_norm_g=mla_kv_norm_g, mla_w_ukv=mla_w_ukv[0], mla_proj=mla_proj[0], w_out=w_out[0],
             final_norm_g=final_norm_g)
    mom = dict(meta_tokens=m_meta_tokens, norm_g=m_norm_g, w_in=m_w_in[0], gla_gate_w=m_gla_gate_w[0],
               gla_gate_b=m_gla_gate_b, gla_norm_g=m_gla_norm_g, gla_proj=m_gla_proj[0], mla_q_norm_g=m_mla_q_norm_g,
               mla_w_uq=m_mla_w_uq[0], mla_kv_norm_g=m_mla_kv_norm_g, mla_w_ukv=m_mla_w_ukv[0], mla_proj=m_mla_proj[0],
               w_out=m_w_out[0], final_norm_g=m_final_norm_g)
    var = dict(meta_tokens=v_meta_tokens, norm_g=v_norm_g, w_in=v_w_in[0], gla_gate_w=v_gla_gate_w[0],
               gla_gate_b=v_gla_gate_b, gla_norm_g=v_gla_norm_g, gla_proj=v_gla_proj[0], mla_q_norm_g=v_mla_q_norm_g,
               mla_w_uq=v_mla_w_uq[0], mla_kv_norm_g=v_mla_kv_norm_g, mla_w_ukv=v_mla_w_ukv[0], mla_proj=v_mla_proj[0],
               w_out=v_w_out[0], final_norm_g=v_final_norm_g)
    out_shapes = {n: a.shape for n, a in zip(_ORDER, (meta_tokens, norm_g, w_in, gla_gate_w, gla_gate_b, gla_norm_g,
                                                     gla_proj, mla_q_norm_g, mla_w_uq, mla_kv_norm_g, mla_w_ukv,
                                                     mla_proj, w_out, final_norm_g))}

    me = (2 * lax.axis_index("x") + lax.axis_index("y")).astype(jnp.int32)
    shards = [w[n].astype(BF16) for n in _MATS] + [meta_tokens]
    gathered = [lax.dynamic_update_slice(gth, own[None], (me, 0, 0))
                for gth, own in zip(_weight_gather(shards), shards)]
    full = {}
    for name, gth in zip(_MATS, gathered):
        if name in _ROW_SHARDED:
            full[name] = gth.reshape(4 * gth.shape[1], gth.shape[2])
        else:
            full[name] = gth.transpose(1, 0, 2).reshape(gth.shape[1], 4 * gth.shape[2])
    meta_full = gathered[-1].transpose(1, 0, 2).reshape(N_META, D)

    loss_local, grad_x, g = _local_step(
        x, loss_target, meta_full, norm_g, full["w_in"], full["gla_gate_w"], gla_gate_b, gla_norm_g, full["gla_proj"],
        mla_q_norm_g, full["mla_w_uq"], mla_kv_norm_g, full["mla_w_ukv"], full["mla_proj"], full["w_out"], final_norm_g)
    loss = lax.psum(loss_local, ("x", "y", "c"))

    def by_owner(name, arr):
        if name in _ROW_SHARDED:
            return arr.reshape(4, arr.shape[0] // 4, arr.shape[1])
        return arr.reshape(arr.shape[0], 4, arr.shape[1] // 4).transpose(1, 0, 2)

    names = _MATS + ("meta_tokens",)
    gps = [by_owner(n, g[n]) for n in names] + [jnp.broadcast_to(_pack_small(g)[None], (4, SMALL_PACK_ROWS, D))]
    recvs = _pair_swap(gps)
    c_idx = lax.axis_index("c").astype(jnp.int32).reshape(1)
    s1 = [_pair_add_big(gps[0], recvs[0], c_idx)] + list(_pair_add_small(gps[1:], recvs[1:]))
    landed = _chip_scatter(s1)
    pos = jnp.stack([c_idx[0], me])
    halves = [_sum_chips_big(landed[0], s1[0], pos)] + list(_sum_chips_small(landed[1:], s1[1:]))
    g_red = [j.reshape(2 * j.shape[1], j.shape[2]) for j in _pair_join(halves)]

    tens = lambda d: [d[n].reshape(g_red[i].shape) for i, n in enumerate(names)] + [_pack_small(d)]
    w_t, m_t, v_t = tens(w), tens(mom), tens(var)
    big = _adamw_big(w_t[0], g_red[0], m_t[0], v_t[0])
    rest = _adamw_small(w_t[1:], g_red[1:], m_t[1:], v_t[1:])
    k = len(names)
    results = {"grad": g_red}
    for i, kind in enumerate(("delta", "new_m", "new_v")):
        results[kind] = [big[i]] + list(rest[i * k:(i + 1) * k])

    outs = []
    for kind in ("grad", "delta", "new_m", "new_v"):
        vals = dict(zip(names, results[kind][:-1]))
        vals.update(_unpack_small(results[kind][-1]))
        outs += [vals[n].reshape(out_shapes[n]) for n in _ORDER]
    return (loss, grad_x, *outs)
```

```python
import functools
import math

import jax
import jax.numpy as jnp
import numpy as np
from jax import lax
from jax.experimental import pallas as pl
from jax.experimental.pallas import tpu as pltpu

F32 = jnp.float32
BF16 = jnp.bfloat16

D = 1024
N_META = 16
QB = 256
FRONT = QB - N_META
HEAD_ROWS = FRONT + N_META
assert FRONT % 64 == 48
EPS = 1e-6

GLA_H, GLA_DK, GLA_DV, GLA_RANK, GLA_C = 4, 128, 256, 16, 64
GLA_NORMALIZER = 16.0
GLA_KW, GLA_VW = GLA_H * GLA_DK, GLA_H * GLA_DV
MLA_H, NOPE, ROPE, MLA_DV, Q_RANK, KV_RANK = 8, 128, 64, 128, 256, 128
MLA_QK = NOPE + ROPE
ROPE_BASE = 10000.0
SPLITS = (GLA_KW, GLA_KW, GLA_VW, GLA_RANK, GLA_VW, Q_RANK, KV_RANK, ROPE, MLA_H * MLA_DV, D, D)
IN_WIDTH = sum(SPLITS)

ADAM_LR, ADAM_B1, ADAM_B2, ADAM_EPS, ADAM_WD, ADAM_STEP = 0.001, 0.9, 0.999, 1e-08, 0.01, 10

LANES = 128
VMEM_CAP_V7X = 56 * 1024 * 1024
MESH = pl.DeviceIdType.MESH
NEG = -1e30

SMALL = (("norm_g", D), ("gla_gate_b", GLA_KW), ("gla_norm_g", GLA_DV), ("mla_q_norm_g", Q_RANK),
         ("mla_kv_norm_g", KV_RANK), ("final_norm_g", D))


def _div_tile(n, target, mult):
    best = None
    for d in range(mult, min(n, target) + 1, mult):
        if n % d == 0:
            best = d
    assert best is not None, (n, target, mult)
    return best


def _params(sem, block_bytes, scratch_bytes=0):
    est = 2 * block_bytes + scratch_bytes + 12 * 1024 * 1024
    return pltpu.CompilerParams(dimension_semantics=sem, vmem_limit_bytes=int(min(max(est, 24 * 1024 * 1024), VMEM_CAP_V7X)))


def _nbytes(shape, dtype):
    return int(np.prod(shape)) * jnp.dtype(dtype).itemsize


def _sigmoid(x):
    return 1.0 / (1.0 + jnp.exp(-x))


def _nt(a, b):
    return lax.dot_general(a, b, (((1,), (1,)), ((), ())), preferred_element_type=F32)


def _tn(a, b):
    return lax.dot_general(a, b, (((0,), (0,)), ((), ())), preferred_element_type=F32)


def _nn(a, b):
    return jnp.dot(a, b, preferred_element_type=F32)


def _split3(x):
    a = x.astype(BF16)
    r = x - a.astype(F32)
    b = r.astype(BF16)
    c = (r - b.astype(F32)).astype(BF16)
    return a, b, c


def _mm(a, b, *, name, trans_a=False, trans_b=False, out_dtype=F32, tm, tn, tk):
    assert not (trans_a and trans_b)
    if trans_a:
        K, M = a.shape
    else:
        M, K = a.shape
    N = b.shape[0] if trans_b else b.shape[1]
    assert (b.shape[1] if trans_b else b.shape[0]) == K
    assert M % tm == 0 and N % tn == 0 and K % tk == 0, (name, M, N, K, tm, tn, tk)
    nk = K // tk

    def body(a_ref, b_ref, o_ref, *scratch):
        av = a_ref[...].astype(BF16)
        bv = b_ref[...].astype(BF16)
        prod = _tn(av, bv) if trans_a else (_nt(av, bv) if trans_b else _nn(av, bv))
        if nk == 1:
            o_ref[...] = prod.astype(out_dtype)
        else:
            acc = scratch[0]
            k = pl.program_id(2)

            @pl.when(k == 0)
            def _():
                acc[...] = prod

            @pl.when(k > 0)
            def _():
                acc[...] += prod

            @pl.when(k == nk - 1)
            def _():
                o_ref[...] = acc[...].astype(out_dtype)

    if trans_a:
        a_spec = pl.BlockSpec((tk, tm), lambda i, j, k: (k, i))
    else:
        a_spec = pl.BlockSpec((tm, tk), lambda i, j, k: (i, k))
    if trans_b:
        b_spec = pl.BlockSpec((tn, tk), lambda i, j, k: (j, k))
    else:
        b_spec = pl.BlockSpec((tk, tn), lambda i, j, k: (k, j))
    blocks = (_nbytes((tm, tk), a.dtype) + _nbytes((tk, tn), b.dtype) + _nbytes((tm, tn), out_dtype))
    scratch = [pltpu.VMEM((tm, tn), F32)] if nk > 1 else []
    return pl.pallas_call(
        body,
        out_shape=jax.ShapeDtypeStruct((M, N), out_dtype),
        grid=(M // tm, N // tn, nk),
        in_specs=[a_spec, b_spec],
        out_specs=pl.BlockSpec((tm, tn), lambda i, j, k: (i, j)),
        scratch_shapes=scratch,
        compiler_params=_params(("parallel", "parallel", "arbitrary"), blocks + _nbytes((tm, tn), F32),
                                _nbytes((tm, tn), F32) if nk > 1 else 0),
        name=name,
    )(a, b)


def _h_tile(j, x_ref, meta_ref):
    head = jnp.concatenate([jnp.zeros((FRONT, D), F32), meta_ref[...]], axis=0)
    return jnp.where(j > 0, x_ref[0], head)


def _x_spec():
    return pl.BlockSpec((1, QB, D), lambda b, j: (b, jnp.maximum(j - 1, 0), 0))


def _rms_in(x, meta, g, B, Lp):
    T = B * Lp
    NQ = Lp // QB

    def body(x_ref, meta_ref, g_ref, u_ref):
        h = _h_tile(pl.program_id(1), x_ref, meta_ref)
        r = lax.rsqrt(jnp.mean(h * h, axis=-1, keepdims=True) + EPS)
        u_ref[...] = (h * r * g_ref[...]).astype(BF16)

    return pl.pallas_call(
        body,
        out_shape=jax.ShapeDtypeStruct((T, D), BF16),
        grid=(B, NQ),
        in_specs=[_x_spec(), pl.BlockSpec((N_META, D), lambda b, j: (0, 0)), pl.BlockSpec((1, D), lambda b, j: (0, 0))],
        out_specs=pl.BlockSpec((QB, D), lambda b, j: (b * NQ + j, 0)),
        compiler_params=_params(("parallel", "parallel"), _nbytes((QB, D), F32) * 2),
        name="rms_in",
    )(x, meta, g)


def _gla_gate(lr, wg, bg, valid):
    pre = _nn(lr.astype(BF16), wg) + bg
    logsig = jnp.minimum(pre, 0.0) - jnp.log(1.0 + jnp.exp(-jnp.abs(pre)))
    return pre, jnp.where(valid, logsig / GLA_NORMALIZER, 0.0)


def _tri_masks():
    ri = lax.broadcasted_iota(jnp.int32, (GLA_C, GLA_C), 0)
    ci = lax.broadcasted_iota(jnp.int32, (GLA_C, GLA_C), 1)
    return ci <= ri, ci >= ri


def _cumsum_rows(x, ones_mask):
    w = jnp.where(ones_mask, 1.0, 0.0).astype(BF16)
    a, b, c = _split3(x)
    return _nn(w, a) + _nn(w, b) + _nn(w, c)


def _gla_fwd(projA, projB, wg, bg, gn4, B, Lp):
    T = B * Lp
    NC = Lp // GLA_C
    C = GLA_C
    scale = GLA_DK ** -0.5

    def body(q_ref, k_ref, v_ref, lr_ref, z_ref, wg_ref, bg_ref, gn_ref, oa_ref, ya_ref, ssave_ref, st_ref):
        n = pl.program_id(0)

        @pl.when(n == 0)
        def _():
            st_ref[...] = jnp.zeros_like(st_ref)

        pos = n * C + lax.broadcasted_iota(jnp.int32, (C, 1), 0)
        lower, _ = _tri_masks()
        is_last = lax.broadcasted_iota(jnp.int32, (C, 1), 0) == C - 1
        for b in range(B):
            ssave_ref[b, 0] = st_ref[b]
            _, glog = _gla_gate(lr_ref[b], wg_ref[...], bg_ref[...], pos >= FRONT)
            bcum = _cumsum_rows(glog, lower)
            for h in range(GLA_H):
                ks = slice(h * GLA_DK, (h + 1) * GLA_DK)
                vs = slice(h * GLA_DV, (h + 1) * GLA_DV)
                bh = bcum[:, ks]
                blast = jnp.sum(jnp.where(is_last, bh, 0.0), axis=0, keepdims=True)
                qh = q_ref[b, :, ks] * scale
                kh = k_ref[b, :, ks]
                qe = (qh * jnp.exp(bh)).astype(BF16)
                ke = (kh * jnp.exp(-bh)).astype(BF16)
                kl = (kh * jnp.exp(blast - bh)).astype(BF16)
                vh = v_ref[b, :, vs].astype(BF16)
                a = jnp.where(lower, _nt(qe, ke), 0.0).astype(BF16)
                st = st_ref[b, h]
                o = _nn(a, vh) + _nt(qe, st.astype(BF16))
                st_ref[b, h] = st * jnp.exp(blast) + _tn(vh, kl)
                oa_ref[b, :, vs] = o
                on = o * lax.rsqrt(jnp.mean(o * o, axis=-1, keepdims=True) + EPS) * gn_ref[:, vs]
                z = z_ref[b, :, vs]
                ya_ref[b, :, vs] = (on * (z * _sigmoid(z))).astype(BF16)

    blocks = B * (_nbytes((C, 512), F32) * 2 + _nbytes((C, 1024), F32) * 3 + _nbytes((C, 1024), BF16)
                  + _nbytes((GLA_H, GLA_DV, GLA_DK), F32)) + _nbytes((128, 512), BF16)
    state = _nbytes((B, GLA_H, GLA_DV, GLA_DK), F32)
    pa = projA.reshape(B, Lp, projA.shape[1])
    oa, ya, ssave = pl.pallas_call(
        body,
        out_shape=(jax.ShapeDtypeStruct((B, Lp, GLA_VW), F32), jax.ShapeDtypeStruct((B, Lp, GLA_VW), BF16),
                   jax.ShapeDtypeStruct((B, NC, GLA_H, GLA_DV, GLA_DK), F32)),
        grid=(NC,),
        in_specs=[
            pl.BlockSpec((B, C, 512), lambda n: (0, n, 10)),
            pl.BlockSpec((B, C, 512), lambda n: (0, n, 11)),
            pl.BlockSpec((B, C, 1024), lambda n: (0, n, 0)),
            pl.BlockSpec((B, C, 128), lambda n: (0, n, 3)),
            pl.BlockSpec((B, C, 1024), lambda n: (0, n, 1)),
            pl.BlockSpec((128, 512), lambda n: (0, 0)),
            pl.BlockSpec((1, 512), lambda n: (0, 0)),
            pl.BlockSpec((1, 1024), lambda n: (0, 0)),
        ],
        out_specs=(pl.BlockSpec((B, C, 1024), lambda n: (0, n, 0)),
                   pl.BlockSpec((B, C, 1024), lambda n: (0, n, 0)),
                   pl.BlockSpec((B, 1, GLA_H, GLA_DV, GLA_DK), lambda n: (0, n, 0, 0, 0))),
        scratch_shapes=[pltpu.VMEM((B, GLA_H, GLA_DV, GLA_DK), F32)],
        compiler_params=_params(("arbitrary",), blocks, state),
        name="gla_fwd",
    )(pa, pa, pa, projB.reshape(B, Lp, projB.shape[1]), pa, wg, bg, gn4)
    return oa.reshape(T, GLA_VW), ya.reshape(T, GLA_VW), ssave


def _swap_halves(x):
    lane = lax.broadcasted_iota(jnp.int32, x.shape, 1)
    return jnp.where((lane % 64) < 32, pltpu.roll(x, 96, 1), pltpu.roll(x, 32, 1))


def _mla_prep(projB, cos_t, sin_t, gq, gkv, wuq2, wukv, B, Lp, tr):
    T = B * Lp
    nt = Lp // tr
    HW = 2 * LANES

    def body(pb_ref, cos_ref, sin_ref, gq_ref, gkv_ref, wuq_ref, wukv_ref, q_ref, k_ref, v_ref, cqn_ref, ckvn_ref):
        cq = pb_ref[:, 0:Q_RANK]
        ckv = pb_ref[:, Q_RANK:Q_RANK + KV_RANK]
        kr = pb_ref[:, 512:640]
        cqn = (cq * lax.rsqrt(jnp.mean(cq * cq, axis=-1, keepdims=True) + EPS) * gq_ref[...]).astype(BF16)
        ckvn = (ckv * lax.rsqrt(jnp.mean(ckv * ckv, axis=-1, keepdims=True) + EPS) * gkv_ref[...]).astype(BF16)
        cqn_ref[...] = cqn
        ckvn_ref[...] = ckvn
        qf = _nn(cqn, wuq_ref[...])
        kvf = _nn(ckvn, wukv_ref[...])
        cs = cos_ref[...]
        sn = sin_ref[...]
        rope = lambda t: t * cs + _swap_halves(t) * sn
        kr_r = rope(kr).astype(BF16)
        for h in range(MLA_H):
            q_ref[:, h * HW:h * HW + LANES] = qf[:, h * HW:h * HW + LANES].astype(BF16)
            q_ref[:, h * HW + LANES:(h + 1) * HW] = rope(qf[:, h * HW + LANES:(h + 1) * HW]).astype(BF16)
            k_ref[:, h * HW:h * HW + LANES] = kvf[:, h * HW:h * HW + LANES].astype(BF16)
            k_ref[:, h * HW + LANES:(h + 1) * HW] = kr_r
            v_ref[:, h * MLA_DV:(h + 1) * MLA_DV] = kvf[:, h * HW + LANES:(h + 1) * HW].astype(BF16)

    blocks = (_nbytes((tr, 640), F32) + 2 * _nbytes((tr, 128), F32) + _nbytes((Q_RANK, 2048), BF16)
              + _nbytes((KV_RANK, 2048), BF16) + _nbytes((tr, 2048 * 2 + 1024 + 384), BF16)
              + 2 * _nbytes((tr, 2048), F32))
    return pl.pallas_call(
        body,
        out_shape=(jax.ShapeDtypeStruct((T, MLA_H * HW), BF16), jax.ShapeDtypeStruct((T, MLA_H * HW), BF16),
                   jax.ShapeDtypeStruct((T, MLA_H * MLA_DV), BF16), jax.ShapeDtypeStruct((T, Q_RANK), BF16),
                   jax.ShapeDtypeStruct((T, KV_RANK), BF16)),
        grid=(B, nt),
        in_specs=[
            pl.BlockSpec((tr, 640), lambda b, j: (b * nt + j, 0)),
            pl.BlockSpec((tr, 128), lambda b, j: (j, 0)),
            pl.BlockSpec((tr, 128), lambda b, j: (j, 0)),
            pl.BlockSpec((1, Q_RANK), lambda b, j: (0, 0)),
            pl.BlockSpec((1, KV_RANK), lambda b, j: (0, 0)),
            pl.BlockSpec((Q_RANK, 2048), lambda b, j: (0, 0)),
            pl.BlockSpec((KV_RANK, 2048), lambda b, j: (0, 0)),
        ],
        out_specs=(pl.BlockSpec((tr, 2048), lambda b, j: (b * nt + j, 0)),
                   pl.BlockSpec((tr, 2048), lambda b, j: (b * nt + j, 0)),
                   pl.BlockSpec((tr, 1024), lambda b, j: (b * nt + j, 0)),
                   pl.BlockSpec((tr, Q_RANK), lambda b, j: (b * nt + j, 0)),
                   pl.BlockSpec((tr, KV_RANK), lambda b, j: (b * nt + j, 0))),
        compiler_params=_params(("parallel", "parallel"), blocks),
        name="mla_prep",
    )(projB, cos_t, sin_t, gq, gkv, wuq2, wukv)


def _attn_mask(row, col):
    return (col <= row) & ((col >= FRONT) | (row < FRONT))


def _attn_fwd(q_att, k_att, v_att, projA, B, Lp):
    T = B * Lp
    NQ = Lp // QB
    HW = 2 * LANES
    scale = 1.0 / math.sqrt(MLA_QK)

    def body(q_ref, k_ref, v_ref, mz_ref, o_ref, yb_ref, lsec_ref, m_ref, l_ref, acc_ref):
        qi = pl.program_id(1)
        m_ref[...] = jnp.full(m_ref.shape, NEG, F32)
        l_ref[...] = jnp.zeros_like(l_ref)
        acc_ref[...] = jnp.zeros_like(acc_ref)
        row = qi * QB + lax.broadcasted_iota(jnp.int32, (QB, QB), 0)
        coli = lax.broadcasted_iota(jnp.int32, (QB, QB), 1)

        def step(kj, carry):
            off = pl.multiple_of(kj * QB, QB)
            ok = _attn_mask(row, kj * QB + coli)
            for h in range(MLA_H):
                q = q_ref[:, h * HW:(h + 1) * HW]
                kb = k_ref[pl.ds(off, QB), h * HW:(h + 1) * HW]
                vb = v_ref[pl.ds(off, QB), h * MLA_DV:(h + 1) * MLA_DV]
                s = jnp.where(ok, _nt(q, kb) * scale, NEG)
                m_old = m_ref[h]
                m_new = jnp.maximum(m_old, jnp.max(s, axis=-1, keepdims=True))
                alpha = jnp.exp(m_old - m_new)
                p = jnp.exp(s - jnp.tile(m_new, (1, QB // LANES)))
                m_ref[h] = m_new
                l_ref[h] = alpha * l_ref[h] + jnp.sum(p, axis=-1, keepdims=True)
                acc_ref[h] = alpha * acc_ref[h] + _nn(p.astype(BF16), vb)
            return carry

        lax.fori_loop(0, qi + 1, step, 0)
        for h in range(MLA_H):
            hs = slice(h * MLA_DV, (h + 1) * MLA_DV)
            l = l_ref[h]
            o = acc_ref[h] / l
            o_ref[:, hs] = o
            z = mz_ref[:, hs]
            yb_ref[:, hs] = (o * (z * _sigmoid(z))).astype(BF16)
            lse = m_ref[h] + jnp.log(l)
            lsec_ref[0, h, pl.ds(qi, 1), :] = jnp.transpose(lse)[0:1, :]

    blocks = (_nbytes((QB, 2048), BF16) + _nbytes((Lp, 2048), BF16) + _nbytes((Lp, 1024), BF16)
              + 2 * _nbytes((QB, 1024), F32) + _nbytes((QB, 1024), BF16) + _nbytes((MLA_H, QB, LANES), F32)
              + _nbytes((MLA_H, NQ, QB), F32))
    return pl.pallas_call(
        body,
        out_shape=(jax.ShapeDtypeStruct((T, MLA_H * MLA_DV), F32), jax.ShapeDtypeStruct((T, MLA_H * MLA_DV), BF16),
                   jax.ShapeDtypeStruct((B, MLA_H, NQ, QB), F32)),
        grid=(B, NQ),
        in_specs=[
            pl.BlockSpec((QB, MLA_H * HW), lambda b, i: (b * NQ + i, 0)),
            pl.BlockSpec((Lp, MLA_H * HW), lambda b, i: (b, 0)),
            pl.BlockSpec((Lp, MLA_H * MLA_DV), lambda b, i: (b, 0)),
            pl.BlockSpec((QB, 1024), lambda b, i: (b * NQ + i, 2)),
        ],
        out_specs=(pl.BlockSpec((QB, 1024), lambda b, i: (b * NQ + i, 0)),
                   pl.BlockSpec((QB, 1024), lambda b, i: (b * NQ + i, 0)),
                   pl.BlockSpec((1, MLA_H, NQ, QB), lambda b, i: (b, 0, 0, 0))),
        scratch_shapes=[pltpu.VMEM((MLA_H, QB, LANES), F32), pltpu.VMEM((MLA_H, QB, LANES), F32),
                        pltpu.VMEM((MLA_H, QB, MLA_DV), F32)],
        compiler_params=_params(("parallel", "arbitrary"), blocks, 3 * _nbytes((MLA_H, QB, LANES), F32)),
        name="attn_fwd",
    )(q_att, k_att, v_att, projA)


def _out_proj_loss(x, meta, projA, ya, yb, w_out, gf, tgt, B, Lp):
    T = B * Lp
    NQ = Lp // QB

    def body(x_ref, meta_ref, gg_ref, gm_ref, ya_ref, yb_ref, w_ref, gf_ref, t_ref,
             dh_ref, dhb_ref, mg_ref, loss_ref, dgf_ref):
        b = pl.program_id(0)
        j = pl.program_id(1)

        @pl.when((b == 0) & (j == 0))
        def _():
            loss_ref[...] = jnp.zeros_like(loss_ref)
            dgf_ref[...] = jnp.zeros_like(dgf_ref)

        merged = (_sigmoid(gg_ref[...]) * ya_ref[...] + _sigmoid(gm_ref[...]) * yb_ref[...]).astype(BF16)
        mg_ref[...] = merged
        h1 = _h_tile(j, x_ref, meta_ref) + _nn(merged, w_ref[...])
        r = lax.rsqrt(jnp.mean(h1 * h1, axis=-1, keepdims=True) + EPS)
        hn = h1 * r
        gfv = gf_ref[...]
        diff = jnp.where(j > 0, hn * gfv - t_ref[0], 0.0)
        loss_ref[...] += (0.5 / D) * jnp.sum(jnp.sum(diff * diff, axis=-1, keepdims=True), axis=0, keepdims=True)
        dout = diff * (1.0 / D)
        dgf_ref[...] += jnp.sum(dout * hn, axis=0, keepdims=True)
        dhn = dout * gfv
        dh = r * (dhn - hn * jnp.mean(dhn * hn, axis=-1, keepdims=True))
        dh_ref[...] = dh
        dhb_ref[...] = dh.astype(BF16)

    rows = lambda c: pl.BlockSpec((QB, D), lambda b, j: (b * NQ + j, c))
    const = lambda s: pl.BlockSpec(s, lambda b, j: (0, 0))
    return pl.pallas_call(
        body,
        out_shape=(jax.ShapeDtypeStruct((T, D), F32), jax.ShapeDtypeStruct((T, D), BF16),
                   jax.ShapeDtypeStruct((T, D), BF16), jax.ShapeDtypeStruct((1, 1), F32),
                   jax.ShapeDtypeStruct((1, D), F32)),
        grid=(B, NQ),
        in_specs=[_x_spec(), const((N_META, D)), rows(3), rows(4), rows(0), rows(0), const((D, D)),
                  const((1, D)), _x_spec()],
        out_specs=(rows(0), rows(0), rows(0), const((1, 1)), const((1, D))),
        compiler_params=_params(("arbitrary", "arbitrary"), 10 * _nbytes((QB, D), F32)),
        name="out_proj_loss",
    )(x, meta, projA, projA, ya, yb, w_out, gf, tgt)


def _merge_bwd(dh1_b, w_out, projA, ya, yb, tr):
    T = dh1_b.shape[0]

    def body(dh_ref, w_ref, gg_ref, gm_ref, ya_ref, yb_ref, dya_ref, dyb_ref, da_ref):
        d = _nt(dh_ref[...], w_ref[...])
        sg = _sigmoid(gg_ref[...])
        sm = _sigmoid(gm_ref[...])
        dya_ref[...] = (d * sg).astype(BF16)
        dyb_ref[...] = (d * sm).astype(BF16)
        da_ref[:, 0:D] = (d * ya_ref[...] * (sg * (1.0 - sg))).astype(BF16)
        da_ref[:, D:2 * D] = (d * yb_ref[...] * (sm * (1.0 - sm))).astype(BF16)

    spec = lambda c: pl.BlockSpec((tr, D), lambda i: (i, c))
    return pl.pallas_call(
        body,
        out_shape=(jax.ShapeDtypeStruct((T, D), BF16), jax.ShapeDtypeStruct((T, D), BF16),
                   jax.ShapeDtypeStruct((T, 2 * D), BF16)),
        grid=(T // tr,),
        in_specs=[spec(0), pl.BlockSpec((D, D), lambda i: (0, 0)), spec(3), spec(4), spec(0), spec(0)],
        out_specs=(spec(0), spec(0), pl.BlockSpec((tr, 2 * D), lambda i: (i, 0))),
        compiler_params=_params(("parallel",), 8 * _nbytes((tr, D), F32)),
        name="merge_bwd",
    )(dh1_b, w_out, projA, projA, ya, yb)


def _gla_out_bwd(dya, gla_proj, oa, projA, gn4, tr):
    T = dya.shape[0]
    nsteps = T // tr

    def body(dya_ref, w_ref, oa_ref, z_ref, gn_ref, do_ref, dz_ref, dgn_ref, acc_ref):
        i = pl.program_id(0)

        @pl.when(i == 0)
        def _():
            acc_ref[...] = jnp.zeros_like(acc_ref)

        dy_all = _nt(dya_ref[...], w_ref[...])
        for h in range(GLA_H):
            vs = slice(h * GLA_DV, (h + 1) * GLA_DV)
            dy = dy_all[:, vs]
            o = oa_ref[:, vs]
            z = z_ref[:, vs]
            gn = gn_ref[:, vs]
            s = _sigmoid(z)
            ra = lax.rsqrt(jnp.mean(o * o, axis=-1, keepdims=True) + EPS)
            on = o * ra
            don = dy * (z * s)
            t = don * gn
            do_ref[:, vs] = (ra * (t - on * jnp.mean(t * on, axis=-1, keepdims=True))).astype(BF16)
            dz_ref[:, vs] = (dy * (on * gn) * (s * (1.0 + z * (1.0 - s)))).astype(BF16)
            acc_ref[:, vs] += jnp.sum(don * on, axis=0, keepdims=True)

        @pl.when(i == nsteps - 1)
        def _():
            a = acc_ref[...]
            dgn_ref[...] = a[:, 0:256] + a[:, 256:512] + a[:, 512:768] + a[:, 768:1024]

    spec = lambda c: pl.BlockSpec((tr, D), lambda i: (i, c))
    return pl.pallas_call(
        body,
        out_shape=(jax.ShapeDtypeStruct((T, D), BF16), jax.ShapeDtypeStruct((T, D), BF16),
                   jax.ShapeDtypeStruct((1, GLA_DV), F32)),
        grid=(nsteps,),
        in_specs=[spec(0), pl.BlockSpec((D, D), lambda i: (0, 0)), spec(0), spec(1),
                  pl.BlockSpec((1, D), lambda i: (0, 0))],
        out_specs=(spec(0), spec(0), pl.BlockSpec((1, GLA_DV), lambda i: (0, 0))),
        scratch_shapes=[pltpu.VMEM((1, D), F32)],
        compiler_params=_params(("arbitrary",), 6 * _nbytes((tr, D), F32)),
        name="gla_out_bwd",
    )(dya, gla_proj, oa, projA, gn4)


def _gla_bwd(projA, projB, ssave, doa, wg, bg, B, Lp):
    T = B * Lp
    NC = Lp // GLA_C
    C = GLA_C
    scale = GLA_DK ** -0.5
    WC = 2304

    def body(q_ref, k_ref, v_ref, lr_ref, ss_ref, do_ref, wg_ref, bg_ref, dc_ref, dwg_ref, dbg_ref, dst_ref):
        i = pl.program_id(0)
        n = NC - 1 - i

        @pl.when(i == 0)
        def _():
            dst_ref[...] = jnp.zeros_like(dst_ref)
            dwg_ref[...] = jnp.zeros_like(dwg_ref)
            dbg_ref[...] = jnp.zeros_like(dbg_ref)

        pos = n * C + lax.broadcasted_iota(jnp.int32, (C, 1), 0)
        valid = pos >= FRONT
        lower, upper = _tri_masks()
        is_last = lax.broadcasted_iota(jnp.int32, (C, 1), 0) == C - 1
        for b in range(B):
            lr = lr_ref[b]
            pre, glog = _gla_gate(lr, wg_ref[...], bg_ref[...], valid)
            bcum = _cumsum_rows(glog, lower)
            db_parts = []
            for h in range(GLA_H):
                ks = slice(h * GLA_DK, (h + 1) * GLA_DK)
                vs = slice(h * GLA_DV, (h + 1) * GLA_DV)
                bh = bcum[:, ks]
                blast = jnp.sum(jnp.where(is_last, bh, 0.0), axis=0, keepdims=True)
                eb, enb, ekl, ebl = jnp.exp(bh), jnp.exp(-bh), jnp.exp(blast - bh), jnp.exp(blast)
                qh = q_ref[b, :, ks] * scale
                kh = k_ref[b, :, ks]
                qe_f, ke_f, kl_f = qh * eb, kh * enb, kh * ekl
                qe, ke, kl = qe_f.astype(BF16), ke_f.astype(BF16), kl_f.astype(BF16)
                vh = v_ref[b, :, vs].astype(BF16)
                doh = do_ref[b, :, vs]
                st = ss_ref[b, 0, h]
                dst = dst_ref[b, h]
                st_b, dst_b = st.astype(BF16), dst.astype(BF16)
                da = jnp.where(lower, _nt(doh, vh), 0.0).astype(BF16)
                da_t = jnp.where(upper, _nt(vh, doh), 0.0).astype(BF16)
                a_t = jnp.where(upper, _nt(ke, qe), 0.0).astype(BF16)
                dqe = _nn(da, ke) + _nn(doh, st_b)
                dke = _nn(da_t, qe)
                dvh = _nn(a_t, doh) + _nt(kl, dst_b)
                dkl = _nn(vh, dst_b)
                dst_ref[b, h] = dst * ebl + _tn(doh, qe)
                deb = jnp.sum(st * dst, axis=0, keepdims=True)
                db = dqe * qe_f - dke * ke_f - dkl * kl_f
                db_last = jnp.sum(dkl * kl_f, axis=0, keepdims=True) + deb * ebl
                db_parts.append(db + jnp.where(is_last, db_last, 0.0))
                dc_ref[b, :, vs] = dvh.astype(BF16)
                dc_ref[b, :, 1024 + h * GLA_DK:1024 + (h + 1) * GLA_DK] = (dqe * eb * scale).astype(BF16)
                dc_ref[b, :, 1536 + h * GLA_DK:1536 + (h + 1) * GLA_DK] = (dke * enb + dkl * ekl).astype(BF16)
            dglog = _cumsum_rows(jnp.concatenate(db_parts, axis=1), upper)
            dpre = jnp.where(valid, dglog * (1.0 / GLA_NORMALIZER) / (1.0 + jnp.exp(pre)), 0.0)
            dpre_b = dpre.astype(BF16)
            dc_ref[b, :, 2048:2176] = _nt(dpre_b, wg_ref[...]).astype(BF16)
            dc_ref[b, :, 2176:2304] = jnp.zeros((C, 128), BF16)
            dwg_ref[...] += _tn(lr.astype(BF16), dpre_b)
            dbg_ref[...] += jnp.sum(dpre, axis=0, keepdims=True)

    blocks = B * (_nbytes((C, 512), F32) * 2 + _nbytes((C, 1024), F32) + _nbytes((C, 1024), BF16)
                  + _nbytes((GLA_H, GLA_DV, GLA_DK), F32) + _nbytes((C, WC), BF16)) + 3 * _nbytes((128, 512), F32)
    state = _nbytes((B, GLA_H, GLA_DV, GLA_DK), F32)
    pa = projA.reshape(B, Lp, projA.shape[1])
    rev = lambda i: NC - 1 - i
    dc, dwg, dbg = pl.pallas_call(
        body,
        out_shape=(jax.ShapeDtypeStruct((B, Lp, WC), BF16), jax.ShapeDtypeStruct((128, GLA_KW), F32),
                   jax.ShapeDtypeStruct((1, GLA_KW), F32)),
        grid=(NC,),
        in_specs=[
            pl.BlockSpec((B, C, 512), lambda i: (0, rev(i), 10)),
            pl.BlockSpec((B, C, 512), lambda i: (0, rev(i), 11)),
            pl.BlockSpec((B, C, 1024), lambda i: (0, rev(i), 0)),
            pl.BlockSpec((B, C, 128), lambda i: (0, rev(i), 3)),
            pl.BlockSpec((B, 1, GLA_H, GLA_DV, GLA_DK), lambda i: (0, rev(i), 0, 0, 0)),
            pl.BlockSpec((B, C, 1024), lambda i: (0, rev(i), 0)),
            pl.BlockSpec((128, 512), lambda i: (0, 0)),
            pl.BlockSpec((1, 512), lambda i: (0, 0)),
        ],
        out_specs=(pl.BlockSpec((B, C, WC), lambda i: (0, rev(i), 0)),
                   pl.BlockSpec((128, GLA_KW), lambda i: (0, 0)),
                   pl.BlockSpec((1, GLA_KW), lambda i: (0, 0))),
        scratch_shapes=[pltpu.VMEM((B, GLA_H, GLA_DV, GLA_DK), F32)],
        compiler_params=_params(("arbitrary",), blocks, state),
        name="gla_bwd",
    )(pa, pa, pa, projB.reshape(B, Lp, projB.shape[1]), ssave, doa.reshape(B, Lp, GLA_VW), wg, bg)
    return dc.reshape(T, WC), dwg, dbg


def _attn_bwd_pre(dyb, mla_proj, projA, ob, B, Lp):
    T = B * Lp
    NQ = Lp // QB

    def body(dyb_ref, w_ref, z_ref, o_ref, do_ref, dz_ref, dcol_ref):
        j = pl.program_id(1)
        dy_all = _nt(dyb_ref[...], w_ref[...])
        for h in range(MLA_H):
            hs = slice(h * MLA_DV, (h + 1) * MLA_DV)
            dy = dy_all[:, hs]
            z = z_ref[:, hs]
            o = o_ref[:, hs]
            s = _sigmoid(z)
            do = dy * (z * s)
            do_ref[:, hs] = do.astype(BF16)
            dz_ref[:, hs] = (dy * o * (s * (1.0 + z * (1.0 - s)))).astype(BF16)
            dl = jnp.broadcast_to(jnp.sum(do * o, axis=-1, keepdims=True), (QB, LANES))
            dcol_ref[0, h, pl.ds(j, 1), :] = jnp.transpose(dl)[0:1, :]

    rows = lambda c: pl.BlockSpec((QB, D), lambda b, j: (b * NQ + j, c))
    return pl.pallas_call(
        body,
        out_shape=(jax.ShapeDtypeStruct((T, D), BF16), jax.ShapeDtypeStruct((T, D), BF16),
                   jax.ShapeDtypeStruct((B, MLA_H, NQ, QB), F32)),
        grid=(B, NQ),
        in_specs=[rows(0), pl.BlockSpec((D, D), lambda b, j: (0, 0)), rows(2), rows(0)],
        out_specs=(rows(0), rows(0), pl.BlockSpec((1, MLA_H, NQ, QB), lambda b, j: (b, 0, 0, 0))),
        compiler_params=_params(("parallel", "arbitrary"), 6 * _nbytes((QB, D), F32)),
        name="attn_bwd_pre",
    )(dyb, mla_proj, projA, ob)


ATTN_BWD_HEADS = 4


def _attn_bwd(q_att, k_att, v_att, do, lse_c, delta_c, B, Lp):
    T = B * Lp
    NQ = Lp // QB
    G = ATTN_BWD_HEADS
    NG = MLA_H // G
    HW = 2 * LANES
    scale = 1.0 / math.sqrt(MLA_QK)

    def body(q_ref, k_ref, v_ref, do_ref, lse_ref, dl_ref, dq_ref, dk_ref, dv_ref):
        kj = pl.program_id(2)

        @pl.when(kj == 0)
        def _():
            dq_ref[...] = jnp.zeros_like(dq_ref)

        dk_ref[...] = jnp.zeros_like(dk_ref)
        dv_ref[...] = jnp.zeros_like(dv_ref)
        col = kj * QB + lax.broadcasted_iota(jnp.int32, (QB, QB), 0)
        rowi = lax.broadcasted_iota(jnp.int32, (QB, QB), 1)

        def step(qi, carry):
            off = pl.multiple_of(qi * QB, QB)
            ok = _attn_mask(qi * QB + rowi, col)
            for h in range(G):
                ws = slice(h * HW, (h + 1) * HW)
                hs = slice(h * MLA_DV, (h + 1) * MLA_DV)
                qb = q_ref[pl.ds(off, QB), ws]
                dob = do_ref[pl.ds(off, QB), hs]
                kb = k_ref[:, ws]
                lse = lse_ref[0, h, pl.ds(qi, 1), :]
                delta = dl_ref[0, h, pl.ds(qi, 1), :]
                s_t = _nt(kb, qb) * scale
                p_t = jnp.where(ok, jnp.exp(s_t - lse), 0.0)
                dv_ref[:, hs] += _nn(p_t.astype(BF16), dob)
                ds_t = (p_t * (_nt(v_ref[:, hs], dob) - delta) * scale).astype(BF16)
                dk_ref[:, ws] += _nn(ds_t, qb)
                dq_ref[pl.ds(off, QB), ws] += _tn(ds_t, kb)
            return carry

        lax.fori_loop(kj, NQ, step, 0)

    blocks = (_nbytes((Lp, G * HW), BF16) + _nbytes((Lp, G * MLA_DV), BF16) + _nbytes((QB, G * 384), BF16)
              + 2 * _nbytes((G, NQ, QB), F32) + _nbytes((QB, G * 384), F32) + _nbytes((Lp, G * HW), F32))
    return pl.pallas_call(
        body,
        out_shape=(jax.ShapeDtypeStruct((T, MLA_H * HW), F32), jax.ShapeDtypeStruct((T, MLA_H * HW), F32),
                   jax.ShapeDtypeStruct((T, MLA_H * MLA_DV), F32)),
        grid=(B, NG, NQ),
        in_specs=[
            pl.BlockSpec((Lp, G * HW), lambda b, g, j: (b, g)),
            pl.BlockSpec((QB, G * HW), lambda b, g, j: (b * NQ + j, g)),
            pl.BlockSpec((QB, G * MLA_DV), lambda b, g, j: (b * NQ + j, g)),
            pl.BlockSpec((Lp, G * MLA_DV), lambda b, g, j: (b, g)),
            pl.BlockSpec((1, G, NQ, QB), lambda b, g, j: (b, g, 0, 0)),
            pl.BlockSpec((1, G, NQ, QB), lambda b, g, j: (b, g, 0, 0)),
        ],
        out_specs=(pl.BlockSpec((Lp, G * HW), lambda b, g, j: (b, g)),
                   pl.BlockSpec((QB, G * HW), lambda b, g, j: (b * NQ + j, g)),
                   pl.BlockSpec((QB, G * MLA_DV), lambda b, g, j: (b * NQ + j, g))),
        compiler_params=_params(("parallel", "parallel", "arbitrary"), blocks),
        name="attn_bwd",
    )(q_att, k_att, v_att, do, lse_c, delta_c)


def _mla_bwd_post(dq, dk, dv, projB, cos_t, sin_t, gq, gkv, wuq2, wukv, B, Lp, tr):
    T = B * Lp
    nt = Lp // tr
    HW = 2 * LANES

    def body(dq_ref, dk_ref, dv_ref, pb_ref, cos_ref, sin_ref, gq_ref, gkv_ref, wuq_ref, wukv_ref,
             dqf_ref, dkvf_ref, de_ref, dgq_ref, dgkv_ref):
        first = (pl.program_id(0) == 0) & (pl.program_id(1) == 0)

        @pl.when(first)
        def _():
            dgq_ref[...] = jnp.zeros_like(dgq_ref)
            dgkv_ref[...] = jnp.zeros_like(dgkv_ref)

        cs = cos_ref[...]
        sn = sin_ref[...]
        rope_t = lambda t: t * cs + _swap_halves(t * sn)
        dkr = jnp.zeros((tr, LANES), F32)
        for h in range(MLA_H):
            dqf_ref[:, h * HW:h * HW + LANES] = dq_ref[:, h * HW:h * HW + LANES].astype(BF16)
            dqf_ref[:, h * HW + LANES:(h + 1) * HW] = rope_t(dq_ref[:, h * HW + LANES:(h + 1) * HW]).astype(BF16)
            dkvf_ref[:, h * HW:h * HW + LANES] = dk_ref[:, h * HW:h * HW + LANES].astype(BF16)
            dkvf_ref[:, h * HW + LANES:(h + 1) * HW] = dv_ref[:, h * MLA_DV:(h + 1) * MLA_DV].astype(BF16)
            dkr = dkr + dk_ref[:, h * HW + LANES:(h + 1) * HW]

        def norm_bwd(x, dn, g):
            r = lax.rsqrt(jnp.mean(x * x, axis=-1, keepdims=True) + EPS)
            xn = x * r
            t = dn * g
            return r * (t - xn * jnp.mean(t * xn, axis=-1, keepdims=True)), jnp.sum(dn * xn, axis=0, keepdims=True)

        dcq, dgq = norm_bwd(pb_ref[:, 0:Q_RANK], _nt(dqf_ref[...], wuq_ref[...]), gq_ref[...])
        dckv, dgkv = norm_bwd(pb_ref[:, Q_RANK:Q_RANK + KV_RANK], _nt(dkvf_ref[...], wukv_ref[...]), gkv_ref[...])
        dgq_ref[...] += dgq
        dgkv_ref[...] += dgkv
        de_ref[:, 0:Q_RANK] = dcq.astype(BF16)
        de_ref[:, Q_RANK:Q_RANK + KV_RANK] = dckv.astype(BF16)
        de_ref[:, 384:512] = rope_t(dkr).astype(BF16)

    rows = lambda w: pl.BlockSpec((tr, w), lambda b, j: (b * nt + j, 0))
    const = lambda s: pl.BlockSpec(s, lambda b, j: (0, 0))
    blocks = (2 * _nbytes((tr, 2048), F32) + _nbytes((tr, 1024), F32) + _nbytes((tr, 640), F32)
              + 2 * _nbytes((tr, 2048), BF16) + _nbytes((2048, 384), BF16) + 2 * _nbytes((tr, 2048), F32))
    return pl.pallas_call(
        body,
        out_shape=(jax.ShapeDtypeStruct((T, 2048), BF16), jax.ShapeDtypeStruct((T, 2048), BF16),
                   jax.ShapeDtypeStruct((T, 512), BF16), jax.ShapeDtypeStruct((1, Q_RANK), F32),
                   jax.ShapeDtypeStruct((1, KV_RANK), F32)),
        grid=(B, nt),
        in_specs=[rows(2048), rows(2048), rows(1024), rows(640),
                  pl.BlockSpec((tr, 128), lambda b, j: (j, 0)), pl.BlockSpec((tr, 128), lambda b, j: (j, 0)),
                  const((1, Q_RANK)), const((1, KV_RANK)), const((Q_RANK, 2048)), const((KV_RANK, 2048))],
        out_specs=(rows(2048), rows(2048), rows(512), const((1, Q_RANK)), const((1, KV_RANK))),
        compiler_params=_params(("arbitrary", "arbitrary"), blocks),
        name="mla_bwd_post",
    )(dq, dk, dv, projB, cos_t, sin_t, gq, gkv, wuq2, wukv)


def _in_proj_bwd(x, meta, dh1, dA, dBz, dC, dDz, dE, wA, wB, g, B, Lp):
    NQ = Lp // QB
    seq = x.shape[1]

    def body(x_ref, meta_ref, dh_ref, da_ref, db_ref, dc_ref, dd_ref, de_ref, wa_ref, wb_ref, g_ref,
             gx_ref, dmeta_ref, dg_ref):
        b = pl.program_id(0)
        j = pl.program_id(1)

        @pl.when((b == 0) & (j == 0))
        def _():
            dg_ref[...] = jnp.zeros_like(dg_ref)

        du = _nt(da_ref[...], wa_ref[:, 3072:5120])
        du = du + _nt(db_ref[...], wa_ref[:, 1024:2048])
        du = du + _nt(dd_ref[...], wa_ref[:, 2048:3072])
        du = du + _nt(dc_ref[:, 0:1024], wa_ref[:, 0:1024])
        du = du + _nt(dc_ref[:, 1024:2048], wa_ref[:, 5120:6144])
        du = du + _nt(dc_ref[:, 2048:2176], wb_ref[:, 384:512])
        du = du + _nt(de_ref[:, 0:384], wb_ref[:, 0:384])
        du = du + _nt(de_ref[:, 384:512], wb_ref[:, 512:640])

        x = _h_tile(j, x_ref, meta_ref)
        r = lax.rsqrt(jnp.mean(x * x, axis=-1, keepdims=True) + EPS)
        xn = x * r
        t = du * g_ref[...]
        dh0 = dh_ref[...] + r * (t - xn * jnp.mean(t * xn, axis=-1, keepdims=True))
        dg_ref[...] += jnp.sum(du * xn, axis=0, keepdims=True)
        gx_ref[0] = dh0

        @pl.when((j == 0) & (b == 0))
        def _():
            dmeta_ref[...] = dh0[FRONT:HEAD_ROWS, :]

        @pl.when((j == 0) & (b > 0))
        def _():
            dmeta_ref[...] += dh0[FRONT:HEAD_ROWS, :]

    rows = lambda w: pl.BlockSpec((QB, w), lambda b, j: (b * NQ + j, 0))
    const = lambda s: pl.BlockSpec(s, lambda b, j: (0, 0))
    widths = [a.shape[1] for a in (dA, dBz, dC, dDz, dE)]
    blocks = (sum(_nbytes((QB, w), BF16) for w in widths) + _nbytes(wA.shape, BF16) + _nbytes(wB.shape, BF16)
              + 4 * _nbytes((QB, D), F32))
    return pl.pallas_call(
        body,
        out_shape=(jax.ShapeDtypeStruct((B, seq, D), F32), jax.ShapeDtypeStruct((N_META, D), F32),
                   jax.ShapeDtypeStruct((1, D), F32)),
        grid=(B, NQ),
        in_specs=[_x_spec(), const((N_META, D)), rows(D)] + [rows(w) for w in widths]
        + [const(wA.shape), const(wB.shape), const((1, D))],
        out_specs=(_x_spec(), const((N_META, D)), const((1, D))),
        compiler_params=_params(("arbitrary", "arbitrary"), blocks),
        name="in_proj_bwd",
    )(x, meta, dh1, dA, dBz, dC, dDz, dE, wA, wB, g)


_VMEM_WHOLE = pl.BlockSpec(memory_space=pltpu.VMEM)


def _params_whole(arrays):
    total = sum(_nbytes(a.shape, a.dtype) for a in arrays)
    return pltpu.CompilerParams(vmem_limit_bytes=int(min(total + 12 * 1024 * 1024, VMEM_CAP_V7X)))


def _wire_dtype(shape):
    return BF16 if shape[-2] * shape[-1] >= WIRE_BF16_MIN_ELEMS else F32


def _pair_add_big(gp, recv, c):
    _, half, cols = recv.shape
    th = _div_tile(half, 64, 16)
    out_dtype = _wire_dtype(recv.shape)

    def body(c_ref, a_ref, b_ref, o_ref):
        o_ref[...] = (a_ref[:, 0] + b_ref[...]).astype(out_dtype)

    return pl.pallas_call(
        body,
        out_shape=jax.ShapeDtypeStruct(recv.shape, out_dtype),
        grid_spec=pltpu.PrefetchScalarGridSpec(
            num_scalar_prefetch=1,
            grid=(half // th,),
            in_specs=[pl.BlockSpec((4, 1, th, cols), lambda i, c_ref: (0, c_ref[0], i, 0)),
                      pl.BlockSpec((4, th, cols), lambda i, c_ref: (0, i, 0))],
            out_specs=pl.BlockSpec((4, th, cols), lambda i, c_ref: (0, i, 0)),
        ),
        compiler_params=_params(("parallel",), 3 * _nbytes((4, th, cols), F32)),
        name="grad_pair_add_big",
    )(c, gp.reshape(4, 2, half, cols), recv)


def _pair_add_small(gps, recvs):
    n = len(gps)

    def body(*refs):
        c = lax.axis_index("c")
        for t in range(n):
            g_ref, r_ref, o_ref = refs[t], refs[n + t], refs[2 * n + t]
            half = r_ref.shape[1]
            s = g_ref[:, pl.ds(pl.multiple_of(c * half, 8), half), :] + r_ref[...]
            o_ref[...] = s.astype(o_ref.dtype)

    return pl.pallas_call(
        body,
        out_shape=[jax.ShapeDtypeStruct(r.shape, _wire_dtype(r.shape)) for r in recvs],
        in_specs=[_VMEM_WHOLE] * (2 * n),
        out_specs=[_VMEM_WHOLE] * n,
        compiler_params=_params_whole(list(gps) + 2 * list(recvs)),
        name="grad_pair_add_small",
    )(*gps, *recvs)


def _chip_order_sum(landed_ref, own_ref, me):
    p = [jnp.where(me == k, own_ref[k], landed_ref[k]).astype(F32) for k in range(4)]
    return ((p[0] + p[1]) + p[2]) + p[3]


def _sum_chips_big(landed, own, pos):
    _, half, cols = landed.shape
    th = _div_tile(half, 64, 16)

    def body(pos_ref, l_ref, s_ref, o_ref):
        o_ref[0] = _chip_order_sum(l_ref, s_ref, pos_ref[1])

    spec = pl.BlockSpec((4, th, cols), lambda i, pos_ref: (0, i, 0))
    return pl.pallas_call(
        body,
        out_shape=jax.ShapeDtypeStruct((2, half, cols), F32),
        grid_spec=pltpu.PrefetchScalarGridSpec(
            num_scalar_prefetch=1,
            grid=(half // th,),
            in_specs=[spec, spec],
            out_specs=pl.BlockSpec((1, th, cols), lambda i, pos_ref: (pos_ref[0], i, 0)),
        ),
        compiler_params=_params(("parallel",), 3 * _nbytes((4, th, cols), F32)),
        name="grad_sum_chips_big",
    )(pos, landed, own)


def _sum_chips_small(landed, own):
    n = len(landed)

    def body(*refs):
        x, y, c = _mesh_pos()
        for t in range(n):
            refs[2 * n + t][c] = _chip_order_sum(refs[t], refs[n + t], 2 * x + y)

    return pl.pallas_call(
        body,
        out_shape=[jax.ShapeDtypeStruct((2,) + p.shape[1:], F32) for p in landed],
        in_specs=[_VMEM_WHOLE] * (2 * n),
        out_specs=[_VMEM_WHOLE] * n,
        compiler_params=_params_whole(list(landed) * 3),
        name="grad_sum_chips_small",
    )(*landed, *own)


def _adamw_update(w_ref, g_ref, m_ref, v_ref, d_ref, mo_ref, vo_ref):
    c1 = 1.0 - ADAM_B1 ** ADAM_STEP
    c2 = 1.0 - ADAM_B2 ** ADAM_STEP
    gv = g_ref[...]
    mn = ADAM_B1 * m_ref[...] + (1.0 - ADAM_B1) * gv
    vn = ADAM_B2 * v_ref[...] + (1.0 - ADAM_B2) * (gv * gv)
    mo_ref[...] = mn
    vo_ref[...] = vn
    d_ref[...] = -ADAM_LR * ((mn / c1) / (jnp.sqrt(vn / c2) + ADAM_EPS) + ADAM_WD * w_ref[...])


def _adamw_big(w, g, m, v):
    lead, (rows, cols) = w.shape[:-2], w.shape[-2:]
    assert all(n == 1 for n in lead)
    tr = _div_tile(rows, 128, 8)
    spec = pl.BlockSpec((1,) * len(lead) + (tr, cols), lambda i: (0,) * len(lead) + (i, 0))
    shp = jax.ShapeDtypeStruct(w.shape, F32)
    return pl.pallas_call(
        functools.partial(_adamw_update),
        out_shape=(shp, shp, shp),
        grid=(rows // tr,),
        in_specs=[spec] * 4,
        out_specs=(spec, spec, spec),
        compiler_params=_params(("parallel",), 7 * _nbytes((tr, cols), F32)),
        name="adamw_big",
    )(w, g, m, v)


def _adamw_small(ws, gs, ms, vs):
    n = len(ws)

    def body(*refs):
        for t in range(n):
            _adamw_update(refs[t], refs[n + t], refs[2 * n + t], refs[3 * n + t],
                          refs[4 * n + t], refs[5 * n + t], refs[6 * n + t])

    shapes = [jax.ShapeDtypeStruct(w.shape, F32) for w in ws]
    return pl.pallas_call(
        body,
        out_shape=shapes * 3,
        in_specs=[_VMEM_WHOLE] * (4 * n),
        out_specs=[_VMEM_WHOLE] * (3 * n),
        compiler_params=_params_whole(list(ws) * 7),
        name="adamw_small",
    )(*ws, *gs, *ms, *vs)


def _mesh_pos():
    return lax.axis_index("x"), lax.axis_index("y"), lax.axis_index("c")


def _other_chips(x, y):
    return [(1 - x, y), (x, 1 - y), (1 - x, 1 - y)]


_ANY = pl.BlockSpec(memory_space=pl.ANY)


PAIR_SPLIT_MIN_ROWS = 64


def _weight_gather(shards):
    n = len(shards)
    split = [s.shape[0] >= PAIR_SPLIT_MIN_ROWS for s in shards]

    def body(*refs):
        w_refs, o_refs = refs[:n], refs[n:2 * n]
        send_sems, recv_sems = refs[2 * n:]
        x, y, c = _mesh_pos()
        me = 2 * x + y
        chips = _other_chips(x, y)

        def rows_of(t, core):
            rows = shards[t].shape[0]
            if not split[t]:
                return pl.ds(0, rows)
            return pl.ds(pl.multiple_of(core * (rows // 2), 16), rows // 2)

        def landed(t, k, slot, rows, to):
            ref = o_refs[t].at[slot, rows]
            return pltpu.make_async_remote_copy(src_ref=ref, dst_ref=ref, send_sem=send_sems.at[6 * t + k],
                                                recv_sem=recv_sems.at[6 * t + k], device_id=to, device_id_type=MESH)

        sends = []
        for t in range(n):
            mine = rows_of(t, c)
            for k, (px, py) in enumerate(chips):
                cp = pltpu.make_async_remote_copy(src_ref=w_refs[t].at[mine], dst_ref=o_refs[t].at[me, mine],
                                                  send_sem=send_sems.at[6 * t + k], recv_sem=recv_sems.at[6 * t + k],
                                                  device_id=(px, py, c), device_id_type=MESH)
                cp.start()
                sends.append(cp)
        for t in range(n):
            mine = rows_of(t, c)
            for k, (px, py) in enumerate(chips):
                landed(t, k, 2 * px + py, mine, (x, y, c)).wait_recv()
                if split[t]:
                    cp = landed(t, 3 + k, 2 * px + py, mine, (x, y, 1 - c))
                    cp.start()
                    sends.append(cp)
        for t in range(n):
            if split[t]:
                for k, (px, py) in enumerate(chips):
                    landed(t, 3 + k, 2 * px + py, rows_of(t, 1 - c), (x, y, c)).wait_recv()
        for cp in sends:
            cp.wait_send()

    return pl.pallas_call(
        body,
        out_shape=[jax.ShapeDtypeStruct((4,) + s.shape, s.dtype) for s in shards],
        in_specs=[_ANY] * n,
        out_specs=[_ANY] * n,
        scratch_shapes=[pltpu.SemaphoreType.DMA((6 * n,)), pltpu.SemaphoreType.DMA((6 * n,))],
        name="weight_gather",
    )(*shards)


def _pair_swap(gps):
    n = len(gps)

    def body(*refs):
        g_refs, o_refs = refs[:n], refs[n:2 * n]
        send_sems, recv_sems = refs[2 * n:]
        x, y, c = _mesh_pos()
        copies = []
        for t in range(n):
            half = gps[t].shape[1] // 2
            theirs = pl.ds(pl.multiple_of((1 - c) * half, 8), half)
            cp = pltpu.make_async_remote_copy(src_ref=g_refs[t].at[:, theirs], dst_ref=o_refs[t],
                                              send_sem=send_sems.at[t], recv_sem=recv_sems.at[t],
                                              device_id=(x, y, 1 - c), device_id_type=MESH)
            cp.start()
            copies.append(cp)
        for cp in copies:
            cp.wait_send()
            cp.wait_recv()

    return pl.pallas_call(
        body,
        out_shape=[jax.ShapeDtypeStruct((4, g.shape[1] // 2, g.shape[2]), g.dtype) for g in gps],
        in_specs=[_ANY] * n,
        out_specs=[_ANY] * n,
        scratch_shapes=[pltpu.SemaphoreType.DMA((n,)), pltpu.SemaphoreType.DMA((n,))],
        name="grad_pair_swap",
    )(*gps)


def _chip_scatter(parts):
    n = len(parts)

    def body(*refs):
        s_refs, o_refs = refs[:n], refs[n:2 * n]
        send_sems, recv_sems = refs[2 * n:]
        x, y, c = _mesh_pos()
        me = 2 * x + y
        chips = _other_chips(x, y)
        sends = []
        for t in range(n):
            for k, (px, py) in enumerate(chips):
                cp = pltpu.make_async_remote_copy(src_ref=s_refs[t].at[2 * px + py], dst_ref=o_refs[t].at[me],
                                                  send_sem=send_sems.at[3 * t + k], recv_sem=recv_sems.at[3 * t + k],
                                                  device_id=(px, py, c), device_id_type=MESH)
                cp.start()
                sends.append(cp)
        for t in range(n):
            for k, (px, py) in enumerate(chips):
                pltpu.make_async_remote_copy(src_ref=s_refs[t].at[me], dst_ref=o_refs[t].at[2 * px + py],
                                             send_sem=send_sems.at[3 * t + k], recv_sem=recv_sems.at[3 * t + k],
                                             device_id=(x, y, c), device_id_type=MESH).wait_recv()
        for cp in sends:
            cp.wait_send()

    return pl.pallas_call(
        body,
        out_shape=[jax.ShapeDtypeStruct(p.shape, p.dtype) for p in parts],
        in_specs=[_ANY] * n,
        out_specs=[_ANY] * n,
        scratch_shapes=[pltpu.SemaphoreType.DMA((3 * n,)), pltpu.SemaphoreType.DMA((3 * n,))],
        name="grad_chip_scatter",
    )(*parts)


def _pair_join(fs):
    n = len(fs)

    def body(*refs):
        f_refs, o_refs = refs[:n], refs[n:2 * n]
        send_sems, recv_sems = refs[2 * n:]
        x, y, c = _mesh_pos()
        sends = []
        for t in range(n):
            cp = pltpu.make_async_remote_copy(src_ref=f_refs[t].at[c], dst_ref=o_refs[t].at[c], send_sem=send_sems.at[t],
                                              recv_sem=recv_sems.at[t], device_id=(x, y, 1 - c), device_id_type=MESH)
            cp.start()
            sends.append(cp)
        for t in range(n):
            pltpu.make_async_remote_copy(src_ref=f_refs[t].at[c], dst_ref=o_refs[t].at[1 - c], send_sem=send_sems.at[t],
                                         recv_sem=recv_sems.at[t], device_id=(x, y, c), device_id_type=MESH).wait_recv()
        for cp in sends:
            cp.wait_send()

    return pl.pallas_call(
        body,
        out_shape=[jax.ShapeDtypeStruct(f.shape, f.dtype) for f in fs],
        in_specs=[_ANY] * n,
        out_specs=[_ANY] * n,
        input_output_aliases={t: t for t in range(n)},
        scratch_shapes=[pltpu.SemaphoreType.DMA((n,)), pltpu.SemaphoreType.DMA((n,))],
        name="grad_pair_join",
    )(*fs)


def _rope_tables(Lp):
    inv = 1.0 / (ROPE_BASE ** (jnp.arange(0, ROPE, 2, dtype=F32) / ROPE))
    ang = (jnp.arange(Lp, dtype=F32) - FRONT)[:, None] * inv[None, :]
    cs, sn = jnp.cos(ang), jnp.sin(ang)
    return jnp.tile(cs, (1, 4)), jnp.concatenate([-sn, sn, -sn, sn], axis=1)


def _local_step(x, loss_target, meta, norm_g, w_in, gate_w, gate_b, gla_norm_g, gla_proj, q_norm_g, w_uq,
                kv_norm_g, w_ukv, mla_proj, w_out, final_norm_g):
    B, seq, _ = x.shape
    Lp = HEAD_ROWS + seq
    T = B * Lp
    tr = _div_tile(Lp, 544, 16)
    tq = _div_tile(T, 1024, QB)

    cuts = np.cumsum((0,) + SPLITS)
    shard_w = IN_WIDTH // 4

    def w_cols(i, width=None):
        parts = []
        for j in range(4):
            a, b = max(cuts[i], j * shard_w), min(cuts[i + 1], (j + 1) * shard_w)
            if a < b:
                parts.append(w_in[j][:, a - j * shard_w:b - j * shard_w])
        if width is not None:
            parts.append(jnp.zeros((D, width - (cuts[i + 1] - cuts[i])), w_in.dtype))
        return parts

    i_q, i_k, i_v, i_lr, i_z, i_cq, i_ckv, i_kr, i_mz, i_gg, i_gm = range(11)
    wA = jnp.concatenate(sum([w_cols(i) for i in (i_v, i_z, i_mz, i_gg, i_gm, i_q, i_k)], []), axis=1)
    wB = jnp.concatenate(w_cols(i_cq) + w_cols(i_ckv) + w_cols(i_lr, 128) + w_cols(i_kr, 128), axis=1)
    wg = jnp.pad(gate_w, ((0, 128 - GLA_RANK), (0, 0)))
    wuq2 = jnp.pad(w_uq.reshape(Q_RANK, MLA_H, MLA_QK), ((0, 0), (0, 0), (0, 256 - MLA_QK))).reshape(Q_RANK, 2048)
    gn4 = jnp.tile(gla_norm_g, (1, GLA_H))
    cos_t, sin_t = _rope_tables(Lp)

    u = _rms_in(x, meta, norm_g, B, Lp)
    projA = _mm(u, wA, name="in_proj_a", tm=tq, tn=1024, tk=D)
    projB = _mm(u, wB, name="in_proj_b", tm=tq, tn=640, tk=D)
    oa, ya_in, ssave = _gla_fwd(projA, projB, wg, gate_b, gn4, B, Lp)
    ya = _mm(ya_in, gla_proj, name="gla_proj", tm=tq, tn=D, tk=D)
    q_att, k_att, v_att, cqn, ckvn = _mla_prep(projB, cos_t, sin_t, q_norm_g, kv_norm_g, wuq2, w_ukv, B, Lp, tr)
    ob, yb_in, lse_c = _attn_fwd(q_att, k_att, v_att, projA, B, Lp)
    yb = _mm(yb_in, mla_proj, name="mla_proj", tm=tq, tn=D, tk=D)
    dh1, dh1_b, merged, loss, d_gf = _out_proj_loss(x, meta, projA, ya, yb, w_out, final_norm_g.reshape(1, D),
                                                     loss_target, B, Lp)

    g_w_out = _mm(merged, dh1_b, name="dw_out", trans_a=True, tm=D, tn=D, tk=tq)
    dya, dyb, dA = _merge_bwd(dh1_b, w_out, projA, ya, yb, tr)
    g_gla_proj = _mm(ya_in, dya, name="dw_gla_proj", trans_a=True, tm=D, tn=D, tk=tq)
    g_mla_proj = _mm(yb_in, dyb, name="dw_mla_proj", trans_a=True, tm=D, tn=D, tk=tq)
    doa, dBz, d_gn = _gla_out_bwd(dya, gla_proj, oa, projA, gn4, tr)
    dC, g_wg, d_bg = _gla_bwd(projA, projB, ssave, doa, wg, gate_b, B, Lp)
    do, dDz, delta_c = _attn_bwd_pre(dyb, mla_proj, projA, ob, B, Lp)
    dq, dk, dv = _attn_bwd(q_att, k_att, v_att, do, lse_c, delta_c, B, Lp)
    dqf, dkvf, dE, d_gq, d_gkv = _mla_bwd_post(dq, dk, dv, projB, cos_t, sin_t, q_norm_g, kv_norm_g,
                                                wuq2, w_ukv, B, Lp, tr)
    g_wuq2 = _mm(cqn, dqf, name="dw_uq", trans_a=True, tm=Q_RANK, tn=2048, tk=tq)
    g_wukv = _mm(ckvn, dkvf, name="dw_ukv", trans_a=True, tm=KV_RANK, tn=2048, tk=tq)
    dparts = [dA, dBz, dC, dDz, dE]
    g_in = [_mm(u, dp, name="dw_in_%d" % i, trans_a=True, tm=D, tn=_div_tile(dp.shape[1], 1024, 256), tk=tq)
            for i, dp in enumerate(dparts)]
    grad_x, d_meta, d_ng = _in_proj_bwd(x, meta, dh1, dA, dBz, dC, dDz, dE, wA, wB, norm_g, B, Lp)

    gA, gBz, gC, gDz, gE = g_in
    src = [(gC, 1024), (gC, 1536), (gC, 0), (gC, 2048), (gBz, 0), (gE, 0), (gE, Q_RANK), (gE, 384), (gDz, 0),
           (gA, 0), (gA, D)]
    owners = []
    for j in range(4):
        parts = []
        for i, (arr, off) in enumerate(src):
            a, b = max(cuts[i], j * shard_w), min(cuts[i + 1], (j + 1) * shard_w)
            if a < b:
                parts.append(arr[:, off + a - cuts[i]:off + b - cuts[i]])
        owners.append(jnp.concatenate(parts, axis=1))
    g_w_in = jnp.stack(owners)
    g_wuq = g_wuq2.reshape(Q_RANK, MLA_H, 256)[:, :, :MLA_QK].reshape(Q_RANK, MLA_H * MLA_QK)
    grads = dict(w_in=g_w_in, gla_gate_w=g_wg[:GLA_RANK], gla_proj=g_gla_proj, mla_w_uq=g_wuq, mla_w_ukv=g_wukv,
                 mla_proj=g_mla_proj, w_out=g_w_out, meta_tokens=d_meta, norm_g=d_ng, gla_gate_b=d_bg,
                 gla_norm_g=d_gn, mla_q_norm_g=d_gq, mla_kv_norm_g=d_gkv, final_norm_g=d_gf)
    return loss[0, 0], grad_x, grads


_MATS = ("w_in", "gla_gate_w", "gla_proj", "mla_w_uq", "mla_w_ukv", "mla_proj", "w_out")
_ROW_SHARDED = ("gla_proj", "mla_proj", "w_out")
_ORDER = ("meta_tokens", "norm_g", "w_in", "gla_gate_w", "gla_gate_b", "gla_norm_g", "gla_proj", "mla_q_norm_g",
          "mla_w_uq", "mla_kv_norm_g", "mla_w_ukv", "mla_proj", "w_out", "final_norm_g")
WIRE_BF16_MIN_ELEMS = 128 * 128
SMALL_PACK_ROWS = 16


def _pack_small(d):
    rows = [jnp.pad(d[n].reshape(1, size), ((0, 0), (0, D - size))) for n, size in SMALL]
    return jnp.pad(jnp.concatenate(rows, axis=0), ((0, SMALL_PACK_ROWS - len(rows)), (0, 0)))


def _unpack_small(packed):
    return {n: packed[i, :size] for i, (n, size) in enumerate(SMALL)}


def kernel(x, meta_tokens, norm_g, w_in, gla_gate_w, gla_gate_b, gla_norm_g, gla_proj, mla_q_norm_g, mla_w_uq, mla_kv_norm_g, mla_w_ukv, mla_proj, w_out, final_norm_g, loss_target, m_meta_tokens, m_norm_g, m_w_in, m_gla_gate_w, m_gla_gate_b, m_gla_norm_g, m_gla_proj, m_mla_q_norm_g, m_mla_w_uq, m_mla_kv_norm_g, m_mla_w_ukv, m_mla_proj, m_w_out, m_final_norm_g, v_meta_tokens, v_norm_g, v_w_in, v_gla_gate_w, v_gla_gate_b, v_gla_norm_g, v_gla_proj, v_mla_q_norm_g, v_mla_w_uq, v_mla_kv_norm_g, v_mla_w_ukv, v_mla_proj, v_w_out, v_final_norm_g):
    w = dict(meta_tokens=meta_tokens, norm_g=norm_g, w_in=w_in[0], gla_gate_w=gla_gate_w[0], gla_gate_b=gla_gate_b,
             gla_norm_g=gla_norm_g, gla_proj=gla_proj[0], mla_q_norm_g=mla_q_norm_g, mla_w_uq=mla_w_uq[0],
             mla_kv_norm_g=mla_kv_norm_g, mla_w_ukv=mla_w_ukv[0], mla_proj=mla_proj[0], w_out=w_out[0],
             final_norm_g=final_norm_g)
    mom = dict(meta_tokens=m_meta_tokens, norm_g=m_norm_g, w_in=m_w_in[0], gla_gate_w=m_gla_gate_w[0],
               gla_gate_b=m_gla_gate_b, gla_norm_g=m_gla_norm_g, gla_proj=m_gla_proj[0], mla_q_norm_g=m_mla_q_norm_g,
               mla_w_uq=m_mla_w_uq[0], mla_kv_norm_g=m_mla_kv_norm_g, mla_w_ukv=m_mla_w_ukv[0], mla_proj=m_mla_proj[0],
               w_out=m_w_out[0], final_norm_g=m_final_norm_g)
    var = dict(meta_tokens=v_meta_tokens, norm_g=v_norm_g, w_in=v_w_in[0], gla_gate_w=v_gla_gate_w[0],
               gla_gate_b=v_gla_gate_b, gla_norm_g=v_gla_norm_g, gla_proj=v_gla_proj[0], mla_q_norm_g=v_mla_q_norm_g,
               mla_w_uq=v_mla_w_uq[0], mla_kv_norm_g=v_mla_kv_norm_g, mla_w_ukv=v_mla_w_ukv[0], mla_proj=v_mla_proj[0],
               w_out=v_w_out[0], final_norm_g=v_final_norm_g)
    out_shapes = {n: a.shape for n, a in zip(_ORDER, (meta_tokens, norm_g, w_in, gla_gate_w, gla_gate_b, gla_norm_g,
                                                     gla_proj, mla_q_norm_g, mla_w_uq, mla_kv_norm_g, mla_w_ukv,
                                                     mla_proj, w_out, final_norm_g))}

    me = (2 * lax.axis_index("x") + lax.axis_index("y")).astype(jnp.int32)
    shards = [w[n].astype(BF16) for n in _MATS] + [meta_tokens]
    gathered = [lax.dynamic_update_slice(gth, own[None], (me, 0, 0))
                for gth, own in zip(_weight_gather(shards), shards)]
    full = {}
    for name, gth in zip(_MATS, gathered):
        if name == "w_in":
            full[name] = gth
        elif name in _ROW_SHARDED:
            full[name] = gth.reshape(4 * gth.shape[1], gth.shape[2])
        else:
            full[name] = gth.transpose(1, 0, 2).reshape(gth.shape[1], 4 * gth.shape[2])
    meta_full = gathered[-1].transpose(1, 0, 2).reshape(N_META, D)

    loss_local, grad_x, g = _local_step(
        x, loss_target, meta_full, norm_g, full["w_in"], full["gla_gate_w"], gla_gate_b, gla_norm_g, full["gla_proj"],
        mla_q_norm_g, full["mla_w_uq"], mla_kv_norm_g, full["mla_w_ukv"], full["mla_proj"], full["w_out"], final_norm_g)
    loss = lax.psum(loss_local, ("x", "y", "c"))

    def by_owner(name, arr):
        if name == "w_in":
            return arr
        if name in _ROW_SHARDED:
            return arr.reshape(4, arr.shape[0] // 4, arr.shape[1])
        return arr.reshape(arr.shape[0], 4, arr.shape[1] // 4).transpose(1, 0, 2)

    names = _MATS + ("meta_tokens",)
    gps = [by_owner(n, g[n]) for n in names] + [jnp.broadcast_to(_pack_small(g)[None], (4, SMALL_PACK_ROWS, D))]
    recvs = _pair_swap(gps)
    c_idx = lax.axis_index("c").astype(jnp.int32).reshape(1)
    s1 = [_pair_add_big(gps[0], recvs[0], c_idx)] + list(_pair_add_small(gps[1:], recvs[1:]))
    landed = _chip_scatter(s1)
    pos = jnp.stack([c_idx[0], me])
    halves = [_sum_chips_big(landed[0], s1[0], pos)] + list(_sum_chips_small(landed[1:], s1[1:]))
    shapes = [out_shapes[n] for n in names] + [(SMALL_PACK_ROWS, D)]
    g_red = [j.reshape(s) for j, s in zip(_pair_join(halves), shapes)]

    tens = lambda d: [d[n].reshape(out_shapes[n]) for n in names] + [_pack_small(d)]
    w_t, m_t, v_t = tens(w), tens(mom), tens(var)
    big = _adamw_big(w_t[0], g_red[0], m_t[0], v_t[0])
    rest = _adamw_small(w_t[1:], g_red[1:], m_t[1:], v_t[1:])
    k = len(names)
    results = {"grad": g_red}
    for i, kind in enumerate(("delta", "new_m", "new_v")):
        results[kind] = [big[i]] + list(rest[i * k:(i + 1) * k])

    outs = []
    for kind in ("grad", "delta", "new_m", "new_v"):
        vals = dict(zip(names, results[kind][:-1]))
        vals.update(_unpack_small(results[kind][-1]))
        outs += [vals[n].reshape(out_shapes[n]) for n in _ORDER]
    return (loss, grad_x, *outs)
```

```python
import functools
import math

import jax
import jax.numpy as jnp
import numpy as np
from jax import lax
from jax.experimental import pallas as pl
from jax.experimental.pallas import tpu as pltpu

F32 = jnp.float32
BF16 = jnp.bfloat16

D = 1024
N_META = 16
QB = 256
FRONT = QB - N_META
HEAD_ROWS = FRONT + N_META
assert FRONT % 64 == 48
EPS = 1e-6

GLA_H, GLA_DK, GLA_DV, GLA_RANK, GLA_C = 4, 128, 256, 16, 64
GLA_NORMALIZER = 16.0
GLA_KW, GLA_VW = GLA_H * GLA_DK, GLA_H * GLA_DV
MLA_H, NOPE, ROPE, MLA_DV, Q_RANK, KV_RANK = 8, 128, 64, 128, 256, 128
MLA_QK = NOPE + ROPE
ROPE_BASE = 10000.0
SPLITS = (GLA_KW, GLA_KW, GLA_VW, GLA_RANK, GLA_VW, Q_RANK, KV_RANK, ROPE, MLA_H * MLA_DV, D, D)
IN_WIDTH = sum(SPLITS)

ADAM_LR, ADAM_B1, ADAM_B2, ADAM_EPS, ADAM_WD, ADAM_STEP = 0.001, 0.9, 0.999, 1e-08, 0.01, 10

LANES = 128
VMEM_CAP_V7X = 56 * 1024 * 1024
MESH = pl.DeviceIdType.MESH
NEG = -1e30

SMALL = (("norm_g", D), ("gla_gate_b", GLA_KW), ("gla_norm_g", GLA_DV), ("mla_q_norm_g", Q_RANK),
         ("mla_kv_norm_g", KV_RANK), ("final_norm_g", D))


def _div_tile(n, target, mult):
    best = None
    for d in range(mult, min(n, target) + 1, mult):
        if n % d == 0:
            best = d
    assert best is not None, (n, target, mult)
    return best


def _params(sem, block_bytes, scratch_bytes=0):
    est = 2 * block_bytes + scratch_bytes + 12 * 1024 * 1024
    return pltpu.CompilerParams(dimension_semantics=sem, vmem_limit_bytes=int(min(max(est, 24 * 1024 * 1024), VMEM_CAP_V7X)))


def _nbytes(shape, dtype):
    return int(np.prod(shape)) * jnp.dtype(dtype).itemsize


def _sigmoid(x):
    return 1.0 / (1.0 + jnp.exp(-x))


def _nt(a, b):
    return lax.dot_general(a, b, (((1,), (1,)), ((), ())), preferred_element_type=F32)


def _tn(a, b):
    return lax.dot_general(a, b, (((0,), (0,)), ((), ())), preferred_element_type=F32)


def _nn(a, b):
    return jnp.dot(a, b, preferred_element_type=F32)


def _split3(x):
    a = x.astype(BF16)
    r = x - a.astype(F32)
    b = r.astype(BF16)
    c = (r - b.astype(F32)).astype(BF16)
    return a, b, c


def _mm(a, b, *, name, trans_a=False, trans_b=False, out_dtype=F32, tm, tn, tk):
    assert not (trans_a and trans_b)
    if trans_a:
        K, M = a.shape
    else:
        M, K = a.shape
    N = b.shape[0] if trans_b else b.shape[1]
    assert (b.shape[1] if trans_b else b.shape[0]) == K
    assert M % tm == 0 and N % tn == 0 and K % tk == 0, (name, M, N, K, tm, tn, tk)
    nk = K // tk

    def body(a_ref, b_ref, o_ref, *scratch):
        av = a_ref[...].astype(BF16)
        bv = b_ref[...].astype(BF16)
        prod = _tn(av, bv) if trans_a else (_nt(av, bv) if trans_b else _nn(av, bv))
        if nk == 1:
            o_ref[...] = prod.astype(out_dtype)
        else:
            acc = scratch[0]
            k = pl.program_id(2)

            @pl.when(k == 0)
            def _():
                acc[...] = prod

            @pl.when(k > 0)
            def _():
                acc[...] += prod

            @pl.when(k == nk - 1)
            def _():
                o_ref[...] = acc[...].astype(out_dtype)

    if trans_a:
        a_spec = pl.BlockSpec((tk, tm), lambda i, j, k: (k, i))
    else:
        a_spec = pl.BlockSpec((tm, tk), lambda i, j, k: (i, k))
    if trans_b:
        b_spec = pl.BlockSpec((tn, tk), lambda i, j, k: (j, k))
    else:
        b_spec = pl.BlockSpec((tk, tn), lambda i, j, k: (k, j))
    blocks = (_nbytes((tm, tk), a.dtype) + _nbytes((tk, tn), b.dtype) + _nbytes((tm, tn), out_dtype))
    scratch = [pltpu.VMEM((tm, tn), F32)] if nk > 1 else []
    return pl.pallas_call(
        body,
        out_shape=jax.ShapeDtypeStruct((M, N), out_dtype),
        grid=(M // tm, N // tn, nk),
        in_specs=[a_spec, b_spec],
        out_specs=pl.BlockSpec((tm, tn), lambda i, j, k: (i, j)),
        scratch_shapes=scratch,
        compiler_params=_params(("parallel", "parallel", "arbitrary"), blocks + _nbytes((tm, tn), F32),
                                _nbytes((tm, tn), F32) if nk > 1 else 0),
        name=name,
    )(a, b)


def _h_tile(j, x_ref, meta_ref):
    head = jnp.concatenate([jnp.zeros((FRONT, D), F32), meta_ref[...]], axis=0)
    return jnp.where(j > 0, x_ref[0], head)


def _x_spec():
    return pl.BlockSpec((1, QB, D), lambda b, j: (b, jnp.maximum(j - 1, 0), 0))


def _rms_in(x, meta, g, B, Lp):
    T = B * Lp
    NQ = Lp // QB

    def body(x_ref, meta_ref, g_ref, u_ref):
        h = _h_tile(pl.program_id(1), x_ref, meta_ref)
        r = lax.rsqrt(jnp.mean(h * h, axis=-1, keepdims=True) + EPS)
        u_ref[...] = (h * r * g_ref[...]).astype(BF16)

    return pl.pallas_call(
        body,
        out_shape=jax.ShapeDtypeStruct((T, D), BF16),
        grid=(B, NQ),
        in_specs=[_x_spec(), pl.BlockSpec((N_META, D), lambda b, j: (0, 0)), pl.BlockSpec((1, D), lambda b, j: (0, 0))],
        out_specs=pl.BlockSpec((QB, D), lambda b, j: (b * NQ + j, 0)),
        compiler_params=_params(("parallel", "parallel"), _nbytes((QB, D), F32) * 2),
        name="rms_in",
    )(x, meta, g)


def _gla_gate(lr, wg, bg, valid):
    pre = _nn(lr.astype(BF16), wg) + bg
    logsig = jnp.minimum(pre, 0.0) - jnp.log(1.0 + jnp.exp(-jnp.abs(pre)))
    return pre, jnp.where(valid, logsig / GLA_NORMALIZER, 0.0)


def _tri_masks():
    ri = lax.broadcasted_iota(jnp.int32, (GLA_C, GLA_C), 0)
    ci = lax.broadcasted_iota(jnp.int32, (GLA_C, GLA_C), 1)
    return ci <= ri, ci >= ri


def _cumsum_rows(x, ones_mask):
    w = jnp.where(ones_mask, 1.0, 0.0).astype(BF16)
    a, b, c = _split3(x)
    return _nn(w, a) + _nn(w, b) + _nn(w, c)


def _gla_fwd(projA, projB, wg, bg, gn4, B, Lp):
    T = B * Lp
    NC = Lp // GLA_C
    C = GLA_C
    scale = GLA_DK ** -0.5

    def body(q_ref, k_ref, v_ref, lr_ref, z_ref, wg_ref, bg_ref, gn_ref, oa_ref, ya_ref, ssave_ref, st_ref):
        n = pl.program_id(0)

        @pl.when(n == 0)
        def _():
            st_ref[...] = jnp.zeros_like(st_ref)

        pos = n * C + lax.broadcasted_iota(jnp.int32, (C, 1), 0)
        lower, _ = _tri_masks()
        is_last = lax.broadcasted_iota(jnp.int32, (C, 1), 0) == C - 1
        for b in range(B):
            ssave_ref[b, 0] = st_ref[b]
            _, glog = _gla_gate(lr_ref[b], wg_ref[...], bg_ref[...], pos >= FRONT)
            bcum = _cumsum_rows(glog, lower)
            for h in range(GLA_H):
                ks = slice(h * GLA_DK, (h + 1) * GLA_DK)
                vs = slice(h * GLA_DV, (h + 1) * GLA_DV)
                bh = bcum[:, ks]
                blast = jnp.sum(jnp.where(is_last, bh, 0.0), axis=0, keepdims=True)
                qh = q_ref[b, :, ks].astype(F32) * scale
                kh = k_ref[b, :, ks].astype(F32)
                qe = (qh * jnp.exp(bh)).astype(BF16)
                ke = (kh * jnp.exp(-bh)).astype(BF16)
                kl = (kh * jnp.exp(blast - bh)).astype(BF16)
                vh = v_ref[b, :, vs].astype(BF16)
                a = jnp.where(lower, _nt(qe, ke), 0.0).astype(BF16)
                st = st_ref[b, h]
                o = _nn(a, vh) + _nt(qe, st.astype(BF16))
                st_ref[b, h] = st * jnp.exp(blast) + _tn(vh, kl)
                oa_ref[b, :, vs] = o
                on = o * lax.rsqrt(jnp.mean(o * o, axis=-1, keepdims=True) + EPS) * gn_ref[:, vs]
                z = z_ref[b, :, vs].astype(F32)
                ya_ref[b, :, vs] = (on * (z * _sigmoid(z))).astype(BF16)

    blocks = B * (_nbytes((C, 512), F32) * 2 + _nbytes((C, 1024), F32) * 3 + _nbytes((C, 1024), BF16)
                  + _nbytes((GLA_H, GLA_DV, GLA_DK), F32)) + _nbytes((128, 512), BF16)
    state = _nbytes((B, GLA_H, GLA_DV, GLA_DK), F32)
    pa = projA.reshape(B, Lp, projA.shape[1])
    oa, ya, ssave = pl.pallas_call(
        body,
        out_shape=(jax.ShapeDtypeStruct((B, Lp, GLA_VW), F32), jax.ShapeDtypeStruct((B, Lp, GLA_VW), BF16),
                   jax.ShapeDtypeStruct((B, NC, GLA_H, GLA_DV, GLA_DK), F32)),
        grid=(NC,),
        in_specs=[
            pl.BlockSpec((B, C, 512), lambda n: (0, n, 10)),
            pl.BlockSpec((B, C, 512), lambda n: (0, n, 11)),
            pl.BlockSpec((B, C, 1024), lambda n: (0, n, 0)),
            pl.BlockSpec((B, C, 128), lambda n: (0, n, 3)),
            pl.BlockSpec((B, C, 1024), lambda n: (0, n, 1)),
            pl.BlockSpec((128, 512), lambda n: (0, 0)),
            pl.BlockSpec((1, 512), lambda n: (0, 0)),
            pl.BlockSpec((1, 1024), lambda n: (0, 0)),
        ],
        out_specs=(pl.BlockSpec((B, C, 1024), lambda n: (0, n, 0)),
                   pl.BlockSpec((B, C, 1024), lambda n: (0, n, 0)),
                   pl.BlockSpec((B, 1, GLA_H, GLA_DV, GLA_DK), lambda n: (0, n, 0, 0, 0))),
        scratch_shapes=[pltpu.VMEM((B, GLA_H, GLA_DV, GLA_DK), F32)],
        compiler_params=_params(("arbitrary",), blocks, state),
        name="gla_fwd",
    )(pa, pa, pa, projB.reshape(B, Lp, projB.shape[1]), pa, wg, bg, gn4)
    return oa.reshape(T, GLA_VW), ya.reshape(T, GLA_VW), ssave


def _swap_halves(x):
    lane = lax.broadcasted_iota(jnp.int32, x.shape, 1)
    return jnp.where((lane % 64) < 32, pltpu.roll(x, 96, 1), pltpu.roll(x, 32, 1))


def _mla_prep(projB, cos_t, sin_t, gq, gkv, wuq2, wukv, B, Lp, tr):
    T = B * Lp
    nt = Lp // tr
    HW = 2 * LANES

    def body(pb_ref, cos_ref, sin_ref, gq_ref, gkv_ref, wuq_ref, wukv_ref, q_ref, k_ref, v_ref, cqn_ref, ckvn_ref):
        cq = pb_ref[:, 0:Q_RANK].astype(F32)
        ckv = pb_ref[:, Q_RANK:Q_RANK + KV_RANK].astype(F32)
        kr = pb_ref[:, 512:640].astype(F32)
        cqn = (cq * lax.rsqrt(jnp.mean(cq * cq, axis=-1, keepdims=True) + EPS) * gq_ref[...]).astype(BF16)
        ckvn = (ckv * lax.rsqrt(jnp.mean(ckv * ckv, axis=-1, keepdims=True) + EPS) * gkv_ref[...]).astype(BF16)
        cqn_ref[...] = cqn
        ckvn_ref[...] = ckvn
        qf = _nn(cqn, wuq_ref[...])
        kvf = _nn(ckvn, wukv_ref[...])
        cs = cos_ref[...]
        sn = sin_ref[...]
        rope = lambda t: t * cs + _swap_halves(t) * sn
        kr_r = rope(kr).astype(BF16)
        for h in range(MLA_H):
            q_ref[:, h * HW:h * HW + LANES] = qf[:, h * HW:h * HW + LANES].astype(BF16)
            q_ref[:, h * HW + LANES:(h + 1) * HW] = rope(qf[:, h * HW + LANES:(h + 1) * HW]).astype(BF16)
            k_ref[:, h * HW:h * HW + LANES] = kvf[:, h * HW:h * HW + LANES].astype(BF16)
            k_ref[:, h * HW + LANES:(h + 1) * HW] = kr_r
            v_ref[:, h * MLA_DV:(h + 1) * MLA_DV] = kvf[:, h * HW + LANES:(h + 1) * HW].astype(BF16)

    blocks = (_nbytes((tr, 640), F32) + 2 * _nbytes((tr, 128), F32) + _nbytes((Q_RANK, 2048), BF16)
              + _nbytes((KV_RANK, 2048), BF16) + _nbytes((tr, 2048 * 2 + 1024 + 384), BF16)
              + 2 * _nbytes((tr, 2048), F32))
    return pl.pallas_call(
        body,
        out_shape=(jax.ShapeDtypeStruct((T, MLA_H * HW), BF16), jax.ShapeDtypeStruct((T, MLA_H * HW), BF16),
                   jax.ShapeDtypeStruct((T, MLA_H * MLA_DV), BF16), jax.ShapeDtypeStruct((T, Q_RANK), BF16),
                   jax.ShapeDtypeStruct((T, KV_RANK), BF16)),
        grid=(B, nt),
        in_specs=[
            pl.BlockSpec((tr, 640), lambda b, j: (b * nt + j, 0)),
            pl.BlockSpec((tr, 128), lambda b, j: (j, 0)),
            pl.BlockSpec((tr, 128), lambda b, j: (j, 0)),
            pl.BlockSpec((1, Q_RANK), lambda b, j: (0, 0)),
            pl.BlockSpec((1, KV_RANK), lambda b, j: (0, 0)),
            pl.BlockSpec((Q_RANK, 2048), lambda b, j: (0, 0)),
            pl.BlockSpec((KV_RANK, 2048), lambda b, j: (0, 0)),
        ],
        out_specs=(pl.BlockSpec((tr, 2048), lambda b, j: (b * nt + j, 0)),
                   pl.BlockSpec((tr, 2048), lambda b, j: (b * nt + j, 0)),
                   pl.BlockSpec((tr, 1024), lambda b, j: (b * nt + j, 0)),
                   pl.BlockSpec((tr, Q_RANK), lambda b, j: (b * nt + j, 0)),
                   pl.BlockSpec((tr, KV_RANK), lambda b, j: (b * nt + j, 0))),
        compiler_params=_params(("parallel", "parallel"), blocks),
        name="mla_prep",
    )(projB, cos_t, sin_t, gq, gkv, wuq2, wukv)


def _attn_mask(row, col):
    return (col <= row) & ((col >= FRONT) | (row < FRONT))


def _attn_fwd(q_att, k_att, v_att, projA, B, Lp):
    T = B * Lp
    NQ = Lp // QB
    HW = 2 * LANES
    scale = 1.0 / math.sqrt(MLA_QK)

    def body(q_ref, k_ref, v_ref, mz_ref, o_ref, yb_ref, lsec_ref, m_ref, l_ref, acc_ref):
        qi = pl.program_id(1)
        m_ref[...] = jnp.full(m_ref.shape, NEG, F32)
        l_ref[...] = jnp.zeros_like(l_ref)
        acc_ref[...] = jnp.zeros_like(acc_ref)
        row = qi * QB + lax.broadcasted_iota(jnp.int32, (QB, QB), 0)
        coli = lax.broadcasted_iota(jnp.int32, (QB, QB), 1)

        def step(kj, carry):
            off = pl.multiple_of(kj * QB, QB)
            ok = _attn_mask(row, kj * QB + coli)
            for h in range(MLA_H):
                q = q_ref[:, h * HW:(h + 1) * HW]
                kb = k_ref[pl.ds(off, QB), h * HW:(h + 1) * HW]
                vb = v_ref[pl.ds(off, QB), h * MLA_DV:(h + 1) * MLA_DV]
                s = jnp.where(ok, _nt(q, kb) * scale, NEG)
                m_old = m_ref[h]
                m_new = jnp.maximum(m_old, jnp.max(s, axis=-1, keepdims=True))
                alpha = jnp.exp(m_old - m_new)
                p = jnp.exp(s - jnp.tile(m_new, (1, QB // LANES)))
                m_ref[h] = m_new
                l_ref[h] = alpha * l_ref[h] + jnp.sum(p, axis=-1, keepdims=True)
                acc_ref[h] = alpha * acc_ref[h] + _nn(p.astype(BF16), vb)
            return carry

        lax.fori_loop(0, qi + 1, step, 0)
        for h in range(MLA_H):
            hs = slice(h * MLA_DV, (h + 1) * MLA_DV)
            l = l_ref[h]
            o = acc_ref[h] / l
            o_ref[:, hs] = o
            z = mz_ref[:, hs].astype(F32)
            yb_ref[:, hs] = (o * (z * _sigmoid(z))).astype(BF16)
            lse = m_ref[h] + jnp.log(l)
            lsec_ref[0, h, pl.ds(qi, 1), :] = jnp.transpose(lse)[0:1, :]

    blocks = (_nbytes((QB, 2048), BF16) + _nbytes((Lp, 2048), BF16) + _nbytes((Lp, 1024), BF16)
              + 2 * _nbytes((QB, 1024), F32) + _nbytes((QB, 1024), BF16) + _nbytes((MLA_H, QB, LANES), F32)
              + _nbytes((MLA_H, NQ, QB), F32))
    return pl.pallas_call(
        body,
        out_shape=(jax.ShapeDtypeStruct((T, MLA_H * MLA_DV), F32), jax.ShapeDtypeStruct((T, MLA_H * MLA_DV), BF16),
                   jax.ShapeDtypeStruct((B, MLA_H, NQ, QB), F32)),
        grid=(B, NQ),
        in_specs=[
            pl.BlockSpec((QB, MLA_H * HW), lambda b, i: (b * NQ + i, 0)),
            pl.BlockSpec((Lp, MLA_H * HW), lambda b, i: (b, 0)),
            pl.BlockSpec((Lp, MLA_H * MLA_DV), lambda b, i: (b, 0)),
            pl.BlockSpec((QB, 1024), lambda b, i: (b * NQ + i, 2)),
        ],
        out_specs=(pl.BlockSpec((QB, 1024), lambda b, i: (b * NQ + i, 0)),
                   pl.BlockSpec((QB, 1024), lambda b, i: (b * NQ + i, 0)),
                   pl.BlockSpec((1, MLA_H, NQ, QB), lambda b, i: (b, 0, 0, 0))),
        scratch_shapes=[pltpu.VMEM((MLA_H, QB, LANES), F32), pltpu.VMEM((MLA_H, QB, LANES), F32),
                        pltpu.VMEM((MLA_H, QB, MLA_DV), F32)],
        compiler_params=_params(("parallel", "arbitrary"), blocks, 3 * _nbytes((MLA_H, QB, LANES), F32)),
        name="attn_fwd",
    )(q_att, k_att, v_att, projA)


def _out_proj_loss(x, meta, projA, ya, yb, w_out, gf, tgt, B, Lp):
    T = B * Lp
    NQ = Lp // QB

    def body(x_ref, meta_ref, gg_ref, gm_ref, ya_ref, yb_ref, w_ref, gf_ref, t_ref,
             dh_ref, dhb_ref, mg_ref, loss_ref, dgf_ref):
        b = pl.program_id(0)
        j = pl.program_id(1)

        @pl.when((b == 0) & (j == 0))
        def _():
            loss_ref[...] = jnp.zeros_like(loss_ref)
            dgf_ref[...] = jnp.zeros_like(dgf_ref)

        f32 = lambda ref: ref[...].astype(F32)
        merged = (_sigmoid(f32(gg_ref)) * f32(ya_ref) + _sigmoid(f32(gm_ref)) * f32(yb_ref)).astype(BF16)
        mg_ref[...] = merged
        h1 = _h_tile(j, x_ref, meta_ref) + _nn(merged, w_ref[...])
        r = lax.rsqrt(jnp.mean(h1 * h1, axis=-1, keepdims=True) + EPS)
        hn = h1 * r
        gfv = gf_ref[...]
        diff = jnp.where(j > 0, hn * gfv - t_ref[0], 0.0)
        loss_ref[...] += (0.5 / D) * jnp.sum(jnp.sum(diff * diff, axis=-1, keepdims=True), axis=0, keepdims=True)
        dout = diff * (1.0 / D)
        dgf_ref[...] += jnp.sum(dout * hn, axis=0, keepdims=True)
        dhn = dout * gfv
        dh = r * (dhn - hn * jnp.mean(dhn * hn, axis=-1, keepdims=True))
        dh_ref[...] = dh
        dhb_ref[...] = dh.astype(BF16)

    rows = lambda c: pl.BlockSpec((QB, D), lambda b, j: (b * NQ + j, c))
    const = lambda s: pl.BlockSpec(s, lambda b, j: (0, 0))
    return pl.pallas_call(
        body,
        out_shape=(jax.ShapeDtypeStruct((T, D), F32), jax.ShapeDtypeStruct((T, D), BF16),
                   jax.ShapeDtypeStruct((T, D), BF16), jax.ShapeDtypeStruct((1, 1), F32),
                   jax.ShapeDtypeStruct((1, D), F32)),
        grid=(B, NQ),
        in_specs=[_x_spec(), const((N_META, D)), rows(3), rows(4), rows(0), rows(0), const((D, D)),
                  const((1, D)), _x_spec()],
        out_specs=(rows(0), rows(0), rows(0), const((1, 1)), const((1, D))),
        compiler_params=_params(("arbitrary", "arbitrary"), 10 * _nbytes((QB, D), F32)),
        name="out_proj_loss",
    )(x, meta, projA, projA, ya, yb, w_out, gf, tgt)


def _merge_bwd(dh1_b, w_out, projA, ya, yb, tr):
    T = dh1_b.shape[0]

    def body(dh_ref, w_ref, gg_ref, gm_ref, ya_ref, yb_ref, dya_ref, dyb_ref, da_ref):
        d = _nt(dh_ref[...], w_ref[...])
        sg = _sigmoid(gg_ref[...].astype(F32))
        sm = _sigmoid(gm_ref[...].astype(F32))
        dya_ref[...] = (d * sg).astype(BF16)
        dyb_ref[...] = (d * sm).astype(BF16)
        da_ref[:, 0:D] = (d * ya_ref[...].astype(F32) * (sg * (1.0 - sg))).astype(BF16)
        da_ref[:, D:2 * D] = (d * yb_ref[...].astype(F32) * (sm * (1.0 - sm))).astype(BF16)

    spec = lambda c: pl.BlockSpec((tr, D), lambda i: (i, c))
    return pl.pallas_call(
        body,
        out_shape=(jax.ShapeDtypeStruct((T, D), BF16), jax.ShapeDtypeStruct((T, D), BF16),
                   jax.ShapeDtypeStruct((T, 2 * D), BF16)),
        grid=(T // tr,),
        in_specs=[spec(0), pl.BlockSpec((D, D), lambda i: (0, 0)), spec(3), spec(4), spec(0), spec(0)],
        out_specs=(spec(0), spec(0), pl.BlockSpec((tr, 2 * D), lambda i: (i, 0))),
        compiler_params=_params(("parallel",), 8 * _nbytes((tr, D), F32)),
        name="merge_bwd",
    )(dh1_b, w_out, projA, projA, ya, yb)


def _gla_out_bwd(dya, gla_proj, oa, projA, gn4, tr):
    T = dya.shape[0]
    nsteps = T // tr

    def body(dya_ref, w_ref, oa_ref, z_ref, gn_ref, do_ref, dz_ref, dgn_ref, acc_ref):
        i = pl.program_id(0)

        @pl.when(i == 0)
        def _():
            acc_ref[...] = jnp.zeros_like(acc_ref)

        dy_all = _nt(dya_ref[...], w_ref[...])
        for h in range(GLA_H):
            vs = slice(h * GLA_DV, (h + 1) * GLA_DV)
            dy = dy_all[:, vs]
            o = oa_ref[:, vs]
            z = z_ref[:, vs].astype(F32)
            gn = gn_ref[:, vs]
            s = _sigmoid(z)
            ra = lax.rsqrt(jnp.mean(o * o, axis=-1, keepdims=True) + EPS)
            on = o * ra
            don = dy * (z * s)
            t = don * gn
            do_ref[:, vs] = (ra * (t - on * jnp.mean(t * on, axis=-1, keepdims=True))).astype(BF16)
            dz_ref[:, vs] = (dy * (on * gn) * (s * (1.0 + z * (1.0 - s)))).astype(BF16)
            acc_ref[:, vs] += jnp.sum(don * on, axis=0, keepdims=True)

        @pl.when(i == nsteps - 1)
        def _():
            a = acc_ref[...]
            dgn_ref[...] = a[:, 0:256] + a[:, 256:512] + a[:, 512:768] + a[:, 768:1024]

    spec = lambda c: pl.BlockSpec((tr, D), lambda i: (i, c))
    return pl.pallas_call(
        body,
        out_shape=(jax.ShapeDtypeStruct((T, D), BF16), jax.ShapeDtypeStruct((T, D), BF16),
                   jax.ShapeDtypeStruct((1, GLA_DV), F32)),
        grid=(nsteps,),
        in_specs=[spec(0), pl.BlockSpec((D, D), lambda i: (0, 0)), spec(0), spec(1),
                  pl.BlockSpec((1, D), lambda i: (0, 0))],
        out_specs=(spec(0), spec(0), pl.BlockSpec((1, GLA_DV), lambda i: (0, 0))),
        scratch_shapes=[pltpu.VMEM((1, D), F32)],
        compiler_params=_params(("arbitrary",), 6 * _nbytes((tr, D), F32)),
        name="gla_out_bwd",
    )(dya, gla_proj, oa, projA, gn4)


def _gla_bwd(projA, projB, ssave, doa, wg, bg, B, Lp):
    T = B * Lp
    NC = Lp // GLA_C
    C = GLA_C
    scale = GLA_DK ** -0.5
    WC = 2304

    def body(q_ref, k_ref, v_ref, lr_ref, ss_ref, do_ref, wg_ref, bg_ref, dc_ref, dwg_ref, dbg_ref, dst_ref):
        i = pl.program_id(0)
        n = NC - 1 - i

        @pl.when(i == 0)
        def _():
            dst_ref[...] = jnp.zeros_like(dst_ref)
            dwg_ref[...] = jnp.zeros_like(dwg_ref)
            dbg_ref[...] = jnp.zeros_like(dbg_ref)

        pos = n * C + lax.broadcasted_iota(jnp.int32, (C, 1), 0)
        valid = pos >= FRONT
        lower, upper = _tri_masks()
        is_last = lax.broadcasted_iota(jnp.int32, (C, 1), 0) == C - 1
        for b in range(B):
            lr = lr_ref[b]
            pre, glog = _gla_gate(lr, wg_ref[...], bg_ref[...], valid)
            bcum = _cumsum_rows(glog, lower)
            db_parts = []
            for h in range(GLA_H):
                ks = slice(h * GLA_DK, (h + 1) * GLA_DK)
                vs = slice(h * GLA_DV, (h + 1) * GLA_DV)
                bh = bcum[:, ks]
                blast = jnp.sum(jnp.where(is_last, bh, 0.0), axis=0, keepdims=True)
                eb, enb, ekl, ebl = jnp.exp(bh), jnp.exp(-bh), jnp.exp(blast - bh), jnp.exp(blast)
                qh = q_ref[b, :, ks].astype(F32) * scale
                kh = k_ref[b, :, ks].astype(F32)
                qe_f, ke_f, kl_f = qh * eb, kh * enb, kh * ekl
                qe, ke, kl = qe_f.astype(BF16), ke_f.astype(BF16), kl_f.astype(BF16)
                vh = v_ref[b, :, vs].astype(BF16)
                doh = do_ref[b, :, vs]
                st = ss_ref[b, 0, h]
                dst = dst_ref[b, h]
                st_b, dst_b = st.astype(BF16), dst.astype(BF16)
                da = jnp.where(lower, _nt(doh, vh), 0.0).astype(BF16)
                da_t = jnp.where(upper, _nt(vh, doh), 0.0).astype(BF16)
                a_t = jnp.where(upper, _nt(ke, qe), 0.0).astype(BF16)
                dqe = _nn(da, ke) + _nn(doh, st_b)
                dke = _nn(da_t, qe)
                dvh = _nn(a_t, doh) + _nt(kl, dst_b)
                dkl = _nn(vh, dst_b)
                dst_ref[b, h] = dst * ebl + _tn(doh, qe)
                deb = jnp.sum(st * dst, axis=0, keepdims=True)
                db = dqe * qe_f - dke * ke_f - dkl * kl_f
                db_last = jnp.sum(dkl * kl_f, axis=0, keepdims=True) + deb * ebl
                db_parts.append(db + jnp.where(is_last, db_last, 0.0))
                dc_ref[b, :, vs] = dvh.astype(BF16)
                dc_ref[b, :, 1024 + h * GLA_DK:1024 + (h + 1) * GLA_DK] = (dqe * eb * scale).astype(BF16)
                dc_ref[b, :, 1536 + h * GLA_DK:1536 + (h + 1) * GLA_DK] = (dke * enb + dkl * ekl).astype(BF16)
            dglog = _cumsum_rows(jnp.concatenate(db_parts, axis=1), upper)
            dpre = jnp.where(valid, dglog * (1.0 / GLA_NORMALIZER) / (1.0 + jnp.exp(pre)), 0.0)
            dpre_b = dpre.astype(BF16)
            dc_ref[b, :, 2048:2176] = _nt(dpre_b, wg_ref[...]).astype(BF16)
            dc_ref[b, :, 2176:2304] = jnp.zeros((C, 128), BF16)
            dwg_ref[...] += _tn(lr.astype(BF16), dpre_b)
            dbg_ref[...] += jnp.sum(dpre, axis=0, keepdims=True)

    blocks = B * (_nbytes((C, 512), F32) * 2 + _nbytes((C, 1024), F32) + _nbytes((C, 1024), BF16)
                  + _nbytes((GLA_H, GLA_DV, GLA_DK), F32) + _nbytes((C, WC), BF16)) + 3 * _nbytes((128, 512), F32)
    state = _nbytes((B, GLA_H, GLA_DV, GLA_DK), F32)
    pa = projA.reshape(B, Lp, projA.shape[1])
    rev = lambda i: NC - 1 - i
    dc, dwg, dbg = pl.pallas_call(
        body,
        out_shape=(jax.ShapeDtypeStruct((B, Lp, WC), BF16), jax.ShapeDtypeStruct((128, GLA_KW), F32),
                   jax.ShapeDtypeStruct((1, GLA_KW), F32)),
        grid=(NC,),
        in_specs=[
            pl.BlockSpec((B, C, 512), lambda i: (0, rev(i), 10)),
            pl.BlockSpec((B, C, 512), lambda i: (0, rev(i), 11)),
            pl.BlockSpec((B, C, 1024), lambda i: (0, rev(i), 0)),
            pl.BlockSpec((B, C, 128), lambda i: (0, rev(i), 3)),
            pl.BlockSpec((B, 1, GLA_H, GLA_DV, GLA_DK), lambda i: (0, rev(i), 0, 0, 0)),
            pl.BlockSpec((B, C, 1024), lambda i: (0, rev(i), 0)),
            pl.BlockSpec((128, 512), lambda i: (0, 0)),
            pl.BlockSpec((1, 512), lambda i: (0, 0)),
        ],
        out_specs=(pl.BlockSpec((B, C, WC), lambda i: (0, rev(i), 0)),
                   pl.BlockSpec((128, GLA_KW), lambda i: (0, 0)),
                   pl.BlockSpec((1, GLA_KW), lambda i: (0, 0))),
        scratch_shapes=[pltpu.VMEM((B, GLA_H, GLA_DV, GLA_DK), F32)],
        compiler_params=_params(("arbitrary",), blocks, state),
        name="gla_bwd",
    )(pa, pa, pa, projB.reshape(B, Lp, projB.shape[1]), ssave, doa.reshape(B, Lp, GLA_VW), wg, bg)
    return dc.reshape(T, WC), dwg, dbg


def _attn_bwd_pre(dyb, mla_proj, projA, ob, B, Lp):
    T = B * Lp
    NQ = Lp // QB

    def body(dyb_ref, w_ref, z_ref, o_ref, do_ref, dz_ref, dcol_ref):
        j = pl.program_id(1)
        dy_all = _nt(dyb_ref[...], w_ref[...])
        for h in range(MLA_H):
            hs = slice(h * MLA_DV, (h + 1) * MLA_DV)
            dy = dy_all[:, hs]
            z = z_ref[:, hs].astype(F32)
            o = o_ref[:, hs]
            s = _sigmoid(z)
            do = dy * (z * s)
            do_ref[:, hs] = do.astype(BF16)
            dz_ref[:, hs] = (dy * o * (s * (1.0 + z * (1.0 - s)))).astype(BF16)
            dl = jnp.broadcast_to(jnp.sum(do * o, axis=-1, keepdims=True), (QB, LANES))
            dcol_ref[0, h, pl.ds(j, 1), :] = jnp.transpose(dl)[0:1, :]

    rows = lambda c: pl.BlockSpec((QB, D), lambda b, j: (b * NQ + j, c))
    return pl.pallas_call(
        body,
        out_shape=(jax.ShapeDtypeStruct((T, D), BF16), jax.ShapeDtypeStruct((T, D), BF16),
                   jax.ShapeDtypeStruct((B, MLA_H, NQ, QB), F32)),
        grid=(B, NQ),
        in_specs=[rows(0), pl.BlockSpec((D, D), lambda b, j: (0, 0)), rows(2), rows(0)],
        out_specs=(rows(0), rows(0), pl.BlockSpec((1, MLA_H, NQ, QB), lambda b, j: (b, 0, 0, 0))),
        compiler_params=_params(("parallel", "arbitrary"), 6 * _nbytes((QB, D), F32)),
        name="attn_bwd_pre",
    )(dyb, mla_proj, projA, ob)


ATTN_BWD_HEADS = 4


def _attn_bwd(q_att, k_att, v_att, do, lse_c, delta_c, B, Lp):
    T = B * Lp
    NQ = Lp // QB
    G = ATTN_BWD_HEADS
    NG = MLA_H // G
    HW = 2 * LANES
    scale = 1.0 / math.sqrt(MLA_QK)

    def body(q_ref, k_ref, v_ref, do_ref, lse_ref, dl_ref, dq_out, dk_out, dv_out, dq_ref, dk_ref, dv_ref):
        kj = pl.program_id(2)

        @pl.when(kj == 0)
        def _():
            dq_ref[...] = jnp.zeros_like(dq_ref)

        dk_ref[...] = jnp.zeros_like(dk_ref)
        dv_ref[...] = jnp.zeros_like(dv_ref)
        col = kj * QB + lax.broadcasted_iota(jnp.int32, (QB, QB), 0)
        rowi = lax.broadcasted_iota(jnp.int32, (QB, QB), 1)

        def step(qi, carry):
            off = pl.multiple_of(qi * QB, QB)
            ok = _attn_mask(qi * QB + rowi, col)
            for h in range(G):
                ws = slice(h * HW, (h + 1) * HW)
                hs = slice(h * MLA_DV, (h + 1) * MLA_DV)
                qb = q_ref[pl.ds(off, QB), ws]
                dob = do_ref[pl.ds(off, QB), hs]
                kb = k_ref[:, ws]
                lse = lse_ref[0, h, pl.ds(qi, 1), :]
                delta = dl_ref[0, h, pl.ds(qi, 1), :]
                s_t = _nt(kb, qb) * scale
                p_t = jnp.where(ok, jnp.exp(s_t - lse), 0.0)
                dv_ref[:, hs] += _nn(p_t.astype(BF16), dob)
                ds_t = (p_t * (_nt(v_ref[:, hs], dob) - delta) * scale).astype(BF16)
                dk_ref[:, ws] += _nn(ds_t, qb)
                dq_ref[pl.ds(off, QB), ws] += _tn(ds_t, kb)
            return carry

        lax.fori_loop(kj, NQ, step, 0)
        dk_out[...] = dk_ref[...].astype(BF16)
        dv_out[...] = dv_ref[...].astype(BF16)

        @pl.when(kj == NQ - 1)
        def _():
            dq_out[...] = dq_ref[...].astype(BF16)

    blocks = (2 * _nbytes((Lp, G * HW), BF16) + _nbytes((Lp, G * MLA_DV), BF16) + 2 * _nbytes((QB, G * 384), BF16)
              + 2 * _nbytes((G, NQ, QB), F32))
    scratch = [pltpu.VMEM((Lp, G * HW), F32), pltpu.VMEM((QB, G * HW), F32), pltpu.VMEM((QB, G * MLA_DV), F32)]
    return pl.pallas_call(
        body,
        out_shape=(jax.ShapeDtypeStruct((T, MLA_H * HW), BF16), jax.ShapeDtypeStruct((T, MLA_H * HW), BF16),
                   jax.ShapeDtypeStruct((T, MLA_H * MLA_DV), BF16)),
        scratch_shapes=scratch,
        grid=(B, NG, NQ),
        in_specs=[
            pl.BlockSpec((Lp, G * HW), lambda b, g, j: (b, g)),
            pl.BlockSpec((QB, G * HW), lambda b, g, j: (b * NQ + j, g)),
            pl.BlockSpec((QB, G * MLA_DV), lambda b, g, j: (b * NQ + j, g)),
            pl.BlockSpec((Lp, G * MLA_DV), lambda b, g, j: (b, g)),
            pl.BlockSpec((1, G, NQ, QB), lambda b, g, j: (b, g, 0, 0)),
            pl.BlockSpec((1, G, NQ, QB), lambda b, g, j: (b, g, 0, 0)),
        ],
        out_specs=(pl.BlockSpec((Lp, G * HW), lambda b, g, j: (b, g)),
                   pl.BlockSpec((QB, G * HW), lambda b, g, j: (b * NQ + j, g)),
                   pl.BlockSpec((QB, G * MLA_DV), lambda b, g, j: (b * NQ + j, g))),
        compiler_params=_params(("parallel", "parallel", "arbitrary"), blocks,
                                _nbytes((Lp, G * HW), F32) + _nbytes((QB, G * 384), F32)),
        name="attn_bwd",
    )(q_att, k_att, v_att, do, lse_c, delta_c)


def _mla_bwd_post(dq, dk, dv, projB, cos_t, sin_t, gq, gkv, wuq2, wukv, B, Lp, tr):
    T = B * Lp
    nt = Lp // tr
    HW = 2 * LANES

    def body(dq_ref, dk_ref, dv_ref, pb_ref, cos_ref, sin_ref, gq_ref, gkv_ref, wuq_ref, wukv_ref,
             dqf_ref, dkvf_ref, de_ref, dgq_ref, dgkv_ref):
        first = (pl.program_id(0) == 0) & (pl.program_id(1) == 0)

        @pl.when(first)
        def _():
            dgq_ref[...] = jnp.zeros_like(dgq_ref)
            dgkv_ref[...] = jnp.zeros_like(dgkv_ref)

        cs = cos_ref[...]
        sn = sin_ref[...]
        rope_t = lambda t: t * cs + _swap_halves(t * sn)
        dkr = jnp.zeros((tr, LANES), F32)
        for h in range(MLA_H):
            dqf_ref[:, h * HW:h * HW + LANES] = dq_ref[:, h * HW:h * HW + LANES]
            dq_rope = dq_ref[:, h * HW + LANES:(h + 1) * HW].astype(F32)
            dqf_ref[:, h * HW + LANES:(h + 1) * HW] = rope_t(dq_rope).astype(BF16)
            dkvf_ref[:, h * HW:h * HW + LANES] = dk_ref[:, h * HW:h * HW + LANES]
            dkvf_ref[:, h * HW + LANES:(h + 1) * HW] = dv_ref[:, h * MLA_DV:(h + 1) * MLA_DV]
            dkr = dkr + dk_ref[:, h * HW + LANES:(h + 1) * HW].astype(F32)

        def norm_bwd(x, dn, g):
            r = lax.rsqrt(jnp.mean(x * x, axis=-1, keepdims=True) + EPS)
            xn = x * r
            t = dn * g
            return r * (t - xn * jnp.mean(t * xn, axis=-1, keepdims=True)), jnp.sum(dn * xn, axis=0, keepdims=True)

        cq = pb_ref[:, 0:Q_RANK].astype(F32)
        ckv = pb_ref[:, Q_RANK:Q_RANK + KV_RANK].astype(F32)
        dcq, dgq = norm_bwd(cq, _nt(dqf_ref[...], wuq_ref[...]), gq_ref[...])
        dckv, dgkv = norm_bwd(ckv, _nt(dkvf_ref[...], wukv_ref[...]), gkv_ref[...])
        dgq_ref[...] += dgq
        dgkv_ref[...] += dgkv
        de_ref[:, 0:Q_RANK] = dcq.astype(BF16)
        de_ref[:, Q_RANK:Q_RANK + KV_RANK] = dckv.astype(BF16)
        de_ref[:, 384:512] = rope_t(dkr).astype(BF16)

    rows = lambda w: pl.BlockSpec((tr, w), lambda b, j: (b * nt + j, 0))
    const = lambda s: pl.BlockSpec(s, lambda b, j: (0, 0))
    blocks = (2 * _nbytes((tr, 2048), F32) + _nbytes((tr, 1024), F32) + _nbytes((tr, 640), F32)
              + 2 * _nbytes((tr, 2048), BF16) + _nbytes((2048, 384), BF16) + 2 * _nbytes((tr, 2048), F32))
    return pl.pallas_call(
        body,
        out_shape=(jax.ShapeDtypeStruct((T, 2048), BF16), jax.ShapeDtypeStruct((T, 2048), BF16),
                   jax.ShapeDtypeStruct((T, 512), BF16), jax.ShapeDtypeStruct((1, Q_RANK), F32),
                   jax.ShapeDtypeStruct((1, KV_RANK), F32)),
        grid=(B, nt),
        in_specs=[rows(2048), rows(2048), rows(1024), rows(640),
                  pl.BlockSpec((tr, 128), lambda b, j: (j, 0)), pl.BlockSpec((tr, 128), lambda b, j: (j, 0)),
                  const((1, Q_RANK)), const((1, KV_RANK)), const((Q_RANK, 2048)), const((KV_RANK, 2048))],
        out_specs=(rows(2048), rows(2048), rows(512), const((1, Q_RANK)), const((1, KV_RANK))),
        compiler_params=_params(("arbitrary", "arbitrary"), blocks),
        name="mla_bwd_post",
    )(dq, dk, dv, projB, cos_t, sin_t, gq, gkv, wuq2, wukv)


def _in_proj_bwd(x, meta, dh1, dA, dBz, dC, dDz, dE, wA, wB, g, B, Lp):
    NQ = Lp // QB
    seq = x.shape[1]

    def body(x_ref, meta_ref, dh_ref, da_ref, db_ref, dc_ref, dd_ref, de_ref, wa_ref, wb_ref, g_ref,
             gx_ref, dmeta_ref, dg_ref):
        b = pl.program_id(0)
        j = pl.program_id(1)

        @pl.when((b == 0) & (j == 0))
        def _():
            dg_ref[...] = jnp.zeros_like(dg_ref)

        du = _nt(da_ref[...], wa_ref[:, 3072:5120])
        du = du + _nt(db_ref[...], wa_ref[:, 1024:2048])
        du = du + _nt(dd_ref[...], wa_ref[:, 2048:3072])
        du = du + _nt(dc_ref[:, 0:1024], wa_ref[:, 0:1024])
        du = du + _nt(dc_ref[:, 1024:2048], wa_ref[:, 5120:6144])
        du = du + _nt(dc_ref[:, 2048:2176], wb_ref[:, 384:512])
        du = du + _nt(de_ref[:, 0:384], wb_ref[:, 0:384])
        du = du + _nt(de_ref[:, 384:512], wb_ref[:, 512:640])

        x = _h_tile(j, x_ref, meta_ref)
        r = lax.rsqrt(jnp.mean(x * x, axis=-1, keepdims=True) + EPS)
        xn = x * r
        t = du * g_ref[...]
        dh0 = dh_ref[...] + r * (t - xn * jnp.mean(t * xn, axis=-1, keepdims=True))
        dg_ref[...] += jnp.sum(du * xn, axis=0, keepdims=True)
        gx_ref[0] = dh0

        @pl.when((j == 0) & (b == 0))
        def _():
            dmeta_ref[...] = dh0[FRONT:HEAD_ROWS, :]

        @pl.when((j == 0) & (b > 0))
        def _():
            dmeta_ref[...] += dh0[FRONT:HEAD_ROWS, :]

    rows = lambda w: pl.BlockSpec((QB, w), lambda b, j: (b * NQ + j, 0))
    const = lambda s: pl.BlockSpec(s, lambda b, j: (0, 0))
    widths = [a.shape[1] for a in (dA, dBz, dC, dDz, dE)]
    blocks = (sum(_nbytes((QB, w), BF16) for w in widths) + _nbytes(wA.shape, BF16) + _nbytes(wB.shape, BF16)
              + 4 * _nbytes((QB, D), F32))
    return pl.pallas_call(
        body,
        out_shape=(jax.ShapeDtypeStruct((B, seq, D), F32), jax.ShapeDtypeStruct((N_META, D), F32),
                   jax.ShapeDtypeStruct((1, D), F32)),
        grid=(B, NQ),
        in_specs=[_x_spec(), const((N_META, D)), rows(D)] + [rows(w) for w in widths]
        + [const(wA.shape), const(wB.shape), const((1, D))],
        out_specs=(_x_spec(), const((N_META, D)), const((1, D))),
        compiler_params=_params(("arbitrary", "arbitrary"), blocks),
        name="in_proj_bwd",
    )(x, meta, dh1, dA, dBz, dC, dDz, dE, wA, wB, g)


_VMEM_WHOLE = pl.BlockSpec(memory_space=pltpu.VMEM)


def _params_whole(arrays):
    total = sum(_nbytes(a.shape, a.dtype) for a in arrays)
    return pltpu.CompilerParams(vmem_limit_bytes=int(min(total + 12 * 1024 * 1024, VMEM_CAP_V7X)))


def _wire_dtype(shape):
    return BF16 if shape[-2] * shape[-1] >= WIRE_BF16_MIN_ELEMS else F32


def _pair_add_big(gp, recv, c):
    _, half, cols = recv.shape
    th = _div_tile(half, 64, 16)
    out_dtype = _wire_dtype(recv.shape)

    def body(c_ref, a_ref, b_ref, o_ref):
        o_ref[...] = (a_ref[:, 0] + b_ref[...]).astype(out_dtype)

    return pl.pallas_call(
        body,
        out_shape=jax.ShapeDtypeStruct(recv.shape, out_dtype),
        grid_spec=pltpu.PrefetchScalarGridSpec(
            num_scalar_prefetch=1,
            grid=(half // th,),
            in_specs=[pl.BlockSpec((4, 1, th, cols), lambda i, c_ref: (0, c_ref[0], i, 0)),
                      pl.BlockSpec((4, th, cols), lambda i, c_ref: (0, i, 0))],
            out_specs=pl.BlockSpec((4, th, cols), lambda i, c_ref: (0, i, 0)),
        ),
        compiler_params=_params(("parallel",), 3 * _nbytes((4, th, cols), F32)),
        name="grad_pair_add_big",
    )(c, gp.reshape(4, 2, half, cols), recv)


def _pair_add_small(gps, recvs):
    n = len(gps)

    def body(*refs):
        c = lax.axis_index("c")
        for t in range(n):
            g_ref, r_ref, o_ref = refs[t], refs[n + t], refs[2 * n + t]
            half = r_ref.shape[1]
            s = g_ref[:, pl.ds(pl.multiple_of(c * half, 8), half), :] + r_ref[...]
            o_ref[...] = s.astype(o_ref.dtype)

    return pl.pallas_call(
        body,
        out_shape=[jax.ShapeDtypeStruct(r.shape, _wire_dtype(r.shape)) for r in recvs],
        in_specs=[_VMEM_WHOLE] * (2 * n),
        out_specs=[_VMEM_WHOLE] * n,
        compiler_params=_params_whole(list(gps) + 2 * list(recvs)),
        name="grad_pair_add_small",
    )(*gps, *recvs)


def _chip_order_sum(landed_ref, own_ref, me):
    p = [jnp.where(me == k, own_ref[k], landed_ref[k]).astype(F32) for k in range(4)]
    return ((p[0] + p[1]) + p[2]) + p[3]


def _sum_chips_big(landed, own, pos):
    _, half, cols = landed.shape
    th = _div_tile(half, 64, 16)

    def body(pos_ref, l_ref, s_ref, o_ref):
        o_ref[0] = _chip_order_sum(l_ref, s_ref, pos_ref[1])

    spec = pl.BlockSpec((4, th, cols), lambda i, pos_ref: (0, i, 0))
    return pl.pallas_call(
        body,
        out_shape=jax.ShapeDtypeStruct((2, half, cols), F32),
        grid_spec=pltpu.PrefetchScalarGridSpec(
            num_scalar_prefetch=1,
            grid=(half // th,),
            in_specs=[spec, spec],
            out_specs=pl.BlockSpec((1, th, cols), lambda i, pos_ref: (pos_ref[0], i, 0)),
        ),
        compiler_params=_params(("parallel",), 3 * _nbytes((4, th, cols), F32)),
        name="grad_sum_chips_big",
    )(pos, landed, own)


def _sum_chips_small(landed, own):
    n = len(landed)

    def body(*refs):
        x, y, c = _mesh_pos()
        for t in range(n):
            refs[2 * n + t][c] = _chip_order_sum(refs[t], refs[n + t], 2 * x + y)

    return pl.pallas_call(
        body,
        out_shape=[jax.ShapeDtypeStruct((2,) + p.shape[1:], F32) for p in landed],
        in_specs=[_VMEM_WHOLE] * (2 * n),
        out_specs=[_VMEM_WHOLE] * n,
        compiler_params=_params_whole(list(landed) * 3),
        name="grad_sum_chips_small",
    )(*landed, *own)


def _adamw_update(w_ref, g_ref, m_ref, v_ref, d_ref, mo_ref, vo_ref):
    c1 = 1.0 - ADAM_B1 ** ADAM_STEP
    c2 = 1.0 - ADAM_B2 ** ADAM_STEP
    gv = g_ref[...]
    mn = ADAM_B1 * m_ref[...] + (1.0 - ADAM_B1) * gv
    vn = ADAM_B2 * v_ref[...] + (1.0 - ADAM_B2) * (gv * gv)
    mo_ref[...] = mn
    vo_ref[...] = vn
    d_ref[...] = -ADAM_LR * ((mn / c1) / (jnp.sqrt(vn / c2) + ADAM_EPS) + ADAM_WD * w_ref[...])


def _adamw_big(w, g, m, v):
    lead, (rows, cols) = w.shape[:-2], w.shape[-2:]
    assert all(n == 1 for n in lead)
    tr = _div_tile(rows, 128, 8)
    spec = pl.BlockSpec((1,) * len(lead) + (tr, cols), lambda i: (0,) * len(lead) + (i, 0))
    shp = jax.ShapeDtypeStruct(w.shape, F32)
    return pl.pallas_call(
        functools.partial(_adamw_update),
        out_shape=(shp, shp, shp),
        grid=(rows // tr,),
        in_specs=[spec] * 4,
        out_specs=(spec, spec, spec),
        compiler_params=_params(("parallel",), 7 * _nbytes((tr, cols), F32)),
        name="adamw_big",
    )(w, g, m, v)


def _adamw_small(ws, gs, ms, vs):
    n = len(ws)

    def body(*refs):
        for t in range(n):
            _adamw_update(refs[t], refs[n + t], refs[2 * n + t], refs[3 * n + t],
                          refs[4 * n + t], refs[5 * n + t], refs[6 * n + t])

    shapes = [jax.ShapeDtypeStruct(w.shape, F32) for w in ws]
    return pl.pallas_call(
        body,
        out_shape=shapes * 3,
        in_specs=[_VMEM_WHOLE] * (4 * n),
        out_specs=[_VMEM_WHOLE] * (3 * n),
        compiler_params=_params_whole(list(ws) * 7),
        name="adamw_small",
    )(*ws, *gs, *ms, *vs)


def _mesh_pos():
    return lax.axis_index("x"), lax.axis_index("y"), lax.axis_index("c")


def _other_chips(x, y):
    return [(1 - x, y), (x, 1 - y), (1 - x, 1 - y)]


_ANY = pl.BlockSpec(memory_space=pl.ANY)


PAIR_SPLIT_MIN_ROWS = 64


def _weight_gather(shards):
    n = len(shards)
    split = [s.shape[0] >= PAIR_SPLIT_MIN_ROWS for s in shards]

    def body(*refs):
        w_refs, o_refs = refs[:n], refs[n:2 * n]
        send_sems, recv_sems = refs[2 * n:]
        x, y, c = _mesh_pos()
        me = 2 * x + y
        chips = _other_chips(x, y)

        def rows_of(t, core):
            rows = shards[t].shape[0]
            if not split[t]:
                return pl.ds(0, rows)
            return pl.ds(pl.multiple_of(core * (rows // 2), 16), rows // 2)

        def landed(t, k, slot, rows, to):
            ref = o_refs[t].at[slot, rows]
            return pltpu.make_async_remote_copy(src_ref=ref, dst_ref=ref, send_sem=send_sems.at[6 * t + k],
                                                recv_sem=recv_sems.at[6 * t + k], device_id=to, device_id_type=MESH)

        sends = []
        for t in range(n):
            mine = rows_of(t, c)
            for k, (px, py) in enumerate(chips):
                cp = pltpu.make_async_remote_copy(src_ref=w_refs[t].at[mine], dst_ref=o_refs[t].at[me, mine],
                                                  send_sem=send_sems.at[6 * t + k], recv_sem=recv_sems.at[6 * t + k],
                                                  device_id=(px, py, c), device_id_type=MESH)
                cp.start()
                sends.append(cp)
        for t in range(n):
            mine = rows_of(t, c)
            for k, (px, py) in enumerate(chips):
                landed(t, k, 2 * px + py, mine, (x, y, c)).wait_recv()
                if split[t]:
                    cp = landed(t, 3 + k, 2 * px + py, mine, (x, y, 1 - c))
                    cp.start()
                    sends.append(cp)
        for t in range(n):
            if split[t]:
                for k, (px, py) in enumerate(chips):
                    landed(t, 3 + k, 2 * px + py, rows_of(t, 1 - c), (x, y, c)).wait_recv()
        for cp in sends:
            cp.wait_send()

    return pl.pallas_call(
        body,
        out_shape=[jax.ShapeDtypeStruct((4,) + s.shape, s.dtype) for s in shards],
        in_specs=[_ANY] * n,
        out_specs=[_ANY] * n,
        scratch_shapes=[pltpu.SemaphoreType.DMA((6 * n,)), pltpu.SemaphoreType.DMA((6 * n,))],
        name="weight_gather",
    )(*shards)


def _pair_swap(gps):
    n = len(gps)

    def body(*refs):
        g_refs, o_refs = refs[:n], refs[n:2 * n]
        send_sems, recv_sems = refs[2 * n:]
        x, y, c = _mesh_pos()
        copies = []
        for t in range(n):
            half = gps[t].shape[1] // 2
            theirs = pl.ds(pl.multiple_of((1 - c) * half, 8), half)
            cp = pltpu.make_async_remote_copy(src_ref=g_refs[t].at[:, theirs], dst_ref=o_refs[t],
                                              send_sem=send_sems.at[t], recv_sem=recv_sems.at[t],
                                              device_id=(x, y, 1 - c), device_id_type=MESH)
            cp.start()
            copies.append(cp)
        for cp in copies:
            cp.wait_send()
            cp.wait_recv()

    return pl.pallas_call(
        body,
        out_shape=[jax.ShapeDtypeStruct((4, g.shape[1] // 2, g.shape[2]), g.dtype) for g in gps],
        in_specs=[_ANY] * n,
        out_specs=[_ANY] * n,
        scratch_shapes=[pltpu.SemaphoreType.DMA((n,)), pltpu.SemaphoreType.DMA((n,))],
        name="grad_pair_swap",
    )(*gps)


def _chip_scatter(parts):
    n = len(parts)

    def body(*refs):
        s_refs, o_refs = refs[:n], refs[n:2 * n]
        send_sems, recv_sems = refs[2 * n:]
        x, y, c = _mesh_pos()
        me = 2 * x + y
        chips = _other_chips(x, y)
        sends = []
        for t in range(n):
            for k, (px, py) in enumerate(chips):
                cp = pltpu.make_async_remote_copy(src_ref=s_refs[t].at[2 * px + py], dst_ref=o_refs[t].at[me],
                                                  send_sem=send_sems.at[3 * t + k], recv_sem=recv_sems.at[3 * t + k],
                                                  device_id=(px, py, c), device_id_type=MESH)
                cp.start()
                sends.append(cp)
        for t in range(n):
            for k, (px, py) in enumerate(chips):
                pltpu.make_async_remote_copy(src_ref=s_refs[t].at[me], dst_ref=o_refs[t].at[2 * px + py],
                                             send_sem=send_sems.at[3 * t + k], recv_sem=recv_sems.at[3 * t + k],
                                             device_id=(x, y, c), device_id_type=MESH).wait_recv()
        for cp in sends:
            cp.wait_send()

    return pl.pallas_call(
        body,
        out_shape=[jax.ShapeDtypeStruct(p.shape, p.dtype) for p in parts],
        in_specs=[_ANY] * n,
        out_specs=[_ANY] * n,
        scratch_shapes=[pltpu.SemaphoreType.DMA((3 * n,)), pltpu.SemaphoreType.DMA((3 * n,))],
        name="grad_chip_scatter",
    )(*parts)


def _pair_join(fs):
    n = len(fs)

    def body(*refs):
        f_refs, o_refs = refs[:n], refs[n:2 * n]
        send_sems, recv_sems = refs[2 * n:]
        x, y, c = _mesh_pos()
        sends = []
        for t in range(n):
            cp = pltpu.make_async_remote_copy(src_ref=f_refs[t].at[c], dst_ref=o_refs[t].at[c], send_sem=send_sems.at[t],
                                              recv_sem=recv_sems.at[t], device_id=(x, y, 1 - c), device_id_type=MESH)
            cp.start()
            sends.append(cp)
        for t in range(n):
            pltpu.make_async_remote_copy(src_ref=f_refs[t].at[c], dst_ref=o_refs[t].at[1 - c], send_sem=send_sems.at[t],
                                         recv_sem=recv_sems.at[t], device_id=(x, y, c), device_id_type=MESH).wait_recv()
        for cp in sends:
            cp.wait_send()

    return pl.pallas_call(
        body,
        out_shape=[jax.ShapeDtypeStruct(f.shape, f.dtype) for f in fs],
        in_specs=[_ANY] * n,
        out_specs=[_ANY] * n,
        input_output_aliases={t: t for t in range(n)},
        scratch_shapes=[pltpu.SemaphoreType.DMA((n,)), pltpu.SemaphoreType.DMA((n,))],
        name="grad_pair_join",
    )(*fs)


def _rope_tables(Lp):
    inv = 1.0 / (ROPE_BASE ** (jnp.arange(0, ROPE, 2, dtype=F32) / ROPE))
    ang = (jnp.arange(Lp, dtype=F32) - FRONT)[:, None] * inv[None, :]
    cs, sn = jnp.cos(ang), jnp.sin(ang)
    return jnp.tile(cs, (1, 4)), jnp.concatenate([-sn, sn, -sn, sn], axis=1)


def _local_step(x, loss_target, meta, norm_g, w_in, gate_w, gate_b, gla_norm_g, gla_proj, q_norm_g, w_uq,
                kv_norm_g, w_ukv, mla_proj, w_out, final_norm_g):
    B, seq, _ = x.shape
    Lp = HEAD_ROWS + seq
    T = B * Lp
    tr = _div_tile(Lp, 544, 16)
    tq = _div_tile(T, 1024, QB)

    cuts = np.cumsum((0,) + SPLITS)
    shard_w = IN_WIDTH // 4

    def w_cols(i, width=None):
        parts = []
        for j in range(4):
            a, b = max(cuts[i], j * shard_w), min(cuts[i + 1], (j + 1) * shard_w)
            if a < b:
                parts.append(w_in[j][:, a - j * shard_w:b - j * shard_w])
        if width is not None:
            parts.append(jnp.zeros((D, width - (cuts[i + 1] - cuts[i])), w_in.dtype))
        return parts

    i_q, i_k, i_v, i_lr, i_z, i_cq, i_ckv, i_kr, i_mz, i_gg, i_gm = range(11)
    wA = jnp.concatenate(sum([w_cols(i) for i in (i_v, i_z, i_mz, i_gg, i_gm, i_q, i_k)], []), axis=1)
    wB = jnp.concatenate(w_cols(i_cq) + w_cols(i_ckv) + w_cols(i_lr, 128) + w_cols(i_kr, 128), axis=1)
    wg = jnp.pad(gate_w, ((0, 128 - GLA_RANK), (0, 0)))
    wuq2 = jnp.pad(w_uq.reshape(Q_RANK, MLA_H, MLA_QK), ((0, 0), (0, 0), (0, 256 - MLA_QK))).reshape(Q_RANK, 2048)
    gn4 = jnp.tile(gla_norm_g, (1, GLA_H))
    cos_t, sin_t = _rope_tables(Lp)

    u = _rms_in(x, meta, norm_g, B, Lp)
    projA = _mm(u, wA, name="in_proj_a", out_dtype=BF16, tm=tq, tn=1024, tk=D)
    projB = _mm(u, wB, name="in_proj_b", out_dtype=BF16, tm=tq, tn=640, tk=D)
    oa, ya_in, ssave = _gla_fwd(projA, projB, wg, gate_b, gn4, B, Lp)
    ya = _mm(ya_in, gla_proj, name="gla_proj", out_dtype=BF16, tm=tq, tn=D, tk=D)
    q_att, k_att, v_att, cqn, ckvn = _mla_prep(projB, cos_t, sin_t, q_norm_g, kv_norm_g, wuq2, w_ukv, B, Lp, tr)
    ob, yb_in, lse_c = _attn_fwd(q_att, k_att, v_att, projA, B, Lp)
    yb = _mm(yb_in, mla_proj, name="mla_proj", out_dtype=BF16, tm=tq, tn=D, tk=D)
    dh1, dh1_b, merged, loss, d_gf = _out_proj_loss(x, meta, projA, ya, yb, w_out, final_norm_g.reshape(1, D),
                                                     loss_target, B, Lp)

    g_w_out = _mm(merged, dh1_b, name="dw_out", trans_a=True, tm=D, tn=D, tk=tq)
    dya, dyb, dA = _merge_bwd(dh1_b, w_out, projA, ya, yb, tr)
    g_gla_proj = _mm(ya_in, dya, name="dw_gla_proj", trans_a=True, tm=D, tn=D, tk=tq)
    g_mla_proj = _mm(yb_in, dyb, name="dw_mla_proj", trans_a=True, tm=D, tn=D, tk=tq)
    doa, dBz, d_gn = _gla_out_bwd(dya, gla_proj, oa, projA, gn4, tr)
    dC, g_wg, d_bg = _gla_bwd(projA, projB, ssave, doa, wg, gate_b, B, Lp)
    do, dDz, delta_c = _attn_bwd_pre(dyb, mla_proj, projA, ob, B, Lp)
    dq, dk, dv = _attn_bwd(q_att, k_att, v_att, do, lse_c, delta_c, B, Lp)
    dqf, dkvf, dE, d_gq, d_gkv = _mla_bwd_post(dq, dk, dv, projB, cos_t, sin_t, q_norm_g, kv_norm_g,
                                                wuq2, w_ukv, B, Lp, tr)
    g_wuq2 = _mm(cqn, dqf, name="dw_uq", trans_a=True, tm=Q_RANK, tn=2048, tk=tq)
    g_wukv = _mm(ckvn, dkvf, name="dw_ukv", trans_a=True, tm=KV_RANK, tn=2048, tk=tq)
    dparts = [dA, dBz, dC, dDz, dE]
    g_in = [_mm(u, dp, name="dw_in_%d" % i, trans_a=True, tm=D, tn=_div_tile(dp.shape[1], 1024, 256), tk=tq)
            for i, dp in enumerate(dparts)]
    grad_x, d_meta, d_ng = _in_proj_bwd(x, meta, dh1, dA, dBz, dC, dDz, dE, wA, wB, norm_g, B, Lp)

    gA, gBz, gC, gDz, gE = g_in
    src = [(gC, 1024), (gC, 1536), (gC, 0), (gC, 2048), (gBz, 0), (gE, 0), (gE, Q_RANK), (gE, 384), (gDz, 0),
           (gA, 0), (gA, D)]
    owners = []
    for j in range(4):
        parts = []
        for i, (arr, off) in enumerate(src):
            a, b = max(cuts[i], j * shard_w), min(cuts[i + 1], (j + 1) * shard_w)
            if a < b:
                parts.append(arr[:, off + a - cuts[i]:off + b - cuts[i]])
        owners.append(jnp.concatenate(parts, axis=1))
    g_w_in = jnp.stack(owners)
    g_wuq = g_wuq2.reshape(Q_RANK, MLA_H, 256)[:, :, :MLA_QK].reshape(Q_RANK, MLA_H * MLA_QK)
    grads = dict(w_in=g_w_in, gla_gate_w=g_wg[:GLA_RANK], gla_proj=g_gla_proj, mla_w_uq=g_wuq, mla_w_ukv=g_wukv,
                 mla_proj=g_mla_proj, w_out=g_w_out, meta_tokens=d_meta, norm_g=d_ng, gla_gate_b=d_bg,
                 gla_norm_g=d_gn, mla_q_norm_g=d_gq, mla_kv_norm_g=d_gkv, final_norm_g=d_gf)
    return loss[0, 0], grad_x, grads


_MATS = ("w_in", "gla_gate_w", "gla_proj", "mla_w_uq", "mla_w_ukv", "mla_proj", "w_out")
_ROW_SHARDED = ("gla_proj", "mla_proj", "w_out")
_ORDER = ("meta_tokens", "norm_g", "w_in", "gla_gate_w", "gla_gate_b", "gla_norm_g", "gla_proj", "mla_q_norm_g",
          "mla_w_uq", "mla_kv_norm_g", "mla_w_ukv", "mla_proj", "w_out", "final_norm_g")
WIRE_BF16_MIN_ELEMS = 128 * 128
SMALL_PACK_ROWS = 16


def _pack_small(d):
    rows = [jnp.pad(d[n].reshape(1, size), ((0, 0), (0, D - size))) for n, size in SMALL]
    return jnp.pad(jnp.concatenate(rows, axis=0), ((0, SMALL_PACK_ROWS - len(rows)), (0, 0)))


def _unpack_small(packed):
    return {n: packed[i, :size] for i, (n, size) in enumerate(SMALL)}


def kernel(x, meta_tokens, norm_g, w_in, gla_gate_w, gla_gate_b, gla_norm_g, gla_proj, mla_q_norm_g, mla_w_uq, mla_kv_norm_g, mla_w_ukv, mla_proj, w_out, final_norm_g, loss_target, m_meta_tokens, m_norm_g, m_w_in, m_gla_gate_w, m_gla_gate_b, m_gla_norm_g, m_gla_proj, m_mla_q_norm_g, m_mla_w_uq, m_mla_kv_norm_g, m_mla_w_ukv, m_mla_proj, m_w_out, m_final_norm_g, v_meta_tokens, v_norm_g, v_w_in, v_gla_gate_w, v_gla_gate_b, v_gla_norm_g, v_gla_proj, v_mla_q_norm_g, v_mla_w_uq, v_mla_kv_norm_g, v_mla_w_ukv, v_mla_proj, v_w_out, v_final_norm_g):
    w = dict(meta_tokens=meta_tokens, norm_g=norm_g, w_in=w_in[0], gla_gate_w=gla_gate_w[0], gla_gate_b=gla_gate_b,
             gla_norm_g=gla_norm_g, gla_proj=gla_proj[0], mla_q_norm_g=mla_q_norm_g, mla_w_uq=mla_w_uq[0],
             mla_kv_norm_g=mla_kv_norm_g, mla_w_ukv=mla_w_ukv[0], mla_proj=mla_proj[0], w_out=w_out[0],
             final_norm_g=final_norm_g)
    mom = dict(meta_tokens=m_meta_tokens, norm_g=m_norm_g, w_in=m_w_in[0], gla_gate_w=m_gla_gate_w[0],
               gla_gate_b=m_gla_gate_b, gla_norm_g=m_gla_norm_g, gla_proj=m_gla_proj[0], mla_q_norm_g=m_mla_q_norm_g,
               mla_w_uq=m_mla_w_uq[0], mla_kv_norm_g=m_mla_kv_norm_g, mla_w_ukv=m_mla_w_ukv[0], mla_proj=m_mla_proj[0],
               w_out=m_w_out[0], final_norm_g=m_final_norm_g)
    var = dict(meta_tokens=v_meta_tokens, norm_g=v_norm_g, w_in=v_w_in[0], gla_gate_w=v_gla_gate_w[0],
               gla_gate_b=v_gla_gate_b, gla_norm_g=v_gla_norm_g, gla_proj=v_gla_proj[0], mla_q_norm_g=v_mla_q_norm_g,
               mla_w_uq=v_mla_w_uq[0], mla_kv_norm_g=v_mla_kv_norm_g, mla_w_ukv=v_mla_w_ukv[0], mla_proj=v_mla_proj[0],
               w_out=v_w_out[0], final_norm_g=v_final_norm_g)
    out_shapes = {n: a.shape for n, a in zip(_ORDER, (meta_tokens, norm_g, w_in, gla_gate_w, gla_gate_b, gla_norm_g,
                                                     gla_proj, mla_q_norm_g, mla_w_uq, mla_kv_norm_g, mla_w_ukv,
                                                     mla_proj, w_out, final_norm_g))}

    me = (2 * lax.axis_index("x") + lax.axis_index("y")).astype(jnp.int32)
    shards = [w[n].astype(BF16) for n in _MATS] + [meta_tokens]
    gathered = [lax.dynamic_update_slice(gth, own[None], (me, 0, 0))
                for gth, own in zip(_weight_gather(shards), shards)]
    full = {}
    for name, gth in zip(_MATS, gathered):
        if name == "w_in":
            full[name] = gth
        elif name in _ROW_SHARDED:
            full[name] = gth.reshape(4 * gth.shape[1], gth.shape[2])
        else:
            full[name] = gth.transpose(1, 0, 2).reshape(gth.shape[1], 4 * gth.shape[2])
    meta_full = gathered[-1].transpose(1, 0, 2).reshape(N_META, D)

    loss_local, grad_x, g = _local_step(
        x, loss_target, meta_full, norm_g, full["w_in"], full["gla_gate_w"], gla_gate_b, gla_norm_g, full["gla_proj"],
        mla_q_norm_g, full["mla_w_uq"], mla_kv_norm_g, full["mla_w_ukv"], full["mla_proj"], full["w_out"], final_norm_g)
    loss = lax.psum(loss_local, ("x", "y", "c"))

    def by_owner(name, arr):
        if name == "w_in":
            return arr
        if name in _ROW_SHARDED:
            return arr.reshape(4, arr.shape[0] // 4, arr.shape[1])
        return arr.reshape(arr.shape[0], 4, arr.shape[1] // 4).transpose(1, 0, 2)

    names = _MATS + ("meta_tokens",)
    gps = [by_owner(n, g[n]) for n in names] + [jnp.broadcast_to(_pack_small(g)[None], (4, SMALL_PACK_ROWS, D))]
    recvs = _pair_swap(gps)
    c_idx = lax.axis_index("c").astype(jnp.int32).reshape(1)
    s1 = [_pair_add_big(gps[0], recvs[0], c_idx)] + list(_pair_add_small(gps[1:], recvs[1:]))
    landed = _chip_scatter(s1)
    pos = jnp.stack([c_idx[0], me])
    halves = [_sum_chips_big(landed[0], s1[0], pos)] + list(_sum_chips_small(landed[1:], s1[1:]))
    shapes = [out_shapes[n] for n in names] + [(SMALL_PACK_ROWS, D)]
    g_red = [j.reshape(s) for j, s in zip(_pair_join(halves), shapes)]

    tens = lambda d: [d[n].reshape(out_shapes[n]) for n in names] + [_pack_small(d)]
    w_t, m_t, v_t = tens(w), tens(mom), tens(var)
    big = _adamw_big(w_t[0], g_red[0], m_t[0], v_t[0])
    rest = _adamw_small(w_t[1:], g_red[1:], m_t[1:], v_t[1:])
    k = len(names)
    results = {"grad": g_red}
    for i, kind in enumerate(("delta", "new_m", "new_v")):
        results[kind] = [big[i]] + list(rest[i * k:(i + 1) * k])

    outs = []
    for kind in ("grad", "delta", "new_m", "new_v"):
        vals = dict(zip(names, results[kind][:-1]))
        vals.update(_unpack_small(results[kind][-1]))
        outs += [vals[n].reshape(out_shapes[n]) for n in _ORDER]
    return (loss, grad_x, *outs)
```

```python
import functools
import math

import jax
import jax.numpy as jnp
import numpy as np
from jax import lax
from jax.experimental import pallas as pl
from jax.experimental.pallas import tpu as pltpu

F32 = jnp.float32
BF16 = jnp.bfloat16

D = 1024
N_META = 16
QB = 256
FRONT = QB - N_META
HEAD_ROWS = FRONT + N_META
assert FRONT % 64 == 48
EPS = 1e-6

GLA_H, GLA_DK, GLA_DV, GLA_RANK, GLA_C = 4, 128, 256, 16, 64
GLA_NORMALIZER = 16.0
GLA_KW, GLA_VW = GLA_H * GLA_DK, GLA_H * GLA_DV
MLA_H, NOPE, ROPE, MLA_DV, Q_RANK, KV_RANK = 8, 128, 64, 128, 256, 128
MLA_QK = NOPE + ROPE
ROPE_BASE = 10000.0
SPLITS = (GLA_KW, GLA_KW, GLA_VW, GLA_RANK, GLA_VW, Q_RANK, KV_RANK, ROPE, MLA_H * MLA_DV, D, D)
IN_WIDTH = sum(SPLITS)

ADAM_LR, ADAM_B1, ADAM_B2, ADAM_EPS, ADAM_WD, ADAM_STEP = 0.001, 0.9, 0.999, 1e-08, 0.01, 10

LANES = 128
VMEM_CAP_V7X = 56 * 1024 * 1024
MESH = pl.DeviceIdType.MESH
NEG = -1e30

SMALL = (("norm_g", D), ("gla_gate_b", GLA_KW), ("gla_norm_g", GLA_DV), ("mla_q_norm_g", Q_RANK),
         ("mla_kv_norm_g", KV_RANK), ("final_norm_g", D))


def _div_tile(n, target, mult):
    best = None
    for d in range(mult, min(n, target) + 1, mult):
        if n % d == 0:
            best = d
    assert best is not None, (n, target, mult)
    return best


def _params(sem, block_bytes, scratch_bytes=0):
    est = 2 * block_bytes + scratch_bytes + 12 * 1024 * 1024
    return pltpu.CompilerParams(dimension_semantics=sem, vmem_limit_bytes=int(min(max(est, 24 * 1024 * 1024), VMEM_CAP_V7X)))


def _nbytes(shape, dtype):
    return int(np.prod(shape)) * jnp.dtype(dtype).itemsize


def _sigmoid(x):
    return 1.0 / (1.0 + jnp.exp(-x))


def _nt(a, b):
    return lax.dot_general(a, b, (((1,), (1,)), ((), ())), preferred_element_type=F32)


def _tn(a, b):
    return lax.dot_general(a, b, (((0,), (0,)), ((), ())), preferred_element_type=F32)


def _nn(a, b):
    return jnp.dot(a, b, preferred_element_type=F32)


def _split3(x):
    a = x.astype(BF16)
    r = x - a.astype(F32)
    b = r.astype(BF16)
    c = (r - b.astype(F32)).astype(BF16)
    return a, b, c


def _mm(a, b, *, name, trans_a=False, trans_b=False, out_dtype=F32, tm, tn, tk):
    assert not (trans_a and trans_b)
    if trans_a:
        K, M = a.shape
    else:
        M, K = a.shape
    N = b.shape[0] if trans_b else b.shape[1]
    assert (b.shape[1] if trans_b else b.shape[0]) == K
    assert M % tm == 0 and N % tn == 0 and K % tk == 0, (name, M, N, K, tm, tn, tk)
    nk = K // tk

    def body(a_ref, b_ref, o_ref, *scratch):
        av = a_ref[...].astype(BF16)
        bv = b_ref[...].astype(BF16)
        prod = _tn(av, bv) if trans_a else (_nt(av, bv) if trans_b else _nn(av, bv))
        if nk == 1:
            o_ref[...] = prod.astype(out_dtype)
        else:
            acc = scratch[0]
            k = pl.program_id(2)

            @pl.when(k == 0)
            def _():
                acc[...] = prod

            @pl.when(k > 0)
            def _():
                acc[...] += prod

            @pl.when(k == nk - 1)
            def _():
                o_ref[...] = acc[...].astype(out_dtype)

    if trans_a:
        a_spec = pl.BlockSpec((tk, tm), lambda i, j, k: (k, i))
    else:
        a_spec = pl.BlockSpec((tm, tk), lambda i, j, k: (i, k))
    if trans_b:
        b_spec = pl.BlockSpec((tn, tk), lambda i, j, k: (j, k))
    else:
        b_spec = pl.BlockSpec((tk, tn), lambda i, j, k: (k, j))
    blocks = (_nbytes((tm, tk), a.dtype) + _nbytes((tk, tn), b.dtype) + _nbytes((tm, tn), out_dtype))
    scratch = [pltpu.VMEM((tm, tn), F32)] if nk > 1 else []
    return pl.pallas_call(
        body,
        out_shape=jax.ShapeDtypeStruct((M, N), out_dtype),
        grid=(M // tm, N // tn, nk),
        in_specs=[a_spec, b_spec],
        out_specs=pl.BlockSpec((tm, tn), lambda i, j, k: (i, j)),
        scratch_shapes=scratch,
        compiler_params=_params(("parallel", "parallel", "arbitrary"), blocks + _nbytes((tm, tn), F32),
                                _nbytes((tm, tn), F32) if nk > 1 else 0),
        name=name,
    )(a, b)


def _h_tile(j, x_ref, meta_ref):
    head = jnp.concatenate([jnp.zeros((FRONT, D), F32), meta_ref[...]], axis=0)
    return jnp.where(j > 0, x_ref[0], head)


def _x_spec():
    return pl.BlockSpec((1, QB, D), lambda b, j: (b, jnp.maximum(j - 1, 0), 0))


def _rms_in(x, meta, g, B, Lp):
    T = B * Lp
    NQ = Lp // QB

    def body(x_ref, meta_ref, g_ref, u_ref):
        h = _h_tile(pl.program_id(1), x_ref, meta_ref)
        r = lax.rsqrt(jnp.mean(h * h, axis=-1, keepdims=True) + EPS)
        u_ref[...] = (h * r * g_ref[...]).astype(BF16)

    return pl.pallas_call(
        body,
        out_shape=jax.ShapeDtypeStruct((T, D), BF16),
        grid=(B, NQ),
        in_specs=[_x_spec(), pl.BlockSpec((N_META, D), lambda b, j: (0, 0)), pl.BlockSpec((1, D), lambda b, j: (0, 0))],
        out_specs=pl.BlockSpec((QB, D), lambda b, j: (b * NQ + j, 0)),
        compiler_params=_params(("parallel", "parallel"), _nbytes((QB, D), F32) * 2),
        name="rms_in",
    )(x, meta, g)


def _gla_gate(lr, wg, bg, valid):
    pre = _nn(lr.astype(BF16), wg) + bg
    logsig = jnp.minimum(pre, 0.0) - jnp.log(1.0 + jnp.exp(-jnp.abs(pre)))
    return pre, jnp.where(valid, logsig / GLA_NORMALIZER, 0.0)


def _tri_masks():
    ri = lax.broadcasted_iota(jnp.int32, (GLA_C, GLA_C), 0)
    ci = lax.broadcasted_iota(jnp.int32, (GLA_C, GLA_C), 1)
    return ci <= ri, ci >= ri


def _cumsum_rows(x, ones_mask):
    w = jnp.where(ones_mask, 1.0, 0.0).astype(BF16)
    a, b, c = _split3(x)
    return _nn(w, a) + _nn(w, b) + _nn(w, c)


def _gla_fwd(projA, projB, wg, bg, gn4, B, Lp):
    T = B * Lp
    NC = Lp // GLA_C
    C = GLA_C
    scale = GLA_DK ** -0.5

    def body(q_ref, k_ref, v_ref, lr_ref, z_ref, wg_ref, bg_ref, gn_ref, oa_ref, ya_ref, ssave_ref, st_ref):
        n = pl.program_id(0)

        @pl.when(n == 0)
        def _():
            st_ref[...] = jnp.zeros_like(st_ref)

        pos = n * C + lax.broadcasted_iota(jnp.int32, (C, 1), 0)
        lower, _ = _tri_masks()
        is_last = lax.broadcasted_iota(jnp.int32, (C, 1), 0) == C - 1
        for b in range(B):
            ssave_ref[b, 0] = st_ref[b]
            _, glog = _gla_gate(lr_ref[b], wg_ref[...], bg_ref[...], pos >= FRONT)
            bcum = _cumsum_rows(glog, lower)
            for h in range(GLA_H):
                ks = slice(h * GLA_DK, (h + 1) * GLA_DK)
                vs = slice(h * GLA_DV, (h + 1) * GLA_DV)
                bh = bcum[:, ks]
                blast = jnp.sum(jnp.where(is_last, bh, 0.0), axis=0, keepdims=True)
                qh = q_ref[b, :, ks].astype(F32) * scale
                kh = k_ref[b, :, ks].astype(F32)
                qe = (qh * jnp.exp(bh)).astype(BF16)
                ke = (kh * jnp.exp(-bh)).astype(BF16)
                kl = (kh * jnp.exp(blast - bh)).astype(BF16)
                vh = v_ref[b, :, vs].astype(BF16)
                a = jnp.where(lower, _nt(qe, ke), 0.0).astype(BF16)
                st = st_ref[b, h]
                o = _nn(a, vh) + _nt(qe, st.astype(BF16))
                st_ref[b, h] = st * jnp.exp(blast) + _tn(vh, kl)
                oa_ref[b, :, vs] = o
                on = o * lax.rsqrt(jnp.mean(o * o, axis=-1, keepdims=True) + EPS) * gn_ref[:, vs]
                z = z_ref[b, :, vs].astype(F32)
                ya_ref[b, :, vs] = (on * (z * _sigmoid(z))).astype(BF16)

    blocks = B * (_nbytes((C, 512), F32) * 2 + _nbytes((C, 1024), F32) * 3 + _nbytes((C, 1024), BF16)
                  + _nbytes((GLA_H, GLA_DV, GLA_DK), F32)) + _nbytes((128, 512), BF16)
    state = _nbytes((B, GLA_H, GLA_DV, GLA_DK), F32)
    pa = projA.reshape(B, Lp, projA.shape[1])
    oa, ya, ssave = pl.pallas_call(
        body,
        out_shape=(jax.ShapeDtypeStruct((B, Lp, GLA_VW), F32), jax.ShapeDtypeStruct((B, Lp, GLA_VW), BF16),
                   jax.ShapeDtypeStruct((B, NC, GLA_H, GLA_DV, GLA_DK), F32)),
        grid=(NC,),
        in_specs=[
            pl.BlockSpec((B, C, 512), lambda n: (0, n, 10)),
            pl.BlockSpec((B, C, 512), lambda n: (0, n, 11)),
            pl.BlockSpec((B, C, 1024), lambda n: (0, n, 0)),
            pl.BlockSpec((B, C, 128), lambda n: (0, n, 3)),
            pl.BlockSpec((B, C, 1024), lambda n: (0, n, 1)),
            pl.BlockSpec((128, 512), lambda n: (0, 0)),
            pl.BlockSpec((1, 512), lambda n: (0, 0)),
            pl.BlockSpec((1, 1024), lambda n: (0, 0)),
        ],
        out_specs=(pl.BlockSpec((B, C, 1024), lambda n: (0, n, 0)),
                   pl.BlockSpec((B, C, 1024), lambda n: (0, n, 0)),
                   pl.BlockSpec((B, 1, GLA_H, GLA_DV, GLA_DK), lambda n: (0, n, 0, 0, 0))),
        scratch_shapes=[pltpu.VMEM((B, GLA_H, GLA_DV, GLA_DK), F32)],
        compiler_params=_params(("arbitrary",), blocks, state),
        name="gla_fwd",
    )(pa, pa, pa, projB.reshape(B, Lp, projB.shape[1]), pa, wg, bg, gn4)
    return oa.reshape(T, GLA_VW), ya.reshape(T, GLA_VW), ssave


def _swap_halves(x):
    lane = lax.broadcasted_iota(jnp.int32, x.shape, 1)
    return jnp.where((lane % 64) < 32, pltpu.roll(x, 96, 1), pltpu.roll(x, 32, 1))


def _mla_prep(projB, cos_t, sin_t, gq, gkv, wuq2, wukv, B, Lp, tr):
    T = B * Lp
    nt = Lp // tr
    HW = 2 * LANES

    def body(pb_ref, cos_ref, sin_ref, gq_ref, gkv_ref, wuq_ref, wukv_ref, q_ref, k_ref, v_ref, cqn_ref, ckvn_ref):
        cq = pb_ref[:, 0:Q_RANK].astype(F32)
        ckv = pb_ref[:, Q_RANK:Q_RANK + KV_RANK].astype(F32)
        kr = pb_ref[:, 512:640].astype(F32)
        cqn = (cq * lax.rsqrt(jnp.mean(cq * cq, axis=-1, keepdims=True) + EPS) * gq_ref[...]).astype(BF16)
        ckvn = (ckv * lax.rsqrt(jnp.mean(ckv * ckv, axis=-1, keepdims=True) + EPS) * gkv_ref[...]).astype(BF16)
        cqn_ref[...] = cqn
        ckvn_ref[...] = ckvn
        qf = _nn(cqn, wuq_ref[...])
        kvf = _nn(ckvn, wukv_ref[...])
        cs = cos_ref[...]
        sn = sin_ref[...]
        rope = lambda t: t * cs + _swap_halves(t) * sn
        kr_r = rope(kr).astype(BF16)
        for h in range(MLA_H):
            q_ref[:, h * HW:h * HW + LANES] = qf[:, h * HW:h * HW + LANES].astype(BF16)
            q_ref[:, h * HW + LANES:(h + 1) * HW] = rope(qf[:, h * HW + LANES:(h + 1) * HW]).astype(BF16)
            k_ref[:, h * HW:h * HW + LANES] = kvf[:, h * HW:h * HW + LANES].astype(BF16)
            k_ref[:, h * HW + LANES:(h + 1) * HW] = kr_r
            v_ref[:, h * MLA_DV:(h + 1) * MLA_DV] = kvf[:, h * HW + LANES:(h + 1) * HW].astype(BF16)

    blocks = (_nbytes((tr, 640), F32) + 2 * _nbytes((tr, 128), F32) + _nbytes((Q_RANK, 2048), BF16)
              + _nbytes((KV_RANK, 2048), BF16) + _nbytes((tr, 2048 * 2 + 1024 + 384), BF16)
              + 2 * _nbytes((tr, 2048), F32))
    return pl.pallas_call(
        body,
        out_shape=(jax.ShapeDtypeStruct((T, MLA_H * HW), BF16), jax.ShapeDtypeStruct((T, MLA_H * HW), BF16),
                   jax.ShapeDtypeStruct((T, MLA_H * MLA_DV), BF16), jax.ShapeDtypeStruct((T, Q_RANK), BF16),
                   jax.ShapeDtypeStruct((T, KV_RANK), BF16)),
        grid=(B, nt),
        in_specs=[
            pl.BlockSpec((tr, 640), lambda b, j: (b * nt + j, 0)),
            pl.BlockSpec((tr, 128), lambda b, j: (j, 0)),
            pl.BlockSpec((tr, 128), lambda b, j: (j, 0)),
            pl.BlockSpec((1, Q_RANK), lambda b, j: (0, 0)),
            pl.BlockSpec((1, KV_RANK), lambda b, j: (0, 0)),
            pl.BlockSpec((Q_RANK, 2048), lambda b, j: (0, 0)),
            pl.BlockSpec((KV_RANK, 2048), lambda b, j: (0, 0)),
        ],
        out_specs=(pl.BlockSpec((tr, 2048), lambda b, j: (b * nt + j, 0)),
                   pl.BlockSpec((tr, 2048), lambda b, j: (b * nt + j, 0)),
                   pl.BlockSpec((tr, 1024), lambda b, j: (b * nt + j, 0)),
                   pl.BlockSpec((tr, Q_RANK), lambda b, j: (b * nt + j, 0)),
                   pl.BlockSpec((tr, KV_RANK), lambda b, j: (b * nt + j, 0))),
        compiler_params=_params(("parallel", "parallel"), blocks),
        name="mla_prep",
    )(projB, cos_t, sin_t, gq, gkv, wuq2, wukv)


def _attn_mask(row, col):
    return (col <= row) & ((col >= FRONT) | (row < FRONT))


def _attn_fwd(q_att, k_att, v_att, projA, B, Lp):
    T = B * Lp
    NQ = Lp // QB
    HW = 2 * LANES
    scale = 1.0 / math.sqrt(MLA_QK)

    def body(q_ref, k_ref, v_ref, mz_ref, o_ref, yb_ref, lsec_ref, m_ref, l_ref, acc_ref):
        qi = pl.program_id(1)
        m_ref[...] = jnp.full(m_ref.shape, NEG, F32)
        l_ref[...] = jnp.zeros_like(l_ref)
        acc_ref[...] = jnp.zeros_like(acc_ref)
        row = qi * QB + lax.broadcasted_iota(jnp.int32, (QB, QB), 0)
        coli = lax.broadcasted_iota(jnp.int32, (QB, QB), 1)

        def step(kj, carry):
            off = pl.multiple_of(kj * QB, QB)
            ok = _attn_mask(row, kj * QB + coli)
            for h in range(MLA_H):
                q = q_ref[:, h * HW:(h + 1) * HW]
                kb = k_ref[pl.ds(off, QB), h * HW:(h + 1) * HW]
                vb = v_ref[pl.ds(off, QB), h * MLA_DV:(h + 1) * MLA_DV]
                s = jnp.where(ok, _nt(q, kb) * scale, NEG)
                m_old = m_ref[h]
                m_new = jnp.maximum(m_old, jnp.max(s, axis=-1, keepdims=True))
                alpha = jnp.exp(m_old - m_new)
                p = jnp.exp(s - jnp.tile(m_new, (1, QB // LANES)))
                m_ref[h] = m_new
                l_ref[h] = alpha * l_ref[h] + jnp.sum(p, axis=-1, keepdims=True)
                acc_ref[h] = alpha * acc_ref[h] + _nn(p.astype(BF16), vb)
            return carry

        lax.fori_loop(0, qi + 1, step, 0)
        for h in range(MLA_H):
            hs = slice(h * MLA_DV, (h + 1) * MLA_DV)
            l = l_ref[h]
            o = acc_ref[h] / l
            o_ref[:, hs] = o
            z = mz_ref[:, hs].astype(F32)
            yb_ref[:, hs] = (o * (z * _sigmoid(z))).astype(BF16)
            lse = m_ref[h] + jnp.log(l)
            lsec_ref[0, h, pl.ds(qi, 1), :] = jnp.transpose(lse)[0:1, :]

    blocks = (_nbytes((QB, 2048), BF16) + _nbytes((Lp, 2048), BF16) + _nbytes((Lp, 1024), BF16)
              + 2 * _nbytes((QB, 1024), F32) + _nbytes((QB, 1024), BF16) + _nbytes((MLA_H, QB, LANES), F32)
              + _nbytes((MLA_H, NQ, QB), F32))
    return pl.pallas_call(
        body,
        out_shape=(jax.ShapeDtypeStruct((T, MLA_H * MLA_DV), F32), jax.ShapeDtypeStruct((T, MLA_H * MLA_DV), BF16),
                   jax.ShapeDtypeStruct((B, MLA_H, NQ, QB), F32)),
        grid=(B, NQ),
        in_specs=[
            pl.BlockSpec((QB, MLA_H * HW), lambda b, i: (b * NQ + i, 0)),
            pl.BlockSpec((Lp, MLA_H * HW), lambda b, i: (b, 0)),
            pl.BlockSpec((Lp, MLA_H * MLA_DV), lambda b, i: (b, 0)),
            pl.BlockSpec((QB, 1024), lambda b, i: (b * NQ + i, 2)),
        ],
        out_specs=(pl.BlockSpec((QB, 1024), lambda b, i: (b * NQ + i, 0)),
                   pl.BlockSpec((QB, 1024), lambda b, i: (b * NQ + i, 0)),
                   pl.BlockSpec((1, MLA_H, NQ, QB), lambda b, i: (b, 0, 0, 0))),
        scratch_shapes=[pltpu.VMEM((MLA_H, QB, LANES), F32), pltpu.VMEM((MLA_H, QB, LANES), F32),
                        pltpu.VMEM((MLA_H, QB, MLA_DV), F32)],
        compiler_params=_params(("parallel", "arbitrary"), blocks, 3 * _nbytes((MLA_H, QB, LANES), F32)),
        name="attn_fwd",
    )(q_att, k_att, v_att, projA)


def _out_proj_loss(x, meta, projA, ya, yb, w_out, gf, tgt, B, Lp):
    T = B * Lp
    NQ = Lp // QB

    def body(x_ref, meta_ref, gg_ref, gm_ref, ya_ref, yb_ref, w_ref, gf_ref, t_ref,
             dh_ref, dhb_ref, mg_ref, loss_ref, dgf_ref):
        b = pl.program_id(0)
        j = pl.program_id(1)

        @pl.when((b == 0) & (j == 0))
        def _():
            loss_ref[...] = jnp.zeros_like(loss_ref)
            dgf_ref[...] = jnp.zeros_like(dgf_ref)

        f32 = lambda ref: ref[...].astype(F32)
        merged = (_sigmoid(f32(gg_ref)) * f32(ya_ref) + _sigmoid(f32(gm_ref)) * f32(yb_ref)).astype(BF16)
        mg_ref[...] = merged
        h1 = _h_tile(j, x_ref, meta_ref) + _nn(merged, w_ref[...])
        r = lax.rsqrt(jnp.mean(h1 * h1, axis=-1, keepdims=True) + EPS)
        hn = h1 * r
        gfv = gf_ref[...]
        diff = jnp.where(j > 0, hn * gfv - t_ref[0], 0.0)
        loss_ref[...] += (0.5 / D) * jnp.sum(jnp.sum(diff * diff, axis=-1, keepdims=True), axis=0, keepdims=True)
        dout = diff * (1.0 / D)
        dgf_ref[...] += jnp.sum(dout * hn, axis=0, keepdims=True)
        dhn = dout * gfv
        dh = r * (dhn - hn * jnp.mean(dhn * hn, axis=-1, keepdims=True))
        dh_ref[...] = dh
        dhb_ref[...] = dh.astype(BF16)

    rows = lambda c: pl.BlockSpec((QB, D), lambda b, j: (b * NQ + j, c))
    const = lambda s: pl.BlockSpec(s, lambda b, j: (0, 0))
    return pl.pallas_call(
        body,
        out_shape=(jax.ShapeDtypeStruct((T, D), F32), jax.ShapeDtypeStruct((T, D), BF16),
                   jax.ShapeDtypeStruct((T, D), BF16), jax.ShapeDtypeStruct((1, 1), F32),
                   jax.ShapeDtypeStruct((1, D), F32)),
        grid=(B, NQ),
        in_specs=[_x_spec(), const((N_META, D)), rows(3), rows(4), rows(0), rows(0), const((D, D)),
                  const((1, D)), _x_spec()],
        out_specs=(rows(0), rows(0), rows(0), const((1, 1)), const((1, D))),
        compiler_params=_params(("arbitrary", "arbitrary"), 10 * _nbytes((QB, D), F32)),
        name="out_proj_loss",
    )(x, meta, projA, projA, ya, yb, w_out, gf, tgt)


def _merge_bwd(dh1_b, w_out, projA, ya, yb, tr):
    T = dh1_b.shape[0]

    def body(dh_ref, w_ref, gg_ref, gm_ref, ya_ref, yb_ref, dya_ref, dyb_ref, da_ref):
        d = _nt(dh_ref[...], w_ref[...])
        sg = _sigmoid(gg_ref[...].astype(F32))
        sm = _sigmoid(gm_ref[...].astype(F32))
        dya_ref[...] = (d * sg).astype(BF16)
        dyb_ref[...] = (d * sm).astype(BF16)
        da_ref[:, 0:D] = (d * ya_ref[...].astype(F32) * (sg * (1.0 - sg))).astype(BF16)
        da_ref[:, D:2 * D] = (d * yb_ref[...].astype(F32) * (sm * (1.0 - sm))).astype(BF16)

    spec = lambda c: pl.BlockSpec((tr, D), lambda i: (i, c))
    return pl.pallas_call(
        body,
        out_shape=(jax.ShapeDtypeStruct((T, D), BF16), jax.ShapeDtypeStruct((T, D), BF16),
                   jax.ShapeDtypeStruct((T, 2 * D), BF16)),
        grid=(T // tr,),
        in_specs=[spec(0), pl.BlockSpec((D, D), lambda i: (0, 0)), spec(3), spec(4), spec(0), spec(0)],
        out_specs=(spec(0), spec(0), pl.BlockSpec((tr, 2 * D), lambda i: (i, 0))),
        compiler_params=_params(("parallel",), 8 * _nbytes((tr, D), F32)),
        name="merge_bwd",
    )(dh1_b, w_out, projA, projA, ya, yb)


def _gla_out_bwd(dya, gla_proj, oa, projA, gn4, tr):
    T = dya.shape[0]
    nsteps = T // tr

    def body(dya_ref, w_ref, oa_ref, z_ref, gn_ref, do_ref, dz_ref, dgn_ref, acc_ref):
        i = pl.program_id(0)

        @pl.when(i == 0)
        def _():
            acc_ref[...] = jnp.zeros_like(acc_ref)

        dy_all = _nt(dya_ref[...], w_ref[...])
        for h in range(GLA_H):
            vs = slice(h * GLA_DV, (h + 1) * GLA_DV)
            dy = dy_all[:, vs]
            o = oa_ref[:, vs]
            z = z_ref[:, vs].astype(F32)
            gn = gn_ref[:, vs]
            s = _sigmoid(z)
            ra = lax.rsqrt(jnp.mean(o * o, axis=-1, keepdims=True) + EPS)
            on = o * ra
            don = dy * (z * s)
            t = don * gn
            do_ref[:, vs] = (ra * (t - on * jnp.mean(t * on, axis=-1, keepdims=True))).astype(BF16)
            dz_ref[:, vs] = (dy * (on * gn) * (s * (1.0 + z * (1.0 - s)))).astype(BF16)
            acc_ref[:, vs] += jnp.sum(don * on, axis=0, keepdims=True)

        @pl.when(i == nsteps - 1)
        def _():
            a = acc_ref[...]
            dgn_ref[...] = a[:, 0:256] + a[:, 256:512] + a[:, 512:768] + a[:, 768:1024]

    spec = lambda c: pl.BlockSpec((tr, D), lambda i: (i, c))
    return pl.pallas_call(
        body,
        out_shape=(jax.ShapeDtypeStruct((T, D), BF16), jax.ShapeDtypeStruct((T, D), BF16),
                   jax.ShapeDtypeStruct((1, GLA_DV), F32)),
        grid=(nsteps,),
        in_specs=[spec(0), pl.BlockSpec((D, D), lambda i: (0, 0)), spec(0), spec(1),
                  pl.BlockSpec((1, D), lambda i: (0, 0))],
        out_specs=(spec(0), spec(0), pl.BlockSpec((1, GLA_DV), lambda i: (0, 0))),
        scratch_shapes=[pltpu.VMEM((1, D), F32)],
        compiler_params=_params(("arbitrary",), 6 * _nbytes((tr, D), F32)),
        name="gla_out_bwd",
    )(dya, gla_proj, oa, projA, gn4)


def _gla_bwd(projA, projB, ssave, doa, wg, bg, B, Lp):
    T = B * Lp
    NC = Lp // GLA_C
    C = GLA_C
    scale = GLA_DK ** -0.5
    WC = 2304

    def body(q_ref, k_ref, v_ref, lr_ref, ss_ref, do_ref, wg_ref, bg_ref, dc_ref, dwg_ref, dbg_ref, dst_ref):
        i = pl.program_id(0)
        n = NC - 1 - i

        @pl.when(i == 0)
        def _():
            dst_ref[...] = jnp.zeros_like(dst_ref)
            dwg_ref[...] = jnp.zeros_like(dwg_ref)
            dbg_ref[...] = jnp.zeros_like(dbg_ref)

        pos = n * C + lax.broadcasted_iota(jnp.int32, (C, 1), 0)
        valid = pos >= FRONT
        lower, upper = _tri_masks()
        is_last = lax.broadcasted_iota(jnp.int32, (C, 1), 0) == C - 1
        for b in range(B):
            lr = lr_ref[b]
            pre, glog = _gla_gate(lr, wg_ref[...], bg_ref[...], valid)
            bcum = _cumsum_rows(glog, lower)
            db_parts = []
            for h in range(GLA_H):
                ks = slice(h * GLA_DK, (h + 1) * GLA_DK)
                vs = slice(h * GLA_DV, (h + 1) * GLA_DV)
                bh = bcum[:, ks]
                blast = jnp.sum(jnp.where(is_last, bh, 0.0), axis=0, keepdims=True)
                eb, enb, ekl, ebl = jnp.exp(bh), jnp.exp(-bh), jnp.exp(blast - bh), jnp.exp(blast)
                qh = q_ref[b, :, ks].astype(F32) * scale
                kh = k_ref[b, :, ks].astype(F32)
                qe_f, ke_f, kl_f = qh * eb, kh * enb, kh * ekl
                qe, ke, kl = qe_f.astype(BF16), ke_f.astype(BF16), kl_f.astype(BF16)
                vh = v_ref[b, :, vs].astype(BF16)
                doh = do_ref[b, :, vs]
                st = ss_ref[b, 0, h]
                dst = dst_ref[b, h]
                st_b, dst_b = st.astype(BF16), dst.astype(BF16)
                da = jnp.where(lower, _nt(doh, vh), 0.0).astype(BF16)
                da_t = jnp.where(upper, _nt(vh, doh), 0.0).astype(BF16)
                a_t = jnp.where(upper, _nt(ke, qe), 0.0).astype(BF16)
                dqe = _nn(da, ke) + _nn(doh, st_b)
                dke = _nn(da_t, qe)
                dvh = _nn(a_t, doh) + _nt(kl, dst_b)
                dkl = _nn(vh, dst_b)
                dst_ref[b, h] = dst * ebl + _tn(doh, qe)
                deb = jnp.sum(st * dst, axis=0, keepdims=True)
                db = dqe * qe_f - dke * ke_f - dkl * kl_f
                db_last = jnp.sum(dkl * kl_f, axis=0, keepdims=True) + deb * ebl
                db_parts.append(db + jnp.where(is_last, db_last, 0.0))
                dc_ref[b, :, vs] = dvh.astype(BF16)
                dc_ref[b, :, 1024 + h * GLA_DK:1024 + (h + 1) * GLA_DK] = (dqe * eb * scale).astype(BF16)
                dc_ref[b, :, 1536 + h * GLA_DK:1536 + (h + 1) * GLA_DK] = (dke * enb + dkl * ekl).astype(BF16)
            dglog = _cumsum_rows(jnp.concatenate(db_parts, axis=1), upper)
            dpre = jnp.where(valid, dglog * (1.0 / GLA_NORMALIZER) / (1.0 + jnp.exp(pre)), 0.0)
            dpre_b = dpre.astype(BF16)
            dc_ref[b, :, 2048:2176] = _nt(dpre_b, wg_ref[...]).astype(BF16)
            dc_ref[b, :, 2176:2304] = jnp.zeros((C, 128), BF16)
            dwg_ref[...] += _tn(lr.astype(BF16), dpre_b)
            dbg_ref[...] += jnp.sum(dpre, axis=0, keepdims=True)

    blocks = B * (_nbytes((C, 512), F32) * 2 + _nbytes((C, 1024), F32) + _nbytes((C, 1024), BF16)
                  + _nbytes((GLA_H, GLA_DV, GLA_DK), F32) + _nbytes((C, WC), BF16)) + 3 * _nbytes((128, 512), F32)
    state = _nbytes((B, GLA_H, GLA_DV, GLA_DK), F32)
    pa = projA.reshape(B, Lp, projA.shape[1])
    rev = lambda i: NC - 1 - i
    dc, dwg, dbg = pl.pallas_call(
        body,
        out_shape=(jax.ShapeDtypeStruct((B, Lp, WC), BF16), jax.ShapeDtypeStruct((128, GLA_KW), F32),
                   jax.ShapeDtypeStruct((1, GLA_KW), F32)),
        grid=(NC,),
        in_specs=[
            pl.BlockSpec((B, C, 512), lambda i: (0, rev(i), 10)),
            pl.BlockSpec((B, C, 512), lambda i: (0, rev(i), 11)),
            pl.BlockSpec((B, C, 1024), lambda i: (0, rev(i), 0)),
            pl.BlockSpec((B, C, 128), lambda i: (0, rev(i), 3)),
            pl.BlockSpec((B, 1, GLA_H, GLA_DV, GLA_DK), lambda i: (0, rev(i), 0, 0, 0)),
            pl.BlockSpec((B, C, 1024), lambda i: (0, rev(i), 0)),
            pl.BlockSpec((128, 512), lambda i: (0, 0)),
            pl.BlockSpec((1, 512), lambda i: (0, 0)),
        ],
        out_specs=(pl.BlockSpec((B, C, WC), lambda i: (0, rev(i), 0)),
                   pl.BlockSpec((128, GLA_KW), lambda i: (0, 0)),
                   pl.BlockSpec((1, GLA_KW), lambda i: (0, 0))),
        scratch_shapes=[pltpu.VMEM((B, GLA_H, GLA_DV, GLA_DK), F32)],
        compiler_params=_params(("arbitrary",), blocks, state),
        name="gla_bwd",
    )(pa, pa, pa, projB.reshape(B, Lp, projB.shape[1]), ssave, doa.reshape(B, Lp, GLA_VW), wg, bg)
    return dc.reshape(T, WC), dwg, dbg


def _attn_bwd_pre(dyb, mla_proj, projA, ob, B, Lp):
    T = B * Lp
    NQ = Lp // QB

    def body(dyb_ref, w_ref, z_ref, o_ref, do_ref, dz_ref, dcol_ref):
        j = pl.program_id(1)
        dy_all = _nt(dyb_ref[...], w_ref[...])
        for h in range(MLA_H):
            hs = slice(h * MLA_DV, (h + 1) * MLA_DV)
            dy = dy_all[:, hs]
            z = z_ref[:, hs].astype(F32)
            o = o_ref[:, hs]
            s = _sigmoid(z)
            do = dy * (z * s)
            do_ref[:, hs] = do.astype(BF16)
            dz_ref[:, hs] = (dy * o * (s * (1.0 + z * (1.0 - s)))).astype(BF16)
            dl = jnp.broadcast_to(jnp.sum(do * o, axis=-1, keepdims=True), (QB, LANES))
            dcol_ref[0, h, pl.ds(j, 1), :] = jnp.transpose(dl)[0:1, :]

    rows = lambda c: pl.BlockSpec((QB, D), lambda b, j: (b * NQ + j, c))
    return pl.pallas_call(
        body,
        out_shape=(jax.ShapeDtypeStruct((T, D), BF16), jax.ShapeDtypeStruct((T, D), BF16),
                   jax.ShapeDtypeStruct((B, MLA_H, NQ, QB), F32)),
        grid=(B, NQ),
        in_specs=[rows(0), pl.BlockSpec((D, D), lambda b, j: (0, 0)), rows(2), rows(0)],
        out_specs=(rows(0), rows(0), pl.BlockSpec((1, MLA_H, NQ, QB), lambda b, j: (b, 0, 0, 0))),
        compiler_params=_params(("parallel", "arbitrary"), 6 * _nbytes((QB, D), F32)),
        name="attn_bwd_pre",
    )(dyb, mla_proj, projA, ob)


ATTN_BWD_HEADS = 8


def _attn_bwd(q_att, k_att, v_att, do, lse_c, delta_c, B, Lp):
    T = B * Lp
    NQ = Lp // QB
    G = ATTN_BWD_HEADS
    NG = MLA_H // G
    HW = 2 * LANES
    scale = 1.0 / math.sqrt(MLA_QK)

    def body(q_ref, k_ref, v_ref, do_ref, lse_ref, dl_ref, dq_out, dk_out, dv_out, dq_ref, dk_ref, dv_ref):
        kj = pl.program_id(2)

        @pl.when(kj == 0)
        def _():
            dq_ref[...] = jnp.zeros_like(dq_ref)

        dk_ref[...] = jnp.zeros_like(dk_ref)
        dv_ref[...] = jnp.zeros_like(dv_ref)
        col = kj * QB + lax.broadcasted_iota(jnp.int32, (QB, QB), 0)
        rowi = lax.broadcasted_iota(jnp.int32, (QB, QB), 1)

        def step(qi, carry):
            off = pl.multiple_of(qi * QB, QB)
            ok = _attn_mask(qi * QB + rowi, col)
            for h in range(G):
                ws = slice(h * HW, (h + 1) * HW)
                hs = slice(h * MLA_DV, (h + 1) * MLA_DV)
                qb = q_ref[pl.ds(off, QB), ws]
                dob = do_ref[pl.ds(off, QB), hs]
                kb = k_ref[:, ws]
                lse = lse_ref[0, h, pl.ds(qi, 1), :]
                delta = dl_ref[0, h, pl.ds(qi, 1), :]
                s_t = _nt(kb, qb) * scale
                p_t = jnp.where(ok, jnp.exp(s_t - lse), 0.0)
                dv_ref[:, hs] += _nn(p_t.astype(BF16), dob)
                ds_t = (p_t * (_nt(v_ref[:, hs], dob) - delta) * scale).astype(BF16)
                dk_ref[:, ws] += _nn(ds_t, qb)
                dq_ref[pl.ds(off, QB), ws] += _tn(ds_t, kb)
            return carry

        lax.fori_loop(kj, NQ, step, 0)
        dk_out[...] = dk_ref[...].astype(BF16)
        dv_out[...] = dv_ref[...].astype(BF16)

        @pl.when(kj == NQ - 1)
        def _():
            dq_out[...] = dq_ref[...].astype(BF16)

    blocks = (2 * _nbytes((Lp, G * HW), BF16) + _nbytes((Lp, G * MLA_DV), BF16) + 2 * _nbytes((QB, G * 384), BF16)
              + 2 * _nbytes((G, NQ, QB), F32))
    scratch = [pltpu.VMEM((Lp, G * HW), F32), pltpu.VMEM((QB, G * HW), F32), pltpu.VMEM((QB, G * MLA_DV), F32)]
    return pl.pallas_call(
        body,
        out_shape=(jax.ShapeDtypeStruct((T, MLA_H * HW), BF16), jax.ShapeDtypeStruct((T, MLA_H * HW), BF16),
                   jax.ShapeDtypeStruct((T, MLA_H * MLA_DV), BF16)),
        scratch_shapes=scratch,
        grid=(B, NG, NQ),
        in_specs=[
            pl.BlockSpec((Lp, G * HW), lambda b, g, j: (b, g), pipeline_mode=pl.Buffered(1)),
            pl.BlockSpec((QB, G * HW), lambda b, g, j: (b * NQ + j, g)),
            pl.BlockSpec((QB, G * MLA_DV), lambda b, g, j: (b * NQ + j, g)),
            pl.BlockSpec((Lp, G * MLA_DV), lambda b, g, j: (b, g), pipeline_mode=pl.Buffered(1)),
            pl.BlockSpec((1, G, NQ, QB), lambda b, g, j: (b, g, 0, 0)),
            pl.BlockSpec((1, G, NQ, QB), lambda b, g, j: (b, g, 0, 0)),
        ],
        out_specs=(pl.BlockSpec((Lp, G * HW), lambda b, g, j: (b, g), pipeline_mode=pl.Buffered(1)),
                   pl.BlockSpec((QB, G * HW), lambda b, g, j: (b * NQ + j, g)),
                   pl.BlockSpec((QB, G * MLA_DV), lambda b, g, j: (b * NQ + j, g))),
        compiler_params=_params(("parallel", "parallel", "arbitrary"), blocks,
                                _nbytes((Lp, G * HW), F32) + _nbytes((QB, G * 384), F32)),
        name="attn_bwd",
    )(q_att, k_att, v_att, do, lse_c, delta_c)


def _mla_bwd_post(dq, dk, dv, projB, cos_t, sin_t, gq, gkv, wuq2, wukv, B, Lp, tr):
    T = B * Lp
    nt = Lp // tr
    HW = 2 * LANES

    def body(dq_ref, dk_ref, dv_ref, pb_ref, cos_ref, sin_ref, gq_ref, gkv_ref, wuq_ref, wukv_ref,
             dqf_ref, dkvf_ref, de_ref, dgq_ref, dgkv_ref):
        first = (pl.program_id(0) == 0) & (pl.program_id(1) == 0)

        @pl.when(first)
        def _():
            dgq_ref[...] = jnp.zeros_like(dgq_ref)
            dgkv_ref[...] = jnp.zeros_like(dgkv_ref)

        cs = cos_ref[...]
        sn = sin_ref[...]
        rope_t = lambda t: t * cs + _swap_halves(t * sn)
        dkr = jnp.zeros((tr, LANES), F32)
        for h in range(MLA_H):
            dqf_ref[:, h * HW:h * HW + LANES] = dq_ref[:, h * HW:h * HW + LANES]
            dq_rope = dq_ref[:, h * HW + LANES:(h + 1) * HW].astype(F32)
            dqf_ref[:, h * HW + LANES:(h + 1) * HW] = rope_t(dq_rope).astype(BF16)
            dkvf_ref[:, h * HW:h * HW + LANES] = dk_ref[:, h * HW:h * HW + LANES]
            dkvf_ref[:, h * HW + LANES:(h + 1) * HW] = dv_ref[:, h * MLA_DV:(h + 1) * MLA_DV]
            dkr = dkr + dk_ref[:, h * HW + LANES:(h + 1) * HW].astype(F32)

        def norm_bwd(x, dn, g):
            r = lax.rsqrt(jnp.mean(x * x, axis=-1, keepdims=True) + EPS)
            xn = x * r
            t = dn * g
            return r * (t - xn * jnp.mean(t * xn, axis=-1, keepdims=True)), jnp.sum(dn * xn, axis=0, keepdims=True)

        cq = pb_ref[:, 0:Q_RANK].astype(F32)
        ckv = pb_ref[:, Q_RANK:Q_RANK + KV_RANK].astype(F32)
        dcq, dgq = norm_bwd(cq, _nt(dqf_ref[...], wuq_ref[...]), gq_ref[...])
        dckv, dgkv = norm_bwd(ckv, _nt(dkvf_ref[...], wukv_ref[...]), gkv_ref[...])
        dgq_ref[...] += dgq
        dgkv_ref[...] += dgkv
        de_ref[:, 0:Q_RANK] = dcq.astype(BF16)
        de_ref[:, Q_RANK:Q_RANK + KV_RANK] = dckv.astype(BF16)
        de_ref[:, 384:512] = rope_t(dkr).astype(BF16)

    rows = lambda w: pl.BlockSpec((tr, w), lambda b, j: (b * nt + j, 0))
    const = lambda s: pl.BlockSpec(s, lambda b, j: (0, 0))
    blocks = (2 * _nbytes((tr, 2048), F32) + _nbytes((tr, 1024), F32) + _nbytes((tr, 640), F32)
              + 2 * _nbytes((tr, 2048), BF16) + _nbytes((2048, 384), BF16) + 2 * _nbytes((tr, 2048), F32))
    return pl.pallas_call(
        body,
        out_shape=(jax.ShapeDtypeStruct((T, 2048), BF16), jax.ShapeDtypeStruct((T, 2048), BF16),
                   jax.ShapeDtypeStruct((T, 512), BF16), jax.ShapeDtypeStruct((1, Q_RANK), F32),
                   jax.ShapeDtypeStruct((1, KV_RANK), F32)),
        grid=(B, nt),
        in_specs=[rows(2048), rows(2048), rows(1024), rows(640),
                  pl.BlockSpec((tr, 128), lambda b, j: (j, 0)), pl.BlockSpec((tr, 128), lambda b, j: (j, 0)),
                  const((1, Q_RANK)), const((1, KV_RANK)), const((Q_RANK, 2048)), const((KV_RANK, 2048))],
        out_specs=(rows(2048), rows(2048), rows(512), const((1, Q_RANK)), const((1, KV_RANK))),
        compiler_params=_params(("arbitrary", "arbitrary"), blocks),
        name="mla_bwd_post",
    )(dq, dk, dv, projB, cos_t, sin_t, gq, gkv, wuq2, wukv)


def _in_proj_bwd(x, meta, dh1, dA, dBz, dC, dDz, dE, wA, wB, g, B, Lp):
    NQ = Lp // QB
    seq = x.shape[1]

    def body(x_ref, meta_ref, dh_ref, da_ref, db_ref, dc_ref, dd_ref, de_ref, wa_ref, wb_ref, g_ref,
             gx_ref, dmeta_ref, dg_ref):
        b = pl.program_id(0)
        j = pl.program_id(1)

        @pl.when((b == 0) & (j == 0))
        def _():
            dg_ref[...] = jnp.zeros_like(dg_ref)

        du = _nt(da_ref[...], wa_ref[:, 3072:5120])
        du = du + _nt(db_ref[...], wa_ref[:, 1024:2048])
        du = du + _nt(dd_ref[...], wa_ref[:, 2048:3072])
        du = du + _nt(dc_ref[:, 0:1024], wa_ref[:, 0:1024])
        du = du + _nt(dc_ref[:, 1024:2048], wa_ref[:, 5120:6144])
        du = du + _nt(dc_ref[:, 2048:2176], wb_ref[:, 384:512])
        du = du + _nt(de_ref[:, 0:384], wb_ref[:, 0:384])
        du = du + _nt(de_ref[:, 384:512], wb_ref[:, 512:640])

        x = _h_tile(j, x_ref, meta_ref)
        r = lax.rsqrt(jnp.mean(x * x, axis=-1, keepdims=True) + EPS)
        xn = x * r
        t = du * g_ref[...]
        dh0 = dh_ref[...] + r * (t - xn * jnp.mean(t * xn, axis=-1, keepdims=True))
        dg_ref[...] += jnp.sum(du * xn, axis=0, keepdims=True)
        gx_ref[0] = dh0

        @pl.when((j == 0) & (b == 0))
        def _():
            dmeta_ref[...] = dh0[FRONT:HEAD_ROWS, :]

        @pl.when((j == 0) & (b > 0))
        def _():
            dmeta_ref[...] += dh0[FRONT:HEAD_ROWS, :]

    rows = lambda w: pl.BlockSpec((QB, w), lambda b, j: (b * NQ + j, 0))
    const = lambda s: pl.BlockSpec(s, lambda b, j: (0, 0))
    widths = [a.shape[1] for a in (dA, dBz, dC, dDz, dE)]
    blocks = (sum(_nbytes((QB, w), BF16) for w in widths) + _nbytes(wA.shape, BF16) + _nbytes(wB.shape, BF16)
              + 4 * _nbytes((QB, D), F32))
    return pl.pallas_call(
        body,
        out_shape=(jax.ShapeDtypeStruct((B, seq, D), F32), jax.ShapeDtypeStruct((N_META, D), F32),
                   jax.ShapeDtypeStruct((1, D), F32)),
        grid=(B, NQ),
        in_specs=[_x_spec(), const((N_META, D)), rows(D)] + [rows(w) for w in widths]
        + [const(wA.shape), const(wB.shape), const((1, D))],
        out_specs=(_x_spec(), const((N_META, D)), const((1, D))),
        compiler_params=_params(("arbitrary", "arbitrary"), blocks),
        name="in_proj_bwd",
    )(x, meta, dh1, dA, dBz, dC, dDz, dE, wA, wB, g)


_VMEM_WHOLE = pl.BlockSpec(memory_space=pltpu.VMEM)


def _params_whole(arrays):
    total = sum(_nbytes(a.shape, a.dtype) for a in arrays)
    return pltpu.CompilerParams(vmem_limit_bytes=int(min(total + 12 * 1024 * 1024, VMEM_CAP_V7X)))


def _wire_dtype(shape):
    return BF16 if shape[-2] * shape[-1] >= WIRE_BF16_MIN_ELEMS else F32


def _pair_add_big(gp, recv, c):
    _, half, cols = recv.shape
    th = _div_tile(half, 64, 16)
    out_dtype = _wire_dtype(recv.shape)

    def body(c_ref, a_ref, b_ref, o_ref):
        o_ref[...] = (a_ref[:, 0] + b_ref[...]).astype(out_dtype)

    return pl.pallas_call(
        body,
        out_shape=jax.ShapeDtypeStruct(recv.shape, out_dtype),
        grid_spec=pltpu.PrefetchScalarGridSpec(
            num_scalar_prefetch=1,
            grid=(half // th,),
            in_specs=[pl.BlockSpec((4, 1, th, cols), lambda i, c_ref: (0, c_ref[0], i, 0)),
                      pl.BlockSpec((4, th, cols), lambda i, c_ref: (0, i, 0))],
            out_specs=pl.BlockSpec((4, th, cols), lambda i, c_ref: (0, i, 0)),
        ),
        compiler_params=_params(("parallel",), 3 * _nbytes((4, th, cols), F32)),
        name="grad_pair_add_big",
    )(c, gp.reshape(4, 2, half, cols), recv)


def _pair_add_small(gps, recvs):
    n = len(gps)

    def body(*refs):
        c = lax.axis_index("c")
        for t in range(n):
            g_ref, r_ref, o_ref = refs[t], refs[n + t], refs[2 * n + t]
            half = r_ref.shape[1]
            s = g_ref[:, pl.ds(pl.multiple_of(c * half, 8), half), :] + r_ref[...]
            o_ref[...] = s.astype(o_ref.dtype)

    return pl.pallas_call(
        body,
        out_shape=[jax.ShapeDtypeStruct(r.shape, _wire_dtype(r.shape)) for r in recvs],
        in_specs=[_VMEM_WHOLE] * (2 * n),
        out_specs=[_VMEM_WHOLE] * n,
        compiler_params=_params_whole(list(gps) + 2 * list(recvs)),
        name="grad_pair_add_small",
    )(*gps, *recvs)


def _chip_order_sum(landed_ref, own_ref, me):
    p = [jnp.where(me == k, own_ref[k], landed_ref[k]).astype(F32) for k in range(4)]
    return ((p[0] + p[1]) + p[2]) + p[3]


def _sum_chips_big(landed, own, pos):
    _, half, cols = landed.shape
    th = _div_tile(half, 64, 16)

    def body(pos_ref, l_ref, s_ref, o_ref):
        o_ref[0] = _chip_order_sum(l_ref, s_ref, pos_ref[1])

    spec = pl.BlockSpec((4, th, cols), lambda i, pos_ref: (0, i, 0))
    return pl.pallas_call(
        body,
        out_shape=jax.ShapeDtypeStruct((2, half, cols), F32),
        grid_spec=pltpu.PrefetchScalarGridSpec(
            num_scalar_prefetch=1,
            grid=(half // th,),
            in_specs=[spec, spec],
            out_specs=pl.BlockSpec((1, th, cols), lambda i, pos_ref: (pos_ref[0], i, 0)),
        ),
        compiler_params=_params(("parallel",), 3 * _nbytes((4, th, cols), F32)),
        name="grad_sum_chips_big",
    )(pos, landed, own)


def _sum_chips_small(landed, own):
    n = len(landed)

    def body(*refs):
        x, y, c = _mesh_pos()
        for t in range(n):
            refs[2 * n + t][c] = _chip_order_sum(refs[t], refs[n + t], 2 * x + y)

    return pl.pallas_call(
        body,
        out_shape=[jax.ShapeDtypeStruct((2,) + p.shape[1:], F32) for p in landed],
        in_specs=[_VMEM_WHOLE] * (2 * n),
        out_specs=[_VMEM_WHOLE] * n,
        compiler_params=_params_whole(list(landed) * 3),
        name="grad_sum_chips_small",
    )(*landed, *own)


def _adamw_update(w_ref, g_ref, m_ref, v_ref, d_ref, mo_ref, vo_ref):
    c1 = 1.0 - ADAM_B1 ** ADAM_STEP
    c2 = 1.0 - ADAM_B2 ** ADAM_STEP
    gv = g_ref[...]
    mn = ADAM_B1 * m_ref[...] + (1.0 - ADAM_B1) * gv
    vn = ADAM_B2 * v_ref[...] + (1.0 - ADAM_B2) * (gv * gv)
    mo_ref[...] = mn
    vo_ref[...] = vn
    d_ref[...] = -ADAM_LR * ((mn / c1) / (jnp.sqrt(vn / c2) + ADAM_EPS) + ADAM_WD * w_ref[...])


def _adamw_big(w, g, m, v):
    lead, (rows, cols) = w.shape[:-2], w.shape[-2:]
    assert all(n == 1 for n in lead)
    tr = _div_tile(rows, 128, 8)
    spec = pl.BlockSpec((1,) * len(lead) + (tr, cols), lambda i: (0,) * len(lead) + (i, 0))
    shp = jax.ShapeDtypeStruct(w.shape, F32)
    return pl.pallas_call(
        functools.partial(_adamw_update),
        out_shape=(shp, shp, shp),
        grid=(rows // tr,),
        in_specs=[spec] * 4,
        out_specs=(spec, spec, spec),
        compiler_params=_params(("parallel",), 7 * _nbytes((tr, cols), F32)),
        name="adamw_big",
    )(w, g, m, v)


def _adamw_small(ws, gs, ms, vs):
    n = len(ws)

    def body(*refs):
        for t in range(n):
            _adamw_update(refs[t], refs[n + t], refs[2 * n + t], refs[3 * n + t],
                          refs[4 * n + t], refs[5 * n + t], refs[6 * n + t])

    shapes = [jax.ShapeDtypeStruct(w.shape, F32) for w in ws]
    return pl.pallas_call(
        body,
        out_shape=shapes * 3,
        in_specs=[_VMEM_WHOLE] * (4 * n),
        out_specs=[_VMEM_WHOLE] * (3 * n),
        compiler_params=_params_whole(list(ws) * 7),
        name="adamw_small",
    )(*ws, *gs, *ms, *vs)


def _mesh_pos():
    return lax.axis_index("x"), lax.axis_index("y"), lax.axis_index("c")


def _other_chips(x, y):
    return [(1 - x, y), (x, 1 - y), (1 - x, 1 - y)]


_ANY = pl.BlockSpec(memory_space=pl.ANY)


PAIR_SPLIT_MIN_ROWS = 64


def _weight_gather(shards):
    n = len(shards)
    split = [s.shape[0] >= PAIR_SPLIT_MIN_ROWS for s in shards]

    def body(*refs):
        w_refs, o_refs = refs[:n], refs[n:2 * n]
        send_sems, recv_sems = refs[2 * n:]
        x, y, c = _mesh_pos()
        me = 2 * x + y
        chips = _other_chips(x, y)

        def rows_of(t, core):
            rows = shards[t].shape[0]
            if not split[t]:
                return pl.ds(0, rows)
            return pl.ds(pl.multiple_of(core * (rows // 2), 16), rows // 2)

        def landed(t, k, slot, rows, to):
            ref = o_refs[t].at[slot, rows]
            return pltpu.make_async_remote_copy(src_ref=ref, dst_ref=ref, send_sem=send_sems.at[6 * t + k],
                                                recv_sem=recv_sems.at[6 * t + k], device_id=to, device_id_type=MESH)

        sends = []
        for t in range(n):
            mine = rows_of(t, c)
            for k, (px, py) in enumerate(chips):
                cp = pltpu.make_async_remote_copy(src_ref=w_refs[t].at[mine], dst_ref=o_refs[t].at[me, mine],
                                                  send_sem=send_sems.at[6 * t + k], recv_sem=recv_sems.at[6 * t + k],
                                                  device_id=(px, py, c), device_id_type=MESH)
                cp.start()
                sends.append(cp)
        for t in range(n):
            mine = rows_of(t, c)
            for k, (px, py) in enumerate(chips):
                landed(t, k, 2 * px + py, mine, (x, y, c)).wait_recv()
                if split[t]:
                    cp = landed(t, 3 + k, 2 * px + py, mine, (x, y, 1 - c))
                    cp.start()
                    sends.append(cp)
        for t in range(n):
            if split[t]:
                for k, (px, py) in enumerate(chips):
                    landed(t, 3 + k, 2 * px + py, rows_of(t, 1 - c), (x, y, c)).wait_recv()
        for cp in sends:
            cp.wait_send()

    return pl.pallas_call(
        body,
        out_shape=[jax.ShapeDtypeStruct((4,) + s.shape, s.dtype) for s in shards],
        in_specs=[_ANY] * n,
        out_specs=[_ANY] * n,
        scratch_shapes=[pltpu.SemaphoreType.DMA((6 * n,)), pltpu.SemaphoreType.DMA((6 * n,))],
        name="weight_gather",
    )(*shards)


def _pair_swap(gps):
    n = len(gps)

    def body(*refs):
        g_refs, o_refs = refs[:n], refs[n:2 * n]
        send_sems, recv_sems = refs[2 * n:]
        x, y, c = _mesh_pos()
        copies = []
        for t in range(n):
            half = gps[t].shape[1] // 2
            theirs = pl.ds(pl.multiple_of((1 - c) * half, 8), half)
            cp = pltpu.make_async_remote_copy(src_ref=g_refs[t].at[:, theirs], dst_ref=o_refs[t],
                                              send_sem=send_sems.at[t], recv_sem=recv_sems.at[t],
                                              device_id=(x, y, 1 - c), device_id_type=MESH)
            cp.start()
            copies.append(cp)
        for cp in copies:
            cp.wait_send()
            cp.wait_recv()

    return pl.pallas_call(
        body,
        out_shape=[jax.ShapeDtypeStruct((4, g.shape[1] // 2, g.shape[2]), g.dtype) for g in gps],
        in_specs=[_ANY] * n,
        out_specs=[_ANY] * n,
        scratch_shapes=[pltpu.SemaphoreType.DMA((n,)), pltpu.SemaphoreType.DMA((n,))],
        name="grad_pair_swap",
    )(*gps)


def _chip_scatter(parts):
    n = len(parts)

    def body(*refs):
        s_refs, o_refs = refs[:n], refs[n:2 * n]
        send_sems, recv_sems = refs[2 * n:]
        x, y, c = _mesh_pos()
        me = 2 * x + y
        chips = _other_chips(x, y)
        sends = []
        for t in range(n):
            for k, (px, py) in enumerate(chips):
                cp = pltpu.make_async_remote_copy(src_ref=s_refs[t].at[2 * px + py], dst_ref=o_refs[t].at[me],
                                                  send_sem=send_sems.at[3 * t + k], recv_sem=recv_sems.at[3 * t + k],
                                                  device_id=(px, py, c), device_id_type=MESH)
                cp.start()
                sends.append(cp)
        for t in range(n):
            for k, (px, py) in enumerate(chips):
                pltpu.make_async_remote_copy(src_ref=s_refs[t].at[me], dst_ref=o_refs[t].at[2 * px + py],
                                             send_sem=send_sems.at[3 * t + k], recv_sem=recv_sems.at[3 * t + k],
                                             device_id=(x, y, c), device_id_type=MESH).wait_recv()
        for cp in sends:
            cp.wait_send()

    return pl.pallas_call(
        body,
        out_shape=[jax.ShapeDtypeStruct(p.shape, p.dtype) for p in parts],
        in_specs=[_ANY] * n,
        out_specs=[_ANY] * n,
        scratch_shapes=[pltpu.SemaphoreType.DMA((3 * n,)), pltpu.SemaphoreType.DMA((3 * n,))],
        name="grad_chip_scatter",
    )(*parts)


def _pair_join(fs):
    n = len(fs)

    def body(*refs):
        f_refs, o_refs = refs[:n], refs[n:2 * n]
        send_sems, recv_sems = refs[2 * n:]
        x, y, c = _mesh_pos()
        sends = []
        for t in range(n):
            cp = pltpu.make_async_remote_copy(src_ref=f_refs[t].at[c], dst_ref=o_refs[t].at[c], send_sem=send_sems.at[t],
                                              recv_sem=recv_sems.at[t], device_id=(x, y, 1 - c), device_id_type=MESH)
            cp.start()
            sends.append(cp)
        for t in range(n):
            pltpu.make_async_remote_copy(src_ref=f_refs[t].at[c], dst_ref=o_refs[t].at[1 - c], send_sem=send_sems.at[t],
                                         recv_sem=recv_sems.at[t], device_id=(x, y, c), device_id_type=MESH).wait_recv()
        for cp in sends:
            cp.wait_send()

    return pl.pallas_call(
        body,
        out_shape=[jax.ShapeDtypeStruct(f.shape, f.dtype) for f in fs],
        in_specs=[_ANY] * n,
        out_specs=[_ANY] * n,
        input_output_aliases={t: t for t in range(n)},
        scratch_shapes=[pltpu.SemaphoreType.DMA((n,)), pltpu.SemaphoreType.DMA((n,))],
        name="grad_pair_join",
    )(*fs)


def _rope_tables(Lp):
    inv = 1.0 / (ROPE_BASE ** (jnp.arange(0, ROPE, 2, dtype=F32) / ROPE))
    ang = (jnp.arange(Lp, dtype=F32) - FRONT)[:, None] * inv[None, :]
    cs, sn = jnp.cos(ang), jnp.sin(ang)
    return jnp.tile(cs, (1, 4)), jnp.concatenate([-sn, sn, -sn, sn], axis=1)


def _local_step(x, loss_target, meta, norm_g, w_in, gate_w, gate_b, gla_norm_g, gla_proj, q_norm_g, w_uq,
                kv_norm_g, w_ukv, mla_proj, w_out, final_norm_g):
    B, seq, _ = x.shape
    Lp = HEAD_ROWS + seq
    T = B * Lp
    tr = _div_tile(Lp, 544, 16)
    tq = _div_tile(T, 1024, QB)

    cuts = np.cumsum((0,) + SPLITS)
    shard_w = IN_WIDTH // 4

    def w_cols(i, width=None):
        parts = []
        for j in range(4):
            a, b = max(cuts[i], j * shard_w), min(cuts[i + 1], (j + 1) * shard_w)
            if a < b:
                parts.append(w_in[j][:, a - j * shard_w:b - j * shard_w])
        if width is not None:
            parts.append(jnp.zeros((D, width - (cuts[i + 1] - cuts[i])), w_in.dtype))
        return parts

    i_q, i_k, i_v, i_lr, i_z, i_cq, i_ckv, i_kr, i_mz, i_gg, i_gm = range(11)
    wA = jnp.concatenate(sum([w_cols(i) for i in (i_v, i_z, i_mz, i_gg, i_gm, i_q, i_k)], []), axis=1)
    wB = jnp.concatenate(w_cols(i_cq) + w_cols(i_ckv) + w_cols(i_lr, 128) + w_cols(i_kr, 128), axis=1)
    wg = jnp.pad(gate_w, ((0, 128 - GLA_RANK), (0, 0)))
    wuq2 = jnp.pad(w_uq.reshape(Q_RANK, MLA_H, MLA_QK), ((0, 0), (0, 0), (0, 256 - MLA_QK))).reshape(Q_RANK, 2048)
    gn4 = jnp.tile(gla_norm_g, (1, GLA_H))
    cos_t, sin_t = _rope_tables(Lp)

    u = _rms_in(x, meta, norm_g, B, Lp)
    projA = _mm(u, wA, name="in_proj_a", out_dtype=BF16, tm=tq, tn=1024, tk=D)
    projB = _mm(u, wB, name="in_proj_b", out_dtype=BF16, tm=tq, tn=640, tk=D)
    oa, ya_in, ssave = _gla_fwd(projA, projB, wg, gate_b, gn4, B, Lp)
    ya = _mm(ya_in, gla_proj, name="gla_proj", out_dtype=BF16, tm=tq, tn=D, tk=D)
    q_att, k_att, v_att, cqn, ckvn = _mla_prep(projB, cos_t, sin_t, q_norm_g, kv_norm_g, wuq2, w_ukv, B, Lp, tr)
    ob, yb_in, lse_c = _attn_fwd(q_att, k_att, v_att, projA, B, Lp)
    yb = _mm(yb_in, mla_proj, name="mla_proj", out_dtype=BF16, tm=tq, tn=D, tk=D)
    dh1, dh1_b, merged, loss, d_gf = _out_proj_loss(x, meta, projA, ya, yb, w_out, final_norm_g.reshape(1, D),
                                                     loss_target, B, Lp)

    g_w_out = _mm(merged, dh1_b, name="dw_out", trans_a=True, tm=D, tn=D, tk=tq)
    dya, dyb, dA = _merge_bwd(dh1_b, w_out, projA, ya, yb, tr)
    g_gla_proj = _mm(ya_in, dya, name="dw_gla_proj", trans_a=True, tm=D, tn=D, tk=tq)
    g_mla_proj = _mm(yb_in, dyb, name="dw_mla_proj", trans_a=True, tm=D, tn=D, tk=tq)
    doa, dBz, d_gn = _gla_out_bwd(dya, gla_proj, oa, projA, gn4, tr)
    dC, g_wg, d_bg = _gla_bwd(projA, projB, ssave, doa, wg, gate_b, B, Lp)
    do, dDz, delta_c = _attn_bwd_pre(dyb, mla_proj, projA, ob, B, Lp)
    dq, dk, dv = _attn_bwd(q_att, k_att, v_att, do, lse_c, delta_c, B, Lp)
    dqf, dkvf, dE, d_gq, d_gkv = _mla_bwd_post(dq, dk, dv, projB, cos_t, sin_t, q_norm_g, kv_norm_g,
                                                wuq2, w_ukv, B, Lp, tr)
    g_wuq2 = _mm(cqn, dqf, name="dw_uq", trans_a=True, tm=Q_RANK, tn=2048, tk=tq)
    g_wukv = _mm(ckvn, dkvf, name="dw_ukv", trans_a=True, tm=KV_RANK, tn=2048, tk=tq)
    dparts = [dA, dBz, dC, dDz, dE]
    g_in = [_mm(u, dp, name="dw_in_%d" % i, trans_a=True, tm=D, tn=_div_tile(dp.shape[1], 1024, 256), tk=tq)
            for i, dp in enumerate(dparts)]
    grad_x, d_meta, d_ng = _in_proj_bwd(x, meta, dh1, dA, dBz, dC, dDz, dE, wA, wB, norm_g, B, Lp)

    gA, gBz, gC, gDz, gE = g_in
    src = [(gC, 1024), (gC, 1536), (gC, 0), (gC, 2048), (gBz, 0), (gE, 0), (gE, Q_RANK), (gE, 384), (gDz, 0),
           (gA, 0), (gA, D)]
    owners = []
    for j in range(4):
        parts = []
        for i, (arr, off) in enumerate(src):
            a, b = max(cuts[i], j * shard_w), min(cuts[i + 1], (j + 1) * shard_w)
            if a < b:
                parts.append(arr[:, off + a - cuts[i]:off + b - cuts[i]])
        owners.append(jnp.concatenate(parts, axis=1))
    g_w_in = jnp.stack(owners)
    g_wuq = g_wuq2.reshape(Q_RANK, MLA_H, 256)[:, :, :MLA_QK].reshape(Q_RANK, MLA_H * MLA_QK)
    grads = dict(w_in=g_w_in, gla_gate_w=g_wg[:GLA_RANK], gla_proj=g_gla_proj, mla_w_uq=g_wuq, mla_w_ukv=g_wukv,
                 mla_proj=g_mla_proj, w_out=g_w_out, meta_tokens=d_meta, norm_g=d_ng, gla_gate_b=d_bg,
                 gla_norm_g=d_gn, mla_q_norm_g=d_gq, mla_kv_norm_g=d_gkv, final_norm_g=d_gf)
    return loss[0, 0], grad_x, grads


_MATS = ("w_in", "gla_gate_w", "gla_proj", "mla_w_uq", "mla_w_ukv", "mla_proj", "w_out")
_ROW_SHARDED = ("gla_proj", "mla_proj", "w_out")
_ORDER = ("meta_tokens", "norm_g", "w_in", "gla_gate_w", "gla_gate_b", "gla_norm_g", "gla_proj", "mla_q_norm_g",
          "mla_w_uq", "mla_kv_norm_g", "mla_w_ukv", "mla_proj", "w_out", "final_norm_g")
WIRE_BF16_MIN_ELEMS = 128 * 128
SMALL_PACK_ROWS = 16


def _pack_small(d):
    rows = [jnp.pad(d[n].reshape(1, size), ((0, 0), (0, D - size))) for n, size in SMALL]
    return jnp.pad(jnp.concatenate(rows, axis=0), ((0, SMALL_PACK_ROWS - len(rows)), (0, 0)))


def _unpack_small(packed):
    return {n: packed[i, :size] for i, (n, size) in enumerate(SMALL)}


def kernel(x, meta_tokens, norm_g, w_in, gla_gate_w, gla_gate_b, gla_norm_g, gla_proj, mla_q_norm_g, mla_w_uq, mla_kv_norm_g, mla_w_ukv, mla_proj, w_out, final_norm_g, loss_target, m_meta_tokens, m_norm_g, m_w_in, m_gla_gate_w, m_gla_gate_b, m_gla_norm_g, m_gla_proj, m_mla_q_norm_g, m_mla_w_uq, m_mla_kv_norm_g, m_mla_w_ukv, m_mla_proj, m_w_out, m_final_norm_g, v_meta_tokens, v_norm_g, v_w_in, v_gla_gate_w, v_gla_gate_b, v_gla_norm_g, v_gla_proj, v_mla_q_norm_g, v_mla_w_uq, v_mla_kv_norm_g, v_mla_w_ukv, v_mla_proj, v_w_out, v_final_norm_g):
    w = dict(meta_tokens=meta_tokens, norm_g=norm_g, w_in=w_in[0], gla_gate_w=gla_gate_w[0], gla_gate_b=gla_gate_b,
             gla_norm_g=gla_norm_g, gla_proj=gla_proj[0], mla_q_norm_g=mla_q_norm_g, mla_w_uq=mla_w_uq[0],
             mla_kv_norm_g=mla_kv_norm_g, mla_w_ukv=mla_w_ukv[0], mla_proj=mla_proj[0], w_out=w_out[0],
             final_norm_g=final_norm_g)
    mom = dict(meta_tokens=m_meta_tokens, norm_g=m_norm_g, w_in=m_w_in[0], gla_gate_w=m_gla_gate_w[0],
               gla_gate_b=m_gla_gate_b, gla_norm_g=m_gla_norm_g, gla_proj=m_gla_proj[0], mla_q_norm_g=m_mla_q_norm_g,
               mla_w_uq=m_mla_w_uq[0], mla_kv_norm_g=m_mla_kv_norm_g, mla_w_ukv=m_mla_w_ukv[0], mla_proj=m_mla_proj[0],
               w_out=m_w_out[0], final_norm_g=m_final_norm_g)
    var = dict(meta_tokens=v_meta_tokens, norm_g=v_norm_g, w_in=v_w_in[0], gla_gate_w=v_gla_gate_w[0],
               gla_gate_b=v_gla_gate_b, gla_norm_g=v_gla_norm_g, gla_proj=v_gla_proj[0], mla_q_norm_g=v_mla_q_norm_g,
               mla_w_uq=v_mla_w_uq[0], mla_kv_norm_g=v_mla_kv_norm_g, mla_w_ukv=v_mla_w_ukv[0], mla_proj=v_mla_proj[0],
               w_out=v_w_out[0], final_norm_g=v_final_norm_g)
    out_shapes = {n: a.shape for n, a in zip(_ORDER, (meta_tokens, norm_g, w_in, gla_gate_w, gla_gate_b, gla_norm_g,
                                                     gla_proj, mla_q_norm_g, mla_w_uq, mla_kv_norm_g, mla_w_ukv,
                                                     mla_proj, w_out, final_norm_g))}

    me = (2 * lax.axis_index("x") + lax.axis_index("y")).astype(jnp.int32)
    shards = [w[n].astype(BF16) for n in _MATS] + [meta_tokens]
    gathered = [lax.dynamic_update_slice(gth, own[None], (me, 0, 0))
                for gth, own in zip(_weight_gather(shards), shards)]
    full = {}
    for name, gth in zip(_MATS, gathered):
        if name == "w_in":
            full[name] = gth
        elif name in _ROW_SHARDED:
            full[name] = gth.reshape(4 * gth.shape[1], gth.shape[2])
        else:
            full[name] = gth.transpose(1, 0, 2).reshape(gth.shape[1], 4 * gth.shape[2])
    meta_full = gathered[-1].transpose(1, 0, 2).reshape(N_META, D)

    loss_local, grad_x, g = _local_step(
        x, loss_target, meta_full, norm_g, full["w_in"], full["gla_gate_w"], gla_gate_b, gla_norm_g, full["gla_proj"],
        mla_q_norm_g, full["mla_w_uq"], mla_kv_norm_g, full["mla_w_ukv"], full["mla_proj"], full["w_out"], final_norm_g)
    loss = lax.psum(loss_local, ("x", "y", "c"))

    def by_owner(name, arr):
        if name == "w_in":
            return arr
        if name in _ROW_SHARDED:
            return arr.reshape(4, arr.shape[0] // 4, arr.shape[1])
        return arr.reshape(arr.shape[0], 4, arr.shape[1] // 4).transpose(1, 0, 2)

    names = _MATS + ("meta_tokens",)
    gps = [by_owner(n, g[n]) for n in names] + [jnp.broadcast_to(_pack_small(g)[None], (4, SMALL_PACK_ROWS, D))]
    recvs = _pair_swap(gps)
    c_idx = lax.axis_index("c").astype(jnp.int32).reshape(1)
    s1 = [_pair_add_big(gps[0], recvs[0], c_idx)] + list(_pair_add_small(gps[1:], recvs[1:]))
    landed = _chip_scatter(s1)
    pos = jnp.stack([c_idx[0], me])
    halves = [_sum_chips_big(landed[0], s1[0], pos)] + list(_sum_chips_small(landed[1:], s1[1:]))
    shapes = [out_shapes[n] for n in names] + [(SMALL_PACK_ROWS, D)]
    g_red = [j.reshape(s) for j, s in zip(_pair_join(halves), shapes)]

    tens = lambda d: [d[n].reshape(out_shapes[n]) for n in names] + [_pack_small(d)]
    w_t, m_t, v_t = tens(w), tens(mom), tens(var)
    big = _adamw_big(w_t[0], g_red[0], m_t[0], v_t[0])
    rest = _adamw_small(w_t[1:], g_red[1:], m_t[1:], v_t[1:])
    k = len(names)
    results = {"grad": g_red}
    for i, kind in enumerate(("delta", "new_m", "new_v")):
        results[kind] = [big[i]] + list(rest[i * k:(i + 1) * k])

    outs = []
    for kind in ("grad", "delta", "new_m", "new_v"):
        vals = dict(zip(names, results[kind][:-1]))
        vals.update(_unpack_small(results[kind][-1]))
        outs += [vals[n].reshape(out_shapes[n]) for n in _ORDER]
    return (loss, grad_x, *outs)
```

```python
import functools
import math

import jax
import jax.numpy as jnp
import numpy as np
from jax import lax
from jax.experimental import pallas as pl
from jax.experimental.pallas import tpu as pltpu

F32 = jnp.float32
BF16 = jnp.bfloat16

D = 1024
N_META = 16
QB = 256
FRONT = QB - N_META
HEAD_ROWS = FRONT + N_META
assert FRONT % 64 == 48
EPS = 1e-6

GLA_H, GLA_DK, GLA_DV, GLA_RANK, GLA_C = 4, 128, 256, 16, 64
GLA_NORMALIZER = 16.0
GLA_KW, GLA_VW = GLA_H * GLA_DK, GLA_H * GLA_DV
MLA_H, NOPE, ROPE, MLA_DV, Q_RANK, KV_RANK = 8, 128, 64, 128, 256, 128
MLA_QK = NOPE + ROPE
ROPE_BASE = 10000.0
SPLITS = (GLA_KW, GLA_KW, GLA_VW, GLA_RANK, GLA_VW, Q_RANK, KV_RANK, ROPE, MLA_H * MLA_DV, D, D)
IN_WIDTH = sum(SPLITS)

ADAM_LR, ADAM_B1, ADAM_B2, ADAM_EPS, ADAM_WD, ADAM_STEP = 0.001, 0.9, 0.999, 1e-08, 0.01, 10

LANES = 128
VMEM_CAP_V7X = 56 * 1024 * 1024
MESH = pl.DeviceIdType.MESH
NEG = -1e30

SMALL = (("norm_g", D), ("gla_gate_b", GLA_KW), ("gla_norm_g", GLA_DV), ("mla_q_norm_g", Q_RANK),
         ("mla_kv_norm_g", KV_RANK), ("final_norm_g", D))


def _div_tile(n, target, mult):
    best = None
    for d in range(mult, min(n, target) + 1, mult):
        if n % d == 0:
            best = d
    assert best is not None, (n, target, mult)
    return best


def _params(sem, block_bytes, scratch_bytes=0):
    est = 2 * block_bytes + scratch_bytes + 12 * 1024 * 1024
    return pltpu.CompilerParams(dimension_semantics=sem, vmem_limit_bytes=int(min(max(est, 24 * 1024 * 1024), VMEM_CAP_V7X)))


def _nbytes(shape, dtype):
    return int(np.prod(shape)) * jnp.dtype(dtype).itemsize


def _sigmoid(x):
    return 1.0 / (1.0 + jnp.exp(-x))


def _nt(a, b):
    return lax.dot_general(a, b, (((1,), (1,)), ((), ())), preferred_element_type=F32)


def _tn(a, b):
    return lax.dot_general(a, b, (((0,), (0,)), ((), ())), preferred_element_type=F32)


def _nn(a, b):
    return jnp.dot(a, b, preferred_element_type=F32)


def _split3(x):
    a = x.astype(BF16)
    r = x - a.astype(F32)
    b = r.astype(BF16)
    c = (r - b.astype(F32)).astype(BF16)
    return a, b, c


def _mm(a, b, *, name, trans_a=False, trans_b=False, out_dtype=F32, tm, tn, tk):
    assert not (trans_a and trans_b)
    if trans_a:
        K, M = a.shape
    else:
        M, K = a.shape
    N = b.shape[0] if trans_b else b.shape[1]
    assert (b.shape[1] if trans_b else b.shape[0]) == K
    assert M % tm == 0 and N % tn == 0 and K % tk == 0, (name, M, N, K, tm, tn, tk)
    nk = K // tk

    def body(a_ref, b_ref, o_ref, *scratch):
        av = a_ref[...].astype(BF16)
        bv = b_ref[...].astype(BF16)
        prod = _tn(av, bv) if trans_a else (_nt(av, bv) if trans_b else _nn(av, bv))
        if nk == 1:
            o_ref[...] = prod.astype(out_dtype)
        else:
            acc = scratch[0]
            k = pl.program_id(2)

            @pl.when(k == 0)
            def _():
                acc[...] = prod

            @pl.when(k > 0)
            def _():
                acc[...] += prod

            @pl.when(k == nk - 1)
            def _():
                o_ref[...] = acc[...].astype(out_dtype)

    if trans_a:
        a_spec = pl.BlockSpec((tk, tm), lambda i, j, k: (k, i))
    else:
        a_spec = pl.BlockSpec((tm, tk), lambda i, j, k: (i, k))
    if trans_b:
        b_spec = pl.BlockSpec((tn, tk), lambda i, j, k: (j, k))
    else:
        b_spec = pl.BlockSpec((tk, tn), lambda i, j, k: (k, j))
    blocks = (_nbytes((tm, tk), a.dtype) + _nbytes((tk, tn), b.dtype) + _nbytes((tm, tn), out_dtype))
    scratch = [pltpu.VMEM((tm, tn), F32)] if nk > 1 else []
    return pl.pallas_call(
        body,
        out_shape=jax.ShapeDtypeStruct((M, N), out_dtype),
        grid=(M // tm, N // tn, nk),
        in_specs=[a_spec, b_spec],
        out_specs=pl.BlockSpec((tm, tn), lambda i, j, k: (i, j)),
        scratch_shapes=scratch,
        compiler_params=_params(("parallel", "parallel", "arbitrary"), blocks + _nbytes((tm, tn), F32),
                                _nbytes((tm, tn), F32) if nk > 1 else 0),
        name=name,
    )(a, b)


def _h_tile(j, x_ref, meta_ref):
    head = jnp.concatenate([jnp.zeros((FRONT, D), F32), meta_ref[...]], axis=0)
    return jnp.where(j > 0, x_ref[0], head)


def _x_spec():
    return pl.BlockSpec((1, QB, D), lambda b, j: (b, jnp.maximum(j - 1, 0), 0))


def _rms_in(x, meta, g, B, Lp):
    T = B * Lp
    NQ = Lp // QB

    def body(x_ref, meta_ref, g_ref, u_ref):
        h = _h_tile(pl.program_id(1), x_ref, meta_ref)
        r = lax.rsqrt(jnp.mean(h * h, axis=-1, keepdims=True) + EPS)
        u_ref[...] = (h * r * g_ref[...]).astype(BF16)

    return pl.pallas_call(
        body,
        out_shape=jax.ShapeDtypeStruct((T, D), BF16),
        grid=(B, NQ),
        in_specs=[_x_spec(), pl.BlockSpec((N_META, D), lambda b, j: (0, 0)), pl.BlockSpec((1, D), lambda b, j: (0, 0))],
        out_specs=pl.BlockSpec((QB, D), lambda b, j: (b * NQ + j, 0)),
        compiler_params=_params(("parallel", "parallel"), _nbytes((QB, D), F32) * 2),
        name="rms_in",
    )(x, meta, g)


def _gla_gate(lr, wg, bg, valid):
    pre = _nn(lr.astype(BF16), wg) + bg
    logsig = jnp.minimum(pre, 0.0) - jnp.log(1.0 + jnp.exp(-jnp.abs(pre)))
    return pre, jnp.where(valid, logsig / GLA_NORMALIZER, 0.0)


def _tri_masks():
    ri = lax.broadcasted_iota(jnp.int32, (GLA_C, GLA_C), 0)
    ci = lax.broadcasted_iota(jnp.int32, (GLA_C, GLA_C), 1)
    return ci <= ri, ci >= ri


def _cumsum_rows(x, ones_mask):
    w = jnp.where(ones_mask, 1.0, 0.0).astype(BF16)
    a, b, c = _split3(x)
    return _nn(w, a) + _nn(w, b) + _nn(w, c)


def _gla_fwd(projA, projB, wg, bg, gn4, B, Lp):
    T = B * Lp
    NC = Lp // GLA_C
    C = GLA_C
    scale = GLA_DK ** -0.5

    def body(q_ref, k_ref, v_ref, lr_ref, z_ref, wg_ref, bg_ref, gn_ref, oa_ref, ya_ref, ssave_ref, st_ref):
        n = pl.program_id(0)

        @pl.when(n == 0)
        def _():
            st_ref[...] = jnp.zeros_like(st_ref)

        pos = n * C + lax.broadcasted_iota(jnp.int32, (C, 1), 0)
        lower, _ = _tri_masks()
        is_last = lax.broadcasted_iota(jnp.int32, (C, 1), 0) == C - 1
        for b in range(B):
            ssave_ref[b, 0] = st_ref[b]
            _, glog = _gla_gate(lr_ref[b], wg_ref[...], bg_ref[...], pos >= FRONT)
            bcum = _cumsum_rows(glog, lower)
            for h in range(GLA_H):
                ks = slice(h * GLA_DK, (h + 1) * GLA_DK)
                vs = slice(h * GLA_DV, (h + 1) * GLA_DV)
                bh = bcum[:, ks]
                blast = jnp.sum(jnp.where(is_last, bh, 0.0), axis=0, keepdims=True)
                qh = q_ref[b, :, ks].astype(F32) * scale
                kh = k_ref[b, :, ks].astype(F32)
                qe = (qh * jnp.exp(bh)).astype(BF16)
                ke = (kh * jnp.exp(-bh)).astype(BF16)
                kl = (kh * jnp.exp(blast - bh)).astype(BF16)
                vh = v_ref[b, :, vs].astype(BF16)
                a = jnp.where(lower, _nt(qe, ke), 0.0).astype(BF16)
                st = st_ref[b, h]
                o = _nn(a, vh) + _nt(qe, st.astype(BF16))
                st_ref[b, h] = st * jnp.exp(blast) + _tn(vh, kl)
                oa_ref[b, :, vs] = o
                on = o * lax.rsqrt(jnp.mean(o * o, axis=-1, keepdims=True) + EPS) * gn_ref[:, vs]
                z = z_ref[b, :, vs].astype(F32)
                ya_ref[b, :, vs] = (on * (z * _sigmoid(z))).astype(BF16)

    blocks = B * (_nbytes((C, 512), F32) * 2 + _nbytes((C, 1024), F32) * 3 + _nbytes((C, 1024), BF16)
                  + _nbytes((GLA_H, GLA_DV, GLA_DK), F32)) + _nbytes((128, 512), BF16)
    state = _nbytes((B, GLA_H, GLA_DV, GLA_DK), F32)
    pa = projA.reshape(B, Lp, projA.shape[1])
    oa, ya, ssave = pl.pallas_call(
        body,
        out_shape=(jax.ShapeDtypeStruct((B, Lp, GLA_VW), F32), jax.ShapeDtypeStruct((B, Lp, GLA_VW), BF16),
                   jax.ShapeDtypeStruct((B, NC, GLA_H, GLA_DV, GLA_DK), F32)),
        grid=(NC,),
        in_specs=[
            pl.BlockSpec((B, C, 512), lambda n: (0, n, 10)),
            pl.BlockSpec((B, C, 512), lambda n: (0, n, 11)),
            pl.BlockSpec((B, C, 1024), lambda n: (0, n, 0)),
            pl.BlockSpec((B, C, 128), lambda n: (0, n, 3)),
            pl.BlockSpec((B, C, 1024), lambda n: (0, n, 1)),
            pl.BlockSpec((128, 512), lambda n: (0, 0)),
            pl.BlockSpec((1, 512), lambda n: (0, 0)),
            pl.BlockSpec((1, 1024), lambda n: (0, 0)),
        ],
        out_specs=(pl.BlockSpec((B, C, 1024), lambda n: (0, n, 0)),
                   pl.BlockSpec((B, C, 1024), lambda n: (0, n, 0)),
                   pl.BlockSpec((B, 1, GLA_H, GLA_DV, GLA_DK), lambda n: (0, n, 0, 0, 0))),
        scratch_shapes=[pltpu.VMEM((B, GLA_H, GLA_DV, GLA_DK), F32)],
        compiler_params=_params(("arbitrary",), blocks, state),
        name="gla_fwd",
    )(pa, pa, pa, projB.reshape(B, Lp, projB.shape[1]), pa, wg, bg, gn4)
    return oa.reshape(T, GLA_VW), ya.reshape(T, GLA_VW), ssave


def _swap_halves(x):
    lane = lax.broadcasted_iota(jnp.int32, x.shape, 1)
    return jnp.where((lane % 64) < 32, pltpu.roll(x, 96, 1), pltpu.roll(x, 32, 1))


def _mla_prep(projB, cos_t, sin_t, gq, gkv, wuq2, wukv, B, Lp, tr):
    T = B * Lp
    nt = Lp // tr
    HW = 2 * LANES

    def body(pb_ref, cos_ref, sin_ref, gq_ref, gkv_ref, wuq_ref, wukv_ref, q_ref, k_ref, v_ref, cqn_ref, ckvn_ref):
        cq = pb_ref[:, 0:Q_RANK].astype(F32)
        ckv = pb_ref[:, Q_RANK:Q_RANK + KV_RANK].astype(F32)
        kr = pb_ref[:, 512:640].astype(F32)
        cqn = (cq * lax.rsqrt(jnp.mean(cq * cq, axis=-1, keepdims=True) + EPS) * gq_ref[...]).astype(BF16)
        ckvn = (ckv * lax.rsqrt(jnp.mean(ckv * ckv, axis=-1, keepdims=True) + EPS) * gkv_ref[...]).astype(BF16)
        cqn_ref[...] = cqn
        ckvn_ref[...] = ckvn
        qf = _nn(cqn, wuq_ref[...])
        kvf = _nn(ckvn, wukv_ref[...])
        cs = cos_ref[...]
        sn = sin_ref[...]
        rope = lambda t: t * cs + _swap_halves(t) * sn
        kr_r = rope(kr).astype(BF16)
        for h in range(MLA_H):
            q_ref[:, h * HW:h * HW + LANES] = qf[:, h * HW:h * HW + LANES].astype(BF16)
            q_ref[:, h * HW + LANES:(h + 1) * HW] = rope(qf[:, h * HW + LANES:(h + 1) * HW]).astype(BF16)
            k_ref[:, h * HW:h * HW + LANES] = kvf[:, h * HW:h * HW + LANES].astype(BF16)
            k_ref[:, h * HW + LANES:(h + 1) * HW] = kr_r
            v_ref[:, h * MLA_DV:(h + 1) * MLA_DV] = kvf[:, h * HW + LANES:(h + 1) * HW].astype(BF16)

    blocks = (_nbytes((tr, 640), F32) + 2 * _nbytes((tr, 128), F32) + _nbytes((Q_RANK, 2048), BF16)
              + _nbytes((KV_RANK, 2048), BF16) + _nbytes((tr, 2048 * 2 + 1024 + 384), BF16)
              + 2 * _nbytes((tr, 2048), F32))
    return pl.pallas_call(
        body,
        out_shape=(jax.ShapeDtypeStruct((T, MLA_H * HW), BF16), jax.ShapeDtypeStruct((T, MLA_H * HW), BF16),
                   jax.ShapeDtypeStruct((T, MLA_H * MLA_DV), BF16), jax.ShapeDtypeStruct((T, Q_RANK), BF16),
                   jax.ShapeDtypeStruct((T, KV_RANK), BF16)),
        grid=(B, nt),
        in_specs=[
            pl.BlockSpec((tr, 640), lambda b, j: (b * nt + j, 0)),
            pl.BlockSpec((tr, 128), lambda b, j: (j, 0)),
            pl.BlockSpec((tr, 128), lambda b, j: (j, 0)),
            pl.BlockSpec((1, Q_RANK), lambda b, j: (0, 0)),
            pl.BlockSpec((1, KV_RANK), lambda b, j: (0, 0)),
            pl.BlockSpec((Q_RANK, 2048), lambda b, j: (0, 0)),
            pl.BlockSpec((KV_RANK, 2048), lambda b, j: (0, 0)),
        ],
        out_specs=(pl.BlockSpec((tr, 2048), lambda b, j: (b * nt + j, 0)),
                   pl.BlockSpec((tr, 2048), lambda b, j: (b * nt + j, 0)),
                   pl.BlockSpec((tr, 1024), lambda b, j: (b * nt + j, 0)),
                   pl.BlockSpec((tr, Q_RANK), lambda b, j: (b * nt + j, 0)),
                   pl.BlockSpec((tr, KV_RANK), lambda b, j: (b * nt + j, 0))),
        compiler_params=_params(("parallel", "parallel"), blocks),
        name="mla_prep",
    )(projB, cos_t, sin_t, gq, gkv, wuq2, wukv)


def _attn_mask(row, col):
    return (col <= row) & ((col >= FRONT) | (row < FRONT))


def _attn_fwd(q_att, k_att, v_att, projA, B, Lp):
    T = B * Lp
    NQ = Lp // QB
    HW = 2 * LANES
    scale = 1.0 / math.sqrt(MLA_QK)

    def body(q_ref, k_ref, v_ref, mz_ref, o_ref, yb_ref, lsec_ref, m_ref, l_ref, acc_ref):
        qi = pl.program_id(1)
        m_ref[...] = jnp.full(m_ref.shape, NEG, F32)
        l_ref[...] = jnp.zeros_like(l_ref)
        acc_ref[...] = jnp.zeros_like(acc_ref)
        row = qi * QB + lax.broadcasted_iota(jnp.int32, (QB, QB), 0)
        coli = lax.broadcasted_iota(jnp.int32, (QB, QB), 1)

        def step(kj, carry):
            off = pl.multiple_of(kj * QB, QB)
            ok = _attn_mask(row, kj * QB + coli)
            for h in range(MLA_H):
                q = q_ref[:, h * HW:(h + 1) * HW]
                kb = k_ref[pl.ds(off, QB), h * HW:(h + 1) * HW]
                vb = v_ref[pl.ds(off, QB), h * MLA_DV:(h + 1) * MLA_DV]
                s = jnp.where(ok, _nt(q, kb) * scale, NEG)
                m_old = m_ref[h]
                m_new = jnp.maximum(m_old, jnp.max(s, axis=-1, keepdims=True))
                alpha = jnp.exp(m_old - m_new)
                p = jnp.exp(s - jnp.tile(m_new, (1, QB // LANES)))
                m_ref[h] = m_new
                l_ref[h] = alpha * l_ref[h] + jnp.sum(p, axis=-1, keepdims=True)
                acc_ref[h] = alpha * acc_ref[h] + _nn(p.astype(BF16), vb)
            return carry

        lax.fori_loop(0, qi + 1, step, 0)
        for h in range(MLA_H):
            hs = slice(h * MLA_DV, (h + 1) * MLA_DV)
            l = l_ref[h]
            o = acc_ref[h] / l
            o_ref[:, hs] = o
            z = mz_ref[:, hs].astype(F32)
            yb_ref[:, hs] = (o * (z * _sigmoid(z))).astype(BF16)
            lse = m_ref[h] + jnp.log(l)
            lsec_ref[0, h, pl.ds(qi, 1), :] = jnp.transpose(lse)[0:1, :]

    blocks = (_nbytes((QB, 2048), BF16) + _nbytes((Lp, 2048), BF16) + _nbytes((Lp, 1024), BF16)
              + 2 * _nbytes((QB, 1024), F32) + _nbytes((QB, 1024), BF16) + _nbytes((MLA_H, QB, LANES), F32)
              + _nbytes((MLA_H, NQ, QB), F32))
    return pl.pallas_call(
        body,
        out_shape=(jax.ShapeDtypeStruct((T, MLA_H * MLA_DV), F32), jax.ShapeDtypeStruct((T, MLA_H * MLA_DV), BF16),
                   jax.ShapeDtypeStruct((B, MLA_H, NQ, QB), F32)),
        grid=(B, NQ),
        in_specs=[
            pl.BlockSpec((QB, MLA_H * HW), lambda b, i: (b * NQ + i, 0)),
            pl.BlockSpec((Lp, MLA_H * HW), lambda b, i: (b, 0)),
            pl.BlockSpec((Lp, MLA_H * MLA_DV), lambda b, i: (b, 0)),
            pl.BlockSpec((QB, 1024), lambda b, i: (b * NQ + i, 2)),
        ],
        out_specs=(pl.BlockSpec((QB, 1024), lambda b, i: (b * NQ + i, 0)),
                   pl.BlockSpec((QB, 1024), lambda b, i: (b * NQ + i, 0)),
                   pl.BlockSpec((1, MLA_H, NQ, QB), lambda b, i: (b, 0, 0, 0))),
        scratch_shapes=[pltpu.VMEM((MLA_H, QB, LANES), F32), pltpu.VMEM((MLA_H, QB, LANES), F32),
                        pltpu.VMEM((MLA_H, QB, MLA_DV), F32)],
        compiler_params=_params(("parallel", "arbitrary"), blocks, 3 * _nbytes((MLA_H, QB, LANES), F32)),
        name="attn_fwd",
    )(q_att, k_att, v_att, projA)


def _out_proj_loss(x, meta, projA, ya, yb, w_out, gf, tgt, B, Lp):
    T = B * Lp
    NQ = Lp // QB

    def body(x_ref, meta_ref, gg_ref, gm_ref, ya_ref, yb_ref, w_ref, gf_ref, t_ref,
             dh_ref, dhb_ref, mg_ref, loss_ref, dgf_ref):
        b = pl.program_id(0)
        j = pl.program_id(1)

        @pl.when((b == 0) & (j == 0))
        def _():
            loss_ref[...] = jnp.zeros_like(loss_ref)
            dgf_ref[...] = jnp.zeros_like(dgf_ref)

        f32 = lambda ref: ref[...].astype(F32)
        merged = (_sigmoid(f32(gg_ref)) * f32(ya_ref) + _sigmoid(f32(gm_ref)) * f32(yb_ref)).astype(BF16)
        mg_ref[...] = merged
        h1 = _h_tile(j, x_ref, meta_ref) + _nn(merged, w_ref[...])
        r = lax.rsqrt(jnp.mean(h1 * h1, axis=-1, keepdims=True) + EPS)
        hn = h1 * r
        gfv = gf_ref[...]
        diff = jnp.where(j > 0, hn * gfv - t_ref[0], 0.0)
        loss_ref[...] += (0.5 / D) * jnp.sum(jnp.sum(diff * diff, axis=-1, keepdims=True), axis=0, keepdims=True)
        dout = diff * (1.0 / D)
        dgf_ref[...] += jnp.sum(dout * hn, axis=0, keepdims=True)
        dhn = dout * gfv
        dh = r * (dhn - hn * jnp.mean(dhn * hn, axis=-1, keepdims=True))
        dh_ref[...] = dh
        dhb_ref[...] = dh.astype(BF16)

    rows = lambda c: pl.BlockSpec((QB, D), lambda b, j: (b * NQ + j, c))
    const = lambda s: pl.BlockSpec(s, lambda b, j: (0, 0))
    return pl.pallas_call(
        body,
        out_shape=(jax.ShapeDtypeStruct((T, D), F32), jax.ShapeDtypeStruct((T, D), BF16),
                   jax.ShapeDtypeStruct((T, D), BF16), jax.ShapeDtypeStruct((1, 1), F32),
                   jax.ShapeDtypeStruct((1, D), F32)),
        grid=(B, NQ),
        in_specs=[_x_spec(), const((N_META, D)), rows(3), rows(4), rows(0), rows(0), const((D, D)),
                  const((1, D)), _x_spec()],
        out_specs=(rows(0), rows(0), rows(0), const((1, 1)), const((1, D))),
        compiler_params=_params(("arbitrary", "arbitrary"), 10 * _nbytes((QB, D), F32)),
        name="out_proj_loss",
    )(x, meta, projA, projA, ya, yb, w_out, gf, tgt)


def _merge_bwd(dh1_b, w_out, projA, ya, yb, tr):
    T = dh1_b.shape[0]

    def body(dh_ref, w_ref, gg_ref, gm_ref, ya_ref, yb_ref, dya_ref, dyb_ref, da_ref):
        d = _nt(dh_ref[...], w_ref[...])
        sg = _sigmoid(gg_ref[...].astype(F32))
        sm = _sigmoid(gm_ref[...].astype(F32))
        dya_ref[...] = (d * sg).astype(BF16)
        dyb_ref[...] = (d * sm).astype(BF16)
        da_ref[:, 0:D] = (d * ya_ref[...].astype(F32) * (sg * (1.0 - sg))).astype(BF16)
        da_ref[:, D:2 * D] = (d * yb_ref[...].astype(F32) * (sm * (1.0 - sm))).astype(BF16)

    spec = lambda c: pl.BlockSpec((tr, D), lambda i: (i, c))
    return pl.pallas_call(
        body,
        out_shape=(jax.ShapeDtypeStruct((T, D), BF16), jax.ShapeDtypeStruct((T, D), BF16),
                   jax.ShapeDtypeStruct((T, 2 * D), BF16)),
        grid=(T // tr,),
        in_specs=[spec(0), pl.BlockSpec((D, D), lambda i: (0, 0)), spec(3), spec(4), spec(0), spec(0)],
        out_specs=(spec(0), spec(0), pl.BlockSpec((tr, 2 * D), lambda i: (i, 0))),
        compiler_params=_params(("parallel",), 8 * _nbytes((tr, D), F32)),
        name="merge_bwd",
    )(dh1_b, w_out, projA, projA, ya, yb)


def _gla_out_bwd(dya, gla_proj, oa, projA, gn4, tr):
    T = dya.shape[0]
    nsteps = T // tr

    def body(dya_ref, w_ref, oa_ref, z_ref, gn_ref, do_ref, dz_ref, dgn_ref, acc_ref):
        i = pl.program_id(0)

        @pl.when(i == 0)
        def _():
            acc_ref[...] = jnp.zeros_like(acc_ref)

        dy_all = _nt(dya_ref[...], w_ref[...])
        for h in range(GLA_H):
            vs = slice(h * GLA_DV, (h + 1) * GLA_DV)
            dy = dy_all[:, vs]
            o = oa_ref[:, vs]
            z = z_ref[:, vs].astype(F32)
            gn = gn_ref[:, vs]
            s = _sigmoid(z)
            ra = lax.rsqrt(jnp.mean(o * o, axis=-1, keepdims=True) + EPS)
            on = o * ra
            don = dy * (z * s)
            t = don * gn
            do_ref[:, vs] = (ra * (t - on * jnp.mean(t * on, axis=-1, keepdims=True))).astype(BF16)
            dz_ref[:, vs] = (dy * (on * gn) * (s * (1.0 + z * (1.0 - s)))).astype(BF16)
            acc_ref[:, vs] += jnp.sum(don * on, axis=0, keepdims=True)

        @pl.when(i == nsteps - 1)
        def _():
            a = acc_ref[...]
            dgn_ref[...] = a[:, 0:256] + a[:, 256:512] + a[:, 512:768] + a[:, 768:1024]

    spec = lambda c: pl.BlockSpec((tr, D), lambda i: (i, c))
    return pl.pallas_call(
        body,
        out_shape=(jax.ShapeDtypeStruct((T, D), BF16), jax.ShapeDtypeStruct((T, D), BF16),
                   jax.ShapeDtypeStruct((1, GLA_DV), F32)),
        grid=(nsteps,),
        in_specs=[spec(0), pl.BlockSpec((D, D), lambda i: (0, 0)), spec(0), spec(1),
                  pl.BlockSpec((1, D), lambda i: (0, 0))],
        out_specs=(spec(0), spec(0), pl.BlockSpec((1, GLA_DV), lambda i: (0, 0))),
        scratch_shapes=[pltpu.VMEM((1, D), F32)],
        compiler_params=_params(("arbitrary",), 6 * _nbytes((tr, D), F32)),
        name="gla_out_bwd",
    )(dya, gla_proj, oa, projA, gn4)


def _gla_bwd(projA, projB, ssave, doa, wg, bg, B, Lp):
    T = B * Lp
    NC = Lp // GLA_C
    C = GLA_C
    scale = GLA_DK ** -0.5
    WC = 2304

    def body(q_ref, k_ref, v_ref, lr_ref, ss_ref, do_ref, wg_ref, bg_ref, dc_ref, dwg_ref, dbg_ref, dst_ref):
        i = pl.program_id(0)
        n = NC - 1 - i

        @pl.when(i == 0)
        def _():
            dst_ref[...] = jnp.zeros_like(dst_ref)
            dwg_ref[...] = jnp.zeros_like(dwg_ref)
            dbg_ref[...] = jnp.zeros_like(dbg_ref)

        pos = n * C + lax.broadcasted_iota(jnp.int32, (C, 1), 0)
        valid = pos >= FRONT
        lower, upper = _tri_masks()
        is_last = lax.broadcasted_iota(jnp.int32, (C, 1), 0) == C - 1
        for b in range(B):
            lr = lr_ref[b]
            pre, glog = _gla_gate(lr, wg_ref[...], bg_ref[...], valid)
            bcum = _cumsum_rows(glog, lower)
            db_parts = []
            for h in range(GLA_H):
                ks = slice(h * GLA_DK, (h + 1) * GLA_DK)
                vs = slice(h * GLA_DV, (h + 1) * GLA_DV)
                bh = bcum[:, ks]
                blast = jnp.sum(jnp.where(is_last, bh, 0.0), axis=0, keepdims=True)
                eb, enb, ekl, ebl = jnp.exp(bh), jnp.exp(-bh), jnp.exp(blast - bh), jnp.exp(blast)
                qh = q_ref[b, :, ks].astype(F32) * scale
                kh = k_ref[b, :, ks].astype(F32)
                qe_f, ke_f, kl_f = qh * eb, kh * enb, kh * ekl
                qe, ke, kl = qe_f.astype(BF16), ke_f.astype(BF16), kl_f.astype(BF16)
                vh = v_ref[b, :, vs].astype(BF16)
                doh = do_ref[b, :, vs]
                st = ss_ref[b, 0, h]
                dst = dst_ref[b, h]
                st_b, dst_b = st.astype(BF16), dst.astype(BF16)
                da = jnp.where(lower, _nt(doh, vh), 0.0).astype(BF16)
                da_t = jnp.where(upper, _nt(vh, doh), 0.0).astype(BF16)
                a_t = jnp.where(upper, _nt(ke, qe), 0.0).astype(BF16)
                dqe = _nn(da, ke) + _nn(doh, st_b)
                dke = _nn(da_t, qe)
                dvh = _nn(a_t, doh) + _nt(kl, dst_b)
                dkl = _nn(vh, dst_b)
                dst_ref[b, h] = dst * ebl + _tn(doh, qe)
                deb = jnp.sum(st * dst, axis=0, keepdims=True)
                db = dqe * qe_f - dke * ke_f - dkl * kl_f
                db_last = jnp.sum(dkl * kl_f, axis=0, keepdims=True) + deb * ebl
                db_parts.append(db + jnp.where(is_last, db_last, 0.0))
                dc_ref[b, :, vs] = dvh.astype(BF16)
                dc_ref[b, :, 1024 + h * GLA_DK:1024 + (h + 1) * GLA_DK] = (dqe * eb * scale).astype(BF16)
                dc_ref[b, :, 1536 + h * GLA_DK:1536 + (h + 1) * GLA_DK] = (dke * enb + dkl * ekl).astype(BF16)
            dglog = _cumsum_rows(jnp.concatenate(db_parts, axis=1), upper)
            dpre = jnp.where(valid, dglog * (1.0 / GLA_NORMALIZER) / (1.0 + jnp.exp(pre)), 0.0)
            dpre_b = dpre.astype(BF16)
            dc_ref[b, :, 2048:2176] = _nt(dpre_b, wg_ref[...]).astype(BF16)
            dc_ref[b, :, 2176:2304] = jnp.zeros((C, 128), BF16)
            dwg_ref[...] += _tn(lr.astype(BF16), dpre_b)
            dbg_ref[...] += jnp.sum(dpre, axis=0, keepdims=True)

    blocks = B * (_nbytes((C, 512), F32) * 2 + _nbytes((C, 1024), F32) + _nbytes((C, 1024), BF16)
                  + _nbytes((GLA_H, GLA_DV, GLA_DK), F32) + _nbytes((C, WC), BF16)) + 3 * _nbytes((128, 512), F32)
    state = _nbytes((B, GLA_H, GLA_DV, GLA_DK), F32)
    pa = projA.reshape(B, Lp, projA.shape[1])
    rev = lambda i: NC - 1 - i
    dc, dwg, dbg = pl.pallas_call(
        body,
        out_shape=(jax.ShapeDtypeStruct((B, Lp, WC), BF16), jax.ShapeDtypeStruct((128, GLA_KW), F32),
                   jax.ShapeDtypeStruct((1, GLA_KW), F32)),
        grid=(NC,),
        in_specs=[
            pl.BlockSpec((B, C, 512), lambda i: (0, rev(i), 10)),
            pl.BlockSpec((B, C, 512), lambda i: (0, rev(i), 11)),
            pl.BlockSpec((B, C, 1024), lambda i: (0, rev(i), 0)),
            pl.BlockSpec((B, C, 128), lambda i: (0, rev(i), 3)),
            pl.BlockSpec((B, 1, GLA_H, GLA_DV, GLA_DK), lambda i: (0, rev(i), 0, 0, 0)),
            pl.BlockSpec((B, C, 1024), lambda i: (0, rev(i), 0)),
            pl.BlockSpec((128, 512), lambda i: (0, 0)),
            pl.BlockSpec((1, 512), lambda i: (0, 0)),
        ],
        out_specs=(pl.BlockSpec((B, C, WC), lambda i: (0, rev(i), 0)),
                   pl.BlockSpec((128, GLA_KW), lambda i: (0, 0)),
                   pl.BlockSpec((1, GLA_KW), lambda i: (0, 0))),
        scratch_shapes=[pltpu.VMEM((B, GLA_H, GLA_DV, GLA_DK), F32)],
        compiler_params=_params(("arbitrary",), blocks, state),
        name="gla_bwd",
    )(pa, pa, pa, projB.reshape(B, Lp, projB.shape[1]), ssave, doa.reshape(B, Lp, GLA_VW), wg, bg)
    return dc.reshape(T, WC), dwg, dbg


def _attn_bwd_pre(dyb, mla_proj, projA, ob, B, Lp):
    T = B * Lp
    NQ = Lp // QB

    def body(dyb_ref, w_ref, z_ref, o_ref, do_ref, dz_ref, dcol_ref):
        j = pl.program_id(1)
        dy_all = _nt(dyb_ref[...], w_ref[...])
        for h in range(MLA_H):
            hs = slice(h * MLA_DV, (h + 1) * MLA_DV)
            dy = dy_all[:, hs]
            z = z_ref[:, hs].astype(F32)
            o = o_ref[:, hs]
            s = _sigmoid(z)
            do = dy * (z * s)
            do_ref[:, hs] = do.astype(BF16)
            dz_ref[:, hs] = (dy * o * (s * (1.0 + z * (1.0 - s)))).astype(BF16)
            dl = jnp.broadcast_to(jnp.sum(do * o, axis=-1, keepdims=True), (QB, LANES))
            dcol_ref[0, h, pl.ds(j, 1), :] = jnp.transpose(dl)[0:1, :]

    rows = lambda c: pl.BlockSpec((QB, D), lambda b, j: (b * NQ + j, c))
    return pl.pallas_call(
        body,
        out_shape=(jax.ShapeDtypeStruct((T, D), BF16), jax.ShapeDtypeStruct((T, D), BF16),
                   jax.ShapeDtypeStruct((B, MLA_H, NQ, QB), F32)),
        grid=(B, NQ),
        in_specs=[rows(0), pl.BlockSpec((D, D), lambda b, j: (0, 0)), rows(2), rows(0)],
        out_specs=(rows(0), rows(0), pl.BlockSpec((1, MLA_H, NQ, QB), lambda b, j: (b, 0, 0, 0))),
        compiler_params=_params(("parallel", "arbitrary"), 6 * _nbytes((QB, D), F32)),
        name="attn_bwd_pre",
    )(dyb, mla_proj, projA, ob)


ATTN_BWD_HEADS = 8


def _attn_bwd(q_att, k_att, v_att, do, lse_c, delta_c, B, Lp):
    T = B * Lp
    NQ = Lp // QB
    G = ATTN_BWD_HEADS
    NG = MLA_H // G
    HW = 2 * LANES
    scale = 1.0 / math.sqrt(MLA_QK)

    def body(q_ref, k_ref, v_ref, do_ref, lse_ref, dl_ref, dq_out, dk_out, dv_out, dq_ref, dk_ref, dv_ref):
        kj = pl.program_id(2)

        @pl.when(kj == 0)
        def _():
            dq_ref[...] = jnp.zeros_like(dq_ref)

        dk_ref[...] = jnp.zeros_like(dk_ref)
        dv_ref[...] = jnp.zeros_like(dv_ref)
        col = kj * QB + lax.broadcasted_iota(jnp.int32, (QB, QB), 0)
        rowi = lax.broadcasted_iota(jnp.int32, (QB, QB), 1)

        def step(qi, carry):
            off = pl.multiple_of(qi * QB, QB)
            ok = _attn_mask(qi * QB + rowi, col)
            for h in range(G):
                ws = slice(h * HW, (h + 1) * HW)
                hs = slice(h * MLA_DV, (h + 1) * MLA_DV)
                qb = q_ref[pl.ds(off, QB), ws]
                dob = do_ref[pl.ds(off, QB), hs]
                kb = k_ref[:, ws]
                lse = lse_ref[0, h, pl.ds(qi, 1), :]
                delta = dl_ref[0, h, pl.ds(qi, 1), :]
                s_t = _nt(kb, qb) * scale
                p_t = jnp.where(ok, jnp.exp(s_t - lse), 0.0)
                dv_ref[:, hs] += _nn(p_t.astype(BF16), dob)
                ds_t = (p_t * (_nt(v_ref[:, hs], dob) - delta) * scale).astype(BF16)
                dk_ref[:, ws] += _nn(ds_t, qb)
                dq_ref[pl.ds(off, QB), ws] += _tn(ds_t, kb)
            return carry

        lax.fori_loop(kj, NQ, step, 0)
        dk_out[...] = dk_ref[...].astype(BF16)
        dv_out[...] = dv_ref[...].astype(BF16)

        @pl.when(kj == NQ - 1)
        def _():
            dq_out[...] = dq_ref[...].astype(BF16)

    blocks = (2 * _nbytes((Lp, G * HW), BF16) + _nbytes((Lp, G * MLA_DV), BF16) + 2 * _nbytes((QB, G * 384), BF16)
              + 2 * _nbytes((G, NQ, QB), F32))
    scratch = [pltpu.VMEM((Lp, G * HW), F32), pltpu.VMEM((QB, G * HW), F32), pltpu.VMEM((QB, G * MLA_DV), F32)]
    return pl.pallas_call(
        body,
        out_shape=(jax.ShapeDtypeStruct((T, MLA_H * HW), BF16), jax.ShapeDtypeStruct((T, MLA_H * HW), BF16),
                   jax.ShapeDtypeStruct((T, MLA_H * MLA_DV), BF16)),
        scratch_shapes=scratch,
        grid=(B, NG, NQ),
        in_specs=[
            pl.BlockSpec((Lp, G * HW), lambda b, g, j: (b, g), pipeline_mode=pl.Buffered(1)),
            pl.BlockSpec((QB, G * HW), lambda b, g, j: (b * NQ + j, g)),
            pl.BlockSpec((QB, G * MLA_DV), lambda b, g, j: (b * NQ + j, g)),
            pl.BlockSpec((Lp, G * MLA_DV), lambda b, g, j: (b, g), pipeline_mode=pl.Buffered(1)),
            pl.BlockSpec((1, G, NQ, QB), lambda b, g, j: (b, g, 0, 0)),
            pl.BlockSpec((1, G, NQ, QB), lambda b, g, j: (b, g, 0, 0)),
        ],
        out_specs=(pl.BlockSpec((Lp, G * HW), lambda b, g, j: (b, g), pipeline_mode=pl.Buffered(1)),
                   pl.BlockSpec((QB, G * HW), lambda b, g, j: (b * NQ + j, g)),
                   pl.BlockSpec((QB, G * MLA_DV), lambda b, g, j: (b * NQ + j, g))),
        compiler_params=_params(("parallel", "parallel", "arbitrary"), blocks,
                                _nbytes((Lp, G * HW), F32) + _nbytes((QB, G * 384), F32)),
        name="attn_bwd",
    )(q_att, k_att, v_att, do, lse_c, delta_c)


def _mla_bwd_post(dq, dk, dv, projB, cos_t, sin_t, gq, gkv, wuq2, wukv, B, Lp, tr):
    T = B * Lp
    nt = Lp // tr
    HW = 2 * LANES

    def body(dq_ref, dk_ref, dv_ref, pb_ref, cos_ref, sin_ref, gq_ref, gkv_ref, wuq_ref, wukv_ref,
             dqf_ref, dkvf_ref, de_ref, dgq_ref, dgkv_ref):
        first = (pl.program_id(0) == 0) & (pl.program_id(1) == 0)

        @pl.when(first)
        def _():
            dgq_ref[...] = jnp.zeros_like(dgq_ref)
            dgkv_ref[...] = jnp.zeros_like(dgkv_ref)

        cs = cos_ref[...]
        sn = sin_ref[...]
        rope_t = lambda t: t * cs + _swap_halves(t * sn)
        dkr = jnp.zeros((tr, LANES), F32)
        for h in range(MLA_H):
            dqf_ref[:, h * HW:h * HW + LANES] = dq_ref[:, h * HW:h * HW + LANES]
            dq_rope = dq_ref[:, h * HW + LANES:(h + 1) * HW].astype(F32)
            dqf_ref[:, h * HW + LANES:(h + 1) * HW] = rope_t(dq_rope).astype(BF16)
            dkvf_ref[:, h * HW:h * HW + LANES] = dk_ref[:, h * HW:h * HW + LANES]
            dkvf_ref[:, h * HW + LANES:(h + 1) * HW] = dv_ref[:, h * MLA_DV:(h + 1) * MLA_DV]
            dkr = dkr + dk_ref[:, h * HW + LANES:(h + 1) * HW].astype(F32)

        def norm_bwd(x, dn, g):
            r = lax.rsqrt(jnp.mean(x * x, axis=-1, keepdims=True) + EPS)
            xn = x * r
            t = dn * g
            return r * (t - xn * jnp.mean(t * xn, axis=-1, keepdims=True)), jnp.sum(dn * xn, axis=0, keepdims=True)

        cq = pb_ref[:, 0:Q_RANK].astype(F32)
        ckv = pb_ref[:, Q_RANK:Q_RANK + KV_RANK].astype(F32)
        dcq, dgq = norm_bwd(cq, _nt(dqf_ref[...], wuq_ref[...]), gq_ref[...])
        dckv, dgkv = norm_bwd(ckv, _nt(dkvf_ref[...], wukv_ref[...]), gkv_ref[...])
        dgq_ref[...] += dgq
        dgkv_ref[...] += dgkv
        de_ref[:, 0:Q_RANK] = dcq.astype(BF16)
        de_ref[:, Q_RANK:Q_RANK + KV_RANK] = dckv.astype(BF16)
        de_ref[:, 384:512] = rope_t(dkr).astype(BF16)

    rows = lambda w: pl.BlockSpec((tr, w), lambda b, j: (b * nt + j, 0))
    const = lambda s: pl.BlockSpec(s, lambda b, j: (0, 0))
    blocks = (2 * _nbytes((tr, 2048), F32) + _nbytes((tr, 1024), F32) + _nbytes((tr, 640), F32)
              + 2 * _nbytes((tr, 2048), BF16) + _nbytes((2048, 384), BF16) + 2 * _nbytes((tr, 2048), F32))
    return pl.pallas_call(
        body,
        out_shape=(jax.ShapeDtypeStruct((T, 2048), BF16), jax.ShapeDtypeStruct((T, 2048), BF16),
                   jax.ShapeDtypeStruct((T, 512), BF16), jax.ShapeDtypeStruct((1, Q_RANK), F32),
                   jax.ShapeDtypeStruct((1, KV_RANK), F32)),
        grid=(B, nt),
        in_specs=[rows(2048), rows(2048), rows(1024), rows(640),
                  pl.BlockSpec((tr, 128), lambda b, j: (j, 0)), pl.BlockSpec((tr, 128), lambda b, j: (j, 0)),
                  const((1, Q_RANK)), const((1, KV_RANK)), const((Q_RANK, 2048)), const((KV_RANK, 2048))],
        out_specs=(rows(2048), rows(2048), rows(512), const((1, Q_RANK)), const((1, KV_RANK))),
        compiler_params=_params(("arbitrary", "arbitrary"), blocks),
        name="mla_bwd_post",
    )(dq, dk, dv, projB, cos_t, sin_t, gq, gkv, wuq2, wukv)


def _in_proj_bwd(x, meta, dh1, dA, dBz, dC, dDz, dE, wA, wB, g, B, Lp):
    NQ = Lp // QB
    seq = x.shape[1]

    def body(x_ref, meta_ref, dh_ref, da_ref, db_ref, dc_ref, dd_ref, de_ref, wa_ref, wb_ref, g_ref,
             gx_ref, dmeta_ref, dg_ref):
        b = pl.program_id(0)
        j = pl.program_id(1)

        @pl.when((b == 0) & (j == 0))
        def _():
            dg_ref[...] = jnp.zeros_like(dg_ref)

        du = _nt(da_ref[...], wa_ref[:, 3072:5120])
        du = du + _nt(db_ref[...], wa_ref[:, 1024:2048])
        du = du + _nt(dd_ref[...], wa_ref[:, 2048:3072])
        du = du + _nt(dc_ref[:, 0:1024], wa_ref[:, 0:1024])
        du = du + _nt(dc_ref[:, 1024:2048], wa_ref[:, 5120:6144])
        du = du + _nt(dc_ref[:, 2048:2176], wb_ref[:, 384:512])
        du = du + _nt(de_ref[:, 0:384], wb_ref[:, 0:384])
        du = du + _nt(de_ref[:, 384:512], wb_ref[:, 512:640])

        x = _h_tile(j, x_ref, meta_ref)
        r = lax.rsqrt(jnp.mean(x * x, axis=-1, keepdims=True) + EPS)
        xn = x * r
        t = du * g_ref[...]
        dh0 = dh_ref[...] + r * (t - xn * jnp.mean(t * xn, axis=-1, keepdims=True))
        dg_ref[...] += jnp.sum(du * xn, axis=0, keepdims=True)
        gx_ref[0] = dh0

        @pl.when((j == 0) & (b == 0))
        def _():
            dmeta_ref[...] = dh0[FRONT:HEAD_ROWS, :]

        @pl.when((j == 0) & (b > 0))
        def _():
            dmeta_ref[...] += dh0[FRONT:HEAD_ROWS, :]

    rows = lambda w: pl.BlockSpec((QB, w), lambda b, j: (b * NQ + j, 0))
    const = lambda s: pl.BlockSpec(s, lambda b, j: (0, 0))
    widths = [a.shape[1] for a in (dA, dBz, dC, dDz, dE)]
    blocks = (sum(_nbytes((QB, w), BF16) for w in widths) + _nbytes(wA.shape, BF16) + _nbytes(wB.shape, BF16)
              + 4 * _nbytes((QB, D), F32))
    return pl.pallas_call(
        body,
        out_shape=(jax.ShapeDtypeStruct((B, seq, D), F32), jax.ShapeDtypeStruct((N_META, D), F32),
                   jax.ShapeDtypeStruct((1, D), F32)),
        grid=(B, NQ),
        in_specs=[_x_spec(), const((N_META, D)), rows(D)] + [rows(w) for w in widths]
        + [const(wA.shape), const(wB.shape), const((1, D))],
        out_specs=(_x_spec(), const((N_META, D)), const((1, D))),
        compiler_params=_params(("arbitrary", "arbitrary"), blocks),
        name="in_proj_bwd",
    )(x, meta, dh1, dA, dBz, dC, dDz, dE, wA, wB, g)


_VMEM_WHOLE = pl.BlockSpec(memory_space=pltpu.VMEM)


def _params_whole(arrays):
    total = sum(_nbytes(a.shape, a.dtype) for a in arrays)
    return pltpu.CompilerParams(vmem_limit_bytes=int(min(total + 12 * 1024 * 1024, VMEM_CAP_V7X)))


def _wire_dtype(shape):
    return BF16 if shape[-2] * shape[-1] >= WIRE_BF16_MIN_ELEMS else F32


def _pair_add_big(gp, recv, c):
    _, half, cols = recv.shape
    th = _div_tile(half, 64, 16)
    out_dtype = _wire_dtype(recv.shape)

    def body(c_ref, a_ref, b_ref, o_ref):
        o_ref[...] = (a_ref[:, 0] + b_ref[...]).astype(out_dtype)

    return pl.pallas_call(
        body,
        out_shape=jax.ShapeDtypeStruct(recv.shape, out_dtype),
        grid_spec=pltpu.PrefetchScalarGridSpec(
            num_scalar_prefetch=1,
            grid=(half // th,),
            in_specs=[pl.BlockSpec((4, 1, th, cols), lambda i, c_ref: (0, c_ref[0], i, 0)),
                      pl.BlockSpec((4, th, cols), lambda i, c_ref: (0, i, 0))],
            out_specs=pl.BlockSpec((4, th, cols), lambda i, c_ref: (0, i, 0)),
        ),
        compiler_params=_params(("parallel",), 3 * _nbytes((4, th, cols), F32)),
        name="grad_pair_add_big",
    )(c, gp.reshape(4, 2, half, cols), recv)


def _pair_add_small(gps, recvs):
    n = len(gps)

    def body(*refs):
        c = lax.axis_index("c")
        for t in range(n):
            g_ref, r_ref, o_ref = refs[t], refs[n + t], refs[2 * n + t]
            half = r_ref.shape[1]
            s = g_ref[:, pl.ds(pl.multiple_of(c * half, 8), half), :] + r_ref[...]
            o_ref[...] = s.astype(o_ref.dtype)

    return pl.pallas_call(
        body,
        out_shape=[jax.ShapeDtypeStruct(r.shape, _wire_dtype(r.shape)) for r in recvs],
        in_specs=[_VMEM_WHOLE] * (2 * n),
        out_specs=[_VMEM_WHOLE] * n,
        compiler_params=_params_whole(list(gps) + 2 * list(recvs)),
        name="grad_pair_add_small",
    )(*gps, *recvs)


def _chip_order_sum(landed_ref, own_ref, me):
    p = [jnp.where(me == k, own_ref[k], landed_ref[k]).astype(F32) for k in range(4)]
    return ((p[0] + p[1]) + p[2]) + p[3]


def _sum_chips_big(landed, own, pos):
    _, half, cols = landed.shape
    th = _div_tile(half, 64, 16)

    def body(pos_ref, l_ref, s_ref, o_ref):
        o_ref[0] = _chip_order_sum(l_ref, s_ref, pos_ref[1])

    spec = pl.BlockSpec((4, th, cols), lambda i, pos_ref: (0, i, 0))
    return pl.pallas_call(
        body,
        out_shape=jax.ShapeDtypeStruct((2, half, cols), F32),
        grid_spec=pltpu.PrefetchScalarGridSpec(
            num_scalar_prefetch=1,
            grid=(half // th,),
            in_specs=[spec, spec],
            out_specs=pl.BlockSpec((1, th, cols), lambda i, pos_ref: (pos_ref[0], i, 0)),
        ),
        compiler_params=_params(("parallel",), 3 * _nbytes((4, th, cols), F32)),
        name="grad_sum_chips_big",
    )(pos, landed, own)


def _sum_chips_small(landed, own):
    n = len(landed)

    def body(*refs):
        x, y, c = _mesh_pos()
        for t in range(n):
            refs[2 * n + t][c] = _chip_order_sum(refs[t], refs[n + t], 2 * x + y)

    return pl.pallas_call(
        body,
        out_shape=[jax.ShapeDtypeStruct((2,) + p.shape[1:], F32) for p in landed],
        in_specs=[_VMEM_WHOLE] * (2 * n),
        out_specs=[_VMEM_WHOLE] * n,
        compiler_params=_params_whole(list(landed) * 3),
        name="grad_sum_chips_small",
    )(*landed, *own)


def _adamw_update(w_ref, g_ref, m_ref, v_ref, d_ref, mo_ref, vo_ref):
    c1 = 1.0 - ADAM_B1 ** ADAM_STEP
    c2 = 1.0 - ADAM_B2 ** ADAM_STEP
    gv = g_ref[...]
    mn = ADAM_B1 * m_ref[...] + (1.0 - ADAM_B1) * gv
    vn = ADAM_B2 * v_ref[...] + (1.0 - ADAM_B2) * (gv * gv)
    mo_ref[...] = mn
    vo_ref[...] = vn
    d_ref[...] = -ADAM_LR * ((mn / c1) / (jnp.sqrt(vn / c2) + ADAM_EPS) + ADAM_WD * w_ref[...])


def _adamw_big(w, g, m, v):
    lead, (rows, cols) = w.shape[:-2], w.shape[-2:]
    assert all(n == 1 for n in lead)
    tr = _div_tile(rows, (1 << 19) // cols, 8)
    spec = pl.BlockSpec((1,) * len(lead) + (tr, cols), lambda i: (0,) * len(lead) + (i, 0))
    shp = jax.ShapeDtypeStruct(w.shape, F32)
    return pl.pallas_call(
        functools.partial(_adamw_update),
        out_shape=(shp, shp, shp),
        grid=(rows // tr,),
        in_specs=[spec] * 4,
        out_specs=(spec, spec, spec),
        compiler_params=_params(("parallel",), 7 * _nbytes((tr, cols), F32)),
        name="adamw_big",
    )(w, g, m, v)


def _adamw_small(ws, gs, ms, vs):
    n = len(ws)

    def body(*refs):
        for t in range(n):
            _adamw_update(refs[t], refs[n + t], refs[2 * n + t], refs[3 * n + t],
                          refs[4 * n + t], refs[5 * n + t], refs[6 * n + t])

    shapes = [jax.ShapeDtypeStruct(w.shape, F32) for w in ws]
    return pl.pallas_call(
        body,
        out_shape=shapes * 3,
        in_specs=[_VMEM_WHOLE] * (4 * n),
        out_specs=[_VMEM_WHOLE] * (3 * n),
        compiler_params=_params_whole(list(ws) * 7),
        name="adamw_small",
    )(*ws, *gs, *ms, *vs)


def _mesh_pos():
    return lax.axis_index("x"), lax.axis_index("y"), lax.axis_index("c")


def _other_chips(x, y):
    return [(1 - x, y), (x, 1 - y), (1 - x, 1 - y)]


_ANY = pl.BlockSpec(memory_space=pl.ANY)


PAIR_SPLIT_MIN_ROWS = 64


def _weight_gather(shards):
    n = len(shards)
    split = [s.shape[0] >= PAIR_SPLIT_MIN_ROWS for s in shards]

    def body(*refs):
        w_refs, o_refs = refs[:n], refs[n:2 * n]
        send_sems, recv_sems = refs[2 * n:]
        x, y, c = _mesh_pos()
        me = 2 * x + y
        chips = _other_chips(x, y)

        def rows_of(t, core):
            rows = shards[t].shape[0]
            if not split[t]:
                return pl.ds(0, rows)
            return pl.ds(pl.multiple_of(core * (rows // 2), 16), rows // 2)

        def landed(t, k, slot, rows, to):
            ref = o_refs[t].at[slot, rows]
            return pltpu.make_async_remote_copy(src_ref=ref, dst_ref=ref, send_sem=send_sems.at[6 * t + k],
                                                recv_sem=recv_sems.at[6 * t + k], device_id=to, device_id_type=MESH)

        sends = []
        for t in range(n):
            mine = rows_of(t, c)
            for k, (px, py) in enumerate(chips):
                cp = pltpu.make_async_remote_copy(src_ref=w_refs[t].at[mine], dst_ref=o_refs[t].at[me, mine],
                                                  send_sem=send_sems.at[6 * t + k], recv_sem=recv_sems.at[6 * t + k],
                                                  device_id=(px, py, c), device_id_type=MESH)
                cp.start()
                sends.append(cp)
        for t in range(n):
            mine = rows_of(t, c)
            for k, (px, py) in enumerate(chips):
                landed(t, k, 2 * px + py, mine, (x, y, c)).wait_recv()
                if split[t]:
                    cp = landed(t, 3 + k, 2 * px + py, mine, (x, y, 1 - c))
                    cp.start()
                    sends.append(cp)
        for t in range(n):
            if split[t]:
                for k, (px, py) in enumerate(chips):
                    landed(t, 3 + k, 2 * px + py, rows_of(t, 1 - c), (x, y, c)).wait_recv()
        for cp in sends:
            cp.wait_send()

    return pl.pallas_call(
        body,
        out_shape=[jax.ShapeDtypeStruct((4,) + s.shape, s.dtype) for s in shards],
        in_specs=[_ANY] * n,
        out_specs=[_ANY] * n,
        scratch_shapes=[pltpu.SemaphoreType.DMA((6 * n,)), pltpu.SemaphoreType.DMA((6 * n,))],
        name="weight_gather",
    )(*shards)


def _pair_swap(gps):
    n = len(gps)

    def body(*refs):
        g_refs, o_refs = refs[:n], refs[n:2 * n]
        send_sems, recv_sems = refs[2 * n:]
        x, y, c = _mesh_pos()
        copies = []
        for t in range(n):
            half = gps[t].shape[1] // 2
            theirs = pl.ds(pl.multiple_of((1 - c) * half, 8), half)
            cp = pltpu.make_async_remote_copy(src_ref=g_refs[t].at[:, theirs], dst_ref=o_refs[t],
                                              send_sem=send_sems.at[t], recv_sem=recv_sems.at[t],
                                              device_id=(x, y, 1 - c), device_id_type=MESH)
            cp.start()
            copies.append(cp)
        for cp in copies:
            cp.wait_send()
            cp.wait_recv()

    return pl.pallas_call(
        body,
        out_shape=[jax.ShapeDtypeStruct((4, g.shape[1] // 2, g.shape[2]), g.dtype) for g in gps],
        in_specs=[_ANY] * n,
        out_specs=[_ANY] * n,
        scratch_shapes=[pltpu.SemaphoreType.DMA((n,)), pltpu.SemaphoreType.DMA((n,))],
        name="grad_pair_swap",
    )(*gps)


def _chip_scatter(parts):
    n = len(parts)

    def body(*refs):
        s_refs, o_refs = refs[:n], refs[n:2 * n]
        send_sems, recv_sems = refs[2 * n:]
        x, y, c = _mesh_pos()
        me = 2 * x + y
        chips = _other_chips(x, y)
        sends = []
        for t in range(n):
            for k, (px, py) in enumerate(chips):
                cp = pltpu.make_async_remote_copy(src_ref=s_refs[t].at[2 * px + py], dst_ref=o_refs[t].at[me],
                                                  send_sem=send_sems.at[3 * t + k], recv_sem=recv_sems.at[3 * t + k],
                                                  device_id=(px, py, c), device_id_type=MESH)
                cp.start()
                sends.append(cp)
        for t in range(n):
            for k, (px, py) in enumerate(chips):
                pltpu.make_async_remote_copy(src_ref=s_refs[t].at[me], dst_ref=o_refs[t].at[2 * px + py],
                                             send_sem=send_sems.at[3 * t + k], recv_sem=recv_sems.at[3 * t + k],
                                             device_id=(x, y, c), device_id_type=MESH).wait_recv()
        for cp in sends:
            cp.wait_send()

    return pl.pallas_call(
        body,
        out_shape=[jax.ShapeDtypeStruct(p.shape, p.dtype) for p in parts],
        in_specs=[_ANY] * n,
        out_specs=[_ANY] * n,
        scratch_shapes=[pltpu.SemaphoreType.DMA((3 * n,)), pltpu.SemaphoreType.DMA((3 * n,))],
        name="grad_chip_scatter",
    )(*parts)


def _pair_join(fs):
    n = len(fs)

    def body(*refs):
        f_refs, o_refs = refs[:n], refs[n:2 * n]
        send_sems, recv_sems = refs[2 * n:]
        x, y, c = _mesh_pos()
        sends = []
        for t in range(n):
            cp = pltpu.make_async_remote_copy(src_ref=f_refs[t].at[c], dst_ref=o_refs[t].at[c], send_sem=send_sems.at[t],
                                              recv_sem=recv_sems.at[t], device_id=(x, y, 1 - c), device_id_type=MESH)
            cp.start()
            sends.append(cp)
        for t in range(n):
            pltpu.make_async_remote_copy(src_ref=f_refs[t].at[c], dst_ref=o_refs[t].at[1 - c], send_sem=send_sems.at[t],
                                         recv_sem=recv_sems.at[t], device_id=(x, y, c), device_id_type=MESH).wait_recv()
        for cp in sends:
            cp.wait_send()

    return pl.pallas_call(
        body,
        out_shape=[jax.ShapeDtypeStruct(f.shape, f.dtype) for f in fs],
        in_specs=[_ANY] * n,
        out_specs=[_ANY] * n,
        input_output_aliases={t: t for t in range(n)},
        scratch_shapes=[pltpu.SemaphoreType.DMA((n,)), pltpu.SemaphoreType.DMA((n,))],
        name="grad_pair_join",
    )(*fs)


def _rope_tables(Lp):
    inv = 1.0 / (ROPE_BASE ** (jnp.arange(0, ROPE, 2, dtype=F32) / ROPE))
    ang = (jnp.arange(Lp, dtype=F32) - FRONT)[:, None] * inv[None, :]
    cs, sn = jnp.cos(ang), jnp.sin(ang)
    return jnp.tile(cs, (1, 4)), jnp.concatenate([-sn, sn, -sn, sn], axis=1)


def _local_step(x, loss_target, meta, norm_g, w_in, gate_w, gate_b, gla_norm_g, gla_proj, q_norm_g, w_uq,
                kv_norm_g, w_ukv, mla_proj, w_out, final_norm_g):
    B, seq, _ = x.shape
    Lp = HEAD_ROWS + seq
    T = B * Lp
    tr = _div_tile(Lp, 544, 16)
    tq = _div_tile(T, 1024, QB)
    tkw = _div_tile(T, Lp, QB)

    cuts = np.cumsum((0,) + SPLITS)
    shard_w = IN_WIDTH // 4

    def w_cols(i, width=None):
        parts = []
        for j in range(4):
            a, b = max(cuts[i], j * shard_w), min(cuts[i + 1], (j + 1) * shard_w)
            if a < b:
                parts.append(w_in[j][:, a - j * shard_w:b - j * shard_w])
        if width is not None:
            parts.append(jnp.zeros((D, width - (cuts[i + 1] - cuts[i])), w_in.dtype))
        return parts

    i_q, i_k, i_v, i_lr, i_z, i_cq, i_ckv, i_kr, i_mz, i_gg, i_gm = range(11)
    wA = jnp.concatenate(sum([w_cols(i) for i in (i_v, i_z, i_mz, i_gg, i_gm, i_q, i_k)], []), axis=1)
    wB = jnp.concatenate(w_cols(i_cq) + w_cols(i_ckv) + w_cols(i_lr, 128) + w_cols(i_kr, 128), axis=1)
    wg = jnp.pad(gate_w, ((0, 128 - GLA_RANK), (0, 0)))
    wuq2 = jnp.pad(w_uq.reshape(Q_RANK, MLA_H, MLA_QK), ((0, 0), (0, 0), (0, 256 - MLA_QK))).reshape(Q_RANK, 2048)
    gn4 = jnp.tile(gla_norm_g, (1, GLA_H))
    cos_t, sin_t = _rope_tables(Lp)

    u = _rms_in(x, meta, norm_g, B, Lp)
    projA = _mm(u, wA, name="in_proj_a", out_dtype=BF16, tm=tq, tn=1024, tk=D)
    projB = _mm(u, wB, name="in_proj_b", out_dtype=BF16, tm=tq, tn=640, tk=D)
    oa, ya_in, ssave = _gla_fwd(projA, projB, wg, gate_b, gn4, B, Lp)
    ya = _mm(ya_in, gla_proj, name="gla_proj", out_dtype=BF16, tm=tq, tn=D, tk=D)
    q_att, k_att, v_att, cqn, ckvn = _mla_prep(projB, cos_t, sin_t, q_norm_g, kv_norm_g, wuq2, w_ukv, B, Lp, tr)
    ob, yb_in, lse_c = _attn_fwd(q_att, k_att, v_att, projA, B, Lp)
    yb = _mm(yb_in, mla_proj, name="mla_proj", out_dtype=BF16, tm=tq, tn=D, tk=D)
    dh1, dh1_b, merged, loss, d_gf = _out_proj_loss(x, meta, projA, ya, yb, w_out, final_norm_g.reshape(1, D),
                                                     loss_target, B, Lp)

    g_w_out = _mm(merged, dh1_b, name="dw_out", trans_a=True, tm=D, tn=D, tk=tkw)
    dya, dyb, dA = _merge_bwd(dh1_b, w_out, projA, ya, yb, tr)
    g_gla_proj = _mm(ya_in, dya, name="dw_gla_proj", trans_a=True, tm=D, tn=D, tk=tkw)
    g_mla_proj = _mm(yb_in, dyb, name="dw_mla_proj", trans_a=True, tm=D, tn=D, tk=tkw)
    doa, dBz, d_gn = _gla_out_bwd(dya, gla_proj, oa, projA, gn4, tr)
    dC, g_wg, d_bg = _gla_bwd(projA, projB, ssave, doa, wg, gate_b, B, Lp)
    do, dDz, delta_c = _attn_bwd_pre(dyb, mla_proj, projA, ob, B, Lp)
    dq, dk, dv = _attn_bwd(q_att, k_att, v_att, do, lse_c, delta_c, B, Lp)
    dqf, dkvf, dE, d_gq, d_gkv = _mla_bwd_post(dq, dk, dv, projB, cos_t, sin_t, q_norm_g, kv_norm_g,
                                                wuq2, w_ukv, B, Lp, tr)
    g_wuq2 = _mm(cqn, dqf, name="dw_uq", trans_a=True, tm=Q_RANK, tn=2048, tk=tkw)
    g_wukv = _mm(ckvn, dkvf, name="dw_ukv", trans_a=True, tm=KV_RANK, tn=2048, tk=tkw)
    dparts = [dA, dBz, dC, dDz, dE]
    g_in = [_mm(u, dp, name="dw_in_%d" % i, trans_a=True, tm=D, tn=_div_tile(dp.shape[1], 1024, 256), tk=tkw)
            for i, dp in enumerate(dparts)]
    grad_x, d_meta, d_ng = _in_proj_bwd(x, meta, dh1, dA, dBz, dC, dDz, dE, wA, wB, norm_g, B, Lp)

    gA, gBz, gC, gDz, gE = g_in
    src = [(gC, 1024), (gC, 1536), (gC, 0), (gC, 2048), (gBz, 0), (gE, 0), (gE, Q_RANK), (gE, 384), (gDz, 0),
           (gA, 0), (gA, D)]
    owners = []
    for j in range(4):
        parts = []
        for i, (arr, off) in enumerate(src):
            a, b = max(cuts[i], j * shard_w), min(cuts[i + 1], (j + 1) * shard_w)
            if a < b:
                parts.append(arr[:, off + a - cuts[i]:off + b - cuts[i]])
        owners.append(jnp.concatenate(parts, axis=1))
    g_w_in = jnp.stack(owners)
    g_wuq = g_wuq2.reshape(Q_RANK, MLA_H, 256)[:, :, :MLA_QK].reshape(Q_RANK, MLA_H * MLA_QK)
    grads = dict(w_in=g_w_in, gla_gate_w=g_wg[:GLA_RANK], gla_proj=g_gla_proj, mla_w_uq=g_wuq, mla_w_ukv=g_wukv,
                 mla_proj=g_mla_proj, w_out=g_w_out, meta_tokens=d_meta, norm_g=d_ng, gla_gate_b=d_bg,
                 gla_norm_g=d_gn, mla_q_norm_g=d_gq, mla_kv_norm_g=d_gkv, final_norm_g=d_gf)
    return loss[0, 0], grad_x, grads


_MATS = ("w_in", "gla_gate_w", "gla_proj", "mla_w_uq", "mla_w_ukv", "mla_proj", "w_out")
_ROW_SHARDED = ("gla_proj", "mla_proj", "w_out")
_ORDER = ("meta_tokens", "norm_g", "w_in", "gla_gate_w", "gla_gate_b", "gla_norm_g", "gla_proj", "mla_q_norm_g",
          "mla_w_uq", "mla_kv_norm_g", "mla_w_ukv", "mla_proj", "w_out", "final_norm_g")
WIRE_BF16_MIN_ELEMS = 128 * 128
SMALL_PACK_ROWS = 16


def _pack_small(d):
    rows = [jnp.pad(d[n].reshape(1, size), ((0, 0), (0, D - size))) for n, size in SMALL]
    return jnp.pad(jnp.concatenate(rows, axis=0), ((0, SMALL_PACK_ROWS - len(rows)), (0, 0)))


def _unpack_small(packed):
    return {n: packed[i, :size] for i, (n, size) in enumerate(SMALL)}


def kernel(x, meta_tokens, norm_g, w_in, gla_gate_w, gla_gate_b, gla_norm_g, gla_proj, mla_q_norm_g, mla_w_uq, mla_kv_norm_g, mla_w_ukv, mla_proj, w_out, final_norm_g, loss_target, m_meta_tokens, m_norm_g, m_w_in, m_gla_gate_w, m_gla_gate_b, m_gla_norm_g, m_gla_proj, m_mla_q_norm_g, m_mla_w_uq, m_mla_kv_norm_g, m_mla_w_ukv, m_mla_proj, m_w_out, m_final_norm_g, v_meta_tokens, v_norm_g, v_w_in, v_gla_gate_w, v_gla_gate_b, v_gla_norm_g, v_gla_proj, v_mla_q_norm_g, v_mla_w_uq, v_mla_kv_norm_g, v_mla_w_ukv, v_mla_proj, v_w_out, v_final_norm_g):
    w = dict(meta_tokens=meta_tokens, norm_g=norm_g, w_in=w_in[0], gla_gate_w=gla_gate_w[0], gla_gate_b=gla_gate_b,
             gla_norm_g=gla_norm_g, gla_proj=gla_proj[0], mla_q_norm_g=mla_q_norm_g, mla_w_uq=mla_w_uq[0],
             mla_kv_norm_g=mla_kv_norm_g, mla_w_ukv=mla_w_ukv[0], mla_proj=mla_proj[0], w_out=w_out[0],
             final_norm_g=final_norm_g)
    mom = dict(meta_tokens=m_meta_tokens, norm_g=m_norm_g, w_in=m_w_in[0], gla_gate_w=m_gla_gate_w[0],
               gla_gate_b=m_gla_gate_b, gla_norm_g=m_gla_norm_g, gla_proj=m_gla_proj[0], mla_q_norm_g=m_mla_q_norm_g,
               mla_w_uq=m_mla_w_uq[0], mla_kv_norm_g=m_mla_kv_norm_g, mla_w_ukv=m_mla_w_ukv[0], mla_proj=m_mla_proj[0],
               w_out=m_w_out[0], final_norm_g=m_final_norm_g)
    var = dict(meta_tokens=v_meta_tokens, norm_g=v_norm_g, w_in=v_w_in[0], gla_gate_w=v_gla_gate_w[0],
               gla_gate_b=v_gla_gate_b, gla_norm_g=v_gla_norm_g, gla_proj=v_gla_proj[0], mla_q_norm_g=v_mla_q_norm_g,
               mla_w_uq=v_mla_w_uq[0], mla_kv_norm_g=v_mla_kv_norm_g, mla_w_ukv=v_mla_w_ukv[0], mla_proj=v_mla_proj[0],
               w_out=v_w_out[0], final_norm_g=v_final_norm_g)
    out_shapes = {n: a.shape for n, a in zip(_ORDER, (meta_tokens, norm_g, w_in, gla_gate_w, gla_gate_b, gla_norm_g,
                                                     gla_proj, mla_q_norm_g, mla_w_uq, mla_kv_norm_g, mla_w_ukv,
                                                     mla_proj, w_out, final_norm_g))}

    me = (2 * lax.axis_index("x") + lax.axis_index("y")).astype(jnp.int32)
    shards = [w[n].astype(BF16) for n in _MATS] + [meta_tokens]
    is_mine = lax.broadcasted_iota(jnp.int32, (4, 1, 1), 0) == me
    gathered = [jnp.where(is_mine, own[None], gth) for gth, own in zip(_weight_gather(shards), shards)]
    full = {}
    for name, gth in zip(_MATS, gathered):
        if name == "w_in":
            full[name] = gth
        elif name in _ROW_SHARDED:
            full[name] = gth.reshape(4 * gth.shape[1], gth.shape[2])
        else:
            full[name] = gth.transpose(1, 0, 2).reshape(gth.shape[1], 4 * gth.shape[2])
    meta_full = gathered[-1].transpose(1, 0, 2).reshape(N_META, D)

    loss_local, grad_x, g = _local_step(
        x, loss_target, meta_full, norm_g, full["w_in"], full["gla_gate_w"], gla_gate_b, gla_norm_g, full["gla_proj"],
        mla_q_norm_g, full["mla_w_uq"], mla_kv_norm_g, full["mla_w_ukv"], full["mla_proj"], full["w_out"], final_norm_g)
    loss = lax.psum(loss_local, ("x", "y", "c"))

    def by_owner(name, arr):
        if name == "w_in":
            return arr
        if name in _ROW_SHARDED:
            return arr.reshape(4, arr.shape[0] // 4, arr.shape[1])
        return arr.reshape(arr.shape[0], 4, arr.shape[1] // 4).transpose(1, 0, 2)

    names = _MATS + ("meta_tokens",)
    gps = [by_owner(n, g[n]) for n in names] + [jnp.broadcast_to(_pack_small(g)[None], (4, SMALL_PACK_ROWS, D))]
    recvs = _pair_swap(gps)
    c_idx = lax.axis_index("c").astype(jnp.int32).reshape(1)
    s1 = [_pair_add_big(gps[0], recvs[0], c_idx)] + list(_pair_add_small(gps[1:], recvs[1:]))
    landed = _chip_scatter(s1)
    pos = jnp.stack([c_idx[0], me])
    halves = [_sum_chips_big(landed[0], s1[0], pos)] + list(_sum_chips_small(landed[1:], s1[1:]))
    shapes = [out_shapes[n] for n in names] + [(SMALL_PACK_ROWS, D)]
    g_red = [j.reshape(s) for j, s in zip(_pair_join(halves), shapes)]

    tens = lambda d: [d[n].reshape(out_shapes[n]) for n in names] + [_pack_small(d)]
    w_t, m_t, v_t = tens(w), tens(mom), tens(var)
    big = _adamw_big(w_t[0], g_red[0], m_t[0], v_t[0])
    rest = _adamw_small(w_t[1:], g_red[1:], m_t[1:], v_t[1:])
    k = len(names)
    results = {"grad": g_red}
    for i, kind in enumerate(("delta", "new_m", "new_v")):
        results[kind] = [big[i]] + list(rest[i * k:(i + 1) * k])

    outs = []
    for kind in ("grad", "delta", "new_m", "new_v"):
        vals = dict(zip(names, results[kind][:-1]))
        vals.update(_unpack_small(results[kind][-1]))
        outs += [vals[n].reshape(out_shapes[n]) for n in _ORDER]
    return (loss, grad_x, *outs)
```

```python
import functools
import math

import jax
import jax.numpy as jnp
import numpy as np
from jax import lax
from jax.experimental import pallas as pl
from jax.experimental.pallas import tpu as pltpu

F32 = jnp.float32
BF16 = jnp.bfloat16

D = 1024
N_META = 16
QB = 256
FRONT = QB - N_META
HEAD_ROWS = FRONT + N_META
assert FRONT % 64 == 48
EPS = 1e-6

GLA_H, GLA_DK, GLA_DV, GLA_RANK, GLA_C = 4, 128, 256, 16, 64
GLA_NORMALIZER = 16.0
GLA_KW, GLA_VW = GLA_H * GLA_DK, GLA_H * GLA_DV
MLA_H, NOPE, ROPE, MLA_DV, Q_RANK, KV_RANK = 8, 128, 64, 128, 256, 128
MLA_QK = NOPE + ROPE
ROPE_BASE = 10000.0
SPLITS = (GLA_KW, GLA_KW, GLA_VW, GLA_RANK, GLA_VW, Q_RANK, KV_RANK, ROPE, MLA_H * MLA_DV, D, D)
IN_WIDTH = sum(SPLITS)

ADAM_LR, ADAM_B1, ADAM_B2, ADAM_EPS, ADAM_WD, ADAM_STEP = 0.001, 0.9, 0.999, 1e-08, 0.01, 10

LANES = 128
VMEM_CAP_V7X = 56 * 1024 * 1024
MESH = pl.DeviceIdType.MESH
NEG = -1e30

SMALL = (("norm_g", D), ("gla_gate_b", GLA_KW), ("gla_norm_g", GLA_DV), ("mla_q_norm_g", Q_RANK),
         ("mla_kv_norm_g", KV_RANK), ("final_norm_g", D))


def _div_tile(n, target, mult):
    best = None
    for d in range(mult, min(n, target) + 1, mult):
        if n % d == 0:
            best = d
    assert best is not None, (n, target, mult)
    return best


def _params(sem, block_bytes, scratch_bytes=0):
    est = 2 * block_bytes + scratch_bytes + 12 * 1024 * 1024
    return pltpu.CompilerParams(dimension_semantics=sem, vmem_limit_bytes=int(min(max(est, 24 * 1024 * 1024), VMEM_CAP_V7X)))


def _nbytes(shape, dtype):
    return int(np.prod(shape)) * jnp.dtype(dtype).itemsize


def _sigmoid(x):
    return 1.0 / (1.0 + jnp.exp(-x))


def _nt(a, b):
    return lax.dot_general(a, b, (((1,), (1,)), ((), ())), preferred_element_type=F32)


def _tn(a, b):
    return lax.dot_general(a, b, (((0,), (0,)), ((), ())), preferred_element_type=F32)


def _nn(a, b):
    return jnp.dot(a, b, preferred_element_type=F32)


def _split3(x):
    a = x.astype(BF16)
    r = x - a.astype(F32)
    b = r.astype(BF16)
    c = (r - b.astype(F32)).astype(BF16)
    return a, b, c


def _mm(a, b, *, name, trans_a=False, trans_b=False, out_dtype=F32, tm, tn, tk):
    assert not (trans_a and trans_b)
    if trans_a:
        K, M = a.shape
    else:
        M, K = a.shape
    N = b.shape[0] if trans_b else b.shape[1]
    assert (b.shape[1] if trans_b else b.shape[0]) == K
    assert M % tm == 0 and N % tn == 0 and K % tk == 0, (name, M, N, K, tm, tn, tk)
    nk = K // tk

    def body(a_ref, b_ref, o_ref, *scratch):
        av = a_ref[...].astype(BF16)
        bv = b_ref[...].astype(BF16)
        prod = _tn(av, bv) if trans_a else (_nt(av, bv) if trans_b else _nn(av, bv))
        if nk == 1:
            o_ref[...] = prod.astype(out_dtype)
        else:
            acc = scratch[0]
            k = pl.program_id(2)

            @pl.when(k == 0)
            def _():
                acc[...] = prod

            @pl.when(k > 0)
            def _():
                acc[...] += prod

            @pl.when(k == nk - 1)
            def _():
                o_ref[...] = acc[...].astype(out_dtype)

    if trans_a:
        a_spec = pl.BlockSpec((tk, tm), lambda i, j, k: (k, i))
    else:
        a_spec = pl.BlockSpec((tm, tk), lambda i, j, k: (i, k))
    if trans_b:
        b_spec = pl.BlockSpec((tn, tk), lambda i, j, k: (j, k))
    else:
        b_spec = pl.BlockSpec((tk, tn), lambda i, j, k: (k, j))
    blocks = (_nbytes((tm, tk), a.dtype) + _nbytes((tk, tn), b.dtype) + _nbytes((tm, tn), out_dtype))
    scratch = [pltpu.VMEM((tm, tn), F32)] if nk > 1 else []
    return pl.pallas_call(
        body,
        out_shape=jax.ShapeDtypeStruct((M, N), out_dtype),
        grid=(M // tm, N // tn, nk),
        in_specs=[a_spec, b_spec],
        out_specs=pl.BlockSpec((tm, tn), lambda i, j, k: (i, j)),
        scratch_shapes=scratch,
        compiler_params=_params(("parallel", "parallel", "arbitrary"), blocks + _nbytes((tm, tn), F32),
                                _nbytes((tm, tn), F32) if nk > 1 else 0),
        name=name,
    )(a, b)


def _h_tile(j, x_ref, meta_ref):
    head = jnp.concatenate([jnp.zeros((FRONT, D), F32), meta_ref[...]], axis=0)
    return jnp.where(j > 0, x_ref[0], head)


def _x_spec():
    return pl.BlockSpec((1, QB, D), lambda b, j: (b, jnp.maximum(j - 1, 0), 0))


def _rms_in(x, meta, g, B, Lp):
    T = B * Lp
    NQ = Lp // QB

    def body(x_ref, meta_ref, g_ref, u_ref):
        h = _h_tile(pl.program_id(1), x_ref, meta_ref)
        r = lax.rsqrt(jnp.mean(h * h, axis=-1, keepdims=True) + EPS)
        u_ref[...] = (h * r * g_ref[...]).astype(BF16)

    return pl.pallas_call(
        body,
        out_shape=jax.ShapeDtypeStruct((T, D), BF16),
        grid=(B, NQ),
        in_specs=[_x_spec(), pl.BlockSpec((N_META, D), lambda b, j: (0, 0)), pl.BlockSpec((1, D), lambda b, j: (0, 0))],
        out_specs=pl.BlockSpec((QB, D), lambda b, j: (b * NQ + j, 0)),
        compiler_params=_params(("parallel", "parallel"), _nbytes((QB, D), F32) * 2),
        name="rms_in",
    )(x, meta, g)


def _gla_gate(lr, wg, bg, valid):
    pre = _nn(lr.astype(BF16), wg) + bg
    logsig = jnp.minimum(pre, 0.0) - jnp.log(1.0 + jnp.exp(-jnp.abs(pre)))
    return pre, jnp.where(valid, logsig / GLA_NORMALIZER, 0.0)


def _tri_masks():
    ri = lax.broadcasted_iota(jnp.int32, (GLA_C, GLA_C), 0)
    ci = lax.broadcasted_iota(jnp.int32, (GLA_C, GLA_C), 1)
    return ci <= ri, ci >= ri


def _cumsum_rows(x, ones_mask):
    w = jnp.where(ones_mask, 1.0, 0.0).astype(BF16)
    a, b, c = _split3(x)
    return _nn(w, a) + _nn(w, b) + _nn(w, c)


def _gla_fwd(projA, projB, wg, bg, gn4, B, Lp):
    T = B * Lp
    NC = Lp // GLA_C
    C = GLA_C
    scale = GLA_DK ** -0.5

    def body(q_ref, k_ref, v_ref, lr_ref, z_ref, wg_ref, bg_ref, gn_ref, oa_ref, ya_ref, ssave_ref, st_ref):
        n = pl.program_id(0)

        @pl.when(n == 0)
        def _():
            st_ref[...] = jnp.zeros_like(st_ref)

        pos = n * C + lax.broadcasted_iota(jnp.int32, (C, 1), 0)
        lower, _ = _tri_masks()
        is_last = lax.broadcasted_iota(jnp.int32, (C, 1), 0) == C - 1
        for b in range(B):
            ssave_ref[b, 0] = st_ref[b]
            _, glog = _gla_gate(lr_ref[b], wg_ref[...], bg_ref[...], pos >= FRONT)
            bcum = _cumsum_rows(glog, lower)
            for h in range(GLA_H):
                ks = slice(h * GLA_DK, (h + 1) * GLA_DK)
                vs = slice(h * GLA_DV, (h + 1) * GLA_DV)
                bh = bcum[:, ks]
                blast = jnp.sum(jnp.where(is_last, bh, 0.0), axis=0, keepdims=True)
                qh = q_ref[b, :, ks].astype(F32) * scale
                kh = k_ref[b, :, ks].astype(F32)
                qe = (qh * jnp.exp(bh)).astype(BF16)
                ke = (kh * jnp.exp(-bh)).astype(BF16)
                kl = (kh * jnp.exp(blast - bh)).astype(BF16)
                vh = v_ref[b, :, vs].astype(BF16)
                a = jnp.where(lower, _nt(qe, ke), 0.0).astype(BF16)
                st = st_ref[b, h]
                o = _nn(a, vh) + _nt(qe, st.astype(BF16))
                st_ref[b, h] = st * jnp.exp(blast) + _tn(vh, kl)
                oa_ref[b, :, vs] = o
                on = o * lax.rsqrt(jnp.mean(o * o, axis=-1, keepdims=True) + EPS) * gn_ref[:, vs]
                z = z_ref[b, :, vs].astype(F32)
                ya_ref[b, :, vs] = (on * (z * _sigmoid(z))).astype(BF16)

    blocks = B * (_nbytes((C, 512), F32) * 2 + _nbytes((C, 1024), F32) * 3 + _nbytes((C, 1024), BF16)
                  + _nbytes((GLA_H, GLA_DV, GLA_DK), F32)) + _nbytes((128, 512), BF16)
    state = _nbytes((B, GLA_H, GLA_DV, GLA_DK), F32)
    pa = projA.reshape(B, Lp, projA.shape[1])
    oa, ya, ssave = pl.pallas_call(
        body,
        out_shape=(jax.ShapeDtypeStruct((B, Lp, GLA_VW), F32), jax.ShapeDtypeStruct((B, Lp, GLA_VW), BF16),
                   jax.ShapeDtypeStruct((B, NC, GLA_H, GLA_DV, GLA_DK), F32)),
        grid=(NC,),
        in_specs=[
            pl.BlockSpec((B, C, 512), lambda n: (0, n, 10)),
            pl.BlockSpec((B, C, 512), lambda n: (0, n, 11)),
            pl.BlockSpec((B, C, 1024), lambda n: (0, n, 0)),
            pl.BlockSpec((B, C, 128), lambda n: (0, n, 3)),
            pl.BlockSpec((B, C, 1024), lambda n: (0, n, 1)),
            pl.BlockSpec((128, 512), lambda n: (0, 0)),
            pl.BlockSpec((1, 512), lambda n: (0, 0)),
            pl.BlockSpec((1, 1024), lambda n: (0, 0)),
        ],
        out_specs=(pl.BlockSpec((B, C, 1024), lambda n: (0, n, 0)),
                   pl.BlockSpec((B, C, 1024), lambda n: (0, n, 0)),
                   pl.BlockSpec((B, 1, GLA_H, GLA_DV, GLA_DK), lambda n: (0, n, 0, 0, 0))),
        scratch_shapes=[pltpu.VMEM((B, GLA_H, GLA_DV, GLA_DK), F32)],
        compiler_params=_params(("arbitrary",), blocks, state),
        name="gla_fwd",
    )(pa, pa, pa, projB.reshape(B, Lp, projB.shape[1]), pa, wg, bg, gn4)
    return oa.reshape(T, GLA_VW), ya.reshape(T, GLA_VW), ssave


def _swap_halves(x):
    lane = lax.broadcasted_iota(jnp.int32, x.shape, 1)
    return jnp.where((lane % 64) < 32, pltpu.roll(x, 96, 1), pltpu.roll(x, 32, 1))


def _mla_prep(projB, cos_t, sin_t, gq, gkv, wuq2, wukv, B, Lp, tr):
    T = B * Lp
    nt = Lp // tr
    HW = 2 * LANES

    def body(pb_ref, cos_ref, sin_ref, gq_ref, gkv_ref, wuq_ref, wukv_ref, q_ref, k_ref, v_ref, cqn_ref, ckvn_ref):
        cq = pb_ref[:, 0:Q_RANK].astype(F32)
        ckv = pb_ref[:, Q_RANK:Q_RANK + KV_RANK].astype(F32)
        kr = pb_ref[:, 512:640].astype(F32)
        cqn = (cq * lax.rsqrt(jnp.mean(cq * cq, axis=-1, keepdims=True) + EPS) * gq_ref[...]).astype(BF16)
        ckvn = (ckv * lax.rsqrt(jnp.mean(ckv * ckv, axis=-1, keepdims=True) + EPS) * gkv_ref[...]).astype(BF16)
        cqn_ref[...] = cqn
        ckvn_ref[...] = ckvn
        qf = _nn(cqn, wuq_ref[...])
        kvf = _nn(ckvn, wukv_ref[...])
        cs = cos_ref[...]
        sn = sin_ref[...]
        rope = lambda t: t * cs + _swap_halves(t) * sn
        kr_r = rope(kr).astype(BF16)
        for h in range(MLA_H):
            q_ref[:, h * HW:h * HW + LANES] = qf[:, h * HW:h * HW + LANES].astype(BF16)
            q_ref[:, h * HW + LANES:(h + 1) * HW] = rope(qf[:, h * HW + LANES:(h + 1) * HW]).astype(BF16)
            k_ref[:, h * HW:h * HW + LANES] = kvf[:, h * HW:h * HW + LANES].astype(BF16)
            k_ref[:, h * HW + LANES:(h + 1) * HW] = kr_r
            v_ref[:, h * MLA_DV:(h + 1) * MLA_DV] = kvf[:, h * HW + LANES:(h + 1) * HW].astype(BF16)

    blocks = (_nbytes((tr, 640), F32) + 2 * _nbytes((tr, 128), F32) + _nbytes((Q_RANK, 2048), BF16)
              + _nbytes((KV_RANK, 2048), BF16) + _nbytes((tr, 2048 * 2 + 1024 + 384), BF16)
              + 2 * _nbytes((tr, 2048), F32))
    return pl.pallas_call(
        body,
        out_shape=(jax.ShapeDtypeStruct((T, MLA_H * HW), BF16), jax.ShapeDtypeStruct((T, MLA_H * HW), BF16),
                   jax.ShapeDtypeStruct((T, MLA_H * MLA_DV), BF16), jax.ShapeDtypeStruct((T, Q_RANK), BF16),
                   jax.ShapeDtypeStruct((T, KV_RANK), BF16)),
        grid=(B, nt),
        in_specs=[
            pl.BlockSpec((tr, 640), lambda b, j: (b * nt + j, 0)),
            pl.BlockSpec((tr, 128), lambda b, j: (j, 0)),
            pl.BlockSpec((tr, 128), lambda b, j: (j, 0)),
            pl.BlockSpec((1, Q_RANK), lambda b, j: (0, 0)),
            pl.BlockSpec((1, KV_RANK), lambda b, j: (0, 0)),
            pl.BlockSpec((Q_RANK, 2048), lambda b, j: (0, 0)),
            pl.BlockSpec((KV_RANK, 2048), lambda b, j: (0, 0)),
        ],
        out_specs=(pl.BlockSpec((tr, 2048), lambda b, j: (b * nt + j, 0)),
                   pl.BlockSpec((tr, 2048), lambda b, j: (b * nt + j, 0)),
                   pl.BlockSpec((tr, 1024), lambda b, j: (b * nt + j, 0)),
                   pl.BlockSpec((tr, Q_RANK), lambda b, j: (b * nt + j, 0)),
                   pl.BlockSpec((tr, KV_RANK), lambda b, j: (b * nt + j, 0))),
        compiler_params=_params(("parallel", "parallel"), blocks),
        name="mla_prep",
    )(projB, cos_t, sin_t, gq, gkv, wuq2, wukv)


def _attn_mask(row, col):
    return (col <= row) & ((col >= FRONT) | (row < FRONT))


def _attn_fwd(q_att, k_att, v_att, projA, B, Lp):
    T = B * Lp
    NQ = Lp // QB
    HW = 2 * LANES
    scale = 1.0 / math.sqrt(MLA_QK)

    def body(q_ref, k_ref, v_ref, mz_ref, o_ref, yb_ref, lsec_ref, m_ref, l_ref, acc_ref):
        qi = pl.program_id(1)
        m_ref[...] = jnp.full(m_ref.shape, NEG, F32)
        l_ref[...] = jnp.zeros_like(l_ref)
        acc_ref[...] = jnp.zeros_like(acc_ref)
        row = qi * QB + lax.broadcasted_iota(jnp.int32, (QB, QB), 0)
        coli = lax.broadcasted_iota(jnp.int32, (QB, QB), 1)

        def step(kj, carry):
            off = pl.multiple_of(kj * QB, QB)
            ok = _attn_mask(row, kj * QB + coli)
            for h in range(MLA_H):
                q = q_ref[:, h * HW:(h + 1) * HW]
                kb = k_ref[pl.ds(off, QB), h * HW:(h + 1) * HW]
                vb = v_ref[pl.ds(off, QB), h * MLA_DV:(h + 1) * MLA_DV]
                s = jnp.where(ok, _nt(q, kb) * scale, NEG)
                m_old = m_ref[h]
                m_new = jnp.maximum(m_old, jnp.max(s, axis=-1, keepdims=True))
                alpha = jnp.exp(m_old - m_new)
                p = jnp.exp(s - jnp.tile(m_new, (1, QB // LANES)))
                m_ref[h] = m_new
                l_ref[h] = alpha * l_ref[h] + jnp.sum(p, axis=-1, keepdims=True)
                acc_ref[h] = alpha * acc_ref[h] + _nn(p.astype(BF16), vb)
            return carry

        lax.fori_loop(0, qi + 1, step, 0)
        for h in range(MLA_H):
            hs = slice(h * MLA_DV, (h + 1) * MLA_DV)
            l = l_ref[h]
            o = acc_ref[h] / l
            o_ref[:, hs] = o
            z = mz_ref[:, hs].astype(F32)
            yb_ref[:, hs] = (o * (z * _sigmoid(z))).astype(BF16)
            lse = m_ref[h] + jnp.log(l)
            lsec_ref[0, h, pl.ds(qi, 1), :] = jnp.transpose(lse)[0:1, :]

    blocks = (_nbytes((QB, 2048), BF16) + _nbytes((Lp, 2048), BF16) + _nbytes((Lp, 1024), BF16)
              + 2 * _nbytes((QB, 1024), F32) + _nbytes((QB, 1024), BF16) + _nbytes((MLA_H, QB, LANES), F32)
              + _nbytes((MLA_H, NQ, QB), F32))
    return pl.pallas_call(
        body,
        out_shape=(jax.ShapeDtypeStruct((T, MLA_H * MLA_DV), F32), jax.ShapeDtypeStruct((T, MLA_H * MLA_DV), BF16),
                   jax.ShapeDtypeStruct((B, MLA_H, NQ, QB), F32)),
        grid=(B, NQ),
        in_specs=[
            pl.BlockSpec((QB, MLA_H * HW), lambda b, i: (b * NQ + i, 0)),
            pl.BlockSpec((Lp, MLA_H * HW), lambda b, i: (b, 0)),
            pl.BlockSpec((Lp, MLA_H * MLA_DV), lambda b, i: (b, 0)),
            pl.BlockSpec((QB, 1024), lambda b, i: (b * NQ + i, 2)),
        ],
        out_specs=(pl.BlockSpec((QB, 1024), lambda b, i: (b * NQ + i, 0)),
                   pl.BlockSpec((QB, 1024), lambda b, i: (b * NQ + i, 0)),
                   pl.BlockSpec((1, MLA_H, NQ, QB), lambda b, i: (b, 0, 0, 0))),
        scratch_shapes=[pltpu.VMEM((MLA_H, QB, LANES), F32), pltpu.VMEM((MLA_H, QB, LANES), F32),
                        pltpu.VMEM((MLA_H, QB, MLA_DV), F32)],
        compiler_params=_params(("parallel", "arbitrary"), blocks, 3 * _nbytes((MLA_H, QB, LANES), F32)),
        name="attn_fwd",
    )(q_att, k_att, v_att, projA)


def _out_proj_loss(x, meta, projA, ya, yb, w_out, gf, tgt, B, Lp):
    T = B * Lp
    NQ = Lp // QB

    def body(x_ref, meta_ref, gg_ref, gm_ref, ya_ref, yb_ref, w_ref, gf_ref, t_ref,
             dh_ref, dhb_ref, mg_ref, loss_ref, dgf_ref):
        b = pl.program_id(0)
        j = pl.program_id(1)

        @pl.when((b == 0) & (j == 0))
        def _():
            loss_ref[...] = jnp.zeros_like(loss_ref)
            dgf_ref[...] = jnp.zeros_like(dgf_ref)

        f32 = lambda ref: ref[...].astype(F32)
        merged = (_sigmoid(f32(gg_ref)) * f32(ya_ref) + _sigmoid(f32(gm_ref)) * f32(yb_ref)).astype(BF16)
        mg_ref[...] = merged
        h1 = _h_tile(j, x_ref, meta_ref) + _nn(merged, w_ref[...])
        r = lax.rsqrt(jnp.mean(h1 * h1, axis=-1, keepdims=True) + EPS)
        hn = h1 * r
        gfv = gf_ref[...]
        diff = jnp.where(j > 0, hn * gfv - t_ref[0], 0.0)
        loss_ref[...] += (0.5 / D) * jnp.sum(jnp.sum(diff * diff, axis=-1, keepdims=True), axis=0, keepdims=True)
        dout = diff * (1.0 / D)
        dgf_ref[...] += jnp.sum(dout * hn, axis=0, keepdims=True)
        dhn = dout * gfv
        dh = r * (dhn - hn * jnp.mean(dhn * hn, axis=-1, keepdims=True))
        dh_ref[...] = dh
        dhb_ref[...] = dh.astype(BF16)

    rows = lambda c: pl.BlockSpec((QB, D), lambda b, j: (b * NQ + j, c))
    const = lambda s: pl.BlockSpec(s, lambda b, j: (0, 0))
    return pl.pallas_call(
        body,
        out_shape=(jax.ShapeDtypeStruct((T, D), F32), jax.ShapeDtypeStruct((T, D), BF16),
                   jax.ShapeDtypeStruct((T, D), BF16), jax.ShapeDtypeStruct((1, 1), F32),
                   jax.ShapeDtypeStruct((1, D), F32)),
        grid=(B, NQ),
        in_specs=[_x_spec(), const((N_META, D)), rows(3), rows(4), rows(0), rows(0), const((D, D)),
                  const((1, D)), _x_spec()],
        out_specs=(rows(0), rows(0), rows(0), const((1, 1)), const((1, D))),
        compiler_params=_params(("arbitrary", "arbitrary"), 10 * _nbytes((QB, D), F32)),
        name="out_proj_loss",
    )(x, meta, projA, projA, ya, yb, w_out, gf, tgt)


def _merge_bwd(dh1_b, w_out, projA, ya, yb, tr):
    T = dh1_b.shape[0]

    def body(dh_ref, w_ref, gg_ref, gm_ref, ya_ref, yb_ref, dya_ref, dyb_ref, da_ref):
        d = _nt(dh_ref[...], w_ref[...])
        sg = _sigmoid(gg_ref[...].astype(F32))
        sm = _sigmoid(gm_ref[...].astype(F32))
        dya_ref[...] = (d * sg).astype(BF16)
        dyb_ref[...] = (d * sm).astype(BF16)
        da_ref[:, 0:D] = (d * ya_ref[...].astype(F32) * (sg * (1.0 - sg))).astype(BF16)
        da_ref[:, D:2 * D] = (d * yb_ref[...].astype(F32) * (sm * (1.0 - sm))).astype(BF16)

    spec = lambda c: pl.BlockSpec((tr, D), lambda i: (i, c))
    return pl.pallas_call(
        body,
        out_shape=(jax.ShapeDtypeStruct((T, D), BF16), jax.ShapeDtypeStruct((T, D), BF16),
                   jax.ShapeDtypeStruct((T, 2 * D), BF16)),
        grid=(T // tr,),
        in_specs=[spec(0), pl.BlockSpec((D, D), lambda i: (0, 0)), spec(3), spec(4), spec(0), spec(0)],
        out_specs=(spec(0), spec(0), pl.BlockSpec((tr, 2 * D), lambda i: (i, 0))),
        compiler_params=_params(("parallel",), 8 * _nbytes((tr, D), F32)),
        name="merge_bwd",
    )(dh1_b, w_out, projA, projA, ya, yb)


def _gla_out_bwd(dya, gla_proj, oa, projA, gn4, tr):
    T = dya.shape[0]
    nsteps = T // tr

    def body(dya_ref, w_ref, oa_ref, z_ref, gn_ref, do_ref, dz_ref, dgn_ref, acc_ref):
        i = pl.program_id(0)

        @pl.when(i == 0)
        def _():
            acc_ref[...] = jnp.zeros_like(acc_ref)

        dy_all = _nt(dya_ref[...], w_ref[...])
        for h in range(GLA_H):
            vs = slice(h * GLA_DV, (h + 1) * GLA_DV)
            dy = dy_all[:, vs]
            o = oa_ref[:, vs]
            z = z_ref[:, vs].astype(F32)
            gn = gn_ref[:, vs]
            s = _sigmoid(z)
            ra = lax.rsqrt(jnp.mean(o * o, axis=-1, keepdims=True) + EPS)
            on = o * ra
            don = dy * (z * s)
            t = don * gn
            do_ref[:, vs] = (ra * (t - on * jnp.mean(t * on, axis=-1, keepdims=True))).astype(BF16)
            dz_ref[:, vs] = (dy * (on * gn) * (s * (1.0 + z * (1.0 - s)))).astype(BF16)
            acc_ref[:, vs] += jnp.sum(don * on, axis=0, keepdims=True)

        @pl.when(i == nsteps - 1)
        def _():
            a = acc_ref[...]
            dgn_ref[...] = a[:, 0:256] + a[:, 256:512] + a[:, 512:768] + a[:, 768:1024]

    spec = lambda c: pl.BlockSpec((tr, D), lambda i: (i, c))
    return pl.pallas_call(
        body,
        out_shape=(jax.ShapeDtypeStruct((T, D), BF16), jax.ShapeDtypeStruct((T, D), BF16),
                   jax.ShapeDtypeStruct((1, GLA_DV), F32)),
        grid=(nsteps,),
        in_specs=[spec(0), pl.BlockSpec((D, D), lambda i: (0, 0)), spec(0), spec(1),
                  pl.BlockSpec((1, D), lambda i: (0, 0))],
        out_specs=(spec(0), spec(0), pl.BlockSpec((1, GLA_DV), lambda i: (0, 0))),
        scratch_shapes=[pltpu.VMEM((1, D), F32)],
        compiler_params=_params(("arbitrary",), 6 * _nbytes((tr, D), F32)),
        name="gla_out_bwd",
    )(dya, gla_proj, oa, projA, gn4)


def _gla_bwd(projA, projB, ssave, doa, wg, bg, B, Lp):
    T = B * Lp
    NC = Lp // GLA_C
    C = GLA_C
    scale = GLA_DK ** -0.5
    WC = 2304

    def body(q_ref, k_ref, v_ref, lr_ref, ss_ref, do_ref, wg_ref, bg_ref, dc_ref, dwg_ref, dbg_ref, dst_ref):
        i = pl.program_id(0)
        n = NC - 1 - i

        @pl.when(i == 0)
        def _():
            dst_ref[...] = jnp.zeros_like(dst_ref)
            dwg_ref[...] = jnp.zeros_like(dwg_ref)
            dbg_ref[...] = jnp.zeros_like(dbg_ref)

        pos = n * C + lax.broadcasted_iota(jnp.int32, (C, 1), 0)
        valid = pos >= FRONT
        lower, upper = _tri_masks()
        is_last = lax.broadcasted_iota(jnp.int32, (C, 1), 0) == C - 1
        for b in range(B):
            lr = lr_ref[b]
            pre, glog = _gla_gate(lr, wg_ref[...], bg_ref[...], valid)
            bcum = _cumsum_rows(glog, lower)
            db_parts = []
            for h in range(GLA_H):
                ks = slice(h * GLA_DK, (h + 1) * GLA_DK)
                vs = slice(h * GLA_DV, (h + 1) * GLA_DV)
                bh = bcum[:, ks]
                blast = jnp.sum(jnp.where(is_last, bh, 0.0), axis=0, keepdims=True)
                eb, enb, ekl, ebl = jnp.exp(bh), jnp.exp(-bh), jnp.exp(blast - bh), jnp.exp(blast)
                qh = q_ref[b, :, ks].astype(F32) * scale
                kh = k_ref[b, :, ks].astype(F32)
                qe_f, ke_f, kl_f = qh * eb, kh * enb, kh * ekl
                qe, ke, kl = qe_f.astype(BF16), ke_f.astype(BF16), kl_f.astype(BF16)
                vh = v_ref[b, :, vs].astype(BF16)
                doh = do_ref[b, :, vs]
                st = ss_ref[b, 0, h]
                dst = dst_ref[b, h]
                st_b, dst_b = st.astype(BF16), dst.astype(BF16)
                da = jnp.where(lower, _nt(doh, vh), 0.0).astype(BF16)
                da_t = jnp.where(upper, _nt(vh, doh), 0.0).astype(BF16)
                a_t = jnp.where(upper, _nt(ke, qe), 0.0).astype(BF16)
                dqe = _nn(da, ke) + _nn(doh, st_b)
                dke = _nn(da_t, qe)
                dvh = _nn(a_t, doh) + _nt(kl, dst_b)
                dkl = _nn(vh, dst_b)
                dst_ref[b, h] = dst * ebl + _tn(doh, qe)
                deb = jnp.sum(st * dst, axis=0, keepdims=True)
                db = dqe * qe_f - dke * ke_f - dkl * kl_f
                db_last = jnp.sum(dkl * kl_f, axis=0, keepdims=True) + deb * ebl
                db_parts.append(db + jnp.where(is_last, db_last, 0.0))
                dc_ref[b, :, vs] = dvh.astype(BF16)
                dc_ref[b, :, 1024 + h * GLA_DK:1024 + (h + 1) * GLA_DK] = (dqe * eb * scale).astype(BF16)
                dc_ref[b, :, 1536 + h * GLA_DK:1536 + (h + 1) * GLA_DK] = (dke * enb + dkl * ekl).astype(BF16)
            dglog = _cumsum_rows(jnp.concatenate(db_parts, axis=1), upper)
            dpre = jnp.where(valid, dglog * (1.0 / GLA_NORMALIZER) / (1.0 + jnp.exp(pre)), 0.0)
            dpre_b = dpre.astype(BF16)
            dc_ref[b, :, 2048:2176] = _nt(dpre_b, wg_ref[...]).astype(BF16)
            dc_ref[b, :, 2176:2304] = jnp.zeros((C, 128), BF16)
            dwg_ref[...] += _tn(lr.astype(BF16), dpre_b)
            dbg_ref[...] += jnp.sum(dpre, axis=0, keepdims=True)

    blocks = B * (_nbytes((C, 512), F32) * 2 + _nbytes((C, 1024), F32) + _nbytes((C, 1024), BF16)
                  + _nbytes((GLA_H, GLA_DV, GLA_DK), F32) + _nbytes((C, WC), BF16)) + 3 * _nbytes((128, 512), F32)
    state = _nbytes((B, GLA_H, GLA_DV, GLA_DK), F32)
    pa = projA.reshape(B, Lp, projA.shape[1])
    rev = lambda i: NC - 1 - i
    dc, dwg, dbg = pl.pallas_call(
        body,
        out_shape=(jax.ShapeDtypeStruct((B, Lp, WC), BF16), jax.ShapeDtypeStruct((128, GLA_KW), F32),
                   jax.ShapeDtypeStruct((1, GLA_KW), F32)),
        grid=(NC,),
        in_specs=[
            pl.BlockSpec((B, C, 512), lambda i: (0, rev(i), 10)),
            pl.BlockSpec((B, C, 512), lambda i: (0, rev(i), 11)),
            pl.BlockSpec((B, C, 1024), lambda i: (0, rev(i), 0)),
            pl.BlockSpec((B, C, 128), lambda i: (0, rev(i), 3)),
            pl.BlockSpec((B, 1, GLA_H, GLA_DV, GLA_DK), lambda i: (0, rev(i), 0, 0, 0)),
            pl.BlockSpec((B, C, 1024), lambda i: (0, rev(i), 0)),
            pl.BlockSpec((128, 512), lambda i: (0, 0)),
            pl.BlockSpec((1, 512), lambda i: (0, 0)),
        ],
        out_specs=(pl.BlockSpec((B, C, WC), lambda i: (0, rev(i), 0)),
                   pl.BlockSpec((128, GLA_KW), lambda i: (0, 0)),
                   pl.BlockSpec((1, GLA_KW), lambda i: (0, 0))),
        scratch_shapes=[pltpu.VMEM((B, GLA_H, GLA_DV, GLA_DK), F32)],
        compiler_params=_params(("arbitrary",), blocks, state),
        name="gla_bwd",
    )(pa, pa, pa, projB.reshape(B, Lp, projB.shape[1]), ssave, doa.reshape(B, Lp, GLA_VW), wg, bg)
    return dc.reshape(T, WC), dwg, dbg


def _attn_bwd_pre(dyb, mla_proj, projA, ob, B, Lp):
    T = B * Lp
    NQ = Lp // QB

    def body(dyb_ref, w_ref, z_ref, o_ref, do_ref, dz_ref, dcol_ref):
        j = pl.program_id(1)
        dy_all = _nt(dyb_ref[...], w_ref[...])
        for h in range(MLA_H):
            hs = slice(h * MLA_DV, (h + 1) * MLA_DV)
            dy = dy_all[:, hs]
            z = z_ref[:, hs].astype(F32)
            o = o_ref[:, hs]
            s = _sigmoid(z)
            do = dy * (z * s)
            do_ref[:, hs] = do.astype(BF16)
            dz_ref[:, hs] = (dy * o * (s * (1.0 + z * (1.0 - s)))).astype(BF16)
            dl = jnp.broadcast_to(jnp.sum(do * o, axis=-1, keepdims=True), (QB, LANES))
            dcol_ref[0, h, pl.ds(j, 1), :] = jnp.transpose(dl)[0:1, :]

    rows = lambda c: pl.BlockSpec((QB, D), lambda b, j: (b * NQ + j, c))
    return pl.pallas_call(
        body,
        out_shape=(jax.ShapeDtypeStruct((T, D), BF16), jax.ShapeDtypeStruct((T, D), BF16),
                   jax.ShapeDtypeStruct((B, MLA_H, NQ, QB), F32)),
        grid=(B, NQ),
        in_specs=[rows(0), pl.BlockSpec((D, D), lambda b, j: (0, 0)), rows(2), rows(0)],
        out_specs=(rows(0), rows(0), pl.BlockSpec((1, MLA_H, NQ, QB), lambda b, j: (b, 0, 0, 0))),
        compiler_params=_params(("parallel", "arbitrary"), 6 * _nbytes((QB, D), F32)),
        name="attn_bwd_pre",
    )(dyb, mla_proj, projA, ob)


ATTN_BWD_HEADS = 8


def _attn_bwd(q_att, k_att, v_att, do, lse_c, delta_c, B, Lp):
    T = B * Lp
    NQ = Lp // QB
    G = ATTN_BWD_HEADS
    NG = MLA_H // G
    HW = 2 * LANES
    scale = 1.0 / math.sqrt(MLA_QK)

    def body(q_ref, k_ref, v_ref, do_ref, lse_ref, dl_ref, dq_out, dk_out, dv_out, dq_ref, dk_ref, dv_ref):
        kj = pl.program_id(2)

        @pl.when(kj == 0)
        def _():
            dq_ref[...] = jnp.zeros_like(dq_ref)

        dk_ref[...] = jnp.zeros_like(dk_ref)
        dv_ref[...] = jnp.zeros_like(dv_ref)
        col = kj * QB + lax.broadcasted_iota(jnp.int32, (QB, QB), 0)
        rowi = lax.broadcasted_iota(jnp.int32, (QB, QB), 1)

        def step(qi, carry):
            off = pl.multiple_of(qi * QB, QB)
            ok = _attn_mask(qi * QB + rowi, col)
            for h in range(G):
                ws = slice(h * HW, (h + 1) * HW)
                hs = slice(h * MLA_DV, (h + 1) * MLA_DV)
                qb = q_ref[pl.ds(off, QB), ws]
                dob = do_ref[pl.ds(off, QB), hs]
                kb = k_ref[:, ws]
                lse = lse_ref[0, h, pl.ds(qi, 1), :]
                delta = dl_ref[0, h, pl.ds(qi, 1), :]
                s_t = _nt(kb, qb) * scale
                p_t = jnp.where(ok, jnp.exp(s_t - lse), 0.0)
                dv_ref[:, hs] += _nn(p_t.astype(BF16), dob)
                ds_t = (p_t * (_nt(v_ref[:, hs], dob) - delta) * scale).astype(BF16)
                dk_ref[:, ws] += _nn(ds_t, qb)
                dq_ref[pl.ds(off, QB), ws] += _tn(ds_t, kb)
            return carry

        lax.fori_loop(kj, NQ, step, 0)
        dk_out[...] = dk_ref[...].astype(BF16)
        dv_out[...] = dv_ref[...].astype(BF16)

        @pl.when(kj == NQ - 1)
        def _():
            dq_out[...] = dq_ref[...].astype(BF16)

    blocks = (2 * _nbytes((Lp, G * HW), BF16) + _nbytes((Lp, G * MLA_DV), BF16) + 2 * _nbytes((QB, G * 384), BF16)
              + 2 * _nbytes((G, NQ, QB), F32))
    scratch = [pltpu.VMEM((Lp, G * HW), F32), pltpu.VMEM((QB, G * HW), F32), pltpu.VMEM((QB, G * MLA_DV), F32)]
    return pl.pallas_call(
        body,
        out_shape=(jax.ShapeDtypeStruct((T, MLA_H * HW), BF16), jax.ShapeDtypeStruct((T, MLA_H * HW), BF16),
                   jax.ShapeDtypeStruct((T, MLA_H * MLA_DV), BF16)),
        scratch_shapes=scratch,
        grid=(B, NG, NQ),
        in_specs=[
            pl.BlockSpec((Lp, G * HW), lambda b, g, j: (b, g), pipeline_mode=pl.Buffered(1)),
            pl.BlockSpec((QB, G * HW), lambda b, g, j: (b * NQ + j, g)),
            pl.BlockSpec((QB, G * MLA_DV), lambda b, g, j: (b * NQ + j, g)),
            pl.BlockSpec((Lp, G * MLA_DV), lambda b, g, j: (b, g), pipeline_mode=pl.Buffered(1)),
            pl.BlockSpec((1, G, NQ, QB), lambda b, g, j: (b, g, 0, 0)),
            pl.BlockSpec((1, G, NQ, QB), lambda b, g, j: (b, g, 0, 0)),
        ],
        out_specs=(pl.BlockSpec((Lp, G * HW), lambda b, g, j: (b, g), pipeline_mode=pl.Buffered(1)),
                   pl.BlockSpec((QB, G * HW), lambda b, g, j: (b * NQ + j, g)),
                   pl.BlockSpec((QB, G * MLA_DV), lambda b, g, j: (b * NQ + j, g))),
        compiler_params=_params(("parallel", "parallel", "arbitrary"), blocks,
                                _nbytes((Lp, G * HW), F32) + _nbytes((QB, G * 384), F32)),
        name="attn_bwd",
    )(q_att, k_att, v_att, do, lse_c, delta_c)


def _mla_bwd_post(dq, dk, dv, projB, cos_t, sin_t, gq, gkv, wuq2, wukv, B, Lp, tr):
    T = B * Lp
    nt = Lp // tr
    HW = 2 * LANES

    def body(dq_ref, dk_ref, dv_ref, pb_ref, cos_ref, sin_ref, gq_ref, gkv_ref, wuq_ref, wukv_ref,
             dqf_ref, dkvf_ref, de_ref, dgq_ref, dgkv_ref):
        first = (pl.program_id(0) == 0) & (pl.program_id(1) == 0)

        @pl.when(first)
        def _():
            dgq_ref[...] = jnp.zeros_like(dgq_ref)
            dgkv_ref[...] = jnp.zeros_like(dgkv_ref)

        cs = cos_ref[...]
        sn = sin_ref[...]
        rope_t = lambda t: t * cs + _swap_halves(t * sn)
        dkr = jnp.zeros((tr, LANES), F32)
        for h in range(MLA_H):
            dqf_ref[:, h * HW:h * HW + LANES] = dq_ref[:, h * HW:h * HW + LANES]
            dq_rope = dq_ref[:, h * HW + LANES:(h + 1) * HW].astype(F32)
            dqf_ref[:, h * HW + LANES:(h + 1) * HW] = rope_t(dq_rope).astype(BF16)
            dkvf_ref[:, h * HW:h * HW + LANES] = dk_ref[:, h * HW:h * HW + LANES]
            dkvf_ref[:, h * HW + LANES:(h + 1) * HW] = dv_ref[:, h * MLA_DV:(h + 1) * MLA_DV]
            dkr = dkr + dk_ref[:, h * HW + LANES:(h + 1) * HW].astype(F32)

        def norm_bwd(x, dn, g):
            r = lax.rsqrt(jnp.mean(x * x, axis=-1, keepdims=True) + EPS)
            xn = x * r
            t = dn * g
            return r * (t - xn * jnp.mean(t * xn, axis=-1, keepdims=True)), jnp.sum(dn * xn, axis=0, keepdims=True)

        cq = pb_ref[:, 0:Q_RANK].astype(F32)
        ckv = pb_ref[:, Q_RANK:Q_RANK + KV_RANK].astype(F32)
        dcq, dgq = norm_bwd(cq, _nt(dqf_ref[...], wuq_ref[...]), gq_ref[...])
        dckv, dgkv = norm_bwd(ckv, _nt(dkvf_ref[...], wukv_ref[...]), gkv_ref[...])
        dgq_ref[...] += dgq
        dgkv_ref[...] += dgkv
        de_ref[:, 0:Q_RANK] = dcq.astype(BF16)
        de_ref[:, Q_RANK:Q_RANK + KV_RANK] = dckv.astype(BF16)
        de_ref[:, 384:512] = rope_t(dkr).astype(BF16)

    rows = lambda w: pl.BlockSpec((tr, w), lambda b, j: (b * nt + j, 0))
    const = lambda s: pl.BlockSpec(s, lambda b, j: (0, 0))
    blocks = (2 * _nbytes((tr, 2048), F32) + _nbytes((tr, 1024), F32) + _nbytes((tr, 640), F32)
              + 2 * _nbytes((tr, 2048), BF16) + _nbytes((2048, 384), BF16) + 2 * _nbytes((tr, 2048), F32))
    return pl.pallas_call(
        body,
        out_shape=(jax.ShapeDtypeStruct((T, 2048), BF16), jax.ShapeDtypeStruct((T, 2048), BF16),
                   jax.ShapeDtypeStruct((T, 512), BF16), jax.ShapeDtypeStruct((1, Q_RANK), F32),
                   jax.ShapeDtypeStruct((1, KV_RANK), F32)),
        grid=(B, nt),
        in_specs=[rows(2048), rows(2048), rows(1024), rows(640),
                  pl.BlockSpec((tr, 128), lambda b, j: (j, 0)), pl.BlockSpec((tr, 128), lambda b, j: (j, 0)),
                  const((1, Q_RANK)), const((1, KV_RANK)), const((Q_RANK, 2048)), const((KV_RANK, 2048))],
        out_specs=(rows(2048), rows(2048), rows(512), const((1, Q_RANK)), const((1, KV_RANK))),
        compiler_params=_params(("arbitrary", "arbitrary"), blocks),
        name="mla_bwd_post",
    )(dq, dk, dv, projB, cos_t, sin_t, gq, gkv, wuq2, wukv)


def _in_proj_bwd(x, meta, dh1, dA, dBz, dC, dDz, dE, wA, wB, g, B, Lp):
    NQ = Lp // QB
    seq = x.shape[1]

    def body(x_ref, meta_ref, dh_ref, da_ref, db_ref, dc_ref, dd_ref, de_ref, wa_ref, wb_ref, g_ref,
             gx_ref, dmeta_ref, dg_ref):
        b = pl.program_id(0)
        j = pl.program_id(1)

        @pl.when((b == 0) & (j == 0))
        def _():
            dg_ref[...] = jnp.zeros_like(dg_ref)

        du = _nt(da_ref[...], wa_ref[:, 3072:5120])
        du = du + _nt(db_ref[...], wa_ref[:, 1024:2048])
        du = du + _nt(dd_ref[...], wa_ref[:, 2048:3072])
        du = du + _nt(dc_ref[:, 0:1024], wa_ref[:, 0:1024])
        du = du + _nt(dc_ref[:, 1024:2048], wa_ref[:, 5120:6144])
        du = du + _nt(dc_ref[:, 2048:2176], wb_ref[:, 384:512])
        du = du + _nt(de_ref[:, 0:384], wb_ref[:, 0:384])
        du = du + _nt(de_ref[:, 384:512], wb_ref[:, 512:640])

        x = _h_tile(j, x_ref, meta_ref)
        r = lax.rsqrt(jnp.mean(x * x, axis=-1, keepdims=True) + EPS)
        xn = x * r
        t = du * g_ref[...]
        dh0 = dh_ref[...] + r * (t - xn * jnp.mean(t * xn, axis=-1, keepdims=True))
        dg_ref[...] += jnp.sum(du * xn, axis=0, keepdims=True)
        gx_ref[0] = dh0

        @pl.when((j == 0) & (b == 0))
        def _():
            dmeta_ref[...] = dh0[FRONT:HEAD_ROWS, :]

        @pl.when((j == 0) & (b > 0))
        def _():
            dmeta_ref[...] += dh0[FRONT:HEAD_ROWS, :]

    rows = lambda w: pl.BlockSpec((QB, w), lambda b, j: (b * NQ + j, 0))
    const = lambda s: pl.BlockSpec(s, lambda b, j: (0, 0))
    widths = [a.shape[1] for a in (dA, dBz, dC, dDz, dE)]
    blocks = (sum(_nbytes((QB, w), BF16) for w in widths) + _nbytes(wA.shape, BF16) + _nbytes(wB.shape, BF16)
              + 4 * _nbytes((QB, D), F32))
    return pl.pallas_call(
        body,
        out_shape=(jax.ShapeDtypeStruct((B, seq, D), F32), jax.ShapeDtypeStruct((N_META, D), F32),
                   jax.ShapeDtypeStruct((1, D), F32)),
        grid=(B, NQ),
        in_specs=[_x_spec(), const((N_META, D)), rows(D)] + [rows(w) for w in widths]
        + [const(wA.shape), const(wB.shape), const((1, D))],
        out_specs=(_x_spec(), const((N_META, D)), const((1, D))),
        compiler_params=_params(("arbitrary", "arbitrary"), blocks),
        name="in_proj_bwd",
    )(x, meta, dh1, dA, dBz, dC, dDz, dE, wA, wB, g)


_VMEM_WHOLE = pl.BlockSpec(memory_space=pltpu.VMEM)


def _params_whole(arrays):
    total = sum(_nbytes(a.shape, a.dtype) for a in arrays)
    return pltpu.CompilerParams(vmem_limit_bytes=int(min(total + 12 * 1024 * 1024, VMEM_CAP_V7X)))


def _wire_dtype(shape):
    return BF16 if shape[-2] * shape[-1] >= WIRE_BF16_MIN_ELEMS else F32


def _pair_add_big(gp, recv, c):
    _, half, cols = recv.shape
    th = _div_tile(half, 64, 16)
    out_dtype = _wire_dtype(recv.shape)

    steps = half // th

    def body(c_ref, a_ref, b_ref, o_ref):
        o_ref[...] = (a_ref[...] + b_ref[...]).astype(out_dtype)

    return pl.pallas_call(
        body,
        out_shape=jax.ShapeDtypeStruct(recv.shape, out_dtype),
        grid_spec=pltpu.PrefetchScalarGridSpec(
            num_scalar_prefetch=1,
            grid=(steps,),
            in_specs=[pl.BlockSpec((4, th, cols), lambda i, c_ref: (0, c_ref[0] * steps + i, 0)),
                      pl.BlockSpec((4, th, cols), lambda i, c_ref: (0, i, 0))],
            out_specs=pl.BlockSpec((4, th, cols), lambda i, c_ref: (0, i, 0)),
        ),
        compiler_params=_params(("parallel",), 3 * _nbytes((4, th, cols), F32)),
        name="grad_pair_add_big",
    )(c, gp, recv)


def _pair_add_small(gps, recvs):
    n = len(gps)

    def body(*refs):
        c = lax.axis_index("c")
        for t in range(n):
            g_ref, r_ref, o_ref = refs[t], refs[n + t], refs[2 * n + t]
            half = r_ref.shape[1]
            s = g_ref[:, pl.ds(pl.multiple_of(c * half, 8), half), :] + r_ref[...]
            o_ref[...] = s.astype(o_ref.dtype)

    return pl.pallas_call(
        body,
        out_shape=[jax.ShapeDtypeStruct(r.shape, _wire_dtype(r.shape)) for r in recvs],
        in_specs=[_VMEM_WHOLE] * (2 * n),
        out_specs=[_VMEM_WHOLE] * n,
        compiler_params=_params_whole(list(gps) + 2 * list(recvs)),
        name="grad_pair_add_small",
    )(*gps, *recvs)


def _chip_order_sum(landed_ref, own_ref, me):
    p = [jnp.where(me == k, own_ref[k], landed_ref[k]).astype(F32) for k in range(4)]
    return ((p[0] + p[1]) + p[2]) + p[3]


def _sum_chips_big(landed, own, pos):
    _, half, cols = landed.shape
    th = _div_tile(half, 64, 16)

    def body(pos_ref, l_ref, s_ref, o_ref):
        o_ref[0] = _chip_order_sum(l_ref, s_ref, pos_ref[1])

    spec = pl.BlockSpec((4, th, cols), lambda i, pos_ref: (0, i, 0))
    return pl.pallas_call(
        body,
        out_shape=jax.ShapeDtypeStruct((2, half, cols), F32),
        grid_spec=pltpu.PrefetchScalarGridSpec(
            num_scalar_prefetch=1,
            grid=(half // th,),
            in_specs=[spec, spec],
            out_specs=pl.BlockSpec((1, th, cols), lambda i, pos_ref: (pos_ref[0], i, 0)),
        ),
        compiler_params=_params(("parallel",), 3 * _nbytes((4, th, cols), F32)),
        name="grad_sum_chips_big",
    )(pos, landed, own)


def _sum_chips_small(landed, own):
    n = len(landed)

    def body(*refs):
        x, y, c = _mesh_pos()
        for t in range(n):
            refs[2 * n + t][c] = _chip_order_sum(refs[t], refs[n + t], 2 * x + y)

    return pl.pallas_call(
        body,
        out_shape=[jax.ShapeDtypeStruct((2,) + p.shape[1:], F32) for p in landed],
        in_specs=[_VMEM_WHOLE] * (2 * n),
        out_specs=[_VMEM_WHOLE] * n,
        compiler_params=_params_whole(list(landed) * 3),
        name="grad_sum_chips_small",
    )(*landed, *own)


def _adamw_update(w_ref, g_ref, m_ref, v_ref, d_ref, mo_ref, vo_ref):
    c1 = 1.0 - ADAM_B1 ** ADAM_STEP
    c2 = 1.0 - ADAM_B2 ** ADAM_STEP
    gv = g_ref[...]
    mn = ADAM_B1 * m_ref[...] + (1.0 - ADAM_B1) * gv
    vn = ADAM_B2 * v_ref[...] + (1.0 - ADAM_B2) * (gv * gv)
    mo_ref[...] = mn
    vo_ref[...] = vn
    d_ref[...] = -ADAM_LR * ((mn / c1) / (jnp.sqrt(vn / c2) + ADAM_EPS) + ADAM_WD * w_ref[...])


def _adamw_big(w, g, m, v):
    lead, (rows, cols) = w.shape[:-2], w.shape[-2:]
    assert all(n == 1 for n in lead)
    tr = _div_tile(rows, (1 << 19) // cols, 8)
    spec = pl.BlockSpec((1,) * len(lead) + (tr, cols), lambda i: (0,) * len(lead) + (i, 0))
    shp = jax.ShapeDtypeStruct(w.shape, F32)
    return pl.pallas_call(
        functools.partial(_adamw_update),
        out_shape=(shp, shp, shp),
        grid=(rows // tr,),
        in_specs=[spec] * 4,
        out_specs=(spec, spec, spec),
        compiler_params=_params(("parallel",), 7 * _nbytes((tr, cols), F32)),
        name="adamw_big",
    )(w, g, m, v)


def _adamw_small(ws, gs, ms, vs):
    n = len(ws)

    def body(*refs):
        for t in range(n):
            _adamw_update(refs[t], refs[n + t], refs[2 * n + t], refs[3 * n + t],
                          refs[4 * n + t], refs[5 * n + t], refs[6 * n + t])

    shapes = [jax.ShapeDtypeStruct(w.shape, F32) for w in ws]
    return pl.pallas_call(
        body,
        out_shape=shapes * 3,
        in_specs=[_VMEM_WHOLE] * (4 * n),
        out_specs=[_VMEM_WHOLE] * (3 * n),
        compiler_params=_params_whole(list(ws) * 7),
        name="adamw_small",
    )(*ws, *gs, *ms, *vs)


def _mesh_pos():
    return lax.axis_index("x"), lax.axis_index("y"), lax.axis_index("c")


def _other_chips(x, y):
    return [(1 - x, y), (x, 1 - y), (1 - x, 1 - y)]


_ANY = pl.BlockSpec(memory_space=pl.ANY)


PAIR_SPLIT_MIN_ROWS = 64


def _weight_gather(shards):
    n = len(shards)
    split = [s.shape[0] >= PAIR_SPLIT_MIN_ROWS for s in shards]

    def body(*refs):
        w_refs, o_refs = refs[:n], refs[n:2 * n]
        send_sems, recv_sems = refs[2 * n:]
        x, y, c = _mesh_pos()
        me = 2 * x + y
        chips = _other_chips(x, y)

        def rows_of(t, core):
            rows = shards[t].shape[0]
            if not split[t]:
                return pl.ds(0, rows)
            return pl.ds(pl.multiple_of(core * (rows // 2), 16), rows // 2)

        def landed(t, k, slot, rows, to):
            ref = o_refs[t].at[slot, rows]
            return pltpu.make_async_remote_copy(src_ref=ref, dst_ref=ref, send_sem=send_sems.at[6 * t + k],
                                                recv_sem=recv_sems.at[6 * t + k], device_id=to, device_id_type=MESH)

        sends = []
        for t in range(n):
            mine = rows_of(t, c)
            for k, (px, py) in enumerate(chips):
                cp = pltpu.make_async_remote_copy(src_ref=w_refs[t].at[mine], dst_ref=o_refs[t].at[me, mine],
                                                  send_sem=send_sems.at[6 * t + k], recv_sem=recv_sems.at[6 * t + k],
                                                  device_id=(px, py, c), device_id_type=MESH)
                cp.start()
                sends.append(cp)
        for t in range(n):
            mine = rows_of(t, c)
            for k, (px, py) in enumerate(chips):
                landed(t, k, 2 * px + py, mine, (x, y, c)).wait_recv()
                if split[t]:
                    cp = landed(t, 3 + k, 2 * px + py, mine, (x, y, 1 - c))
                    cp.start()
                    sends.append(cp)
        for t in range(n):
            if split[t]:
                for k, (px, py) in enumerate(chips):
                    landed(t, 3 + k, 2 * px + py, rows_of(t, 1 - c), (x, y, c)).wait_recv()
        for cp in sends:
            cp.wait_send()

    return pl.pallas_call(
        body,
        out_shape=[jax.ShapeDtypeStruct((4,) + s.shape, s.dtype) for s in shards],
        in_specs=[_ANY] * n,
        out_specs=[_ANY] * n,
        scratch_shapes=[pltpu.SemaphoreType.DMA((6 * n,)), pltpu.SemaphoreType.DMA((6 * n,))],
        name="weight_gather",
    )(*shards)


def _pair_swap(gps):
    n = len(gps)

    def body(*refs):
        g_refs, o_refs = refs[:n], refs[n:2 * n]
        send_sems, recv_sems = refs[2 * n:]
        x, y, c = _mesh_pos()
        copies = []
        for t in range(n):
            half = gps[t].shape[1] // 2
            theirs = pl.ds(pl.multiple_of((1 - c) * half, 8), half)
            cp = pltpu.make_async_remote_copy(src_ref=g_refs[t].at[:, theirs], dst_ref=o_refs[t],
                                              send_sem=send_sems.at[t], recv_sem=recv_sems.at[t],
                                              device_id=(x, y, 1 - c), device_id_type=MESH)
            cp.start()
            copies.append(cp)
        for cp in copies:
            cp.wait_send()
            cp.wait_recv()

    return pl.pallas_call(
        body,
        out_shape=[jax.ShapeDtypeStruct((4, g.shape[1] // 2, g.shape[2]), g.dtype) for g in gps],
        in_specs=[_ANY] * n,
        out_specs=[_ANY] * n,
        scratch_shapes=[pltpu.SemaphoreType.DMA((n,)), pltpu.SemaphoreType.DMA((n,))],
        name="grad_pair_swap",
    )(*gps)


def _chip_scatter(parts):
    n = len(parts)

    def body(*refs):
        s_refs, o_refs = refs[:n], refs[n:2 * n]
        send_sems, recv_sems = refs[2 * n:]
        x, y, c = _mesh_pos()
        me = 2 * x + y
        chips = _other_chips(x, y)
        sends = []
        for t in range(n):
            for k, (px, py) in enumerate(chips):
                cp = pltpu.make_async_remote_copy(src_ref=s_refs[t].at[2 * px + py], dst_ref=o_refs[t].at[me],
                                                  send_sem=send_sems.at[3 * t + k], recv_sem=recv_sems.at[3 * t + k],
                                                  device_id=(px, py, c), device_id_type=MESH)
                cp.start()
                sends.append(cp)
        for t in range(n):
            for k, (px, py) in enumerate(chips):
                pltpu.make_async_remote_copy(src_ref=s_refs[t].at[me], dst_ref=o_refs[t].at[2 * px + py],
                                             send_sem=send_sems.at[3 * t + k], recv_sem=recv_sems.at[3 * t + k],
                                             device_id=(x, y, c), device_id_type=MESH).wait_recv()
        for cp in sends:
            cp.wait_send()

    return pl.pallas_call(
        body,
        out_shape=[jax.ShapeDtypeStruct(p.shape, p.dtype) for p in parts],
        in_specs=[_ANY] * n,
        out_specs=[_ANY] * n,
        scratch_shapes=[pltpu.SemaphoreType.DMA((3 * n,)), pltpu.SemaphoreType.DMA((3 * n,))],
        name="grad_chip_scatter",
    )(*parts)


_HBM = pl.BlockSpec(memory_space=pltpu.HBM)
_SEM = pl.BlockSpec(memory_space=pltpu.SEMAPHORE)


def _in_hbm(a):
    return pltpu.with_memory_space_constraint(a, pltpu.HBM)


def _chip_scatter_start(parts):
    n = len(parts)

    def body(*refs):
        s_refs, l_refs = refs[:n], refs[n:2 * n]
        send_sems, recv_sems = refs[2 * n], refs[2 * n + 1]
        token = refs[-1]
        x, y, c = _mesh_pos()
        me = 2 * x + y
        for t in range(n):
            for k, (px, py) in enumerate(_other_chips(x, y)):
                pltpu.make_async_remote_copy(src_ref=s_refs[t].at[2 * px + py], dst_ref=l_refs[t].at[me],
                                             send_sem=send_sems.at[3 * t + k], recv_sem=recv_sems.at[3 * t + k],
                                             device_id=(px, py, c), device_id_type=MESH).start()
        token[...] = jnp.zeros_like(token)

    hbm = [pltpu.HBM(p.shape, p.dtype) for p in parts]
    outs = pl.pallas_call(
        body,
        name="grad_scatter_start",
        out_shape=(pltpu.SemaphoreType.DMA((3 * n,)), pltpu.SemaphoreType.DMA((3 * n,)), *hbm, *hbm,
                   jax.ShapeDtypeStruct((8, LANES), F32)),
        in_specs=[_HBM] * (2 * n),
        out_specs=(_SEM, _SEM, *([_HBM] * (2 * n)), pl.BlockSpec(memory_space=pltpu.VMEM)),
        input_output_aliases={i: 2 + i for i in range(2 * n)},
        compiler_params=pltpu.CompilerParams(has_side_effects=pltpu.SideEffectType.DATAFLOW_SIDE_EFFECTING),
    )(*[_in_hbm(p) for p in parts], *[_in_hbm(lax.empty(p.shape, p.dtype)) for p in parts])
    return outs[0], outs[1], list(outs[2:2 + n]), list(outs[2 + n:2 + 2 * n]), outs[-1]


def _chip_scatter_wait(send_sems, recv_sems, parts, lands, after):
    n = len(parts)

    def body(*refs):
        s_refs, l_refs = refs[:n], refs[n:2 * n]
        send_sems, recv_sems = refs[2 * n], refs[2 * n + 1]
        x, y, c = _mesh_pos()
        me = 2 * x + y
        for t in range(n):
            for k, (px, py) in enumerate(_other_chips(x, y)):
                cp = pltpu.make_async_remote_copy(src_ref=s_refs[t].at[2 * px + py], dst_ref=l_refs[t].at[2 * px + py],
                                                  send_sem=send_sems.at[3 * t + k], recv_sem=recv_sems.at[3 * t + k],
                                                  device_id=(x, y, c), device_id_type=MESH)
                cp.wait_send()
                cp.wait_recv()

    hbm = [pltpu.HBM(p.shape, p.dtype) for p in parts]
    outs = pl.pallas_call(
        body,
        name="grad_scatter_wait",
        out_shape=(*hbm, *hbm),
        in_specs=[_HBM] * (2 * n) + [_SEM, _SEM, _ANY],
        out_specs=[_HBM] * (2 * n),
        input_output_aliases={i: i for i in range(2 * n)},
        compiler_params=pltpu.CompilerParams(has_side_effects=pltpu.SideEffectType.DATAFLOW_SIDE_EFFECTING),
    )(*parts, *lands, send_sems, recv_sems, after)
    return list(outs[:n]), list(outs[n:])


def _all_to_all_small(parts):
    n = len(parts)

    def body(*refs):
        p_refs, o_refs = refs[:n], refs[n:2 * n]
        send_sems, recv_sems = refs[2 * n:]
        x, y, c = _mesh_pos()
        me = 4 * x + 2 * y + c
        sends = []
        for t in range(n):
            for k in range(1, 8):
                px, py, pc = x ^ (k >> 2), y ^ ((k >> 1) & 1), c ^ (k & 1)
                cp = pltpu.make_async_remote_copy(src_ref=p_refs[t], dst_ref=o_refs[t].at[me],
                                                  send_sem=send_sems.at[7 * t + k - 1], recv_sem=recv_sems.at[7 * t + k - 1],
                                                  device_id=(px, py, pc), device_id_type=MESH)
                cp.start()
                sends.append(cp)
        for t in range(n):
            for k in range(1, 8):
                peer = 4 * (x ^ (k >> 2)) + 2 * (y ^ ((k >> 1) & 1)) + (c ^ (k & 1))
                pltpu.make_async_remote_copy(src_ref=p_refs[t], dst_ref=o_refs[t].at[peer],
                                             send_sem=send_sems.at[7 * t + k - 1], recv_sem=recv_sems.at[7 * t + k - 1],
                                             device_id=(x, y, c), device_id_type=MESH).wait_recv()
        for cp in sends:
            cp.wait_send()

    return pl.pallas_call(
        body,
        out_shape=[jax.ShapeDtypeStruct((8,) + p.shape, p.dtype) for p in parts],
        in_specs=[_ANY] * n,
        out_specs=[_ANY] * n,
        scratch_shapes=[pltpu.SemaphoreType.DMA((7 * n,)), pltpu.SemaphoreType.DMA((7 * n,))],
        name="grad_small_all_to_all",
    )(*parts)


def _sum_devices_small(landed, own):
    n = len(landed)

    def body(*refs):
        x, y, c = _mesh_pos()
        me = 4 * x + 2 * y + c
        for t in range(n):
            acc = jnp.where(me == 0, refs[n + t][...], refs[t][0])
            for d in range(1, 8):
                acc = acc + jnp.where(me == d, refs[n + t][...], refs[t][d])
            refs[2 * n + t][...] = acc

    return pl.pallas_call(
        body,
        out_shape=[jax.ShapeDtypeStruct(p.shape, F32) for p in own],
        in_specs=[_VMEM_WHOLE] * (2 * n),
        out_specs=[_VMEM_WHOLE] * n,
        compiler_params=_params_whole(list(landed) + 2 * list(own)),
        name="grad_sum_devices_small",
    )(*landed, *own)


def _pair_join(fs):
    n = len(fs)

    def body(*refs):
        f_refs, o_refs = refs[:n], refs[n:2 * n]
        send_sems, recv_sems = refs[2 * n:]
        x, y, c = _mesh_pos()
        sends = []
        for t in range(n):
            cp = pltpu.make_async_remote_copy(src_ref=f_refs[t].at[c], dst_ref=o_refs[t].at[c], send_sem=send_sems.at[t],
                                              recv_sem=recv_sems.at[t], device_id=(x, y, 1 - c), device_id_type=MESH)
            cp.start()
            sends.append(cp)
        for t in range(n):
            pltpu.make_async_remote_copy(src_ref=f_refs[t].at[c], dst_ref=o_refs[t].at[1 - c], send_sem=send_sems.at[t],
                                         recv_sem=recv_sems.at[t], device_id=(x, y, c), device_id_type=MESH).wait_recv()
        for cp in sends:
            cp.wait_send()

    return pl.pallas_call(
        body,
        out_shape=[jax.ShapeDtypeStruct(f.shape, f.dtype) for f in fs],
        in_specs=[_ANY] * n,
        out_specs=[_ANY] * n,
        input_output_aliases={t: t for t in range(n)},
        scratch_shapes=[pltpu.SemaphoreType.DMA((n,)), pltpu.SemaphoreType.DMA((n,))],
        name="grad_pair_join",
    )(*fs)


def _rope_tables(Lp):
    inv = 1.0 / (ROPE_BASE ** (jnp.arange(0, ROPE, 2, dtype=F32) / ROPE))
    ang = (jnp.arange(Lp, dtype=F32) - FRONT)[:, None] * inv[None, :]
    cs, sn = jnp.cos(ang), jnp.sin(ang)
    return jnp.tile(cs, (1, 4)), jnp.concatenate([-sn, sn, -sn, sn], axis=1)


def _local_step(x, loss_target, meta, norm_g, w_in, gate_w, gate_b, gla_norm_g, gla_proj, q_norm_g, w_uq,
                kv_norm_g, w_ukv, mla_proj, w_out, final_norm_g, early_grads_hook=None):
    B, seq, _ = x.shape
    Lp = HEAD_ROWS + seq
    T = B * Lp
    tr = _div_tile(Lp, 544, 16)
    tq = _div_tile(T, 1024, QB)
    tkw = _div_tile(T, Lp, QB)

    cuts = np.cumsum((0,) + SPLITS)
    shard_w = IN_WIDTH // 4

    def w_cols(i, width=None):
        parts = []
        for j in range(4):
            a, b = max(cuts[i], j * shard_w), min(cuts[i + 1], (j + 1) * shard_w)
            if a < b:
                parts.append(w_in[j][:, a - j * shard_w:b - j * shard_w])
        if width is not None:
            parts.append(jnp.zeros((D, width - (cuts[i + 1] - cuts[i])), w_in.dtype))
        return parts

    i_q, i_k, i_v, i_lr, i_z, i_cq, i_ckv, i_kr, i_mz, i_gg, i_gm = range(11)
    wA = jnp.concatenate(sum([w_cols(i) for i in (i_v, i_z, i_mz, i_gg, i_gm, i_q, i_k)], []), axis=1)
    wB = jnp.concatenate(w_cols(i_cq) + w_cols(i_ckv) + w_cols(i_lr, 128) + w_cols(i_kr, 128), axis=1)
    wg = jnp.pad(gate_w, ((0, 128 - GLA_RANK), (0, 0)))
    wuq2 = jnp.pad(w_uq.reshape(Q_RANK, MLA_H, MLA_QK), ((0, 0), (0, 0), (0, 256 - MLA_QK))).reshape(Q_RANK, 2048)
    gn4 = jnp.tile(gla_norm_g, (1, GLA_H))
    cos_t, sin_t = _rope_tables(Lp)

    u = _rms_in(x, meta, norm_g, B, Lp)
    projA = _mm(u, wA, name="in_proj_a", out_dtype=BF16, tm=tq, tn=1024, tk=D)
    projB = _mm(u, wB, name="in_proj_b", out_dtype=BF16, tm=tq, tn=640, tk=D)
    oa, ya_in, ssave = _gla_fwd(projA, projB, wg, gate_b, gn4, B, Lp)
    ya = _mm(ya_in, gla_proj, name="gla_proj", out_dtype=BF16, tm=tq, tn=D, tk=D)
    q_att, k_att, v_att, cqn, ckvn = _mla_prep(projB, cos_t, sin_t, q_norm_g, kv_norm_g, wuq2, w_ukv, B, Lp, tr)
    ob, yb_in, lse_c = _attn_fwd(q_att, k_att, v_att, projA, B, Lp)
    yb = _mm(yb_in, mla_proj, name="mla_proj", out_dtype=BF16, tm=tq, tn=D, tk=D)
    dh1, dh1_b, merged, loss, d_gf = _out_proj_loss(x, meta, projA, ya, yb, w_out, final_norm_g.reshape(1, D),
                                                     loss_target, B, Lp)

    g_w_out = _mm(merged, dh1_b, name="dw_out", trans_a=True, tm=D, tn=D, tk=tkw)
    dya, dyb, dA = _merge_bwd(dh1_b, w_out, projA, ya, yb, tr)
    g_gla_proj = _mm(ya_in, dya, name="dw_gla_proj", trans_a=True, tm=D, tn=D, tk=tkw)
    g_mla_proj = _mm(yb_in, dyb, name="dw_mla_proj", trans_a=True, tm=D, tn=D, tk=tkw)
    doa, dBz, d_gn = _gla_out_bwd(dya, gla_proj, oa, projA, gn4, tr)
    dC, g_wg, d_bg = _gla_bwd(projA, projB, ssave, doa, wg, gate_b, B, Lp)
    do, dDz, delta_c = _attn_bwd_pre(dyb, mla_proj, projA, ob, B, Lp)
    dq, dk, dv = _attn_bwd(q_att, k_att, v_att, do, lse_c, delta_c, B, Lp)
    dqf, dkvf, dE, d_gq, d_gkv = _mla_bwd_post(dq, dk, dv, projB, cos_t, sin_t, q_norm_g, kv_norm_g,
                                                wuq2, w_ukv, B, Lp, tr)
    g_wuq2 = _mm(cqn, dqf, name="dw_uq", trans_a=True, tm=Q_RANK, tn=2048, tk=tkw)
    g_wukv = _mm(ckvn, dkvf, name="dw_ukv", trans_a=True, tm=KV_RANK, tn=2048, tk=tkw)
    dparts = [dA, dBz, dC, dDz, dE]
    g_in = [_mm(u, dp, name="dw_in_%d" % i, trans_a=True, tm=D, tn=_div_tile(dp.shape[1], 1024, 256), tk=tkw)
            for i, dp in enumerate(dparts)]

    gA, gBz, gC, gDz, gE = g_in
    src = [(gC, 1024), (gC, 1536), (gC, 0), (gC, 2048), (gBz, 0), (gE, 0), (gE, Q_RANK), (gE, 384), (gDz, 0),
           (gA, 0), (gA, D)]
    owners = []
    for j in range(4):
        parts = []
        for i, (arr, off) in enumerate(src):
            a, b = max(cuts[i], j * shard_w), min(cuts[i + 1], (j + 1) * shard_w)
            if a < b:
                parts.append(arr[:, off + a - cuts[i]:off + b - cuts[i]])
        owners.append(jnp.concatenate(parts, axis=1))
    g_w_in = jnp.stack(owners)
    g_wuq = g_wuq2.reshape(Q_RANK, MLA_H, 256)[:, :, :MLA_QK].reshape(Q_RANK, MLA_H * MLA_QK)
    grads = dict(w_in=g_w_in, gla_gate_w=g_wg[:GLA_RANK], gla_proj=g_gla_proj, mla_w_uq=g_wuq, mla_w_ukv=g_wukv,
                 mla_proj=g_mla_proj, w_out=g_w_out, gla_gate_b=d_bg,
                 gla_norm_g=d_gn, mla_q_norm_g=d_gq, mla_kv_norm_g=d_gkv, final_norm_g=d_gf)
    token = None if early_grads_hook is None else early_grads_hook(grads)
    ng = norm_g if token is None else norm_g + token[0:1, 0:1]
    grad_x, d_meta, d_ng = _in_proj_bwd(x, meta, dh1, dA, dBz, dC, dDz, dE, wA, wB, ng, B, Lp)
    grads.update(meta_tokens=d_meta, norm_g=d_ng)
    return loss[0, 0], grad_x, grads


_MATS = ("w_in", "gla_gate_w", "gla_proj", "mla_w_uq", "mla_w_ukv", "mla_proj", "w_out")
_ROW_SHARDED = ("gla_proj", "mla_proj", "w_out")
_ORDER = ("meta_tokens", "norm_g", "w_in", "gla_gate_w", "gla_gate_b", "gla_norm_g", "gla_proj", "mla_q_norm_g",
          "mla_w_uq", "mla_kv_norm_g", "mla_w_ukv", "mla_proj", "w_out", "final_norm_g")
WIRE_BF16_MIN_ELEMS = 128 * 128
SMALL_PACK_ROWS = 16


def _pack_small(d):
    rows = [jnp.pad(d[n].reshape(1, size), ((0, 0), (0, D - size))) for n, size in SMALL]
    return jnp.pad(jnp.concatenate(rows, axis=0), ((0, SMALL_PACK_ROWS - len(rows)), (0, 0)))


def _unpack_small(packed):
    return {n: packed[i, :size] for i, (n, size) in enumerate(SMALL)}


def kernel(x, meta_tokens, norm_g, w_in, gla_gate_w, gla_gate_b, gla_norm_g, gla_proj, mla_q_norm_g, mla_w_uq, mla_kv_norm_g, mla_w_ukv, mla_proj, w_out, final_norm_g, loss_target, m_meta_tokens, m_norm_g, m_w_in, m_gla_gate_w, m_gla_gate_b, m_gla_norm_g, m_gla_proj, m_mla_q_norm_g, m_mla_w_uq, m_mla_kv_norm_g, m_mla_w_ukv, m_mla_proj, m_w_out, m_final_norm_g, v_meta_tokens, v_norm_g, v_w_in, v_gla_gate_w, v_gla_gate_b, v_gla_norm_g, v_gla_proj, v_mla_q_norm_g, v_mla_w_uq, v_mla_kv_norm_g, v_mla_w_ukv, v_mla_proj, v_w_out, v_final_norm_g):
    w = dict(meta_tokens=meta_tokens, norm_g=norm_g, w_in=w_in[0], gla_gate_w=gla_gate_w[0], gla_gate_b=gla_gate_b,
             gla_norm_g=gla_norm_g, gla_proj=gla_proj[0], mla_q_norm_g=mla_q_norm_g, mla_w_uq=mla_w_uq[0],
             mla_kv_norm_g=mla_kv_norm_g, mla_w_ukv=mla_w_ukv[0], mla_proj=mla_proj[0], w_out=w_out[0],
             final_norm_g=final_norm_g)
    mom = dict(meta_tokens=m_meta_tokens, norm_g=m_norm_g, w_in=m_w_in[0], gla_gate_w=m_gla_gate_w[0],
               gla_gate_b=m_gla_gate_b, gla_norm_g=m_gla_norm_g, gla_proj=m_gla_proj[0], mla_q_norm_g=m_mla_q_norm_g,
               mla_w_uq=m_mla_w_uq[0], mla_kv_norm_g=m_mla_kv_norm_g, mla_w_ukv=m_mla_w_ukv[0], mla_proj=m_mla_proj[0],
               w_out=m_w_out[0], final_norm_g=m_final_norm_g)
    var = dict(meta_tokens=v_meta_tokens, norm_g=v_norm_g, w_in=v_w_in[0], gla_gate_w=v_gla_gate_w[0],
               gla_gate_b=v_gla_gate_b, gla_norm_g=v_gla_norm_g, gla_proj=v_gla_proj[0], mla_q_norm_g=v_mla_q_norm_g,
               mla_w_uq=v_mla_w_uq[0], mla_kv_norm_g=v_mla_kv_norm_g, mla_w_ukv=v_mla_w_ukv[0], mla_proj=v_mla_proj[0],
               w_out=v_w_out[0], final_norm_g=v_final_norm_g)
    out_shapes = {n: a.shape for n, a in zip(_ORDER, (meta_tokens, norm_g, w_in, gla_gate_w, gla_gate_b, gla_norm_g,
                                                     gla_proj, mla_q_norm_g, mla_w_uq, mla_kv_norm_g, mla_w_ukv,
                                                     mla_proj, w_out, final_norm_g))}

    me = (2 * lax.axis_index("x") + lax.axis_index("y")).astype(jnp.int32)
    shards = [w[n].astype(BF16) for n in _MATS] + [meta_tokens]
    is_mine = lax.broadcasted_iota(jnp.int32, (4, 1, 1), 0) == me
    gathered = [jnp.where(is_mine, own[None], gth) for gth, own in zip(_weight_gather(shards), shards)]
    full = {}
    for name, gth in zip(_MATS, gathered):
        if name == "w_in":
            full[name] = gth
        elif name in _ROW_SHARDED:
            full[name] = gth.reshape(4 * gth.shape[1], gth.shape[2])
        else:
            full[name] = gth.transpose(1, 0, 2).reshape(gth.shape[1], 4 * gth.shape[2])
    meta_full = gathered[-1].transpose(1, 0, 2).reshape(N_META, D)

    def by_owner(name, arr):
        if name == "w_in":
            return arr
        if name in _ROW_SHARDED:
            return arr.reshape(4, arr.shape[0] // 4, arr.shape[1])
        return arr.reshape(arr.shape[0], 4, arr.shape[1] // 4).transpose(1, 0, 2)

    c_idx = lax.axis_index("c").astype(jnp.int32).reshape(1)
    pos = jnp.stack([c_idx[0], me])
    in_flight = {}

    def start_matrix_reduce(early):
        gps = [by_owner(n, early[n]) for n in _MATS]
        recvs = _pair_swap(gps)
        s1 = [_pair_add_big(gps[0], recvs[0], c_idx)] + list(_pair_add_small(gps[1:], recvs[1:]))
        send_sems, recv_sems, parts, lands, token = _chip_scatter_start(s1)
        in_flight.update(send_sems=send_sems, recv_sems=recv_sems, parts=parts, lands=lands)
        return token

    loss_local, grad_x, g = _local_step(
        x, loss_target, meta_full, norm_g, full["w_in"], full["gla_gate_w"], gla_gate_b, gla_norm_g, full["gla_proj"],
        mla_q_norm_g, full["mla_w_uq"], mla_kv_norm_g, full["mla_w_ukv"], full["mla_proj"], full["w_out"], final_norm_g,
        early_grads_hook=start_matrix_reduce)
    loss = lax.psum(loss_local, ("x", "y", "c"))

    s1, landed = _chip_scatter_wait(in_flight["send_sems"], in_flight["recv_sems"], in_flight["parts"],
                                    in_flight["lands"], after=g["norm_g"])
    halves = [_sum_chips_big(landed[0], s1[0], pos)] + list(_sum_chips_small(landed[1:], s1[1:]))
    g_mats = [j.reshape(out_shapes[n]) for j, n in zip(_pair_join(halves), _MATS)]

    late = [g["meta_tokens"], _pack_small(g)]
    meta_sum, small_sum = _sum_devices_small(_all_to_all_small(late), late)
    g_meta = lax.dynamic_slice(meta_sum, (0, me * (D // 4)), (N_META, D // 4))
    names = _MATS + ("meta_tokens",)
    g_red = g_mats + [g_meta, small_sum]

    tens = lambda d: [d[n].reshape(out_shapes[n]) for n in names] + [_pack_small(d)]
    w_t, m_t, v_t = tens(w), tens(mom), tens(var)
    big = _adamw_big(w_t[0], g_red[0], m_t[0], v_t[0])
    rest = _adamw_small(w_t[1:], g_red[1:], m_t[1:], v_t[1:])
    k = len(names)
    results = {"grad": g_red}
    for i, kind in enumerate(("delta", "new_m", "new_v")):
        results[kind] = [big[i]] + list(rest[i * k:(i + 1) * k])

    outs = []
    for kind in ("grad", "delta", "new_m", "new_v"):
        vals = dict(zip(names, results[kind][:-1]))
        vals.update(_unpack_small(results[kind][-1]))
        outs += [vals[n].reshape(out_shapes[n]) for n in _ORDER]
    return (loss, grad_x, *outs)
```

```python
import functools
import math

import jax
import jax.numpy as jnp
import numpy as np
from jax import lax
from jax.experimental import pallas as pl
from jax.experimental.pallas import tpu as pltpu

F32 = jnp.float32
BF16 = jnp.bfloat16

D = 1024
N_META = 16
QB = 256
FRONT = QB - N_META
HEAD_ROWS = FRONT + N_META
assert FRONT % 64 == 48
EPS = 1e-6

GLA_H, GLA_DK, GLA_DV, GLA_RANK, GLA_C = 4, 128, 256, 16, 64
GLA_NORMALIZER = 16.0
GLA_KW, GLA_VW = GLA_H * GLA_DK, GLA_H * GLA_DV
MLA_H, NOPE, ROPE, MLA_DV, Q_RANK, KV_RANK = 8, 128, 64, 128, 256, 128
MLA_QK = NOPE + ROPE
ROPE_BASE = 10000.0
SPLITS = (GLA_KW, GLA_KW, GLA_VW, GLA_RANK, GLA_VW, Q_RANK, KV_RANK, ROPE, MLA_H * MLA_DV, D, D)
IN_WIDTH = sum(SPLITS)

ADAM_LR, ADAM_B1, ADAM_B2, ADAM_EPS, ADAM_WD, ADAM_STEP = 0.001, 0.9, 0.999, 1e-08, 0.01, 10

LANES = 128
VMEM_CAP_V7X = 56 * 1024 * 1024
MESH = pl.DeviceIdType.MESH
NEG = -1e30

SMALL = (("norm_g", D), ("gla_gate_b", GLA_KW), ("gla_norm_g", GLA_DV), ("mla_q_norm_g", Q_RANK),
         ("mla_kv_norm_g", KV_RANK), ("final_norm_g", D))


def _div_tile(n, target, mult):
    best = None
    for d in range(mult, min(n, target) + 1, mult):
        if n % d == 0:
            best = d
    assert best is not None, (n, target, mult)
    return best


def _params(sem, block_bytes, scratch_bytes=0):
    est = 2 * block_bytes + scratch_bytes + 12 * 1024 * 1024
    return pltpu.CompilerParams(dimension_semantics=sem, vmem_limit_bytes=int(min(max(est, 24 * 1024 * 1024), VMEM_CAP_V7X)))


def _nbytes(shape, dtype):
    return int(np.prod(shape)) * jnp.dtype(dtype).itemsize


def _sigmoid(x):
    return 1.0 / (1.0 + jnp.exp(-x))


def _nt(a, b):
    return lax.dot_general(a, b, (((1,), (1,)), ((), ())), preferred_element_type=F32)


def _tn(a, b):
    return lax.dot_general(a, b, (((0,), (0,)), ((), ())), preferred_element_type=F32)


def _nn(a, b):
    return jnp.dot(a, b, preferred_element_type=F32)


def _split3(x):
    a = x.astype(BF16)
    r = x - a.astype(F32)
    b = r.astype(BF16)
    c = (r - b.astype(F32)).astype(BF16)
    return a, b, c


def _mm(a, b, *, name, trans_a=False, trans_b=False, out_dtype=F32, tm, tn, tk):
    assert not (trans_a and trans_b)
    if trans_a:
        K, M = a.shape
    else:
        M, K = a.shape
    N = b.shape[0] if trans_b else b.shape[1]
    assert (b.shape[1] if trans_b else b.shape[0]) == K
    assert M % tm == 0 and N % tn == 0 and K % tk == 0, (name, M, N, K, tm, tn, tk)
    nk = K // tk

    def body(a_ref, b_ref, o_ref, *scratch):
        av = a_ref[...].astype(BF16)
        bv = b_ref[...].astype(BF16)
        prod = _tn(av, bv) if trans_a else (_nt(av, bv) if trans_b else _nn(av, bv))
        if nk == 1:
            o_ref[...] = prod.astype(out_dtype)
        else:
            acc = scratch[0]
            k = pl.program_id(2)

            @pl.when(k == 0)
            def _():
                acc[...] = prod

            @pl.when(k > 0)
            def _():
                acc[...] += prod

            @pl.when(k == nk - 1)
            def _():
                o_ref[...] = acc[...].astype(out_dtype)

    if trans_a:
        a_spec = pl.BlockSpec((tk, tm), lambda i, j, k: (k, i))
    else:
        a_spec = pl.BlockSpec((tm, tk), lambda i, j, k: (i, k))
    if trans_b:
        b_spec = pl.BlockSpec((tn, tk), lambda i, j, k: (j, k))
    else:
        b_spec = pl.BlockSpec((tk, tn), lambda i, j, k: (k, j))
    blocks = (_nbytes((tm, tk), a.dtype) + _nbytes((tk, tn), b.dtype) + _nbytes((tm, tn), out_dtype))
    scratch = [pltpu.VMEM((tm, tn), F32)] if nk > 1 else []
    return pl.pallas_call(
        body,
        out_shape=jax.ShapeDtypeStruct((M, N), out_dtype),
        grid=(M // tm, N // tn, nk),
        in_specs=[a_spec, b_spec],
        out_specs=pl.BlockSpec((tm, tn), lambda i, j, k: (i, j)),
        scratch_shapes=scratch,
        compiler_params=_params(("parallel", "parallel", "arbitrary"), blocks + _nbytes((tm, tn), F32),
                                _nbytes((tm, tn), F32) if nk > 1 else 0),
        name=name,
    )(a, b)


def _h_tile(j, x_ref, meta_ref):
    head = jnp.concatenate([jnp.zeros((FRONT, D), F32), meta_ref[...]], axis=0)
    return jnp.where(j > 0, x_ref[0], head)


def _x_spec():
    return pl.BlockSpec((1, QB, D), lambda b, j: (b, jnp.maximum(j - 1, 0), 0))


def _rms_in(x, meta, g, B, Lp):
    T = B * Lp
    NQ = Lp // QB

    def body(x_ref, meta_ref, g_ref, u_ref):
        h = _h_tile(pl.program_id(1), x_ref, meta_ref)
        r = lax.rsqrt(jnp.mean(h * h, axis=-1, keepdims=True) + EPS)
        u_ref[...] = (h * r * g_ref[...]).astype(BF16)

    return pl.pallas_call(
        body,
        out_shape=jax.ShapeDtypeStruct((T, D), BF16),
        grid=(B, NQ),
        in_specs=[_x_spec(), pl.BlockSpec((N_META, D), lambda b, j: (0, 0)), pl.BlockSpec((1, D), lambda b, j: (0, 0))],
        out_specs=pl.BlockSpec((QB, D), lambda b, j: (b * NQ + j, 0)),
        compiler_params=_params(("parallel", "parallel"), _nbytes((QB, D), F32) * 2),
        name="rms_in",
    )(x, meta, g)


def _gla_gate(lr, wg, bg, valid):
    pre = _nn(lr.astype(BF16), wg) + bg
    logsig = jnp.minimum(pre, 0.0) - jnp.log(1.0 + jnp.exp(-jnp.abs(pre)))
    return pre, jnp.where(valid, logsig / GLA_NORMALIZER, 0.0)


def _tri_masks():
    ri = lax.broadcasted_iota(jnp.int32, (GLA_C, GLA_C), 0)
    ci = lax.broadcasted_iota(jnp.int32, (GLA_C, GLA_C), 1)
    return ci <= ri, ci >= ri


def _cumsum_rows(x, ones_mask):
    w = jnp.where(ones_mask, 1.0, 0.0).astype(BF16)
    a, b, c = _split3(x)
    return _nn(w, a) + _nn(w, b) + _nn(w, c)


def _gla_fwd(projA, projB, wg, bg, gn4, B, Lp):
    T = B * Lp
    NC = Lp // GLA_C
    C = GLA_C
    scale = GLA_DK ** -0.5

    def body(q_ref, k_ref, v_ref, lr_ref, z_ref, wg_ref, bg_ref, gn_ref, oa_ref, ya_ref, ssave_ref, st_ref):
        n = pl.program_id(0)

        @pl.when(n == 0)
        def _():
            st_ref[...] = jnp.zeros_like(st_ref)

        pos = n * C + lax.broadcasted_iota(jnp.int32, (C, 1), 0)
        lower, _ = _tri_masks()
        is_last = lax.broadcasted_iota(jnp.int32, (C, 1), 0) == C - 1
        for b in range(B):
            ssave_ref[b, 0] = st_ref[b]
            _, glog = _gla_gate(lr_ref[b], wg_ref[...], bg_ref[...], pos >= FRONT)
            bcum = _cumsum_rows(glog, lower)
            for h in range(GLA_H):
                ks = slice(h * GLA_DK, (h + 1) * GLA_DK)
                vs = slice(h * GLA_DV, (h + 1) * GLA_DV)
                bh = bcum[:, ks]
                blast = jnp.sum(jnp.where(is_last, bh, 0.0), axis=0, keepdims=True)
                qh = q_ref[b, :, ks].astype(F32) * scale
                kh = k_ref[b, :, ks].astype(F32)
                qe = (qh * jnp.exp(bh)).astype(BF16)
                ke = (kh * jnp.exp(-bh)).astype(BF16)
                kl = (kh * jnp.exp(blast - bh)).astype(BF16)
                vh = v_ref[b, :, vs].astype(BF16)
                a = jnp.where(lower, _nt(qe, ke), 0.0).astype(BF16)
                st = st_ref[b, h]
                o = _nn(a, vh) + _nt(qe, st.astype(BF16))
                st_ref[b, h] = st * jnp.exp(blast) + _tn(vh, kl)
                oa_ref[b, :, vs] = o
                on = o * lax.rsqrt(jnp.mean(o * o, axis=-1, keepdims=True) + EPS) * gn_ref[:, vs]
                z = z_ref[b, :, vs].astype(F32)
                ya_ref[b, :, vs] = (on * (z * _sigmoid(z))).astype(BF16)

    blocks = B * (_nbytes((C, 512), F32) * 2 + _nbytes((C, 1024), F32) * 3 + _nbytes((C, 1024), BF16)
                  + _nbytes((GLA_H, GLA_DV, GLA_DK), F32)) + _nbytes((128, 512), BF16)
    state = _nbytes((B, GLA_H, GLA_DV, GLA_DK), F32)
    pa = projA.reshape(B, Lp, projA.shape[1])
    oa, ya, ssave = pl.pallas_call(
        body,
        out_shape=(jax.ShapeDtypeStruct((B, Lp, GLA_VW), F32), jax.ShapeDtypeStruct((B, Lp, GLA_VW), BF16),
                   jax.ShapeDtypeStruct((B, NC, GLA_H, GLA_DV, GLA_DK), F32)),
        grid=(NC,),
        in_specs=[
            pl.BlockSpec((B, C, 512), lambda n: (0, n, 10)),
            pl.BlockSpec((B, C, 512), lambda n: (0, n, 11)),
            pl.BlockSpec((B, C, 1024), lambda n: (0, n, 0)),
            pl.BlockSpec((B, C, 128), lambda n: (0, n, 3)),
            pl.BlockSpec((B, C, 1024), lambda n: (0, n, 1)),
            pl.BlockSpec((128, 512), lambda n: (0, 0)),
            pl.BlockSpec((1, 512), lambda n: (0, 0)),
            pl.BlockSpec((1, 1024), lambda n: (0, 0)),
        ],
        out_specs=(pl.BlockSpec((B, C, 1024), lambda n: (0, n, 0)),
                   pl.BlockSpec((B, C, 1024), lambda n: (0, n, 0)),
                   pl.BlockSpec((B, 1, GLA_H, GLA_DV, GLA_DK), lambda n: (0, n, 0, 0, 0))),
        scratch_shapes=[pltpu.VMEM((B, GLA_H, GLA_DV, GLA_DK), F32)],
        compiler_params=_params(("arbitrary",), blocks, state),
        name="gla_fwd",
    )(pa, pa, pa, projB.reshape(B, Lp, projB.shape[1]), pa, wg, bg, gn4)
    return oa.reshape(T, GLA_VW), ya.reshape(T, GLA_VW), ssave


def _swap_halves(x):
    lane = lax.broadcasted_iota(jnp.int32, x.shape, 1)
    return jnp.where((lane % 64) < 32, pltpu.roll(x, 96, 1), pltpu.roll(x, 32, 1))


def _mla_prep(projB, cos_t, sin_t, gq, gkv, wuq2, wukv, B, Lp, tr):
    T = B * Lp
    nt = Lp // tr
    HW = 2 * LANES

    def body(pb_ref, cos_ref, sin_ref, gq_ref, gkv_ref, wuq_ref, wukv_ref, q_ref, k_ref, v_ref, cqn_ref, ckvn_ref):
        cq = pb_ref[:, 0:Q_RANK].astype(F32)
        ckv = pb_ref[:, Q_RANK:Q_RANK + KV_RANK].astype(F32)
        kr = pb_ref[:, 512:640].astype(F32)
        cqn = (cq * lax.rsqrt(jnp.mean(cq * cq, axis=-1, keepdims=True) + EPS) * gq_ref[...]).astype(BF16)
        ckvn = (ckv * lax.rsqrt(jnp.mean(ckv * ckv, axis=-1, keepdims=True) + EPS) * gkv_ref[...]).astype(BF16)
        cqn_ref[...] = cqn
        ckvn_ref[...] = ckvn
        qf = _nn(cqn, wuq_ref[...])
        kvf = _nn(ckvn, wukv_ref[...])
        cs = cos_ref[...]
        sn = sin_ref[...]
        rope = lambda t: t * cs + _swap_halves(t) * sn
        kr_r = rope(kr).astype(BF16)
        for h in range(MLA_H):
            q_ref[:, h * HW:h * HW + LANES] = qf[:, h * HW:h * HW + LANES].astype(BF16)
            q_ref[:, h * HW + LANES:(h + 1) * HW] = rope(qf[:, h * HW + LANES:(h + 1) * HW]).astype(BF16)
            k_ref[:, h * HW:h * HW + LANES] = kvf[:, h * HW:h * HW + LANES].astype(BF16)
            k_ref[:, h * HW + LANES:(h + 1) * HW] = kr_r
            v_ref[:, h * MLA_DV:(h + 1) * MLA_DV] = kvf[:, h * HW + LANES:(h + 1) * HW].astype(BF16)

    blocks = (_nbytes((tr, 640), F32) + 2 * _nbytes((tr, 128), F32) + _nbytes((Q_RANK, 2048), BF16)
              + _nbytes((KV_RANK, 2048), BF16) + _nbytes((tr, 2048 * 2 + 1024 + 384), BF16)
              + 2 * _nbytes((tr, 2048), F32))
    return pl.pallas_call(
        body,
        out_shape=(jax.ShapeDtypeStruct((T, MLA_H * HW), BF16), jax.ShapeDtypeStruct((T, MLA_H * HW), BF16),
                   jax.ShapeDtypeStruct((T, MLA_H * MLA_DV), BF16), jax.ShapeDtypeStruct((T, Q_RANK), BF16),
                   jax.ShapeDtypeStruct((T, KV_RANK), BF16)),
        grid=(B, nt),
        in_specs=[
            pl.BlockSpec((tr, 640), lambda b, j: (b * nt + j, 0)),
            pl.BlockSpec((tr, 128), lambda b, j: (j, 0)),
            pl.BlockSpec((tr, 128), lambda b, j: (j, 0)),
            pl.BlockSpec((1, Q_RANK), lambda b, j: (0, 0)),
            pl.BlockSpec((1, KV_RANK), lambda b, j: (0, 0)),
            pl.BlockSpec((Q_RANK, 2048), lambda b, j: (0, 0)),
            pl.BlockSpec((KV_RANK, 2048), lambda b, j: (0, 0)),
        ],
        out_specs=(pl.BlockSpec((tr, 2048), lambda b, j: (b * nt + j, 0)),
                   pl.BlockSpec((tr, 2048), lambda b, j: (b * nt + j, 0)),
                   pl.BlockSpec((tr, 1024), lambda b, j: (b * nt + j, 0)),
                   pl.BlockSpec((tr, Q_RANK), lambda b, j: (b * nt + j, 0)),
                   pl.BlockSpec((tr, KV_RANK), lambda b, j: (b * nt + j, 0))),
        compiler_params=_params(("parallel", "parallel"), blocks),
        name="mla_prep",
    )(projB, cos_t, sin_t, gq, gkv, wuq2, wukv)


def _attn_mask(row, col):
    return (col <= row) & ((col >= FRONT) | (row < FRONT))


def _attn_fwd(q_att, k_att, v_att, projA, B, Lp):
    T = B * Lp
    NQ = Lp // QB
    HW = 2 * LANES
    scale = 1.0 / math.sqrt(MLA_QK)

    def body(q_ref, k_ref, v_ref, mz_ref, o_ref, yb_ref, lsec_ref, m_ref, l_ref, acc_ref):
        qi = pl.program_id(1)
        m_ref[...] = jnp.full(m_ref.shape, NEG, F32)
        l_ref[...] = jnp.zeros_like(l_ref)
        acc_ref[...] = jnp.zeros_like(acc_ref)
        row = qi * QB + lax.broadcasted_iota(jnp.int32, (QB, QB), 0)
        coli = lax.broadcasted_iota(jnp.int32, (QB, QB), 1)

        def step(kj, carry):
            off = pl.multiple_of(kj * QB, QB)
            ok = _attn_mask(row, kj * QB + coli)
            for h in range(MLA_H):
                q = q_ref[:, h * HW:(h + 1) * HW]
                kb = k_ref[pl.ds(off, QB), h * HW:(h + 1) * HW]
                vb = v_ref[pl.ds(off, QB), h * MLA_DV:(h + 1) * MLA_DV]
                s = jnp.where(ok, _nt(q, kb) * scale, NEG)
                m_old = m_ref[h]
                m_new = jnp.maximum(m_old, jnp.max(s, axis=-1, keepdims=True))
                alpha = jnp.exp(m_old - m_new)
                p = jnp.exp(s - jnp.tile(m_new, (1, QB // LANES)))
                m_ref[h] = m_new
                l_ref[h] = alpha * l_ref[h] + jnp.sum(p, axis=-1, keepdims=True)
                acc_ref[h] = alpha * acc_ref[h] + _nn(p.astype(BF16), vb)
            return carry

        lax.fori_loop(0, qi + 1, step, 0)
        for h in range(MLA_H):
            hs = slice(h * MLA_DV, (h + 1) * MLA_DV)
            l = l_ref[h]
            o = acc_ref[h] / l
            o_ref[:, hs] = o
            z = mz_ref[:, hs].astype(F32)
            yb_ref[:, hs] = (o * (z * _sigmoid(z))).astype(BF16)
            lse = m_ref[h] + jnp.log(l)
            lsec_ref[0, h, pl.ds(qi, 1), :] = jnp.transpose(lse)[0:1, :]

    blocks = (_nbytes((QB, 2048), BF16) + _nbytes((Lp, 2048), BF16) + _nbytes((Lp, 1024), BF16)
              + 2 * _nbytes((QB, 1024), F32) + _nbytes((QB, 1024), BF16) + _nbytes((MLA_H, QB, LANES), F32)
              + _nbytes((MLA_H, NQ, QB), F32))
    return pl.pallas_call(
        body,
        out_shape=(jax.ShapeDtypeStruct((T, MLA_H * MLA_DV), F32), jax.ShapeDtypeStruct((T, MLA_H * MLA_DV), BF16),
                   jax.ShapeDtypeStruct((B, MLA_H, NQ, QB), F32)),
        grid=(B, NQ),
        in_specs=[
            pl.BlockSpec((QB, MLA_H * HW), lambda b, i: (b * NQ + i, 0)),
            pl.BlockSpec((Lp, MLA_H * HW), lambda b, i: (b, 0)),
            pl.BlockSpec((Lp, MLA_H * MLA_DV), lambda b, i: (b, 0)),
            pl.BlockSpec((QB, 1024), lambda b, i: (b * NQ + i, 2)),
        ],
        out_specs=(pl.BlockSpec((QB, 1024), lambda b, i: (b * NQ + i, 0)),
                   pl.BlockSpec((QB, 1024), lambda b, i: (b * NQ + i, 0)),
                   pl.BlockSpec((1, MLA_H, NQ, QB), lambda b, i: (b, 0, 0, 0))),
        scratch_shapes=[pltpu.VMEM((MLA_H, QB, LANES), F32), pltpu.VMEM((MLA_H, QB, LANES), F32),
                        pltpu.VMEM((MLA_H, QB, MLA_DV), F32)],
        compiler_params=_params(("parallel", "arbitrary"), blocks, 3 * _nbytes((MLA_H, QB, LANES), F32)),
        name="attn_fwd",
    )(q_att, k_att, v_att, projA)


def _out_proj_loss(x, meta, projA, ya, yb, w_out, gf, tgt, B, Lp):
    T = B * Lp
    NQ = Lp // QB

    def body(x_ref, meta_ref, gg_ref, gm_ref, ya_ref, yb_ref, w_ref, gf_ref, t_ref,
             dh_ref, dhb_ref, mg_ref, loss_ref, dgf_ref):
        b = pl.program_id(0)
        j = pl.program_id(1)

        @pl.when((b == 0) & (j == 0))
        def _():
            loss_ref[...] = jnp.zeros_like(loss_ref)
            dgf_ref[...] = jnp.zeros_like(dgf_ref)

        f32 = lambda ref: ref[...].astype(F32)
        merged = (_sigmoid(f32(gg_ref)) * f32(ya_ref) + _sigmoid(f32(gm_ref)) * f32(yb_ref)).astype(BF16)
        mg_ref[...] = merged
        h1 = _h_tile(j, x_ref, meta_ref) + _nn(merged, w_ref[...])
        r = lax.rsqrt(jnp.mean(h1 * h1, axis=-1, keepdims=True) + EPS)
        hn = h1 * r
        gfv = gf_ref[...]
        diff = jnp.where(j > 0, hn * gfv - t_ref[0], 0.0)
        loss_ref[...] += (0.5 / D) * jnp.sum(jnp.sum(diff * diff, axis=-1, keepdims=True), axis=0, keepdims=True)
        dout = diff * (1.0 / D)
        dgf_ref[...] += jnp.sum(dout * hn, axis=0, keepdims=True)
        dhn = dout * gfv
        dh = r * (dhn - hn * jnp.mean(dhn * hn, axis=-1, keepdims=True))
        dh_ref[...] = dh
        dhb_ref[...] = dh.astype(BF16)

    rows = lambda c: pl.BlockSpec((QB, D), lambda b, j: (b * NQ + j, c))
    const = lambda s: pl.BlockSpec(s, lambda b, j: (0, 0))
    return pl.pallas_call(
        body,
        out_shape=(jax.ShapeDtypeStruct((T, D), F32), jax.ShapeDtypeStruct((T, D), BF16),
                   jax.ShapeDtypeStruct((T, D), BF16), jax.ShapeDtypeStruct((1, 1), F32),
                   jax.ShapeDtypeStruct((1, D), F32)),
        grid=(B, NQ),
        in_specs=[_x_spec(), const((N_META, D)), rows(3), rows(4), rows(0), rows(0), const((D, D)),
                  const((1, D)), _x_spec()],
        out_specs=(rows(0), rows(0), rows(0), const((1, 1)), const((1, D))),
        compiler_params=_params(("arbitrary", "arbitrary"), 10 * _nbytes((QB, D), F32)),
        name="out_proj_loss",
    )(x, meta, projA, projA, ya, yb, w_out, gf, tgt)


def _merge_bwd(dh1_b, w_out, projA, ya, yb, tr):
    T = dh1_b.shape[0]

    def body(dh_ref, w_ref, gg_ref, gm_ref, ya_ref, yb_ref, dya_ref, dyb_ref, da_ref):
        d = _nt(dh_ref[...], w_ref[...])
        sg = _sigmoid(gg_ref[...].astype(F32))
        sm = _sigmoid(gm_ref[...].astype(F32))
        dya_ref[...] = (d * sg).astype(BF16)
        dyb_ref[...] = (d * sm).astype(BF16)
        da_ref[:, 0:D] = (d * ya_ref[...].astype(F32) * (sg * (1.0 - sg))).astype(BF16)
        da_ref[:, D:2 * D] = (d * yb_ref[...].astype(F32) * (sm * (1.0 - sm))).astype(BF16)

    spec = lambda c: pl.BlockSpec((tr, D), lambda i: (i, c))
    return pl.pallas_call(
        body,
        out_shape=(jax.ShapeDtypeStruct((T, D), BF16), jax.ShapeDtypeStruct((T, D), BF16),
                   jax.ShapeDtypeStruct((T, 2 * D), BF16)),
        grid=(T // tr,),
        in_specs=[spec(0), pl.BlockSpec((D, D), lambda i: (0, 0)), spec(3), spec(4), spec(0), spec(0)],
        out_specs=(spec(0), spec(0), pl.BlockSpec((tr, 2 * D), lambda i: (i, 0))),
        compiler_params=_params(("parallel",), 8 * _nbytes((tr, D), F32)),
        name="merge_bwd",
    )(dh1_b, w_out, projA, projA, ya, yb)


def _gla_out_bwd(dya, gla_proj, oa, projA, gn4, tr):
    T = dya.shape[0]
    nsteps = T // tr

    def body(dya_ref, w_ref, oa_ref, z_ref, gn_ref, do_ref, dz_ref, dgn_ref, acc_ref):
        i = pl.program_id(0)

        @pl.when(i == 0)
        def _():
            acc_ref[...] = jnp.zeros_like(acc_ref)

        dy_all = _nt(dya_ref[...], w_ref[...])
        for h in range(GLA_H):
            vs = slice(h * GLA_DV, (h + 1) * GLA_DV)
            dy = dy_all[:, vs]
            o = oa_ref[:, vs]
            z = z_ref[:, vs].astype(F32)
            gn = gn_ref[:, vs]
            s = _sigmoid(z)
            ra = lax.rsqrt(jnp.mean(o * o, axis=-1, keepdims=True) + EPS)
            on = o * ra
            don = dy * (z * s)
            t = don * gn
            do_ref[:, vs] = (ra * (t - on * jnp.mean(t * on, axis=-1, keepdims=True))).astype(BF16)
            dz_ref[:, vs] = (dy * (on * gn) * (s * (1.0 + z * (1.0 - s)))).astype(BF16)
            acc_ref[:, vs] += jnp.sum(don * on, axis=0, keepdims=True)

        @pl.when(i == nsteps - 1)
        def _():
            a = acc_ref[...]
            dgn_ref[...] = a[:, 0:256] + a[:, 256:512] + a[:, 512:768] + a[:, 768:1024]

    spec = lambda c: pl.BlockSpec((tr, D), lambda i: (i, c))
    return pl.pallas_call(
        body,
        out_shape=(jax.ShapeDtypeStruct((T, D), BF16), jax.ShapeDtypeStruct((T, D), BF16),
                   jax.ShapeDtypeStruct((1, GLA_DV), F32)),
        grid=(nsteps,),
        in_specs=[spec(0), pl.BlockSpec((D, D), lambda i: (0, 0)), spec(0), spec(1),
                  pl.BlockSpec((1, D), lambda i: (0, 0))],
        out_specs=(spec(0), spec(0), pl.BlockSpec((1, GLA_DV), lambda i: (0, 0))),
        scratch_shapes=[pltpu.VMEM((1, D), F32)],
        compiler_params=_params(("arbitrary",), 6 * _nbytes((tr, D), F32)),
        name="gla_out_bwd",
    )(dya, gla_proj, oa, projA, gn4)


def _gla_bwd(projA, projB, ssave, doa, wg, bg, B, Lp):
    T = B * Lp
    NC = Lp // GLA_C
    C = GLA_C
    scale = GLA_DK ** -0.5
    WC = 2304

    def body(q_ref, k_ref, v_ref, lr_ref, ss_ref, do_ref, wg_ref, bg_ref, dc_ref, dwg_ref, dbg_ref, dst_ref):
        i = pl.program_id(0)
        n = NC - 1 - i

        @pl.when(i == 0)
        def _():
            dst_ref[...] = jnp.zeros_like(dst_ref)
            dwg_ref[...] = jnp.zeros_like(dwg_ref)
            dbg_ref[...] = jnp.zeros_like(dbg_ref)

        pos = n * C + lax.broadcasted_iota(jnp.int32, (C, 1), 0)
        valid = pos >= FRONT
        lower, upper = _tri_masks()
        is_last = lax.broadcasted_iota(jnp.int32, (C, 1), 0) == C - 1
        for b in range(B):
            lr = lr_ref[b]
            pre, glog = _gla_gate(lr, wg_ref[...], bg_ref[...], valid)
            bcum = _cumsum_rows(glog, lower)
            db_parts = []
            for h in range(GLA_H):
                ks = slice(h * GLA_DK, (h + 1) * GLA_DK)
                vs = slice(h * GLA_DV, (h + 1) * GLA_DV)
                bh = bcum[:, ks]
                blast = jnp.sum(jnp.where(is_last, bh, 0.0), axis=0, keepdims=True)
                eb, enb, ekl, ebl = jnp.exp(bh), jnp.exp(-bh), jnp.exp(blast - bh), jnp.exp(blast)
                qh = q_ref[b, :, ks].astype(F32) * scale
                kh = k_ref[b, :, ks].astype(F32)
                qe_f, ke_f, kl_f = qh * eb, kh * enb, kh * ekl
                qe, ke, kl = qe_f.astype(BF16), ke_f.astype(BF16), kl_f.astype(BF16)
                vh = v_ref[b, :, vs].astype(BF16)
                doh = do_ref[b, :, vs]
                st = ss_ref[b, 0, h]
                dst = dst_ref[b, h]
                st_b, dst_b = st.astype(BF16), dst.astype(BF16)
                da = jnp.where(lower, _nt(doh, vh), 0.0).astype(BF16)
                da_t = jnp.where(upper, _nt(vh, doh), 0.0).astype(BF16)
                a_t = jnp.where(upper, _nt(ke, qe), 0.0).astype(BF16)
                dqe = _nn(da, ke) + _nn(doh, st_b)
                dke = _nn(da_t, qe)
                dvh = _nn(a_t, doh) + _nt(kl, dst_b)
                dkl = _nn(vh, dst_b)
                dst_ref[b, h] = dst * ebl + _tn(doh, qe)
                deb = jnp.sum(st * dst, axis=0, keepdims=True)
                db = dqe * qe_f - dke * ke_f - dkl * kl_f
                db_last = jnp.sum(dkl * kl_f, axis=0, keepdims=True) + deb * ebl
                db_parts.append(db + jnp.where(is_last, db_last, 0.0))
                dc_ref[b, :, vs] = dvh.astype(BF16)
                dc_ref[b, :, 1024 + h * GLA_DK:1024 + (h + 1) * GLA_DK] = (dqe * eb * scale).astype(BF16)
                dc_ref[b, :, 1536 + h * GLA_DK:1536 + (h + 1) * GLA_DK] = (dke * enb + dkl * ekl).astype(BF16)
            dglog = _cumsum_rows(jnp.concatenate(db_parts, axis=1), upper)
            dpre = jnp.where(valid, dglog * (1.0 / GLA_NORMALIZER) / (1.0 + jnp.exp(pre)), 0.0)
            dpre_b = dpre.astype(BF16)
            dc_ref[b, :, 2048:2176] = _nt(dpre_b, wg_ref[...]).astype(BF16)
            dc_ref[b, :, 2176:2304] = jnp.zeros((C, 128), BF16)
            dwg_ref[...] += _tn(lr.astype(BF16), dpre_b)
            dbg_ref[...] += jnp.sum(dpre, axis=0, keepdims=True)

    blocks = B * (_nbytes((C, 512), F32) * 2 + _nbytes((C, 1024), F32) + _nbytes((C, 1024), BF16)
                  + _nbytes((GLA_H, GLA_DV, GLA_DK), F32) + _nbytes((C, WC), BF16)) + 3 * _nbytes((128, 512), F32)
    state = _nbytes((B, GLA_H, GLA_DV, GLA_DK), F32)
    pa = projA.reshape(B, Lp, projA.shape[1])
    rev = lambda i: NC - 1 - i
    dc, dwg, dbg = pl.pallas_call(
        body,
        out_shape=(jax.ShapeDtypeStruct((B, Lp, WC), BF16), jax.ShapeDtypeStruct((128, GLA_KW), F32),
                   jax.ShapeDtypeStruct((1, GLA_KW), F32)),
        grid=(NC,),
        in_specs=[
            pl.BlockSpec((B, C, 512), lambda i: (0, rev(i), 10)),
            pl.BlockSpec((B, C, 512), lambda i: (0, rev(i), 11)),
            pl.BlockSpec((B, C, 1024), lambda i: (0, rev(i), 0)),
            pl.BlockSpec((B, C, 128), lambda i: (0, rev(i), 3)),
            pl.BlockSpec((B, 1, GLA_H, GLA_DV, GLA_DK), lambda i: (0, rev(i), 0, 0, 0)),
            pl.BlockSpec((B, C, 1024), lambda i: (0, rev(i), 0)),
            pl.BlockSpec((128, 512), lambda i: (0, 0)),
            pl.BlockSpec((1, 512), lambda i: (0, 0)),
        ],
        out_specs=(pl.BlockSpec((B, C, WC), lambda i: (0, rev(i), 0)),
                   pl.BlockSpec((128, GLA_KW), lambda i: (0, 0)),
                   pl.BlockSpec((1, GLA_KW), lambda i: (0, 0))),
        scratch_shapes=[pltpu.VMEM((B, GLA_H, GLA_DV, GLA_DK), F32)],
        compiler_params=_params(("arbitrary",), blocks, state),
        name="gla_bwd",
    )(pa, pa, pa, projB.reshape(B, Lp, projB.shape[1]), ssave, doa.reshape(B, Lp, GLA_VW), wg, bg)
    return dc.reshape(T, WC), dwg, dbg


def _attn_bwd_pre(dyb, mla_proj, projA, ob, B, Lp):
    T = B * Lp
    NQ = Lp // QB

    def body(dyb_ref, w_ref, z_ref, o_ref, do_ref, dz_ref, dcol_ref):
        j = pl.program_id(1)
        dy_all = _nt(dyb_ref[...], w_ref[...])
        for h in range(MLA_H):
            hs = slice(h * MLA_DV, (h + 1) * MLA_DV)
            dy = dy_all[:, hs]
            z = z_ref[:, hs].astype(F32)
            o = o_ref[:, hs]
            s = _sigmoid(z)
            do = dy * (z * s)
            do_ref[:, hs] = do.astype(BF16)
            dz_ref[:, hs] = (dy * o * (s * (1.0 + z * (1.0 - s)))).astype(BF16)
            dl = jnp.broadcast_to(jnp.sum(do * o, axis=-1, keepdims=True), (QB, LANES))
            dcol_ref[0, h, pl.ds(j, 1), :] = jnp.transpose(dl)[0:1, :]

    rows = lambda c: pl.BlockSpec((QB, D), lambda b, j: (b * NQ + j, c))
    return pl.pallas_call(
        body,
        out_shape=(jax.ShapeDtypeStruct((T, D), BF16), jax.ShapeDtypeStruct((T, D), BF16),
                   jax.ShapeDtypeStruct((B, MLA_H, NQ, QB), F32)),
        grid=(B, NQ),
        in_specs=[rows(0), pl.BlockSpec((D, D), lambda b, j: (0, 0)), rows(2), rows(0)],
        out_specs=(rows(0), rows(0), pl.BlockSpec((1, MLA_H, NQ, QB), lambda b, j: (b, 0, 0, 0))),
        compiler_params=_params(("parallel", "arbitrary"), 6 * _nbytes((QB, D), F32)),
        name="attn_bwd_pre",
    )(dyb, mla_proj, projA, ob)


ATTN_BWD_HEADS = 8


def _attn_bwd(q_att, k_att, v_att, do, lse_c, delta_c, B, Lp):
    T = B * Lp
    NQ = Lp // QB
    G = ATTN_BWD_HEADS
    NG = MLA_H // G
    HW = 2 * LANES
    scale = 1.0 / math.sqrt(MLA_QK)

    def body(q_ref, k_ref, v_ref, do_ref, lse_ref, dl_ref, dq_out, dk_out, dv_out, dq_ref, dk_ref, dv_ref):
        kj = pl.program_id(2)

        @pl.when(kj == 0)
        def _():
            dq_ref[...] = jnp.zeros_like(dq_ref)

        dk_ref[...] = jnp.zeros_like(dk_ref)
        dv_ref[...] = jnp.zeros_like(dv_ref)
        col = kj * QB + lax.broadcasted_iota(jnp.int32, (QB, QB), 0)
        rowi = lax.broadcasted_iota(jnp.int32, (QB, QB), 1)

        def step(qi, carry):
            off = pl.multiple_of(qi * QB, QB)
            ok = _attn_mask(qi * QB + rowi, col)
            for h in range(G):
                ws = slice(h * HW, (h + 1) * HW)
                hs = slice(h * MLA_DV, (h + 1) * MLA_DV)
                qb = q_ref[pl.ds(off, QB), ws]
                dob = do_ref[pl.ds(off, QB), hs]
                kb = k_ref[:, ws]
                lse = lse_ref[0, h, pl.ds(qi, 1), :]
                delta = dl_ref[0, h, pl.ds(qi, 1), :]
                s_t = _nt(kb, qb) * scale
                p_t = jnp.where(ok, jnp.exp(s_t - lse), 0.0)
                dv_ref[:, hs] += _nn(p_t.astype(BF16), dob)
                ds_t = (p_t * (_nt(v_ref[:, hs], dob) - delta) * scale).astype(BF16)
                dk_ref[:, ws] += _nn(ds_t, qb)
                dq_ref[pl.ds(off, QB), ws] += _tn(ds_t, kb)
            return carry

        lax.fori_loop(kj, NQ, step, 0)
        dk_out[...] = dk_ref[...].astype(BF16)
        dv_out[...] = dv_ref[...].astype(BF16)

        @pl.when(kj == NQ - 1)
        def _():
            dq_out[...] = dq_ref[...].astype(BF16)

    blocks = (2 * _nbytes((Lp, G * HW), BF16) + _nbytes((Lp, G * MLA_DV), BF16) + 2 * _nbytes((QB, G * 384), BF16)
              + 2 * _nbytes((G, NQ, QB), F32))
    scratch = [pltpu.VMEM((Lp, G * HW), F32), pltpu.VMEM((QB, G * HW), F32), pltpu.VMEM((QB, G * MLA_DV), F32)]
    return pl.pallas_call(
        body,
        out_shape=(jax.ShapeDtypeStruct((T, MLA_H * HW), BF16), jax.ShapeDtypeStruct((T, MLA_H * HW), BF16),
                   jax.ShapeDtypeStruct((T, MLA_H * MLA_DV), BF16)),
        scratch_shapes=scratch,
        grid=(B, NG, NQ),
        in_specs=[
            pl.BlockSpec((Lp, G * HW), lambda b, g, j: (b, g), pipeline_mode=pl.Buffered(1)),
            pl.BlockSpec((QB, G * HW), lambda b, g, j: (b * NQ + j, g)),
            pl.BlockSpec((QB, G * MLA_DV), lambda b, g, j: (b * NQ + j, g)),
            pl.BlockSpec((Lp, G * MLA_DV), lambda b, g, j: (b, g), pipeline_mode=pl.Buffered(1)),
            pl.BlockSpec((1, G, NQ, QB), lambda b, g, j: (b, g, 0, 0)),
            pl.BlockSpec((1, G, NQ, QB), lambda b, g, j: (b, g, 0, 0)),
        ],
        out_specs=(pl.BlockSpec((Lp, G * HW), lambda b, g, j: (b, g), pipeline_mode=pl.Buffered(1)),
                   pl.BlockSpec((QB, G * HW), lambda b, g, j: (b * NQ + j, g)),
                   pl.BlockSpec((QB, G * MLA_DV), lambda b, g, j: (b * NQ + j, g))),
        compiler_params=_params(("parallel", "parallel", "arbitrary"), blocks,
                                _nbytes((Lp, G * HW), F32) + _nbytes((QB, G * 384), F32)),
        name="attn_bwd",
    )(q_att, k_att, v_att, do, lse_c, delta_c)


def _mla_bwd_post(dq, dk, dv, projB, cos_t, sin_t, gq, gkv, wuq2, wukv, B, Lp, tr):
    T = B * Lp
    nt = Lp // tr
    HW = 2 * LANES

    def body(dq_ref, dk_ref, dv_ref, pb_ref, cos_ref, sin_ref, gq_ref, gkv_ref, wuq_ref, wukv_ref,
             dqf_ref, dkvf_ref, de_ref, dgq_ref, dgkv_ref):
        first = (pl.program_id(0) == 0) & (pl.program_id(1) == 0)

        @pl.when(first)
        def _():
            dgq_ref[...] = jnp.zeros_like(dgq_ref)
            dgkv_ref[...] = jnp.zeros_like(dgkv_ref)

        cs = cos_ref[...]
        sn = sin_ref[...]
        rope_t = lambda t: t * cs + _swap_halves(t * sn)
        dkr = jnp.zeros((tr, LANES), F32)
        for h in range(MLA_H):
            dqf_ref[:, h * HW:h * HW + LANES] = dq_ref[:, h * HW:h * HW + LANES]
            dq_rope = dq_ref[:, h * HW + LANES:(h + 1) * HW].astype(F32)
            dqf_ref[:, h * HW + LANES:(h + 1) * HW] = rope_t(dq_rope).astype(BF16)
            dkvf_ref[:, h * HW:h * HW + LANES] = dk_ref[:, h * HW:h * HW + LANES]
            dkvf_ref[:, h * HW + LANES:(h + 1) * HW] = dv_ref[:, h * MLA_DV:(h + 1) * MLA_DV]
            dkr = dkr + dk_ref[:, h * HW + LANES:(h + 1) * HW].astype(F32)

        def norm_bwd(x, dn, g):
            r = lax.rsqrt(jnp.mean(x * x, axis=-1, keepdims=True) + EPS)
            xn = x * r
            t = dn * g
            return r * (t - xn * jnp.mean(t * xn, axis=-1, keepdims=True)), jnp.sum(dn * xn, axis=0, keepdims=True)

        cq = pb_ref[:, 0:Q_RANK].astype(F32)
        ckv = pb_ref[:, Q_RANK:Q_RANK + KV_RANK].astype(F32)
        dcq, dgq = norm_bwd(cq, _nt(dqf_ref[...], wuq_ref[...]), gq_ref[...])
        dckv, dgkv = norm_bwd(ckv, _nt(dkvf_ref[...], wukv_ref[...]), gkv_ref[...])
        dgq_ref[...] += dgq
        dgkv_ref[...] += dgkv
        de_ref[:, 0:Q_RANK] = dcq.astype(BF16)
        de_ref[:, Q_RANK:Q_RANK + KV_RANK] = dckv.astype(BF16)
        de_ref[:, 384:512] = rope_t(dkr).astype(BF16)

    rows = lambda w: pl.BlockSpec((tr, w), lambda b, j: (b * nt + j, 0))
    const = lambda s: pl.BlockSpec(s, lambda b, j: (0, 0))
    blocks = (2 * _nbytes((tr, 2048), F32) + _nbytes((tr, 1024), F32) + _nbytes((tr, 640), F32)
              + 2 * _nbytes((tr, 2048), BF16) + _nbytes((2048, 384), BF16) + 2 * _nbytes((tr, 2048), F32))
    return pl.pallas_call(
        body,
        out_shape=(jax.ShapeDtypeStruct((T, 2048), BF16), jax.ShapeDtypeStruct((T, 2048), BF16),
                   jax.ShapeDtypeStruct((T, 512), BF16), jax.ShapeDtypeStruct((1, Q_RANK), F32),
                   jax.ShapeDtypeStruct((1, KV_RANK), F32)),
        grid=(B, nt),
        in_specs=[rows(2048), rows(2048), rows(1024), rows(640),
                  pl.BlockSpec((tr, 128), lambda b, j: (j, 0)), pl.BlockSpec((tr, 128), lambda b, j: (j, 0)),
                  const((1, Q_RANK)), const((1, KV_RANK)), const((Q_RANK, 2048)), const((KV_RANK, 2048))],
        out_specs=(rows(2048), rows(2048), rows(512), const((1, Q_RANK)), const((1, KV_RANK))),
        compiler_params=_params(("arbitrary", "arbitrary"), blocks),
        name="mla_bwd_post",
    )(dq, dk, dv, projB, cos_t, sin_t, gq, gkv, wuq2, wukv)


def _in_proj_bwd(x, meta, dh1, dA, dBz, dC, dDz, dE, wA, wB, g, B, Lp):
    NQ = Lp // QB
    seq = x.shape[1]

    def body(x_ref, meta_ref, dh_ref, da_ref, db_ref, dc_ref, dd_ref, de_ref, wa_ref, wb_ref, g_ref,
             gx_ref, dmeta_ref, dg_ref):
        b = pl.program_id(0)
        j = pl.program_id(1)

        @pl.when((b == 0) & (j == 0))
        def _():
            dg_ref[...] = jnp.zeros_like(dg_ref)

        du = _nt(da_ref[...], wa_ref[:, 3072:5120])
        du = du + _nt(db_ref[...], wa_ref[:, 1024:2048])
        du = du + _nt(dd_ref[...], wa_ref[:, 2048:3072])
        du = du + _nt(dc_ref[:, 0:1024], wa_ref[:, 0:1024])
        du = du + _nt(dc_ref[:, 1024:2048], wa_ref[:, 5120:6144])
        du = du + _nt(dc_ref[:, 2048:2176], wb_ref[:, 384:512])
        du = du + _nt(de_ref[:, 0:384], wb_ref[:, 0:384])
        du = du + _nt(de_ref[:, 384:512], wb_ref[:, 512:640])

        x = _h_tile(j, x_ref, meta_ref)
        r = lax.rsqrt(jnp.mean(x * x, axis=-1, keepdims=True) + EPS)
        xn = x * r
        t = du * g_ref[...]
        dh0 = dh_ref[...] + r * (t - xn * jnp.mean(t * xn, axis=-1, keepdims=True))
        dg_ref[...] += jnp.sum(du * xn, axis=0, keepdims=True)
        gx_ref[0] = dh0

        @pl.when((j == 0) & (b == 0))
        def _():
            dmeta_ref[...] = dh0[FRONT:HEAD_ROWS, :]

        @pl.when((j == 0) & (b > 0))
        def _():
            dmeta_ref[...] += dh0[FRONT:HEAD_ROWS, :]

    rows = lambda w: pl.BlockSpec((QB, w), lambda b, j: (b * NQ + j, 0))
    const = lambda s: pl.BlockSpec(s, lambda b, j: (0, 0))
    widths = [a.shape[1] for a in (dA, dBz, dC, dDz, dE)]
    blocks = (sum(_nbytes((QB, w), BF16) for w in widths) + _nbytes(wA.shape, BF16) + _nbytes(wB.shape, BF16)
              + 4 * _nbytes((QB, D), F32))
    return pl.pallas_call(
        body,
        out_shape=(jax.ShapeDtypeStruct((B, seq, D), F32), jax.ShapeDtypeStruct((N_META, D), F32),
                   jax.ShapeDtypeStruct((1, D), F32)),
        grid=(B, NQ),
        in_specs=[_x_spec(), const((N_META, D)), rows(D)] + [rows(w) for w in widths]
        + [const(wA.shape), const(wB.shape), const((1, D))],
        out_specs=(_x_spec(), const((N_META, D)), const((1, D))),
        compiler_params=_params(("arbitrary", "arbitrary"), blocks),
        name="in_proj_bwd",
    )(x, meta, dh1, dA, dBz, dC, dDz, dE, wA, wB, g)


_VMEM_WHOLE = pl.BlockSpec(memory_space=pltpu.VMEM)


def _params_whole(arrays):
    total = sum(_nbytes(a.shape, a.dtype) for a in arrays)
    return pltpu.CompilerParams(vmem_limit_bytes=int(min(total + 12 * 1024 * 1024, VMEM_CAP_V7X)))


def _wire_dtype(shape):
    return BF16 if shape[-2] * shape[-1] >= WIRE_BF16_MIN_ELEMS else F32


def _pair_add_big(gp, recv, c):
    _, half, cols = recv.shape
    th = _div_tile(half, 64, 16)
    out_dtype = _wire_dtype(recv.shape)

    steps = half // th

    def body(c_ref, a_ref, b_ref, o_ref):
        o_ref[...] = (a_ref[...] + b_ref[...]).astype(out_dtype)

    return pl.pallas_call(
        body,
        out_shape=jax.ShapeDtypeStruct(recv.shape, out_dtype),
        grid_spec=pltpu.PrefetchScalarGridSpec(
            num_scalar_prefetch=1,
            grid=(steps,),
            in_specs=[pl.BlockSpec((4, th, cols), lambda i, c_ref: (0, c_ref[0] * steps + i, 0)),
                      pl.BlockSpec((4, th, cols), lambda i, c_ref: (0, i, 0))],
            out_specs=pl.BlockSpec((4, th, cols), lambda i, c_ref: (0, i, 0)),
        ),
        compiler_params=_params(("parallel",), 3 * _nbytes((4, th, cols), F32)),
        name="grad_pair_add_big",
    )(c, gp, recv)


def _pair_add_small(gps, recvs):
    n = len(gps)

    def body(*refs):
        c = lax.axis_index("c")
        for t in range(n):
            g_ref, r_ref, o_ref = refs[t], refs[n + t], refs[2 * n + t]
            half = r_ref.shape[1]
            s = g_ref[:, pl.ds(pl.multiple_of(c * half, 8), half), :] + r_ref[...]
            o_ref[...] = s.astype(o_ref.dtype)

    return pl.pallas_call(
        body,
        out_shape=[jax.ShapeDtypeStruct(r.shape, _wire_dtype(r.shape)) for r in recvs],
        in_specs=[_VMEM_WHOLE] * (2 * n),
        out_specs=[_VMEM_WHOLE] * n,
        compiler_params=_params_whole(list(gps) + 2 * list(recvs)),
        name="grad_pair_add_small",
    )(*gps, *recvs)


def _chip_order_sum(landed_ref, own_ref, me):
    p = [jnp.where(me == k, own_ref[k], landed_ref[k]).astype(F32) for k in range(4)]
    return ((p[0] + p[1]) + p[2]) + p[3]


def _sum_chips_big(landed, own, pos):
    _, half, cols = landed.shape
    th = _div_tile(half, 64, 16)

    def body(pos_ref, l_ref, s_ref, o_ref):
        o_ref[0] = _chip_order_sum(l_ref, s_ref, pos_ref[1])

    spec = pl.BlockSpec((4, th, cols), lambda i, pos_ref: (0, i, 0))
    return pl.pallas_call(
        body,
        out_shape=jax.ShapeDtypeStruct((2, half, cols), F32),
        grid_spec=pltpu.PrefetchScalarGridSpec(
            num_scalar_prefetch=1,
            grid=(half // th,),
            in_specs=[spec, spec],
            out_specs=pl.BlockSpec((1, th, cols), lambda i, pos_ref: (pos_ref[0], i, 0)),
        ),
        compiler_params=_params(("parallel",), 3 * _nbytes((4, th, cols), F32)),
        name="grad_sum_chips_big",
    )(pos, landed, own)


def _sum_chips_small(landed, own):
    n = len(landed)

    def body(*refs):
        x, y, c = _mesh_pos()
        for t in range(n):
            refs[2 * n + t][c] = _chip_order_sum(refs[t], refs[n + t], 2 * x + y)

    return pl.pallas_call(
        body,
        out_shape=[jax.ShapeDtypeStruct((2,) + p.shape[1:], F32) for p in landed],
        in_specs=[_VMEM_WHOLE] * (2 * n),
        out_specs=[_VMEM_WHOLE] * n,
        compiler_params=_params_whole(list(landed) * 3),
        name="grad_sum_chips_small",
    )(*landed, *own)


def _adamw_update(w_ref, g_ref, m_ref, v_ref, d_ref, mo_ref, vo_ref):
    c1 = 1.0 - ADAM_B1 ** ADAM_STEP
    c2 = 1.0 - ADAM_B2 ** ADAM_STEP
    gv = g_ref[...]
    mn = ADAM_B1 * m_ref[...] + (1.0 - ADAM_B1) * gv
    vn = ADAM_B2 * v_ref[...] + (1.0 - ADAM_B2) * (gv * gv)
    mo_ref[...] = mn
    vo_ref[...] = vn
    d_ref[...] = -ADAM_LR * ((mn / c1) / (jnp.sqrt(vn / c2) + ADAM_EPS) + ADAM_WD * w_ref[...])


def _adamw_big(w, g, m, v):
    lead, (rows, cols) = w.shape[:-2], w.shape[-2:]
    assert all(n == 1 for n in lead)
    tr = _div_tile(rows, (1 << 19) // cols, 8)
    spec = pl.BlockSpec((1,) * len(lead) + (tr, cols), lambda i: (0,) * len(lead) + (i, 0))
    shp = jax.ShapeDtypeStruct(w.shape, F32)
    return pl.pallas_call(
        functools.partial(_adamw_update),
        out_shape=(shp, shp, shp),
        grid=(rows // tr,),
        in_specs=[spec] * 4,
        out_specs=(spec, spec, spec),
        compiler_params=_params(("parallel",), 7 * _nbytes((tr, cols), F32)),
        name="adamw_big",
    )(w, g, m, v)


def _adamw_small(ws, gs, ms, vs):
    n = len(ws)

    def body(*refs):
        for t in range(n):
            _adamw_update(refs[t], refs[n + t], refs[2 * n + t], refs[3 * n + t],
                          refs[4 * n + t], refs[5 * n + t], refs[6 * n + t])

    shapes = [jax.ShapeDtypeStruct(w.shape, F32) for w in ws]
    return pl.pallas_call(
        body,
        out_shape=shapes * 3,
        in_specs=[_VMEM_WHOLE] * (4 * n),
        out_specs=[_VMEM_WHOLE] * (3 * n),
        compiler_params=_params_whole(list(ws) * 7),
        name="adamw_small",
    )(*ws, *gs, *ms, *vs)


def _mesh_pos():
    return lax.axis_index("x"), lax.axis_index("y"), lax.axis_index("c")


def _other_chips(x, y):
    return [(1 - x, y), (x, 1 - y), (1 - x, 1 - y)]


_ANY = pl.BlockSpec(memory_space=pl.ANY)


PAIR_SPLIT_MIN_ROWS = 64


def _weight_gather(shards):
    n = len(shards)
    split = [s.shape[0] >= PAIR_SPLIT_MIN_ROWS for s in shards]

    def body(*refs):
        w_refs, o_refs = refs[:n], refs[n:2 * n]
        send_sems, recv_sems = refs[2 * n:]
        x, y, c = _mesh_pos()
        me = 2 * x + y
        chips = _other_chips(x, y)

        def rows_of(t, core):
            rows = shards[t].shape[0]
            if not split[t]:
                return pl.ds(0, rows)
            return pl.ds(pl.multiple_of(core * (rows // 2), 16), rows // 2)

        def landed(t, k, slot, rows, to):
            ref = o_refs[t].at[slot, rows]
            return pltpu.make_async_remote_copy(src_ref=ref, dst_ref=ref, send_sem=send_sems.at[6 * t + k],
                                                recv_sem=recv_sems.at[6 * t + k], device_id=to, device_id_type=MESH)

        sends = []
        for t in range(n):
            mine = rows_of(t, c)
            for k, (px, py) in enumerate(chips):
                cp = pltpu.make_async_remote_copy(src_ref=w_refs[t].at[mine], dst_ref=o_refs[t].at[me, mine],
                                                  send_sem=send_sems.at[6 * t + k], recv_sem=recv_sems.at[6 * t + k],
                                                  device_id=(px, py, c), device_id_type=MESH)
                cp.start()
                sends.append(cp)
        for t in range(n):
            mine = rows_of(t, c)
            for k, (px, py) in enumerate(chips):
                landed(t, k, 2 * px + py, mine, (x, y, c)).wait_recv()
                if split[t]:
                    cp = landed(t, 3 + k, 2 * px + py, mine, (x, y, 1 - c))
                    cp.start()
                    sends.append(cp)
        for t in range(n):
            if split[t]:
                for k, (px, py) in enumerate(chips):
                    landed(t, 3 + k, 2 * px + py, rows_of(t, 1 - c), (x, y, c)).wait_recv()
        for cp in sends:
            cp.wait_send()

    return pl.pallas_call(
        body,
        out_shape=[jax.ShapeDtypeStruct((4,) + s.shape, s.dtype) for s in shards],
        in_specs=[_ANY] * n,
        out_specs=[_ANY] * n,
        scratch_shapes=[pltpu.SemaphoreType.DMA((6 * n,)), pltpu.SemaphoreType.DMA((6 * n,))],
        name="weight_gather",
    )(*shards)


def _pair_swap(gps):
    n = len(gps)

    def body(*refs):
        g_refs, o_refs = refs[:n], refs[n:2 * n]
        send_sems, recv_sems = refs[2 * n:]
        x, y, c = _mesh_pos()
        copies = []
        for t in range(n):
            half = gps[t].shape[1] // 2
            theirs = pl.ds(pl.multiple_of((1 - c) * half, 8), half)
            cp = pltpu.make_async_remote_copy(src_ref=g_refs[t].at[:, theirs], dst_ref=o_refs[t],
                                              send_sem=send_sems.at[t], recv_sem=recv_sems.at[t],
                                              device_id=(x, y, 1 - c), device_id_type=MESH)
            cp.start()
            copies.append(cp)
        for cp in copies:
            cp.wait_send()
            cp.wait_recv()

    return pl.pallas_call(
        body,
        out_shape=[jax.ShapeDtypeStruct((4, g.shape[1] // 2, g.shape[2]), g.dtype) for g in gps],
        in_specs=[_ANY] * n,
        out_specs=[_ANY] * n,
        scratch_shapes=[pltpu.SemaphoreType.DMA((n,)), pltpu.SemaphoreType.DMA((n,))],
        name="grad_pair_swap",
    )(*gps)


def _chip_scatter(parts):
    n = len(parts)

    def body(*refs):
        s_refs, o_refs = refs[:n], refs[n:2 * n]
        send_sems, recv_sems = refs[2 * n:]
        x, y, c = _mesh_pos()
        me = 2 * x + y
        chips = _other_chips(x, y)
        sends = []
        for t in range(n):
            for k, (px, py) in enumerate(chips):
                cp = pltpu.make_async_remote_copy(src_ref=s_refs[t].at[2 * px + py], dst_ref=o_refs[t].at[me],
                                                  send_sem=send_sems.at[3 * t + k], recv_sem=recv_sems.at[3 * t + k],
                                                  device_id=(px, py, c), device_id_type=MESH)
                cp.start()
                sends.append(cp)
        for t in range(n):
            for k, (px, py) in enumerate(chips):
                pltpu.make_async_remote_copy(src_ref=s_refs[t].at[me], dst_ref=o_refs[t].at[2 * px + py],
                                             send_sem=send_sems.at[3 * t + k], recv_sem=recv_sems.at[3 * t + k],
                                             device_id=(x, y, c), device_id_type=MESH).wait_recv()
        for cp in sends:
            cp.wait_send()

    return pl.pallas_call(
        body,
        out_shape=[jax.ShapeDtypeStruct(p.shape, p.dtype) for p in parts],
        in_specs=[_ANY] * n,
        out_specs=[_ANY] * n,
        scratch_shapes=[pltpu.SemaphoreType.DMA((3 * n,)), pltpu.SemaphoreType.DMA((3 * n,))],
        name="grad_chip_scatter",
    )(*parts)


_HBM = pl.BlockSpec(memory_space=pltpu.HBM)
_SEM = pl.BlockSpec(memory_space=pltpu.SEMAPHORE)


def _in_hbm(a):
    return pltpu.with_memory_space_constraint(a, pltpu.HBM)


def _chip_scatter_start(parts):
    n = len(parts)

    def body(*refs):
        s_refs, l_refs = refs[:n], refs[n:2 * n]
        send_sems, recv_sems = refs[2 * n], refs[2 * n + 1]
        token = refs[-1]
        x, y, c = _mesh_pos()
        me = 2 * x + y
        for t in range(n):
            for k, (px, py) in enumerate(_other_chips(x, y)):
                pltpu.make_async_remote_copy(src_ref=s_refs[t].at[2 * px + py], dst_ref=l_refs[t].at[me],
                                             send_sem=send_sems.at[3 * t + k], recv_sem=recv_sems.at[3 * t + k],
                                             device_id=(px, py, c), device_id_type=MESH).start()
        token[...] = jnp.zeros_like(token)

    hbm = [pltpu.HBM(p.shape, p.dtype) for p in parts]
    outs = pl.pallas_call(
        body,
        name="grad_scatter_start",
        out_shape=(pltpu.SemaphoreType.DMA((3 * n,)), pltpu.SemaphoreType.DMA((3 * n,)), *hbm, *hbm,
                   jax.ShapeDtypeStruct((8, LANES), F32)),
        in_specs=[_HBM] * (2 * n),
        out_specs=(_SEM, _SEM, *([_HBM] * (2 * n)), pl.BlockSpec(memory_space=pltpu.VMEM)),
        input_output_aliases={i: 2 + i for i in range(2 * n)},
        compiler_params=pltpu.CompilerParams(has_side_effects=pltpu.SideEffectType.DATAFLOW_SIDE_EFFECTING),
    )(*[_in_hbm(p) for p in parts], *[_in_hbm(lax.empty(p.shape, p.dtype)) for p in parts])
    return outs[0], outs[1], list(outs[2:2 + n]), list(outs[2 + n:2 + 2 * n]), outs[-1]


def _chip_scatter_wait(send_sems, recv_sems, parts, lands, after):
    n = len(parts)

    def body(*refs):
        s_refs, l_refs = refs[:n], refs[n:2 * n]
        send_sems, recv_sems = refs[2 * n], refs[2 * n + 1]
        x, y, c = _mesh_pos()
        me = 2 * x + y
        for t in range(n):
            for k, (px, py) in enumerate(_other_chips(x, y)):
                cp = pltpu.make_async_remote_copy(src_ref=s_refs[t].at[2 * px + py], dst_ref=l_refs[t].at[2 * px + py],
                                                  send_sem=send_sems.at[3 * t + k], recv_sem=recv_sems.at[3 * t + k],
                                                  device_id=(x, y, c), device_id_type=MESH)
                cp.wait_send()
                cp.wait_recv()

    hbm = [pltpu.HBM(p.shape, p.dtype) for p in parts]
    outs = pl.pallas_call(
        body,
        name="grad_scatter_wait",
        out_shape=(*hbm, *hbm),
        in_specs=[_HBM] * (2 * n) + [_SEM, _SEM, _ANY],
        out_specs=[_HBM] * (2 * n),
        input_output_aliases={i: i for i in range(2 * n)},
        compiler_params=pltpu.CompilerParams(has_side_effects=pltpu.SideEffectType.DATAFLOW_SIDE_EFFECTING),
    )(*parts, *lands, send_sems, recv_sems, after)
    return list(outs[:n]), list(outs[n:])


def _late_gather_start(shards):
    n = len(shards)

    def body(*refs):
        w_refs, l_refs = refs[:n], refs[n:2 * n]
        send_sems, recv_sems = refs[2 * n], refs[2 * n + 1]
        token = refs[-1]
        x, y, c = _mesh_pos()
        me = 2 * x + y
        for t in range(n):
            for k, (px, py) in enumerate(_other_chips(x, y)):
                pltpu.make_async_remote_copy(src_ref=w_refs[t], dst_ref=l_refs[t].at[me],
                                             send_sem=send_sems.at[3 * t + k], recv_sem=recv_sems.at[3 * t + k],
                                             device_id=(px, py, c), device_id_type=MESH).start()
        token[...] = jnp.zeros_like(token)

    src = [pltpu.HBM(s.shape, s.dtype) for s in shards]
    land = [pltpu.HBM((4,) + s.shape, s.dtype) for s in shards]
    outs = pl.pallas_call(
        body,
        name="late_gather_start",
        out_shape=(pltpu.SemaphoreType.DMA((3 * n,)), pltpu.SemaphoreType.DMA((3 * n,)), *src, *land,
                   jax.ShapeDtypeStruct((8, LANES), F32)),
        in_specs=[_HBM] * (2 * n),
        out_specs=(_SEM, _SEM, *([_HBM] * (2 * n)), pl.BlockSpec(memory_space=pltpu.VMEM)),
        input_output_aliases={i: 2 + i for i in range(2 * n)},
        compiler_params=pltpu.CompilerParams(has_side_effects=pltpu.SideEffectType.DATAFLOW_SIDE_EFFECTING),
    )(*[_in_hbm(s) for s in shards], *[_in_hbm(lax.empty((4,) + s.shape, s.dtype)) for s in shards])
    return outs[0], outs[1], list(outs[2:2 + n]), list(outs[2 + n:2 + 2 * n]), outs[-1]


def _late_gather_wait(send_sems, recv_sems, shards, lands, after):
    n = len(shards)

    def body(*refs):
        w_refs, l_refs = refs[:n], refs[n:2 * n]
        send_sems, recv_sems = refs[2 * n], refs[2 * n + 1]
        x, y, c = _mesh_pos()
        for t in range(n):
            for k, (px, py) in enumerate(_other_chips(x, y)):
                cp = pltpu.make_async_remote_copy(src_ref=w_refs[t], dst_ref=l_refs[t].at[2 * px + py],
                                                  send_sem=send_sems.at[3 * t + k], recv_sem=recv_sems.at[3 * t + k],
                                                  device_id=(x, y, c), device_id_type=MESH)
                cp.wait_send()
                cp.wait_recv()

    src = [pltpu.HBM(s.shape, s.dtype) for s in shards]
    land = [pltpu.HBM(l.shape, l.dtype) for l in lands]
    outs = pl.pallas_call(
        body,
        name="late_gather_wait",
        out_shape=(*src, *land),
        in_specs=[_HBM] * (2 * n) + [_SEM, _SEM, _ANY],
        out_specs=[_HBM] * (2 * n),
        input_output_aliases={i: i for i in range(2 * n)},
        compiler_params=pltpu.CompilerParams(has_side_effects=pltpu.SideEffectType.DATAFLOW_SIDE_EFFECTING),
    )(*shards, *lands, send_sems, recv_sems, after)
    return list(outs[n:])


def _all_to_all_small(parts):
    n = len(parts)

    def body(*refs):
        p_refs, o_refs = refs[:n], refs[n:2 * n]
        send_sems, recv_sems = refs[2 * n:]
        x, y, c = _mesh_pos()
        me = 4 * x + 2 * y + c
        sends = []
        for t in range(n):
            for k in range(1, 8):
                px, py, pc = x ^ (k >> 2), y ^ ((k >> 1) & 1), c ^ (k & 1)
                cp = pltpu.make_async_remote_copy(src_ref=p_refs[t], dst_ref=o_refs[t].at[me],
                                                  send_sem=send_sems.at[7 * t + k - 1], recv_sem=recv_sems.at[7 * t + k - 1],
                                                  device_id=(px, py, pc), device_id_type=MESH)
                cp.start()
                sends.append(cp)
        for t in range(n):
            for k in range(1, 8):
                peer = 4 * (x ^ (k >> 2)) + 2 * (y ^ ((k >> 1) & 1)) + (c ^ (k & 1))
                pltpu.make_async_remote_copy(src_ref=p_refs[t], dst_ref=o_refs[t].at[peer],
                                             send_sem=send_sems.at[7 * t + k - 1], recv_sem=recv_sems.at[7 * t + k - 1],
                                             device_id=(x, y, c), device_id_type=MESH).wait_recv()
        for cp in sends:
            cp.wait_send()

    return pl.pallas_call(
        body,
        out_shape=[jax.ShapeDtypeStruct((8,) + p.shape, p.dtype) for p in parts],
        in_specs=[_ANY] * n,
        out_specs=[_ANY] * n,
        scratch_shapes=[pltpu.SemaphoreType.DMA((7 * n,)), pltpu.SemaphoreType.DMA((7 * n,))],
        name="grad_small_all_to_all",
    )(*parts)


def _sum_devices_small(landed, own):
    n = len(landed)

    def body(*refs):
        x, y, c = _mesh_pos()
        me = 4 * x + 2 * y + c
        for t in range(n):
            acc = jnp.where(me == 0, refs[n + t][...], refs[t][0])
            for d in range(1, 8):
                acc = acc + jnp.where(me == d, refs[n + t][...], refs[t][d])
            refs[2 * n + t][...] = acc

    return pl.pallas_call(
        body,
        out_shape=[jax.ShapeDtypeStruct(p.shape, F32) for p in own],
        in_specs=[_VMEM_WHOLE] * (2 * n),
        out_specs=[_VMEM_WHOLE] * n,
        compiler_params=_params_whole(list(landed) + 2 * list(own)),
        name="grad_sum_devices_small",
    )(*landed, *own)


def _pair_join(fs):
    n = len(fs)

    def body(*refs):
        f_refs, o_refs = refs[:n], refs[n:2 * n]
        send_sems, recv_sems = refs[2 * n:]
        x, y, c = _mesh_pos()
        sends = []
        for t in range(n):
            cp = pltpu.make_async_remote_copy(src_ref=f_refs[t].at[c], dst_ref=o_refs[t].at[c], send_sem=send_sems.at[t],
                                              recv_sem=recv_sems.at[t], device_id=(x, y, 1 - c), device_id_type=MESH)
            cp.start()
            sends.append(cp)
        for t in range(n):
            pltpu.make_async_remote_copy(src_ref=f_refs[t].at[c], dst_ref=o_refs[t].at[1 - c], send_sem=send_sems.at[t],
                                         recv_sem=recv_sems.at[t], device_id=(x, y, c), device_id_type=MESH).wait_recv()
        for cp in sends:
            cp.wait_send()

    return pl.pallas_call(
        body,
        out_shape=[jax.ShapeDtypeStruct(f.shape, f.dtype) for f in fs],
        in_specs=[_ANY] * n,
        out_specs=[_ANY] * n,
        input_output_aliases={t: t for t in range(n)},
        scratch_shapes=[pltpu.SemaphoreType.DMA((n,)), pltpu.SemaphoreType.DMA((n,))],
        name="grad_pair_join",
    )(*fs)


def _rope_tables(Lp):
    inv = 1.0 / (ROPE_BASE ** (jnp.arange(0, ROPE, 2, dtype=F32) / ROPE))
    ang = (jnp.arange(Lp, dtype=F32) - FRONT)[:, None] * inv[None, :]
    cs, sn = jnp.cos(ang), jnp.sin(ang)
    return jnp.tile(cs, (1, 4)), jnp.concatenate([-sn, sn, -sn, sn], axis=1)


def _local_step(x, loss_target, meta, norm_g, w_in, gate_w, gate_b, gla_norm_g, gla_proj, q_norm_g, w_uq,
                kv_norm_g, w_ukv, mla_proj, w_out, final_norm_g, early_grads_hook=None, late_weights_hook=None):
    B, seq, _ = x.shape
    Lp = HEAD_ROWS + seq
    T = B * Lp
    tr = _div_tile(Lp, 544, 16)
    tq = _div_tile(T, 1024, QB)
    tkw = _div_tile(T, Lp, QB)

    cuts = np.cumsum((0,) + SPLITS)
    shard_w = IN_WIDTH // 4

    def w_cols(i, width=None):
        parts = []
        for j in range(4):
            a, b = max(cuts[i], j * shard_w), min(cuts[i + 1], (j + 1) * shard_w)
            if a < b:
                parts.append(w_in[j][:, a - j * shard_w:b - j * shard_w])
        if width is not None:
            parts.append(jnp.zeros((D, width - (cuts[i + 1] - cuts[i])), w_in.dtype))
        return parts

    i_q, i_k, i_v, i_lr, i_z, i_cq, i_ckv, i_kr, i_mz, i_gg, i_gm = range(11)
    wA = jnp.concatenate(sum([w_cols(i) for i in (i_v, i_z, i_mz, i_gg, i_gm, i_q, i_k)], []), axis=1)
    wB = jnp.concatenate(w_cols(i_cq) + w_cols(i_ckv) + w_cols(i_lr, 128) + w_cols(i_kr, 128), axis=1)
    gn4 = jnp.tile(gla_norm_g, (1, GLA_H))
    cos_t, sin_t = _rope_tables(Lp)

    u = _rms_in(x, meta, norm_g, B, Lp)
    projA = _mm(u, wA, name="in_proj_a", out_dtype=BF16, tm=tq, tn=1024, tk=D)
    projB = _mm(u, wB, name="in_proj_b", out_dtype=BF16, tm=tq, tn=640, tk=D)
    if late_weights_hook is not None:
        gate_w, gla_proj, w_uq, w_ukv, mla_proj, w_out = late_weights_hook(projB)
    wg = jnp.pad(gate_w, ((0, 128 - GLA_RANK), (0, 0)))
    wuq2 = jnp.pad(w_uq.reshape(Q_RANK, MLA_H, MLA_QK), ((0, 0), (0, 0), (0, 256 - MLA_QK))).reshape(Q_RANK, 2048)
    oa, ya_in, ssave = _gla_fwd(projA, projB, wg, gate_b, gn4, B, Lp)
    ya = _mm(ya_in, gla_proj, name="gla_proj", out_dtype=BF16, tm=tq, tn=D, tk=D)
    q_att, k_att, v_att, cqn, ckvn = _mla_prep(projB, cos_t, sin_t, q_norm_g, kv_norm_g, wuq2, w_ukv, B, Lp, tr)
    ob, yb_in, lse_c = _attn_fwd(q_att, k_att, v_att, projA, B, Lp)
    yb = _mm(yb_in, mla_proj, name="mla_proj", out_dtype=BF16, tm=tq, tn=D, tk=D)
    dh1, dh1_b, merged, loss, d_gf = _out_proj_loss(x, meta, projA, ya, yb, w_out, final_norm_g.reshape(1, D),
                                                     loss_target, B, Lp)

    g_w_out = _mm(merged, dh1_b, name="dw_out", trans_a=True, tm=D, tn=D, tk=tkw)
    dya, dyb, dA = _merge_bwd(dh1_b, w_out, projA, ya, yb, tr)
    g_gla_proj = _mm(ya_in, dya, name="dw_gla_proj", trans_a=True, tm=D, tn=D, tk=tkw)
    g_mla_proj = _mm(yb_in, dyb, name="dw_mla_proj", trans_a=True, tm=D, tn=D, tk=tkw)
    doa, dBz, d_gn = _gla_out_bwd(dya, gla_proj, oa, projA, gn4, tr)
    dC, g_wg, d_bg = _gla_bwd(projA, projB, ssave, doa, wg, gate_b, B, Lp)
    do, dDz, delta_c = _attn_bwd_pre(dyb, mla_proj, projA, ob, B, Lp)
    dq, dk, dv = _attn_bwd(q_att, k_att, v_att, do, lse_c, delta_c, B, Lp)
    dqf, dkvf, dE, d_gq, d_gkv = _mla_bwd_post(dq, dk, dv, projB, cos_t, sin_t, q_norm_g, kv_norm_g,
                                                wuq2, w_ukv, B, Lp, tr)
    g_wuq2 = _mm(cqn, dqf, name="dw_uq", trans_a=True, tm=Q_RANK, tn=2048, tk=tkw)
    g_wukv = _mm(ckvn, dkvf, name="dw_ukv", trans_a=True, tm=KV_RANK, tn=2048, tk=tkw)
    dparts = [dA, dBz, dC, dDz, dE]
    g_in = [_mm(u, dp, name="dw_in_%d" % i, trans_a=True, tm=D, tn=_div_tile(dp.shape[1], 1024, 256), tk=tkw)
            for i, dp in enumerate(dparts)]

    gA, gBz, gC, gDz, gE = g_in
    src = [(gC, 1024), (gC, 1536), (gC, 0), (gC, 2048), (gBz, 0), (gE, 0), (gE, Q_RANK), (gE, 384), (gDz, 0),
           (gA, 0), (gA, D)]
    owners = []
    for j in range(4):
        parts = []
        for i, (arr, off) in enumerate(src):
            a, b = max(cuts[i], j * shard_w), min(cuts[i + 1], (j + 1) * shard_w)
            if a < b:
                parts.append(arr[:, off + a - cuts[i]:off + b - cuts[i]])
        owners.append(jnp.concatenate(parts, axis=1))
    g_w_in = jnp.stack(owners)
    g_wuq = g_wuq2.reshape(Q_RANK, MLA_H, 256)[:, :, :MLA_QK].reshape(Q_RANK, MLA_H * MLA_QK)
    grads = dict(w_in=g_w_in, gla_gate_w=g_wg[:GLA_RANK], gla_proj=g_gla_proj, mla_w_uq=g_wuq, mla_w_ukv=g_wukv,
                 mla_proj=g_mla_proj, w_out=g_w_out, gla_gate_b=d_bg,
                 gla_norm_g=d_gn, mla_q_norm_g=d_gq, mla_kv_norm_g=d_gkv, final_norm_g=d_gf)
    token = None if early_grads_hook is None else early_grads_hook(grads)
    ng = norm_g if token is None else norm_g + token[0:1, 0:1]
    grad_x, d_meta, d_ng = _in_proj_bwd(x, meta, dh1, dA, dBz, dC, dDz, dE, wA, wB, ng, B, Lp)
    grads.update(meta_tokens=d_meta, norm_g=d_ng)
    return loss[0, 0], grad_x, grads


_MATS = ("w_in", "gla_gate_w", "gla_proj", "mla_w_uq", "mla_w_ukv", "mla_proj", "w_out")
_ROW_SHARDED = ("gla_proj", "mla_proj", "w_out")
_ORDER = ("meta_tokens", "norm_g", "w_in", "gla_gate_w", "gla_gate_b", "gla_norm_g", "gla_proj", "mla_q_norm_g",
          "mla_w_uq", "mla_kv_norm_g", "mla_w_ukv", "mla_proj", "w_out", "final_norm_g")
WIRE_BF16_MIN_ELEMS = 128 * 128
SMALL_PACK_ROWS = 16


def _pack_small(d):
    rows = [jnp.pad(d[n].reshape(1, size), ((0, 0), (0, D - size))) for n, size in SMALL]
    return jnp.pad(jnp.concatenate(rows, axis=0), ((0, SMALL_PACK_ROWS - len(rows)), (0, 0)))


def _unpack_small(packed):
    return {n: packed[i, :size] for i, (n, size) in enumerate(SMALL)}


def kernel(x, meta_tokens, norm_g, w_in, gla_gate_w, gla_gate_b, gla_norm_g, gla_proj, mla_q_norm_g, mla_w_uq, mla_kv_norm_g, mla_w_ukv, mla_proj, w_out, final_norm_g, loss_target, m_meta_tokens, m_norm_g, m_w_in, m_gla_gate_w, m_gla_gate_b, m_gla_norm_g, m_gla_proj, m_mla_q_norm_g, m_mla_w_uq, m_mla_kv_norm_g, m_mla_w_ukv, m_mla_proj, m_w_out, m_final_norm_g, v_meta_tokens, v_norm_g, v_w_in, v_gla_gate_w, v_gla_gate_b, v_gla_norm_g, v_gla_proj, v_mla_q_norm_g, v_mla_w_uq, v_mla_kv_norm_g, v_mla_w_ukv, v_mla_proj, v_w_out, v_final_norm_g):
    w = dict(meta_tokens=meta_tokens, norm_g=norm_g, w_in=w_in[0], gla_gate_w=gla_gate_w[0], gla_gate_b=gla_gate_b,
             gla_norm_g=gla_norm_g, gla_proj=gla_proj[0], mla_q_norm_g=mla_q_norm_g, mla_w_uq=mla_w_uq[0],
             mla_kv_norm_g=mla_kv_norm_g, mla_w_ukv=mla_w_ukv[0], mla_proj=mla_proj[0], w_out=w_out[0],
             final_norm_g=final_norm_g)
    mom = dict(meta_tokens=m_meta_tokens, norm_g=m_norm_g, w_in=m_w_in[0], gla_gate_w=m_gla_gate_w[0],
               gla_gate_b=m_gla_gate_b, gla_norm_g=m_gla_norm_g, gla_proj=m_gla_proj[0], mla_q_norm_g=m_mla_q_norm_g,
               mla_w_uq=m_mla_w_uq[0], mla_kv_norm_g=m_mla_kv_norm_g, mla_w_ukv=m_mla_w_ukv[0], mla_proj=m_mla_proj[0],
               w_out=m_w_out[0], final_norm_g=m_final_norm_g)
    var = dict(meta_tokens=v_meta_tokens, norm_g=v_norm_g, w_in=v_w_in[0], gla_gate_w=v_gla_gate_w[0],
               gla_gate_b=v_gla_gate_b, gla_norm_g=v_gla_norm_g, gla_proj=v_gla_proj[0], mla_q_norm_g=v_mla_q_norm_g,
               mla_w_uq=v_mla_w_uq[0], mla_kv_norm_g=v_mla_kv_norm_g, mla_w_ukv=v_mla_w_ukv[0], mla_proj=v_mla_proj[0],
               w_out=v_w_out[0], final_norm_g=v_final_norm_g)
    out_shapes = {n: a.shape for n, a in zip(_ORDER, (meta_tokens, norm_g, w_in, gla_gate_w, gla_gate_b, gla_norm_g,
                                                     gla_proj, mla_q_norm_g, mla_w_uq, mla_kv_norm_g, mla_w_ukv,
                                                     mla_proj, w_out, final_norm_g))}

    me = (2 * lax.axis_index("x") + lax.axis_index("y")).astype(jnp.int32)
    is_mine = lax.broadcasted_iota(jnp.int32, (4, 1, 1), 0) == me
    with_own = lambda gth, own: jnp.where(is_mine, own[None], gth)
    first = [w["w_in"].astype(BF16), meta_tokens]
    w_in_owner, meta_owner = [with_own(gth, own) for gth, own in zip(_weight_gather(first), first)]
    meta_full = meta_owner.transpose(1, 0, 2).reshape(N_META, D)
    late_names = _MATS[1:]
    late = [w[n].astype(BF16) for n in late_names]
    gather_sems = _late_gather_start(late)

    def late_weights(after):
        lands = _late_gather_wait(gather_sems[0], gather_sems[1], gather_sems[2], gather_sems[3], after)
        full = []
        for name, land, own in zip(late_names, lands, late):
            gth = with_own(land, own)
            if name in _ROW_SHARDED:
                full.append(gth.reshape(4 * gth.shape[1], gth.shape[2]))
            else:
                full.append(gth.transpose(1, 0, 2).reshape(gth.shape[1], 4 * gth.shape[2]))
        return full

    def by_owner(name, arr):
        if name == "w_in":
            return arr
        if name in _ROW_SHARDED:
            return arr.reshape(4, arr.shape[0] // 4, arr.shape[1])
        return arr.reshape(arr.shape[0], 4, arr.shape[1] // 4).transpose(1, 0, 2)

    c_idx = lax.axis_index("c").astype(jnp.int32).reshape(1)
    pos = jnp.stack([c_idx[0], me])
    in_flight = {}

    def start_matrix_reduce(early):
        gps = [by_owner(n, early[n]) for n in _MATS]
        recvs = _pair_swap(gps)
        s1 = [_pair_add_big(gps[0], recvs[0], c_idx)] + list(_pair_add_small(gps[1:], recvs[1:]))
        send_sems, recv_sems, parts, lands, token = _chip_scatter_start(s1)
        in_flight.update(send_sems=send_sems, recv_sems=recv_sems, parts=parts, lands=lands)
        return token

    norm_g_after_start = norm_g + gather_sems[4][0:1, 0:1]
    loss_local, grad_x, g = _local_step(
        x, loss_target, meta_full, norm_g_after_start, w_in_owner, None, gla_gate_b, gla_norm_g, None,
        mla_q_norm_g, None, mla_kv_norm_g, None, None, None, final_norm_g,
        early_grads_hook=start_matrix_reduce, late_weights_hook=late_weights)
    loss = lax.psum(loss_local, ("x", "y", "c"))

    s1, landed = _chip_scatter_wait(in_flight["send_sems"], in_flight["recv_sems"], in_flight["parts"],
                                    in_flight["lands"], after=g["norm_g"])
    halves = [_sum_chips_big(landed[0], s1[0], pos)] + list(_sum_chips_small(landed[1:], s1[1:]))
    g_mats = [j.reshape(out_shapes[n]) for j, n in zip(_pair_join(halves), _MATS)]

    late = [g["meta_tokens"], _pack_small(g)]
    meta_sum, small_sum = _sum_devices_small(_all_to_all_small(late), late)
    g_meta = lax.dynamic_slice(meta_sum, (0, me * (D // 4)), (N_META, D // 4))
    names = _MATS + ("meta_tokens",)
    g_red = g_mats + [g_meta, small_sum]

    tens = lambda d: [d[n].reshape(out_shapes[n]) for n in names] + [_pack_small(d)]
    w_t, m_t, v_t = tens(w), tens(mom), tens(var)
    big = _adamw_big(w_t[0], g_red[0], m_t[0], v_t[0])
    rest = _adamw_small(w_t[1:], g_red[1:], m_t[1:], v_t[1:])
    k = len(names)
    results = {"grad": g_red}
    for i, kind in enumerate(("delta", "new_m", "new_v")):
        results[kind] = [big[i]] + list(rest[i * k:(i + 1) * k])

    outs = []
    for kind in ("grad", "delta", "new_m", "new_v"):
        vals = dict(zip(names, results[kind][:-1]))
        vals.update(_unpack_small(results[kind][-1]))
        outs += [vals[n].reshape(out_shapes[n]) for n in _ORDER]
    return (loss, grad_x, *outs)
```

```python
import functools
import math

import jax
import jax.numpy as jnp
import numpy as np
from jax import lax
from jax.experimental import pallas as pl
from jax.experimental.pallas import tpu as pltpu

F32 = jnp.float32
BF16 = jnp.bfloat16

D = 1024
N_META = 16
QB = 256
FRONT = QB - N_META
HEAD_ROWS = FRONT + N_META
assert FRONT % 64 == 48
EPS = 1e-6

GLA_H, GLA_DK, GLA_DV, GLA_RANK, GLA_C = 4, 128, 256, 16, 64
GLA_NORMALIZER = 16.0
GLA_KW, GLA_VW = GLA_H * GLA_DK, GLA_H * GLA_DV
MLA_H, NOPE, ROPE, MLA_DV, Q_RANK, KV_RANK = 8, 128, 64, 128, 256, 128
MLA_QK = NOPE + ROPE
ROPE_BASE = 10000.0
SPLITS = (GLA_KW, GLA_KW, GLA_VW, GLA_RANK, GLA_VW, Q_RANK, KV_RANK, ROPE, MLA_H * MLA_DV, D, D)
IN_WIDTH = sum(SPLITS)

ADAM_LR, ADAM_B1, ADAM_B2, ADAM_EPS, ADAM_WD, ADAM_STEP = 0.001, 0.9, 0.999, 1e-08, 0.01, 10

LANES = 128
VMEM_CAP_V7X = 56 * 1024 * 1024
MESH = pl.DeviceIdType.MESH
NEG = -1e30
LOG2E = math.log2(math.e)

SMALL = (("norm_g", D), ("gla_gate_b", GLA_KW), ("gla_norm_g", GLA_DV), ("mla_q_norm_g", Q_RANK),
         ("mla_kv_norm_g", KV_RANK), ("final_norm_g", D))


def _div_tile(n, target, mult):
    best = None
    for d in range(mult, min(n, target) + 1, mult):
        if n % d == 0:
            best = d
    assert best is not None, (n, target, mult)
    return best


def _params(sem, block_bytes, scratch_bytes=0):
    est = 2 * block_bytes + scratch_bytes + 12 * 1024 * 1024
    return pltpu.CompilerParams(dimension_semantics=sem, vmem_limit_bytes=int(min(max(est, 24 * 1024 * 1024), VMEM_CAP_V7X)))


def _nbytes(shape, dtype):
    return int(np.prod(shape)) * jnp.dtype(dtype).itemsize


def _sigmoid(x):
    return 1.0 / (1.0 + jnp.exp(-x))


def _nt(a, b):
    return lax.dot_general(a, b, (((1,), (1,)), ((), ())), preferred_element_type=F32)


def _tn(a, b):
    return lax.dot_general(a, b, (((0,), (0,)), ((), ())), preferred_element_type=F32)


def _nn(a, b):
    return jnp.dot(a, b, preferred_element_type=F32)


def _split3(x):
    a = x.astype(BF16)
    r = x - a.astype(F32)
    b = r.astype(BF16)
    c = (r - b.astype(F32)).astype(BF16)
    return a, b, c


def _mm(a, b, *, name, trans_a=False, trans_b=False, out_dtype=F32, tm, tn, tk):
    assert not (trans_a and trans_b)
    if trans_a:
        K, M = a.shape
    else:
        M, K = a.shape
    N = b.shape[0] if trans_b else b.shape[1]
    assert (b.shape[1] if trans_b else b.shape[0]) == K
    assert M % tm == 0 and N % tn == 0 and K % tk == 0, (name, M, N, K, tm, tn, tk)
    nk = K // tk

    def body(a_ref, b_ref, o_ref, *scratch):
        av = a_ref[...].astype(BF16)
        bv = b_ref[...].astype(BF16)
        prod = _tn(av, bv) if trans_a else (_nt(av, bv) if trans_b else _nn(av, bv))
        if nk == 1:
            o_ref[...] = prod.astype(out_dtype)
        else:
            acc = scratch[0]
            k = pl.program_id(2)

            @pl.when(k == 0)
            def _():
                acc[...] = prod

            @pl.when(k > 0)
            def _():
                acc[...] += prod

            @pl.when(k == nk - 1)
            def _():
                o_ref[...] = acc[...].astype(out_dtype)

    if trans_a:
        a_spec = pl.BlockSpec((tk, tm), lambda i, j, k: (k, i))
    else:
        a_spec = pl.BlockSpec((tm, tk), lambda i, j, k: (i, k))
    if trans_b:
        b_spec = pl.BlockSpec((tn, tk), lambda i, j, k: (j, k))
    else:
        b_spec = pl.BlockSpec((tk, tn), lambda i, j, k: (k, j))
    blocks = (_nbytes((tm, tk), a.dtype) + _nbytes((tk, tn), b.dtype) + _nbytes((tm, tn), out_dtype))
    scratch = [pltpu.VMEM((tm, tn), F32)] if nk > 1 else []
    return pl.pallas_call(
        body,
        out_shape=jax.ShapeDtypeStruct((M, N), out_dtype),
        grid=(M // tm, N // tn, nk),
        in_specs=[a_spec, b_spec],
        out_specs=pl.BlockSpec((tm, tn), lambda i, j, k: (i, j)),
        scratch_shapes=scratch,
        compiler_params=_params(("parallel", "parallel", "arbitrary"), blocks + _nbytes((tm, tn), F32),
                                _nbytes((tm, tn), F32) if nk > 1 else 0),
        name=name,
    )(a, b)


def _h_tile(j, x_ref, meta_ref):
    head = jnp.concatenate([jnp.zeros((FRONT, D), F32), meta_ref[...]], axis=0)
    return jnp.where(j > 0, x_ref[0], head)


def _x_spec():
    return pl.BlockSpec((1, QB, D), lambda b, j: (b, jnp.maximum(j - 1, 0), 0))


def _rms_in(x, meta, g, B, Lp):
    T = B * Lp
    NQ = Lp // QB

    def body(x_ref, meta_ref, g_ref, u_ref):
        h = _h_tile(pl.program_id(1), x_ref, meta_ref)
        r = lax.rsqrt(jnp.mean(h * h, axis=-1, keepdims=True) + EPS)
        u_ref[...] = (h * r * g_ref[...]).astype(BF16)

    return pl.pallas_call(
        body,
        out_shape=jax.ShapeDtypeStruct((T, D), BF16),
        grid=(B, NQ),
        in_specs=[_x_spec(), pl.BlockSpec((N_META, D), lambda b, j: (0, 0)), pl.BlockSpec((1, D), lambda b, j: (0, 0))],
        out_specs=pl.BlockSpec((QB, D), lambda b, j: (b * NQ + j, 0)),
        compiler_params=_params(("parallel", "parallel"), _nbytes((QB, D), F32) * 2),
        name="rms_in",
    )(x, meta, g)


def _gla_gate(lr, wg, bg, valid):
    pre = _nn(lr.astype(BF16), wg) + bg
    logsig = jnp.minimum(pre, 0.0) - jnp.log(1.0 + jnp.exp(-jnp.abs(pre)))
    return pre, jnp.where(valid, logsig / GLA_NORMALIZER, 0.0)


def _tri_masks():
    ri = lax.broadcasted_iota(jnp.int32, (GLA_C, GLA_C), 0)
    ci = lax.broadcasted_iota(jnp.int32, (GLA_C, GLA_C), 1)
    return ci <= ri, ci >= ri


def _cumsum_rows(x, ones_mask):
    w = jnp.where(ones_mask, 1.0, 0.0).astype(BF16)
    a, b, c = _split3(x)
    return _nn(w, a) + _nn(w, b) + _nn(w, c)


def _gla_fwd(projA, projB, wg, bg, gn4, B, Lp):
    T = B * Lp
    NC = Lp // GLA_C
    C = GLA_C
    scale = GLA_DK ** -0.5

    def body(q_ref, k_ref, v_ref, lr_ref, z_ref, wg_ref, bg_ref, gn_ref, oa_ref, ya_ref, ssave_ref, st_ref):
        n = pl.program_id(0)

        @pl.when(n == 0)
        def _():
            st_ref[...] = jnp.zeros_like(st_ref)

        pos = n * C + lax.broadcasted_iota(jnp.int32, (C, 1), 0)
        lower, _ = _tri_masks()
        is_last = lax.broadcasted_iota(jnp.int32, (C, 1), 0) == C - 1
        for b in range(B):
            ssave_ref[b, 0] = st_ref[b]
            _, glog = _gla_gate(lr_ref[b], wg_ref[...], bg_ref[...], pos >= FRONT)
            bcum = _cumsum_rows(glog, lower)
            for h in range(GLA_H):
                ks = slice(h * GLA_DK, (h + 1) * GLA_DK)
                vs = slice(h * GLA_DV, (h + 1) * GLA_DV)
                bh = bcum[:, ks]
                blast = jnp.sum(jnp.where(is_last, bh, 0.0), axis=0, keepdims=True)
                qh = q_ref[b, :, ks].astype(F32) * scale
                kh = k_ref[b, :, ks].astype(F32)
                qe = (qh * jnp.exp(bh)).astype(BF16)
                ke = (kh * jnp.exp(-bh)).astype(BF16)
                kl = (kh * jnp.exp(blast - bh)).astype(BF16)
                vh = v_ref[b, :, vs].astype(BF16)
                a = jnp.where(lower, _nt(qe, ke), 0.0).astype(BF16)
                st = st_ref[b, h]
                o = _nn(a, vh) + _nt(qe, st.astype(BF16))
                st_ref[b, h] = st * jnp.exp(blast) + _tn(vh, kl)
                oa_ref[b, :, vs] = o
                on = o * lax.rsqrt(jnp.mean(o * o, axis=-1, keepdims=True) + EPS) * gn_ref[:, vs]
                z = z_ref[b, :, vs].astype(F32)
                ya_ref[b, :, vs] = (on * (z * _sigmoid(z))).astype(BF16)

    blocks = B * (_nbytes((C, 512), F32) * 2 + _nbytes((C, 1024), F32) * 3 + _nbytes((C, 1024), BF16)
                  + _nbytes((GLA_H, GLA_DV, GLA_DK), F32)) + _nbytes((128, 512), BF16)
    state = _nbytes((B, GLA_H, GLA_DV, GLA_DK), F32)
    pa = projA.reshape(B, Lp, projA.shape[1])
    oa, ya, ssave = pl.pallas_call(
        body,
        out_shape=(jax.ShapeDtypeStruct((B, Lp, GLA_VW), F32), jax.ShapeDtypeStruct((B, Lp, GLA_VW), BF16),
                   jax.ShapeDtypeStruct((B, NC, GLA_H, GLA_DV, GLA_DK), F32)),
        grid=(NC,),
        in_specs=[
            pl.BlockSpec((B, C, 512), lambda n: (0, n, 10)),
            pl.BlockSpec((B, C, 512), lambda n: (0, n, 11)),
            pl.BlockSpec((B, C, 1024), lambda n: (0, n, 0)),
            pl.BlockSpec((B, C, 128), lambda n: (0, n, 3)),
            pl.BlockSpec((B, C, 1024), lambda n: (0, n, 1)),
            pl.BlockSpec((128, 512), lambda n: (0, 0)),
            pl.BlockSpec((1, 512), lambda n: (0, 0)),
            pl.BlockSpec((1, 1024), lambda n: (0, 0)),
        ],
        out_specs=(pl.BlockSpec((B, C, 1024), lambda n: (0, n, 0)),
                   pl.BlockSpec((B, C, 1024), lambda n: (0, n, 0)),
                   pl.BlockSpec((B, 1, GLA_H, GLA_DV, GLA_DK), lambda n: (0, n, 0, 0, 0))),
        scratch_shapes=[pltpu.VMEM((B, GLA_H, GLA_DV, GLA_DK), F32)],
        compiler_params=_params(("arbitrary",), blocks, state),
        name="gla_fwd",
    )(pa, pa, pa, projB.reshape(B, Lp, projB.shape[1]), pa, wg, bg, gn4)
    return oa.reshape(T, GLA_VW), ya.reshape(T, GLA_VW), ssave


def _swap_halves(x):
    lane = lax.broadcasted_iota(jnp.int32, x.shape, 1)
    return jnp.where((lane % 64) < 32, pltpu.roll(x, 96, 1), pltpu.roll(x, 32, 1))


def _mla_prep(projB, cos_t, sin_t, gq, gkv, wuq2, wukv, B, Lp, tr):
    T = B * Lp
    nt = Lp // tr
    HW = 2 * LANES

    def body(pb_ref, cos_ref, sin_ref, gq_ref, gkv_ref, wuq_ref, wukv_ref, q_ref, k_ref, v_ref, cqn_ref, ckvn_ref):
        cq = pb_ref[:, 0:Q_RANK].astype(F32)
        ckv = pb_ref[:, Q_RANK:Q_RANK + KV_RANK].astype(F32)
        kr = pb_ref[:, 512:640].astype(F32)
        cqn = (cq * lax.rsqrt(jnp.mean(cq * cq, axis=-1, keepdims=True) + EPS) * gq_ref[...]).astype(BF16)
        ckvn = (ckv * lax.rsqrt(jnp.mean(ckv * ckv, axis=-1, keepdims=True) + EPS) * gkv_ref[...]).astype(BF16)
        cqn_ref[...] = cqn
        ckvn_ref[...] = ckvn
        qf = _nn(cqn, wuq_ref[...])
        kvf = _nn(ckvn, wukv_ref[...])
        cs = cos_ref[...]
        sn = sin_ref[...]
        rope = lambda t: t * cs + _swap_halves(t) * sn
        kr_r = rope(kr).astype(BF16)
        for h in range(MLA_H):
            q_ref[:, h * HW:h * HW + LANES] = qf[:, h * HW:h * HW + LANES].astype(BF16)
            q_ref[:, h * HW + LANES:(h + 1) * HW] = rope(qf[:, h * HW + LANES:(h + 1) * HW]).astype(BF16)
            k_ref[:, h * HW:h * HW + LANES] = kvf[:, h * HW:h * HW + LANES].astype(BF16)
            k_ref[:, h * HW + LANES:(h + 1) * HW] = kr_r
            v_ref[:, h * MLA_DV:(h + 1) * MLA_DV] = kvf[:, h * HW + LANES:(h + 1) * HW].astype(BF16)

    blocks = (_nbytes((tr, 640), F32) + 2 * _nbytes((tr, 128), F32) + _nbytes((Q_RANK, 2048), BF16)
              + _nbytes((KV_RANK, 2048), BF16) + _nbytes((tr, 2048 * 2 + 1024 + 384), BF16)
              + 2 * _nbytes((tr, 2048), F32))
    return pl.pallas_call(
        body,
        out_shape=(jax.ShapeDtypeStruct((T, MLA_H * HW), BF16), jax.ShapeDtypeStruct((T, MLA_H * HW), BF16),
                   jax.ShapeDtypeStruct((T, MLA_H * MLA_DV), BF16), jax.ShapeDtypeStruct((T, Q_RANK), BF16),
                   jax.ShapeDtypeStruct((T, KV_RANK), BF16)),
        grid=(B, nt),
        in_specs=[
            pl.BlockSpec((tr, 640), lambda b, j: (b * nt + j, 0)),
            pl.BlockSpec((tr, 128), lambda b, j: (j, 0)),
            pl.BlockSpec((tr, 128), lambda b, j: (j, 0)),
            pl.BlockSpec((1, Q_RANK), lambda b, j: (0, 0)),
            pl.BlockSpec((1, KV_RANK), lambda b, j: (0, 0)),
            pl.BlockSpec((Q_RANK, 2048), lambda b, j: (0, 0)),
            pl.BlockSpec((KV_RANK, 2048), lambda b, j: (0, 0)),
        ],
        out_specs=(pl.BlockSpec((tr, 2048), lambda b, j: (b * nt + j, 0)),
                   pl.BlockSpec((tr, 2048), lambda b, j: (b * nt + j, 0)),
                   pl.BlockSpec((tr, 1024), lambda b, j: (b * nt + j, 0)),
                   pl.BlockSpec((tr, Q_RANK), lambda b, j: (b * nt + j, 0)),
                   pl.BlockSpec((tr, KV_RANK), lambda b, j: (b * nt + j, 0))),
        compiler_params=_params(("parallel", "parallel"), blocks),
        name="mla_prep",
    )(projB, cos_t, sin_t, gq, gkv, wuq2, wukv)


def _attn_mask(row, col):
    return (col <= row) & ((col >= FRONT) | (row < FRONT))


def _attn_fwd(q_att, k_att, v_att, projA, B, Lp):
    T = B * Lp
    NQ = Lp // QB
    HW = 2 * LANES
    scale = 1.0 / math.sqrt(MLA_QK)

    def body(q_ref, k_ref, v_ref, mz_ref, o_ref, yb_ref, lsec_ref, m_ref, l_ref, acc_ref):
        qi = pl.program_id(1)
        m_ref[...] = jnp.full(m_ref.shape, NEG, F32)
        l_ref[...] = jnp.zeros_like(l_ref)
        acc_ref[...] = jnp.zeros_like(acc_ref)
        row = qi * QB + lax.broadcasted_iota(jnp.int32, (QB, QB), 0)
        coli = lax.broadcasted_iota(jnp.int32, (QB, QB), 1)

        def step(kj, masked):
            off = pl.multiple_of(kj * QB, QB)
            ok = _attn_mask(row, kj * QB + coli) if masked else None
            for h in range(MLA_H):
                q = q_ref[:, h * HW:(h + 1) * HW]
                kb = k_ref[pl.ds(off, QB), h * HW:(h + 1) * HW]
                vb = v_ref[pl.ds(off, QB), h * MLA_DV:(h + 1) * MLA_DV]
                s = _nt(q, kb) * (scale * LOG2E)
                if masked:
                    s = jnp.where(ok, s, NEG)
                m_old = m_ref[h]
                m_new = jnp.maximum(m_old, jnp.max(s, axis=-1, keepdims=True))
                alpha = jnp.exp2(m_old - m_new)
                p = jnp.exp2(s - jnp.tile(m_new, (1, QB // LANES)))
                m_ref[h] = m_new
                l_ref[h] = alpha * l_ref[h] + jnp.sum(p, axis=-1, keepdims=True)
                acc_ref[h] = alpha * acc_ref[h] + _nn(p.astype(BF16), vb)

        step(0, True)

        def unmasked(kj, carry):
            step(kj, False)
            return carry

        lax.fori_loop(1, qi, unmasked, 0)

        @pl.when(qi > 0)
        def _():
            step(qi, True)

        for h in range(MLA_H):
            hs = slice(h * MLA_DV, (h + 1) * MLA_DV)
            l = l_ref[h]
            o = acc_ref[h] / l
            o_ref[:, hs] = o
            z = mz_ref[:, hs].astype(F32)
            yb_ref[:, hs] = (o * (z * _sigmoid(z))).astype(BF16)
            lse2 = m_ref[h] + jnp.log(l) * LOG2E
            lsec_ref[0, h, pl.ds(qi, 1), :] = jnp.transpose(lse2)[0:1, :]

    blocks = (_nbytes((QB, 2048), BF16) + _nbytes((Lp, 2048), BF16) + _nbytes((Lp, 1024), BF16)
              + 2 * _nbytes((QB, 1024), F32) + _nbytes((QB, 1024), BF16) + _nbytes((MLA_H, QB, LANES), F32)
              + _nbytes((MLA_H, NQ, QB), F32))
    return pl.pallas_call(
        body,
        out_shape=(jax.ShapeDtypeStruct((T, MLA_H * MLA_DV), F32), jax.ShapeDtypeStruct((T, MLA_H * MLA_DV), BF16),
                   jax.ShapeDtypeStruct((B, MLA_H, NQ, QB), F32)),
        grid=(B, NQ),
        in_specs=[
            pl.BlockSpec((QB, MLA_H * HW), lambda b, i: (b * NQ + i, 0)),
            pl.BlockSpec((Lp, MLA_H * HW), lambda b, i: (b, 0)),
            pl.BlockSpec((Lp, MLA_H * MLA_DV), lambda b, i: (b, 0)),
            pl.BlockSpec((QB, 1024), lambda b, i: (b * NQ + i, 2)),
        ],
        out_specs=(pl.BlockSpec((QB, 1024), lambda b, i: (b * NQ + i, 0)),
                   pl.BlockSpec((QB, 1024), lambda b, i: (b * NQ + i, 0)),
                   pl.BlockSpec((1, MLA_H, NQ, QB), lambda b, i: (b, 0, 0, 0))),
        scratch_shapes=[pltpu.VMEM((MLA_H, QB, LANES), F32), pltpu.VMEM((MLA_H, QB, LANES), F32),
                        pltpu.VMEM((MLA_H, QB, MLA_DV), F32)],
        compiler_params=_params(("parallel", "arbitrary"), blocks, 3 * _nbytes((MLA_H, QB, LANES), F32)),
        name="attn_fwd",
    )(q_att, k_att, v_att, projA)


def _out_proj_loss(x, meta, projA, ya, yb, w_out, gf, tgt, B, Lp):
    T = B * Lp
    NQ = Lp // QB

    def body(x_ref, meta_ref, gg_ref, gm_ref, ya_ref, yb_ref, w_ref, gf_ref, t_ref,
             dh_ref, dhb_ref, mg_ref, loss_ref, dgf_ref):
        b = pl.program_id(0)
        j = pl.program_id(1)

        @pl.when((b == 0) & (j == 0))
        def _():
            loss_ref[...] = jnp.zeros_like(loss_ref)
            dgf_ref[...] = jnp.zeros_like(dgf_ref)

        f32 = lambda ref: ref[...].astype(F32)
        merged = (_sigmoid(f32(gg_ref)) * f32(ya_ref) + _sigmoid(f32(gm_ref)) * f32(yb_ref)).astype(BF16)
        mg_ref[...] = merged
        h1 = _h_tile(j, x_ref, meta_ref) + _nn(merged, w_ref[...])
        r = lax.rsqrt(jnp.mean(h1 * h1, axis=-1, keepdims=True) + EPS)
        hn = h1 * r
        gfv = gf_ref[...]
        diff = jnp.where(j > 0, hn * gfv - t_ref[0], 0.0)
        loss_ref[...] += (0.5 / D) * jnp.sum(jnp.sum(diff * diff, axis=-1, keepdims=True), axis=0, keepdims=True)
        dout = diff * (1.0 / D)
        dgf_ref[...] += jnp.sum(dout * hn, axis=0, keepdims=True)
        dhn = dout * gfv
        dh = r * (dhn - hn * jnp.mean(dhn * hn, axis=-1, keepdims=True))
        dh_ref[...] = dh
        dhb_ref[...] = dh.astype(BF16)

    rows = lambda c: pl.BlockSpec((QB, D), lambda b, j: (b * NQ + j, c))
    const = lambda s: pl.BlockSpec(s, lambda b, j: (0, 0))
    return pl.pallas_call(
        body,
        out_shape=(jax.ShapeDtypeStruct((T, D), F32), jax.ShapeDtypeStruct((T, D), BF16),
                   jax.ShapeDtypeStruct((T, D), BF16), jax.ShapeDtypeStruct((1, 1), F32),
                   jax.ShapeDtypeStruct((1, D), F32)),
        grid=(B, NQ),
        in_specs=[_x_spec(), const((N_META, D)), rows(3), rows(4), rows(0), rows(0), const((D, D)),
                  const((1, D)), _x_spec()],
        out_specs=(rows(0), rows(0), rows(0), const((1, 1)), const((1, D))),
        compiler_params=_params(("arbitrary", "arbitrary"), 10 * _nbytes((QB, D), F32)),
        name="out_proj_loss",
    )(x, meta, projA, projA, ya, yb, w_out, gf, tgt)


def _merge_bwd(dh1_b, w_out, projA, ya, yb, tr):
    T = dh1_b.shape[0]

    def body(dh_ref, w_ref, gg_ref, gm_ref, ya_ref, yb_ref, dya_ref, dyb_ref, da_ref):
        d = _nt(dh_ref[...], w_ref[...])
        sg = _sigmoid(gg_ref[...].astype(F32))
        sm = _sigmoid(gm_ref[...].astype(F32))
        dya_ref[...] = (d * sg).astype(BF16)
        dyb_ref[...] = (d * sm).astype(BF16)
        da_ref[:, 0:D] = (d * ya_ref[...].astype(F32) * (sg * (1.0 - sg))).astype(BF16)
        da_ref[:, D:2 * D] = (d * yb_ref[...].astype(F32) * (sm * (1.0 - sm))).astype(BF16)

    spec = lambda c: pl.BlockSpec((tr, D), lambda i: (i, c))
    return pl.pallas_call(
        body,
        out_shape=(jax.ShapeDtypeStruct((T, D), BF16), jax.ShapeDtypeStruct((T, D), BF16),
                   jax.ShapeDtypeStruct((T, 2 * D), BF16)),
        grid=(T // tr,),
        in_specs=[spec(0), pl.BlockSpec((D, D), lambda i: (0, 0)), spec(3), spec(4), spec(0), spec(0)],
        out_specs=(spec(0), spec(0), pl.BlockSpec((tr, 2 * D), lambda i: (i, 0))),
        compiler_params=_params(("parallel",), 8 * _nbytes((tr, D), F32)),
        name="merge_bwd",
    )(dh1_b, w_out, projA, projA, ya, yb)


def _gla_out_bwd(dya, gla_proj, oa, projA, gn4, tr):
    T = dya.shape[0]
    nsteps = T // tr

    def body(dya_ref, w_ref, oa_ref, z_ref, gn_ref, do_ref, dz_ref, dgn_ref, acc_ref):
        i = pl.program_id(0)

        @pl.when(i == 0)
        def _():
            acc_ref[...] = jnp.zeros_like(acc_ref)

        dy_all = _nt(dya_ref[...], w_ref[...])
        for h in range(GLA_H):
            vs = slice(h * GLA_DV, (h + 1) * GLA_DV)
            dy = dy_all[:, vs]
            o = oa_ref[:, vs]
            z = z_ref[:, vs].astype(F32)
            gn = gn_ref[:, vs]
            s = _sigmoid(z)
            ra = lax.rsqrt(jnp.mean(o * o, axis=-1, keepdims=True) + EPS)
            on = o * ra
            don = dy * (z * s)
            t = don * gn
            do_ref[:, vs] = (ra * (t - on * jnp.mean(t * on, axis=-1, keepdims=True))).astype(BF16)
            dz_ref[:, vs] = (dy * (on * gn) * (s * (1.0 + z * (1.0 - s)))).astype(BF16)
            acc_ref[:, vs] += jnp.sum(don * on, axis=0, keepdims=True)

        @pl.when(i == nsteps - 1)
        def _():
            a = acc_ref[...]
            dgn_ref[...] = a[:, 0:256] + a[:, 256:512] + a[:, 512:768] + a[:, 768:1024]

    spec = lambda c: pl.BlockSpec((tr, D), lambda i: (i, c))
    return pl.pallas_call(
        body,
        out_shape=(jax.ShapeDtypeStruct((T, D), BF16), jax.ShapeDtypeStruct((T, D), BF16),
                   jax.ShapeDtypeStruct((1, GLA_DV), F32)),
        grid=(nsteps,),
        in_specs=[spec(0), pl.BlockSpec((D, D), lambda i: (0, 0)), spec(0), spec(1),
                  pl.BlockSpec((1, D), lambda i: (0, 0))],
        out_specs=(spec(0), spec(0), pl.BlockSpec((1, GLA_DV), lambda i: (0, 0))),
        scratch_shapes=[pltpu.VMEM((1, D), F32)],
        compiler_params=_params(("arbitrary",), 6 * _nbytes((tr, D), F32)),
        name="gla_out_bwd",
    )(dya, gla_proj, oa, projA, gn4)


def _gla_bwd(projA, projB, ssave, doa, wg, bg, B, Lp):
    T = B * Lp
    NC = Lp // GLA_C
    C = GLA_C
    scale = GLA_DK ** -0.5
    WC = 2304

    def body(q_ref, k_ref, v_ref, lr_ref, ss_ref, do_ref, wg_ref, bg_ref, dc_ref, dwg_ref, dbg_ref, dst_ref):
        i = pl.program_id(0)
        n = NC - 1 - i

        @pl.when(i == 0)
        def _():
            dst_ref[...] = jnp.zeros_like(dst_ref)
            dwg_ref[...] = jnp.zeros_like(dwg_ref)
            dbg_ref[...] = jnp.zeros_like(dbg_ref)

        pos = n * C + lax.broadcasted_iota(jnp.int32, (C, 1), 0)
        valid = pos >= FRONT
        lower, upper = _tri_masks()
        is_last = lax.broadcasted_iota(jnp.int32, (C, 1), 0) == C - 1
        for b in range(B):
            lr = lr_ref[b]
            pre, glog = _gla_gate(lr, wg_ref[...], bg_ref[...], valid)
            bcum = _cumsum_rows(glog, lower)
            db_parts = []
            for h in range(GLA_H):
                ks = slice(h * GLA_DK, (h + 1) * GLA_DK)
                vs = slice(h * GLA_DV, (h + 1) * GLA_DV)
                bh = bcum[:, ks]
                blast = jnp.sum(jnp.where(is_last, bh, 0.0), axis=0, keepdims=True)
                eb, enb, ekl, ebl = jnp.exp(bh), jnp.exp(-bh), jnp.exp(blast - bh), jnp.exp(blast)
                qh = q_ref[b, :, ks].astype(F32) * scale
                kh = k_ref[b, :, ks].astype(F32)
                qe_f, ke_f, kl_f = qh * eb, kh * enb, kh * ekl
                qe, ke, kl = qe_f.astype(BF16), ke_f.astype(BF16), kl_f.astype(BF16)
                vh = v_ref[b, :, vs].astype(BF16)
                doh = do_ref[b, :, vs]
                st = ss_ref[b, 0, h]
                dst = dst_ref[b, h]
                st_b, dst_b = st.astype(BF16), dst.astype(BF16)
                da = jnp.where(lower, _nt(doh, vh), 0.0).astype(BF16)
                da_t = jnp.where(upper, _nt(vh, doh), 0.0).astype(BF16)
                a_t = jnp.where(upper, _nt(ke, qe), 0.0).astype(BF16)
                dqe = _nn(da, ke) + _nn(doh, st_b)
                dke = _nn(da_t, qe)
                dvh = _nn(a_t, doh) + _nt(kl, dst_b)
                dkl = _nn(vh, dst_b)
                dst_ref[b, h] = dst * ebl + _tn(doh, qe)
                deb = jnp.sum(st * dst, axis=0, keepdims=True)
                db = dqe * qe_f - dke * ke_f - dkl * kl_f
                db_last = jnp.sum(dkl * kl_f, axis=0, keepdims=True) + deb * ebl
                db_parts.append(db + jnp.where(is_last, db_last, 0.0))
                dc_ref[b, :, vs] = dvh.astype(BF16)
                dc_ref[b, :, 1024 + h * GLA_DK:1024 + (h + 1) * GLA_DK] = (dqe * eb * scale).astype(BF16)
                dc_ref[b, :, 1536 + h * GLA_DK:1536 + (h + 1) * GLA_DK] = (dke * enb + dkl * ekl).astype(BF16)
            dglog = _cumsum_rows(jnp.concatenate(db_parts, axis=1), upper)
            dpre = jnp.where(valid, dglog * (1.0 / GLA_NORMALIZER) / (1.0 + jnp.exp(pre)), 0.0)
            dpre_b = dpre.astype(BF16)
            dc_ref[b, :, 2048:2176] = _nt(dpre_b, wg_ref[...]).astype(BF16)
            dc_ref[b, :, 2176:2304] = jnp.zeros((C, 128), BF16)
            dwg_ref[...] += _tn(lr.astype(BF16), dpre_b)
            dbg_ref[...] += jnp.sum(dpre, axis=0, keepdims=True)

    blocks = B * (_nbytes((C, 512), F32) * 2 + _nbytes((C, 1024), F32) + _nbytes((C, 1024), BF16)
                  + _nbytes((GLA_H, GLA_DV, GLA_DK), F32) + _nbytes((C, WC), BF16)) + 3 * _nbytes((128, 512), F32)
    state = _nbytes((B, GLA_H, GLA_DV, GLA_DK), F32)
    pa = projA.reshape(B, Lp, projA.shape[1])
    rev = lambda i: NC - 1 - i
    dc, dwg, dbg = pl.pallas_call(
        body,
        out_shape=(jax.ShapeDtypeStruct((B, Lp, WC), BF16), jax.ShapeDtypeStruct((128, GLA_KW), F32),
                   jax.ShapeDtypeStruct((1, GLA_KW), F32)),
        grid=(NC,),
        in_specs=[
            pl.BlockSpec((B, C, 512), lambda i: (0, rev(i), 10)),
            pl.BlockSpec((B, C, 512), lambda i: (0, rev(i), 11)),
            pl.BlockSpec((B, C, 1024), lambda i: (0, rev(i), 0)),
            pl.BlockSpec((B, C, 128), lambda i: (0, rev(i), 3)),
            pl.BlockSpec((B, 1, GLA_H, GLA_DV, GLA_DK), lambda i: (0, rev(i), 0, 0, 0)),
            pl.BlockSpec((B, C, 1024), lambda i: (0, rev(i), 0)),
            pl.BlockSpec((128, 512), lambda i: (0, 0)),
            pl.BlockSpec((1, 512), lambda i: (0, 0)),
        ],
        out_specs=(pl.BlockSpec((B, C, WC), lambda i: (0, rev(i), 0)),
                   pl.BlockSpec((128, GLA_KW), lambda i: (0, 0)),
                   pl.BlockSpec((1, GLA_KW), lambda i: (0, 0))),
        scratch_shapes=[pltpu.VMEM((B, GLA_H, GLA_DV, GLA_DK), F32)],
        compiler_params=_params(("arbitrary",), blocks, state),
        name="gla_bwd",
    )(pa, pa, pa, projB.reshape(B, Lp, projB.shape[1]), ssave, doa.reshape(B, Lp, GLA_VW), wg, bg)
    return dc.reshape(T, WC), dwg, dbg


def _attn_bwd_pre(dyb, mla_proj, projA, ob, B, Lp):
    T = B * Lp
    NQ = Lp // QB

    def body(dyb_ref, w_ref, z_ref, o_ref, do_ref, dz_ref, dcol_ref):
        j = pl.program_id(1)
        dy_all = _nt(dyb_ref[...], w_ref[...])
        for h in range(MLA_H):
            hs = slice(h * MLA_DV, (h + 1) * MLA_DV)
            dy = dy_all[:, hs]
            z = z_ref[:, hs].astype(F32)
            o = o_ref[:, hs]
            s = _sigmoid(z)
            do = dy * (z * s)
            do_ref[:, hs] = do.astype(BF16)
            dz_ref[:, hs] = (dy * o * (s * (1.0 + z * (1.0 - s)))).astype(BF16)
            dl = jnp.broadcast_to(jnp.sum(do * o, axis=-1, keepdims=True), (QB, LANES))
            dcol_ref[0, h, pl.ds(j, 1), :] = jnp.transpose(dl)[0:1, :]

    rows = lambda c: pl.BlockSpec((QB, D), lambda b, j: (b * NQ + j, c))
    return pl.pallas_call(
        body,
        out_shape=(jax.ShapeDtypeStruct((T, D), BF16), jax.ShapeDtypeStruct((T, D), BF16),
                   jax.ShapeDtypeStruct((B, MLA_H, NQ, QB), F32)),
        grid=(B, NQ),
        in_specs=[rows(0), pl.BlockSpec((D, D), lambda b, j: (0, 0)), rows(2), rows(0)],
        out_specs=(rows(0), rows(0), pl.BlockSpec((1, MLA_H, NQ, QB), lambda b, j: (b, 0, 0, 0))),
        compiler_params=_params(("parallel", "arbitrary"), 6 * _nbytes((QB, D), F32)),
        name="attn_bwd_pre",
    )(dyb, mla_proj, projA, ob)


ATTN_BWD_HEADS = 8


def _attn_bwd(q_att, k_att, v_att, do, lse_c, delta_c, B, Lp):
    T = B * Lp
    NQ = Lp // QB
    G = ATTN_BWD_HEADS
    NG = MLA_H // G
    HW = 2 * LANES
    scale = 1.0 / math.sqrt(MLA_QK)

    def body(q_ref, k_ref, v_ref, do_ref, lse_ref, dl_ref, dq_out, dk_out, dv_out, dq_ref, dk_ref, dv_ref):
        kj = pl.program_id(2)

        @pl.when(kj == 0)
        def _():
            dq_ref[...] = jnp.zeros_like(dq_ref)

        dk_ref[...] = jnp.zeros_like(dk_ref)
        dv_ref[...] = jnp.zeros_like(dv_ref)
        col = kj * QB + lax.broadcasted_iota(jnp.int32, (QB, QB), 0)
        rowi = lax.broadcasted_iota(jnp.int32, (QB, QB), 1)

        def step(qi, masked):
            off = pl.multiple_of(qi * QB, QB)
            ok = _attn_mask(qi * QB + rowi, col) if masked else None
            for h in range(G):
                ws = slice(h * HW, (h + 1) * HW)
                hs = slice(h * MLA_DV, (h + 1) * MLA_DV)
                qb = q_ref[pl.ds(off, QB), ws]
                dob = do_ref[pl.ds(off, QB), hs]
                kb = k_ref[:, ws]
                lse2 = lse_ref[0, h, pl.ds(qi, 1), :]
                delta = dl_ref[0, h, pl.ds(qi, 1), :]
                p_t = jnp.exp2(_nt(kb, qb) * (scale * LOG2E) - lse2)
                if masked:
                    p_t = jnp.where(ok, p_t, 0.0)
                dv_ref[:, hs] += _nn(p_t.astype(BF16), dob)
                ds_t = (p_t * (_nt(v_ref[:, hs], dob) - delta) * scale).astype(BF16)
                dk_ref[:, ws] += _nn(ds_t, qb)
                dq_ref[pl.ds(off, QB), ws] += _tn(ds_t, kb)

        def loop(masked):
            def it(qi, carry):
                step(qi, masked)
                return carry
            lax.fori_loop(kj + 1, NQ, it, 0)

        step(kj, True)
        pl.when(kj == 0)(lambda: loop(True))
        pl.when(kj > 0)(lambda: loop(False))
        dk_out[...] = dk_ref[...].astype(BF16)
        dv_out[...] = dv_ref[...].astype(BF16)

        @pl.when(kj == NQ - 1)
        def _():
            dq_out[...] = dq_ref[...].astype(BF16)

    blocks = (2 * _nbytes((Lp, G * HW), BF16) + _nbytes((Lp, G * MLA_DV), BF16) + 2 * _nbytes((QB, G * 384), BF16)
              + 2 * _nbytes((G, NQ, QB), F32))
    scratch = [pltpu.VMEM((Lp, G * HW), F32), pltpu.VMEM((QB, G * HW), F32), pltpu.VMEM((QB, G * MLA_DV), F32)]
    return pl.pallas_call(
        body,
        out_shape=(jax.ShapeDtypeStruct((T, MLA_H * HW), BF16), jax.ShapeDtypeStruct((T, MLA_H * HW), BF16),
                   jax.ShapeDtypeStruct((T, MLA_H * MLA_DV), BF16)),
        scratch_shapes=scratch,
        grid=(B, NG, NQ),
        in_specs=[
            pl.BlockSpec((Lp, G * HW), lambda b, g, j: (b, g), pipeline_mode=pl.Buffered(1)),
            pl.BlockSpec((QB, G * HW), lambda b, g, j: (b * NQ + j, g)),
            pl.BlockSpec((QB, G * MLA_DV), lambda b, g, j: (b * NQ + j, g)),
            pl.BlockSpec((Lp, G * MLA_DV), lambda b, g, j: (b, g), pipeline_mode=pl.Buffered(1)),
            pl.BlockSpec((1, G, NQ, QB), lambda b, g, j: (b, g, 0, 0)),
            pl.BlockSpec((1, G, NQ, QB), lambda b, g, j: (b, g, 0, 0)),
        ],
        out_specs=(pl.BlockSpec((Lp, G * HW), lambda b, g, j: (b, g), pipeline_mode=pl.Buffered(1)),
                   pl.BlockSpec((QB, G * HW), lambda b, g, j: (b * NQ + j, g)),
                   pl.BlockSpec((QB, G * MLA_DV), lambda b, g, j: (b * NQ + j, g))),
        compiler_params=_params(("parallel", "parallel", "arbitrary"), blocks,
                                _nbytes((Lp, G * HW), F32) + _nbytes((QB, G * 384), F32)),
        name="attn_bwd",
    )(q_att, k_att, v_att, do, lse_c, delta_c)


def _mla_bwd_post(dq, dk, dv, projB, cos_t, sin_t, gq, gkv, wuq2, wukv, B, Lp, tr):
    T = B * Lp
    nt = Lp // tr
    HW = 2 * LANES

    def body(dq_ref, dk_ref, dv_ref, pb_ref, cos_ref, sin_ref, gq_ref, gkv_ref, wuq_ref, wukv_ref,
             dqf_ref, dkvf_ref, de_ref, dgq_ref, dgkv_ref):
        first = (pl.program_id(0) == 0) & (pl.program_id(1) == 0)

        @pl.when(first)
        def _():
            dgq_ref[...] = jnp.zeros_like(dgq_ref)
            dgkv_ref[...] = jnp.zeros_like(dgkv_ref)

        cs = cos_ref[...]
        sn = sin_ref[...]
        rope_t = lambda t: t * cs + _swap_halves(t * sn)
        dkr = jnp.zeros((tr, LANES), F32)
        for h in range(MLA_H):
            dqf_ref[:, h * HW:h * HW + LANES] = dq_ref[:, h * HW:h * HW + LANES]
            dq_rope = dq_ref[:, h * HW + LANES:(h + 1) * HW].astype(F32)
            dqf_ref[:, h * HW + LANES:(h + 1) * HW] = rope_t(dq_rope).astype(BF16)
            dkvf_ref[:, h * HW:h * HW + LANES] = dk_ref[:, h * HW:h * HW + LANES]
            dkvf_ref[:, h * HW + LANES:(h + 1) * HW] = dv_ref[:, h * MLA_DV:(h + 1) * MLA_DV]
            dkr = dkr + dk_ref[:, h * HW + LANES:(h + 1) * HW].astype(F32)

        def norm_bwd(x, dn, g):
            r = lax.rsqrt(jnp.mean(x * x, axis=-1, keepdims=True) + EPS)
            xn = x * r
            t = dn * g
            return r * (t - xn * jnp.mean(t * xn, axis=-1, keepdims=True)), jnp.sum(dn * xn, axis=0, keepdims=True)

        cq = pb_ref[:, 0:Q_RANK].astype(F32)
        ckv = pb_ref[:, Q_RANK:Q_RANK + KV_RANK].astype(F32)
        dcq, dgq = norm_bwd(cq, _nt(dqf_ref[...], wuq_ref[...]), gq_ref[...])
        dckv, dgkv = norm_bwd(ckv, _nt(dkvf_ref[...], wukv_ref[...]), gkv_ref[...])
        dgq_ref[...] += dgq
        dgkv_ref[...] += dgkv
        de_ref[:, 0:Q_RANK] = dcq.astype(BF16)
        de_ref[:, Q_RANK:Q_RANK + KV_RANK] = dckv.astype(BF16)
        de_ref[:, 384:512] = rope_t(dkr).astype(BF16)

    rows = lambda w: pl.BlockSpec((tr, w), lambda b, j: (b * nt + j, 0))
    const = lambda s: pl.BlockSpec(s, lambda b, j: (0, 0))
    blocks = (2 * _nbytes((tr, 2048), F32) + _nbytes((tr, 1024), F32) + _nbytes((tr, 640), F32)
              + 2 * _nbytes((tr, 2048), BF16) + _nbytes((2048, 384), BF16) + 2 * _nbytes((tr, 2048), F32))
    return pl.pallas_call(
        body,
        out_shape=(jax.ShapeDtypeStruct((T, 2048), BF16), jax.ShapeDtypeStruct((T, 2048), BF16),
                   jax.ShapeDtypeStruct((T, 512), BF16), jax.ShapeDtypeStruct((1, Q_RANK), F32),
                   jax.ShapeDtypeStruct((1, KV_RANK), F32)),
        grid=(B, nt),
        in_specs=[rows(2048), rows(2048), rows(1024), rows(640),
                  pl.BlockSpec((tr, 128), lambda b, j: (j, 0)), pl.BlockSpec((tr, 128), lambda b, j: (j, 0)),
                  const((1, Q_RANK)), const((1, KV_RANK)), const((Q_RANK, 2048)), const((KV_RANK, 2048))],
        out_specs=(rows(2048), rows(2048), rows(512), const((1, Q_RANK)), const((1, KV_RANK))),
        compiler_params=_params(("arbitrary", "arbitrary"), blocks),
        name="mla_bwd_post",
    )(dq, dk, dv, projB, cos_t, sin_t, gq, gkv, wuq2, wukv)


def _in_proj_bwd(x, meta, dh1, dA, dBz, dC, dDz, dE, wA, wB, g, B, Lp):
    NQ = Lp // QB
    seq = x.shape[1]

    def body(x_ref, meta_ref, dh_ref, da_ref, db_ref, dc_ref, dd_ref, de_ref, wa_ref, wb_ref, g_ref,
             gx_ref, dmeta_ref, dg_ref):
        b = pl.program_id(0)
        j = pl.program_id(1)

        @pl.when((b == 0) & (j == 0))
        def _():
            dg_ref[...] = jnp.zeros_like(dg_ref)

        du = _nt(da_ref[...], wa_ref[:, 3072:5120])
        du = du + _nt(db_ref[...], wa_ref[:, 1024:2048])
        du = du + _nt(dd_ref[...], wa_ref[:, 2048:3072])
        du = du + _nt(dc_ref[:, 0:1024], wa_ref[:, 0:1024])
        du = du + _nt(dc_ref[:, 1024:2048], wa_ref[:, 5120:6144])
        du = du + _nt(dc_ref[:, 2048:2176], wb_ref[:, 384:512])
        du = du + _nt(de_ref[:, 0:384], wb_ref[:, 0:384])
        du = du + _nt(de_ref[:, 384:512], wb_ref[:, 512:640])

        x = _h_tile(j, x_ref, meta_ref)
        r = lax.rsqrt(jnp.mean(x * x, axis=-1, keepdims=True) + EPS)
        xn = x * r
        t = du * g_ref[...]
        dh0 = dh_ref[...] + r * (t - xn * jnp.mean(t * xn, axis=-1, keepdims=True))
        dg_ref[...] += jnp.sum(du * xn, axis=0, keepdims=True)
        gx_ref[0] = dh0

        @pl.when((j == 0) & (b == 0))
        def _():
            dmeta_ref[...] = dh0[FRONT:HEAD_ROWS, :]

        @pl.when((j == 0) & (b > 0))
        def _():
            dmeta_ref[...] += dh0[FRONT:HEAD_ROWS, :]

    rows = lambda w: pl.BlockSpec((QB, w), lambda b, j: (b * NQ + j, 0))
    const = lambda s: pl.BlockSpec(s, lambda b, j: (0, 0))
    widths = [a.shape[1] for a in (dA, dBz, dC, dDz, dE)]
    blocks = (sum(_nbytes((QB, w), BF16) for w in widths) + _nbytes(wA.shape, BF16) + _nbytes(wB.shape, BF16)
              + 4 * _nbytes((QB, D), F32))
    return pl.pallas_call(
        body,
        out_shape=(jax.ShapeDtypeStruct((B, seq, D), F32), jax.ShapeDtypeStruct((N_META, D), F32),
                   jax.ShapeDtypeStruct((1, D), F32)),
        grid=(B, NQ),
        in_specs=[_x_spec(), const((N_META, D)), rows(D)] + [rows(w) for w in widths]
        + [const(wA.shape), const(wB.shape), const((1, D))],
        out_specs=(_x_spec(), const((N_META, D)), const((1, D))),
        compiler_params=_params(("arbitrary", "arbitrary"), blocks),
        name="in_proj_bwd",
    )(x, meta, dh1, dA, dBz, dC, dDz, dE, wA, wB, g)


_VMEM_WHOLE = pl.BlockSpec(memory_space=pltpu.VMEM)


def _params_whole(arrays):
    total = sum(_nbytes(a.shape, a.dtype) for a in arrays)
    return pltpu.CompilerParams(vmem_limit_bytes=int(min(total + 12 * 1024 * 1024, VMEM_CAP_V7X)))


def _wire_dtype(shape):
    return BF16 if shape[-2] * shape[-1] >= WIRE_BF16_MIN_ELEMS else F32


def _pair_add_big(gp, recv, c):
    _, half, cols = recv.shape
    th = _div_tile(half, 64, 16)
    out_dtype = _wire_dtype(recv.shape)

    steps = half // th

    def body(c_ref, a_ref, b_ref, o_ref):
        o_ref[...] = (a_ref[...] + b_ref[...]).astype(out_dtype)

    return pl.pallas_call(
        body,
        out_shape=jax.ShapeDtypeStruct(recv.shape, out_dtype),
        grid_spec=pltpu.PrefetchScalarGridSpec(
            num_scalar_prefetch=1,
            grid=(steps,),
            in_specs=[pl.BlockSpec((4, th, cols), lambda i, c_ref: (0, c_ref[0] * steps + i, 0)),
                      pl.BlockSpec((4, th, cols), lambda i, c_ref: (0, i, 0))],
            out_specs=pl.BlockSpec((4, th, cols), lambda i, c_ref: (0, i, 0)),
        ),
        compiler_params=_params(("parallel",), 3 * _nbytes((4, th, cols), F32)),
        name="grad_pair_add_big",
    )(c, gp, recv)


def _pair_add_small(gps, recvs):
    n = len(gps)

    def body(*refs):
        c = lax.axis_index("c")
        for t in range(n):
            g_ref, r_ref, o_ref = refs[t], refs[n + t], refs[2 * n + t]
            half = r_ref.shape[1]
            s = g_ref[:, pl.ds(pl.multiple_of(c * half, 8), half), :] + r_ref[...]
            o_ref[...] = s.astype(o_ref.dtype)

    return pl.pallas_call(
        body,
        out_shape=[jax.ShapeDtypeStruct(r.shape, _wire_dtype(r.shape)) for r in recvs],
        in_specs=[_VMEM_WHOLE] * (2 * n),
        out_specs=[_VMEM_WHOLE] * n,
        compiler_params=_params_whole(list(gps) + 2 * list(recvs)),
        name="grad_pair_add_small",
    )(*gps, *recvs)


def _chip_order_sum(landed_ref, own_ref, me):
    p = [jnp.where(me == k, own_ref[k], landed_ref[k]).astype(F32) for k in range(4)]
    return ((p[0] + p[1]) + p[2]) + p[3]


def _sum_chips_big(landed, own, pos):
    _, half, cols = landed.shape
    th = _div_tile(half, 64, 16)

    def body(pos_ref, l_ref, s_ref, o_ref):
        o_ref[0] = _chip_order_sum(l_ref, s_ref, pos_ref[1])

    spec = pl.BlockSpec((4, th, cols), lambda i, pos_ref: (0, i, 0))
    return pl.pallas_call(
        body,
        out_shape=jax.ShapeDtypeStruct((2, half, cols), F32),
        grid_spec=pltpu.PrefetchScalarGridSpec(
            num_scalar_prefetch=1,
            grid=(half // th,),
            in_specs=[spec, spec],
            out_specs=pl.BlockSpec((1, th, cols), lambda i, pos_ref: (pos_ref[0], i, 0)),
        ),
        compiler_params=_params(("parallel",), 3 * _nbytes((4, th, cols), F32)),
        name="grad_sum_chips_big",
    )(pos, landed, own)


def _sum_chips_small(landed, own):
    n = len(landed)

    def body(*refs):
        x, y, c = _mesh_pos()
        for t in range(n):
            refs[2 * n + t][c] = _chip_order_sum(refs[t], refs[n + t], 2 * x + y)

    return pl.pallas_call(
        body,
        out_shape=[jax.ShapeDtypeStruct((2,) + p.shape[1:], F32) for p in landed],
        in_specs=[_VMEM_WHOLE] * (2 * n),
        out_specs=[_VMEM_WHOLE] * n,
        compiler_params=_params_whole(list(landed) * 3),
        name="grad_sum_chips_small",
    )(*landed, *own)


def _adamw_update(w_ref, g_ref, m_ref, v_ref, d_ref, mo_ref, vo_ref):
    c1 = 1.0 - ADAM_B1 ** ADAM_STEP
    c2 = 1.0 - ADAM_B2 ** ADAM_STEP
    gv = g_ref[...]
    mn = ADAM_B1 * m_ref[...] + (1.0 - ADAM_B1) * gv
    vn = ADAM_B2 * v_ref[...] + (1.0 - ADAM_B2) * (gv * gv)
    mo_ref[...] = mn
    vo_ref[...] = vn
    d_ref[...] = -ADAM_LR * ((mn / c1) / (jnp.sqrt(vn / c2) + ADAM_EPS) + ADAM_WD * w_ref[...])


def _adamw_big(w, g, m, v):
    lead, (rows, cols) = w.shape[:-2], w.shape[-2:]
    assert all(n == 1 for n in lead)
    tr = _div_tile(rows, (1 << 19) // cols, 8)
    spec = pl.BlockSpec((1,) * len(lead) + (tr, cols), lambda i: (0,) * len(lead) + (i, 0))
    shp = jax.ShapeDtypeStruct(w.shape, F32)
    return pl.pallas_call(
        functools.partial(_adamw_update),
        out_shape=(shp, shp, shp),
        grid=(rows // tr,),
        in_specs=[spec] * 4,
        out_specs=(spec, spec, spec),
        compiler_params=_params(("parallel",), 7 * _nbytes((tr, cols), F32)),
        name="adamw_big",
    )(w, g, m, v)


def _adamw_small(ws, gs, ms, vs):
    n = len(ws)

    def body(*refs):
        for t in range(n):
            _adamw_update(refs[t], refs[n + t], refs[2 * n + t], refs[3 * n + t],
                          refs[4 * n + t], refs[5 * n + t], refs[6 * n + t])

    shapes = [jax.ShapeDtypeStruct(w.shape, F32) for w in ws]
    return pl.pallas_call(
        body,
        out_shape=shapes * 3,
        in_specs=[_VMEM_WHOLE] * (4 * n),
        out_specs=[_VMEM_WHOLE] * (3 * n),
        compiler_params=_params_whole(list(ws) * 7),
        name="adamw_small",
    )(*ws, *gs, *ms, *vs)


def _mesh_pos():
    return lax.axis_index("x"), lax.axis_index("y"), lax.axis_index("c")


def _other_chips(x, y):
    return [(1 - x, y), (x, 1 - y), (1 - x, 1 - y)]


_ANY = pl.BlockSpec(memory_space=pl.ANY)


PAIR_SPLIT_MIN_ROWS = 64


def _weight_gather(shards):
    n = len(shards)
    split = [s.shape[0] >= PAIR_SPLIT_MIN_ROWS for s in shards]

    def body(*refs):
        w_refs, o_refs = refs[:n], refs[n:2 * n]
        send_sems, recv_sems = refs[2 * n:]
        x, y, c = _mesh_pos()
        me = 2 * x + y
        chips = _other_chips(x, y)

        def rows_of(t, core):
            rows = shards[t].shape[0]
            if not split[t]:
                return pl.ds(0, rows)
            return pl.ds(pl.multiple_of(core * (rows // 2), 16), rows // 2)

        def landed(t, k, slot, rows, to):
            ref = o_refs[t].at[slot, rows]
            return pltpu.make_async_remote_copy(src_ref=ref, dst_ref=ref, send_sem=send_sems.at[6 * t + k],
                                                recv_sem=recv_sems.at[6 * t + k], device_id=to, device_id_type=MESH)

        sends = []
        for t in range(n):
            mine = rows_of(t, c)
            for k, (px, py) in enumerate(chips):
                cp = pltpu.make_async_remote_copy(src_ref=w_refs[t].at[mine], dst_ref=o_refs[t].at[me, mine],
                                                  send_sem=send_sems.at[6 * t + k], recv_sem=recv_sems.at[6 * t + k],
                                                  device_id=(px, py, c), device_id_type=MESH)
                cp.start()
                sends.append(cp)
        for t in range(n):
            mine = rows_of(t, c)
            for k, (px, py) in enumerate(chips):
                landed(t, k, 2 * px + py, mine, (x, y, c)).wait_recv()
                if split[t]:
                    cp = landed(t, 3 + k, 2 * px + py, mine, (x, y, 1 - c))
                    cp.start()
                    sends.append(cp)
        for t in range(n):
            if split[t]:
                for k, (px, py) in enumerate(chips):
                    landed(t, 3 + k, 2 * px + py, rows_of(t, 1 - c), (x, y, c)).wait_recv()
        for cp in sends:
            cp.wait_send()

    return pl.pallas_call(
        body,
        out_shape=[jax.ShapeDtypeStruct((4,) + s.shape, s.dtype) for s in shards],
        in_specs=[_ANY] * n,
        out_specs=[_ANY] * n,
        scratch_shapes=[pltpu.SemaphoreType.DMA((6 * n,)), pltpu.SemaphoreType.DMA((6 * n,))],
        name="weight_gather",
    )(*shards)


def _pair_swap(gps):
    n = len(gps)

    def body(*refs):
        g_refs, o_refs = refs[:n], refs[n:2 * n]
        send_sems, recv_sems = refs[2 * n:]
        x, y, c = _mesh_pos()
        copies = []
        for t in range(n):
            half = gps[t].shape[1] // 2
            theirs = pl.ds(pl.multiple_of((1 - c) * half, 8), half)
            cp = pltpu.make_async_remote_copy(src_ref=g_refs[t].at[:, theirs], dst_ref=o_refs[t],
                                              send_sem=send_sems.at[t], recv_sem=recv_sems.at[t],
                                              device_id=(x, y, 1 - c), device_id_type=MESH)
            cp.start()
            copies.append(cp)
        for cp in copies:
            cp.wait_send()
            cp.wait_recv()

    return pl.pallas_call(
        body,
        out_shape=[jax.ShapeDtypeStruct((4, g.shape[1] // 2, g.shape[2]), g.dtype) for g in gps],
        in_specs=[_ANY] * n,
        out_specs=[_ANY] * n,
        scratch_shapes=[pltpu.SemaphoreType.DMA((n,)), pltpu.SemaphoreType.DMA((n,))],
        name="grad_pair_swap",
    )(*gps)


_HBM = pl.BlockSpec(memory_space=pltpu.HBM)
_SEM = pl.BlockSpec(memory_space=pltpu.SEMAPHORE)


def _in_hbm(a):
    return pltpu.with_memory_space_constraint(a, pltpu.HBM)


def _chip_scatter_start(parts):
    n = len(parts)

    def body(*refs):
        s_refs, l_refs = refs[:n], refs[n:2 * n]
        send_sems, recv_sems = refs[2 * n], refs[2 * n + 1]
        token = refs[-1]
        x, y, c = _mesh_pos()
        me = 2 * x + y
        for t in range(n):
            for k, (px, py) in enumerate(_other_chips(x, y)):
                pltpu.make_async_remote_copy(src_ref=s_refs[t].at[2 * px + py], dst_ref=l_refs[t].at[me],
                                             send_sem=send_sems.at[3 * t + k], recv_sem=recv_sems.at[3 * t + k],
                                             device_id=(px, py, c), device_id_type=MESH).start()
        token[...] = jnp.zeros_like(token)

    hbm = [pltpu.HBM(p.shape, p.dtype) for p in parts]
    outs = pl.pallas_call(
        body,
        name="grad_scatter_start",
        out_shape=(pltpu.SemaphoreType.DMA((3 * n,)), pltpu.SemaphoreType.DMA((3 * n,)), *hbm, *hbm,
                   jax.ShapeDtypeStruct((8, LANES), F32)),
        in_specs=[_HBM] * (2 * n),
        out_specs=(_SEM, _SEM, *([_HBM] * (2 * n)), pl.BlockSpec(memory_space=pltpu.VMEM)),
        input_output_aliases={i: 2 + i for i in range(2 * n)},
        compiler_params=pltpu.CompilerParams(has_side_effects=pltpu.SideEffectType.DATAFLOW_SIDE_EFFECTING),
    )(*[_in_hbm(p) for p in parts], *[_in_hbm(lax.empty(p.shape, p.dtype)) for p in parts])
    return outs[0], outs[1], list(outs[2:2 + n]), list(outs[2 + n:2 + 2 * n]), outs[-1]


def _chip_scatter_wait(send_sems, recv_sems, parts, lands, after):
    n = len(parts)

    def body(*refs):
        s_refs, l_refs = refs[:n], refs[n:2 * n]
        send_sems, recv_sems = refs[2 * n], refs[2 * n + 1]
        x, y, c = _mesh_pos()
        me = 2 * x + y
        for t in range(n):
            for k, (px, py) in enumerate(_other_chips(x, y)):
                cp = pltpu.make_async_remote_copy(src_ref=s_refs[t].at[2 * px + py], dst_ref=l_refs[t].at[2 * px + py],
                                                  send_sem=send_sems.at[3 * t + k], recv_sem=recv_sems.at[3 * t + k],
                                                  device_id=(x, y, c), device_id_type=MESH)
                cp.wait_send()
                cp.wait_recv()

    hbm = [pltpu.HBM(p.shape, p.dtype) for p in parts]
    outs = pl.pallas_call(
        body,
        name="grad_scatter_wait",
        out_shape=(*hbm, *hbm),
        in_specs=[_HBM] * (2 * n) + [_SEM, _SEM, _ANY],
        out_specs=[_HBM] * (2 * n),
        input_output_aliases={i: i for i in range(2 * n)},
        compiler_params=pltpu.CompilerParams(has_side_effects=pltpu.SideEffectType.DATAFLOW_SIDE_EFFECTING),
    )(*parts, *lands, send_sems, recv_sems, after)
    return list(outs[:n]), list(outs[n:])


def _late_gather_start(shards):
    n = len(shards)

    def body(*refs):
        w_refs, l_refs = refs[:n], refs[n:2 * n]
        send_sems, recv_sems = refs[2 * n], refs[2 * n + 1]
        token = refs[-1]
        x, y, c = _mesh_pos()
        me = 2 * x + y
        for t in range(n):
            for k, (px, py) in enumerate(_other_chips(x, y)):
                pltpu.make_async_remote_copy(src_ref=w_refs[t], dst_ref=l_refs[t].at[me],
                                             send_sem=send_sems.at[3 * t + k], recv_sem=recv_sems.at[3 * t + k],
                                             device_id=(px, py, c), device_id_type=MESH).start()
        token[...] = jnp.zeros_like(token)

    src = [pltpu.HBM(s.shape, s.dtype) for s in shards]
    land = [pltpu.HBM((4,) + s.shape, s.dtype) for s in shards]
    outs = pl.pallas_call(
        body,
        name="late_gather_start",
        out_shape=(pltpu.SemaphoreType.DMA((3 * n,)), pltpu.SemaphoreType.DMA((3 * n,)), *src, *land,
                   jax.ShapeDtypeStruct((8, LANES), F32)),
        in_specs=[_HBM] * (2 * n),
        out_specs=(_SEM, _SEM, *([_HBM] * (2 * n)), pl.BlockSpec(memory_space=pltpu.VMEM)),
        input_output_aliases={i: 2 + i for i in range(2 * n)},
        compiler_params=pltpu.CompilerParams(has_side_effects=pltpu.SideEffectType.DATAFLOW_SIDE_EFFECTING),
    )(*[_in_hbm(s) for s in shards], *[_in_hbm(lax.empty((4,) + s.shape, s.dtype)) for s in shards])
    return outs[0], outs[1], list(outs[2:2 + n]), list(outs[2 + n:2 + 2 * n]), outs[-1]


def _late_gather_wait(send_sems, recv_sems, shards, lands, after):
    n = len(shards)

    def body(*refs):
        w_refs, l_refs = refs[:n], refs[n:2 * n]
        send_sems, recv_sems = refs[2 * n], refs[2 * n + 1]
        x, y, c = _mesh_pos()
        for t in range(n):
            for k, (px, py) in enumerate(_other_chips(x, y)):
                cp = pltpu.make_async_remote_copy(src_ref=w_refs[t], dst_ref=l_refs[t].at[2 * px + py],
                                                  send_sem=send_sems.at[3 * t + k], recv_sem=recv_sems.at[3 * t + k],
                                                  device_id=(x, y, c), device_id_type=MESH)
                cp.wait_send()
                cp.wait_recv()

    src = [pltpu.HBM(s.shape, s.dtype) for s in shards]
    land = [pltpu.HBM(l.shape, l.dtype) for l in lands]
    outs = pl.pallas_call(
        body,
        name="late_gather_wait",
        out_shape=(*src, *land),
        in_specs=[_HBM] * (2 * n) + [_SEM, _SEM, _ANY],
        out_specs=[_HBM] * (2 * n),
        input_output_aliases={i: i for i in range(2 * n)},
        compiler_params=pltpu.CompilerParams(has_side_effects=pltpu.SideEffectType.DATAFLOW_SIDE_EFFECTING),
    )(*shards, *lands, send_sems, recv_sems, after)
    return list(outs[n:])


def _all_to_all_small(parts):
    n = len(parts)

    def body(*refs):
        p_refs, o_refs = refs[:n], refs[n:2 * n]
        send_sems, recv_sems = refs[2 * n:]
        x, y, c = _mesh_pos()
        me = 4 * x + 2 * y + c
        sends = []
        for t in range(n):
            for k in range(1, 8):
                px, py, pc = x ^ (k >> 2), y ^ ((k >> 1) & 1), c ^ (k & 1)
                cp = pltpu.make_async_remote_copy(src_ref=p_refs[t], dst_ref=o_refs[t].at[me],
                                                  send_sem=send_sems.at[7 * t + k - 1], recv_sem=recv_sems.at[7 * t + k - 1],
                                                  device_id=(px, py, pc), device_id_type=MESH)
                cp.start()
                sends.append(cp)
        for t in range(n):
            for k in range(1, 8):
                peer = 4 * (x ^ (k >> 2)) + 2 * (y ^ ((k >> 1) & 1)) + (c ^ (k & 1))
                pltpu.make_async_remote_copy(src_ref=p_refs[t], dst_ref=o_refs[t].at[peer],
                                             send_sem=send_sems.at[7 * t + k - 1], recv_sem=recv_sems.at[7 * t + k - 1],
                                             device_id=(x, y, c), device_id_type=MESH).wait_recv()
        for cp in sends:
            cp.wait_send()

    return pl.pallas_call(
        body,
        out_shape=[jax.ShapeDtypeStruct((8,) + p.shape, p.dtype) for p in parts],
        in_specs=[_ANY] * n,
        out_specs=[_ANY] * n,
        scratch_shapes=[pltpu.SemaphoreType.DMA((7 * n,)), pltpu.SemaphoreType.DMA((7 * n,))],
        name="grad_small_all_to_all",
    )(*parts)


def _sum_devices_small(landed, own):
    n = len(landed)

    def body(*refs):
        x, y, c = _mesh_pos()
        me = 4 * x + 2 * y + c
        for t in range(n):
            acc = jnp.where(me == 0, refs[n + t][...], refs[t][0])
            for d in range(1, 8):
                acc = acc + jnp.where(me == d, refs[n + t][...], refs[t][d])
            refs[2 * n + t][...] = acc

    return pl.pallas_call(
        body,
        out_shape=[jax.ShapeDtypeStruct(p.shape, F32) for p in own],
        in_specs=[_VMEM_WHOLE] * (2 * n),
        out_specs=[_VMEM_WHOLE] * n,
        compiler_params=_params_whole(list(landed) + 2 * list(own)),
        name="grad_sum_devices_small",
    )(*landed, *own)


def _pair_join(fs):
    n = len(fs)

    def body(*refs):
        f_refs, o_refs = refs[:n], refs[n:2 * n]
        send_sems, recv_sems = refs[2 * n:]
        x, y, c = _mesh_pos()
        sends = []
        for t in range(n):
            cp = pltpu.make_async_remote_copy(src_ref=f_refs[t].at[c], dst_ref=o_refs[t].at[c], send_sem=send_sems.at[t],
                                              recv_sem=recv_sems.at[t], device_id=(x, y, 1 - c), device_id_type=MESH)
            cp.start()
            sends.append(cp)
        for t in range(n):
            pltpu.make_async_remote_copy(src_ref=f_refs[t].at[c], dst_ref=o_refs[t].at[1 - c], send_sem=send_sems.at[t],
                                         recv_sem=recv_sems.at[t], device_id=(x, y, c), device_id_type=MESH).wait_recv()
        for cp in sends:
            cp.wait_send()

    return pl.pallas_call(
        body,
        out_shape=[jax.ShapeDtypeStruct(f.shape, f.dtype) for f in fs],
        in_specs=[_ANY] * n,
        out_specs=[_ANY] * n,
        input_output_aliases={t: t for t in range(n)},
        scratch_shapes=[pltpu.SemaphoreType.DMA((n,)), pltpu.SemaphoreType.DMA((n,))],
        name="grad_pair_join",
    )(*fs)


def _rope_tables(Lp):
    inv = 1.0 / (ROPE_BASE ** (jnp.arange(0, ROPE, 2, dtype=F32) / ROPE))
    ang = (jnp.arange(Lp, dtype=F32) - FRONT)[:, None] * inv[None, :]
    cs, sn = jnp.cos(ang), jnp.sin(ang)
    return jnp.tile(cs, (1, 4)), jnp.concatenate([-sn, sn, -sn, sn], axis=1)


def _local_step(x, loss_target, meta, norm_g, w_in, gate_w, gate_b, gla_norm_g, gla_proj, q_norm_g, w_uq,
                kv_norm_g, w_ukv, mla_proj, w_out, final_norm_g, early_grads_hook=None, late_weights_hook=None):
    B, seq, _ = x.shape
    Lp = HEAD_ROWS + seq
    T = B * Lp
    tr = _div_tile(Lp, 544, 16)
    tq = _div_tile(T, 1024, QB)
    tkw = _div_tile(T, Lp, QB)

    cuts = np.cumsum((0,) + SPLITS)
    shard_w = IN_WIDTH // 4

    def w_cols(i, width=None):
        parts = []
        for j in range(4):
            a, b = max(cuts[i], j * shard_w), min(cuts[i + 1], (j + 1) * shard_w)
            if a < b:
                parts.append(w_in[j][:, a - j * shard_w:b - j * shard_w])
        if width is not None:
            parts.append(jnp.zeros((D, width - (cuts[i + 1] - cuts[i])), w_in.dtype))
        return parts

    i_q, i_k, i_v, i_lr, i_z, i_cq, i_ckv, i_kr, i_mz, i_gg, i_gm = range(11)
    wA = jnp.concatenate(sum([w_cols(i) for i in (i_v, i_z, i_mz, i_gg, i_gm, i_q, i_k)], []), axis=1)
    wB = jnp.concatenate(w_cols(i_cq) + w_cols(i_ckv) + w_cols(i_lr, 128) + w_cols(i_kr, 128), axis=1)
    gn4 = jnp.tile(gla_norm_g, (1, GLA_H))
    cos_t, sin_t = _rope_tables(Lp)

    u = _rms_in(x, meta, norm_g, B, Lp)
    projA = _mm(u, wA, name="in_proj_a", out_dtype=BF16, tm=tq, tn=1024, tk=D)
    projB = _mm(u, wB, name="in_proj_b", out_dtype=BF16, tm=tq, tn=640, tk=D)
    if late_weights_hook is not None:
        gate_w, gla_proj, w_uq, w_ukv, mla_proj, w_out = late_weights_hook(projB)
    wg = jnp.pad(gate_w, ((0, 128 - GLA_RANK), (0, 0)))
    wuq2 = jnp.pad(w_uq.reshape(Q_RANK, MLA_H, MLA_QK), ((0, 0), (0, 0), (0, 256 - MLA_QK))).reshape(Q_RANK, 2048)
    oa, ya_in, ssave = _gla_fwd(projA, projB, wg, gate_b, gn4, B, Lp)
    ya = _mm(ya_in, gla_proj, name="gla_proj", out_dtype=BF16, tm=tq, tn=D, tk=D)
    q_att, k_att, v_att, cqn, ckvn = _mla_prep(projB, cos_t, sin_t, q_norm_g, kv_norm_g, wuq2, w_ukv, B, Lp, tr)
    ob, yb_in, lse_c = _attn_fwd(q_att, k_att, v_att, projA, B, Lp)
    yb = _mm(yb_in, mla_proj, name="mla_proj", out_dtype=BF16, tm=tq, tn=D, tk=D)
    dh1, dh1_b, merged, loss, d_gf = _out_proj_loss(x, meta, projA, ya, yb, w_out, final_norm_g.reshape(1, D),
                                                     loss_target, B, Lp)

    g_w_out = _mm(merged, dh1_b, name="dw_out", trans_a=True, tm=D, tn=D, tk=tkw)
    dya, dyb, dA = _merge_bwd(dh1_b, w_out, projA, ya, yb, tr)
    g_gla_proj = _mm(ya_in, dya, name="dw_gla_proj", trans_a=True, tm=D, tn=D, tk=tkw)
    g_mla_proj = _mm(yb_in, dyb, name="dw_mla_proj", trans_a=True, tm=D, tn=D, tk=tkw)
    doa, dBz, d_gn = _gla_out_bwd(dya, gla_proj, oa, projA, gn4, tr)
    dC, g_wg, d_bg = _gla_bwd(projA, projB, ssave, doa, wg, gate_b, B, Lp)
    do, dDz, delta_c = _attn_bwd_pre(dyb, mla_proj, projA, ob, B, Lp)
    dq, dk, dv = _attn_bwd(q_att, k_att, v_att, do, lse_c, delta_c, B, Lp)
    dqf, dkvf, dE, d_gq, d_gkv = _mla_bwd_post(dq, dk, dv, projB, cos_t, sin_t, q_norm_g, kv_norm_g,
                                                wuq2, w_ukv, B, Lp, tr)
    g_wuq2 = _mm(cqn, dqf, name="dw_uq", trans_a=True, tm=Q_RANK, tn=2048, tk=tkw)
    g_wukv = _mm(ckvn, dkvf, name="dw_ukv", trans_a=True, tm=KV_RANK, tn=2048, tk=tkw)
    dparts = [dA, dBz, dC, dDz, dE]
    g_in = [_mm(u, dp, name="dw_in_%d" % i, trans_a=True, tm=D, tn=_div_tile(dp.shape[1], 1024, 256), tk=tkw)
            for i, dp in enumerate(dparts)]

    gA, gBz, gC, gDz, gE = g_in
    src = [(gC, 1024), (gC, 1536), (gC, 0), (gC, 2048), (gBz, 0), (gE, 0), (gE, Q_RANK), (gE, 384), (gDz, 0),
           (gA, 0), (gA, D)]
    owners = []
    for j in range(4):
        parts = []
        for i, (arr, off) in enumerate(src):
            a, b = max(cuts[i], j * shard_w), min(cuts[i + 1], (j + 1) * shard_w)
            if a < b:
                parts.append(arr[:, off + a - cuts[i]:off + b - cuts[i]])
        owners.append(jnp.concatenate(parts, axis=1))
    g_w_in = jnp.stack(owners)
    g_wuq = g_wuq2.reshape(Q_RANK, MLA_H, 256)[:, :, :MLA_QK].reshape(Q_RANK, MLA_H * MLA_QK)
    grads = dict(w_in=g_w_in, gla_gate_w=g_wg[:GLA_RANK], gla_proj=g_gla_proj, mla_w_uq=g_wuq, mla_w_ukv=g_wukv,
                 mla_proj=g_mla_proj, w_out=g_w_out, gla_gate_b=d_bg,
                 gla_norm_g=d_gn, mla_q_norm_g=d_gq, mla_kv_norm_g=d_gkv, final_norm_g=d_gf)
    token = None if early_grads_hook is None else early_grads_hook(grads)
    ng = norm_g if token is None else norm_g + token[0:1, 0:1]
    grad_x, d_meta, d_ng = _in_proj_bwd(x, meta, dh1, dA, dBz, dC, dDz, dE, wA, wB, ng, B, Lp)
    grads.update(meta_tokens=d_meta, norm_g=d_ng)
    return loss[0, 0], grad_x, grads


_MATS = ("w_in", "gla_gate_w", "gla_proj", "mla_w_uq", "mla_w_ukv", "mla_proj", "w_out")
_ROW_SHARDED = ("gla_proj", "mla_proj", "w_out")
_ORDER = ("meta_tokens", "norm_g", "w_in", "gla_gate_w", "gla_gate_b", "gla_norm_g", "gla_proj", "mla_q_norm_g",
          "mla_w_uq", "mla_kv_norm_g", "mla_w_ukv", "mla_proj", "w_out", "final_norm_g")
WIRE_BF16_MIN_ELEMS = 128 * 128
SMALL_PACK_ROWS = 16


def _pack_small(d):
    rows = [jnp.pad(d[n].reshape(1, size), ((0, 0), (0, D - size))) for n, size in SMALL]
    return jnp.pad(jnp.concatenate(rows, axis=0), ((0, SMALL_PACK_ROWS - len(rows)), (0, 0)))


def _unpack_small(packed):
    return {n: packed[i, :size] for i, (n, size) in enumerate(SMALL)}


def kernel(x, meta_tokens, norm_g, w_in, gla_gate_w, gla_gate_b, gla_norm_g, gla_proj, mla_q_norm_g, mla_w_uq, mla_kv_norm_g, mla_w_ukv, mla_proj, w_out, final_norm_g, loss_target, m_meta_tokens, m_norm_g, m_w_in, m_gla_gate_w, m_gla_gate_b, m_gla_norm_g, m_gla_proj, m_mla_q_norm_g, m_mla_w_uq, m_mla_kv_norm_g, m_mla_w_ukv, m_mla_proj, m_w_out, m_final_norm_g, v_meta_tokens, v_norm_g, v_w_in, v_gla_gate_w, v_gla_gate_b, v_gla_norm_g, v_gla_proj, v_mla_q_norm_g, v_mla_w_uq, v_mla_kv_norm_g, v_mla_w_ukv, v_mla_proj, v_w_out, v_final_norm_g):
    w = dict(meta_tokens=meta_tokens, norm_g=norm_g, w_in=w_in[0], gla_gate_w=gla_gate_w[0], gla_gate_b=gla_gate_b,
             gla_norm_g=gla_norm_g, gla_proj=gla_proj[0], mla_q_norm_g=mla_q_norm_g, mla_w_uq=mla_w_uq[0],
             mla_kv_norm_g=mla_kv_norm_g, mla_w_ukv=mla_w_ukv[0], mla_proj=mla_proj[0], w_out=w_out[0],
             final_norm_g=final_norm_g)
    mom = dict(meta_tokens=m_meta_tokens, norm_g=m_norm_g, w_in=m_w_in[0], gla_gate_w=m_gla_gate_w[0],
               gla_gate_b=m_gla_gate_b, gla_norm_g=m_gla_norm_g, gla_proj=m_gla_proj[0], mla_q_norm_g=m_mla_q_norm_g,
               mla_w_uq=m_mla_w_uq[0], mla_kv_norm_g=m_mla_kv_norm_g, mla_w_ukv=m_mla_w_ukv[0], mla_proj=m_mla_proj[0],
               w_out=m_w_out[0], final_norm_g=m_final_norm_g)
    var = dict(meta_tokens=v_meta_tokens, norm_g=v_norm_g, w_in=v_w_in[0], gla_gate_w=v_gla_gate_w[0],
               gla_gate_b=v_gla_gate_b, gla_norm_g=v_gla_norm_g, gla_proj=v_gla_proj[0], mla_q_norm_g=v_mla_q_norm_g,
               mla_w_uq=v_mla_w_uq[0], mla_kv_norm_g=v_mla_kv_norm_g, mla_w_ukv=v_mla_w_ukv[0], mla_proj=v_mla_proj[0],
               w_out=v_w_out[0], final_norm_g=v_final_norm_g)
    out_shapes = {n: a.shape for n, a in zip(_ORDER, (meta_tokens, norm_g, w_in, gla_gate_w, gla_gate_b, gla_norm_g,
                                                     gla_proj, mla_q_norm_g, mla_w_uq, mla_kv_norm_g, mla_w_ukv,
                                                     mla_proj, w_out, final_norm_g))}

    me = (2 * lax.axis_index("x") + lax.axis_index("y")).astype(jnp.int32)
    is_mine = lax.broadcasted_iota(jnp.int32, (4, 1, 1), 0) == me
    with_own = lambda gth, own: jnp.where(is_mine, own[None], gth)
    first = [w["w_in"].astype(BF16), meta_tokens]
    w_in_owner, meta_owner = [with_own(gth, own) for gth, own in zip(_weight_gather(first), first)]
    meta_full = meta_owner.transpose(1, 0, 2).reshape(N_META, D)
    late_names = _MATS[1:]
    late = [w[n].astype(BF16) for n in late_names]
    gather_sems = _late_gather_start(late)

    def late_weights(after):
        lands = _late_gather_wait(gather_sems[0], gather_sems[1], gather_sems[2], gather_sems[3], after)
        full = []
        for name, land, own in zip(late_names, lands, late):
            gth = with_own(land, own)
            if name in _ROW_SHARDED:
                full.append(gth.reshape(4 * gth.shape[1], gth.shape[2]))
            else:
                full.append(gth.transpose(1, 0, 2).reshape(gth.shape[1], 4 * gth.shape[2]))
        return full

    def by_owner(name, arr):
        if name == "w_in":
            return arr
        if name in _ROW_SHARDED:
            return arr.reshape(4, arr.shape[0] // 4, arr.shape[1])
        return arr.reshape(arr.shape[0], 4, arr.shape[1] // 4).transpose(1, 0, 2)

    c_idx = lax.axis_index("c").astype(jnp.int32).reshape(1)
    pos = jnp.stack([c_idx[0], me])
    in_flight = {}

    def start_matrix_reduce(early):
        gps = [by_owner(n, early[n]) for n in _MATS]
        recvs = _pair_swap(gps)
        s1 = [_pair_add_big(gps[0], recvs[0], c_idx)] + list(_pair_add_small(gps[1:], recvs[1:]))
        send_sems, recv_sems, parts, lands, token = _chip_scatter_start(s1)
        in_flight.update(send_sems=send_sems, recv_sems=recv_sems, parts=parts, lands=lands)
        return token

    norm_g_after_start = norm_g + gather_sems[4][0:1, 0:1]
    loss_local, grad_x, g = _local_step(
        x, loss_target, meta_full, norm_g_after_start, w_in_owner, None, gla_gate_b, gla_norm_g, None,
        mla_q_norm_g, None, mla_kv_norm_g, None, None, None, final_norm_g,
        early_grads_hook=start_matrix_reduce, late_weights_hook=late_weights)
    loss = lax.psum(loss_local, ("x", "y", "c"))

    s1, landed = _chip_scatter_wait(in_flight["send_sems"], in_flight["recv_sems"], in_flight["parts"],
                                    in_flight["lands"], after=g["norm_g"])
    halves = [_sum_chips_big(landed[0], s1[0], pos)] + list(_sum_chips_small(landed[1:], s1[1:]))
    g_mats = [j.reshape(out_shapes[n]) for j, n in zip(_pair_join(halves), _MATS)]

    late = [g["meta_tokens"], _pack_small(g)]
    meta_sum, small_sum = _sum_devices_small(_all_to_all_small(late), late)
    g_meta = lax.dynamic_slice(meta_sum, (0, me * (D // 4)), (N_META, D // 4))
    names = _MATS + ("meta_tokens",)
    g_red = g_mats + [g_meta, small_sum]

    tens = lambda d: [d[n].reshape(out_shapes[n]) for n in names] + [_pack_small(d)]
    w_t, m_t, v_t = tens(w), tens(mom), tens(var)
    big = _adamw_big(w_t[0], g_red[0], m_t[0], v_t[0])
    rest = _adamw_small(w_t[1:], g_red[1:], m_t[1:], v_t[1:])
    k = len(names)
    results = {"grad": g_red}
    for i, kind in enumerate(("delta", "new_m", "new_v")):
        results[kind] = [big[i]] + list(rest[i * k:(i + 1) * k])

    outs = []
    for kind in ("grad", "delta", "new_m", "new_v"):
        vals = dict(zip(names, results[kind][:-1]))
        vals.update(_unpack_small(results[kind][-1]))
        outs += [vals[n].reshape(out_shapes[n]) for n in _ORDER]
    return (loss, grad_x, *outs)
```

```python
import functools
import math

import jax
import jax.numpy as jnp
import numpy as np
from jax import lax
from jax.experimental import pallas as pl
from jax.experimental.pallas import tpu as pltpu

F32 = jnp.float32
BF16 = jnp.bfloat16

D = 1024
N_META = 16
QB = 256
FRONT = QB - N_META
HEAD_ROWS = FRONT + N_META
assert FRONT % 64 == 48
EPS = 1e-6

GLA_H, GLA_DK, GLA_DV, GLA_RANK, GLA_C = 4, 128, 256, 16, 64
GLA_NORMALIZER = 16.0
GLA_KW, GLA_VW = GLA_H * GLA_DK, GLA_H * GLA_DV
MLA_H, NOPE, ROPE, MLA_DV, Q_RANK, KV_RANK = 8, 128, 64, 128, 256, 128
MLA_QK = NOPE + ROPE
ROPE_BASE = 10000.0
SPLITS = (GLA_KW, GLA_KW, GLA_VW, GLA_RANK, GLA_VW, Q_RANK, KV_RANK, ROPE, MLA_H * MLA_DV, D, D)
IN_WIDTH = sum(SPLITS)

ADAM_LR, ADAM_B1, ADAM_B2, ADAM_EPS, ADAM_WD, ADAM_STEP = 0.001, 0.9, 0.999, 1e-08, 0.01, 10

LANES = 128
VMEM_CAP_V7X = 56 * 1024 * 1024
MESH = pl.DeviceIdType.MESH
NEG = -1e30
LOG2E = math.log2(math.e)

SMALL = (("norm_g", D), ("gla_gate_b", GLA_KW), ("gla_norm_g", GLA_DV), ("mla_q_norm_g", Q_RANK),
         ("mla_kv_norm_g", KV_RANK), ("final_norm_g", D))


def _div_tile(n, target, mult):
    best = None
    for d in range(mult, min(n, target) + 1, mult):
        if n % d == 0:
            best = d
    assert best is not None, (n, target, mult)
    return best


def _params(sem, block_bytes, scratch_bytes=0):
    est = 2 * block_bytes + scratch_bytes + 12 * 1024 * 1024
    return pltpu.CompilerParams(dimension_semantics=sem, vmem_limit_bytes=int(min(max(est, 24 * 1024 * 1024), VMEM_CAP_V7X)))


def _nbytes(shape, dtype):
    return int(np.prod(shape)) * jnp.dtype(dtype).itemsize


def _sigmoid(x):
    return 1.0 / (1.0 + jnp.exp(-x))


def _nt(a, b):
    return lax.dot_general(a, b, (((1,), (1,)), ((), ())), preferred_element_type=F32)


def _tn(a, b):
    return lax.dot_general(a, b, (((0,), (0,)), ((), ())), preferred_element_type=F32)


def _nn(a, b):
    return jnp.dot(a, b, preferred_element_type=F32)


def _split3(x):
    a = x.astype(BF16)
    r = x - a.astype(F32)
    b = r.astype(BF16)
    c = (r - b.astype(F32)).astype(BF16)
    return a, b, c


def _mm(a, b, *, name, trans_a=False, trans_b=False, out_dtype=F32, tm, tn, tk):
    assert not (trans_a and trans_b)
    if trans_a:
        K, M = a.shape
    else:
        M, K = a.shape
    N = b.shape[0] if trans_b else b.shape[1]
    assert (b.shape[1] if trans_b else b.shape[0]) == K
    assert M % tm == 0 and N % tn == 0 and K % tk == 0, (name, M, N, K, tm, tn, tk)
    nk = K // tk

    def body(a_ref, b_ref, o_ref, *scratch):
        av = a_ref[...].astype(BF16)
        bv = b_ref[...].astype(BF16)
        prod = _tn(av, bv) if trans_a else (_nt(av, bv) if trans_b else _nn(av, bv))
        if nk == 1:
            o_ref[...] = prod.astype(out_dtype)
        else:
            acc = scratch[0]
            k = pl.program_id(2)

            @pl.when(k == 0)
            def _():
                acc[...] = prod

            @pl.when(k > 0)
            def _():
                acc[...] += prod

            @pl.when(k == nk - 1)
            def _():
                o_ref[...] = acc[...].astype(out_dtype)

    if trans_a:
        a_spec = pl.BlockSpec((tk, tm), lambda i, j, k: (k, i))
    else:
        a_spec = pl.BlockSpec((tm, tk), lambda i, j, k: (i, k))
    if trans_b:
        b_spec = pl.BlockSpec((tn, tk), lambda i, j, k: (j, k))
    else:
        b_spec = pl.BlockSpec((tk, tn), lambda i, j, k: (k, j))
    blocks = (_nbytes((tm, tk), a.dtype) + _nbytes((tk, tn), b.dtype) + _nbytes((tm, tn), out_dtype))
    scratch = [pltpu.VMEM((tm, tn), F32)] if nk > 1 else []
    return pl.pallas_call(
        body,
        out_shape=jax.ShapeDtypeStruct((M, N), out_dtype),
        grid=(M // tm, N // tn, nk),
        in_specs=[a_spec, b_spec],
        out_specs=pl.BlockSpec((tm, tn), lambda i, j, k: (i, j)),
        scratch_shapes=scratch,
        compiler_params=_params(("parallel", "parallel", "arbitrary"), blocks + _nbytes((tm, tn), F32),
                                _nbytes((tm, tn), F32) if nk > 1 else 0),
        name=name,
    )(a, b)


def _h_tile(j, x_ref, meta_ref):
    head = jnp.concatenate([jnp.zeros((FRONT, D), F32), meta_ref[...]], axis=0)
    return jnp.where(j > 0, x_ref[0], head)


def _x_spec():
    return pl.BlockSpec((1, QB, D), lambda b, j: (b, jnp.maximum(j - 1, 0), 0))


def _rms_in(x, meta, g, B, Lp):
    T = B * Lp
    NQ = Lp // QB

    def body(x_ref, meta_ref, g_ref, u_ref):
        h = _h_tile(pl.program_id(1), x_ref, meta_ref)
        r = lax.rsqrt(jnp.mean(h * h, axis=-1, keepdims=True) + EPS)
        u_ref[...] = (h * r * g_ref[...]).astype(BF16)

    return pl.pallas_call(
        body,
        out_shape=jax.ShapeDtypeStruct((T, D), BF16),
        grid=(B, NQ),
        in_specs=[_x_spec(), pl.BlockSpec((N_META, D), lambda b, j: (0, 0)), pl.BlockSpec((1, D), lambda b, j: (0, 0))],
        out_specs=pl.BlockSpec((QB, D), lambda b, j: (b * NQ + j, 0)),
        compiler_params=_params(("parallel", "parallel"), _nbytes((QB, D), F32) * 2),
        name="rms_in",
    )(x, meta, g)


def _gla_gate(lr, wg, bg, valid):
    pre = _nn(lr.astype(BF16), wg) + bg
    logsig = jnp.minimum(pre, 0.0) - jnp.log(1.0 + jnp.exp(-jnp.abs(pre)))
    return pre, jnp.where(valid, logsig / GLA_NORMALIZER, 0.0)


def _tri_masks():
    ri = lax.broadcasted_iota(jnp.int32, (GLA_C, GLA_C), 0)
    ci = lax.broadcasted_iota(jnp.int32, (GLA_C, GLA_C), 1)
    return ci <= ri, ci >= ri


def _cumsum_rows(x, ones_mask):
    w = jnp.where(ones_mask, 1.0, 0.0).astype(BF16)
    a, b, c = _split3(x)
    return _nn(w, a) + _nn(w, b) + _nn(w, c)


def _gla_fwd(projA, projB, wg, bg, gn4, B, Lp):
    T = B * Lp
    NC = Lp // GLA_C
    C = GLA_C
    scale = GLA_DK ** -0.5

    def body(q_ref, k_ref, v_ref, lr_ref, z_ref, wg_ref, bg_ref, gn_ref, oa_ref, ya_ref, ssave_ref, st_ref):
        n = pl.program_id(0)

        @pl.when(n == 0)
        def _():
            st_ref[...] = jnp.zeros_like(st_ref)

        pos = n * C + lax.broadcasted_iota(jnp.int32, (C, 1), 0)
        lower, _ = _tri_masks()
        is_last = lax.broadcasted_iota(jnp.int32, (C, 1), 0) == C - 1
        for b in range(B):
            ssave_ref[b, 0] = st_ref[b]
            _, glog = _gla_gate(lr_ref[b], wg_ref[...], bg_ref[...], pos >= FRONT)
            bcum = _cumsum_rows(glog, lower)
            for h in range(GLA_H):
                ks = slice(h * GLA_DK, (h + 1) * GLA_DK)
                vs = slice(h * GLA_DV, (h + 1) * GLA_DV)
                bh = bcum[:, ks]
                blast = jnp.sum(jnp.where(is_last, bh, 0.0), axis=0, keepdims=True)
                qh = q_ref[b, :, ks].astype(F32) * scale
                kh = k_ref[b, :, ks].astype(F32)
                qe = (qh * jnp.exp(bh)).astype(BF16)
                ke = (kh * jnp.exp(-bh)).astype(BF16)
                kl = (kh * jnp.exp(blast - bh)).astype(BF16)
                vh = v_ref[b, :, vs].astype(BF16)
                a = jnp.where(lower, _nt(qe, ke), 0.0).astype(BF16)
                st = st_ref[b, h]
                o = _nn(a, vh) + _nt(qe, st.astype(BF16))
                st_ref[b, h] = st * jnp.exp(blast) + _tn(vh, kl)
                oa_ref[b, :, vs] = o.astype(BF16)
                on = o * lax.rsqrt(jnp.mean(o * o, axis=-1, keepdims=True) + EPS) * gn_ref[:, vs]
                z = z_ref[b, :, vs].astype(F32)
                ya_ref[b, :, vs] = (on * (z * _sigmoid(z))).astype(BF16)

    blocks = B * (_nbytes((C, 512), F32) * 2 + _nbytes((C, 1024), F32) * 3 + _nbytes((C, 1024), BF16)
                  + _nbytes((GLA_H, GLA_DV, GLA_DK), F32)) + _nbytes((128, 512), BF16)
    state = _nbytes((B, GLA_H, GLA_DV, GLA_DK), F32)
    pa = projA.reshape(B, Lp, projA.shape[1])
    oa, ya, ssave = pl.pallas_call(
        body,
        out_shape=(jax.ShapeDtypeStruct((B, Lp, GLA_VW), BF16), jax.ShapeDtypeStruct((B, Lp, GLA_VW), BF16),
                   jax.ShapeDtypeStruct((B, NC, GLA_H, GLA_DV, GLA_DK), F32)),
        grid=(NC,),
        in_specs=[
            pl.BlockSpec((B, C, 512), lambda n: (0, n, 10)),
            pl.BlockSpec((B, C, 512), lambda n: (0, n, 11)),
            pl.BlockSpec((B, C, 1024), lambda n: (0, n, 0)),
            pl.BlockSpec((B, C, 128), lambda n: (0, n, 3)),
            pl.BlockSpec((B, C, 1024), lambda n: (0, n, 1)),
            pl.BlockSpec((128, 512), lambda n: (0, 0)),
            pl.BlockSpec((1, 512), lambda n: (0, 0)),
            pl.BlockSpec((1, 1024), lambda n: (0, 0)),
        ],
        out_specs=(pl.BlockSpec((B, C, 1024), lambda n: (0, n, 0)),
                   pl.BlockSpec((B, C, 1024), lambda n: (0, n, 0)),
                   pl.BlockSpec((B, 1, GLA_H, GLA_DV, GLA_DK), lambda n: (0, n, 0, 0, 0))),
        scratch_shapes=[pltpu.VMEM((B, GLA_H, GLA_DV, GLA_DK), F32)],
        compiler_params=_params(("arbitrary",), blocks, state),
        name="gla_fwd",
    )(pa, pa, pa, projB.reshape(B, Lp, projB.shape[1]), pa, wg, bg, gn4)
    return oa.reshape(T, GLA_VW), ya.reshape(T, GLA_VW), ssave


def _swap_halves(x):
    lane = lax.broadcasted_iota(jnp.int32, x.shape, 1)
    return jnp.where((lane % 64) < 32, pltpu.roll(x, 96, 1), pltpu.roll(x, 32, 1))


def _mla_prep(projB, cos_t, sin_t, gq, gkv, wuq2, wukv, B, Lp, tr):
    T = B * Lp
    nt = Lp // tr
    HW = 2 * LANES

    def body(pb_ref, cos_ref, sin_ref, gq_ref, gkv_ref, wuq_ref, wukv_ref, q_ref, k_ref, v_ref, cqn_ref, ckvn_ref):
        cq = pb_ref[:, 0:Q_RANK].astype(F32)
        ckv = pb_ref[:, Q_RANK:Q_RANK + KV_RANK].astype(F32)
        kr = pb_ref[:, 512:640].astype(F32)
        cqn = (cq * lax.rsqrt(jnp.mean(cq * cq, axis=-1, keepdims=True) + EPS) * gq_ref[...]).astype(BF16)
        ckvn = (ckv * lax.rsqrt(jnp.mean(ckv * ckv, axis=-1, keepdims=True) + EPS) * gkv_ref[...]).astype(BF16)
        cqn_ref[...] = cqn
        ckvn_ref[...] = ckvn
        qf = _nn(cqn, wuq_ref[...])
        kvf = _nn(ckvn, wukv_ref[...])
        cs = cos_ref[...]
        sn = sin_ref[...]
        rope = lambda t: t * cs + _swap_halves(t) * sn
        kr_r = rope(kr).astype(BF16)
        for h in range(MLA_H):
            q_ref[:, h * HW:h * HW + LANES] = qf[:, h * HW:h * HW + LANES].astype(BF16)
            q_ref[:, h * HW + LANES:(h + 1) * HW] = rope(qf[:, h * HW + LANES:(h + 1) * HW]).astype(BF16)
            k_ref[:, h * HW:h * HW + LANES] = kvf[:, h * HW:h * HW + LANES].astype(BF16)
            k_ref[:, h * HW + LANES:(h + 1) * HW] = kr_r
            v_ref[:, h * MLA_DV:(h + 1) * MLA_DV] = kvf[:, h * HW + LANES:(h + 1) * HW].astype(BF16)

    blocks = (_nbytes((tr, 640), F32) + 2 * _nbytes((tr, 128), F32) + _nbytes((Q_RANK, 2048), BF16)
              + _nbytes((KV_RANK, 2048), BF16) + _nbytes((tr, 2048 * 2 + 1024 + 384), BF16)
              + 2 * _nbytes((tr, 2048), F32))
    return pl.pallas_call(
        body,
        out_shape=(jax.ShapeDtypeStruct((T, MLA_H * HW), BF16), jax.ShapeDtypeStruct((T, MLA_H * HW), BF16),
                   jax.ShapeDtypeStruct((T, MLA_H * MLA_DV), BF16), jax.ShapeDtypeStruct((T, Q_RANK), BF16),
                   jax.ShapeDtypeStruct((T, KV_RANK), BF16)),
        grid=(B, nt),
        in_specs=[
            pl.BlockSpec((tr, 640), lambda b, j: (b * nt + j, 0)),
            pl.BlockSpec((tr, 128), lambda b, j: (j, 0)),
            pl.BlockSpec((tr, 128), lambda b, j: (j, 0)),
            pl.BlockSpec((1, Q_RANK), lambda b, j: (0, 0)),
            pl.BlockSpec((1, KV_RANK), lambda b, j: (0, 0)),
            pl.BlockSpec((Q_RANK, 2048), lambda b, j: (0, 0)),
            pl.BlockSpec((KV_RANK, 2048), lambda b, j: (0, 0)),
        ],
        out_specs=(pl.BlockSpec((tr, 2048), lambda b, j: (b * nt + j, 0)),
                   pl.BlockSpec((tr, 2048), lambda b, j: (b * nt + j, 0)),
                   pl.BlockSpec((tr, 1024), lambda b, j: (b * nt + j, 0)),
                   pl.BlockSpec((tr, Q_RANK), lambda b, j: (b * nt + j, 0)),
                   pl.BlockSpec((tr, KV_RANK), lambda b, j: (b * nt + j, 0))),
        compiler_params=_params(("parallel", "parallel"), blocks),
        name="mla_prep",
    )(projB, cos_t, sin_t, gq, gkv, wuq2, wukv)


def _attn_mask(row, col):
    return (col <= row) & ((col >= FRONT) | (row < FRONT))


def _attn_fwd(q_att, k_att, v_att, projA, B, Lp):
    T = B * Lp
    NQ = Lp // QB
    HW = 2 * LANES
    scale = 1.0 / math.sqrt(MLA_QK)

    def body(q_ref, k_ref, v_ref, mz_ref, o_ref, yb_ref, lsec_ref, m_ref, l_ref, acc_ref):
        qi = pl.program_id(1)
        m_ref[...] = jnp.full(m_ref.shape, NEG, F32)
        l_ref[...] = jnp.zeros_like(l_ref)
        acc_ref[...] = jnp.zeros_like(acc_ref)
        row = qi * QB + lax.broadcasted_iota(jnp.int32, (QB, QB), 0)
        coli = lax.broadcasted_iota(jnp.int32, (QB, QB), 1)

        def step(kj, masked):
            off = pl.multiple_of(kj * QB, QB)
            ok = _attn_mask(row, kj * QB + coli) if masked else None
            for h in range(MLA_H):
                q = q_ref[:, h * HW:(h + 1) * HW]
                kb = k_ref[pl.ds(off, QB), h * HW:(h + 1) * HW]
                vb = v_ref[pl.ds(off, QB), h * MLA_DV:(h + 1) * MLA_DV]
                s = _nt(q, kb) * (scale * LOG2E)
                if masked:
                    s = jnp.where(ok, s, NEG)
                m_old = m_ref[h]
                m_new = jnp.maximum(m_old, jnp.max(s, axis=-1, keepdims=True))
                alpha = jnp.exp2(m_old - m_new)
                p = jnp.exp2(s - jnp.tile(m_new, (1, QB // LANES)))
                m_ref[h] = m_new
                l_ref[h] = alpha * l_ref[h] + jnp.sum(p, axis=-1, keepdims=True)
                acc_ref[h] = alpha * acc_ref[h] + _nn(p.astype(BF16), vb)

        step(0, True)

        def unmasked(kj, carry):
            step(kj, False)
            return carry

        lax.fori_loop(1, qi, unmasked, 0)

        @pl.when(qi > 0)
        def _():
            step(qi, True)

        for h in range(MLA_H):
            hs = slice(h * MLA_DV, (h + 1) * MLA_DV)
            l = l_ref[h]
            o = acc_ref[h] / l
            o_ref[:, hs] = o.astype(BF16)
            z = mz_ref[:, hs].astype(F32)
            yb_ref[:, hs] = (o * (z * _sigmoid(z))).astype(BF16)
            lse2 = m_ref[h] + jnp.log(l) * LOG2E
            lsec_ref[0, h, pl.ds(qi, 1), :] = jnp.transpose(lse2)[0:1, :]

    blocks = (_nbytes((QB, 2048), BF16) + _nbytes((Lp, 2048), BF16) + _nbytes((Lp, 1024), BF16)
              + 2 * _nbytes((QB, 1024), F32) + _nbytes((QB, 1024), BF16) + _nbytes((MLA_H, QB, LANES), F32)
              + _nbytes((MLA_H, NQ, QB), F32))
    return pl.pallas_call(
        body,
        out_shape=(jax.ShapeDtypeStruct((T, MLA_H * MLA_DV), BF16), jax.ShapeDtypeStruct((T, MLA_H * MLA_DV), BF16),
                   jax.ShapeDtypeStruct((B, MLA_H, NQ, QB), F32)),
        grid=(B, NQ),
        in_specs=[
            pl.BlockSpec((QB, MLA_H * HW), lambda b, i: (b * NQ + i, 0)),
            pl.BlockSpec((Lp, MLA_H * HW), lambda b, i: (b, 0)),
            pl.BlockSpec((Lp, MLA_H * MLA_DV), lambda b, i: (b, 0)),
            pl.BlockSpec((QB, 1024), lambda b, i: (b * NQ + i, 2)),
        ],
        out_specs=(pl.BlockSpec((QB, 1024), lambda b, i: (b * NQ + i, 0)),
                   pl.BlockSpec((QB, 1024), lambda b, i: (b * NQ + i, 0)),
                   pl.BlockSpec((1, MLA_H, NQ, QB), lambda b, i: (b, 0, 0, 0))),
        scratch_shapes=[pltpu.VMEM((MLA_H, QB, LANES), F32), pltpu.VMEM((MLA_H, QB, LANES), F32),
                        pltpu.VMEM((MLA_H, QB, MLA_DV), F32)],
        compiler_params=_params(("parallel", "arbitrary"), blocks, 3 * _nbytes((MLA_H, QB, LANES), F32)),
        name="attn_fwd",
    )(q_att, k_att, v_att, projA)


def _out_proj_loss(x, meta, projA, ya, yb, w_out, gf, tgt, B, Lp):
    T = B * Lp
    NQ = Lp // QB

    def body(x_ref, meta_ref, gg_ref, gm_ref, ya_ref, yb_ref, w_ref, gf_ref, t_ref,
             dhb_ref, mg_ref, loss_ref, dgf_ref):
        b = pl.program_id(0)
        j = pl.program_id(1)

        @pl.when((b == 0) & (j == 0))
        def _():
            loss_ref[...] = jnp.zeros_like(loss_ref)
            dgf_ref[...] = jnp.zeros_like(dgf_ref)

        f32 = lambda ref: ref[...].astype(F32)
        merged = (_sigmoid(f32(gg_ref)) * f32(ya_ref) + _sigmoid(f32(gm_ref)) * f32(yb_ref)).astype(BF16)
        mg_ref[...] = merged
        h1 = _h_tile(j, x_ref, meta_ref) + _nn(merged, w_ref[...])
        r = lax.rsqrt(jnp.mean(h1 * h1, axis=-1, keepdims=True) + EPS)
        hn = h1 * r
        gfv = gf_ref[...]
        diff = jnp.where(j > 0, hn * gfv - t_ref[0], 0.0)
        loss_ref[...] += (0.5 / D) * jnp.sum(jnp.sum(diff * diff, axis=-1, keepdims=True), axis=0, keepdims=True)
        dout = diff * (1.0 / D)
        dgf_ref[...] += jnp.sum(dout * hn, axis=0, keepdims=True)
        dhn = dout * gfv
        dh = r * (dhn - hn * jnp.mean(dhn * hn, axis=-1, keepdims=True))
        dhb_ref[...] = dh.astype(BF16)

    rows = lambda c: pl.BlockSpec((QB, D), lambda b, j: (b * NQ + j, c))
    const = lambda s: pl.BlockSpec(s, lambda b, j: (0, 0))
    return pl.pallas_call(
        body,
        out_shape=(jax.ShapeDtypeStruct((T, D), BF16), jax.ShapeDtypeStruct((T, D), BF16),
                   jax.ShapeDtypeStruct((1, 1), F32), jax.ShapeDtypeStruct((1, D), F32)),
        grid=(B, NQ),
        in_specs=[_x_spec(), const((N_META, D)), rows(3), rows(4), rows(0), rows(0), const((D, D)),
                  const((1, D)), _x_spec()],
        out_specs=(rows(0), rows(0), const((1, 1)), const((1, D))),
        compiler_params=_params(("arbitrary", "arbitrary"), 10 * _nbytes((QB, D), F32)),
        name="out_proj_loss",
    )(x, meta, projA, projA, ya, yb, w_out, gf, tgt)


def _merge_bwd(dh1_b, w_out, projA, ya, yb, tr):
    T = dh1_b.shape[0]

    def body(dh_ref, w_ref, gg_ref, gm_ref, ya_ref, yb_ref, dya_ref, dyb_ref, da_ref):
        d = _nt(dh_ref[...], w_ref[...])
        sg = _sigmoid(gg_ref[...].astype(F32))
        sm = _sigmoid(gm_ref[...].astype(F32))
        dya_ref[...] = (d * sg).astype(BF16)
        dyb_ref[...] = (d * sm).astype(BF16)
        da_ref[:, 0:D] = (d * ya_ref[...].astype(F32) * (sg * (1.0 - sg))).astype(BF16)
        da_ref[:, D:2 * D] = (d * yb_ref[...].astype(F32) * (sm * (1.0 - sm))).astype(BF16)

    spec = lambda c: pl.BlockSpec((tr, D), lambda i: (i, c))
    return pl.pallas_call(
        body,
        out_shape=(jax.ShapeDtypeStruct((T, D), BF16), jax.ShapeDtypeStruct((T, D), BF16),
                   jax.ShapeDtypeStruct((T, 2 * D), BF16)),
        grid=(T // tr,),
        in_specs=[spec(0), pl.BlockSpec((D, D), lambda i: (0, 0)), spec(3), spec(4), spec(0), spec(0)],
        out_specs=(spec(0), spec(0), pl.BlockSpec((tr, 2 * D), lambda i: (i, 0))),
        compiler_params=_params(("parallel",), 8 * _nbytes((tr, D), F32)),
        name="merge_bwd",
    )(dh1_b, w_out, projA, projA, ya, yb)


def _gla_out_bwd(dya, gla_proj, oa, projA, gn4, tr):
    T = dya.shape[0]
    nsteps = T // tr

    def body(dya_ref, w_ref, oa_ref, z_ref, gn_ref, do_ref, dz_ref, dgn_ref, acc_ref):
        i = pl.program_id(0)

        @pl.when(i == 0)
        def _():
            acc_ref[...] = jnp.zeros_like(acc_ref)

        dy_all = _nt(dya_ref[...], w_ref[...])
        for h in range(GLA_H):
            vs = slice(h * GLA_DV, (h + 1) * GLA_DV)
            dy = dy_all[:, vs]
            o = oa_ref[:, vs].astype(F32)
            z = z_ref[:, vs].astype(F32)
            gn = gn_ref[:, vs]
            s = _sigmoid(z)
            ra = lax.rsqrt(jnp.mean(o * o, axis=-1, keepdims=True) + EPS)
            on = o * ra
            don = dy * (z * s)
            t = don * gn
            do_ref[:, vs] = (ra * (t - on * jnp.mean(t * on, axis=-1, keepdims=True))).astype(BF16)
            dz_ref[:, vs] = (dy * (on * gn) * (s * (1.0 + z * (1.0 - s)))).astype(BF16)
            acc_ref[:, vs] += jnp.sum(don * on, axis=0, keepdims=True)

        @pl.when(i == nsteps - 1)
        def _():
            a = acc_ref[...]
            dgn_ref[...] = a[:, 0:256] + a[:, 256:512] + a[:, 512:768] + a[:, 768:1024]

    spec = lambda c: pl.BlockSpec((tr, D), lambda i: (i, c))
    return pl.pallas_call(
        body,
        out_shape=(jax.ShapeDtypeStruct((T, D), BF16), jax.ShapeDtypeStruct((T, D), BF16),
                   jax.ShapeDtypeStruct((1, GLA_DV), F32)),
        grid=(nsteps,),
        in_specs=[spec(0), pl.BlockSpec((D, D), lambda i: (0, 0)), spec(0), spec(1),
                  pl.BlockSpec((1, D), lambda i: (0, 0))],
        out_specs=(spec(0), spec(0), pl.BlockSpec((1, GLA_DV), lambda i: (0, 0))),
        scratch_shapes=[pltpu.VMEM((1, D), F32)],
        compiler_params=_params(("arbitrary",), 6 * _nbytes((tr, D), F32)),
        name="gla_out_bwd",
    )(dya, gla_proj, oa, projA, gn4)


def _gla_bwd(projA, projB, ssave, doa, wg, bg, B, Lp):
    T = B * Lp
    NC = Lp // GLA_C
    C = GLA_C
    scale = GLA_DK ** -0.5
    WC = 2304

    def body(q_ref, k_ref, v_ref, lr_ref, ss_ref, do_ref, wg_ref, bg_ref, dc_ref, dwg_ref, dbg_ref, dst_ref):
        i = pl.program_id(0)
        n = NC - 1 - i

        @pl.when(i == 0)
        def _():
            dst_ref[...] = jnp.zeros_like(dst_ref)
            dwg_ref[...] = jnp.zeros_like(dwg_ref)
            dbg_ref[...] = jnp.zeros_like(dbg_ref)

        pos = n * C + lax.broadcasted_iota(jnp.int32, (C, 1), 0)
        valid = pos >= FRONT
        lower, upper = _tri_masks()
        is_last = lax.broadcasted_iota(jnp.int32, (C, 1), 0) == C - 1
        for b in range(B):
            lr = lr_ref[b]
            pre, glog = _gla_gate(lr, wg_ref[...], bg_ref[...], valid)
            bcum = _cumsum_rows(glog, lower)
            db_parts = []
            for h in range(GLA_H):
                ks = slice(h * GLA_DK, (h + 1) * GLA_DK)
                vs = slice(h * GLA_DV, (h + 1) * GLA_DV)
                bh = bcum[:, ks]
                blast = jnp.sum(jnp.where(is_last, bh, 0.0), axis=0, keepdims=True)
                eb, enb, ekl, ebl = jnp.exp(bh), jnp.exp(-bh), jnp.exp(blast - bh), jnp.exp(blast)
                qh = q_ref[b, :, ks].astype(F32) * scale
                kh = k_ref[b, :, ks].astype(F32)
                qe_f, ke_f, kl_f = qh * eb, kh * enb, kh * ekl
                qe, ke, kl = qe_f.astype(BF16), ke_f.astype(BF16), kl_f.astype(BF16)
                vh = v_ref[b, :, vs].astype(BF16)
                doh = do_ref[b, :, vs]
                st = ss_ref[b, 0, h]
                dst = dst_ref[b, h]
                st_b, dst_b = st.astype(BF16), dst.astype(BF16)
                da = jnp.where(lower, _nt(doh, vh), 0.0).astype(BF16)
                da_t = jnp.where(upper, _nt(vh, doh), 0.0).astype(BF16)
                a_t = jnp.where(upper, _nt(ke, qe), 0.0).astype(BF16)
                dqe = _nn(da, ke) + _nn(doh, st_b)
                dke = _nn(da_t, qe)
                dvh = _nn(a_t, doh) + _nt(kl, dst_b)
                dkl = _nn(vh, dst_b)
                dst_ref[b, h] = dst * ebl + _tn(doh, qe)
                deb = jnp.sum(st * dst, axis=0, keepdims=True)
                db = dqe * qe_f - dke * ke_f - dkl * kl_f
                db_last = jnp.sum(dkl * kl_f, axis=0, keepdims=True) + deb * ebl
                db_parts.append(db + jnp.where(is_last, db_last, 0.0))
                dc_ref[b, :, vs] = dvh.astype(BF16)
                dc_ref[b, :, 1024 + h * GLA_DK:1024 + (h + 1) * GLA_DK] = (dqe * eb * scale).astype(BF16)
                dc_ref[b, :, 1536 + h * GLA_DK:1536 + (h + 1) * GLA_DK] = (dke * enb + dkl * ekl).astype(BF16)
            dglog = _cumsum_rows(jnp.concatenate(db_parts, axis=1), upper)
            dpre = jnp.where(valid, dglog * (1.0 / GLA_NORMALIZER) / (1.0 + jnp.exp(pre)), 0.0)
            dpre_b = dpre.astype(BF16)
            dc_ref[b, :, 2048:2176] = _nt(dpre_b, wg_ref[...]).astype(BF16)
            dc_ref[b, :, 2176:2304] = jnp.zeros((C, 128), BF16)
            dwg_ref[...] += _tn(lr.astype(BF16), dpre_b)
            dbg_ref[...] += jnp.sum(dpre, axis=0, keepdims=True)

    blocks = B * (_nbytes((C, 512), F32) * 2 + _nbytes((C, 1024), F32) + _nbytes((C, 1024), BF16)
                  + _nbytes((GLA_H, GLA_DV, GLA_DK), F32) + _nbytes((C, WC), BF16)) + 3 * _nbytes((128, 512), F32)
    state = _nbytes((B, GLA_H, GLA_DV, GLA_DK), F32)
    pa = projA.reshape(B, Lp, projA.shape[1])
    rev = lambda i: NC - 1 - i
    dc, dwg, dbg = pl.pallas_call(
        body,
        out_shape=(jax.ShapeDtypeStruct((B, Lp, WC), BF16), jax.ShapeDtypeStruct((128, GLA_KW), F32),
                   jax.ShapeDtypeStruct((1, GLA_KW), F32)),
        grid=(NC,),
        in_specs=[
            pl.BlockSpec((B, C, 512), lambda i: (0, rev(i), 10)),
            pl.BlockSpec((B, C, 512), lambda i: (0, rev(i), 11)),
            pl.BlockSpec((B, C, 1024), lambda i: (0, rev(i), 0)),
            pl.BlockSpec((B, C, 128), lambda i: (0, rev(i), 3)),
            pl.BlockSpec((B, 1, GLA_H, GLA_DV, GLA_DK), lambda i: (0, rev(i), 0, 0, 0)),
            pl.BlockSpec((B, C, 1024), lambda i: (0, rev(i), 0)),
            pl.BlockSpec((128, 512), lambda i: (0, 0)),
            pl.BlockSpec((1, 512), lambda i: (0, 0)),
        ],
        out_specs=(pl.BlockSpec((B, C, WC), lambda i: (0, rev(i), 0)),
                   pl.BlockSpec((128, GLA_KW), lambda i: (0, 0)),
                   pl.BlockSpec((1, GLA_KW), lambda i: (0, 0))),
        scratch_shapes=[pltpu.VMEM((B, GLA_H, GLA_DV, GLA_DK), F32)],
        compiler_params=_params(("arbitrary",), blocks, state),
        name="gla_bwd",
    )(pa, pa, pa, projB.reshape(B, Lp, projB.shape[1]), ssave, doa.reshape(B, Lp, GLA_VW), wg, bg)
    return dc.reshape(T, WC), dwg, dbg


def _attn_bwd_pre(dyb, mla_proj, projA, ob, B, Lp):
    T = B * Lp
    NQ = Lp // QB

    def body(dyb_ref, w_ref, z_ref, o_ref, do_ref, dz_ref, dcol_ref):
        j = pl.program_id(1)
        dy_all = _nt(dyb_ref[...], w_ref[...])
        for h in range(MLA_H):
            hs = slice(h * MLA_DV, (h + 1) * MLA_DV)
            dy = dy_all[:, hs]
            z = z_ref[:, hs].astype(F32)
            o = o_ref[:, hs].astype(F32)
            s = _sigmoid(z)
            do = dy * (z * s)
            do_ref[:, hs] = do.astype(BF16)
            dz_ref[:, hs] = (dy * o * (s * (1.0 + z * (1.0 - s)))).astype(BF16)
            dl = jnp.broadcast_to(jnp.sum(do * o, axis=-1, keepdims=True), (QB, LANES))
            dcol_ref[0, h, pl.ds(j, 1), :] = jnp.transpose(dl)[0:1, :]

    rows = lambda c: pl.BlockSpec((QB, D), lambda b, j: (b * NQ + j, c))
    return pl.pallas_call(
        body,
        out_shape=(jax.ShapeDtypeStruct((T, D), BF16), jax.ShapeDtypeStruct((T, D), BF16),
                   jax.ShapeDtypeStruct((B, MLA_H, NQ, QB), F32)),
        grid=(B, NQ),
        in_specs=[rows(0), pl.BlockSpec((D, D), lambda b, j: (0, 0)), rows(2), rows(0)],
        out_specs=(rows(0), rows(0), pl.BlockSpec((1, MLA_H, NQ, QB), lambda b, j: (b, 0, 0, 0))),
        compiler_params=_params(("parallel", "arbitrary"), 6 * _nbytes((QB, D), F32)),
        name="attn_bwd_pre",
    )(dyb, mla_proj, projA, ob)


ATTN_BWD_HEADS = 8


def _attn_bwd(q_att, k_att, v_att, do, lse_c, delta_c, B, Lp):
    T = B * Lp
    NQ = Lp // QB
    G = ATTN_BWD_HEADS
    NG = MLA_H // G
    HW = 2 * LANES
    scale = 1.0 / math.sqrt(MLA_QK)

    def body(q_ref, k_ref, v_ref, do_ref, lse_ref, dl_ref, dq_out, dk_out, dv_out, dq_ref, dk_ref, dv_ref):
        kj = pl.program_id(2)

        @pl.when(kj == 0)
        def _():
            dq_ref[...] = jnp.zeros_like(dq_ref)

        dk_ref[...] = jnp.zeros_like(dk_ref)
        dv_ref[...] = jnp.zeros_like(dv_ref)
        col = kj * QB + lax.broadcasted_iota(jnp.int32, (QB, QB), 0)
        rowi = lax.broadcasted_iota(jnp.int32, (QB, QB), 1)

        def step(qi, masked):
            off = pl.multiple_of(qi * QB, QB)
            ok = _attn_mask(qi * QB + rowi, col) if masked else None
            for h in range(G):
                ws = slice(h * HW, (h + 1) * HW)
                hs = slice(h * MLA_DV, (h + 1) * MLA_DV)
                qb = q_ref[pl.ds(off, QB), ws]
                dob = do_ref[pl.ds(off, QB), hs]
                kb = k_ref[:, ws]
                lse2 = lse_ref[0, h, pl.ds(qi, 1), :]
                delta = dl_ref[0, h, pl.ds(qi, 1), :]
                p_t = jnp.exp2(_nt(kb, qb) * (scale * LOG2E) - lse2)
                if masked:
                    p_t = jnp.where(ok, p_t, 0.0)
                dv_ref[:, hs] += _nn(p_t.astype(BF16), dob)
                ds_t = (p_t * (_nt(v_ref[:, hs], dob) - delta) * scale).astype(BF16)
                dk_ref[:, ws] += _nn(ds_t, qb)
                dq_ref[pl.ds(off, QB), ws] += _tn(ds_t, kb)

        def loop(masked):
            def it(qi, carry):
                step(qi, masked)
                return carry
            lax.fori_loop(kj + 1, NQ, it, 0)

        step(kj, True)
        pl.when(kj == 0)(lambda: loop(True))
        pl.when(kj > 0)(lambda: loop(False))
        dk_out[...] = dk_ref[...].astype(BF16)
        dv_out[...] = dv_ref[...].astype(BF16)

        @pl.when(kj == NQ - 1)
        def _():
            dq_out[...] = dq_ref[...].astype(BF16)

    blocks = (2 * _nbytes((Lp, G * HW), BF16) + _nbytes((Lp, G * MLA_DV), BF16) + 2 * _nbytes((QB, G * 384), BF16)
              + 2 * _nbytes((G, NQ, QB), F32))
    scratch = [pltpu.VMEM((Lp, G * HW), F32), pltpu.VMEM((QB, G * HW), F32), pltpu.VMEM((QB, G * MLA_DV), F32)]
    return pl.pallas_call(
        body,
        out_shape=(jax.ShapeDtypeStruct((T, MLA_H * HW), BF16), jax.ShapeDtypeStruct((T, MLA_H * HW), BF16),
                   jax.ShapeDtypeStruct((T, MLA_H * MLA_DV), BF16)),
        scratch_shapes=scratch,
        grid=(B, NG, NQ),
        in_specs=[
            pl.BlockSpec((Lp, G * HW), lambda b, g, j: (b, g), pipeline_mode=pl.Buffered(1)),
            pl.BlockSpec((QB, G * HW), lambda b, g, j: (b * NQ + j, g)),
            pl.BlockSpec((QB, G * MLA_DV), lambda b, g, j: (b * NQ + j, g)),
            pl.BlockSpec((Lp, G * MLA_DV), lambda b, g, j: (b, g), pipeline_mode=pl.Buffered(1)),
            pl.BlockSpec((1, G, NQ, QB), lambda b, g, j: (b, g, 0, 0)),
            pl.BlockSpec((1, G, NQ, QB), lambda b, g, j: (b, g, 0, 0)),
        ],
        out_specs=(pl.BlockSpec((Lp, G * HW), lambda b, g, j: (b, g), pipeline_mode=pl.Buffered(1)),
                   pl.BlockSpec((QB, G * HW), lambda b, g, j: (b * NQ + j, g)),
                   pl.BlockSpec((QB, G * MLA_DV), lambda b, g, j: (b * NQ + j, g))),
        compiler_params=_params(("parallel", "parallel", "arbitrary"), blocks,
                                _nbytes((Lp, G * HW), F32) + _nbytes((QB, G * 384), F32)),
        name="attn_bwd",
    )(q_att, k_att, v_att, do, lse_c, delta_c)


def _mla_bwd_post(dq, dk, dv, projB, cos_t, sin_t, gq, gkv, wuq2, wukv, B, Lp, tr):
    T = B * Lp
    nt = Lp // tr
    HW = 2 * LANES

    def body(dq_ref, dk_ref, dv_ref, pb_ref, cos_ref, sin_ref, gq_ref, gkv_ref, wuq_ref, wukv_ref,
             dqf_ref, dkvf_ref, de_ref, dgq_ref, dgkv_ref):
        first = (pl.program_id(0) == 0) & (pl.program_id(1) == 0)

        @pl.when(first)
        def _():
            dgq_ref[...] = jnp.zeros_like(dgq_ref)
            dgkv_ref[...] = jnp.zeros_like(dgkv_ref)

        cs = cos_ref[...]
        sn = sin_ref[...]
        rope_t = lambda t: t * cs + _swap_halves(t * sn)
        dkr = jnp.zeros((tr, LANES), F32)
        for h in range(MLA_H):
            dqf_ref[:, h * HW:h * HW + LANES] = dq_ref[:, h * HW:h * HW + LANES]
            dq_rope = dq_ref[:, h * HW + LANES:(h + 1) * HW].astype(F32)
            dqf_ref[:, h * HW + LANES:(h + 1) * HW] = rope_t(dq_rope).astype(BF16)
            dkvf_ref[:, h * HW:h * HW + LANES] = dk_ref[:, h * HW:h * HW + LANES]
            dkvf_ref[:, h * HW + LANES:(h + 1) * HW] = dv_ref[:, h * MLA_DV:(h + 1) * MLA_DV]
            dkr = dkr + dk_ref[:, h * HW + LANES:(h + 1) * HW].astype(F32)

        def norm_bwd(x, dn, g):
            r = lax.rsqrt(jnp.mean(x * x, axis=-1, keepdims=True) + EPS)
            xn = x * r
            t = dn * g
            return r * (t - xn * jnp.mean(t * xn, axis=-1, keepdims=True)), jnp.sum(dn * xn, axis=0, keepdims=True)

        cq = pb_ref[:, 0:Q_RANK].astype(F32)
        ckv = pb_ref[:, Q_RANK:Q_RANK + KV_RANK].astype(F32)
        dcq, dgq = norm_bwd(cq, _nt(dqf_ref[...], wuq_ref[...]), gq_ref[...])
        dckv, dgkv = norm_bwd(ckv, _nt(dkvf_ref[...], wukv_ref[...]), gkv_ref[...])
        dgq_ref[...] += dgq
        dgkv_ref[...] += dgkv
        de_ref[:, 0:Q_RANK] = dcq.astype(BF16)
        de_ref[:, Q_RANK:Q_RANK + KV_RANK] = dckv.astype(BF16)
        de_ref[:, 384:512] = rope_t(dkr).astype(BF16)

    rows = lambda w: pl.BlockSpec((tr, w), lambda b, j: (b * nt + j, 0))
    const = lambda s: pl.BlockSpec(s, lambda b, j: (0, 0))
    blocks = (2 * _nbytes((tr, 2048), F32) + _nbytes((tr, 1024), F32) + _nbytes((tr, 640), F32)
              + 2 * _nbytes((tr, 2048), BF16) + _nbytes((2048, 384), BF16) + 2 * _nbytes((tr, 2048), F32))
    return pl.pallas_call(
        body,
        out_shape=(jax.ShapeDtypeStruct((T, 2048), BF16), jax.ShapeDtypeStruct((T, 2048), BF16),
                   jax.ShapeDtypeStruct((T, 512), BF16), jax.ShapeDtypeStruct((1, Q_RANK), F32),
                   jax.ShapeDtypeStruct((1, KV_RANK), F32)),
        grid=(B, nt),
        in_specs=[rows(2048), rows(2048), rows(1024), rows(640),
                  pl.BlockSpec((tr, 128), lambda b, j: (j, 0)), pl.BlockSpec((tr, 128), lambda b, j: (j, 0)),
                  const((1, Q_RANK)), const((1, KV_RANK)), const((Q_RANK, 2048)), const((KV_RANK, 2048))],
        out_specs=(rows(2048), rows(2048), rows(512), const((1, Q_RANK)), const((1, KV_RANK))),
        compiler_params=_params(("arbitrary", "arbitrary"), blocks),
        name="mla_bwd_post",
    )(dq, dk, dv, projB, cos_t, sin_t, gq, gkv, wuq2, wukv)


def _in_proj_bwd(x, meta, dh1, dA, dBz, dC, dDz, dE, wA, wB, g, B, Lp):
    NQ = Lp // QB
    seq = x.shape[1]

    def body(x_ref, meta_ref, dh_ref, da_ref, db_ref, dc_ref, dd_ref, de_ref, wa_ref, wb_ref, g_ref,
             gx_ref, dmeta_ref, dg_ref):
        b = pl.program_id(0)
        j = pl.program_id(1)

        @pl.when((b == 0) & (j == 0))
        def _():
            dg_ref[...] = jnp.zeros_like(dg_ref)

        du = _nt(da_ref[...], wa_ref[:, 3072:5120])
        du = du + _nt(db_ref[...], wa_ref[:, 1024:2048])
        du = du + _nt(dd_ref[...], wa_ref[:, 2048:3072])
        du = du + _nt(dc_ref[:, 0:1024], wa_ref[:, 0:1024])
        du = du + _nt(dc_ref[:, 1024:2048], wa_ref[:, 5120:6144])
        du = du + _nt(dc_ref[:, 2048:2176], wb_ref[:, 384:512])
        du = du + _nt(de_ref[:, 0:384], wb_ref[:, 0:384])
        du = du + _nt(de_ref[:, 384:512], wb_ref[:, 512:640])

        x = _h_tile(j, x_ref, meta_ref)
        r = lax.rsqrt(jnp.mean(x * x, axis=-1, keepdims=True) + EPS)
        xn = x * r
        t = du * g_ref[...]
        dh0 = dh_ref[...].astype(F32) + r * (t - xn * jnp.mean(t * xn, axis=-1, keepdims=True))
        dg_ref[...] += jnp.sum(du * xn, axis=0, keepdims=True)
        gx_ref[0] = dh0

        @pl.when((j == 0) & (b == 0))
        def _():
            dmeta_ref[...] = dh0[FRONT:HEAD_ROWS, :]

        @pl.when((j == 0) & (b > 0))
        def _():
            dmeta_ref[...] += dh0[FRONT:HEAD_ROWS, :]

    rows = lambda w: pl.BlockSpec((QB, w), lambda b, j: (b * NQ + j, 0))
    const = lambda s: pl.BlockSpec(s, lambda b, j: (0, 0))
    widths = [a.shape[1] for a in (dA, dBz, dC, dDz, dE)]
    blocks = (sum(_nbytes((QB, w), BF16) for w in widths) + _nbytes(wA.shape, BF16) + _nbytes(wB.shape, BF16)
              + 4 * _nbytes((QB, D), F32))
    return pl.pallas_call(
        body,
        out_shape=(jax.ShapeDtypeStruct((B, seq, D), F32), jax.ShapeDtypeStruct((N_META, D), F32),
                   jax.ShapeDtypeStruct((1, D), F32)),
        grid=(B, NQ),
        in_specs=[_x_spec(), const((N_META, D)), rows(D)] + [rows(w) for w in widths]
        + [const(wA.shape), const(wB.shape), const((1, D))],
        out_specs=(_x_spec(), const((N_META, D)), const((1, D))),
        compiler_params=_params(("arbitrary", "arbitrary"), blocks),
        name="in_proj_bwd",
    )(x, meta, dh1, dA, dBz, dC, dDz, dE, wA, wB, g)


_VMEM_WHOLE = pl.BlockSpec(memory_space=pltpu.VMEM)


def _params_whole(arrays):
    total = sum(_nbytes(a.shape, a.dtype) for a in arrays)
    return pltpu.CompilerParams(vmem_limit_bytes=int(min(total + 12 * 1024 * 1024, VMEM_CAP_V7X)))


def _wire_dtype(shape):
    return BF16 if shape[-2] * shape[-1] >= WIRE_BF16_MIN_ELEMS else F32


def _pair_add_big(gp, recv, c):
    _, half, cols = recv.shape
    th = _div_tile(half, 64, 16)
    out_dtype = _wire_dtype(recv.shape)

    steps = half // th

    def body(c_ref, a_ref, b_ref, o_ref):
        o_ref[...] = (a_ref[...] + b_ref[...]).astype(out_dtype)

    return pl.pallas_call(
        body,
        out_shape=jax.ShapeDtypeStruct(recv.shape, out_dtype),
        grid_spec=pltpu.PrefetchScalarGridSpec(
            num_scalar_prefetch=1,
            grid=(steps,),
            in_specs=[pl.BlockSpec((4, th, cols), lambda i, c_ref: (0, c_ref[0] * steps + i, 0)),
                      pl.BlockSpec((4, th, cols), lambda i, c_ref: (0, i, 0))],
            out_specs=pl.BlockSpec((4, th, cols), lambda i, c_ref: (0, i, 0)),
        ),
        compiler_params=_params(("parallel",), 3 * _nbytes((4, th, cols), F32)),
        name="grad_pair_add_big",
    )(c, gp, recv)


def _pair_add_small(gps, recvs):
    n = len(gps)

    def body(*refs):
        c = lax.axis_index("c")
        for t in range(n):
            g_ref, r_ref, o_ref = refs[t], refs[n + t], refs[2 * n + t]
            half = r_ref.shape[1]
            s = g_ref[:, pl.ds(pl.multiple_of(c * half, 8), half), :] + r_ref[...]
            o_ref[...] = s.astype(o_ref.dtype)

    return pl.pallas_call(
        body,
        out_shape=[jax.ShapeDtypeStruct(r.shape, _wire_dtype(r.shape)) for r in recvs],
        in_specs=[_VMEM_WHOLE] * (2 * n),
        out_specs=[_VMEM_WHOLE] * n,
        compiler_params=_params_whole(list(gps) + 2 * list(recvs)),
        name="grad_pair_add_small",
    )(*gps, *recvs)


def _chip_order_sum(landed_ref, own_ref, me):
    p = [jnp.where(me == k, own_ref[k], landed_ref[k]).astype(F32) for k in range(4)]
    return ((p[0] + p[1]) + p[2]) + p[3]


def _sum_chips_big(landed, own, pos):
    _, half, cols = landed.shape
    th = _div_tile(half, 64, 16)

    def body(pos_ref, l_ref, s_ref, o_ref):
        o_ref[0] = _chip_order_sum(l_ref, s_ref, pos_ref[1])

    spec = pl.BlockSpec((4, th, cols), lambda i, pos_ref: (0, i, 0))
    return pl.pallas_call(
        body,
        out_shape=jax.ShapeDtypeStruct((2, half, cols), F32),
        grid_spec=pltpu.PrefetchScalarGridSpec(
            num_scalar_prefetch=1,
            grid=(half // th,),
            in_specs=[spec, spec],
            out_specs=pl.BlockSpec((1, th, cols), lambda i, pos_ref: (pos_ref[0], i, 0)),
        ),
        compiler_params=_params(("parallel",), 3 * _nbytes((4, th, cols), F32)),
        name="grad_sum_chips_big",
    )(pos, landed, own)


def _sum_chips_small(landed, own):
    n = len(landed)

    def body(*refs):
        x, y, c = _mesh_pos()
        for t in range(n):
            refs[2 * n + t][c] = _chip_order_sum(refs[t], refs[n + t], 2 * x + y)

    return pl.pallas_call(
        body,
        out_shape=[jax.ShapeDtypeStruct((2,) + p.shape[1:], F32) for p in landed],
        in_specs=[_VMEM_WHOLE] * (2 * n),
        out_specs=[_VMEM_WHOLE] * n,
        compiler_params=_params_whole(list(landed) * 3),
        name="grad_sum_chips_small",
    )(*landed, *own)


def _adamw_update(w_ref, g_ref, m_ref, v_ref, d_ref, mo_ref, vo_ref):
    c1 = 1.0 - ADAM_B1 ** ADAM_STEP
    c2 = 1.0 - ADAM_B2 ** ADAM_STEP
    gv = g_ref[...]
    mn = ADAM_B1 * m_ref[...] + (1.0 - ADAM_B1) * gv
    vn = ADAM_B2 * v_ref[...] + (1.0 - ADAM_B2) * (gv * gv)
    mo_ref[...] = mn
    vo_ref[...] = vn
    d_ref[...] = -ADAM_LR * ((mn / c1) / (jnp.sqrt(vn / c2) + ADAM_EPS) + ADAM_WD * w_ref[...])


def _adamw_big(w, g, m, v):
    lead, (rows, cols) = w.shape[:-2], w.shape[-2:]
    assert all(n == 1 for n in lead)
    tr = _div_tile(rows, (1 << 19) // cols, 8)
    spec = pl.BlockSpec((1,) * len(lead) + (tr, cols), lambda i: (0,) * len(lead) + (i, 0))
    shp = jax.ShapeDtypeStruct(w.shape, F32)
    return pl.pallas_call(
        functools.partial(_adamw_update),
        out_shape=(shp, shp, shp),
        grid=(rows // tr,),
        in_specs=[spec] * 4,
        out_specs=(spec, spec, spec),
        compiler_params=_params(("parallel",), 7 * _nbytes((tr, cols), F32)),
        name="adamw_big",
    )(w, g, m, v)


def _adamw_small(ws, gs, ms, vs):
    n = len(ws)

    def body(*refs):
        for t in range(n):
            _adamw_update(refs[t], refs[n + t], refs[2 * n + t], refs[3 * n + t],
                          refs[4 * n + t], refs[5 * n + t], refs[6 * n + t])

    shapes = [jax.ShapeDtypeStruct(w.shape, F32) for w in ws]
    return pl.pallas_call(
        body,
        out_shape=shapes * 3,
        in_specs=[_VMEM_WHOLE] * (4 * n),
        out_specs=[_VMEM_WHOLE] * (3 * n),
        compiler_params=_params_whole(list(ws) * 7),
        name="adamw_small",
    )(*ws, *gs, *ms, *vs)


def _mesh_pos():
    return lax.axis_index("x"), lax.axis_index("y"), lax.axis_index("c")


def _other_chips(x, y):
    return [(1 - x, y), (x, 1 - y), (1 - x, 1 - y)]


_ANY = pl.BlockSpec(memory_space=pl.ANY)


PAIR_SPLIT_MIN_ROWS = 64


def _weight_gather(shards):
    n = len(shards)
    split = [s.shape[0] >= PAIR_SPLIT_MIN_ROWS for s in shards]

    def body(*refs):
        w_refs, o_refs = refs[:n], refs[n:2 * n]
        send_sems, recv_sems = refs[2 * n:]
        x, y, c = _mesh_pos()
        me = 2 * x + y
        chips = _other_chips(x, y)

        def rows_of(t, core):
            rows = shards[t].shape[0]
            if not split[t]:
                return pl.ds(0, rows)
            return pl.ds(pl.multiple_of(core * (rows // 2), 16), rows // 2)

        def landed(t, k, slot, rows, to):
            ref = o_refs[t].at[slot, rows]
            return pltpu.make_async_remote_copy(src_ref=ref, dst_ref=ref, send_sem=send_sems.at[6 * t + k],
                                                recv_sem=recv_sems.at[6 * t + k], device_id=to, device_id_type=MESH)

        sends = []
        for t in range(n):
            mine = rows_of(t, c)
            for k, (px, py) in enumerate(chips):
                cp = pltpu.make_async_remote_copy(src_ref=w_refs[t].at[mine], dst_ref=o_refs[t].at[me, mine],
                                                  send_sem=send_sems.at[6 * t + k], recv_sem=recv_sems.at[6 * t + k],
                                                  device_id=(px, py, c), device_id_type=MESH)
                cp.start()
                sends.append(cp)
        for t in range(n):
            mine = rows_of(t, c)
            for k, (px, py) in enumerate(chips):
                landed(t, k, 2 * px + py, mine, (x, y, c)).wait_recv()
                if split[t]:
                    cp = landed(t, 3 + k, 2 * px + py, mine, (x, y, 1 - c))
                    cp.start()
                    sends.append(cp)
        for t in range(n):
            if split[t]:
                for k, (px, py) in enumerate(chips):
                    landed(t, 3 + k, 2 * px + py, rows_of(t, 1 - c), (x, y, c)).wait_recv()
        for cp in sends:
            cp.wait_send()

    return pl.pallas_call(
        body,
        out_shape=[jax.ShapeDtypeStruct((4,) + s.shape, s.dtype) for s in shards],
        in_specs=[_ANY] * n,
        out_specs=[_ANY] * n,
        scratch_shapes=[pltpu.SemaphoreType.DMA((6 * n,)), pltpu.SemaphoreType.DMA((6 * n,))],
        name="weight_gather",
    )(*shards)


def _pair_swap(gps):
    n = len(gps)

    def body(*refs):
        g_refs, o_refs = refs[:n], refs[n:2 * n]
        send_sems, recv_sems = refs[2 * n:]
        x, y, c = _mesh_pos()
        copies = []
        for t in range(n):
            half = gps[t].shape[1] // 2
            theirs = pl.ds(pl.multiple_of((1 - c) * half, 8), half)
            cp = pltpu.make_async_remote_copy(src_ref=g_refs[t].at[:, theirs], dst_ref=o_refs[t],
                                              send_sem=send_sems.at[t], recv_sem=recv_sems.at[t],
                                              device_id=(x, y, 1 - c), device_id_type=MESH)
            cp.start()
            copies.append(cp)
        for cp in copies:
            cp.wait_send()
            cp.wait_recv()

    return pl.pallas_call(
        body,
        out_shape=[jax.ShapeDtypeStruct((4, g.shape[1] // 2, g.shape[2]), g.dtype) for g in gps],
        in_specs=[_ANY] * n,
        out_specs=[_ANY] * n,
        scratch_shapes=[pltpu.SemaphoreType.DMA((n,)), pltpu.SemaphoreType.DMA((n,))],
        name="grad_pair_swap",
    )(*gps)


_HBM = pl.BlockSpec(memory_space=pltpu.HBM)
_SEM = pl.BlockSpec(memory_space=pltpu.SEMAPHORE)


def _in_hbm(a):
    return pltpu.with_memory_space_constraint(a, pltpu.HBM)


def _chip_scatter_start(parts):
    n = len(parts)

    def body(*refs):
        s_refs, l_refs = refs[:n], refs[n:2 * n]
        send_sems, recv_sems = refs[2 * n], refs[2 * n + 1]
        token = refs[-1]
        x, y, c = _mesh_pos()
        me = 2 * x + y
        for t in range(n):
            for k, (px, py) in enumerate(_other_chips(x, y)):
                pltpu.make_async_remote_copy(src_ref=s_refs[t].at[2 * px + py], dst_ref=l_refs[t].at[me],
                                             send_sem=send_sems.at[3 * t + k], recv_sem=recv_sems.at[3 * t + k],
                                             device_id=(px, py, c), device_id_type=MESH).start()
        token[...] = jnp.zeros_like(token)

    hbm = [pltpu.HBM(p.shape, p.dtype) for p in parts]
    outs = pl.pallas_call(
        body,
        name="grad_scatter_start",
        out_shape=(pltpu.SemaphoreType.DMA((3 * n,)), pltpu.SemaphoreType.DMA((3 * n,)), *hbm, *hbm,
                   jax.ShapeDtypeStruct((8, LANES), F32)),
        in_specs=[_HBM] * (2 * n),
        out_specs=(_SEM, _SEM, *([_HBM] * (2 * n)), pl.BlockSpec(memory_space=pltpu.VMEM)),
        input_output_aliases={i: 2 + i for i in range(2 * n)},
        compiler_params=pltpu.CompilerParams(has_side_effects=pltpu.SideEffectType.DATAFLOW_SIDE_EFFECTING),
    )(*[_in_hbm(p) for p in parts], *[_in_hbm(lax.empty(p.shape, p.dtype)) for p in parts])
    return outs[0], outs[1], list(outs[2:2 + n]), list(outs[2 + n:2 + 2 * n]), outs[-1]


def _chip_scatter_wait(send_sems, recv_sems, parts, lands, after):
    n = len(parts)

    def body(*refs):
        s_refs, l_refs = refs[:n], refs[n:2 * n]
        send_sems, recv_sems = refs[2 * n], refs[2 * n + 1]
        x, y, c = _mesh_pos()
        me = 2 * x + y
        for t in range(n):
            for k, (px, py) in enumerate(_other_chips(x, y)):
                cp = pltpu.make_async_remote_copy(src_ref=s_refs[t].at[2 * px + py], dst_ref=l_refs[t].at[2 * px + py],
                                                  send_sem=send_sems.at[3 * t + k], recv_sem=recv_sems.at[3 * t + k],
                                                  device_id=(x, y, c), device_id_type=MESH)
                cp.wait_send()
                cp.wait_recv()

    hbm = [pltpu.HBM(p.shape, p.dtype) for p in parts]
    outs = pl.pallas_call(
        body,
        name="grad_scatter_wait",
        out_shape=(*hbm, *hbm),
        in_specs=[_HBM] * (2 * n) + [_SEM, _SEM, _ANY],
        out_specs=[_HBM] * (2 * n),
        input_output_aliases={i: i for i in range(2 * n)},
        compiler_params=pltpu.CompilerParams(has_side_effects=pltpu.SideEffectType.DATAFLOW_SIDE_EFFECTING),
    )(*parts, *lands, send_sems, recv_sems, after)
    return list(outs[:n]), list(outs[n:])


def _late_gather_start(shards):
    n = len(shards)

    def body(*refs):
        w_refs, l_refs = refs[:n], refs[n:2 * n]
        send_sems, recv_sems = refs[2 * n], refs[2 * n + 1]
        token = refs[-1]
        x, y, c = _mesh_pos()
        me = 2 * x + y
        for t in range(n):
            for k, (px, py) in enumerate(_other_chips(x, y)):
                pltpu.make_async_remote_copy(src_ref=w_refs[t], dst_ref=l_refs[t].at[me],
                                             send_sem=send_sems.at[3 * t + k], recv_sem=recv_sems.at[3 * t + k],
                                             device_id=(px, py, c), device_id_type=MESH).start()
        token[...] = jnp.zeros_like(token)

    src = [pltpu.HBM(s.shape, s.dtype) for s in shards]
    land = [pltpu.HBM((4,) + s.shape, s.dtype) for s in shards]
    outs = pl.pallas_call(
        body,
        name="late_gather_start",
        out_shape=(pltpu.SemaphoreType.DMA((3 * n,)), pltpu.SemaphoreType.DMA((3 * n,)), *src, *land,
                   jax.ShapeDtypeStruct((8, LANES), F32)),
        in_specs=[_HBM] * (2 * n),
        out_specs=(_SEM, _SEM, *([_HBM] * (2 * n)), pl.BlockSpec(memory_space=pltpu.VMEM)),
        input_output_aliases={i: 2 + i for i in range(2 * n)},
        compiler_params=pltpu.CompilerParams(has_side_effects=pltpu.SideEffectType.DATAFLOW_SIDE_EFFECTING),
    )(*[_in_hbm(s) for s in shards], *[_in_hbm(lax.empty((4,) + s.shape, s.dtype)) for s in shards])
    return outs[0], outs[1], list(outs[2:2 + n]), list(outs[2 + n:2 + 2 * n]), outs[-1]


def _late_gather_wait(send_sems, recv_sems, shards, lands, after):
    n = len(shards)

    def body(*refs):
        w_refs, l_refs = refs[:n], refs[n:2 * n]
        send_sems, recv_sems = refs[2 * n], refs[2 * n + 1]
        x, y, c = _mesh_pos()
        for t in range(n):
            for k, (px, py) in enumerate(_other_chips(x, y)):
                cp = pltpu.make_async_remote_copy(src_ref=w_refs[t], dst_ref=l_refs[t].at[2 * px + py],
                                                  send_sem=send_sems.at[3 * t + k], recv_sem=recv_sems.at[3 * t + k],
                                                  device_id=(x, y, c), device_id_type=MESH)
                cp.wait_send()
                cp.wait_recv()

    src = [pltpu.HBM(s.shape, s.dtype) for s in shards]
    land = [pltpu.HBM(l.shape, l.dtype) for l in lands]
    outs = pl.pallas_call(
        body,
        name="late_gather_wait",
        out_shape=(*src, *land),
        in_specs=[_HBM] * (2 * n) + [_SEM, _SEM, _ANY],
        out_specs=[_HBM] * (2 * n),
        input_output_aliases={i: i for i in range(2 * n)},
        compiler_params=pltpu.CompilerParams(has_side_effects=pltpu.SideEffectType.DATAFLOW_SIDE_EFFECTING),
    )(*shards, *lands, send_sems, recv_sems, after)
    return list(outs[n:])


def _all_to_all_small(parts):
    n = len(parts)

    def body(*refs):
        p_refs, o_refs = refs[:n], refs[n:2 * n]
        send_sems, recv_sems = refs[2 * n:]
        x, y, c = _mesh_pos()
        me = 4 * x + 2 * y + c
        sends = []
        for t in range(n):
            for k in range(1, 8):
                px, py, pc = x ^ (k >> 2), y ^ ((k >> 1) & 1), c ^ (k & 1)
                cp = pltpu.make_async_remote_copy(src_ref=p_refs[t], dst_ref=o_refs[t].at[me],
                                                  send_sem=send_sems.at[7 * t + k - 1], recv_sem=recv_sems.at[7 * t + k - 1],
                                                  device_id=(px, py, pc), device_id_type=MESH)
                cp.start()
                sends.append(cp)
        for t in range(n):
            for k in range(1, 8):
                peer = 4 * (x ^ (k >> 2)) + 2 * (y ^ ((k >> 1) & 1)) + (c ^ (k & 1))
                pltpu.make_async_remote_copy(src_ref=p_refs[t], dst_ref=o_refs[t].at[peer],
                                             send_sem=send_sems.at[7 * t + k - 1], recv_sem=recv_sems.at[7 * t + k - 1],
                                             device_id=(x, y, c), device_id_type=MESH).wait_recv()
        for cp in sends:
            cp.wait_send()

    return pl.pallas_call(
        body,
        out_shape=[jax.ShapeDtypeStruct((8,) + p.shape, p.dtype) for p in parts],
        in_specs=[_ANY] * n,
        out_specs=[_ANY] * n,
        scratch_shapes=[pltpu.SemaphoreType.DMA((7 * n,)), pltpu.SemaphoreType.DMA((7 * n,))],
        name="grad_small_all_to_all",
    )(*parts)


def _sum_devices_small(landed, own):
    n = len(landed)

    def body(*refs):
        x, y, c = _mesh_pos()
        me = 4 * x + 2 * y + c
        for t in range(n):
            acc = jnp.where(me == 0, refs[n + t][...], refs[t][0])
            for d in range(1, 8):
                acc = acc + jnp.where(me == d, refs[n + t][...], refs[t][d])
            refs[2 * n + t][...] = acc

    return pl.pallas_call(
        body,
        out_shape=[jax.ShapeDtypeStruct(p.shape, F32) for p in own],
        in_specs=[_VMEM_WHOLE] * (2 * n),
        out_specs=[_VMEM_WHOLE] * n,
        compiler_params=_params_whole(list(landed) + 2 * list(own)),
        name="grad_sum_devices_small",
    )(*landed, *own)


def _pair_join(fs):
    n = len(fs)

    def body(*refs):
        f_refs, o_refs = refs[:n], refs[n:2 * n]
        send_sems, recv_sems = refs[2 * n:]
        x, y, c = _mesh_pos()
        sends = []
        for t in range(n):
            cp = pltpu.make_async_remote_copy(src_ref=f_refs[t].at[c], dst_ref=o_refs[t].at[c], send_sem=send_sems.at[t],
                                              recv_sem=recv_sems.at[t], device_id=(x, y, 1 - c), device_id_type=MESH)
            cp.start()
            sends.append(cp)
        for t in range(n):
            pltpu.make_async_remote_copy(src_ref=f_refs[t].at[c], dst_ref=o_refs[t].at[1 - c], send_sem=send_sems.at[t],
                                         recv_sem=recv_sems.at[t], device_id=(x, y, c), device_id_type=MESH).wait_recv()
        for cp in sends:
            cp.wait_send()

    return pl.pallas_call(
        body,
        out_shape=[jax.ShapeDtypeStruct(f.shape, f.dtype) for f in fs],
        in_specs=[_ANY] * n,
        out_specs=[_ANY] * n,
        input_output_aliases={t: t for t in range(n)},
        scratch_shapes=[pltpu.SemaphoreType.DMA((n,)), pltpu.SemaphoreType.DMA((n,))],
        name="grad_pair_join",
    )(*fs)


def _rope_tables(Lp):
    inv = 1.0 / (ROPE_BASE ** (jnp.arange(0, ROPE, 2, dtype=F32) / ROPE))
    ang = (jnp.arange(Lp, dtype=F32) - FRONT)[:, None] * inv[None, :]
    cs, sn = jnp.cos(ang), jnp.sin(ang)
    return jnp.tile(cs, (1, 4)), jnp.concatenate([-sn, sn, -sn, sn], axis=1)


def _local_step(x, loss_target, meta, norm_g, w_in, gate_w, gate_b, gla_norm_g, gla_proj, q_norm_g, w_uq,
                kv_norm_g, w_ukv, mla_proj, w_out, final_norm_g, early_grads_hook=None, late_weights_hook=None):
    B, seq, _ = x.shape
    Lp = HEAD_ROWS + seq
    T = B * Lp
    tr = _div_tile(Lp, 544, 16)
    tq = _div_tile(T, 1024, QB)
    tkw = _div_tile(T, Lp, QB)

    cuts = np.cumsum((0,) + SPLITS)
    shard_w = IN_WIDTH // 4

    def w_cols(i, width=None):
        parts = []
        for j in range(4):
            a, b = max(cuts[i], j * shard_w), min(cuts[i + 1], (j + 1) * shard_w)
            if a < b:
                parts.append(w_in[j][:, a - j * shard_w:b - j * shard_w])
        if width is not None:
            parts.append(jnp.zeros((D, width - (cuts[i + 1] - cuts[i])), w_in.dtype))
        return parts

    i_q, i_k, i_v, i_lr, i_z, i_cq, i_ckv, i_kr, i_mz, i_gg, i_gm = range(11)
    wA = jnp.concatenate(sum([w_cols(i) for i in (i_v, i_z, i_mz, i_gg, i_gm, i_q, i_k)], []), axis=1)
    wB = jnp.concatenate(w_cols(i_cq) + w_cols(i_ckv) + w_cols(i_lr, 128) + w_cols(i_kr, 128), axis=1)
    gn4 = jnp.tile(gla_norm_g, (1, GLA_H))
    cos_t, sin_t = _rope_tables(Lp)

    u = _rms_in(x, meta, norm_g, B, Lp)
    projA = _mm(u, wA, name="in_proj_a", out_dtype=BF16, tm=tq, tn=1024, tk=D)
    projB = _mm(u, wB, name="in_proj_b", out_dtype=BF16, tm=tq, tn=640, tk=D)
    if late_weights_hook is not None:
        gate_w, gla_proj, w_uq, w_ukv, mla_proj, w_out = late_weights_hook(projB)
    wg = jnp.pad(gate_w, ((0, 128 - GLA_RANK), (0, 0)))
    wuq2 = jnp.pad(w_uq.reshape(Q_RANK, MLA_H, MLA_QK), ((0, 0), (0, 0), (0, 256 - MLA_QK))).reshape(Q_RANK, 2048)
    oa, ya_in, ssave = _gla_fwd(projA, projB, wg, gate_b, gn4, B, Lp)
    ya = _mm(ya_in, gla_proj, name="gla_proj", out_dtype=BF16, tm=tq, tn=D, tk=D)
    q_att, k_att, v_att, cqn, ckvn = _mla_prep(projB, cos_t, sin_t, q_norm_g, kv_norm_g, wuq2, w_ukv, B, Lp, tr)
    ob, yb_in, lse_c = _attn_fwd(q_att, k_att, v_att, projA, B, Lp)
    yb = _mm(yb_in, mla_proj, name="mla_proj", out_dtype=BF16, tm=tq, tn=D, tk=D)
    dh1_b, merged, loss, d_gf = _out_proj_loss(x, meta, projA, ya, yb, w_out, final_norm_g.reshape(1, D),
                                                loss_target, B, Lp)

    g_w_out = _mm(merged, dh1_b, name="dw_out", trans_a=True, tm=D, tn=D, tk=tkw)
    dya, dyb, dA = _merge_bwd(dh1_b, w_out, projA, ya, yb, tr)
    g_gla_proj = _mm(ya_in, dya, name="dw_gla_proj", trans_a=True, tm=D, tn=D, tk=tkw)
    g_mla_proj = _mm(yb_in, dyb, name="dw_mla_proj", trans_a=True, tm=D, tn=D, tk=tkw)
    doa, dBz, d_gn = _gla_out_bwd(dya, gla_proj, oa, projA, gn4, tr)
    dC, g_wg, d_bg = _gla_bwd(projA, projB, ssave, doa, wg, gate_b, B, Lp)
    do, dDz, delta_c = _attn_bwd_pre(dyb, mla_proj, projA, ob, B, Lp)
    dq, dk, dv = _attn_bwd(q_att, k_att, v_att, do, lse_c, delta_c, B, Lp)
    dqf, dkvf, dE, d_gq, d_gkv = _mla_bwd_post(dq, dk, dv, projB, cos_t, sin_t, q_norm_g, kv_norm_g,
                                                wuq2, w_ukv, B, Lp, tr)
    g_wuq2 = _mm(cqn, dqf, name="dw_uq", trans_a=True, tm=Q_RANK, tn=2048, tk=tkw)
    g_wukv = _mm(ckvn, dkvf, name="dw_ukv", trans_a=True, tm=KV_RANK, tn=2048, tk=tkw)
    dparts = [dA, dBz, dC, dDz, dE]
    g_in = [_mm(u, dp, name="dw_in_%d" % i, trans_a=True, tm=D, tn=_div_tile(dp.shape[1], 1024, 256), tk=tkw)
            for i, dp in enumerate(dparts)]

    gA, gBz, gC, gDz, gE = g_in
    src = [(gC, 1024), (gC, 1536), (gC, 0), (gC, 2048), (gBz, 0), (gE, 0), (gE, Q_RANK), (gE, 384), (gDz, 0),
           (gA, 0), (gA, D)]
    owners = []
    for j in range(4):
        parts = []
        for i, (arr, off) in enumerate(src):
            a, b = max(cuts[i], j * shard_w), min(cuts[i + 1], (j + 1) * shard_w)
            if a < b:
                parts.append(arr[:, off + a - cuts[i]:off + b - cuts[i]])
        owners.append(jnp.concatenate(parts, axis=1))
    g_w_in = jnp.stack(owners)
    g_wuq = g_wuq2.reshape(Q_RANK, MLA_H, 256)[:, :, :MLA_QK].reshape(Q_RANK, MLA_H * MLA_QK)
    grads = dict(w_in=g_w_in, gla_gate_w=g_wg[:GLA_RANK], gla_proj=g_gla_proj, mla_w_uq=g_wuq, mla_w_ukv=g_wukv,
                 mla_proj=g_mla_proj, w_out=g_w_out, gla_gate_b=d_bg,
                 gla_norm_g=d_gn, mla_q_norm_g=d_gq, mla_kv_norm_g=d_gkv, final_norm_g=d_gf)
    token = None if early_grads_hook is None else early_grads_hook(grads)
    ng = norm_g if token is None else norm_g + token[0:1, 0:1]
    grad_x, d_meta, d_ng = _in_proj_bwd(x, meta, dh1_b, dA, dBz, dC, dDz, dE, wA, wB, ng, B, Lp)
    grads.update(meta_tokens=d_meta, norm_g=d_ng)
    return loss[0, 0], grad_x, grads


_MATS = ("w_in", "gla_gate_w", "gla_proj", "mla_w_uq", "mla_w_ukv", "mla_proj", "w_out")
_ROW_SHARDED = ("gla_proj", "mla_proj", "w_out")
_ORDER = ("meta_tokens", "norm_g", "w_in", "gla_gate_w", "gla_gate_b", "gla_norm_g", "gla_proj", "mla_q_norm_g",
          "mla_w_uq", "mla_kv_norm_g", "mla_w_ukv", "mla_proj", "w_out", "final_norm_g")
WIRE_BF16_MIN_ELEMS = 128 * 128
SMALL_PACK_ROWS = 16


def _pack_small(d):
    rows = [jnp.pad(d[n].reshape(1, size), ((0, 0), (0, D - size))) for n, size in SMALL]
    return jnp.pad(jnp.concatenate(rows, axis=0), ((0, SMALL_PACK_ROWS - len(rows)), (0, 0)))


def _unpack_small(packed):
    return {n: packed[i, :size] for i, (n, size) in enumerate(SMALL)}


def kernel(x, meta_tokens, norm_g, w_in, gla_gate_w, gla_gate_b, gla_norm_g, gla_proj, mla_q_norm_g, mla_w_uq, mla_kv_norm_g, mla_w_ukv, mla_proj, w_out, final_norm_g, loss_target, m_meta_tokens, m_norm_g, m_w_in, m_gla_gate_w, m_gla_gate_b, m_gla_norm_g, m_gla_proj, m_mla_q_norm_g, m_mla_w_uq, m_mla_kv_norm_g, m_mla_w_ukv, m_mla_proj, m_w_out, m_final_norm_g, v_meta_tokens, v_norm_g, v_w_in, v_gla_gate_w, v_gla_gate_b, v_gla_norm_g, v_gla_proj, v_mla_q_norm_g, v_mla_w_uq, v_mla_kv_norm_g, v_mla_w_ukv, v_mla_proj, v_w_out, v_final_norm_g):
    w = dict(meta_tokens=meta_tokens, norm_g=norm_g, w_in=w_in[0], gla_gate_w=gla_gate_w[0], gla_gate_b=gla_gate_b,
             gla_norm_g=gla_norm_g, gla_proj=gla_proj[0], mla_q_norm_g=mla_q_norm_g, mla_w_uq=mla_w_uq[0],
             mla_kv_norm_g=mla_kv_norm_g, mla_w_ukv=mla_w_ukv[0], mla_proj=mla_proj[0], w_out=w_out[0],
             final_norm_g=final_norm_g)
    mom = dict(meta_tokens=m_meta_tokens, norm_g=m_norm_g, w_in=m_w_in[0], gla_gate_w=m_gla_gate_w[0],
               gla_gate_b=m_gla_gate_b, gla_norm_g=m_gla_norm_g, gla_proj=m_gla_proj[0], mla_q_norm_g=m_mla_q_norm_g,
               mla_w_uq=m_mla_w_uq[0], mla_kv_norm_g=m_mla_kv_norm_g, mla_w_ukv=m_mla_w_ukv[0], mla_proj=m_mla_proj[0],
               w_out=m_w_out[0], final_norm_g=m_final_norm_g)
    var = dict(meta_tokens=v_meta_tokens, norm_g=v_norm_g, w_in=v_w_in[0], gla_gate_w=v_gla_gate_w[0],
               gla_gate_b=v_gla_gate_b, gla_norm_g=v_gla_norm_g, gla_proj=v_gla_proj[0], mla_q_norm_g=v_mla_q_norm_g,
               mla_w_uq=v_mla_w_uq[0], mla_kv_norm_g=v_mla_kv_norm_g, mla_w_ukv=v_mla_w_ukv[0], mla_proj=v_mla_proj[0],
               w_out=v_w_out[0], final_norm_g=v_final_norm_g)
    out_shapes = {n: a.shape for n, a in zip(_ORDER, (meta_tokens, norm_g, w_in, gla_gate_w, gla_gate_b, gla_norm_g,
                                                     gla_proj, mla_q_norm_g, mla_w_uq, mla_kv_norm_g, mla_w_ukv,
                                                     mla_proj, w_out, final_norm_g))}

    me = (2 * lax.axis_index("x") + lax.axis_index("y")).astype(jnp.int32)
    is_mine = lax.broadcasted_iota(jnp.int32, (4, 1, 1), 0) == me
    with_own = lambda gth, own: jnp.where(is_mine, own[None], gth)
    first = [w["w_in"].astype(BF16), meta_tokens]
    w_in_owner, meta_owner = [with_own(gth, own) for gth, own in zip(_weight_gather(first), first)]
    meta_full = meta_owner.transpose(1, 0, 2).reshape(N_META, D)
    late_names = _MATS[1:]
    late = [w[n].astype(BF16) for n in late_names]
    gather_sems = _late_gather_start(late)

    def late_weights(after):
        lands = _late_gather_wait(gather_sems[0], gather_sems[1], gather_sems[2], gather_sems[3], after)
        full = []
        for name, land, own in zip(late_names, lands, late):
            gth = with_own(land, own)
            if name in _ROW_SHARDED:
                full.append(gth.reshape(4 * gth.shape[1], gth.shape[2]))
            else:
                full.append(gth.transpose(1, 0, 2).reshape(gth.shape[1], 4 * gth.shape[2]))
        return full

    def by_owner(name, arr):
        if name == "w_in":
            return arr
        if name in _ROW_SHARDED:
            return arr.reshape(4, arr.shape[0] // 4, arr.shape[1])
        return arr.reshape(arr.shape[0], 4, arr.shape[1] // 4).transpose(1, 0, 2)

    c_idx = lax.axis_index("c").astype(jnp.int32).reshape(1)
    pos = jnp.stack([c_idx[0], me])
    in_flight = {}

    def start_matrix_reduce(early):
        gps = [by_owner(n, early[n]) for n in _MATS]
        recvs = _pair_swap(gps)
        s1 = [_pair_add_big(gps[0], recvs[0], c_idx)] + list(_pair_add_small(gps[1:], recvs[1:]))
        send_sems, recv_sems, parts, lands, token = _chip_scatter_start(s1)
        in_flight.update(send_sems=send_sems, recv_sems=recv_sems, parts=parts, lands=lands)
        return token

    norm_g_after_start = norm_g + gather_sems[4][0:1, 0:1]
    loss_local, grad_x, g = _local_step(
        x, loss_target, meta_full, norm_g_after_start, w_in_owner, None, gla_gate_b, gla_norm_g, None,
        mla_q_norm_g, None, mla_kv_norm_g, None, None, None, final_norm_g,
        early_grads_hook=start_matrix_reduce, late_weights_hook=late_weights)
    loss = lax.psum(loss_local, ("x", "y", "c"))

    s1, landed = _chip_scatter_wait(in_flight["send_sems"], in_flight["recv_sems"], in_flight["parts"],
                                    in_flight["lands"], after=g["norm_g"])
    halves = [_sum_chips_big(landed[0], s1[0], pos)] + list(_sum_chips_small(landed[1:], s1[1:]))
    g_mats = [j.reshape(out_shapes[n]) for j, n in zip(_pair_join(halves), _MATS)]

    late = [g["meta_tokens"], _pack_small(g)]
    meta_sum, small_sum = _sum_devices_small(_all_to_all_small(late), late)
    g_meta = lax.dynamic_slice(meta_sum, (0, me * (D // 4)), (N_META, D // 4))
    names = _MATS + ("meta_tokens",)
    g_red = g_mats + [g_meta, small_sum]

    tens = lambda d: [d[n].reshape(out_shapes[n]) for n in names] + [_pack_small(d)]
    w_t, m_t, v_t = tens(w), tens(mom), tens(var)
    big = _adamw_big(w_t[0], g_red[0], m_t[0], v_t[0])
    rest = _adamw_small(w_t[1:], g_red[1:], m_t[1:], v_t[1:])
    k = len(names)
    results = {"grad": g_red}
    for i, kind in enumerate(("delta", "new_m", "new_v")):
        results[kind] = [big[i]] + list(rest[i * k:(i + 1) * k])

    outs = []
    for kind in ("grad", "delta", "new_m", "new_v"):
        vals = dict(zip(names, results[kind][:-1]))
        vals.update(_unpack_small(results[kind][-1]))
        outs += [vals[n].reshape(out_shapes[n]) for n in _ORDER]
    return (loss, grad_x, *outs)
```

```python
import functools
import math

import jax
import jax.numpy as jnp
import numpy as np
from jax import lax
from jax.experimental import pallas as pl
from jax.experimental.pallas import tpu as pltpu

F32 = jnp.float32
BF16 = jnp.bfloat16

D = 1024
N_META = 16
QB = 256
FRONT = QB - N_META
HEAD_ROWS = FRONT + N_META
assert FRONT % 64 == 48
EPS = 1e-6

GLA_H, GLA_DK, GLA_DV, GLA_RANK, GLA_C = 4, 128, 256, 16, 64
GLA_NORMALIZER = 16.0
GLA_KW, GLA_VW = GLA_H * GLA_DK, GLA_H * GLA_DV
MLA_H, NOPE, ROPE, MLA_DV, Q_RANK, KV_RANK = 8, 128, 64, 128, 256, 128
MLA_QK = NOPE + ROPE
ROPE_BASE = 10000.0
SPLITS = (GLA_KW, GLA_KW, GLA_VW, GLA_RANK, GLA_VW, Q_RANK, KV_RANK, ROPE, MLA_H * MLA_DV, D, D)
IN_WIDTH = sum(SPLITS)

ADAM_LR, ADAM_B1, ADAM_B2, ADAM_EPS, ADAM_WD, ADAM_STEP = 0.001, 0.9, 0.999, 1e-08, 0.01, 10

LANES = 128
VMEM_CAP_V7X = 56 * 1024 * 1024
MESH = pl.DeviceIdType.MESH
NEG = -1e30
LOG2E = math.log2(math.e)

SMALL = (("norm_g", D), ("gla_gate_b", GLA_KW), ("gla_norm_g", GLA_DV), ("mla_q_norm_g", Q_RANK),
         ("mla_kv_norm_g", KV_RANK), ("final_norm_g", D))


def _div_tile(n, target, mult):
    best = None
    for d in range(mult, min(n, target) + 1, mult):
        if n % d == 0:
            best = d
    assert best is not None, (n, target, mult)
    return best


def _params(sem, block_bytes, scratch_bytes=0):
    est = 2 * block_bytes + scratch_bytes + 12 * 1024 * 1024
    return pltpu.CompilerParams(dimension_semantics=sem, vmem_limit_bytes=int(min(max(est, 24 * 1024 * 1024), VMEM_CAP_V7X)))


def _nbytes(shape, dtype):
    return int(np.prod(shape)) * jnp.dtype(dtype).itemsize


def _sigmoid(x):
    return 1.0 / (1.0 + jnp.exp(-x))


def _nt(a, b):
    return lax.dot_general(a, b, (((1,), (1,)), ((), ())), preferred_element_type=F32)


def _tn(a, b):
    return lax.dot_general(a, b, (((0,), (0,)), ((), ())), preferred_element_type=F32)


def _nn(a, b):
    return jnp.dot(a, b, preferred_element_type=F32)


def _split3(x):
    a = x.astype(BF16)
    r = x - a.astype(F32)
    b = r.astype(BF16)
    c = (r - b.astype(F32)).astype(BF16)
    return a, b, c


def _mm(a, b, *, name, trans_a=False, trans_b=False, out_dtype=F32, tm, tn, tk):
    assert not (trans_a and trans_b)
    if trans_a:
        K, M = a.shape
    else:
        M, K = a.shape
    N = b.shape[0] if trans_b else b.shape[1]
    assert (b.shape[1] if trans_b else b.shape[0]) == K
    assert M % tm == 0 and N % tn == 0 and K % tk == 0, (name, M, N, K, tm, tn, tk)
    nk = K // tk

    def body(a_ref, b_ref, o_ref, *scratch):
        av = a_ref[...].astype(BF16)
        bv = b_ref[...].astype(BF16)
        prod = _tn(av, bv) if trans_a else (_nt(av, bv) if trans_b else _nn(av, bv))
        if nk == 1:
            o_ref[...] = prod.astype(out_dtype)
        else:
            acc = scratch[0]
            k = pl.program_id(2)

            @pl.when(k == 0)
            def _():
                acc[...] = prod

            @pl.when(k > 0)
            def _():
                acc[...] += prod

            @pl.when(k == nk - 1)
            def _():
                o_ref[...] = acc[...].astype(out_dtype)

    if trans_a:
        a_spec = pl.BlockSpec((tk, tm), lambda i, j, k: (k, i))
    else:
        a_spec = pl.BlockSpec((tm, tk), lambda i, j, k: (i, k))
    if trans_b:
        b_spec = pl.BlockSpec((tn, tk), lambda i, j, k: (j, k))
    else:
        b_spec = pl.BlockSpec((tk, tn), lambda i, j, k: (k, j))
    blocks = (_nbytes((tm, tk), a.dtype) + _nbytes((tk, tn), b.dtype) + _nbytes((tm, tn), out_dtype))
    scratch = [pltpu.VMEM((tm, tn), F32)] if nk > 1 else []
    return pl.pallas_call(
        body,
        out_shape=jax.ShapeDtypeStruct((M, N), out_dtype),
        grid=(M // tm, N // tn, nk),
        in_specs=[a_spec, b_spec],
        out_specs=pl.BlockSpec((tm, tn), lambda i, j, k: (i, j)),
        scratch_shapes=scratch,
        compiler_params=_params(("parallel", "parallel", "arbitrary"), blocks + _nbytes((tm, tn), F32),
                                _nbytes((tm, tn), F32) if nk > 1 else 0),
        name=name,
    )(a, b)


def _h_tile(j, x_ref, meta_ref):
    head = jnp.concatenate([jnp.zeros((FRONT, D), F32), meta_ref[...]], axis=0)
    return jnp.where(j > 0, x_ref[0], head)


def _x_spec():
    return pl.BlockSpec((1, QB, D), lambda b, j: (b, jnp.maximum(j - 1, 0), 0))


def _rms_in(x, meta, g, B, Lp):
    T = B * Lp
    NQ = Lp // QB

    def body(x_ref, meta_ref, g_ref, u_ref):
        h = _h_tile(pl.program_id(1), x_ref, meta_ref)
        r = lax.rsqrt(jnp.mean(h * h, axis=-1, keepdims=True) + EPS)
        u_ref[...] = (h * r * g_ref[...]).astype(BF16)

    return pl.pallas_call(
        body,
        out_shape=jax.ShapeDtypeStruct((T, D), BF16),
        grid=(B, NQ),
        in_specs=[_x_spec(), pl.BlockSpec((N_META, D), lambda b, j: (0, 0)), pl.BlockSpec((1, D), lambda b, j: (0, 0))],
        out_specs=pl.BlockSpec((QB, D), lambda b, j: (b * NQ + j, 0)),
        compiler_params=_params(("parallel", "parallel"), _nbytes((QB, D), F32) * 2),
        name="rms_in",
    )(x, meta, g)


def _gla_gate(lr, wg, bg, valid):
    pre = _nn(lr.astype(BF16), wg) + bg
    logsig = jnp.minimum(pre, 0.0) - jnp.log(1.0 + jnp.exp(-jnp.abs(pre)))
    return pre, jnp.where(valid, logsig / GLA_NORMALIZER, 0.0)


def _tri_masks():
    ri = lax.broadcasted_iota(jnp.int32, (GLA_C, GLA_C), 0)
    ci = lax.broadcasted_iota(jnp.int32, (GLA_C, GLA_C), 1)
    return ci <= ri, ci >= ri


def _cumsum_rows(x, ones_mask):
    w = jnp.where(ones_mask, 1.0, 0.0).astype(BF16)
    a, b, c = _split3(x)
    return _nn(w, a) + _nn(w, b) + _nn(w, c)


def _gla_fwd(projA, projB, wg, bg, gn4, B, Lp):
    T = B * Lp
    NC = Lp // GLA_C
    C = GLA_C
    scale = GLA_DK ** -0.5

    def body(q_ref, k_ref, v_ref, lr_ref, z_ref, wg_ref, bg_ref, gn_ref, oa_ref, ya_ref, ssave_ref, st_ref):
        n = pl.program_id(0)

        @pl.when(n == 0)
        def _():
            st_ref[...] = jnp.zeros_like(st_ref)

        pos = n * C + lax.broadcasted_iota(jnp.int32, (C, 1), 0)
        lower, _ = _tri_masks()
        is_last = lax.broadcasted_iota(jnp.int32, (C, 1), 0) == C - 1
        for b in range(B):
            ssave_ref[b, 0] = st_ref[b]
            _, glog = _gla_gate(lr_ref[b], wg_ref[...], bg_ref[...], pos >= FRONT)
            bcum = _cumsum_rows(glog, lower)
            for h in range(GLA_H):
                ks = slice(h * GLA_DK, (h + 1) * GLA_DK)
                vs = slice(h * GLA_DV, (h + 1) * GLA_DV)
                bh = bcum[:, ks]
                blast = jnp.sum(jnp.where(is_last, bh, 0.0), axis=0, keepdims=True)
                qh = q_ref[b, :, ks].astype(F32) * scale
                kh = k_ref[b, :, ks].astype(F32)
                qe = (qh * jnp.exp(bh)).astype(BF16)
                ke = (kh * jnp.exp(-bh)).astype(BF16)
                kl = (kh * jnp.exp(blast - bh)).astype(BF16)
                vh = v_ref[b, :, vs].astype(BF16)
                a = jnp.where(lower, _nt(qe, ke), 0.0).astype(BF16)
                st = st_ref[b, h]
                o = _nn(a, vh) + _nt(qe, st.astype(BF16))
                st_ref[b, h] = st * jnp.exp(blast) + _tn(vh, kl)
                oa_ref[b, :, vs] = o.astype(BF16)
                on = o * lax.rsqrt(jnp.mean(o * o, axis=-1, keepdims=True) + EPS) * gn_ref[:, vs]
                z = z_ref[b, :, vs].astype(F32)
                ya_ref[b, :, vs] = (on * (z * _sigmoid(z))).astype(BF16)

    blocks = B * (_nbytes((C, 512), F32) * 2 + _nbytes((C, 1024), F32) * 3 + _nbytes((C, 1024), BF16)
                  + _nbytes((GLA_H, GLA_DV, GLA_DK), F32)) + _nbytes((128, 512), BF16)
    state = _nbytes((B, GLA_H, GLA_DV, GLA_DK), F32)
    pa = projA.reshape(B, Lp, projA.shape[1])
    oa, ya, ssave = pl.pallas_call(
        body,
        out_shape=(jax.ShapeDtypeStruct((B, Lp, GLA_VW), BF16), jax.ShapeDtypeStruct((B, Lp, GLA_VW), BF16),
                   jax.ShapeDtypeStruct((B, NC, GLA_H, GLA_DV, GLA_DK), F32)),
        grid=(NC,),
        in_specs=[
            pl.BlockSpec((B, C, 512), lambda n: (0, n, 10)),
            pl.BlockSpec((B, C, 512), lambda n: (0, n, 11)),
            pl.BlockSpec((B, C, 1024), lambda n: (0, n, 0)),
            pl.BlockSpec((B, C, 128), lambda n: (0, n, 3)),
            pl.BlockSpec((B, C, 1024), lambda n: (0, n, 1)),
            pl.BlockSpec((128, 512), lambda n: (0, 0)),
            pl.BlockSpec((1, 512), lambda n: (0, 0)),
            pl.BlockSpec((1, 1024), lambda n: (0, 0)),
        ],
        out_specs=(pl.BlockSpec((B, C, 1024), lambda n: (0, n, 0)),
                   pl.BlockSpec((B, C, 1024), lambda n: (0, n, 0)),
                   pl.BlockSpec((B, 1, GLA_H, GLA_DV, GLA_DK), lambda n: (0, n, 0, 0, 0))),
        scratch_shapes=[pltpu.VMEM((B, GLA_H, GLA_DV, GLA_DK), F32)],
        compiler_params=_params(("arbitrary",), blocks, state),
        name="gla_fwd",
    )(pa, pa, pa, projB.reshape(B, Lp, projB.shape[1]), pa, wg, bg, gn4)
    return oa.reshape(T, GLA_VW), ya.reshape(T, GLA_VW), ssave


def _swap_halves(x):
    lane = lax.broadcasted_iota(jnp.int32, x.shape, 1)
    return jnp.where((lane % 64) < 32, pltpu.roll(x, 96, 1), pltpu.roll(x, 32, 1))


def _mla_prep(projB, cos_t, sin_t, gq, gkv, wuq2, wukv, B, Lp, tr):
    T = B * Lp
    nt = Lp // tr
    HW = 2 * LANES

    def body(pb_ref, cos_ref, sin_ref, gq_ref, gkv_ref, wuq_ref, wukv_ref, q_ref, k_ref, v_ref, cqn_ref, ckvn_ref):
        cq = pb_ref[:, 0:Q_RANK].astype(F32)
        ckv = pb_ref[:, Q_RANK:Q_RANK + KV_RANK].astype(F32)
        kr = pb_ref[:, 512:640].astype(F32)
        cqn = (cq * lax.rsqrt(jnp.mean(cq * cq, axis=-1, keepdims=True) + EPS) * gq_ref[...]).astype(BF16)
        ckvn = (ckv * lax.rsqrt(jnp.mean(ckv * ckv, axis=-1, keepdims=True) + EPS) * gkv_ref[...]).astype(BF16)
        cqn_ref[...] = cqn
        ckvn_ref[...] = ckvn
        qf = _nn(cqn, wuq_ref[...])
        kvf = _nn(ckvn, wukv_ref[...])
        cs = cos_ref[...]
        sn = sin_ref[...]
        rope = lambda t: t * cs + _swap_halves(t) * sn
        kr_r = rope(kr).astype(BF16)
        for h in range(MLA_H):
            q_ref[:, h * HW:h * HW + LANES] = qf[:, h * HW:h * HW + LANES].astype(BF16)
            q_ref[:, h * HW + LANES:(h + 1) * HW] = rope(qf[:, h * HW + LANES:(h + 1) * HW]).astype(BF16)
            k_ref[:, h * HW:h * HW + LANES] = kvf[:, h * HW:h * HW + LANES].astype(BF16)
            k_ref[:, h * HW + LANES:(h + 1) * HW] = kr_r
            v_ref[:, h * MLA_DV:(h + 1) * MLA_DV] = kvf[:, h * HW + LANES:(h + 1) * HW].astype(BF16)

    blocks = (_nbytes((tr, 640), F32) + 2 * _nbytes((tr, 128), F32) + _nbytes((Q_RANK, 2048), BF16)
              + _nbytes((KV_RANK, 2048), BF16) + _nbytes((tr, 2048 * 2 + 1024 + 384), BF16)
              + 2 * _nbytes((tr, 2048), F32))
    return pl.pallas_call(
        body,
        out_shape=(jax.ShapeDtypeStruct((T, MLA_H * HW), BF16), jax.ShapeDtypeStruct((T, MLA_H * HW), BF16),
                   jax.ShapeDtypeStruct((T, MLA_H * MLA_DV), BF16), jax.ShapeDtypeStruct((T, Q_RANK), BF16),
                   jax.ShapeDtypeStruct((T, KV_RANK), BF16)),
        grid=(B, nt),
        in_specs=[
            pl.BlockSpec((tr, 640), lambda b, j: (b * nt + j, 0)),
            pl.BlockSpec((tr, 128), lambda b, j: (j, 0)),
            pl.BlockSpec((tr, 128), lambda b, j: (j, 0)),
            pl.BlockSpec((1, Q_RANK), lambda b, j: (0, 0)),
            pl.BlockSpec((1, KV_RANK), lambda b, j: (0, 0)),
            pl.BlockSpec((Q_RANK, 2048), lambda b, j: (0, 0)),
            pl.BlockSpec((KV_RANK, 2048), lambda b, j: (0, 0)),
        ],
        out_specs=(pl.BlockSpec((tr, 2048), lambda b, j: (b * nt + j, 0)),
                   pl.BlockSpec((tr, 2048), lambda b, j: (b * nt + j, 0)),
                   pl.BlockSpec((tr, 1024), lambda b, j: (b * nt + j, 0)),
                   pl.BlockSpec((tr, Q_RANK), lambda b, j: (b * nt + j, 0)),
                   pl.BlockSpec((tr, KV_RANK), lambda b, j: (b * nt + j, 0))),
        compiler_params=_params(("parallel", "parallel"), blocks),
        name="mla_prep",
    )(projB, cos_t, sin_t, gq, gkv, wuq2, wukv)


def _attn_mask(row, col):
    return (col <= row) & ((col >= FRONT) | (row < FRONT))


def _attn_fwd(q_att, k_att, v_att, projA, B, Lp):
    T = B * Lp
    NQ = Lp // QB
    HW = 2 * LANES
    scale = 1.0 / math.sqrt(MLA_QK)

    def body(q_ref, k_ref, v_ref, mz_ref, o_ref, yb_ref, lsec_ref, m_ref, l_ref, acc_ref):
        qi = pl.program_id(1)
        m_ref[...] = jnp.full(m_ref.shape, NEG, F32)
        l_ref[...] = jnp.zeros_like(l_ref)
        acc_ref[...] = jnp.zeros_like(acc_ref)
        row = qi * QB + lax.broadcasted_iota(jnp.int32, (QB, QB), 0)
        coli = lax.broadcasted_iota(jnp.int32, (QB, QB), 1)

        def step(kj, masked):
            off = pl.multiple_of(kj * QB, QB)
            ok = _attn_mask(row, kj * QB + coli) if masked else None
            for h in range(MLA_H):
                q = q_ref[:, h * HW:(h + 1) * HW]
                kb = k_ref[pl.ds(off, QB), h * HW:(h + 1) * HW]
                vb = v_ref[pl.ds(off, QB), h * MLA_DV:(h + 1) * MLA_DV]
                s = _nt(q, kb) * (scale * LOG2E)
                if masked:
                    s = jnp.where(ok, s, NEG)
                m_old = m_ref[h]
                m_new = jnp.maximum(m_old, jnp.max(s, axis=-1, keepdims=True))
                alpha = jnp.exp2(m_old - m_new)
                p = jnp.exp2(s - jnp.tile(m_new, (1, QB // LANES)))
                m_ref[h] = m_new
                l_ref[h] = alpha * l_ref[h] + jnp.sum(p, axis=-1, keepdims=True)
                acc_ref[h] = alpha * acc_ref[h] + _nn(p.astype(BF16), vb)

        step(0, True)

        def unmasked(kj, carry):
            step(kj, False)
            return carry

        lax.fori_loop(1, qi, unmasked, 0)

        @pl.when(qi > 0)
        def _():
            step(qi, True)

        for h in range(MLA_H):
            hs = slice(h * MLA_DV, (h + 1) * MLA_DV)
            l = l_ref[h]
            o = acc_ref[h] / l
            o_ref[:, hs] = o.astype(BF16)
            z = mz_ref[:, hs].astype(F32)
            yb_ref[:, hs] = (o * (z * _sigmoid(z))).astype(BF16)
            lse2 = m_ref[h] + jnp.log(l) * LOG2E
            lsec_ref[0, h, pl.ds(qi, 1), :] = jnp.transpose(lse2)[0:1, :]

    blocks = (_nbytes((QB, 2048), BF16) + _nbytes((Lp, 2048), BF16) + _nbytes((Lp, 1024), BF16)
              + 2 * _nbytes((QB, 1024), F32) + _nbytes((QB, 1024), BF16) + _nbytes((MLA_H, QB, LANES), F32)
              + _nbytes((MLA_H, NQ, QB), F32))
    return pl.pallas_call(
        body,
        out_shape=(jax.ShapeDtypeStruct((T, MLA_H * MLA_DV), BF16), jax.ShapeDtypeStruct((T, MLA_H * MLA_DV), BF16),
                   jax.ShapeDtypeStruct((B, MLA_H, NQ, QB), F32)),
        grid=(B, NQ),
        in_specs=[
            pl.BlockSpec((QB, MLA_H * HW), lambda b, i: (b * NQ + i, 0)),
            pl.BlockSpec((Lp, MLA_H * HW), lambda b, i: (b, 0)),
            pl.BlockSpec((Lp, MLA_H * MLA_DV), lambda b, i: (b, 0)),
            pl.BlockSpec((QB, 1024), lambda b, i: (b * NQ + i, 2)),
        ],
        out_specs=(pl.BlockSpec((QB, 1024), lambda b, i: (b * NQ + i, 0)),
                   pl.BlockSpec((QB, 1024), lambda b, i: (b * NQ + i, 0)),
                   pl.BlockSpec((1, MLA_H, NQ, QB), lambda b, i: (b, 0, 0, 0))),
        scratch_shapes=[pltpu.VMEM((MLA_H, QB, LANES), F32), pltpu.VMEM((MLA_H, QB, LANES), F32),
                        pltpu.VMEM((MLA_H, QB, MLA_DV), F32)],
        compiler_params=_params(("parallel", "arbitrary"), blocks, 3 * _nbytes((MLA_H, QB, LANES), F32)),
        name="attn_fwd",
    )(q_att, k_att, v_att, projA)


def _out_proj_loss(x, meta, projA, ya, yb, w_out, gf, tgt, B, Lp):
    T = B * Lp
    NQ = Lp // QB

    def body(x_ref, meta_ref, gg_ref, gm_ref, ya_ref, yb_ref, w_ref, gf_ref, t_ref,
             dhb_ref, mg_ref, loss_ref, dgf_ref):
        b = pl.program_id(0)
        j = pl.program_id(1)

        @pl.when((b == 0) & (j == 0))
        def _():
            loss_ref[...] = jnp.zeros_like(loss_ref)
            dgf_ref[...] = jnp.zeros_like(dgf_ref)

        f32 = lambda ref: ref[...].astype(F32)
        merged = (_sigmoid(f32(gg_ref)) * f32(ya_ref) + _sigmoid(f32(gm_ref)) * f32(yb_ref)).astype(BF16)
        mg_ref[...] = merged
        h1 = _h_tile(j, x_ref, meta_ref) + _nn(merged, w_ref[...])
        r = lax.rsqrt(jnp.mean(h1 * h1, axis=-1, keepdims=True) + EPS)
        hn = h1 * r
        gfv = gf_ref[...]
        diff = jnp.where(j > 0, hn * gfv - t_ref[0], 0.0)
        loss_ref[...] += (0.5 / D) * jnp.sum(jnp.sum(diff * diff, axis=-1, keepdims=True), axis=0, keepdims=True)
        dout = diff * (1.0 / D)
        dgf_ref[...] += jnp.sum(dout * hn, axis=0, keepdims=True)
        dhn = dout * gfv
        dh = r * (dhn - hn * jnp.mean(dhn * hn, axis=-1, keepdims=True))
        dhb_ref[...] = dh.astype(BF16)

    rows = lambda c: pl.BlockSpec((QB, D), lambda b, j: (b * NQ + j, c))
    const = lambda s: pl.BlockSpec(s, lambda b, j: (0, 0))
    return pl.pallas_call(
        body,
        out_shape=(jax.ShapeDtypeStruct((T, D), BF16), jax.ShapeDtypeStruct((T, D), BF16),
                   jax.ShapeDtypeStruct((1, 1), F32), jax.ShapeDtypeStruct((1, D), F32)),
        grid=(B, NQ),
        in_specs=[_x_spec(), const((N_META, D)), rows(3), rows(4), rows(0), rows(0), const((D, D)),
                  const((1, D)), _x_spec()],
        out_specs=(rows(0), rows(0), const((1, 1)), const((1, D))),
        compiler_params=_params(("arbitrary", "arbitrary"), 10 * _nbytes((QB, D), F32)),
        name="out_proj_loss",
    )(x, meta, projA, projA, ya, yb, w_out, gf, tgt)


def _merge_bwd(dh1_b, w_out, projA, ya, yb, tr):
    T = dh1_b.shape[0]

    def body(dh_ref, w_ref, gg_ref, gm_ref, ya_ref, yb_ref, dya_ref, dyb_ref, da_ref):
        d = _nt(dh_ref[...], w_ref[...])
        sg = _sigmoid(gg_ref[...].astype(F32))
        sm = _sigmoid(gm_ref[...].astype(F32))
        dya_ref[...] = (d * sg).astype(BF16)
        dyb_ref[...] = (d * sm).astype(BF16)
        da_ref[:, 0:D] = (d * ya_ref[...].astype(F32) * (sg * (1.0 - sg))).astype(BF16)
        da_ref[:, D:2 * D] = (d * yb_ref[...].astype(F32) * (sm * (1.0 - sm))).astype(BF16)

    spec = lambda c: pl.BlockSpec((tr, D), lambda i: (i, c))
    return pl.pallas_call(
        body,
        out_shape=(jax.ShapeDtypeStruct((T, D), BF16), jax.ShapeDtypeStruct((T, D), BF16),
                   jax.ShapeDtypeStruct((T, 2 * D), BF16)),
        grid=(T // tr,),
        in_specs=[spec(0), pl.BlockSpec((D, D), lambda i: (0, 0)), spec(3), spec(4), spec(0), spec(0)],
        out_specs=(spec(0), spec(0), pl.BlockSpec((tr, 2 * D), lambda i: (i, 0))),
        compiler_params=_params(("parallel",), 8 * _nbytes((tr, D), F32)),
        name="merge_bwd",
    )(dh1_b, w_out, projA, projA, ya, yb)


def _gla_out_bwd(dya, gla_proj, oa, projA, gn4, tr):
    T = dya.shape[0]
    nsteps = T // tr

    def body(dya_ref, w_ref, oa_ref, z_ref, gn_ref, do_ref, dz_ref, dgn_ref, acc_ref):
        i = pl.program_id(0)

        @pl.when(i == 0)
        def _():
            acc_ref[...] = jnp.zeros_like(acc_ref)

        dy_all = _nt(dya_ref[...], w_ref[...])
        for h in range(GLA_H):
            vs = slice(h * GLA_DV, (h + 1) * GLA_DV)
            dy = dy_all[:, vs]
            o = oa_ref[:, vs].astype(F32)
            z = z_ref[:, vs].astype(F32)
            gn = gn_ref[:, vs]
            s = _sigmoid(z)
            ra = lax.rsqrt(jnp.mean(o * o, axis=-1, keepdims=True) + EPS)
            on = o * ra
            don = dy * (z * s)
            t = don * gn
            do_ref[:, vs] = (ra * (t - on * jnp.mean(t * on, axis=-1, keepdims=True))).astype(BF16)
            dz_ref[:, vs] = (dy * (on * gn) * (s * (1.0 + z * (1.0 - s)))).astype(BF16)
            acc_ref[:, vs] += jnp.sum(don * on, axis=0, keepdims=True)

        @pl.when(i == nsteps - 1)
        def _():
            a = acc_ref[...]
            dgn_ref[...] = a[:, 0:256] + a[:, 256:512] + a[:, 512:768] + a[:, 768:1024]

    spec = lambda c: pl.BlockSpec((tr, D), lambda i: (i, c))
    return pl.pallas_call(
        body,
        out_shape=(jax.ShapeDtypeStruct((T, D), BF16), jax.ShapeDtypeStruct((T, D), BF16),
                   jax.ShapeDtypeStruct((1, GLA_DV), F32)),
        grid=(nsteps,),
        in_specs=[spec(0), pl.BlockSpec((D, D), lambda i: (0, 0)), spec(0), spec(1),
                  pl.BlockSpec((1, D), lambda i: (0, 0))],
        out_specs=(spec(0), spec(0), pl.BlockSpec((1, GLA_DV), lambda i: (0, 0))),
        scratch_shapes=[pltpu.VMEM((1, D), F32)],
        compiler_params=_params(("arbitrary",), 6 * _nbytes((tr, D), F32)),
        name="gla_out_bwd",
    )(dya, gla_proj, oa, projA, gn4)


def _gla_bwd(projA, projB, ssave, doa, wg, bg, B, Lp):
    T = B * Lp
    NC = Lp // GLA_C
    C = GLA_C
    scale = GLA_DK ** -0.5
    WC = 2304

    def body(q_ref, k_ref, v_ref, lr_ref, ss_ref, do_ref, wg_ref, bg_ref, dc_ref, dwg_ref, dbg_ref, dst_ref):
        i = pl.program_id(0)
        n = NC - 1 - i

        @pl.when(i == 0)
        def _():
            dst_ref[...] = jnp.zeros_like(dst_ref)
            dwg_ref[...] = jnp.zeros_like(dwg_ref)
            dbg_ref[...] = jnp.zeros_like(dbg_ref)

        pos = n * C + lax.broadcasted_iota(jnp.int32, (C, 1), 0)
        valid = pos >= FRONT
        lower, upper = _tri_masks()
        is_last = lax.broadcasted_iota(jnp.int32, (C, 1), 0) == C - 1
        for b in range(B):
            lr = lr_ref[b]
            pre, glog = _gla_gate(lr, wg_ref[...], bg_ref[...], valid)
            bcum = _cumsum_rows(glog, lower)
            db_parts = []
            for h in range(GLA_H):
                ks = slice(h * GLA_DK, (h + 1) * GLA_DK)
                vs = slice(h * GLA_DV, (h + 1) * GLA_DV)
                bh = bcum[:, ks]
                blast = jnp.sum(jnp.where(is_last, bh, 0.0), axis=0, keepdims=True)
                eb, enb, ekl, ebl = jnp.exp(bh), jnp.exp(-bh), jnp.exp(blast - bh), jnp.exp(blast)
                qh = q_ref[b, :, ks].astype(F32) * scale
                kh = k_ref[b, :, ks].astype(F32)
                qe_f, ke_f, kl_f = qh * eb, kh * enb, kh * ekl
                qe, ke, kl = qe_f.astype(BF16), ke_f.astype(BF16), kl_f.astype(BF16)
                vh = v_ref[b, :, vs].astype(BF16)
                doh = do_ref[b, :, vs]
                st = ss_ref[b, 0, h]
                dst = dst_ref[b, h]
                st_b, dst_b = st.astype(BF16), dst.astype(BF16)
                da = jnp.where(lower, _nt(doh, vh), 0.0).astype(BF16)
                da_t = jnp.where(upper, _nt(vh, doh), 0.0).astype(BF16)
                a_t = jnp.where(upper, _nt(ke, qe), 0.0).astype(BF16)
                dqe = _nn(da, ke) + _nn(doh, st_b)
                dke = _nn(da_t, qe)
                dvh = _nn(a_t, doh) + _nt(kl, dst_b)
                dkl = _nn(vh, dst_b)
                dst_ref[b, h] = dst * ebl + _tn(doh, qe)
                deb = jnp.sum(st * dst, axis=0, keepdims=True)
                db = dqe * qe_f - dke * ke_f - dkl * kl_f
                db_last = jnp.sum(dkl * kl_f, axis=0, keepdims=True) + deb * ebl
                db_parts.append(db + jnp.where(is_last, db_last, 0.0))
                dc_ref[b, :, vs] = dvh.astype(BF16)
                dc_ref[b, :, 1024 + h * GLA_DK:1024 + (h + 1) * GLA_DK] = (dqe * eb * scale).astype(BF16)
                dc_ref[b, :, 1536 + h * GLA_DK:1536 + (h + 1) * GLA_DK] = (dke * enb + dkl * ekl).astype(BF16)
            dglog = _cumsum_rows(jnp.concatenate(db_parts, axis=1), upper)
            dpre = jnp.where(valid, dglog * (1.0 / GLA_NORMALIZER) / (1.0 + jnp.exp(pre)), 0.0)
            dpre_b = dpre.astype(BF16)
            dc_ref[b, :, 2048:2176] = _nt(dpre_b, wg_ref[...]).astype(BF16)
            dc_ref[b, :, 2176:2304] = jnp.zeros((C, 128), BF16)
            dwg_ref[...] += _tn(lr.astype(BF16), dpre_b)
            dbg_ref[...] += jnp.sum(dpre, axis=0, keepdims=True)

    blocks = B * (_nbytes((C, 512), F32) * 2 + _nbytes((C, 1024), F32) + _nbytes((C, 1024), BF16)
                  + _nbytes((GLA_H, GLA_DV, GLA_DK), F32) + _nbytes((C, WC), BF16)) + 3 * _nbytes((128, 512), F32)
    state = _nbytes((B, GLA_H, GLA_DV, GLA_DK), F32)
    pa = projA.reshape(B, Lp, projA.shape[1])
    rev = lambda i: NC - 1 - i
    dc, dwg, dbg = pl.pallas_call(
        body,
        out_shape=(jax.ShapeDtypeStruct((B, Lp, WC), BF16), jax.ShapeDtypeStruct((128, GLA_KW), F32),
                   jax.ShapeDtypeStruct((1, GLA_KW), F32)),
        grid=(NC,),
        in_specs=[
            pl.BlockSpec((B, C, 512), lambda i: (0, rev(i), 10)),
            pl.BlockSpec((B, C, 512), lambda i: (0, rev(i), 11)),
            pl.BlockSpec((B, C, 1024), lambda i: (0, rev(i), 0)),
            pl.BlockSpec((B, C, 128), lambda i: (0, rev(i), 3)),
            pl.BlockSpec((B, 1, GLA_H, GLA_DV, GLA_DK), lambda i: (0, rev(i), 0, 0, 0)),
            pl.BlockSpec((B, C, 1024), lambda i: (0, rev(i), 0)),
            pl.BlockSpec((128, 512), lambda i: (0, 0)),
            pl.BlockSpec((1, 512), lambda i: (0, 0)),
        ],
        out_specs=(pl.BlockSpec((B, C, WC), lambda i: (0, rev(i), 0)),
                   pl.BlockSpec((128, GLA_KW), lambda i: (0, 0)),
                   pl.BlockSpec((1, GLA_KW), lambda i: (0, 0))),
        scratch_shapes=[pltpu.VMEM((B, GLA_H, GLA_DV, GLA_DK), F32)],
        compiler_params=_params(("arbitrary",), blocks, state),
        name="gla_bwd",
    )(pa, pa, pa, projB.reshape(B, Lp, projB.shape[1]), ssave, doa.reshape(B, Lp, GLA_VW), wg, bg)
    return dc.reshape(T, WC), dwg, dbg


def _attn_bwd_pre(dyb, mla_proj, projA, ob, B, Lp):
    T = B * Lp
    NQ = Lp // QB

    def body(dyb_ref, w_ref, z_ref, o_ref, do_ref, dz_ref, dcol_ref):
        j = pl.program_id(1)
        dy_all = _nt(dyb_ref[...], w_ref[...])
        for h in range(MLA_H):
            hs = slice(h * MLA_DV, (h + 1) * MLA_DV)
            dy = dy_all[:, hs]
            z = z_ref[:, hs].astype(F32)
            o = o_ref[:, hs].astype(F32)
            s = _sigmoid(z)
            do = dy * (z * s)
            do_ref[:, hs] = do.astype(BF16)
            dz_ref[:, hs] = (dy * o * (s * (1.0 + z * (1.0 - s)))).astype(BF16)
            dl = jnp.broadcast_to(jnp.sum(do * o, axis=-1, keepdims=True), (QB, LANES))
            dcol_ref[0, h, pl.ds(j, 1), :] = jnp.transpose(dl)[0:1, :]

    rows = lambda c: pl.BlockSpec((QB, D), lambda b, j: (b * NQ + j, c))
    return pl.pallas_call(
        body,
        out_shape=(jax.ShapeDtypeStruct((T, D), BF16), jax.ShapeDtypeStruct((T, D), BF16),
                   jax.ShapeDtypeStruct((B, MLA_H, NQ, QB), F32)),
        grid=(B, NQ),
        in_specs=[rows(0), pl.BlockSpec((D, D), lambda b, j: (0, 0)), rows(2), rows(0)],
        out_specs=(rows(0), rows(0), pl.BlockSpec((1, MLA_H, NQ, QB), lambda b, j: (b, 0, 0, 0))),
        compiler_params=_params(("parallel", "arbitrary"), 6 * _nbytes((QB, D), F32)),
        name="attn_bwd_pre",
    )(dyb, mla_proj, projA, ob)


ATTN_BWD_HEADS = 8


def _attn_bwd(q_att, k_att, v_att, do, lse_c, delta_c, B, Lp):
    T = B * Lp
    NQ = Lp // QB
    G = ATTN_BWD_HEADS
    NG = MLA_H // G
    HW = 2 * LANES
    scale = 1.0 / math.sqrt(MLA_QK)

    def body(q_ref, k_ref, v_ref, do_ref, lse_ref, dl_ref, dq_out, dk_out, dv_out, dq_ref, dk_ref, dv_ref):
        kj = pl.program_id(2)

        @pl.when(kj == 0)
        def _():
            dq_ref[...] = jnp.zeros_like(dq_ref)

        dk_ref[...] = jnp.zeros_like(dk_ref)
        dv_ref[...] = jnp.zeros_like(dv_ref)
        col = kj * QB + lax.broadcasted_iota(jnp.int32, (QB, QB), 0)
        rowi = lax.broadcasted_iota(jnp.int32, (QB, QB), 1)

        def step(qi, masked):
            off = pl.multiple_of(qi * QB, QB)
            ok = _attn_mask(qi * QB + rowi, col) if masked else None
            for h in range(G):
                ws = slice(h * HW, (h + 1) * HW)
                hs = slice(h * MLA_DV, (h + 1) * MLA_DV)
                qb = q_ref[pl.ds(off, QB), ws]
                dob = do_ref[pl.ds(off, QB), hs]
                kb = k_ref[:, ws]
                lse2 = lse_ref[0, h, pl.ds(qi, 1), :]
                delta = dl_ref[0, h, pl.ds(qi, 1), :]
                p_t = jnp.exp2(_nt(kb, qb) * (scale * LOG2E) - lse2)
                if masked:
                    p_t = jnp.where(ok, p_t, 0.0)
                dv_ref[:, hs] += _nn(p_t.astype(BF16), dob)
                ds_t = (p_t * (_nt(v_ref[:, hs], dob) - delta) * scale).astype(BF16)
                dk_ref[:, ws] += _nn(ds_t, qb)
                dq_ref[pl.ds(off, QB), ws] += _tn(ds_t, kb)

        def loop(masked):
            def it(qi, carry):
                step(qi, masked)
                return carry
            lax.fori_loop(kj + 1, NQ, it, 0)

        step(kj, True)
        pl.when(kj == 0)(lambda: loop(True))
        pl.when(kj > 0)(lambda: loop(False))
        dk_out[...] = dk_ref[...].astype(BF16)
        dv_out[...] = dv_ref[...].astype(BF16)

        @pl.when(kj == NQ - 1)
        def _():
            dq_out[...] = dq_ref[...].astype(BF16)

    blocks = (2 * _nbytes((Lp, G * HW), BF16) + _nbytes((Lp, G * MLA_DV), BF16) + 2 * _nbytes((QB, G * 384), BF16)
              + 2 * _nbytes((G, NQ, QB), F32))
    scratch = [pltpu.VMEM((Lp, G * HW), F32), pltpu.VMEM((QB, G * HW), F32), pltpu.VMEM((QB, G * MLA_DV), F32)]
    return pl.pallas_call(
        body,
        out_shape=(jax.ShapeDtypeStruct((T, MLA_H * HW), BF16), jax.ShapeDtypeStruct((T, MLA_H * HW), BF16),
                   jax.ShapeDtypeStruct((T, MLA_H * MLA_DV), BF16)),
        scratch_shapes=scratch,
        grid=(B, NG, NQ),
        in_specs=[
            pl.BlockSpec((Lp, G * HW), lambda b, g, j: (b, g), pipeline_mode=pl.Buffered(1)),
            pl.BlockSpec((QB, G * HW), lambda b, g, j: (b * NQ + j, g)),
            pl.BlockSpec((QB, G * MLA_DV), lambda b, g, j: (b * NQ + j, g)),
            pl.BlockSpec((Lp, G * MLA_DV), lambda b, g, j: (b, g), pipeline_mode=pl.Buffered(1)),
            pl.BlockSpec((1, G, NQ, QB), lambda b, g, j: (b, g, 0, 0)),
            pl.BlockSpec((1, G, NQ, QB), lambda b, g, j: (b, g, 0, 0)),
        ],
        out_specs=(pl.BlockSpec((Lp, G * HW), lambda b, g, j: (b, g), pipeline_mode=pl.Buffered(1)),
                   pl.BlockSpec((QB, G * HW), lambda b, g, j: (b * NQ + j, g)),
                   pl.BlockSpec((QB, G * MLA_DV), lambda b, g, j: (b * NQ + j, g))),
        compiler_params=_params(("parallel", "parallel", "arbitrary"), blocks,
                                _nbytes((Lp, G * HW), F32) + _nbytes((QB, G * 384), F32)),
        name="attn_bwd",
    )(q_att, k_att, v_att, do, lse_c, delta_c)


def _mla_bwd_post(dq, dk, dv, projB, cos_t, sin_t, gq, gkv, wuq2, wukv, B, Lp, tr):
    T = B * Lp
    nt = Lp // tr
    HW = 2 * LANES

    def body(dq_ref, dk_ref, dv_ref, pb_ref, cos_ref, sin_ref, gq_ref, gkv_ref, wuq_ref, wukv_ref,
             dqf_ref, dkvf_ref, de_ref, dgq_ref, dgkv_ref):
        first = (pl.program_id(0) == 0) & (pl.program_id(1) == 0)

        @pl.when(first)
        def _():
            dgq_ref[...] = jnp.zeros_like(dgq_ref)
            dgkv_ref[...] = jnp.zeros_like(dgkv_ref)

        cs = cos_ref[...]
        sn = sin_ref[...]
        rope_t = lambda t: t * cs + _swap_halves(t * sn)
        dkr = jnp.zeros((tr, LANES), F32)
        for h in range(MLA_H):
            dqf_ref[:, h * HW:h * HW + LANES] = dq_ref[:, h * HW:h * HW + LANES]
            dq_rope = dq_ref[:, h * HW + LANES:(h + 1) * HW].astype(F32)
            dqf_ref[:, h * HW + LANES:(h + 1) * HW] = rope_t(dq_rope).astype(BF16)
            dkvf_ref[:, h * HW:h * HW + LANES] = dk_ref[:, h * HW:h * HW + LANES]
            dkvf_ref[:, h * HW + LANES:(h + 1) * HW] = dv_ref[:, h * MLA_DV:(h + 1) * MLA_DV]
            dkr = dkr + dk_ref[:, h * HW + LANES:(h + 1) * HW].astype(F32)

        def norm_bwd(x, dn, g):
            r = lax.rsqrt(jnp.mean(x * x, axis=-1, keepdims=True) + EPS)
            xn = x * r
            t = dn * g
            return r * (t - xn * jnp.mean(t * xn, axis=-1, keepdims=True)), jnp.sum(dn * xn, axis=0, keepdims=True)

        cq = pb_ref[:, 0:Q_RANK].astype(F32)
        ckv = pb_ref[:, Q_RANK:Q_RANK + KV_RANK].astype(F32)
        dcq, dgq = norm_bwd(cq, _nt(dqf_ref[...], wuq_ref[...]), gq_ref[...])
        dckv, dgkv = norm_bwd(ckv, _nt(dkvf_ref[...], wukv_ref[...]), gkv_ref[...])
        dgq_ref[...] += dgq
        dgkv_ref[...] += dgkv
        de_ref[:, 0:Q_RANK] = dcq.astype(BF16)
        de_ref[:, Q_RANK:Q_RANK + KV_RANK] = dckv.astype(BF16)
        de_ref[:, 384:512] = rope_t(dkr).astype(BF16)

    rows = lambda w: pl.BlockSpec((tr, w), lambda b, j: (b * nt + j, 0))
    const = lambda s: pl.BlockSpec(s, lambda b, j: (0, 0))
    blocks = (2 * _nbytes((tr, 2048), F32) + _nbytes((tr, 1024), F32) + _nbytes((tr, 640), F32)
              + 2 * _nbytes((tr, 2048), BF16) + _nbytes((2048, 384), BF16) + 2 * _nbytes((tr, 2048), F32))
    return pl.pallas_call(
        body,
        out_shape=(jax.ShapeDtypeStruct((T, 2048), BF16), jax.ShapeDtypeStruct((T, 2048), BF16),
                   jax.ShapeDtypeStruct((T, 512), BF16), jax.ShapeDtypeStruct((1, Q_RANK), F32),
                   jax.ShapeDtypeStruct((1, KV_RANK), F32)),
        grid=(B, nt),
        in_specs=[rows(2048), rows(2048), rows(1024), rows(640),
                  pl.BlockSpec((tr, 128), lambda b, j: (j, 0)), pl.BlockSpec((tr, 128), lambda b, j: (j, 0)),
                  const((1, Q_RANK)), const((1, KV_RANK)), const((Q_RANK, 2048)), const((KV_RANK, 2048))],
        out_specs=(rows(2048), rows(2048), rows(512), const((1, Q_RANK)), const((1, KV_RANK))),
        compiler_params=_params(("arbitrary", "arbitrary"), blocks),
        name="mla_bwd_post",
    )(dq, dk, dv, projB, cos_t, sin_t, gq, gkv, wuq2, wukv)


def _in_proj_bwd(x, meta, dh1, dA, dBz, dC, dDz, dE, wA, wB, g, B, Lp):
    NQ = Lp // QB
    seq = x.shape[1]
    R = 2 if B % 2 == 0 else 1
    M = R * QB

    def body(x_ref, meta_ref, dh_ref, da_ref, db_ref, dc_ref, dd_ref, de_ref, wa_ref, wb_ref, g_ref,
             gx_ref, dmeta_ref, dg_ref):
        b = pl.program_id(0)
        j = pl.program_id(1)

        @pl.when((b == 0) & (j == 0))
        def _():
            dg_ref[...] = jnp.zeros_like(dg_ref)

        flat = lambda ref: ref[...].reshape(M, ref.shape[-1])
        da, dbz, dc, dd, de = flat(da_ref), flat(db_ref), flat(dc_ref), flat(dd_ref), flat(de_ref)
        du = _nt(da, wa_ref[:, 3072:5120])
        du = du + _nt(dbz, wa_ref[:, 1024:2048])
        du = du + _nt(dd, wa_ref[:, 2048:3072])
        du = du + _nt(dc[:, 0:1024], wa_ref[:, 0:1024])
        du = du + _nt(dc[:, 1024:2048], wa_ref[:, 5120:6144])
        du = du + _nt(dc[:, 2048:2176], wb_ref[:, 384:512])
        du = du + _nt(de[:, 0:384], wb_ref[:, 0:384])
        du = du + _nt(de[:, 384:512], wb_ref[:, 512:640])

        head = jnp.concatenate([jnp.zeros((FRONT, D), F32), meta_ref[...]], axis=0)
        x = jnp.concatenate([jnp.where(j > 0, x_ref[i], head) for i in range(R)], axis=0)
        r = lax.rsqrt(jnp.mean(x * x, axis=-1, keepdims=True) + EPS)
        xn = x * r
        t = du * g_ref[...]
        dh0 = flat(dh_ref).astype(F32) + r * (t - xn * jnp.mean(t * xn, axis=-1, keepdims=True))
        dg_ref[...] += jnp.sum(du * xn, axis=0, keepdims=True)
        dmeta = dh0[FRONT:HEAD_ROWS, :]
        for i in range(R):
            gx_ref[i] = dh0[i * QB:(i + 1) * QB, :]
            if i > 0:
                dmeta = dmeta + dh0[i * QB + FRONT:i * QB + HEAD_ROWS, :]

        @pl.when((j == 0) & (b == 0))
        def _():
            dmeta_ref[...] = dmeta

        @pl.when((j == 0) & (b > 0))
        def _():
            dmeta_ref[...] += dmeta

    rows = lambda w: pl.BlockSpec((R, QB, w), lambda b, j: (b, j, 0))
    x_rows = pl.BlockSpec((R, QB, D), lambda b, j: (b, jnp.maximum(j - 1, 0), 0))
    const = lambda s: pl.BlockSpec(s, lambda b, j: (0,) * len(s))
    resident = lambda s: pl.BlockSpec(s, lambda b, j: (0, 0), pipeline_mode=pl.Buffered(1))
    by_row = lambda a: a.reshape(B, Lp, a.shape[1])
    widths = [a.shape[1] for a in (dA, dBz, dC, dDz, dE)]
    blocks = sum(_nbytes((M, w), BF16) for w in widths) + 4 * _nbytes((M, D), F32)
    return pl.pallas_call(
        body,
        out_shape=(jax.ShapeDtypeStruct((B, seq, D), F32), jax.ShapeDtypeStruct((N_META, D), F32),
                   jax.ShapeDtypeStruct((1, D), F32)),
        grid=(B // R, NQ),
        in_specs=[x_rows, const((N_META, D)), rows(D)] + [rows(w) for w in widths]
        + [resident(wA.shape), resident(wB.shape), const((1, D))],
        out_specs=(x_rows, const((N_META, D)), const((1, D))),
        compiler_params=_params(("arbitrary", "arbitrary"), blocks, _nbytes(wA.shape, BF16) + _nbytes(wB.shape, BF16)),
        name="in_proj_bwd",
    )(x, meta, by_row(dh1), *[by_row(a) for a in (dA, dBz, dC, dDz, dE)], wA, wB, g)


_VMEM_WHOLE = pl.BlockSpec(memory_space=pltpu.VMEM)


def _params_whole(arrays):
    total = sum(_nbytes(a.shape, a.dtype) for a in arrays)
    return pltpu.CompilerParams(vmem_limit_bytes=int(min(total + 12 * 1024 * 1024, VMEM_CAP_V7X)))


def _wire_dtype(shape):
    return BF16 if shape[-2] * shape[-1] >= WIRE_BF16_MIN_ELEMS else F32


def _pair_add_big(gp, recv, c):
    _, half, cols = recv.shape
    th = _div_tile(half, 64, 16)
    out_dtype = _wire_dtype(recv.shape)

    steps = half // th

    def body(c_ref, a_ref, b_ref, o_ref):
        o_ref[...] = (a_ref[...] + b_ref[...]).astype(out_dtype)

    return pl.pallas_call(
        body,
        out_shape=jax.ShapeDtypeStruct(recv.shape, out_dtype),
        grid_spec=pltpu.PrefetchScalarGridSpec(
            num_scalar_prefetch=1,
            grid=(steps,),
            in_specs=[pl.BlockSpec((4, th, cols), lambda i, c_ref: (0, c_ref[0] * steps + i, 0)),
                      pl.BlockSpec((4, th, cols), lambda i, c_ref: (0, i, 0))],
            out_specs=pl.BlockSpec((4, th, cols), lambda i, c_ref: (0, i, 0)),
        ),
        compiler_params=_params(("parallel",), 3 * _nbytes((4, th, cols), F32)),
        name="grad_pair_add_big",
    )(c, gp, recv)


def _pair_add_small(gps, recvs):
    n = len(gps)

    def body(*refs):
        c = lax.axis_index("c")
        for t in range(n):
            g_ref, r_ref, o_ref = refs[t], refs[n + t], refs[2 * n + t]
            half = r_ref.shape[1]
            s = g_ref[:, pl.ds(pl.multiple_of(c * half, 8), half), :] + r_ref[...]
            o_ref[...] = s.astype(o_ref.dtype)

    return pl.pallas_call(
        body,
        out_shape=[jax.ShapeDtypeStruct(r.shape, _wire_dtype(r.shape)) for r in recvs],
        in_specs=[_VMEM_WHOLE] * (2 * n),
        out_specs=[_VMEM_WHOLE] * n,
        compiler_params=_params_whole(list(gps) + 2 * list(recvs)),
        name="grad_pair_add_small",
    )(*gps, *recvs)


def _chip_order_sum(landed_ref, own_ref, me):
    p = [jnp.where(me == k, own_ref[k], landed_ref[k]).astype(F32) for k in range(4)]
    return ((p[0] + p[1]) + p[2]) + p[3]


def _sum_chips_big(landed, own, pos):
    _, half, cols = landed.shape
    th = _div_tile(half, 64, 16)

    def body(pos_ref, l_ref, s_ref, o_ref):
        o_ref[0] = _chip_order_sum(l_ref, s_ref, pos_ref[1])

    spec = pl.BlockSpec((4, th, cols), lambda i, pos_ref: (0, i, 0))
    return pl.pallas_call(
        body,
        out_shape=jax.ShapeDtypeStruct((2, half, cols), F32),
        grid_spec=pltpu.PrefetchScalarGridSpec(
            num_scalar_prefetch=1,
            grid=(half // th,),
            in_specs=[spec, spec],
            out_specs=pl.BlockSpec((1, th, cols), lambda i, pos_ref: (pos_ref[0], i, 0)),
        ),
        compiler_params=_params(("parallel",), 3 * _nbytes((4, th, cols), F32)),
        name="grad_sum_chips_big",
    )(pos, landed, own)


def _sum_chips_small(landed, own):
    n = len(landed)

    def body(*refs):
        x, y, c = _mesh_pos()
        for t in range(n):
            refs[2 * n + t][c] = _chip_order_sum(refs[t], refs[n + t], 2 * x + y)

    return pl.pallas_call(
        body,
        out_shape=[jax.ShapeDtypeStruct((2,) + p.shape[1:], F32) for p in landed],
        in_specs=[_VMEM_WHOLE] * (2 * n),
        out_specs=[_VMEM_WHOLE] * n,
        compiler_params=_params_whole(list(landed) * 3),
        name="grad_sum_chips_small",
    )(*landed, *own)


def _adamw_update(w_ref, g_ref, m_ref, v_ref, d_ref, mo_ref, vo_ref):
    c1 = 1.0 - ADAM_B1 ** ADAM_STEP
    c2 = 1.0 - ADAM_B2 ** ADAM_STEP
    gv = g_ref[...]
    mn = ADAM_B1 * m_ref[...] + (1.0 - ADAM_B1) * gv
    vn = ADAM_B2 * v_ref[...] + (1.0 - ADAM_B2) * (gv * gv)
    mo_ref[...] = mn
    vo_ref[...] = vn
    d_ref[...] = -ADAM_LR * ((mn / c1) / (jnp.sqrt(vn / c2) + ADAM_EPS) + ADAM_WD * w_ref[...])


def _adamw_big(w, g, m, v):
    lead, (rows, cols) = w.shape[:-2], w.shape[-2:]
    assert all(n == 1 for n in lead)
    tr = _div_tile(rows, (1 << 19) // cols, 8)
    spec = pl.BlockSpec((1,) * len(lead) + (tr, cols), lambda i: (0,) * len(lead) + (i, 0))
    shp = jax.ShapeDtypeStruct(w.shape, F32)
    return pl.pallas_call(
        functools.partial(_adamw_update),
        out_shape=(shp, shp, shp),
        grid=(rows // tr,),
        in_specs=[spec] * 4,
        out_specs=(spec, spec, spec),
        compiler_params=_params(("parallel",), 7 * _nbytes((tr, cols), F32)),
        name="adamw_big",
    )(w, g, m, v)


def _adamw_small(ws, gs, ms, vs):
    n = len(ws)

    def body(*refs):
        for t in range(n):
            _adamw_update(refs[t], refs[n + t], refs[2 * n + t], refs[3 * n + t],
                          refs[4 * n + t], refs[5 * n + t], refs[6 * n + t])

    shapes = [jax.ShapeDtypeStruct(w.shape, F32) for w in ws]
    return pl.pallas_call(
        body,
        out_shape=shapes * 3,
        in_specs=[_VMEM_WHOLE] * (4 * n),
        out_specs=[_VMEM_WHOLE] * (3 * n),
        compiler_params=_params_whole(list(ws) * 7),
        name="adamw_small",
    )(*ws, *gs, *ms, *vs)


def _mesh_pos():
    return lax.axis_index("x"), lax.axis_index("y"), lax.axis_index("c")


def _other_chips(x, y):
    return [(1 - x, y), (x, 1 - y), (1 - x, 1 - y)]


_ANY = pl.BlockSpec(memory_space=pl.ANY)


PAIR_SPLIT_MIN_ROWS = 64


def _weight_gather(shards):
    n = len(shards)
    split = [s.shape[0] >= PAIR_SPLIT_MIN_ROWS for s in shards]

    def body(*refs):
        w_refs, o_refs = refs[:n], refs[n:2 * n]
        send_sems, recv_sems = refs[2 * n:]
        x, y, c = _mesh_pos()
        me = 2 * x + y
        chips = _other_chips(x, y)

        def rows_of(t, core):
            rows = shards[t].shape[0]
            if not split[t]:
                return pl.ds(0, rows)
            return pl.ds(pl.multiple_of(core * (rows // 2), 16), rows // 2)

        def landed(t, k, slot, rows, to):
            ref = o_refs[t].at[slot, rows]
            return pltpu.make_async_remote_copy(src_ref=ref, dst_ref=ref, send_sem=send_sems.at[6 * t + k],
                                                recv_sem=recv_sems.at[6 * t + k], device_id=to, device_id_type=MESH)

        sends = []
        for t in range(n):
            mine = rows_of(t, c)
            for k, (px, py) in enumerate(chips):
                cp = pltpu.make_async_remote_copy(src_ref=w_refs[t].at[mine], dst_ref=o_refs[t].at[me, mine],
                                                  send_sem=send_sems.at[6 * t + k], recv_sem=recv_sems.at[6 * t + k],
                                                  device_id=(px, py, c), device_id_type=MESH)
                cp.start()
                sends.append(cp)
        for t in range(n):
            mine = rows_of(t, c)
            for k, (px, py) in enumerate(chips):
                landed(t, k, 2 * px + py, mine, (x, y, c)).wait_recv()
                if split[t]:
                    cp = landed(t, 3 + k, 2 * px + py, mine, (x, y, 1 - c))
                    cp.start()
                    sends.append(cp)
        for t in range(n):
            if split[t]:
                for k, (px, py) in enumerate(chips):
                    landed(t, 3 + k, 2 * px + py, rows_of(t, 1 - c), (x, y, c)).wait_recv()
        for cp in sends:
            cp.wait_send()

    return pl.pallas_call(
        body,
        out_shape=[jax.ShapeDtypeStruct((4,) + s.shape, s.dtype) for s in shards],
        in_specs=[_ANY] * n,
        out_specs=[_ANY] * n,
        scratch_shapes=[pltpu.SemaphoreType.DMA((6 * n,)), pltpu.SemaphoreType.DMA((6 * n,))],
        name="weight_gather",
    )(*shards)


def _pair_swap(gps):
    n = len(gps)

    def body(*refs):
        g_refs, o_refs = refs[:n], refs[n:2 * n]
        send_sems, recv_sems = refs[2 * n:]
        x, y, c = _mesh_pos()
        copies = []
        for t in range(n):
            half = gps[t].shape[1] // 2
            theirs = pl.ds(pl.multiple_of((1 - c) * half, 8), half)
            cp = pltpu.make_async_remote_copy(src_ref=g_refs[t].at[:, theirs], dst_ref=o_refs[t],
                                              send_sem=send_sems.at[t], recv_sem=recv_sems.at[t],
                                              device_id=(x, y, 1 - c), device_id_type=MESH)
            cp.start()
            copies.append(cp)
        for cp in copies:
            cp.wait_send()
            cp.wait_recv()

    return pl.pallas_call(
        body,
        out_shape=[jax.ShapeDtypeStruct((4, g.shape[1] // 2, g.shape[2]), g.dtype) for g in gps],
        in_specs=[_ANY] * n,
        out_specs=[_ANY] * n,
        scratch_shapes=[pltpu.SemaphoreType.DMA((n,)), pltpu.SemaphoreType.DMA((n,))],
        name="grad_pair_swap",
    )(*gps)


_HBM = pl.BlockSpec(memory_space=pltpu.HBM)
_SEM = pl.BlockSpec(memory_space=pltpu.SEMAPHORE)


def _in_hbm(a):
    return pltpu.with_memory_space_constraint(a, pltpu.HBM)


def _chip_scatter_start(parts):
    n = len(parts)

    def body(*refs):
        s_refs, l_refs = refs[:n], refs[n:2 * n]
        send_sems, recv_sems = refs[2 * n], refs[2 * n + 1]
        token = refs[-1]
        x, y, c = _mesh_pos()
        me = 2 * x + y
        for t in range(n):
            for k, (px, py) in enumerate(_other_chips(x, y)):
                pltpu.make_async_remote_copy(src_ref=s_refs[t].at[2 * px + py], dst_ref=l_refs[t].at[me],
                                             send_sem=send_sems.at[3 * t + k], recv_sem=recv_sems.at[3 * t + k],
                                             device_id=(px, py, c), device_id_type=MESH).start()
        token[...] = jnp.zeros_like(token)

    hbm = [pltpu.HBM(p.shape, p.dtype) for p in parts]
    outs = pl.pallas_call(
        body,
        name="grad_scatter_start",
        out_shape=(pltpu.SemaphoreType.DMA((3 * n,)), pltpu.SemaphoreType.DMA((3 * n,)), *hbm, *hbm,
                   jax.ShapeDtypeStruct((8, LANES), F32)),
        in_specs=[_HBM] * (2 * n),
        out_specs=(_SEM, _SEM, *([_HBM] * (2 * n)), pl.BlockSpec(memory_space=pltpu.VMEM)),
        input_output_aliases={i: 2 + i for i in range(2 * n)},
        compiler_params=pltpu.CompilerParams(has_side_effects=pltpu.SideEffectType.DATAFLOW_SIDE_EFFECTING),
    )(*[_in_hbm(p) for p in parts], *[_in_hbm(lax.empty(p.shape, p.dtype)) for p in parts])
    return outs[0], outs[1], list(outs[2:2 + n]), list(outs[2 + n:2 + 2 * n]), outs[-1]


def _chip_scatter_wait(send_sems, recv_sems, parts, lands, after):
    n = len(parts)

    def body(*refs):
        s_refs, l_refs = refs[:n], refs[n:2 * n]
        send_sems, recv_sems = refs[2 * n], refs[2 * n + 1]
        x, y, c = _mesh_pos()
        me = 2 * x + y
        for t in range(n):
            for k, (px, py) in enumerate(_other_chips(x, y)):
                cp = pltpu.make_async_remote_copy(src_ref=s_refs[t].at[2 * px + py], dst_ref=l_refs[t].at[2 * px + py],
                                                  send_sem=send_sems.at[3 * t + k], recv_sem=recv_sems.at[3 * t + k],
                                                  device_id=(x, y, c), device_id_type=MESH)
                cp.wait_send()
                cp.wait_recv()

    hbm = [pltpu.HBM(p.shape, p.dtype) for p in parts]
    outs = pl.pallas_call(
        body,
        name="grad_scatter_wait",
        out_shape=(*hbm, *hbm),
        in_specs=[_HBM] * (2 * n) + [_SEM, _SEM, _ANY],
        out_specs=[_HBM] * (2 * n),
        input_output_aliases={i: i for i in range(2 * n)},
        compiler_params=pltpu.CompilerParams(has_side_effects=pltpu.SideEffectType.DATAFLOW_SIDE_EFFECTING),
    )(*parts, *lands, send_sems, recv_sems, after)
    return list(outs[:n]), list(outs[n:])


def _late_gather_start(shards):
    n = len(shards)

    def body(*refs):
        w_refs, l_refs = refs[:n], refs[n:2 * n]
        send_sems, recv_sems = refs[2 * n], refs[2 * n + 1]
        token = refs[-1]
        x, y, c = _mesh_pos()
        me = 2 * x + y
        for t in range(n):
            for k, (px, py) in enumerate(_other_chips(x, y)):
                pltpu.make_async_remote_copy(src_ref=w_refs[t], dst_ref=l_refs[t].at[me],
                                             send_sem=send_sems.at[3 * t + k], recv_sem=recv_sems.at[3 * t + k],
                                             device_id=(px, py, c), device_id_type=MESH).start()
        token[...] = jnp.zeros_like(token)

    src = [pltpu.HBM(s.shape, s.dtype) for s in shards]
    land = [pltpu.HBM((4,) + s.shape, s.dtype) for s in shards]
    outs = pl.pallas_call(
        body,
        name="late_gather_start",
        out_shape=(pltpu.SemaphoreType.DMA((3 * n,)), pltpu.SemaphoreType.DMA((3 * n,)), *src, *land,
                   jax.ShapeDtypeStruct((8, LANES), F32)),
        in_specs=[_HBM] * (2 * n),
        out_specs=(_SEM, _SEM, *([_HBM] * (2 * n)), pl.BlockSpec(memory_space=pltpu.VMEM)),
        input_output_aliases={i: 2 + i for i in range(2 * n)},
        compiler_params=pltpu.CompilerParams(has_side_effects=pltpu.SideEffectType.DATAFLOW_SIDE_EFFECTING),
    )(*[_in_hbm(s) for s in shards], *[_in_hbm(lax.empty((4,) + s.shape, s.dtype)) for s in shards])
    return outs[0], outs[1], list(outs[2:2 + n]), list(outs[2 + n:2 + 2 * n]), outs[-1]


def _late_gather_wait(send_sems, recv_sems, shards, lands, after):
    n = len(shards)

    def body(*refs):
        w_refs, l_refs = refs[:n], refs[n:2 * n]
        send_sems, recv_sems = refs[2 * n], refs[2 * n + 1]
        x, y, c = _mesh_pos()
        for t in range(n):
            for k, (px, py) in enumerate(_other_chips(x, y)):
                cp = pltpu.make_async_remote_copy(src_ref=w_refs[t], dst_ref=l_refs[t].at[2 * px + py],
                                                  send_sem=send_sems.at[3 * t + k], recv_sem=recv_sems.at[3 * t + k],
                                                  device_id=(x, y, c), device_id_type=MESH)
                cp.wait_send()
                cp.wait_recv()

    src = [pltpu.HBM(s.shape, s.dtype) for s in shards]
    land = [pltpu.HBM(l.shape, l.dtype) for l in lands]
    outs = pl.pallas_call(
        body,
        name="late_gather_wait",
        out_shape=(*src, *land),
        in_specs=[_HBM] * (2 * n) + [_SEM, _SEM, _ANY],
        out_specs=[_HBM] * (2 * n),
        input_output_aliases={i: i for i in range(2 * n)},
        compiler_params=pltpu.CompilerParams(has_side_effects=pltpu.SideEffectType.DATAFLOW_SIDE_EFFECTING),
    )(*shards, *lands, send_sems, recv_sems, after)
    return list(outs[n:])


def _all_to_all_small(parts):
    n = len(parts)

    def body(*refs):
        p_refs, o_refs = refs[:n], refs[n:2 * n]
        send_sems, recv_sems = refs[2 * n:]
        x, y, c = _mesh_pos()
        me = 4 * x + 2 * y + c
        sends = []
        for t in range(n):
            for k in range(1, 8):
                px, py, pc = x ^ (k >> 2), y ^ ((k >> 1) & 1), c ^ (k & 1)
                cp = pltpu.make_async_remote_copy(src_ref=p_refs[t], dst_ref=o_refs[t].at[me],
                                                  send_sem=send_sems.at[7 * t + k - 1], recv_sem=recv_sems.at[7 * t + k - 1],
                                                  device_id=(px, py, pc), device_id_type=MESH)
                cp.start()
                sends.append(cp)
        for t in range(n):
            for k in range(1, 8):
                peer = 4 * (x ^ (k >> 2)) + 2 * (y ^ ((k >> 1) & 1)) + (c ^ (k & 1))
                pltpu.make_async_remote_copy(src_ref=p_refs[t], dst_ref=o_refs[t].at[peer],
                                             send_sem=send_sems.at[7 * t + k - 1], recv_sem=recv_sems.at[7 * t + k - 1],
                                             device_id=(x, y, c), device_id_type=MESH).wait_recv()
        for cp in sends:
            cp.wait_send()

    return pl.pallas_call(
        body,
        out_shape=[jax.ShapeDtypeStruct((8,) + p.shape, p.dtype) for p in parts],
        in_specs=[_ANY] * n,
        out_specs=[_ANY] * n,
        scratch_shapes=[pltpu.SemaphoreType.DMA((7 * n,)), pltpu.SemaphoreType.DMA((7 * n,))],
        name="grad_small_all_to_all",
    )(*parts)


def _sum_devices_small(landed, own):
    n = len(landed)

    def body(*refs):
        x, y, c = _mesh_pos()
        me = 4 * x + 2 * y + c
        for t in range(n):
            acc = jnp.where(me == 0, refs[n + t][...], refs[t][0])
            for d in range(1, 8):
                acc = acc + jnp.where(me == d, refs[n + t][...], refs[t][d])
            refs[2 * n + t][...] = acc

    return pl.pallas_call(
        body,
        out_shape=[jax.ShapeDtypeStruct(p.shape, F32) for p in own],
        in_specs=[_VMEM_WHOLE] * (2 * n),
        out_specs=[_VMEM_WHOLE] * n,
        compiler_params=_params_whole(list(landed) + 2 * list(own)),
        name="grad_sum_devices_small",
    )(*landed, *own)


def _pair_join(fs):
    n = len(fs)

    def body(*refs):
        f_refs, o_refs = refs[:n], refs[n:2 * n]
        send_sems, recv_sems = refs[2 * n:]
        x, y, c = _mesh_pos()
        sends = []
        for t in range(n):
            cp = pltpu.make_async_remote_copy(src_ref=f_refs[t].at[c], dst_ref=o_refs[t].at[c], send_sem=send_sems.at[t],
                                              recv_sem=recv_sems.at[t], device_id=(x, y, 1 - c), device_id_type=MESH)
            cp.start()
            sends.append(cp)
        for t in range(n):
            pltpu.make_async_remote_copy(src_ref=f_refs[t].at[c], dst_ref=o_refs[t].at[1 - c], send_sem=send_sems.at[t],
                                         recv_sem=recv_sems.at[t], device_id=(x, y, c), device_id_type=MESH).wait_recv()
        for cp in sends:
            cp.wait_send()

    return pl.pallas_call(
        body,
        out_shape=[jax.ShapeDtypeStruct(f.shape, f.dtype) for f in fs],
        in_specs=[_ANY] * n,
        out_specs=[_ANY] * n,
        input_output_aliases={t: t for t in range(n)},
        scratch_shapes=[pltpu.SemaphoreType.DMA((n,)), pltpu.SemaphoreType.DMA((n,))],
        name="grad_pair_join",
    )(*fs)


def _rope_tables(Lp):
    inv = 1.0 / (ROPE_BASE ** (jnp.arange(0, ROPE, 2, dtype=F32) / ROPE))
    ang = (jnp.arange(Lp, dtype=F32) - FRONT)[:, None] * inv[None, :]
    cs, sn = jnp.cos(ang), jnp.sin(ang)
    return jnp.tile(cs, (1, 4)), jnp.concatenate([-sn, sn, -sn, sn], axis=1)


def _local_step(x, loss_target, meta, norm_g, w_in, gate_w, gate_b, gla_norm_g, gla_proj, q_norm_g, w_uq,
                kv_norm_g, w_ukv, mla_proj, w_out, final_norm_g, early_grads_hook=None, late_weights_hook=None):
    B, seq, _ = x.shape
    Lp = HEAD_ROWS + seq
    T = B * Lp
    tr = _div_tile(Lp, 544, 16)
    tkw = _div_tile(T, Lp, QB)

    cuts = np.cumsum((0,) + SPLITS)
    shard_w = IN_WIDTH // 4

    def w_cols(i, width=None):
        parts = []
        for j in range(4):
            a, b = max(cuts[i], j * shard_w), min(cuts[i + 1], (j + 1) * shard_w)
            if a < b:
                parts.append(w_in[j][:, a - j * shard_w:b - j * shard_w])
        if width is not None:
            parts.append(jnp.zeros((D, width - (cuts[i + 1] - cuts[i])), w_in.dtype))
        return parts

    i_q, i_k, i_v, i_lr, i_z, i_cq, i_ckv, i_kr, i_mz, i_gg, i_gm = range(11)
    wA = jnp.concatenate(sum([w_cols(i) for i in (i_v, i_z, i_mz, i_gg, i_gm, i_q, i_k)], []), axis=1)
    wB = jnp.concatenate(w_cols(i_cq) + w_cols(i_ckv) + w_cols(i_lr, 128) + w_cols(i_kr, 128), axis=1)
    gn4 = jnp.tile(gla_norm_g, (1, GLA_H))
    cos_t, sin_t = _rope_tables(Lp)

    u = _rms_in(x, meta, norm_g, B, Lp)
    projA = _mm(u, wA, name="in_proj_a", out_dtype=BF16, tm=tkw, tn=1024, tk=D)
    projB = _mm(u, wB, name="in_proj_b", out_dtype=BF16, tm=tkw, tn=640, tk=D)
    if late_weights_hook is not None:
        gate_w, gla_proj, w_uq, w_ukv, mla_proj, w_out = late_weights_hook(projB)
    wg = jnp.pad(gate_w, ((0, 128 - GLA_RANK), (0, 0)))
    wuq2 = jnp.pad(w_uq.reshape(Q_RANK, MLA_H, MLA_QK), ((0, 0), (0, 0), (0, 256 - MLA_QK))).reshape(Q_RANK, 2048)
    oa, ya_in, ssave = _gla_fwd(projA, projB, wg, gate_b, gn4, B, Lp)
    ya = _mm(ya_in, gla_proj, name="gla_proj", out_dtype=BF16, tm=tkw, tn=D, tk=D)
    q_att, k_att, v_att, cqn, ckvn = _mla_prep(projB, cos_t, sin_t, q_norm_g, kv_norm_g, wuq2, w_ukv, B, Lp, tr)
    ob, yb_in, lse_c = _attn_fwd(q_att, k_att, v_att, projA, B, Lp)
    yb = _mm(yb_in, mla_proj, name="mla_proj", out_dtype=BF16, tm=tkw, tn=D, tk=D)
    dh1_b, merged, loss, d_gf = _out_proj_loss(x, meta, projA, ya, yb, w_out, final_norm_g.reshape(1, D),
                                                loss_target, B, Lp)

    g_w_out = _mm(merged, dh1_b, name="dw_out", trans_a=True, tm=D, tn=D, tk=tkw)
    dya, dyb, dA = _merge_bwd(dh1_b, w_out, projA, ya, yb, tr)
    g_gla_proj = _mm(ya_in, dya, name="dw_gla_proj", trans_a=True, tm=D, tn=D, tk=tkw)
    g_mla_proj = _mm(yb_in, dyb, name="dw_mla_proj", trans_a=True, tm=D, tn=D, tk=tkw)
    doa, dBz, d_gn = _gla_out_bwd(dya, gla_proj, oa, projA, gn4, tr)
    dC, g_wg, d_bg = _gla_bwd(projA, projB, ssave, doa, wg, gate_b, B, Lp)
    do, dDz, delta_c = _attn_bwd_pre(dyb, mla_proj, projA, ob, B, Lp)
    dq, dk, dv = _attn_bwd(q_att, k_att, v_att, do, lse_c, delta_c, B, Lp)
    dqf, dkvf, dE, d_gq, d_gkv = _mla_bwd_post(dq, dk, dv, projB, cos_t, sin_t, q_norm_g, kv_norm_g,
                                                wuq2, w_ukv, B, Lp, tr)
    g_wuq2 = _mm(cqn, dqf, name="dw_uq", trans_a=True, tm=Q_RANK, tn=2048, tk=tkw)
    g_wukv = _mm(ckvn, dkvf, name="dw_ukv", trans_a=True, tm=KV_RANK, tn=2048, tk=tkw)
    dparts = [dA, dBz, dC, dDz, dE]
    g_in = [_mm(u, dp, name="dw_in_%d" % i, trans_a=True, tm=D, tn=_div_tile(dp.shape[1], 1024, 256), tk=tkw)
            for i, dp in enumerate(dparts)]

    gA, gBz, gC, gDz, gE = g_in
    src = [(gC, 1024), (gC, 1536), (gC, 0), (gC, 2048), (gBz, 0), (gE, 0), (gE, Q_RANK), (gE, 384), (gDz, 0),
           (gA, 0), (gA, D)]
    owners = []
    for j in range(4):
        parts = []
        for i, (arr, off) in enumerate(src):
            a, b = max(cuts[i], j * shard_w), min(cuts[i + 1], (j + 1) * shard_w)
            if a < b:
                parts.append(arr[:, off + a - cuts[i]:off + b - cuts[i]])
        owners.append(jnp.concatenate(parts, axis=1))
    g_w_in = jnp.stack(owners)
    g_wuq = g_wuq2.reshape(Q_RANK, MLA_H, 256)[:, :, :MLA_QK].reshape(Q_RANK, MLA_H * MLA_QK)
    grads = dict(w_in=g_w_in, gla_gate_w=g_wg[:GLA_RANK], gla_proj=g_gla_proj, mla_w_uq=g_wuq, mla_w_ukv=g_wukv,
                 mla_proj=g_mla_proj, w_out=g_w_out, gla_gate_b=d_bg,
                 gla_norm_g=d_gn, mla_q_norm_g=d_gq, mla_kv_norm_g=d_gkv, final_norm_g=d_gf)
    token = None if early_grads_hook is None else early_grads_hook(grads)
    ng = norm_g if token is None else norm_g + token[0:1, 0:1]
    grad_x, d_meta, d_ng = _in_proj_bwd(x, meta, dh1_b, dA, dBz, dC, dDz, dE, wA, wB, ng, B, Lp)
    grads.update(meta_tokens=d_meta, norm_g=d_ng)
    return loss[0, 0], grad_x, grads


_MATS = ("w_in", "gla_gate_w", "gla_proj", "mla_w_uq", "mla_w_ukv", "mla_proj", "w_out")
_ROW_SHARDED = ("gla_proj", "mla_proj", "w_out")
_ORDER = ("meta_tokens", "norm_g", "w_in", "gla_gate_w", "gla_gate_b", "gla_norm_g", "gla_proj", "mla_q_norm_g",
          "mla_w_uq", "mla_kv_norm_g", "mla_w_ukv", "mla_proj", "w_out", "final_norm_g")
WIRE_BF16_MIN_ELEMS = 128 * 128
SMALL_PACK_ROWS = 16


def _pack_small(d):
    rows = [jnp.pad(d[n].reshape(1, size), ((0, 0), (0, D - size))) for n, size in SMALL]
    return jnp.pad(jnp.concatenate(rows, axis=0), ((0, SMALL_PACK_ROWS - len(rows)), (0, 0)))


def _unpack_small(packed):
    return {n: packed[i, :size] for i, (n, size) in enumerate(SMALL)}


def kernel(x, meta_tokens, norm_g, w_in, gla_gate_w, gla_gate_b, gla_norm_g, gla_proj, mla_q_norm_g, mla_w_uq, mla_kv_norm_g, mla_w_ukv, mla_proj, w_out, final_norm_g, loss_target, m_meta_tokens, m_norm_g, m_w_in, m_gla_gate_w, m_gla_gate_b, m_gla_norm_g, m_gla_proj, m_mla_q_norm_g, m_mla_w_uq, m_mla_kv_norm_g, m_mla_w_ukv, m_mla_proj, m_w_out, m_final_norm_g, v_meta_tokens, v_norm_g, v_w_in, v_gla_gate_w, v_gla_gate_b, v_gla_norm_g, v_gla_proj, v_mla_q_norm_g, v_mla_w_uq, v_mla_kv_norm_g, v_mla_w_ukv, v_mla_proj, v_w_out, v_final_norm_g):
    w = dict(meta_tokens=meta_tokens, norm_g=norm_g, w_in=w_in[0], gla_gate_w=gla_gate_w[0], gla_gate_b=gla_gate_b,
             gla_norm_g=gla_norm_g, gla_proj=gla_proj[0], mla_q_norm_g=mla_q_norm_g, mla_w_uq=mla_w_uq[0],
             mla_kv_norm_g=mla_kv_norm_g, mla_w_ukv=mla_w_ukv[0], mla_proj=mla_proj[0], w_out=w_out[0],
             final_norm_g=final_norm_g)
    mom = dict(meta_tokens=m_meta_tokens, norm_g=m_norm_g, w_in=m_w_in[0], gla_gate_w=m_gla_gate_w[0],
               gla_gate_b=m_gla_gate_b, gla_norm_g=m_gla_norm_g, gla_proj=m_gla_proj[0], mla_q_norm_g=m_mla_q_norm_g,
               mla_w_uq=m_mla_w_uq[0], mla_kv_norm_g=m_mla_kv_norm_g, mla_w_ukv=m_mla_w_ukv[0], mla_proj=m_mla_proj[0],
               w_out=m_w_out[0], final_norm_g=m_final_norm_g)
    var = dict(meta_tokens=v_meta_tokens, norm_g=v_norm_g, w_in=v_w_in[0], gla_gate_w=v_gla_gate_w[0],
               gla_gate_b=v_gla_gate_b, gla_norm_g=v_gla_norm_g, gla_proj=v_gla_proj[0], mla_q_norm_g=v_mla_q_norm_g,
               mla_w_uq=v_mla_w_uq[0], mla_kv_norm_g=v_mla_kv_norm_g, mla_w_ukv=v_mla_w_ukv[0], mla_proj=v_mla_proj[0],
               w_out=v_w_out[0], final_norm_g=v_final_norm_g)
    out_shapes = {n: a.shape for n, a in zip(_ORDER, (meta_tokens, norm_g, w_in, gla_gate_w, gla_gate_b, gla_norm_g,
                                                     gla_proj, mla_q_norm_g, mla_w_uq, mla_kv_norm_g, mla_w_ukv,
                                                     mla_proj, w_out, final_norm_g))}

    me = (2 * lax.axis_index("x") + lax.axis_index("y")).astype(jnp.int32)
    is_mine = lax.broadcasted_iota(jnp.int32, (4, 1, 1), 0) == me
    with_own = lambda gth, own: jnp.where(is_mine, own[None], gth)
    first = [w["w_in"].astype(BF16), meta_tokens]
    w_in_owner, meta_owner = [with_own(gth, own) for gth, own in zip(_weight_gather(first), first)]
    meta_full = meta_owner.transpose(1, 0, 2).reshape(N_META, D)
    late_names = _MATS[1:]
    late = [w[n].astype(BF16) for n in late_names]
    gather_sems = _late_gather_start(late)

    def late_weights(after):
        lands = _late_gather_wait(gather_sems[0], gather_sems[1], gather_sems[2], gather_sems[3], after)
        full = []
        for name, land, own in zip(late_names, lands, late):
            gth = with_own(land, own)
            if name in _ROW_SHARDED:
                full.append(gth.reshape(4 * gth.shape[1], gth.shape[2]))
            else:
                full.append(gth.transpose(1, 0, 2).reshape(gth.shape[1], 4 * gth.shape[2]))
        return full

    def by_owner(name, arr):
        if name == "w_in":
            return arr
        if name in _ROW_SHARDED:
            return arr.reshape(4, arr.shape[0] // 4, arr.shape[1])
        return arr.reshape(arr.shape[0], 4, arr.shape[1] // 4).transpose(1, 0, 2)

    c_idx = lax.axis_index("c").astype(jnp.int32).reshape(1)
    pos = jnp.stack([c_idx[0], me])
    in_flight = {}

    def start_matrix_reduce(early):
        gps = [by_owner(n, early[n]) for n in _MATS]
        recvs = _pair_swap(gps)
        s1 = [_pair_add_big(gps[0], recvs[0], c_idx)] + list(_pair_add_small(gps[1:], recvs[1:]))
        send_sems, recv_sems, parts, lands, token = _chip_scatter_start(s1)
        in_flight.update(send_sems=send_sems, recv_sems=recv_sems, parts=parts, lands=lands)
        return token

    norm_g_after_start = norm_g + gather_sems[4][0:1, 0:1]
    loss_local, grad_x, g = _local_step(
        x, loss_target, meta_full, norm_g_after_start, w_in_owner, None, gla_gate_b, gla_norm_g, None,
        mla_q_norm_g, None, mla_kv_norm_g, None, None, None, final_norm_g,
        early_grads_hook=start_matrix_reduce, late_weights_hook=late_weights)
    loss = lax.psum(loss_local, ("x", "y", "c"))

    s1, landed = _chip_scatter_wait(in_flight["send_sems"], in_flight["recv_sems"], in_flight["parts"],
                                    in_flight["lands"], after=g["norm_g"])
    halves = [_sum_chips_big(landed[0], s1[0], pos)] + list(_sum_chips_small(landed[1:], s1[1:]))
    g_mats = [j.reshape(out_shapes[n]) for j, n in zip(_pair_join(halves), _MATS)]

    late = [g["meta_tokens"], _pack_small(g)]
    meta_sum, small_sum = _sum_devices_small(_all_to_all_small(late), late)
    g_meta = lax.dynamic_slice(meta_sum, (0, me * (D // 4)), (N_META, D // 4))
    names = _MATS + ("meta_tokens",)
    g_red = g_mats + [g_meta, small_sum]

    tens = lambda d: [d[n].reshape(out_shapes[n]) for n in names] + [_pack_small(d)]
    w_t, m_t, v_t = tens(w), tens(mom), tens(var)
    big = _adamw_big(w_t[0], g_red[0], m_t[0], v_t[0])
    rest = _adamw_small(w_t[1:], g_red[1:], m_t[1:], v_t[1:])
    k = len(names)
    results = {"grad": g_red}
    for i, kind in enumerate(("delta", "new_m", "new_v")):
        results[kind] = [big[i]] + list(rest[i * k:(i + 1) * k])

    outs = []
    for kind in ("grad", "delta", "new_m", "new_v"):
        vals = dict(zip(names, results[kind][:-1]))
        vals.update(_unpack_small(results[kind][-1]))
        outs += [vals[n].reshape(out_shapes[n]) for n in _ORDER]
    return (loss, grad_x, *outs)
```

```python
import functools
import math

import jax
import jax.numpy as jnp
import numpy as np
from jax import lax
from jax.experimental import pallas as pl
from jax.experimental.pallas import tpu as pltpu

F32 = jnp.float32
BF16 = jnp.bfloat16

D = 1024
N_META = 16
QB = 256
FRONT = QB - N_META
HEAD_ROWS = FRONT + N_META
assert FRONT % 64 == 48
EPS = 1e-6

GLA_H, GLA_DK, GLA_DV, GLA_RANK, GLA_C = 4, 128, 256, 16, 64
GLA_NORMALIZER = 16.0
GLA_KW, GLA_VW = GLA_H * GLA_DK, GLA_H * GLA_DV
MLA_H, NOPE, ROPE, MLA_DV, Q_RANK, KV_RANK = 8, 128, 64, 128, 256, 128
MLA_QK = NOPE + ROPE
ROPE_BASE = 10000.0
SPLITS = (GLA_KW, GLA_KW, GLA_VW, GLA_RANK, GLA_VW, Q_RANK, KV_RANK, ROPE, MLA_H * MLA_DV, D, D)
IN_WIDTH = sum(SPLITS)

ADAM_LR, ADAM_B1, ADAM_B2, ADAM_EPS, ADAM_WD, ADAM_STEP = 0.001, 0.9, 0.999, 1e-08, 0.01, 10

LANES = 128
VMEM_CAP_V7X = 56 * 1024 * 1024
MESH = pl.DeviceIdType.MESH
NEG = -1e30
LOG2E = math.log2(math.e)

SMALL = (("norm_g", D), ("gla_gate_b", GLA_KW), ("gla_norm_g", GLA_DV), ("mla_q_norm_g", Q_RANK),
         ("mla_kv_norm_g", KV_RANK), ("final_norm_g", D))


def _div_tile(n, target, mult):
    best = None
    for d in range(mult, min(n, target) + 1, mult):
        if n % d == 0:
            best = d
    assert best is not None, (n, target, mult)
    return best


def _params(sem, block_bytes, scratch_bytes=0):
    est = 2 * block_bytes + scratch_bytes + 12 * 1024 * 1024
    return pltpu.CompilerParams(dimension_semantics=sem, vmem_limit_bytes=int(min(max(est, 24 * 1024 * 1024), VMEM_CAP_V7X)))


def _nbytes(shape, dtype):
    return int(np.prod(shape)) * jnp.dtype(dtype).itemsize


def _sigmoid(x):
    return 1.0 / (1.0 + jnp.exp(-x))


def _nt(a, b):
    return lax.dot_general(a, b, (((1,), (1,)), ((), ())), preferred_element_type=F32)


def _tn(a, b):
    return lax.dot_general(a, b, (((0,), (0,)), ((), ())), preferred_element_type=F32)


def _nn(a, b):
    return jnp.dot(a, b, preferred_element_type=F32)


def _split3(x):
    a = x.astype(BF16)
    r = x - a.astype(F32)
    b = r.astype(BF16)
    c = (r - b.astype(F32)).astype(BF16)
    return a, b, c


def _mm(a, b, *, name, trans_a=False, trans_b=False, out_dtype=F32, tm, tn, tk):
    assert not (trans_a and trans_b)
    if trans_a:
        K, M = a.shape
    else:
        M, K = a.shape
    N = b.shape[0] if trans_b else b.shape[1]
    assert (b.shape[1] if trans_b else b.shape[0]) == K
    assert M % tm == 0 and N % tn == 0 and K % tk == 0, (name, M, N, K, tm, tn, tk)
    nk = K // tk

    def body(a_ref, b_ref, o_ref, *scratch):
        av = a_ref[...].astype(BF16)
        bv = b_ref[...].astype(BF16)
        prod = _tn(av, bv) if trans_a else (_nt(av, bv) if trans_b else _nn(av, bv))
        if nk == 1:
            o_ref[...] = prod.astype(out_dtype)
        else:
            acc = scratch[0]
            k = pl.program_id(2)

            @pl.when(k == 0)
            def _():
                acc[...] = prod

            @pl.when(k > 0)
            def _():
                acc[...] += prod

            @pl.when(k == nk - 1)
            def _():
                o_ref[...] = acc[...].astype(out_dtype)

    if trans_a:
        a_spec = pl.BlockSpec((tk, tm), lambda i, j, k: (k, i))
    else:
        a_spec = pl.BlockSpec((tm, tk), lambda i, j, k: (i, k))
    if trans_b:
        b_spec = pl.BlockSpec((tn, tk), lambda i, j, k: (j, k))
    else:
        b_spec = pl.BlockSpec((tk, tn), lambda i, j, k: (k, j))
    blocks = (_nbytes((tm, tk), a.dtype) + _nbytes((tk, tn), b.dtype) + _nbytes((tm, tn), out_dtype))
    scratch = [pltpu.VMEM((tm, tn), F32)] if nk > 1 else []
    return pl.pallas_call(
        body,
        out_shape=jax.ShapeDtypeStruct((M, N), out_dtype),
        grid=(M // tm, N // tn, nk),
        in_specs=[a_spec, b_spec],
        out_specs=pl.BlockSpec((tm, tn), lambda i, j, k: (i, j)),
        scratch_shapes=scratch,
        compiler_params=_params(("parallel", "parallel", "arbitrary"), blocks + _nbytes((tm, tn), F32),
                                _nbytes((tm, tn), F32) if nk > 1 else 0),
        name=name,
    )(a, b)


def _h_tile(j, x_ref, meta_ref):
    head = jnp.concatenate([jnp.zeros((FRONT, D), F32), meta_ref[...]], axis=0)
    return jnp.where(j > 0, x_ref[0], head)


def _x_spec():
    return pl.BlockSpec((1, QB, D), lambda b, j: (b, jnp.maximum(j - 1, 0), 0))


def _rms_in(x, meta, g, B, Lp):
    T = B * Lp
    NQ = Lp // QB

    def body(x_ref, meta_ref, g_ref, u_ref):
        h = _h_tile(pl.program_id(1), x_ref, meta_ref)
        r = lax.rsqrt(jnp.mean(h * h, axis=-1, keepdims=True) + EPS)
        u_ref[...] = (h * r * g_ref[...]).astype(BF16)

    return pl.pallas_call(
        body,
        out_shape=jax.ShapeDtypeStruct((T, D), BF16),
        grid=(B, NQ),
        in_specs=[_x_spec(), pl.BlockSpec((N_META, D), lambda b, j: (0, 0)), pl.BlockSpec((1, D), lambda b, j: (0, 0))],
        out_specs=pl.BlockSpec((QB, D), lambda b, j: (b * NQ + j, 0)),
        compiler_params=_params(("parallel", "parallel"), _nbytes((QB, D), F32) * 2),
        name="rms_in",
    )(x, meta, g)


def _gla_gate(lr, wg, bg, valid):
    pre = _nn(lr.astype(BF16), wg) + bg
    logsig = jnp.minimum(pre, 0.0) - jnp.log(1.0 + jnp.exp(-jnp.abs(pre)))
    return pre, jnp.where(valid, logsig / GLA_NORMALIZER, 0.0)


def _tri_masks():
    ri = lax.broadcasted_iota(jnp.int32, (GLA_C, GLA_C), 0)
    ci = lax.broadcasted_iota(jnp.int32, (GLA_C, GLA_C), 1)
    return ci <= ri, ci >= ri


def _cumsum_rows(x, ones_mask):
    w = jnp.where(ones_mask, 1.0, 0.0).astype(BF16)
    a, b, c = _split3(x)
    return _nn(w, a) + _nn(w, b) + _nn(w, c)


def _gla_fwd(projA, projB, wg, bg, gn4, B, Lp):
    T = B * Lp
    NC = Lp // GLA_C
    C = GLA_C
    scale = GLA_DK ** -0.5

    def body(q_ref, k_ref, v_ref, lr_ref, z_ref, wg_ref, bg_ref, gn_ref, oa_ref, ya_ref, ssave_ref, st_ref):
        n = pl.program_id(0)

        @pl.when(n == 0)
        def _():
            st_ref[...] = jnp.zeros_like(st_ref)

        pos = n * C + lax.broadcasted_iota(jnp.int32, (C, 1), 0)
        lower, _ = _tri_masks()
        is_last = lax.broadcasted_iota(jnp.int32, (C, 1), 0) == C - 1
        for b in range(B):
            ssave_ref[b, 0] = st_ref[b]
            _, glog = _gla_gate(lr_ref[b], wg_ref[...], bg_ref[...], pos >= FRONT)
            bcum = _cumsum_rows(glog, lower)
            for h in range(GLA_H):
                ks = slice(h * GLA_DK, (h + 1) * GLA_DK)
                vs = slice(h * GLA_DV, (h + 1) * GLA_DV)
                bh = bcum[:, ks]
                blast = jnp.sum(jnp.where(is_last, bh, 0.0), axis=0, keepdims=True)
                qh = q_ref[b, :, ks].astype(F32) * scale
                kh = k_ref[b, :, ks].astype(F32)
                qe = (qh * jnp.exp(bh)).astype(BF16)
                ke = (kh * jnp.exp(-bh)).astype(BF16)
                kl = (kh * jnp.exp(blast - bh)).astype(BF16)
                vh = v_ref[b, :, vs].astype(BF16)
                a = jnp.where(lower, _nt(qe, ke), 0.0).astype(BF16)
                st = st_ref[b, h]
                o = _nn(a, vh) + _nt(qe, st.astype(BF16))
                st_ref[b, h] = st * jnp.exp(blast) + _tn(vh, kl)
                oa_ref[b, :, vs] = o.astype(BF16)
                on = o * lax.rsqrt(jnp.mean(o * o, axis=-1, keepdims=True) + EPS) * gn_ref[:, vs]
                z = z_ref[b, :, vs].astype(F32)
                ya_ref[b, :, vs] = (on * (z * _sigmoid(z))).astype(BF16)

    blocks = B * (_nbytes((C, 512), F32) * 2 + _nbytes((C, 1024), F32) * 3 + _nbytes((C, 1024), BF16)
                  + _nbytes((GLA_H, GLA_DV, GLA_DK), F32)) + _nbytes((128, 512), BF16)
    state = _nbytes((B, GLA_H, GLA_DV, GLA_DK), F32)
    pa = projA.reshape(B, Lp, projA.shape[1])
    oa, ya, ssave = pl.pallas_call(
        body,
        out_shape=(jax.ShapeDtypeStruct((B, Lp, GLA_VW), BF16), jax.ShapeDtypeStruct((B, Lp, GLA_VW), BF16),
                   jax.ShapeDtypeStruct((B, NC, GLA_H, GLA_DV, GLA_DK), F32)),
        grid=(NC,),
        in_specs=[
            pl.BlockSpec((B, C, 512), lambda n: (0, n, 10)),
            pl.BlockSpec((B, C, 512), lambda n: (0, n, 11)),
            pl.BlockSpec((B, C, 1024), lambda n: (0, n, 0)),
            pl.BlockSpec((B, C, 128), lambda n: (0, n, 3)),
            pl.BlockSpec((B, C, 1024), lambda n: (0, n, 1)),
            pl.BlockSpec((128, 512), lambda n: (0, 0)),
            pl.BlockSpec((1, 512), lambda n: (0, 0)),
            pl.BlockSpec((1, 1024), lambda n: (0, 0)),
        ],
        out_specs=(pl.BlockSpec((B, C, 1024), lambda n: (0, n, 0)),
                   pl.BlockSpec((B, C, 1024), lambda n: (0, n, 0)),
                   pl.BlockSpec((B, 1, GLA_H, GLA_DV, GLA_DK), lambda n: (0, n, 0, 0, 0))),
        scratch_shapes=[pltpu.VMEM((B, GLA_H, GLA_DV, GLA_DK), F32)],
        compiler_params=_params(("arbitrary",), blocks, state),
        name="gla_fwd",
    )(pa, pa, pa, projB.reshape(B, Lp, projB.shape[1]), pa, wg, bg, gn4)
    return oa.reshape(T, GLA_VW), ya.reshape(T, GLA_VW), ssave


def _swap_halves(x):
    lane = lax.broadcasted_iota(jnp.int32, x.shape, 1)
    return jnp.where((lane % 64) < 32, pltpu.roll(x, 96, 1), pltpu.roll(x, 32, 1))


def _mla_prep(projB, cos_t, sin_t, gq, gkv, wuq2, wukv, B, Lp, tr):
    T = B * Lp
    nt = Lp // tr
    HW = 2 * LANES

    def body(pb_ref, cos_ref, sin_ref, gq_ref, gkv_ref, wuq_ref, wukv_ref, q_ref, k_ref, v_ref, cqn_ref, ckvn_ref):
        cq = pb_ref[:, 0:Q_RANK].astype(F32)
        ckv = pb_ref[:, Q_RANK:Q_RANK + KV_RANK].astype(F32)
        kr = pb_ref[:, 512:640].astype(F32)
        cqn = (cq * lax.rsqrt(jnp.mean(cq * cq, axis=-1, keepdims=True) + EPS) * gq_ref[...]).astype(BF16)
        ckvn = (ckv * lax.rsqrt(jnp.mean(ckv * ckv, axis=-1, keepdims=True) + EPS) * gkv_ref[...]).astype(BF16)
        cqn_ref[...] = cqn
        ckvn_ref[...] = ckvn
        qf = _nn(cqn, wuq_ref[...])
        kvf = _nn(ckvn, wukv_ref[...])
        cs = cos_ref[...]
        sn = sin_ref[...]
        rope = lambda t: t * cs + _swap_halves(t) * sn
        kr_r = rope(kr).astype(BF16)
        for h in range(MLA_H):
            q_ref[:, h * HW:h * HW + LANES] = qf[:, h * HW:h * HW + LANES].astype(BF16)
            q_ref[:, h * HW + LANES:(h + 1) * HW] = rope(qf[:, h * HW + LANES:(h + 1) * HW]).astype(BF16)
            k_ref[:, h * HW:h * HW + LANES] = kvf[:, h * HW:h * HW + LANES].astype(BF16)
            k_ref[:, h * HW + LANES:(h + 1) * HW] = kr_r
            v_ref[:, h * MLA_DV:(h + 1) * MLA_DV] = kvf[:, h * HW + LANES:(h + 1) * HW].astype(BF16)

    blocks = (_nbytes((tr, 640), F32) + 2 * _nbytes((tr, 128), F32) + _nbytes((Q_RANK, 2048), BF16)
              + _nbytes((KV_RANK, 2048), BF16) + _nbytes((tr, 2048 * 2 + 1024 + 384), BF16)
              + 2 * _nbytes((tr, 2048), F32))
    return pl.pallas_call(
        body,
        out_shape=(jax.ShapeDtypeStruct((T, MLA_H * HW), BF16), jax.ShapeDtypeStruct((T, MLA_H * HW), BF16),
                   jax.ShapeDtypeStruct((T, MLA_H * MLA_DV), BF16), jax.ShapeDtypeStruct((T, Q_RANK), BF16),
                   jax.ShapeDtypeStruct((T, KV_RANK), BF16)),
        grid=(B, nt),
        in_specs=[
            pl.BlockSpec((tr, 640), lambda b, j: (b * nt + j, 0)),
            pl.BlockSpec((tr, 128), lambda b, j: (j, 0)),
            pl.BlockSpec((tr, 128), lambda b, j: (j, 0)),
            pl.BlockSpec((1, Q_RANK), lambda b, j: (0, 0)),
            pl.BlockSpec((1, KV_RANK), lambda b, j: (0, 0)),
            pl.BlockSpec((Q_RANK, 2048), lambda b, j: (0, 0)),
            pl.BlockSpec((KV_RANK, 2048), lambda b, j: (0, 0)),
        ],
        out_specs=(pl.BlockSpec((tr, 2048), lambda b, j: (b * nt + j, 0)),
                   pl.BlockSpec((tr, 2048), lambda b, j: (b * nt + j, 0)),
                   pl.BlockSpec((tr, 1024), lambda b, j: (b * nt + j, 0)),
                   pl.BlockSpec((tr, Q_RANK), lambda b, j: (b * nt + j, 0)),
                   pl.BlockSpec((tr, KV_RANK), lambda b, j: (b * nt + j, 0))),
        compiler_params=_params(("parallel", "parallel"), blocks),
        name="mla_prep",
    )(projB, cos_t, sin_t, gq, gkv, wuq2, wukv)


def _attn_mask(row, col):
    return (col <= row) & ((col >= FRONT) | (row < FRONT))


def _attn_fwd(q_att, k_att, v_att, projA, B, Lp):
    T = B * Lp
    NQ = Lp // QB
    HW = 2 * LANES
    scale = 1.0 / math.sqrt(MLA_QK)

    def body(q_ref, k_ref, v_ref, mz_ref, o_ref, yb_ref, lsec_ref, m_ref, l_ref, acc_ref):
        qi = pl.program_id(1)
        m_ref[...] = jnp.full(m_ref.shape, NEG, F32)
        l_ref[...] = jnp.zeros_like(l_ref)
        acc_ref[...] = jnp.zeros_like(acc_ref)
        row = qi * QB + lax.broadcasted_iota(jnp.int32, (QB, QB), 0)
        coli = lax.broadcasted_iota(jnp.int32, (QB, QB), 1)

        def step(kj, masked):
            off = pl.multiple_of(kj * QB, QB)
            ok = _attn_mask(row, kj * QB + coli) if masked else None
            for h in range(MLA_H):
                q = q_ref[:, h * HW:(h + 1) * HW]
                kb = k_ref[pl.ds(off, QB), h * HW:(h + 1) * HW]
                vb = v_ref[pl.ds(off, QB), h * MLA_DV:(h + 1) * MLA_DV]
                s = _nt(q, kb) * (scale * LOG2E)
                if masked:
                    s = jnp.where(ok, s, NEG)
                m_old = m_ref[h]
                m_new = jnp.maximum(m_old, jnp.max(s, axis=-1, keepdims=True))
                alpha = jnp.exp2(m_old - m_new)
                p = jnp.exp2(s - jnp.tile(m_new, (1, QB // LANES)))
                m_ref[h] = m_new
                l_ref[h] = alpha * l_ref[h] + jnp.sum(p, axis=-1, keepdims=True)
                acc_ref[h] = alpha * acc_ref[h] + _nn(p.astype(BF16), vb)

        step(0, True)

        def unmasked(kj, carry):
            step(kj, False)
            return carry

        lax.fori_loop(1, qi, unmasked, 0)

        @pl.when(qi > 0)
        def _():
            step(qi, True)

        for h in range(MLA_H):
            hs = slice(h * MLA_DV, (h + 1) * MLA_DV)
            l = l_ref[h]
            o = acc_ref[h] / l
            o_ref[:, hs] = o.astype(BF16)
            z = mz_ref[:, hs].astype(F32)
            yb_ref[:, hs] = (o * (z * _sigmoid(z))).astype(BF16)
            lse2 = m_ref[h] + jnp.log(l) * LOG2E
            lsec_ref[0, h, pl.ds(qi, 1), :] = jnp.transpose(lse2)[0:1, :]

    blocks = (_nbytes((QB, 2048), BF16) + _nbytes((Lp, 2048), BF16) + _nbytes((Lp, 1024), BF16)
              + 2 * _nbytes((QB, 1024), F32) + _nbytes((QB, 1024), BF16) + _nbytes((MLA_H, QB, LANES), F32)
              + _nbytes((MLA_H, NQ, QB), F32))
    return pl.pallas_call(
        body,
        out_shape=(jax.ShapeDtypeStruct((T, MLA_H * MLA_DV), BF16), jax.ShapeDtypeStruct((T, MLA_H * MLA_DV), BF16),
                   jax.ShapeDtypeStruct((B, MLA_H, NQ, QB), F32)),
        grid=(B, NQ),
        in_specs=[
            pl.BlockSpec((QB, MLA_H * HW), lambda b, i: (b * NQ + i, 0)),
            pl.BlockSpec((Lp, MLA_H * HW), lambda b, i: (b, 0)),
            pl.BlockSpec((Lp, MLA_H * MLA_DV), lambda b, i: (b, 0)),
            pl.BlockSpec((QB, 1024), lambda b, i: (b * NQ + i, 2)),
        ],
        out_specs=(pl.BlockSpec((QB, 1024), lambda b, i: (b * NQ + i, 0)),
                   pl.BlockSpec((QB, 1024), lambda b, i: (b * NQ + i, 0)),
                   pl.BlockSpec((1, MLA_H, NQ, QB), lambda b, i: (b, 0, 0, 0))),
        scratch_shapes=[pltpu.VMEM((MLA_H, QB, LANES), F32), pltpu.VMEM((MLA_H, QB, LANES), F32),
                        pltpu.VMEM((MLA_H, QB, MLA_DV), F32)],
        compiler_params=_params(("parallel", "arbitrary"), blocks, 3 * _nbytes((MLA_H, QB, LANES), F32)),
        name="attn_fwd",
    )(q_att, k_att, v_att, projA)


def _out_proj_loss(x, meta, projA, ya, yb, w_out, gf, tgt, B, Lp):
    T = B * Lp
    NQ = Lp // QB

    def body(x_ref, meta_ref, gg_ref, gm_ref, ya_ref, yb_ref, w_ref, gf_ref, t_ref,
             dhb_ref, mg_ref, loss_ref, dgf_ref):
        b = pl.program_id(0)
        j = pl.program_id(1)

        @pl.when((b == 0) & (j == 0))
        def _():
            loss_ref[...] = jnp.zeros_like(loss_ref)
            dgf_ref[...] = jnp.zeros_like(dgf_ref)

        f32 = lambda ref: ref[...].astype(F32)
        merged = (_sigmoid(f32(gg_ref)) * f32(ya_ref) + _sigmoid(f32(gm_ref)) * f32(yb_ref)).astype(BF16)
        mg_ref[...] = merged
        h1 = _h_tile(j, x_ref, meta_ref) + _nn(merged, w_ref[...])
        r = lax.rsqrt(jnp.mean(h1 * h1, axis=-1, keepdims=True) + EPS)
        hn = h1 * r
        gfv = gf_ref[...]
        diff = jnp.where(j > 0, hn * gfv - t_ref[0], 0.0)
        loss_ref[...] += (0.5 / D) * jnp.sum(jnp.sum(diff * diff, axis=-1, keepdims=True), axis=0, keepdims=True)
        dout = diff * (1.0 / D)
        dgf_ref[...] += jnp.sum(dout * hn, axis=0, keepdims=True)
        dhn = dout * gfv
        dh = r * (dhn - hn * jnp.mean(dhn * hn, axis=-1, keepdims=True))
        dhb_ref[...] = dh.astype(BF16)

    rows = lambda c: pl.BlockSpec((QB, D), lambda b, j: (b * NQ + j, c))
    const = lambda s: pl.BlockSpec(s, lambda b, j: (0, 0))
    return pl.pallas_call(
        body,
        out_shape=(jax.ShapeDtypeStruct((T, D), BF16), jax.ShapeDtypeStruct((T, D), BF16),
                   jax.ShapeDtypeStruct((1, 1), F32), jax.ShapeDtypeStruct((1, D), F32)),
        grid=(B, NQ),
        in_specs=[_x_spec(), const((N_META, D)), rows(3), rows(4), rows(0), rows(0), const((D, D)),
                  const((1, D)), _x_spec()],
        out_specs=(rows(0), rows(0), const((1, 1)), const((1, D))),
        compiler_params=_params(("arbitrary", "arbitrary"), 10 * _nbytes((QB, D), F32)),
        name="out_proj_loss",
    )(x, meta, projA, projA, ya, yb, w_out, gf, tgt)


def _merge_bwd(dh1_b, w_out, projA, ya, yb, tr):
    T = dh1_b.shape[0]

    def body(dh_ref, w_ref, gg_ref, gm_ref, ya_ref, yb_ref, dya_ref, dyb_ref, da_ref):
        d = _nt(dh_ref[...], w_ref[...])
        sg = _sigmoid(gg_ref[...].astype(F32))
        sm = _sigmoid(gm_ref[...].astype(F32))
        dya_ref[...] = (d * sg).astype(BF16)
        dyb_ref[...] = (d * sm).astype(BF16)
        da_ref[:, 0:D] = (d * ya_ref[...].astype(F32) * (sg * (1.0 - sg))).astype(BF16)
        da_ref[:, D:2 * D] = (d * yb_ref[...].astype(F32) * (sm * (1.0 - sm))).astype(BF16)

    spec = lambda c: pl.BlockSpec((tr, D), lambda i: (i, c))
    return pl.pallas_call(
        body,
        out_shape=(jax.ShapeDtypeStruct((T, D), BF16), jax.ShapeDtypeStruct((T, D), BF16),
                   jax.ShapeDtypeStruct((T, 2 * D), BF16)),
        grid=(T // tr,),
        in_specs=[spec(0), pl.BlockSpec((D, D), lambda i: (0, 0)), spec(3), spec(4), spec(0), spec(0)],
        out_specs=(spec(0), spec(0), pl.BlockSpec((tr, 2 * D), lambda i: (i, 0))),
        compiler_params=_params(("parallel",), 8 * _nbytes((tr, D), F32)),
        name="merge_bwd",
    )(dh1_b, w_out, projA, projA, ya, yb)


def _gla_out_bwd(dya, gla_proj, oa, projA, gn4, tr):
    T = dya.shape[0]
    nsteps = T // tr

    def body(dya_ref, w_ref, oa_ref, z_ref, gn_ref, do_ref, dz_ref, dgn_ref, acc_ref):
        i = pl.program_id(0)

        @pl.when(i == 0)
        def _():
            acc_ref[...] = jnp.zeros_like(acc_ref)

        dy_all = _nt(dya_ref[...], w_ref[...])
        for h in range(GLA_H):
            vs = slice(h * GLA_DV, (h + 1) * GLA_DV)
            dy = dy_all[:, vs]
            o = oa_ref[:, vs].astype(F32)
            z = z_ref[:, vs].astype(F32)
            gn = gn_ref[:, vs]
            s = _sigmoid(z)
            ra = lax.rsqrt(jnp.mean(o * o, axis=-1, keepdims=True) + EPS)
            on = o * ra
            don = dy * (z * s)
            t = don * gn
            do_ref[:, vs] = (ra * (t - on * jnp.mean(t * on, axis=-1, keepdims=True))).astype(BF16)
            dz_ref[:, vs] = (dy * (on * gn) * (s * (1.0 + z * (1.0 - s)))).astype(BF16)
            acc_ref[:, vs] += jnp.sum(don * on, axis=0, keepdims=True)

        @pl.when(i == nsteps - 1)
        def _():
            a = acc_ref[...]
            dgn_ref[...] = a[:, 0:256] + a[:, 256:512] + a[:, 512:768] + a[:, 768:1024]

    spec = lambda c: pl.BlockSpec((tr, D), lambda i: (i, c))
    return pl.pallas_call(
        body,
        out_shape=(jax.ShapeDtypeStruct((T, D), BF16), jax.ShapeDtypeStruct((T, D), BF16),
                   jax.ShapeDtypeStruct((1, GLA_DV), F32)),
        grid=(nsteps,),
        in_specs=[spec(0), pl.BlockSpec((D, D), lambda i: (0, 0)), spec(0), spec(1),
                  pl.BlockSpec((1, D), lambda i: (0, 0))],
        out_specs=(spec(0), spec(0), pl.BlockSpec((1, GLA_DV), lambda i: (0, 0))),
        scratch_shapes=[pltpu.VMEM((1, D), F32)],
        compiler_params=_params(("arbitrary",), 6 * _nbytes((tr, D), F32)),
        name="gla_out_bwd",
    )(dya, gla_proj, oa, projA, gn4)


def _gla_bwd(projA, projB, ssave, doa, wg, bg, B, Lp):
    T = B * Lp
    NC = Lp // GLA_C
    C = GLA_C
    scale = GLA_DK ** -0.5
    WC = 2304

    def body(q_ref, k_ref, v_ref, lr_ref, ss_ref, do_ref, wg_ref, bg_ref, dc_ref, dwg_ref, dbg_ref, dst_ref):
        i = pl.program_id(0)
        n = NC - 1 - i

        @pl.when(i == 0)
        def _():
            dst_ref[...] = jnp.zeros_like(dst_ref)
            dwg_ref[...] = jnp.zeros_like(dwg_ref)
            dbg_ref[...] = jnp.zeros_like(dbg_ref)

        pos = n * C + lax.broadcasted_iota(jnp.int32, (C, 1), 0)
        valid = pos >= FRONT
        lower, upper = _tri_masks()
        is_last = lax.broadcasted_iota(jnp.int32, (C, 1), 0) == C - 1
        for b in range(B):
            lr = lr_ref[b]
            pre, glog = _gla_gate(lr, wg_ref[...], bg_ref[...], valid)
            bcum = _cumsum_rows(glog, lower)
            db_parts = []
            for h in range(GLA_H):
                ks = slice(h * GLA_DK, (h + 1) * GLA_DK)
                vs = slice(h * GLA_DV, (h + 1) * GLA_DV)
                bh = bcum[:, ks]
                blast = jnp.sum(jnp.where(is_last, bh, 0.0), axis=0, keepdims=True)
                eb, enb, ekl, ebl = jnp.exp(bh), jnp.exp(-bh), jnp.exp(blast - bh), jnp.exp(blast)
                qh = q_ref[b, :, ks].astype(F32) * scale
                kh = k_ref[b, :, ks].astype(F32)
                qe_f, ke_f, kl_f = qh * eb, kh * enb, kh * ekl
                qe, ke, kl = qe_f.astype(BF16), ke_f.astype(BF16), kl_f.astype(BF16)
                vh = v_ref[b, :, vs].astype(BF16)
                doh = do_ref[b, :, vs]
                st = ss_ref[b, 0, h]
                dst = dst_ref[b, h]
                st_b, dst_b = st.astype(BF16), dst.astype(BF16)
                da = jnp.where(lower, _nt(doh, vh), 0.0).astype(BF16)
                da_t = jnp.where(upper, _nt(vh, doh), 0.0).astype(BF16)
                a_t = jnp.where(upper, _nt(ke, qe), 0.0).astype(BF16)
                dqe = _nn(da, ke) + _nn(doh, st_b)
                dke = _nn(da_t, qe)
                dvh = _nn(a_t, doh) + _nt(kl, dst_b)
                dkl = _nn(vh, dst_b)
                dst_ref[b, h] = dst * ebl + _tn(doh, qe)
                deb = jnp.sum(st * dst, axis=0, keepdims=True)
                db = dqe * qe_f - dke * ke_f - dkl * kl_f
                db_last = jnp.sum(dkl * kl_f, axis=0, keepdims=True) + deb * ebl
                db_parts.append(db + jnp.where(is_last, db_last, 0.0))
                dc_ref[b, :, vs] = dvh.astype(BF16)
                dc_ref[b, :, 1024 + h * GLA_DK:1024 + (h + 1) * GLA_DK] = (dqe * eb * scale).astype(BF16)
                dc_ref[b, :, 1536 + h * GLA_DK:1536 + (h + 1) * GLA_DK] = (dke * enb + dkl * ekl).astype(BF16)
            dglog = _cumsum_rows(jnp.concatenate(db_parts, axis=1), upper)
            dpre = jnp.where(valid, dglog * (1.0 / GLA_NORMALIZER) / (1.0 + jnp.exp(pre)), 0.0)
            dpre_b = dpre.astype(BF16)
            dc_ref[b, :, 2048:2176] = _nt(dpre_b, wg_ref[...]).astype(BF16)
            dc_ref[b, :, 2176:2304] = jnp.zeros((C, 128), BF16)
            dwg_ref[...] += _tn(lr.astype(BF16), dpre_b)
            dbg_ref[...] += jnp.sum(dpre, axis=0, keepdims=True)

    blocks = B * (_nbytes((C, 512), F32) * 2 + _nbytes((C, 1024), F32) + _nbytes((C, 1024), BF16)
                  + _nbytes((GLA_H, GLA_DV, GLA_DK), F32) + _nbytes((C, WC), BF16)) + 3 * _nbytes((128, 512), F32)
    state = _nbytes((B, GLA_H, GLA_DV, GLA_DK), F32)
    pa = projA.reshape(B, Lp, projA.shape[1])
    rev = lambda i: NC - 1 - i
    dc, dwg, dbg = pl.pallas_call(
        body,
        out_shape=(jax.ShapeDtypeStruct((B, Lp, WC), BF16), jax.ShapeDtypeStruct((128, GLA_KW), F32),
                   jax.ShapeDtypeStruct((1, GLA_KW), F32)),
        grid=(NC,),
        in_specs=[
            pl.BlockSpec((B, C, 512), lambda i: (0, rev(i), 10)),
            pl.BlockSpec((B, C, 512), lambda i: (0, rev(i), 11)),
            pl.BlockSpec((B, C, 1024), lambda i: (0, rev(i), 0)),
            pl.BlockSpec((B, C, 128), lambda i: (0, rev(i), 3)),
            pl.BlockSpec((B, 1, GLA_H, GLA_DV, GLA_DK), lambda i: (0, rev(i), 0, 0, 0)),
            pl.BlockSpec((B, C, 1024), lambda i: (0, rev(i), 0)),
            pl.BlockSpec((128, 512), lambda i: (0, 0)),
            pl.BlockSpec((1, 512), lambda i: (0, 0)),
        ],
        out_specs=(pl.BlockSpec((B, C, WC), lambda i: (0, rev(i), 0)),
                   pl.BlockSpec((128, GLA_KW), lambda i: (0, 0)),
                   pl.BlockSpec((1, GLA_KW), lambda i: (0, 0))),
        scratch_shapes=[pltpu.VMEM((B, GLA_H, GLA_DV, GLA_DK), F32)],
        compiler_params=_params(("arbitrary",), blocks, state),
        name="gla_bwd",
    )(pa, pa, pa, projB.reshape(B, Lp, projB.shape[1]), ssave, doa.reshape(B, Lp, GLA_VW), wg, bg)
    return dc.reshape(T, WC), dwg, dbg


def _attn_bwd_pre(dyb, mla_proj, projA, ob, B, Lp):
    T = B * Lp
    NQ = Lp // QB

    def body(dyb_ref, w_ref, z_ref, o_ref, do_ref, dz_ref, dcol_ref):
        j = pl.program_id(1)
        dy_all = _nt(dyb_ref[...], w_ref[...])
        for h in range(MLA_H):
            hs = slice(h * MLA_DV, (h + 1) * MLA_DV)
            dy = dy_all[:, hs]
            z = z_ref[:, hs].astype(F32)
            o = o_ref[:, hs].astype(F32)
            s = _sigmoid(z)
            do = dy * (z * s)
            do_ref[:, hs] = do.astype(BF16)
            dz_ref[:, hs] = (dy * o * (s * (1.0 + z * (1.0 - s)))).astype(BF16)
            dl = jnp.broadcast_to(jnp.sum(do * o, axis=-1, keepdims=True), (QB, LANES))
            dcol_ref[0, h, pl.ds(j, 1), :] = jnp.transpose(dl)[0:1, :]

    rows = lambda c: pl.BlockSpec((QB, D), lambda b, j: (b * NQ + j, c))
    return pl.pallas_call(
        body,
        out_shape=(jax.ShapeDtypeStruct((T, D), BF16), jax.ShapeDtypeStruct((T, D), BF16),
                   jax.ShapeDtypeStruct((B, MLA_H, NQ, QB), F32)),
        grid=(B, NQ),
        in_specs=[rows(0), pl.BlockSpec((D, D), lambda b, j: (0, 0)), rows(2), rows(0)],
        out_specs=(rows(0), rows(0), pl.BlockSpec((1, MLA_H, NQ, QB), lambda b, j: (b, 0, 0, 0))),
        compiler_params=_params(("parallel", "arbitrary"), 6 * _nbytes((QB, D), F32)),
        name="attn_bwd_pre",
    )(dyb, mla_proj, projA, ob)


ATTN_BWD_HEADS = 8


def _attn_bwd(q_att, k_att, v_att, do, lse_c, delta_c, B, Lp):
    T = B * Lp
    NQ = Lp // QB
    G = ATTN_BWD_HEADS
    NG = MLA_H // G
    HW = 2 * LANES
    scale = 1.0 / math.sqrt(MLA_QK)

    def body(q_ref, k_ref, v_ref, do_ref, lse_ref, dl_ref, dq_out, dk_out, dv_out, dq_ref, dk_ref, dv_ref):
        kj = pl.program_id(2)

        @pl.when(kj == 0)
        def _():
            dq_ref[...] = jnp.zeros_like(dq_ref)

        dk_ref[...] = jnp.zeros_like(dk_ref)
        dv_ref[...] = jnp.zeros_like(dv_ref)
        col = kj * QB + lax.broadcasted_iota(jnp.int32, (QB, QB), 0)
        rowi = lax.broadcasted_iota(jnp.int32, (QB, QB), 1)

        def step(qi, masked):
            off = pl.multiple_of(qi * QB, QB)
            ok = _attn_mask(qi * QB + rowi, col) if masked else None
            for h in range(G):
                ws = slice(h * HW, (h + 1) * HW)
                hs = slice(h * MLA_DV, (h + 1) * MLA_DV)
                qb = q_ref[pl.ds(off, QB), ws]
                dob = do_ref[pl.ds(off, QB), hs]
                kb = k_ref[:, ws]
                lse2 = lse_ref[0, h, pl.ds(qi, 1), :]
                delta = dl_ref[0, h, pl.ds(qi, 1), :]
                p_t = jnp.exp2(_nt(kb, qb) * (scale * LOG2E) - lse2)
                if masked:
                    p_t = jnp.where(ok, p_t, 0.0)
                dv_ref[:, hs] += _nn(p_t.astype(BF16), dob)
                ds_t = (p_t * (_nt(v_ref[:, hs], dob) - delta) * scale).astype(BF16)
                dk_ref[:, ws] += _nn(ds_t, qb)
                dq_ref[pl.ds(off, QB), ws] += _tn(ds_t, kb)

        def loop(masked):
            def it(qi, carry):
                step(qi, masked)
                return carry
            lax.fori_loop(kj + 1, NQ, it, 0)

        step(kj, True)
        pl.when(kj == 0)(lambda: loop(True))
        pl.when(kj > 0)(lambda: loop(False))
        dk_out[...] = dk_ref[...].astype(BF16)
        dv_out[...] = dv_ref[...].astype(BF16)

        @pl.when(kj == NQ - 1)
        def _():
            dq_out[...] = dq_ref[...].astype(BF16)

    blocks = (2 * _nbytes((Lp, G * HW), BF16) + _nbytes((Lp, G * MLA_DV), BF16) + 2 * _nbytes((QB, G * 384), BF16)
              + 2 * _nbytes((G, NQ, QB), F32))
    scratch = [pltpu.VMEM((Lp, G * HW), F32), pltpu.VMEM((QB, G * HW), F32), pltpu.VMEM((QB, G * MLA_DV), F32)]
    return pl.pallas_call(
        body,
        out_shape=(jax.ShapeDtypeStruct((T, MLA_H * HW), BF16), jax.ShapeDtypeStruct((T, MLA_H * HW), BF16),
                   jax.ShapeDtypeStruct((T, MLA_H * MLA_DV), BF16)),
        scratch_shapes=scratch,
        grid=(B, NG, NQ),
        in_specs=[
            pl.BlockSpec((Lp, G * HW), lambda b, g, j: (b, g), pipeline_mode=pl.Buffered(1)),
            pl.BlockSpec((QB, G * HW), lambda b, g, j: (b * NQ + j, g)),
            pl.BlockSpec((QB, G * MLA_DV), lambda b, g, j: (b * NQ + j, g)),
            pl.BlockSpec((Lp, G * MLA_DV), lambda b, g, j: (b, g), pipeline_mode=pl.Buffered(1)),
            pl.BlockSpec((1, G, NQ, QB), lambda b, g, j: (b, g, 0, 0)),
            pl.BlockSpec((1, G, NQ, QB), lambda b, g, j: (b, g, 0, 0)),
        ],
        out_specs=(pl.BlockSpec((Lp, G * HW), lambda b, g, j: (b, g), pipeline_mode=pl.Buffered(1)),
                   pl.BlockSpec((QB, G * HW), lambda b, g, j: (b * NQ + j, g)),
                   pl.BlockSpec((QB, G * MLA_DV), lambda b, g, j: (b * NQ + j, g))),
        compiler_params=_params(("parallel", "parallel", "arbitrary"), blocks,
                                _nbytes((Lp, G * HW), F32) + _nbytes((QB, G * 384), F32)),
        name="attn_bwd",
    )(q_att, k_att, v_att, do, lse_c, delta_c)


def _mla_bwd_post(dq, dk, dv, projB, cos_t, sin_t, gq, gkv, wuq2, wukv, B, Lp, tr):
    T = B * Lp
    nt = Lp // tr
    HW = 2 * LANES

    def body(dq_ref, dk_ref, dv_ref, pb_ref, cos_ref, sin_ref, gq_ref, gkv_ref, wuq_ref, wukv_ref,
             dqf_ref, dkvf_ref, de_ref, dgq_ref, dgkv_ref):
        first = (pl.program_id(0) == 0) & (pl.program_id(1) == 0)

        @pl.when(first)
        def _():
            dgq_ref[...] = jnp.zeros_like(dgq_ref)
            dgkv_ref[...] = jnp.zeros_like(dgkv_ref)

        cs = cos_ref[...]
        sn = sin_ref[...]
        rope_t = lambda t: t * cs + _swap_halves(t * sn)
        dkr = jnp.zeros((tr, LANES), F32)
        for h in range(MLA_H):
            dqf_ref[:, h * HW:h * HW + LANES] = dq_ref[:, h * HW:h * HW + LANES]
            dq_rope = dq_ref[:, h * HW + LANES:(h + 1) * HW].astype(F32)
            dqf_ref[:, h * HW + LANES:(h + 1) * HW] = rope_t(dq_rope).astype(BF16)
            dkvf_ref[:, h * HW:h * HW + LANES] = dk_ref[:, h * HW:h * HW + LANES]
            dkvf_ref[:, h * HW + LANES:(h + 1) * HW] = dv_ref[:, h * MLA_DV:(h + 1) * MLA_DV]
            dkr = dkr + dk_ref[:, h * HW + LANES:(h + 1) * HW].astype(F32)

        def norm_bwd(x, dn, g):
            r = lax.rsqrt(jnp.mean(x * x, axis=-1, keepdims=True) + EPS)
            xn = x * r
            t = dn * g
            return r * (t - xn * jnp.mean(t * xn, axis=-1, keepdims=True)), jnp.sum(dn * xn, axis=0, keepdims=True)

        cq = pb_ref[:, 0:Q_RANK].astype(F32)
        ckv = pb_ref[:, Q_RANK:Q_RANK + KV_RANK].astype(F32)
        dcq, dgq = norm_bwd(cq, _nt(dqf_ref[...], wuq_ref[...]), gq_ref[...])
        dckv, dgkv = norm_bwd(ckv, _nt(dkvf_ref[...], wukv_ref[...]), gkv_ref[...])
        dgq_ref[...] += dgq
        dgkv_ref[...] += dgkv
        de_ref[:, 0:Q_RANK] = dcq.astype(BF16)
        de_ref[:, Q_RANK:Q_RANK + KV_RANK] = dckv.astype(BF16)
        de_ref[:, 384:512] = rope_t(dkr).astype(BF16)

    rows = lambda w: pl.BlockSpec((tr, w), lambda b, j: (b * nt + j, 0))
    const = lambda s: pl.BlockSpec(s, lambda b, j: (0, 0))
    blocks = (2 * _nbytes((tr, 2048), F32) + _nbytes((tr, 1024), F32) + _nbytes((tr, 640), F32)
              + 2 * _nbytes((tr, 2048), BF16) + _nbytes((2048, 384), BF16) + 2 * _nbytes((tr, 2048), F32))
    return pl.pallas_call(
        body,
        out_shape=(jax.ShapeDtypeStruct((T, 2048), BF16), jax.ShapeDtypeStruct((T, 2048), BF16),
                   jax.ShapeDtypeStruct((T, 512), BF16), jax.ShapeDtypeStruct((1, Q_RANK), F32),
                   jax.ShapeDtypeStruct((1, KV_RANK), F32)),
        grid=(B, nt),
        in_specs=[rows(2048), rows(2048), rows(1024), rows(640),
                  pl.BlockSpec((tr, 128), lambda b, j: (j, 0)), pl.BlockSpec((tr, 128), lambda b, j: (j, 0)),
                  const((1, Q_RANK)), const((1, KV_RANK)), const((Q_RANK, 2048)), const((KV_RANK, 2048))],
        out_specs=(rows(2048), rows(2048), rows(512), const((1, Q_RANK)), const((1, KV_RANK))),
        compiler_params=_params(("arbitrary", "arbitrary"), blocks),
        name="mla_bwd_post",
    )(dq, dk, dv, projB, cos_t, sin_t, gq, gkv, wuq2, wukv)


def _in_proj_bwd(x, meta, dh1, dA, dBz, dC, dDz, dE, wA, wB, g, B, Lp):
    NQ = Lp // QB
    seq = x.shape[1]
    R = 2 if B % 2 == 0 else 1
    M = R * QB

    def body(x_ref, meta_ref, dh_ref, da_ref, db_ref, dc_ref, dd_ref, de_ref, wa_ref, wb_ref, g_ref,
             gx_ref, dmeta_ref, dg_ref):
        b = pl.program_id(0)
        j = pl.program_id(1)

        @pl.when((b == 0) & (j == 0))
        def _():
            dg_ref[...] = jnp.zeros_like(dg_ref)

        flat = lambda ref: ref[...].reshape(M, ref.shape[-1])
        da, dbz, dc, dd, de = flat(da_ref), flat(db_ref), flat(dc_ref), flat(dd_ref), flat(de_ref)
        du = _nt(da, wa_ref[:, 3072:5120])
        du = du + _nt(dbz, wa_ref[:, 1024:2048])
        du = du + _nt(dd, wa_ref[:, 2048:3072])
        du = du + _nt(dc[:, 0:1024], wa_ref[:, 0:1024])
        du = du + _nt(dc[:, 1024:2048], wa_ref[:, 5120:6144])
        du = du + _nt(dc[:, 2048:2176], wb_ref[:, 384:512])
        du = du + _nt(de[:, 0:384], wb_ref[:, 0:384])
        du = du + _nt(de[:, 384:512], wb_ref[:, 512:640])

        head = jnp.concatenate([jnp.zeros((FRONT, D), F32), meta_ref[...]], axis=0)
        x = jnp.concatenate([jnp.where(j > 0, x_ref[i], head) for i in range(R)], axis=0)
        r = lax.rsqrt(jnp.mean(x * x, axis=-1, keepdims=True) + EPS)
        xn = x * r
        t = du * g_ref[...]
        dh0 = flat(dh_ref).astype(F32) + r * (t - xn * jnp.mean(t * xn, axis=-1, keepdims=True))
        dg_ref[...] += jnp.sum(du * xn, axis=0, keepdims=True)
        dmeta = dh0[FRONT:HEAD_ROWS, :]
        for i in range(R):
            gx_ref[i] = dh0[i * QB:(i + 1) * QB, :]
            if i > 0:
                dmeta = dmeta + dh0[i * QB + FRONT:i * QB + HEAD_ROWS, :]

        @pl.when((j == 0) & (b == 0))
        def _():
            dmeta_ref[...] = dmeta

        @pl.when((j == 0) & (b > 0))
        def _():
            dmeta_ref[...] += dmeta

    rows = lambda w: pl.BlockSpec((R, QB, w), lambda b, j: (b, j, 0))
    x_rows = pl.BlockSpec((R, QB, D), lambda b, j: (b, jnp.maximum(j - 1, 0), 0))
    const = lambda s: pl.BlockSpec(s, lambda b, j: (0,) * len(s))
    resident = lambda s: pl.BlockSpec(s, lambda b, j: (0, 0), pipeline_mode=pl.Buffered(1))
    by_row = lambda a: a.reshape(B, Lp, a.shape[1])
    widths = [a.shape[1] for a in (dA, dBz, dC, dDz, dE)]
    blocks = sum(_nbytes((M, w), BF16) for w in widths) + 4 * _nbytes((M, D), F32)
    return pl.pallas_call(
        body,
        out_shape=(jax.ShapeDtypeStruct((B, seq, D), F32), jax.ShapeDtypeStruct((N_META, D), F32),
                   jax.ShapeDtypeStruct((1, D), F32)),
        grid=(B // R, NQ),
        in_specs=[x_rows, const((N_META, D)), rows(D)] + [rows(w) for w in widths]
        + [resident(wA.shape), resident(wB.shape), const((1, D))],
        out_specs=(x_rows, const((N_META, D)), const((1, D))),
        compiler_params=_params(("arbitrary", "arbitrary"), blocks, _nbytes(wA.shape, BF16) + _nbytes(wB.shape, BF16)),
        name="in_proj_bwd",
    )(x, meta, by_row(dh1), *[by_row(a) for a in (dA, dBz, dC, dDz, dE)], wA, wB, g)


_VMEM_WHOLE = pl.BlockSpec(memory_space=pltpu.VMEM)


def _params_whole(arrays):
    total = sum(_nbytes(a.shape, a.dtype) for a in arrays)
    return pltpu.CompilerParams(vmem_limit_bytes=int(min(total + 12 * 1024 * 1024, VMEM_CAP_V7X)))


def _wire_dtype(shape):
    return BF16 if shape[-2] * shape[-1] >= WIRE_BF16_MIN_ELEMS else F32


def _pair_add_big(gp, recv, c):
    _, half, cols = recv.shape
    th = _div_tile(half, 64, 16)
    out_dtype = _wire_dtype(recv.shape)

    steps = half // th

    def body(c_ref, a_ref, b_ref, o_ref):
        o_ref[...] = (a_ref[...] + b_ref[...]).astype(out_dtype)

    return pl.pallas_call(
        body,
        out_shape=jax.ShapeDtypeStruct(recv.shape, out_dtype),
        grid_spec=pltpu.PrefetchScalarGridSpec(
            num_scalar_prefetch=1,
            grid=(steps,),
            in_specs=[pl.BlockSpec((4, th, cols), lambda i, c_ref: (0, c_ref[0] * steps + i, 0)),
                      pl.BlockSpec((4, th, cols), lambda i, c_ref: (0, i, 0))],
            out_specs=pl.BlockSpec((4, th, cols), lambda i, c_ref: (0, i, 0)),
        ),
        compiler_params=_params(("parallel",), 3 * _nbytes((4, th, cols), F32)),
        name="grad_pair_add_big",
    )(c, gp, recv)


def _pair_add_small(gps, recvs):
    n = len(gps)

    def body(*refs):
        c = lax.axis_index("c")
        for t in range(n):
            g_ref, r_ref, o_ref = refs[t], refs[n + t], refs[2 * n + t]
            half = r_ref.shape[1]
            s = g_ref[:, pl.ds(pl.multiple_of(c * half, 8), half), :] + r_ref[...]
            o_ref[...] = s.astype(o_ref.dtype)

    return pl.pallas_call(
        body,
        out_shape=[jax.ShapeDtypeStruct(r.shape, _wire_dtype(r.shape)) for r in recvs],
        in_specs=[_VMEM_WHOLE] * (2 * n),
        out_specs=[_VMEM_WHOLE] * n,
        compiler_params=_params_whole(list(gps) + 2 * list(recvs)),
        name="grad_pair_add_small",
    )(*gps, *recvs)


def _chip_order_sum(landed_ref, own_ref, me):
    p = [jnp.where(me == k, own_ref[k], landed_ref[k]).astype(F32) for k in range(4)]
    return ((p[0] + p[1]) + p[2]) + p[3]


def _sum_chips_big(landed, own, pos):
    _, half, cols = landed.shape
    th = _div_tile(half, 64, 16)

    def body(pos_ref, l_ref, s_ref, o_ref):
        o_ref[0] = _chip_order_sum(l_ref, s_ref, pos_ref[1])

    spec = pl.BlockSpec((4, th, cols), lambda i, pos_ref: (0, i, 0))
    return pl.pallas_call(
        body,
        out_shape=jax.ShapeDtypeStruct((2, half, cols), F32),
        grid_spec=pltpu.PrefetchScalarGridSpec(
            num_scalar_prefetch=1,
            grid=(half // th,),
            in_specs=[spec, spec],
            out_specs=pl.BlockSpec((1, th, cols), lambda i, pos_ref: (pos_ref[0], i, 0)),
        ),
        compiler_params=_params(("parallel",), 3 * _nbytes((4, th, cols), F32)),
        name="grad_sum_chips_big",
    )(pos, landed, own)


def _sum_chips_small(landed, own):
    n = len(landed)

    def body(*refs):
        x, y, c = _mesh_pos()
        for t in range(n):
            refs[2 * n + t][c] = _chip_order_sum(refs[t], refs[n + t], 2 * x + y)

    return pl.pallas_call(
        body,
        out_shape=[jax.ShapeDtypeStruct((2,) + p.shape[1:], F32) for p in landed],
        in_specs=[_VMEM_WHOLE] * (2 * n),
        out_specs=[_VMEM_WHOLE] * n,
        compiler_params=_params_whole(list(landed) * 3),
        name="grad_sum_chips_small",
    )(*landed, *own)


def _adamw_update(w_ref, g_ref, m_ref, v_ref, d_ref, mo_ref, vo_ref):
    c1 = 1.0 - ADAM_B1 ** ADAM_STEP
    c2 = 1.0 - ADAM_B2 ** ADAM_STEP
    gv = g_ref[...]
    mn = ADAM_B1 * m_ref[...] + (1.0 - ADAM_B1) * gv
    vn = ADAM_B2 * v_ref[...] + (1.0 - ADAM_B2) * (gv * gv)
    mo_ref[...] = mn
    vo_ref[...] = vn
    d_ref[...] = -ADAM_LR * ((mn / c1) / (jnp.sqrt(vn / c2) + ADAM_EPS) + ADAM_WD * w_ref[...])


def _adamw_big(w, g, m, v):
    lead, (rows, cols) = w.shape[:-2], w.shape[-2:]
    assert all(n == 1 for n in lead)
    tr = _div_tile(rows, (1 << 19) // cols, 8)
    spec = pl.BlockSpec((1,) * len(lead) + (tr, cols), lambda i: (0,) * len(lead) + (i, 0))
    shp = jax.ShapeDtypeStruct(w.shape, F32)
    return pl.pallas_call(
        functools.partial(_adamw_update),
        out_shape=(shp, shp, shp),
        grid=(rows // tr,),
        in_specs=[spec] * 4,
        out_specs=(spec, spec, spec),
        compiler_params=_params(("parallel",), 7 * _nbytes((tr, cols), F32)),
        name="adamw_big",
    )(w, g, m, v)


def _adamw_small(ws, gs, ms, vs):
    n = len(ws)

    def body(*refs):
        for t in range(n):
            _adamw_update(refs[t], refs[n + t], refs[2 * n + t], refs[3 * n + t],
                          refs[4 * n + t], refs[5 * n + t], refs[6 * n + t])

    shapes = [jax.ShapeDtypeStruct(w.shape, F32) for w in ws]
    return pl.pallas_call(
        body,
        out_shape=shapes * 3,
        in_specs=[_VMEM_WHOLE] * (4 * n),
        out_specs=[_VMEM_WHOLE] * (3 * n),
        compiler_params=_params_whole(list(ws) * 7),
        name="adamw_small",
    )(*ws, *gs, *ms, *vs)


def _mesh_pos():
    return lax.axis_index("x"), lax.axis_index("y"), lax.axis_index("c")


def _other_chips(x, y):
    return [(1 - x, y), (x, 1 - y), (1 - x, 1 - y)]


_ANY = pl.BlockSpec(memory_space=pl.ANY)


PAIR_SPLIT_MIN_ROWS = 64


def _weight_gather(shards):
    n = len(shards)
    split = [s.shape[0] >= PAIR_SPLIT_MIN_ROWS for s in shards]

    def body(*refs):
        w_refs, o_refs = refs[:n], refs[n:2 * n]
        send_sems, recv_sems = refs[2 * n:]
        x, y, c = _mesh_pos()
        me = 2 * x + y
        chips = _other_chips(x, y)

        def rows_of(t, core):
            rows = shards[t].shape[0]
            if not split[t]:
                return pl.ds(0, rows)
            return pl.ds(pl.multiple_of(core * (rows // 2), 16), rows // 2)

        def landed(t, k, slot, rows, to):
            ref = o_refs[t].at[slot, rows]
            return pltpu.make_async_remote_copy(src_ref=ref, dst_ref=ref, send_sem=send_sems.at[6 * t + k],
                                                recv_sem=recv_sems.at[6 * t + k], device_id=to, device_id_type=MESH)

        sends = []
        for t in range(n):
            mine = rows_of(t, c)
            for k, (px, py) in enumerate(chips):
                cp = pltpu.make_async_remote_copy(src_ref=w_refs[t].at[mine], dst_ref=o_refs[t].at[me, mine],
                                                  send_sem=send_sems.at[6 * t + k], recv_sem=recv_sems.at[6 * t + k],
                                                  device_id=(px, py, c), device_id_type=MESH)
                cp.start()
                sends.append(cp)
        for t in range(n):
            mine = rows_of(t, c)
            for k, (px, py) in enumerate(chips):
                landed(t, k, 2 * px + py, mine, (x, y, c)).wait_recv()
                if split[t]:
                    cp = landed(t, 3 + k, 2 * px + py, mine, (x, y, 1 - c))
                    cp.start()
                    sends.append(cp)
        for t in range(n):
            if split[t]:
                for k, (px, py) in enumerate(chips):
                    landed(t, 3 + k, 2 * px + py, rows_of(t, 1 - c), (x, y, c)).wait_recv()
        for cp in sends:
            cp.wait_send()

    return pl.pallas_call(
        body,
        out_shape=[jax.ShapeDtypeStruct((4,) + s.shape, s.dtype) for s in shards],
        in_specs=[_ANY] * n,
        out_specs=[_ANY] * n,
        scratch_shapes=[pltpu.SemaphoreType.DMA((6 * n,)), pltpu.SemaphoreType.DMA((6 * n,))],
        name="weight_gather",
    )(*shards)


def _pair_swap(gps):
    n = len(gps)

    def body(*refs):
        g_refs, o_refs = refs[:n], refs[n:2 * n]
        send_sems, recv_sems = refs[2 * n:]
        x, y, c = _mesh_pos()
        copies = []
        for t in range(n):
            half = gps[t].shape[1] // 2
            theirs = pl.ds(pl.multiple_of((1 - c) * half, 8), half)
            cp = pltpu.make_async_remote_copy(src_ref=g_refs[t].at[:, theirs], dst_ref=o_refs[t],
                                              send_sem=send_sems.at[t], recv_sem=recv_sems.at[t],
                                              device_id=(x, y, 1 - c), device_id_type=MESH)
            cp.start()
            copies.append(cp)
        for cp in copies:
            cp.wait_send()
            cp.wait_recv()

    return pl.pallas_call(
        body,
        out_shape=[jax.ShapeDtypeStruct((4, g.shape[1] // 2, g.shape[2]), g.dtype) for g in gps],
        in_specs=[_ANY] * n,
        out_specs=[_ANY] * n,
        scratch_shapes=[pltpu.SemaphoreType.DMA((n,)), pltpu.SemaphoreType.DMA((n,))],
        name="grad_pair_swap",
    )(*gps)


_HBM = pl.BlockSpec(memory_space=pltpu.HBM)
_SEM = pl.BlockSpec(memory_space=pltpu.SEMAPHORE)


def _in_hbm(a):
    return pltpu.with_memory_space_constraint(a, pltpu.HBM)


def _chip_scatter_start(parts):
    n = len(parts)

    def body(*refs):
        s_refs, l_refs = refs[:n], refs[n:2 * n]
        send_sems, recv_sems = refs[2 * n], refs[2 * n + 1]
        token = refs[-1]
        x, y, c = _mesh_pos()
        me = 2 * x + y
        for t in range(n):
            for k, (px, py) in enumerate(_other_chips(x, y)):
                pltpu.make_async_remote_copy(src_ref=s_refs[t].at[2 * px + py], dst_ref=l_refs[t].at[me],
                                             send_sem=send_sems.at[3 * t + k], recv_sem=recv_sems.at[3 * t + k],
                                             device_id=(px, py, c), device_id_type=MESH).start()
        token[...] = jnp.zeros_like(token)

    hbm = [pltpu.HBM(p.shape, p.dtype) for p in parts]
    outs = pl.pallas_call(
        body,
        name="grad_scatter_start",
        out_shape=(pltpu.SemaphoreType.DMA((3 * n,)), pltpu.SemaphoreType.DMA((3 * n,)), *hbm, *hbm,
                   jax.ShapeDtypeStruct((8, LANES), F32)),
        in_specs=[_HBM] * (2 * n),
        out_specs=(_SEM, _SEM, *([_HBM] * (2 * n)), pl.BlockSpec(memory_space=pltpu.VMEM)),
        input_output_aliases={i: 2 + i for i in range(2 * n)},
        compiler_params=pltpu.CompilerParams(has_side_effects=pltpu.SideEffectType.DATAFLOW_SIDE_EFFECTING),
    )(*[_in_hbm(p) for p in parts], *[_in_hbm(lax.empty(p.shape, p.dtype)) for p in parts])
    return outs[0], outs[1], list(outs[2:2 + n]), list(outs[2 + n:2 + 2 * n]), outs[-1]


def _chip_scatter_wait(send_sems, recv_sems, parts, lands, after):
    n = len(parts)

    def body(*refs):
        s_refs, l_refs = refs[:n], refs[n:2 * n]
        send_sems, recv_sems = refs[2 * n], refs[2 * n + 1]
        x, y, c = _mesh_pos()
        me = 2 * x + y
        for t in range(n):
            for k, (px, py) in enumerate(_other_chips(x, y)):
                cp = pltpu.make_async_remote_copy(src_ref=s_refs[t].at[2 * px + py], dst_ref=l_refs[t].at[2 * px + py],
                                                  send_sem=send_sems.at[3 * t + k], recv_sem=recv_sems.at[3 * t + k],
                                                  device_id=(x, y, c), device_id_type=MESH)
                cp.wait_send()
                cp.wait_recv()

    hbm = [pltpu.HBM(p.shape, p.dtype) for p in parts]
    outs = pl.pallas_call(
        body,
        name="grad_scatter_wait",
        out_shape=(*hbm, *hbm),
        in_specs=[_HBM] * (2 * n) + [_SEM, _SEM, _ANY],
        out_specs=[_HBM] * (2 * n),
        input_output_aliases={i: i for i in range(2 * n)},
        compiler_params=pltpu.CompilerParams(has_side_effects=pltpu.SideEffectType.DATAFLOW_SIDE_EFFECTING),
    )(*parts, *lands, send_sems, recv_sems, after)
    return list(outs[:n]), list(outs[n:])


def _late_gather_start(shards):
    n = len(shards)

    def body(*refs):
        w_refs, l_refs = refs[:n], refs[n:2 * n]
        send_sems, recv_sems = refs[2 * n], refs[2 * n + 1]
        token = refs[-1]
        x, y, c = _mesh_pos()
        me = 2 * x + y
        for t in range(n):
            for k, (px, py) in enumerate(_other_chips(x, y)):
                pltpu.make_async_remote_copy(src_ref=w_refs[t], dst_ref=l_refs[t].at[me],
                                             send_sem=send_sems.at[3 * t + k], recv_sem=recv_sems.at[3 * t + k],
                                             device_id=(px, py, c), device_id_type=MESH).start()
        token[...] = jnp.zeros_like(token)

    src = [pltpu.HBM(s.shape, s.dtype) for s in shards]
    land = [pltpu.HBM((4,) + s.shape, s.dtype) for s in shards]
    outs = pl.pallas_call(
        body,
        name="late_gather_start",
        out_shape=(pltpu.SemaphoreType.DMA((3 * n,)), pltpu.SemaphoreType.DMA((3 * n,)), *src, *land,
                   jax.ShapeDtypeStruct((8, LANES), F32)),
        in_specs=[_HBM] * (2 * n),
        out_specs=(_SEM, _SEM, *([_HBM] * (2 * n)), pl.BlockSpec(memory_space=pltpu.VMEM)),
        input_output_aliases={i: 2 + i for i in range(2 * n)},
        compiler_params=pltpu.CompilerParams(has_side_effects=pltpu.SideEffectType.DATAFLOW_SIDE_EFFECTING),
    )(*[_in_hbm(s) for s in shards], *[_in_hbm(lax.empty((4,) + s.shape, s.dtype)) for s in shards])
    return outs[0], outs[1], list(outs[2:2 + n]), list(outs[2 + n:2 + 2 * n]), outs[-1]


def _late_gather_wait(send_sems, recv_sems, shards, lands, after):
    n = len(shards)

    def body(*refs):
        w_refs, l_refs = refs[:n], refs[n:2 * n]
        send_sems, recv_sems = refs[2 * n], refs[2 * n + 1]
        x, y, c = _mesh_pos()
        for t in range(n):
            for k, (px, py) in enumerate(_other_chips(x, y)):
                cp = pltpu.make_async_remote_copy(src_ref=w_refs[t], dst_ref=l_refs[t].at[2 * px + py],
                                                  send_sem=send_sems.at[3 * t + k], recv_sem=recv_sems.at[3 * t + k],
                                                  device_id=(x, y, c), device_id_type=MESH)
                cp.wait_send()
                cp.wait_recv()

    src = [pltpu.HBM(s.shape, s.dtype) for s in shards]
    land = [pltpu.HBM(l.shape, l.dtype) for l in lands]
    outs = pl.pallas_call(
        body,
        name="late_gather_wait",
        out_shape=(*src, *land),
        in_specs=[_HBM] * (2 * n) + [_SEM, _SEM, _ANY],
        out_specs=[_HBM] * (2 * n),
        input_output_aliases={i: i for i in range(2 * n)},
        compiler_params=pltpu.CompilerParams(has_side_effects=pltpu.SideEffectType.DATAFLOW_SIDE_EFFECTING),
    )(*shards, *lands, send_sems, recv_sems, after)
    return list(outs[n:])


def _all_to_all_small(parts):
    n = len(parts)

    def body(*refs):
        p_refs, o_refs = refs[:n], refs[n:2 * n]
        send_sems, recv_sems = refs[2 * n:]
        x, y, c = _mesh_pos()
        me = 4 * x + 2 * y + c
        sends = []
        for t in range(n):
            for k in range(1, 8):
                px, py, pc = x ^ (k >> 2), y ^ ((k >> 1) & 1), c ^ (k & 1)
                cp = pltpu.make_async_remote_copy(src_ref=p_refs[t], dst_ref=o_refs[t].at[me],
                                                  send_sem=send_sems.at[7 * t + k - 1], recv_sem=recv_sems.at[7 * t + k - 1],
                                                  device_id=(px, py, pc), device_id_type=MESH)
                cp.start()
                sends.append(cp)
        for t in range(n):
            for k in range(1, 8):
                peer = 4 * (x ^ (k >> 2)) + 2 * (y ^ ((k >> 1) & 1)) + (c ^ (k & 1))
                pltpu.make_async_remote_copy(src_ref=p_refs[t], dst_ref=o_refs[t].at[peer],
                                             send_sem=send_sems.at[7 * t + k - 1], recv_sem=recv_sems.at[7 * t + k - 1],
                                             device_id=(x, y, c), device_id_type=MESH).wait_recv()
        for cp in sends:
            cp.wait_send()

    return pl.pallas_call(
        body,
        out_shape=[jax.ShapeDtypeStruct((8,) + p.shape, p.dtype) for p in parts],
        in_specs=[_ANY] * n,
        out_specs=[_ANY] * n,
        scratch_shapes=[pltpu.SemaphoreType.DMA((7 * n,)), pltpu.SemaphoreType.DMA((7 * n,))],
        name="grad_small_all_to_all",
    )(*parts)


def _sum_devices_small(landed, own):
    n = len(landed)

    def body(*refs):
        x, y, c = _mesh_pos()
        me = 4 * x + 2 * y + c
        for t in range(n):
            acc = jnp.where(me == 0, refs[n + t][...], refs[t][0])
            for d in range(1, 8):
                acc = acc + jnp.where(me == d, refs[n + t][...], refs[t][d])
            refs[2 * n + t][...] = acc

    return pl.pallas_call(
        body,
        out_shape=[jax.ShapeDtypeStruct(p.shape, F32) for p in own],
        in_specs=[_VMEM_WHOLE] * (2 * n),
        out_specs=[_VMEM_WHOLE] * n,
        compiler_params=_params_whole(list(landed) + 2 * list(own)),
        name="grad_sum_devices_small",
    )(*landed, *own)


def _pair_join(fs):
    n = len(fs)

    def body(*refs):
        f_refs, o_refs = refs[:n], refs[n:2 * n]
        send_sems, recv_sems = refs[2 * n:]
        x, y, c = _mesh_pos()
        sends = []
        for t in range(n):
            cp = pltpu.make_async_remote_copy(src_ref=f_refs[t].at[c], dst_ref=o_refs[t].at[c], send_sem=send_sems.at[t],
                                              recv_sem=recv_sems.at[t], device_id=(x, y, 1 - c), device_id_type=MESH)
            cp.start()
            sends.append(cp)
        for t in range(n):
            pltpu.make_async_remote_copy(src_ref=f_refs[t].at[c], dst_ref=o_refs[t].at[1 - c], send_sem=send_sems.at[t],
                                         recv_sem=recv_sems.at[t], device_id=(x, y, c), device_id_type=MESH).wait_recv()
        for cp in sends:
            cp.wait_send()

    return pl.pallas_call(
        body,
        out_shape=[jax.ShapeDtypeStruct(f.shape, f.dtype) for f in fs],
        in_specs=[_ANY] * n,
        out_specs=[_ANY] * n,
        input_output_aliases={t: t for t in range(n)},
        scratch_shapes=[pltpu.SemaphoreType.DMA((n,)), pltpu.SemaphoreType.DMA((n,))],
        name="grad_pair_join",
    )(*fs)


def _rope_tables(Lp):
    inv = 1.0 / (ROPE_BASE ** (jnp.arange(0, ROPE, 2, dtype=F32) / ROPE))
    ang = (jnp.arange(Lp, dtype=F32) - FRONT)[:, None] * inv[None, :]
    cs, sn = jnp.cos(ang), jnp.sin(ang)
    return jnp.tile(cs, (1, 4)), jnp.concatenate([-sn, sn, -sn, sn], axis=1)


def _local_step(x, loss_target, meta, norm_g, w_in, gate_w, gate_b, gla_norm_g, gla_proj, q_norm_g, w_uq,
                kv_norm_g, w_ukv, mla_proj, w_out, final_norm_g, early_grads_hook=None, late_weights_hook=None):
    B, seq, _ = x.shape
    Lp = HEAD_ROWS + seq
    T = B * Lp
    tr = _div_tile(Lp, 544, 16)
    tkw = _div_tile(T, Lp, QB)
    tm_sq = _div_tile(T, 1024, QB)

    cuts = np.cumsum((0,) + SPLITS)
    shard_w = IN_WIDTH // 4

    def w_cols(i, width=None):
        parts = []
        for j in range(4):
            a, b = max(cuts[i], j * shard_w), min(cuts[i + 1], (j + 1) * shard_w)
            if a < b:
                parts.append(w_in[j][:, a - j * shard_w:b - j * shard_w])
        if width is not None:
            parts.append(jnp.zeros((D, width - (cuts[i + 1] - cuts[i])), w_in.dtype))
        return parts

    i_q, i_k, i_v, i_lr, i_z, i_cq, i_ckv, i_kr, i_mz, i_gg, i_gm = range(11)
    wA = jnp.concatenate(sum([w_cols(i) for i in (i_v, i_z, i_mz, i_gg, i_gm, i_q, i_k)], []), axis=1)
    wB = jnp.concatenate(w_cols(i_cq) + w_cols(i_ckv) + w_cols(i_lr, 128) + w_cols(i_kr, 128), axis=1)
    gn4 = jnp.tile(gla_norm_g, (1, GLA_H))
    cos_t, sin_t = _rope_tables(Lp)

    u = _rms_in(x, meta, norm_g, B, Lp)
    projA = _mm(u, wA, name="in_proj_a", out_dtype=BF16, tm=tkw, tn=1024, tk=D)
    projB = _mm(u, wB, name="in_proj_b", out_dtype=BF16, tm=tkw, tn=640, tk=D)
    if late_weights_hook is not None:
        gate_w, gla_proj, w_uq, w_ukv, mla_proj, w_out = late_weights_hook(projA)
    wg = jnp.pad(gate_w, ((0, 128 - GLA_RANK), (0, 0)))
    wuq2 = jnp.pad(w_uq.reshape(Q_RANK, MLA_H, MLA_QK), ((0, 0), (0, 0), (0, 256 - MLA_QK))).reshape(Q_RANK, 2048)
    oa, ya_in, ssave = _gla_fwd(projA, projB, wg, gate_b, gn4, B, Lp)
    ya = _mm(ya_in, gla_proj, name="gla_proj", out_dtype=BF16, tm=tm_sq, tn=D, tk=D)
    q_att, k_att, v_att, cqn, ckvn = _mla_prep(projB, cos_t, sin_t, q_norm_g, kv_norm_g, wuq2, w_ukv, B, Lp, tr)
    ob, yb_in, lse_c = _attn_fwd(q_att, k_att, v_att, projA, B, Lp)
    yb = _mm(yb_in, mla_proj, name="mla_proj", out_dtype=BF16, tm=tm_sq, tn=D, tk=D)
    dh1_b, merged, loss, d_gf = _out_proj_loss(x, meta, projA, ya, yb, w_out, final_norm_g.reshape(1, D),
                                                loss_target, B, Lp)

    g_w_out = _mm(merged, dh1_b, name="dw_out", trans_a=True, tm=D, tn=D, tk=tkw)
    dya, dyb, dA = _merge_bwd(dh1_b, w_out, projA, ya, yb, tr)
    g_gla_proj = _mm(ya_in, dya, name="dw_gla_proj", trans_a=True, tm=D, tn=D, tk=tkw)
    g_mla_proj = _mm(yb_in, dyb, name="dw_mla_proj", trans_a=True, tm=D, tn=D, tk=tkw)
    doa, dBz, d_gn = _gla_out_bwd(dya, gla_proj, oa, projA, gn4, tr)
    dC, g_wg, d_bg = _gla_bwd(projA, projB, ssave, doa, wg, gate_b, B, Lp)
    do, dDz, delta_c = _attn_bwd_pre(dyb, mla_proj, projA, ob, B, Lp)
    dq, dk, dv = _attn_bwd(q_att, k_att, v_att, do, lse_c, delta_c, B, Lp)
    dqf, dkvf, dE, d_gq, d_gkv = _mla_bwd_post(dq, dk, dv, projB, cos_t, sin_t, q_norm_g, kv_norm_g,
                                                wuq2, w_ukv, B, Lp, tr)
    g_wuq2 = _mm(cqn, dqf, name="dw_uq", trans_a=True, tm=Q_RANK, tn=2048, tk=tkw)
    g_wukv = _mm(ckvn, dkvf, name="dw_ukv", trans_a=True, tm=KV_RANK, tn=2048, tk=tkw)
    dparts = [dA, dBz, dC, dDz, dE]
    g_in = [_mm(u, dp, name="dw_in_%d" % i, trans_a=True, tm=D, tn=_div_tile(dp.shape[1], 1024, 256), tk=tkw)
            for i, dp in enumerate(dparts)]

    gA, gBz, gC, gDz, gE = g_in
    src = [(gC, 1024), (gC, 1536), (gC, 0), (gC, 2048), (gBz, 0), (gE, 0), (gE, Q_RANK), (gE, 384), (gDz, 0),
           (gA, 0), (gA, D)]
    owners = []
    for j in range(4):
        parts = []
        for i, (arr, off) in enumerate(src):
            a, b = max(cuts[i], j * shard_w), min(cuts[i + 1], (j + 1) * shard_w)
            if a < b:
                parts.append(arr[:, off + a - cuts[i]:off + b - cuts[i]])
        owners.append(jnp.concatenate(parts, axis=1))
    g_w_in = jnp.stack(owners)
    g_wuq = g_wuq2.reshape(Q_RANK, MLA_H, 256)[:, :, :MLA_QK].reshape(Q_RANK, MLA_H * MLA_QK)
    grads = dict(w_in=g_w_in, gla_gate_w=g_wg[:GLA_RANK], gla_proj=g_gla_proj, mla_w_uq=g_wuq, mla_w_ukv=g_wukv,
                 mla_proj=g_mla_proj, w_out=g_w_out, gla_gate_b=d_bg,
                 gla_norm_g=d_gn, mla_q_norm_g=d_gq, mla_kv_norm_g=d_gkv, final_norm_g=d_gf)
    token = None if early_grads_hook is None else early_grads_hook(grads)
    ng = norm_g if token is None else norm_g + token[0:1, 0:1]
    grad_x, d_meta, d_ng = _in_proj_bwd(x, meta, dh1_b, dA, dBz, dC, dDz, dE, wA, wB, ng, B, Lp)
    grads.update(meta_tokens=d_meta, norm_g=d_ng)
    return loss[0, 0], grad_x, grads


_MATS = ("w_in", "gla_gate_w", "gla_proj", "mla_w_uq", "mla_w_ukv", "mla_proj", "w_out")
_ROW_SHARDED = ("gla_proj", "mla_proj", "w_out")
_ORDER = ("meta_tokens", "norm_g", "w_in", "gla_gate_w", "gla_gate_b", "gla_norm_g", "gla_proj", "mla_q_norm_g",
          "mla_w_uq", "mla_kv_norm_g", "mla_w_ukv", "mla_proj", "w_out", "final_norm_g")
WIRE_BF16_MIN_ELEMS = 128 * 128
SMALL_PACK_ROWS = 16


def _pack_small(d):
    rows = [jnp.pad(d[n].reshape(1, size), ((0, 0), (0, D - size))) for n, size in SMALL]
    return jnp.pad(jnp.concatenate(rows, axis=0), ((0, SMALL_PACK_ROWS - len(rows)), (0, 0)))


def _unpack_small(packed):
    return {n: packed[i, :size] for i, (n, size) in enumerate(SMALL)}


def kernel(x, meta_tokens, norm_g, w_in, gla_gate_w, gla_gate_b, gla_norm_g, gla_proj, mla_q_norm_g, mla_w_uq, mla_kv_norm_g, mla_w_ukv, mla_proj, w_out, final_norm_g, loss_target, m_meta_tokens, m_norm_g, m_w_in, m_gla_gate_w, m_gla_gate_b, m_gla_norm_g, m_gla_proj, m_mla_q_norm_g, m_mla_w_uq, m_mla_kv_norm_g, m_mla_w_ukv, m_mla_proj, m_w_out, m_final_norm_g, v_meta_tokens, v_norm_g, v_w_in, v_gla_gate_w, v_gla_gate_b, v_gla_norm_g, v_gla_proj, v_mla_q_norm_g, v_mla_w_uq, v_mla_kv_norm_g, v_mla_w_ukv, v_mla_proj, v_w_out, v_final_norm_g):
    w = dict(meta_tokens=meta_tokens, norm_g=norm_g, w_in=w_in[0], gla_gate_w=gla_gate_w[0], gla_gate_b=gla_gate_b,
             gla_norm_g=gla_norm_g, gla_proj=gla_proj[0], mla_q_norm_g=mla_q_norm_g, mla_w_uq=mla_w_uq[0],
             mla_kv_norm_g=mla_kv_norm_g, mla_w_ukv=mla_w_ukv[0], mla_proj=mla_proj[0], w_out=w_out[0],
             final_norm_g=final_norm_g)
    mom = dict(meta_tokens=m_meta_tokens, norm_g=m_norm_g, w_in=m_w_in[0], gla_gate_w=m_gla_gate_w[0],
               gla_gate_b=m_gla_gate_b, gla_norm_g=m_gla_norm_g, gla_proj=m_gla_proj[0], mla_q_norm_g=m_mla_q_norm_g,
               mla_w_uq=m_mla_w_uq[0], mla_kv_norm_g=m_mla_kv_norm_g, mla_w_ukv=m_mla_w_ukv[0], mla_proj=m_mla_proj[0],
               w_out=m_w_out[0], final_norm_g=m_final_norm_g)
    var = dict(meta_tokens=v_meta_tokens, norm_g=v_norm_g, w_in=v_w_in[0], gla_gate_w=v_gla_gate_w[0],
               gla_gate_b=v_gla_gate_b, gla_norm_g=v_gla_norm_g, gla_proj=v_gla_proj[0], mla_q_norm_g=v_mla_q_norm_g,
               mla_w_uq=v_mla_w_uq[0], mla_kv_norm_g=v_mla_kv_norm_g, mla_w_ukv=v_mla_w_ukv[0], mla_proj=v_mla_proj[0],
               w_out=v_w_out[0], final_norm_g=v_final_norm_g)
    out_shapes = {n: a.shape for n, a in zip(_ORDER, (meta_tokens, norm_g, w_in, gla_gate_w, gla_gate_b, gla_norm_g,
                                                     gla_proj, mla_q_norm_g, mla_w_uq, mla_kv_norm_g, mla_w_ukv,
                                                     mla_proj, w_out, final_norm_g))}

    me = (2 * lax.axis_index("x") + lax.axis_index("y")).astype(jnp.int32)
    is_mine = lax.broadcasted_iota(jnp.int32, (4, 1, 1), 0) == me
    with_own = lambda gth, own: jnp.where(is_mine, own[None], gth)
    first = [w["w_in"].astype(BF16), meta_tokens]
    w_in_owner, meta_owner = [with_own(gth, own) for gth, own in zip(_weight_gather(first), first)]
    meta_full = meta_owner.transpose(1, 0, 2).reshape(N_META, D)
    late_names = _MATS[1:]
    late = [w[n].astype(BF16) for n in late_names]
    gather_sems = _late_gather_start(late)

    def late_weights(after):
        lands = _late_gather_wait(gather_sems[0], gather_sems[1], gather_sems[2], gather_sems[3], after)
        full = []
        for name, land, own in zip(late_names, lands, late):
            gth = with_own(land, own)
            if name in _ROW_SHARDED:
                full.append(gth.reshape(4 * gth.shape[1], gth.shape[2]))
            else:
                full.append(gth.transpose(1, 0, 2).reshape(gth.shape[1], 4 * gth.shape[2]))
        return full

    def by_owner(name, arr):
        if name == "w_in":
            return arr
        if name in _ROW_SHARDED:
            return arr.reshape(4, arr.shape[0] // 4, arr.shape[1])
        return arr.reshape(arr.shape[0], 4, arr.shape[1] // 4).transpose(1, 0, 2)

    c_idx = lax.axis_index("c").astype(jnp.int32).reshape(1)
    pos = jnp.stack([c_idx[0], me])
    in_flight = {}

    def start_matrix_reduce(early):
        gps = [by_owner(n, early[n]) for n in _MATS]
        recvs = _pair_swap(gps)
        s1 = [_pair_add_big(gps[0], recvs[0], c_idx)] + list(_pair_add_small(gps[1:], recvs[1:]))
        send_sems, recv_sems, parts, lands, token = _chip_scatter_start(s1)
        in_flight.update(send_sems=send_sems, recv_sems=recv_sems, parts=parts, lands=lands)
        return token

    norm_g_after_start = norm_g + gather_sems[4][0:1, 0:1]
    loss_local, grad_x, g = _local_step(
        x, loss_target, meta_full, norm_g_after_start, w_in_owner, None, gla_gate_b, gla_norm_g, None,
        mla_q_norm_g, None, mla_kv_norm_g, None, None, None, final_norm_g,
        early_grads_hook=start_matrix_reduce, late_weights_hook=late_weights)
    loss = lax.psum(loss_local, ("x", "y", "c"))

    s1, landed = _chip_scatter_wait(in_flight["send_sems"], in_flight["recv_sems"], in_flight["parts"],
                                    in_flight["lands"], after=g["norm_g"])
    halves = [_sum_chips_big(landed[0], s1[0], pos)] + list(_sum_chips_small(landed[1:], s1[1:]))
    g_mats = [j.reshape(out_shapes[n]) for j, n in zip(_pair_join(halves), _MATS)]

    late = [g["meta_tokens"], _pack_small(g)]
    meta_sum, small_sum = _sum_devices_small(_all_to_all_small(late), late)
    g_meta = lax.dynamic_slice(meta_sum, (0, me * (D // 4)), (N_META, D // 4))
    names = _MATS + ("meta_tokens",)
    g_red = g_mats + [g_meta, small_sum]

    tens = lambda d: [d[n].reshape(out_shapes[n]) for n in names] + [_pack_small(d)]
    w_t, m_t, v_t = tens(w), tens(mom), tens(var)
    big = _adamw_big(w_t[0], g_red[0], m_t[0], v_t[0])
    rest = _adamw_small(w_t[1:], g_red[1:], m_t[1:], v_t[1:])
    k = len(names)
    results = {"grad": g_red}
    for i, kind in enumerate(("delta", "new_m", "new_v")):
        results[kind] = [big[i]] + list(rest[i * k:(i + 1) * k])

    outs = []
    for kind in ("grad", "delta", "new_m", "new_v"):
        vals = dict(zip(names, results[kind][:-1]))
        vals.update(_unpack_small(results[kind][-1]))
        outs += [vals[n].reshape(out_shapes[n]) for n in _ORDER]
    return (loss, grad_x, *outs)
```

```python
import functools
import math

import jax
import jax.numpy as jnp
import numpy as np
from jax import lax
from jax.experimental import pallas as pl
from jax.experimental.pallas import tpu as pltpu

F32 = jnp.float32
BF16 = jnp.bfloat16

D = 1024
N_META = 16
QB = 256
FRONT = QB - N_META
HEAD_ROWS = FRONT + N_META
assert FRONT % 64 == 48
EPS = 1e-6

GLA_H, GLA_DK, GLA_DV, GLA_RANK, GLA_C = 4, 128, 256, 16, 64
GLA_NORMALIZER = 16.0
GLA_KW, GLA_VW = GLA_H * GLA_DK, GLA_H * GLA_DV
MLA_H, NOPE, ROPE, MLA_DV, Q_RANK, KV_RANK = 8, 128, 64, 128, 256, 128
MLA_QK = NOPE + ROPE
ROPE_BASE = 10000.0
SPLITS = (GLA_KW, GLA_KW, GLA_VW, GLA_RANK, GLA_VW, Q_RANK, KV_RANK, ROPE, MLA_H * MLA_DV, D, D)
IN_WIDTH = sum(SPLITS)

ADAM_LR, ADAM_B1, ADAM_B2, ADAM_EPS, ADAM_WD, ADAM_STEP = 0.001, 0.9, 0.999, 1e-08, 0.01, 10

LANES = 128
VMEM_CAP_V7X = 56 * 1024 * 1024
MESH = pl.DeviceIdType.MESH
NEG = -1e30
LOG2E = math.log2(math.e)

SMALL = (("norm_g", D), ("gla_gate_b", GLA_KW), ("gla_norm_g", GLA_DV), ("mla_q_norm_g", Q_RANK),
         ("mla_kv_norm_g", KV_RANK), ("final_norm_g", D))


def _div_tile(n, target, mult):
    best = None
    for d in range(mult, min(n, target) + 1, mult):
        if n % d == 0:
            best = d
    assert best is not None, (n, target, mult)
    return best


def _params(sem, block_bytes, scratch_bytes=0):
    est = 2 * block_bytes + scratch_bytes + 12 * 1024 * 1024
    return pltpu.CompilerParams(dimension_semantics=sem, vmem_limit_bytes=int(min(max(est, 24 * 1024 * 1024), VMEM_CAP_V7X)))


def _nbytes(shape, dtype):
    return int(np.prod(shape)) * jnp.dtype(dtype).itemsize


def _sigmoid(x):
    return 1.0 / (1.0 + jnp.exp(-x))


def _nt(a, b):
    return lax.dot_general(a, b, (((1,), (1,)), ((), ())), preferred_element_type=F32)


def _tn(a, b):
    return lax.dot_general(a, b, (((0,), (0,)), ((), ())), preferred_element_type=F32)


def _nn(a, b):
    return jnp.dot(a, b, preferred_element_type=F32)


def _split3(x):
    a = x.astype(BF16)
    r = x - a.astype(F32)
    b = r.astype(BF16)
    c = (r - b.astype(F32)).astype(BF16)
    return a, b, c


def _mm(a, b, *, name, trans_a=False, trans_b=False, out_dtype=F32, tm, tn, tk):
    assert not (trans_a and trans_b)
    if trans_a:
        K, M = a.shape
    else:
        M, K = a.shape
    N = b.shape[0] if trans_b else b.shape[1]
    assert (b.shape[1] if trans_b else b.shape[0]) == K
    assert M % tm == 0 and N % tn == 0 and K % tk == 0, (name, M, N, K, tm, tn, tk)
    nk = K // tk

    def body(a_ref, b_ref, o_ref, *scratch):
        av = a_ref[...].astype(BF16)
        bv = b_ref[...].astype(BF16)
        prod = _tn(av, bv) if trans_a else (_nt(av, bv) if trans_b else _nn(av, bv))
        if nk == 1:
            o_ref[...] = prod.astype(out_dtype)
        else:
            acc = scratch[0]
            k = pl.program_id(2)

            @pl.when(k == 0)
            def _():
                acc[...] = prod

            @pl.when(k > 0)
            def _():
                acc[...] += prod

            @pl.when(k == nk - 1)
            def _():
                o_ref[...] = acc[...].astype(out_dtype)

    if trans_a:
        a_spec = pl.BlockSpec((tk, tm), lambda i, j, k: (k, i))
    else:
        a_spec = pl.BlockSpec((tm, tk), lambda i, j, k: (i, k))
    if trans_b:
        b_spec = pl.BlockSpec((tn, tk), lambda i, j, k: (j, k))
    else:
        b_spec = pl.BlockSpec((tk, tn), lambda i, j, k: (k, j))
    blocks = (_nbytes((tm, tk), a.dtype) + _nbytes((tk, tn), b.dtype) + _nbytes((tm, tn), out_dtype))
    scratch = [pltpu.VMEM((tm, tn), F32)] if nk > 1 else []
    return pl.pallas_call(
        body,
        out_shape=jax.ShapeDtypeStruct((M, N), out_dtype),
        grid=(M // tm, N // tn, nk),
        in_specs=[a_spec, b_spec],
        out_specs=pl.BlockSpec((tm, tn), lambda i, j, k: (i, j)),
        scratch_shapes=scratch,
        compiler_params=_params(("parallel", "parallel", "arbitrary"), blocks + _nbytes((tm, tn), F32),
                                _nbytes((tm, tn), F32) if nk > 1 else 0),
        name=name,
    )(a, b)


def _h_tile(j, x_ref, meta_ref):
    head = jnp.concatenate([jnp.zeros((FRONT, D), F32), meta_ref[...]], axis=0)
    return jnp.where(j > 0, x_ref[0], head)


def _x_spec():
    return pl.BlockSpec((1, QB, D), lambda b, j: (b, jnp.maximum(j - 1, 0), 0))


def _rms_in(x, meta, g, B, Lp):
    T = B * Lp
    NQ = Lp // QB

    def body(x_ref, meta_ref, g_ref, u_ref):
        h = _h_tile(pl.program_id(1), x_ref, meta_ref)
        r = lax.rsqrt(jnp.mean(h * h, axis=-1, keepdims=True) + EPS)
        u_ref[...] = (h * r * g_ref[...]).astype(BF16)

    return pl.pallas_call(
        body,
        out_shape=jax.ShapeDtypeStruct((T, D), BF16),
        grid=(B, NQ),
        in_specs=[_x_spec(), pl.BlockSpec((N_META, D), lambda b, j: (0, 0)), pl.BlockSpec((1, D), lambda b, j: (0, 0))],
        out_specs=pl.BlockSpec((QB, D), lambda b, j: (b * NQ + j, 0)),
        compiler_params=_params(("parallel", "parallel"), _nbytes((QB, D), F32) * 2),
        name="rms_in",
    )(x, meta, g)


def _gla_gate(lr, wg, bg, valid):
    pre = _nn(lr.astype(BF16), wg) + bg
    logsig = jnp.minimum(pre, 0.0) - jnp.log(1.0 + jnp.exp(-jnp.abs(pre)))
    return pre, jnp.where(valid, logsig / GLA_NORMALIZER, 0.0)


def _tri_masks():
    ri = lax.broadcasted_iota(jnp.int32, (GLA_C, GLA_C), 0)
    ci = lax.broadcasted_iota(jnp.int32, (GLA_C, GLA_C), 1)
    return ci <= ri, ci >= ri


def _cumsum_rows(x, ones_mask):
    w = jnp.where(ones_mask, 1.0, 0.0).astype(BF16)
    a, b, c = _split3(x)
    return _nn(w, a) + _nn(w, b) + _nn(w, c)


def _gla_fwd(projA, projB, wg, bg, gn4, B, Lp):
    T = B * Lp
    NC = Lp // GLA_C
    C = GLA_C
    scale = GLA_DK ** -0.5

    def body(q_ref, k_ref, v_ref, lr_ref, z_ref, wg_ref, bg_ref, gn_ref, oa_ref, ya_ref, ssave_ref, st_ref):
        n = pl.program_id(0)

        @pl.when(n == 0)
        def _():
            st_ref[...] = jnp.zeros_like(st_ref)

        pos = n * C + lax.broadcasted_iota(jnp.int32, (C, 1), 0)
        lower, _ = _tri_masks()
        is_last = lax.broadcasted_iota(jnp.int32, (C, 1), 0) == C - 1
        for b in range(B):
            ssave_ref[b, 0] = st_ref[b]
            _, glog = _gla_gate(lr_ref[b], wg_ref[...], bg_ref[...], pos >= FRONT)
            bcum = _cumsum_rows(glog, lower)
            for h in range(GLA_H):
                ks = slice(h * GLA_DK, (h + 1) * GLA_DK)
                vs = slice(h * GLA_DV, (h + 1) * GLA_DV)
                bh = bcum[:, ks]
                blast = jnp.sum(jnp.where(is_last, bh, 0.0), axis=0, keepdims=True)
                qh = q_ref[b, :, ks].astype(F32) * scale
                kh = k_ref[b, :, ks].astype(F32)
                qe = (qh * jnp.exp(bh)).astype(BF16)
                ke = (kh * jnp.exp(-bh)).astype(BF16)
                kl = (kh * jnp.exp(blast - bh)).astype(BF16)
                vh = v_ref[b, :, vs].astype(BF16)
                a = jnp.where(lower, _nt(qe, ke), 0.0).astype(BF16)
                st = st_ref[b, h]
                o = _nn(a, vh) + _nt(qe, st.astype(BF16))
                st_ref[b, h] = st * jnp.exp(blast) + _tn(vh, kl)
                oa_ref[b, :, vs] = o.astype(BF16)
                on = o * lax.rsqrt(jnp.mean(o * o, axis=-1, keepdims=True) + EPS) * gn_ref[:, vs]
                z = z_ref[b, :, vs].astype(F32)
                ya_ref[b, :, vs] = (on * (z * _sigmoid(z))).astype(BF16)

    blocks = B * (_nbytes((C, 512), F32) * 2 + _nbytes((C, 1024), F32) * 3 + _nbytes((C, 1024), BF16)
                  + _nbytes((GLA_H, GLA_DV, GLA_DK), F32)) + _nbytes((128, 512), BF16)
    state = _nbytes((B, GLA_H, GLA_DV, GLA_DK), F32)
    pa = projA.reshape(B, Lp, projA.shape[1])
    oa, ya, ssave = pl.pallas_call(
        body,
        out_shape=(jax.ShapeDtypeStruct((B, Lp, GLA_VW), BF16), jax.ShapeDtypeStruct((B, Lp, GLA_VW), BF16),
                   jax.ShapeDtypeStruct((B, NC, GLA_H, GLA_DV, GLA_DK), F32)),
        grid=(NC,),
        in_specs=[
            pl.BlockSpec((B, C, 512), lambda n: (0, n, 10)),
            pl.BlockSpec((B, C, 512), lambda n: (0, n, 11)),
            pl.BlockSpec((B, C, 1024), lambda n: (0, n, 0)),
            pl.BlockSpec((B, C, 128), lambda n: (0, n, 3)),
            pl.BlockSpec((B, C, 1024), lambda n: (0, n, 1)),
            pl.BlockSpec((128, 512), lambda n: (0, 0)),
            pl.BlockSpec((1, 512), lambda n: (0, 0)),
            pl.BlockSpec((1, 1024), lambda n: (0, 0)),
        ],
        out_specs=(pl.BlockSpec((B, C, 1024), lambda n: (0, n, 0)),
                   pl.BlockSpec((B, C, 1024), lambda n: (0, n, 0)),
                   pl.BlockSpec((B, 1, GLA_H, GLA_DV, GLA_DK), lambda n: (0, n, 0, 0, 0))),
        scratch_shapes=[pltpu.VMEM((B, GLA_H, GLA_DV, GLA_DK), F32)],
        compiler_params=_params(("arbitrary",), blocks, state),
        name="gla_fwd",
    )(pa, pa, pa, projB.reshape(B, Lp, projB.shape[1]), pa, wg, bg, gn4)
    return oa.reshape(T, GLA_VW), ya.reshape(T, GLA_VW), ssave


def _swap_halves(x):
    lane = lax.broadcasted_iota(jnp.int32, x.shape, 1)
    return jnp.where((lane % 64) < 32, pltpu.roll(x, 96, 1), pltpu.roll(x, 32, 1))


def _mla_prep(projB, cos_t, sin_t, gq, gkv, wuq2, wukv, B, Lp, tr):
    T = B * Lp
    nt = Lp // tr
    HW = 2 * LANES

    def body(pb_ref, cos_ref, sin_ref, gq_ref, gkv_ref, wuq_ref, wukv_ref, q_ref, k_ref, v_ref, cqn_ref, ckvn_ref):
        cq = pb_ref[:, 0:Q_RANK].astype(F32)
        ckv = pb_ref[:, Q_RANK:Q_RANK + KV_RANK].astype(F32)
        kr = pb_ref[:, 512:640].astype(F32)
        cqn = (cq * lax.rsqrt(jnp.mean(cq * cq, axis=-1, keepdims=True) + EPS) * gq_ref[...]).astype(BF16)
        ckvn = (ckv * lax.rsqrt(jnp.mean(ckv * ckv, axis=-1, keepdims=True) + EPS) * gkv_ref[...]).astype(BF16)
        cqn_ref[...] = cqn
        ckvn_ref[...] = ckvn
        qf = _nn(cqn, wuq_ref[...])
        kvf = _nn(ckvn, wukv_ref[...])
        cs = cos_ref[...]
        sn = sin_ref[...]
        rope = lambda t: t * cs + _swap_halves(t) * sn
        kr_r = rope(kr).astype(BF16)
        for h in range(MLA_H):
            q_ref[:, h * HW:h * HW + LANES] = qf[:, h * HW:h * HW + LANES].astype(BF16)
            q_ref[:, h * HW + LANES:(h + 1) * HW] = rope(qf[:, h * HW + LANES:(h + 1) * HW]).astype(BF16)
            k_ref[:, h * HW:h * HW + LANES] = kvf[:, h * HW:h * HW + LANES].astype(BF16)
            k_ref[:, h * HW + LANES:(h + 1) * HW] = kr_r
            v_ref[:, h * MLA_DV:(h + 1) * MLA_DV] = kvf[:, h * HW + LANES:(h + 1) * HW].astype(BF16)

    blocks = (_nbytes((tr, 640), F32) + 2 * _nbytes((tr, 128), F32) + _nbytes((Q_RANK, 2048), BF16)
              + _nbytes((KV_RANK, 2048), BF16) + _nbytes((tr, 2048 * 2 + 1024 + 384), BF16)
              + 2 * _nbytes((tr, 2048), F32))
    return pl.pallas_call(
        body,
        out_shape=(jax.ShapeDtypeStruct((T, MLA_H * HW), BF16), jax.ShapeDtypeStruct((T, MLA_H * HW), BF16),
                   jax.ShapeDtypeStruct((T, MLA_H * MLA_DV), BF16), jax.ShapeDtypeStruct((T, Q_RANK), BF16),
                   jax.ShapeDtypeStruct((T, KV_RANK), BF16)),
        grid=(B, nt),
        in_specs=[
            pl.BlockSpec((tr, 640), lambda b, j: (b * nt + j, 0)),
            pl.BlockSpec((tr, 128), lambda b, j: (j, 0)),
            pl.BlockSpec((tr, 128), lambda b, j: (j, 0)),
            pl.BlockSpec((1, Q_RANK), lambda b, j: (0, 0)),
            pl.BlockSpec((1, KV_RANK), lambda b, j: (0, 0)),
            pl.BlockSpec((Q_RANK, 2048), lambda b, j: (0, 0)),
            pl.BlockSpec((KV_RANK, 2048), lambda b, j: (0, 0)),
        ],
        out_specs=(pl.BlockSpec((tr, 2048), lambda b, j: (b * nt + j, 0)),
                   pl.BlockSpec((tr, 2048), lambda b, j: (b * nt + j, 0)),
                   pl.BlockSpec((tr, 1024), lambda b, j: (b * nt + j, 0)),
                   pl.BlockSpec((tr, Q_RANK), lambda b, j: (b * nt + j, 0)),
                   pl.BlockSpec((tr, KV_RANK), lambda b, j: (b * nt + j, 0))),
        compiler_params=_params(("parallel", "parallel"), blocks),
        name="mla_prep",
    )(projB, cos_t, sin_t, gq, gkv, wuq2, wukv)


def _attn_mask(row, col):
    return (col <= row) & ((col >= FRONT) | (row < FRONT))


def _attn_fwd(q_att, k_att, v_att, projA, B, Lp):
    T = B * Lp
    NQ = Lp // QB
    HW = 2 * LANES
    scale = 1.0 / math.sqrt(MLA_QK)

    def body(q_ref, k_ref, v_ref, mz_ref, o_ref, yb_ref, lsec_ref, m_ref, l_ref, acc_ref):
        qi = pl.program_id(1)
        m_ref[...] = jnp.full(m_ref.shape, NEG, F32)
        l_ref[...] = jnp.zeros_like(l_ref)
        acc_ref[...] = jnp.zeros_like(acc_ref)
        row = qi * QB + lax.broadcasted_iota(jnp.int32, (QB, QB), 0)
        coli = lax.broadcasted_iota(jnp.int32, (QB, QB), 1)

        def step(kj, masked):
            off = pl.multiple_of(kj * QB, QB)
            ok = _attn_mask(row, kj * QB + coli) if masked else None
            for h in range(MLA_H):
                q = q_ref[:, h * HW:(h + 1) * HW]
                kb = k_ref[pl.ds(off, QB), h * HW:(h + 1) * HW]
                vb = v_ref[pl.ds(off, QB), h * MLA_DV:(h + 1) * MLA_DV]
                s = _nt(q, kb) * (scale * LOG2E)
                if masked:
                    s = jnp.where(ok, s, NEG)
                m_old = m_ref[h]
                m_new = jnp.maximum(m_old, jnp.max(s, axis=-1, keepdims=True))
                alpha = jnp.exp2(m_old - m_new)
                p = jnp.exp2(s - jnp.tile(m_new, (1, QB // LANES)))
                m_ref[h] = m_new
                l_ref[h] = alpha * l_ref[h] + jnp.sum(p, axis=-1, keepdims=True)
                acc_ref[h] = alpha * acc_ref[h] + _nn(p.astype(BF16), vb)

        step(0, True)

        def unmasked(kj, carry):
            step(kj, False)
            return carry

        lax.fori_loop(1, qi, unmasked, 0)

        @pl.when(qi > 0)
        def _():
            step(qi, True)

        for h in range(MLA_H):
            hs = slice(h * MLA_DV, (h + 1) * MLA_DV)
            l = l_ref[h]
            o = acc_ref[h] / l
            o_ref[:, hs] = o.astype(BF16)
            z = mz_ref[:, hs].astype(F32)
            yb_ref[:, hs] = (o * (z * _sigmoid(z))).astype(BF16)
            lse2 = m_ref[h] + jnp.log(l) * LOG2E
            lsec_ref[0, h, pl.ds(qi, 1), :] = jnp.transpose(lse2)[0:1, :]

    blocks = (_nbytes((QB, 2048), BF16) + _nbytes((Lp, 2048), BF16) + _nbytes((Lp, 1024), BF16)
              + 2 * _nbytes((QB, 1024), F32) + _nbytes((QB, 1024), BF16) + _nbytes((MLA_H, QB, LANES), F32)
              + _nbytes((MLA_H, NQ, QB), F32))
    return pl.pallas_call(
        body,
        out_shape=(jax.ShapeDtypeStruct((T, MLA_H * MLA_DV), BF16), jax.ShapeDtypeStruct((T, MLA_H * MLA_DV), BF16),
                   jax.ShapeDtypeStruct((B, MLA_H, NQ, QB), F32)),
        grid=(B, NQ),
        in_specs=[
            pl.BlockSpec((QB, MLA_H * HW), lambda b, i: (b * NQ + i, 0)),
            pl.BlockSpec((Lp, MLA_H * HW), lambda b, i: (b, 0)),
            pl.BlockSpec((Lp, MLA_H * MLA_DV), lambda b, i: (b, 0)),
            pl.BlockSpec((QB, 1024), lambda b, i: (b * NQ + i, 2)),
        ],
        out_specs=(pl.BlockSpec((QB, 1024), lambda b, i: (b * NQ + i, 0)),
                   pl.BlockSpec((QB, 1024), lambda b, i: (b * NQ + i, 0)),
                   pl.BlockSpec((1, MLA_H, NQ, QB), lambda b, i: (b, 0, 0, 0))),
        scratch_shapes=[pltpu.VMEM((MLA_H, QB, LANES), F32), pltpu.VMEM((MLA_H, QB, LANES), F32),
                        pltpu.VMEM((MLA_H, QB, MLA_DV), F32)],
        compiler_params=_params(("parallel", "arbitrary"), blocks, 3 * _nbytes((MLA_H, QB, LANES), F32)),
        name="attn_fwd",
    )(q_att, k_att, v_att, projA)


def _out_proj_loss(x, meta, projA, ya, yb, w_out, gf, tgt, B, Lp):
    T = B * Lp
    NQ = Lp // QB

    def body(x_ref, meta_ref, gg_ref, gm_ref, ya_ref, yb_ref, w_ref, gf_ref, t_ref,
             dhb_ref, mg_ref, loss_ref, dgf_ref):
        b = pl.program_id(0)
        j = pl.program_id(1)

        @pl.when((b == 0) & (j == 0))
        def _():
            loss_ref[...] = jnp.zeros_like(loss_ref)
            dgf_ref[...] = jnp.zeros_like(dgf_ref)

        f32 = lambda ref: ref[...].astype(F32)
        merged = (_sigmoid(f32(gg_ref)) * f32(ya_ref) + _sigmoid(f32(gm_ref)) * f32(yb_ref)).astype(BF16)
        mg_ref[...] = merged
        h1 = _h_tile(j, x_ref, meta_ref) + _nn(merged, w_ref[...])
        r = lax.rsqrt(jnp.mean(h1 * h1, axis=-1, keepdims=True) + EPS)
        hn = h1 * r
        gfv = gf_ref[...]
        diff = jnp.where(j > 0, hn * gfv - t_ref[0], 0.0)
        loss_ref[...] += (0.5 / D) * jnp.sum(jnp.sum(diff * diff, axis=-1, keepdims=True), axis=0, keepdims=True)
        dout = diff * (1.0 / D)
        dgf_ref[...] += jnp.sum(dout * hn, axis=0, keepdims=True)
        dhn = dout * gfv
        dh = r * (dhn - hn * jnp.mean(dhn * hn, axis=-1, keepdims=True))
        dhb_ref[...] = dh.astype(BF16)

    rows = lambda c: pl.BlockSpec((QB, D), lambda b, j: (b * NQ + j, c))
    const = lambda s: pl.BlockSpec(s, lambda b, j: (0, 0))
    return pl.pallas_call(
        body,
        out_shape=(jax.ShapeDtypeStruct((T, D), BF16), jax.ShapeDtypeStruct((T, D), BF16),
                   jax.ShapeDtypeStruct((1, 1), F32), jax.ShapeDtypeStruct((1, D), F32)),
        grid=(B, NQ),
        in_specs=[_x_spec(), const((N_META, D)), rows(3), rows(4), rows(0), rows(0), const((D, D)),
                  const((1, D)), _x_spec()],
        out_specs=(rows(0), rows(0), const((1, 1)), const((1, D))),
        compiler_params=_params(("arbitrary", "arbitrary"), 10 * _nbytes((QB, D), F32)),
        name="out_proj_loss",
    )(x, meta, projA, projA, ya, yb, w_out, gf, tgt)


def _merge_bwd(dh1_b, w_out, projA, ya, yb, tr):
    T = dh1_b.shape[0]

    def body(dh_ref, w_ref, gg_ref, gm_ref, ya_ref, yb_ref, dya_ref, dyb_ref, da_ref):
        d = _nt(dh_ref[...], w_ref[...])
        sg = _sigmoid(gg_ref[...].astype(F32))
        sm = _sigmoid(gm_ref[...].astype(F32))
        dya_ref[...] = (d * sg).astype(BF16)
        dyb_ref[...] = (d * sm).astype(BF16)
        da_ref[:, 0:D] = (d * ya_ref[...].astype(F32) * (sg * (1.0 - sg))).astype(BF16)
        da_ref[:, D:2 * D] = (d * yb_ref[...].astype(F32) * (sm * (1.0 - sm))).astype(BF16)

    spec = lambda c: pl.BlockSpec((tr, D), lambda i: (i, c))
    return pl.pallas_call(
        body,
        out_shape=(jax.ShapeDtypeStruct((T, D), BF16), jax.ShapeDtypeStruct((T, D), BF16),
                   jax.ShapeDtypeStruct((T, 2 * D), BF16)),
        grid=(T // tr,),
        in_specs=[spec(0), pl.BlockSpec((D, D), lambda i: (0, 0)), spec(3), spec(4), spec(0), spec(0)],
        out_specs=(spec(0), spec(0), pl.BlockSpec((tr, 2 * D), lambda i: (i, 0))),
        compiler_params=_params(("parallel",), 8 * _nbytes((tr, D), F32)),
        name="merge_bwd",
    )(dh1_b, w_out, projA, projA, ya, yb)


def _gla_out_bwd(dya, gla_proj, oa, projA, gn4, tr):
    T = dya.shape[0]
    nsteps = T // tr

    def body(dya_ref, w_ref, oa_ref, z_ref, gn_ref, do_ref, dz_ref, dgn_ref, acc_ref):
        i = pl.program_id(0)

        @pl.when(i == 0)
        def _():
            acc_ref[...] = jnp.zeros_like(acc_ref)

        dy_all = _nt(dya_ref[...], w_ref[...])
        for h in range(GLA_H):
            vs = slice(h * GLA_DV, (h + 1) * GLA_DV)
            dy = dy_all[:, vs]
            o = oa_ref[:, vs].astype(F32)
            z = z_ref[:, vs].astype(F32)
            gn = gn_ref[:, vs]
            s = _sigmoid(z)
            ra = lax.rsqrt(jnp.mean(o * o, axis=-1, keepdims=True) + EPS)
            on = o * ra
            don = dy * (z * s)
            t = don * gn
            do_ref[:, vs] = (ra * (t - on * jnp.mean(t * on, axis=-1, keepdims=True))).astype(BF16)
            dz_ref[:, vs] = (dy * (on * gn) * (s * (1.0 + z * (1.0 - s)))).astype(BF16)
            acc_ref[:, vs] += jnp.sum(don * on, axis=0, keepdims=True)

        @pl.when(i == nsteps - 1)
        def _():
            a = acc_ref[...]
            dgn_ref[...] = a[:, 0:256] + a[:, 256:512] + a[:, 512:768] + a[:, 768:1024]

    spec = lambda c: pl.BlockSpec((tr, D), lambda i: (i, c))
    return pl.pallas_call(
        body,
        out_shape=(jax.ShapeDtypeStruct((T, D), BF16), jax.ShapeDtypeStruct((T, D), BF16),
                   jax.ShapeDtypeStruct((1, GLA_DV), F32)),
        grid=(nsteps,),
        in_specs=[spec(0), pl.BlockSpec((D, D), lambda i: (0, 0)), spec(0), spec(1),
                  pl.BlockSpec((1, D), lambda i: (0, 0))],
        out_specs=(spec(0), spec(0), pl.BlockSpec((1, GLA_DV), lambda i: (0, 0))),
        scratch_shapes=[pltpu.VMEM((1, D), F32)],
        compiler_params=_params(("arbitrary",), 6 * _nbytes((tr, D), F32)),
        name="gla_out_bwd",
    )(dya, gla_proj, oa, projA, gn4)


def _gla_bwd(projA, projB, ssave, doa, wg, bg, B, Lp):
    T = B * Lp
    NC = Lp // GLA_C
    C = GLA_C
    scale = GLA_DK ** -0.5
    WC = 2304

    def body(q_ref, k_ref, v_ref, lr_ref, ss_ref, do_ref, wg_ref, bg_ref, dc_ref, dwg_ref, dbg_ref, dst_ref):
        i = pl.program_id(0)
        n = NC - 1 - i

        @pl.when(i == 0)
        def _():
            dst_ref[...] = jnp.zeros_like(dst_ref)
            dwg_ref[...] = jnp.zeros_like(dwg_ref)
            dbg_ref[...] = jnp.zeros_like(dbg_ref)

        pos = n * C + lax.broadcasted_iota(jnp.int32, (C, 1), 0)
        valid = pos >= FRONT
        lower, upper = _tri_masks()
        is_last = lax.broadcasted_iota(jnp.int32, (C, 1), 0) == C - 1
        for b in range(B):
            lr = lr_ref[b]
            pre, glog = _gla_gate(lr, wg_ref[...], bg_ref[...], valid)
            bcum = _cumsum_rows(glog, lower)
            db_parts = []
            for h in range(GLA_H):
                ks = slice(h * GLA_DK, (h + 1) * GLA_DK)
                vs = slice(h * GLA_DV, (h + 1) * GLA_DV)
                bh = bcum[:, ks]
                blast = jnp.sum(jnp.where(is_last, bh, 0.0), axis=0, keepdims=True)
                eb, enb, ekl, ebl = jnp.exp(bh), jnp.exp(-bh), jnp.exp(blast - bh), jnp.exp(blast)
                qh = q_ref[b, :, ks].astype(F32) * scale
                kh = k_ref[b, :, ks].astype(F32)
                qe_f, ke_f, kl_f = qh * eb, kh * enb, kh * ekl
                qe, ke, kl = qe_f.astype(BF16), ke_f.astype(BF16), kl_f.astype(BF16)
                vh = v_ref[b, :, vs].astype(BF16)
                doh = do_ref[b, :, vs]
                st = ss_ref[b, 0, h]
                dst = dst_ref[b, h]
                st_b, dst_b = st.astype(BF16), dst.astype(BF16)
                da = jnp.where(lower, _nt(doh, vh), 0.0).astype(BF16)
                da_t = jnp.where(upper, _nt(vh, doh), 0.0).astype(BF16)
                a_t = jnp.where(upper, _nt(ke, qe), 0.0).astype(BF16)
                dqe = _nn(da, ke) + _nn(doh, st_b)
                dke = _nn(da_t, qe)
                dvh = _nn(a_t, doh) + _nt(kl, dst_b)
                dkl = _nn(vh, dst_b)
                dst_ref[b, h] = dst * ebl + _tn(doh, qe)
                deb = jnp.sum(st * dst, axis=0, keepdims=True)
                db = dqe * qe_f - dke * ke_f - dkl * kl_f
                db_last = jnp.sum(dkl * kl_f, axis=0, keepdims=True) + deb * ebl
                db_parts.append(db + jnp.where(is_last, db_last, 0.0))
                dc_ref[b, :, vs] = dvh.astype(BF16)
                dc_ref[b, :, 1024 + h * GLA_DK:1024 + (h + 1) * GLA_DK] = (dqe * eb * scale).astype(BF16)
                dc_ref[b, :, 1536 + h * GLA_DK:1536 + (h + 1) * GLA_DK] = (dke * enb + dkl * ekl).astype(BF16)
            dglog = _cumsum_rows(jnp.concatenate(db_parts, axis=1), upper)
            dpre = jnp.where(valid, dglog * (1.0 / GLA_NORMALIZER) / (1.0 + jnp.exp(pre)), 0.0)
            dpre_b = dpre.astype(BF16)
            dc_ref[b, :, 2048:2176] = _nt(dpre_b, wg_ref[...]).astype(BF16)
            dc_ref[b, :, 2176:2304] = jnp.zeros((C, 128), BF16)
            dwg_ref[...] += _tn(lr.astype(BF16), dpre_b)
            dbg_ref[...] += jnp.sum(dpre, axis=0, keepdims=True)

    blocks = B * (_nbytes((C, 512), F32) * 2 + _nbytes((C, 1024), F32) + _nbytes((C, 1024), BF16)
                  + _nbytes((GLA_H, GLA_DV, GLA_DK), F32) + _nbytes((C, WC), BF16)) + 3 * _nbytes((128, 512), F32)
    state = _nbytes((B, GLA_H, GLA_DV, GLA_DK), F32)
    pa = projA.reshape(B, Lp, projA.shape[1])
    rev = lambda i: NC - 1 - i
    dc, dwg, dbg = pl.pallas_call(
        body,
        out_shape=(jax.ShapeDtypeStruct((B, Lp, WC), BF16), jax.ShapeDtypeStruct((128, GLA_KW), F32),
                   jax.ShapeDtypeStruct((1, GLA_KW), F32)),
        grid=(NC,),
        in_specs=[
            pl.BlockSpec((B, C, 512), lambda i: (0, rev(i), 10)),
            pl.BlockSpec((B, C, 512), lambda i: (0, rev(i), 11)),
            pl.BlockSpec((B, C, 1024), lambda i: (0, rev(i), 0)),
            pl.BlockSpec((B, C, 128), lambda i: (0, rev(i), 3)),
            pl.BlockSpec((B, 1, GLA_H, GLA_DV, GLA_DK), lambda i: (0, rev(i), 0, 0, 0)),
            pl.BlockSpec((B, C, 1024), lambda i: (0, rev(i), 0)),
            pl.BlockSpec((128, 512), lambda i: (0, 0)),
            pl.BlockSpec((1, 512), lambda i: (0, 0)),
        ],
        out_specs=(pl.BlockSpec((B, C, WC), lambda i: (0, rev(i), 0)),
                   pl.BlockSpec((128, GLA_KW), lambda i: (0, 0)),
                   pl.BlockSpec((1, GLA_KW), lambda i: (0, 0))),
        scratch_shapes=[pltpu.VMEM((B, GLA_H, GLA_DV, GLA_DK), F32)],
        compiler_params=_params(("arbitrary",), blocks, state),
        name="gla_bwd",
    )(pa, pa, pa, projB.reshape(B, Lp, projB.shape[1]), ssave, doa.reshape(B, Lp, GLA_VW), wg, bg)
    return dc.reshape(T, WC), dwg, dbg


def _attn_bwd_pre(dyb, mla_proj, projA, ob, B, Lp):
    T = B * Lp
    NQ = Lp // QB

    def body(dyb_ref, w_ref, z_ref, o_ref, do_ref, dz_ref, dcol_ref):
        j = pl.program_id(1)
        dy_all = _nt(dyb_ref[...], w_ref[...])
        for h in range(MLA_H):
            hs = slice(h * MLA_DV, (h + 1) * MLA_DV)
            dy = dy_all[:, hs]
            z = z_ref[:, hs].astype(F32)
            o = o_ref[:, hs].astype(F32)
            s = _sigmoid(z)
            do = dy * (z * s)
            do_ref[:, hs] = do.astype(BF16)
            dz_ref[:, hs] = (dy * o * (s * (1.0 + z * (1.0 - s)))).astype(BF16)
            dl = jnp.broadcast_to(jnp.sum(do * o, axis=-1, keepdims=True), (QB, LANES))
            dcol_ref[0, h, pl.ds(j, 1), :] = jnp.transpose(dl)[0:1, :]

    rows = lambda c: pl.BlockSpec((QB, D), lambda b, j: (b * NQ + j, c))
    return pl.pallas_call(
        body,
        out_shape=(jax.ShapeDtypeStruct((T, D), BF16), jax.ShapeDtypeStruct((T, D), BF16),
                   jax.ShapeDtypeStruct((B, MLA_H, NQ, QB), F32)),
        grid=(B, NQ),
        in_specs=[rows(0), pl.BlockSpec((D, D), lambda b, j: (0, 0)), rows(2), rows(0)],
        out_specs=(rows(0), rows(0), pl.BlockSpec((1, MLA_H, NQ, QB), lambda b, j: (b, 0, 0, 0))),
        compiler_params=_params(("parallel", "arbitrary"), 6 * _nbytes((QB, D), F32)),
        name="attn_bwd_pre",
    )(dyb, mla_proj, projA, ob)


ATTN_BWD_HEADS = 8


def _attn_bwd(q_att, k_att, v_att, do, lse_c, delta_c, B, Lp):
    T = B * Lp
    NQ = Lp // QB
    G = ATTN_BWD_HEADS
    NG = MLA_H // G
    HW = 2 * LANES
    scale = 1.0 / math.sqrt(MLA_QK)

    def body(q_ref, k_ref, v_ref, do_ref, lse_ref, dl_ref, dq_out, dk_out, dv_out, dq_ref, dk_ref, dv_ref):
        kj = pl.program_id(2)

        @pl.when(kj == 0)
        def _():
            dq_ref[...] = jnp.zeros_like(dq_ref)

        dk_ref[...] = jnp.zeros_like(dk_ref)
        dv_ref[...] = jnp.zeros_like(dv_ref)
        col = kj * QB + lax.broadcasted_iota(jnp.int32, (QB, QB), 0)
        rowi = lax.broadcasted_iota(jnp.int32, (QB, QB), 1)

        def step(qi, masked):
            off = pl.multiple_of(qi * QB, QB)
            ok = _attn_mask(qi * QB + rowi, col) if masked else None
            for h in range(G):
                ws = slice(h * HW, (h + 1) * HW)
                hs = slice(h * MLA_DV, (h + 1) * MLA_DV)
                qb = q_ref[pl.ds(off, QB), ws]
                dob = do_ref[pl.ds(off, QB), hs]
                kb = k_ref[:, ws]
                lse2 = lse_ref[0, h, pl.ds(qi, 1), :]
                delta = dl_ref[0, h, pl.ds(qi, 1), :]
                p_t = jnp.exp2(_nt(kb, qb) * (scale * LOG2E) - lse2)
                if masked:
                    p_t = jnp.where(ok, p_t, 0.0)
                dv_ref[:, hs] += _nn(p_t.astype(BF16), dob)
                ds_t = (p_t * (_nt(v_ref[:, hs], dob) - delta) * scale).astype(BF16)
                dk_ref[:, ws] += _nn(ds_t, qb)
                dq_ref[pl.ds(off, QB), ws] += _tn(ds_t, kb)

        def loop(masked):
            def it(qi, carry):
                step(qi, masked)
                return carry
            lax.fori_loop(kj + 1, NQ, it, 0)

        step(kj, True)
        pl.when(kj == 0)(lambda: loop(True))
        pl.when(kj > 0)(lambda: loop(False))
        dk_out[...] = dk_ref[...].astype(BF16)
        dv_out[...] = dv_ref[...].astype(BF16)

        @pl.when(kj == NQ - 1)
        def _():
            dq_out[...] = dq_ref[...].astype(BF16)

    blocks = (2 * _nbytes((Lp, G * HW), BF16) + _nbytes((Lp, G * MLA_DV), BF16) + 2 * _nbytes((QB, G * 384), BF16)
              + 2 * _nbytes((G, NQ, QB), F32))
    scratch = [pltpu.VMEM((Lp, G * HW), F32), pltpu.VMEM((QB, G * HW), F32), pltpu.VMEM((QB, G * MLA_DV), F32)]
    return pl.pallas_call(
        body,
        out_shape=(jax.ShapeDtypeStruct((T, MLA_H * HW), BF16), jax.ShapeDtypeStruct((T, MLA_H * HW), BF16),
                   jax.ShapeDtypeStruct((T, MLA_H * MLA_DV), BF16)),
        scratch_shapes=scratch,
        grid=(B, NG, NQ),
        in_specs=[
            pl.BlockSpec((Lp, G * HW), lambda b, g, j: (b, g), pipeline_mode=pl.Buffered(1)),
            pl.BlockSpec((QB, G * HW), lambda b, g, j: (b * NQ + j, g)),
            pl.BlockSpec((QB, G * MLA_DV), lambda b, g, j: (b * NQ + j, g)),
            pl.BlockSpec((Lp, G * MLA_DV), lambda b, g, j: (b, g), pipeline_mode=pl.Buffered(1)),
            pl.BlockSpec((1, G, NQ, QB), lambda b, g, j: (b, g, 0, 0)),
            pl.BlockSpec((1, G, NQ, QB), lambda b, g, j: (b, g, 0, 0)),
        ],
        out_specs=(pl.BlockSpec((Lp, G * HW), lambda b, g, j: (b, g), pipeline_mode=pl.Buffered(1)),
                   pl.BlockSpec((QB, G * HW), lambda b, g, j: (b * NQ + j, g)),
                   pl.BlockSpec((QB, G * MLA_DV), lambda b, g, j: (b * NQ + j, g))),
        compiler_params=_params(("parallel", "parallel", "arbitrary"), blocks,
                                _nbytes((Lp, G * HW), F32) + _nbytes((QB, G * 384), F32)),
        name="attn_bwd",
    )(q_att, k_att, v_att, do, lse_c, delta_c)


def _mla_bwd_post(dq, dk, dv, projB, cos_t, sin_t, gq, gkv, wuq2, wukv, B, Lp, tr):
    T = B * Lp
    nt = Lp // tr
    HW = 2 * LANES

    def body(dq_ref, dk_ref, dv_ref, pb_ref, cos_ref, sin_ref, gq_ref, gkv_ref, wuq_ref, wukv_ref,
             dqf_ref, dkvf_ref, de_ref, dgq_ref, dgkv_ref):
        first = (pl.program_id(0) == 0) & (pl.program_id(1) == 0)

        @pl.when(first)
        def _():
            dgq_ref[...] = jnp.zeros_like(dgq_ref)
            dgkv_ref[...] = jnp.zeros_like(dgkv_ref)

        cs = cos_ref[...]
        sn = sin_ref[...]
        rope_t = lambda t: t * cs + _swap_halves(t * sn)
        dkr = jnp.zeros((tr, LANES), F32)
        for h in range(MLA_H):
            dqf_ref[:, h * HW:h * HW + LANES] = dq_ref[:, h * HW:h * HW + LANES]
            dq_rope = dq_ref[:, h * HW + LANES:(h + 1) * HW].astype(F32)
            dqf_ref[:, h * HW + LANES:(h + 1) * HW] = rope_t(dq_rope).astype(BF16)
            dkvf_ref[:, h * HW:h * HW + LANES] = dk_ref[:, h * HW:h * HW + LANES]
            dkvf_ref[:, h * HW + LANES:(h + 1) * HW] = dv_ref[:, h * MLA_DV:(h + 1) * MLA_DV]
            dkr = dkr + dk_ref[:, h * HW + LANES:(h + 1) * HW].astype(F32)

        def norm_bwd(x, dn, g):
            r = lax.rsqrt(jnp.mean(x * x, axis=-1, keepdims=True) + EPS)
            xn = x * r
            t = dn * g
            return r * (t - xn * jnp.mean(t * xn, axis=-1, keepdims=True)), jnp.sum(dn * xn, axis=0, keepdims=True)

        cq = pb_ref[:, 0:Q_RANK].astype(F32)
        ckv = pb_ref[:, Q_RANK:Q_RANK + KV_RANK].astype(F32)
        dcq, dgq = norm_bwd(cq, _nt(dqf_ref[...], wuq_ref[...]), gq_ref[...])
        dckv, dgkv = norm_bwd(ckv, _nt(dkvf_ref[...], wukv_ref[...]), gkv_ref[...])
        dgq_ref[...] += dgq
        dgkv_ref[...] += dgkv
        de_ref[:, 0:Q_RANK] = dcq.astype(BF16)
        de_ref[:, Q_RANK:Q_RANK + KV_RANK] = dckv.astype(BF16)
        de_ref[:, 384:512] = rope_t(dkr).astype(BF16)

    rows = lambda w: pl.BlockSpec((tr, w), lambda b, j: (b * nt + j, 0))
    const = lambda s: pl.BlockSpec(s, lambda b, j: (0, 0))
    blocks = (2 * _nbytes((tr, 2048), F32) + _nbytes((tr, 1024), F32) + _nbytes((tr, 640), F32)
              + 2 * _nbytes((tr, 2048), BF16) + _nbytes((2048, 384), BF16) + 2 * _nbytes((tr, 2048), F32))
    return pl.pallas_call(
        body,
        out_shape=(jax.ShapeDtypeStruct((T, 2048), BF16), jax.ShapeDtypeStruct((T, 2048), BF16),
                   jax.ShapeDtypeStruct((T, 512), BF16), jax.ShapeDtypeStruct((1, Q_RANK), F32),
                   jax.ShapeDtypeStruct((1, KV_RANK), F32)),
        grid=(B, nt),
        in_specs=[rows(2048), rows(2048), rows(1024), rows(640),
                  pl.BlockSpec((tr, 128), lambda b, j: (j, 0)), pl.BlockSpec((tr, 128), lambda b, j: (j, 0)),
                  const((1, Q_RANK)), const((1, KV_RANK)), const((Q_RANK, 2048)), const((KV_RANK, 2048))],
        out_specs=(rows(2048), rows(2048), rows(512), const((1, Q_RANK)), const((1, KV_RANK))),
        compiler_params=_params(("arbitrary", "arbitrary"), blocks),
        name="mla_bwd_post",
    )(dq, dk, dv, projB, cos_t, sin_t, gq, gkv, wuq2, wukv)


def _in_proj_bwd(x, meta, dh1, dA, dBz, dC, dDz, dE, wA, wB, g, B, Lp):
    NQ = Lp // QB
    seq = x.shape[1]
    R = 2 if B % 2 == 0 else 1
    M = R * QB

    def body(x_ref, meta_ref, dh_ref, da_ref, db_ref, dc_ref, dd_ref, de_ref, wa_ref, wb_ref, g_ref,
             gx_ref, dmeta_ref, dg_ref):
        b = pl.program_id(0)
        j = pl.program_id(1)

        @pl.when((b == 0) & (j == 0))
        def _():
            dg_ref[...] = jnp.zeros_like(dg_ref)

        flat = lambda ref: ref[...].reshape(M, ref.shape[-1])
        da, dbz, dc, dd, de = flat(da_ref), flat(db_ref), flat(dc_ref), flat(dd_ref), flat(de_ref)
        du = _nt(da, wa_ref[:, 3072:5120])
        du = du + _nt(dbz, wa_ref[:, 1024:2048])
        du = du + _nt(dd, wa_ref[:, 2048:3072])
        du = du + _nt(dc[:, 0:1024], wa_ref[:, 0:1024])
        du = du + _nt(dc[:, 1024:2048], wa_ref[:, 5120:6144])
        du = du + _nt(dc[:, 2048:2176], wb_ref[:, 384:512])
        du = du + _nt(de[:, 0:384], wb_ref[:, 0:384])
        du = du + _nt(de[:, 384:512], wb_ref[:, 512:640])

        head = jnp.concatenate([jnp.zeros((FRONT, D), F32), meta_ref[...]], axis=0)
        x = jnp.concatenate([jnp.where(j > 0, x_ref[i], head) for i in range(R)], axis=0)
        r = lax.rsqrt(jnp.mean(x * x, axis=-1, keepdims=True) + EPS)
        xn = x * r
        t = du * g_ref[...]
        dh0 = flat(dh_ref).astype(F32) + r * (t - xn * jnp.mean(t * xn, axis=-1, keepdims=True))
        dg_ref[...] += jnp.sum(du * xn, axis=0, keepdims=True)
        dmeta = dh0[FRONT:HEAD_ROWS, :]
        for i in range(R):
            gx_ref[i] = dh0[i * QB:(i + 1) * QB, :]
            if i > 0:
                dmeta = dmeta + dh0[i * QB + FRONT:i * QB + HEAD_ROWS, :]

        @pl.when((j == 0) & (b == 0))
        def _():
            dmeta_ref[...] = dmeta

        @pl.when((j == 0) & (b > 0))
        def _():
            dmeta_ref[...] += dmeta

    rows = lambda w: pl.BlockSpec((R, QB, w), lambda b, j: (b, j, 0))
    x_rows = pl.BlockSpec((R, QB, D), lambda b, j: (b, jnp.maximum(j - 1, 0), 0))
    const = lambda s: pl.BlockSpec(s, lambda b, j: (0,) * len(s))
    resident = lambda s: pl.BlockSpec(s, lambda b, j: (0, 0), pipeline_mode=pl.Buffered(1))
    by_row = lambda a: a.reshape(B, Lp, a.shape[1])
    widths = [a.shape[1] for a in (dA, dBz, dC, dDz, dE)]
    blocks = sum(_nbytes((M, w), BF16) for w in widths) + 4 * _nbytes((M, D), F32)
    return pl.pallas_call(
        body,
        out_shape=(jax.ShapeDtypeStruct((B, seq, D), F32), jax.ShapeDtypeStruct((N_META, D), F32),
                   jax.ShapeDtypeStruct((1, D), F32)),
        grid=(B // R, NQ),
        in_specs=[x_rows, const((N_META, D)), rows(D)] + [rows(w) for w in widths]
        + [resident(wA.shape), resident(wB.shape), const((1, D))],
        out_specs=(x_rows, const((N_META, D)), const((1, D))),
        compiler_params=_params(("arbitrary", "arbitrary"), blocks, _nbytes(wA.shape, BF16) + _nbytes(wB.shape, BF16)),
        name="in_proj_bwd",
    )(x, meta, by_row(dh1), *[by_row(a) for a in (dA, dBz, dC, dDz, dE)], wA, wB, g)


_VMEM_WHOLE = pl.BlockSpec(memory_space=pltpu.VMEM)


def _params_whole(arrays):
    total = sum(_nbytes(a.shape, a.dtype) for a in arrays)
    return pltpu.CompilerParams(vmem_limit_bytes=int(min(total + 12 * 1024 * 1024, VMEM_CAP_V7X)))


def _wire_dtype(shape):
    return BF16 if shape[-2] * shape[-1] >= WIRE_BF16_MIN_ELEMS else F32


def _pair_add_big(gp, recv, c):
    _, half, cols = recv.shape
    th = _div_tile(half, 64, 16)
    out_dtype = _wire_dtype(recv.shape)

    steps = half // th

    def body(c_ref, a_ref, b_ref, o_ref):
        o_ref[...] = (a_ref[...] + b_ref[...]).astype(out_dtype)

    return pl.pallas_call(
        body,
        out_shape=jax.ShapeDtypeStruct(recv.shape, out_dtype),
        grid_spec=pltpu.PrefetchScalarGridSpec(
            num_scalar_prefetch=1,
            grid=(steps,),
            in_specs=[pl.BlockSpec((4, th, cols), lambda i, c_ref: (0, c_ref[0] * steps + i, 0)),
                      pl.BlockSpec((4, th, cols), lambda i, c_ref: (0, i, 0))],
            out_specs=pl.BlockSpec((4, th, cols), lambda i, c_ref: (0, i, 0)),
        ),
        compiler_params=_params(("parallel",), 3 * _nbytes((4, th, cols), F32)),
        name="grad_pair_add_big",
    )(c, gp, recv)


def _pair_add_small(gps, recvs):
    n = len(gps)

    def body(*refs):
        c = lax.axis_index("c")
        for t in range(n):
            g_ref, r_ref, o_ref = refs[t], refs[n + t], refs[2 * n + t]
            half = r_ref.shape[1]
            s = g_ref[:, pl.ds(pl.multiple_of(c * half, 8), half), :] + r_ref[...]
            o_ref[...] = s.astype(o_ref.dtype)

    return pl.pallas_call(
        body,
        out_shape=[jax.ShapeDtypeStruct(r.shape, _wire_dtype(r.shape)) for r in recvs],
        in_specs=[_VMEM_WHOLE] * (2 * n),
        out_specs=[_VMEM_WHOLE] * n,
        compiler_params=_params_whole(list(gps) + 2 * list(recvs)),
        name="grad_pair_add_small",
    )(*gps, *recvs)


def _chip_order_sum(landed_ref, own_ref, me):
    p = [jnp.where(me == k, own_ref[k], landed_ref[k]).astype(F32) for k in range(4)]
    return ((p[0] + p[1]) + p[2]) + p[3]


def _sum_chips_big(landed, own, pos):
    _, half, cols = landed.shape
    th = _div_tile(half, 64, 16)

    def body(pos_ref, l_ref, s_ref, o_ref):
        o_ref[0] = _chip_order_sum(l_ref, s_ref, pos_ref[1])

    spec = pl.BlockSpec((4, th, cols), lambda i, pos_ref: (0, i, 0))
    return pl.pallas_call(
        body,
        out_shape=jax.ShapeDtypeStruct((2, half, cols), F32),
        grid_spec=pltpu.PrefetchScalarGridSpec(
            num_scalar_prefetch=1,
            grid=(half // th,),
            in_specs=[spec, spec],
            out_specs=pl.BlockSpec((1, th, cols), lambda i, pos_ref: (pos_ref[0], i, 0)),
        ),
        compiler_params=_params(("parallel",), 3 * _nbytes((4, th, cols), F32)),
        name="grad_sum_chips_big",
    )(pos, landed, own)


def _sum_chips_small(landed, own):
    n = len(landed)

    def body(*refs):
        x, y, c = _mesh_pos()
        for t in range(n):
            refs[2 * n + t][c] = _chip_order_sum(refs[t], refs[n + t], 2 * x + y)

    return pl.pallas_call(
        body,
        out_shape=[jax.ShapeDtypeStruct((2,) + p.shape[1:], F32) for p in landed],
        in_specs=[_VMEM_WHOLE] * (2 * n),
        out_specs=[_VMEM_WHOLE] * n,
        compiler_params=_params_whole(list(landed) * 3),
        name="grad_sum_chips_small",
    )(*landed, *own)


def _adamw_update(w_ref, g_ref, m_ref, v_ref, d_ref, mo_ref, vo_ref):
    c1 = 1.0 - ADAM_B1 ** ADAM_STEP
    c2 = 1.0 - ADAM_B2 ** ADAM_STEP
    gv = g_ref[...]
    mn = ADAM_B1 * m_ref[...] + (1.0 - ADAM_B1) * gv
    vn = ADAM_B2 * v_ref[...] + (1.0 - ADAM_B2) * (gv * gv)
    mo_ref[...] = mn
    vo_ref[...] = vn
    d_ref[...] = -ADAM_LR * ((mn / c1) / (jnp.sqrt(vn / c2) + ADAM_EPS) + ADAM_WD * w_ref[...])


def _adamw_big(w, g, m, v):
    lead, (rows, cols) = w.shape[:-2], w.shape[-2:]
    assert all(n == 1 for n in lead)
    tr = _div_tile(rows, (1 << 19) // cols, 8)
    spec = pl.BlockSpec((1,) * len(lead) + (tr, cols), lambda i: (0,) * len(lead) + (i, 0))
    shp = jax.ShapeDtypeStruct(w.shape, F32)
    return pl.pallas_call(
        functools.partial(_adamw_update),
        out_shape=(shp, shp, shp),
        grid=(rows // tr,),
        in_specs=[spec] * 4,
        out_specs=(spec, spec, spec),
        compiler_params=_params(("parallel",), 7 * _nbytes((tr, cols), F32)),
        name="adamw_big",
    )(w, g, m, v)


def _adamw_small(ws, gs, ms, vs):
    n = len(ws)

    def body(*refs):
        for t in range(n):
            _adamw_update(refs[t], refs[n + t], refs[2 * n + t], refs[3 * n + t],
                          refs[4 * n + t], refs[5 * n + t], refs[6 * n + t])

    shapes = [jax.ShapeDtypeStruct(w.shape, F32) for w in ws]
    return pl.pallas_call(
        body,
        out_shape=shapes * 3,
        in_specs=[_VMEM_WHOLE] * (4 * n),
        out_specs=[_VMEM_WHOLE] * (3 * n),
        compiler_params=_params_whole(list(ws) * 7),
        name="adamw_small",
    )(*ws, *gs, *ms, *vs)


def _mesh_pos():
    return lax.axis_index("x"), lax.axis_index("y"), lax.axis_index("c")


def _other_chips(x, y):
    return [(1 - x, y), (x, 1 - y), (1 - x, 1 - y)]


_ANY = pl.BlockSpec(memory_space=pl.ANY)


PAIR_SPLIT_MIN_ROWS = 64


def _weight_gather(shards):
    n = len(shards)
    split = [s.shape[0] >= PAIR_SPLIT_MIN_ROWS for s in shards]

    def body(*refs):
        w_refs, o_refs = refs[:n], refs[n:2 * n]
        send_sems, recv_sems = refs[2 * n:]
        x, y, c = _mesh_pos()
        me = 2 * x + y
        chips = _other_chips(x, y)

        def rows_of(t, core):
            rows = shards[t].shape[0]
            if not split[t]:
                return pl.ds(0, rows)
            return pl.ds(pl.multiple_of(core * (rows // 2), 16), rows // 2)

        def landed(t, k, slot, rows, to):
            ref = o_refs[t].at[slot, rows]
            return pltpu.make_async_remote_copy(src_ref=ref, dst_ref=ref, send_sem=send_sems.at[6 * t + k],
                                                recv_sem=recv_sems.at[6 * t + k], device_id=to, device_id_type=MESH)

        sends = []
        for t in range(n):
            mine = rows_of(t, c)
            for k, (px, py) in enumerate(chips):
                cp = pltpu.make_async_remote_copy(src_ref=w_refs[t].at[mine], dst_ref=o_refs[t].at[me, mine],
                                                  send_sem=send_sems.at[6 * t + k], recv_sem=recv_sems.at[6 * t + k],
                                                  device_id=(px, py, c), device_id_type=MESH)
                cp.start()
                sends.append(cp)
        for t in range(n):
            mine = rows_of(t, c)
            for k, (px, py) in enumerate(chips):
                landed(t, k, 2 * px + py, mine, (x, y, c)).wait_recv()
                if split[t]:
                    cp = landed(t, 3 + k, 2 * px + py, mine, (x, y, 1 - c))
                    cp.start()
                    sends.append(cp)
        for t in range(n):
            if split[t]:
                for k, (px, py) in enumerate(chips):
                    landed(t, 3 + k, 2 * px + py, rows_of(t, 1 - c), (x, y, c)).wait_recv()
        for cp in sends:
            cp.wait_send()

    return pl.pallas_call(
        body,
        out_shape=[jax.ShapeDtypeStruct((4,) + s.shape, s.dtype) for s in shards],
        in_specs=[_ANY] * n,
        out_specs=[_ANY] * n,
        scratch_shapes=[pltpu.SemaphoreType.DMA((6 * n,)), pltpu.SemaphoreType.DMA((6 * n,))],
        name="weight_gather",
    )(*shards)


def _pair_swap(gps):
    n = len(gps)

    def body(*refs):
        g_refs, o_refs = refs[:n], refs[n:2 * n]
        send_sems, recv_sems = refs[2 * n:]
        x, y, c = _mesh_pos()
        copies = []
        for t in range(n):
            half = gps[t].shape[1] // 2
            theirs = pl.ds(pl.multiple_of((1 - c) * half, 8), half)
            cp = pltpu.make_async_remote_copy(src_ref=g_refs[t].at[:, theirs], dst_ref=o_refs[t],
                                              send_sem=send_sems.at[t], recv_sem=recv_sems.at[t],
                                              device_id=(x, y, 1 - c), device_id_type=MESH)
            cp.start()
            copies.append(cp)
        for cp in copies:
            cp.wait_send()
            cp.wait_recv()

    return pl.pallas_call(
        body,
        out_shape=[jax.ShapeDtypeStruct((4, g.shape[1] // 2, g.shape[2]), g.dtype) for g in gps],
        in_specs=[_ANY] * n,
        out_specs=[_ANY] * n,
        scratch_shapes=[pltpu.SemaphoreType.DMA((n,)), pltpu.SemaphoreType.DMA((n,))],
        name="grad_pair_swap",
    )(*gps)


_HBM = pl.BlockSpec(memory_space=pltpu.HBM)
_SEM = pl.BlockSpec(memory_space=pltpu.SEMAPHORE)


def _in_hbm(a):
    return pltpu.with_memory_space_constraint(a, pltpu.HBM)


def _chip_scatter_start(parts):
    n = len(parts)

    def body(*refs):
        s_refs, l_refs = refs[:n], refs[n:2 * n]
        send_sems, recv_sems = refs[2 * n], refs[2 * n + 1]
        token = refs[-1]
        x, y, c = _mesh_pos()
        me = 2 * x + y
        for t in range(n):
            for k, (px, py) in enumerate(_other_chips(x, y)):
                pltpu.make_async_remote_copy(src_ref=s_refs[t].at[2 * px + py], dst_ref=l_refs[t].at[me],
                                             send_sem=send_sems.at[3 * t + k], recv_sem=recv_sems.at[3 * t + k],
                                             device_id=(px, py, c), device_id_type=MESH).start()
        token[...] = jnp.zeros_like(token)

    hbm = [pltpu.HBM(p.shape, p.dtype) for p in parts]
    outs = pl.pallas_call(
        body,
        name="grad_scatter_start",
        out_shape=(pltpu.SemaphoreType.DMA((3 * n,)), pltpu.SemaphoreType.DMA((3 * n,)), *hbm, *hbm,
                   jax.ShapeDtypeStruct((8, LANES), F32)),
        in_specs=[_HBM] * (2 * n),
        out_specs=(_SEM, _SEM, *([_HBM] * (2 * n)), pl.BlockSpec(memory_space=pltpu.VMEM)),
        input_output_aliases={i: 2 + i for i in range(2 * n)},
        compiler_params=pltpu.CompilerParams(has_side_effects=pltpu.SideEffectType.DATAFLOW_SIDE_EFFECTING),
    )(*[_in_hbm(p) for p in parts], *[_in_hbm(lax.empty(p.shape, p.dtype)) for p in parts])
    return outs[0], outs[1], list(outs[2:2 + n]), list(outs[2 + n:2 + 2 * n]), outs[-1]


def _chip_scatter_wait(send_sems, recv_sems, parts, lands, after):
    n = len(parts)

    def body(*refs):
        s_refs, l_refs = refs[:n], refs[n:2 * n]
        send_sems, recv_sems = refs[2 * n], refs[2 * n + 1]
        x, y, c = _mesh_pos()
        me = 2 * x + y
        for t in range(n):
            for k, (px, py) in enumerate(_other_chips(x, y)):
                cp = pltpu.make_async_remote_copy(src_ref=s_refs[t].at[2 * px + py], dst_ref=l_refs[t].at[2 * px + py],
                                                  send_sem=send_sems.at[3 * t + k], recv_sem=recv_sems.at[3 * t + k],
                                                  device_id=(x, y, c), device_id_type=MESH)
                cp.wait_send()
                cp.wait_recv()

    hbm = [pltpu.HBM(p.shape, p.dtype) for p in parts]
    outs = pl.pallas_call(
        body,
        name="grad_scatter_wait",
        out_shape=(*hbm, *hbm),
        in_specs=[_HBM] * (2 * n) + [_SEM, _SEM, _ANY],
        out_specs=[_HBM] * (2 * n),
        input_output_aliases={i: i for i in range(2 * n)},
        compiler_params=pltpu.CompilerParams(has_side_effects=pltpu.SideEffectType.DATAFLOW_SIDE_EFFECTING),
    )(*parts, *lands, send_sems, recv_sems, after)
    return list(outs[:n]), list(outs[n:])


def _late_gather_start(shards):
    n = len(shards)

    def body(*refs):
        w_refs, l_refs = refs[:n], refs[n:2 * n]
        send_sems, recv_sems = refs[2 * n], refs[2 * n + 1]
        token = refs[-1]
        x, y, c = _mesh_pos()
        me = 2 * x + y
        for t in range(n):
            for k, (px, py) in enumerate(_other_chips(x, y)):
                pltpu.make_async_remote_copy(src_ref=w_refs[t], dst_ref=l_refs[t].at[me],
                                             send_sem=send_sems.at[3 * t + k], recv_sem=recv_sems.at[3 * t + k],
                                             device_id=(px, py, c), device_id_type=MESH).start()
        token[...] = jnp.zeros_like(token)

    src = [pltpu.HBM(s.shape, s.dtype) for s in shards]
    land = [pltpu.HBM((4,) + s.shape, s.dtype) for s in shards]
    outs = pl.pallas_call(
        body,
        name="late_gather_start",
        out_shape=(pltpu.SemaphoreType.DMA((3 * n,)), pltpu.SemaphoreType.DMA((3 * n,)), *src, *land,
                   jax.ShapeDtypeStruct((8, LANES), F32)),
        in_specs=[_HBM] * (2 * n),
        out_specs=(_SEM, _SEM, *([_HBM] * (2 * n)), pl.BlockSpec(memory_space=pltpu.VMEM)),
        input_output_aliases={i: 2 + i for i in range(2 * n)},
        compiler_params=pltpu.CompilerParams(has_side_effects=pltpu.SideEffectType.DATAFLOW_SIDE_EFFECTING),
    )(*[_in_hbm(s) for s in shards], *[_in_hbm(lax.empty((4,) + s.shape, s.dtype)) for s in shards])
    return outs[0], outs[1], list(outs[2:2 + n]), list(outs[2 + n:2 + 2 * n]), outs[-1]


def _late_gather_wait(send_sems, recv_sems, shards, lands, after):
    n = len(shards)

    def body(*refs):
        w_refs, l_refs = refs[:n], refs[n:2 * n]
        send_sems, recv_sems = refs[2 * n], refs[2 * n + 1]
        x, y, c = _mesh_pos()
        for t in range(n):
            for k, (px, py) in enumerate(_other_chips(x, y)):
                cp = pltpu.make_async_remote_copy(src_ref=w_refs[t], dst_ref=l_refs[t].at[2 * px + py],
                                                  send_sem=send_sems.at[3 * t + k], recv_sem=recv_sems.at[3 * t + k],
                                                  device_id=(x, y, c), device_id_type=MESH)
                cp.wait_send()
                cp.wait_recv()

    src = [pltpu.HBM(s.shape, s.dtype) for s in shards]
    land = [pltpu.HBM(l.shape, l.dtype) for l in lands]
    outs = pl.pallas_call(
        body,
        name="late_gather_wait",
        out_shape=(*src, *land),
        in_specs=[_HBM] * (2 * n) + [_SEM, _SEM, _ANY],
        out_specs=[_HBM] * (2 * n),
        input_output_aliases={i: i for i in range(2 * n)},
        compiler_params=pltpu.CompilerParams(has_side_effects=pltpu.SideEffectType.DATAFLOW_SIDE_EFFECTING),
    )(*shards, *lands, send_sems, recv_sems, after)
    return list(outs[n:])


def _all_to_all_small(parts):
    n = len(parts)

    def body(*refs):
        p_refs, o_refs = refs[:n], refs[n:2 * n]
        send_sems, recv_sems = refs[2 * n:]
        x, y, c = _mesh_pos()
        me = 4 * x + 2 * y + c
        sends = []
        for t in range(n):
            for k in range(1, 8):
                px, py, pc = x ^ (k >> 2), y ^ ((k >> 1) & 1), c ^ (k & 1)
                cp = pltpu.make_async_remote_copy(src_ref=p_refs[t], dst_ref=o_refs[t].at[me],
                                                  send_sem=send_sems.at[7 * t + k - 1], recv_sem=recv_sems.at[7 * t + k - 1],
                                                  device_id=(px, py, pc), device_id_type=MESH)
                cp.start()
                sends.append(cp)
        for t in range(n):
            for k in range(1, 8):
                peer = 4 * (x ^ (k >> 2)) + 2 * (y ^ ((k >> 1) & 1)) + (c ^ (k & 1))
                pltpu.make_async_remote_copy(src_ref=p_refs[t], dst_ref=o_refs[t].at[peer],
                                             send_sem=send_sems.at[7 * t + k - 1], recv_sem=recv_sems.at[7 * t + k - 1],
                                             device_id=(x, y, c), device_id_type=MESH).wait_recv()
        for cp in sends:
            cp.wait_send()

    return pl.pallas_call(
        body,
        out_shape=[jax.ShapeDtypeStruct((8,) + p.shape, p.dtype) for p in parts],
        in_specs=[_ANY] * n,
        out_specs=[_ANY] * n,
        scratch_shapes=[pltpu.SemaphoreType.DMA((7 * n,)), pltpu.SemaphoreType.DMA((7 * n,))],
        name="grad_small_all_to_all",
    )(*parts)


def _sum_devices_small(landed, own):
    n = len(landed)

    def body(*refs):
        x, y, c = _mesh_pos()
        me = 4 * x + 2 * y + c
        for t in range(n):
            acc = jnp.where(me == 0, refs[n + t][...], refs[t][0])
            for d in range(1, 8):
                acc = acc + jnp.where(me == d, refs[n + t][...], refs[t][d])
            refs[2 * n + t][...] = acc

    return pl.pallas_call(
        body,
        out_shape=[jax.ShapeDtypeStruct(p.shape, F32) for p in own],
        in_specs=[_VMEM_WHOLE] * (2 * n),
        out_specs=[_VMEM_WHOLE] * n,
        compiler_params=_params_whole(list(landed) + 2 * list(own)),
        name="grad_sum_devices_small",
    )(*landed, *own)


def _pair_join(fs):
    n = len(fs)

    def body(*refs):
        f_refs, o_refs = refs[:n], refs[n:2 * n]
        send_sems, recv_sems = refs[2 * n:]
        x, y, c = _mesh_pos()
        sends = []
        for t in range(n):
            cp = pltpu.make_async_remote_copy(src_ref=f_refs[t].at[c], dst_ref=o_refs[t].at[c], send_sem=send_sems.at[t],
                                              recv_sem=recv_sems.at[t], device_id=(x, y, 1 - c), device_id_type=MESH)
            cp.start()
            sends.append(cp)
        for t in range(n):
            pltpu.make_async_remote_copy(src_ref=f_refs[t].at[c], dst_ref=o_refs[t].at[1 - c], send_sem=send_sems.at[t],
                                         recv_sem=recv_sems.at[t], device_id=(x, y, c), device_id_type=MESH).wait_recv()
        for cp in sends:
            cp.wait_send()

    return pl.pallas_call(
        body,
        out_shape=[jax.ShapeDtypeStruct(f.shape, f.dtype) for f in fs],
        in_specs=[_ANY] * n,
        out_specs=[_ANY] * n,
        input_output_aliases={t: t for t in range(n)},
        scratch_shapes=[pltpu.SemaphoreType.DMA((n,)), pltpu.SemaphoreType.DMA((n,))],
        name="grad_pair_join",
    )(*fs)


def _rope_tables(Lp):
    inv = 1.0 / (ROPE_BASE ** (jnp.arange(0, ROPE, 2, dtype=F32) / ROPE))
    ang = (jnp.arange(Lp, dtype=F32) - FRONT)[:, None] * inv[None, :]
    cs, sn = jnp.cos(ang), jnp.sin(ang)
    return jnp.tile(cs, (1, 4)), jnp.concatenate([-sn, sn, -sn, sn], axis=1)


def _local_step(x, loss_target, meta, norm_g, w_in, gate_w, gate_b, gla_norm_g, gla_proj, q_norm_g, w_uq,
                kv_norm_g, w_ukv, mla_proj, w_out, final_norm_g, early_grads_hook=None, late_weights_hook=None):
    B, seq, _ = x.shape
    Lp = HEAD_ROWS + seq
    T = B * Lp
    tr = _div_tile(Lp, 544, 16)
    tkw = _div_tile(T, Lp, QB)
    tm_sq = _div_tile(T, 1024, QB)

    cuts = np.cumsum((0,) + SPLITS)
    shard_w = IN_WIDTH // 4

    def w_cols(i, width=None):
        parts = []
        for j in range(4):
            a, b = max(cuts[i], j * shard_w), min(cuts[i + 1], (j + 1) * shard_w)
            if a < b:
                parts.append(w_in[j][:, a - j * shard_w:b - j * shard_w])
        if width is not None:
            parts.append(jnp.zeros((D, width - (cuts[i + 1] - cuts[i])), w_in.dtype))
        return parts

    i_q, i_k, i_v, i_lr, i_z, i_cq, i_ckv, i_kr, i_mz, i_gg, i_gm = range(11)
    wA = jnp.concatenate(sum([w_cols(i) for i in (i_v, i_z, i_mz, i_gg, i_gm, i_q, i_k)], []), axis=1)
    wB = jnp.concatenate(w_cols(i_cq) + w_cols(i_ckv) + w_cols(i_lr, 128) + w_cols(i_kr, 128), axis=1)
    gn4 = jnp.tile(gla_norm_g, (1, GLA_H))
    cos_t, sin_t = _rope_tables(Lp)

    u = _rms_in(x, meta, norm_g, B, Lp)
    projA = _mm(u, wA, name="in_proj_a", out_dtype=BF16, tm=tkw, tn=1024, tk=D)
    projB = _mm(u, wB, name="in_proj_b", out_dtype=BF16, tm=tkw, tn=640, tk=D)
    if late_weights_hook is not None:
        gate_w, gla_proj, w_uq, w_ukv, mla_proj, w_out = late_weights_hook(projA)
    wg = jnp.pad(gate_w, ((0, 128 - GLA_RANK), (0, 0)))
    wuq2 = jnp.pad(w_uq.reshape(Q_RANK, MLA_H, MLA_QK), ((0, 0), (0, 0), (0, 256 - MLA_QK))).reshape(Q_RANK, 2048)
    oa, ya_in, ssave = _gla_fwd(projA, projB, wg, gate_b, gn4, B, Lp)
    ya = _mm(ya_in, gla_proj, name="gla_proj", out_dtype=BF16, tm=tm_sq, tn=D, tk=D)
    q_att, k_att, v_att, cqn, ckvn = _mla_prep(projB, cos_t, sin_t, q_norm_g, kv_norm_g, wuq2, w_ukv, B, Lp, tr)
    ob, yb_in, lse_c = _attn_fwd(q_att, k_att, v_att, projA, B, Lp)
    yb = _mm(yb_in, mla_proj, name="mla_proj", out_dtype=BF16, tm=tm_sq, tn=D, tk=D)
    dh1_b, merged, loss, d_gf = _out_proj_loss(x, meta, projA, ya, yb, w_out, final_norm_g.reshape(1, D),
                                                loss_target, B, Lp)

    g_w_out = _mm(merged, dh1_b, name="dw_out", trans_a=True, tm=D, tn=D, tk=tkw)
    dya, dyb, dA = _merge_bwd(dh1_b, w_out, projA, ya, yb, tr)
    g_gla_proj = _mm(ya_in, dya, name="dw_gla_proj", trans_a=True, tm=D, tn=D, tk=tkw)
    g_mla_proj = _mm(yb_in, dyb, name="dw_mla_proj", trans_a=True, tm=D, tn=D, tk=tkw)
    doa, dBz, d_gn = _gla_out_bwd(dya, gla_proj, oa, projA, gn4, tr)
    dC, g_wg, d_bg = _gla_bwd(projA, projB, ssave, doa, wg, gate_b, B, Lp)
    do, dDz, delta_c = _attn_bwd_pre(dyb, mla_proj, projA, ob, B, Lp)
    dq, dk, dv = _attn_bwd(q_att, k_att, v_att, do, lse_c, delta_c, B, Lp)
    dqf, dkvf, dE, d_gq, d_gkv = _mla_bwd_post(dq, dk, dv, projB, cos_t, sin_t, q_norm_g, kv_norm_g,
                                                wuq2, w_ukv, B, Lp, tr)
    g_wuq2 = _mm(cqn, dqf, name="dw_uq", trans_a=True, tm=Q_RANK, tn=2048, tk=tkw)
    g_wukv = _mm(ckvn, dkvf, name="dw_ukv", trans_a=True, tm=KV_RANK, tn=2048, tk=tkw)
    dparts = [dA, dBz, dC, dDz, dE]
    g_in = [_mm(u, dp, name="dw_in_%d" % i, trans_a=True, tm=D, tn=_div_tile(dp.shape[1], 1024, 256), tk=tkw)
            for i, dp in enumerate(dparts)]

    gA, gBz, gC, gDz, gE = g_in
    src = [(gC, 1024), (gC, 1536), (gC, 0), (gC, 2048), (gBz, 0), (gE, 0), (gE, Q_RANK), (gE, 384), (gDz, 0),
           (gA, 0), (gA, D)]
    owners = []
    for j in range(4):
        parts = []
        for i, (arr, off) in enumerate(src):
            a, b = max(cuts[i], j * shard_w), min(cuts[i + 1], (j + 1) * shard_w)
            if a < b:
                parts.append(arr[:, off + a - cuts[i]:off + b - cuts[i]])
        owners.append(jnp.concatenate(parts, axis=1))
    g_w_in = jnp.stack(owners)
    g_wuq = g_wuq2.reshape(Q_RANK, MLA_H, 256)[:, :, :MLA_QK].reshape(Q_RANK, MLA_H * MLA_QK)
    grads = dict(w_in=g_w_in, gla_gate_w=g_wg[:GLA_RANK], gla_proj=g_gla_proj, mla_w_uq=g_wuq, mla_w_ukv=g_wukv,
                 mla_proj=g_mla_proj, w_out=g_w_out, gla_gate_b=d_bg,
                 gla_norm_g=d_gn, mla_q_norm_g=d_gq, mla_kv_norm_g=d_gkv, final_norm_g=d_gf)
    token = None if early_grads_hook is None else early_grads_hook(grads)
    ng = norm_g if token is None else norm_g + token[0:1, 0:1]
    grad_x, d_meta, d_ng = _in_proj_bwd(x, meta, dh1_b, dA, dBz, dC, dDz, dE, wA, wB, ng, B, Lp)
    grads.update(meta_tokens=d_meta, norm_g=d_ng)
    return loss[0, 0], grad_x, grads


_MATS = ("w_in", "gla_gate_w", "gla_proj", "mla_w_uq", "mla_w_ukv", "mla_proj", "w_out")
_ROW_SHARDED = ("gla_proj", "mla_proj", "w_out")
_ORDER = ("meta_tokens", "norm_g", "w_in", "gla_gate_w", "gla_gate_b", "gla_norm_g", "gla_proj", "mla_q_norm_g",
          "mla_w_uq", "mla_kv_norm_g", "mla_w_ukv", "mla_proj", "w_out", "final_norm_g")
WIRE_BF16_MIN_ELEMS = 128 * 128
SMALL_PACK_ROWS = 16


def _pack_small(d, scalar=None):
    rows = [jnp.pad(d[n].reshape(1, size), ((0, 0), (0, D - size))) for n, size in SMALL]
    if scalar is not None:
        rows.append(jnp.pad(scalar.reshape(1, 1), ((0, 0), (0, D - 1))))
    return jnp.pad(jnp.concatenate(rows, axis=0), ((0, SMALL_PACK_ROWS - len(rows)), (0, 0)))


def _unpack_small(packed):
    return {n: packed[i, :size] for i, (n, size) in enumerate(SMALL)}


def kernel(x, meta_tokens, norm_g, w_in, gla_gate_w, gla_gate_b, gla_norm_g, gla_proj, mla_q_norm_g, mla_w_uq, mla_kv_norm_g, mla_w_ukv, mla_proj, w_out, final_norm_g, loss_target, m_meta_tokens, m_norm_g, m_w_in, m_gla_gate_w, m_gla_gate_b, m_gla_norm_g, m_gla_proj, m_mla_q_norm_g, m_mla_w_uq, m_mla_kv_norm_g, m_mla_w_ukv, m_mla_proj, m_w_out, m_final_norm_g, v_meta_tokens, v_norm_g, v_w_in, v_gla_gate_w, v_gla_gate_b, v_gla_norm_g, v_gla_proj, v_mla_q_norm_g, v_mla_w_uq, v_mla_kv_norm_g, v_mla_w_ukv, v_mla_proj, v_w_out, v_final_norm_g):
    w = dict(meta_tokens=meta_tokens, norm_g=norm_g, w_in=w_in[0], gla_gate_w=gla_gate_w[0], gla_gate_b=gla_gate_b,
             gla_norm_g=gla_norm_g, gla_proj=gla_proj[0], mla_q_norm_g=mla_q_norm_g, mla_w_uq=mla_w_uq[0],
             mla_kv_norm_g=mla_kv_norm_g, mla_w_ukv=mla_w_ukv[0], mla_proj=mla_proj[0], w_out=w_out[0],
             final_norm_g=final_norm_g)
    mom = dict(meta_tokens=m_meta_tokens, norm_g=m_norm_g, w_in=m_w_in[0], gla_gate_w=m_gla_gate_w[0],
               gla_gate_b=m_gla_gate_b, gla_norm_g=m_gla_norm_g, gla_proj=m_gla_proj[0], mla_q_norm_g=m_mla_q_norm_g,
               mla_w_uq=m_mla_w_uq[0], mla_kv_norm_g=m_mla_kv_norm_g, mla_w_ukv=m_mla_w_ukv[0], mla_proj=m_mla_proj[0],
               w_out=m_w_out[0], final_norm_g=m_final_norm_g)
    var = dict(meta_tokens=v_meta_tokens, norm_g=v_norm_g, w_in=v_w_in[0], gla_gate_w=v_gla_gate_w[0],
               gla_gate_b=v_gla_gate_b, gla_norm_g=v_gla_norm_g, gla_proj=v_gla_proj[0], mla_q_norm_g=v_mla_q_norm_g,
               mla_w_uq=v_mla_w_uq[0], mla_kv_norm_g=v_mla_kv_norm_g, mla_w_ukv=v_mla_w_ukv[0], mla_proj=v_mla_proj[0],
               w_out=v_w_out[0], final_norm_g=v_final_norm_g)
    out_shapes = {n: a.shape for n, a in zip(_ORDER, (meta_tokens, norm_g, w_in, gla_gate_w, gla_gate_b, gla_norm_g,
                                                     gla_proj, mla_q_norm_g, mla_w_uq, mla_kv_norm_g, mla_w_ukv,
                                                     mla_proj, w_out, final_norm_g))}

    me = (2 * lax.axis_index("x") + lax.axis_index("y")).astype(jnp.int32)
    is_mine = lax.broadcasted_iota(jnp.int32, (4, 1, 1), 0) == me
    with_own = lambda gth, own: jnp.where(is_mine, own[None], gth)
    first = [w["w_in"].astype(BF16), meta_tokens]
    w_in_owner, meta_owner = [with_own(gth, own) for gth, own in zip(_weight_gather(first), first)]
    meta_full = meta_owner.transpose(1, 0, 2).reshape(N_META, D)
    late_names = _MATS[1:]
    late = [w[n].astype(BF16) for n in late_names]
    gather_sems = _late_gather_start(late)

    def late_weights(after):
        lands = _late_gather_wait(gather_sems[0], gather_sems[1], gather_sems[2], gather_sems[3], after)
        full = []
        for name, land, own in zip(late_names, lands, late):
            gth = with_own(land, own)
            if name in _ROW_SHARDED:
                full.append(gth.reshape(4 * gth.shape[1], gth.shape[2]))
            else:
                full.append(gth.transpose(1, 0, 2).reshape(gth.shape[1], 4 * gth.shape[2]))
        return full

    def by_owner(name, arr):
        if name == "w_in":
            return arr
        if name in _ROW_SHARDED:
            return arr.reshape(4, arr.shape[0] // 4, arr.shape[1])
        return arr.reshape(arr.shape[0], 4, arr.shape[1] // 4).transpose(1, 0, 2)

    c_idx = lax.axis_index("c").astype(jnp.int32).reshape(1)
    pos = jnp.stack([c_idx[0], me])
    in_flight = {}

    def start_matrix_reduce(early):
        gps = [by_owner(n, early[n]) for n in _MATS]
        recvs = _pair_swap(gps)
        s1 = [_pair_add_big(gps[0], recvs[0], c_idx)] + list(_pair_add_small(gps[1:], recvs[1:]))
        send_sems, recv_sems, parts, lands, token = _chip_scatter_start(s1)
        in_flight.update(send_sems=send_sems, recv_sems=recv_sems, parts=parts, lands=lands)
        return token

    norm_g_after_start = norm_g + gather_sems[4][0:1, 0:1]
    loss_local, grad_x, g = _local_step(
        x, loss_target, meta_full, norm_g_after_start, w_in_owner, None, gla_gate_b, gla_norm_g, None,
        mla_q_norm_g, None, mla_kv_norm_g, None, None, None, final_norm_g,
        early_grads_hook=start_matrix_reduce, late_weights_hook=late_weights)

    s1, landed = _chip_scatter_wait(in_flight["send_sems"], in_flight["recv_sems"], in_flight["parts"],
                                    in_flight["lands"], after=g["norm_g"])
    halves = [_sum_chips_big(landed[0], s1[0], pos)] + list(_sum_chips_small(landed[1:], s1[1:]))
    g_mats = [j.reshape(out_shapes[n]) for j, n in zip(_pair_join(halves), _MATS)]

    late = [g["meta_tokens"], _pack_small(g, scalar=loss_local)]
    meta_sum, small_sum = _sum_devices_small(_all_to_all_small(late), late)
    loss = small_sum[len(SMALL), 0]
    g_meta = lax.dynamic_slice(meta_sum, (0, me * (D // 4)), (N_META, D // 4))
    names = _MATS + ("meta_tokens",)
    g_red = g_mats + [g_meta, small_sum]

    tens = lambda d: [d[n].reshape(out_shapes[n]) for n in names] + [_pack_small(d)]
    w_t, m_t, v_t = tens(w), tens(mom), tens(var)
    big = _adamw_big(w_t[0], g_red[0], m_t[0], v_t[0])
    rest = _adamw_small(w_t[1:], g_red[1:], m_t[1:], v_t[1:])
    k = len(names)
    results = {"grad": g_red}
    for i, kind in enumerate(("delta", "new_m", "new_v")):
        results[kind] = [big[i]] + list(rest[i * k:(i + 1) * k])

    outs = []
    for kind in ("grad", "delta", "new_m", "new_v"):
        vals = dict(zip(names, results[kind][:-1]))
        vals.update(_unpack_small(results[kind][-1]))
        outs += [vals[n].reshape(out_shapes[n]) for n in _ORDER]
    return (loss, grad_x, *outs)
```

```python
import functools
import math

import jax
import jax.numpy as jnp
import numpy as np
from jax import lax
from jax.experimental import pallas as pl
from jax.experimental.pallas import tpu as pltpu

F32 = jnp.float32
BF16 = jnp.bfloat16

D = 1024
N_META = 16
QB = 256
FRONT = QB - N_META
HEAD_ROWS = FRONT + N_META
assert FRONT % 64 == 48
EPS = 1e-6

GLA_H, GLA_DK, GLA_DV, GLA_RANK, GLA_C = 4, 128, 256, 16, 64
GLA_NORMALIZER = 16.0
GLA_KW, GLA_VW = GLA_H * GLA_DK, GLA_H * GLA_DV
MLA_H, NOPE, ROPE, MLA_DV, Q_RANK, KV_RANK = 8, 128, 64, 128, 256, 128
MLA_QK = NOPE + ROPE
ROPE_BASE = 10000.0
SPLITS = (GLA_KW, GLA_KW, GLA_VW, GLA_RANK, GLA_VW, Q_RANK, KV_RANK, ROPE, MLA_H * MLA_DV, D, D)
IN_WIDTH = sum(SPLITS)

ADAM_LR, ADAM_B1, ADAM_B2, ADAM_EPS, ADAM_WD, ADAM_STEP = 0.001, 0.9, 0.999, 1e-08, 0.01, 10

LANES = 128
VMEM_CAP_V7X = 56 * 1024 * 1024
MESH = pl.DeviceIdType.MESH
NEG = -1e30
LOG2E = math.log2(math.e)

SMALL = (("norm_g", D), ("gla_gate_b", GLA_KW), ("gla_norm_g", GLA_DV), ("mla_q_norm_g", Q_RANK),
         ("mla_kv_norm_g", KV_RANK), ("final_norm_g", D))


def _div_tile(n, target, mult):
    best = None
    for d in range(mult, min(n, target) + 1, mult):
        if n % d == 0:
            best = d
    assert best is not None, (n, target, mult)
    return best


def _params(sem, block_bytes, scratch_bytes=0):
    est = 2 * block_bytes + scratch_bytes + 12 * 1024 * 1024
    return pltpu.CompilerParams(dimension_semantics=sem, vmem_limit_bytes=int(min(max(est, 24 * 1024 * 1024), VMEM_CAP_V7X)))


def _nbytes(shape, dtype):
    return int(np.prod(shape)) * jnp.dtype(dtype).itemsize


def _sigmoid(x):
    return 1.0 / (1.0 + jnp.exp(-x))


def _nt(a, b):
    return lax.dot_general(a, b, (((1,), (1,)), ((), ())), preferred_element_type=F32)


def _tn(a, b):
    return lax.dot_general(a, b, (((0,), (0,)), ((), ())), preferred_element_type=F32)


def _nn(a, b):
    return jnp.dot(a, b, preferred_element_type=F32)


def _split2(x):
    a = x.astype(BF16)
    b = (x - a.astype(F32)).astype(BF16)
    return a, b


def _mm(a, b, *, name, trans_a=False, trans_b=False, out_dtype=F32, tm, tn, tk):
    assert not (trans_a and trans_b)
    if trans_a:
        K, M = a.shape
    else:
        M, K = a.shape
    N = b.shape[0] if trans_b else b.shape[1]
    assert (b.shape[1] if trans_b else b.shape[0]) == K
    assert M % tm == 0 and N % tn == 0 and K % tk == 0, (name, M, N, K, tm, tn, tk)
    nk = K // tk

    def body(a_ref, b_ref, o_ref, *scratch):
        av = a_ref[...].astype(BF16)
        bv = b_ref[...].astype(BF16)
        prod = _tn(av, bv) if trans_a else (_nt(av, bv) if trans_b else _nn(av, bv))
        if nk == 1:
            o_ref[...] = prod.astype(out_dtype)
        else:
            acc = scratch[0]
            k = pl.program_id(2)

            @pl.when(k == 0)
            def _():
                acc[...] = prod

            @pl.when(k > 0)
            def _():
                acc[...] += prod

            @pl.when(k == nk - 1)
            def _():
                o_ref[...] = acc[...].astype(out_dtype)

    if trans_a:
        a_spec = pl.BlockSpec((tk, tm), lambda i, j, k: (k, i))
    else:
        a_spec = pl.BlockSpec((tm, tk), lambda i, j, k: (i, k))
    if trans_b:
        b_spec = pl.BlockSpec((tn, tk), lambda i, j, k: (j, k))
    else:
        b_spec = pl.BlockSpec((tk, tn), lambda i, j, k: (k, j))
    blocks = (_nbytes((tm, tk), a.dtype) + _nbytes((tk, tn), b.dtype) + _nbytes((tm, tn), out_dtype))
    scratch = [pltpu.VMEM((tm, tn), F32)] if nk > 1 else []
    return pl.pallas_call(
        body,
        out_shape=jax.ShapeDtypeStruct((M, N), out_dtype),
        grid=(M // tm, N // tn, nk),
        in_specs=[a_spec, b_spec],
        out_specs=pl.BlockSpec((tm, tn), lambda i, j, k: (i, j)),
        scratch_shapes=scratch,
        compiler_params=_params(("parallel", "parallel", "arbitrary"), blocks + _nbytes((tm, tn), F32),
                                _nbytes((tm, tn), F32) if nk > 1 else 0),
        name=name,
    )(a, b)


def _h_tile(j, x_ref, meta_ref):
    head = jnp.concatenate([jnp.zeros((FRONT, D), F32), meta_ref[...]], axis=0)
    return jnp.where(j > 0, x_ref[0], head)


def _x_spec():
    return pl.BlockSpec((1, QB, D), lambda b, j: (b, jnp.maximum(j - 1, 0), 0))


def _rms_in(x, meta, g, B, Lp):
    T = B * Lp
    NQ = Lp // QB

    def body(x_ref, meta_ref, g_ref, u_ref):
        h = _h_tile(pl.program_id(1), x_ref, meta_ref)
        r = lax.rsqrt(jnp.mean(h * h, axis=-1, keepdims=True) + EPS)
        u_ref[...] = (h * r * g_ref[...]).astype(BF16)

    return pl.pallas_call(
        body,
        out_shape=jax.ShapeDtypeStruct((T, D), BF16),
        grid=(B, NQ),
        in_specs=[_x_spec(), pl.BlockSpec((N_META, D), lambda b, j: (0, 0)), pl.BlockSpec((1, D), lambda b, j: (0, 0))],
        out_specs=pl.BlockSpec((QB, D), lambda b, j: (b * NQ + j, 0)),
        compiler_params=_params(("parallel", "parallel"), _nbytes((QB, D), F32) * 2),
        name="rms_in",
    )(x, meta, g)


def _gla_gate(lr, wg, bg, valid):
    pre = _nn(lr.astype(BF16), wg) + bg
    logsig = jnp.minimum(pre, 0.0) - jnp.log(1.0 + jnp.exp(-jnp.abs(pre)))
    return pre, jnp.where(valid, logsig / GLA_NORMALIZER, 0.0)


def _tri_masks():
    ri = lax.broadcasted_iota(jnp.int32, (GLA_C, GLA_C), 0)
    ci = lax.broadcasted_iota(jnp.int32, (GLA_C, GLA_C), 1)
    return ci <= ri, ci >= ri


def _cumsum_rows(x, ones_mask):
    w = jnp.where(ones_mask, 1.0, 0.0).astype(BF16)
    a, b = _split2(x)
    return _nn(w, a) + _nn(w, b)


def _gla_fwd(projA, projB, wg, bg, gn4, B, Lp):
    T = B * Lp
    NC = Lp // GLA_C
    C = GLA_C
    scale = GLA_DK ** -0.5

    def body(q_ref, k_ref, v_ref, lr_ref, z_ref, wg_ref, bg_ref, gn_ref, oa_ref, ya_ref, ssave_ref, st_ref):
        n = pl.program_id(0)

        @pl.when(n == 0)
        def _():
            st_ref[...] = jnp.zeros_like(st_ref)

        pos = n * C + lax.broadcasted_iota(jnp.int32, (C, 1), 0)
        lower, _ = _tri_masks()
        is_last = lax.broadcasted_iota(jnp.int32, (C, 1), 0) == C - 1
        for b in range(B):
            ssave_ref[b, 0] = st_ref[b]
            _, glog = _gla_gate(lr_ref[b], wg_ref[...], bg_ref[...], pos >= FRONT)
            bcum = _cumsum_rows(glog, lower)
            for h in range(GLA_H):
                ks = slice(h * GLA_DK, (h + 1) * GLA_DK)
                vs = slice(h * GLA_DV, (h + 1) * GLA_DV)
                bh = bcum[:, ks]
                blast = jnp.sum(jnp.where(is_last, bh, 0.0), axis=0, keepdims=True)
                qh = q_ref[b, :, ks].astype(F32) * scale
                kh = k_ref[b, :, ks].astype(F32)
                qe = (qh * jnp.exp(bh)).astype(BF16)
                ke = (kh * jnp.exp(-bh)).astype(BF16)
                kl = (kh * jnp.exp(blast - bh)).astype(BF16)
                vh = v_ref[b, :, vs].astype(BF16)
                a = jnp.where(lower, _nt(qe, ke), 0.0).astype(BF16)
                st = st_ref[b, h]
                o = _nn(a, vh) + _nt(qe, st.astype(BF16))
                st_ref[b, h] = st * jnp.exp(blast) + _tn(vh, kl)
                oa_ref[b, :, vs] = o.astype(BF16)
                on = o * lax.rsqrt(jnp.mean(o * o, axis=-1, keepdims=True) + EPS) * gn_ref[:, vs]
                z = z_ref[b, :, vs].astype(F32)
                ya_ref[b, :, vs] = (on * (z * _sigmoid(z))).astype(BF16)

    blocks = B * (_nbytes((C, 512), F32) * 2 + _nbytes((C, 1024), F32) * 3 + _nbytes((C, 1024), BF16)
                  + _nbytes((GLA_H, GLA_DV, GLA_DK), F32)) + _nbytes((128, 512), BF16)
    state = _nbytes((B, GLA_H, GLA_DV, GLA_DK), F32)
    pa = projA.reshape(B, Lp, projA.shape[1])
    oa, ya, ssave = pl.pallas_call(
        body,
        out_shape=(jax.ShapeDtypeStruct((B, Lp, GLA_VW), BF16), jax.ShapeDtypeStruct((B, Lp, GLA_VW), BF16),
                   jax.ShapeDtypeStruct((B, NC, GLA_H, GLA_DV, GLA_DK), F32)),
        grid=(NC,),
        in_specs=[
            pl.BlockSpec((B, C, 512), lambda n: (0, n, 10)),
            pl.BlockSpec((B, C, 512), lambda n: (0, n, 11)),
            pl.BlockSpec((B, C, 1024), lambda n: (0, n, 0)),
            pl.BlockSpec((B, C, 128), lambda n: (0, n, 3)),
            pl.BlockSpec((B, C, 1024), lambda n: (0, n, 1)),
            pl.BlockSpec((128, 512), lambda n: (0, 0)),
            pl.BlockSpec((1, 512), lambda n: (0, 0)),
            pl.BlockSpec((1, 1024), lambda n: (0, 0)),
        ],
        out_specs=(pl.BlockSpec((B, C, 1024), lambda n: (0, n, 0)),
                   pl.BlockSpec((B, C, 1024), lambda n: (0, n, 0)),
                   pl.BlockSpec((B, 1, GLA_H, GLA_DV, GLA_DK), lambda n: (0, n, 0, 0, 0))),
        scratch_shapes=[pltpu.VMEM((B, GLA_H, GLA_DV, GLA_DK), F32)],
        compiler_params=_params(("arbitrary",), blocks, state),
        name="gla_fwd",
    )(pa, pa, pa, projB.reshape(B, Lp, projB.shape[1]), pa, wg, bg, gn4)
    return oa.reshape(T, GLA_VW), ya.reshape(T, GLA_VW), ssave


def _swap_halves(x):
    lane = lax.broadcasted_iota(jnp.int32, x.shape, 1)
    return jnp.where((lane % 64) < 32, pltpu.roll(x, 96, 1), pltpu.roll(x, 32, 1))


def _mla_prep(projB, cos_t, sin_t, gq, gkv, wuq2, wukv, B, Lp, tr):
    T = B * Lp
    nt = Lp // tr
    HW = 2 * LANES

    def body(pb_ref, cos_ref, sin_ref, gq_ref, gkv_ref, wuq_ref, wukv_ref, q_ref, k_ref, v_ref, cqn_ref, ckvn_ref):
        cq = pb_ref[:, 0:Q_RANK].astype(F32)
        ckv = pb_ref[:, Q_RANK:Q_RANK + KV_RANK].astype(F32)
        kr = pb_ref[:, 512:640].astype(F32)
        cqn = (cq * lax.rsqrt(jnp.mean(cq * cq, axis=-1, keepdims=True) + EPS) * gq_ref[...]).astype(BF16)
        ckvn = (ckv * lax.rsqrt(jnp.mean(ckv * ckv, axis=-1, keepdims=True) + EPS) * gkv_ref[...]).astype(BF16)
        cqn_ref[...] = cqn
        ckvn_ref[...] = ckvn
        qf = _nn(cqn, wuq_ref[...])
        kvf = _nn(ckvn, wukv_ref[...])
        cs = cos_ref[...]
        sn = sin_ref[...]
        rope = lambda t: t * cs + _swap_halves(t) * sn
        kr_r = rope(kr).astype(BF16)
        for h in range(MLA_H):
            q_ref[:, h * HW:h * HW + LANES] = qf[:, h * HW:h * HW + LANES].astype(BF16)
            q_ref[:, h * HW + LANES:(h + 1) * HW] = rope(qf[:, h * HW + LANES:(h + 1) * HW]).astype(BF16)
            k_ref[:, h * HW:h * HW + LANES] = kvf[:, h * HW:h * HW + LANES].astype(BF16)
            k_ref[:, h * HW + LANES:(h + 1) * HW] = kr_r
            v_ref[:, h * MLA_DV:(h + 1) * MLA_DV] = kvf[:, h * HW + LANES:(h + 1) * HW].astype(BF16)

    blocks = (_nbytes((tr, 640), F32) + 2 * _nbytes((tr, 128), F32) + _nbytes((Q_RANK, 2048), BF16)
              + _nbytes((KV_RANK, 2048), BF16) + _nbytes((tr, 2048 * 2 + 1024 + 384), BF16)
              + 2 * _nbytes((tr, 2048), F32))
    return pl.pallas_call(
        body,
        out_shape=(jax.ShapeDtypeStruct((T, MLA_H * HW), BF16), jax.ShapeDtypeStruct((T, MLA_H * HW), BF16),
                   jax.ShapeDtypeStruct((T, MLA_H * MLA_DV), BF16), jax.ShapeDtypeStruct((T, Q_RANK), BF16),
                   jax.ShapeDtypeStruct((T, KV_RANK), BF16)),
        grid=(B, nt),
        in_specs=[
            pl.BlockSpec((tr, 640), lambda b, j: (b * nt + j, 0)),
            pl.BlockSpec((tr, 128), lambda b, j: (j, 0)),
            pl.BlockSpec((tr, 128), lambda b, j: (j, 0)),
            pl.BlockSpec((1, Q_RANK), lambda b, j: (0, 0)),
            pl.BlockSpec((1, KV_RANK), lambda b, j: (0, 0)),
            pl.BlockSpec((Q_RANK, 2048), lambda b, j: (0, 0)),
            pl.BlockSpec((KV_RANK, 2048), lambda b, j: (0, 0)),
        ],
        out_specs=(pl.BlockSpec((tr, 2048), lambda b, j: (b * nt + j, 0)),
                   pl.BlockSpec((tr, 2048), lambda b, j: (b * nt + j, 0)),
                   pl.BlockSpec((tr, 1024), lambda b, j: (b * nt + j, 0)),
                   pl.BlockSpec((tr, Q_RANK), lambda b, j: (b * nt + j, 0)),
                   pl.BlockSpec((tr, KV_RANK), lambda b, j: (b * nt + j, 0))),
        compiler_params=_params(("parallel", "parallel"), blocks),
        name="mla_prep",
    )(projB, cos_t, sin_t, gq, gkv, wuq2, wukv)


def _attn_mask(row, col):
    return (col <= row) & ((col >= FRONT) | (row < FRONT))


def _attn_fwd(q_att, k_att, v_att, projA, B, Lp):
    T = B * Lp
    NQ = Lp // QB
    HW = 2 * LANES
    scale = 1.0 / math.sqrt(MLA_QK)

    def body(q_ref, k_ref, v_ref, mz_ref, o_ref, yb_ref, lsec_ref, m_ref, l_ref, acc_ref):
        qi = pl.program_id(1)
        m_ref[...] = jnp.full(m_ref.shape, NEG, F32)
        l_ref[...] = jnp.zeros_like(l_ref)
        acc_ref[...] = jnp.zeros_like(acc_ref)
        row = qi * QB + lax.broadcasted_iota(jnp.int32, (QB, QB), 0)
        coli = lax.broadcasted_iota(jnp.int32, (QB, QB), 1)

        def step(kj, masked):
            off = pl.multiple_of(kj * QB, QB)
            ok = _attn_mask(row, kj * QB + coli) if masked else None
            for h in range(MLA_H):
                q = q_ref[:, h * HW:(h + 1) * HW]
                kb = k_ref[pl.ds(off, QB), h * HW:(h + 1) * HW]
                vb = v_ref[pl.ds(off, QB), h * MLA_DV:(h + 1) * MLA_DV]
                s = _nt(q, kb) * (scale * LOG2E)
                if masked:
                    s = jnp.where(ok, s, NEG)
                m_old = m_ref[h]
                m_new = jnp.maximum(m_old, jnp.max(s, axis=-1, keepdims=True))
                alpha = jnp.exp2(m_old - m_new)
                p = jnp.exp2(s - jnp.tile(m_new, (1, QB // LANES)))
                m_ref[h] = m_new
                l_ref[h] = alpha * l_ref[h] + jnp.sum(p, axis=-1, keepdims=True)
                acc_ref[h] = alpha * acc_ref[h] + _nn(p.astype(BF16), vb)

        step(0, True)

        def unmasked(kj, carry):
            step(kj, False)
            return carry

        lax.fori_loop(1, qi, unmasked, 0)

        @pl.when(qi > 0)
        def _():
            step(qi, True)

        for h in range(MLA_H):
            hs = slice(h * MLA_DV, (h + 1) * MLA_DV)
            l = l_ref[h]
            o = acc_ref[h] / l
            o_ref[:, hs] = o.astype(BF16)
            z = mz_ref[:, hs].astype(F32)
            yb_ref[:, hs] = (o * (z * _sigmoid(z))).astype(BF16)
            lse2 = m_ref[h] + jnp.log(l) * LOG2E
            lsec_ref[0, h, pl.ds(qi, 1), :] = jnp.transpose(lse2)[0:1, :]

    blocks = (_nbytes((QB, 2048), BF16) + _nbytes((Lp, 2048), BF16) + _nbytes((Lp, 1024), BF16)
              + 2 * _nbytes((QB, 1024), F32) + _nbytes((QB, 1024), BF16) + _nbytes((MLA_H, QB, LANES), F32)
              + _nbytes((MLA_H, NQ, QB), F32))
    return pl.pallas_call(
        body,
        out_shape=(jax.ShapeDtypeStruct((T, MLA_H * MLA_DV), BF16), jax.ShapeDtypeStruct((T, MLA_H * MLA_DV), BF16),
                   jax.ShapeDtypeStruct((B, MLA_H, NQ, QB), F32)),
        grid=(B, NQ),
        in_specs=[
            pl.BlockSpec((QB, MLA_H * HW), lambda b, i: (b * NQ + i, 0)),
            pl.BlockSpec((Lp, MLA_H * HW), lambda b, i: (b, 0)),
            pl.BlockSpec((Lp, MLA_H * MLA_DV), lambda b, i: (b, 0)),
            pl.BlockSpec((QB, 1024), lambda b, i: (b * NQ + i, 2)),
        ],
        out_specs=(pl.BlockSpec((QB, 1024), lambda b, i: (b * NQ + i, 0)),
                   pl.BlockSpec((QB, 1024), lambda b, i: (b * NQ + i, 0)),
                   pl.BlockSpec((1, MLA_H, NQ, QB), lambda b, i: (b, 0, 0, 0))),
        scratch_shapes=[pltpu.VMEM((MLA_H, QB, LANES), F32), pltpu.VMEM((MLA_H, QB, LANES), F32),
                        pltpu.VMEM((MLA_H, QB, MLA_DV), F32)],
        compiler_params=_params(("parallel", "arbitrary"), blocks, 3 * _nbytes((MLA_H, QB, LANES), F32)),
        name="attn_fwd",
    )(q_att, k_att, v_att, projA)


def _out_proj_loss(x, meta, projA, ya, yb, w_out, gf, tgt, B, Lp):
    T = B * Lp
    NQ = Lp // QB

    def body(x_ref, meta_ref, gg_ref, gm_ref, ya_ref, yb_ref, w_ref, gf_ref, t_ref,
             dhb_ref, mg_ref, loss_ref, dgf_ref):
        b = pl.program_id(0)
        j = pl.program_id(1)

        @pl.when((b == 0) & (j == 0))
        def _():
            loss_ref[...] = jnp.zeros_like(loss_ref)
            dgf_ref[...] = jnp.zeros_like(dgf_ref)

        f32 = lambda ref: ref[...].astype(F32)
        merged = (_sigmoid(f32(gg_ref)) * f32(ya_ref) + _sigmoid(f32(gm_ref)) * f32(yb_ref)).astype(BF16)
        mg_ref[...] = merged
        h1 = _h_tile(j, x_ref, meta_ref) + _nn(merged, w_ref[...])
        r = lax.rsqrt(jnp.mean(h1 * h1, axis=-1, keepdims=True) + EPS)
        hn = h1 * r
        gfv = gf_ref[...]
        diff = jnp.where(j > 0, hn * gfv - t_ref[0], 0.0)
        loss_ref[...] += (0.5 / D) * jnp.sum(jnp.sum(diff * diff, axis=-1, keepdims=True), axis=0, keepdims=True)
        dout = diff * (1.0 / D)
        dgf_ref[...] += jnp.sum(dout * hn, axis=0, keepdims=True)
        dhn = dout * gfv
        dh = r * (dhn - hn * jnp.mean(dhn * hn, axis=-1, keepdims=True))
        dhb_ref[...] = dh.astype(BF16)

    rows = lambda c: pl.BlockSpec((QB, D), lambda b, j: (b * NQ + j, c))
    const = lambda s: pl.BlockSpec(s, lambda b, j: (0, 0))
    return pl.pallas_call(
        body,
        out_shape=(jax.ShapeDtypeStruct((T, D), BF16), jax.ShapeDtypeStruct((T, D), BF16),
                   jax.ShapeDtypeStruct((1, 1), F32), jax.ShapeDtypeStruct((1, D), F32)),
        grid=(B, NQ),
        in_specs=[_x_spec(), const((N_META, D)), rows(3), rows(4), rows(0), rows(0), const((D, D)),
                  const((1, D)), _x_spec()],
        out_specs=(rows(0), rows(0), const((1, 1)), const((1, D))),
        compiler_params=_params(("arbitrary", "arbitrary"), 10 * _nbytes((QB, D), F32)),
        name="out_proj_loss",
    )(x, meta, projA, projA, ya, yb, w_out, gf, tgt)


def _merge_bwd(dh1_b, w_out, projA, ya, yb, tr):
    T = dh1_b.shape[0]

    def body(dh_ref, w_ref, gg_ref, gm_ref, ya_ref, yb_ref, dya_ref, dyb_ref, da_ref):
        d = _nt(dh_ref[...], w_ref[...])
        sg = _sigmoid(gg_ref[...].astype(F32))
        sm = _sigmoid(gm_ref[...].astype(F32))
        dya_ref[...] = (d * sg).astype(BF16)
        dyb_ref[...] = (d * sm).astype(BF16)
        da_ref[:, 0:D] = (d * ya_ref[...].astype(F32) * (sg * (1.0 - sg))).astype(BF16)
        da_ref[:, D:2 * D] = (d * yb_ref[...].astype(F32) * (sm * (1.0 - sm))).astype(BF16)

    spec = lambda c: pl.BlockSpec((tr, D), lambda i: (i, c))
    return pl.pallas_call(
        body,
        out_shape=(jax.ShapeDtypeStruct((T, D), BF16), jax.ShapeDtypeStruct((T, D), BF16),
                   jax.ShapeDtypeStruct((T, 2 * D), BF16)),
        grid=(T // tr,),
        in_specs=[spec(0), pl.BlockSpec((D, D), lambda i: (0, 0)), spec(3), spec(4), spec(0), spec(0)],
        out_specs=(spec(0), spec(0), pl.BlockSpec((tr, 2 * D), lambda i: (i, 0))),
        compiler_params=_params(("parallel",), 8 * _nbytes((tr, D), F32)),
        name="merge_bwd",
    )(dh1_b, w_out, projA, projA, ya, yb)


def _gla_out_bwd(dya, gla_proj, oa, projA, gn4, tr):
    T = dya.shape[0]
    nsteps = T // tr

    def body(dya_ref, w_ref, oa_ref, z_ref, gn_ref, do_ref, dz_ref, dgn_ref, acc_ref):
        i = pl.program_id(0)

        @pl.when(i == 0)
        def _():
            acc_ref[...] = jnp.zeros_like(acc_ref)

        dy_all = _nt(dya_ref[...], w_ref[...])
        for h in range(GLA_H):
            vs = slice(h * GLA_DV, (h + 1) * GLA_DV)
            dy = dy_all[:, vs]
            o = oa_ref[:, vs].astype(F32)
            z = z_ref[:, vs].astype(F32)
            gn = gn_ref[:, vs]
            s = _sigmoid(z)
            ra = lax.rsqrt(jnp.mean(o * o, axis=-1, keepdims=True) + EPS)
            on = o * ra
            don = dy * (z * s)
            t = don * gn
            do_ref[:, vs] = (ra * (t - on * jnp.mean(t * on, axis=-1, keepdims=True))).astype(BF16)
            dz_ref[:, vs] = (dy * (on * gn) * (s * (1.0 + z * (1.0 - s)))).astype(BF16)
            acc_ref[:, vs] += jnp.sum(don * on, axis=0, keepdims=True)

        @pl.when(i == nsteps - 1)
        def _():
            a = acc_ref[...]
            dgn_ref[...] = a[:, 0:256] + a[:, 256:512] + a[:, 512:768] + a[:, 768:1024]

    spec = lambda c: pl.BlockSpec((tr, D), lambda i: (i, c))
    return pl.pallas_call(
        body,
        out_shape=(jax.ShapeDtypeStruct((T, D), BF16), jax.ShapeDtypeStruct((T, D), BF16),
                   jax.ShapeDtypeStruct((1, GLA_DV), F32)),
        grid=(nsteps,),
        in_specs=[spec(0), pl.BlockSpec((D, D), lambda i: (0, 0)), spec(0), spec(1),
                  pl.BlockSpec((1, D), lambda i: (0, 0))],
        out_specs=(spec(0), spec(0), pl.BlockSpec((1, GLA_DV), lambda i: (0, 0))),
        scratch_shapes=[pltpu.VMEM((1, D), F32)],
        compiler_params=_params(("arbitrary",), 6 * _nbytes((tr, D), F32)),
        name="gla_out_bwd",
    )(dya, gla_proj, oa, projA, gn4)


def _gla_bwd(projA, projB, ssave, doa, wg, bg, B, Lp):
    T = B * Lp
    NC = Lp // GLA_C
    C = GLA_C
    scale = GLA_DK ** -0.5
    WC = 2304

    def body(q_ref, k_ref, v_ref, lr_ref, ss_ref, do_ref, wg_ref, bg_ref, dc_ref, dwg_ref, dbg_ref, dst_ref):
        i = pl.program_id(0)
        n = NC - 1 - i

        @pl.when(i == 0)
        def _():
            dst_ref[...] = jnp.zeros_like(dst_ref)
            dwg_ref[...] = jnp.zeros_like(dwg_ref)
            dbg_ref[...] = jnp.zeros_like(dbg_ref)

        pos = n * C + lax.broadcasted_iota(jnp.int32, (C, 1), 0)
        valid = pos >= FRONT
        lower, upper = _tri_masks()
        is_last = lax.broadcasted_iota(jnp.int32, (C, 1), 0) == C - 1
        for b in range(B):
            lr = lr_ref[b]
            pre, glog = _gla_gate(lr, wg_ref[...], bg_ref[...], valid)
            bcum = _cumsum_rows(glog, lower)
            db_parts = []
            for h in range(GLA_H):
                ks = slice(h * GLA_DK, (h + 1) * GLA_DK)
                vs = slice(h * GLA_DV, (h + 1) * GLA_DV)
                bh = bcum[:, ks]
                blast = jnp.sum(jnp.where(is_last, bh, 0.0), axis=0, keepdims=True)
                eb, enb, ekl, ebl = jnp.exp(bh), jnp.exp(-bh), jnp.exp(blast - bh), jnp.exp(blast)
                qh = q_ref[b, :, ks].astype(F32) * scale
                kh = k_ref[b, :, ks].astype(F32)
                qe_f, ke_f, kl_f = qh * eb, kh * enb, kh * ekl
                qe, ke, kl = qe_f.astype(BF16), ke_f.astype(BF16), kl_f.astype(BF16)
                vh = v_ref[b, :, vs].astype(BF16)
                doh = do_ref[b, :, vs]
                st = ss_ref[b, 0, h]
                dst = dst_ref[b, h]
                st_b, dst_b = st.astype(BF16), dst.astype(BF16)
                da = jnp.where(lower, _nt(doh, vh), 0.0).astype(BF16)
                da_t = jnp.where(upper, _nt(vh, doh), 0.0).astype(BF16)
                a_t = jnp.where(upper, _nt(ke, qe), 0.0).astype(BF16)
                dqe = _nn(da, ke) + _nn(doh, st_b)
                dke = _nn(da_t, qe)
                dvh = _nn(a_t, doh) + _nt(kl, dst_b)
                dkl = _nn(vh, dst_b)
                dst_ref[b, h] = dst * ebl + _tn(doh, qe)
                deb = jnp.sum(st * dst, axis=0, keepdims=True)
                db = dqe * qe_f - dke * ke_f - dkl * kl_f
                db_last = jnp.sum(dkl * kl_f, axis=0, keepdims=True) + deb * ebl
                db_parts.append(db + jnp.where(is_last, db_last, 0.0))
                dc_ref[b, :, vs] = dvh.astype(BF16)
                dc_ref[b, :, 1024 + h * GLA_DK:1024 + (h + 1) * GLA_DK] = (dqe * eb * scale).astype(BF16)
                dc_ref[b, :, 1536 + h * GLA_DK:1536 + (h + 1) * GLA_DK] = (dke * enb + dkl * ekl).astype(BF16)
            dglog = _cumsum_rows(jnp.concatenate(db_parts, axis=1), upper)
            dpre = jnp.where(valid, dglog * (1.0 / GLA_NORMALIZER) / (1.0 + jnp.exp(pre)), 0.0)
            dpre_b = dpre.astype(BF16)
            dc_ref[b, :, 2048:2176] = _nt(dpre_b, wg_ref[...]).astype(BF16)
            dc_ref[b, :, 2176:2304] = jnp.zeros((C, 128), BF16)
            dwg_ref[...] += _tn(lr.astype(BF16), dpre_b)
            dbg_ref[...] += jnp.sum(dpre, axis=0, keepdims=True)

    blocks = B * (_nbytes((C, 512), F32) * 2 + _nbytes((C, 1024), F32) + _nbytes((C, 1024), BF16)
                  + _nbytes((GLA_H, GLA_DV, GLA_DK), F32) + _nbytes((C, WC), BF16)) + 3 * _nbytes((128, 512), F32)
    state = _nbytes((B, GLA_H, GLA_DV, GLA_DK), F32)
    pa = projA.reshape(B, Lp, projA.shape[1])
    rev = lambda i: NC - 1 - i
    dc, dwg, dbg = pl.pallas_call(
        body,
        out_shape=(jax.ShapeDtypeStruct((B, Lp, WC), BF16), jax.ShapeDtypeStruct((128, GLA_KW), F32),
                   jax.ShapeDtypeStruct((1, GLA_KW), F32)),
        grid=(NC,),
        in_specs=[
            pl.BlockSpec((B, C, 512), lambda i: (0, rev(i), 10)),
            pl.BlockSpec((B, C, 512), lambda i: (0, rev(i), 11)),
            pl.BlockSpec((B, C, 1024), lambda i: (0, rev(i), 0)),
            pl.BlockSpec((B, C, 128), lambda i: (0, rev(i), 3)),
            pl.BlockSpec((B, 1, GLA_H, GLA_DV, GLA_DK), lambda i: (0, rev(i), 0, 0, 0)),
            pl.BlockSpec((B, C, 1024), lambda i: (0, rev(i), 0)),
            pl.BlockSpec((128, 512), lambda i: (0, 0)),
            pl.BlockSpec((1, 512), lambda i: (0, 0)),
        ],
        out_specs=(pl.BlockSpec((B, C, WC), lambda i: (0, rev(i), 0)),
                   pl.BlockSpec((128, GLA_KW), lambda i: (0, 0)),
                   pl.BlockSpec((1, GLA_KW), lambda i: (0, 0))),
        scratch_shapes=[pltpu.VMEM((B, GLA_H, GLA_DV, GLA_DK), F32)],
        compiler_params=_params(("arbitrary",), blocks, state),
        name="gla_bwd",
    )(pa, pa, pa, projB.reshape(B, Lp, projB.shape[1]), ssave, doa.reshape(B, Lp, GLA_VW), wg, bg)
    return dc.reshape(T, WC), dwg, dbg


def _attn_bwd_pre(dyb, mla_proj, projA, ob, B, Lp):
    T = B * Lp
    NQ = Lp // QB

    def body(dyb_ref, w_ref, z_ref, o_ref, do_ref, dz_ref, dcol_ref):
        j = pl.program_id(1)
        dy_all = _nt(dyb_ref[...], w_ref[...])
        for h in range(MLA_H):
            hs = slice(h * MLA_DV, (h + 1) * MLA_DV)
            dy = dy_all[:, hs]
            z = z_ref[:, hs].astype(F32)
            o = o_ref[:, hs].astype(F32)
            s = _sigmoid(z)
            do = dy * (z * s)
            do_ref[:, hs] = do.astype(BF16)
            dz_ref[:, hs] = (dy * o * (s * (1.0 + z * (1.0 - s)))).astype(BF16)
            dl = jnp.broadcast_to(jnp.sum(do * o, axis=-1, keepdims=True), (QB, LANES))
            dcol_ref[0, h, pl.ds(j, 1), :] = jnp.transpose(dl)[0:1, :]

    rows = lambda c: pl.BlockSpec((QB, D), lambda b, j: (b * NQ + j, c))
    return pl.pallas_call(
        body,
        out_shape=(jax.ShapeDtypeStruct((T, D), BF16), jax.ShapeDtypeStruct((T, D), BF16),
                   jax.ShapeDtypeStruct((B, MLA_H, NQ, QB), F32)),
        grid=(B, NQ),
        in_specs=[rows(0), pl.BlockSpec((D, D), lambda b, j: (0, 0)), rows(2), rows(0)],
        out_specs=(rows(0), rows(0), pl.BlockSpec((1, MLA_H, NQ, QB), lambda b, j: (b, 0, 0, 0))),
        compiler_params=_params(("parallel", "arbitrary"), 6 * _nbytes((QB, D), F32)),
        name="attn_bwd_pre",
    )(dyb, mla_proj, projA, ob)


ATTN_BWD_HEADS = 8


def _attn_bwd(q_att, k_att, v_att, do, lse_c, delta_c, B, Lp):
    T = B * Lp
    NQ = Lp // QB
    G = ATTN_BWD_HEADS
    NG = MLA_H // G
    HW = 2 * LANES
    scale = 1.0 / math.sqrt(MLA_QK)

    def body(q_ref, k_ref, v_ref, do_ref, lse_ref, dl_ref, dq_out, dk_out, dv_out, dq_ref, dk_ref, dv_ref):
        kj = pl.program_id(2)
        dk_ref[...] = jnp.zeros_like(dk_ref)
        dv_ref[...] = jnp.zeros_like(dv_ref)
        col = kj * QB + lax.broadcasted_iota(jnp.int32, (QB, QB), 0)
        rowi = lax.broadcasted_iota(jnp.int32, (QB, QB), 1)

        def step(qi, masked, first):
            off = pl.multiple_of(qi * QB, QB)
            ok = _attn_mask(qi * QB + rowi, col) if masked else None
            for h in range(G):
                ws = slice(h * HW, (h + 1) * HW)
                hs = slice(h * MLA_DV, (h + 1) * MLA_DV)
                qb = q_ref[pl.ds(off, QB), ws]
                dob = do_ref[pl.ds(off, QB), hs]
                kb = k_ref[:, ws]
                lse2 = lse_ref[0, h, pl.ds(qi, 1), :]
                delta = dl_ref[0, h, pl.ds(qi, 1), :]
                p_t = jnp.exp2(_nt(kb, qb) * (scale * LOG2E) - lse2)
                if masked:
                    p_t = jnp.where(ok, p_t, 0.0)
                dv_ref[:, hs] += _nn(p_t.astype(BF16), dob)
                ds_t = (p_t * (_nt(v_ref[:, hs], dob) - delta) * scale).astype(BF16)
                dk_ref[:, ws] += _nn(ds_t, qb)
                if first:
                    dq_ref[pl.ds(off, QB), ws] = _tn(ds_t, kb)
                else:
                    dq_ref[pl.ds(off, QB), ws] += _tn(ds_t, kb)

        def sweep(masked, first):
            step(kj, True, first)

            def it(qi, carry):
                step(qi, masked, first)
                return carry
            lax.fori_loop(kj + 1, NQ, it, 0)

        pl.when(kj == 0)(lambda: sweep(True, True))
        pl.when(kj > 0)(lambda: sweep(False, False))
        dk_out[...] = dk_ref[...].astype(BF16)
        dv_out[...] = dv_ref[...].astype(BF16)

        @pl.when(kj == NQ - 1)
        def _():
            dq_out[...] = dq_ref[...].astype(BF16)

    blocks = (2 * _nbytes((Lp, G * HW), BF16) + _nbytes((Lp, G * MLA_DV), BF16) + 2 * _nbytes((QB, G * 384), BF16)
              + 2 * _nbytes((G, NQ, QB), F32))
    scratch = [pltpu.VMEM((Lp, G * HW), F32), pltpu.VMEM((QB, G * HW), F32), pltpu.VMEM((QB, G * MLA_DV), F32)]
    return pl.pallas_call(
        body,
        out_shape=(jax.ShapeDtypeStruct((T, MLA_H * HW), BF16), jax.ShapeDtypeStruct((T, MLA_H * HW), BF16),
                   jax.ShapeDtypeStruct((T, MLA_H * MLA_DV), BF16)),
        scratch_shapes=scratch,
        grid=(B, NG, NQ),
        in_specs=[
            pl.BlockSpec((Lp, G * HW), lambda b, g, j: (b, g), pipeline_mode=pl.Buffered(1)),
            pl.BlockSpec((QB, G * HW), lambda b, g, j: (b * NQ + j, g)),
            pl.BlockSpec((QB, G * MLA_DV), lambda b, g, j: (b * NQ + j, g)),
            pl.BlockSpec((Lp, G * MLA_DV), lambda b, g, j: (b, g), pipeline_mode=pl.Buffered(1)),
            pl.BlockSpec((1, G, NQ, QB), lambda b, g, j: (b, g, 0, 0)),
            pl.BlockSpec((1, G, NQ, QB), lambda b, g, j: (b, g, 0, 0)),
        ],
        out_specs=(pl.BlockSpec((Lp, G * HW), lambda b, g, j: (b, g), pipeline_mode=pl.Buffered(1)),
                   pl.BlockSpec((QB, G * HW), lambda b, g, j: (b * NQ + j, g)),
                   pl.BlockSpec((QB, G * MLA_DV), lambda b, g, j: (b * NQ + j, g))),
        compiler_params=_params(("parallel", "parallel", "arbitrary"), blocks,
                                _nbytes((Lp, G * HW), F32) + _nbytes((QB, G * 384), F32)),
        name="attn_bwd",
    )(q_att, k_att, v_att, do, lse_c, delta_c)


def _mla_bwd_post(dq, dk, dv, projB, cos_t, sin_t, gq, gkv, wuq2, wukv, B, Lp, tr):
    T = B * Lp
    nt = Lp // tr
    HW = 2 * LANES

    def body(dq_ref, dk_ref, dv_ref, pb_ref, cos_ref, sin_ref, gq_ref, gkv_ref, wuq_ref, wukv_ref,
             dqf_ref, dkvf_ref, de_ref, dgq_ref, dgkv_ref):
        first = (pl.program_id(0) == 0) & (pl.program_id(1) == 0)

        @pl.when(first)
        def _():
            dgq_ref[...] = jnp.zeros_like(dgq_ref)
            dgkv_ref[...] = jnp.zeros_like(dgkv_ref)

        cs = cos_ref[...]
        sn = sin_ref[...]
        rope_t = lambda t: t * cs + _swap_halves(t * sn)
        dkr = jnp.zeros((tr, LANES), F32)
        for h in range(MLA_H):
            dqf_ref[:, h * HW:h * HW + LANES] = dq_ref[:, h * HW:h * HW + LANES]
            dq_rope = dq_ref[:, h * HW + LANES:(h + 1) * HW].astype(F32)
            dqf_ref[:, h * HW + LANES:(h + 1) * HW] = rope_t(dq_rope).astype(BF16)
            dkvf_ref[:, h * HW:h * HW + LANES] = dk_ref[:, h * HW:h * HW + LANES]
            dkvf_ref[:, h * HW + LANES:(h + 1) * HW] = dv_ref[:, h * MLA_DV:(h + 1) * MLA_DV]
            dkr = dkr + dk_ref[:, h * HW + LANES:(h + 1) * HW].astype(F32)

        def norm_bwd(x, dn, g):
            r = lax.rsqrt(jnp.mean(x * x, axis=-1, keepdims=True) + EPS)
            xn = x * r
            t = dn * g
            return r * (t - xn * jnp.mean(t * xn, axis=-1, keepdims=True)), jnp.sum(dn * xn, axis=0, keepdims=True)

        cq = pb_ref[:, 0:Q_RANK].astype(F32)
        ckv = pb_ref[:, Q_RANK:Q_RANK + KV_RANK].astype(F32)
        dcq, dgq = norm_bwd(cq, _nt(dqf_ref[...], wuq_ref[...]), gq_ref[...])
        dckv, dgkv = norm_bwd(ckv, _nt(dkvf_ref[...], wukv_ref[...]), gkv_ref[...])
        dgq_ref[...] += dgq
        dgkv_ref[...] += dgkv
        de_ref[:, 0:Q_RANK] = dcq.astype(BF16)
        de_ref[:, Q_RANK:Q_RANK + KV_RANK] = dckv.astype(BF16)
        de_ref[:, 384:512] = rope_t(dkr).astype(BF16)

    rows = lambda w: pl.BlockSpec((tr, w), lambda b, j: (b * nt + j, 0))
    const = lambda s: pl.BlockSpec(s, lambda b, j: (0, 0))
    blocks = (2 * _nbytes((tr, 2048), F32) + _nbytes((tr, 1024), F32) + _nbytes((tr, 640), F32)
              + 2 * _nbytes((tr, 2048), BF16) + _nbytes((2048, 384), BF16) + 2 * _nbytes((tr, 2048), F32))
    return pl.pallas_call(
        body,
        out_shape=(jax.ShapeDtypeStruct((T, 2048), BF16), jax.ShapeDtypeStruct((T, 2048), BF16),
                   jax.ShapeDtypeStruct((T, 512), BF16), jax.ShapeDtypeStruct((1, Q_RANK), F32),
                   jax.ShapeDtypeStruct((1, KV_RANK), F32)),
        grid=(B, nt),
        in_specs=[rows(2048), rows(2048), rows(1024), rows(640),
                  pl.BlockSpec((tr, 128), lambda b, j: (j, 0)), pl.BlockSpec((tr, 128), lambda b, j: (j, 0)),
                  const((1, Q_RANK)), const((1, KV_RANK)), const((Q_RANK, 2048)), const((KV_RANK, 2048))],
        out_specs=(rows(2048), rows(2048), rows(512), const((1, Q_RANK)), const((1, KV_RANK))),
        compiler_params=_params(("arbitrary", "arbitrary"), blocks),
        name="mla_bwd_post",
    )(dq, dk, dv, projB, cos_t, sin_t, gq, gkv, wuq2, wukv)


def _in_proj_bwd(x, meta, dh1, dA, dBz, dC, dDz, dE, wA, wB, g, B, Lp):
    NQ = Lp // QB
    seq = x.shape[1]
    R = 2 if B % 2 == 0 else 1
    M = R * QB

    def body(x_ref, meta_ref, dh_ref, da_ref, db_ref, dc_ref, dd_ref, de_ref, wa_ref, wb_ref, g_ref,
             gx_ref, dmeta_ref, dg_ref):
        b = pl.program_id(0)
        j = pl.program_id(1)

        @pl.when((b == 0) & (j == 0))
        def _():
            dg_ref[...] = jnp.zeros_like(dg_ref)

        flat = lambda ref: ref[...].reshape(M, ref.shape[-1])
        da, dbz, dc, dd, de = flat(da_ref), flat(db_ref), flat(dc_ref), flat(dd_ref), flat(de_ref)
        du = _nt(da, wa_ref[:, 3072:5120])
        du = du + _nt(dbz, wa_ref[:, 1024:2048])
        du = du + _nt(dd, wa_ref[:, 2048:3072])
        du = du + _nt(dc[:, 0:1024], wa_ref[:, 0:1024])
        du = du + _nt(dc[:, 1024:2048], wa_ref[:, 5120:6144])
        du = du + _nt(dc[:, 2048:2176], wb_ref[:, 384:512])
        du = du + _nt(de[:, 0:384], wb_ref[:, 0:384])
        du = du + _nt(de[:, 384:512], wb_ref[:, 512:640])

        head = jnp.concatenate([jnp.zeros((FRONT, D), F32), meta_ref[...]], axis=0)
        x = jnp.concatenate([jnp.where(j > 0, x_ref[i], head) for i in range(R)], axis=0)
        r = lax.rsqrt(jnp.mean(x * x, axis=-1, keepdims=True) + EPS)
        xn = x * r
        t = du * g_ref[...]
        dh0 = flat(dh_ref).astype(F32) + r * (t - xn * jnp.mean(t * xn, axis=-1, keepdims=True))
        dg_ref[...] += jnp.sum(du * xn, axis=0, keepdims=True)
        dmeta = dh0[FRONT:HEAD_ROWS, :]
        for i in range(R):
            gx_ref[i] = dh0[i * QB:(i + 1) * QB, :]
            if i > 0:
                dmeta = dmeta + dh0[i * QB + FRONT:i * QB + HEAD_ROWS, :]

        @pl.when((j == 0) & (b == 0))
        def _():
            dmeta_ref[...] = dmeta

        @pl.when((j == 0) & (b > 0))
        def _():
            dmeta_ref[...] += dmeta

    rows = lambda w: pl.BlockSpec((R, QB, w), lambda b, j: (b, j, 0))
    x_rows = pl.BlockSpec((R, QB, D), lambda b, j: (b, jnp.maximum(j - 1, 0), 0))
    const = lambda s: pl.BlockSpec(s, lambda b, j: (0,) * len(s))
    resident = lambda s: pl.BlockSpec(s, lambda b, j: (0, 0), pipeline_mode=pl.Buffered(1))
    by_row = lambda a: a.reshape(B, Lp, a.shape[1])
    widths = [a.shape[1] for a in (dA, dBz, dC, dDz, dE)]
    blocks = sum(_nbytes((M, w), BF16) for w in widths) + 4 * _nbytes((M, D), F32)
    return pl.pallas_call(
        body,
        out_shape=(jax.ShapeDtypeStruct((B, seq, D), F32), jax.ShapeDtypeStruct((N_META, D), F32),
                   jax.ShapeDtypeStruct((1, D), F32)),
        grid=(B // R, NQ),
        in_specs=[x_rows, const((N_META, D)), rows(D)] + [rows(w) for w in widths]
        + [resident(wA.shape), resident(wB.shape), const((1, D))],
        out_specs=(x_rows, const((N_META, D)), const((1, D))),
        compiler_params=_params(("arbitrary", "arbitrary"), blocks, _nbytes(wA.shape, BF16) + _nbytes(wB.shape, BF16)),
        name="in_proj_bwd",
    )(x, meta, by_row(dh1), *[by_row(a) for a in (dA, dBz, dC, dDz, dE)], wA, wB, g)


_VMEM_WHOLE = pl.BlockSpec(memory_space=pltpu.VMEM)


def _params_whole(arrays):
    total = sum(_nbytes(a.shape, a.dtype) for a in arrays)
    return pltpu.CompilerParams(vmem_limit_bytes=int(min(total + 12 * 1024 * 1024, VMEM_CAP_V7X)))


def _wire_dtype(shape):
    return BF16 if shape[-2] * shape[-1] >= WIRE_BF16_MIN_ELEMS else F32


def _pair_add_big(gp, recv, c):
    _, half, cols = recv.shape
    th = _div_tile(half, 64, 16)
    out_dtype = _wire_dtype(recv.shape)

    steps = half // th

    def body(c_ref, a_ref, b_ref, o_ref):
        o_ref[...] = (a_ref[...] + b_ref[...]).astype(out_dtype)

    return pl.pallas_call(
        body,
        out_shape=jax.ShapeDtypeStruct(recv.shape, out_dtype),
        grid_spec=pltpu.PrefetchScalarGridSpec(
            num_scalar_prefetch=1,
            grid=(steps,),
            in_specs=[pl.BlockSpec((4, th, cols), lambda i, c_ref: (0, c_ref[0] * steps + i, 0)),
                      pl.BlockSpec((4, th, cols), lambda i, c_ref: (0, i, 0))],
            out_specs=pl.BlockSpec((4, th, cols), lambda i, c_ref: (0, i, 0)),
        ),
        compiler_params=_params(("parallel",), 3 * _nbytes((4, th, cols), F32)),
        name="grad_pair_add_big",
    )(c, gp, recv)


def _pair_add_small(gps, recvs):
    n = len(gps)

    def body(*refs):
        c = lax.axis_index("c")
        for t in range(n):
            g_ref, r_ref, o_ref = refs[t], refs[n + t], refs[2 * n + t]
            half = r_ref.shape[1]
            s = g_ref[:, pl.ds(pl.multiple_of(c * half, 8), half), :] + r_ref[...]
            o_ref[...] = s.astype(o_ref.dtype)

    return pl.pallas_call(
        body,
        out_shape=[jax.ShapeDtypeStruct(r.shape, _wire_dtype(r.shape)) for r in recvs],
        in_specs=[_VMEM_WHOLE] * (2 * n),
        out_specs=[_VMEM_WHOLE] * n,
        compiler_params=_params_whole(list(gps) + 2 * list(recvs)),
        name="grad_pair_add_small",
    )(*gps, *recvs)


def _chip_order_sum(landed_ref, own_ref, me):
    p = [jnp.where(me == k, own_ref[k], landed_ref[k]).astype(F32) for k in range(4)]
    return ((p[0] + p[1]) + p[2]) + p[3]


def _sum_chips_big(landed, own, pos):
    _, half, cols = landed.shape
    th = _div_tile(half, 64, 16)

    def body(pos_ref, l_ref, s_ref, o_ref):
        o_ref[0] = _chip_order_sum(l_ref, s_ref, pos_ref[1])

    spec = pl.BlockSpec((4, th, cols), lambda i, pos_ref: (0, i, 0))
    return pl.pallas_call(
        body,
        out_shape=jax.ShapeDtypeStruct((2, half, cols), F32),
        grid_spec=pltpu.PrefetchScalarGridSpec(
            num_scalar_prefetch=1,
            grid=(half // th,),
            in_specs=[spec, spec],
            out_specs=pl.BlockSpec((1, th, cols), lambda i, pos_ref: (pos_ref[0], i, 0)),
        ),
        compiler_params=_params(("parallel",), 3 * _nbytes((4, th, cols), F32)),
        name="grad_sum_chips_big",
    )(pos, landed, own)


def _sum_chips_small(landed, own):
    n = len(landed)

    def body(*refs):
        x, y, c = _mesh_pos()
        for t in range(n):
            refs[2 * n + t][c] = _chip_order_sum(refs[t], refs[n + t], 2 * x + y)

    return pl.pallas_call(
        body,
        out_shape=[jax.ShapeDtypeStruct((2,) + p.shape[1:], F32) for p in landed],
        in_specs=[_VMEM_WHOLE] * (2 * n),
        out_specs=[_VMEM_WHOLE] * n,
        compiler_params=_params_whole(list(landed) * 3),
        name="grad_sum_chips_small",
    )(*landed, *own)


def _adamw_update(w_ref, g_ref, m_ref, v_ref, d_ref, mo_ref, vo_ref):
    c1 = 1.0 - ADAM_B1 ** ADAM_STEP
    c2 = 1.0 - ADAM_B2 ** ADAM_STEP
    gv = g_ref[...]
    mn = ADAM_B1 * m_ref[...] + (1.0 - ADAM_B1) * gv
    vn = ADAM_B2 * v_ref[...] + (1.0 - ADAM_B2) * (gv * gv)
    mo_ref[...] = mn
    vo_ref[...] = vn
    d_ref[...] = -ADAM_LR * ((mn / c1) / (jnp.sqrt(vn / c2) + ADAM_EPS) + ADAM_WD * w_ref[...])


def _adamw_big(w, g, m, v):
    lead, (rows, cols) = w.shape[:-2], w.shape[-2:]
    assert all(n == 1 for n in lead)
    tr = _div_tile(rows, (1 << 19) // cols, 8)
    spec = pl.BlockSpec((1,) * len(lead) + (tr, cols), lambda i: (0,) * len(lead) + (i, 0))
    shp = jax.ShapeDtypeStruct(w.shape, F32)
    return pl.pallas_call(
        functools.partial(_adamw_update),
        out_shape=(shp, shp, shp),
        grid=(rows // tr,),
        in_specs=[spec] * 4,
        out_specs=(spec, spec, spec),
        compiler_params=_params(("parallel",), 7 * _nbytes((tr, cols), F32)),
        name="adamw_big",
    )(w, g, m, v)


def _adamw_small(ws, gs, ms, vs):
    n = len(ws)

    def body(*refs):
        for t in range(n):
            _adamw_update(refs[t], refs[n + t], refs[2 * n + t], refs[3 * n + t],
                          refs[4 * n + t], refs[5 * n + t], refs[6 * n + t])

    shapes = [jax.ShapeDtypeStruct(w.shape, F32) for w in ws]
    return pl.pallas_call(
        body,
        out_shape=shapes * 3,
        in_specs=[_VMEM_WHOLE] * (4 * n),
        out_specs=[_VMEM_WHOLE] * (3 * n),
        compiler_params=_params_whole(list(ws) * 7),
        name="adamw_small",
    )(*ws, *gs, *ms, *vs)


def _mesh_pos():
    return lax.axis_index("x"), lax.axis_index("y"), lax.axis_index("c")


def _other_chips(x, y):
    return [(1 - x, y), (x, 1 - y), (1 - x, 1 - y)]


_ANY = pl.BlockSpec(memory_space=pl.ANY)


PAIR_SPLIT_MIN_ROWS = 64


def _weight_gather(shards):
    n = len(shards)
    split = [s.shape[0] >= PAIR_SPLIT_MIN_ROWS for s in shards]

    def body(*refs):
        w_refs, o_refs = refs[:n], refs[n:2 * n]
        send_sems, recv_sems = refs[2 * n:]
        x, y, c = _mesh_pos()
        me = 2 * x + y
        chips = _other_chips(x, y)

        def rows_of(t, core):
            rows = shards[t].shape[0]
            if not split[t]:
                return pl.ds(0, rows)
            return pl.ds(pl.multiple_of(core * (rows // 2), 16), rows // 2)

        def landed(t, k, slot, rows, to):
            ref = o_refs[t].at[slot, rows]
            return pltpu.make_async_remote_copy(src_ref=ref, dst_ref=ref, send_sem=send_sems.at[6 * t + k],
                                                recv_sem=recv_sems.at[6 * t + k], device_id=to, device_id_type=MESH)

        sends = []
        for t in range(n):
            mine = rows_of(t, c)
            for k, (px, py) in enumerate(chips):
                cp = pltpu.make_async_remote_copy(src_ref=w_refs[t].at[mine], dst_ref=o_refs[t].at[me, mine],
                                                  send_sem=send_sems.at[6 * t + k], recv_sem=recv_sems.at[6 * t + k],
                                                  device_id=(px, py, c), device_id_type=MESH)
                cp.start()
                sends.append(cp)
        for t in range(n):
            mine = rows_of(t, c)
            for k, (px, py) in enumerate(chips):
                landed(t, k, 2 * px + py, mine, (x, y, c)).wait_recv()
                if split[t]:
                    cp = landed(t, 3 + k, 2 * px + py, mine, (x, y, 1 - c))
                    cp.start()
                    sends.append(cp)
        for t in range(n):
            if split[t]:
                for k, (px, py) in enumerate(chips):
                    landed(t, 3 + k, 2 * px + py, rows_of(t, 1 - c), (x, y, c)).wait_recv()
        for cp in sends:
            cp.wait_send()

    return pl.pallas_call(
        body,
        out_shape=[jax.ShapeDtypeStruct((4,) + s.shape, s.dtype) for s in shards],
        in_specs=[_ANY] * n,
        out_specs=[_ANY] * n,
        scratch_shapes=[pltpu.SemaphoreType.DMA((6 * n,)), pltpu.SemaphoreType.DMA((6 * n,))],
        name="weight_gather",
    )(*shards)


def _pair_swap(gps):
    n = len(gps)

    def body(*refs):
        g_refs, o_refs = refs[:n], refs[n:2 * n]
        send_sems, recv_sems = refs[2 * n:]
        x, y, c = _mesh_pos()
        copies = []
        for t in range(n):
            half = gps[t].shape[1] // 2
            theirs = pl.ds(pl.multiple_of((1 - c) * half, 8), half)
            cp = pltpu.make_async_remote_copy(src_ref=g_refs[t].at[:, theirs], dst_ref=o_refs[t],
                                              send_sem=send_sems.at[t], recv_sem=recv_sems.at[t],
                                              device_id=(x, y, 1 - c), device_id_type=MESH)
            cp.start()
            copies.append(cp)
        for cp in copies:
            cp.wait_send()
            cp.wait_recv()

    return pl.pallas_call(
        body,
        out_shape=[jax.ShapeDtypeStruct((4, g.shape[1] // 2, g.shape[2]), g.dtype) for g in gps],
        in_specs=[_ANY] * n,
        out_specs=[_ANY] * n,
        scratch_shapes=[pltpu.SemaphoreType.DMA((n,)), pltpu.SemaphoreType.DMA((n,))],
        name="grad_pair_swap",
    )(*gps)


_HBM = pl.BlockSpec(memory_space=pltpu.HBM)
_SEM = pl.BlockSpec(memory_space=pltpu.SEMAPHORE)


def _in_hbm(a):
    return pltpu.with_memory_space_constraint(a, pltpu.HBM)


def _chip_scatter_start(parts):
    n = len(parts)

    def body(*refs):
        s_refs, l_refs = refs[:n], refs[n:2 * n]
        send_sems, recv_sems = refs[2 * n], refs[2 * n + 1]
        token = refs[-1]
        x, y, c = _mesh_pos()
        me = 2 * x + y
        for t in range(n):
            for k, (px, py) in enumerate(_other_chips(x, y)):
                pltpu.make_async_remote_copy(src_ref=s_refs[t].at[2 * px + py], dst_ref=l_refs[t].at[me],
                                             send_sem=send_sems.at[3 * t + k], recv_sem=recv_sems.at[3 * t + k],
                                             device_id=(px, py, c), device_id_type=MESH).start()
        token[...] = jnp.zeros_like(token)

    hbm = [pltpu.HBM(p.shape, p.dtype) for p in parts]
    outs = pl.pallas_call(
        body,
        name="grad_scatter_start",
        out_shape=(pltpu.SemaphoreType.DMA((3 * n,)), pltpu.SemaphoreType.DMA((3 * n,)), *hbm, *hbm,
                   jax.ShapeDtypeStruct((8, LANES), F32)),
        in_specs=[_HBM] * (2 * n),
        out_specs=(_SEM, _SEM, *([_HBM] * (2 * n)), pl.BlockSpec(memory_space=pltpu.VMEM)),
        input_output_aliases={i: 2 + i for i in range(2 * n)},
        compiler_params=pltpu.CompilerParams(has_side_effects=pltpu.SideEffectType.DATAFLOW_SIDE_EFFECTING),
    )(*[_in_hbm(p) for p in parts], *[_in_hbm(lax.empty(p.shape, p.dtype)) for p in parts])
    return outs[0], outs[1], list(outs[2:2 + n]), list(outs[2 + n:2 + 2 * n]), outs[-1]


def _chip_scatter_wait(send_sems, recv_sems, parts, lands, after):
    n = len(parts)

    def body(*refs):
        s_refs, l_refs = refs[:n], refs[n:2 * n]
        send_sems, recv_sems = refs[2 * n], refs[2 * n + 1]
        x, y, c = _mesh_pos()
        me = 2 * x + y
        for t in range(n):
            for k, (px, py) in enumerate(_other_chips(x, y)):
                cp = pltpu.make_async_remote_copy(src_ref=s_refs[t].at[2 * px + py], dst_ref=l_refs[t].at[2 * px + py],
                                                  send_sem=send_sems.at[3 * t + k], recv_sem=recv_sems.at[3 * t + k],
                                                  device_id=(x, y, c), device_id_type=MESH)
                cp.wait_send()
                cp.wait_recv()

    hbm = [pltpu.HBM(p.shape, p.dtype) for p in parts]
    outs = pl.pallas_call(
        body,
        name="grad_scatter_wait",
        out_shape=(*hbm, *hbm),
        in_specs=[_HBM] * (2 * n) + [_SEM, _SEM, _ANY],
        out_specs=[_HBM] * (2 * n),
        input_output_aliases={i: i for i in range(2 * n)},
        compiler_params=pltpu.CompilerParams(has_side_effects=pltpu.SideEffectType.DATAFLOW_SIDE_EFFECTING),
    )(*parts, *lands, send_sems, recv_sems, after)
    return list(outs[:n]), list(outs[n:])


def _late_gather_start(shards):
    n = len(shards)

    def body(*refs):
        w_refs, l_refs = refs[:n], refs[n:2 * n]
        send_sems, recv_sems = refs[2 * n], refs[2 * n + 1]
        token = refs[-1]
        x, y, c = _mesh_pos()
        me = 2 * x + y
        for t in range(n):
            for k, (px, py) in enumerate(_other_chips(x, y)):
                pltpu.make_async_remote_copy(src_ref=w_refs[t], dst_ref=l_refs[t].at[me],
                                             send_sem=send_sems.at[3 * t + k], recv_sem=recv_sems.at[3 * t + k],
                                             device_id=(px, py, c), device_id_type=MESH).start()
        token[...] = jnp.zeros_like(token)

    src = [pltpu.HBM(s.shape, s.dtype) for s in shards]
    land = [pltpu.HBM((4,) + s.shape, s.dtype) for s in shards]
    outs = pl.pallas_call(
        body,
        name="late_gather_start",
        out_shape=(pltpu.SemaphoreType.DMA((3 * n,)), pltpu.SemaphoreType.DMA((3 * n,)), *src, *land,
                   jax.ShapeDtypeStruct((8, LANES), F32)),
        in_specs=[_HBM] * (2 * n),
        out_specs=(_SEM, _SEM, *([_HBM] * (2 * n)), pl.BlockSpec(memory_space=pltpu.VMEM)),
        input_output_aliases={i: 2 + i for i in range(2 * n)},
        compiler_params=pltpu.CompilerParams(has_side_effects=pltpu.SideEffectType.DATAFLOW_SIDE_EFFECTING),
    )(*[_in_hbm(s) for s in shards], *[_in_hbm(lax.empty((4,) + s.shape, s.dtype)) for s in shards])
    return outs[0], outs[1], list(outs[2:2 + n]), list(outs[2 + n:2 + 2 * n]), outs[-1]


def _late_gather_wait(send_sems, recv_sems, shards, lands, after):
    n = len(shards)

    def body(*refs):
        w_refs, l_refs = refs[:n], refs[n:2 * n]
        send_sems, recv_sems = refs[2 * n], refs[2 * n + 1]
        x, y, c = _mesh_pos()
        for t in range(n):
            for k, (px, py) in enumerate(_other_chips(x, y)):
                cp = pltpu.make_async_remote_copy(src_ref=w_refs[t], dst_ref=l_refs[t].at[2 * px + py],
                                                  send_sem=send_sems.at[3 * t + k], recv_sem=recv_sems.at[3 * t + k],
                                                  device_id=(x, y, c), device_id_type=MESH)
                cp.wait_send()
                cp.wait_recv()

    src = [pltpu.HBM(s.shape, s.dtype) for s in shards]
    land = [pltpu.HBM(l.shape, l.dtype) for l in lands]
    outs = pl.pallas_call(
        body,
        name="late_gather_wait",
        out_shape=(*src, *land),
        in_specs=[_HBM] * (2 * n) + [_SEM, _SEM, _ANY],
        out_specs=[_HBM] * (2 * n),
        input_output_aliases={i: i for i in range(2 * n)},
        compiler_params=pltpu.CompilerParams(has_side_effects=pltpu.SideEffectType.DATAFLOW_SIDE_EFFECTING),
    )(*shards, *lands, send_sems, recv_sems, after)
    return list(outs[n:])


def _all_to_all_small(parts):
    n = len(parts)

    def body(*refs):
        p_refs, o_refs = refs[:n], refs[n:2 * n]
        send_sems, recv_sems = refs[2 * n:]
        x, y, c = _mesh_pos()
        me = 4 * x + 2 * y + c
        sends = []
        for t in range(n):
            for k in range(1, 8):
                px, py, pc = x ^ (k >> 2), y ^ ((k >> 1) & 1), c ^ (k & 1)
                cp = pltpu.make_async_remote_copy(src_ref=p_refs[t], dst_ref=o_refs[t].at[me],
                                                  send_sem=send_sems.at[7 * t + k - 1], recv_sem=recv_sems.at[7 * t + k - 1],
                                                  device_id=(px, py, pc), device_id_type=MESH)
                cp.start()
                sends.append(cp)
        for t in range(n):
            for k in range(1, 8):
                peer = 4 * (x ^ (k >> 2)) + 2 * (y ^ ((k >> 1) & 1)) + (c ^ (k & 1))
                pltpu.make_async_remote_copy(src_ref=p_refs[t], dst_ref=o_refs[t].at[peer],
                                             send_sem=send_sems.at[7 * t + k - 1], recv_sem=recv_sems.at[7 * t + k - 1],
                                             device_id=(x, y, c), device_id_type=MESH).wait_recv()
        for cp in sends:
            cp.wait_send()

    return pl.pallas_call(
        body,
        out_shape=[jax.ShapeDtypeStruct((8,) + p.shape, p.dtype) for p in parts],
        in_specs=[_ANY] * n,
        out_specs=[_ANY] * n,
        scratch_shapes=[pltpu.SemaphoreType.DMA((7 * n,)), pltpu.SemaphoreType.DMA((7 * n,))],
        name="grad_small_all_to_all",
    )(*parts)


def _sum_devices_small(landed, own):
    n = len(landed)

    def body(*refs):
        x, y, c = _mesh_pos()
        me = 4 * x + 2 * y + c
        for t in range(n):
            acc = jnp.where(me == 0, refs[n + t][...], refs[t][0])
            for d in range(1, 8):
                acc = acc + jnp.where(me == d, refs[n + t][...], refs[t][d])
            refs[2 * n + t][...] = acc

    return pl.pallas_call(
        body,
        out_shape=[jax.ShapeDtypeStruct(p.shape, F32) for p in own],
        in_specs=[_VMEM_WHOLE] * (2 * n),
        out_specs=[_VMEM_WHOLE] * n,
        compiler_params=_params_whole(list(landed) + 2 * list(own)),
        name="grad_sum_devices_small",
    )(*landed, *own)


def _pair_join(fs):
    n = len(fs)

    def body(*refs):
        f_refs, o_refs = refs[:n], refs[n:2 * n]
        send_sems, recv_sems = refs[2 * n:]
        x, y, c = _mesh_pos()
        sends = []
        for t in range(n):
            cp = pltpu.make_async_remote_copy(src_ref=f_refs[t].at[c], dst_ref=o_refs[t].at[c], send_sem=send_sems.at[t],
                                              recv_sem=recv_sems.at[t], device_id=(x, y, 1 - c), device_id_type=MESH)
            cp.start()
            sends.append(cp)
        for t in range(n):
            pltpu.make_async_remote_copy(src_ref=f_refs[t].at[c], dst_ref=o_refs[t].at[1 - c], send_sem=send_sems.at[t],
                                         recv_sem=recv_sems.at[t], device_id=(x, y, c), device_id_type=MESH).wait_recv()
        for cp in sends:
            cp.wait_send()

    return pl.pallas_call(
        body,
        out_shape=[jax.ShapeDtypeStruct(f.shape, f.dtype) for f in fs],
        in_specs=[_ANY] * n,
        out_specs=[_ANY] * n,
        input_output_aliases={t: t for t in range(n)},
        scratch_shapes=[pltpu.SemaphoreType.DMA((n,)), pltpu.SemaphoreType.DMA((n,))],
        name="grad_pair_join",
    )(*fs)


def _rope_tables(Lp):
    inv = 1.0 / (ROPE_BASE ** (jnp.arange(0, ROPE, 2, dtype=F32) / ROPE))
    ang = (jnp.arange(Lp, dtype=F32) - FRONT)[:, None] * inv[None, :]
    cs, sn = jnp.cos(ang), jnp.sin(ang)
    return jnp.tile(cs, (1, 4)), jnp.concatenate([-sn, sn, -sn, sn], axis=1)


def _local_step(x, loss_target, meta, norm_g, w_in, gate_w, gate_b, gla_norm_g, gla_proj, q_norm_g, w_uq,
                kv_norm_g, w_ukv, mla_proj, w_out, final_norm_g, early_grads_hook=None, late_weights_hook=None):
    B, seq, _ = x.shape
    Lp = HEAD_ROWS + seq
    T = B * Lp
    tr = _div_tile(Lp, 544, 16)
    tkw = _div_tile(T, Lp, QB)
    tm_sq = _div_tile(T, 1024, QB)

    cuts = np.cumsum((0,) + SPLITS)
    shard_w = IN_WIDTH // 4

    def w_cols(i, width=None):
        parts = []
        for j in range(4):
            a, b = max(cuts[i], j * shard_w), min(cuts[i + 1], (j + 1) * shard_w)
            if a < b:
                parts.append(w_in[j][:, a - j * shard_w:b - j * shard_w])
        if width is not None:
            parts.append(jnp.zeros((D, width - (cuts[i + 1] - cuts[i])), w_in.dtype))
        return parts

    i_q, i_k, i_v, i_lr, i_z, i_cq, i_ckv, i_kr, i_mz, i_gg, i_gm = range(11)
    wA = jnp.concatenate(sum([w_cols(i) for i in (i_v, i_z, i_mz, i_gg, i_gm, i_q, i_k)], []), axis=1)
    wB = jnp.concatenate(w_cols(i_cq) + w_cols(i_ckv) + w_cols(i_lr, 128) + w_cols(i_kr, 128), axis=1)
    gn4 = jnp.tile(gla_norm_g, (1, GLA_H))
    cos_t, sin_t = _rope_tables(Lp)

    u = _rms_in(x, meta, norm_g, B, Lp)
    projA = _mm(u, wA, name="in_proj_a", out_dtype=BF16, tm=tkw, tn=1024, tk=D)
    projB = _mm(u, wB, name="in_proj_b", out_dtype=BF16, tm=tkw, tn=640, tk=D)
    if late_weights_hook is not None:
        gate_w, gla_proj, w_uq, w_ukv, mla_proj, w_out = late_weights_hook(projA)
    wg = jnp.pad(gate_w, ((0, 128 - GLA_RANK), (0, 0)))
    wuq2 = jnp.pad(w_uq.reshape(Q_RANK, MLA_H, MLA_QK), ((0, 0), (0, 0), (0, 256 - MLA_QK))).reshape(Q_RANK, 2048)
    oa, ya_in, ssave = _gla_fwd(projA, projB, wg, gate_b, gn4, B, Lp)
    ya = _mm(ya_in, gla_proj, name="gla_proj", out_dtype=BF16, tm=tm_sq, tn=D, tk=D)
    q_att, k_att, v_att, cqn, ckvn = _mla_prep(projB, cos_t, sin_t, q_norm_g, kv_norm_g, wuq2, w_ukv, B, Lp, tr)
    ob, yb_in, lse_c = _attn_fwd(q_att, k_att, v_att, projA, B, Lp)
    yb = _mm(yb_in, mla_proj, name="mla_proj", out_dtype=BF16, tm=tm_sq, tn=D, tk=D)
    dh1_b, merged, loss, d_gf = _out_proj_loss(x, meta, projA, ya, yb, w_out, final_norm_g.reshape(1, D),
                                                loss_target, B, Lp)

    g_w_out = _mm(merged, dh1_b, name="dw_out", trans_a=True, tm=D, tn=D, tk=tkw)
    dya, dyb, dA = _merge_bwd(dh1_b, w_out, projA, ya, yb, tr)
    g_gla_proj = _mm(ya_in, dya, name="dw_gla_proj", trans_a=True, tm=D, tn=D, tk=tkw)
    g_mla_proj = _mm(yb_in, dyb, name="dw_mla_proj", trans_a=True, tm=D, tn=D, tk=tkw)
    doa, dBz, d_gn = _gla_out_bwd(dya, gla_proj, oa, projA, gn4, tr)
    dC, g_wg, d_bg = _gla_bwd(projA, projB, ssave, doa, wg, gate_b, B, Lp)
    do, dDz, delta_c = _attn_bwd_pre(dyb, mla_proj, projA, ob, B, Lp)
    dq, dk, dv = _attn_bwd(q_att, k_att, v_att, do, lse_c, delta_c, B, Lp)
    dqf, dkvf, dE, d_gq, d_gkv = _mla_bwd_post(dq, dk, dv, projB, cos_t, sin_t, q_norm_g, kv_norm_g,
                                                wuq2, w_ukv, B, Lp, tr)
    g_wuq2 = _mm(cqn, dqf, name="dw_uq", trans_a=True, tm=Q_RANK, tn=2048, tk=tkw)
    g_wukv = _mm(ckvn, dkvf, name="dw_ukv", trans_a=True, tm=KV_RANK, tn=2048, tk=tkw)
    dparts = [dA, dBz, dC, dDz, dE]
    g_in = [_mm(u, dp, name="dw_in_%d" % i, trans_a=True, tm=D, tn=_div_tile(dp.shape[1], 1024, 256), tk=tkw)
            for i, dp in enumerate(dparts)]

    gA, gBz, gC, gDz, gE = g_in
    src = [(gC, 1024), (gC, 1536), (gC, 0), (gC, 2048), (gBz, 0), (gE, 0), (gE, Q_RANK), (gE, 384), (gDz, 0),
           (gA, 0), (gA, D)]
    owners = []
    for j in range(4):
        parts = []
        for i, (arr, off) in enumerate(src):
            a, b = max(cuts[i], j * shard_w), min(cuts[i + 1], (j + 1) * shard_w)
            if a < b:
                parts.append(arr[:, off + a - cuts[i]:off + b - cuts[i]])
        owners.append(jnp.concatenate(parts, axis=1))
    g_w_in = jnp.stack(owners)
    g_wuq = g_wuq2.reshape(Q_RANK, MLA_H, 256)[:, :, :MLA_QK].reshape(Q_RANK, MLA_H * MLA_QK)
    grads = dict(w_in=g_w_in, gla_gate_w=g_wg[:GLA_RANK], gla_proj=g_gla_proj, mla_w_uq=g_wuq, mla_w_ukv=g_wukv,
                 mla_proj=g_mla_proj, w_out=g_w_out, gla_gate_b=d_bg,
                 gla_norm_g=d_gn, mla_q_norm_g=d_gq, mla_kv_norm_g=d_gkv, final_norm_g=d_gf)
    token = None if early_grads_hook is None else early_grads_hook(grads)
    ng = norm_g if token is None else norm_g + token[0:1, 0:1]
    grad_x, d_meta, d_ng = _in_proj_bwd(x, meta, dh1_b, dA, dBz, dC, dDz, dE, wA, wB, ng, B, Lp)
    grads.update(meta_tokens=d_meta, norm_g=d_ng)
    return loss[0, 0], grad_x, grads


_MATS = ("w_in", "gla_gate_w", "gla_proj", "mla_w_uq", "mla_w_ukv", "mla_proj", "w_out")
_ROW_SHARDED = ("gla_proj", "mla_proj", "w_out")
_ORDER = ("meta_tokens", "norm_g", "w_in", "gla_gate_w", "gla_gate_b", "gla_norm_g", "gla_proj", "mla_q_norm_g",
          "mla_w_uq", "mla_kv_norm_g", "mla_w_ukv", "mla_proj", "w_out", "final_norm_g")
WIRE_BF16_MIN_ELEMS = 128 * 128
SMALL_PACK_ROWS = 16


def _pack_small(d, scalar=None):
    rows = [jnp.pad(d[n].reshape(1, size), ((0, 0), (0, D - size))) for n, size in SMALL]
    if scalar is not None:
        rows.append(jnp.pad(scalar.reshape(1, 1), ((0, 0), (0, D - 1))))
    return jnp.pad(jnp.concatenate(rows, axis=0), ((0, SMALL_PACK_ROWS - len(rows)), (0, 0)))


def _unpack_small(packed):
    return {n: packed[i, :size] for i, (n, size) in enumerate(SMALL)}


def kernel(x, meta_tokens, norm_g, w_in, gla_gate_w, gla_gate_b, gla_norm_g, gla_proj, mla_q_norm_g, mla_w_uq, mla_kv_norm_g, mla_w_ukv, mla_proj, w_out, final_norm_g, loss_target, m_meta_tokens, m_norm_g, m_w_in, m_gla_gate_w, m_gla_gate_b, m_gla_norm_g, m_gla_proj, m_mla_q_norm_g, m_mla_w_uq, m_mla_kv_norm_g, m_mla_w_ukv, m_mla_proj, m_w_out, m_final_norm_g, v_meta_tokens, v_norm_g, v_w_in, v_gla_gate_w, v_gla_gate_b, v_gla_norm_g, v_gla_proj, v_mla_q_norm_g, v_mla_w_uq, v_mla_kv_norm_g, v_mla_w_ukv, v_mla_proj, v_w_out, v_final_norm_g):
    w = dict(meta_tokens=meta_tokens, norm_g=norm_g, w_in=w_in[0], gla_gate_w=gla_gate_w[0], gla_gate_b=gla_gate_b,
             gla_norm_g=gla_norm_g, gla_proj=gla_proj[0], mla_q_norm_g=mla_q_norm_g, mla_w_uq=mla_w_uq[0],
             mla_kv_norm_g=mla_kv_norm_g, mla_w_ukv=mla_w_ukv[0], mla_proj=mla_proj[0], w_out=w_out[0],
             final_norm_g=final_norm_g)
    mom = dict(meta_tokens=m_meta_tokens, norm_g=m_norm_g, w_in=m_w_in[0], gla_gate_w=m_gla_gate_w[0],
               gla_gate_b=m_gla_gate_b, gla_norm_g=m_gla_norm_g, gla_proj=m_gla_proj[0], mla_q_norm_g=m_mla_q_norm_g,
               mla_w_uq=m_mla_w_uq[0], mla_kv_norm_g=m_mla_kv_norm_g, mla_w_ukv=m_mla_w_ukv[0], mla_proj=m_mla_proj[0],
               w_out=m_w_out[0], final_norm_g=m_final_norm_g)
    var = dict(meta_tokens=v_meta_tokens, norm_g=v_norm_g, w_in=v_w_in[0], gla_gate_w=v_gla_gate_w[0],
               gla_gate_b=v_gla_gate_b, gla_norm_g=v_gla_norm_g, gla_proj=v_gla_proj[0], mla_q_norm_g=v_mla_q_norm_g,
               mla_w_uq=v_mla_w_uq[0], mla_kv_norm_g=v_mla_kv_norm_g, mla_w_ukv=v_mla_w_ukv[0], mla_proj=v_mla_proj[0],
               w_out=v_w_out[0], final_norm_g=v_final_norm_g)
    out_shapes = {n: a.shape for n, a in zip(_ORDER, (meta_tokens, norm_g, w_in, gla_gate_w, gla_gate_b, gla_norm_g,
                                                     gla_proj, mla_q_norm_g, mla_w_uq, mla_kv_norm_g, mla_w_ukv,
                                                     mla_proj, w_out, final_norm_g))}

    me = (2 * lax.axis_index("x") + lax.axis_index("y")).astype(jnp.int32)
    is_mine = lax.broadcasted_iota(jnp.int32, (4, 1, 1), 0) == me
    with_own = lambda gth, own: jnp.where(is_mine, own[None], gth)
    first = [w["w_in"].astype(BF16), meta_tokens]
    w_in_owner, meta_owner = [with_own(gth, own) for gth, own in zip(_weight_gather(first), first)]
    meta_full = meta_owner.transpose(1, 0, 2).reshape(N_META, D)
    late_names = _MATS[1:]
    late = [w[n].astype(BF16) for n in late_names]
    gather_sems = _late_gather_start(late)

    def late_weights(after):
        lands = _late_gather_wait(gather_sems[0], gather_sems[1], gather_sems[2], gather_sems[3], after)
        full = []
        for name, land, own in zip(late_names, lands, late):
            gth = with_own(land, own)
            if name in _ROW_SHARDED:
                full.append(gth.reshape(4 * gth.shape[1], gth.shape[2]))
            else:
                full.append(gth.transpose(1, 0, 2).reshape(gth.shape[1], 4 * gth.shape[2]))
        return full

    def by_owner(name, arr):
        if name == "w_in":
            return arr
        if name in _ROW_SHARDED:
            return arr.reshape(4, arr.shape[0] // 4, arr.shape[1])
        return arr.reshape(arr.shape[0], 4, arr.shape[1] // 4).transpose(1, 0, 2)

    c_idx = lax.axis_index("c").astype(jnp.int32).reshape(1)
    pos = jnp.stack([c_idx[0], me])
    in_flight = {}

    def start_matrix_reduce(early):
        gps = [by_owner(n, early[n]) for n in _MATS]
        recvs = _pair_swap(gps)
        s1 = [_pair_add_big(gps[0], recvs[0], c_idx)] + list(_pair_add_small(gps[1:], recvs[1:]))
        send_sems, recv_sems, parts, lands, token = _chip_scatter_start(s1)
        in_flight.update(send_sems=send_sems, recv_sems=recv_sems, parts=parts, lands=lands)
        return token

    norm_g_after_start = norm_g + gather_sems[4][0:1, 0:1]
    loss_local, grad_x, g = _local_step(
        x, loss_target, meta_full, norm_g_after_start, w_in_owner, None, gla_gate_b, gla_norm_g, None,
        mla_q_norm_g, None, mla_kv_norm_g, None, None, None, final_norm_g,
        early_grads_hook=start_matrix_reduce, late_weights_hook=late_weights)

    s1, landed = _chip_scatter_wait(in_flight["send_sems"], in_flight["recv_sems"], in_flight["parts"],
                                    in_flight["lands"], after=g["norm_g"])
    halves = [_sum_chips_big(landed[0], s1[0], pos)] + list(_sum_chips_small(landed[1:], s1[1:]))
    g_mats = [j.reshape(out_shapes[n]) for j, n in zip(_pair_join(halves), _MATS)]

    late = [g["meta_tokens"], _pack_small(g, scalar=loss_local)]
    meta_sum, small_sum = _sum_devices_small(_all_to_all_small(late), late)
    loss = small_sum[len(SMALL), 0]
    g_meta = lax.dynamic_slice(meta_sum, (0, me * (D // 4)), (N_META, D // 4))
    names = _MATS + ("meta_tokens",)
    g_red = g_mats + [g_meta, small_sum]

    tens = lambda d: [d[n].reshape(out_shapes[n]) for n in names] + [_pack_small(d)]
    w_t, m_t, v_t = tens(w), tens(mom), tens(var)
    big = _adamw_big(w_t[0], g_red[0], m_t[0], v_t[0])
    rest = _adamw_small(w_t[1:], g_red[1:], m_t[1:], v_t[1:])
    k = len(names)
    results = {"grad": g_red}
    for i, kind in enumerate(("delta", "new_m", "new_v")):
        results[kind] = [big[i]] + list(rest[i * k:(i + 1) * k])

    outs = []
    for kind in ("grad", "delta", "new_m", "new_v"):
        vals = dict(zip(names, results[kind][:-1]))
        vals.update(_unpack_small(results[kind][-1]))
        outs += [vals[n].reshape(out_shapes[n]) for n in _ORDER]
    return (loss, grad_x, *outs)
```

```python
import functools
import math

import jax
import jax.numpy as jnp
import numpy as np
from jax import lax
from jax.experimental import pallas as pl
from jax.experimental.pallas import tpu as pltpu

F32 = jnp.float32
BF16 = jnp.bfloat16

D = 1024
N_META = 16
QB = 256
FRONT = QB - N_META
HEAD_ROWS = FRONT + N_META
assert FRONT % 64 == 48
EPS = 1e-6

GLA_H, GLA_DK, GLA_DV, GLA_RANK, GLA_C = 4, 128, 256, 16, 64
GLA_NORMALIZER = 16.0
GLA_KW, GLA_VW = GLA_H * GLA_DK, GLA_H * GLA_DV
MLA_H, NOPE, ROPE, MLA_DV, Q_RANK, KV_RANK = 8, 128, 64, 128, 256, 128
MLA_QK = NOPE + ROPE
ROPE_BASE = 10000.0
SPLITS = (GLA_KW, GLA_KW, GLA_VW, GLA_RANK, GLA_VW, Q_RANK, KV_RANK, ROPE, MLA_H * MLA_DV, D, D)
IN_WIDTH = sum(SPLITS)

ADAM_LR, ADAM_B1, ADAM_B2, ADAM_EPS, ADAM_WD, ADAM_STEP = 0.001, 0.9, 0.999, 1e-08, 0.01, 10

LANES = 128
VMEM_CAP_V7X = 56 * 1024 * 1024
MESH = pl.DeviceIdType.MESH
NEG = -1e30
LOG2E = math.log2(math.e)

SMALL = (("norm_g", D), ("gla_gate_b", GLA_KW), ("gla_norm_g", GLA_DV), ("mla_q_norm_g", Q_RANK),
         ("mla_kv_norm_g", KV_RANK), ("final_norm_g", D))


def _div_tile(n, target, mult):
    best = None
    for d in range(mult, min(n, target) + 1, mult):
        if n % d == 0:
            best = d
    assert best is not None, (n, target, mult)
    return best


def _params(sem, block_bytes, scratch_bytes=0):
    est = 2 * block_bytes + scratch_bytes + 12 * 1024 * 1024
    return pltpu.CompilerParams(dimension_semantics=sem, vmem_limit_bytes=int(min(max(est, 24 * 1024 * 1024), VMEM_CAP_V7X)))


def _nbytes(shape, dtype):
    return int(np.prod(shape)) * jnp.dtype(dtype).itemsize


def _sigmoid(x):
    return 1.0 / (1.0 + jnp.exp(-x))


def _nt(a, b):
    return lax.dot_general(a, b, (((1,), (1,)), ((), ())), preferred_element_type=F32)


def _tn(a, b):
    return lax.dot_general(a, b, (((0,), (0,)), ((), ())), preferred_element_type=F32)


def _nn(a, b):
    return jnp.dot(a, b, preferred_element_type=F32)


def _split2(x):
    a = x.astype(BF16)
    b = (x - a.astype(F32)).astype(BF16)
    return a, b


def _mm(a, b, *, name, trans_a=False, trans_b=False, out_dtype=F32, tm, tn, tk):
    assert not (trans_a and trans_b)
    if trans_a:
        K, M = a.shape
    else:
        M, K = a.shape
    N = b.shape[0] if trans_b else b.shape[1]
    assert (b.shape[1] if trans_b else b.shape[0]) == K
    assert M % tm == 0 and N % tn == 0 and K % tk == 0, (name, M, N, K, tm, tn, tk)
    nk = K // tk

    def body(a_ref, b_ref, o_ref, *scratch):
        av = a_ref[...].astype(BF16)
        bv = b_ref[...].astype(BF16)
        prod = _tn(av, bv) if trans_a else (_nt(av, bv) if trans_b else _nn(av, bv))
        if nk == 1:
            o_ref[...] = prod.astype(out_dtype)
        else:
            acc = scratch[0]
            k = pl.program_id(2)

            @pl.when(k == 0)
            def _():
                acc[...] = prod

            @pl.when(k > 0)
            def _():
                acc[...] += prod

            @pl.when(k == nk - 1)
            def _():
                o_ref[...] = acc[...].astype(out_dtype)

    if trans_a:
        a_spec = pl.BlockSpec((tk, tm), lambda i, j, k: (k, i))
    else:
        a_spec = pl.BlockSpec((tm, tk), lambda i, j, k: (i, k))
    if trans_b:
        b_spec = pl.BlockSpec((tn, tk), lambda i, j, k: (j, k))
    else:
        b_spec = pl.BlockSpec((tk, tn), lambda i, j, k: (k, j))
    blocks = (_nbytes((tm, tk), a.dtype) + _nbytes((tk, tn), b.dtype) + _nbytes((tm, tn), out_dtype))
    scratch = [pltpu.VMEM((tm, tn), F32)] if nk > 1 else []
    return pl.pallas_call(
        body,
        out_shape=jax.ShapeDtypeStruct((M, N), out_dtype),
        grid=(M // tm, N // tn, nk),
        in_specs=[a_spec, b_spec],
        out_specs=pl.BlockSpec((tm, tn), lambda i, j, k: (i, j)),
        scratch_shapes=scratch,
        compiler_params=_params(("parallel", "parallel", "arbitrary"), blocks + _nbytes((tm, tn), F32),
                                _nbytes((tm, tn), F32) if nk > 1 else 0),
        name=name,
    )(a, b)


def _h_tile(j, x_ref, meta_ref):
    head = jnp.concatenate([jnp.zeros((FRONT, D), F32), meta_ref[...]], axis=0)
    return jnp.where(j > 0, x_ref[0], head)


def _x_spec():
    return pl.BlockSpec((1, QB, D), lambda b, j: (b, jnp.maximum(j - 1, 0), 0))


def _rms_in(x, meta, g, B, Lp):
    T = B * Lp
    NQ = Lp // QB

    def body(x_ref, meta_ref, g_ref, u_ref):
        h = _h_tile(pl.program_id(1), x_ref, meta_ref)
        r = lax.rsqrt(jnp.mean(h * h, axis=-1, keepdims=True) + EPS)
        u_ref[...] = (h * r * g_ref[...]).astype(BF16)

    return pl.pallas_call(
        body,
        out_shape=jax.ShapeDtypeStruct((T, D), BF16),
        grid=(B, NQ),
        in_specs=[_x_spec(), pl.BlockSpec((N_META, D), lambda b, j: (0, 0)), pl.BlockSpec((1, D), lambda b, j: (0, 0))],
        out_specs=pl.BlockSpec((QB, D), lambda b, j: (b * NQ + j, 0)),
        compiler_params=_params(("parallel", "parallel"), _nbytes((QB, D), F32) * 2),
        name="rms_in",
    )(x, meta, g)


def _gla_gate(lr, wg, bg, valid):
    pre = _nn(lr.astype(BF16), wg) + bg
    logsig = jnp.minimum(pre, 0.0) - jnp.log(1.0 + jnp.exp(-jnp.abs(pre)))
    return pre, jnp.where(valid, logsig / GLA_NORMALIZER, 0.0)


def _tri_masks():
    ri = lax.broadcasted_iota(jnp.int32, (GLA_C, GLA_C), 0)
    ci = lax.broadcasted_iota(jnp.int32, (GLA_C, GLA_C), 1)
    return ci <= ri, ci >= ri


def _cumsum_rows(x, ones_mask):
    w = jnp.where(ones_mask, 1.0, 0.0).astype(BF16)
    a, b = _split2(x)
    return _nn(w, a) + _nn(w, b)


def _gla_fwd(projA, projB, wg, bg, gn4, B, Lp):
    T = B * Lp
    NC = Lp // GLA_C
    C = GLA_C
    scale = GLA_DK ** -0.5

    def body(q_ref, k_ref, v_ref, lr_ref, z_ref, wg_ref, bg_ref, gn_ref, oa_ref, ya_ref, ssave_ref, st_ref):
        n = pl.program_id(0)

        @pl.when(n == 0)
        def _():
            st_ref[...] = jnp.zeros_like(st_ref)

        pos = n * C + lax.broadcasted_iota(jnp.int32, (C, 1), 0)
        lower, _ = _tri_masks()
        is_last = lax.broadcasted_iota(jnp.int32, (C, 1), 0) == C - 1
        for b in range(B):
            ssave_ref[b, 0] = st_ref[b]
            _, glog = _gla_gate(lr_ref[b], wg_ref[...], bg_ref[...], pos >= FRONT)
            bcum = _cumsum_rows(glog, lower)
            for h in range(GLA_H):
                ks = slice(h * GLA_DK, (h + 1) * GLA_DK)
                vs = slice(h * GLA_DV, (h + 1) * GLA_DV)
                bh = bcum[:, ks]
                blast = jnp.sum(jnp.where(is_last, bh, 0.0), axis=0, keepdims=True)
                qh = q_ref[b, :, ks].astype(F32) * scale
                kh = k_ref[b, :, ks].astype(F32)
                qe = (qh * jnp.exp(bh)).astype(BF16)
                ke = (kh * jnp.exp(-bh)).astype(BF16)
                kl = (kh * jnp.exp(blast - bh)).astype(BF16)
                vh = v_ref[b, :, vs].astype(BF16)
                a = jnp.where(lower, _nt(qe, ke), 0.0).astype(BF16)
                st = st_ref[b, h]
                o = _nn(a, vh) + _nt(qe, st.astype(BF16))
                st_ref[b, h] = st * jnp.exp(blast) + _tn(vh, kl)
                oa_ref[b, :, vs] = o.astype(BF16)
                on = o * lax.rsqrt(jnp.mean(o * o, axis=-1, keepdims=True) + EPS) * gn_ref[:, vs]
                z = z_ref[b, :, vs].astype(F32)
                ya_ref[b, :, vs] = (on * (z * _sigmoid(z))).astype(BF16)

    blocks = B * (_nbytes((C, 512), F32) * 2 + _nbytes((C, 1024), F32) * 3 + _nbytes((C, 1024), BF16)
                  + _nbytes((GLA_H, GLA_DV, GLA_DK), F32)) + _nbytes((128, 512), BF16)
    state = _nbytes((B, GLA_H, GLA_DV, GLA_DK), F32)
    pa = projA.reshape(B, Lp, projA.shape[1])
    oa, ya, ssave = pl.pallas_call(
        body,
        out_shape=(jax.ShapeDtypeStruct((B, Lp, GLA_VW), BF16), jax.ShapeDtypeStruct((B, Lp, GLA_VW), BF16),
                   jax.ShapeDtypeStruct((B, NC, GLA_H, GLA_DV, GLA_DK), F32)),
        grid=(NC,),
        in_specs=[
            pl.BlockSpec((B, C, 512), lambda n: (0, n, 10)),
            pl.BlockSpec((B, C, 512), lambda n: (0, n, 11)),
            pl.BlockSpec((B, C, 1024), lambda n: (0, n, 0)),
            pl.BlockSpec((B, C, 128), lambda n: (0, n, 3)),
            pl.BlockSpec((B, C, 1024), lambda n: (0, n, 1)),
            pl.BlockSpec((128, 512), lambda n: (0, 0)),
            pl.BlockSpec((1, 512), lambda n: (0, 0)),
            pl.BlockSpec((1, 1024), lambda n: (0, 0)),
        ],
        out_specs=(pl.BlockSpec((B, C, 1024), lambda n: (0, n, 0)),
                   pl.BlockSpec((B, C, 1024), lambda n: (0, n, 0)),
                   pl.BlockSpec((B, 1, GLA_H, GLA_DV, GLA_DK), lambda n: (0, n, 0, 0, 0))),
        scratch_shapes=[pltpu.VMEM((B, GLA_H, GLA_DV, GLA_DK), F32)],
        compiler_params=_params(("arbitrary",), blocks, state),
        name="gla_fwd",
    )(pa, pa, pa, projB.reshape(B, Lp, projB.shape[1]), pa, wg, bg, gn4)
    return oa.reshape(T, GLA_VW), ya.reshape(T, GLA_VW), ssave


def _swap_halves(x):
    lane = lax.broadcasted_iota(jnp.int32, x.shape, 1)
    return jnp.where((lane % 64) < 32, pltpu.roll(x, 96, 1), pltpu.roll(x, 32, 1))


def _mla_prep(projB, cos_t, sin_t, gq, gkv, wuq2, wukv, B, Lp, tr):
    T = B * Lp
    nt = Lp // tr
    HW = 2 * LANES

    def body(pb_ref, cos_ref, sin_ref, gq_ref, gkv_ref, wuq_ref, wukv_ref, q_ref, k_ref, v_ref, cqn_ref, ckvn_ref):
        cq = pb_ref[:, 0:Q_RANK].astype(F32)
        ckv = pb_ref[:, Q_RANK:Q_RANK + KV_RANK].astype(F32)
        kr = pb_ref[:, 512:640].astype(F32)
        cqn = (cq * lax.rsqrt(jnp.mean(cq * cq, axis=-1, keepdims=True) + EPS) * gq_ref[...]).astype(BF16)
        ckvn = (ckv * lax.rsqrt(jnp.mean(ckv * ckv, axis=-1, keepdims=True) + EPS) * gkv_ref[...]).astype(BF16)
        cqn_ref[...] = cqn
        ckvn_ref[...] = ckvn
        qf = _nn(cqn, wuq_ref[...])
        kvf = _nn(ckvn, wukv_ref[...])
        cs = cos_ref[...]
        sn = sin_ref[...]
        rope = lambda t: t * cs + _swap_halves(t) * sn
        kr_r = rope(kr).astype(BF16)
        for h in range(MLA_H):
            q_ref[:, h * HW:h * HW + LANES] = qf[:, h * HW:h * HW + LANES].astype(BF16)
            q_ref[:, h * HW + LANES:(h + 1) * HW] = rope(qf[:, h * HW + LANES:(h + 1) * HW]).astype(BF16)
            k_ref[:, h * HW:h * HW + LANES] = kvf[:, h * HW:h * HW + LANES].astype(BF16)
            k_ref[:, h * HW + LANES:(h + 1) * HW] = kr_r
            v_ref[:, h * MLA_DV:(h + 1) * MLA_DV] = kvf[:, h * HW + LANES:(h + 1) * HW].astype(BF16)

    blocks = (_nbytes((tr, 640), F32) + 2 * _nbytes((tr, 128), F32) + _nbytes((Q_RANK, 2048), BF16)
              + _nbytes((KV_RANK, 2048), BF16) + _nbytes((tr, 2048 * 2 + 1024 + 384), BF16)
              + 2 * _nbytes((tr, 2048), F32))
    return pl.pallas_call(
        body,
        out_shape=(jax.ShapeDtypeStruct((T, MLA_H * HW), BF16), jax.ShapeDtypeStruct((T, MLA_H * HW), BF16),
                   jax.ShapeDtypeStruct((T, MLA_H * MLA_DV), BF16), jax.ShapeDtypeStruct((T, Q_RANK), BF16),
                   jax.ShapeDtypeStruct((T, KV_RANK), BF16)),
        grid=(B, nt),
        in_specs=[
            pl.BlockSpec((tr, 640), lambda b, j: (b * nt + j, 0)),
            pl.BlockSpec((tr, 128), lambda b, j: (j, 0)),
            pl.BlockSpec((tr, 128), lambda b, j: (j, 0)),
            pl.BlockSpec((1, Q_RANK), lambda b, j: (0, 0)),
            pl.BlockSpec((1, KV_RANK), lambda b, j: (0, 0)),
            pl.BlockSpec((Q_RANK, 2048), lambda b, j: (0, 0)),
            pl.BlockSpec((KV_RANK, 2048), lambda b, j: (0, 0)),
        ],
        out_specs=(pl.BlockSpec((tr, 2048), lambda b, j: (b * nt + j, 0)),
                   pl.BlockSpec((tr, 2048), lambda b, j: (b * nt + j, 0)),
                   pl.BlockSpec((tr, 1024), lambda b, j: (b * nt + j, 0)),
                   pl.BlockSpec((tr, Q_RANK), lambda b, j: (b * nt + j, 0)),
                   pl.BlockSpec((tr, KV_RANK), lambda b, j: (b * nt + j, 0))),
        compiler_params=_params(("parallel", "parallel"), blocks),
        name="mla_prep",
    )(projB, cos_t, sin_t, gq, gkv, wuq2, wukv)


def _attn_mask(row, col):
    return (col <= row) & ((col >= FRONT) | (row < FRONT))


def _attn_fwd(q_att, k_att, v_att, projA, B, Lp):
    T = B * Lp
    NQ = Lp // QB
    HW = 2 * LANES
    scale = 1.0 / math.sqrt(MLA_QK)

    def body(q_ref, k_ref, v_ref, mz_ref, o_ref, yb_ref, lsec_ref, m_ref, l_ref, acc_ref):
        qi = pl.program_id(1)
        m_ref[...] = jnp.full(m_ref.shape, NEG, F32)
        l_ref[...] = jnp.zeros_like(l_ref)
        acc_ref[...] = jnp.zeros_like(acc_ref)
        row = qi * QB + lax.broadcasted_iota(jnp.int32, (QB, QB), 0)
        coli = lax.broadcasted_iota(jnp.int32, (QB, QB), 1)

        def step(kj, masked):
            off = pl.multiple_of(kj * QB, QB)
            ok = _attn_mask(row, kj * QB + coli) if masked else None
            for h in range(MLA_H):
                q = q_ref[:, h * HW:(h + 1) * HW]
                kb = k_ref[pl.ds(off, QB), h * HW:(h + 1) * HW]
                vb = v_ref[pl.ds(off, QB), h * MLA_DV:(h + 1) * MLA_DV]
                s = _nt(q, kb) * (scale * LOG2E)
                if masked:
                    s = jnp.where(ok, s, NEG)
                m_old = m_ref[h]
                m_new = jnp.maximum(m_old, jnp.max(s, axis=-1, keepdims=True))
                alpha = jnp.exp2(m_old - m_new)
                p = jnp.exp2(s - jnp.tile(m_new, (1, QB // LANES)))
                m_ref[h] = m_new
                l_ref[h] = alpha * l_ref[h] + jnp.sum(p, axis=-1, keepdims=True)
                acc_ref[h] = alpha * acc_ref[h] + _nn(p.astype(BF16), vb)

        step(0, True)

        def unmasked(kj, carry):
            step(kj, False)
            return carry

        lax.fori_loop(1, qi, unmasked, 0)

        @pl.when(qi > 0)
        def _():
            step(qi, True)

        for h in range(MLA_H):
            hs = slice(h * MLA_DV, (h + 1) * MLA_DV)
            l = l_ref[h]
            o = acc_ref[h] / l
            o_ref[:, hs] = o.astype(BF16)
            z = mz_ref[:, hs].astype(F32)
            yb_ref[:, hs] = (o * (z * _sigmoid(z))).astype(BF16)
            lse2 = m_ref[h] + jnp.log(l) * LOG2E
            lsec_ref[0, h, pl.ds(qi, 1), :] = jnp.transpose(lse2)[0:1, :]

    blocks = (_nbytes((QB, 2048), BF16) + _nbytes((Lp, 2048), BF16) + _nbytes((Lp, 1024), BF16)
              + 2 * _nbytes((QB, 1024), F32) + _nbytes((QB, 1024), BF16) + _nbytes((MLA_H, QB, LANES), F32)
              + _nbytes((MLA_H, NQ, QB), F32))
    return pl.pallas_call(
        body,
        out_shape=(jax.ShapeDtypeStruct((T, MLA_H * MLA_DV), BF16), jax.ShapeDtypeStruct((T, MLA_H * MLA_DV), BF16),
                   jax.ShapeDtypeStruct((B, MLA_H, NQ, QB), F32)),
        grid=(B, NQ),
        in_specs=[
            pl.BlockSpec((QB, MLA_H * HW), lambda b, i: (b * NQ + i, 0)),
            pl.BlockSpec((Lp, MLA_H * HW), lambda b, i: (b, 0)),
            pl.BlockSpec((Lp, MLA_H * MLA_DV), lambda b, i: (b, 0)),
            pl.BlockSpec((QB, 1024), lambda b, i: (b * NQ + i, 2)),
        ],
        out_specs=(pl.BlockSpec((QB, 1024), lambda b, i: (b * NQ + i, 0)),
                   pl.BlockSpec((QB, 1024), lambda b, i: (b * NQ + i, 0)),
                   pl.BlockSpec((1, MLA_H, NQ, QB), lambda b, i: (b, 0, 0, 0))),
        scratch_shapes=[pltpu.VMEM((MLA_H, QB, LANES), F32), pltpu.VMEM((MLA_H, QB, LANES), F32),
                        pltpu.VMEM((MLA_H, QB, MLA_DV), F32)],
        compiler_params=_params(("parallel", "arbitrary"), blocks, 3 * _nbytes((MLA_H, QB, LANES), F32)),
        name="attn_fwd",
    )(q_att, k_att, v_att, projA)


def _out_proj_loss(x, meta, projA, ya, yb, w_out, gf, tgt, B, Lp):
    T = B * Lp
    NQ = Lp // QB

    def body(x_ref, meta_ref, gg_ref, gm_ref, ya_ref, yb_ref, w_ref, gf_ref, t_ref,
             dhb_ref, mg_ref, loss_ref, dgf_ref):
        b = pl.program_id(0)
        j = pl.program_id(1)

        @pl.when((b == 0) & (j == 0))
        def _():
            loss_ref[...] = jnp.zeros_like(loss_ref)
            dgf_ref[...] = jnp.zeros_like(dgf_ref)

        f32 = lambda ref: ref[...].astype(F32)
        merged = (_sigmoid(f32(gg_ref)) * f32(ya_ref) + _sigmoid(f32(gm_ref)) * f32(yb_ref)).astype(BF16)
        mg_ref[...] = merged
        h1 = _h_tile(j, x_ref, meta_ref) + _nn(merged, w_ref[...])
        r = lax.rsqrt(jnp.mean(h1 * h1, axis=-1, keepdims=True) + EPS)
        hn = h1 * r
        gfv = gf_ref[...]
        diff = jnp.where(j > 0, hn * gfv - t_ref[0], 0.0)
        loss_ref[...] += (0.5 / D) * jnp.sum(jnp.sum(diff * diff, axis=-1, keepdims=True), axis=0, keepdims=True)
        dout = diff * (1.0 / D)
        dgf_ref[...] += jnp.sum(dout * hn, axis=0, keepdims=True)
        dhn = dout * gfv
        dh = r * (dhn - hn * jnp.mean(dhn * hn, axis=-1, keepdims=True))
        dhb_ref[...] = dh.astype(BF16)

    rows = lambda c: pl.BlockSpec((QB, D), lambda b, j: (b * NQ + j, c))
    const = lambda s: pl.BlockSpec(s, lambda b, j: (0, 0))
    return pl.pallas_call(
        body,
        out_shape=(jax.ShapeDtypeStruct((T, D), BF16), jax.ShapeDtypeStruct((T, D), BF16),
                   jax.ShapeDtypeStruct((1, 1), F32), jax.ShapeDtypeStruct((1, D), F32)),
        grid=(B, NQ),
        in_specs=[_x_spec(), const((N_META, D)), rows(3), rows(4), rows(0), rows(0), const((D, D)),
                  const((1, D)), _x_spec()],
        out_specs=(rows(0), rows(0), const((1, 1)), const((1, D))),
        compiler_params=_params(("arbitrary", "arbitrary"), 10 * _nbytes((QB, D), F32)),
        name="out_proj_loss",
    )(x, meta, projA, projA, ya, yb, w_out, gf, tgt)


def _merge_bwd(dh1_b, w_out, projA, ya, yb, tr):
    T = dh1_b.shape[0]

    def body(dh_ref, w_ref, gg_ref, gm_ref, ya_ref, yb_ref, dya_ref, dyb_ref, da_ref):
        d = _nt(dh_ref[...], w_ref[...])
        sg = _sigmoid(gg_ref[...].astype(F32))
        sm = _sigmoid(gm_ref[...].astype(F32))
        dya_ref[...] = (d * sg).astype(BF16)
        dyb_ref[...] = (d * sm).astype(BF16)
        da_ref[:, 0:D] = (d * ya_ref[...].astype(F32) * (sg * (1.0 - sg))).astype(BF16)
        da_ref[:, D:2 * D] = (d * yb_ref[...].astype(F32) * (sm * (1.0 - sm))).astype(BF16)

    spec = lambda c: pl.BlockSpec((tr, D), lambda i: (i, c))
    return pl.pallas_call(
        body,
        out_shape=(jax.ShapeDtypeStruct((T, D), BF16), jax.ShapeDtypeStruct((T, D), BF16),
                   jax.ShapeDtypeStruct((T, 2 * D), BF16)),
        grid=(T // tr,),
        in_specs=[spec(0), pl.BlockSpec((D, D), lambda i: (0, 0)), spec(3), spec(4), spec(0), spec(0)],
        out_specs=(spec(0), spec(0), pl.BlockSpec((tr, 2 * D), lambda i: (i, 0))),
        compiler_params=_params(("parallel",), 8 * _nbytes((tr, D), F32)),
        name="merge_bwd",
    )(dh1_b, w_out, projA, projA, ya, yb)


def _gla_out_bwd(dya, gla_proj, oa, projA, gn4, tr):
    T = dya.shape[0]
    nsteps = T // tr

    def body(dya_ref, w_ref, oa_ref, z_ref, gn_ref, do_ref, dz_ref, dgn_ref, acc_ref):
        i = pl.program_id(0)

        @pl.when(i == 0)
        def _():
            acc_ref[...] = jnp.zeros_like(acc_ref)

        dy_all = _nt(dya_ref[...], w_ref[...])
        for h in range(GLA_H):
            vs = slice(h * GLA_DV, (h + 1) * GLA_DV)
            dy = dy_all[:, vs]
            o = oa_ref[:, vs].astype(F32)
            z = z_ref[:, vs].astype(F32)
            gn = gn_ref[:, vs]
            s = _sigmoid(z)
            ra = lax.rsqrt(jnp.mean(o * o, axis=-1, keepdims=True) + EPS)
            on = o * ra
            don = dy * (z * s)
            t = don * gn
            do_ref[:, vs] = (ra * (t - on * jnp.mean(t * on, axis=-1, keepdims=True))).astype(BF16)
            dz_ref[:, vs] = (dy * (on * gn) * (s * (1.0 + z * (1.0 - s)))).astype(BF16)
            acc_ref[:, vs] += jnp.sum(don * on, axis=0, keepdims=True)

        @pl.when(i == nsteps - 1)
        def _():
            a = acc_ref[...]
            dgn_ref[...] = a[:, 0:256] + a[:, 256:512] + a[:, 512:768] + a[:, 768:1024]

    spec = lambda c: pl.BlockSpec((tr, D), lambda i: (i, c))
    return pl.pallas_call(
        body,
        out_shape=(jax.ShapeDtypeStruct((T, D), BF16), jax.ShapeDtypeStruct((T, D), BF16),
                   jax.ShapeDtypeStruct((1, GLA_DV), F32)),
        grid=(nsteps,),
        in_specs=[spec(0), pl.BlockSpec((D, D), lambda i: (0, 0)), spec(0), spec(1),
                  pl.BlockSpec((1, D), lambda i: (0, 0))],
        out_specs=(spec(0), spec(0), pl.BlockSpec((1, GLA_DV), lambda i: (0, 0))),
        scratch_shapes=[pltpu.VMEM((1, D), F32)],
        compiler_params=_params(("arbitrary",), 6 * _nbytes((tr, D), F32)),
        name="gla_out_bwd",
    )(dya, gla_proj, oa, projA, gn4)


def _gla_bwd(projA, projB, ssave, doa, wg, bg, B, Lp):
    T = B * Lp
    NC = Lp // GLA_C
    C = GLA_C
    scale = GLA_DK ** -0.5
    WC = 2304

    def body(q_ref, k_ref, v_ref, lr_ref, ss_ref, do_ref, wg_ref, bg_ref, dc_ref, dwg_ref, dbg_ref, dst_ref):
        i = pl.program_id(0)
        n = NC - 1 - i

        @pl.when(i == 0)
        def _():
            dst_ref[...] = jnp.zeros_like(dst_ref)
            dwg_ref[...] = jnp.zeros_like(dwg_ref)
            dbg_ref[...] = jnp.zeros_like(dbg_ref)

        pos = n * C + lax.broadcasted_iota(jnp.int32, (C, 1), 0)
        valid = pos >= FRONT
        lower, upper = _tri_masks()
        is_last = lax.broadcasted_iota(jnp.int32, (C, 1), 0) == C - 1
        for b in range(B):
            lr = lr_ref[b]
            pre, glog = _gla_gate(lr, wg_ref[...], bg_ref[...], valid)
            bcum = _cumsum_rows(glog, lower)
            db_parts = []
            for h in range(GLA_H):
                ks = slice(h * GLA_DK, (h + 1) * GLA_DK)
                vs = slice(h * GLA_DV, (h + 1) * GLA_DV)
                bh = bcum[:, ks]
                blast = jnp.sum(jnp.where(is_last, bh, 0.0), axis=0, keepdims=True)
                eb, enb, ekl, ebl = jnp.exp(bh), jnp.exp(-bh), jnp.exp(blast - bh), jnp.exp(blast)
                qh = q_ref[b, :, ks].astype(F32) * scale
                kh = k_ref[b, :, ks].astype(F32)
                qe_f, ke_f, kl_f = qh * eb, kh * enb, kh * ekl
                qe, ke, kl = qe_f.astype(BF16), ke_f.astype(BF16), kl_f.astype(BF16)
                vh = v_ref[b, :, vs].astype(BF16)
                doh = do_ref[b, :, vs]
                st = ss_ref[b, 0, h]
                dst = dst_ref[b, h]
                st_b, dst_b = st.astype(BF16), dst.astype(BF16)
                da = jnp.where(lower, _nt(doh, vh), 0.0).astype(BF16)
                da_t = jnp.where(upper, _nt(vh, doh), 0.0).astype(BF16)
                a_t = jnp.where(upper, _nt(ke, qe), 0.0).astype(BF16)
                dqe = _nn(da, ke) + _nn(doh, st_b)
                dke = _nn(da_t, qe)
                dvh = _nn(a_t, doh) + _nt(kl, dst_b)
                dkl = _nn(vh, dst_b)
                dst_ref[b, h] = dst * ebl + _tn(doh, qe)
                deb = jnp.sum(st * dst, axis=0, keepdims=True)
                db = dqe * qe_f - dke * ke_f - dkl * kl_f
                db_last = jnp.sum(dkl * kl_f, axis=0, keepdims=True) + deb * ebl
                db_parts.append(db + jnp.where(is_last, db_last, 0.0))
                dc_ref[b, :, vs] = dvh.astype(BF16)
                dc_ref[b, :, 1024 + h * GLA_DK:1024 + (h + 1) * GLA_DK] = (dqe * eb * scale).astype(BF16)
                dc_ref[b, :, 1536 + h * GLA_DK:1536 + (h + 1) * GLA_DK] = (dke * enb + dkl * ekl).astype(BF16)
            dglog = _cumsum_rows(jnp.concatenate(db_parts, axis=1), upper)
            dpre = jnp.where(valid, dglog * (1.0 / GLA_NORMALIZER) / (1.0 + jnp.exp(pre)), 0.0)
            dpre_b = dpre.astype(BF16)
            dc_ref[b, :, 2048:2176] = _nt(dpre_b, wg_ref[...]).astype(BF16)
            dc_ref[b, :, 2176:2304] = jnp.zeros((C, 128), BF16)
            dwg_ref[...] += _tn(lr.astype(BF16), dpre_b)
            dbg_ref[...] += jnp.sum(dpre, axis=0, keepdims=True)

    blocks = B * (_nbytes((C, 512), F32) * 2 + _nbytes((C, 1024), F32) + _nbytes((C, 1024), BF16)
                  + _nbytes((GLA_H, GLA_DV, GLA_DK), F32) + _nbytes((C, WC), BF16)) + 3 * _nbytes((128, 512), F32)
    state = _nbytes((B, GLA_H, GLA_DV, GLA_DK), F32)
    pa = projA.reshape(B, Lp, projA.shape[1])
    rev = lambda i: NC - 1 - i
    dc, dwg, dbg = pl.pallas_call(
        body,
        out_shape=(jax.ShapeDtypeStruct((B, Lp, WC), BF16), jax.ShapeDtypeStruct((128, GLA_KW), F32),
                   jax.ShapeDtypeStruct((1, GLA_KW), F32)),
        grid=(NC,),
        in_specs=[
            pl.BlockSpec((B, C, 512), lambda i: (0, rev(i), 10)),
            pl.BlockSpec((B, C, 512), lambda i: (0, rev(i), 11)),
            pl.BlockSpec((B, C, 1024), lambda i: (0, rev(i), 0)),
            pl.BlockSpec((B, C, 128), lambda i: (0, rev(i), 3)),
            pl.BlockSpec((B, 1, GLA_H, GLA_DV, GLA_DK), lambda i: (0, rev(i), 0, 0, 0)),
            pl.BlockSpec((B, C, 1024), lambda i: (0, rev(i), 0)),
            pl.BlockSpec((128, 512), lambda i: (0, 0)),
            pl.BlockSpec((1, 512), lambda i: (0, 0)),
        ],
        out_specs=(pl.BlockSpec((B, C, WC), lambda i: (0, rev(i), 0)),
                   pl.BlockSpec((128, GLA_KW), lambda i: (0, 0)),
                   pl.BlockSpec((1, GLA_KW), lambda i: (0, 0))),
        scratch_shapes=[pltpu.VMEM((B, GLA_H, GLA_DV, GLA_DK), F32)],
        compiler_params=_params(("arbitrary",), blocks, state),
        name="gla_bwd",
    )(pa, pa, pa, projB.reshape(B, Lp, projB.shape[1]), ssave, doa.reshape(B, Lp, GLA_VW), wg, bg)
    return dc.reshape(T, WC), dwg, dbg


def _attn_bwd_pre(dyb, mla_proj, projA, ob, B, Lp):
    T = B * Lp
    NQ = Lp // QB

    def body(dyb_ref, w_ref, z_ref, o_ref, do_ref, dz_ref, dcol_ref):
        j = pl.program_id(1)
        dy_all = _nt(dyb_ref[...], w_ref[...])
        for h in range(MLA_H):
            hs = slice(h * MLA_DV, (h + 1) * MLA_DV)
            dy = dy_all[:, hs]
            z = z_ref[:, hs].astype(F32)
            o = o_ref[:, hs].astype(F32)
            s = _sigmoid(z)
            do = dy * (z * s)
            do_ref[:, hs] = do.astype(BF16)
            dz_ref[:, hs] = (dy * o * (s * (1.0 + z * (1.0 - s)))).astype(BF16)
            dl = jnp.broadcast_to(jnp.sum(do * o, axis=-1, keepdims=True), (QB, LANES))
            dcol_ref[0, h, pl.ds(j, 1), :] = jnp.transpose(dl)[0:1, :]

    rows = lambda c: pl.BlockSpec((QB, D), lambda b, j: (b * NQ + j, c))
    return pl.pallas_call(
        body,
        out_shape=(jax.ShapeDtypeStruct((T, D), BF16), jax.ShapeDtypeStruct((T, D), BF16),
                   jax.ShapeDtypeStruct((B, MLA_H, NQ, QB), F32)),
        grid=(B, NQ),
        in_specs=[rows(0), pl.BlockSpec((D, D), lambda b, j: (0, 0)), rows(2), rows(0)],
        out_specs=(rows(0), rows(0), pl.BlockSpec((1, MLA_H, NQ, QB), lambda b, j: (b, 0, 0, 0))),
        compiler_params=_params(("parallel", "arbitrary"), 6 * _nbytes((QB, D), F32)),
        name="attn_bwd_pre",
    )(dyb, mla_proj, projA, ob)


ATTN_BWD_HEADS = 8


def _attn_bwd(q_att, k_att, v_att, do, lse_c, delta_c, B, Lp):
    T = B * Lp
    NQ = Lp // QB
    G = ATTN_BWD_HEADS
    NG = MLA_H // G
    HW = 2 * LANES
    scale = 1.0 / math.sqrt(MLA_QK)

    def body(q_ref, k_ref, v_ref, do_ref, lse_ref, dl_ref, dq_out, dk_out, dv_out, dq_ref, dk_ref, dv_ref):
        kj = pl.program_id(2)
        dk_ref[...] = jnp.zeros_like(dk_ref)
        dv_ref[...] = jnp.zeros_like(dv_ref)
        col = kj * QB + lax.broadcasted_iota(jnp.int32, (QB, QB), 0)
        rowi = lax.broadcasted_iota(jnp.int32, (QB, QB), 1)

        def step(qi, masked, first):
            off = pl.multiple_of(qi * QB, QB)
            ok = _attn_mask(qi * QB + rowi, col) if masked else None
            for h in range(G):
                ws = slice(h * HW, (h + 1) * HW)
                hs = slice(h * MLA_DV, (h + 1) * MLA_DV)
                qb = q_ref[pl.ds(off, QB), ws]
                dob = do_ref[pl.ds(off, QB), hs]
                kb = k_ref[:, ws]
                lse2 = lse_ref[0, h, pl.ds(qi, 1), :]
                delta = dl_ref[0, h, pl.ds(qi, 1), :]
                p_t = jnp.exp2(_nt(kb, qb) * (scale * LOG2E) - lse2)
                if masked:
                    p_t = jnp.where(ok, p_t, 0.0)
                dv_ref[:, hs] += _nn(p_t.astype(BF16), dob)
                ds_t = (p_t * (_nt(v_ref[:, hs], dob) - delta) * scale).astype(BF16)
                dk_ref[:, ws] += _nn(ds_t, qb)
                if first:
                    dq_ref[pl.ds(off, QB), ws] = _tn(ds_t, kb)
                else:
                    dq_ref[pl.ds(off, QB), ws] += _tn(ds_t, kb)

        def sweep(masked, first):
            step(kj, True, first)

            def it(qi, carry):
                step(qi, masked, first)
                return carry
            lax.fori_loop(kj + 1, NQ, it, 0)

        pl.when(kj == 0)(lambda: sweep(True, True))
        pl.when(kj > 0)(lambda: sweep(False, False))
        dk_out[...] = dk_ref[...].astype(BF16)
        dv_out[...] = dv_ref[...].astype(BF16)

        @pl.when(kj == NQ - 1)
        def _():
            dq_out[...] = dq_ref[...].astype(BF16)

    blocks = (2 * _nbytes((Lp, G * HW), BF16) + _nbytes((Lp, G * MLA_DV), BF16) + 2 * _nbytes((QB, G * 384), BF16)
              + 2 * _nbytes((G, NQ, QB), F32))
    scratch = [pltpu.VMEM((Lp, G * HW), F32), pltpu.VMEM((QB, G * HW), F32), pltpu.VMEM((QB, G * MLA_DV), F32)]
    return pl.pallas_call(
        body,
        out_shape=(jax.ShapeDtypeStruct((T, MLA_H * HW), BF16), jax.ShapeDtypeStruct((T, MLA_H * HW), BF16),
                   jax.ShapeDtypeStruct((T, MLA_H * MLA_DV), BF16)),
        scratch_shapes=scratch,
        grid=(B, NG, NQ),
        in_specs=[
            pl.BlockSpec((Lp, G * HW), lambda b, g, j: (b, g), pipeline_mode=pl.Buffered(1)),
            pl.BlockSpec((QB, G * HW), lambda b, g, j: (b * NQ + j, g)),
            pl.BlockSpec((QB, G * MLA_DV), lambda b, g, j: (b * NQ + j, g)),
            pl.BlockSpec((Lp, G * MLA_DV), lambda b, g, j: (b, g), pipeline_mode=pl.Buffered(1)),
            pl.BlockSpec((1, G, NQ, QB), lambda b, g, j: (b, g, 0, 0)),
            pl.BlockSpec((1, G, NQ, QB), lambda b, g, j: (b, g, 0, 0)),
        ],
        out_specs=(pl.BlockSpec((Lp, G * HW), lambda b, g, j: (b, g), pipeline_mode=pl.Buffered(1)),
                   pl.BlockSpec((QB, G * HW), lambda b, g, j: (b * NQ + j, g)),
                   pl.BlockSpec((QB, G * MLA_DV), lambda b, g, j: (b * NQ + j, g))),
        compiler_params=_params(("parallel", "parallel", "arbitrary"), blocks,
                                _nbytes((Lp, G * HW), F32) + _nbytes((QB, G * 384), F32)),
        name="attn_bwd",
    )(q_att, k_att, v_att, do, lse_c, delta_c)


def _mla_bwd_post(dq, dk, dv, projB, cos_t, sin_t, gq, gkv, wuq2, wukv, B, Lp, tr):
    T = B * Lp
    nt = Lp // tr
    HW = 2 * LANES

    def body(dq_ref, dk_ref, dv_ref, pb_ref, cos_ref, sin_ref, gq_ref, gkv_ref, wuq_ref, wukv_ref,
             dqf_ref, dkvf_ref, de_ref, dgq_ref, dgkv_ref):
        first = (pl.program_id(0) == 0) & (pl.program_id(1) == 0)

        @pl.when(first)
        def _():
            dgq_ref[...] = jnp.zeros_like(dgq_ref)
            dgkv_ref[...] = jnp.zeros_like(dgkv_ref)

        cs = cos_ref[...]
        sn = sin_ref[...]
        rope_t = lambda t: t * cs + _swap_halves(t * sn)
        dkr = jnp.zeros((tr, LANES), F32)
        for h in range(MLA_H):
            dqf_ref[:, h * HW:h * HW + LANES] = dq_ref[:, h * HW:h * HW + LANES]
            dq_rope = dq_ref[:, h * HW + LANES:(h + 1) * HW].astype(F32)
            dqf_ref[:, h * HW + LANES:(h + 1) * HW] = rope_t(dq_rope).astype(BF16)
            dkvf_ref[:, h * HW:h * HW + LANES] = dk_ref[:, h * HW:h * HW + LANES]
            dkvf_ref[:, h * HW + LANES:(h + 1) * HW] = dv_ref[:, h * MLA_DV:(h + 1) * MLA_DV]
            dkr = dkr + dk_ref[:, h * HW + LANES:(h + 1) * HW].astype(F32)

        def norm_bwd(x, dn, g):
            r = lax.rsqrt(jnp.mean(x * x, axis=-1, keepdims=True) + EPS)
            xn = x * r
            t = dn * g
            return r * (t - xn * jnp.mean(t * xn, axis=-1, keepdims=True)), jnp.sum(dn * xn, axis=0, keepdims=True)

        cq = pb_ref[:, 0:Q_RANK].astype(F32)
        ckv = pb_ref[:, Q_RANK:Q_RANK + KV_RANK].astype(F32)
        dcq, dgq = norm_bwd(cq, _nt(dqf_ref[...], wuq_ref[...]), gq_ref[...])
        dckv, dgkv = norm_bwd(ckv, _nt(dkvf_ref[...], wukv_ref[...]), gkv_ref[...])
        dgq_ref[...] += dgq
        dgkv_ref[...] += dgkv
        de_ref[:, 0:Q_RANK] = dcq.astype(BF16)
        de_ref[:, Q_RANK:Q_RANK + KV_RANK] = dckv.astype(BF16)
        de_ref[:, 384:512] = rope_t(dkr).astype(BF16)

    rows = lambda w: pl.BlockSpec((tr, w), lambda b, j: (b * nt + j, 0))
    const = lambda s: pl.BlockSpec(s, lambda b, j: (0, 0))
    blocks = (2 * _nbytes((tr, 2048), F32) + _nbytes((tr, 1024), F32) + _nbytes((tr, 640), F32)
              + 2 * _nbytes((tr, 2048), BF16) + _nbytes((2048, 384), BF16) + 2 * _nbytes((tr, 2048), F32))
    return pl.pallas_call(
        body,
        out_shape=(jax.ShapeDtypeStruct((T, 2048), BF16), jax.ShapeDtypeStruct((T, 2048), BF16),
                   jax.ShapeDtypeStruct((T, 512), BF16), jax.ShapeDtypeStruct((1, Q_RANK), F32),
                   jax.ShapeDtypeStruct((1, KV_RANK), F32)),
        grid=(B, nt),
        in_specs=[rows(2048), rows(2048), rows(1024), rows(640),
                  pl.BlockSpec((tr, 128), lambda b, j: (j, 0)), pl.BlockSpec((tr, 128), lambda b, j: (j, 0)),
                  const((1, Q_RANK)), const((1, KV_RANK)), const((Q_RANK, 2048)), const((KV_RANK, 2048))],
        out_specs=(rows(2048), rows(2048), rows(512), const((1, Q_RANK)), const((1, KV_RANK))),
        compiler_params=_params(("arbitrary", "arbitrary"), blocks),
        name="mla_bwd_post",
    )(dq, dk, dv, projB, cos_t, sin_t, gq, gkv, wuq2, wukv)


def _in_proj_bwd(x, meta, dh1, dA, dBz, dC, dDz, dE, wA, wB, g, B, Lp):
    NQ = Lp // QB
    seq = x.shape[1]
    R = 2 if B % 2 == 0 else 1
    M = R * QB

    def body(x_ref, meta_ref, dh_ref, da_ref, db_ref, dc_ref, dd_ref, de_ref, wa_ref, wb_ref, g_ref,
             gx_ref, dmeta_ref, dg_ref):
        b = pl.program_id(0)
        j = pl.program_id(1)

        @pl.when((b == 0) & (j == 0))
        def _():
            dg_ref[...] = jnp.zeros_like(dg_ref)

        flat = lambda ref: ref[...].reshape(M, ref.shape[-1])
        da, dbz, dc, dd, de = flat(da_ref), flat(db_ref), flat(dc_ref), flat(dd_ref), flat(de_ref)
        du = _nt(da, wa_ref[:, 3072:5120])
        du = du + _nt(dbz, wa_ref[:, 1024:2048])
        du = du + _nt(dd, wa_ref[:, 2048:3072])
        du = du + _nt(dc[:, 0:1024], wa_ref[:, 0:1024])
        du = du + _nt(dc[:, 1024:2048], wa_ref[:, 5120:6144])
        du = du + _nt(dc[:, 2048:2176], wb_ref[:, 384:512])
        du = du + _nt(de[:, 0:384], wb_ref[:, 0:384])
        du = du + _nt(de[:, 384:512], wb_ref[:, 512:640])

        head = jnp.concatenate([jnp.zeros((FRONT, D), F32), meta_ref[...]], axis=0)
        x = jnp.concatenate([jnp.where(j > 0, x_ref[i], head) for i in range(R)], axis=0)
        r = lax.rsqrt(jnp.mean(x * x, axis=-1, keepdims=True) + EPS)
        xn = x * r
        t = du * g_ref[...]
        dh0 = flat(dh_ref).astype(F32) + r * (t - xn * jnp.mean(t * xn, axis=-1, keepdims=True))
        dg_ref[...] += jnp.sum(du * xn, axis=0, keepdims=True)
        dmeta = dh0[FRONT:HEAD_ROWS, :]
        for i in range(R):
            gx_ref[i] = dh0[i * QB:(i + 1) * QB, :]
            if i > 0:
                dmeta = dmeta + dh0[i * QB + FRONT:i * QB + HEAD_ROWS, :]

        @pl.when((j == 0) & (b == 0))
        def _():
            dmeta_ref[...] = dmeta

        @pl.when((j == 0) & (b > 0))
        def _():
            dmeta_ref[...] += dmeta

    rows = lambda w: pl.BlockSpec((R, QB, w), lambda b, j: (b, j, 0))
    x_rows = pl.BlockSpec((R, QB, D), lambda b, j: (b, jnp.maximum(j - 1, 0), 0))
    const = lambda s: pl.BlockSpec(s, lambda b, j: (0,) * len(s))
    resident = lambda s: pl.BlockSpec(s, lambda b, j: (0, 0), pipeline_mode=pl.Buffered(1))
    by_row = lambda a: a.reshape(B, Lp, a.shape[1])
    widths = [a.shape[1] for a in (dA, dBz, dC, dDz, dE)]
    blocks = sum(_nbytes((M, w), BF16) for w in widths) + 4 * _nbytes((M, D), F32)
    return pl.pallas_call(
        body,
        out_shape=(jax.ShapeDtypeStruct((B, seq, D), F32), jax.ShapeDtypeStruct((N_META, D), F32),
                   jax.ShapeDtypeStruct((1, D), F32)),
        grid=(B // R, NQ),
        in_specs=[x_rows, const((N_META, D)), rows(D)] + [rows(w) for w in widths]
        + [resident(wA.shape), resident(wB.shape), const((1, D))],
        out_specs=(x_rows, const((N_META, D)), const((1, D))),
        compiler_params=_params(("arbitrary", "arbitrary"), blocks, _nbytes(wA.shape, BF16) + _nbytes(wB.shape, BF16)),
        name="in_proj_bwd",
    )(x, meta, by_row(dh1), *[by_row(a) for a in (dA, dBz, dC, dDz, dE)], wA, wB, g)


_VMEM_WHOLE = pl.BlockSpec(memory_space=pltpu.VMEM)


def _params_whole(arrays):
    total = sum(_nbytes(a.shape, a.dtype) for a in arrays)
    return pltpu.CompilerParams(vmem_limit_bytes=int(min(total + 12 * 1024 * 1024, VMEM_CAP_V7X)))


def _wire_dtype(shape):
    return BF16 if shape[-2] * shape[-1] >= WIRE_BF16_MIN_ELEMS else F32


def _pair_add_big(gp, recv, c):
    _, half, cols = recv.shape
    th = _div_tile(half, 64, 16)
    out_dtype = _wire_dtype(recv.shape)

    steps = half // th

    def body(c_ref, a_ref, b_ref, o_ref):
        o_ref[...] = (a_ref[...].astype(F32) + b_ref[...].astype(F32)).astype(out_dtype)

    return pl.pallas_call(
        body,
        out_shape=jax.ShapeDtypeStruct(recv.shape, out_dtype),
        grid_spec=pltpu.PrefetchScalarGridSpec(
            num_scalar_prefetch=1,
            grid=(steps,),
            in_specs=[pl.BlockSpec((4, th, cols), lambda i, c_ref: (0, c_ref[0] * steps + i, 0)),
                      pl.BlockSpec((4, th, cols), lambda i, c_ref: (0, i, 0))],
            out_specs=pl.BlockSpec((4, th, cols), lambda i, c_ref: (0, i, 0)),
        ),
        compiler_params=_params(("parallel",), 3 * _nbytes((4, th, cols), F32)),
        name="grad_pair_add_big",
    )(c, gp, recv)


def _pair_add_small(gps, recvs):
    n = len(gps)

    def body(*refs):
        c = lax.axis_index("c")
        for t in range(n):
            g_ref, r_ref, o_ref = refs[t], refs[n + t], refs[2 * n + t]
            half = r_ref.shape[1]
            mine = g_ref[:, pl.ds(pl.multiple_of(c * half, 16 if half % 16 == 0 else 8), half), :]
            s = mine.astype(F32) + r_ref[...].astype(F32)
            o_ref[...] = s.astype(o_ref.dtype)

    return pl.pallas_call(
        body,
        out_shape=[jax.ShapeDtypeStruct(r.shape, _wire_dtype(r.shape)) for r in recvs],
        in_specs=[_VMEM_WHOLE] * (2 * n),
        out_specs=[_VMEM_WHOLE] * n,
        compiler_params=_params_whole(list(gps) + 2 * list(recvs)),
        name="grad_pair_add_small",
    )(*gps, *recvs)


def _chip_order_sum(landed_ref, own_ref, me):
    p = [jnp.where(me == k, own_ref[k], landed_ref[k]).astype(F32) for k in range(4)]
    return ((p[0] + p[1]) + p[2]) + p[3]


def _sum_chips_big(landed, own, pos):
    _, half, cols = landed.shape
    th = _div_tile(half, 64, 16)

    def body(pos_ref, l_ref, s_ref, o_ref):
        o_ref[0] = _chip_order_sum(l_ref, s_ref, pos_ref[1])

    spec = pl.BlockSpec((4, th, cols), lambda i, pos_ref: (0, i, 0))
    return pl.pallas_call(
        body,
        out_shape=jax.ShapeDtypeStruct((2, half, cols), F32),
        grid_spec=pltpu.PrefetchScalarGridSpec(
            num_scalar_prefetch=1,
            grid=(half // th,),
            in_specs=[spec, spec],
            out_specs=pl.BlockSpec((1, th, cols), lambda i, pos_ref: (pos_ref[0], i, 0)),
        ),
        compiler_params=_params(("parallel",), 3 * _nbytes((4, th, cols), F32)),
        name="grad_sum_chips_big",
    )(pos, landed, own)


def _sum_chips_small(landed, own):
    n = len(landed)

    def body(*refs):
        x, y, c = _mesh_pos()
        for t in range(n):
            refs[2 * n + t][c] = _chip_order_sum(refs[t], refs[n + t], 2 * x + y)

    return pl.pallas_call(
        body,
        out_shape=[jax.ShapeDtypeStruct((2,) + p.shape[1:], F32) for p in landed],
        in_specs=[_VMEM_WHOLE] * (2 * n),
        out_specs=[_VMEM_WHOLE] * n,
        compiler_params=_params_whole(list(landed) * 3),
        name="grad_sum_chips_small",
    )(*landed, *own)


def _adamw_update(w_ref, g_ref, m_ref, v_ref, d_ref, mo_ref, vo_ref):
    c1 = 1.0 - ADAM_B1 ** ADAM_STEP
    c2 = 1.0 - ADAM_B2 ** ADAM_STEP
    gv = g_ref[...]
    mn = ADAM_B1 * m_ref[...] + (1.0 - ADAM_B1) * gv
    vn = ADAM_B2 * v_ref[...] + (1.0 - ADAM_B2) * (gv * gv)
    mo_ref[...] = mn
    vo_ref[...] = vn
    d_ref[...] = -ADAM_LR * ((mn / c1) / (jnp.sqrt(vn / c2) + ADAM_EPS) + ADAM_WD * w_ref[...])


def _adamw_big(w, g, m, v):
    lead, (rows, cols) = w.shape[:-2], w.shape[-2:]
    assert all(n == 1 for n in lead)
    tr = _div_tile(rows, (1 << 19) // cols, 8)
    spec = pl.BlockSpec((1,) * len(lead) + (tr, cols), lambda i: (0,) * len(lead) + (i, 0))
    shp = jax.ShapeDtypeStruct(w.shape, F32)
    return pl.pallas_call(
        functools.partial(_adamw_update),
        out_shape=(shp, shp, shp),
        grid=(rows // tr,),
        in_specs=[spec] * 4,
        out_specs=(spec, spec, spec),
        compiler_params=_params(("parallel",), 7 * _nbytes((tr, cols), F32)),
        name="adamw_big",
    )(w, g, m, v)


def _adamw_small(ws, gs, ms, vs):
    n = len(ws)

    def body(*refs):
        for t in range(n):
            _adamw_update(refs[t], refs[n + t], refs[2 * n + t], refs[3 * n + t],
                          refs[4 * n + t], refs[5 * n + t], refs[6 * n + t])

    shapes = [jax.ShapeDtypeStruct(w.shape, F32) for w in ws]
    return pl.pallas_call(
        body,
        out_shape=shapes * 3,
        in_specs=[_VMEM_WHOLE] * (4 * n),
        out_specs=[_VMEM_WHOLE] * (3 * n),
        compiler_params=_params_whole(list(ws) * 7),
        name="adamw_small",
    )(*ws, *gs, *ms, *vs)


def _mesh_pos():
    return lax.axis_index("x"), lax.axis_index("y"), lax.axis_index("c")


def _other_chips(x, y):
    return [(1 - x, y), (x, 1 - y), (1 - x, 1 - y)]


_ANY = pl.BlockSpec(memory_space=pl.ANY)


PAIR_SPLIT_MIN_ROWS = 64


def _weight_gather(shards):
    n = len(shards)
    split = [s.shape[0] >= PAIR_SPLIT_MIN_ROWS for s in shards]

    def body(*refs):
        w_refs, o_refs = refs[:n], refs[n:2 * n]
        send_sems, recv_sems = refs[2 * n:]
        x, y, c = _mesh_pos()
        me = 2 * x + y
        chips = _other_chips(x, y)

        def rows_of(t, core):
            rows = shards[t].shape[0]
            if not split[t]:
                return pl.ds(0, rows)
            return pl.ds(pl.multiple_of(core * (rows // 2), 16), rows // 2)

        def landed(t, k, slot, rows, to):
            ref = o_refs[t].at[slot, rows]
            return pltpu.make_async_remote_copy(src_ref=ref, dst_ref=ref, send_sem=send_sems.at[6 * t + k],
                                                recv_sem=recv_sems.at[6 * t + k], device_id=to, device_id_type=MESH)

        sends = []
        for t in range(n):
            mine = rows_of(t, c)
            for k, (px, py) in enumerate(chips):
                cp = pltpu.make_async_remote_copy(src_ref=w_refs[t].at[mine], dst_ref=o_refs[t].at[me, mine],
                                                  send_sem=send_sems.at[6 * t + k], recv_sem=recv_sems.at[6 * t + k],
                                                  device_id=(px, py, c), device_id_type=MESH)
                cp.start()
                sends.append(cp)
        for t in range(n):
            mine = rows_of(t, c)
            for k, (px, py) in enumerate(chips):
                landed(t, k, 2 * px + py, mine, (x, y, c)).wait_recv()
                if split[t]:
                    cp = landed(t, 3 + k, 2 * px + py, mine, (x, y, 1 - c))
                    cp.start()
                    sends.append(cp)
        for t in range(n):
            if split[t]:
                for k, (px, py) in enumerate(chips):
                    landed(t, 3 + k, 2 * px + py, rows_of(t, 1 - c), (x, y, c)).wait_recv()
        for cp in sends:
            cp.wait_send()

    return pl.pallas_call(
        body,
        out_shape=[jax.ShapeDtypeStruct((4,) + s.shape, s.dtype) for s in shards],
        in_specs=[_ANY] * n,
        out_specs=[_ANY] * n,
        scratch_shapes=[pltpu.SemaphoreType.DMA((6 * n,)), pltpu.SemaphoreType.DMA((6 * n,))],
        name="weight_gather",
    )(*shards)


def _pair_swap(gps):
    n = len(gps)

    def body(*refs):
        g_refs, o_refs = refs[:n], refs[n:2 * n]
        send_sems, recv_sems = refs[2 * n:]
        x, y, c = _mesh_pos()
        copies = []
        for t in range(n):
            half = gps[t].shape[1] // 2
            theirs = pl.ds(pl.multiple_of((1 - c) * half, 8), half)
            cp = pltpu.make_async_remote_copy(src_ref=g_refs[t].at[:, theirs], dst_ref=o_refs[t],
                                              send_sem=send_sems.at[t], recv_sem=recv_sems.at[t],
                                              device_id=(x, y, 1 - c), device_id_type=MESH)
            cp.start()
            copies.append(cp)
        for cp in copies:
            cp.wait_send()
            cp.wait_recv()

    return pl.pallas_call(
        body,
        out_shape=[jax.ShapeDtypeStruct((4, g.shape[1] // 2, g.shape[2]), g.dtype) for g in gps],
        in_specs=[_ANY] * n,
        out_specs=[_ANY] * n,
        scratch_shapes=[pltpu.SemaphoreType.DMA((n,)), pltpu.SemaphoreType.DMA((n,))],
        name="grad_pair_swap",
    )(*gps)


_HBM = pl.BlockSpec(memory_space=pltpu.HBM)
_SEM = pl.BlockSpec(memory_space=pltpu.SEMAPHORE)


def _in_hbm(a):
    return pltpu.with_memory_space_constraint(a, pltpu.HBM)


def _chip_scatter_start(parts):
    n = len(parts)

    def body(*refs):
        s_refs, l_refs = refs[:n], refs[n:2 * n]
        send_sems, recv_sems = refs[2 * n], refs[2 * n + 1]
        token = refs[-1]
        x, y, c = _mesh_pos()
        me = 2 * x + y
        for t in range(n):
            for k, (px, py) in enumerate(_other_chips(x, y)):
                pltpu.make_async_remote_copy(src_ref=s_refs[t].at[2 * px + py], dst_ref=l_refs[t].at[me],
                                             send_sem=send_sems.at[3 * t + k], recv_sem=recv_sems.at[3 * t + k],
                                             device_id=(px, py, c), device_id_type=MESH).start()
        token[...] = jnp.zeros_like(token)

    hbm = [pltpu.HBM(p.shape, p.dtype) for p in parts]
    outs = pl.pallas_call(
        body,
        name="grad_scatter_start",
        out_shape=(pltpu.SemaphoreType.DMA((3 * n,)), pltpu.SemaphoreType.DMA((3 * n,)), *hbm, *hbm,
                   jax.ShapeDtypeStruct((8, LANES), F32)),
        in_specs=[_HBM] * (2 * n),
        out_specs=(_SEM, _SEM, *([_HBM] * (2 * n)), pl.BlockSpec(memory_space=pltpu.VMEM)),
        input_output_aliases={i: 2 + i for i in range(2 * n)},
        compiler_params=pltpu.CompilerParams(has_side_effects=pltpu.SideEffectType.DATAFLOW_SIDE_EFFECTING),
    )(*[_in_hbm(p) for p in parts], *[_in_hbm(lax.empty(p.shape, p.dtype)) for p in parts])
    return outs[0], outs[1], list(outs[2:2 + n]), list(outs[2 + n:2 + 2 * n]), outs[-1]


def _chip_scatter_wait(send_sems, recv_sems, parts, lands, after):
    n = len(parts)

    def body(*refs):
        s_refs, l_refs = refs[:n], refs[n:2 * n]
        send_sems, recv_sems = refs[2 * n], refs[2 * n + 1]
        x, y, c = _mesh_pos()
        me = 2 * x + y
        for t in range(n):
            for k, (px, py) in enumerate(_other_chips(x, y)):
                cp = pltpu.make_async_remote_copy(src_ref=s_refs[t].at[2 * px + py], dst_ref=l_refs[t].at[2 * px + py],
                                                  send_sem=send_sems.at[3 * t + k], recv_sem=recv_sems.at[3 * t + k],
                                                  device_id=(x, y, c), device_id_type=MESH)
                cp.wait_send()
                cp.wait_recv()

    hbm = [pltpu.HBM(p.shape, p.dtype) for p in parts]
    outs = pl.pallas_call(
        body,
        name="grad_scatter_wait",
        out_shape=(*hbm, *hbm),
        in_specs=[_HBM] * (2 * n) + [_SEM, _SEM, _ANY],
        out_specs=[_HBM] * (2 * n),
        input_output_aliases={i: i for i in range(2 * n)},
        compiler_params=pltpu.CompilerParams(has_side_effects=pltpu.SideEffectType.DATAFLOW_SIDE_EFFECTING),
    )(*parts, *lands, send_sems, recv_sems, after)
    return list(outs[:n]), list(outs[n:])


def _late_gather_start(shards):
    n = len(shards)

    def body(*refs):
        w_refs, l_refs = refs[:n], refs[n:2 * n]
        send_sems, recv_sems = refs[2 * n], refs[2 * n + 1]
        token = refs[-1]
        x, y, c = _mesh_pos()
        me = 2 * x + y
        for t in range(n):
            for k, (px, py) in enumerate(_other_chips(x, y)):
                pltpu.make_async_remote_copy(src_ref=w_refs[t], dst_ref=l_refs[t].at[me],
                                             send_sem=send_sems.at[3 * t + k], recv_sem=recv_sems.at[3 * t + k],
                                             device_id=(px, py, c), device_id_type=MESH).start()
        token[...] = jnp.zeros_like(token)

    src = [pltpu.HBM(s.shape, s.dtype) for s in shards]
    land = [pltpu.HBM((4,) + s.shape, s.dtype) for s in shards]
    outs = pl.pallas_call(
        body,
        name="late_gather_start",
        out_shape=(pltpu.SemaphoreType.DMA((3 * n,)), pltpu.SemaphoreType.DMA((3 * n,)), *src, *land,
                   jax.ShapeDtypeStruct((8, LANES), F32)),
        in_specs=[_HBM] * (2 * n),
        out_specs=(_SEM, _SEM, *([_HBM] * (2 * n)), pl.BlockSpec(memory_space=pltpu.VMEM)),
        input_output_aliases={i: 2 + i for i in range(2 * n)},
        compiler_params=pltpu.CompilerParams(has_side_effects=pltpu.SideEffectType.DATAFLOW_SIDE_EFFECTING),
    )(*[_in_hbm(s) for s in shards], *[_in_hbm(lax.empty((4,) + s.shape, s.dtype)) for s in shards])
    return outs[0], outs[1], list(outs[2:2 + n]), list(outs[2 + n:2 + 2 * n]), outs[-1]


def _late_gather_wait(send_sems, recv_sems, shards, lands, after):
    n = len(shards)

    def body(*refs):
        w_refs, l_refs = refs[:n], refs[n:2 * n]
        send_sems, recv_sems = refs[2 * n], refs[2 * n + 1]
        x, y, c = _mesh_pos()
        for t in range(n):
            for k, (px, py) in enumerate(_other_chips(x, y)):
                cp = pltpu.make_async_remote_copy(src_ref=w_refs[t], dst_ref=l_refs[t].at[2 * px + py],
                                                  send_sem=send_sems.at[3 * t + k], recv_sem=recv_sems.at[3 * t + k],
                                                  device_id=(x, y, c), device_id_type=MESH)
                cp.wait_send()
                cp.wait_recv()

    src = [pltpu.HBM(s.shape, s.dtype) for s in shards]
    land = [pltpu.HBM(l.shape, l.dtype) for l in lands]
    outs = pl.pallas_call(
        body,
        name="late_gather_wait",
        out_shape=(*src, *land),
        in_specs=[_HBM] * (2 * n) + [_SEM, _SEM, _ANY],
        out_specs=[_HBM] * (2 * n),
        input_output_aliases={i: i for i in range(2 * n)},
        compiler_params=pltpu.CompilerParams(has_side_effects=pltpu.SideEffectType.DATAFLOW_SIDE_EFFECTING),
    )(*shards, *lands, send_sems, recv_sems, after)
    return list(outs[n:])


def _all_to_all_small(parts):
    n = len(parts)

    def body(*refs):
        p_refs, o_refs = refs[:n], refs[n:2 * n]
        send_sems, recv_sems = refs[2 * n:]
        x, y, c = _mesh_pos()
        me = 4 * x + 2 * y + c
        sends = []
        for t in range(n):
            for k in range(1, 8):
                px, py, pc = x ^ (k >> 2), y ^ ((k >> 1) & 1), c ^ (k & 1)
                cp = pltpu.make_async_remote_copy(src_ref=p_refs[t], dst_ref=o_refs[t].at[me],
                                                  send_sem=send_sems.at[7 * t + k - 1], recv_sem=recv_sems.at[7 * t + k - 1],
                                                  device_id=(px, py, pc), device_id_type=MESH)
                cp.start()
                sends.append(cp)
        for t in range(n):
            for k in range(1, 8):
                peer = 4 * (x ^ (k >> 2)) + 2 * (y ^ ((k >> 1) & 1)) + (c ^ (k & 1))
                pltpu.make_async_remote_copy(src_ref=p_refs[t], dst_ref=o_refs[t].at[peer],
                                             send_sem=send_sems.at[7 * t + k - 1], recv_sem=recv_sems.at[7 * t + k - 1],
                                             device_id=(x, y, c), device_id_type=MESH).wait_recv()
        for cp in sends:
            cp.wait_send()

    return pl.pallas_call(
        body,
        out_shape=[jax.ShapeDtypeStruct((8,) + p.shape, p.dtype) for p in parts],
        in_specs=[_ANY] * n,
        out_specs=[_ANY] * n,
        scratch_shapes=[pltpu.SemaphoreType.DMA((7 * n,)), pltpu.SemaphoreType.DMA((7 * n,))],
        name="grad_small_all_to_all",
    )(*parts)


def _sum_devices_small(landed, own):
    n = len(landed)

    def body(*refs):
        x, y, c = _mesh_pos()
        me = 4 * x + 2 * y + c
        for t in range(n):
            acc = jnp.where(me == 0, refs[n + t][...], refs[t][0])
            for d in range(1, 8):
                acc = acc + jnp.where(me == d, refs[n + t][...], refs[t][d])
            refs[2 * n + t][...] = acc

    return pl.pallas_call(
        body,
        out_shape=[jax.ShapeDtypeStruct(p.shape, F32) for p in own],
        in_specs=[_VMEM_WHOLE] * (2 * n),
        out_specs=[_VMEM_WHOLE] * n,
        compiler_params=_params_whole(list(landed) + 2 * list(own)),
        name="grad_sum_devices_small",
    )(*landed, *own)


def _pair_join(fs):
    n = len(fs)

    def body(*refs):
        f_refs, o_refs = refs[:n], refs[n:2 * n]
        send_sems, recv_sems = refs[2 * n:]
        x, y, c = _mesh_pos()
        sends = []
        for t in range(n):
            cp = pltpu.make_async_remote_copy(src_ref=f_refs[t].at[c], dst_ref=o_refs[t].at[c], send_sem=send_sems.at[t],
                                              recv_sem=recv_sems.at[t], device_id=(x, y, 1 - c), device_id_type=MESH)
            cp.start()
            sends.append(cp)
        for t in range(n):
            pltpu.make_async_remote_copy(src_ref=f_refs[t].at[c], dst_ref=o_refs[t].at[1 - c], send_sem=send_sems.at[t],
                                         recv_sem=recv_sems.at[t], device_id=(x, y, c), device_id_type=MESH).wait_recv()
        for cp in sends:
            cp.wait_send()

    return pl.pallas_call(
        body,
        out_shape=[jax.ShapeDtypeStruct(f.shape, f.dtype) for f in fs],
        in_specs=[_ANY] * n,
        out_specs=[_ANY] * n,
        input_output_aliases={t: t for t in range(n)},
        scratch_shapes=[pltpu.SemaphoreType.DMA((n,)), pltpu.SemaphoreType.DMA((n,))],
        name="grad_pair_join",
    )(*fs)


def _rope_tables(Lp):
    inv = 1.0 / (ROPE_BASE ** (jnp.arange(0, ROPE, 2, dtype=F32) / ROPE))
    ang = (jnp.arange(Lp, dtype=F32) - FRONT)[:, None] * inv[None, :]
    cs, sn = jnp.cos(ang), jnp.sin(ang)
    return jnp.tile(cs, (1, 4)), jnp.concatenate([-sn, sn, -sn, sn], axis=1)


def _local_step(x, loss_target, meta, norm_g, w_in, gate_w, gate_b, gla_norm_g, gla_proj, q_norm_g, w_uq,
                kv_norm_g, w_ukv, mla_proj, w_out, final_norm_g, early_grads_hook=None, late_weights_hook=None):
    B, seq, _ = x.shape
    Lp = HEAD_ROWS + seq
    T = B * Lp
    tr = _div_tile(Lp, 544, 16)
    tkw = _div_tile(T, Lp, QB)
    tm_sq = _div_tile(T, 1024, QB)

    cuts = np.cumsum((0,) + SPLITS)
    shard_w = IN_WIDTH // 4

    def w_cols(i, width=None):
        parts = []
        for j in range(4):
            a, b = max(cuts[i], j * shard_w), min(cuts[i + 1], (j + 1) * shard_w)
            if a < b:
                parts.append(w_in[j][:, a - j * shard_w:b - j * shard_w])
        if width is not None:
            parts.append(jnp.zeros((D, width - (cuts[i + 1] - cuts[i])), w_in.dtype))
        return parts

    i_q, i_k, i_v, i_lr, i_z, i_cq, i_ckv, i_kr, i_mz, i_gg, i_gm = range(11)
    wA = jnp.concatenate(sum([w_cols(i) for i in (i_v, i_z, i_mz, i_gg, i_gm, i_q, i_k)], []), axis=1)
    wB = jnp.concatenate(w_cols(i_cq) + w_cols(i_ckv) + w_cols(i_lr, 128) + w_cols(i_kr, 128), axis=1)
    gn4 = jnp.tile(gla_norm_g, (1, GLA_H))
    cos_t, sin_t = _rope_tables(Lp)

    u = _rms_in(x, meta, norm_g, B, Lp)
    projA = _mm(u, wA, name="in_proj_a", out_dtype=BF16, tm=tkw, tn=1024, tk=D)
    projB = _mm(u, wB, name="in_proj_b", out_dtype=BF16, tm=tkw, tn=640, tk=D)
    if late_weights_hook is not None:
        gate_w, gla_proj, w_uq, w_ukv, mla_proj, w_out = late_weights_hook(projA)
    wg = jnp.pad(gate_w, ((0, 128 - GLA_RANK), (0, 0)))
    wuq2 = jnp.pad(w_uq.reshape(Q_RANK, MLA_H, MLA_QK), ((0, 0), (0, 0), (0, 256 - MLA_QK))).reshape(Q_RANK, 2048)
    oa, ya_in, ssave = _gla_fwd(projA, projB, wg, gate_b, gn4, B, Lp)
    ya = _mm(ya_in, gla_proj, name="gla_proj", out_dtype=BF16, tm=tm_sq, tn=D, tk=D)
    q_att, k_att, v_att, cqn, ckvn = _mla_prep(projB, cos_t, sin_t, q_norm_g, kv_norm_g, wuq2, w_ukv, B, Lp, tr)
    ob, yb_in, lse_c = _attn_fwd(q_att, k_att, v_att, projA, B, Lp)
    yb = _mm(yb_in, mla_proj, name="mla_proj", out_dtype=BF16, tm=tm_sq, tn=D, tk=D)
    dh1_b, merged, loss, d_gf = _out_proj_loss(x, meta, projA, ya, yb, w_out, final_norm_g.reshape(1, D),
                                                loss_target, B, Lp)

    g_w_out = _mm(merged, dh1_b, name="dw_out", trans_a=True, out_dtype=BF16, tm=D, tn=D, tk=tkw)
    dya, dyb, dA = _merge_bwd(dh1_b, w_out, projA, ya, yb, tr)
    g_gla_proj = _mm(ya_in, dya, name="dw_gla_proj", trans_a=True, out_dtype=BF16, tm=D, tn=D, tk=tkw)
    g_mla_proj = _mm(yb_in, dyb, name="dw_mla_proj", trans_a=True, out_dtype=BF16, tm=D, tn=D, tk=tkw)
    doa, dBz, d_gn = _gla_out_bwd(dya, gla_proj, oa, projA, gn4, tr)
    dC, g_wg, d_bg = _gla_bwd(projA, projB, ssave, doa, wg, gate_b, B, Lp)
    do, dDz, delta_c = _attn_bwd_pre(dyb, mla_proj, projA, ob, B, Lp)
    dq, dk, dv = _attn_bwd(q_att, k_att, v_att, do, lse_c, delta_c, B, Lp)
    dqf, dkvf, dE, d_gq, d_gkv = _mla_bwd_post(dq, dk, dv, projB, cos_t, sin_t, q_norm_g, kv_norm_g,
                                                wuq2, w_ukv, B, Lp, tr)
    g_wuq2 = _mm(cqn, dqf, name="dw_uq", trans_a=True, out_dtype=BF16, tm=Q_RANK, tn=2048, tk=tkw)
    g_wukv = _mm(ckvn, dkvf, name="dw_ukv", trans_a=True, out_dtype=BF16, tm=KV_RANK, tn=2048, tk=tkw)
    dparts = [dA, dBz, dC, dDz, dE]
    g_in = [_mm(u, dp, name="dw_in_%d" % i, trans_a=True, out_dtype=BF16, tm=D, tn=_div_tile(dp.shape[1], 1024, 256), tk=tkw)
            for i, dp in enumerate(dparts)]

    gA, gBz, gC, gDz, gE = g_in
    src = [(gC, 1024), (gC, 1536), (gC, 0), (gC, 2048), (gBz, 0), (gE, 0), (gE, Q_RANK), (gE, 384), (gDz, 0),
           (gA, 0), (gA, D)]
    owners = []
    for j in range(4):
        parts = []
        for i, (arr, off) in enumerate(src):
            a, b = max(cuts[i], j * shard_w), min(cuts[i + 1], (j + 1) * shard_w)
            if a < b:
                parts.append(arr[:, off + a - cuts[i]:off + b - cuts[i]])
        owners.append(jnp.concatenate(parts, axis=1))
    g_w_in = jnp.stack(owners)
    g_wuq = g_wuq2.reshape(Q_RANK, MLA_H, 256)[:, :, :MLA_QK].reshape(Q_RANK, MLA_H * MLA_QK)
    grads = dict(w_in=g_w_in, gla_gate_w=g_wg[:GLA_RANK], gla_proj=g_gla_proj, mla_w_uq=g_wuq, mla_w_ukv=g_wukv,
                 mla_proj=g_mla_proj, w_out=g_w_out, gla_gate_b=d_bg,
                 gla_norm_g=d_gn, mla_q_norm_g=d_gq, mla_kv_norm_g=d_gkv, final_norm_g=d_gf)
    token = None if early_grads_hook is None else early_grads_hook(grads)
    ng = norm_g if token is None else norm_g + token[0:1, 0:1]
    grad_x, d_meta, d_ng = _in_proj_bwd(x, meta, dh1_b, dA, dBz, dC, dDz, dE, wA, wB, ng, B, Lp)
    grads.update(meta_tokens=d_meta, norm_g=d_ng)
    return loss[0, 0], grad_x, grads


_MATS = ("w_in", "gla_gate_w", "gla_proj", "mla_w_uq", "mla_w_ukv", "mla_proj", "w_out")
_ROW_SHARDED = ("gla_proj", "mla_proj", "w_out")
_ORDER = ("meta_tokens", "norm_g", "w_in", "gla_gate_w", "gla_gate_b", "gla_norm_g", "gla_proj", "mla_q_norm_g",
          "mla_w_uq", "mla_kv_norm_g", "mla_w_ukv", "mla_proj", "w_out", "final_norm_g")
WIRE_BF16_MIN_ELEMS = 128 * 128
SMALL_PACK_ROWS = 16


def _pack_small(d, scalar=None):
    rows = [jnp.pad(d[n].reshape(1, size), ((0, 0), (0, D - size))) for n, size in SMALL]
    if scalar is not None:
        rows.append(jnp.pad(scalar.reshape(1, 1), ((0, 0), (0, D - 1))))
    return jnp.pad(jnp.concatenate(rows, axis=0), ((0, SMALL_PACK_ROWS - len(rows)), (0, 0)))


def _unpack_small(packed):
    return {n: packed[i, :size] for i, (n, size) in enumerate(SMALL)}


def kernel(x, meta_tokens, norm_g, w_in, gla_gate_w, gla_gate_b, gla_norm_g, gla_proj, mla_q_norm_g, mla_w_uq, mla_kv_norm_g, mla_w_ukv, mla_proj, w_out, final_norm_g, loss_target, m_meta_tokens, m_norm_g, m_w_in, m_gla_gate_w, m_gla_gate_b, m_gla_norm_g, m_gla_proj, m_mla_q_norm_g, m_mla_w_uq, m_mla_kv_norm_g, m_mla_w_ukv, m_mla_proj, m_w_out, m_final_norm_g, v_meta_tokens, v_norm_g, v_w_in, v_gla_gate_w, v_gla_gate_b, v_gla_norm_g, v_gla_proj, v_mla_q_norm_g, v_mla_w_uq, v_mla_kv_norm_g, v_mla_w_ukv, v_mla_proj, v_w_out, v_final_norm_g):
    w = dict(meta_tokens=meta_tokens, norm_g=norm_g, w_in=w_in[0], gla_gate_w=gla_gate_w[0], gla_gate_b=gla_gate_b,
             gla_norm_g=gla_norm_g, gla_proj=gla_proj[0], mla_q_norm_g=mla_q_norm_g, mla_w_uq=mla_w_uq[0],
             mla_kv_norm_g=mla_kv_norm_g, mla_w_ukv=mla_w_ukv[0], mla_proj=mla_proj[0], w_out=w_out[0],
             final_norm_g=final_norm_g)
    mom = dict(meta_tokens=m_meta_tokens, norm_g=m_norm_g, w_in=m_w_in[0], gla_gate_w=m_gla_gate_w[0],
               gla_gate_b=m_gla_gate_b, gla_norm_g=m_gla_norm_g, gla_proj=m_gla_proj[0], mla_q_norm_g=m_mla_q_norm_g,
               mla_w_uq=m_mla_w_uq[0], mla_kv_norm_g=m_mla_kv_norm_g, mla_w_ukv=m_mla_w_ukv[0], mla_proj=m_mla_proj[0],
               w_out=m_w_out[0], final_norm_g=m_final_norm_g)
    var = dict(meta_tokens=v_meta_tokens, norm_g=v_norm_g, w_in=v_w_in[0], gla_gate_w=v_gla_gate_w[0],
               gla_gate_b=v_gla_gate_b, gla_norm_g=v_gla_norm_g, gla_proj=v_gla_proj[0], mla_q_norm_g=v_mla_q_norm_g,
               mla_w_uq=v_mla_w_uq[0], mla_kv_norm_g=v_mla_kv_norm_g, mla_w_ukv=v_mla_w_ukv[0], mla_proj=v_mla_proj[0],
               w_out=v_w_out[0], final_norm_g=v_final_norm_g)
    out_shapes = {n: a.shape for n, a in zip(_ORDER, (meta_tokens, norm_g, w_in, gla_gate_w, gla_gate_b, gla_norm_g,
                                                     gla_proj, mla_q_norm_g, mla_w_uq, mla_kv_norm_g, mla_w_ukv,
                                                     mla_proj, w_out, final_norm_g))}

    me = (2 * lax.axis_index("x") + lax.axis_index("y")).astype(jnp.int32)
    is_mine = lax.broadcasted_iota(jnp.int32, (4, 1, 1), 0) == me
    with_own = lambda gth, own: jnp.where(is_mine, own[None], gth)
    first = [w["w_in"].astype(BF16), meta_tokens]
    w_in_owner, meta_owner = [with_own(gth, own) for gth, own in zip(_weight_gather(first), first)]
    meta_full = meta_owner.transpose(1, 0, 2).reshape(N_META, D)
    late_names = _MATS[1:]
    late = [w[n].astype(BF16) for n in late_names]
    gather_sems = _late_gather_start(late)

    def late_weights(after):
        lands = _late_gather_wait(gather_sems[0], gather_sems[1], gather_sems[2], gather_sems[3], after)
        full = []
        for name, land, own in zip(late_names, lands, late):
            gth = with_own(land, own)
            if name in _ROW_SHARDED:
                full.append(gth.reshape(4 * gth.shape[1], gth.shape[2]))
            else:
                full.append(gth.transpose(1, 0, 2).reshape(gth.shape[1], 4 * gth.shape[2]))
        return full

    def by_owner(name, arr):
        if name == "w_in":
            return arr
        if name in _ROW_SHARDED:
            return arr.reshape(4, arr.shape[0] // 4, arr.shape[1])
        return arr.reshape(arr.shape[0], 4, arr.shape[1] // 4).transpose(1, 0, 2)

    c_idx = lax.axis_index("c").astype(jnp.int32).reshape(1)
    pos = jnp.stack([c_idx[0], me])
    in_flight = {}

    def start_matrix_reduce(early):
        gps = [by_owner(n, early[n]) for n in _MATS]
        recvs = _pair_swap(gps)
        s1 = [_pair_add_big(gps[0], recvs[0], c_idx)] + list(_pair_add_small(gps[1:], recvs[1:]))
        send_sems, recv_sems, parts, lands, token = _chip_scatter_start(s1)
        in_flight.update(send_sems=send_sems, recv_sems=recv_sems, parts=parts, lands=lands)
        return token

    norm_g_after_start = norm_g + gather_sems[4][0:1, 0:1]
    loss_local, grad_x, g = _local_step(
        x, loss_target, meta_full, norm_g_after_start, w_in_owner, None, gla_gate_b, gla_norm_g, None,
        mla_q_norm_g, None, mla_kv_norm_g, None, None, None, final_norm_g,
        early_grads_hook=start_matrix_reduce, late_weights_hook=late_weights)

    s1, landed = _chip_scatter_wait(in_flight["send_sems"], in_flight["recv_sems"], in_flight["parts"],
                                    in_flight["lands"], after=g["norm_g"])
    halves = [_sum_chips_big(landed[0], s1[0], pos)] + list(_sum_chips_small(landed[1:], s1[1:]))
    g_mats = [j.reshape(out_shapes[n]) for j, n in zip(_pair_join(halves), _MATS)]

    late = [g["meta_tokens"], _pack_small(g, scalar=loss_local)]
    meta_sum, small_sum = _sum_devices_small(_all_to_all_small(late), late)
    loss = small_sum[len(SMALL), 0]
    g_meta = lax.dynamic_slice(meta_sum, (0, me * (D // 4)), (N_META, D // 4))
    names = _MATS + ("meta_tokens",)
    g_red = g_mats + [g_meta, small_sum]

    tens = lambda d: [d[n].reshape(out_shapes[n]) for n in names] + [_pack_small(d)]
    w_t, m_t, v_t = tens(w), tens(mom), tens(var)
    big = _adamw_big(w_t[0], g_red[0], m_t[0], v_t[0])
    rest = _adamw_small(w_t[1:], g_red[1:], m_t[1:], v_t[1:])
    k = len(names)
    results = {"grad": g_red}
    for i, kind in enumerate(("delta", "new_m", "new_v")):
        results[kind] = [big[i]] + list(rest[i * k:(i + 1) * k])

    outs = []
    for kind in ("grad", "delta", "new_m", "new_v"):
        vals = dict(zip(names, results[kind][:-1]))
        vals.update(_unpack_small(results[kind][-1]))
        outs += [vals[n].reshape(out_shapes[n]) for n in _ORDER]
    return (loss, grad_x, *outs)
```

```python
import functools
import math

import jax
import jax.numpy as jnp
import numpy as np
from jax import lax
from jax.experimental import pallas as pl
from jax.experimental.pallas import tpu as pltpu

F32 = jnp.float32
BF16 = jnp.bfloat16

D = 1024
N_META = 16
QB = 256
FRONT = QB - N_META
HEAD_ROWS = FRONT + N_META
assert FRONT % 64 == 48
EPS = 1e-6

GLA_H, GLA_DK, GLA_DV, GLA_RANK, GLA_C = 4, 128, 256, 16, 64
GLA_NORMALIZER = 16.0
GLA_KW, GLA_VW = GLA_H * GLA_DK, GLA_H * GLA_DV
MLA_H, NOPE, ROPE, MLA_DV, Q_RANK, KV_RANK = 8, 128, 64, 128, 256, 128
MLA_QK = NOPE + ROPE
ROPE_BASE = 10000.0
SPLITS = (GLA_KW, GLA_KW, GLA_VW, GLA_RANK, GLA_VW, Q_RANK, KV_RANK, ROPE, MLA_H * MLA_DV, D, D)
IN_WIDTH = sum(SPLITS)

ADAM_LR, ADAM_B1, ADAM_B2, ADAM_EPS, ADAM_WD, ADAM_STEP = 0.001, 0.9, 0.999, 1e-08, 0.01, 10

LANES = 128
VMEM_CAP_V7X = 56 * 1024 * 1024
MESH = pl.DeviceIdType.MESH
NEG = -1e30
LOG2E = math.log2(math.e)

SMALL = (("norm_g", D), ("gla_gate_b", GLA_KW), ("gla_norm_g", GLA_DV), ("mla_q_norm_g", Q_RANK),
         ("mla_kv_norm_g", KV_RANK), ("final_norm_g", D))


def _div_tile(n, target, mult):
    best = None
    for d in range(mult, min(n, target) + 1, mult):
        if n % d == 0:
            best = d
    assert best is not None, (n, target, mult)
    return best


def _params(sem, block_bytes, scratch_bytes=0):
    est = 2 * block_bytes + scratch_bytes + 12 * 1024 * 1024
    return pltpu.CompilerParams(dimension_semantics=sem, vmem_limit_bytes=int(min(max(est, 24 * 1024 * 1024), VMEM_CAP_V7X)))


def _nbytes(shape, dtype):
    return int(np.prod(shape)) * jnp.dtype(dtype).itemsize


def _sigmoid(x):
    return 1.0 / (1.0 + jnp.exp(-x))


def _nt(a, b):
    return lax.dot_general(a, b, (((1,), (1,)), ((), ())), preferred_element_type=F32)


def _tn(a, b):
    return lax.dot_general(a, b, (((0,), (0,)), ((), ())), preferred_element_type=F32)


def _nn(a, b):
    return jnp.dot(a, b, preferred_element_type=F32)


def _split2(x):
    a = x.astype(BF16)
    b = (x - a.astype(F32)).astype(BF16)
    return a, b


def _mm(a, b, *, name, trans_a=False, trans_b=False, out_dtype=F32, tm, tn, tk):
    assert not (trans_a and trans_b)
    if trans_a:
        K, M = a.shape
    else:
        M, K = a.shape
    N = b.shape[0] if trans_b else b.shape[1]
    assert (b.shape[1] if trans_b else b.shape[0]) == K
    assert M % tm == 0 and N % tn == 0 and K % tk == 0, (name, M, N, K, tm, tn, tk)
    nk = K // tk

    def body(a_ref, b_ref, o_ref, *scratch):
        av = a_ref[...].astype(BF16)
        bv = b_ref[...].astype(BF16)
        prod = _tn(av, bv) if trans_a else (_nt(av, bv) if trans_b else _nn(av, bv))
        if nk == 1:
            o_ref[...] = prod.astype(out_dtype)
        else:
            acc = scratch[0]
            k = pl.program_id(2)

            @pl.when(k == 0)
            def _():
                acc[...] = prod

            @pl.when(k > 0)
            def _():
                acc[...] += prod

            @pl.when(k == nk - 1)
            def _():
                o_ref[...] = acc[...].astype(out_dtype)

    if trans_a:
        a_spec = pl.BlockSpec((tk, tm), lambda i, j, k: (k, i))
    else:
        a_spec = pl.BlockSpec((tm, tk), lambda i, j, k: (i, k))
    if trans_b:
        b_spec = pl.BlockSpec((tn, tk), lambda i, j, k: (j, k))
    else:
        b_spec = pl.BlockSpec((tk, tn), lambda i, j, k: (k, j))
    blocks = (_nbytes((tm, tk), a.dtype) + _nbytes((tk, tn), b.dtype) + _nbytes((tm, tn), out_dtype))
    scratch = [pltpu.VMEM((tm, tn), F32)] if nk > 1 else []
    return pl.pallas_call(
        body,
        out_shape=jax.ShapeDtypeStruct((M, N), out_dtype),
        grid=(M // tm, N // tn, nk),
        in_specs=[a_spec, b_spec],
        out_specs=pl.BlockSpec((tm, tn), lambda i, j, k: (i, j)),
        scratch_shapes=scratch,
        compiler_params=_params(("parallel", "parallel", "arbitrary"), blocks + _nbytes((tm, tn), F32),
                                _nbytes((tm, tn), F32) if nk > 1 else 0),
        name=name,
    )(a, b)


def _h_tile(j, x_ref, meta_ref):
    head = jnp.concatenate([jnp.zeros((FRONT, D), F32), meta_ref[...]], axis=0)
    return jnp.where(j > 0, x_ref[0], head)


def _x_spec():
    return pl.BlockSpec((1, QB, D), lambda b, j: (b, jnp.maximum(j - 1, 0), 0))


def _rms_in(x, meta, g, B, Lp):
    T = B * Lp
    NQ = Lp // QB

    def body(x_ref, meta_ref, g_ref, u_ref):
        h = _h_tile(pl.program_id(1), x_ref, meta_ref)
        r = lax.rsqrt(jnp.mean(h * h, axis=-1, keepdims=True) + EPS)
        u_ref[...] = (h * r * g_ref[...]).astype(BF16)

    return pl.pallas_call(
        body,
        out_shape=jax.ShapeDtypeStruct((T, D), BF16),
        grid=(B, NQ),
        in_specs=[_x_spec(), pl.BlockSpec((N_META, D), lambda b, j: (0, 0)), pl.BlockSpec((1, D), lambda b, j: (0, 0))],
        out_specs=pl.BlockSpec((QB, D), lambda b, j: (b * NQ + j, 0)),
        compiler_params=_params(("parallel", "parallel"), _nbytes((QB, D), F32) * 2),
        name="rms_in",
    )(x, meta, g)


def _gla_gate(lr, wg, bg, valid):
    pre = _nn(lr.astype(BF16), wg) + bg
    logsig = jnp.minimum(pre, 0.0) - jnp.log(1.0 + jnp.exp(-jnp.abs(pre)))
    return pre, jnp.where(valid, logsig / GLA_NORMALIZER, 0.0)


def _tri_masks():
    ri = lax.broadcasted_iota(jnp.int32, (GLA_C, GLA_C), 0)
    ci = lax.broadcasted_iota(jnp.int32, (GLA_C, GLA_C), 1)
    return ci <= ri, ci >= ri


def _cumsum_rows(x, ones_mask):
    w = jnp.where(ones_mask, 1.0, 0.0).astype(BF16)
    a, b = _split2(x)
    return _nn(w, a) + _nn(w, b)


def _gla_fwd(projA, projB, wg, bg, gn4, B, Lp):
    T = B * Lp
    NC = Lp // GLA_C
    C = GLA_C
    scale = GLA_DK ** -0.5

    def body(q_ref, k_ref, v_ref, lr_ref, z_ref, wg_ref, bg_ref, gn_ref, oa_ref, ya_ref, ssave_ref, st_ref):
        n = pl.program_id(0)

        @pl.when(n == 0)
        def _():
            st_ref[...] = jnp.zeros_like(st_ref)

        pos = n * C + lax.broadcasted_iota(jnp.int32, (C, 1), 0)
        lower, _ = _tri_masks()
        is_last = lax.broadcasted_iota(jnp.int32, (C, 1), 0) == C - 1
        for b in range(B):
            ssave_ref[b, 0] = st_ref[b]
            _, glog = _gla_gate(lr_ref[b], wg_ref[...], bg_ref[...], pos >= FRONT)
            bcum = _cumsum_rows(glog, lower)
            for h in range(GLA_H):
                ks = slice(h * GLA_DK, (h + 1) * GLA_DK)
                vs = slice(h * GLA_DV, (h + 1) * GLA_DV)
                bh = bcum[:, ks]
                blast = jnp.sum(jnp.where(is_last, bh, 0.0), axis=0, keepdims=True)
                qh = q_ref[b, :, ks].astype(F32) * scale
                kh = k_ref[b, :, ks].astype(F32)
                qe = (qh * jnp.exp(bh)).astype(BF16)
                ke = (kh * jnp.exp(-bh)).astype(BF16)
                kl = (kh * jnp.exp(blast - bh)).astype(BF16)
                vh = v_ref[b, :, vs].astype(BF16)
                a = jnp.where(lower, _nt(qe, ke), 0.0).astype(BF16)
                st = st_ref[b, h]
                o = _nn(a, vh) + _nt(qe, st.astype(BF16))
                st_ref[b, h] = st * jnp.exp(blast) + _tn(vh, kl)
                oa_ref[b, :, vs] = o.astype(BF16)
                on = o * lax.rsqrt(jnp.mean(o * o, axis=-1, keepdims=True) + EPS) * gn_ref[:, vs]
                z = z_ref[b, :, vs].astype(F32)
                ya_ref[b, :, vs] = (on * (z * _sigmoid(z))).astype(BF16)

    blocks = B * (_nbytes((C, 512), F32) * 2 + _nbytes((C, 1024), F32) * 3 + _nbytes((C, 1024), BF16)
                  + _nbytes((GLA_H, GLA_DV, GLA_DK), F32)) + _nbytes((128, 512), BF16)
    state = _nbytes((B, GLA_H, GLA_DV, GLA_DK), F32)
    pa = projA.reshape(B, Lp, projA.shape[1])
    oa, ya, ssave = pl.pallas_call(
        body,
        out_shape=(jax.ShapeDtypeStruct((B, Lp, GLA_VW), BF16), jax.ShapeDtypeStruct((B, Lp, GLA_VW), BF16),
                   jax.ShapeDtypeStruct((B, NC, GLA_H, GLA_DV, GLA_DK), F32)),
        grid=(NC,),
        in_specs=[
            pl.BlockSpec((B, C, 512), lambda n: (0, n, 10)),
            pl.BlockSpec((B, C, 512), lambda n: (0, n, 11)),
            pl.BlockSpec((B, C, 1024), lambda n: (0, n, 0)),
            pl.BlockSpec((B, C, 128), lambda n: (0, n, 3)),
            pl.BlockSpec((B, C, 1024), lambda n: (0, n, 1)),
            pl.BlockSpec((128, 512), lambda n: (0, 0)),
            pl.BlockSpec((1, 512), lambda n: (0, 0)),
            pl.BlockSpec((1, 1024), lambda n: (0, 0)),
        ],
        out_specs=(pl.BlockSpec((B, C, 1024), lambda n: (0, n, 0)),
                   pl.BlockSpec((B, C, 1024), lambda n: (0, n, 0)),
                   pl.BlockSpec((B, 1, GLA_H, GLA_DV, GLA_DK), lambda n: (0, n, 0, 0, 0))),
        scratch_shapes=[pltpu.VMEM((B, GLA_H, GLA_DV, GLA_DK), F32)],
        compiler_params=_params(("arbitrary",), blocks, state),
        name="gla_fwd",
    )(pa, pa, pa, projB.reshape(B, Lp, projB.shape[1]), pa, wg, bg, gn4)
    return oa.reshape(T, GLA_VW), ya.reshape(T, GLA_VW), ssave


def _swap_halves(x):
    lane = lax.broadcasted_iota(jnp.int32, x.shape, 1)
    return jnp.where((lane % 64) < 32, pltpu.roll(x, 96, 1), pltpu.roll(x, 32, 1))


def _mla_prep(projB, cos_t, sin_t, gq, gkv, wuq2, wukv, B, Lp, tr):
    T = B * Lp
    nt = Lp // tr
    HW = 2 * LANES

    def body(pb_ref, cos_ref, sin_ref, gq_ref, gkv_ref, wuq_ref, wukv_ref, q_ref, k_ref, v_ref, cqn_ref, ckvn_ref):
        cq = pb_ref[:, 0:Q_RANK].astype(F32)
        ckv = pb_ref[:, Q_RANK:Q_RANK + KV_RANK].astype(F32)
        kr = pb_ref[:, 512:640].astype(F32)
        cqn = (cq * lax.rsqrt(jnp.mean(cq * cq, axis=-1, keepdims=True) + EPS) * gq_ref[...]).astype(BF16)
        ckvn = (ckv * lax.rsqrt(jnp.mean(ckv * ckv, axis=-1, keepdims=True) + EPS) * gkv_ref[...]).astype(BF16)
        cqn_ref[...] = cqn
        ckvn_ref[...] = ckvn
        qf = _nn(cqn, wuq_ref[...])
        kvf = _nn(ckvn, wukv_ref[...])
        cs = cos_ref[...]
        sn = sin_ref[...]
        rope = lambda t: t * cs + _swap_halves(t) * sn
        kr_r = rope(kr).astype(BF16)
        for h in range(MLA_H):
            q_ref[:, h * HW:h * HW + LANES] = qf[:, h * HW:h * HW + LANES].astype(BF16)
            q_ref[:, h * HW + LANES:(h + 1) * HW] = rope(qf[:, h * HW + LANES:(h + 1) * HW]).astype(BF16)
            k_ref[:, h * HW:h * HW + LANES] = kvf[:, h * HW:h * HW + LANES].astype(BF16)
            k_ref[:, h * HW + LANES:(h + 1) * HW] = kr_r
            v_ref[:, h * MLA_DV:(h + 1) * MLA_DV] = kvf[:, h * HW + LANES:(h + 1) * HW].astype(BF16)

    blocks = (_nbytes((tr, 640), F32) + 2 * _nbytes((tr, 128), F32) + _nbytes((Q_RANK, 2048), BF16)
              + _nbytes((KV_RANK, 2048), BF16) + _nbytes((tr, 2048 * 2 + 1024 + 384), BF16)
              + 2 * _nbytes((tr, 2048), F32))
    return pl.pallas_call(
        body,
        out_shape=(jax.ShapeDtypeStruct((T, MLA_H * HW), BF16), jax.ShapeDtypeStruct((T, MLA_H * HW), BF16),
                   jax.ShapeDtypeStruct((T, MLA_H * MLA_DV), BF16), jax.ShapeDtypeStruct((T, Q_RANK), BF16),
                   jax.ShapeDtypeStruct((T, KV_RANK), BF16)),
        grid=(B, nt),
        in_specs=[
            pl.BlockSpec((tr, 640), lambda b, j: (b * nt + j, 0)),
            pl.BlockSpec((tr, 128), lambda b, j: (j, 0)),
            pl.BlockSpec((tr, 128), lambda b, j: (j, 0)),
            pl.BlockSpec((1, Q_RANK), lambda b, j: (0, 0)),
            pl.BlockSpec((1, KV_RANK), lambda b, j: (0, 0)),
            pl.BlockSpec((Q_RANK, 2048), lambda b, j: (0, 0)),
            pl.BlockSpec((KV_RANK, 2048), lambda b, j: (0, 0)),
        ],
        out_specs=(pl.BlockSpec((tr, 2048), lambda b, j: (b * nt + j, 0)),
                   pl.BlockSpec((tr, 2048), lambda b, j: (b * nt + j, 0)),
                   pl.BlockSpec((tr, 1024), lambda b, j: (b * nt + j, 0)),
                   pl.BlockSpec((tr, Q_RANK), lambda b, j: (b * nt + j, 0)),
                   pl.BlockSpec((tr, KV_RANK), lambda b, j: (b * nt + j, 0))),
        compiler_params=_params(("parallel", "parallel"), blocks),
        name="mla_prep",
    )(projB, cos_t, sin_t, gq, gkv, wuq2, wukv)


def _attn_mask(row, col):
    return (col <= row) & ((col >= FRONT) | (row < FRONT))


def _attn_fwd(q_att, k_att, v_att, projA, B, Lp):
    T = B * Lp
    NQ = Lp // QB
    HW = 2 * LANES
    scale = 1.0 / math.sqrt(MLA_QK)

    def body(q_ref, k_ref, v_ref, mz_ref, o_ref, yb_ref, lsec_ref, m_ref, l_ref, acc_ref):
        qi = pl.program_id(1)
        m_ref[...] = jnp.full(m_ref.shape, NEG, F32)
        l_ref[...] = jnp.zeros_like(l_ref)
        acc_ref[...] = jnp.zeros_like(acc_ref)
        row = qi * QB + lax.broadcasted_iota(jnp.int32, (QB, QB), 0)
        coli = lax.broadcasted_iota(jnp.int32, (QB, QB), 1)

        def step(kj, masked):
            off = pl.multiple_of(kj * QB, QB)
            ok = _attn_mask(row, kj * QB + coli) if masked else None
            for h in range(MLA_H):
                q = q_ref[:, h * HW:(h + 1) * HW]
                kb = k_ref[pl.ds(off, QB), h * HW:(h + 1) * HW]
                vb = v_ref[pl.ds(off, QB), h * MLA_DV:(h + 1) * MLA_DV]
                s = _nt(q, kb) * (scale * LOG2E)
                if masked:
                    s = jnp.where(ok, s, NEG)
                m_old = m_ref[h]
                m_new = jnp.maximum(m_old, jnp.max(s, axis=-1, keepdims=True))
                alpha = jnp.exp2(m_old - m_new)
                p = jnp.exp2(s - jnp.tile(m_new, (1, QB // LANES)))
                m_ref[h] = m_new
                l_ref[h] = alpha * l_ref[h] + jnp.sum(p, axis=-1, keepdims=True)
                acc_ref[h] = alpha * acc_ref[h] + _nn(p.astype(BF16), vb)

        step(0, True)

        def unmasked(kj, carry):
            step(kj, False)
            return carry

        lax.fori_loop(1, qi, unmasked, 0)

        @pl.when(qi > 0)
        def _():
            step(qi, True)

        for h in range(MLA_H):
            hs = slice(h * MLA_DV, (h + 1) * MLA_DV)
            l = l_ref[h]
            o = acc_ref[h] / l
            o_ref[:, hs] = o.astype(BF16)
            z = mz_ref[:, hs].astype(F32)
            yb_ref[:, hs] = (o * (z * _sigmoid(z))).astype(BF16)
            lse2 = m_ref[h] + jnp.log(l) * LOG2E
            lsec_ref[0, h, pl.ds(qi, 1), :] = jnp.transpose(lse2)[0:1, :]

    blocks = (_nbytes((QB, 2048), BF16) + _nbytes((Lp, 2048), BF16) + _nbytes((Lp, 1024), BF16)
              + 2 * _nbytes((QB, 1024), F32) + _nbytes((QB, 1024), BF16) + _nbytes((MLA_H, QB, LANES), F32)
              + _nbytes((MLA_H, NQ, QB), F32))
    return pl.pallas_call(
        body,
        out_shape=(jax.ShapeDtypeStruct((T, MLA_H * MLA_DV), BF16), jax.ShapeDtypeStruct((T, MLA_H * MLA_DV), BF16),
                   jax.ShapeDtypeStruct((B, MLA_H, NQ, QB), F32)),
        grid=(B, NQ),
        in_specs=[
            pl.BlockSpec((QB, MLA_H * HW), lambda b, i: (b * NQ + i, 0)),
            pl.BlockSpec((Lp, MLA_H * HW), lambda b, i: (b, 0)),
            pl.BlockSpec((Lp, MLA_H * MLA_DV), lambda b, i: (b, 0)),
            pl.BlockSpec((QB, 1024), lambda b, i: (b * NQ + i, 2)),
        ],
        out_specs=(pl.BlockSpec((QB, 1024), lambda b, i: (b * NQ + i, 0)),
                   pl.BlockSpec((QB, 1024), lambda b, i: (b * NQ + i, 0)),
                   pl.BlockSpec((1, MLA_H, NQ, QB), lambda b, i: (b, 0, 0, 0))),
        scratch_shapes=[pltpu.VMEM((MLA_H, QB, LANES), F32), pltpu.VMEM((MLA_H, QB, LANES), F32),
                        pltpu.VMEM((MLA_H, QB, MLA_DV), F32)],
        compiler_params=_params(("parallel", "arbitrary"), blocks, 3 * _nbytes((MLA_H, QB, LANES), F32)),
        name="attn_fwd",
    )(q_att, k_att, v_att, projA)


def _out_proj_loss(x, meta, projA, ya, yb, w_out, gf, tgt, B, Lp):
    T = B * Lp
    NQ = Lp // QB

    def body(x_ref, meta_ref, gg_ref, gm_ref, ya_ref, yb_ref, w_ref, gf_ref, t_ref,
             dhb_ref, mg_ref, loss_ref, dgf_ref):
        b = pl.program_id(0)
        j = pl.program_id(1)

        @pl.when((b == 0) & (j == 0))
        def _():
            loss_ref[...] = jnp.zeros_like(loss_ref)
            dgf_ref[...] = jnp.zeros_like(dgf_ref)

        f32 = lambda ref: ref[...].astype(F32)
        merged = (_sigmoid(f32(gg_ref)) * f32(ya_ref) + _sigmoid(f32(gm_ref)) * f32(yb_ref)).astype(BF16)
        mg_ref[...] = merged
        h1 = _h_tile(j, x_ref, meta_ref) + _nn(merged, w_ref[...])
        r = lax.rsqrt(jnp.mean(h1 * h1, axis=-1, keepdims=True) + EPS)
        hn = h1 * r
        gfv = gf_ref[...]
        diff = jnp.where(j > 0, hn * gfv - t_ref[0], 0.0)
        loss_ref[...] += (0.5 / D) * jnp.sum(jnp.sum(diff * diff, axis=-1, keepdims=True), axis=0, keepdims=True)
        dout = diff * (1.0 / D)
        dgf_ref[...] += jnp.sum(dout * hn, axis=0, keepdims=True)
        dhn = dout * gfv
        dh = r * (dhn - hn * jnp.mean(dhn * hn, axis=-1, keepdims=True))
        dhb_ref[...] = dh.astype(BF16)

    rows = lambda c: pl.BlockSpec((QB, D), lambda b, j: (b * NQ + j, c))
    const = lambda s: pl.BlockSpec(s, lambda b, j: (0, 0))
    return pl.pallas_call(
        body,
        out_shape=(jax.ShapeDtypeStruct((T, D), BF16), jax.ShapeDtypeStruct((T, D), BF16),
                   jax.ShapeDtypeStruct((1, 1), F32), jax.ShapeDtypeStruct((1, D), F32)),
        grid=(B, NQ),
        in_specs=[_x_spec(), const((N_META, D)), rows(3), rows(4), rows(0), rows(0), const((D, D)),
                  const((1, D)), _x_spec()],
        out_specs=(rows(0), rows(0), const((1, 1)), const((1, D))),
        compiler_params=_params(("arbitrary", "arbitrary"), 10 * _nbytes((QB, D), F32)),
        name="out_proj_loss",
    )(x, meta, projA, projA, ya, yb, w_out, gf, tgt)


def _merge_bwd(dh1_b, w_out, projA, ya, yb, tr):
    T = dh1_b.shape[0]

    def body(dh_ref, w_ref, gg_ref, gm_ref, ya_ref, yb_ref, dya_ref, dyb_ref, da_ref):
        d = _nt(dh_ref[...], w_ref[...])
        sg = _sigmoid(gg_ref[...].astype(F32))
        sm = _sigmoid(gm_ref[...].astype(F32))
        dya_ref[...] = (d * sg).astype(BF16)
        dyb_ref[...] = (d * sm).astype(BF16)
        da_ref[:, 0:D] = (d * ya_ref[...].astype(F32) * (sg * (1.0 - sg))).astype(BF16)
        da_ref[:, D:2 * D] = (d * yb_ref[...].astype(F32) * (sm * (1.0 - sm))).astype(BF16)

    spec = lambda c: pl.BlockSpec((tr, D), lambda i: (i, c))
    return pl.pallas_call(
        body,
        out_shape=(jax.ShapeDtypeStruct((T, D), BF16), jax.ShapeDtypeStruct((T, D), BF16),
                   jax.ShapeDtypeStruct((T, 2 * D), BF16)),
        grid=(T // tr,),
        in_specs=[spec(0), pl.BlockSpec((D, D), lambda i: (0, 0)), spec(3), spec(4), spec(0), spec(0)],
        out_specs=(spec(0), spec(0), pl.BlockSpec((tr, 2 * D), lambda i: (i, 0))),
        compiler_params=_params(("parallel",), 8 * _nbytes((tr, D), F32)),
        name="merge_bwd",
    )(dh1_b, w_out, projA, projA, ya, yb)


def _gla_out_bwd(dya, gla_proj, oa, projA, gn4, tr):
    T = dya.shape[0]
    nsteps = T // tr

    def body(dya_ref, w_ref, oa_ref, z_ref, gn_ref, do_ref, dz_ref, dgn_ref, acc_ref):
        i = pl.program_id(0)

        @pl.when(i == 0)
        def _():
            acc_ref[...] = jnp.zeros_like(acc_ref)

        dy_all = _nt(dya_ref[...], w_ref[...])
        for h in range(GLA_H):
            vs = slice(h * GLA_DV, (h + 1) * GLA_DV)
            dy = dy_all[:, vs]
            o = oa_ref[:, vs].astype(F32)
            z = z_ref[:, vs].astype(F32)
            gn = gn_ref[:, vs]
            s = _sigmoid(z)
            ra = lax.rsqrt(jnp.mean(o * o, axis=-1, keepdims=True) + EPS)
            on = o * ra
            don = dy * (z * s)
            t = don * gn
            do_ref[:, vs] = (ra * (t - on * jnp.mean(t * on, axis=-1, keepdims=True))).astype(BF16)
            dz_ref[:, vs] = (dy * (on * gn) * (s * (1.0 + z * (1.0 - s)))).astype(BF16)
            acc_ref[:, vs] += jnp.sum(don * on, axis=0, keepdims=True)

        @pl.when(i == nsteps - 1)
        def _():
            a = acc_ref[...]
            dgn_ref[...] = a[:, 0:256] + a[:, 256:512] + a[:, 512:768] + a[:, 768:1024]

    spec = lambda c: pl.BlockSpec((tr, D), lambda i: (i, c))
    return pl.pallas_call(
        body,
        out_shape=(jax.ShapeDtypeStruct((T, D), BF16), jax.ShapeDtypeStruct((T, D), BF16),
                   jax.ShapeDtypeStruct((1, GLA_DV), F32)),
        grid=(nsteps,),
        in_specs=[spec(0), pl.BlockSpec((D, D), lambda i: (0, 0)), spec(0), spec(1),
                  pl.BlockSpec((1, D), lambda i: (0, 0))],
        out_specs=(spec(0), spec(0), pl.BlockSpec((1, GLA_DV), lambda i: (0, 0))),
        scratch_shapes=[pltpu.VMEM((1, D), F32)],
        compiler_params=_params(("arbitrary",), 6 * _nbytes((tr, D), F32)),
        name="gla_out_bwd",
    )(dya, gla_proj, oa, projA, gn4)


def _gla_bwd(projA, projB, ssave, doa, wg, bg, B, Lp):
    T = B * Lp
    NC = Lp // GLA_C
    C = GLA_C
    scale = GLA_DK ** -0.5
    WC = 2304

    def body(q_ref, k_ref, v_ref, lr_ref, ss_ref, do_ref, wg_ref, bg_ref, dc_ref, dwg_ref, dbg_ref, dst_ref):
        i = pl.program_id(0)
        n = NC - 1 - i

        @pl.when(i == 0)
        def _():
            dst_ref[...] = jnp.zeros_like(dst_ref)
            dwg_ref[...] = jnp.zeros_like(dwg_ref)
            dbg_ref[...] = jnp.zeros_like(dbg_ref)

        pos = n * C + lax.broadcasted_iota(jnp.int32, (C, 1), 0)
        valid = pos >= FRONT
        lower, upper = _tri_masks()
        is_last = lax.broadcasted_iota(jnp.int32, (C, 1), 0) == C - 1
        for b in range(B):
            lr = lr_ref[b]
            pre, glog = _gla_gate(lr, wg_ref[...], bg_ref[...], valid)
            bcum = _cumsum_rows(glog, lower)
            db_parts = []
            for h in range(GLA_H):
                ks = slice(h * GLA_DK, (h + 1) * GLA_DK)
                vs = slice(h * GLA_DV, (h + 1) * GLA_DV)
                bh = bcum[:, ks]
                blast = jnp.sum(jnp.where(is_last, bh, 0.0), axis=0, keepdims=True)
                eb, enb, ekl, ebl = jnp.exp(bh), jnp.exp(-bh), jnp.exp(blast - bh), jnp.exp(blast)
                qh = q_ref[b, :, ks].astype(F32) * scale
                kh = k_ref[b, :, ks].astype(F32)
                qe_f, ke_f, kl_f = qh * eb, kh * enb, kh * ekl
                qe, ke, kl = qe_f.astype(BF16), ke_f.astype(BF16), kl_f.astype(BF16)
                vh = v_ref[b, :, vs].astype(BF16)
                doh = do_ref[b, :, vs]
                st = ss_ref[b, 0, h]
                dst = dst_ref[b, h]
                st_b, dst_b = st.astype(BF16), dst.astype(BF16)
                da = jnp.where(lower, _nt(doh, vh), 0.0).astype(BF16)
                da_t = jnp.where(upper, _nt(vh, doh), 0.0).astype(BF16)
                a_t = jnp.where(upper, _nt(ke, qe), 0.0).astype(BF16)
                dqe = _nn(da, ke) + _nn(doh, st_b)
                dke = _nn(da_t, qe)
                dvh = _nn(a_t, doh) + _nt(kl, dst_b)
                dkl = _nn(vh, dst_b)
                dst_ref[b, h] = dst * ebl + _tn(doh, qe)
                deb = jnp.sum(st * dst, axis=0, keepdims=True)
                db = dqe * qe_f - dke * ke_f - dkl * kl_f
                db_last = jnp.sum(dkl * kl_f, axis=0, keepdims=True) + deb * ebl
                db_parts.append(db + jnp.where(is_last, db_last, 0.0))
                dc_ref[b, :, vs] = dvh.astype(BF16)
                dc_ref[b, :, 1024 + h * GLA_DK:1024 + (h + 1) * GLA_DK] = (dqe * eb * scale).astype(BF16)
                dc_ref[b, :, 1536 + h * GLA_DK:1536 + (h + 1) * GLA_DK] = (dke * enb + dkl * ekl).astype(BF16)
            dglog = _cumsum_rows(jnp.concatenate(db_parts, axis=1), upper)
            dpre = jnp.where(valid, dglog * (1.0 / GLA_NORMALIZER) / (1.0 + jnp.exp(pre)), 0.0)
            dpre_b = dpre.astype(BF16)
            dc_ref[b, :, 2048:2176] = _nt(dpre_b, wg_ref[...]).astype(BF16)
            dc_ref[b, :, 2176:2304] = jnp.zeros((C, 128), BF16)
            dwg_ref[...] += _tn(lr.astype(BF16), dpre_b)
            dbg_ref[...] += jnp.sum(dpre, axis=0, keepdims=True)

    blocks = B * (_nbytes((C, 512), F32) * 2 + _nbytes((C, 1024), F32) + _nbytes((C, 1024), BF16)
                  + _nbytes((GLA_H, GLA_DV, GLA_DK), F32) + _nbytes((C, WC), BF16)) + 3 * _nbytes((128, 512), F32)
    state = _nbytes((B, GLA_H, GLA_DV, GLA_DK), F32)
    pa = projA.reshape(B, Lp, projA.shape[1])
    rev = lambda i: NC - 1 - i
    dc, dwg, dbg = pl.pallas_call(
        body,
        out_shape=(jax.ShapeDtypeStruct((B, Lp, WC), BF16), jax.ShapeDtypeStruct((128, GLA_KW), F32),
                   jax.ShapeDtypeStruct((1, GLA_KW), F32)),
        grid=(NC,),
        in_specs=[
            pl.BlockSpec((B, C, 512), lambda i: (0, rev(i), 10)),
            pl.BlockSpec((B, C, 512), lambda i: (0, rev(i), 11)),
            pl.BlockSpec((B, C, 1024), lambda i: (0, rev(i), 0)),
            pl.BlockSpec((B, C, 128), lambda i: (0, rev(i), 3)),
            pl.BlockSpec((B, 1, GLA_H, GLA_DV, GLA_DK), lambda i: (0, rev(i), 0, 0, 0)),
            pl.BlockSpec((B, C, 1024), lambda i: (0, rev(i), 0)),
            pl.BlockSpec((128, 512), lambda i: (0, 0)),
            pl.BlockSpec((1, 512), lambda i: (0, 0)),
        ],
        out_specs=(pl.BlockSpec((B, C, WC), lambda i: (0, rev(i), 0)),
                   pl.BlockSpec((128, GLA_KW), lambda i: (0, 0)),
                   pl.BlockSpec((1, GLA_KW), lambda i: (0, 0))),
        scratch_shapes=[pltpu.VMEM((B, GLA_H, GLA_DV, GLA_DK), F32)],
        compiler_params=_params(("arbitrary",), blocks, state),
        name="gla_bwd",
    )(pa, pa, pa, projB.reshape(B, Lp, projB.shape[1]), ssave, doa.reshape(B, Lp, GLA_VW), wg, bg)
    return dc.reshape(T, WC), dwg, dbg


def _attn_bwd_pre(dyb, mla_proj, projA, ob, B, Lp):
    T = B * Lp
    NQ = Lp // QB

    def body(dyb_ref, w_ref, z_ref, o_ref, do_ref, dz_ref, dcol_ref):
        j = pl.program_id(1)
        dy_all = _nt(dyb_ref[...], w_ref[...])
        for h in range(MLA_H):
            hs = slice(h * MLA_DV, (h + 1) * MLA_DV)
            dy = dy_all[:, hs]
            z = z_ref[:, hs].astype(F32)
            o = o_ref[:, hs].astype(F32)
            s = _sigmoid(z)
            do = dy * (z * s)
            do_ref[:, hs] = do.astype(BF16)
            dz_ref[:, hs] = (dy * o * (s * (1.0 + z * (1.0 - s)))).astype(BF16)
            dl = jnp.broadcast_to(jnp.sum(do * o, axis=-1, keepdims=True), (QB, LANES))
            dcol_ref[0, h, pl.ds(j, 1), :] = jnp.transpose(dl)[0:1, :]

    rows = lambda c: pl.BlockSpec((QB, D), lambda b, j: (b * NQ + j, c))
    return pl.pallas_call(
        body,
        out_shape=(jax.ShapeDtypeStruct((T, D), BF16), jax.ShapeDtypeStruct((T, D), BF16),
                   jax.ShapeDtypeStruct((B, MLA_H, NQ, QB), F32)),
        grid=(B, NQ),
        in_specs=[rows(0), pl.BlockSpec((D, D), lambda b, j: (0, 0)), rows(2), rows(0)],
        out_specs=(rows(0), rows(0), pl.BlockSpec((1, MLA_H, NQ, QB), lambda b, j: (b, 0, 0, 0))),
        compiler_params=_params(("parallel", "arbitrary"), 6 * _nbytes((QB, D), F32)),
        name="attn_bwd_pre",
    )(dyb, mla_proj, projA, ob)


ATTN_BWD_HEADS = 8


def _attn_bwd(q_att, k_att, v_att, do, lse_c, delta_c, B, Lp):
    T = B * Lp
    NQ = Lp // QB
    G = ATTN_BWD_HEADS
    NG = MLA_H // G
    HW = 2 * LANES
    scale = 1.0 / math.sqrt(MLA_QK)

    def body(q_ref, k_ref, v_ref, do_ref, lse_ref, dl_ref, dq_out, dk_out, dv_out, dq_ref, dk_ref, dv_ref):
        kj = pl.program_id(2)
        dk_ref[...] = jnp.zeros_like(dk_ref)
        dv_ref[...] = jnp.zeros_like(dv_ref)
        col = kj * QB + lax.broadcasted_iota(jnp.int32, (QB, QB), 0)
        rowi = lax.broadcasted_iota(jnp.int32, (QB, QB), 1)

        def step(qi, masked, first):
            off = pl.multiple_of(qi * QB, QB)
            ok = _attn_mask(qi * QB + rowi, col) if masked else None
            for h in range(G):
                ws = slice(h * HW, (h + 1) * HW)
                hs = slice(h * MLA_DV, (h + 1) * MLA_DV)
                qb = q_ref[pl.ds(off, QB), ws]
                dob = do_ref[pl.ds(off, QB), hs]
                kb = k_ref[:, ws]
                lse2 = lse_ref[0, h, pl.ds(qi, 1), :]
                delta = dl_ref[0, h, pl.ds(qi, 1), :]
                p_t = jnp.exp2(_nt(kb, qb) * (scale * LOG2E) - lse2)
                if masked:
                    p_t = jnp.where(ok, p_t, 0.0)
                dv_ref[:, hs] += _nn(p_t.astype(BF16), dob)
                ds_t = (p_t * (_nt(v_ref[:, hs], dob) - delta) * scale).astype(BF16)
                dk_ref[:, ws] += _nn(ds_t, qb)
                if first:
                    dq_ref[pl.ds(off, QB), ws] = _tn(ds_t, kb)
                else:
                    dq_ref[pl.ds(off, QB), ws] += _tn(ds_t, kb)

        def sweep(masked, first):
            step(kj, True, first)

            def it(qi, carry):
                step(qi, masked, first)
                return carry
            lax.fori_loop(kj + 1, NQ, it, 0)

        pl.when(kj == 0)(lambda: sweep(True, True))
        pl.when(kj > 0)(lambda: sweep(False, False))
        dk_out[...] = dk_ref[...].astype(BF16)
        dv_out[...] = dv_ref[...].astype(BF16)

        @pl.when(kj == NQ - 1)
        def _():
            dq_out[...] = dq_ref[...].astype(BF16)

    blocks = (2 * _nbytes((Lp, G * HW), BF16) + _nbytes((Lp, G * MLA_DV), BF16) + 2 * _nbytes((QB, G * 384), BF16)
              + 2 * _nbytes((G, NQ, QB), F32))
    scratch = [pltpu.VMEM((Lp, G * HW), F32), pltpu.VMEM((QB, G * HW), F32), pltpu.VMEM((QB, G * MLA_DV), F32)]
    return pl.pallas_call(
        body,
        out_shape=(jax.ShapeDtypeStruct((T, MLA_H * HW), BF16), jax.ShapeDtypeStruct((T, MLA_H * HW), BF16),
                   jax.ShapeDtypeStruct((T, MLA_H * MLA_DV), BF16)),
        scratch_shapes=scratch,
        grid=(B, NG, NQ),
        in_specs=[
            pl.BlockSpec((Lp, G * HW), lambda b, g, j: (b, g), pipeline_mode=pl.Buffered(1)),
            pl.BlockSpec((QB, G * HW), lambda b, g, j: (b * NQ + j, g)),
            pl.BlockSpec((QB, G * MLA_DV), lambda b, g, j: (b * NQ + j, g)),
            pl.BlockSpec((Lp, G * MLA_DV), lambda b, g, j: (b, g), pipeline_mode=pl.Buffered(1)),
            pl.BlockSpec((1, G, NQ, QB), lambda b, g, j: (b, g, 0, 0)),
            pl.BlockSpec((1, G, NQ, QB), lambda b, g, j: (b, g, 0, 0)),
        ],
        out_specs=(pl.BlockSpec((Lp, G * HW), lambda b, g, j: (b, g), pipeline_mode=pl.Buffered(1)),
                   pl.BlockSpec((QB, G * HW), lambda b, g, j: (b * NQ + j, g)),
                   pl.BlockSpec((QB, G * MLA_DV), lambda b, g, j: (b * NQ + j, g))),
        compiler_params=_params(("parallel", "parallel", "arbitrary"), blocks,
                                _nbytes((Lp, G * HW), F32) + _nbytes((QB, G * 384), F32)),
        name="attn_bwd",
    )(q_att, k_att, v_att, do, lse_c, delta_c)


def _mla_bwd_post(dq, dk, dv, projB, cos_t, sin_t, gq, gkv, wuq2, wukv, B, Lp, tr):
    T = B * Lp
    nt = Lp // tr
    HW = 2 * LANES

    def body(dq_ref, dk_ref, dv_ref, pb_ref, cos_ref, sin_ref, gq_ref, gkv_ref, wuq_ref, wukv_ref,
             dqf_ref, dkvf_ref, de_ref, dgq_ref, dgkv_ref):
        first = (pl.program_id(0) == 0) & (pl.program_id(1) == 0)

        @pl.when(first)
        def _():
            dgq_ref[...] = jnp.zeros_like(dgq_ref)
            dgkv_ref[...] = jnp.zeros_like(dgkv_ref)

        cs = cos_ref[...]
        sn = sin_ref[...]
        rope_t = lambda t: t * cs + _swap_halves(t * sn)
        dkr = jnp.zeros((tr, LANES), F32)
        for h in range(MLA_H):
            dqf_ref[:, h * HW:h * HW + LANES] = dq_ref[:, h * HW:h * HW + LANES]
            dq_rope = dq_ref[:, h * HW + LANES:(h + 1) * HW].astype(F32)
            dqf_ref[:, h * HW + LANES:(h + 1) * HW] = rope_t(dq_rope).astype(BF16)
            dkvf_ref[:, h * HW:h * HW + LANES] = dk_ref[:, h * HW:h * HW + LANES]
            dkvf_ref[:, h * HW + LANES:(h + 1) * HW] = dv_ref[:, h * MLA_DV:(h + 1) * MLA_DV]
            dkr = dkr + dk_ref[:, h * HW + LANES:(h + 1) * HW].astype(F32)

        def norm_bwd(x, dn, g):
            r = lax.rsqrt(jnp.mean(x * x, axis=-1, keepdims=True) + EPS)
            xn = x * r
            t = dn * g
            return r * (t - xn * jnp.mean(t * xn, axis=-1, keepdims=True)), jnp.sum(dn * xn, axis=0, keepdims=True)

        cq = pb_ref[:, 0:Q_RANK].astype(F32)
        ckv = pb_ref[:, Q_RANK:Q_RANK + KV_RANK].astype(F32)
        dcq, dgq = norm_bwd(cq, _nt(dqf_ref[...], wuq_ref[...]), gq_ref[...])
        dckv, dgkv = norm_bwd(ckv, _nt(dkvf_ref[...], wukv_ref[...]), gkv_ref[...])
        dgq_ref[...] += dgq
        dgkv_ref[...] += dgkv
        de_ref[:, 0:Q_RANK] = dcq.astype(BF16)
        de_ref[:, Q_RANK:Q_RANK + KV_RANK] = dckv.astype(BF16)
        de_ref[:, 384:512] = rope_t(dkr).astype(BF16)

    rows = lambda w: pl.BlockSpec((tr, w), lambda b, j: (b * nt + j, 0))
    const = lambda s: pl.BlockSpec(s, lambda b, j: (0, 0))
    blocks = (2 * _nbytes((tr, 2048), F32) + _nbytes((tr, 1024), F32) + _nbytes((tr, 640), F32)
              + 2 * _nbytes((tr, 2048), BF16) + _nbytes((2048, 384), BF16) + 2 * _nbytes((tr, 2048), F32))
    return pl.pallas_call(
        body,
        out_shape=(jax.ShapeDtypeStruct((T, 2048), BF16), jax.ShapeDtypeStruct((T, 2048), BF16),
                   jax.ShapeDtypeStruct((T, 512), BF16), jax.ShapeDtypeStruct((1, Q_RANK), F32),
                   jax.ShapeDtypeStruct((1, KV_RANK), F32)),
        grid=(B, nt),
        in_specs=[rows(2048), rows(2048), rows(1024), rows(640),
                  pl.BlockSpec((tr, 128), lambda b, j: (j, 0)), pl.BlockSpec((tr, 128), lambda b, j: (j, 0)),
                  const((1, Q_RANK)), const((1, KV_RANK)), const((Q_RANK, 2048)), const((KV_RANK, 2048))],
        out_specs=(rows(2048), rows(2048), rows(512), const((1, Q_RANK)), const((1, KV_RANK))),
        compiler_params=_params(("arbitrary", "arbitrary"), blocks),
        name="mla_bwd_post",
    )(dq, dk, dv, projB, cos_t, sin_t, gq, gkv, wuq2, wukv)


def _in_proj_bwd(x, meta, dh1, dA, dBz, dC, dDz, dE, wA, wB, g, B, Lp):
    NQ = Lp // QB
    seq = x.shape[1]
    R = 2 if B % 2 == 0 else 1
    M = R * QB

    def body(x_ref, meta_ref, dh_ref, da_ref, db_ref, dc_ref, dd_ref, de_ref, wa_ref, wb_ref, g_ref,
             gx_ref, dmeta_ref, dg_ref):
        b = pl.program_id(0)
        j = pl.program_id(1)

        @pl.when((b == 0) & (j == 0))
        def _():
            dg_ref[...] = jnp.zeros_like(dg_ref)

        flat = lambda ref: ref[...].reshape(M, ref.shape[-1])
        da, dbz, dc, dd, de = flat(da_ref), flat(db_ref), flat(dc_ref), flat(dd_ref), flat(de_ref)
        du = _nt(da, wa_ref[:, 3072:5120])
        du = du + _nt(dbz, wa_ref[:, 1024:2048])
        du = du + _nt(dd, wa_ref[:, 2048:3072])
        du = du + _nt(dc[:, 0:1024], wa_ref[:, 0:1024])
        du = du + _nt(dc[:, 1024:2048], wa_ref[:, 5120:6144])
        du = du + _nt(dc[:, 2048:2176], wb_ref[:, 384:512])
        du = du + _nt(de[:, 0:384], wb_ref[:, 0:384])
        du = du + _nt(de[:, 384:512], wb_ref[:, 512:640])

        head = jnp.concatenate([jnp.zeros((FRONT, D), F32), meta_ref[...]], axis=0)
        x = jnp.concatenate([jnp.where(j > 0, x_ref[i], head) for i in range(R)], axis=0)
        r = lax.rsqrt(jnp.mean(x * x, axis=-1, keepdims=True) + EPS)
        xn = x * r
        t = du * g_ref[...]
        dh0 = flat(dh_ref).astype(F32) + r * (t - xn * jnp.mean(t * xn, axis=-1, keepdims=True))
        dg_ref[...] += jnp.sum(du * xn, axis=0, keepdims=True)
        dmeta = dh0[FRONT:HEAD_ROWS, :]
        for i in range(R):
            gx_ref[i] = dh0[i * QB:(i + 1) * QB, :]
            if i > 0:
                dmeta = dmeta + dh0[i * QB + FRONT:i * QB + HEAD_ROWS, :]

        @pl.when((j == 0) & (b == 0))
        def _():
            dmeta_ref[...] = dmeta

        @pl.when((j == 0) & (b > 0))
        def _():
            dmeta_ref[...] += dmeta

    rows = lambda w: pl.BlockSpec((R, QB, w), lambda b, j: (b, j, 0))
    x_rows = pl.BlockSpec((R, QB, D), lambda b, j: (b, jnp.maximum(j - 1, 0), 0))
    const = lambda s: pl.BlockSpec(s, lambda b, j: (0,) * len(s))
    resident = lambda s: pl.BlockSpec(s, lambda b, j: (0, 0), pipeline_mode=pl.Buffered(1))
    by_row = lambda a: a.reshape(B, Lp, a.shape[1])
    widths = [a.shape[1] for a in (dA, dBz, dC, dDz, dE)]
    blocks = sum(_nbytes((M, w), BF16) for w in widths) + 4 * _nbytes((M, D), F32)
    return pl.pallas_call(
        body,
        out_shape=(jax.ShapeDtypeStruct((B, seq, D), F32), jax.ShapeDtypeStruct((N_META, D), F32),
                   jax.ShapeDtypeStruct((1, D), F32)),
        grid=(B // R, NQ),
        in_specs=[x_rows, const((N_META, D)), rows(D)] + [rows(w) for w in widths]
        + [resident(wA.shape), resident(wB.shape), const((1, D))],
        out_specs=(x_rows, const((N_META, D)), const((1, D))),
        compiler_params=_params(("arbitrary", "arbitrary"), blocks, _nbytes(wA.shape, BF16) + _nbytes(wB.shape, BF16)),
        name="in_proj_bwd",
    )(x, meta, by_row(dh1), *[by_row(a) for a in (dA, dBz, dC, dDz, dE)], wA, wB, g)


_VMEM_WHOLE = pl.BlockSpec(memory_space=pltpu.VMEM)


def _params_whole(arrays):
    total = sum(_nbytes(a.shape, a.dtype) for a in arrays)
    return pltpu.CompilerParams(vmem_limit_bytes=int(min(total + 12 * 1024 * 1024, VMEM_CAP_V7X)))


def _wire_dtype(shape):
    return BF16 if shape[-2] * shape[-1] >= WIRE_BF16_MIN_ELEMS else F32


def _pair_add_big(gp, recv, c):
    _, half, cols = recv.shape
    th = _div_tile(half, 64, 16)
    out_dtype = _wire_dtype(recv.shape)

    steps = half // th

    def body(c_ref, a_ref, b_ref, o_ref):
        o_ref[...] = (a_ref[...].astype(F32) + b_ref[...].astype(F32)).astype(out_dtype)

    return pl.pallas_call(
        body,
        out_shape=jax.ShapeDtypeStruct(recv.shape, out_dtype),
        grid_spec=pltpu.PrefetchScalarGridSpec(
            num_scalar_prefetch=1,
            grid=(steps,),
            in_specs=[pl.BlockSpec((4, th, cols), lambda i, c_ref: (0, c_ref[0] * steps + i, 0)),
                      pl.BlockSpec((4, th, cols), lambda i, c_ref: (0, i, 0))],
            out_specs=pl.BlockSpec((4, th, cols), lambda i, c_ref: (0, i, 0)),
        ),
        compiler_params=_params(("parallel",), 3 * _nbytes((4, th, cols), F32)),
        name="grad_pair_add_big",
    )(c, gp, recv)


def _pair_add_small(gps, recvs):
    n = len(gps)

    def body(*refs):
        c = lax.axis_index("c")
        for t in range(n):
            g_ref, r_ref, o_ref = refs[t], refs[n + t], refs[2 * n + t]
            half = r_ref.shape[1]
            mine = g_ref[:, pl.ds(pl.multiple_of(c * half, 16 if half % 16 == 0 else 8), half), :]
            s = mine.astype(F32) + r_ref[...].astype(F32)
            o_ref[...] = s.astype(o_ref.dtype)

    return pl.pallas_call(
        body,
        out_shape=[jax.ShapeDtypeStruct(r.shape, _wire_dtype(r.shape)) for r in recvs],
        in_specs=[_VMEM_WHOLE] * (2 * n),
        out_specs=[_VMEM_WHOLE] * n,
        compiler_params=_params_whole(list(gps) + 2 * list(recvs)),
        name="grad_pair_add_small",
    )(*gps, *recvs)


def _chip_order_sum(landed_ref, own_ref, me):
    p = [jnp.where(me == k, own_ref[k], landed_ref[k]).astype(F32) for k in range(4)]
    return ((p[0] + p[1]) + p[2]) + p[3]


def _sum_chips_big(landed, own, pos):
    _, half, cols = landed.shape
    th = _div_tile(half, 64, 16)

    def body(pos_ref, l_ref, s_ref, o_ref):
        o_ref[0] = _chip_order_sum(l_ref, s_ref, pos_ref[1])

    spec = pl.BlockSpec((4, th, cols), lambda i, pos_ref: (0, i, 0))
    return pl.pallas_call(
        body,
        out_shape=jax.ShapeDtypeStruct((2, half, cols), F32),
        grid_spec=pltpu.PrefetchScalarGridSpec(
            num_scalar_prefetch=1,
            grid=(half // th,),
            in_specs=[spec, spec],
            out_specs=pl.BlockSpec((1, th, cols), lambda i, pos_ref: (pos_ref[0], i, 0)),
        ),
        compiler_params=_params(("parallel",), 3 * _nbytes((4, th, cols), F32)),
        name="grad_sum_chips_big",
    )(pos, landed, own)


def _sum_chips_small(landed, own):
    n = len(landed)

    def body(*refs):
        x, y, c = _mesh_pos()
        for t in range(n):
            refs[2 * n + t][c] = _chip_order_sum(refs[t], refs[n + t], 2 * x + y)

    return pl.pallas_call(
        body,
        out_shape=[jax.ShapeDtypeStruct((2,) + p.shape[1:], F32) for p in landed],
        in_specs=[_VMEM_WHOLE] * (2 * n),
        out_specs=[_VMEM_WHOLE] * n,
        compiler_params=_params_whole(list(landed) * 3),
        name="grad_sum_chips_small",
    )(*landed, *own)


def _adamw_update(w_ref, g_ref, m_ref, v_ref, d_ref, mo_ref, vo_ref):
    c1 = 1.0 - ADAM_B1 ** ADAM_STEP
    c2 = 1.0 - ADAM_B2 ** ADAM_STEP
    gv = g_ref[...]
    mn = ADAM_B1 * m_ref[...] + (1.0 - ADAM_B1) * gv
    vn = ADAM_B2 * v_ref[...] + (1.0 - ADAM_B2) * (gv * gv)
    mo_ref[...] = mn
    vo_ref[...] = vn
    d_ref[...] = -ADAM_LR * ((mn / c1) / (jnp.sqrt(vn / c2) + ADAM_EPS) + ADAM_WD * w_ref[...])


def _adamw_big(w, g, m, v):
    lead, (rows, cols) = w.shape[:-2], w.shape[-2:]
    assert all(n == 1 for n in lead)
    tr = _div_tile(rows, (1 << 19) // cols, 8)
    spec = pl.BlockSpec((1,) * len(lead) + (tr, cols), lambda i: (0,) * len(lead) + (i, 0))
    shp = jax.ShapeDtypeStruct(w.shape, F32)
    return pl.pallas_call(
        functools.partial(_adamw_update),
        out_shape=(shp, shp, shp),
        grid=(rows // tr,),
        in_specs=[spec] * 4,
        out_specs=(spec, spec, spec),
        compiler_params=_params(("parallel",), 7 * _nbytes((tr, cols), F32)),
        name="adamw_big",
    )(w, g, m, v)


def _adamw_small(ws, gs, ms, vs):
    n = len(ws)

    def body(*refs):
        for t in range(n):
            _adamw_update(refs[t], refs[n + t], refs[2 * n + t], refs[3 * n + t],
                          refs[4 * n + t], refs[5 * n + t], refs[6 * n + t])

    shapes = [jax.ShapeDtypeStruct(w.shape, F32) for w in ws]
    return pl.pallas_call(
        body,
        out_shape=shapes * 3,
        in_specs=[_VMEM_WHOLE] * (4 * n),
        out_specs=[_VMEM_WHOLE] * (3 * n),
        compiler_params=_params_whole(list(ws) * 7),
        name="adamw_small",
    )(*ws, *gs, *ms, *vs)


def _mesh_pos():
    return lax.axis_index("x"), lax.axis_index("y"), lax.axis_index("c")


def _other_chips(x, y):
    return [(1 - x, y), (x, 1 - y), (1 - x, 1 - y)]


_ANY = pl.BlockSpec(memory_space=pl.ANY)


PAIR_SPLIT_MIN_ROWS = 64


def _weight_gather(shards):
    n = len(shards)
    split = [s.shape[0] >= PAIR_SPLIT_MIN_ROWS for s in shards]

    def body(*refs):
        w_refs, o_refs = refs[:n], refs[n:2 * n]
        send_sems, recv_sems = refs[2 * n:]
        x, y, c = _mesh_pos()
        me = 2 * x + y
        chips = _other_chips(x, y)

        def rows_of(t, core):
            rows = shards[t].shape[0]
            if not split[t]:
                return pl.ds(0, rows)
            return pl.ds(pl.multiple_of(core * (rows // 2), 16), rows // 2)

        def landed(t, k, slot, rows, to):
            ref = o_refs[t].at[slot, rows]
            return pltpu.make_async_remote_copy(src_ref=ref, dst_ref=ref, send_sem=send_sems.at[6 * t + k],
                                                recv_sem=recv_sems.at[6 * t + k], device_id=to, device_id_type=MESH)

        sends = []
        for t in range(n):
            mine = rows_of(t, c)
            for k, (px, py) in enumerate(chips):
                cp = pltpu.make_async_remote_copy(src_ref=w_refs[t].at[mine], dst_ref=o_refs[t].at[me, mine],
                                                  send_sem=send_sems.at[6 * t + k], recv_sem=recv_sems.at[6 * t + k],
                                                  device_id=(px, py, c), device_id_type=MESH)
                cp.start()
                sends.append(cp)
        for t in range(n):
            mine = rows_of(t, c)
            for k, (px, py) in enumerate(chips):
                landed(t, k, 2 * px + py, mine, (x, y, c)).wait_recv()
                if split[t]:
                    cp = landed(t, 3 + k, 2 * px + py, mine, (x, y, 1 - c))
                    cp.start()
                    sends.append(cp)
        for t in range(n):
            if split[t]:
                for k, (px, py) in enumerate(chips):
                    landed(t, 3 + k, 2 * px + py, rows_of(t, 1 - c), (x, y, c)).wait_recv()
        for cp in sends:
            cp.wait_send()

    return pl.pallas_call(
        body,
        out_shape=[jax.ShapeDtypeStruct((4,) + s.shape, s.dtype) for s in shards],
        in_specs=[_ANY] * n,
        out_specs=[_ANY] * n,
        scratch_shapes=[pltpu.SemaphoreType.DMA((6 * n,)), pltpu.SemaphoreType.DMA((6 * n,))],
        name="weight_gather",
    )(*shards)


def _pair_swap(gps):
    n = len(gps)

    def body(*refs):
        g_refs, o_refs = refs[:n], refs[n:2 * n]
        send_sems, recv_sems = refs[2 * n:]
        x, y, c = _mesh_pos()
        copies = []
        for t in range(n):
            half = gps[t].shape[1] // 2
            theirs = pl.ds(pl.multiple_of((1 - c) * half, 8), half)
            cp = pltpu.make_async_remote_copy(src_ref=g_refs[t].at[:, theirs], dst_ref=o_refs[t],
                                              send_sem=send_sems.at[t], recv_sem=recv_sems.at[t],
                                              device_id=(x, y, 1 - c), device_id_type=MESH)
            cp.start()
            copies.append(cp)
        for cp in copies:
            cp.wait_send()
            cp.wait_recv()

    return pl.pallas_call(
        body,
        out_shape=[jax.ShapeDtypeStruct((4, g.shape[1] // 2, g.shape[2]), g.dtype) for g in gps],
        in_specs=[_ANY] * n,
        out_specs=[_ANY] * n,
        scratch_shapes=[pltpu.SemaphoreType.DMA((n,)), pltpu.SemaphoreType.DMA((n,))],
        name="grad_pair_swap",
    )(*gps)


_HBM = pl.BlockSpec(memory_space=pltpu.HBM)
_SEM = pl.BlockSpec(memory_space=pltpu.SEMAPHORE)


def _in_hbm(a):
    return pltpu.with_memory_space_constraint(a, pltpu.HBM)


def _chip_scatter_start(parts):
    n = len(parts)

    def body(*refs):
        s_refs, l_refs = refs[:n], refs[n:2 * n]
        send_sems, recv_sems = refs[2 * n], refs[2 * n + 1]
        token = refs[-1]
        x, y, c = _mesh_pos()
        me = 2 * x + y
        for t in range(n):
            for k, (px, py) in enumerate(_other_chips(x, y)):
                pltpu.make_async_remote_copy(src_ref=s_refs[t].at[2 * px + py], dst_ref=l_refs[t].at[me],
                                             send_sem=send_sems.at[3 * t + k], recv_sem=recv_sems.at[3 * t + k],
                                             device_id=(px, py, c), device_id_type=MESH).start()
        token[...] = jnp.zeros_like(token)

    hbm = [pltpu.HBM(p.shape, p.dtype) for p in parts]
    outs = pl.pallas_call(
        body,
        name="grad_scatter_start",
        out_shape=(pltpu.SemaphoreType.DMA((3 * n,)), pltpu.SemaphoreType.DMA((3 * n,)), *hbm, *hbm,
                   jax.ShapeDtypeStruct((8, LANES), F32)),
        in_specs=[_HBM] * (2 * n),
        out_specs=(_SEM, _SEM, *([_HBM] * (2 * n)), pl.BlockSpec(memory_space=pltpu.VMEM)),
        input_output_aliases={i: 2 + i for i in range(2 * n)},
        compiler_params=pltpu.CompilerParams(has_side_effects=pltpu.SideEffectType.DATAFLOW_SIDE_EFFECTING),
    )(*[_in_hbm(p) for p in parts], *[_in_hbm(lax.empty(p.shape, p.dtype)) for p in parts])
    return outs[0], outs[1], list(outs[2:2 + n]), list(outs[2 + n:2 + 2 * n]), outs[-1]


def _chip_scatter_wait(send_sems, recv_sems, parts, lands, after):
    n = len(parts)

    def body(*refs):
        s_refs, l_refs = refs[:n], refs[n:2 * n]
        send_sems, recv_sems = refs[2 * n], refs[2 * n + 1]
        x, y, c = _mesh_pos()
        me = 2 * x + y
        for t in range(n):
            for k, (px, py) in enumerate(_other_chips(x, y)):
                cp = pltpu.make_async_remote_copy(src_ref=s_refs[t].at[2 * px + py], dst_ref=l_refs[t].at[2 * px + py],
                                                  send_sem=send_sems.at[3 * t + k], recv_sem=recv_sems.at[3 * t + k],
                                                  device_id=(x, y, c), device_id_type=MESH)
                cp.wait_send()
                cp.wait_recv()

    hbm = [pltpu.HBM(p.shape, p.dtype) for p in parts]
    outs = pl.pallas_call(
        body,
        name="grad_scatter_wait",
        out_shape=(*hbm, *hbm),
        in_specs=[_HBM] * (2 * n) + [_SEM, _SEM, _ANY],
        out_specs=[_HBM] * (2 * n),
        input_output_aliases={i: i for i in range(2 * n)},
        compiler_params=pltpu.CompilerParams(has_side_effects=pltpu.SideEffectType.DATAFLOW_SIDE_EFFECTING),
    )(*parts, *lands, send_sems, recv_sems, after)
    return list(outs[:n]), list(outs[n:])


def _late_gather_start(shards):
    n = len(shards)

    def body(*refs):
        w_refs, l_refs = refs[:n], refs[n:2 * n]
        send_sems, recv_sems = refs[2 * n], refs[2 * n + 1]
        token = refs[-1]
        x, y, c = _mesh_pos()
        me = 2 * x + y
        for t in range(n):
            for k, (px, py) in enumerate(_other_chips(x, y)):
                pltpu.make_async_remote_copy(src_ref=w_refs[t], dst_ref=l_refs[t].at[me],
                                             send_sem=send_sems.at[3 * t + k], recv_sem=recv_sems.at[3 * t + k],
                                             device_id=(px, py, c), device_id_type=MESH).start()
        token[...] = jnp.zeros_like(token)

    src = [pltpu.HBM(s.shape, s.dtype) for s in shards]
    land = [pltpu.HBM((4,) + s.shape, s.dtype) for s in shards]
    outs = pl.pallas_call(
        body,
        name="late_gather_start",
        out_shape=(pltpu.SemaphoreType.DMA((3 * n,)), pltpu.SemaphoreType.DMA((3 * n,)), *src, *land,
                   jax.ShapeDtypeStruct((8, LANES), F32)),
        in_specs=[_HBM] * (2 * n),
        out_specs=(_SEM, _SEM, *([_HBM] * (2 * n)), pl.BlockSpec(memory_space=pltpu.VMEM)),
        input_output_aliases={i: 2 + i for i in range(2 * n)},
        compiler_params=pltpu.CompilerParams(has_side_effects=pltpu.SideEffectType.DATAFLOW_SIDE_EFFECTING),
    )(*[_in_hbm(s) for s in shards], *[_in_hbm(lax.empty((4,) + s.shape, s.dtype)) for s in shards])
    return outs[0], outs[1], list(outs[2:2 + n]), list(outs[2 + n:2 + 2 * n]), outs[-1]


def _late_gather_wait(send_sems, recv_sems, shards, lands, after):
    n = len(shards)

    def body(*refs):
        w_refs, l_refs = refs[:n], refs[n:2 * n]
        send_sems, recv_sems = refs[2 * n], refs[2 * n + 1]
        x, y, c = _mesh_pos()
        for t in range(n):
            for k, (px, py) in enumerate(_other_chips(x, y)):
                cp = pltpu.make_async_remote_copy(src_ref=w_refs[t], dst_ref=l_refs[t].at[2 * px + py],
                                                  send_sem=send_sems.at[3 * t + k], recv_sem=recv_sems.at[3 * t + k],
                                                  device_id=(x, y, c), device_id_type=MESH)
                cp.wait_send()
                cp.wait_recv()

    src = [pltpu.HBM(s.shape, s.dtype) for s in shards]
    land = [pltpu.HBM(l.shape, l.dtype) for l in lands]
    outs = pl.pallas_call(
        body,
        name="late_gather_wait",
        out_shape=(*src, *land),
        in_specs=[_HBM] * (2 * n) + [_SEM, _SEM, _ANY],
        out_specs=[_HBM] * (2 * n),
        input_output_aliases={i: i for i in range(2 * n)},
        compiler_params=pltpu.CompilerParams(has_side_effects=pltpu.SideEffectType.DATAFLOW_SIDE_EFFECTING),
    )(*shards, *lands, send_sems, recv_sems, after)
    return list(outs[n:])


def _all_to_all_small(parts):
    n = len(parts)

    def body(*refs):
        p_refs, o_refs = refs[:n], refs[n:2 * n]
        send_sems, recv_sems = refs[2 * n:]
        x, y, c = _mesh_pos()
        me = 4 * x + 2 * y + c
        sends = []
        for t in range(n):
            for k in range(1, 8):
                px, py, pc = x ^ (k >> 2), y ^ ((k >> 1) & 1), c ^ (k & 1)
                cp = pltpu.make_async_remote_copy(src_ref=p_refs[t], dst_ref=o_refs[t].at[me],
                                                  send_sem=send_sems.at[7 * t + k - 1], recv_sem=recv_sems.at[7 * t + k - 1],
                                                  device_id=(px, py, pc), device_id_type=MESH)
                cp.start()
                sends.append(cp)
        for t in range(n):
            for k in range(1, 8):
                peer = 4 * (x ^ (k >> 2)) + 2 * (y ^ ((k >> 1) & 1)) + (c ^ (k & 1))
                pltpu.make_async_remote_copy(src_ref=p_refs[t], dst_ref=o_refs[t].at[peer],
                                             send_sem=send_sems.at[7 * t + k - 1], recv_sem=recv_sems.at[7 * t + k - 1],
                                             device_id=(x, y, c), device_id_type=MESH).wait_recv()
        for cp in sends:
            cp.wait_send()

    return pl.pallas_call(
        body,
        out_shape=[jax.ShapeDtypeStruct((8,) + p.shape, p.dtype) for p in parts],
        in_specs=[_ANY] * n,
        out_specs=[_ANY] * n,
        scratch_shapes=[pltpu.SemaphoreType.DMA((7 * n,)), pltpu.SemaphoreType.DMA((7 * n,))],
        name="grad_small_all_to_all",
    )(*parts)


def _sum_devices_small(landed, own):
    n = len(landed)

    def body(*refs):
        x, y, c = _mesh_pos()
        me = 4 * x + 2 * y + c
        for t in range(n):
            acc = jnp.where(me == 0, refs[n + t][...], refs[t][0])
            for d in range(1, 8):
                acc = acc + jnp.where(me == d, refs[n + t][...], refs[t][d])
            refs[2 * n + t][...] = acc

    return pl.pallas_call(
        body,
        out_shape=[jax.ShapeDtypeStruct(p.shape, F32) for p in own],
        in_specs=[_VMEM_WHOLE] * (2 * n),
        out_specs=[_VMEM_WHOLE] * n,
        compiler_params=_params_whole(list(landed) + 2 * list(own)),
        name="grad_sum_devices_small",
    )(*landed, *own)


def _pair_join(fs):
    n = len(fs)

    def body(*refs):
        f_refs, o_refs = refs[:n], refs[n:2 * n]
        send_sems, recv_sems = refs[2 * n:]
        x, y, c = _mesh_pos()
        sends = []
        for t in range(n):
            cp = pltpu.make_async_remote_copy(src_ref=f_refs[t].at[c], dst_ref=o_refs[t].at[c], send_sem=send_sems.at[t],
                                              recv_sem=recv_sems.at[t], device_id=(x, y, 1 - c), device_id_type=MESH)
            cp.start()
            sends.append(cp)
        for t in range(n):
            pltpu.make_async_remote_copy(src_ref=f_refs[t].at[c], dst_ref=o_refs[t].at[1 - c], send_sem=send_sems.at[t],
                                         recv_sem=recv_sems.at[t], device_id=(x, y, c), device_id_type=MESH).wait_recv()
        for cp in sends:
            cp.wait_send()

    return pl.pallas_call(
        body,
        out_shape=[jax.ShapeDtypeStruct(f.shape, f.dtype) for f in fs],
        in_specs=[_ANY] * n,
        out_specs=[_ANY] * n,
        input_output_aliases={t: t for t in range(n)},
        scratch_shapes=[pltpu.SemaphoreType.DMA((n,)), pltpu.SemaphoreType.DMA((n,))],
        name="grad_pair_join",
    )(*fs)


def _rope_tables(Lp):
    inv = 1.0 / (ROPE_BASE ** (jnp.arange(0, ROPE, 2, dtype=F32) / ROPE))
    ang = (jnp.arange(Lp, dtype=F32) - FRONT)[:, None] * inv[None, :]
    cs, sn = jnp.cos(ang), jnp.sin(ang)
    return jnp.tile(cs, (1, 4)), jnp.concatenate([-sn, sn, -sn, sn], axis=1)


def _local_step(x, loss_target, meta, norm_g, w_in, gate_w, gate_b, gla_norm_g, gla_proj, q_norm_g, w_uq,
                kv_norm_g, w_ukv, mla_proj, w_out, final_norm_g, early_grads_hook=None, late_weights_hook=None):
    B, seq, _ = x.shape
    Lp = HEAD_ROWS + seq
    T = B * Lp
    tr = _div_tile(Lp, 768, 16)
    tkw = _div_tile(T, Lp, QB)
    tm_sq = _div_tile(T, 1024, QB)

    cuts = np.cumsum((0,) + SPLITS)
    shard_w = IN_WIDTH // 4

    def w_cols(i, width=None):
        parts = []
        for j in range(4):
            a, b = max(cuts[i], j * shard_w), min(cuts[i + 1], (j + 1) * shard_w)
            if a < b:
                parts.append(w_in[j][:, a - j * shard_w:b - j * shard_w])
        if width is not None:
            parts.append(jnp.zeros((D, width - (cuts[i + 1] - cuts[i])), w_in.dtype))
        return parts

    i_q, i_k, i_v, i_lr, i_z, i_cq, i_ckv, i_kr, i_mz, i_gg, i_gm = range(11)
    wA = jnp.concatenate(sum([w_cols(i) for i in (i_v, i_z, i_mz, i_gg, i_gm, i_q, i_k)], []), axis=1)
    wB = jnp.concatenate(w_cols(i_cq) + w_cols(i_ckv) + w_cols(i_lr, 128) + w_cols(i_kr, 128), axis=1)
    gn4 = jnp.tile(gla_norm_g, (1, GLA_H))
    cos_t, sin_t = _rope_tables(Lp)

    u = _rms_in(x, meta, norm_g, B, Lp)
    projA = _mm(u, wA, name="in_proj_a", out_dtype=BF16, tm=tkw, tn=1024, tk=D)
    projB = _mm(u, wB, name="in_proj_b", out_dtype=BF16, tm=tkw, tn=640, tk=D)
    if late_weights_hook is not None:
        gate_w, gla_proj, w_uq, w_ukv, mla_proj, w_out = late_weights_hook(projA)
    wg = jnp.pad(gate_w, ((0, 128 - GLA_RANK), (0, 0)))
    wuq2 = jnp.pad(w_uq.reshape(Q_RANK, MLA_H, MLA_QK), ((0, 0), (0, 0), (0, 256 - MLA_QK))).reshape(Q_RANK, 2048)
    oa, ya_in, ssave = _gla_fwd(projA, projB, wg, gate_b, gn4, B, Lp)
    ya = _mm(ya_in, gla_proj, name="gla_proj", out_dtype=BF16, tm=tm_sq, tn=D, tk=D)
    q_att, k_att, v_att, cqn, ckvn = _mla_prep(projB, cos_t, sin_t, q_norm_g, kv_norm_g, wuq2, w_ukv, B, Lp, tr)
    ob, yb_in, lse_c = _attn_fwd(q_att, k_att, v_att, projA, B, Lp)
    yb = _mm(yb_in, mla_proj, name="mla_proj", out_dtype=BF16, tm=tm_sq, tn=D, tk=D)
    dh1_b, merged, loss, d_gf = _out_proj_loss(x, meta, projA, ya, yb, w_out, final_norm_g.reshape(1, D),
                                                loss_target, B, Lp)

    g_w_out = _mm(merged, dh1_b, name="dw_out", trans_a=True, out_dtype=BF16, tm=D, tn=D, tk=tkw)
    dya, dyb, dA = _merge_bwd(dh1_b, w_out, projA, ya, yb, tr)
    g_gla_proj = _mm(ya_in, dya, name="dw_gla_proj", trans_a=True, out_dtype=BF16, tm=D, tn=D, tk=tkw)
    g_mla_proj = _mm(yb_in, dyb, name="dw_mla_proj", trans_a=True, out_dtype=BF16, tm=D, tn=D, tk=tkw)
    doa, dBz, d_gn = _gla_out_bwd(dya, gla_proj, oa, projA, gn4, tr)
    dC, g_wg, d_bg = _gla_bwd(projA, projB, ssave, doa, wg, gate_b, B, Lp)
    do, dDz, delta_c = _attn_bwd_pre(dyb, mla_proj, projA, ob, B, Lp)
    dq, dk, dv = _attn_bwd(q_att, k_att, v_att, do, lse_c, delta_c, B, Lp)
    dqf, dkvf, dE, d_gq, d_gkv = _mla_bwd_post(dq, dk, dv, projB, cos_t, sin_t, q_norm_g, kv_norm_g,
                                                wuq2, w_ukv, B, Lp, tr)
    g_wuq2 = _mm(cqn, dqf, name="dw_uq", trans_a=True, out_dtype=BF16, tm=Q_RANK, tn=2048, tk=tkw)
    g_wukv = _mm(ckvn, dkvf, name="dw_ukv", trans_a=True, out_dtype=BF16, tm=KV_RANK, tn=2048, tk=tkw)
    dparts = [dA, dBz, dC, dDz, dE]
    g_in = [_mm(u, dp, name="dw_in_%d" % i, trans_a=True, out_dtype=BF16, tm=D, tn=_div_tile(dp.shape[1], 1024, 256), tk=tkw)
            for i, dp in enumerate(dparts)]

    gA, gBz, gC, gDz, gE = g_in
    src = [(gC, 1024), (gC, 1536), (gC, 0), (gC, 2048), (gBz, 0), (gE, 0), (gE, Q_RANK), (gE, 384), (gDz, 0),
           (gA, 0), (gA, D)]
    owners = []
    for j in range(4):
        parts = []
        for i, (arr, off) in enumerate(src):
            a, b = max(cuts[i], j * shard_w), min(cuts[i + 1], (j + 1) * shard_w)
            if a < b:
                parts.append(arr[:, off + a - cuts[i]:off + b - cuts[i]])
        owners.append(jnp.concatenate(parts, axis=1))
    g_w_in = jnp.stack(owners)
    g_wuq = g_wuq2.reshape(Q_RANK, MLA_H, 256)[:, :, :MLA_QK].reshape(Q_RANK, MLA_H * MLA_QK)
    grads = dict(w_in=g_w_in, gla_gate_w=g_wg[:GLA_RANK], gla_proj=g_gla_proj, mla_w_uq=g_wuq, mla_w_ukv=g_wukv,
                 mla_proj=g_mla_proj, w_out=g_w_out, gla_gate_b=d_bg,
                 gla_norm_g=d_gn, mla_q_norm_g=d_gq, mla_kv_norm_g=d_gkv, final_norm_g=d_gf)
    token = None if early_grads_hook is None else early_grads_hook(grads)
    ng = norm_g if token is None else norm_g + token[0:1, 0:1]
    grad_x, d_meta, d_ng = _in_proj_bwd(x, meta, dh1_b, dA, dBz, dC, dDz, dE, wA, wB, ng, B, Lp)
    grads.update(meta_tokens=d_meta, norm_g=d_ng)
    return loss[0, 0], grad_x, grads


_MATS = ("w_in", "gla_gate_w", "gla_proj", "mla_w_uq", "mla_w_ukv", "mla_proj", "w_out")
_ROW_SHARDED = ("gla_proj", "mla_proj", "w_out")
_ORDER = ("meta_tokens", "norm_g", "w_in", "gla_gate_w", "gla_gate_b", "gla_norm_g", "gla_proj", "mla_q_norm_g",
          "mla_w_uq", "mla_kv_norm_g", "mla_w_ukv", "mla_proj", "w_out", "final_norm_g")
WIRE_BF16_MIN_ELEMS = 128 * 128
SMALL_PACK_ROWS = 16


def _pack_small(d, scalar=None):
    rows = [jnp.pad(d[n].reshape(1, size), ((0, 0), (0, D - size))) for n, size in SMALL]
    if scalar is not None:
        rows.append(jnp.pad(scalar.reshape(1, 1), ((0, 0), (0, D - 1))))
    return jnp.pad(jnp.concatenate(rows, axis=0), ((0, SMALL_PACK_ROWS - len(rows)), (0, 0)))


def _unpack_small(packed):
    return {n: packed[i, :size] for i, (n, size) in enumerate(SMALL)}


def kernel(x, meta_tokens, norm_g, w_in, gla_gate_w, gla_gate_b, gla_norm_g, gla_proj, mla_q_norm_g, mla_w_uq, mla_kv_norm_g, mla_w_ukv, mla_proj, w_out, final_norm_g, loss_target, m_meta_tokens, m_norm_g, m_w_in, m_gla_gate_w, m_gla_gate_b, m_gla_norm_g, m_gla_proj, m_mla_q_norm_g, m_mla_w_uq, m_mla_kv_norm_g, m_mla_w_ukv, m_mla_proj, m_w_out, m_final_norm_g, v_meta_tokens, v_norm_g, v_w_in, v_gla_gate_w, v_gla_gate_b, v_gla_norm_g, v_gla_proj, v_mla_q_norm_g, v_mla_w_uq, v_mla_kv_norm_g, v_mla_w_ukv, v_mla_proj, v_w_out, v_final_norm_g):
    w = dict(meta_tokens=meta_tokens, norm_g=norm_g, w_in=w_in[0], gla_gate_w=gla_gate_w[0], gla_gate_b=gla_gate_b,
             gla_norm_g=gla_norm_g, gla_proj=gla_proj[0], mla_q_norm_g=mla_q_norm_g, mla_w_uq=mla_w_uq[0],
             mla_kv_norm_g=mla_kv_norm_g, mla_w_ukv=mla_w_ukv[0], mla_proj=mla_proj[0], w_out=w_out[0],
             final_norm_g=final_norm_g)
    mom = dict(meta_tokens=m_meta_tokens, norm_g=m_norm_g, w_in=m_w_in[0], gla_gate_w=m_gla_gate_w[0],
               gla_gate_b=m_gla_gate_b, gla_norm_g=m_gla_norm_g, gla_proj=m_gla_proj[0], mla_q_norm_g=m_mla_q_norm_g,
               mla_w_uq=m_mla_w_uq[0], mla_kv_norm_g=m_mla_kv_norm_g, mla_w_ukv=m_mla_w_ukv[0], mla_proj=m_mla_proj[0],
               w_out=m_w_out[0], final_norm_g=m_final_norm_g)
    var = dict(meta_tokens=v_meta_tokens, norm_g=v_norm_g, w_in=v_w_in[0], gla_gate_w=v_gla_gate_w[0],
               gla_gate_b=v_gla_gate_b, gla_norm_g=v_gla_norm_g, gla_proj=v_gla_proj[0], mla_q_norm_g=v_mla_q_norm_g,
               mla_w_uq=v_mla_w_uq[0], mla_kv_norm_g=v_mla_kv_norm_g, mla_w_ukv=v_mla_w_ukv[0], mla_proj=v_mla_proj[0],
               w_out=v_w_out[0], final_norm_g=v_final_norm_g)
    out_shapes = {n: a.shape for n, a in zip(_ORDER, (meta_tokens, norm_g, w_in, gla_gate_w, gla_gate_b, gla_norm_g,
                                                     gla_proj, mla_q_norm_g, mla_w_uq, mla_kv_norm_g, mla_w_ukv,
                                                     mla_proj, w_out, final_norm_g))}

    me = (2 * lax.axis_index("x") + lax.axis_index("y")).astype(jnp.int32)
    is_mine = lax.broadcasted_iota(jnp.int32, (4, 1, 1), 0) == me
    with_own = lambda gth, own: jnp.where(is_mine, own[None], gth)
    first = [w["w_in"].astype(BF16), meta_tokens]
    w_in_owner, meta_owner = [with_own(gth, own) for gth, own in zip(_weight_gather(first), first)]
    meta_full = meta_owner.transpose(1, 0, 2).reshape(N_META, D)
    late_names = _MATS[1:]
    late = [w[n].astype(BF16) for n in late_names]
    gather_sems = _late_gather_start(late)

    def late_weights(after):
        lands = _late_gather_wait(gather_sems[0], gather_sems[1], gather_sems[2], gather_sems[3], after)
        full = []
        for name, land, own in zip(late_names, lands, late):
            gth = with_own(land, own)
            if name in _ROW_SHARDED:
                full.append(gth.reshape(4 * gth.shape[1], gth.shape[2]))
            else:
                full.append(gth.transpose(1, 0, 2).reshape(gth.shape[1], 4 * gth.shape[2]))
        return full

    def by_owner(name, arr):
        if name == "w_in":
            return arr
        if name in _ROW_SHARDED:
            return arr.reshape(4, arr.shape[0] // 4, arr.shape[1])
        return arr.reshape(arr.shape[0], 4, arr.shape[1] // 4).transpose(1, 0, 2)

    c_idx = lax.axis_index("c").astype(jnp.int32).reshape(1)
    pos = jnp.stack([c_idx[0], me])
    in_flight = {}

    def start_matrix_reduce(early):
        gps = [by_owner(n, early[n]) for n in _MATS]
        recvs = _pair_swap(gps)
        s1 = [_pair_add_big(gps[0], recvs[0], c_idx)] + list(_pair_add_small(gps[1:], recvs[1:]))
        send_sems, recv_sems, parts, lands, token = _chip_scatter_start(s1)
        in_flight.update(send_sems=send_sems, recv_sems=recv_sems, parts=parts, lands=lands)
        return token

    norm_g_after_start = norm_g + gather_sems[4][0:1, 0:1]
    loss_local, grad_x, g = _local_step(
        x, loss_target, meta_full, norm_g_after_start, w_in_owner, None, gla_gate_b, gla_norm_g, None,
        mla_q_norm_g, None, mla_kv_norm_g, None, None, None, final_norm_g,
        early_grads_hook=start_matrix_reduce, late_weights_hook=late_weights)

    s1, landed = _chip_scatter_wait(in_flight["send_sems"], in_flight["recv_sems"], in_flight["parts"],
                                    in_flight["lands"], after=g["norm_g"])
    halves = [_sum_chips_big(landed[0], s1[0], pos)] + list(_sum_chips_small(landed[1:], s1[1:]))
    g_mats = [j.reshape(out_shapes[n]) for j, n in zip(_pair_join(halves), _MATS)]

    late = [g["meta_tokens"], _pack_small(g, scalar=loss_local)]
    meta_sum, small_sum = _sum_devices_small(_all_to_all_small(late), late)
    loss = small_sum[len(SMALL), 0]
    g_meta = lax.dynamic_slice(meta_sum, (0, me * (D // 4)), (N_META, D // 4))
    names = _MATS + ("meta_tokens",)
    g_red = g_mats + [g_meta, small_sum]

    tens = lambda d: [d[n].reshape(out_shapes[n]) for n in names] + [_pack_small(d)]
    w_t, m_t, v_t = tens(w), tens(mom), tens(var)
    big = _adamw_big(w_t[0], g_red[0], m_t[0], v_t[0])
    rest = _adamw_small(w_t[1:], g_red[1:], m_t[1:], v_t[1:])
    k = len(names)
    results = {"grad": g_red}
    for i, kind in enumerate(("delta", "new_m", "new_v")):
        results[kind] = [big[i]] + list(rest[i * k:(i + 1) * k])

    outs = []
    for kind in ("grad", "delta", "new_m", "new_v"):
        vals = dict(zip(names, results[kind][:-1]))
        vals.update(_unpack_small(results[kind][-1]))
        outs += [vals[n].reshape(out_shapes[n]) for n in _ORDER]
    return (loss, grad_x, *outs)
```

```python
import functools
import math

import jax
import jax.numpy as jnp
import numpy as np
from jax import lax
from jax.experimental import pallas as pl
from jax.experimental.pallas import tpu as pltpu

F32 = jnp.float32
BF16 = jnp.bfloat16

D = 1024
N_META = 16
QB = 256
FRONT = QB - N_META
HEAD_ROWS = FRONT + N_META
assert FRONT % 64 == 48
EPS = 1e-6

GLA_H, GLA_DK, GLA_DV, GLA_RANK, GLA_C = 4, 128, 256, 16, 64
GLA_NORMALIZER = 16.0
GLA_KW, GLA_VW = GLA_H * GLA_DK, GLA_H * GLA_DV
MLA_H, NOPE, ROPE, MLA_DV, Q_RANK, KV_RANK = 8, 128, 64, 128, 256, 128
MLA_QK = NOPE + ROPE
ROPE_BASE = 10000.0
SPLITS = (GLA_KW, GLA_KW, GLA_VW, GLA_RANK, GLA_VW, Q_RANK, KV_RANK, ROPE, MLA_H * MLA_DV, D, D)
IN_WIDTH = sum(SPLITS)

ADAM_LR, ADAM_B1, ADAM_B2, ADAM_EPS, ADAM_WD, ADAM_STEP = 0.001, 0.9, 0.999, 1e-08, 0.01, 10

LANES = 128
VMEM_CAP_V7X = 56 * 1024 * 1024
MESH = pl.DeviceIdType.MESH
NEG = -1e30
LOG2E = math.log2(math.e)

SMALL = (("norm_g", D), ("gla_gate_b", GLA_KW), ("gla_norm_g", GLA_DV), ("mla_q_norm_g", Q_RANK),
         ("mla_kv_norm_g", KV_RANK), ("final_norm_g", D))


def _div_tile(n, target, mult):
    best = None
    for d in range(mult, min(n, target) + 1, mult):
        if n % d == 0:
            best = d
    assert best is not None, (n, target, mult)
    return best


def _params(sem, block_bytes, scratch_bytes=0):
    est = 2 * block_bytes + scratch_bytes + 12 * 1024 * 1024
    return pltpu.CompilerParams(dimension_semantics=sem, vmem_limit_bytes=int(min(max(est, 24 * 1024 * 1024), VMEM_CAP_V7X)))


def _nbytes(shape, dtype):
    return int(np.prod(shape)) * jnp.dtype(dtype).itemsize


def _sigmoid(x):
    return 1.0 / (1.0 + jnp.exp(-x))


def _nt(a, b):
    return lax.dot_general(a, b, (((1,), (1,)), ((), ())), preferred_element_type=F32)


def _tn(a, b):
    return lax.dot_general(a, b, (((0,), (0,)), ((), ())), preferred_element_type=F32)


def _nn(a, b):
    return jnp.dot(a, b, preferred_element_type=F32)


def _split2(x):
    a = x.astype(BF16)
    b = (x - a.astype(F32)).astype(BF16)
    return a, b


def _mm(a, b, *, name, trans_a=False, trans_b=False, out_dtype=F32, tm, tn, tk):
    assert not (trans_a and trans_b)
    if trans_a:
        K, M = a.shape
    else:
        M, K = a.shape
    N = b.shape[0] if trans_b else b.shape[1]
    assert (b.shape[1] if trans_b else b.shape[0]) == K
    assert M % tm == 0 and N % tn == 0 and K % tk == 0, (name, M, N, K, tm, tn, tk)
    nk = K // tk

    def body(a_ref, b_ref, o_ref, *scratch):
        av = a_ref[...].astype(BF16)
        bv = b_ref[...].astype(BF16)
        prod = _tn(av, bv) if trans_a else (_nt(av, bv) if trans_b else _nn(av, bv))
        if nk == 1:
            o_ref[...] = prod.astype(out_dtype)
        else:
            acc = scratch[0]
            k = pl.program_id(2)

            @pl.when(k == 0)
            def _():
                acc[...] = prod

            @pl.when(k > 0)
            def _():
                acc[...] += prod

            @pl.when(k == nk - 1)
            def _():
                o_ref[...] = acc[...].astype(out_dtype)

    if trans_a:
        a_spec = pl.BlockSpec((tk, tm), lambda i, j, k: (k, i))
    else:
        a_spec = pl.BlockSpec((tm, tk), lambda i, j, k: (i, k))
    if trans_b:
        b_spec = pl.BlockSpec((tn, tk), lambda i, j, k: (j, k))
    else:
        b_spec = pl.BlockSpec((tk, tn), lambda i, j, k: (k, j))
    blocks = (_nbytes((tm, tk), a.dtype) + _nbytes((tk, tn), b.dtype) + _nbytes((tm, tn), out_dtype))
    scratch = [pltpu.VMEM((tm, tn), F32)] if nk > 1 else []
    return pl.pallas_call(
        body,
        out_shape=jax.ShapeDtypeStruct((M, N), out_dtype),
        grid=(M // tm, N // tn, nk),
        in_specs=[a_spec, b_spec],
        out_specs=pl.BlockSpec((tm, tn), lambda i, j, k: (i, j)),
        scratch_shapes=scratch,
        compiler_params=_params(("parallel", "parallel", "arbitrary"), blocks + _nbytes((tm, tn), F32),
                                _nbytes((tm, tn), F32) if nk > 1 else 0),
        name=name,
    )(a, b)


def _h_tile(j, x_ref, meta_ref):
    head = jnp.concatenate([jnp.zeros((FRONT, D), F32), meta_ref[...]], axis=0)
    return jnp.where(j > 0, x_ref[0], head)


def _x_spec():
    return pl.BlockSpec((1, QB, D), lambda b, j: (b, jnp.maximum(j - 1, 0), 0))


def _rms_in(x, meta, g, B, Lp):
    T = B * Lp
    NQ = Lp // QB

    def body(x_ref, meta_ref, g_ref, u_ref):
        h = _h_tile(pl.program_id(1), x_ref, meta_ref)
        r = lax.rsqrt(jnp.mean(h * h, axis=-1, keepdims=True) + EPS)
        u_ref[...] = (h * r * g_ref[...]).astype(BF16)

    return pl.pallas_call(
        body,
        out_shape=jax.ShapeDtypeStruct((T, D), BF16),
        grid=(B, NQ),
        in_specs=[_x_spec(), pl.BlockSpec((N_META, D), lambda b, j: (0, 0)), pl.BlockSpec((1, D), lambda b, j: (0, 0))],
        out_specs=pl.BlockSpec((QB, D), lambda b, j: (b * NQ + j, 0)),
        compiler_params=_params(("parallel", "parallel"), _nbytes((QB, D), F32) * 2),
        name="rms_in",
    )(x, meta, g)


def _gla_gate(lr, wg, bg, valid):
    pre = _nn(lr.astype(BF16), wg) + bg
    logsig = jnp.minimum(pre, 0.0) - jnp.log(1.0 + jnp.exp(-jnp.abs(pre)))
    return pre, jnp.where(valid, logsig / GLA_NORMALIZER, 0.0)


def _tri_masks():
    ri = lax.broadcasted_iota(jnp.int32, (GLA_C, GLA_C), 0)
    ci = lax.broadcasted_iota(jnp.int32, (GLA_C, GLA_C), 1)
    return ci <= ri, ci >= ri


def _cumsum_rows(x, ones_mask):
    w = jnp.where(ones_mask, 1.0, 0.0).astype(BF16)
    a, b = _split2(x)
    return _nn(w, a) + _nn(w, b)


def _gla_fwd(projA, projB, wg, bg, gn4, B, Lp):
    T = B * Lp
    NC = Lp // GLA_C
    C = GLA_C
    scale = GLA_DK ** -0.5

    def body(q_ref, k_ref, v_ref, lr_ref, z_ref, wg_ref, bg_ref, gn_ref, oa_ref, ya_ref, ssave_ref, st_ref):
        n = pl.program_id(0)

        @pl.when(n == 0)
        def _():
            st_ref[...] = jnp.zeros_like(st_ref)

        pos = n * C + lax.broadcasted_iota(jnp.int32, (C, 1), 0)
        lower, _ = _tri_masks()
        is_last = lax.broadcasted_iota(jnp.int32, (C, 1), 0) == C - 1
        for b in range(B):
            ssave_ref[b, 0] = st_ref[b]
            _, glog = _gla_gate(lr_ref[b], wg_ref[...], bg_ref[...], pos >= FRONT)
            bcum = _cumsum_rows(glog, lower)
            for h in range(GLA_H):
                ks = slice(h * GLA_DK, (h + 1) * GLA_DK)
                vs = slice(h * GLA_DV, (h + 1) * GLA_DV)
                bh = bcum[:, ks]
                blast = jnp.sum(jnp.where(is_last, bh, 0.0), axis=0, keepdims=True)
                qh = q_ref[b, :, ks].astype(F32) * scale
                kh = k_ref[b, :, ks].astype(F32)
                qe = (qh * jnp.exp(bh)).astype(BF16)
                ke = (kh * jnp.exp(-bh)).astype(BF16)
                kl = (kh * jnp.exp(blast - bh)).astype(BF16)
                vh = v_ref[b, :, vs].astype(BF16)
                a = jnp.where(lower, _nt(qe, ke), 0.0).astype(BF16)
                st = st_ref[b, h]
                o = _nn(a, vh) + _nt(qe, st.astype(BF16))
                st_ref[b, h] = st * jnp.exp(blast) + _tn(vh, kl)
                oa_ref[b, :, vs] = o.astype(BF16)
                on = o * lax.rsqrt(jnp.mean(o * o, axis=-1, keepdims=True) + EPS) * gn_ref[:, vs]
                z = z_ref[b, :, vs].astype(F32)
                ya_ref[b, :, vs] = (on * (z * _sigmoid(z))).astype(BF16)

    blocks = B * (_nbytes((C, 512), F32) * 2 + _nbytes((C, 1024), F32) * 3 + _nbytes((C, 1024), BF16)
                  + _nbytes((GLA_H, GLA_DV, GLA_DK), F32)) + _nbytes((128, 512), BF16)
    state = _nbytes((B, GLA_H, GLA_DV, GLA_DK), F32)
    pa = projA.reshape(B, Lp, projA.shape[1])
    oa, ya, ssave = pl.pallas_call(
        body,
        out_shape=(jax.ShapeDtypeStruct((B, Lp, GLA_VW), BF16), jax.ShapeDtypeStruct((B, Lp, GLA_VW), BF16),
                   jax.ShapeDtypeStruct((B, NC, GLA_H, GLA_DV, GLA_DK), F32)),
        grid=(NC,),
        in_specs=[
            pl.BlockSpec((B, C, 512), lambda n: (0, n, 10)),
            pl.BlockSpec((B, C, 512), lambda n: (0, n, 11)),
            pl.BlockSpec((B, C, 1024), lambda n: (0, n, 0)),
            pl.BlockSpec((B, C, 128), lambda n: (0, n, 3)),
            pl.BlockSpec((B, C, 1024), lambda n: (0, n, 1)),
            pl.BlockSpec((128, 512), lambda n: (0, 0)),
            pl.BlockSpec((1, 512), lambda n: (0, 0)),
            pl.BlockSpec((1, 1024), lambda n: (0, 0)),
        ],
        out_specs=(pl.BlockSpec((B, C, 1024), lambda n: (0, n, 0)),
                   pl.BlockSpec((B, C, 1024), lambda n: (0, n, 0)),
                   pl.BlockSpec((B, 1, GLA_H, GLA_DV, GLA_DK), lambda n: (0, n, 0, 0, 0))),
        scratch_shapes=[pltpu.VMEM((B, GLA_H, GLA_DV, GLA_DK), F32)],
        compiler_params=_params(("arbitrary",), blocks, state),
        name="gla_fwd",
    )(pa, pa, pa, projB.reshape(B, Lp, projB.shape[1]), pa, wg, bg, gn4)
    return oa.reshape(T, GLA_VW), ya.reshape(T, GLA_VW), ssave


def _swap_halves(x):
    lane = lax.broadcasted_iota(jnp.int32, x.shape, 1)
    return jnp.where((lane % 64) < 32, pltpu.roll(x, 96, 1), pltpu.roll(x, 32, 1))


def _mla_prep(projB, cos_t, sin_t, gq, gkv, wuq2, wukv, B, Lp, tr):
    T = B * Lp
    nt = Lp // tr
    HW = 2 * LANES

    def body(pb_ref, cos_ref, sin_ref, gq_ref, gkv_ref, wuq_ref, wukv_ref, q_ref, k_ref, v_ref, cqn_ref, ckvn_ref):
        cq = pb_ref[:, 0:Q_RANK].astype(F32)
        ckv = pb_ref[:, Q_RANK:Q_RANK + KV_RANK].astype(F32)
        kr = pb_ref[:, 512:640].astype(F32)
        cqn = (cq * lax.rsqrt(jnp.mean(cq * cq, axis=-1, keepdims=True) + EPS) * gq_ref[...]).astype(BF16)
        ckvn = (ckv * lax.rsqrt(jnp.mean(ckv * ckv, axis=-1, keepdims=True) + EPS) * gkv_ref[...]).astype(BF16)
        cqn_ref[...] = cqn
        ckvn_ref[...] = ckvn
        qf = _nn(cqn, wuq_ref[...])
        kvf = _nn(ckvn, wukv_ref[...])
        cs = cos_ref[...]
        sn = sin_ref[...]
        rope = lambda t: t * cs + _swap_halves(t) * sn
        kr_r = rope(kr).astype(BF16)
        for h in range(MLA_H):
            q_ref[:, h * HW:h * HW + LANES] = qf[:, h * HW:h * HW + LANES].astype(BF16)
            q_ref[:, h * HW + LANES:(h + 1) * HW] = rope(qf[:, h * HW + LANES:(h + 1) * HW]).astype(BF16)
            k_ref[:, h * HW:h * HW + LANES] = kvf[:, h * HW:h * HW + LANES].astype(BF16)
            k_ref[:, h * HW + LANES:(h + 1) * HW] = kr_r
            v_ref[:, h * MLA_DV:(h + 1) * MLA_DV] = kvf[:, h * HW + LANES:(h + 1) * HW].astype(BF16)

    blocks = (_nbytes((tr, 640), F32) + 2 * _nbytes((tr, 128), F32) + _nbytes((Q_RANK, 2048), BF16)
              + _nbytes((KV_RANK, 2048), BF16) + _nbytes((tr, 2048 * 2 + 1024 + 384), BF16)
              + 2 * _nbytes((tr, 2048), F32))
    return pl.pallas_call(
        body,
        out_shape=(jax.ShapeDtypeStruct((T, MLA_H * HW), BF16), jax.ShapeDtypeStruct((T, MLA_H * HW), BF16),
                   jax.ShapeDtypeStruct((T, MLA_H * MLA_DV), BF16), jax.ShapeDtypeStruct((T, Q_RANK), BF16),
                   jax.ShapeDtypeStruct((T, KV_RANK), BF16)),
        grid=(B, nt),
        in_specs=[
            pl.BlockSpec((tr, 640), lambda b, j: (b * nt + j, 0)),
            pl.BlockSpec((tr, 128), lambda b, j: (j, 0)),
            pl.BlockSpec((tr, 128), lambda b, j: (j, 0)),
            pl.BlockSpec((1, Q_RANK), lambda b, j: (0, 0)),
            pl.BlockSpec((1, KV_RANK), lambda b, j: (0, 0)),
            pl.BlockSpec((Q_RANK, 2048), lambda b, j: (0, 0)),
            pl.BlockSpec((KV_RANK, 2048), lambda b, j: (0, 0)),
        ],
        out_specs=(pl.BlockSpec((tr, 2048), lambda b, j: (b * nt + j, 0)),
                   pl.BlockSpec((tr, 2048), lambda b, j: (b * nt + j, 0)),
                   pl.BlockSpec((tr, 1024), lambda b, j: (b * nt + j, 0)),
                   pl.BlockSpec((tr, Q_RANK), lambda b, j: (b * nt + j, 0)),
                   pl.BlockSpec((tr, KV_RANK), lambda b, j: (b * nt + j, 0))),
        compiler_params=_params(("parallel", "parallel"), blocks),
        name="mla_prep",
    )(projB, cos_t, sin_t, gq, gkv, wuq2, wukv)


def _attn_mask(row, col):
    return (col <= row) & ((col >= FRONT) | (row < FRONT))


def _attn_fwd(q_att, k_att, v_att, projA, B, Lp):
    T = B * Lp
    NQ = Lp // QB
    HW = 2 * LANES
    scale = 1.0 / math.sqrt(MLA_QK)

    def body(q_ref, k_ref, v_ref, mz_ref, o_ref, yb_ref, lsec_ref, m_ref, l_ref, acc_ref):
        qi = pl.program_id(1)
        m_ref[...] = jnp.full(m_ref.shape, NEG, F32)
        l_ref[...] = jnp.zeros_like(l_ref)
        acc_ref[...] = jnp.zeros_like(acc_ref)
        row = qi * QB + lax.broadcasted_iota(jnp.int32, (QB, QB), 0)
        coli = lax.broadcasted_iota(jnp.int32, (QB, QB), 1)

        def step(kj, masked):
            off = pl.multiple_of(kj * QB, QB)
            ok = _attn_mask(row, kj * QB + coli) if masked else None
            for h in range(MLA_H):
                q = q_ref[:, h * HW:(h + 1) * HW]
                kb = k_ref[pl.ds(off, QB), h * HW:(h + 1) * HW]
                vb = v_ref[pl.ds(off, QB), h * MLA_DV:(h + 1) * MLA_DV]
                s = _nt(q, kb) * (scale * LOG2E)
                if masked:
                    s = jnp.where(ok, s, NEG)
                m_old = m_ref[h]
                m_new = jnp.maximum(m_old, jnp.max(s, axis=-1, keepdims=True))
                alpha = jnp.exp2(m_old - m_new)
                p = jnp.exp2(s - jnp.tile(m_new, (1, QB // LANES)))
                m_ref[h] = m_new
                l_ref[h] = alpha * l_ref[h] + jnp.sum(p, axis=-1, keepdims=True)
                acc_ref[h] = alpha * acc_ref[h] + _nn(p.astype(BF16), vb)

        step(0, True)

        def unmasked(kj, carry):
            step(kj, False)
            return carry

        lax.fori_loop(1, qi, unmasked, 0)

        @pl.when(qi > 0)
        def _():
            step(qi, True)

        for h in range(MLA_H):
            hs = slice(h * MLA_DV, (h + 1) * MLA_DV)
            l = l_ref[h]
            o = acc_ref[h] / l
            o_ref[:, hs] = o.astype(BF16)
            z = mz_ref[:, hs].astype(F32)
            yb_ref[:, hs] = (o * (z * _sigmoid(z))).astype(BF16)
            lse2 = m_ref[h] + jnp.log(l) * LOG2E
            lsec_ref[0, h, pl.ds(qi, 1), :] = jnp.transpose(lse2)[0:1, :]

    blocks = (_nbytes((QB, 2048), BF16) + _nbytes((Lp, 2048), BF16) + _nbytes((Lp, 1024), BF16)
              + 2 * _nbytes((QB, 1024), F32) + _nbytes((QB, 1024), BF16) + _nbytes((MLA_H, QB, LANES), F32)
              + _nbytes((MLA_H, NQ, QB), F32))
    return pl.pallas_call(
        body,
        out_shape=(jax.ShapeDtypeStruct((T, MLA_H * MLA_DV), BF16), jax.ShapeDtypeStruct((T, MLA_H * MLA_DV), BF16),
                   jax.ShapeDtypeStruct((B, MLA_H, NQ, QB), F32)),
        grid=(B, NQ),
        in_specs=[
            pl.BlockSpec((QB, MLA_H * HW), lambda b, i: (b * NQ + i, 0)),
            pl.BlockSpec((Lp, MLA_H * HW), lambda b, i: (b, 0)),
            pl.BlockSpec((Lp, MLA_H * MLA_DV), lambda b, i: (b, 0)),
            pl.BlockSpec((QB, 1024), lambda b, i: (b * NQ + i, 2)),
        ],
        out_specs=(pl.BlockSpec((QB, 1024), lambda b, i: (b * NQ + i, 0)),
                   pl.BlockSpec((QB, 1024), lambda b, i: (b * NQ + i, 0)),
                   pl.BlockSpec((1, MLA_H, NQ, QB), lambda b, i: (b, 0, 0, 0))),
        scratch_shapes=[pltpu.VMEM((MLA_H, QB, LANES), F32), pltpu.VMEM((MLA_H, QB, LANES), F32),
                        pltpu.VMEM((MLA_H, QB, MLA_DV), F32)],
        compiler_params=_params(("parallel", "arbitrary"), blocks, 3 * _nbytes((MLA_H, QB, LANES), F32)),
        name="attn_fwd",
    )(q_att, k_att, v_att, projA)


def _out_proj_loss(x, meta, projA, ya, yb, w_out, gf, tgt, B, Lp):
    T = B * Lp
    NQ = Lp // QB

    def body(x_ref, meta_ref, gg_ref, gm_ref, ya_ref, yb_ref, w_ref, gf_ref, t_ref,
             dhb_ref, mg_ref, loss_ref, dgf_ref):
        b = pl.program_id(0)
        j = pl.program_id(1)

        @pl.when((b == 0) & (j == 0))
        def _():
            loss_ref[...] = jnp.zeros_like(loss_ref)
            dgf_ref[...] = jnp.zeros_like(dgf_ref)

        f32 = lambda ref: ref[...].astype(F32)
        merged = (_sigmoid(f32(gg_ref)) * f32(ya_ref) + _sigmoid(f32(gm_ref)) * f32(yb_ref)).astype(BF16)
        mg_ref[...] = merged
        h1 = _h_tile(j, x_ref, meta_ref) + _nn(merged, w_ref[...])
        r = lax.rsqrt(jnp.mean(h1 * h1, axis=-1, keepdims=True) + EPS)
        hn = h1 * r
        gfv = gf_ref[...]
        diff = jnp.where(j > 0, hn * gfv - t_ref[0], 0.0)
        loss_ref[...] += (0.5 / D) * jnp.sum(jnp.sum(diff * diff, axis=-1, keepdims=True), axis=0, keepdims=True)
        dout = diff * (1.0 / D)
        dgf_ref[...] += jnp.sum(dout * hn, axis=0, keepdims=True)
        dhn = dout * gfv
        dh = r * (dhn - hn * jnp.mean(dhn * hn, axis=-1, keepdims=True))
        dhb_ref[...] = dh.astype(BF16)

    rows = lambda c: pl.BlockSpec((QB, D), lambda b, j: (b * NQ + j, c))
    const = lambda s: pl.BlockSpec(s, lambda b, j: (0, 0))
    return pl.pallas_call(
        body,
        out_shape=(jax.ShapeDtypeStruct((T, D), BF16), jax.ShapeDtypeStruct((T, D), BF16),
                   jax.ShapeDtypeStruct((1, 1), F32), jax.ShapeDtypeStruct((1, D), F32)),
        grid=(B, NQ),
        in_specs=[_x_spec(), const((N_META, D)), rows(3), rows(4), rows(0), rows(0), const((D, D)),
                  const((1, D)), _x_spec()],
        out_specs=(rows(0), rows(0), const((1, 1)), const((1, D))),
        compiler_params=_params(("arbitrary", "arbitrary"), 10 * _nbytes((QB, D), F32)),
        name="out_proj_loss",
    )(x, meta, projA, projA, ya, yb, w_out, gf, tgt)


def _merge_bwd(dh1_b, w_out, projA, ya, yb, tr):
    T = dh1_b.shape[0]

    def body(dh_ref, w_ref, gg_ref, gm_ref, ya_ref, yb_ref, dya_ref, dyb_ref, da_ref):
        d = _nt(dh_ref[...], w_ref[...])
        sg = _sigmoid(gg_ref[...].astype(F32))
        sm = _sigmoid(gm_ref[...].astype(F32))
        dya_ref[...] = (d * sg).astype(BF16)
        dyb_ref[...] = (d * sm).astype(BF16)
        da_ref[:, 0:D] = (d * ya_ref[...].astype(F32) * (sg * (1.0 - sg))).astype(BF16)
        da_ref[:, D:2 * D] = (d * yb_ref[...].astype(F32) * (sm * (1.0 - sm))).astype(BF16)

    spec = lambda c: pl.BlockSpec((tr, D), lambda i: (i, c))
    return pl.pallas_call(
        body,
        out_shape=(jax.ShapeDtypeStruct((T, D), BF16), jax.ShapeDtypeStruct((T, D), BF16),
                   jax.ShapeDtypeStruct((T, 2 * D), BF16)),
        grid=(T // tr,),
        in_specs=[spec(0), pl.BlockSpec((D, D), lambda i: (0, 0)), spec(3), spec(4), spec(0), spec(0)],
        out_specs=(spec(0), spec(0), pl.BlockSpec((tr, 2 * D), lambda i: (i, 0))),
        compiler_params=_params(("parallel",), 8 * _nbytes((tr, D), F32)),
        name="merge_bwd",
    )(dh1_b, w_out, projA, projA, ya, yb)


def _gla_out_bwd(dya, gla_proj, oa, projA, gn4, tr):
    T = dya.shape[0]
    nsteps = T // tr

    def body(dya_ref, w_ref, oa_ref, z_ref, gn_ref, do_ref, dz_ref, dgn_ref, acc_ref):
        i = pl.program_id(0)

        @pl.when(i == 0)
        def _():
            acc_ref[...] = jnp.zeros_like(acc_ref)

        dy_all = _nt(dya_ref[...], w_ref[...])
        for h in range(GLA_H):
            vs = slice(h * GLA_DV, (h + 1) * GLA_DV)
            dy = dy_all[:, vs]
            o = oa_ref[:, vs].astype(F32)
            z = z_ref[:, vs].astype(F32)
            gn = gn_ref[:, vs]
            s = _sigmoid(z)
            ra = lax.rsqrt(jnp.mean(o * o, axis=-1, keepdims=True) + EPS)
            on = o * ra
            don = dy * (z * s)
            t = don * gn
            do_ref[:, vs] = (ra * (t - on * jnp.mean(t * on, axis=-1, keepdims=True))).astype(BF16)
            dz_ref[:, vs] = (dy * (on * gn) * (s * (1.0 + z * (1.0 - s)))).astype(BF16)
            acc_ref[:, vs] += jnp.sum(don * on, axis=0, keepdims=True)

        @pl.when(i == nsteps - 1)
        def _():
            a = acc_ref[...]
            dgn_ref[...] = a[:, 0:256] + a[:, 256:512] + a[:, 512:768] + a[:, 768:1024]

    spec = lambda c: pl.BlockSpec((tr, D), lambda i: (i, c))
    return pl.pallas_call(
        body,
        out_shape=(jax.ShapeDtypeStruct((T, D), BF16), jax.ShapeDtypeStruct((T, D), BF16),
                   jax.ShapeDtypeStruct((1, GLA_DV), F32)),
        grid=(nsteps,),
        in_specs=[spec(0), pl.BlockSpec((D, D), lambda i: (0, 0)), spec(0), spec(1),
                  pl.BlockSpec((1, D), lambda i: (0, 0))],
        out_specs=(spec(0), spec(0), pl.BlockSpec((1, GLA_DV), lambda i: (0, 0))),
        scratch_shapes=[pltpu.VMEM((1, D), F32)],
        compiler_params=_params(("arbitrary",), 6 * _nbytes((tr, D), F32)),
        name="gla_out_bwd",
    )(dya, gla_proj, oa, projA, gn4)


def _gla_bwd(projA, projB, ssave, doa, wg, bg, B, Lp):
    T = B * Lp
    NC = Lp // GLA_C
    C = GLA_C
    scale = GLA_DK ** -0.5
    WC = 2304

    def body(q_ref, k_ref, v_ref, lr_ref, ss_ref, do_ref, wg_ref, bg_ref, dc_ref, dwg_ref, dbg_ref, dst_ref):
        i = pl.program_id(0)
        n = NC - 1 - i

        @pl.when(i == 0)
        def _():
            dst_ref[...] = jnp.zeros_like(dst_ref)
            dwg_ref[...] = jnp.zeros_like(dwg_ref)
            dbg_ref[...] = jnp.zeros_like(dbg_ref)

        pos = n * C + lax.broadcasted_iota(jnp.int32, (C, 1), 0)
        valid = pos >= FRONT
        lower, upper = _tri_masks()
        is_last = lax.broadcasted_iota(jnp.int32, (C, 1), 0) == C - 1
        for b in range(B):
            lr = lr_ref[b]
            pre, glog = _gla_gate(lr, wg_ref[...], bg_ref[...], valid)
            bcum = _cumsum_rows(glog, lower)
            db_parts = []
            for h in range(GLA_H):
                ks = slice(h * GLA_DK, (h + 1) * GLA_DK)
                vs = slice(h * GLA_DV, (h + 1) * GLA_DV)
                bh = bcum[:, ks]
                blast = jnp.sum(jnp.where(is_last, bh, 0.0), axis=0, keepdims=True)
                eb, enb, ekl, ebl = jnp.exp(bh), jnp.exp(-bh), jnp.exp(blast - bh), jnp.exp(blast)
                qh = q_ref[b, :, ks].astype(F32) * scale
                kh = k_ref[b, :, ks].astype(F32)
                qe_f, ke_f, kl_f = qh * eb, kh * enb, kh * ekl
                qe, ke, kl = qe_f.astype(BF16), ke_f.astype(BF16), kl_f.astype(BF16)
                vh = v_ref[b, :, vs].astype(BF16)
                doh = do_ref[b, :, vs]
                st = ss_ref[b, 0, h]
                dst = dst_ref[b, h]
                st_b, dst_b = st.astype(BF16), dst.astype(BF16)
                da = jnp.where(lower, _nt(doh, vh), 0.0).astype(BF16)
                da_t = jnp.where(upper, _nt(vh, doh), 0.0).astype(BF16)
                a_t = jnp.where(upper, _nt(ke, qe), 0.0).astype(BF16)
                dqe = _nn(da, ke) + _nn(doh, st_b)
                dke = _nn(da_t, qe)
                dvh = _nn(a_t, doh) + _nt(kl, dst_b)
                dkl = _nn(vh, dst_b)
                dst_ref[b, h] = dst * ebl + _tn(doh, qe)
                deb = jnp.sum(st * dst, axis=0, keepdims=True)
                db = dqe * qe_f - dke * ke_f - dkl * kl_f
                db_last = jnp.sum(dkl * kl_f, axis=0, keepdims=True) + deb * ebl
                db_parts.append(db + jnp.where(is_last, db_last, 0.0))
                dc_ref[b, :, vs] = dvh.astype(BF16)
                dc_ref[b, :, 1024 + h * GLA_DK:1024 + (h + 1) * GLA_DK] = (dqe * eb * scale).astype(BF16)
                dc_ref[b, :, 1536 + h * GLA_DK:1536 + (h + 1) * GLA_DK] = (dke * enb + dkl * ekl).astype(BF16)
            dglog = _cumsum_rows(jnp.concatenate(db_parts, axis=1), upper)
            dpre = jnp.where(valid, dglog * (1.0 / GLA_NORMALIZER) / (1.0 + jnp.exp(pre)), 0.0)
            dpre_b = dpre.astype(BF16)
            dc_ref[b, :, 2048:2176] = _nt(dpre_b, wg_ref[...]).astype(BF16)
            dc_ref[b, :, 2176:2304] = jnp.zeros((C, 128), BF16)
            dwg_ref[...] += _tn(lr.astype(BF16), dpre_b)
            dbg_ref[...] += jnp.sum(dpre, axis=0, keepdims=True)

    blocks = B * (_nbytes((C, 512), F32) * 2 + _nbytes((C, 1024), F32) + _nbytes((C, 1024), BF16)
                  + _nbytes((GLA_H, GLA_DV, GLA_DK), F32) + _nbytes((C, WC), BF16)) + 3 * _nbytes((128, 512), F32)
    state = _nbytes((B, GLA_H, GLA_DV, GLA_DK), F32)
    pa = projA.reshape(B, Lp, projA.shape[1])
    rev = lambda i: NC - 1 - i
    dc, dwg, dbg = pl.pallas_call(
        body,
        out_shape=(jax.ShapeDtypeStruct((B, Lp, WC), BF16), jax.ShapeDtypeStruct((128, GLA_KW), F32),
                   jax.ShapeDtypeStruct((1, GLA_KW), F32)),
        grid=(NC,),
        in_specs=[
            pl.BlockSpec((B, C, 512), lambda i: (0, rev(i), 10)),
            pl.BlockSpec((B, C, 512), lambda i: (0, rev(i), 11)),
            pl.BlockSpec((B, C, 1024), lambda i: (0, rev(i), 0)),
            pl.BlockSpec((B, C, 128), lambda i: (0, rev(i), 3)),
            pl.BlockSpec((B, 1, GLA_H, GLA_DV, GLA_DK), lambda i: (0, rev(i), 0, 0, 0)),
            pl.BlockSpec((B, C, 1024), lambda i: (0, rev(i), 0)),
            pl.BlockSpec((128, 512), lambda i: (0, 0)),
            pl.BlockSpec((1, 512), lambda i: (0, 0)),
        ],
        out_specs=(pl.BlockSpec((B, C, WC), lambda i: (0, rev(i), 0)),
                   pl.BlockSpec((128, GLA_KW), lambda i: (0, 0)),
                   pl.BlockSpec((1, GLA_KW), lambda i: (0, 0))),
        scratch_shapes=[pltpu.VMEM((B, GLA_H, GLA_DV, GLA_DK), F32)],
        compiler_params=_params(("arbitrary",), blocks, state),
        name="gla_bwd",
    )(pa, pa, pa, projB.reshape(B, Lp, projB.shape[1]), ssave, doa.reshape(B, Lp, GLA_VW), wg, bg)
    return dc.reshape(T, WC), dwg, dbg


def _attn_bwd_pre(dyb, mla_proj, projA, ob, B, Lp):
    T = B * Lp
    NQ = Lp // QB

    RB = 3 if NQ % 3 == 0 else 1
    NS = NQ // RB

    def body(dyb_ref, w_ref, z_ref, o_ref, do_ref, dz_ref, dcol_ref, dy_ref):
        j = pl.program_id(1)
        dy_ref[...] = _nt(dyb_ref[...], w_ref[...])
        for r in range(RB):
            rs = slice(r * QB, (r + 1) * QB)
            for h in range(MLA_H):
                hs = slice(h * MLA_DV, (h + 1) * MLA_DV)
                dy = dy_ref[rs, hs]
                z = z_ref[rs, hs].astype(F32)
                o = o_ref[rs, hs].astype(F32)
                s = _sigmoid(z)
                do = dy * (z * s)
                do_ref[rs, hs] = do.astype(BF16)
                dz_ref[rs, hs] = (dy * o * (s * (1.0 + z * (1.0 - s)))).astype(BF16)
                dl = jnp.broadcast_to(jnp.sum(do * o, axis=-1, keepdims=True), (QB, LANES))
                dcol_ref[0, h, pl.ds(j * RB + r, 1), :] = jnp.transpose(dl)[0:1, :]

    rows = lambda c: pl.BlockSpec((RB * QB, D), lambda b, j: (b * NS + j, c))
    return pl.pallas_call(
        body,
        out_shape=(jax.ShapeDtypeStruct((T, D), BF16), jax.ShapeDtypeStruct((T, D), BF16),
                   jax.ShapeDtypeStruct((B, MLA_H, NQ, QB), F32)),
        scratch_shapes=[pltpu.VMEM((RB * QB, D), F32)],
        grid=(B, NS),
        in_specs=[rows(0), pl.BlockSpec((D, D), lambda b, j: (0, 0)), rows(2), rows(0)],
        out_specs=(rows(0), rows(0), pl.BlockSpec((1, MLA_H, NQ, QB), lambda b, j: (b, 0, 0, 0))),
        compiler_params=_params(("parallel", "arbitrary"), 6 * _nbytes((RB * QB, D), F32),
                                _nbytes((RB * QB, D), F32)),
        name="attn_bwd_pre",
    )(dyb, mla_proj, projA, ob)


ATTN_BWD_HEADS = 8


def _attn_bwd(q_att, k_att, v_att, do, lse_c, delta_c, B, Lp):
    T = B * Lp
    NQ = Lp // QB
    G = ATTN_BWD_HEADS
    NG = MLA_H // G
    HW = 2 * LANES
    scale = 1.0 / math.sqrt(MLA_QK)

    def body(q_ref, k_ref, v_ref, do_ref, lse_ref, dl_ref, dq_out, dk_out, dv_out, dq_ref, dk_ref, dv_ref):
        kj = pl.program_id(2)
        col = kj * QB + lax.broadcasted_iota(jnp.int32, (QB, QB), 0)
        rowi = lax.broadcasted_iota(jnp.int32, (QB, QB), 1)

        def step(qi, masked, first, diag=False):
            off = pl.multiple_of(qi * QB, QB)
            ok = _attn_mask(qi * QB + rowi, col) if masked else None
            for h in range(G):
                ws = slice(h * HW, (h + 1) * HW)
                hs = slice(h * MLA_DV, (h + 1) * MLA_DV)
                qb = q_ref[pl.ds(off, QB), ws]
                dob = do_ref[pl.ds(off, QB), hs]
                kb = k_ref[:, ws]
                lse2 = lse_ref[0, h, pl.ds(qi, 1), :]
                delta = dl_ref[0, h, pl.ds(qi, 1), :]
                p_t = jnp.exp2(_nt(kb, qb) * (scale * LOG2E) - lse2)
                if masked:
                    p_t = jnp.where(ok, p_t, 0.0)
                ds_t = (p_t * (_nt(v_ref[:, hs], dob) - delta) * scale).astype(BF16)
                if diag:
                    dv_ref[:, hs] = _nn(p_t.astype(BF16), dob)
                    dk_ref[:, ws] = _nn(ds_t, qb)
                else:
                    dv_ref[:, hs] += _nn(p_t.astype(BF16), dob)
                    dk_ref[:, ws] += _nn(ds_t, qb)
                if first:
                    dq_ref[pl.ds(off, QB), ws] = _tn(ds_t, kb)
                else:
                    dq_ref[pl.ds(off, QB), ws] += _tn(ds_t, kb)

        def sweep(masked, first):
            step(kj, True, first, diag=True)

            def it(qi, carry):
                step(qi, masked, first)
                return carry
            lax.fori_loop(kj + 1, NQ, it, 0)

        pl.when(kj == 0)(lambda: sweep(True, True))
        pl.when(kj > 0)(lambda: sweep(False, False))
        dk_out[...] = dk_ref[...].astype(BF16)
        dv_out[...] = dv_ref[...].astype(BF16)

        @pl.when(kj == NQ - 1)
        def _():
            dq_out[...] = dq_ref[...].astype(BF16)

    blocks = (2 * _nbytes((Lp, G * HW), BF16) + _nbytes((Lp, G * MLA_DV), BF16) + 2 * _nbytes((QB, G * 384), BF16)
              + 2 * _nbytes((G, NQ, QB), F32))
    scratch = [pltpu.VMEM((Lp, G * HW), F32), pltpu.VMEM((QB, G * HW), F32), pltpu.VMEM((QB, G * MLA_DV), F32)]
    return pl.pallas_call(
        body,
        out_shape=(jax.ShapeDtypeStruct((T, MLA_H * HW), BF16), jax.ShapeDtypeStruct((T, MLA_H * HW), BF16),
                   jax.ShapeDtypeStruct((T, MLA_H * MLA_DV), BF16)),
        scratch_shapes=scratch,
        grid=(B, NG, NQ),
        in_specs=[
            pl.BlockSpec((Lp, G * HW), lambda b, g, j: (b, g), pipeline_mode=pl.Buffered(1)),
            pl.BlockSpec((QB, G * HW), lambda b, g, j: (b * NQ + j, g)),
            pl.BlockSpec((QB, G * MLA_DV), lambda b, g, j: (b * NQ + j, g)),
            pl.BlockSpec((Lp, G * MLA_DV), lambda b, g, j: (b, g), pipeline_mode=pl.Buffered(1)),
            pl.BlockSpec((1, G, NQ, QB), lambda b, g, j: (b, g, 0, 0)),
            pl.BlockSpec((1, G, NQ, QB), lambda b, g, j: (b, g, 0, 0)),
        ],
        out_specs=(pl.BlockSpec((Lp, G * HW), lambda b, g, j: (b, g), pipeline_mode=pl.Buffered(1)),
                   pl.BlockSpec((QB, G * HW), lambda b, g, j: (b * NQ + j, g)),
                   pl.BlockSpec((QB, G * MLA_DV), lambda b, g, j: (b * NQ + j, g))),
        compiler_params=_params(("parallel", "parallel", "arbitrary"), blocks,
                                _nbytes((Lp, G * HW), F32) + _nbytes((QB, G * 384), F32)),
        name="attn_bwd",
    )(q_att, k_att, v_att, do, lse_c, delta_c)


def _mla_bwd_post(dq, dk, dv, projB, cos_t, sin_t, gq, gkv, wuq2, wukv, B, Lp, tr):
    T = B * Lp
    nt = Lp // tr
    HW = 2 * LANES

    def body(dq_ref, dk_ref, dv_ref, pb_ref, cos_ref, sin_ref, gq_ref, gkv_ref, wuq_ref, wukv_ref,
             dqf_ref, dkvf_ref, de_ref, dgq_ref, dgkv_ref):
        first = (pl.program_id(0) == 0) & (pl.program_id(1) == 0)

        @pl.when(first)
        def _():
            dgq_ref[...] = jnp.zeros_like(dgq_ref)
            dgkv_ref[...] = jnp.zeros_like(dgkv_ref)

        cs = cos_ref[...]
        sn = sin_ref[...]
        rope_t = lambda t: t * cs + _swap_halves(t * sn)
        dkr = jnp.zeros((tr, LANES), F32)
        for h in range(MLA_H):
            dqf_ref[:, h * HW:h * HW + LANES] = dq_ref[:, h * HW:h * HW + LANES]
            dq_rope = dq_ref[:, h * HW + LANES:(h + 1) * HW].astype(F32)
            dqf_ref[:, h * HW + LANES:(h + 1) * HW] = rope_t(dq_rope).astype(BF16)
            dkvf_ref[:, h * HW:h * HW + LANES] = dk_ref[:, h * HW:h * HW + LANES]
            dkvf_ref[:, h * HW + LANES:(h + 1) * HW] = dv_ref[:, h * MLA_DV:(h + 1) * MLA_DV]
            dkr = dkr + dk_ref[:, h * HW + LANES:(h + 1) * HW].astype(F32)

        def norm_bwd(x, dn, g):
            r = lax.rsqrt(jnp.mean(x * x, axis=-1, keepdims=True) + EPS)
            xn = x * r
            t = dn * g
            return r * (t - xn * jnp.mean(t * xn, axis=-1, keepdims=True)), jnp.sum(dn * xn, axis=0, keepdims=True)

        cq = pb_ref[:, 0:Q_RANK].astype(F32)
        ckv = pb_ref[:, Q_RANK:Q_RANK + KV_RANK].astype(F32)
        dcq, dgq = norm_bwd(cq, _nt(dqf_ref[...], wuq_ref[...]), gq_ref[...])
        dckv, dgkv = norm_bwd(ckv, _nt(dkvf_ref[...], wukv_ref[...]), gkv_ref[...])
        dgq_ref[...] += dgq
        dgkv_ref[...] += dgkv
        de_ref[:, 0:Q_RANK] = dcq.astype(BF16)
        de_ref[:, Q_RANK:Q_RANK + KV_RANK] = dckv.astype(BF16)
        de_ref[:, 384:512] = rope_t(dkr).astype(BF16)

    rows = lambda w: pl.BlockSpec((tr, w), lambda b, j: (b * nt + j, 0))
    const = lambda s: pl.BlockSpec(s, lambda b, j: (0, 0))
    blocks = (2 * _nbytes((tr, 2048), F32) + _nbytes((tr, 1024), F32) + _nbytes((tr, 640), F32)
              + 2 * _nbytes((tr, 2048), BF16) + _nbytes((2048, 384), BF16) + 2 * _nbytes((tr, 2048), F32))
    return pl.pallas_call(
        body,
        out_shape=(jax.ShapeDtypeStruct((T, 2048), BF16), jax.ShapeDtypeStruct((T, 2048), BF16),
                   jax.ShapeDtypeStruct((T, 512), BF16), jax.ShapeDtypeStruct((1, Q_RANK), F32),
                   jax.ShapeDtypeStruct((1, KV_RANK), F32)),
        grid=(B, nt),
        in_specs=[rows(2048), rows(2048), rows(1024), rows(640),
                  pl.BlockSpec((tr, 128), lambda b, j: (j, 0)), pl.BlockSpec((tr, 128), lambda b, j: (j, 0)),
                  const((1, Q_RANK)), const((1, KV_RANK)), const((Q_RANK, 2048)), const((KV_RANK, 2048))],
        out_specs=(rows(2048), rows(2048), rows(512), const((1, Q_RANK)), const((1, KV_RANK))),
        compiler_params=_params(("arbitrary", "arbitrary"), blocks),
        name="mla_bwd_post",
    )(dq, dk, dv, projB, cos_t, sin_t, gq, gkv, wuq2, wukv)


def _in_proj_bwd(x, meta, dh1, dA, dBz, dC, dDz, dE, wA, wB, g, B, Lp):
    NQ = Lp // QB
    seq = x.shape[1]
    R = 2 if B % 2 == 0 else 1
    M = R * QB

    def body(x_ref, meta_ref, dh_ref, da_ref, db_ref, dc_ref, dd_ref, de_ref, wa_ref, wb_ref, g_ref,
             gx_ref, dmeta_ref, dg_ref):
        b = pl.program_id(0)
        j = pl.program_id(1)

        @pl.when((b == 0) & (j == 0))
        def _():
            dg_ref[...] = jnp.zeros_like(dg_ref)

        flat = lambda ref: ref[...].reshape(M, ref.shape[-1])
        da, dbz, dc, dd, de = flat(da_ref), flat(db_ref), flat(dc_ref), flat(dd_ref), flat(de_ref)
        du = _nt(da, wa_ref[:, 3072:5120])
        du = du + _nt(dbz, wa_ref[:, 1024:2048])
        du = du + _nt(dd, wa_ref[:, 2048:3072])
        du = du + _nt(dc[:, 0:1024], wa_ref[:, 0:1024])
        du = du + _nt(dc[:, 1024:2048], wa_ref[:, 5120:6144])
        du = du + _nt(dc[:, 2048:2176], wb_ref[:, 384:512])
        du = du + _nt(de[:, 0:384], wb_ref[:, 0:384])
        du = du + _nt(de[:, 384:512], wb_ref[:, 512:640])

        head = jnp.concatenate([jnp.zeros((FRONT, D), F32), meta_ref[...]], axis=0)
        x = jnp.concatenate([jnp.where(j > 0, x_ref[i], head) for i in range(R)], axis=0)
        r = lax.rsqrt(jnp.mean(x * x, axis=-1, keepdims=True) + EPS)
        xn = x * r
        t = du * g_ref[...]
        dh0 = flat(dh_ref).astype(F32) + r * (t - xn * jnp.mean(t * xn, axis=-1, keepdims=True))
        dg_ref[...] += jnp.sum(du * xn, axis=0, keepdims=True)
        dmeta = dh0[FRONT:HEAD_ROWS, :]
        for i in range(R):
            gx_ref[i] = dh0[i * QB:(i + 1) * QB, :]
            if i > 0:
                dmeta = dmeta + dh0[i * QB + FRONT:i * QB + HEAD_ROWS, :]

        @pl.when((j == 0) & (b == 0))
        def _():
            dmeta_ref[...] = dmeta

        @pl.when((j == 0) & (b > 0))
        def _():
            dmeta_ref[...] += dmeta

    rows = lambda w: pl.BlockSpec((R, QB, w), lambda b, j: (b, j, 0))
    x_rows = pl.BlockSpec((R, QB, D), lambda b, j: (b, jnp.maximum(j - 1, 0), 0))
    const = lambda s: pl.BlockSpec(s, lambda b, j: (0,) * len(s))
    resident = lambda s: pl.BlockSpec(s, lambda b, j: (0, 0), pipeline_mode=pl.Buffered(1))
    by_row = lambda a: a.reshape(B, Lp, a.shape[1])
    widths = [a.shape[1] for a in (dA, dBz, dC, dDz, dE)]
    blocks = sum(_nbytes((M, w), BF16) for w in widths) + 4 * _nbytes((M, D), F32)
    return pl.pallas_call(
        body,
        out_shape=(jax.ShapeDtypeStruct((B, seq, D), F32), jax.ShapeDtypeStruct((N_META, D), F32),
                   jax.ShapeDtypeStruct((1, D), F32)),
        grid=(B // R, NQ),
        in_specs=[x_rows, const((N_META, D)), rows(D)] + [rows(w) for w in widths]
        + [resident(wA.shape), resident(wB.shape), const((1, D))],
        out_specs=(x_rows, const((N_META, D)), const((1, D))),
        compiler_params=_params(("arbitrary", "arbitrary"), blocks, _nbytes(wA.shape, BF16) + _nbytes(wB.shape, BF16)),
        name="in_proj_bwd",
    )(x, meta, by_row(dh1), *[by_row(a) for a in (dA, dBz, dC, dDz, dE)], wA, wB, g)


_VMEM_WHOLE = pl.BlockSpec(memory_space=pltpu.VMEM)


def _params_whole(arrays):
    total = sum(_nbytes(a.shape, a.dtype) for a in arrays)
    return pltpu.CompilerParams(vmem_limit_bytes=int(min(total + 12 * 1024 * 1024, VMEM_CAP_V7X)))


def _wire_dtype(shape):
    return BF16 if shape[-2] * shape[-1] >= WIRE_BF16_MIN_ELEMS else F32


def _pair_add_big(gp, recv, c):
    _, half, cols = recv.shape
    th = _div_tile(half, 64, 16)
    out_dtype = _wire_dtype(recv.shape)

    steps = half // th

    def body(c_ref, a_ref, b_ref, o_ref):
        o_ref[...] = (a_ref[...].astype(F32) + b_ref[...].astype(F32)).astype(out_dtype)

    return pl.pallas_call(
        body,
        out_shape=jax.ShapeDtypeStruct(recv.shape, out_dtype),
        grid_spec=pltpu.PrefetchScalarGridSpec(
            num_scalar_prefetch=1,
            grid=(steps,),
            in_specs=[pl.BlockSpec((4, th, cols), lambda i, c_ref: (0, c_ref[0] * steps + i, 0)),
                      pl.BlockSpec((4, th, cols), lambda i, c_ref: (0, i, 0))],
            out_specs=pl.BlockSpec((4, th, cols), lambda i, c_ref: (0, i, 0)),
        ),
        compiler_params=_params(("parallel",), 3 * _nbytes((4, th, cols), F32)),
        name="grad_pair_add_big",
    )(c, gp, recv)


def _pair_add_small(gps, recvs):
    n = len(gps)

    def body(*refs):
        c = lax.axis_index("c")
        for t in range(n):
            g_ref, r_ref, o_ref = refs[t], refs[n + t], refs[2 * n + t]
            half = r_ref.shape[1]
            mine = g_ref[:, pl.ds(pl.multiple_of(c * half, 16 if half % 16 == 0 else 8), half), :]
            s = mine.astype(F32) + r_ref[...].astype(F32)
            o_ref[...] = s.astype(o_ref.dtype)

    return pl.pallas_call(
        body,
        out_shape=[jax.ShapeDtypeStruct(r.shape, _wire_dtype(r.shape)) for r in recvs],
        in_specs=[_VMEM_WHOLE] * (2 * n),
        out_specs=[_VMEM_WHOLE] * n,
        compiler_params=_params_whole(list(gps) + 2 * list(recvs)),
        name="grad_pair_add_small",
    )(*gps, *recvs)


def _chip_order_sum(landed_ref, own_ref, me):
    p = [jnp.where(me == k, own_ref[k], landed_ref[k]).astype(F32) for k in range(4)]
    return ((p[0] + p[1]) + p[2]) + p[3]


def _sum_chips_big(landed, own, pos):
    _, half, cols = landed.shape
    th = _div_tile(half, 64, 16)

    def body(pos_ref, l_ref, s_ref, o_ref):
        o_ref[0] = _chip_order_sum(l_ref, s_ref, pos_ref[1])

    spec = pl.BlockSpec((4, th, cols), lambda i, pos_ref: (0, i, 0))
    return pl.pallas_call(
        body,
        out_shape=jax.ShapeDtypeStruct((2, half, cols), F32),
        grid_spec=pltpu.PrefetchScalarGridSpec(
            num_scalar_prefetch=1,
            grid=(half // th,),
            in_specs=[spec, spec],
            out_specs=pl.BlockSpec((1, th, cols), lambda i, pos_ref: (pos_ref[0], i, 0)),
        ),
        compiler_params=_params(("parallel",), 3 * _nbytes((4, th, cols), F32)),
        name="grad_sum_chips_big",
    )(pos, landed, own)


def _sum_chips_small(landed, own):
    n = len(landed)

    def body(*refs):
        x, y, c = _mesh_pos()
        for t in range(n):
            refs[2 * n + t][c] = _chip_order_sum(refs[t], refs[n + t], 2 * x + y)

    return pl.pallas_call(
        body,
        out_shape=[jax.ShapeDtypeStruct((2,) + p.shape[1:], F32) for p in landed],
        in_specs=[_VMEM_WHOLE] * (2 * n),
        out_specs=[_VMEM_WHOLE] * n,
        compiler_params=_params_whole(list(landed) * 3),
        name="grad_sum_chips_small",
    )(*landed, *own)


def _adamw_update(w_ref, g_ref, m_ref, v_ref, d_ref, mo_ref, vo_ref):
    c1 = 1.0 - ADAM_B1 ** ADAM_STEP
    c2 = 1.0 - ADAM_B2 ** ADAM_STEP
    gv = g_ref[...]
    mn = ADAM_B1 * m_ref[...] + (1.0 - ADAM_B1) * gv
    vn = ADAM_B2 * v_ref[...] + (1.0 - ADAM_B2) * (gv * gv)
    mo_ref[...] = mn
    vo_ref[...] = vn
    d_ref[...] = -ADAM_LR * ((mn / c1) / (jnp.sqrt(vn / c2) + ADAM_EPS) + ADAM_WD * w_ref[...])


def _adamw_big(w, g, m, v):
    lead, (rows, cols) = w.shape[:-2], w.shape[-2:]
    assert all(n == 1 for n in lead)
    tr = _div_tile(rows, (1 << 19) // cols, 8)
    spec = pl.BlockSpec((1,) * len(lead) + (tr, cols), lambda i: (0,) * len(lead) + (i, 0))
    shp = jax.ShapeDtypeStruct(w.shape, F32)
    return pl.pallas_call(
        functools.partial(_adamw_update),
        out_shape=(shp, shp, shp),
        grid=(rows // tr,),
        in_specs=[spec] * 4,
        out_specs=(spec, spec, spec),
        compiler_params=_params(("parallel",), 7 * _nbytes((tr, cols), F32)),
        name="adamw_big",
    )(w, g, m, v)


def _adamw_small(ws, gs, ms, vs):
    n = len(ws)

    def body(*refs):
        for t in range(n):
            _adamw_update(refs[t], refs[n + t], refs[2 * n + t], refs[3 * n + t],
                          refs[4 * n + t], refs[5 * n + t], refs[6 * n + t])

    shapes = [jax.ShapeDtypeStruct(w.shape, F32) for w in ws]
    return pl.pallas_call(
        body,
        out_shape=shapes * 3,
        in_specs=[_VMEM_WHOLE] * (4 * n),
        out_specs=[_VMEM_WHOLE] * (3 * n),
        compiler_params=_params_whole(list(ws) * 7),
        name="adamw_small",
    )(*ws, *gs, *ms, *vs)


def _mesh_pos():
    return lax.axis_index("x"), lax.axis_index("y"), lax.axis_index("c")


def _other_chips(x, y):
    return [(1 - x, y), (x, 1 - y), (1 - x, 1 - y)]


_ANY = pl.BlockSpec(memory_space=pl.ANY)


PAIR_SPLIT_MIN_ROWS = 64


def _weight_gather(shards):
    n = len(shards)
    split = [s.shape[0] >= PAIR_SPLIT_MIN_ROWS for s in shards]

    def body(*refs):
        w_refs, o_refs = refs[:n], refs[n:2 * n]
        send_sems, recv_sems = refs[2 * n:]
        x, y, c = _mesh_pos()
        me = 2 * x + y
        chips = _other_chips(x, y)

        def rows_of(t, core):
            rows = shards[t].shape[0]
            if not split[t]:
                return pl.ds(0, rows)
            return pl.ds(pl.multiple_of(core * (rows // 2), 16), rows // 2)

        def landed(t, k, slot, rows, to):
            ref = o_refs[t].at[slot, rows]
            return pltpu.make_async_remote_copy(src_ref=ref, dst_ref=ref, send_sem=send_sems.at[6 * t + k],
                                                recv_sem=recv_sems.at[6 * t + k], device_id=to, device_id_type=MESH)

        sends = []
        for t in range(n):
            mine = rows_of(t, c)
            for k, (px, py) in enumerate(chips):
                cp = pltpu.make_async_remote_copy(src_ref=w_refs[t].at[mine], dst_ref=o_refs[t].at[me, mine],
                                                  send_sem=send_sems.at[6 * t + k], recv_sem=recv_sems.at[6 * t + k],
                                                  device_id=(px, py, c), device_id_type=MESH)
                cp.start()
                sends.append(cp)
        for t in range(n):
            mine = rows_of(t, c)
            for k, (px, py) in enumerate(chips):
                landed(t, k, 2 * px + py, mine, (x, y, c)).wait_recv()
                if split[t]:
                    cp = landed(t, 3 + k, 2 * px + py, mine, (x, y, 1 - c))
                    cp.start()
                    sends.append(cp)
        for t in range(n):
            if split[t]:
                for k, (px, py) in enumerate(chips):
                    landed(t, 3 + k, 2 * px + py, rows_of(t, 1 - c), (x, y, c)).wait_recv()
        for cp in sends:
            cp.wait_send()

    return pl.pallas_call(
        body,
        out_shape=[jax.ShapeDtypeStruct((4,) + s.shape, s.dtype) for s in shards],
        in_specs=[_ANY] * n,
        out_specs=[_ANY] * n,
        scratch_shapes=[pltpu.SemaphoreType.DMA((6 * n,)), pltpu.SemaphoreType.DMA((6 * n,))],
        name="weight_gather",
    )(*shards)


def _pair_swap(gps):
    n = len(gps)

    def body(*refs):
        g_refs, o_refs = refs[:n], refs[n:2 * n]
        send_sems, recv_sems = refs[2 * n:]
        x, y, c = _mesh_pos()
        copies = []
        for t in range(n):
            half = gps[t].shape[1] // 2
            theirs = pl.ds(pl.multiple_of((1 - c) * half, 8), half)
            cp = pltpu.make_async_remote_copy(src_ref=g_refs[t].at[:, theirs], dst_ref=o_refs[t],
                                              send_sem=send_sems.at[t], recv_sem=recv_sems.at[t],
                                              device_id=(x, y, 1 - c), device_id_type=MESH)
            cp.start()
            copies.append(cp)
        for cp in copies:
            cp.wait_send()
            cp.wait_recv()

    return pl.pallas_call(
        body,
        out_shape=[jax.ShapeDtypeStruct((4, g.shape[1] // 2, g.shape[2]), g.dtype) for g in gps],
        in_specs=[_ANY] * n,
        out_specs=[_ANY] * n,
        scratch_shapes=[pltpu.SemaphoreType.DMA((n,)), pltpu.SemaphoreType.DMA((n,))],
        name="grad_pair_swap",
    )(*gps)


_HBM = pl.BlockSpec(memory_space=pltpu.HBM)
_SEM = pl.BlockSpec(memory_space=pltpu.SEMAPHORE)


def _in_hbm(a):
    return pltpu.with_memory_space_constraint(a, pltpu.HBM)


def _chip_scatter_start(parts):
    n = len(parts)

    def body(*refs):
        s_refs, l_refs = refs[:n], refs[n:2 * n]
        send_sems, recv_sems = refs[2 * n], refs[2 * n + 1]
        token = refs[-1]
        x, y, c = _mesh_pos()
        me = 2 * x + y
        for t in range(n):
            for k, (px, py) in enumerate(_other_chips(x, y)):
                pltpu.make_async_remote_copy(src_ref=s_refs[t].at[2 * px + py], dst_ref=l_refs[t].at[me],
                                             send_sem=send_sems.at[3 * t + k], recv_sem=recv_sems.at[3 * t + k],
                                             device_id=(px, py, c), device_id_type=MESH).start()
        token[...] = jnp.zeros_like(token)

    hbm = [pltpu.HBM(p.shape, p.dtype) for p in parts]
    outs = pl.pallas_call(
        body,
        name="grad_scatter_start",
        out_shape=(pltpu.SemaphoreType.DMA((3 * n,)), pltpu.SemaphoreType.DMA((3 * n,)), *hbm, *hbm,
                   jax.ShapeDtypeStruct((8, LANES), F32)),
        in_specs=[_HBM] * (2 * n),
        out_specs=(_SEM, _SEM, *([_HBM] * (2 * n)), pl.BlockSpec(memory_space=pltpu.VMEM)),
        input_output_aliases={i: 2 + i for i in range(2 * n)},
        compiler_params=pltpu.CompilerParams(has_side_effects=pltpu.SideEffectType.DATAFLOW_SIDE_EFFECTING),
    )(*[_in_hbm(p) for p in parts], *[_in_hbm(lax.empty(p.shape, p.dtype)) for p in parts])
    return outs[0], outs[1], list(outs[2:2 + n]), list(outs[2 + n:2 + 2 * n]), outs[-1]


def _chip_scatter_wait(send_sems, recv_sems, parts, lands, after):
    n = len(parts)

    def body(*refs):
        s_refs, l_refs = refs[:n], refs[n:2 * n]
        send_sems, recv_sems = refs[2 * n], refs[2 * n + 1]
        x, y, c = _mesh_pos()
        me = 2 * x + y
        for t in range(n):
            for k, (px, py) in enumerate(_other_chips(x, y)):
                cp = pltpu.make_async_remote_copy(src_ref=s_refs[t].at[2 * px + py], dst_ref=l_refs[t].at[2 * px + py],
                                                  send_sem=send_sems.at[3 * t + k], recv_sem=recv_sems.at[3 * t + k],
                                                  device_id=(x, y, c), device_id_type=MESH)
                cp.wait_send()
                cp.wait_recv()

    hbm = [pltpu.HBM(p.shape, p.dtype) for p in parts]
    outs = pl.pallas_call(
        body,
        name="grad_scatter_wait",
        out_shape=(*hbm, *hbm),
        in_specs=[_HBM] * (2 * n) + [_SEM, _SEM, _ANY],
        out_specs=[_HBM] * (2 * n),
        input_output_aliases={i: i for i in range(2 * n)},
        compiler_params=pltpu.CompilerParams(has_side_effects=pltpu.SideEffectType.DATAFLOW_SIDE_EFFECTING),
    )(*parts, *lands, send_sems, recv_sems, after)
    return list(outs[:n]), list(outs[n:])


def _late_gather_start(shards):
    n = len(shards)

    def body(*refs):
        w_refs, l_refs = refs[:n], refs[n:2 * n]
        send_sems, recv_sems = refs[2 * n], refs[2 * n + 1]
        token = refs[-1]
        x, y, c = _mesh_pos()
        me = 2 * x + y
        for t in range(n):
            for k, (px, py) in enumerate(_other_chips(x, y)):
                pltpu.make_async_remote_copy(src_ref=w_refs[t], dst_ref=l_refs[t].at[me],
                                             send_sem=send_sems.at[3 * t + k], recv_sem=recv_sems.at[3 * t + k],
                                             device_id=(px, py, c), device_id_type=MESH).start()
        token[...] = jnp.zeros_like(token)

    src = [pltpu.HBM(s.shape, s.dtype) for s in shards]
    land = [pltpu.HBM((4,) + s.shape, s.dtype) for s in shards]
    outs = pl.pallas_call(
        body,
        name="late_gather_start",
        out_shape=(pltpu.SemaphoreType.DMA((3 * n,)), pltpu.SemaphoreType.DMA((3 * n,)), *src, *land,
                   jax.ShapeDtypeStruct((8, LANES), F32)),
        in_specs=[_HBM] * (2 * n),
        out_specs=(_SEM, _SEM, *([_HBM] * (2 * n)), pl.BlockSpec(memory_space=pltpu.VMEM)),
        input_output_aliases={i: 2 + i for i in range(2 * n)},
        compiler_params=pltpu.CompilerParams(has_side_effects=pltpu.SideEffectType.DATAFLOW_SIDE_EFFECTING),
    )(*[_in_hbm(s) for s in shards], *[_in_hbm(lax.empty((4,) + s.shape, s.dtype)) for s in shards])
    return outs[0], outs[1], list(outs[2:2 + n]), list(outs[2 + n:2 + 2 * n]), outs[-1]


def _late_gather_wait(send_sems, recv_sems, shards, lands, after):
    n = len(shards)

    def body(*refs):
        w_refs, l_refs = refs[:n], refs[n:2 * n]
        send_sems, recv_sems = refs[2 * n], refs[2 * n + 1]
        x, y, c = _mesh_pos()
        for t in range(n):
            for k, (px, py) in enumerate(_other_chips(x, y)):
                cp = pltpu.make_async_remote_copy(src_ref=w_refs[t], dst_ref=l_refs[t].at[2 * px + py],
                                                  send_sem=send_sems.at[3 * t + k], recv_sem=recv_sems.at[3 * t + k],
                                                  device_id=(x, y, c), device_id_type=MESH)
                cp.wait_send()
                cp.wait_recv()

    src = [pltpu.HBM(s.shape, s.dtype) for s in shards]
    land = [pltpu.HBM(l.shape, l.dtype) for l in lands]
    outs = pl.pallas_call(
        body,
        name="late_gather_wait",
        out_shape=(*src, *land),
        in_specs=[_HBM] * (2 * n) + [_SEM, _SEM, _ANY],
        out_specs=[_HBM] * (2 * n),
        input_output_aliases={i: i for i in range(2 * n)},
        compiler_params=pltpu.CompilerParams(has_side_effects=pltpu.SideEffectType.DATAFLOW_SIDE_EFFECTING),
    )(*shards, *lands, send_sems, recv_sems, after)
    return list(outs[n:])


def _all_to_all_small(parts):
    n = len(parts)

    def body(*refs):
        p_refs, o_refs = refs[:n], refs[n:2 * n]
        send_sems, recv_sems = refs[2 * n:]
        x, y, c = _mesh_pos()
        me = 4 * x + 2 * y + c
        sends = []
        for t in range(n):
            for k in range(1, 8):
                px, py, pc = x ^ (k >> 2), y ^ ((k >> 1) & 1), c ^ (k & 1)
                cp = pltpu.make_async_remote_copy(src_ref=p_refs[t], dst_ref=o_refs[t].at[me],
                                                  send_sem=send_sems.at[7 * t + k - 1], recv_sem=recv_sems.at[7 * t + k - 1],
                                                  device_id=(px, py, pc), device_id_type=MESH)
                cp.start()
                sends.append(cp)
        for t in range(n):
            for k in range(1, 8):
                peer = 4 * (x ^ (k >> 2)) + 2 * (y ^ ((k >> 1) & 1)) + (c ^ (k & 1))
                pltpu.make_async_remote_copy(src_ref=p_refs[t], dst_ref=o_refs[t].at[peer],
                                             send_sem=send_sems.at[7 * t + k - 1], recv_sem=recv_sems.at[7 * t + k - 1],
                                             device_id=(x, y, c), device_id_type=MESH).wait_recv()
        for cp in sends:
            cp.wait_send()

    return pl.pallas_call(
        body,
        out_shape=[jax.ShapeDtypeStruct((8,) + p.shape, p.dtype) for p in parts],
        in_specs=[_ANY] * n,
        out_specs=[_ANY] * n,
        scratch_shapes=[pltpu.SemaphoreType.DMA((7 * n,)), pltpu.SemaphoreType.DMA((7 * n,))],
        name="grad_small_all_to_all",
    )(*parts)


def _sum_devices_small(landed, own):
    n = len(landed)

    def body(*refs):
        x, y, c = _mesh_pos()
        me = 4 * x + 2 * y + c
        for t in range(n):
            acc = jnp.where(me == 0, refs[n + t][...], refs[t][0])
            for d in range(1, 8):
                acc = acc + jnp.where(me == d, refs[n + t][...], refs[t][d])
            refs[2 * n + t][...] = acc

    return pl.pallas_call(
        body,
        out_shape=[jax.ShapeDtypeStruct(p.shape, F32) for p in own],
        in_specs=[_VMEM_WHOLE] * (2 * n),
        out_specs=[_VMEM_WHOLE] * n,
        compiler_params=_params_whole(list(landed) + 2 * list(own)),
        name="grad_sum_devices_small",
    )(*landed, *own)


def _pair_join(fs):
    n = len(fs)

    def body(*refs):
        f_refs, o_refs = refs[:n], refs[n:2 * n]
        send_sems, recv_sems = refs[2 * n:]
        x, y, c = _mesh_pos()
        sends = []
        for t in range(n):
            cp = pltpu.make_async_remote_copy(src_ref=f_refs[t].at[c], dst_ref=o_refs[t].at[c], send_sem=send_sems.at[t],
                                              recv_sem=recv_sems.at[t], device_id=(x, y, 1 - c), device_id_type=MESH)
            cp.start()
            sends.append(cp)
        for t in range(n):
            pltpu.make_async_remote_copy(src_ref=f_refs[t].at[c], dst_ref=o_refs[t].at[1 - c], send_sem=send_sems.at[t],
                                         recv_sem=recv_sems.at[t], device_id=(x, y, c), device_id_type=MESH).wait_recv()
        for cp in sends:
            cp.wait_send()

    return pl.pallas_call(
        body,
        out_shape=[jax.ShapeDtypeStruct(f.shape, f.dtype) for f in fs],
        in_specs=[_ANY] * n,
        out_specs=[_ANY] * n,
        input_output_aliases={t: t for t in range(n)},
        scratch_shapes=[pltpu.SemaphoreType.DMA((n,)), pltpu.SemaphoreType.DMA((n,))],
        name="grad_pair_join",
    )(*fs)


def _rope_tables(Lp):
    inv = 1.0 / (ROPE_BASE ** (jnp.arange(0, ROPE, 2, dtype=F32) / ROPE))
    ang = (jnp.arange(Lp, dtype=F32) - FRONT)[:, None] * inv[None, :]
    cs, sn = jnp.cos(ang), jnp.sin(ang)
    return jnp.tile(cs, (1, 4)), jnp.concatenate([-sn, sn, -sn, sn], axis=1)


def _local_step(x, loss_target, meta, norm_g, w_in, gate_w, gate_b, gla_norm_g, gla_proj, q_norm_g, w_uq,
                kv_norm_g, w_ukv, mla_proj, w_out, final_norm_g, early_grads_hook=None, late_weights_hook=None):
    B, seq, _ = x.shape
    Lp = HEAD_ROWS + seq
    T = B * Lp
    tr = _div_tile(Lp, 768, 16)
    tkw = _div_tile(T, Lp, QB)
    tm_sq = _div_tile(T, 1024, QB)

    cuts = np.cumsum((0,) + SPLITS)
    shard_w = IN_WIDTH // 4

    def w_cols(i, width=None):
        parts = []
        for j in range(4):
            a, b = max(cuts[i], j * shard_w), min(cuts[i + 1], (j + 1) * shard_w)
            if a < b:
                parts.append(w_in[j][:, a - j * shard_w:b - j * shard_w])
        if width is not None:
            parts.append(jnp.zeros((D, width - (cuts[i + 1] - cuts[i])), w_in.dtype))
        return parts

    i_q, i_k, i_v, i_lr, i_z, i_cq, i_ckv, i_kr, i_mz, i_gg, i_gm = range(11)
    wA = jnp.concatenate(sum([w_cols(i) for i in (i_v, i_z, i_mz, i_gg, i_gm, i_q, i_k)], []), axis=1)
    wB = jnp.concatenate(w_cols(i_cq) + w_cols(i_ckv) + w_cols(i_lr, 128) + w_cols(i_kr, 128), axis=1)
    gn4 = jnp.tile(gla_norm_g, (1, GLA_H))
    cos_t, sin_t = _rope_tables(Lp)

    u = _rms_in(x, meta, norm_g, B, Lp)
    projA = _mm(u, wA, name="in_proj_a", out_dtype=BF16, tm=tkw, tn=1024, tk=D)
    projB = _mm(u, wB, name="in_proj_b", out_dtype=BF16, tm=tkw, tn=640, tk=D)
    if late_weights_hook is not None:
        gate_w, gla_proj, w_uq, w_ukv, mla_proj, w_out = late_weights_hook(projA)
    wg = jnp.pad(gate_w, ((0, 128 - GLA_RANK), (0, 0)))
    wuq2 = jnp.pad(w_uq.reshape(Q_RANK, MLA_H, MLA_QK), ((0, 0), (0, 0), (0, 256 - MLA_QK))).reshape(Q_RANK, 2048)
    oa, ya_in, ssave = _gla_fwd(projA, projB, wg, gate_b, gn4, B, Lp)
    ya = _mm(ya_in, gla_proj, name="gla_proj", out_dtype=BF16, tm=tm_sq, tn=D, tk=D)
    q_att, k_att, v_att, cqn, ckvn = _mla_prep(projB, cos_t, sin_t, q_norm_g, kv_norm_g, wuq2, w_ukv, B, Lp, tr)
    ob, yb_in, lse_c = _attn_fwd(q_att, k_att, v_att, projA, B, Lp)
    yb = _mm(yb_in, mla_proj, name="mla_proj", out_dtype=BF16, tm=tm_sq, tn=D, tk=D)
    dh1_b, merged, loss, d_gf = _out_proj_loss(x, meta, projA, ya, yb, w_out, final_norm_g.reshape(1, D),
                                                loss_target, B, Lp)

    g_w_out = _mm(merged, dh1_b, name="dw_out", trans_a=True, out_dtype=BF16, tm=D, tn=D, tk=tkw)
    dya, dyb, dA = _merge_bwd(dh1_b, w_out, projA, ya, yb, tr)
    g_gla_proj = _mm(ya_in, dya, name="dw_gla_proj", trans_a=True, out_dtype=BF16, tm=D, tn=D, tk=tkw)
    g_mla_proj = _mm(yb_in, dyb, name="dw_mla_proj", trans_a=True, out_dtype=BF16, tm=D, tn=D, tk=tkw)
    doa, dBz, d_gn = _gla_out_bwd(dya, gla_proj, oa, projA, gn4, tr)
    dC, g_wg, d_bg = _gla_bwd(projA, projB, ssave, doa, wg, gate_b, B, Lp)
    do, dDz, delta_c = _attn_bwd_pre(dyb, mla_proj, projA, ob, B, Lp)
    dq, dk, dv = _attn_bwd(q_att, k_att, v_att, do, lse_c, delta_c, B, Lp)
    dqf, dkvf, dE, d_gq, d_gkv = _mla_bwd_post(dq, dk, dv, projB, cos_t, sin_t, q_norm_g, kv_norm_g,
                                                wuq2, w_ukv, B, Lp, tr)
    g_wuq2 = _mm(cqn, dqf, name="dw_uq", trans_a=True, out_dtype=BF16, tm=Q_RANK, tn=2048, tk=tkw)
    g_wukv = _mm(ckvn, dkvf, name="dw_ukv", trans_a=True, out_dtype=BF16, tm=KV_RANK, tn=2048, tk=tkw)
    dparts = [dA, dBz, dC, dDz, dE]
    g_in = [_mm(u, dp, name="dw_in_%d" % i, trans_a=True, out_dtype=BF16, tm=D, tn=_div_tile(dp.shape[1], 1024, 256), tk=tkw)
            for i, dp in enumerate(dparts)]

    gA, gBz, gC, gDz, gE = g_in
    src = [(gC, 1024), (gC, 1536), (gC, 0), (gC, 2048), (gBz, 0), (gE, 0), (gE, Q_RANK), (gE, 384), (gDz, 0),
           (gA, 0), (gA, D)]
    owners = []
    for j in range(4):
        parts = []
        for i, (arr, off) in enumerate(src):
            a, b = max(cuts[i], j * shard_w), min(cuts[i + 1], (j + 1) * shard_w)
            if a < b:
                parts.append(arr[:, off + a - cuts[i]:off + b - cuts[i]])
        owners.append(jnp.concatenate(parts, axis=1))
    g_w_in = jnp.stack(owners)
    g_wuq = g_wuq2.reshape(Q_RANK, MLA_H, 256)[:, :, :MLA_QK].reshape(Q_RANK, MLA_H * MLA_QK)
    grads = dict(w_in=g_w_in, gla_gate_w=g_wg[:GLA_RANK], gla_proj=g_gla_proj, mla_w_uq=g_wuq, mla_w_ukv=g_wukv,
                 mla_proj=g_mla_proj, w_out=g_w_out, gla_gate_b=d_bg,
                 gla_norm_g=d_gn, mla_q_norm_g=d_gq, mla_kv_norm_g=d_gkv, final_norm_g=d_gf)
    token = None if early_grads_hook is None else early_grads_hook(grads)
    ng = norm_g if token is None else norm_g + token[0:1, 0:1]
    grad_x, d_meta, d_ng = _in_proj_bwd(x, meta, dh1_b, dA, dBz, dC, dDz, dE, wA, wB, ng, B, Lp)
    grads.update(meta_tokens=d_meta, norm_g=d_ng)
    return loss[0, 0], grad_x, grads


_MATS = ("w_in", "gla_gate_w", "gla_proj", "mla_w_uq", "mla_w_ukv", "mla_proj", "w_out")
_ROW_SHARDED = ("gla_proj", "mla_proj", "w_out")
_ORDER = ("meta_tokens", "norm_g", "w_in", "gla_gate_w", "gla_gate_b", "gla_norm_g", "gla_proj", "mla_q_norm_g",
          "mla_w_uq", "mla_kv_norm_g", "mla_w_ukv", "mla_proj", "w_out", "final_norm_g")
WIRE_BF16_MIN_ELEMS = 128 * 128
SMALL_PACK_ROWS = 16


def _pack_small(d, scalar=None):
    rows = [jnp.pad(d[n].reshape(1, size), ((0, 0), (0, D - size))) for n, size in SMALL]
    if scalar is not None:
        rows.append(jnp.pad(scalar.reshape(1, 1), ((0, 0), (0, D - 1))))
    return jnp.pad(jnp.concatenate(rows, axis=0), ((0, SMALL_PACK_ROWS - len(rows)), (0, 0)))


def _unpack_small(packed):
    return {n: packed[i, :size] for i, (n, size) in enumerate(SMALL)}


def kernel(x, meta_tokens, norm_g, w_in, gla_gate_w, gla_gate_b, gla_norm_g, gla_proj, mla_q_norm_g, mla_w_uq, mla_kv_norm_g, mla_w_ukv, mla_proj, w_out, final_norm_g, loss_target, m_meta_tokens, m_norm_g, m_w_in, m_gla_gate_w, m_gla_gate_b, m_gla_norm_g, m_gla_proj, m_mla_q_norm_g, m_mla_w_uq, m_mla_kv_norm_g, m_mla_w_ukv, m_mla_proj, m_w_out, m_final_norm_g, v_meta_tokens, v_norm_g, v_w_in, v_gla_gate_w, v_gla_gate_b, v_gla_norm_g, v_gla_proj, v_mla_q_norm_g, v_mla_w_uq, v_mla_kv_norm_g, v_mla_w_ukv, v_mla_proj, v_w_out, v_final_norm_g):
    w = dict(meta_tokens=meta_tokens, norm_g=norm_g, w_in=w_in[0], gla_gate_w=gla_gate_w[0], gla_gate_b=gla_gate_b,
             gla_norm_g=gla_norm_g, gla_proj=gla_proj[0], mla_q_norm_g=mla_q_norm_g, mla_w_uq=mla_w_uq[0],
             mla_kv_norm_g=mla_kv_norm_g, mla_w_ukv=mla_w_ukv[0], mla_proj=mla_proj[0], w_out=w_out[0],
             final_norm_g=final_norm_g)
    mom = dict(meta_tokens=m_meta_tokens, norm_g=m_norm_g, w_in=m_w_in[0], gla_gate_w=m_gla_gate_w[0],
               gla_gate_b=m_gla_gate_b, gla_norm_g=m_gla_norm_g, gla_proj=m_gla_proj[0], mla_q_norm_g=m_mla_q_norm_g,
               mla_w_uq=m_mla_w_uq[0], mla_kv_norm_g=m_mla_kv_norm_g, mla_w_ukv=m_mla_w_ukv[0], mla_proj=m_mla_proj[0],
               w_out=m_w_out[0], final_norm_g=m_final_norm_g)
    var = dict(meta_tokens=v_meta_tokens, norm_g=v_norm_g, w_in=v_w_in[0], gla_gate_w=v_gla_gate_w[0],
               gla_gate_b=v_gla_gate_b, gla_norm_g=v_gla_norm_g, gla_proj=v_gla_proj[0], mla_q_norm_g=v_mla_q_norm_g,
               mla_w_uq=v_mla_w_uq[0], mla_kv_norm_g=v_mla_kv_norm_g, mla_w_ukv=v_mla_w_ukv[0], mla_proj=v_mla_proj[0],
               w_out=v_w_out[0], final_norm_g=v_final_norm_g)
    out_shapes = {n: a.shape for n, a in zip(_ORDER, (meta_tokens, norm_g, w_in, gla_gate_w, gla_gate_b, gla_norm_g,
                                                     gla_proj, mla_q_norm_g, mla_w_uq, mla_kv_norm_g, mla_w_ukv,
                                                     mla_proj, w_out, final_norm_g))}

    me = (2 * lax.axis_index("x") + lax.axis_index("y")).astype(jnp.int32)
    is_mine = lax.broadcasted_iota(jnp.int32, (4, 1, 1), 0) == me
    with_own = lambda gth, own: jnp.where(is_mine, own[None], gth)
    first = [w["w_in"].astype(BF16), meta_tokens]
    w_in_owner, meta_owner = [with_own(gth, own) for gth, own in zip(_weight_gather(first), first)]
    meta_full = meta_owner.transpose(1, 0, 2).reshape(N_META, D)
    late_names = _MATS[1:]
    late = [w[n].astype(BF16) for n in late_names]
    gather_sems = _late_gather_start(late)

    def late_weights(after):
        lands = _late_gather_wait(gather_sems[0], gather_sems[1], gather_sems[2], gather_sems[3], after)
        full = []
        for name, land, own in zip(late_names, lands, late):
            gth = with_own(land, own)
            if name in _ROW_SHARDED:
                full.append(gth.reshape(4 * gth.shape[1], gth.shape[2]))
            else:
                full.append(gth.transpose(1, 0, 2).reshape(gth.shape[1], 4 * gth.shape[2]))
        return full

    def by_owner(name, arr):
        if name == "w_in":
            return arr
        if name in _ROW_SHARDED:
            return arr.reshape(4, arr.shape[0] // 4, arr.shape[1])
        return arr.reshape(arr.shape[0], 4, arr.shape[1] // 4).transpose(1, 0, 2)

    c_idx = lax.axis_index("c").astype(jnp.int32).reshape(1)
    pos = jnp.stack([c_idx[0], me])
    in_flight = {}

    def start_matrix_reduce(early):
        gps = [by_owner(n, early[n]) for n in _MATS]
        recvs = _pair_swap(gps)
        s1 = [_pair_add_big(gps[0], recvs[0], c_idx)] + list(_pair_add_small(gps[1:], recvs[1:]))
        send_sems, recv_sems, parts, lands, token = _chip_scatter_start(s1)
        in_flight.update(send_sems=send_sems, recv_sems=recv_sems, parts=parts, lands=lands)
        return token

    norm_g_after_start = norm_g + gather_sems[4][0:1, 0:1]
    loss_local, grad_x, g = _local_step(
        x, loss_target, meta_full, norm_g_after_start, w_in_owner, None, gla_gate_b, gla_norm_g, None,
        mla_q_norm_g, None, mla_kv_norm_g, None, None, None, final_norm_g,
        early_grads_hook=start_matrix_reduce, late_weights_hook=late_weights)

    s1, landed = _chip_scatter_wait(in_flight["send_sems"], in_flight["recv_sems"], in_flight["parts"],
                                    in_flight["lands"], after=g["norm_g"])
    halves = [_sum_chips_big(landed[0], s1[0], pos)] + list(_sum_chips_small(landed[1:], s1[1:]))
    g_mats = [j.reshape(out_shapes[n]) for j, n in zip(_pair_join(halves), _MATS)]

    late = [g["meta_tokens"], _pack_small(g, scalar=loss_local)]
    meta_sum, small_sum = _sum_devices_small(_all_to_all_small(late), late)
    loss = small_sum[len(SMALL), 0]
    g_meta = lax.dynamic_slice(meta_sum, (0, me * (D // 4)), (N_META, D // 4))
    names = _MATS + ("meta_tokens",)
    g_red = g_mats + [g_meta, small_sum]

    tens = lambda d: [d[n].reshape(out_shapes[n]) for n in names] + [_pack_small(d)]
    w_t, m_t, v_t = tens(w), tens(mom), tens(var)
    big = _adamw_big(w_t[0], g_red[0], m_t[0], v_t[0])
    rest = _adamw_small(w_t[1:], g_red[1:], m_t[1:], v_t[1:])
    k = len(names)
    results = {"grad": g_red}
    for i, kind in enumerate(("delta", "new_m", "new_v")):
        results[kind] = [big[i]] + list(rest[i * k:(i + 1) * k])

    outs = []
    for kind in ("grad", "delta", "new_m", "new_v"):
        vals = dict(zip(names, results[kind][:-1]))
        vals.update(_unpack_small(results[kind][-1]))
        outs += [vals[n].reshape(out_shapes[n]) for n in _ORDER]
    return (loss, grad_x, *outs)
```

```python
import functools
import math

import jax
import jax.numpy as jnp
import numpy as np
from jax import lax
from jax.experimental import pallas as pl
from jax.experimental.pallas import tpu as pltpu

F32 = jnp.float32
BF16 = jnp.bfloat16

D = 1024
N_META = 16
QB = 256
FRONT = QB - N_META
HEAD_ROWS = FRONT + N_META
assert FRONT % 64 == 48
EPS = 1e-6

GLA_H, GLA_DK, GLA_DV, GLA_RANK, GLA_C = 4, 128, 256, 16, 64
GLA_NORMALIZER = 16.0
GLA_KW, GLA_VW = GLA_H * GLA_DK, GLA_H * GLA_DV
MLA_H, NOPE, ROPE, MLA_DV, Q_RANK, KV_RANK = 8, 128, 64, 128, 256, 128
MLA_QK = NOPE + ROPE
ROPE_BASE = 10000.0
SPLITS = (GLA_KW, GLA_KW, GLA_VW, GLA_RANK, GLA_VW, Q_RANK, KV_RANK, ROPE, MLA_H * MLA_DV, D, D)
IN_WIDTH = sum(SPLITS)

ADAM_LR, ADAM_B1, ADAM_B2, ADAM_EPS, ADAM_WD, ADAM_STEP = 0.001, 0.9, 0.999, 1e-08, 0.01, 10

LANES = 128
VMEM_CAP_V7X = 56 * 1024 * 1024
MESH = pl.DeviceIdType.MESH
NEG = -1e30
LOG2E = math.log2(math.e)

SMALL = (("norm_g", D), ("gla_gate_b", GLA_KW), ("gla_norm_g", GLA_DV), ("mla_q_norm_g", Q_RANK),
         ("mla_kv_norm_g", KV_RANK), ("final_norm_g", D))


def _div_tile(n, target, mult):
    best = None
    for d in range(mult, min(n, target) + 1, mult):
        if n % d == 0:
            best = d
    assert best is not None, (n, target, mult)
    return best


def _params(sem, block_bytes, scratch_bytes=0):
    est = 2 * block_bytes + scratch_bytes + 12 * 1024 * 1024
    return pltpu.CompilerParams(dimension_semantics=sem, vmem_limit_bytes=int(min(max(est, 24 * 1024 * 1024), VMEM_CAP_V7X)))


def _nbytes(shape, dtype):
    return int(np.prod(shape)) * jnp.dtype(dtype).itemsize


def _sigmoid(x):
    return 1.0 / (1.0 + jnp.exp(-x))


def _nt(a, b):
    return lax.dot_general(a, b, (((1,), (1,)), ((), ())), preferred_element_type=F32)


def _tn(a, b):
    return lax.dot_general(a, b, (((0,), (0,)), ((), ())), preferred_element_type=F32)


def _nn(a, b):
    return jnp.dot(a, b, preferred_element_type=F32)


def _split2(x):
    a = x.astype(BF16)
    b = (x - a.astype(F32)).astype(BF16)
    return a, b


def _mm(a, b, *, name, trans_a=False, trans_b=False, out_dtype=F32, tm, tn, tk):
    assert not (trans_a and trans_b)
    if trans_a:
        K, M = a.shape
    else:
        M, K = a.shape
    N = b.shape[0] if trans_b else b.shape[1]
    assert (b.shape[1] if trans_b else b.shape[0]) == K
    assert M % tm == 0 and N % tn == 0 and K % tk == 0, (name, M, N, K, tm, tn, tk)
    nk = K // tk

    def body(a_ref, b_ref, o_ref, *scratch):
        av = a_ref[...].astype(BF16)
        bv = b_ref[...].astype(BF16)
        prod = _tn(av, bv) if trans_a else (_nt(av, bv) if trans_b else _nn(av, bv))
        if nk == 1:
            o_ref[...] = prod.astype(out_dtype)
        else:
            acc = scratch[0]
            k = pl.program_id(2)

            @pl.when(k == 0)
            def _():
                acc[...] = prod

            @pl.when(k > 0)
            def _():
                acc[...] += prod

            @pl.when(k == nk - 1)
            def _():
                o_ref[...] = acc[...].astype(out_dtype)

    if trans_a:
        a_spec = pl.BlockSpec((tk, tm), lambda i, j, k: (k, i))
    else:
        a_spec = pl.BlockSpec((tm, tk), lambda i, j, k: (i, k))
    if trans_b:
        b_spec = pl.BlockSpec((tn, tk), lambda i, j, k: (j, k))
    else:
        b_spec = pl.BlockSpec((tk, tn), lambda i, j, k: (k, j))
    blocks = (_nbytes((tm, tk), a.dtype) + _nbytes((tk, tn), b.dtype) + _nbytes((tm, tn), out_dtype))
    scratch = [pltpu.VMEM((tm, tn), F32)] if nk > 1 else []
    return pl.pallas_call(
        body,
        out_shape=jax.ShapeDtypeStruct((M, N), out_dtype),
        grid=(M // tm, N // tn, nk),
        in_specs=[a_spec, b_spec],
        out_specs=pl.BlockSpec((tm, tn), lambda i, j, k: (i, j)),
        scratch_shapes=scratch,
        compiler_params=_params(("parallel", "parallel", "arbitrary"), blocks + _nbytes((tm, tn), F32),
                                _nbytes((tm, tn), F32) if nk > 1 else 0),
        name=name,
    )(a, b)


def _h_tile(j, x_ref, meta_ref):
    head = jnp.concatenate([jnp.zeros((FRONT, D), F32), meta_ref[...]], axis=0)
    return jnp.where(j > 0, x_ref[0], head)


def _x_spec():
    return pl.BlockSpec((1, QB, D), lambda b, j: (b, jnp.maximum(j - 1, 0), 0))


def _rms_in(x, meta, g, B, Lp):
    T = B * Lp
    NQ = Lp // QB

    def body(x_ref, meta_ref, g_ref, u_ref):
        h = _h_tile(pl.program_id(1), x_ref, meta_ref)
        r = lax.rsqrt(jnp.mean(h * h, axis=-1, keepdims=True) + EPS)
        u_ref[...] = (h * r * g_ref[...]).astype(BF16)

    return pl.pallas_call(
        body,
        out_shape=jax.ShapeDtypeStruct((T, D), BF16),
        grid=(B, NQ),
        in_specs=[_x_spec(), pl.BlockSpec((N_META, D), lambda b, j: (0, 0)), pl.BlockSpec((1, D), lambda b, j: (0, 0))],
        out_specs=pl.BlockSpec((QB, D), lambda b, j: (b * NQ + j, 0)),
        compiler_params=_params(("parallel", "parallel"), _nbytes((QB, D), F32) * 2),
        name="rms_in",
    )(x, meta, g)


def _gla_gate(lr, wg, bg, valid):
    pre = _nn(lr.astype(BF16), wg) + bg
    logsig = jnp.minimum(pre, 0.0) - jnp.log(1.0 + jnp.exp(-jnp.abs(pre)))
    return pre, jnp.where(valid, logsig / GLA_NORMALIZER, 0.0)


def _tri_masks():
    ri = lax.broadcasted_iota(jnp.int32, (GLA_C, GLA_C), 0)
    ci = lax.broadcasted_iota(jnp.int32, (GLA_C, GLA_C), 1)
    return ci <= ri, ci >= ri


def _cumsum_rows(x, ones_mask):
    w = jnp.where(ones_mask, 1.0, 0.0).astype(BF16)
    a, b = _split2(x)
    return _nn(w, a) + _nn(w, b)


def _gla_fwd(projA, projB, wg, bg, gn4, B, Lp):
    T = B * Lp
    NC = Lp // GLA_C
    C = GLA_C
    scale = GLA_DK ** -0.5

    def body(q_ref, k_ref, v_ref, lr_ref, z_ref, wg_ref, bg_ref, gn_ref, oa_ref, ya_ref, ssave_ref, st_ref):
        n = pl.program_id(0)

        @pl.when(n == 0)
        def _():
            st_ref[...] = jnp.zeros_like(st_ref)

        pos = n * C + lax.broadcasted_iota(jnp.int32, (C, 1), 0)
        lower, _ = _tri_masks()
        is_last = lax.broadcasted_iota(jnp.int32, (C, 1), 0) == C - 1
        for b in range(B):
            ssave_ref[b, 0] = st_ref[b]
            _, glog = _gla_gate(lr_ref[b], wg_ref[...], bg_ref[...], pos >= FRONT)
            bcum = _cumsum_rows(glog, lower)
            for h in range(GLA_H):
                ks = slice(h * GLA_DK, (h + 1) * GLA_DK)
                vs = slice(h * GLA_DV, (h + 1) * GLA_DV)
                bh = bcum[:, ks]
                blast = jnp.sum(jnp.where(is_last, bh, 0.0), axis=0, keepdims=True)
                qh = q_ref[b, :, ks].astype(F32) * scale
                kh = k_ref[b, :, ks].astype(F32)
                qe = (qh * jnp.exp(bh)).astype(BF16)
                ke = (kh * jnp.exp(-bh)).astype(BF16)
                kl = (kh * jnp.exp(blast - bh)).astype(BF16)
                vh = v_ref[b, :, vs].astype(BF16)
                a = jnp.where(lower, _nt(qe, ke), 0.0).astype(BF16)
                st = st_ref[b, h]
                o = _nn(a, vh) + _nt(qe, st.astype(BF16))
                st_ref[b, h] = st * jnp.exp(blast) + _tn(vh, kl)
                oa_ref[b, :, vs] = o.astype(BF16)
                on = o * lax.rsqrt(jnp.mean(o * o, axis=-1, keepdims=True) + EPS) * gn_ref[:, vs]
                z = z_ref[b, :, vs].astype(F32)
                ya_ref[b, :, vs] = (on * (z * _sigmoid(z))).astype(BF16)

    blocks = B * (_nbytes((C, 512), F32) * 2 + _nbytes((C, 1024), F32) * 3 + _nbytes((C, 1024), BF16)
                  + _nbytes((GLA_H, GLA_DV, GLA_DK), F32)) + _nbytes((128, 512), BF16)
    state = _nbytes((B, GLA_H, GLA_DV, GLA_DK), F32)
    pa = projA.reshape(B, Lp, projA.shape[1])
    oa, ya, ssave = pl.pallas_call(
        body,
        out_shape=(jax.ShapeDtypeStruct((B, Lp, GLA_VW), BF16), jax.ShapeDtypeStruct((B, Lp, GLA_VW), BF16),
                   jax.ShapeDtypeStruct((B, NC, GLA_H, GLA_DV, GLA_DK), F32)),
        grid=(NC,),
        in_specs=[
            pl.BlockSpec((B, C, 512), lambda n: (0, n, 10)),
            pl.BlockSpec((B, C, 512), lambda n: (0, n, 11)),
            pl.BlockSpec((B, C, 1024), lambda n: (0, n, 0)),
            pl.BlockSpec((B, C, 128), lambda n: (0, n, 3)),
            pl.BlockSpec((B, C, 1024), lambda n: (0, n, 1)),
            pl.BlockSpec((128, 512), lambda n: (0, 0)),
            pl.BlockSpec((1, 512), lambda n: (0, 0)),
            pl.BlockSpec((1, 1024), lambda n: (0, 0)),
        ],
        out_specs=(pl.BlockSpec((B, C, 1024), lambda n: (0, n, 0)),
                   pl.BlockSpec((B, C, 1024), lambda n: (0, n, 0)),
                   pl.BlockSpec((B, 1, GLA_H, GLA_DV, GLA_DK), lambda n: (0, n, 0, 0, 0))),
        scratch_shapes=[pltpu.VMEM((B, GLA_H, GLA_DV, GLA_DK), F32)],
        compiler_params=_params(("arbitrary",), blocks, state),
        name="gla_fwd",
    )(pa, pa, pa, projB.reshape(B, Lp, projB.shape[1]), pa, wg, bg, gn4)
    return oa.reshape(T, GLA_VW), ya.reshape(T, GLA_VW), ssave


def _swap_halves(x):
    lane = lax.broadcasted_iota(jnp.int32, x.shape, 1)
    return jnp.where((lane % 64) < 32, pltpu.roll(x, 96, 1), pltpu.roll(x, 32, 1))


def _mla_prep(projB, cos_t, sin_t, gq, gkv, wuq2, wukv, B, Lp, tr):
    T = B * Lp
    nt = Lp // tr
    HW = 2 * LANES

    def body(pb_ref, cos_ref, sin_ref, gq_ref, gkv_ref, wuq_ref, wukv_ref, q_ref, k_ref, v_ref, cqn_ref, ckvn_ref):
        cq = pb_ref[:, 0:Q_RANK].astype(F32)
        ckv = pb_ref[:, Q_RANK:Q_RANK + KV_RANK].astype(F32)
        kr = pb_ref[:, 512:640].astype(F32)
        cqn = (cq * lax.rsqrt(jnp.mean(cq * cq, axis=-1, keepdims=True) + EPS) * gq_ref[...]).astype(BF16)
        ckvn = (ckv * lax.rsqrt(jnp.mean(ckv * ckv, axis=-1, keepdims=True) + EPS) * gkv_ref[...]).astype(BF16)
        cqn_ref[...] = cqn
        ckvn_ref[...] = ckvn
        qf = _nn(cqn, wuq_ref[...])
        kvf = _nn(ckvn, wukv_ref[...])
        cs = cos_ref[...]
        sn = sin_ref[...]
        rope = lambda t: t * cs + _swap_halves(t) * sn
        kr_r = rope(kr).astype(BF16)
        for h in range(MLA_H):
            q_ref[:, h * HW:h * HW + LANES] = qf[:, h * HW:h * HW + LANES].astype(BF16)
            q_ref[:, h * HW + LANES:(h + 1) * HW] = rope(qf[:, h * HW + LANES:(h + 1) * HW]).astype(BF16)
            k_ref[:, h * HW:h * HW + LANES] = kvf[:, h * HW:h * HW + LANES].astype(BF16)
            k_ref[:, h * HW + LANES:(h + 1) * HW] = kr_r
            v_ref[:, h * MLA_DV:(h + 1) * MLA_DV] = kvf[:, h * HW + LANES:(h + 1) * HW].astype(BF16)

    blocks = (_nbytes((tr, 640), F32) + 2 * _nbytes((tr, 128), F32) + _nbytes((Q_RANK, 2048), BF16)
              + _nbytes((KV_RANK, 2048), BF16) + _nbytes((tr, 2048 * 2 + 1024 + 384), BF16)
              + 2 * _nbytes((tr, 2048), F32))
    return pl.pallas_call(
        body,
        out_shape=(jax.ShapeDtypeStruct((T, MLA_H * HW), BF16), jax.ShapeDtypeStruct((T, MLA_H * HW), BF16),
                   jax.ShapeDtypeStruct((T, MLA_H * MLA_DV), BF16), jax.ShapeDtypeStruct((T, Q_RANK), BF16),
                   jax.ShapeDtypeStruct((T, KV_RANK), BF16)),
        grid=(B, nt),
        in_specs=[
            pl.BlockSpec((tr, 640), lambda b, j: (b * nt + j, 0)),
            pl.BlockSpec((tr, 128), lambda b, j: (j, 0)),
            pl.BlockSpec((tr, 128), lambda b, j: (j, 0)),
            pl.BlockSpec((1, Q_RANK), lambda b, j: (0, 0)),
            pl.BlockSpec((1, KV_RANK), lambda b, j: (0, 0)),
            pl.BlockSpec((Q_RANK, 2048), lambda b, j: (0, 0)),
            pl.BlockSpec((KV_RANK, 2048), lambda b, j: (0, 0)),
        ],
        out_specs=(pl.BlockSpec((tr, 2048), lambda b, j: (b * nt + j, 0)),
                   pl.BlockSpec((tr, 2048), lambda b, j: (b * nt + j, 0)),
                   pl.BlockSpec((tr, 1024), lambda b, j: (b * nt + j, 0)),
                   pl.BlockSpec((tr, Q_RANK), lambda b, j: (b * nt + j, 0)),
                   pl.BlockSpec((tr, KV_RANK), lambda b, j: (b * nt + j, 0))),
        compiler_params=_params(("parallel", "parallel"), blocks),
        name="mla_prep",
    )(projB, cos_t, sin_t, gq, gkv, wuq2, wukv)


def _attn_mask(row, col):
    return (col <= row) & ((col >= FRONT) | (row < FRONT))


def _attn_fwd(q_att, k_att, v_att, projA, B, Lp):
    T = B * Lp
    NQ = Lp // QB
    HW = 2 * LANES
    scale = 1.0 / math.sqrt(MLA_QK)

    def body(q_ref, k_ref, v_ref, mz_ref, o_ref, yb_ref, lsec_ref, m_ref, l_ref, acc_ref):
        qi = pl.program_id(1)
        m_ref[...] = jnp.full(m_ref.shape, NEG, F32)
        l_ref[...] = jnp.zeros_like(l_ref)
        acc_ref[...] = jnp.zeros_like(acc_ref)
        row = qi * QB + lax.broadcasted_iota(jnp.int32, (QB, QB), 0)
        coli = lax.broadcasted_iota(jnp.int32, (QB, QB), 1)
        ones = jnp.ones((QB, LANES), BF16)

        def step(kj, masked):
            off = pl.multiple_of(kj * QB, QB)
            ok = _attn_mask(row, kj * QB + coli) if masked else None
            for h in range(MLA_H):
                q = q_ref[:, h * HW:(h + 1) * HW]
                kb = k_ref[pl.ds(off, QB), h * HW:(h + 1) * HW]
                vb = v_ref[pl.ds(off, QB), h * MLA_DV:(h + 1) * MLA_DV]
                s = _nt(q, kb) * (scale * LOG2E)
                if masked:
                    s = jnp.where(ok, s, NEG)
                m_old = m_ref[h]
                m_new = jnp.maximum(m_old, jnp.max(s, axis=-1, keepdims=True))
                alpha = jnp.exp2(m_old - m_new)
                p = jnp.exp2((s - jnp.tile(m_new, (1, QB // LANES))).astype(BF16))
                m_ref[h] = m_new
                l_ref[h] = alpha * l_ref[h] + _nn(p, ones)
                acc_ref[h] = alpha * acc_ref[h] + _nn(p, vb)

        step(0, True)

        def unmasked(kj, carry):
            step(kj, False)
            return carry

        lax.fori_loop(1, qi, unmasked, 0)

        @pl.when(qi > 0)
        def _():
            step(qi, True)

        for h in range(MLA_H):
            hs = slice(h * MLA_DV, (h + 1) * MLA_DV)
            l = l_ref[h]
            o = acc_ref[h] / l
            o_ref[:, hs] = o.astype(BF16)
            z = mz_ref[:, hs].astype(F32)
            yb_ref[:, hs] = (o * (z * _sigmoid(z))).astype(BF16)
            lse2 = m_ref[h] + jnp.log(l) * LOG2E
            lsec_ref[0, h, pl.ds(qi, 1), :] = jnp.transpose(lse2)[0:1, :]

    blocks = (_nbytes((QB, 2048), BF16) + _nbytes((Lp, 2048), BF16) + _nbytes((Lp, 1024), BF16)
              + 2 * _nbytes((QB, 1024), F32) + _nbytes((QB, 1024), BF16) + _nbytes((MLA_H, QB, LANES), F32)
              + _nbytes((MLA_H, NQ, QB), F32))
    return pl.pallas_call(
        body,
        out_shape=(jax.ShapeDtypeStruct((T, MLA_H * MLA_DV), BF16), jax.ShapeDtypeStruct((T, MLA_H * MLA_DV), BF16),
                   jax.ShapeDtypeStruct((B, MLA_H, NQ, QB), F32)),
        grid=(B, NQ),
        in_specs=[
            pl.BlockSpec((QB, MLA_H * HW), lambda b, i: (b * NQ + i, 0)),
            pl.BlockSpec((Lp, MLA_H * HW), lambda b, i: (b, 0)),
            pl.BlockSpec((Lp, MLA_H * MLA_DV), lambda b, i: (b, 0)),
            pl.BlockSpec((QB, 1024), lambda b, i: (b * NQ + i, 2)),
        ],
        out_specs=(pl.BlockSpec((QB, 1024), lambda b, i: (b * NQ + i, 0)),
                   pl.BlockSpec((QB, 1024), lambda b, i: (b * NQ + i, 0)),
                   pl.BlockSpec((1, MLA_H, NQ, QB), lambda b, i: (b, 0, 0, 0))),
        scratch_shapes=[pltpu.VMEM((MLA_H, QB, LANES), F32), pltpu.VMEM((MLA_H, QB, LANES), F32),
                        pltpu.VMEM((MLA_H, QB, MLA_DV), F32)],
        compiler_params=_params(("parallel", "arbitrary"), blocks, 3 * _nbytes((MLA_H, QB, LANES), F32)),
        name="attn_fwd",
    )(q_att, k_att, v_att, projA)


def _out_proj_loss(x, meta, projA, ya, yb, w_out, gf, tgt, B, Lp):
    T = B * Lp
    NQ = Lp // QB

    def body(x_ref, meta_ref, gg_ref, gm_ref, ya_ref, yb_ref, w_ref, gf_ref, t_ref,
             dhb_ref, mg_ref, loss_ref, dgf_ref):
        b = pl.program_id(0)
        j = pl.program_id(1)

        @pl.when((b == 0) & (j == 0))
        def _():
            loss_ref[...] = jnp.zeros_like(loss_ref)
            dgf_ref[...] = jnp.zeros_like(dgf_ref)

        f32 = lambda ref: ref[...].astype(F32)
        merged = (_sigmoid(f32(gg_ref)) * f32(ya_ref) + _sigmoid(f32(gm_ref)) * f32(yb_ref)).astype(BF16)
        mg_ref[...] = merged
        h1 = _h_tile(j, x_ref, meta_ref) + _nn(merged, w_ref[...])
        r = lax.rsqrt(jnp.mean(h1 * h1, axis=-1, keepdims=True) + EPS)
        hn = h1 * r
        gfv = gf_ref[...]
        diff = jnp.where(j > 0, hn * gfv - t_ref[0], 0.0)
        loss_ref[...] += (0.5 / D) * jnp.sum(jnp.sum(diff * diff, axis=-1, keepdims=True), axis=0, keepdims=True)
        dout = diff * (1.0 / D)
        dgf_ref[...] += jnp.sum(dout * hn, axis=0, keepdims=True)
        dhn = dout * gfv
        dh = r * (dhn - hn * jnp.mean(dhn * hn, axis=-1, keepdims=True))
        dhb_ref[...] = dh.astype(BF16)

    rows = lambda c: pl.BlockSpec((QB, D), lambda b, j: (b * NQ + j, c))
    const = lambda s: pl.BlockSpec(s, lambda b, j: (0, 0))
    return pl.pallas_call(
        body,
        out_shape=(jax.ShapeDtypeStruct((T, D), BF16), jax.ShapeDtypeStruct((T, D), BF16),
                   jax.ShapeDtypeStruct((1, 1), F32), jax.ShapeDtypeStruct((1, D), F32)),
        grid=(B, NQ),
        in_specs=[_x_spec(), const((N_META, D)), rows(3), rows(4), rows(0), rows(0), const((D, D)),
                  const((1, D)), _x_spec()],
        out_specs=(rows(0), rows(0), const((1, 1)), const((1, D))),
        compiler_params=_params(("arbitrary", "arbitrary"), 10 * _nbytes((QB, D), F32)),
        name="out_proj_loss",
    )(x, meta, projA, projA, ya, yb, w_out, gf, tgt)


def _merge_bwd(dh1_b, w_out, projA, ya, yb, tr):
    T = dh1_b.shape[0]

    def body(dh_ref, w_ref, gg_ref, gm_ref, ya_ref, yb_ref, dya_ref, dyb_ref, da_ref):
        d = _nt(dh_ref[...], w_ref[...])
        sg = _sigmoid(gg_ref[...].astype(F32))
        sm = _sigmoid(gm_ref[...].astype(F32))
        dya_ref[...] = (d * sg).astype(BF16)
        dyb_ref[...] = (d * sm).astype(BF16)
        da_ref[:, 0:D] = (d * ya_ref[...].astype(F32) * (sg * (1.0 - sg))).astype(BF16)
        da_ref[:, D:2 * D] = (d * yb_ref[...].astype(F32) * (sm * (1.0 - sm))).astype(BF16)

    spec = lambda c: pl.BlockSpec((tr, D), lambda i: (i, c))
    return pl.pallas_call(
        body,
        out_shape=(jax.ShapeDtypeStruct((T, D), BF16), jax.ShapeDtypeStruct((T, D), BF16),
                   jax.ShapeDtypeStruct((T, 2 * D), BF16)),
        grid=(T // tr,),
        in_specs=[spec(0), pl.BlockSpec((D, D), lambda i: (0, 0)), spec(3), spec(4), spec(0), spec(0)],
        out_specs=(spec(0), spec(0), pl.BlockSpec((tr, 2 * D), lambda i: (i, 0))),
        compiler_params=_params(("parallel",), 8 * _nbytes((tr, D), F32)),
        name="merge_bwd",
    )(dh1_b, w_out, projA, projA, ya, yb)


def _gla_out_bwd(dya, gla_proj, oa, projA, gn4, tr):
    T = dya.shape[0]
    nsteps = T // tr

    def body(dya_ref, w_ref, oa_ref, z_ref, gn_ref, do_ref, dz_ref, dgn_ref, acc_ref):
        i = pl.program_id(0)

        @pl.when(i == 0)
        def _():
            acc_ref[...] = jnp.zeros_like(acc_ref)

        dy_all = _nt(dya_ref[...], w_ref[...])
        for h in range(GLA_H):
            vs = slice(h * GLA_DV, (h + 1) * GLA_DV)
            dy = dy_all[:, vs]
            o = oa_ref[:, vs].astype(F32)
            z = z_ref[:, vs].astype(F32)
            gn = gn_ref[:, vs]
            s = _sigmoid(z)
            ra = lax.rsqrt(jnp.mean(o * o, axis=-1, keepdims=True) + EPS)
            on = o * ra
            don = dy * (z * s)
            t = don * gn
            do_ref[:, vs] = (ra * (t - on * jnp.mean(t * on, axis=-1, keepdims=True))).astype(BF16)
            dz_ref[:, vs] = (dy * (on * gn) * (s * (1.0 + z * (1.0 - s)))).astype(BF16)
            acc_ref[:, vs] += jnp.sum(don * on, axis=0, keepdims=True)

        @pl.when(i == nsteps - 1)
        def _():
            a = acc_ref[...]
            dgn_ref[...] = a[:, 0:256] + a[:, 256:512] + a[:, 512:768] + a[:, 768:1024]

    spec = lambda c: pl.BlockSpec((tr, D), lambda i: (i, c))
    return pl.pallas_call(
        body,
        out_shape=(jax.ShapeDtypeStruct((T, D), BF16), jax.ShapeDtypeStruct((T, D), BF16),
                   jax.ShapeDtypeStruct((1, GLA_DV), F32)),
        grid=(nsteps,),
        in_specs=[spec(0), pl.BlockSpec((D, D), lambda i: (0, 0)), spec(0), spec(1),
                  pl.BlockSpec((1, D), lambda i: (0, 0))],
        out_specs=(spec(0), spec(0), pl.BlockSpec((1, GLA_DV), lambda i: (0, 0))),
        scratch_shapes=[pltpu.VMEM((1, D), F32)],
        compiler_params=_params(("arbitrary",), 6 * _nbytes((tr, D), F32)),
        name="gla_out_bwd",
    )(dya, gla_proj, oa, projA, gn4)


def _gla_bwd(projA, projB, ssave, doa, wg, bg, B, Lp):
    T = B * Lp
    NC = Lp // GLA_C
    C = GLA_C
    scale = GLA_DK ** -0.5
    WC = 2304

    def body(q_ref, k_ref, v_ref, lr_ref, ss_ref, do_ref, wg_ref, bg_ref, dc_ref, dwg_ref, dbg_ref, dst_ref):
        i = pl.program_id(0)
        n = NC - 1 - i

        @pl.when(i == 0)
        def _():
            dst_ref[...] = jnp.zeros_like(dst_ref)
            dwg_ref[...] = jnp.zeros_like(dwg_ref)
            dbg_ref[...] = jnp.zeros_like(dbg_ref)

        pos = n * C + lax.broadcasted_iota(jnp.int32, (C, 1), 0)
        valid = pos >= FRONT
        lower, upper = _tri_masks()
        is_last = lax.broadcasted_iota(jnp.int32, (C, 1), 0) == C - 1
        for b in range(B):
            lr = lr_ref[b]
            pre, glog = _gla_gate(lr, wg_ref[...], bg_ref[...], valid)
            bcum = _cumsum_rows(glog, lower)
            db_parts = []
            for h in range(GLA_H):
                ks = slice(h * GLA_DK, (h + 1) * GLA_DK)
                vs = slice(h * GLA_DV, (h + 1) * GLA_DV)
                bh = bcum[:, ks]
                blast = jnp.sum(jnp.where(is_last, bh, 0.0), axis=0, keepdims=True)
                eb, enb, ekl, ebl = jnp.exp(bh), jnp.exp(-bh), jnp.exp(blast - bh), jnp.exp(blast)
                qh = q_ref[b, :, ks].astype(F32) * scale
                kh = k_ref[b, :, ks].astype(F32)
                qe_f, ke_f, kl_f = qh * eb, kh * enb, kh * ekl
                qe, ke, kl = qe_f.astype(BF16), ke_f.astype(BF16), kl_f.astype(BF16)
                vh = v_ref[b, :, vs].astype(BF16)
                doh = do_ref[b, :, vs]
                st = ss_ref[b, 0, h]
                dst = dst_ref[b, h]
                st_b, dst_b = st.astype(BF16), dst.astype(BF16)
                da = jnp.where(lower, _nt(doh, vh), 0.0).astype(BF16)
                da_t = jnp.where(upper, _nt(vh, doh), 0.0).astype(BF16)
                a_t = jnp.where(upper, _nt(ke, qe), 0.0).astype(BF16)
                dqe = _nn(da, ke) + _nn(doh, st_b)
                dke = _nn(da_t, qe)
                dvh = _nn(a_t, doh) + _nt(kl, dst_b)
                dkl = _nn(vh, dst_b)
                dst_ref[b, h] = dst * ebl + _tn(doh, qe)
                deb = jnp.sum(st * dst, axis=0, keepdims=True)
                db = dqe * qe_f - dke * ke_f - dkl * kl_f
                db_last = jnp.sum(dkl * kl_f, axis=0, keepdims=True) + deb * ebl
                db_parts.append(db + jnp.where(is_last, db_last, 0.0))
                dc_ref[b, :, vs] = dvh.astype(BF16)
                dc_ref[b, :, 1024 + h * GLA_DK:1024 + (h + 1) * GLA_DK] = (dqe * eb * scale).astype(BF16)
                dc_ref[b, :, 1536 + h * GLA_DK:1536 + (h + 1) * GLA_DK] = (dke * enb + dkl * ekl).astype(BF16)
            dglog = _cumsum_rows(jnp.concatenate(db_parts, axis=1), upper)
            dpre = jnp.where(valid, dglog * (1.0 / GLA_NORMALIZER) / (1.0 + jnp.exp(pre)), 0.0)
            dpre_b = dpre.astype(BF16)
            dc_ref[b, :, 2048:2176] = _nt(dpre_b, wg_ref[...]).astype(BF16)
            dc_ref[b, :, 2176:2304] = jnp.zeros((C, 128), BF16)
            dwg_ref[...] += _tn(lr.astype(BF16), dpre_b)
            dbg_ref[...] += jnp.sum(dpre, axis=0, keepdims=True)

    blocks = B * (_nbytes((C, 512), F32) * 2 + _nbytes((C, 1024), F32) + _nbytes((C, 1024), BF16)
                  + _nbytes((GLA_H, GLA_DV, GLA_DK), F32) + _nbytes((C, WC), BF16)) + 3 * _nbytes((128, 512), F32)
    state = _nbytes((B, GLA_H, GLA_DV, GLA_DK), F32)
    pa = projA.reshape(B, Lp, projA.shape[1])
    rev = lambda i: NC - 1 - i
    dc, dwg, dbg = pl.pallas_call(
        body,
        out_shape=(jax.ShapeDtypeStruct((B, Lp, WC), BF16), jax.ShapeDtypeStruct((128, GLA_KW), F32),
                   jax.ShapeDtypeStruct((1, GLA_KW), F32)),
        grid=(NC,),
        in_specs=[
            pl.BlockSpec((B, C, 512), lambda i: (0, rev(i), 10)),
            pl.BlockSpec((B, C, 512), lambda i: (0, rev(i), 11)),
            pl.BlockSpec((B, C, 1024), lambda i: (0, rev(i), 0)),
            pl.BlockSpec((B, C, 128), lambda i: (0, rev(i), 3)),
            pl.BlockSpec((B, 1, GLA_H, GLA_DV, GLA_DK), lambda i: (0, rev(i), 0, 0, 0)),
            pl.BlockSpec((B, C, 1024), lambda i: (0, rev(i), 0)),
            pl.BlockSpec((128, 512), lambda i: (0, 0)),
            pl.BlockSpec((1, 512), lambda i: (0, 0)),
        ],
        out_specs=(pl.BlockSpec((B, C, WC), lambda i: (0, rev(i), 0)),
                   pl.BlockSpec((128, GLA_KW), lambda i: (0, 0)),
                   pl.BlockSpec((1, GLA_KW), lambda i: (0, 0))),
        scratch_shapes=[pltpu.VMEM((B, GLA_H, GLA_DV, GLA_DK), F32)],
        compiler_params=_params(("arbitrary",), blocks, state),
        name="gla_bwd",
    )(pa, pa, pa, projB.reshape(B, Lp, projB.shape[1]), ssave, doa.reshape(B, Lp, GLA_VW), wg, bg)
    return dc.reshape(T, WC), dwg, dbg


def _attn_bwd_pre(dyb, mla_proj, projA, ob, B, Lp):
    T = B * Lp
    NQ = Lp // QB

    RB = 3 if NQ % 3 == 0 else 1
    NS = NQ // RB

    def body(dyb_ref, w_ref, z_ref, o_ref, do_ref, dz_ref, dcol_ref, dy_ref):
        j = pl.program_id(1)
        dy_ref[...] = _nt(dyb_ref[...], w_ref[...])
        for r in range(RB):
            rs = slice(r * QB, (r + 1) * QB)
            for h in range(MLA_H):
                hs = slice(h * MLA_DV, (h + 1) * MLA_DV)
                dy = dy_ref[rs, hs]
                z = z_ref[rs, hs].astype(F32)
                o = o_ref[rs, hs].astype(F32)
                s = _sigmoid(z)
                do = dy * (z * s)
                do_ref[rs, hs] = do.astype(BF16)
                dz_ref[rs, hs] = (dy * o * (s * (1.0 + z * (1.0 - s)))).astype(BF16)
                dl = jnp.broadcast_to(jnp.sum(do * o, axis=-1, keepdims=True), (QB, LANES))
                dcol_ref[0, h, pl.ds(j * RB + r, 1), :] = jnp.transpose(dl)[0:1, :]

    rows = lambda c: pl.BlockSpec((RB * QB, D), lambda b, j: (b * NS + j, c))
    return pl.pallas_call(
        body,
        out_shape=(jax.ShapeDtypeStruct((T, D), BF16), jax.ShapeDtypeStruct((T, D), BF16),
                   jax.ShapeDtypeStruct((B, MLA_H, NQ, QB), F32)),
        scratch_shapes=[pltpu.VMEM((RB * QB, D), F32)],
        grid=(B, NS),
        in_specs=[rows(0), pl.BlockSpec((D, D), lambda b, j: (0, 0)), rows(2), rows(0)],
        out_specs=(rows(0), rows(0), pl.BlockSpec((1, MLA_H, NQ, QB), lambda b, j: (b, 0, 0, 0))),
        compiler_params=_params(("parallel", "arbitrary"), 6 * _nbytes((RB * QB, D), F32),
                                _nbytes((RB * QB, D), F32)),
        name="attn_bwd_pre",
    )(dyb, mla_proj, projA, ob)


ATTN_BWD_HEADS = 8


def _attn_bwd(q_att, k_att, v_att, do, lse_c, delta_c, B, Lp):
    T = B * Lp
    NQ = Lp // QB
    G = ATTN_BWD_HEADS
    NG = MLA_H // G
    HW = 2 * LANES
    scale = 1.0 / math.sqrt(MLA_QK)

    def body(q_ref, k_ref, v_ref, do_ref, lse_ref, dl_ref, dq_out, dk_out, dv_out, dq_ref, dk_ref, dv_ref):
        kj = pl.program_id(2)
        col = kj * QB + lax.broadcasted_iota(jnp.int32, (QB, QB), 0)
        rowi = lax.broadcasted_iota(jnp.int32, (QB, QB), 1)

        def step(qi, masked, first, diag=False):
            off = pl.multiple_of(qi * QB, QB)
            ok = _attn_mask(qi * QB + rowi, col) if masked else None
            for h in range(G):
                ws = slice(h * HW, (h + 1) * HW)
                hs = slice(h * MLA_DV, (h + 1) * MLA_DV)
                qb = q_ref[pl.ds(off, QB), ws]
                dob = do_ref[pl.ds(off, QB), hs]
                kb = k_ref[:, ws]
                lse2 = lse_ref[0, h, pl.ds(qi, 1), :]
                delta = dl_ref[0, h, pl.ds(qi, 1), :]
                p_t = jnp.exp2(_nt(kb, qb) * (scale * LOG2E) - lse2)
                if masked:
                    p_t = jnp.where(ok, p_t, 0.0)
                ds_t = (p_t * (_nt(v_ref[:, hs], dob) - delta) * scale).astype(BF16)
                if diag:
                    dv_ref[:, hs] = _nn(p_t.astype(BF16), dob)
                    dk_ref[:, ws] = _nn(ds_t, qb)
                else:
                    dv_ref[:, hs] += _nn(p_t.astype(BF16), dob)
                    dk_ref[:, ws] += _nn(ds_t, qb)
                if first:
                    dq_ref[pl.ds(off, QB), ws] = _tn(ds_t, kb)
                else:
                    dq_ref[pl.ds(off, QB), ws] += _tn(ds_t, kb)

        def sweep(masked, first):
            step(kj, True, first, diag=True)

            def it(qi, carry):
                step(qi, masked, first)
                return carry
            lax.fori_loop(kj + 1, NQ, it, 0)

        pl.when(kj == 0)(lambda: sweep(True, True))
        pl.when(kj > 0)(lambda: sweep(False, False))
        dk_out[...] = dk_ref[...].astype(BF16)
        dv_out[...] = dv_ref[...].astype(BF16)

        @pl.when(kj == NQ - 1)
        def _():
            dq_out[...] = dq_ref[...].astype(BF16)

    blocks = (2 * _nbytes((Lp, G * HW), BF16) + _nbytes((Lp, G * MLA_DV), BF16) + 2 * _nbytes((QB, G * 384), BF16)
              + 2 * _nbytes((G, NQ, QB), F32))
    scratch = [pltpu.VMEM((Lp, G * HW), F32), pltpu.VMEM((QB, G * HW), F32), pltpu.VMEM((QB, G * MLA_DV), F32)]
    return pl.pallas_call(
        body,
        out_shape=(jax.ShapeDtypeStruct((T, MLA_H * HW), BF16), jax.ShapeDtypeStruct((T, MLA_H * HW), BF16),
                   jax.ShapeDtypeStruct((T, MLA_H * MLA_DV), BF16)),
        scratch_shapes=scratch,
        grid=(B, NG, NQ),
        in_specs=[
            pl.BlockSpec((Lp, G * HW), lambda b, g, j: (b, g), pipeline_mode=pl.Buffered(1)),
            pl.BlockSpec((QB, G * HW), lambda b, g, j: (b * NQ + j, g)),
            pl.BlockSpec((QB, G * MLA_DV), lambda b, g, j: (b * NQ + j, g)),
            pl.BlockSpec((Lp, G * MLA_DV), lambda b, g, j: (b, g), pipeline_mode=pl.Buffered(1)),
            pl.BlockSpec((1, G, NQ, QB), lambda b, g, j: (b, g, 0, 0)),
            pl.BlockSpec((1, G, NQ, QB), lambda b, g, j: (b, g, 0, 0)),
        ],
        out_specs=(pl.BlockSpec((Lp, G * HW), lambda b, g, j: (b, g), pipeline_mode=pl.Buffered(1)),
                   pl.BlockSpec((QB, G * HW), lambda b, g, j: (b * NQ + j, g)),
                   pl.BlockSpec((QB, G * MLA_DV), lambda b, g, j: (b * NQ + j, g))),
        compiler_params=_params(("parallel", "parallel", "arbitrary"), blocks,
                                _nbytes((Lp, G * HW), F32) + _nbytes((QB, G * 384), F32)),
        name="attn_bwd",
    )(q_att, k_att, v_att, do, lse_c, delta_c)


def _mla_bwd_post(dq, dk, dv, projB, cos_t, sin_t, gq, gkv, wuq2, wukv, B, Lp, tr):
    T = B * Lp
    nt = Lp // tr
    HW = 2 * LANES

    def body(dq_ref, dk_ref, dv_ref, pb_ref, cos_ref, sin_ref, gq_ref, gkv_ref, wuq_ref, wukv_ref,
             dqf_ref, dkvf_ref, de_ref, dgq_ref, dgkv_ref):
        first = (pl.program_id(0) == 0) & (pl.program_id(1) == 0)

        @pl.when(first)
        def _():
            dgq_ref[...] = jnp.zeros_like(dgq_ref)
            dgkv_ref[...] = jnp.zeros_like(dgkv_ref)

        cs = cos_ref[...]
        sn = sin_ref[...]
        rope_t = lambda t: t * cs + _swap_halves(t * sn)
        dkr = jnp.zeros((tr, LANES), F32)
        for h in range(MLA_H):
            dqf_ref[:, h * HW:h * HW + LANES] = dq_ref[:, h * HW:h * HW + LANES]
            dq_rope = dq_ref[:, h * HW + LANES:(h + 1) * HW].astype(F32)
            dqf_ref[:, h * HW + LANES:(h + 1) * HW] = rope_t(dq_rope).astype(BF16)
            dkvf_ref[:, h * HW:h * HW + LANES] = dk_ref[:, h * HW:h * HW + LANES]
            dkvf_ref[:, h * HW + LANES:(h + 1) * HW] = dv_ref[:, h * MLA_DV:(h + 1) * MLA_DV]
            dkr = dkr + dk_ref[:, h * HW + LANES:(h + 1) * HW].astype(F32)

        def norm_bwd(x, dn, g):
            r = lax.rsqrt(jnp.mean(x * x, axis=-1, keepdims=True) + EPS)
            xn = x * r
            t = dn * g
            return r * (t - xn * jnp.mean(t * xn, axis=-1, keepdims=True)), jnp.sum(dn * xn, axis=0, keepdims=True)

        cq = pb_ref[:, 0:Q_RANK].astype(F32)
        ckv = pb_ref[:, Q_RANK:Q_RANK + KV_RANK].astype(F32)
        dcq, dgq = norm_bwd(cq, _nt(dqf_ref[...], wuq_ref[...]), gq_ref[...])
        dckv, dgkv = norm_bwd(ckv, _nt(dkvf_ref[...], wukv_ref[...]), gkv_ref[...])
        dgq_ref[...] += dgq
        dgkv_ref[...] += dgkv
        de_ref[:, 0:Q_RANK] = dcq.astype(BF16)
        de_ref[:, Q_RANK:Q_RANK + KV_RANK] = dckv.astype(BF16)
        de_ref[:, 384:512] = rope_t(dkr).astype(BF16)

    rows = lambda w: pl.BlockSpec((tr, w), lambda b, j: (b * nt + j, 0))
    const = lambda s: pl.BlockSpec(s, lambda b, j: (0, 0))
    blocks = (2 * _nbytes((tr, 2048), F32) + _nbytes((tr, 1024), F32) + _nbytes((tr, 640), F32)
              + 2 * _nbytes((tr, 2048), BF16) + _nbytes((2048, 384), BF16) + 2 * _nbytes((tr, 2048), F32))
    return pl.pallas_call(
        body,
        out_shape=(jax.ShapeDtypeStruct((T, 2048), BF16), jax.ShapeDtypeStruct((T, 2048), BF16),
                   jax.ShapeDtypeStruct((T, 512), BF16), jax.ShapeDtypeStruct((1, Q_RANK), F32),
                   jax.ShapeDtypeStruct((1, KV_RANK), F32)),
        grid=(B, nt),
        in_specs=[rows(2048), rows(2048), rows(1024), rows(640),
                  pl.BlockSpec((tr, 128), lambda b, j: (j, 0)), pl.BlockSpec((tr, 128), lambda b, j: (j, 0)),
                  const((1, Q_RANK)), const((1, KV_RANK)), const((Q_RANK, 2048)), const((KV_RANK, 2048))],
        out_specs=(rows(2048), rows(2048), rows(512), const((1, Q_RANK)), const((1, KV_RANK))),
        compiler_params=_params(("arbitrary", "arbitrary"), blocks),
        name="mla_bwd_post",
    )(dq, dk, dv, projB, cos_t, sin_t, gq, gkv, wuq2, wukv)


def _in_proj_bwd(x, meta, dh1, dA, dBz, dC, dDz, dE, wA, wB, g, B, Lp):
    NQ = Lp // QB
    seq = x.shape[1]
    R = 2 if B % 2 == 0 else 1
    M = R * QB

    def body(x_ref, meta_ref, dh_ref, da_ref, db_ref, dc_ref, dd_ref, de_ref, wa_ref, wb_ref, g_ref,
             gx_ref, dmeta_ref, dg_ref):
        b = pl.program_id(0)
        j = pl.program_id(1)

        @pl.when((b == 0) & (j == 0))
        def _():
            dg_ref[...] = jnp.zeros_like(dg_ref)

        flat = lambda ref: ref[...].reshape(M, ref.shape[-1])
        da, dbz, dc, dd, de = flat(da_ref), flat(db_ref), flat(dc_ref), flat(dd_ref), flat(de_ref)
        du = _nt(da, wa_ref[:, 3072:5120])
        du = du + _nt(dbz, wa_ref[:, 1024:2048])
        du = du + _nt(dd, wa_ref[:, 2048:3072])
        du = du + _nt(dc[:, 0:1024], wa_ref[:, 0:1024])
        du = du + _nt(dc[:, 1024:2048], wa_ref[:, 5120:6144])
        du = du + _nt(dc[:, 2048:2176], wb_ref[:, 384:512])
        du = du + _nt(de[:, 0:384], wb_ref[:, 0:384])
        du = du + _nt(de[:, 384:512], wb_ref[:, 512:640])

        head = jnp.concatenate([jnp.zeros((FRONT, D), F32), meta_ref[...]], axis=0)
        x = jnp.concatenate([jnp.where(j > 0, x_ref[i], head) for i in range(R)], axis=0)
        r = lax.rsqrt(jnp.mean(x * x, axis=-1, keepdims=True) + EPS)
        xn = x * r
        t = du * g_ref[...]
        dh0 = flat(dh_ref).astype(F32) + r * (t - xn * jnp.mean(t * xn, axis=-1, keepdims=True))
        dg_ref[...] += jnp.sum(du * xn, axis=0, keepdims=True)
        dmeta = dh0[FRONT:HEAD_ROWS, :]
        for i in range(R):
            gx_ref[i] = dh0[i * QB:(i + 1) * QB, :]
            if i > 0:
                dmeta = dmeta + dh0[i * QB + FRONT:i * QB + HEAD_ROWS, :]

        @pl.when((j == 0) & (b == 0))
        def _():
            dmeta_ref[...] = dmeta

        @pl.when((j == 0) & (b > 0))
        def _():
            dmeta_ref[...] += dmeta

    rows = lambda w: pl.BlockSpec((R, QB, w), lambda b, j: (b, j, 0))
    x_rows = pl.BlockSpec((R, QB, D), lambda b, j: (b, jnp.maximum(j - 1, 0), 0))
    const = lambda s: pl.BlockSpec(s, lambda b, j: (0,) * len(s))
    resident = lambda s: pl.BlockSpec(s, lambda b, j: (0, 0), pipeline_mode=pl.Buffered(1))
    by_row = lambda a: a.reshape(B, Lp, a.shape[1])
    widths = [a.shape[1] for a in (dA, dBz, dC, dDz, dE)]
    blocks = sum(_nbytes((M, w), BF16) for w in widths) + 4 * _nbytes((M, D), F32)
    return pl.pallas_call(
        body,
        out_shape=(jax.ShapeDtypeStruct((B, seq, D), F32), jax.ShapeDtypeStruct((N_META, D), F32),
                   jax.ShapeDtypeStruct((1, D), F32)),
        grid=(B // R, NQ),
        in_specs=[x_rows, const((N_META, D)), rows(D)] + [rows(w) for w in widths]
        + [resident(wA.shape), resident(wB.shape), const((1, D))],
        out_specs=(x_rows, const((N_META, D)), const((1, D))),
        compiler_params=_params(("arbitrary", "arbitrary"), blocks, _nbytes(wA.shape, BF16) + _nbytes(wB.shape, BF16)),
        name="in_proj_bwd",
    )(x, meta, by_row(dh1), *[by_row(a) for a in (dA, dBz, dC, dDz, dE)], wA, wB, g)


_VMEM_WHOLE = pl.BlockSpec(memory_space=pltpu.VMEM)


def _params_whole(arrays):
    total = sum(_nbytes(a.shape, a.dtype) for a in arrays)
    return pltpu.CompilerParams(vmem_limit_bytes=int(min(total + 12 * 1024 * 1024, VMEM_CAP_V7X)))


def _wire_dtype(shape):
    return BF16 if shape[-2] * shape[-1] >= WIRE_BF16_MIN_ELEMS else F32


def _pair_add_big(gp, recv, c):
    _, half, cols = recv.shape
    th = _div_tile(half, 64, 16)
    out_dtype = _wire_dtype(recv.shape)

    steps = half // th

    def body(c_ref, a_ref, b_ref, o_ref):
        o_ref[...] = (a_ref[...].astype(F32) + b_ref[...].astype(F32)).astype(out_dtype)

    return pl.pallas_call(
        body,
        out_shape=jax.ShapeDtypeStruct(recv.shape, out_dtype),
        grid_spec=pltpu.PrefetchScalarGridSpec(
            num_scalar_prefetch=1,
            grid=(steps,),
            in_specs=[pl.BlockSpec((4, th, cols), lambda i, c_ref: (0, c_ref[0] * steps + i, 0)),
                      pl.BlockSpec((4, th, cols), lambda i, c_ref: (0, i, 0))],
            out_specs=pl.BlockSpec((4, th, cols), lambda i, c_ref: (0, i, 0)),
        ),
        compiler_params=_params(("parallel",), 3 * _nbytes((4, th, cols), F32)),
        name="grad_pair_add_big",
    )(c, gp, recv)


def _pair_add_small(gps, recvs):
    n = len(gps)

    def body(*refs):
        c = lax.axis_index("c")
        for t in range(n):
            g_ref, r_ref, o_ref = refs[t], refs[n + t], refs[2 * n + t]
            half = r_ref.shape[1]
            mine = g_ref[:, pl.ds(pl.multiple_of(c * half, 16 if half % 16 == 0 else 8), half), :]
            s = mine.astype(F32) + r_ref[...].astype(F32)
            o_ref[...] = s.astype(o_ref.dtype)

    return pl.pallas_call(
        body,
        out_shape=[jax.ShapeDtypeStruct(r.shape, _wire_dtype(r.shape)) for r in recvs],
        in_specs=[_VMEM_WHOLE] * (2 * n),
        out_specs=[_VMEM_WHOLE] * n,
        compiler_params=_params_whole(list(gps) + 2 * list(recvs)),
        name="grad_pair_add_small",
    )(*gps, *recvs)


def _chip_order_sum(landed_ref, own_ref, me):
    p = [jnp.where(me == k, own_ref[k], landed_ref[k]).astype(F32) for k in range(4)]
    return ((p[0] + p[1]) + p[2]) + p[3]


def _sum_chips_big(landed, own, pos):
    _, half, cols = landed.shape
    th = _div_tile(half, 64, 16)

    def body(pos_ref, l_ref, s_ref, o_ref):
        o_ref[0] = _chip_order_sum(l_ref, s_ref, pos_ref[1])

    spec = pl.BlockSpec((4, th, cols), lambda i, pos_ref: (0, i, 0))
    return pl.pallas_call(
        body,
        out_shape=jax.ShapeDtypeStruct((2, half, cols), F32),
        grid_spec=pltpu.PrefetchScalarGridSpec(
            num_scalar_prefetch=1,
            grid=(half // th,),
            in_specs=[spec, spec],
            out_specs=pl.BlockSpec((1, th, cols), lambda i, pos_ref: (pos_ref[0], i, 0)),
        ),
        compiler_params=_params(("parallel",), 3 * _nbytes((4, th, cols), F32)),
        name="grad_sum_chips_big",
    )(pos, landed, own)


def _sum_chips_small(landed, own):
    n = len(landed)

    def body(*refs):
        x, y, c = _mesh_pos()
        for t in range(n):
            refs[2 * n + t][c] = _chip_order_sum(refs[t], refs[n + t], 2 * x + y)

    return pl.pallas_call(
        body,
        out_shape=[jax.ShapeDtypeStruct((2,) + p.shape[1:], F32) for p in landed],
        in_specs=[_VMEM_WHOLE] * (2 * n),
        out_specs=[_VMEM_WHOLE] * n,
        compiler_params=_params_whole(list(landed) * 3),
        name="grad_sum_chips_small",
    )(*landed, *own)


def _adamw_update(w_ref, g_ref, m_ref, v_ref, d_ref, mo_ref, vo_ref):
    c1 = 1.0 - ADAM_B1 ** ADAM_STEP
    c2 = 1.0 - ADAM_B2 ** ADAM_STEP
    gv = g_ref[...]
    mn = ADAM_B1 * m_ref[...] + (1.0 - ADAM_B1) * gv
    vn = ADAM_B2 * v_ref[...] + (1.0 - ADAM_B2) * (gv * gv)
    mo_ref[...] = mn
    vo_ref[...] = vn
    d_ref[...] = -ADAM_LR * ((mn / c1) / (jnp.sqrt(vn / c2) + ADAM_EPS) + ADAM_WD * w_ref[...])


def _adamw_big(w, g, m, v):
    lead, (rows, cols) = w.shape[:-2], w.shape[-2:]
    assert all(n == 1 for n in lead)
    tr = _div_tile(rows, (1 << 19) // cols, 8)
    spec = pl.BlockSpec((1,) * len(lead) + (tr, cols), lambda i: (0,) * len(lead) + (i, 0))
    shp = jax.ShapeDtypeStruct(w.shape, F32)
    return pl.pallas_call(
        functools.partial(_adamw_update),
        out_shape=(shp, shp, shp),
        grid=(rows // tr,),
        in_specs=[spec] * 4,
        out_specs=(spec, spec, spec),
        compiler_params=_params(("parallel",), 7 * _nbytes((tr, cols), F32)),
        name="adamw_big",
    )(w, g, m, v)


def _adamw_small(ws, gs, ms, vs):
    n = len(ws)

    def body(*refs):
        for t in range(n):
            _adamw_update(refs[t], refs[n + t], refs[2 * n + t], refs[3 * n + t],
                          refs[4 * n + t], refs[5 * n + t], refs[6 * n + t])

    shapes = [jax.ShapeDtypeStruct(w.shape, F32) for w in ws]
    return pl.pallas_call(
        body,
        out_shape=shapes * 3,
        in_specs=[_VMEM_WHOLE] * (4 * n),
        out_specs=[_VMEM_WHOLE] * (3 * n),
        compiler_params=_params_whole(list(ws) * 7),
        name="adamw_small",
    )(*ws, *gs, *ms, *vs)


def _mesh_pos():
    return lax.axis_index("x"), lax.axis_index("y"), lax.axis_index("c")


def _other_chips(x, y):
    return [(1 - x, y), (x, 1 - y), (1 - x, 1 - y)]


_ANY = pl.BlockSpec(memory_space=pl.ANY)


PAIR_SPLIT_MIN_ROWS = 64


def _weight_gather(shards):
    n = len(shards)
    split = [s.shape[0] >= PAIR_SPLIT_MIN_ROWS for s in shards]

    def body(*refs):
        w_refs, o_refs = refs[:n], refs[n:2 * n]
        send_sems, recv_sems = refs[2 * n:]
        x, y, c = _mesh_pos()
        me = 2 * x + y
        chips = _other_chips(x, y)

        def rows_of(t, core):
            rows = shards[t].shape[0]
            if not split[t]:
                return pl.ds(0, rows)
            return pl.ds(pl.multiple_of(core * (rows // 2), 16), rows // 2)

        def landed(t, k, slot, rows, to):
            ref = o_refs[t].at[slot, rows]
            return pltpu.make_async_remote_copy(src_ref=ref, dst_ref=ref, send_sem=send_sems.at[6 * t + k],
                                                recv_sem=recv_sems.at[6 * t + k], device_id=to, device_id_type=MESH)

        sends = []
        for t in range(n):
            mine = rows_of(t, c)
            for k, (px, py) in enumerate(chips):
                cp = pltpu.make_async_remote_copy(src_ref=w_refs[t].at[mine], dst_ref=o_refs[t].at[me, mine],
                                                  send_sem=send_sems.at[6 * t + k], recv_sem=recv_sems.at[6 * t + k],
                                                  device_id=(px, py, c), device_id_type=MESH)
                cp.start()
                sends.append(cp)
        for t in range(n):
            mine = rows_of(t, c)
            for k, (px, py) in enumerate(chips):
                landed(t, k, 2 * px + py, mine, (x, y, c)).wait_recv()
                if split[t]:
                    cp = landed(t, 3 + k, 2 * px + py, mine, (x, y, 1 - c))
                    cp.start()
                    sends.append(cp)
        for t in range(n):
            if split[t]:
                for k, (px, py) in enumerate(chips):
                    landed(t, 3 + k, 2 * px + py, rows_of(t, 1 - c), (x, y, c)).wait_recv()
        for cp in sends:
            cp.wait_send()

    return pl.pallas_call(
        body,
        out_shape=[jax.ShapeDtypeStruct((4,) + s.shape, s.dtype) for s in shards],
        in_specs=[_ANY] * n,
        out_specs=[_ANY] * n,
        scratch_shapes=[pltpu.SemaphoreType.DMA((6 * n,)), pltpu.SemaphoreType.DMA((6 * n,))],
        name="weight_gather",
    )(*shards)


def _pair_swap(gps):
    n = len(gps)

    def body(*refs):
        g_refs, o_refs = refs[:n], refs[n:2 * n]
        send_sems, recv_sems = refs[2 * n:]
        x, y, c = _mesh_pos()
        copies = []
        for t in range(n):
            half = gps[t].shape[1] // 2
            theirs = pl.ds(pl.multiple_of((1 - c) * half, 8), half)
            cp = pltpu.make_async_remote_copy(src_ref=g_refs[t].at[:, theirs], dst_ref=o_refs[t],
                                              send_sem=send_sems.at[t], recv_sem=recv_sems.at[t],
                                              device_id=(x, y, 1 - c), device_id_type=MESH)
            cp.start()
            copies.append(cp)
        for cp in copies:
            cp.wait_send()
            cp.wait_recv()

    return pl.pallas_call(
        body,
        out_shape=[jax.ShapeDtypeStruct((4, g.shape[1] // 2, g.shape[2]), g.dtype) for g in gps],
        in_specs=[_ANY] * n,
        out_specs=[_ANY] * n,
        scratch_shapes=[pltpu.SemaphoreType.DMA((n,)), pltpu.SemaphoreType.DMA((n,))],
        name="grad_pair_swap",
    )(*gps)


_HBM = pl.BlockSpec(memory_space=pltpu.HBM)
_SEM = pl.BlockSpec(memory_space=pltpu.SEMAPHORE)


def _in_hbm(a):
    return pltpu.with_memory_space_constraint(a, pltpu.HBM)


def _chip_scatter_start(parts):
    n = len(parts)

    def body(*refs):
        s_refs, l_refs = refs[:n], refs[n:2 * n]
        send_sems, recv_sems = refs[2 * n], refs[2 * n + 1]
        token = refs[-1]
        x, y, c = _mesh_pos()
        me = 2 * x + y
        for t in range(n):
            for k, (px, py) in enumerate(_other_chips(x, y)):
                pltpu.make_async_remote_copy(src_ref=s_refs[t].at[2 * px + py], dst_ref=l_refs[t].at[me],
                                             send_sem=send_sems.at[3 * t + k], recv_sem=recv_sems.at[3 * t + k],
                                             device_id=(px, py, c), device_id_type=MESH).start()
        token[...] = jnp.zeros_like(token)

    hbm = [pltpu.HBM(p.shape, p.dtype) for p in parts]
    outs = pl.pallas_call(
        body,
        name="grad_scatter_start",
        out_shape=(pltpu.SemaphoreType.DMA((3 * n,)), pltpu.SemaphoreType.DMA((3 * n,)), *hbm, *hbm,
                   jax.ShapeDtypeStruct((8, LANES), F32)),
        in_specs=[_HBM] * (2 * n),
        out_specs=(_SEM, _SEM, *([_HBM] * (2 * n)), pl.BlockSpec(memory_space=pltpu.VMEM)),
        input_output_aliases={i: 2 + i for i in range(2 * n)},
        compiler_params=pltpu.CompilerParams(has_side_effects=pltpu.SideEffectType.DATAFLOW_SIDE_EFFECTING),
    )(*[_in_hbm(p) for p in parts], *[_in_hbm(lax.empty(p.shape, p.dtype)) for p in parts])
    return outs[0], outs[1], list(outs[2:2 + n]), list(outs[2 + n:2 + 2 * n]), outs[-1]


def _chip_scatter_wait(send_sems, recv_sems, parts, lands, after):
    n = len(parts)

    def body(*refs):
        s_refs, l_refs = refs[:n], refs[n:2 * n]
        send_sems, recv_sems = refs[2 * n], refs[2 * n + 1]
        x, y, c = _mesh_pos()
        me = 2 * x + y
        for t in range(n):
            for k, (px, py) in enumerate(_other_chips(x, y)):
                cp = pltpu.make_async_remote_copy(src_ref=s_refs[t].at[2 * px + py], dst_ref=l_refs[t].at[2 * px + py],
                                                  send_sem=send_sems.at[3 * t + k], recv_sem=recv_sems.at[3 * t + k],
                                                  device_id=(x, y, c), device_id_type=MESH)
                cp.wait_send()
                cp.wait_recv()

    hbm = [pltpu.HBM(p.shape, p.dtype) for p in parts]
    outs = pl.pallas_call(
        body,
        name="grad_scatter_wait",
        out_shape=(*hbm, *hbm),
        in_specs=[_HBM] * (2 * n) + [_SEM, _SEM, _ANY],
        out_specs=[_HBM] * (2 * n),
        input_output_aliases={i: i for i in range(2 * n)},
        compiler_params=pltpu.CompilerParams(has_side_effects=pltpu.SideEffectType.DATAFLOW_SIDE_EFFECTING),
    )(*parts, *lands, send_sems, recv_sems, after)
    return list(outs[:n]), list(outs[n:])


def _late_gather_start(shards):
    n = len(shards)

    def body(*refs):
        w_refs, l_refs = refs[:n], refs[n:2 * n]
        send_sems, recv_sems = refs[2 * n], refs[2 * n + 1]
        token = refs[-1]
        x, y, c = _mesh_pos()
        me = 2 * x + y
        for t in range(n):
            for k, (px, py) in enumerate(_other_chips(x, y)):
                pltpu.make_async_remote_copy(src_ref=w_refs[t], dst_ref=l_refs[t].at[me],
                                             send_sem=send_sems.at[3 * t + k], recv_sem=recv_sems.at[3 * t + k],
                                             device_id=(px, py, c), device_id_type=MESH).start()
        token[...] = jnp.zeros_like(token)

    src = [pltpu.HBM(s.shape, s.dtype) for s in shards]
    land = [pltpu.HBM((4,) + s.shape, s.dtype) for s in shards]
    outs = pl.pallas_call(
        body,
        name="late_gather_start",
        out_shape=(pltpu.SemaphoreType.DMA((3 * n,)), pltpu.SemaphoreType.DMA((3 * n,)), *src, *land,
                   jax.ShapeDtypeStruct((8, LANES), F32)),
        in_specs=[_HBM] * (2 * n),
        out_specs=(_SEM, _SEM, *([_HBM] * (2 * n)), pl.BlockSpec(memory_space=pltpu.VMEM)),
        input_output_aliases={i: 2 + i for i in range(2 * n)},
        compiler_params=pltpu.CompilerParams(has_side_effects=pltpu.SideEffectType.DATAFLOW_SIDE_EFFECTING),
    )(*[_in_hbm(s) for s in shards], *[_in_hbm(lax.empty((4,) + s.shape, s.dtype)) for s in shards])
    return outs[0], outs[1], list(outs[2:2 + n]), list(outs[2 + n:2 + 2 * n]), outs[-1]


def _late_gather_wait(send_sems, recv_sems, shards, lands, after):
    n = len(shards)

    def body(*refs):
        w_refs, l_refs = refs[:n], refs[n:2 * n]
        send_sems, recv_sems = refs[2 * n], refs[2 * n + 1]
        x, y, c = _mesh_pos()
        for t in range(n):
            for k, (px, py) in enumerate(_other_chips(x, y)):
                cp = pltpu.make_async_remote_copy(src_ref=w_refs[t], dst_ref=l_refs[t].at[2 * px + py],
                                                  send_sem=send_sems.at[3 * t + k], recv_sem=recv_sems.at[3 * t + k],
                                                  device_id=(x, y, c), device_id_type=MESH)
                cp.wait_send()
                cp.wait_recv()

    src = [pltpu.HBM(s.shape, s.dtype) for s in shards]
    land = [pltpu.HBM(l.shape, l.dtype) for l in lands]
    outs = pl.pallas_call(
        body,
        name="late_gather_wait",
        out_shape=(*src, *land),
        in_specs=[_HBM] * (2 * n) + [_SEM, _SEM, _ANY],
        out_specs=[_HBM] * (2 * n),
        input_output_aliases={i: i for i in range(2 * n)},
        compiler_params=pltpu.CompilerParams(has_side_effects=pltpu.SideEffectType.DATAFLOW_SIDE_EFFECTING),
    )(*shards, *lands, send_sems, recv_sems, after)
    return list(outs[n:])


def _all_to_all_small(parts):
    n = len(parts)

    def body(*refs):
        p_refs, o_refs = refs[:n], refs[n:2 * n]
        send_sems, recv_sems = refs[2 * n:]
        x, y, c = _mesh_pos()
        me = 4 * x + 2 * y + c
        sends = []
        for t in range(n):
            for k in range(1, 8):
                px, py, pc = x ^ (k >> 2), y ^ ((k >> 1) & 1), c ^ (k & 1)
                cp = pltpu.make_async_remote_copy(src_ref=p_refs[t], dst_ref=o_refs[t].at[me],
                                                  send_sem=send_sems.at[7 * t + k - 1], recv_sem=recv_sems.at[7 * t + k - 1],
                                                  device_id=(px, py, pc), device_id_type=MESH)
                cp.start()
                sends.append(cp)
        for t in range(n):
            for k in range(1, 8):
                peer = 4 * (x ^ (k >> 2)) + 2 * (y ^ ((k >> 1) & 1)) + (c ^ (k & 1))
                pltpu.make_async_remote_copy(src_ref=p_refs[t], dst_ref=o_refs[t].at[peer],
                                             send_sem=send_sems.at[7 * t + k - 1], recv_sem=recv_sems.at[7 * t + k - 1],
                                             device_id=(x, y, c), device_id_type=MESH).wait_recv()
        for cp in sends:
            cp.wait_send()

    return pl.pallas_call(
        body,
        out_shape=[jax.ShapeDtypeStruct((8,) + p.shape, p.dtype) for p in parts],
        in_specs=[_ANY] * n,
        out_specs=[_ANY] * n,
        scratch_shapes=[pltpu.SemaphoreType.DMA((7 * n,)), pltpu.SemaphoreType.DMA((7 * n,))],
        name="grad_small_all_to_all",
    )(*parts)


def _sum_devices_small(landed, own):
    n = len(landed)

    def body(*refs):
        x, y, c = _mesh_pos()
        me = 4 * x + 2 * y + c
        for t in range(n):
            acc = jnp.where(me == 0, refs[n + t][...], refs[t][0])
            for d in range(1, 8):
                acc = acc + jnp.where(me == d, refs[n + t][...], refs[t][d])
            refs[2 * n + t][...] = acc

    return pl.pallas_call(
        body,
        out_shape=[jax.ShapeDtypeStruct(p.shape, F32) for p in own],
        in_specs=[_VMEM_WHOLE] * (2 * n),
        out_specs=[_VMEM_WHOLE] * n,
        compiler_params=_params_whole(list(landed) + 2 * list(own)),
        name="grad_sum_devices_small",
    )(*landed, *own)


def _pair_join(fs):
    n = len(fs)

    def body(*refs):
        f_refs, o_refs = refs[:n], refs[n:2 * n]
        send_sems, recv_sems = refs[2 * n:]
        x, y, c = _mesh_pos()
        sends = []
        for t in range(n):
            cp = pltpu.make_async_remote_copy(src_ref=f_refs[t].at[c], dst_ref=o_refs[t].at[c], send_sem=send_sems.at[t],
                                              recv_sem=recv_sems.at[t], device_id=(x, y, 1 - c), device_id_type=MESH)
            cp.start()
            sends.append(cp)
        for t in range(n):
            pltpu.make_async_remote_copy(src_ref=f_refs[t].at[c], dst_ref=o_refs[t].at[1 - c], send_sem=send_sems.at[t],
                                         recv_sem=recv_sems.at[t], device_id=(x, y, c), device_id_type=MESH).wait_recv()
        for cp in sends:
            cp.wait_send()

    return pl.pallas_call(
        body,
        out_shape=[jax.ShapeDtypeStruct(f.shape, f.dtype) for f in fs],
        in_specs=[_ANY] * n,
        out_specs=[_ANY] * n,
        input_output_aliases={t: t for t in range(n)},
        scratch_shapes=[pltpu.SemaphoreType.DMA((n,)), pltpu.SemaphoreType.DMA((n,))],
        name="grad_pair_join",
    )(*fs)


def _rope_tables(Lp):
    inv = 1.0 / (ROPE_BASE ** (jnp.arange(0, ROPE, 2, dtype=F32) / ROPE))
    ang = (jnp.arange(Lp, dtype=F32) - FRONT)[:, None] * inv[None, :]
    cs, sn = jnp.cos(ang), jnp.sin(ang)
    return jnp.tile(cs, (1, 4)), jnp.concatenate([-sn, sn, -sn, sn], axis=1)


def _local_step(x, loss_target, meta, norm_g, w_in, gate_w, gate_b, gla_norm_g, gla_proj, q_norm_g, w_uq,
                kv_norm_g, w_ukv, mla_proj, w_out, final_norm_g, early_grads_hook=None, late_weights_hook=None):
    B, seq, _ = x.shape
    Lp = HEAD_ROWS + seq
    T = B * Lp
    tr = _div_tile(Lp, 768, 16)
    tkw = _div_tile(T, Lp, QB)
    tm_sq = _div_tile(T, 1024, QB)

    cuts = np.cumsum((0,) + SPLITS)
    shard_w = IN_WIDTH // 4

    def w_cols(i, width=None):
        parts = []
        for j in range(4):
            a, b = max(cuts[i], j * shard_w), min(cuts[i + 1], (j + 1) * shard_w)
            if a < b:
                parts.append(w_in[j][:, a - j * shard_w:b - j * shard_w])
        if width is not None:
            parts.append(jnp.zeros((D, width - (cuts[i + 1] - cuts[i])), w_in.dtype))
        return parts

    i_q, i_k, i_v, i_lr, i_z, i_cq, i_ckv, i_kr, i_mz, i_gg, i_gm = range(11)
    wA = jnp.concatenate(sum([w_cols(i) for i in (i_v, i_z, i_mz, i_gg, i_gm, i_q, i_k)], []), axis=1)
    wB = jnp.concatenate(w_cols(i_cq) + w_cols(i_ckv) + w_cols(i_lr, 128) + w_cols(i_kr, 128), axis=1)
    gn4 = jnp.tile(gla_norm_g, (1, GLA_H))
    cos_t, sin_t = _rope_tables(Lp)

    u = _rms_in(x, meta, norm_g, B, Lp)
    projA = _mm(u, wA, name="in_proj_a", out_dtype=BF16, tm=tkw, tn=1024, tk=D)
    projB = _mm(u, wB, name="in_proj_b", out_dtype=BF16, tm=tkw, tn=640, tk=D)
    if late_weights_hook is not None:
        gate_w, gla_proj, w_uq, w_ukv, mla_proj, w_out = late_weights_hook(projA)
    wg = jnp.pad(gate_w, ((0, 128 - GLA_RANK), (0, 0)))
    wuq2 = jnp.pad(w_uq.reshape(Q_RANK, MLA_H, MLA_QK), ((0, 0), (0, 0), (0, 256 - MLA_QK))).reshape(Q_RANK, 2048)
    oa, ya_in, ssave = _gla_fwd(projA, projB, wg, gate_b, gn4, B, Lp)
    ya = _mm(ya_in, gla_proj, name="gla_proj", out_dtype=BF16, tm=tm_sq, tn=D, tk=D)
    q_att, k_att, v_att, cqn, ckvn = _mla_prep(projB, cos_t, sin_t, q_norm_g, kv_norm_g, wuq2, w_ukv, B, Lp, tr)
    ob, yb_in, lse_c = _attn_fwd(q_att, k_att, v_att, projA, B, Lp)
    yb = _mm(yb_in, mla_proj, name="mla_proj", out_dtype=BF16, tm=tm_sq, tn=D, tk=D)
    dh1_b, merged, loss, d_gf = _out_proj_loss(x, meta, projA, ya, yb, w_out, final_norm_g.reshape(1, D),
                                                loss_target, B, Lp)

    g_w_out = _mm(merged, dh1_b, name="dw_out", trans_a=True, out_dtype=BF16, tm=D, tn=D, tk=tkw)
    dya, dyb, dA = _merge_bwd(dh1_b, w_out, projA, ya, yb, tr)
    g_gla_proj = _mm(ya_in, dya, name="dw_gla_proj", trans_a=True, out_dtype=BF16, tm=D, tn=D, tk=tkw)
    g_mla_proj = _mm(yb_in, dyb, name="dw_mla_proj", trans_a=True, out_dtype=BF16, tm=D, tn=D, tk=tkw)
    doa, dBz, d_gn = _gla_out_bwd(dya, gla_proj, oa, projA, gn4, tr)
    dC, g_wg, d_bg = _gla_bwd(projA, projB, ssave, doa, wg, gate_b, B, Lp)
    do, dDz, delta_c = _attn_bwd_pre(dyb, mla_proj, projA, ob, B, Lp)
    dq, dk, dv = _attn_bwd(q_att, k_att, v_att, do, lse_c, delta_c, B, Lp)
    dqf, dkvf, dE, d_gq, d_gkv = _mla_bwd_post(dq, dk, dv, projB, cos_t, sin_t, q_norm_g, kv_norm_g,
                                                wuq2, w_ukv, B, Lp, tr)
    g_wuq2 = _mm(cqn, dqf, name="dw_uq", trans_a=True, out_dtype=BF16, tm=Q_RANK, tn=2048, tk=tkw)
    g_wukv = _mm(ckvn, dkvf, name="dw_ukv", trans_a=True, out_dtype=BF16, tm=KV_RANK, tn=2048, tk=tkw)
    dparts = [dA, dBz, dC, dDz, dE]
    g_in = [_mm(u, dp, name="dw_in_%d" % i, trans_a=True, out_dtype=BF16, tm=D, tn=_div_tile(dp.shape[1], 1024, 256), tk=tkw)
            for i, dp in enumerate(dparts)]

    gA, gBz, gC, gDz, gE = g_in
    src = [(gC, 1024), (gC, 1536), (gC, 0), (gC, 2048), (gBz, 0), (gE, 0), (gE, Q_RANK), (gE, 384), (gDz, 0),
           (gA, 0), (gA, D)]
    owners = []
    for j in range(4):
        parts = []
        for i, (arr, off) in enumerate(src):
            a, b = max(cuts[i], j * shard_w), min(cuts[i + 1], (j + 1) * shard_w)
            if a < b:
                parts.append(arr[:, off + a - cuts[i]:off + b - cuts[i]])
        owners.append(jnp.concatenate(parts, axis=1))
    g_w_in = jnp.stack(owners)
    g_wuq = g_wuq2.reshape(Q_RANK, MLA_H, 256)[:, :, :MLA_QK].reshape(Q_RANK, MLA_H * MLA_QK)
    grads = dict(w_in=g_w_in, gla_gate_w=g_wg[:GLA_RANK], gla_proj=g_gla_proj, mla_w_uq=g_wuq, mla_w_ukv=g_wukv,
                 mla_proj=g_mla_proj, w_out=g_w_out, gla_gate_b=d_bg,
                 gla_norm_g=d_gn, mla_q_norm_g=d_gq, mla_kv_norm_g=d_gkv, final_norm_g=d_gf)
    token = None if early_grads_hook is None else early_grads_hook(grads)
    ng = norm_g if token is None else norm_g + token[0:1, 0:1]
    grad_x, d_meta, d_ng = _in_proj_bwd(x, meta, dh1_b, dA, dBz, dC, dDz, dE, wA, wB, ng, B, Lp)
    grads.update(meta_tokens=d_meta, norm_g=d_ng)
    return loss[0, 0], grad_x, grads


_MATS = ("w_in", "gla_gate_w", "gla_proj", "mla_w_uq", "mla_w_ukv", "mla_proj", "w_out")
_ROW_SHARDED = ("gla_proj", "mla_proj", "w_out")
_ORDER = ("meta_tokens", "norm_g", "w_in", "gla_gate_w", "gla_gate_b", "gla_norm_g", "gla_proj", "mla_q_norm_g",
          "mla_w_uq", "mla_kv_norm_g", "mla_w_ukv", "mla_proj", "w_out", "final_norm_g")
WIRE_BF16_MIN_ELEMS = 128 * 128
SMALL_PACK_ROWS = 16


def _pack_small(d, scalar=None):
    rows = [jnp.pad(d[n].reshape(1, size), ((0, 0), (0, D - size))) for n, size in SMALL]
    if scalar is not None:
        rows.append(jnp.pad(scalar.reshape(1, 1), ((0, 0), (0, D - 1))))
    return jnp.pad(jnp.concatenate(rows, axis=0), ((0, SMALL_PACK_ROWS - len(rows)), (0, 0)))


def _unpack_small(packed):
    return {n: packed[i, :size] for i, (n, size) in enumerate(SMALL)}


def kernel(x, meta_tokens, norm_g, w_in, gla_gate_w, gla_gate_b, gla_norm_g, gla_proj, mla_q_norm_g, mla_w_uq, mla_kv_norm_g, mla_w_ukv, mla_proj, w_out, final_norm_g, loss_target, m_meta_tokens, m_norm_g, m_w_in, m_gla_gate_w, m_gla_gate_b, m_gla_norm_g, m_gla_proj, m_mla_q_norm_g, m_mla_w_uq, m_mla_kv_norm_g, m_mla_w_ukv, m_mla_proj, m_w_out, m_final_norm_g, v_meta_tokens, v_norm_g, v_w_in, v_gla_gate_w, v_gla_gate_b, v_gla_norm_g, v_gla_proj, v_mla_q_norm_g, v_mla_w_uq, v_mla_kv_norm_g, v_mla_w_ukv, v_mla_proj, v_w_out, v_final_norm_g):
    w = dict(meta_tokens=meta_tokens, norm_g=norm_g, w_in=w_in[0], gla_gate_w=gla_gate_w[0], gla_gate_b=gla_gate_b,
             gla_norm_g=gla_norm_g, gla_proj=gla_proj[0], mla_q_norm_g=mla_q_norm_g, mla_w_uq=mla_w_uq[0],
             mla_kv_norm_g=mla_kv_norm_g, mla_w_ukv=mla_w_ukv[0], mla_proj=mla_proj[0], w_out=w_out[0],
             final_norm_g=final_norm_g)
    mom = dict(meta_tokens=m_meta_tokens, norm_g=m_norm_g, w_in=m_w_in[0], gla_gate_w=m_gla_gate_w[0],
               gla_gate_b=m_gla_gate_b, gla_norm_g=m_gla_norm_g, gla_proj=m_gla_proj[0], mla_q_norm_g=m_mla_q_norm_g,
               mla_w_uq=m_mla_w_uq[0], mla_kv_norm_g=m_mla_kv_norm_g, mla_w_ukv=m_mla_w_ukv[0], mla_proj=m_mla_proj[0],
               w_out=m_w_out[0], final_norm_g=m_final_norm_g)
    var = dict(meta_tokens=v_meta_tokens, norm_g=v_norm_g, w_in=v_w_in[0], gla_gate_w=v_gla_gate_w[0],
               gla_gate_b=v_gla_gate_b, gla_norm_g=v_gla_norm_g, gla_proj=v_gla_proj[0], mla_q_norm_g=v_mla_q_norm_g,
               mla_w_uq=v_mla_w_uq[0], mla_kv_norm_g=v_mla_kv_norm_g, mla_w_ukv=v_mla_w_ukv[0], mla_proj=v_mla_proj[0],
               w_out=v_w_out[0], final_norm_g=v_final_norm_g)
    out_shapes = {n: a.shape for n, a in zip(_ORDER, (meta_tokens, norm_g, w_in, gla_gate_w, gla_gate_b, gla_norm_g,
                                                     gla_proj, mla_q_norm_g, mla_w_uq, mla_kv_norm_g, mla_w_ukv,
                                                     mla_proj, w_out, final_norm_g))}

    me = (2 * lax.axis_index("x") + lax.axis_index("y")).astype(jnp.int32)
    is_mine = lax.broadcasted_iota(jnp.int32, (4, 1, 1), 0) == me
    with_own = lambda gth, own: jnp.where(is_mine, own[None], gth)
    first = [w["w_in"].astype(BF16), meta_tokens]
    w_in_owner, meta_owner = [with_own(gth, own) for gth, own in zip(_weight_gather(first), first)]
    meta_full = meta_owner.transpose(1, 0, 2).reshape(N_META, D)
    late_names = _MATS[1:]
    late = [w[n].astype(BF16) for n in late_names]
    gather_sems = _late_gather_start(late)

    def late_weights(after):
        lands = _late_gather_wait(gather_sems[0], gather_sems[1], gather_sems[2], gather_sems[3], after)
        full = []
        for name, land, own in zip(late_names, lands, late):
            gth = with_own(land, own)
            if name in _ROW_SHARDED:
                full.append(gth.reshape(4 * gth.shape[1], gth.shape[2]))
            else:
                full.append(gth.transpose(1, 0, 2).reshape(gth.shape[1], 4 * gth.shape[2]))
        return full

    def by_owner(name, arr):
        if name == "w_in":
            return arr
        if name in _ROW_SHARDED:
            return arr.reshape(4, arr.shape[0] // 4, arr.shape[1])
        return arr.reshape(arr.shape[0], 4, arr.shape[1] // 4).transpose(1, 0, 2)

    c_idx = lax.axis_index("c").astype(jnp.int32).reshape(1)
    pos = jnp.stack([c_idx[0], me])
    in_flight = {}

    def start_matrix_reduce(early):
        gps = [by_owner(n, early[n]) for n in _MATS]
        recvs = _pair_swap(gps)
        s1 = [_pair_add_big(gps[0], recvs[0], c_idx)] + list(_pair_add_small(gps[1:], recvs[1:]))
        send_sems, recv_sems, parts, lands, token = _chip_scatter_start(s1)
        in_flight.update(send_sems=send_sems, recv_sems=recv_sems, parts=parts, lands=lands)
        return token

    norm_g_after_start = norm_g + gather_sems[4][0:1, 0:1]
    loss_local, grad_x, g = _local_step(
        x, loss_target, meta_full, norm_g_after_start, w_in_owner, None, gla_gate_b, gla_norm_g, None,
        mla_q_norm_g, None, mla_kv_norm_g, None, None, None, final_norm_g,
        early_grads_hook=start_matrix_reduce, late_weights_hook=late_weights)

    s1, landed = _chip_scatter_wait(in_flight["send_sems"], in_flight["recv_sems"], in_flight["parts"],
                                    in_flight["lands"], after=g["norm_g"])
    halves = [_sum_chips_big(landed[0], s1[0], pos)] + list(_sum_chips_small(landed[1:], s1[1:]))
    g_mats = [j.reshape(out_shapes[n]) for j, n in zip(_pair_join(halves), _MATS)]

    late = [g["meta_tokens"], _pack_small(g, scalar=loss_local)]
    meta_sum, small_sum = _sum_devices_small(_all_to_all_small(late), late)
    loss = small_sum[len(SMALL), 0]
    g_meta = lax.dynamic_slice(meta_sum, (0, me * (D // 4)), (N_META, D // 4))
    names = _MATS + ("meta_tokens",)
    g_red = g_mats + [g_meta, small_sum]

    tens = lambda d: [d[n].reshape(out_shapes[n]) for n in names] + [_pack_small(d)]
    w_t, m_t, v_t = tens(w), tens(mom), tens(var)
    big = _adamw_big(w_t[0], g_red[0], m_t[0], v_t[0])
    rest = _adamw_small(w_t[1:], g_red[1:], m_t[1:], v_t[1:])
    k = len(names)
    results = {"grad": g_red}
    for i, kind in enumerate(("delta", "new_m", "new_v")):
        results[kind] = [big[i]] + list(rest[i * k:(i + 1) * k])

    outs = []
    for kind in ("grad", "delta", "new_m", "new_v"):
        vals = dict(zip(names, results[kind][:-1]))
        vals.update(_unpack_small(results[kind][-1]))
        outs += [vals[n].reshape(out_shapes[n]) for n in _ORDER]
    return (loss, grad_x, *outs)
```

```python
import functools
import math

import jax
import jax.numpy as jnp
import numpy as np
from jax import lax
from jax.experimental import pallas as pl
from jax.experimental.pallas import tpu as pltpu

F32 = jnp.float32
BF16 = jnp.bfloat16

D = 1024
N_META = 16
QB = 256
FRONT = QB - N_META
HEAD_ROWS = FRONT + N_META
assert FRONT % 64 == 48
EPS = 1e-6

GLA_H, GLA_DK, GLA_DV, GLA_RANK, GLA_C = 4, 128, 256, 16, 64
GLA_NORMALIZER = 16.0
GLA_KW, GLA_VW = GLA_H * GLA_DK, GLA_H * GLA_DV
MLA_H, NOPE, ROPE, MLA_DV, Q_RANK, KV_RANK = 8, 128, 64, 128, 256, 128
MLA_QK = NOPE + ROPE
ROPE_BASE = 10000.0
SPLITS = (GLA_KW, GLA_KW, GLA_VW, GLA_RANK, GLA_VW, Q_RANK, KV_RANK, ROPE, MLA_H * MLA_DV, D, D)
IN_WIDTH = sum(SPLITS)

ADAM_LR, ADAM_B1, ADAM_B2, ADAM_EPS, ADAM_WD, ADAM_STEP = 0.001, 0.9, 0.999, 1e-08, 0.01, 10

LANES = 128
VMEM_CAP_V7X = 56 * 1024 * 1024
MESH = pl.DeviceIdType.MESH
NEG = -1e30
LOG2E = math.log2(math.e)

SMALL = (("norm_g", D), ("gla_gate_b", GLA_KW), ("gla_norm_g", GLA_DV), ("mla_q_norm_g", Q_RANK),
         ("mla_kv_norm_g", KV_RANK), ("final_norm_g", D))


def _div_tile(n, target, mult):
    best = None
    for d in range(mult, min(n, target) + 1, mult):
        if n % d == 0:
            best = d
    assert best is not None, (n, target, mult)
    return best


def _params(sem, block_bytes, scratch_bytes=0):
    est = 2 * block_bytes + scratch_bytes + 12 * 1024 * 1024
    return pltpu.CompilerParams(dimension_semantics=sem, vmem_limit_bytes=int(min(max(est, 24 * 1024 * 1024), VMEM_CAP_V7X)))


def _nbytes(shape, dtype):
    return int(np.prod(shape)) * jnp.dtype(dtype).itemsize


def _sigmoid(x):
    return 1.0 / (1.0 + jnp.exp(-x))


def _nt(a, b):
    return lax.dot_general(a, b, (((1,), (1,)), ((), ())), preferred_element_type=F32)


def _tn(a, b):
    return lax.dot_general(a, b, (((0,), (0,)), ((), ())), preferred_element_type=F32)


def _nn(a, b):
    return jnp.dot(a, b, preferred_element_type=F32)


def _split2(x):
    a = x.astype(BF16)
    b = (x - a.astype(F32)).astype(BF16)
    return a, b


def _mm(a, b, *, name, trans_a=False, trans_b=False, out_dtype=F32, tm, tn, tk):
    assert not (trans_a and trans_b)
    if trans_a:
        K, M = a.shape
    else:
        M, K = a.shape
    N = b.shape[0] if trans_b else b.shape[1]
    assert (b.shape[1] if trans_b else b.shape[0]) == K
    assert M % tm == 0 and N % tn == 0 and K % tk == 0, (name, M, N, K, tm, tn, tk)
    nk = K // tk

    def body(a_ref, b_ref, o_ref, *scratch):
        av = a_ref[...].astype(BF16)
        bv = b_ref[...].astype(BF16)
        prod = _tn(av, bv) if trans_a else (_nt(av, bv) if trans_b else _nn(av, bv))
        if nk == 1:
            o_ref[...] = prod.astype(out_dtype)
        else:
            acc = scratch[0]
            k = pl.program_id(2)

            @pl.when(k == 0)
            def _():
                acc[...] = prod

            @pl.when(k > 0)
            def _():
                acc[...] += prod

            @pl.when(k == nk - 1)
            def _():
                o_ref[...] = acc[...].astype(out_dtype)

    if trans_a:
        a_spec = pl.BlockSpec((tk, tm), lambda i, j, k: (k, i))
    else:
        a_spec = pl.BlockSpec((tm, tk), lambda i, j, k: (i, k))
    if trans_b:
        b_spec = pl.BlockSpec((tn, tk), lambda i, j, k: (j, k))
    else:
        b_spec = pl.BlockSpec((tk, tn), lambda i, j, k: (k, j))
    blocks = (_nbytes((tm, tk), a.dtype) + _nbytes((tk, tn), b.dtype) + _nbytes((tm, tn), out_dtype))
    scratch = [pltpu.VMEM((tm, tn), F32)] if nk > 1 else []
    return pl.pallas_call(
        body,
        out_shape=jax.ShapeDtypeStruct((M, N), out_dtype),
        grid=(M // tm, N // tn, nk),
        in_specs=[a_spec, b_spec],
        out_specs=pl.BlockSpec((tm, tn), lambda i, j, k: (i, j)),
        scratch_shapes=scratch,
        compiler_params=_params(("parallel", "parallel", "arbitrary"), blocks + _nbytes((tm, tn), F32),
                                _nbytes((tm, tn), F32) if nk > 1 else 0),
        name=name,
    )(a, b)


def _h_tile(j, x_ref, meta_ref):
    head = jnp.concatenate([jnp.zeros((FRONT, D), F32), meta_ref[...]], axis=0)
    return jnp.where(j > 0, x_ref[0], head)


def _x_spec():
    return pl.BlockSpec((1, QB, D), lambda b, j: (b, jnp.maximum(j - 1, 0), 0))


def _rms_in(x, meta, g, B, Lp):
    T = B * Lp
    NQ = Lp // QB

    def body(x_ref, meta_ref, g_ref, u_ref):
        h = _h_tile(pl.program_id(1), x_ref, meta_ref)
        r = lax.rsqrt(jnp.mean(h * h, axis=-1, keepdims=True) + EPS)
        u_ref[...] = (h * r * g_ref[...]).astype(BF16)

    return pl.pallas_call(
        body,
        out_shape=jax.ShapeDtypeStruct((T, D), BF16),
        grid=(B, NQ),
        in_specs=[_x_spec(), pl.BlockSpec((N_META, D), lambda b, j: (0, 0)), pl.BlockSpec((1, D), lambda b, j: (0, 0))],
        out_specs=pl.BlockSpec((QB, D), lambda b, j: (b * NQ + j, 0)),
        compiler_params=_params(("parallel", "parallel"), _nbytes((QB, D), F32) * 2),
        name="rms_in",
    )(x, meta, g)


def _gla_gate(lr, wg, bg, valid):
    pre = _nn(lr.astype(BF16), wg) + bg
    logsig = jnp.minimum(pre, 0.0) - jnp.log(1.0 + jnp.exp(-jnp.abs(pre)))
    return pre, jnp.where(valid, logsig / GLA_NORMALIZER, 0.0)


def _tri_masks():
    ri = lax.broadcasted_iota(jnp.int32, (GLA_C, GLA_C), 0)
    ci = lax.broadcasted_iota(jnp.int32, (GLA_C, GLA_C), 1)
    return ci <= ri, ci >= ri


def _cumsum_rows(x, ones_mask):
    w = jnp.where(ones_mask, 1.0, 0.0).astype(BF16)
    a, b = _split2(x)
    return _nn(w, a) + _nn(w, b)


def _gla_fwd(projA, projB, wg, bg, gn4, B, Lp):
    T = B * Lp
    NC = Lp // GLA_C
    C = GLA_C
    scale = GLA_DK ** -0.5

    def body(q_ref, k_ref, v_ref, lr_ref, z_ref, wg_ref, bg_ref, gn_ref, oa_ref, ya_ref, ssave_ref, st_ref):
        n = pl.program_id(0)

        @pl.when(n == 0)
        def _():
            st_ref[...] = jnp.zeros_like(st_ref)

        pos = n * C + lax.broadcasted_iota(jnp.int32, (C, 1), 0)
        lower, _ = _tri_masks()
        is_last = lax.broadcasted_iota(jnp.int32, (C, 1), 0) == C - 1
        for b in range(B):
            ssave_ref[b, 0] = st_ref[b]
            _, glog = _gla_gate(lr_ref[b], wg_ref[...], bg_ref[...], pos >= FRONT)
            bcum = _cumsum_rows(glog, lower)
            for h in range(GLA_H):
                ks = slice(h * GLA_DK, (h + 1) * GLA_DK)
                vs = slice(h * GLA_DV, (h + 1) * GLA_DV)
                bh = bcum[:, ks]
                blast = jnp.sum(jnp.where(is_last, bh, 0.0), axis=0, keepdims=True)
                qh = q_ref[b, :, ks].astype(F32) * scale
                kh = k_ref[b, :, ks].astype(F32)
                qe = (qh * jnp.exp(bh)).astype(BF16)
                ke = (kh * jnp.exp(-bh)).astype(BF16)
                kl = (kh * jnp.exp(blast - bh)).astype(BF16)
                vh = v_ref[b, :, vs].astype(BF16)
                a = jnp.where(lower, _nt(qe, ke), 0.0).astype(BF16)
                st = st_ref[b, h]
                o = _nn(a, vh) + _nt(qe, st.astype(BF16))
                st_ref[b, h] = st * jnp.exp(blast) + _tn(vh, kl)
                oa_ref[b, :, vs] = o.astype(BF16)
                on = o * lax.rsqrt(jnp.mean(o * o, axis=-1, keepdims=True) + EPS) * gn_ref[:, vs]
                z = z_ref[b, :, vs].astype(F32)
                ya_ref[b, :, vs] = (on * (z * _sigmoid(z))).astype(BF16)

    blocks = B * (_nbytes((C, 512), F32) * 2 + _nbytes((C, 1024), F32) * 3 + _nbytes((C, 1024), BF16)
                  + _nbytes((GLA_H, GLA_DV, GLA_DK), F32)) + _nbytes((128, 512), BF16)
    state = _nbytes((B, GLA_H, GLA_DV, GLA_DK), F32)
    pa = projA.reshape(B, Lp, projA.shape[1])
    oa, ya, ssave = pl.pallas_call(
        body,
        out_shape=(jax.ShapeDtypeStruct((B, Lp, GLA_VW), BF16), jax.ShapeDtypeStruct((B, Lp, GLA_VW), BF16),
                   jax.ShapeDtypeStruct((B, NC, GLA_H, GLA_DV, GLA_DK), F32)),
        grid=(NC,),
        in_specs=[
            pl.BlockSpec((B, C, 512), lambda n: (0, n, 10)),
            pl.BlockSpec((B, C, 512), lambda n: (0, n, 11)),
            pl.BlockSpec((B, C, 1024), lambda n: (0, n, 0)),
            pl.BlockSpec((B, C, 128), lambda n: (0, n, 3)),
            pl.BlockSpec((B, C, 1024), lambda n: (0, n, 1)),
            pl.BlockSpec((128, 512), lambda n: (0, 0)),
            pl.BlockSpec((1, 512), lambda n: (0, 0)),
            pl.BlockSpec((1, 1024), lambda n: (0, 0)),
        ],
        out_specs=(pl.BlockSpec((B, C, 1024), lambda n: (0, n, 0)),
                   pl.BlockSpec((B, C, 1024), lambda n: (0, n, 0)),
                   pl.BlockSpec((B, 1, GLA_H, GLA_DV, GLA_DK), lambda n: (0, n, 0, 0, 0))),
        scratch_shapes=[pltpu.VMEM((B, GLA_H, GLA_DV, GLA_DK), F32)],
        compiler_params=_params(("arbitrary",), blocks, state),
        name="gla_fwd",
    )(pa, pa, pa, projB.reshape(B, Lp, projB.shape[1]), pa, wg, bg, gn4)
    return oa.reshape(T, GLA_VW), ya.reshape(T, GLA_VW), ssave


def _swap_halves(x):
    lane = lax.broadcasted_iota(jnp.int32, x.shape, 1)
    return jnp.where((lane % 64) < 32, pltpu.roll(x, 96, 1), pltpu.roll(x, 32, 1))


def _mla_prep(projB, cos_t, sin_t, gq, gkv, wuq2, wukv, B, Lp, tr):
    T = B * Lp
    nt = Lp // tr
    HW = 2 * LANES

    def body(pb_ref, cos_ref, sin_ref, gq_ref, gkv_ref, wuq_ref, wukv_ref, q_ref, k_ref, v_ref, cqn_ref, ckvn_ref):
        cq = pb_ref[:, 0:Q_RANK].astype(F32)
        ckv = pb_ref[:, Q_RANK:Q_RANK + KV_RANK].astype(F32)
        kr = pb_ref[:, 512:640].astype(F32)
        cqn = (cq * lax.rsqrt(jnp.mean(cq * cq, axis=-1, keepdims=True) + EPS) * gq_ref[...]).astype(BF16)
        ckvn = (ckv * lax.rsqrt(jnp.mean(ckv * ckv, axis=-1, keepdims=True) + EPS) * gkv_ref[...]).astype(BF16)
        cqn_ref[...] = cqn
        ckvn_ref[...] = ckvn
        qf = _nn(cqn, wuq_ref[...])
        kvf = _nn(ckvn, wukv_ref[...])
        cs = cos_ref[...]
        sn = sin_ref[...]
        rope = lambda t: t * cs + _swap_halves(t) * sn
        kr_r = rope(kr).astype(BF16)
        for h in range(MLA_H):
            q_ref[:, h * HW:h * HW + LANES] = qf[:, h * HW:h * HW + LANES].astype(BF16)
            q_ref[:, h * HW + LANES:(h + 1) * HW] = rope(qf[:, h * HW + LANES:(h + 1) * HW]).astype(BF16)
            k_ref[:, h * HW:h * HW + LANES] = kvf[:, h * HW:h * HW + LANES].astype(BF16)
            k_ref[:, h * HW + LANES:(h + 1) * HW] = kr_r
            v_ref[:, h * MLA_DV:(h + 1) * MLA_DV] = kvf[:, h * HW + LANES:(h + 1) * HW].astype(BF16)

    blocks = (_nbytes((tr, 640), F32) + 2 * _nbytes((tr, 128), F32) + _nbytes((Q_RANK, 2048), BF16)
              + _nbytes((KV_RANK, 2048), BF16) + _nbytes((tr, 2048 * 2 + 1024 + 384), BF16)
              + 2 * _nbytes((tr, 2048), F32))
    return pl.pallas_call(
        body,
        out_shape=(jax.ShapeDtypeStruct((T, MLA_H * HW), BF16), jax.ShapeDtypeStruct((T, MLA_H * HW), BF16),
                   jax.ShapeDtypeStruct((T, MLA_H * MLA_DV), BF16), jax.ShapeDtypeStruct((T, Q_RANK), BF16),
                   jax.ShapeDtypeStruct((T, KV_RANK), BF16)),
        grid=(B, nt),
        in_specs=[
            pl.BlockSpec((tr, 640), lambda b, j: (b * nt + j, 0)),
            pl.BlockSpec((tr, 128), lambda b, j: (j, 0)),
            pl.BlockSpec((tr, 128), lambda b, j: (j, 0)),
            pl.BlockSpec((1, Q_RANK), lambda b, j: (0, 0)),
            pl.BlockSpec((1, KV_RANK), lambda b, j: (0, 0)),
            pl.BlockSpec((Q_RANK, 2048), lambda b, j: (0, 0)),
            pl.BlockSpec((KV_RANK, 2048), lambda b, j: (0, 0)),
        ],
        out_specs=(pl.BlockSpec((tr, 2048), lambda b, j: (b * nt + j, 0)),
                   pl.BlockSpec((tr, 2048), lambda b, j: (b * nt + j, 0)),
                   pl.BlockSpec((tr, 1024), lambda b, j: (b * nt + j, 0)),
                   pl.BlockSpec((tr, Q_RANK), lambda b, j: (b * nt + j, 0)),
                   pl.BlockSpec((tr, KV_RANK), lambda b, j: (b * nt + j, 0))),
        compiler_params=_params(("parallel", "parallel"), blocks),
        name="mla_prep",
    )(projB, cos_t, sin_t, gq, gkv, wuq2, wukv)


def _attn_mask(row, col):
    return (col <= row) & ((col >= FRONT) | (row < FRONT))


def _attn_fwd(q_att, k_att, v_att, projA, B, Lp):
    T = B * Lp
    NQ = Lp // QB
    HW = 2 * LANES
    scale = 1.0 / math.sqrt(MLA_QK)

    def body(q_ref, k_ref, v_ref, mz_ref, o_ref, yb_ref, lsec_ref, m_ref, l_ref, acc_ref):
        qi = pl.program_id(1)
        m_ref[...] = jnp.full(m_ref.shape, NEG, F32)
        l_ref[...] = jnp.zeros_like(l_ref)
        acc_ref[...] = jnp.zeros_like(acc_ref)
        row = qi * QB + lax.broadcasted_iota(jnp.int32, (QB, QB), 0)
        coli = lax.broadcasted_iota(jnp.int32, (QB, QB), 1)

        def step(kj, masked):
            off = pl.multiple_of(kj * QB, QB)
            ok = _attn_mask(row, kj * QB + coli) if masked else None
            for h in range(MLA_H):
                q = q_ref[:, h * HW:(h + 1) * HW]
                kb = k_ref[pl.ds(off, QB), h * HW:(h + 1) * HW]
                vb = v_ref[pl.ds(off, QB), h * MLA_DV:(h + 1) * MLA_DV]
                s = _nt(q, kb) * (scale * LOG2E)
                if masked:
                    s = jnp.where(ok, s, NEG)
                m_old = m_ref[h]
                m_new = jnp.maximum(m_old, jnp.max(s, axis=-1, keepdims=True))
                alpha = jnp.exp2(m_old - m_new)
                p = jnp.exp2((s - jnp.tile(m_new, (1, QB // LANES))).astype(BF16))
                m_ref[h] = m_new
                l_ref[h] = alpha * l_ref[h] + jnp.sum(p.astype(F32), axis=-1, keepdims=True)
                acc_ref[h] = alpha * acc_ref[h] + _nn(p, vb)

        step(0, True)

        def unmasked(kj, carry):
            step(kj, False)
            return carry

        lax.fori_loop(1, qi, unmasked, 0)

        @pl.when(qi > 0)
        def _():
            step(qi, True)

        for h in range(MLA_H):
            hs = slice(h * MLA_DV, (h + 1) * MLA_DV)
            l = l_ref[h]
            o = acc_ref[h] / l
            o_ref[:, hs] = o.astype(BF16)
            z = mz_ref[:, hs].astype(F32)
            yb_ref[:, hs] = (o * (z * _sigmoid(z))).astype(BF16)
            lse2 = m_ref[h] + jnp.log(l) * LOG2E
            lsec_ref[0, h, pl.ds(qi, 1), :] = jnp.transpose(lse2)[0:1, :]

    blocks = (_nbytes((QB, 2048), BF16) + _nbytes((Lp, 2048), BF16) + _nbytes((Lp, 1024), BF16)
              + 2 * _nbytes((QB, 1024), F32) + _nbytes((QB, 1024), BF16) + _nbytes((MLA_H, QB, LANES), F32)
              + _nbytes((MLA_H, NQ, QB), F32))
    return pl.pallas_call(
        body,
        out_shape=(jax.ShapeDtypeStruct((T, MLA_H * MLA_DV), BF16), jax.ShapeDtypeStruct((T, MLA_H * MLA_DV), BF16),
                   jax.ShapeDtypeStruct((B, MLA_H, NQ, QB), F32)),
        grid=(B, NQ),
        in_specs=[
            pl.BlockSpec((QB, MLA_H * HW), lambda b, i: (b * NQ + i, 0)),
            pl.BlockSpec((Lp, MLA_H * HW), lambda b, i: (b, 0)),
            pl.BlockSpec((Lp, MLA_H * MLA_DV), lambda b, i: (b, 0)),
            pl.BlockSpec((QB, 1024), lambda b, i: (b * NQ + i, 2)),
        ],
        out_specs=(pl.BlockSpec((QB, 1024), lambda b, i: (b * NQ + i, 0)),
                   pl.BlockSpec((QB, 1024), lambda b, i: (b * NQ + i, 0)),
                   pl.BlockSpec((1, MLA_H, NQ, QB), lambda b, i: (b, 0, 0, 0))),
        scratch_shapes=[pltpu.VMEM((MLA_H, QB, LANES), F32), pltpu.VMEM((MLA_H, QB, LANES), F32),
                        pltpu.VMEM((MLA_H, QB, MLA_DV), F32)],
        compiler_params=_params(("parallel", "arbitrary"), blocks, 3 * _nbytes((MLA_H, QB, LANES), F32)),
        name="attn_fwd",
    )(q_att, k_att, v_att, projA)


def _out_proj_loss(x, meta, projA, ya, yb, w_out, gf, tgt, B, Lp):
    T = B * Lp
    NQ = Lp // QB

    def body(x_ref, meta_ref, gg_ref, gm_ref, ya_ref, yb_ref, w_ref, gf_ref, t_ref,
             dhb_ref, mg_ref, loss_ref, dgf_ref):
        b = pl.program_id(0)
        j = pl.program_id(1)

        @pl.when((b == 0) & (j == 0))
        def _():
            loss_ref[...] = jnp.zeros_like(loss_ref)
            dgf_ref[...] = jnp.zeros_like(dgf_ref)

        f32 = lambda ref: ref[...].astype(F32)
        merged = (_sigmoid(f32(gg_ref)) * f32(ya_ref) + _sigmoid(f32(gm_ref)) * f32(yb_ref)).astype(BF16)
        mg_ref[...] = merged
        h1 = _h_tile(j, x_ref, meta_ref) + _nn(merged, w_ref[...])
        r = lax.rsqrt(jnp.mean(h1 * h1, axis=-1, keepdims=True) + EPS)
        hn = h1 * r
        gfv = gf_ref[...]
        diff = jnp.where(j > 0, hn * gfv - t_ref[0], 0.0)
        loss_ref[...] += (0.5 / D) * jnp.sum(jnp.sum(diff * diff, axis=-1, keepdims=True), axis=0, keepdims=True)
        dout = diff * (1.0 / D)
        dgf_ref[...] += jnp.sum(dout * hn, axis=0, keepdims=True)
        dhn = dout * gfv
        dh = r * (dhn - hn * jnp.mean(dhn * hn, axis=-1, keepdims=True))
        dhb_ref[...] = dh.astype(BF16)

    rows = lambda c: pl.BlockSpec((QB, D), lambda b, j: (b * NQ + j, c))
    const = lambda s: pl.BlockSpec(s, lambda b, j: (0, 0))
    return pl.pallas_call(
        body,
        out_shape=(jax.ShapeDtypeStruct((T, D), BF16), jax.ShapeDtypeStruct((T, D), BF16),
                   jax.ShapeDtypeStruct((1, 1), F32), jax.ShapeDtypeStruct((1, D), F32)),
        grid=(B, NQ),
        in_specs=[_x_spec(), const((N_META, D)), rows(3), rows(4), rows(0), rows(0), const((D, D)),
                  const((1, D)), _x_spec()],
        out_specs=(rows(0), rows(0), const((1, 1)), const((1, D))),
        compiler_params=_params(("arbitrary", "arbitrary"), 10 * _nbytes((QB, D), F32)),
        name="out_proj_loss",
    )(x, meta, projA, projA, ya, yb, w_out, gf, tgt)


def _merge_bwd(dh1_b, w_out, projA, ya, yb, tr):
    T = dh1_b.shape[0]

    def body(dh_ref, w_ref, gg_ref, gm_ref, ya_ref, yb_ref, dya_ref, dyb_ref, da_ref):
        d = _nt(dh_ref[...], w_ref[...])
        sg = _sigmoid(gg_ref[...].astype(F32))
        sm = _sigmoid(gm_ref[...].astype(F32))
        dya_ref[...] = (d * sg).astype(BF16)
        dyb_ref[...] = (d * sm).astype(BF16)
        da_ref[:, 0:D] = (d * ya_ref[...].astype(F32) * (sg * (1.0 - sg))).astype(BF16)
        da_ref[:, D:2 * D] = (d * yb_ref[...].astype(F32) * (sm * (1.0 - sm))).astype(BF16)

    spec = lambda c: pl.BlockSpec((tr, D), lambda i: (i, c))
    return pl.pallas_call(
        body,
        out_shape=(jax.ShapeDtypeStruct((T, D), BF16), jax.ShapeDtypeStruct((T, D), BF16),
                   jax.ShapeDtypeStruct((T, 2 * D), BF16)),
        grid=(T // tr,),
        in_specs=[spec(0), pl.BlockSpec((D, D), lambda i: (0, 0)), spec(3), spec(4), spec(0), spec(0)],
        out_specs=(spec(0), spec(0), pl.BlockSpec((tr, 2 * D), lambda i: (i, 0))),
        compiler_params=_params(("parallel",), 8 * _nbytes((tr, D), F32)),
        name="merge_bwd",
    )(dh1_b, w_out, projA, projA, ya, yb)


def _gla_out_bwd(dya, gla_proj, oa, projA, gn4, tr):
    T = dya.shape[0]
    nsteps = T // tr

    def body(dya_ref, w_ref, oa_ref, z_ref, gn_ref, do_ref, dz_ref, dgn_ref, acc_ref):
        i = pl.program_id(0)

        @pl.when(i == 0)
        def _():
            acc_ref[...] = jnp.zeros_like(acc_ref)

        dy_all = _nt(dya_ref[...], w_ref[...])
        for h in range(GLA_H):
            vs = slice(h * GLA_DV, (h + 1) * GLA_DV)
            dy = dy_all[:, vs]
            o = oa_ref[:, vs].astype(F32)
            z = z_ref[:, vs].astype(F32)
            gn = gn_ref[:, vs]
            s = _sigmoid(z)
            ra = lax.rsqrt(jnp.mean(o * o, axis=-1, keepdims=True) + EPS)
            on = o * ra
            don = dy * (z * s)
            t = don * gn
            do_ref[:, vs] = (ra * (t - on * jnp.mean(t * on, axis=-1, keepdims=True))).astype(BF16)
            dz_ref[:, vs] = (dy * (on * gn) * (s * (1.0 + z * (1.0 - s)))).astype(BF16)
            acc_ref[:, vs] += jnp.sum(don * on, axis=0, keepdims=True)

        @pl.when(i == nsteps - 1)
        def _():
            a = acc_ref[...]
            dgn_ref[...] = a[:, 0:256] + a[:, 256:512] + a[:, 512:768] + a[:, 768:1024]

    spec = lambda c: pl.BlockSpec((tr, D), lambda i: (i, c))
    return pl.pallas_call(
        body,
        out_shape=(jax.ShapeDtypeStruct((T, D), BF16), jax.ShapeDtypeStruct((T, D), BF16),
                   jax.ShapeDtypeStruct((1, GLA_DV), F32)),
        grid=(nsteps,),
        in_specs=[spec(0), pl.BlockSpec((D, D), lambda i: (0, 0)), spec(0), spec(1),
                  pl.BlockSpec((1, D), lambda i: (0, 0))],
        out_specs=(spec(0), spec(0), pl.BlockSpec((1, GLA_DV), lambda i: (0, 0))),
        scratch_shapes=[pltpu.VMEM((1, D), F32)],
        compiler_params=_params(("arbitrary",), 6 * _nbytes((tr, D), F32)),
        name="gla_out_bwd",
    )(dya, gla_proj, oa, projA, gn4)


def _gla_bwd(projA, projB, ssave, doa, wg, bg, B, Lp):
    T = B * Lp
    NC = Lp // GLA_C
    C = GLA_C
    scale = GLA_DK ** -0.5
    WC = 2304

    def body(q_ref, k_ref, v_ref, lr_ref, ss_ref, do_ref, wg_ref, bg_ref, dc_ref, dwg_ref, dbg_ref, dst_ref):
        i = pl.program_id(0)
        n = NC - 1 - i

        @pl.when(i == 0)
        def _():
            dst_ref[...] = jnp.zeros_like(dst_ref)
            dwg_ref[...] = jnp.zeros_like(dwg_ref)
            dbg_ref[...] = jnp.zeros_like(dbg_ref)

        pos = n * C + lax.broadcasted_iota(jnp.int32, (C, 1), 0)
        valid = pos >= FRONT
        lower, upper = _tri_masks()
        is_last = lax.broadcasted_iota(jnp.int32, (C, 1), 0) == C - 1
        for b in range(B):
            lr = lr_ref[b]
            pre, glog = _gla_gate(lr, wg_ref[...], bg_ref[...], valid)
            bcum = _cumsum_rows(glog, lower)
            db_parts = []
            for h in range(GLA_H):
                ks = slice(h * GLA_DK, (h + 1) * GLA_DK)
                vs = slice(h * GLA_DV, (h + 1) * GLA_DV)
                bh = bcum[:, ks]
                blast = jnp.sum(jnp.where(is_last, bh, 0.0), axis=0, keepdims=True)
                eb, enb, ekl, ebl = jnp.exp(bh), jnp.exp(-bh), jnp.exp(blast - bh), jnp.exp(blast)
                qh = q_ref[b, :, ks].astype(F32) * scale
                kh = k_ref[b, :, ks].astype(F32)
                qe_f, ke_f, kl_f = qh * eb, kh * enb, kh * ekl
                qe, ke, kl = qe_f.astype(BF16), ke_f.astype(BF16), kl_f.astype(BF16)
                vh = v_ref[b, :, vs].astype(BF16)
                doh = do_ref[b, :, vs]
                st = ss_ref[b, 0, h]
                dst = dst_ref[b, h]
                st_b, dst_b = st.astype(BF16), dst.astype(BF16)
                da = jnp.where(lower, _nt(doh, vh), 0.0).astype(BF16)
                da_t = jnp.where(upper, _nt(vh, doh), 0.0).astype(BF16)
                a_t = jnp.where(upper, _nt(ke, qe), 0.0).astype(BF16)
                dqe = _nn(da, ke) + _nn(doh, st_b)
                dke = _nn(da_t, qe)
                dvh = _nn(a_t, doh) + _nt(kl, dst_b)
                dkl = _nn(vh, dst_b)
                dst_ref[b, h] = dst * ebl + _tn(doh, qe)
                deb = jnp.sum(st * dst, axis=0, keepdims=True)
                db = dqe * qe_f - dke * ke_f - dkl * kl_f
                db_last = jnp.sum(dkl * kl_f, axis=0, keepdims=True) + deb * ebl
                db_parts.append(db + jnp.where(is_last, db_last, 0.0))
                dc_ref[b, :, vs] = dvh.astype(BF16)
                dc_ref[b, :, 1024 + h * GLA_DK:1024 + (h + 1) * GLA_DK] = (dqe * eb * scale).astype(BF16)
                dc_ref[b, :, 1536 + h * GLA_DK:1536 + (h + 1) * GLA_DK] = (dke * enb + dkl * ekl).astype(BF16)
            dglog = _cumsum_rows(jnp.concatenate(db_parts, axis=1), upper)
            dpre = jnp.where(valid, dglog * (1.0 / GLA_NORMALIZER) / (1.0 + jnp.exp(pre)), 0.0)
            dpre_b = dpre.astype(BF16)
            dc_ref[b, :, 2048:2176] = _nt(dpre_b, wg_ref[...]).astype(BF16)
            dc_ref[b, :, 2176:2304] = jnp.zeros((C, 128), BF16)
            dwg_ref[...] += _tn(lr.astype(BF16), dpre_b)
            dbg_ref[...] += jnp.sum(dpre, axis=0, keepdims=True)

    blocks = B * (_nbytes((C, 512), F32) * 2 + _nbytes((C, 1024), F32) + _nbytes((C, 1024), BF16)
                  + _nbytes((GLA_H, GLA_DV, GLA_DK), F32) + _nbytes((C, WC), BF16)) + 3 * _nbytes((128, 512), F32)
    state = _nbytes((B, GLA_H, GLA_DV, GLA_DK), F32)
    pa = projA.reshape(B, Lp, projA.shape[1])
    rev = lambda i: NC - 1 - i
    dc, dwg, dbg = pl.pallas_call(
        body,
        out_shape=(jax.ShapeDtypeStruct((B, Lp, WC), BF16), jax.ShapeDtypeStruct((128, GLA_KW), F32),
                   jax.ShapeDtypeStruct((1, GLA_KW), F32)),
        grid=(NC,),
        in_specs=[
            pl.BlockSpec((B, C, 512), lambda i: (0, rev(i), 10)),
            pl.BlockSpec((B, C, 512), lambda i: (0, rev(i), 11)),
            pl.BlockSpec((B, C, 1024), lambda i: (0, rev(i), 0)),
            pl.BlockSpec((B, C, 128), lambda i: (0, rev(i), 3)),
            pl.BlockSpec((B, 1, GLA_H, GLA_DV, GLA_DK), lambda i: (0, rev(i), 0, 0, 0)),
            pl.BlockSpec((B, C, 1024), lambda i: (0, rev(i), 0)),
            pl.BlockSpec((128, 512), lambda i: (0, 0)),
            pl.BlockSpec((1, 512), lambda i: (0, 0)),
        ],
        out_specs=(pl.BlockSpec((B, C, WC), lambda i: (0, rev(i), 0)),
                   pl.BlockSpec((128, GLA_KW), lambda i: (0, 0)),
                   pl.BlockSpec((1, GLA_KW), lambda i: (0, 0))),
        scratch_shapes=[pltpu.VMEM((B, GLA_H, GLA_DV, GLA_DK), F32)],
        compiler_params=_params(("arbitrary",), blocks, state),
        name="gla_bwd",
    )(pa, pa, pa, projB.reshape(B, Lp, projB.shape[1]), ssave, doa.reshape(B, Lp, GLA_VW), wg, bg)
    return dc.reshape(T, WC), dwg, dbg


def _attn_bwd_pre(dyb, mla_proj, projA, ob, B, Lp):
    T = B * Lp
    NQ = Lp // QB

    def body(dyb_ref, w_ref, z_ref, o_ref, do_ref, dz_ref, dcol_ref):
        j = pl.program_id(1)
        dy_all = _nt(dyb_ref[...], w_ref[...])
        for h in range(MLA_H):
            hs = slice(h * MLA_DV, (h + 1) * MLA_DV)
            dy = dy_all[:, hs]
            z = z_ref[:, hs].astype(F32)
            o = o_ref[:, hs].astype(F32)
            s = _sigmoid(z)
            do = dy * (z * s)
            do_ref[:, hs] = do.astype(BF16)
            dz_ref[:, hs] = (dy * o * (s * (1.0 + z * (1.0 - s)))).astype(BF16)
            dl = jnp.broadcast_to(jnp.sum(do * o, axis=-1, keepdims=True), (QB, LANES))
            dcol_ref[0, h, pl.ds(j, 1), :] = jnp.transpose(dl)[0:1, :]

    rows = lambda c: pl.BlockSpec((QB, D), lambda b, j: (b * NQ + j, c))
    return pl.pallas_call(
        body,
        out_shape=(jax.ShapeDtypeStruct((T, D), BF16), jax.ShapeDtypeStruct((T, D), BF16),
                   jax.ShapeDtypeStruct((B, MLA_H, NQ, QB), F32)),
        grid=(B, NQ),
        in_specs=[rows(0), pl.BlockSpec((D, D), lambda b, j: (0, 0)), rows(2), rows(0)],
        out_specs=(rows(0), rows(0), pl.BlockSpec((1, MLA_H, NQ, QB), lambda b, j: (b, 0, 0, 0))),
        compiler_params=_params(("parallel", "arbitrary"), 6 * _nbytes((QB, D), F32)),
        name="attn_bwd_pre",
    )(dyb, mla_proj, projA, ob)


ATTN_BWD_HEADS = 8


def _attn_bwd(q_att, k_att, v_att, do, lse_c, delta_c, B, Lp):
    T = B * Lp
    NQ = Lp // QB
    G = ATTN_BWD_HEADS
    NG = MLA_H // G
    HW = 2 * LANES
    scale = 1.0 / math.sqrt(MLA_QK)

    def body(q_ref, k_ref, v_ref, do_ref, lse_ref, dl_ref, dq_out, dk_out, dv_out, dq_ref, dk_ref, dv_ref):
        kj = pl.program_id(2)
        dk_ref[...] = jnp.zeros_like(dk_ref)
        dv_ref[...] = jnp.zeros_like(dv_ref)
        col = kj * QB + lax.broadcasted_iota(jnp.int32, (QB, QB), 0)
        rowi = lax.broadcasted_iota(jnp.int32, (QB, QB), 1)

        def step(qi, masked, first):
            off = pl.multiple_of(qi * QB, QB)
            ok = _attn_mask(qi * QB + rowi, col) if masked else None
            for h in range(G):
                ws = slice(h * HW, (h + 1) * HW)
                hs = slice(h * MLA_DV, (h + 1) * MLA_DV)
                qb = q_ref[pl.ds(off, QB), ws]
                dob = do_ref[pl.ds(off, QB), hs]
                kb = k_ref[:, ws]
                lse2 = lse_ref[0, h, pl.ds(qi, 1), :]
                delta = dl_ref[0, h, pl.ds(qi, 1), :]
                p_t = jnp.exp2(_nt(kb, qb) * (scale * LOG2E) - lse2)
                if masked:
                    p_t = jnp.where(ok, p_t, 0.0)
                dv_ref[:, hs] += _nn(p_t.astype(BF16), dob)
                ds_t = (p_t * (_nt(v_ref[:, hs], dob) - delta) * scale).astype(BF16)
                dk_ref[:, ws] += _nn(ds_t, qb)
                if first:
                    dq_ref[pl.ds(off, QB), ws] = _tn(ds_t, kb)
                else:
                    dq_ref[pl.ds(off, QB), ws] += _tn(ds_t, kb)

        def sweep(masked, first):
            step(kj, True, first)

            def it(qi, carry):
                step(qi, masked, first)
                return carry
            lax.fori_loop(kj + 1, NQ, it, 0)

        pl.when(kj == 0)(lambda: sweep(True, True))
        pl.when(kj > 0)(lambda: sweep(False, False))
        dk_out[...] = dk_ref[...].astype(BF16)
        dv_out[...] = dv_ref[...].astype(BF16)

        @pl.when(kj == NQ - 1)
        def _():
            dq_out[...] = dq_ref[...].astype(BF16)

    blocks = (2 * _nbytes((Lp, G * HW), BF16) + _nbytes((Lp, G * MLA_DV), BF16) + 2 * _nbytes((QB, G * 384), BF16)
              + 2 * _nbytes((G, NQ, QB), F32))
    scratch = [pltpu.VMEM((Lp, G * HW), F32), pltpu.VMEM((QB, G * HW), F32), pltpu.VMEM((QB, G * MLA_DV), F32)]
    return pl.pallas_call(
        body,
        out_shape=(jax.ShapeDtypeStruct((T, MLA_H * HW), BF16), jax.ShapeDtypeStruct((T, MLA_H * HW), BF16),
                   jax.ShapeDtypeStruct((T, MLA_H * MLA_DV), BF16)),
        scratch_shapes=scratch,
        grid=(B, NG, NQ),
        in_specs=[
            pl.BlockSpec((Lp, G * HW), lambda b, g, j: (b, g), pipeline_mode=pl.Buffered(1)),
            pl.BlockSpec((QB, G * HW), lambda b, g, j: (b * NQ + j, g)),
            pl.BlockSpec((QB, G * MLA_DV), lambda b, g, j: (b * NQ + j, g)),
            pl.BlockSpec((Lp, G * MLA_DV), lambda b, g, j: (b, g), pipeline_mode=pl.Buffered(1)),
            pl.BlockSpec((1, G, NQ, QB), lambda b, g, j: (b, g, 0, 0)),
            pl.BlockSpec((1, G, NQ, QB), lambda b, g, j: (b, g, 0, 0)),
        ],
        out_specs=(pl.BlockSpec((Lp, G * HW), lambda b, g, j: (b, g), pipeline_mode=pl.Buffered(1)),
                   pl.BlockSpec((QB, G * HW), lambda b, g, j: (b * NQ + j, g)),
                   pl.BlockSpec((QB, G * MLA_DV), lambda b, g, j: (b * NQ + j, g))),
        compiler_params=_params(("parallel", "parallel", "arbitrary"), blocks,
                                _nbytes((Lp, G * HW), F32) + _nbytes((QB, G * 384), F32)),
        name="attn_bwd",
    )(q_att, k_att, v_att, do, lse_c, delta_c)


def _mla_bwd_post(dq, dk, dv, projB, cos_t, sin_t, gq, gkv, wuq2, wukv, B, Lp, tr):
    T = B * Lp
    nt = Lp // tr
    HW = 2 * LANES

    def body(dq_ref, dk_ref, dv_ref, pb_ref, cos_ref, sin_ref, gq_ref, gkv_ref, wuq_ref, wukv_ref,
             dqf_ref, dkvf_ref, de_ref, dgq_ref, dgkv_ref):
        first = (pl.program_id(0) == 0) & (pl.program_id(1) == 0)

        @pl.when(first)
        def _():
            dgq_ref[...] = jnp.zeros_like(dgq_ref)
            dgkv_ref[...] = jnp.zeros_like(dgkv_ref)

        cs = cos_ref[...]
        sn = sin_ref[...]
        rope_t = lambda t: t * cs + _swap_halves(t * sn)
        dkr = jnp.zeros((tr, LANES), F32)
        for h in range(MLA_H):
            dqf_ref[:, h * HW:h * HW + LANES] = dq_ref[:, h * HW:h * HW + LANES]
            dq_rope = dq_ref[:, h * HW + LANES:(h + 1) * HW].astype(F32)
            dqf_ref[:, h * HW + LANES:(h + 1) * HW] = rope_t(dq_rope).astype(BF16)
            dkvf_ref[:, h * HW:h * HW + LANES] = dk_ref[:, h * HW:h * HW + LANES]
            dkvf_ref[:, h * HW + LANES:(h + 1) * HW] = dv_ref[:, h * MLA_DV:(h + 1) * MLA_DV]
            dkr = dkr + dk_ref[:, h * HW + LANES:(h + 1) * HW].astype(F32)

        def norm_bwd(x, dn, g):
            r = lax.rsqrt(jnp.mean(x * x, axis=-1, keepdims=True) + EPS)
            xn = x * r
            t = dn * g
            return r * (t - xn * jnp.mean(t * xn, axis=-1, keepdims=True)), jnp.sum(dn * xn, axis=0, keepdims=True)

        cq = pb_ref[:, 0:Q_RANK].astype(F32)
        ckv = pb_ref[:, Q_RANK:Q_RANK + KV_RANK].astype(F32)
        dcq, dgq = norm_bwd(cq, _nt(dqf_ref[...], wuq_ref[...]), gq_ref[...])
        dckv, dgkv = norm_bwd(ckv, _nt(dkvf_ref[...], wukv_ref[...]), gkv_ref[...])
        dgq_ref[...] += dgq
        dgkv_ref[...] += dgkv
        de_ref[:, 0:Q_RANK] = dcq.astype(BF16)
        de_ref[:, Q_RANK:Q_RANK + KV_RANK] = dckv.astype(BF16)
        de_ref[:, 384:512] = rope_t(dkr).astype(BF16)

    rows = lambda w: pl.BlockSpec((tr, w), lambda b, j: (b * nt + j, 0))
    const = lambda s: pl.BlockSpec(s, lambda b, j: (0, 0))
    blocks = (2 * _nbytes((tr, 2048), F32) + _nbytes((tr, 1024), F32) + _nbytes((tr, 640), F32)
              + 2 * _nbytes((tr, 2048), BF16) + _nbytes((2048, 384), BF16) + 2 * _nbytes((tr, 2048), F32))
    return pl.pallas_call(
        body,
        out_shape=(jax.ShapeDtypeStruct((T, 2048), BF16), jax.ShapeDtypeStruct((T, 2048), BF16),
                   jax.ShapeDtypeStruct((T, 512), BF16), jax.ShapeDtypeStruct((1, Q_RANK), F32),
                   jax.ShapeDtypeStruct((1, KV_RANK), F32)),
        grid=(B, nt),
        in_specs=[rows(2048), rows(2048), rows(1024), rows(640),
                  pl.BlockSpec((tr, 128), lambda b, j: (j, 0)), pl.BlockSpec((tr, 128), lambda b, j: (j, 0)),
                  const((1, Q_RANK)), const((1, KV_RANK)), const((Q_RANK, 2048)), const((KV_RANK, 2048))],
        out_specs=(rows(2048), rows(2048), rows(512), const((1, Q_RANK)), const((1, KV_RANK))),
        compiler_params=_params(("arbitrary", "arbitrary"), blocks),
        name="mla_bwd_post",
    )(dq, dk, dv, projB, cos_t, sin_t, gq, gkv, wuq2, wukv)


def _in_proj_bwd(x, meta, dh1, dA, dBz, dC, dDz, dE, wA, wB, g, B, Lp):
    NQ = Lp // QB
    seq = x.shape[1]
    R = 2 if B % 2 == 0 else 1
    M = R * QB

    def body(x_ref, meta_ref, dh_ref, da_ref, db_ref, dc_ref, dd_ref, de_ref, wa_ref, wb_ref, g_ref,
             gx_ref, dmeta_ref, dg_ref):
        b = pl.program_id(0)
        j = pl.program_id(1)

        @pl.when((b == 0) & (j == 0))
        def _():
            dg_ref[...] = jnp.zeros_like(dg_ref)

        flat = lambda ref: ref[...].reshape(M, ref.shape[-1])
        da, dbz, dc, dd, de = flat(da_ref), flat(db_ref), flat(dc_ref), flat(dd_ref), flat(de_ref)
        du = _nt(da, wa_ref[:, 3072:5120])
        du = du + _nt(dbz, wa_ref[:, 1024:2048])
        du = du + _nt(dd, wa_ref[:, 2048:3072])
        du = du + _nt(dc[:, 0:1024], wa_ref[:, 0:1024])
        du = du + _nt(dc[:, 1024:2048], wa_ref[:, 5120:6144])
        du = du + _nt(dc[:, 2048:2176], wb_ref[:, 384:512])
        du = du + _nt(de[:, 0:384], wb_ref[:, 0:384])
        du = du + _nt(de[:, 384:512], wb_ref[:, 512:640])

        head = jnp.concatenate([jnp.zeros((FRONT, D), F32), meta_ref[...]], axis=0)
        x = jnp.concatenate([jnp.where(j > 0, x_ref[i], head) for i in range(R)], axis=0)
        r = lax.rsqrt(jnp.mean(x * x, axis=-1, keepdims=True) + EPS)
        xn = x * r
        t = du * g_ref[...]
        dh0 = flat(dh_ref).astype(F32) + r * (t - xn * jnp.mean(t * xn, axis=-1, keepdims=True))
        dg_ref[...] += jnp.sum(du * xn, axis=0, keepdims=True)
        dmeta = dh0[FRONT:HEAD_ROWS, :]
        for i in range(R):
            gx_ref[i] = dh0[i * QB:(i + 1) * QB, :]
            if i > 0:
                dmeta = dmeta + dh0[i * QB + FRONT:i * QB + HEAD_ROWS, :]

        @pl.when((j == 0) & (b == 0))
        def _():
            dmeta_ref[...] = dmeta

        @pl.when((j == 0) & (b > 0))
        def _():
            dmeta_ref[...] += dmeta

    rows = lambda w: pl.BlockSpec((R, QB, w), lambda b, j: (b, j, 0))
    x_rows = pl.BlockSpec((R, QB, D), lambda b, j: (b, jnp.maximum(j - 1, 0), 0))
    const = lambda s: pl.BlockSpec(s, lambda b, j: (0,) * len(s))
    resident = lambda s: pl.BlockSpec(s, lambda b, j: (0, 0), pipeline_mode=pl.Buffered(1))
    by_row = lambda a: a.reshape(B, Lp, a.shape[1])
    widths = [a.shape[1] for a in (dA, dBz, dC, dDz, dE)]
    blocks = sum(_nbytes((M, w), BF16) for w in widths) + 4 * _nbytes((M, D), F32)
    return pl.pallas_call(
        body,
        out_shape=(jax.ShapeDtypeStruct((B, seq, D), F32), jax.ShapeDtypeStruct((N_META, D), F32),
                   jax.ShapeDtypeStruct((1, D), F32)),
        grid=(B // R, NQ),
        in_specs=[x_rows, const((N_META, D)), rows(D)] + [rows(w) for w in widths]
        + [resident(wA.shape), resident(wB.shape), const((1, D))],
        out_specs=(x_rows, const((N_META, D)), const((1, D))),
        compiler_params=_params(("arbitrary", "arbitrary"), blocks, _nbytes(wA.shape, BF16) + _nbytes(wB.shape, BF16)),
        name="in_proj_bwd",
    )(x, meta, by_row(dh1), *[by_row(a) for a in (dA, dBz, dC, dDz, dE)], wA, wB, g)


_VMEM_WHOLE = pl.BlockSpec(memory_space=pltpu.VMEM)


def _params_whole(arrays):
    total = sum(_nbytes(a.shape, a.dtype) for a in arrays)
    return pltpu.CompilerParams(vmem_limit_bytes=int(min(total + 12 * 1024 * 1024, VMEM_CAP_V7X)))


def _wire_dtype(shape):
    return BF16 if shape[-2] * shape[-1] >= WIRE_BF16_MIN_ELEMS else F32


def _pair_add_big(gp, recv, c):
    _, half, cols = recv.shape
    th = _div_tile(half, 64, 16)
    out_dtype = _wire_dtype(recv.shape)

    steps = half // th

    def body(c_ref, a_ref, b_ref, o_ref):
        o_ref[...] = (a_ref[...].astype(F32) + b_ref[...].astype(F32)).astype(out_dtype)

    return pl.pallas_call(
        body,
        out_shape=jax.ShapeDtypeStruct(recv.shape, out_dtype),
        grid_spec=pltpu.PrefetchScalarGridSpec(
            num_scalar_prefetch=1,
            grid=(steps,),
            in_specs=[pl.BlockSpec((4, th, cols), lambda i, c_ref: (0, c_ref[0] * steps + i, 0)),
                      pl.BlockSpec((4, th, cols), lambda i, c_ref: (0, i, 0))],
            out_specs=pl.BlockSpec((4, th, cols), lambda i, c_ref: (0, i, 0)),
        ),
        compiler_params=_params(("parallel",), 3 * _nbytes((4, th, cols), F32)),
        name="grad_pair_add_big",
    )(c, gp, recv)


def _pair_add_small(gps, recvs):
    n = len(gps)

    def body(*refs):
        c = lax.axis_index("c")
        for t in range(n):
            g_ref, r_ref, o_ref = refs[t], refs[n + t], refs[2 * n + t]
            half = r_ref.shape[1]
            mine = g_ref[:, pl.ds(pl.multiple_of(c * half, 16 if half % 16 == 0 else 8), half), :]
            s = mine.astype(F32) + r_ref[...].astype(F32)
            o_ref[...] = s.astype(o_ref.dtype)

    return pl.pallas_call(
        body,
        out_shape=[jax.ShapeDtypeStruct(r.shape, _wire_dtype(r.shape)) for r in recvs],
        in_specs=[_VMEM_WHOLE] * (2 * n),
        out_specs=[_VMEM_WHOLE] * n,
        compiler_params=_params_whole(list(gps) + 2 * list(recvs)),
        name="grad_pair_add_small",
    )(*gps, *recvs)


def _chip_order_sum(landed_ref, own_ref, me):
    p = [jnp.where(me == k, own_ref[k], landed_ref[k]).astype(F32) for k in range(4)]
    return ((p[0] + p[1]) + p[2]) + p[3]


def _sum_chips_big(landed, own, pos):
    _, half, cols = landed.shape
    th = _div_tile(half, 64, 16)

    def body(pos_ref, l_ref, s_ref, o_ref):
        o_ref[0] = _chip_order_sum(l_ref, s_ref, pos_ref[1])

    spec = pl.BlockSpec((4, th, cols), lambda i, pos_ref: (0, i, 0))
    return pl.pallas_call(
        body,
        out_shape=jax.ShapeDtypeStruct((2, half, cols), F32),
        grid_spec=pltpu.PrefetchScalarGridSpec(
            num_scalar_prefetch=1,
            grid=(half // th,),
            in_specs=[spec, spec],
            out_specs=pl.BlockSpec((1, th, cols), lambda i, pos_ref: (pos_ref[0], i, 0)),
        ),
        compiler_params=_params(("parallel",), 3 * _nbytes((4, th, cols), F32)),
        name="grad_sum_chips_big",
    )(pos, landed, own)


def _sum_chips_small(landed, own):
    n = len(landed)

    def body(*refs):
        x, y, c = _mesh_pos()
        for t in range(n):
            refs[2 * n + t][c] = _chip_order_sum(refs[t], refs[n + t], 2 * x + y)

    return pl.pallas_call(
        body,
        out_shape=[jax.ShapeDtypeStruct((2,) + p.shape[1:], F32) for p in landed],
        in_specs=[_VMEM_WHOLE] * (2 * n),
        out_specs=[_VMEM_WHOLE] * n,
        compiler_params=_params_whole(list(landed) * 3),
        name="grad_sum_chips_small",
    )(*landed, *own)


def _adamw_update(w_ref, g_ref, m_ref, v_ref, d_ref, mo_ref, vo_ref):
    c1 = 1.0 - ADAM_B1 ** ADAM_STEP
    c2 = 1.0 - ADAM_B2 ** ADAM_STEP
    gv = g_ref[...]
    mn = ADAM_B1 * m_ref[...] + (1.0 - ADAM_B1) * gv
    vn = ADAM_B2 * v_ref[...] + (1.0 - ADAM_B2) * (gv * gv)
    mo_ref[...] = mn
    vo_ref[...] = vn
    d_ref[...] = -ADAM_LR * ((mn / c1) / (jnp.sqrt(vn / c2) + ADAM_EPS) + ADAM_WD * w_ref[...])


def _adamw_big(w, g, m, v):
    lead, (rows, cols) = w.shape[:-2], w.shape[-2:]
    assert all(n == 1 for n in lead)
    tr = _div_tile(rows, (1 << 19) // cols, 8)
    spec = pl.BlockSpec((1,) * len(lead) + (tr, cols), lambda i: (0,) * len(lead) + (i, 0))
    shp = jax.ShapeDtypeStruct(w.shape, F32)
    return pl.pallas_call(
        functools.partial(_adamw_update),
        out_shape=(shp, shp, shp),
        grid=(rows // tr,),
        in_specs=[spec] * 4,
        out_specs=(spec, spec, spec),
        compiler_params=_params(("parallel",), 7 * _nbytes((tr, cols), F32)),
        name="adamw_big",
    )(w, g, m, v)


def _adamw_small(ws, gs, ms, vs):
    n = len(ws)

    def body(*refs):
        for t in range(n):
            _adamw_update(refs[t], refs[n + t], refs[2 * n + t], refs[3 * n + t],
                          refs[4 * n + t], refs[5 * n + t], refs[6 * n + t])

    shapes = [jax.ShapeDtypeStruct(w.shape, F32) for w in ws]
    return pl.pallas_call(
        body,
        out_shape=shapes * 3,
        in_specs=[_VMEM_WHOLE] * (4 * n),
        out_specs=[_VMEM_WHOLE] * (3 * n),
        compiler_params=_params_whole(list(ws) * 7),
        name="adamw_small",
    )(*ws, *gs, *ms, *vs)


def _mesh_pos():
    return lax.axis_index("x"), lax.axis_index("y"), lax.axis_index("c")


def _other_chips(x, y):
    return [(1 - x, y), (x, 1 - y), (1 - x, 1 - y)]


_ANY = pl.BlockSpec(memory_space=pl.ANY)


PAIR_SPLIT_MIN_ROWS = 64


def _weight_gather(shards):
    n = len(shards)
    split = [s.shape[0] >= PAIR_SPLIT_MIN_ROWS for s in shards]

    def body(*refs):
        w_refs, o_refs = refs[:n], refs[n:2 * n]
        send_sems, recv_sems = refs[2 * n:]
        x, y, c = _mesh_pos()
        me = 2 * x + y
        chips = _other_chips(x, y)

        def rows_of(t, core):
            rows = shards[t].shape[0]
            if not split[t]:
                return pl.ds(0, rows)
            return pl.ds(pl.multiple_of(core * (rows // 2), 16), rows // 2)

        def landed(t, k, slot, rows, to):
            ref = o_refs[t].at[slot, rows]
            return pltpu.make_async_remote_copy(src_ref=ref, dst_ref=ref, send_sem=send_sems.at[6 * t + k],
                                                recv_sem=recv_sems.at[6 * t + k], device_id=to, device_id_type=MESH)

        sends = []
        for t in range(n):
            mine = rows_of(t, c)
            for k, (px, py) in enumerate(chips):
                cp = pltpu.make_async_remote_copy(src_ref=w_refs[t].at[mine], dst_ref=o_refs[t].at[me, mine],
                                                  send_sem=send_sems.at[6 * t + k], recv_sem=recv_sems.at[6 * t + k],
                                                  device_id=(px, py, c), device_id_type=MESH)
                cp.start()
                sends.append(cp)
        for t in range(n):
            mine = rows_of(t, c)
            for k, (px, py) in enumerate(chips):
                landed(t, k, 2 * px + py, mine, (x, y, c)).wait_recv()
                if split[t]:
                    cp = landed(t, 3 + k, 2 * px + py, mine, (x, y, 1 - c))
                    cp.start()
                    sends.append(cp)
        for t in range(n):
            if split[t]:
                for k, (px, py) in enumerate(chips):
                    landed(t, 3 + k, 2 * px + py, rows_of(t, 1 - c), (x, y, c)).wait_recv()
        for cp in sends:
            cp.wait_send()

    return pl.pallas_call(
        body,
        out_shape=[jax.ShapeDtypeStruct((4,) + s.shape, s.dtype) for s in shards],
        in_specs=[_ANY] * n,
        out_specs=[_ANY] * n,
        scratch_shapes=[pltpu.SemaphoreType.DMA((6 * n,)), pltpu.SemaphoreType.DMA((6 * n,))],
        name="weight_gather",
    )(*shards)


def _pair_swap(gps):
    n = len(gps)

    def body(*refs):
        g_refs, o_refs = refs[:n], refs[n:2 * n]
        send_sems, recv_sems = refs[2 * n:]
        x, y, c = _mesh_pos()
        copies = []
        for t in range(n):
            half = gps[t].shape[1] // 2
            theirs = pl.ds(pl.multiple_of((1 - c) * half, 8), half)
            cp = pltpu.make_async_remote_copy(src_ref=g_refs[t].at[:, theirs], dst_ref=o_refs[t],
                                              send_sem=send_sems.at[t], recv_sem=recv_sems.at[t],
                                              device_id=(x, y, 1 - c), device_id_type=MESH)
            cp.start()
            copies.append(cp)
        for cp in copies:
            cp.wait_send()
            cp.wait_recv()

    return pl.pallas_call(
        body,
        out_shape=[jax.ShapeDtypeStruct((4, g.shape[1] // 2, g.shape[2]), g.dtype) for g in gps],
        in_specs=[_ANY] * n,
        out_specs=[_ANY] * n,
        scratch_shapes=[pltpu.SemaphoreType.DMA((n,)), pltpu.SemaphoreType.DMA((n,))],
        name="grad_pair_swap",
    )(*gps)


_HBM = pl.BlockSpec(memory_space=pltpu.HBM)
_SEM = pl.BlockSpec(memory_space=pltpu.SEMAPHORE)


def _in_hbm(a):
    return pltpu.with_memory_space_constraint(a, pltpu.HBM)


def _chip_scatter_start(parts):
    n = len(parts)

    def body(*refs):
        s_refs, l_refs = refs[:n], refs[n:2 * n]
        send_sems, recv_sems = refs[2 * n], refs[2 * n + 1]
        token = refs[-1]
        x, y, c = _mesh_pos()
        me = 2 * x + y
        for t in range(n):
            for k, (px, py) in enumerate(_other_chips(x, y)):
                pltpu.make_async_remote_copy(src_ref=s_refs[t].at[2 * px + py], dst_ref=l_refs[t].at[me],
                                             send_sem=send_sems.at[3 * t + k], recv_sem=recv_sems.at[3 * t + k],
                                             device_id=(px, py, c), device_id_type=MESH).start()
        token[...] = jnp.zeros_like(token)

    hbm = [pltpu.HBM(p.shape, p.dtype) for p in parts]
    outs = pl.pallas_call(
        body,
        name="grad_scatter_start",
        out_shape=(pltpu.SemaphoreType.DMA((3 * n,)), pltpu.SemaphoreType.DMA((3 * n,)), *hbm, *hbm,
                   jax.ShapeDtypeStruct((8, LANES), F32)),
        in_specs=[_HBM] * (2 * n),
        out_specs=(_SEM, _SEM, *([_HBM] * (2 * n)), pl.BlockSpec(memory_space=pltpu.VMEM)),
        input_output_aliases={i: 2 + i for i in range(2 * n)},
        compiler_params=pltpu.CompilerParams(has_side_effects=pltpu.SideEffectType.DATAFLOW_SIDE_EFFECTING),
    )(*[_in_hbm(p) for p in parts], *[_in_hbm(lax.empty(p.shape, p.dtype)) for p in parts])
    return outs[0], outs[1], list(outs[2:2 + n]), list(outs[2 + n:2 + 2 * n]), outs[-1]


def _chip_scatter_wait(send_sems, recv_sems, parts, lands, after):
    n = len(parts)

    def body(*refs):
        s_refs, l_refs = refs[:n], refs[n:2 * n]
        send_sems, recv_sems = refs[2 * n], refs[2 * n + 1]
        x, y, c = _mesh_pos()
        me = 2 * x + y
        for t in range(n):
            for k, (px, py) in enumerate(_other_chips(x, y)):
                cp = pltpu.make_async_remote_copy(src_ref=s_refs[t].at[2 * px + py], dst_ref=l_refs[t].at[2 * px + py],
                                                  send_sem=send_sems.at[3 * t + k], recv_sem=recv_sems.at[3 * t + k],
                                                  device_id=(x, y, c), device_id_type=MESH)
                cp.wait_send()
                cp.wait_recv()

    hbm = [pltpu.HBM(p.shape, p.dtype) for p in parts]
    outs = pl.pallas_call(
        body,
        name="grad_scatter_wait",
        out_shape=(*hbm, *hbm),
        in_specs=[_HBM] * (2 * n) + [_SEM, _SEM, _ANY],
        out_specs=[_HBM] * (2 * n),
        input_output_aliases={i: i for i in range(2 * n)},
        compiler_params=pltpu.CompilerParams(has_side_effects=pltpu.SideEffectType.DATAFLOW_SIDE_EFFECTING),
    )(*parts, *lands, send_sems, recv_sems, after)
    return list(outs[:n]), list(outs[n:])


def _late_gather_start(shards):
    n = len(shards)

    def body(*refs):
        w_refs, l_refs = refs[:n], refs[n:2 * n]
        send_sems, recv_sems = refs[2 * n], refs[2 * n + 1]
        token = refs[-1]
        x, y, c = _mesh_pos()
        me = 2 * x + y
        for t in range(n):
            for k, (px, py) in enumerate(_other_chips(x, y)):
                pltpu.make_async_remote_copy(src_ref=w_refs[t], dst_ref=l_refs[t].at[me],
                                             send_sem=send_sems.at[3 * t + k], recv_sem=recv_sems.at[3 * t + k],
                                             device_id=(px, py, c), device_id_type=MESH).start()
        token[...] = jnp.zeros_like(token)

    src = [pltpu.HBM(s.shape, s.dtype) for s in shards]
    land = [pltpu.HBM((4,) + s.shape, s.dtype) for s in shards]
    outs = pl.pallas_call(
        body,
        name="late_gather_start",
        out_shape=(pltpu.SemaphoreType.DMA((3 * n,)), pltpu.SemaphoreType.DMA((3 * n,)), *src, *land,
                   jax.ShapeDtypeStruct((8, LANES), F32)),
        in_specs=[_HBM] * (2 * n),
        out_specs=(_SEM, _SEM, *([_HBM] * (2 * n)), pl.BlockSpec(memory_space=pltpu.VMEM)),
        input_output_aliases={i: 2 + i for i in range(2 * n)},
        compiler_params=pltpu.CompilerParams(has_side_effects=pltpu.SideEffectType.DATAFLOW_SIDE_EFFECTING),
    )(*[_in_hbm(s) for s in shards], *[_in_hbm(lax.empty((4,) + s.shape, s.dtype)) for s in shards])
    return outs[0], outs[1], list(outs[2:2 + n]), list(outs[2 + n:2 + 2 * n]), outs[-1]


def _late_gather_wait(send_sems, recv_sems, shards, lands, after):
    n = len(shards)

    def body(*refs):
        w_refs, l_refs = refs[:n], refs[n:2 * n]
        send_sems, recv_sems = refs[2 * n], refs[2 * n + 1]
        x, y, c = _mesh_pos()
        for t in range(n):
            for k, (px, py) in enumerate(_other_chips(x, y)):
                cp = pltpu.make_async_remote_copy(src_ref=w_refs[t], dst_ref=l_refs[t].at[2 * px + py],
                                                  send_sem=send_sems.at[3 * t + k], recv_sem=recv_sems.at[3 * t + k],
                                                  device_id=(x, y, c), device_id_type=MESH)
                cp.wait_send()
                cp.wait_recv()

    src = [pltpu.HBM(s.shape, s.dtype) for s in shards]
    land = [pltpu.HBM(l.shape, l.dtype) for l in lands]
    outs = pl.pallas_call(
        body,
        name="late_gather_wait",
        out_shape=(*src, *land),
        in_specs=[_HBM] * (2 * n) + [_SEM, _SEM, _ANY],
        out_specs=[_HBM] * (2 * n),
        input_output_aliases={i: i for i in range(2 * n)},
        compiler_params=pltpu.CompilerParams(has_side_effects=pltpu.SideEffectType.DATAFLOW_SIDE_EFFECTING),
    )(*shards, *lands, send_sems, recv_sems, after)
    return list(outs[n:])


def _all_to_all_small(parts):
    n = len(parts)

    def body(*refs):
        p_refs, o_refs = refs[:n], refs[n:2 * n]
        send_sems, recv_sems = refs[2 * n:]
        x, y, c = _mesh_pos()
        me = 4 * x + 2 * y + c
        sends = []
        for t in range(n):
            for k in range(1, 8):
                px, py, pc = x ^ (k >> 2), y ^ ((k >> 1) & 1), c ^ (k & 1)
                cp = pltpu.make_async_remote_copy(src_ref=p_refs[t], dst_ref=o_refs[t].at[me],
                                                  send_sem=send_sems.at[7 * t + k - 1], recv_sem=recv_sems.at[7 * t + k - 1],
                                                  device_id=(px, py, pc), device_id_type=MESH)
                cp.start()
                sends.append(cp)
        for t in range(n):
            for k in range(1, 8):
                peer = 4 * (x ^ (k >> 2)) + 2 * (y ^ ((k >> 1) & 1)) + (c ^ (k & 1))
                pltpu.make_async_remote_copy(src_ref=p_refs[t], dst_ref=o_refs[t].at[peer],
                                             send_sem=send_sems.at[7 * t + k - 1], recv_sem=recv_sems.at[7 * t + k - 1],
                                             device_id=(x, y, c), device_id_type=MESH).wait_recv()
        for cp in sends:
            cp.wait_send()

    return pl.pallas_call(
        body,
        out_shape=[jax.ShapeDtypeStruct((8,) + p.shape, p.dtype) for p in parts],
        in_specs=[_ANY] * n,
        out_specs=[_ANY] * n,
        scratch_shapes=[pltpu.SemaphoreType.DMA((7 * n,)), pltpu.SemaphoreType.DMA((7 * n,))],
        name="grad_small_all_to_all",
    )(*parts)


def _sum_devices_small(landed, own):
    n = len(landed)

    def body(*refs):
        x, y, c = _mesh_pos()
        me = 4 * x + 2 * y + c
        for t in range(n):
            acc = jnp.where(me == 0, refs[n + t][...], refs[t][0])
            for d in range(1, 8):
                acc = acc + jnp.where(me == d, refs[n + t][...], refs[t][d])
            refs[2 * n + t][...] = acc

    return pl.pallas_call(
        body,
        out_shape=[jax.ShapeDtypeStruct(p.shape, F32) for p in own],
        in_specs=[_VMEM_WHOLE] * (2 * n),
        out_specs=[_VMEM_WHOLE] * n,
        compiler_params=_params_whole(list(landed) + 2 * list(own)),
        name="grad_sum_devices_small",
    )(*landed, *own)


def _pair_join(fs):
    n = len(fs)

    def body(*refs):
        f_refs, o_refs = refs[:n], refs[n:2 * n]
        send_sems, recv_sems = refs[2 * n:]
        x, y, c = _mesh_pos()
        sends = []
        for t in range(n):
            cp = pltpu.make_async_remote_copy(src_ref=f_refs[t].at[c], dst_ref=o_refs[t].at[c], send_sem=send_sems.at[t],
                                              recv_sem=recv_sems.at[t], device_id=(x, y, 1 - c), device_id_type=MESH)
            cp.start()
            sends.append(cp)
        for t in range(n):
            pltpu.make_async_remote_copy(src_ref=f_refs[t].at[c], dst_ref=o_refs[t].at[1 - c], send_sem=send_sems.at[t],
                                         recv_sem=recv_sems.at[t], device_id=(x, y, c), device_id_type=MESH).wait_recv()
        for cp in sends:
            cp.wait_send()

    return pl.pallas_call(
        body,
        out_shape=[jax.ShapeDtypeStruct(f.shape, f.dtype) for f in fs],
        in_specs=[_ANY] * n,
        out_specs=[_ANY] * n,
        input_output_aliases={t: t for t in range(n)},
        scratch_shapes=[pltpu.SemaphoreType.DMA((n,)), pltpu.SemaphoreType.DMA((n,))],
        name="grad_pair_join",
    )(*fs)


def _rope_tables(Lp):
    inv = 1.0 / (ROPE_BASE ** (jnp.arange(0, ROPE, 2, dtype=F32) / ROPE))
    ang = (jnp.arange(Lp, dtype=F32) - FRONT)[:, None] * inv[None, :]
    cs, sn = jnp.cos(ang), jnp.sin(ang)
    return jnp.tile(cs, (1, 4)), jnp.concatenate([-sn, sn, -sn, sn], axis=1)


def _local_step(x, loss_target, meta, norm_g, w_in, gate_w, gate_b, gla_norm_g, gla_proj, q_norm_g, w_uq,
                kv_norm_g, w_ukv, mla_proj, w_out, final_norm_g, early_grads_hook=None, late_weights_hook=None):
    B, seq, _ = x.shape
    Lp = HEAD_ROWS + seq
    T = B * Lp
    tr = _div_tile(Lp, 768, 16)
    tkw = _div_tile(T, Lp, QB)
    tm_sq = _div_tile(T, 1024, QB)

    cuts = np.cumsum((0,) + SPLITS)
    shard_w = IN_WIDTH // 4

    def w_cols(i, width=None):
        parts = []
        for j in range(4):
            a, b = max(cuts[i], j * shard_w), min(cuts[i + 1], (j + 1) * shard_w)
            if a < b:
                parts.append(w_in[j][:, a - j * shard_w:b - j * shard_w])
        if width is not None:
            parts.append(jnp.zeros((D, width - (cuts[i + 1] - cuts[i])), w_in.dtype))
        return parts

    i_q, i_k, i_v, i_lr, i_z, i_cq, i_ckv, i_kr, i_mz, i_gg, i_gm = range(11)
    wA = jnp.concatenate(sum([w_cols(i) for i in (i_v, i_z, i_mz, i_gg, i_gm, i_q, i_k)], []), axis=1)
    wB = jnp.concatenate(w_cols(i_cq) + w_cols(i_ckv) + w_cols(i_lr, 128) + w_cols(i_kr, 128), axis=1)
    gn4 = jnp.tile(gla_norm_g, (1, GLA_H))
    cos_t, sin_t = _rope_tables(Lp)

    u = _rms_in(x, meta, norm_g, B, Lp)
    projA = _mm(u, wA, name="in_proj_a", out_dtype=BF16, tm=tkw, tn=1024, tk=D)
    projB = _mm(u, wB, name="in_proj_b", out_dtype=BF16, tm=tkw, tn=640, tk=D)
    if late_weights_hook is not None:
        gate_w, gla_proj, w_uq, w_ukv, mla_proj, w_out = late_weights_hook(projA)
    wg = jnp.pad(gate_w, ((0, 128 - GLA_RANK), (0, 0)))
    wuq2 = jnp.pad(w_uq.reshape(Q_RANK, MLA_H, MLA_QK), ((0, 0), (0, 0), (0, 256 - MLA_QK))).reshape(Q_RANK, 2048)
    oa, ya_in, ssave = _gla_fwd(projA, projB, wg, gate_b, gn4, B, Lp)
    ya = _mm(ya_in, gla_proj, name="gla_proj", out_dtype=BF16, tm=tm_sq, tn=D, tk=D)
    q_att, k_att, v_att, cqn, ckvn = _mla_prep(projB, cos_t, sin_t, q_norm_g, kv_norm_g, wuq2, w_ukv, B, Lp, tr)
    ob, yb_in, lse_c = _attn_fwd(q_att, k_att, v_att, projA, B, Lp)
    yb = _mm(yb_in, mla_proj, name="mla_proj", out_dtype=BF16, tm=tm_sq, tn=D, tk=D)
    dh1_b, merged, loss, d_gf = _out_proj_loss(x, meta, projA, ya, yb, w_out, final_norm_g.reshape(1, D),
                                                loss_target, B, Lp)

    g_w_out = _mm(merged, dh1_b, name="dw_out", trans_a=True, out_dtype=BF16, tm=D, tn=D, tk=tkw)
    dya, dyb, dA = _merge_bwd(dh1_b, w_out, projA, ya, yb, tr)
    g_gla_proj = _mm(ya_in, dya, name="dw_gla_proj", trans_a=True, out_dtype=BF16, tm=D, tn=D, tk=tkw)
    g_mla_proj = _mm(yb_in, dyb, name="dw_mla_proj", trans_a=True, out_dtype=BF16, tm=D, tn=D, tk=tkw)
    doa, dBz, d_gn = _gla_out_bwd(dya, gla_proj, oa, projA, gn4, tr)
    dC, g_wg, d_bg = _gla_bwd(projA, projB, ssave, doa, wg, gate_b, B, Lp)
    do, dDz, delta_c = _attn_bwd_pre(dyb, mla_proj, projA, ob, B, Lp)
    dq, dk, dv = _attn_bwd(q_att, k_att, v_att, do, lse_c, delta_c, B, Lp)
    dqf, dkvf, dE, d_gq, d_gkv = _mla_bwd_post(dq, dk, dv, projB, cos_t, sin_t, q_norm_g, kv_norm_g,
                                                wuq2, w_ukv, B, Lp, tr)
    g_wuq2 = _mm(cqn, dqf, name="dw_uq", trans_a=True, out_dtype=BF16, tm=Q_RANK, tn=2048, tk=tkw)
    g_wukv = _mm(ckvn, dkvf, name="dw_ukv", trans_a=True, out_dtype=BF16, tm=KV_RANK, tn=2048, tk=tkw)
    dparts = [dA, dBz, dC, dDz, dE]
    g_in = [_mm(u, dp, name="dw_in_%d" % i, trans_a=True, out_dtype=BF16, tm=D, tn=_div_tile(dp.shape[1], 1024, 256), tk=tkw)
            for i, dp in enumerate(dparts)]

    gA, gBz, gC, gDz, gE = g_in
    src = [(gC, 1024), (gC, 1536), (gC, 0), (gC, 2048), (gBz, 0), (gE, 0), (gE, Q_RANK), (gE, 384), (gDz, 0),
           (gA, 0), (gA, D)]
    owners = []
    for j in range(4):
        parts = []
        for i, (arr, off) in enumerate(src):
            a, b = max(cuts[i], j * shard_w), min(cuts[i + 1], (j + 1) * shard_w)
            if a < b:
                parts.append(arr[:, off + a - cuts[i]:off + b - cuts[i]])
        owners.append(jnp.concatenate(parts, axis=1))
    g_w_in = jnp.stack(owners)
    g_wuq = g_wuq2.reshape(Q_RANK, MLA_H, 256)[:, :, :MLA_QK].reshape(Q_RANK, MLA_H * MLA_QK)
    grads = dict(w_in=g_w_in, gla_gate_w=g_wg[:GLA_RANK], gla_proj=g_gla_proj, mla_w_uq=g_wuq, mla_w_ukv=g_wukv,
                 mla_proj=g_mla_proj, w_out=g_w_out, gla_gate_b=d_bg,
                 gla_norm_g=d_gn, mla_q_norm_g=d_gq, mla_kv_norm_g=d_gkv, final_norm_g=d_gf)
    token = None if early_grads_hook is None else early_grads_hook(grads)
    ng = norm_g if token is None else norm_g + token[0:1, 0:1]
    grad_x, d_meta, d_ng = _in_proj_bwd(x, meta, dh1_b, dA, dBz, dC, dDz, dE, wA, wB, ng, B, Lp)
    grads.update(meta_tokens=d_meta, norm_g=d_ng)
    return loss[0, 0], grad_x, grads


_MATS = ("w_in", "gla_gate_w", "gla_proj", "mla_w_uq", "mla_w_ukv", "mla_proj", "w_out")
_ROW_SHARDED = ("gla_proj", "mla_proj", "w_out")
_ORDER = ("meta_tokens", "norm_g", "w_in", "gla_gate_w", "gla_gate_b", "gla_norm_g", "gla_proj", "mla_q_norm_g",
          "mla_w_uq", "mla_kv_norm_g", "mla_w_ukv", "mla_proj", "w_out", "final_norm_g")
WIRE_BF16_MIN_ELEMS = 128 * 128
SMALL_PACK_ROWS = 16


def _pack_small(d, scalar=None):
    rows = [jnp.pad(d[n].reshape(1, size), ((0, 0), (0, D - size))) for n, size in SMALL]
    if scalar is not None:
        rows.append(jnp.pad(scalar.reshape(1, 1), ((0, 0), (0, D - 1))))
    return jnp.pad(jnp.concatenate(rows, axis=0), ((0, SMALL_PACK_ROWS - len(rows)), (0, 0)))


def _unpack_small(packed):
    return {n: packed[i, :size] for i, (n, size) in enumerate(SMALL)}


def kernel(x, meta_tokens, norm_g, w_in, gla_gate_w, gla_gate_b, gla_norm_g, gla_proj, mla_q_norm_g, mla_w_uq, mla_kv_norm_g, mla_w_ukv, mla_proj, w_out, final_norm_g, loss_target, m_meta_tokens, m_norm_g, m_w_in, m_gla_gate_w, m_gla_gate_b, m_gla_norm_g, m_gla_proj, m_mla_q_norm_g, m_mla_w_uq, m_mla_kv_norm_g, m_mla_w_ukv, m_mla_proj, m_w_out, m_final_norm_g, v_meta_tokens, v_norm_g, v_w_in, v_gla_gate_w, v_gla_gate_b, v_gla_norm_g, v_gla_proj, v_mla_q_norm_g, v_mla_w_uq, v_mla_kv_norm_g, v_mla_w_ukv, v_mla_proj, v_w_out, v_final_norm_g):
    w = dict(meta_tokens=meta_tokens, norm_g=norm_g, w_in=w_in[0], gla_gate_w=gla_gate_w[0], gla_gate_b=gla_gate_b,
             gla_norm_g=gla_norm_g, gla_proj=gla_proj[0], mla_q_norm_g=mla_q_norm_g, mla_w_uq=mla_w_uq[0],
             mla_kv_norm_g=mla_kv_norm_g, mla_w_ukv=mla_w_ukv[0], mla_proj=mla_proj[0], w_out=w_out[0],
             final_norm_g=final_norm_g)
    mom = dict(meta_tokens=m_meta_tokens, norm_g=m_norm_g, w_in=m_w_in[0], gla_gate_w=m_gla_gate_w[0],
               gla_gate_b=m_gla_gate_b, gla_norm_g=m_gla_norm_g, gla_proj=m_gla_proj[0], mla_q_norm_g=m_mla_q_norm_g,
               mla_w_uq=m_mla_w_uq[0], mla_kv_norm_g=m_mla_kv_norm_g, mla_w_ukv=m_mla_w_ukv[0], mla_proj=m_mla_proj[0],
               w_out=m_w_out[0], final_norm_g=m_final_norm_g)
    var = dict(meta_tokens=v_meta_tokens, norm_g=v_norm_g, w_in=v_w_in[0], gla_gate_w=v_gla_gate_w[0],
               gla_gate_b=v_gla_gate_b, gla_norm_g=v_gla_norm_g, gla_proj=v_gla_proj[0], mla_q_norm_g=v_mla_q_norm_g,
               mla_w_uq=v_mla_w_uq[0], mla_kv_norm_g=v_mla_kv_norm_g, mla_w_ukv=v_mla_w_ukv[0], mla_proj=v_mla_proj[0],
               w_out=v_w_out[0], final_norm_g=v_final_norm_g)
    out_shapes = {n: a.shape for n, a in zip(_ORDER, (meta_tokens, norm_g, w_in, gla_gate_w, gla_gate_b, gla_norm_g,
                                                     gla_proj, mla_q_norm_g, mla_w_uq, mla_kv_norm_g, mla_w_ukv,
                                                     mla_proj, w_out, final_norm_g))}

    me = (2 * lax.axis_index("x") + lax.axis_index("y")).astype(jnp.int32)
    is_mine = lax.broadcasted_iota(jnp.int32, (4, 1, 1), 0) == me
    with_own = lambda gth, own: jnp.where(is_mine, own[None], gth)
    first = [w["w_in"].astype(BF16), meta_tokens]
    w_in_owner, meta_owner = [with_own(gth, own) for gth, own in zip(_weight_gather(first), first)]
    meta_full = meta_owner.transpose(1, 0, 2).reshape(N_META, D)
    late_names = _MATS[1:]
    late = [w[n].astype(BF16) for n in late_names]
    gather_sems = _late_gather_start(late)

    def late_weights(after):
        lands = _late_gather_wait(gather_sems[0], gather_sems[1], gather_sems[2], gather_sems[3], after)
        full = []
        for name, land, own in zip(late_names, lands, late):
            gth = with_own(land, own)
            if name in _ROW_SHARDED:
                full.append(gth.reshape(4 * gth.shape[1], gth.shape[2]))
            else:
                full.append(gth.transpose(1, 0, 2).reshape(gth.shape[1], 4 * gth.shape[2]))
        return full

    def by_owner(name, arr):
        if name == "w_in":
            return arr
        if name in _ROW_SHARDED:
            return arr.reshape(4, arr.shape[0] // 4, arr.shape[1])
        return arr.reshape(arr.shape[0], 4, arr.shape[1] // 4).transpose(1, 0, 2)

    c_idx = lax.axis_index("c").astype(jnp.int32).reshape(1)
    pos = jnp.stack([c_idx[0], me])
    in_flight = {}

    def start_matrix_reduce(early):
        gps = [by_owner(n, early[n]) for n in _MATS]
        recvs = _pair_swap(gps)
        s1 = [_pair_add_big(gps[0], recvs[0], c_idx)] + list(_pair_add_small(gps[1:], recvs[1:]))
        send_sems, recv_sems, parts, lands, token = _chip_scatter_start(s1)
        in_flight.update(send_sems=send_sems, recv_sems=recv_sems, parts=parts, lands=lands)
        return token

    norm_g_after_start = norm_g + gather_sems[4][0:1, 0:1]
    loss_local, grad_x, g = _local_step(
        x, loss_target, meta_full, norm_g_after_start, w_in_owner, None, gla_gate_b, gla_norm_g, None,
        mla_q_norm_g, None, mla_kv_norm_g, None, None, None, final_norm_g,
        early_grads_hook=start_matrix_reduce, late_weights_hook=late_weights)

    s1, landed = _chip_scatter_wait(in_flight["send_sems"], in_flight["recv_sems"], in_flight["parts"],
                                    in_flight["lands"], after=g["norm_g"])
    halves = [_sum_chips_big(landed[0], s1[0], pos)] + list(_sum_chips_small(landed[1:], s1[1:]))
    g_mats = [j.reshape(out_shapes[n]) for j, n in zip(_pair_join(halves), _MATS)]

    late = [g["meta_tokens"], _pack_small(g, scalar=loss_local)]
    meta_sum, small_sum = _sum_devices_small(_all_to_all_small(late), late)
    loss = small_sum[len(SMALL), 0]
    g_meta = lax.dynamic_slice(meta_sum, (0, me * (D // 4)), (N_META, D // 4))
    names = _MATS + ("meta_tokens",)
    g_red = g_mats + [g_meta, small_sum]

    tens = lambda d: [d[n].reshape(out_shapes[n]) for n in names] + [_pack_small(d)]
    w_t, m_t, v_t = tens(w), tens(mom), tens(var)
    big = _adamw_big(w_t[0], g_red[0], m_t[0], v_t[0])
    rest = _adamw_small(w_t[1:], g_red[1:], m_t[1:], v_t[1:])
    k = len(names)
    results = {"grad": g_red}
    for i, kind in enumerate(("delta", "new_m", "new_v")):
        results[kind] = [big[i]] + list(rest[i * k:(i + 1) * k])

    outs = []
    for kind in ("grad", "delta", "new_m", "new_v"):
        vals = dict(zip(names, results[kind][:-1]))
        vals.update(_unpack_small(results[kind][-1]))
        outs += [vals[n].reshape(out_shapes[n]) for n in _ORDER]
    return (loss, grad_x, *outs)
```
